```python
import math
import jax
import jax.numpy as jnp
from jax import lax
import numpy as np

D_MODEL = 1024
BATCH = 16
SEQ = 2048
DEPTH = 1

CHUNK = 64
N_META = 16
QBLK = 128

MLA_HEADS = 4
D_NOPE = 128
D_ROPE = 64
D_QK = D_NOPE + D_ROPE
D_V = 128
KV_RANK = 256
Q_RANK = 384
ROPE_THETA = 10000.0
MLA_WIDTH = MLA_HEADS * D_V

LRU_WIDTH = D_MODEL // 2
LRU_BLOCKS = 8
LRU_BLOCK = LRU_WIDTH // LRU_BLOCKS
CONV_W = 4
C_RGLRU = 8.0

MIX_WIDTH = MLA_WIDTH + LRU_WIDTH
IN_WIDTH = Q_RANK + KV_RANK + D_ROPE + LRU_WIDTH + LRU_WIDTH

D_FF = 2816
FFN_RESIDUAL = 0.5
EPS = 1e-6
NEG_INF = -1e30

kernel_name = "hymba_mla_rglru_macaron_block"


def _rmsnorm(x, g):
    xf = x.astype(jnp.float32)
    y = xf * lax.rsqrt(jnp.mean(xf * xf, axis=-1, keepdims=True) + EPS)
    return (y * g.astype(jnp.float32)).astype(x.dtype)


def _swiglu_half(h, g, w_gate, w_up, w_down):
    u = _rmsnorm(h, g)
    return FFN_RESIDUAL * ((jax.nn.silu(u @ w_gate) * (u @ w_up)) @ w_down)


def _rope(x, cos, sin):
    half = x.shape[-1] // 2
    x1, x2 = x[..., :half], x[..., half:]
    return jnp.concatenate([x1 * cos - x2 * sin, x2 * cos + x1 * sin], axis=-1)


def _mla(c_q, c_kv, k_r, q_latent_norm, w_uq, kv_latent_norm, w_uk, w_uv,
         q_head_norm, k_head_norm):
    B, L, _ = c_q.shape
    q = (_rmsnorm(c_q, q_latent_norm) @ w_uq).reshape(B, L, MLA_HEADS, D_QK)
    ckv = _rmsnorm(c_kv, kv_latent_norm)
    k_nope = (ckv @ w_uk).reshape(B, L, MLA_HEADS, D_NOPE)
    v = (ckv @ w_uv).reshape(B, L, MLA_HEADS, D_V)
    k_rope = jnp.broadcast_to(k_r[:, :, None, :], (B, L, MLA_HEADS, D_ROPE))
    k = jnp.concatenate([k_nope, k_rope], axis=-1)
    q = _rmsnorm(q, q_head_norm)
    k = _rmsnorm(k, k_head_norm)
    pos = jnp.arange(L, dtype=jnp.float32)
    inv_freq = ROPE_THETA ** (-jnp.arange(0, D_ROPE // 2, dtype=jnp.float32) / (D_ROPE // 2))
    ang = pos[:, None] * inv_freq[None, :]
    cos = jnp.cos(ang)[:, None, :].astype(q.dtype)
    sin = jnp.sin(ang)[:, None, :].astype(q.dtype)
    q = jnp.concatenate([q[..., :D_NOPE], _rope(q[..., D_NOPE:], cos, sin)], axis=-1)
    k = jnp.concatenate([k[..., :D_NOPE], _rope(k[..., D_NOPE:], cos, sin)], axis=-1)
    n_blk = -(-L // QBLK)
    L_pad = n_blk * QBLK
    padw = ((0, 0), (0, L_pad - L), (0, 0), (0, 0))
    q = jnp.pad(q, padw).transpose(0, 2, 1, 3)
    k = jnp.pad(k, padw).transpose(0, 2, 1, 3)
    v = jnp.pad(v, padw).transpose(0, 2, 1, 3)
    cid = (jnp.arange(L_pad, dtype=jnp.int32) + (CHUNK - N_META)) // CHUNK
    q_blocks = q.reshape(B, MLA_HEADS, n_blk, QBLK, D_QK).transpose(2, 0, 1, 3, 4)
    cid_blocks = cid.reshape(n_blk, QBLK)
    scale = 1.0 / math.sqrt(D_QK)

    def attend(args):
        qb, cq = args
        s = jnp.einsum('bhqd,bhkd->bhqk', qb, k,
                       preferred_element_type=jnp.float32) * scale
        mask = cid[None, :] <= cq[:, None]
        s = jnp.where(mask[None, None], s, NEG_INF)
        p = jax.nn.softmax(s, axis=-1).astype(v.dtype)
        return jnp.einsum('bhqk,bhkd->bhqd', p, v)

    o = lax.map(attend, (q_blocks, cid_blocks))
    o = o.transpose(1, 0, 3, 2, 4).reshape(B, L_pad, MLA_WIDTH)
    return o[:, :L]


def _rglru(u, gate, conv_w, conv_b, gate_a_w, gate_a_b, gate_x_w, gate_x_b, lru_lambda):
    B, L, W = u.shape
    xc = lax.conv_general_dilated(
        u, conv_w[:, None, :].astype(u.dtype), window_strides=(1,),
        padding=[(CONV_W - 1, 0)], dimension_numbers=('NWC', 'WIO', 'NWC'),
        feature_group_count=W) + conv_b
    xb = xc.reshape(B, L, LRU_BLOCKS, LRU_BLOCK)
    r = jax.nn.sigmoid(jnp.einsum('blni,nij->blnj', xb, gate_a_w).reshape(B, L, W) + gate_a_b)
    i = jax.nn.sigmoid(jnp.einsum('blni,nij->blnj', xb, gate_x_w).reshape(B, L, W) + gate_x_b)
    log_a = -C_RGLRU * r.astype(jnp.float32) * jax.nn.softplus(-lru_lambda.astype(jnp.float32))
    a = jnp.exp(log_a)
    mult = jnp.sqrt(-jnp.expm1(2.0 * log_a))
    first = (jnp.arange(L) == 0)[None, :, None]
    mult = jnp.where(first, 1.0, mult)
    b = mult * (i * xc).astype(jnp.float32)

    def combine(lhs, rhs):
        a1, b1 = lhs
        a2, b2 = rhs
        return a1 * a2, a2 * b1 + b2

    _, h = lax.associative_scan(combine, (a, b), axis=1)
    return h.astype(u.dtype) * jax.nn.gelu(gate)


def _fwd_setup_inputs(seed: int = 0) -> dict:
    key = jax.random.key(seed)
    ks = iter(jax.random.split(key, 40))

    def dense(shape, fan_in):
        return jax.random.normal(next(ks), shape, jnp.float32) * (fan_in ** -0.5)

    def gain(n):
        return 1.0 + 0.05 * jax.random.normal(next(ks), (DEPTH, n), jnp.float32)

    def bias(n):
        return 0.01 * jax.random.normal(next(ks), (DEPTH, n), jnp.float32)

    x = jax.random.normal(next(ks), (BATCH, SEQ, D_MODEL), jnp.float32)
    meta_tokens = jax.random.normal(next(ks), (N_META, D_MODEL), jnp.float32)
    a0 = 0.9 + 0.099 * jax.random.uniform(next(ks), (DEPTH, LRU_WIDTH), jnp.float32)
    s0 = a0 ** (1.0 / C_RGLRU)
    lru_lambda = jnp.log(s0) - jnp.log1p(-s0)
    return {
        "x": x,
        "meta_tokens": meta_tokens,
        "ffn1_norm": gain(D_MODEL),
        "ffn1_w_gate": dense((DEPTH, D_MODEL, D_FF), D_MODEL),
        "ffn1_w_up": dense((DEPTH, D_MODEL, D_FF), D_MODEL),
        "ffn1_w_down": dense((DEPTH, D_FF, D_MODEL), D_FF),
        "mix_norm": gain(D_MODEL),
        "w_in": dense((DEPTH, D_MODEL, IN_WIDTH), D_MODEL),
        "q_latent_norm": gain(Q_RANK),
        "w_uq": dense((DEPTH, Q_RANK, MLA_HEADS * D_QK), Q_RANK),
        "kv_latent_norm": gain(KV_RANK),
        "w_uk": dense((DEPTH, KV_RANK, MLA_HEADS * D_NOPE), KV_RANK),
        "w_uv": dense((DEPTH, KV_RANK, MLA_HEADS * D_V), KV_RANK),
        "q_head_norm": gain(D_QK),
        "k_head_norm": gain(D_QK),
        "conv_w": dense((DEPTH, CONV_W, LRU_WIDTH), CONV_W),
        "conv_b": bias(LRU_WIDTH),
        "gate_a_w": dense((DEPTH, LRU_BLOCKS, LRU_BLOCK, LRU_BLOCK), LRU_BLOCK),
        "gate_a_b": bias(LRU_WIDTH),
        "gate_x_w": dense((DEPTH, LRU_BLOCKS, LRU_BLOCK, LRU_BLOCK), LRU_BLOCK),
        "gate_x_b": bias(LRU_WIDTH),
        "lru_lambda": lru_lambda,
        "attn_out_norm": gain(MLA_WIDTH),
        "lru_out_norm": gain(LRU_WIDTH),
        "w_out": dense((DEPTH, MIX_WIDTH, D_MODEL), MIX_WIDTH),
        "ffn2_norm": gain(D_MODEL),
        "ffn2_w_gate": dense((DEPTH, D_MODEL, D_FF), D_MODEL),
        "ffn2_w_up": dense((DEPTH, D_MODEL, D_FF), D_MODEL),
        "ffn2_w_down": dense((DEPTH, D_FF, D_MODEL), D_FF),
        "final_norm": gain(D_MODEL),
    }


def _fwd_reference(x, meta_tokens, ffn1_norm, ffn1_w_gate, ffn1_w_up, ffn1_w_down,
              mix_norm, w_in, q_latent_norm, w_uq, kv_latent_norm, w_uk, w_uv,
              q_head_norm, k_head_norm, conv_w, conv_b, gate_a_w, gate_a_b,
              gate_x_w, gate_x_b, lru_lambda, attn_out_norm, lru_out_norm, w_out,
              ffn2_norm, ffn2_w_gate, ffn2_w_up, ffn2_w_down, final_norm):
    B = x.shape[0]
    meta = jnp.broadcast_to(meta_tokens.astype(x.dtype)[None], (B, N_META, D_MODEL))
    h = jnp.concatenate([meta, x], axis=1)
    o1 = Q_RANK
    o2 = o1 + KV_RANK
    o3 = o2 + D_ROPE
    o4 = o3 + LRU_WIDTH
    for l in range(DEPTH):
        h = h + _swiglu_half(h, ffn1_norm[l], ffn1_w_gate[l], ffn1_w_up[l], ffn1_w_down[l])
        z = _rmsnorm(h, mix_norm[l]) @ w_in[l]
        c_q, c_kv, k_r = z[..., :o1], z[..., o1:o2], z[..., o2:o3]
        u, g = z[..., o3:o4], z[..., o4:]
        y_mla = _mla(c_q, c_kv, k_r, q_latent_norm[l], w_uq[l], kv_latent_norm[l],
                     w_uk[l], w_uv[l], q_head_norm[l], k_head_norm[l])
        y_lru = _rglru(u, g, conv_w[l], conv_b[l], gate_a_w[l], gate_a_b[l],
                       gate_x_w[l], gate_x_b[l], lru_lambda[l])
        y = jnp.concatenate([_rmsnorm(y_mla, attn_out_norm[l]),
                             _rmsnorm(y_lru, lru_out_norm[l])], axis=-1)
        h = h + y @ w_out[l]
        h = h + _swiglu_half(h, ffn2_norm[l], ffn2_w_gate[l], ffn2_w_up[l], ffn2_w_down[l])
        h = _rmsnorm(h, final_norm[l])
    return h[:, N_META:]


import jax as _jax
import jax.numpy as _jnp

TWIN_FORMAT = 'train_step'
FWD_PARAMS = ['x', 'meta_tokens', 'ffn1_norm', 'ffn1_w_gate', 'ffn1_w_up', 'ffn1_w_down', 'mix_norm', 'w_in', 'q_latent_norm', 'w_uq', 'kv_latent_norm', 'w_uk', 'w_uv', 'q_head_norm', 'k_head_norm', 'conv_w', 'conv_b', 'gate_a_w', 'gate_a_b', 'gate_x_w', 'gate_x_b', 'lru_lambda', 'attn_out_norm', 'lru_out_norm', 'w_out', 'ffn2_norm', 'ffn2_w_gate', 'ffn2_w_up', 'ffn2_w_down', 'final_norm']
TWIN_WEIGHTS = ['meta_tokens', 'ffn1_norm', 'ffn1_w_gate', 'ffn1_w_up', 'ffn1_w_down', 'mix_norm', 'w_in', 'q_latent_norm', 'w_uq', 'kv_latent_norm', 'w_uk', 'w_uv', 'q_head_norm', 'k_head_norm', 'conv_w', 'conv_b', 'gate_a_w', 'gate_a_b', 'gate_x_w', 'gate_x_b', 'lru_lambda', 'attn_out_norm', 'lru_out_norm', 'w_out', 'ffn2_norm', 'ffn2_w_gate', 'ffn2_w_up', 'ffn2_w_down', 'final_norm']
TWIN_DIFF_INPUT = 'x'
TWIN_INPUTS = ['x', 'meta_tokens', 'ffn1_norm', 'ffn1_w_gate', 'ffn1_w_up', 'ffn1_w_down', 'mix_norm', 'w_in', 'q_latent_norm', 'w_uq', 'kv_latent_norm', 'w_uk', 'w_uv', 'q_head_norm', 'k_head_norm', 'conv_w', 'conv_b', 'gate_a_w', 'gate_a_b', 'gate_x_w', 'gate_x_b', 'lru_lambda', 'attn_out_norm', 'lru_out_norm', 'w_out', 'ffn2_norm', 'ffn2_w_gate', 'ffn2_w_up', 'ffn2_w_down', 'final_norm', 'loss_target', 'm_meta_tokens', 'm_ffn1_norm', 'm_ffn1_w_gate', 'm_ffn1_w_up', 'm_ffn1_w_down', 'm_mix_norm', 'm_w_in', 'm_q_latent_norm', 'm_w_uq', 'm_kv_latent_norm', 'm_w_uk', 'm_w_uv', 'm_q_head_norm', 'm_k_head_norm', 'm_conv_w', 'm_conv_b', 'm_gate_a_w', 'm_gate_a_b', 'm_gate_x_w', 'm_gate_x_b', 'm_lru_lambda', 'm_attn_out_norm', 'm_lru_out_norm', 'm_w_out', 'm_ffn2_norm', 'm_ffn2_w_gate', 'm_ffn2_w_up', 'm_ffn2_w_down', 'm_final_norm', 'v_meta_tokens', 'v_ffn1_norm', 'v_ffn1_w_gate', 'v_ffn1_w_up', 'v_ffn1_w_down', 'v_mix_norm', 'v_w_in', 'v_q_latent_norm', 'v_w_uq', 'v_kv_latent_norm', 'v_w_uk', 'v_w_uv', 'v_q_head_norm', 'v_k_head_norm', 'v_conv_w', 'v_conv_b', 'v_gate_a_w', 'v_gate_a_b', 'v_gate_x_w', 'v_gate_x_b', 'v_lru_lambda', 'v_attn_out_norm', 'v_lru_out_norm', 'v_w_out', 'v_ffn2_norm', 'v_ffn2_w_gate', 'v_ffn2_w_up', 'v_ffn2_w_down', 'v_final_norm']
TWIN_OUTPUTS = ['loss', 'grad_x', 'grad_meta_tokens', 'grad_ffn1_norm', 'grad_ffn1_w_gate', 'grad_ffn1_w_up', 'grad_ffn1_w_down', 'grad_mix_norm', 'grad_w_in', 'grad_q_latent_norm', 'grad_w_uq', 'grad_kv_latent_norm', 'grad_w_uk', 'grad_w_uv', 'grad_q_head_norm', 'grad_k_head_norm', 'grad_conv_w', 'grad_conv_b', 'grad_gate_a_w', 'grad_gate_a_b', 'grad_gate_x_w', 'grad_gate_x_b', 'grad_lru_lambda', 'grad_attn_out_norm', 'grad_lru_out_norm', 'grad_w_out', 'grad_ffn2_norm', 'grad_ffn2_w_gate', 'grad_ffn2_w_up', 'grad_ffn2_w_down', 'grad_final_norm', 'delta_meta_tokens', 'delta_ffn1_norm', 'delta_ffn1_w_gate', 'delta_ffn1_w_up', 'delta_ffn1_w_down', 'delta_mix_norm', 'delta_w_in', 'delta_q_latent_norm', 'delta_w_uq', 'delta_kv_latent_norm', 'delta_w_uk', 'delta_w_uv', 'delta_q_head_norm', 'delta_k_head_norm', 'delta_conv_w', 'delta_conv_b', 'delta_gate_a_w', 'delta_gate_a_b', 'delta_gate_x_w', 'delta_gate_x_b', 'delta_lru_lambda', 'delta_attn_out_norm', 'delta_lru_out_norm', 'delta_w_out', 'delta_ffn2_norm', 'delta_ffn2_w_gate', 'delta_ffn2_w_up', 'delta_ffn2_w_down', 'delta_final_norm', 'new_m_meta_tokens', 'new_m_ffn1_norm', 'new_m_ffn1_w_gate', 'new_m_ffn1_w_up', 'new_m_ffn1_w_down', 'new_m_mix_norm', 'new_m_w_in', 'new_m_q_latent_norm', 'new_m_w_uq', 'new_m_kv_latent_norm', 'new_m_w_uk', 'new_m_w_uv', 'new_m_q_head_norm', 'new_m_k_head_norm', 'new_m_conv_w', 'new_m_conv_b', 'new_m_gate_a_w', 'new_m_gate_a_b', 'new_m_gate_x_w', 'new_m_gate_x_b', 'new_m_lru_lambda', 'new_m_attn_out_norm', 'new_m_lru_out_norm', 'new_m_w_out', 'new_m_ffn2_norm', 'new_m_ffn2_w_gate', 'new_m_ffn2_w_up', 'new_m_ffn2_w_down', 'new_m_final_norm', 'new_v_meta_tokens', 'new_v_ffn1_norm', 'new_v_ffn1_w_gate', 'new_v_ffn1_w_up', 'new_v_ffn1_w_down', 'new_v_mix_norm', 'new_v_w_in', 'new_v_q_latent_norm', 'new_v_w_uq', 'new_v_kv_latent_norm', 'new_v_w_uk', 'new_v_w_uv', 'new_v_q_head_norm', 'new_v_k_head_norm', 'new_v_conv_w', 'new_v_conv_b', 'new_v_gate_a_w', 'new_v_gate_a_b', 'new_v_gate_x_w', 'new_v_gate_x_b', 'new_v_lru_lambda', 'new_v_attn_out_norm', 'new_v_lru_out_norm', 'new_v_w_out', 'new_v_ffn2_norm', 'new_v_ffn2_w_gate', 'new_v_ffn2_w_up', 'new_v_ffn2_w_down', 'new_v_final_norm']
TWIN_LEAF_KINDS = {'loss': 'loss', 'grad_x': 'grad_x', 'grad_meta_tokens': 'grad_w', 'grad_ffn1_norm': 'grad_w', 'grad_ffn1_w_gate': 'grad_w', 'grad_ffn1_w_up': 'grad_w', 'grad_ffn1_w_down': 'grad_w', 'grad_mix_norm': 'grad_w', 'grad_w_in': 'grad_w', 'grad_q_latent_norm': 'grad_w', 'grad_w_uq': 'grad_w', 'grad_kv_latent_norm': 'grad_w', 'grad_w_uk': 'grad_w', 'grad_w_uv': 'grad_w', 'grad_q_head_norm': 'grad_w', 'grad_k_head_norm': 'grad_w', 'grad_conv_w': 'grad_w', 'grad_conv_b': 'grad_w', 'grad_gate_a_w': 'grad_w', 'grad_gate_a_b': 'grad_w', 'grad_gate_x_w': 'grad_w', 'grad_gate_x_b': 'grad_w', 'grad_lru_lambda': 'grad_w', 'grad_attn_out_norm': 'grad_w', 'grad_lru_out_norm': 'grad_w', 'grad_w_out': 'grad_w', 'grad_ffn2_norm': 'grad_w', 'grad_ffn2_w_gate': 'grad_w', 'grad_ffn2_w_up': 'grad_w', 'grad_ffn2_w_down': 'grad_w', 'grad_final_norm': 'grad_w', 'delta_meta_tokens': 'delta_w', 'delta_ffn1_norm': 'delta_w', 'delta_ffn1_w_gate': 'delta_w', 'delta_ffn1_w_up': 'delta_w', 'delta_ffn1_w_down': 'delta_w', 'delta_mix_norm': 'delta_w', 'delta_w_in': 'delta_w', 'delta_q_latent_norm': 'delta_w', 'delta_w_uq': 'delta_w', 'delta_kv_latent_norm': 'delta_w', 'delta_w_uk': 'delta_w', 'delta_w_uv': 'delta_w', 'delta_q_head_norm': 'delta_w', 'delta_k_head_norm': 'delta_w', 'delta_conv_w': 'delta_w', 'delta_conv_b': 'delta_w', 'delta_gate_a_w': 'delta_w', 'delta_gate_a_b': 'delta_w', 'delta_gate_x_w': 'delta_w', 'delta_gate_x_b': 'delta_w', 'delta_lru_lambda': 'delta_w', 'delta_attn_out_norm': 'delta_w', 'delta_lru_out_norm': 'delta_w', 'delta_w_out': 'delta_w', 'delta_ffn2_norm': 'delta_w', 'delta_ffn2_w_gate': 'delta_w', 'delta_ffn2_w_up': 'delta_w', 'delta_ffn2_w_down': 'delta_w', 'delta_final_norm': 'delta_w', 'new_m_meta_tokens': 'new_m', 'new_m_ffn1_norm': 'new_m', 'new_m_ffn1_w_gate': 'new_m', 'new_m_ffn1_w_up': 'new_m', 'new_m_ffn1_w_down': 'new_m', 'new_m_mix_norm': 'new_m', 'new_m_w_in': 'new_m', 'new_m_q_latent_norm': 'new_m', 'new_m_w_uq': 'new_m', 'new_m_kv_latent_norm': 'new_m', 'new_m_w_uk': 'new_m', 'new_m_w_uv': 'new_m', 'new_m_q_head_norm': 'new_m', 'new_m_k_head_norm': 'new_m', 'new_m_conv_w': 'new_m', 'new_m_conv_b': 'new_m', 'new_m_gate_a_w': 'new_m', 'new_m_gate_a_b': 'new_m', 'new_m_gate_x_w': 'new_m', 'new_m_gate_x_b': 'new_m', 'new_m_lru_lambda': 'new_m', 'new_m_attn_out_norm': 'new_m', 'new_m_lru_out_norm': 'new_m', 'new_m_w_out': 'new_m', 'new_m_ffn2_norm': 'new_m', 'new_m_ffn2_w_gate': 'new_m', 'new_m_ffn2_w_up': 'new_m', 'new_m_ffn2_w_down': 'new_m', 'new_m_final_norm': 'new_m', 'new_v_meta_tokens': 'new_v', 'new_v_ffn1_norm': 'new_v', 'new_v_ffn1_w_gate': 'new_v', 'new_v_ffn1_w_up': 'new_v', 'new_v_ffn1_w_down': 'new_v', 'new_v_mix_norm': 'new_v', 'new_v_w_in': 'new_v', 'new_v_q_latent_norm': 'new_v', 'new_v_w_uq': 'new_v', 'new_v_kv_latent_norm': 'new_v', 'new_v_w_uk': 'new_v', 'new_v_w_uv': 'new_v', 'new_v_q_head_norm': 'new_v', 'new_v_k_head_norm': 'new_v', 'new_v_conv_w': 'new_v', 'new_v_conv_b': 'new_v', 'new_v_gate_a_w': 'new_v', 'new_v_gate_a_b': 'new_v', 'new_v_gate_x_w': 'new_v', 'new_v_gate_x_b': 'new_v', 'new_v_lru_lambda': 'new_v', 'new_v_attn_out_norm': 'new_v', 'new_v_lru_out_norm': 'new_v', 'new_v_w_out': 'new_v', 'new_v_ffn2_norm': 'new_v', 'new_v_ffn2_w_gate': 'new_v', 'new_v_ffn2_w_up': 'new_v', 'new_v_ffn2_w_down': 'new_v', 'new_v_final_norm': 'new_v'}


def _forward(args):
    return _fwd_reference(*[args[k] for k in FWD_PARAMS])


def _output_shape():
    out = _jax.eval_shape(lambda: _forward(_fwd_setup_inputs(0)))
    return out.shape, out.dtype

N_MICROBATCH = 1
ADAM_LR = 0.001
ADAM_B1 = 0.9
ADAM_B2 = 0.999
ADAM_EPS = 1e-08
ADAM_WD = 0.01
ADAM_STEP = 10
PER_EXAMPLE_BATCH_AXIS = {'x': 0, 'loss_target': 0}
SHARED_INPUTS = []
_WEIGHT_DTYPES = {'meta_tokens': _jnp.float32, 'ffn1_norm': _jnp.float32, 'ffn1_w_gate': _jnp.float32, 'ffn1_w_up': _jnp.float32, 'ffn1_w_down': _jnp.float32, 'mix_norm': _jnp.float32, 'w_in': _jnp.float32, 'q_latent_norm': _jnp.float32, 'w_uq': _jnp.float32, 'kv_latent_norm': _jnp.float32, 'w_uk': _jnp.float32, 'w_uv': _jnp.float32, 'q_head_norm': _jnp.float32, 'k_head_norm': _jnp.float32, 'conv_w': _jnp.float32, 'conv_b': _jnp.float32, 'gate_a_w': _jnp.float32, 'gate_a_b': _jnp.float32, 'gate_x_w': _jnp.float32, 'gate_x_b': _jnp.float32, 'lru_lambda': _jnp.float32, 'attn_out_norm': _jnp.float32, 'lru_out_norm': _jnp.float32, 'w_out': _jnp.float32, 'ffn2_norm': _jnp.float32, 'ffn2_w_gate': _jnp.float32, 'ffn2_w_up': _jnp.float32, 'ffn2_w_down': _jnp.float32, 'final_norm': _jnp.float32}
MOMENT_SCALE = {'meta_tokens': 1.207784e-02, 'ffn1_norm': 9.489078e-02, 'ffn1_w_gate': 4.131425e-02, 'ffn1_w_up': 4.006058e-02, 'ffn1_w_down': 6.656036e-02, 'mix_norm': 1.939744e-01, 'w_in': 1.486040e-01, 'q_latent_norm': 1.702019e-01, 'w_uq': 1.040431e-01, 'kv_latent_norm': 4.284307e-01, 'w_uk': 1.217553e-01, 'w_uv': 1.449395e-01, 'q_head_norm': 2.143692e-01, 'k_head_norm': 2.065133e-01, 'conv_w': 1.345519e-01, 'conv_b': 1.730510e+00, 'gate_a_w': 5.635938e-02, 'gate_a_b': 3.784237e-02, 'gate_x_w': 1.054103e-01, 'gate_x_b': 4.544042e-02, 'lru_lambda': 6.610506e-02, 'attn_out_norm': 1.528982e-01, 'lru_out_norm': 1.568766e-01, 'w_out': 1.443405e-01, 'ffn2_norm': 5.242473e-02, 'ffn2_w_gate': 2.246680e-02, 'ffn2_w_up': 2.205798e-02, 'ffn2_w_down': 3.649034e-02, 'final_norm': 3.211078e+01}


def _to_microbatches(a, axis):
    t = _jnp.moveaxis(a, axis, 0)
    t = t.reshape((N_MICROBATCH, t.shape[0] // N_MICROBATCH) + t.shape[1:])
    return _jnp.moveaxis(t, 1, axis + 1)


def setup_inputs(seed: int = 0) -> dict:
    inp = _fwd_setup_inputs(seed)
    key = _jax.random.fold_in(_jax.random.key(seed), 7919)
    shape, _ = _output_shape()
    out = dict(inp)
    out["loss_target"] = _jax.random.normal(_jax.random.fold_in(key, 0), shape, _jnp.float32)
    for i, name in enumerate(TWIN_WEIGHTS):
        w = inp[name].astype(_jnp.float32)
        if MOMENT_SCALE is None:
            s = _jnp.sqrt(_jnp.mean(_jnp.square(w)) + 1e-30)
        else:
            s = MOMENT_SCALE[name]
        km, kv = _jax.random.split(_jax.random.fold_in(key, i + 1))
        out[name] = w
        out["m_" + name] = s * _jax.random.normal(km, w.shape, _jnp.float32)
        out["v_" + name] = (s * s) * _jax.random.uniform(kv, w.shape, _jnp.float32, 0.5, 1.5)
    if N_MICROBATCH > 1:
        for name, axis in PER_EXAMPLE_BATCH_AXIS.items():
            out[name] = _to_microbatches(out[name], axis)
    return {'x': out['x'], 'meta_tokens': out['meta_tokens'], 'ffn1_norm': out['ffn1_norm'], 'ffn1_w_gate': out['ffn1_w_gate'], 'ffn1_w_up': out['ffn1_w_up'], 'ffn1_w_down': out['ffn1_w_down'], 'mix_norm': out['mix_norm'], 'w_in': out['w_in'], 'q_latent_norm': out['q_latent_norm'], 'w_uq': out['w_uq'], 'kv_latent_norm': out['kv_latent_norm'], 'w_uk': out['w_uk'], 'w_uv': out['w_uv'], 'q_head_norm': out['q_head_norm'], 'k_head_norm': out['k_head_norm'], 'conv_w': out['conv_w'], 'conv_b': out['conv_b'], 'gate_a_w': out['gate_a_w'], 'gate_a_b': out['gate_a_b'], 'gate_x_w': out['gate_x_w'], 'gate_x_b': out['gate_x_b'], 'lru_lambda': out['lru_lambda'], 'attn_out_norm': out['attn_out_norm'], 'lru_out_norm': out['lru_out_norm'], 'w_out': out['w_out'], 'ffn2_norm': out['ffn2_norm'], 'ffn2_w_gate': out['ffn2_w_gate'], 'ffn2_w_up': out['ffn2_w_up'], 'ffn2_w_down': out['ffn2_w_down'], 'final_norm': out['final_norm'], 'loss_target': out['loss_target'], 'm_meta_tokens': out['m_meta_tokens'], 'm_ffn1_norm': out['m_ffn1_norm'], 'm_ffn1_w_gate': out['m_ffn1_w_gate'], 'm_ffn1_w_up': out['m_ffn1_w_up'], 'm_ffn1_w_down': out['m_ffn1_w_down'], 'm_mix_norm': out['m_mix_norm'], 'm_w_in': out['m_w_in'], 'm_q_latent_norm': out['m_q_latent_norm'], 'm_w_uq': out['m_w_uq'], 'm_kv_latent_norm': out['m_kv_latent_norm'], 'm_w_uk': out['m_w_uk'], 'm_w_uv': out['m_w_uv'], 'm_q_head_norm': out['m_q_head_norm'], 'm_k_head_norm': out['m_k_head_norm'], 'm_conv_w': out['m_conv_w'], 'm_conv_b': out['m_conv_b'], 'm_gate_a_w': out['m_gate_a_w'], 'm_gate_a_b': out['m_gate_a_b'], 'm_gate_x_w': out['m_gate_x_w'], 'm_gate_x_b': out['m_gate_x_b'], 'm_lru_lambda': out['m_lru_lambda'], 'm_attn_out_norm': out['m_attn_out_norm'], 'm_lru_out_norm': out['m_lru_out_norm'], 'm_w_out': out['m_w_out'], 'm_ffn2_norm': out['m_ffn2_norm'], 'm_ffn2_w_gate': out['m_ffn2_w_gate'], 'm_ffn2_w_up': out['m_ffn2_w_up'], 'm_ffn2_w_down': out['m_ffn2_w_down'], 'm_final_norm': out['m_final_norm'], 'v_meta_tokens': out['v_meta_tokens'], 'v_ffn1_norm': out['v_ffn1_norm'], 'v_ffn1_w_gate': out['v_ffn1_w_gate'], 'v_ffn1_w_up': out['v_ffn1_w_up'], 'v_ffn1_w_down': out['v_ffn1_w_down'], 'v_mix_norm': out['v_mix_norm'], 'v_w_in': out['v_w_in'], 'v_q_latent_norm': out['v_q_latent_norm'], 'v_w_uq': out['v_w_uq'], 'v_kv_latent_norm': out['v_kv_latent_norm'], 'v_w_uk': out['v_w_uk'], 'v_w_uv': out['v_w_uv'], 'v_q_head_norm': out['v_q_head_norm'], 'v_k_head_norm': out['v_k_head_norm'], 'v_conv_w': out['v_conv_w'], 'v_conv_b': out['v_conv_b'], 'v_gate_a_w': out['v_gate_a_w'], 'v_gate_a_b': out['v_gate_a_b'], 'v_gate_x_w': out['v_gate_x_w'], 'v_gate_x_b': out['v_gate_x_b'], 'v_lru_lambda': out['v_lru_lambda'], 'v_attn_out_norm': out['v_attn_out_norm'], 'v_lru_out_norm': out['v_lru_out_norm'], 'v_w_out': out['v_w_out'], 'v_ffn2_norm': out['v_ffn2_norm'], 'v_ffn2_w_gate': out['v_ffn2_w_gate'], 'v_ffn2_w_up': out['v_ffn2_w_up'], 'v_ffn2_w_down': out['v_ffn2_w_down'], 'v_final_norm': out['v_final_norm']}


def _loss(weights, diff, rest, loss_target):
    with _jax.named_scope("forward"):
        args = {**rest, TWIN_DIFF_INPUT: diff, **{k: w.astype(_WEIGHT_DTYPES[k]) for k, w in weights.items()}}
        y = _forward(args)
    with _jax.named_scope("loss_head"):
        err = _jnp.square(y.astype(_jnp.float32) - loss_target)
        return 0.5 * _jnp.sum(_jnp.mean(err, axis=-1)) if err.ndim else 0.5 * err


def _adamw(w, g, m, v):
    m = ADAM_B1 * m + (1.0 - ADAM_B1) * g
    v = ADAM_B2 * v + (1.0 - ADAM_B2) * _jnp.square(g)
    m_hat = m / (1.0 - ADAM_B1 ** ADAM_STEP)
    v_hat = v / (1.0 - ADAM_B2 ** ADAM_STEP)
    delta = -ADAM_LR * (m_hat / (_jnp.sqrt(v_hat) + ADAM_EPS) + ADAM_WD * w)
    return delta, m, v


def reference(x, meta_tokens, ffn1_norm, ffn1_w_gate, ffn1_w_up, ffn1_w_down, mix_norm, w_in, q_latent_norm, w_uq, kv_latent_norm, w_uk, w_uv, q_head_norm, k_head_norm, conv_w, conv_b, gate_a_w, gate_a_b, gate_x_w, gate_x_b, lru_lambda, attn_out_norm, lru_out_norm, w_out, ffn2_norm, ffn2_w_gate, ffn2_w_up, ffn2_w_down, final_norm, loss_target, m_meta_tokens, m_ffn1_norm, m_ffn1_w_gate, m_ffn1_w_up, m_ffn1_w_down, m_mix_norm, m_w_in, m_q_latent_norm, m_w_uq, m_kv_latent_norm, m_w_uk, m_w_uv, m_q_head_norm, m_k_head_norm, m_conv_w, m_conv_b, m_gate_a_w, m_gate_a_b, m_gate_x_w, m_gate_x_b, m_lru_lambda, m_attn_out_norm, m_lru_out_norm, m_w_out, m_ffn2_norm, m_ffn2_w_gate, m_ffn2_w_up, m_ffn2_w_down, m_final_norm, v_meta_tokens, v_ffn1_norm, v_ffn1_w_gate, v_ffn1_w_up, v_ffn1_w_down, v_mix_norm, v_w_in, v_q_latent_norm, v_w_uq, v_kv_latent_norm, v_w_uk, v_w_uv, v_q_head_norm, v_k_head_norm, v_conv_w, v_conv_b, v_gate_a_w, v_gate_a_b, v_gate_x_w, v_gate_x_b, v_lru_lambda, v_attn_out_norm, v_lru_out_norm, v_w_out, v_ffn2_norm, v_ffn2_w_gate, v_ffn2_w_up, v_ffn2_w_down, v_final_norm):
    given = dict(x=x, meta_tokens=meta_tokens, ffn1_norm=ffn1_norm, ffn1_w_gate=ffn1_w_gate, ffn1_w_up=ffn1_w_up, ffn1_w_down=ffn1_w_down, mix_norm=mix_norm, w_in=w_in, q_latent_norm=q_latent_norm, w_uq=w_uq, kv_latent_norm=kv_latent_norm, w_uk=w_uk, w_uv=w_uv, q_head_norm=q_head_norm, k_head_norm=k_head_norm, conv_w=conv_w, conv_b=conv_b, gate_a_w=gate_a_w, gate_a_b=gate_a_b, gate_x_w=gate_x_w, gate_x_b=gate_x_b, lru_lambda=lru_lambda, attn_out_norm=attn_out_norm, lru_out_norm=lru_out_norm, w_out=w_out, ffn2_norm=ffn2_norm, ffn2_w_gate=ffn2_w_gate, ffn2_w_up=ffn2_w_up, ffn2_w_down=ffn2_w_down, final_norm=final_norm, loss_target=loss_target, m_meta_tokens=m_meta_tokens, m_ffn1_norm=m_ffn1_norm, m_ffn1_w_gate=m_ffn1_w_gate, m_ffn1_w_up=m_ffn1_w_up, m_ffn1_w_down=m_ffn1_w_down, m_mix_norm=m_mix_norm, m_w_in=m_w_in, m_q_latent_norm=m_q_latent_norm, m_w_uq=m_w_uq, m_kv_latent_norm=m_kv_latent_norm, m_w_uk=m_w_uk, m_w_uv=m_w_uv, m_q_head_norm=m_q_head_norm, m_k_head_norm=m_k_head_norm, m_conv_w=m_conv_w, m_conv_b=m_conv_b, m_gate_a_w=m_gate_a_w, m_gate_a_b=m_gate_a_b, m_gate_x_w=m_gate_x_w, m_gate_x_b=m_gate_x_b, m_lru_lambda=m_lru_lambda, m_attn_out_norm=m_attn_out_norm, m_lru_out_norm=m_lru_out_norm, m_w_out=m_w_out, m_ffn2_norm=m_ffn2_norm, m_ffn2_w_gate=m_ffn2_w_gate, m_ffn2_w_up=m_ffn2_w_up, m_ffn2_w_down=m_ffn2_w_down, m_final_norm=m_final_norm, v_meta_tokens=v_meta_tokens, v_ffn1_norm=v_ffn1_norm, v_ffn1_w_gate=v_ffn1_w_gate, v_ffn1_w_up=v_ffn1_w_up, v_ffn1_w_down=v_ffn1_w_down, v_mix_norm=v_mix_norm, v_w_in=v_w_in, v_q_latent_norm=v_q_latent_norm, v_w_uq=v_w_uq, v_kv_latent_norm=v_kv_latent_norm, v_w_uk=v_w_uk, v_w_uv=v_w_uv, v_q_head_norm=v_q_head_norm, v_k_head_norm=v_k_head_norm, v_conv_w=v_conv_w, v_conv_b=v_conv_b, v_gate_a_w=v_gate_a_w, v_gate_a_b=v_gate_a_b, v_gate_x_w=v_gate_x_w, v_gate_x_b=v_gate_x_b, v_lru_lambda=v_lru_lambda, v_attn_out_norm=v_attn_out_norm, v_lru_out_norm=v_lru_out_norm, v_w_out=v_w_out, v_ffn2_norm=v_ffn2_norm, v_ffn2_w_gate=v_ffn2_w_gate, v_ffn2_w_up=v_ffn2_w_up, v_ffn2_w_down=v_ffn2_w_down, v_final_norm=v_final_norm)
    weights = {n: given[n] for n in TWIN_WEIGHTS}
    shared = {n: given[n] for n in SHARED_INPUTS}
    per_example = {n: given[n] for n in ['x']}
    grad_fn = _jax.value_and_grad(_loss, argnums=(0, 1))

    def one_microbatch(ex, loss_target):
        ex = dict(ex)
        diff = ex.pop(TWIN_DIFF_INPUT)
        return grad_fn(weights, diff, {**shared, **ex}, loss_target)

    if N_MICROBATCH == 1:
        loss, (grad_w, grad_x) = one_microbatch(per_example, given["loss_target"])
    else:
        def body(carry, xs):
            loss_sum, grad_sum = carry
            l_k, (gw_k, gx_k) = one_microbatch(xs[0], xs[1])
            with _jax.named_scope("update"):
                return (loss_sum + l_k, _jax.tree.map(_jnp.add, grad_sum, gw_k)), gx_k

        init = (_jnp.zeros((), _jnp.float32), _jax.tree.map(_jnp.zeros_like, weights))
        (loss, grad_w), grad_x = _jax.lax.scan(body, init, (per_example, given["loss_target"]))
    with _jax.named_scope("update"):
        delta_w, new_m, new_v = {}, {}, {}
        for n in TWIN_WEIGHTS:
            delta_w[n], new_m[n], new_v[n] = _adamw(weights[n], grad_w[n], given["m_" + n], given["v_" + n])
    return (loss, grad_x, *[grad_w[n] for n in TWIN_WEIGHTS], *[delta_w[n] for n in TWIN_WEIGHTS],
            *[new_m[n] for n in TWIN_WEIGHTS], *[new_v[n] for n in TWIN_WEIGHTS])
```

```python
import functools
import math

import jax
import jax.numpy as jnp
from jax import lax
from jax.experimental import pallas as pl
from jax.experimental.pallas import tpu as pltpu

F32 = jnp.float32
BF16 = jnp.bfloat16
MESH = pl.DeviceIdType.MESH

D_MODEL = 1024
N_META = 16
CHUNK = 64
PAD_ROWS = CHUNK - N_META
HEADS = 4
D_NOPE = 128
D_ROPE = 64
D_QK = D_NOPE + D_ROPE
D_QKP = 256
D_V = 128
KV_RANK = 256
Q_RANK = 384
MLA_W = HEADS * D_V
LRU_W = 512
LRU_TILE = 128
N_LRU_TILES = LRU_W // LRU_TILE
CONV_K = 4
C_RGLRU = 8.0
ROPE_THETA = 10000.0
D_FF = 2816
N_SHARD = 4
EPS = 1e-6
NEG_INF = -1e30
Z_KR = Q_RANK + KV_RANK
Z_MLA = Z_KR + 128
Z_U = Z_MLA
Z_G = Z_U + LRU_W
Z_W = Z_G + LRU_W
IN_WIDTH = Q_RANK + KV_RANK + D_ROPE + 2 * LRU_W

ADAM_LR = 0.001
ADAM_B1 = 0.9
ADAM_B2 = 0.999
ADAM_EPS = 1e-08
ADAM_WD = 0.01
ADAM_STEP = 10

VMEM_LIMIT_BYTES = 56 * 1024 * 1024
SMALL_ROWS = 104
SMALL_ADAM_ROWS = 80


def _params(sem):
    return pltpu.CompilerParams(dimension_semantics=sem, vmem_limit_bytes=VMEM_LIMIT_BYTES)


def _row_tile(rows, target):
    best = 16
    for t in range(16, min(rows, target) + 1, 16):
        if rows % t == 0:
            best = t
    return best


def _col_tile(cols, target):
    best = cols
    for t in range(128, min(cols, target) + 1, 128):
        if cols % t == 0:
            best = t
    return best


def _dot(a, b):
    return jnp.dot(a, b, preferred_element_type=F32)


def _dot_nt(a, b):
    return lax.dot_general(a, b, (((1,), (1,)), ((), ())), preferred_element_type=F32)


def _dot_tn(a, b):
    return lax.dot_general(a, b, (((0,), (0,)), ((), ())), preferred_element_type=F32)


def _rms(x, n):
    return lax.rsqrt(jnp.sum(x * x, axis=-1, keepdims=True) * (1.0 / n) + EPS)


def _rms_bwd(dn, nrm, r, n):
    return r * (dn - nrm * (jnp.sum(dn * nrm, axis=-1, keepdims=True) * (1.0 / n)))


def _gelu(x):
    k = math.sqrt(2.0 / math.pi)
    t = jnp.tanh(k * (x + 0.044715 * x * x * x))
    return 0.5 * x * (1.0 + t), t


def _gelu_grad(x, t):
    k = math.sqrt(2.0 / math.pi)
    return 0.5 * (1.0 + t) + 0.5 * x * (1.0 - t * t) * k * (1.0 + 3.0 * 0.044715 * x * x)


def _neg_expm1(x):
    series = -x * (1.0 + x * (0.5 + x * (1.0 / 6 + x * (1.0 / 24 + x * (1.0 / 120 + x * (1.0 / 720))))))
    return jnp.where(x > -0.25, series, 1.0 - jnp.exp(x))


def _softplus_neg(lam):
    e = jnp.exp(-jnp.abs(lam))
    log1p = jnp.where(e < 0.01, e * (1.0 - e * (0.5 - e * (1.0 / 3 - e * 0.25))), jnp.log(1.0 + e))
    return jnp.maximum(-lam, 0.0) + log1p


def _rope(t, c, s1, s2):
    return t * c + pltpu.roll(t, 96, 1) * s1 + pltpu.roll(t, 32, 1) * s2


def _rope_t(d, c, s1, s2):
    return d * c + pltpu.roll(d * s1, 32, 1) + pltpu.roll(d * s2, 96, 1)


def _rope_tables(lp):
    pos = (jnp.arange(lp, dtype=jnp.int32) - PAD_ROWS).astype(F32)
    inv_freq = ROPE_THETA ** (-jnp.arange(0, D_ROPE // 2, dtype=F32) / (D_ROPE // 2))
    ang = pos[:, None] * inv_freq[None, :]
    cos, sin = jnp.cos(ang), jnp.sin(ang)
    z = jnp.zeros_like(cos)
    return (jnp.concatenate([cos, cos, z, z], 1), jnp.concatenate([-sin, z, z, z], 1),
            jnp.concatenate([z, sin, z, z], 1))


def _rmsnorm_call(name, h, g, tm):
    rows, d = h.shape

    def body(h_ref, g_ref, o_ref):
        x = h_ref[...]
        o_ref[...] = (x * _rms(x, d) * g_ref[...]).astype(BF16)

    return pl.pallas_call(
        body, name=name, grid=(rows // tm,),
        in_specs=[pl.BlockSpec((tm, d), lambda i: (i, 0)), pl.BlockSpec((1, d), lambda i: (0, 0))],
        out_specs=pl.BlockSpec((tm, d), lambda i: (i, 0)),
        out_shape=jax.ShapeDtypeStruct((rows, d), BF16),
        compiler_params=_params(("parallel",)))(h, g)


def _ffn_up_call(name, u, wg, wu, tm):
    rows, d = u.shape
    ns, _, fs = wg.shape

    def body(u_ref, wg_ref, wu_ref, g_ref, p_ref, a_ref):
        uu = u_ref[...]
        g = _dot(uu, wg_ref[0])
        p = _dot(uu, wu_ref[0])
        g_ref[0] = g.astype(BF16)
        p_ref[0] = p.astype(BF16)
        a_ref[0] = (g * jax.nn.sigmoid(g) * p).astype(BF16)

    wspec = pl.BlockSpec((1, d, fs), lambda s, i: (s, 0, 0))
    ospec = pl.BlockSpec((1, tm, fs), lambda s, i: (s, i, 0))
    oshape = jax.ShapeDtypeStruct((ns, rows, fs), BF16)
    return pl.pallas_call(
        body, name=name, grid=(ns, rows // tm),
        in_specs=[pl.BlockSpec((tm, d), lambda s, i: (i, 0)), wspec, wspec],
        out_specs=[ospec, ospec, ospec], out_shape=[oshape, oshape, oshape],
        compiler_params=_params(("parallel", "parallel")))(u, wg, wu)


def _ffn_down_call(name, a, wd, h, tm):
    rows, d = h.shape
    ns, _, fs = a.shape

    def body(a_ref, wd_ref, h_ref, o_ref):
        acc = h_ref[...]
        for s in range(ns):
            acc = acc + 0.5 * _dot(a_ref[s], wd_ref[s])
        o_ref[...] = acc

    return pl.pallas_call(
        body, name=name, grid=(rows // tm,),
        in_specs=[pl.BlockSpec((ns, tm, fs), lambda i: (0, i, 0)),
                  pl.BlockSpec((ns, fs, d), lambda i: (0, 0, 0)),
                  pl.BlockSpec((tm, d), lambda i: (i, 0))],
        out_specs=pl.BlockSpec((tm, d), lambda i: (i, 0)),
        out_shape=jax.ShapeDtypeStruct((rows, d), F32),
        compiler_params=_params(("parallel",)))(a, wd, h)


def _mm_call(name, a, b, tm, out_dtype):
    rows, k = a.shape
    n = b.shape[1]

    def body(a_ref, b_ref, o_ref):
        o_ref[...] = _dot(a_ref[...], b_ref[...]).astype(out_dtype)

    return pl.pallas_call(
        body, name=name, grid=(rows // tm,),
        in_specs=[pl.BlockSpec((tm, k), lambda i: (i, 0)), pl.BlockSpec((k, n), lambda i: (0, 0))],
        out_specs=pl.BlockSpec((tm, n), lambda i: (i, 0)),
        out_shape=jax.ShapeDtypeStruct((rows, n), out_dtype),
        compiler_params=_params(("parallel",)))(a, b)


def _mla_heads(z, gql, gkvl, wuq, wuk, wuv):
    cq = z[:, 0:Q_RANK]
    ckv = z[:, Q_RANK:Z_KR]
    kr = z[:, Z_KR:Z_MLA]
    rq = _rms(cq, Q_RANK)
    nq = cq * rq
    cqn = (nq * gql).astype(BF16)
    rkv = _rms(ckv, KV_RANK)
    nkv = ckv * rkv
    ckvn = (nkv * gkvl).astype(BF16)
    qraw = _dot(cqn, wuq)
    knope = _dot(ckvn, wuk)
    v = _dot(ckvn, wuv)
    skr = jnp.sum(kr * kr, axis=-1, keepdims=True)
    heads = []
    for hd in range(HEADS):
        qh = qraw[:, hd * D_QKP:(hd + 1) * D_QKP]
        rqh = lax.rsqrt(jnp.sum(qh * qh, axis=-1, keepdims=True) * (1.0 / D_QK) + EPS)
        kn = knope[:, hd * D_NOPE:(hd + 1) * D_NOPE]
        rkh = lax.rsqrt((jnp.sum(kn * kn, axis=-1, keepdims=True) + skr) * (1.0 / D_QK) + EPS)
        heads.append((qh * rqh, rqh, kn * rkh, kr * rkh, rkh))
    return dict(rq=rq, nq=nq, cqn=cqn, rkv=rkv, nkv=nkv, ckvn=ckvn, v=v, heads=heads)


def _mla_prep_call(z, gql, gkvl, gqh, gkh, wuq, wuk, wuv, tabs, lp, tm):
    rows = z.shape[0]
    tpe = lp // tm

    def body(z_ref, gql_ref, gkvl_ref, gqh_ref, gkh_ref, wuq_ref, wuk_ref, wuv_ref, c_ref, s1_ref, s2_ref,
             q_ref, k_ref, v_ref, cqn_ref, ckvn_ref):
        m = _mla_heads(z_ref[...], gql_ref[...], gkvl_ref[...], wuq_ref[...], wuk_ref[...], wuv_ref[...])
        c, s1, s2 = c_ref[...], s1_ref[...], s2_ref[...]
        gq, gk = gqh_ref[...], gkh_ref[...]
        for hd in range(HEADS):
            qn, _, knn, krn, _ = m["heads"][hd]
            qg = qn * gq
            q_ref[hd, :, 0:D_NOPE] = qg[:, 0:D_NOPE].astype(BF16)
            q_ref[hd, :, D_NOPE:D_QKP] = _rope(qg[:, D_NOPE:D_QKP], c, s1, s2).astype(BF16)
            k_ref[hd, :, 0:D_NOPE] = (knn * gk[:, 0:D_NOPE]).astype(BF16)
            k_ref[hd, :, D_NOPE:D_QKP] = _rope(krn * gk[:, D_NOPE:D_QKP], c, s1, s2).astype(BF16)
            v_ref[hd] = m["v"][:, hd * D_V:(hd + 1) * D_V].astype(BF16)
        cqn_ref[...] = m["cqn"]
        ckvn_ref[...] = m["ckvn"]

    def const(shape):
        return pl.BlockSpec(shape, lambda i: tuple(0 for _ in shape))

    tab = pl.BlockSpec((tm, 128), lambda i: (i % tpe, 0))
    return pl.pallas_call(
        body, name="mla_prep", grid=(rows // tm,),
        in_specs=[pl.BlockSpec((tm, Z_MLA), lambda i: (i, 0)), const((1, Q_RANK)), const((1, KV_RANK)),
                  const((1, D_QKP)), const((1, D_QKP)), const((Q_RANK, HEADS * D_QKP)),
                  const((KV_RANK, HEADS * D_NOPE)), const((KV_RANK, HEADS * D_V)), tab, tab, tab],
        out_specs=[pl.BlockSpec((HEADS, tm, D_QKP), lambda i: (0, i, 0)),
                   pl.BlockSpec((HEADS, tm, D_QKP), lambda i: (0, i, 0)),
                   pl.BlockSpec((HEADS, tm, D_V), lambda i: (0, i, 0)),
                   pl.BlockSpec((tm, Q_RANK), lambda i: (i, 0)),
                   pl.BlockSpec((tm, KV_RANK), lambda i: (i, 0))],
        out_shape=[jax.ShapeDtypeStruct((HEADS, rows, D_QKP), BF16),
                   jax.ShapeDtypeStruct((HEADS, rows, D_QKP), BF16),
                   jax.ShapeDtypeStruct((HEADS, rows, D_V), BF16),
                   jax.ShapeDtypeStruct((rows, Q_RANK), BF16),
                   jax.ShapeDtypeStruct((rows, KV_RANK), BF16)],
        compiler_params=_params(("parallel",)))(z, gql, gkvl, gqh, gkh, wuq, wuk, wuv, *tabs)


def _q_block(lp):
    chunks = lp // CHUNK
    best = 1
    for g in range(1, 5):
        if chunks % g == 0:
            best = g
    return best * CHUNK


def _attn_mask(j0, qb, ext):
    qrow = j0 + lax.broadcasted_iota(jnp.int32, (qb, ext), 0)
    krow = lax.broadcasted_iota(jnp.int32, (qb, ext), 1)
    shift = CHUNK.bit_length() - 1
    return (jnp.right_shift(krow, shift) <= jnp.right_shift(qrow, shift)) & (krow >= PAD_ROWS)


def _attn_fwd_call(q, k, v, nb, lp):
    rows = nb * lp
    qb = _q_block(lp)
    scale = 1.0 / math.sqrt(D_QK)

    def body(q_ref, k_ref, v_ref, o_ref, lse_ref):
        for j in range(lp // qb):
            j0, ext = j * qb, (j + 1) * qb
            s = _dot_nt(q_ref[0, j0:ext, :], k_ref[0, 0:ext, :]) * scale
            s = jnp.where(_attn_mask(j0, qb, ext), s, NEG_INF)
            mx = jnp.max(s, axis=-1, keepdims=True)
            p = jnp.exp(s - mx)
            l = jnp.sum(p, axis=-1, keepdims=True)
            o = _dot(p.astype(BF16), v_ref[0, 0:ext, :])
            o_ref[j0:ext, :] = o / l
            lse_ref[0, j0:ext, :] = mx + jnp.log(l)

    return pl.pallas_call(
        body, name="attn_fwd", grid=(nb, HEADS),
        in_specs=[pl.BlockSpec((1, lp, D_QKP), lambda b, h: (h, b, 0)),
                  pl.BlockSpec((1, lp, D_QKP), lambda b, h: (h, b, 0)),
                  pl.BlockSpec((1, lp, D_V), lambda b, h: (h, b, 0))],
        out_specs=[pl.BlockSpec((lp, D_V), lambda b, h: (b, h)),
                   pl.BlockSpec((1, lp, 1), lambda b, h: (h, b, 0))],
        out_shape=[jax.ShapeDtypeStruct((rows, MLA_W), F32),
                   jax.ShapeDtypeStruct((HEADS, rows, 1), F32)],
        compiler_params=_params(("parallel", "parallel")))(q, k, v)


def _attn_bwd_call(q, k, v, o, lse, do, nb, lp):
    rows = nb * lp
    qb = _q_block(lp)
    scale = 1.0 / math.sqrt(D_QK)

    def body(q_ref, k_ref, v_ref, o_ref, lse_ref, do_ref, dq_ref, dk_ref, dv_ref, dk_acc, dv_acc):
        dk_acc[...] = jnp.zeros_like(dk_acc)
        dv_acc[...] = jnp.zeros_like(dv_acc)
        for j in range(lp // qb):
            j0, ext = j * qb, (j + 1) * qb
            qj = q_ref[0, j0:ext, :]
            doj = do_ref[j0:ext, :]
            delta = jnp.sum(doj * o_ref[j0:ext, :], axis=-1, keepdims=True)
            dob = doj.astype(BF16)
            kk = k_ref[0, 0:ext, :]
            s = _dot_nt(qj, kk) * scale
            p = jnp.where(_attn_mask(j0, qb, ext), jnp.exp(s - lse_ref[0, j0:ext, :]), 0.0)
            dv_acc[0:ext, :] += _dot_tn(p.astype(BF16), dob)
            dp = _dot_nt(dob, v_ref[0, 0:ext, :])
            ds = (p * (dp - delta) * scale).astype(BF16)
            dq_ref[0, j0:ext, :] = _dot(ds, kk).astype(BF16)
            dk_acc[0:ext, :] += _dot_tn(ds, qj)
        dk_ref[0] = dk_acc[...].astype(BF16)
        dv_ref[0] = dv_acc[...].astype(BF16)

    qspec = pl.BlockSpec((1, lp, D_QKP), lambda b, h: (h, b, 0))
    vspec = pl.BlockSpec((1, lp, D_V), lambda b, h: (h, b, 0))
    ospec = pl.BlockSpec((lp, D_V), lambda b, h: (b, h))
    return pl.pallas_call(
        body, name="attn_bwd", grid=(nb, HEADS),
        in_specs=[qspec, qspec, vspec, ospec, pl.BlockSpec((1, lp, 1), lambda b, h: (h, b, 0)), ospec],
        out_specs=[qspec, qspec, vspec],
        out_shape=[jax.ShapeDtypeStruct((HEADS, rows, D_QKP), BF16),
                   jax.ShapeDtypeStruct((HEADS, rows, D_QKP), BF16),
                   jax.ShapeDtypeStruct((HEADS, rows, D_V), BF16)],
        scratch_shapes=[pltpu.VMEM((lp, D_QKP), F32), pltpu.VMEM((lp, D_V), F32)],
        compiler_params=_params(("parallel", "parallel")))(q, k, v, o, lse, do)


def _lru_gates(u, cw, cb, wa, ba, wx, bx, lam, lp):
    xc = (cw[3:4, :] * u + cw[2:3, :] * pltpu.roll(u, 1, 0) + cw[1:2, :] * pltpu.roll(u, 2, 0)
          + cw[0:1, :] * pltpu.roll(u, 3, 0) + cb)
    xcb = xc.astype(BF16)
    r = jax.nn.sigmoid(_dot(xcb, wa) + ba)
    i = jax.nn.sigmoid(_dot(xcb, wx) + bx)
    sp = _softplus_neg(lam)
    la = -C_RGLRU * r * sp
    a = jnp.exp(la)
    mult = jnp.sqrt(_neg_expm1(2.0 * la))
    row = lax.broadcasted_iota(jnp.int32, (lp, LRU_TILE), 0)
    first = row == PAD_ROWS
    valid = row >= PAD_ROWS
    mult_eff = jnp.where(first, 1.0, mult)
    return dict(xc=xc, xcb=xcb, r=r, i=i, sp=sp, la=la, a=a, mult=mult, mult_eff=mult_eff, first=first, valid=valid)


def _select_rows(rows8):
    n = rows8[0].shape[1]
    sub = lax.broadcasted_iota(jnp.int32, (8, n), 0)
    out = jnp.broadcast_to(rows8[0], (8, n))
    for j in range(1, 8):
        out = jnp.where(sub == j, jnp.broadcast_to(rows8[j], (8, n)), out)
    return out


def _lru_specs(lp):
    seq = lambda col0: pl.BlockSpec((lp, LRU_TILE), lambda t, b: (b, col0 + t))
    cw = pl.BlockSpec((1, CONV_K, LRU_TILE), lambda t, b: (t, 0, 0))
    vec = pl.BlockSpec((1, LRU_TILE), lambda t, b: (0, t))
    mat = pl.BlockSpec((1, LRU_TILE, LRU_TILE), lambda t, b: (t, 0, 0))
    return seq, cw, vec, mat


def _lru_fwd_call(z, cw, cb, wa, ba, wx, bx, lam, nb, lp):
    rows = nb * lp
    seq, cwspec, vec, mat = _lru_specs(lp)

    def body(u_ref, g_ref, cw_ref, cb_ref, wa_ref, ba_ref, wx_ref, bx_ref, lam_ref, y_ref, hs_ref, hp_ref, a_s, b_s):
        m = _lru_gates(u_ref[...], cw_ref[0], cb_ref[...], wa_ref[0], ba_ref[...], wx_ref[0], bx_ref[...],
                       lam_ref[...], lp)
        a_s[...] = jnp.where(m["valid"], m["a"], 0.0)
        b_s[...] = jnp.where(m["valid"], m["mult_eff"] * (m["i"] * m["xc"]), 0.0)

        def group(gi, h):
            r0 = pl.multiple_of(gi * 8, 8)
            a8 = a_s[pl.ds(r0, 8), :]
            b8 = b_s[pl.ds(r0, 8), :]
            prev, cur = [], []
            for j in range(8):
                prev.append(h)
                h = a8[j:j + 1, :] * h + b8[j:j + 1, :]
                cur.append(h)
            hs_ref[pl.ds(r0, 8), :] = _select_rows(cur)
            hp_ref[pl.ds(r0, 8), :] = _select_rows(prev)
            return h

        lax.fori_loop(0, lp // 8, group, jnp.zeros((1, LRU_TILE), F32))
        gl, _ = _gelu(g_ref[...])
        y_ref[...] = hs_ref[...] * gl

    oshape = jax.ShapeDtypeStruct((rows, LRU_W), F32)
    return pl.pallas_call(
        body, name="lru_fwd", grid=(N_LRU_TILES, nb),
        in_specs=[seq(Z_U // LRU_TILE), seq(Z_G // LRU_TILE), cwspec, vec, mat, vec, mat, vec, vec],
        out_specs=[seq(0), seq(0), seq(0)], out_shape=[oshape, oshape, oshape],
        scratch_shapes=[pltpu.VMEM((lp, LRU_TILE), F32), pltpu.VMEM((lp, LRU_TILE), F32)],
        compiler_params=_params(("parallel", "parallel")))(z, z, cw, cb, wa, ba, wx, bx, lam)


def _lru_bwd_call(z, hs, hp, dy, cw, cb, wa, ba, wx, bx, lam, nb, lp):
    rows = nb * lp
    seq, cwspec, vec, mat = _lru_specs(lp)

    def body(u_ref, g_ref, hs_ref, hp_ref, dy_ref, cw_ref, cb_ref, wa_ref, ba_ref, wx_ref, bx_ref, lam_ref,
             du_ref, dg_ref, dcw_ref, dcb_ref, dwa_ref, dba_ref, dwx_ref, dbx_ref, dlam_ref, an_s, d_s):
        b_idx = pl.program_id(1)
        u = u_ref[...]
        cw = cw_ref[0]
        wa, wx = wa_ref[0], wx_ref[0]
        lam = lam_ref[...]
        m = _lru_gates(u, cw, cb_ref[...], wa, ba_ref[...], wx, bx_ref[...], lam, lp)
        gate = g_ref[...]
        gl, th = _gelu(gate)
        dy = dy_ref[...]
        dg_ref[...] = (dy * hs_ref[...] * _gelu_grad(gate, th)).astype(BF16)
        a_eff = jnp.where(m["valid"], m["a"], 0.0)
        an_s[...] = pltpu.roll(a_eff, lp - 1, 0)
        d_s[...] = dy * gl

        def group(gi, carry):
            r0 = pl.multiple_of((lp // 8 - 1 - gi) * 8, 8)
            a8 = an_s[pl.ds(r0, 8), :]
            d8 = d_s[pl.ds(r0, 8), :]
            cur = [None] * 8
            for j in range(7, -1, -1):
                carry = d8[j:j + 1, :] + a8[j:j + 1, :] * carry
                cur[j] = carry
            d_s[pl.ds(r0, 8), :] = _select_rows(cur)
            return carry

        lax.fori_loop(0, lp // 8, group, jnp.zeros((1, LRU_TILE), F32))
        ds = d_s[...]
        xc, r, i = m["xc"], m["r"], m["i"]
        da = ds * hp_ref[...]
        db = jnp.where(m["valid"], ds, 0.0)
        di = db * m["mult_eff"] * xc
        dxc = db * m["mult_eff"] * i
        live = m["valid"] & jnp.logical_not(m["first"])
        e2 = jnp.exp(2.0 * m["la"])
        dm = jnp.where(live, db * i * xc, 0.0)
        dla = da * m["a"] + jnp.where(live, dm * (-e2 / m["mult"]), 0.0)
        dr = dla * (-C_RGLRU * m["sp"])
        dsp = jnp.sum(dla * (-C_RGLRU * r), axis=0, keepdims=True)
        dpr = (dr * r * (1.0 - r))
        dpi = (di * i * (1.0 - i))
        dprb, dpib = dpr.astype(BF16), dpi.astype(BF16)
        dxc = dxc + _dot_nt(dprb, wa) + _dot_nt(dpib, wx)
        du = (cw[3:4, :] * dxc + cw[2:3, :] * pltpu.roll(dxc, lp - 1, 0) + cw[1:2, :] * pltpu.roll(dxc, lp - 2, 0)
              + cw[0:1, :] * pltpu.roll(dxc, lp - 3, 0))
        du_ref[...] = jnp.where(m["valid"], du, 0.0).astype(BF16)
        tap = lax.broadcasted_iota(jnp.int32, (CONV_K, LRU_TILE), 0)
        dcw = jnp.zeros((CONV_K, LRU_TILE), F32)
        for kk in range(CONV_K):
            shifted = u if kk == CONV_K - 1 else pltpu.roll(u, CONV_K - 1 - kk, 0)
            dcw = jnp.where(tap == kk, jnp.sum(dxc * shifted, axis=0, keepdims=True), dcw)
        parts = [(dcw_ref, dcw[None]), (dcb_ref, jnp.sum(dxc, axis=0, keepdims=True)[None]),
                 (dwa_ref, _dot_tn(m["xcb"], dprb)[None]), (dba_ref, jnp.sum(dpr, axis=0, keepdims=True)[None]),
                 (dwx_ref, _dot_tn(m["xcb"], dpib)[None]), (dbx_ref, jnp.sum(dpi, axis=0, keepdims=True)[None]),
                 (dlam_ref, (dsp * (-jax.nn.sigmoid(-lam)))[None])]

        @pl.when(b_idx == 0)
        def _():
            for ref, val in parts:
                ref[...] = val

        @pl.when(b_idx != 0)
        def _():
            for ref, val in parts:
                ref[...] += val

    bshape = jax.ShapeDtypeStruct((rows, LRU_W), BF16)
    vec3 = pl.BlockSpec((1, 1, LRU_TILE), lambda t, b: (t, 0, 0))
    vshape = jax.ShapeDtypeStruct((N_LRU_TILES, 1, LRU_TILE), F32)
    mshape = jax.ShapeDtypeStruct((N_LRU_TILES, LRU_TILE, LRU_TILE), F32)
    return pl.pallas_call(
        body, name="lru_bwd", grid=(N_LRU_TILES, nb),
        in_specs=[seq(Z_U // LRU_TILE), seq(Z_G // LRU_TILE), seq(0), seq(0), seq(0), cwspec, vec, mat, vec, mat,
                  vec, vec],
        out_specs=[seq(0), seq(0), cwspec, vec3, mat, vec3, mat, vec3, vec3],
        out_shape=[bshape, bshape, jax.ShapeDtypeStruct((N_LRU_TILES, CONV_K, LRU_TILE), F32), vshape, mshape,
                   vshape, mshape, vshape, vshape],
        scratch_shapes=[pltpu.VMEM((lp, LRU_TILE), F32), pltpu.VMEM((lp, LRU_TILE), F32)],
        compiler_params=_params(("parallel", "arbitrary")))(z, z, hs, hp, dy, cw, cb, wa, ba, wx, bx, lam)


def _mix_out_call(ya, yl, ga, gl, wout, h, tm):
    rows, d = h.shape

    def body(ya_ref, yl_ref, ga_ref, gl_ref, w_ref, h_ref, y_ref, o_ref):
        a = ya_ref[...]
        l = yl_ref[...]
        an = (a * _rms(a, MLA_W) * ga_ref[...]).astype(BF16)
        ln = (l * _rms(l, LRU_W) * gl_ref[...]).astype(BF16)
        y_ref[:, 0:MLA_W] = an
        y_ref[:, MLA_W:MLA_W + LRU_W] = ln
        o_ref[...] = h_ref[...] + _dot(an, w_ref[0:MLA_W, :]) + _dot(ln, w_ref[MLA_W:MLA_W + LRU_W, :])

    half = pl.BlockSpec((tm, MLA_W), lambda i: (i, 0))
    g = pl.BlockSpec((1, MLA_W), lambda i: (0, 0))
    full = pl.BlockSpec((tm, d), lambda i: (i, 0))
    return pl.pallas_call(
        body, name="mix_out", grid=(rows // tm,),
        in_specs=[half, half, g, g, pl.BlockSpec((MLA_W + LRU_W, d), lambda i: (0, 0)), full],
        out_specs=[full, full],
        out_shape=[jax.ShapeDtypeStruct((rows, MLA_W + LRU_W), BF16), jax.ShapeDtypeStruct((rows, d), F32)],
        compiler_params=_params(("parallel",)))(ya, yl, ga, gl, wout, h)


def _mix_out_bwd_call(dhb, wout, ya, yl, ga, gl, tm):
    rows = ya.shape[0]
    d = dhb.shape[1]

    def body(dh_ref, w_ref, ya_ref, yl_ref, ga_ref, gl_ref, dya_ref, dyl_ref, dga_ref, dgl_ref):
        dy = _dot_nt(dh_ref[...], w_ref[...])
        outs = []
        for val, g_ref, lo, out_ref in ((ya_ref[...], ga_ref, 0, dya_ref), (yl_ref[...], gl_ref, MLA_W, dyl_ref)):
            r = _rms(val, MLA_W)
            n = val * r
            dyn = dy[:, lo:lo + MLA_W]
            out_ref[...] = _rms_bwd(dyn * g_ref[...], n, r, MLA_W)
            outs.append(jnp.sum(dyn * n, axis=0, keepdims=True))

        @pl.when(pl.program_id(0) == 0)
        def _():
            dga_ref[...] = outs[0]
            dgl_ref[...] = outs[1]

        @pl.when(pl.program_id(0) != 0)
        def _():
            dga_ref[...] += outs[0]
            dgl_ref[...] += outs[1]

    half = pl.BlockSpec((tm, MLA_W), lambda i: (i, 0))
    g = pl.BlockSpec((1, MLA_W), lambda i: (0, 0))
    return pl.pallas_call(
        body, name="mix_out_bwd", grid=(rows // tm,),
        in_specs=[pl.BlockSpec((tm, d), lambda i: (i, 0)), pl.BlockSpec((MLA_W + LRU_W, d), lambda i: (0, 0)),
                  half, half, g, g],
        out_specs=[half, half, g, g],
        out_shape=[jax.ShapeDtypeStruct((rows, MLA_W), F32), jax.ShapeDtypeStruct((rows, LRU_W), F32),
                   jax.ShapeDtypeStruct((1, MLA_W), F32), jax.ShapeDtypeStruct((1, LRU_W), F32)],
        compiler_params=_params(("arbitrary",)))(dhb, wout, ya, yl, ga, gl)


def _final_call(h, g, target, lp, tm):
    rows, d = h.shape
    tpe = lp // tm

    def body(h_ref, g_ref, t_ref, dh_ref, dhb_ref, dg_ref, loss_ref):
        i = pl.program_id(0)
        x = h_ref[...]
        g = g_ref[...]
        r = _rms(x, d)
        n = x * r
        row = (i % tpe) * tm + lax.broadcasted_iota(jnp.int32, (tm, 1), 0)
        err = jnp.where(row >= CHUNK, n * g - t_ref[...], 0.0)
        dout = err * (1.0 / d)
        dh = _rms_bwd(dout * g, n, r, d)
        dh_ref[...] = dh
        dhb_ref[...] = dh.astype(BF16)
        dg = jnp.sum(dout * n, axis=0, keepdims=True)
        part = jnp.sum(jnp.sum(err * err, axis=1, keepdims=True), axis=0, keepdims=True) * (0.5 / d)
        loss = jnp.broadcast_to(part, (1, 128))

        @pl.when(i == 0)
        def _():
            dg_ref[...] = dg
            loss_ref[...] = loss

        @pl.when(i != 0)
        def _():
            dg_ref[...] += dg
            loss_ref[...] += loss

    full = pl.BlockSpec((tm, d), lambda i: (i, 0))
    return pl.pallas_call(
        body, name="final_loss", grid=(rows // tm,),
        in_specs=[full, pl.BlockSpec((1, d), lambda i: (0, 0)), full],
        out_specs=[full, full, pl.BlockSpec((1, d), lambda i: (0, 0)), pl.BlockSpec((1, 128), lambda i: (0, 0))],
        out_shape=[jax.ShapeDtypeStruct((rows, d), F32), jax.ShapeDtypeStruct((rows, d), BF16),
                   jax.ShapeDtypeStruct((1, d), F32), jax.ShapeDtypeStruct((1, 128), F32)],
        compiler_params=_params(("arbitrary",)))(h, g, target)


def _ffn_dact_call(name, dhb, wd, gate, up, tm):
    rows, d = dhb.shape
    ns, fs, _ = wd.shape

    def body(dh_ref, wd_ref, g_ref, p_ref, dg_ref, dp_ref):
        da = 0.5 * _dot_nt(dh_ref[...], wd_ref[0])
        g = g_ref[0].astype(F32)
        p = p_ref[0].astype(F32)
        sg = jax.nn.sigmoid(g)
        dg_ref[0] = (da * p * sg * (1.0 + g * (1.0 - sg))).astype(BF16)
        dp_ref[0] = (da * g * sg).astype(BF16)

    aspec = pl.BlockSpec((1, tm, fs), lambda s, i: (s, i, 0))
    oshape = jax.ShapeDtypeStruct((ns, rows, fs), BF16)
    return pl.pallas_call(
        body, name=name, grid=(ns, rows // tm),
        in_specs=[pl.BlockSpec((tm, d), lambda s, i: (i, 0)), pl.BlockSpec((1, fs, d), lambda s, i: (s, 0, 0)),
                  aspec, aspec],
        out_specs=[aspec, aspec], out_shape=[oshape, oshape],
        compiler_params=_params(("parallel", "parallel")))(dhb, wd, gate, up)


def _norm_in_bwd_call(name, pieces, h, g, dres, tm):
    rows, d = h.shape
    npc = len(pieces)

    def body(*refs):
        d_refs = refs[0:2 * npc:2]
        w_refs = refs[1:2 * npc:2]
        h_ref, g_ref, dres_ref, dh_ref, dhb_ref, dg_ref = refs[2 * npc:]
        du = jnp.zeros((tm, d), F32)
        for d_ref, w_ref in zip(d_refs, w_refs):
            if len(d_ref.shape) == 3:
                for s in range(d_ref.shape[0]):
                    du = du + _dot_nt(d_ref[s], w_ref[s])
            else:
                du = du + _dot_nt(d_ref[...], w_ref[...])
        x = h_ref[...]
        r = _rms(x, d)
        n = x * r
        dh = dres_ref[...] + _rms_bwd(du * g_ref[...], n, r, d)
        dh_ref[...] = dh
        dhb_ref[...] = dh.astype(BF16)
        dg = jnp.sum(du * n, axis=0, keepdims=True)

        @pl.when(pl.program_id(0) == 0)
        def _():
            dg_ref[...] = dg

        @pl.when(pl.program_id(0) != 0)
        def _():
            dg_ref[...] += dg

    in_specs, args = [], []
    for dd, w in pieces:
        if dd.ndim == 3:
            in_specs.append(pl.BlockSpec((dd.shape[0], tm, dd.shape[2]), lambda i: (0, i, 0)))
            in_specs.append(pl.BlockSpec(w.shape, lambda i: (0, 0, 0)))
        else:
            in_specs.append(pl.BlockSpec((tm, dd.shape[1]), lambda i: (i, 0)))
            in_specs.append(pl.BlockSpec(w.shape, lambda i: (0, 0)))
        args += [dd, w]
    full = pl.BlockSpec((tm, d), lambda i: (i, 0))
    gspec = pl.BlockSpec((1, d), lambda i: (0, 0))
    return pl.pallas_call(
        body, name=name, grid=(rows // tm,),
        in_specs=in_specs + [full, gspec, full],
        out_specs=[full, full, gspec],
        out_shape=[jax.ShapeDtypeStruct((rows, d), F32), jax.ShapeDtypeStruct((rows, d), BF16),
                   jax.ShapeDtypeStruct((1, d), F32)],
        compiler_params=_params(("arbitrary",)))(*args, h, g, dres)


def _wgrad_call(name, a, b, scale=1.0):
    a3, b3 = a.ndim == 3, b.ndim == 3
    ns = a.shape[0] if a3 else (b.shape[0] if b3 else 1)
    rows, m = a.shape[-2:]
    n = b.shape[-1]
    tmm = m if a3 else _col_tile(m, 256)

    def body(a_ref, b_ref, o_ref):
        av = a_ref[0] if a3 else a_ref[...]
        bv = b_ref[0] if b3 else b_ref[...]
        res = _dot_tn(av, bv)
        if scale != 1.0:
            res = res * scale
        if a3 or b3:
            o_ref[0] = res
        else:
            o_ref[...] = res

    aspec = (pl.BlockSpec((1, rows, tmm), lambda s, j: (s, 0, j)) if a3
             else pl.BlockSpec((rows, tmm), lambda s, j: (0, j)))
    bspec = (pl.BlockSpec((1, rows, n), lambda s, j: (s, 0, 0)) if b3
             else pl.BlockSpec((rows, n), lambda s, j: (0, 0)))
    if a3 or b3:
        ospec = pl.BlockSpec((1, tmm, n), lambda s, j: (s, j, 0))
        oshape = jax.ShapeDtypeStruct((ns, m, n), F32)
    else:
        ospec = pl.BlockSpec((tmm, n), lambda s, j: (j, 0))
        oshape = jax.ShapeDtypeStruct((m, n), F32)
    return pl.pallas_call(
        body, name=name, grid=(ns, m // tmm), in_specs=[aspec, bspec], out_specs=ospec, out_shape=oshape,
        compiler_params=_params(("parallel", "parallel")))(a, b)


def _mla_prep_bwd_call(z, dq, dk, dv, gql, gkvl, gqh, gkh, wuq, wuk, wuv, tabs, lp, tm):
    rows = z.shape[0]
    tpe = lp // tm

    def body(z_ref, dq_ref, dk_ref, dv_ref, gql_ref, gkvl_ref, gqh_ref, gkh_ref, wuq_ref, wuk_ref, wuv_ref,
             c_ref, s1_ref, s2_ref, dz_ref, dqp_ref, dkn_ref, dvv_ref, dgql_ref, dgkvl_ref, dgqh_ref, dgkh_ref):
        gql, gkvl = gql_ref[...], gkvl_ref[...]
        gq, gk = gqh_ref[...], gkh_ref[...]
        wuq, wuk, wuv = wuq_ref[...], wuk_ref[...], wuv_ref[...]
        m = _mla_heads(z_ref[...], gql, gkvl, wuq, wuk, wuv)
        c, s1, s2 = c_ref[...], s1_ref[...], s2_ref[...]
        dgq = jnp.zeros((1, D_QKP), F32)
        dgk = jnp.zeros((1, D_QKP), F32)
        dkr = jnp.zeros((tm, D_QKP - D_NOPE), F32)
        for hd in range(HEADS):
            qn, rqh, knn, krn, rkh = m["heads"][hd]
            dqg = jnp.concatenate([dq_ref[hd, :, 0:D_NOPE].astype(F32),
                                   _rope_t(dq_ref[hd, :, D_NOPE:D_QKP].astype(F32), c, s1, s2)], axis=1)
            dgq = dgq + jnp.sum(dqg * qn, axis=0, keepdims=True)
            dqn = dqg * gq
            dqr = rqh * (dqn - qn * (jnp.sum(dqn * qn, axis=-1, keepdims=True) * (1.0 / D_QK)))
            dqp_ref[:, hd * D_QKP:(hd + 1) * D_QKP] = dqr.astype(BF16)
            kn_full = jnp.concatenate([knn, krn], axis=1)
            dkg = jnp.concatenate([dk_ref[hd, :, 0:D_NOPE].astype(F32),
                                   _rope_t(dk_ref[hd, :, D_NOPE:D_QKP].astype(F32), c, s1, s2)], axis=1)
            dgk = dgk + jnp.sum(dkg * kn_full, axis=0, keepdims=True)
            dkn = dkg * gk
            dkraw = rkh * (dkn - kn_full * (jnp.sum(dkn * kn_full, axis=-1, keepdims=True) * (1.0 / D_QK)))
            dkn_ref[:, hd * D_NOPE:(hd + 1) * D_NOPE] = dkraw[:, 0:D_NOPE].astype(BF16)
            dkr = dkr + dkraw[:, D_NOPE:D_QKP]
            dvv_ref[:, hd * D_V:(hd + 1) * D_V] = dv_ref[hd]
        dcqn = _dot_nt(dqp_ref[...], wuq)
        dckvn = _dot_nt(dkn_ref[...], wuk) + _dot_nt(dvv_ref[...], wuv)
        dz_ref[:, 0:Q_RANK] = _rms_bwd(dcqn * gql, m["nq"], m["rq"], Q_RANK).astype(BF16)
        dz_ref[:, Q_RANK:Z_KR] = _rms_bwd(dckvn * gkvl, m["nkv"], m["rkv"], KV_RANK).astype(BF16)
        dz_ref[:, Z_KR:Z_MLA] = dkr.astype(BF16)
        parts = [(dgql_ref, jnp.sum(dcqn * m["nq"], axis=0, keepdims=True)),
                 (dgkvl_ref, jnp.sum(dckvn * m["nkv"], axis=0, keepdims=True)), (dgqh_ref, dgq), (dgkh_ref, dgk)]

        @pl.when(pl.program_id(0) == 0)
        def _():
            for ref, val in parts:
                ref[...] = val

        @pl.when(pl.program_id(0) != 0)
        def _():
            for ref, val in parts:
                ref[...] += val

    def const(shape):
        return pl.BlockSpec(shape, lambda i: tuple(0 for _ in shape))

    tab = pl.BlockSpec((tm, 128), lambda i: (i % tpe, 0))
    hq = pl.BlockSpec((HEADS, tm, D_QKP), lambda i: (0, i, 0))
    hv = pl.BlockSpec((HEADS, tm, D_V), lambda i: (0, i, 0))

    def rowspec(n):
        return pl.BlockSpec((tm, n), lambda i: (i, 0))

    return pl.pallas_call(
        body, name="mla_prep_bwd", grid=(rows // tm,),
        in_specs=[rowspec(Z_MLA), hq, hq, hv, const((1, Q_RANK)), const((1, KV_RANK)), const((1, D_QKP)),
                  const((1, D_QKP)), const((Q_RANK, HEADS * D_QKP)), const((KV_RANK, HEADS * D_NOPE)),
                  const((KV_RANK, HEADS * D_V)), tab, tab, tab],
        out_specs=[rowspec(Z_MLA), rowspec(HEADS * D_QKP), rowspec(HEADS * D_NOPE), rowspec(HEADS * D_V),
                   const((1, Q_RANK)), const((1, KV_RANK)), const((1, D_QKP)), const((1, D_QKP))],
        out_shape=[jax.ShapeDtypeStruct((rows, Z_MLA), BF16), jax.ShapeDtypeStruct((rows, HEADS * D_QKP), BF16),
                   jax.ShapeDtypeStruct((rows, HEADS * D_NOPE), BF16), jax.ShapeDtypeStruct((rows, HEADS * D_V), BF16),
                   jax.ShapeDtypeStruct((1, Q_RANK), F32), jax.ShapeDtypeStruct((1, KV_RANK), F32),
                   jax.ShapeDtypeStruct((1, D_QKP), F32), jax.ShapeDtypeStruct((1, D_QKP), F32)],
        compiler_params=_params(("arbitrary",)))(z, dq, dk, dv, gql, gkvl, gqh, gkh, wuq, wuk, wuv, *tabs)


def _local_step(h0, target, w, nb, lp):
    tm = _row_tile(lp, 1024)
    te = _row_tile(lp, 512)
    tabs = _rope_tables(lp)
    g = {}

    def ffn_fwd(tag, h):
        u = _rmsnorm_call(tag + "_norm", h, w[tag + "_norm"], te)
        gate, up, act = _ffn_up_call(tag + "_up", u, w[tag + "_w_gate"], w[tag + "_w_up"], tm)
        return _ffn_down_call(tag + "_down", act, w[tag + "_w_down"], h, te), (u, gate, up, act)

    def ffn_bwd(tag, h, saved, dh, dhb):
        u, gate, up, act = saved
        dgate, dup = _ffn_dact_call(tag + "_dact", dhb, w[tag + "_w_down"], gate, up, tm)
        g[tag + "_w_down"] = _wgrad_call(tag + "_dwd", act, dhb, 0.5)
        g[tag + "_w_gate"] = _wgrad_call(tag + "_dwg", u, dgate)
        g[tag + "_w_up"] = _wgrad_call(tag + "_dwu", u, dup)
        dh_in, dhb_in, g[tag + "_norm"] = _norm_in_bwd_call(
            tag + "_din", [(dgate, w[tag + "_w_gate"]), (dup, w[tag + "_w_up"])], h, w[tag + "_norm"], dh, te)
        return dh_in, dhb_in

    h1, s1 = ffn_fwd("ffn1", h0)
    un = _rmsnorm_call("mix_norm", h1, w["mix_norm"], te)
    z = _mm_call("mix_in", un, w["w_in"], tm, F32)
    mla_w = (w["q_latent_norm"], w["kv_latent_norm"], w["q_head_norm"], w["k_head_norm"], w["w_uq"], w["w_uk"],
             w["w_uv"])
    q, k, v, cqn, ckvn = _mla_prep_call(z, *mla_w, tabs, lp, te)
    o, lse = _attn_fwd_call(q, k, v, nb, lp)
    lru_w = (w["conv_w"], w["conv_b"], w["gate_a_w"], w["gate_a_b"], w["gate_x_w"], w["gate_x_b"], w["lru_lambda"])
    yl, hs, hp = _lru_fwd_call(z, *lru_w, nb, lp)
    y, h2 = _mix_out_call(o, yl, w["attn_out_norm"], w["lru_out_norm"], w["w_out"], h1, te)
    h3, s2 = ffn_fwd("ffn2", h2)
    dh3, dh3b, g["final_norm"], loss = _final_call(h3, w["final_norm"], target, lp, te)

    dh2, dh2b = ffn_bwd("ffn2", h2, s2, dh3, dh3b)
    g["w_out"] = _wgrad_call("dw_out", y, dh2b)
    dya, dyl, g["attn_out_norm"], g["lru_out_norm"] = _mix_out_bwd_call(
        dh2b, w["w_out"], o, yl, w["attn_out_norm"], w["lru_out_norm"], te)
    dq, dk, dv = _attn_bwd_call(q, k, v, o, lse, dya, nb, lp)
    (dz_mla, dqp, dkn, dvv, g["q_latent_norm"], g["kv_latent_norm"], g["q_head_norm"],
     g["k_head_norm"]) = _mla_prep_bwd_call(z, dq, dk, dv, *mla_w, tabs, lp, te)
    g["w_uq"] = _wgrad_call("dw_uq", cqn, dqp)
    g["w_uk"] = _wgrad_call("dw_uk", ckvn, dkn)
    g["w_uv"] = _wgrad_call("dw_uv", ckvn, dvv)
    (du, dgt, g["conv_w"], g["conv_b"], g["gate_a_w"], g["gate_a_b"], g["gate_x_w"], g["gate_x_b"],
     g["lru_lambda"]) = _lru_bwd_call(z, hs, hp, dyl, *lru_w, nb, lp)
    win = w["w_in"]
    g["w_in"] = jnp.concatenate(
        [_wgrad_call("dw_in_mla", un, dz_mla), _wgrad_call("dw_in_u", un, du), _wgrad_call("dw_in_g", un, dgt)],
        axis=1)
    dh1, dh1b, g["mix_norm"] = _norm_in_bwd_call(
        "mix_din", [(dz_mla, win[:, 0:Z_MLA]), (du, win[:, Z_U:Z_G]), (dgt, win[:, Z_G:Z_W])], h1, w["mix_norm"],
        dh2, te)
    dh0, _ = ffn_bwd("ffn1", h0, s1, dh1, dh1b)
    return loss, dh0, g


def _place():
    x, y, c = lax.axis_index("x"), lax.axis_index("y"), lax.axis_index("c")
    return x, y, c, [(1 - x, y), (x, 1 - y), (1 - x, 1 - y)]


def _any_specs(n):
    return [pl.BlockSpec(memory_space=pl.ANY)] * n


def _remote(src, dst, sems, k, dev):
    send_sems, recv_sems = sems
    return pltpu.make_async_remote_copy(src_ref=src, dst_ref=dst, send_sem=send_sems.at[k], recv_sem=recv_sems.at[k],
                                        device_id=dev, device_id_type=MESH)


def _gather_call(shards):
    n = len(shards)

    def body(*refs):
        ins, outs = refs[:n], refs[n:2 * n]
        sems, loc_sems = refs[2 * n:2 * n + 2], refs[2 * n + 2]
        x, y, c, chips = _place()
        s = 2 * x + y
        sib = (x, y, 1 - c)
        started, local = [], []
        for a in range(n):
            half = ins[a].shape[0] // 2
            mine = pl.ds(c * half, half)
            cp = pltpu.make_async_copy(ins[a], outs[a].at[s], loc_sems.at[a])
            cp.start()
            local.append(cp)
            for j, (cx, cy) in enumerate(chips):
                cp = _remote(ins[a].at[mine], outs[a].at[s, mine], sems, 6 * a + j, (cx, cy, c))
                cp.start()
                started.append(cp)
        for a in range(n):
            half = ins[a].shape[0] // 2
            mine = pl.ds(c * half, half)
            for j, (cx, cy) in enumerate(chips):
                land = outs[a].at[2 * cx + cy, mine]
                _remote(land, land, sems, 6 * a + j, sib).wait_recv()
                cp = _remote(land, land, sems, 6 * a + 3 + j, sib)
                cp.start()
                started.append(cp)
        for a in range(n):
            half = ins[a].shape[0] // 2
            other = pl.ds((1 - c) * half, half)
            for j, (cx, cy) in enumerate(chips):
                land = outs[a].at[2 * cx + cy, other]
                _remote(land, land, sems, 6 * a + 3 + j, sib).wait_recv()
        for cp in started:
            cp.wait_send()
        for cp in local:
            cp.wait()

    return pl.pallas_call(
        body, name="gather_weights",
        out_shape=[jax.ShapeDtypeStruct((N_SHARD,) + a.shape, a.dtype) for a in shards],
        in_specs=_any_specs(n), out_specs=_any_specs(n),
        scratch_shapes=[pltpu.SemaphoreType.DMA((6 * n,)), pltpu.SemaphoreType.DMA((6 * n,)),
                        pltpu.SemaphoreType.DMA((n,))])(*shards)


def _reduce_pair_call(grads):
    n = len(grads)

    def body(*refs):
        ins, outs, sems = refs[:n], refs[n:2 * n], refs[2 * n:]
        x, y, c, _ = _place()
        sib = (x, y, 1 - c)
        cps = []
        for a in range(n):
            half = ins[a].shape[1] // 2
            cp = _remote(ins[a].at[:, pl.ds((1 - c) * half, half)], outs[a], sems, a, sib)
            cp.start()
            cps.append(cp)
        for cp in cps:
            cp.wait_recv()
        for cp in cps:
            cp.wait_send()

    return pl.pallas_call(
        body, name="reduce_pair",
        out_shape=[jax.ShapeDtypeStruct((a.shape[0], a.shape[1] // 2) + a.shape[2:], a.dtype) for a in grads],
        in_specs=_any_specs(n), out_specs=_any_specs(n),
        scratch_shapes=[pltpu.SemaphoreType.DMA((n,)), pltpu.SemaphoreType.DMA((n,))])(*grads)


def _reduce_chips_call(parts):
    n = len(parts)

    def body(*refs):
        ins, outs, sems = refs[:n], refs[n:2 * n], refs[2 * n:]
        x, y, c, chips = _place()
        cps = []
        for a in range(n):
            for j, (cx, cy) in enumerate(chips):
                cp = _remote(ins[a].at[2 * cx + cy], outs[a].at[j], sems, 3 * a + j, (cx, cy, c))
                cp.start()
                cps.append(cp)
        for cp in cps:
            cp.wait_recv()
        for cp in cps:
            cp.wait_send()

    return pl.pallas_call(
        body, name="reduce_chips",
        out_shape=[jax.ShapeDtypeStruct((3,) + a.shape[1:], a.dtype) for a in parts],
        in_specs=_any_specs(n), out_specs=_any_specs(n),
        scratch_shapes=[pltpu.SemaphoreType.DMA((3 * n,)), pltpu.SemaphoreType.DMA((3 * n,))])(*parts)


def _share_pair_call(halves):
    n = len(halves)

    def body(*refs):
        ins, outs = refs[:n], refs[n:2 * n]
        sems, loc_sems = refs[2 * n:2 * n + 2], refs[2 * n + 2]
        x, y, c, _ = _place()
        sib = (x, y, 1 - c)
        cps, local = [], []
        for a in range(n):
            half = ins[a].shape[0]
            dst = outs[a].at[pl.ds(c * half, half)]
            lc = pltpu.make_async_copy(ins[a], dst, loc_sems.at[a])
            lc.start()
            local.append(lc)
            cp = _remote(ins[a], dst, sems, a, sib)
            cp.start()
            cps.append(cp)
        for a in range(n):
            half = ins[a].shape[0]
            land = outs[a].at[pl.ds((1 - c) * half, half)]
            _remote(land, land, sems, a, sib).wait_recv()
        for cp in cps:
            cp.wait_send()
        for lc in local:
            lc.wait()

    return pl.pallas_call(
        body, name="share_pair",
        out_shape=[jax.ShapeDtypeStruct((2 * a.shape[0],) + a.shape[1:], a.dtype) for a in halves],
        in_specs=_any_specs(n), out_specs=_any_specs(n),
        scratch_shapes=[pltpu.SemaphoreType.DMA((n,)), pltpu.SemaphoreType.DMA((n,)),
                        pltpu.SemaphoreType.DMA((n,))])(*halves)


def _gather_small_call(pack):
    r, d = pack.shape

    def body(in_ref, out_ref, send_sems, recv_sems):
        x, y, c, _ = _place()
        me = 4 * x + 2 * y + c
        out_ref[me] = in_ref[...]
        cps = []
        for k in range(1, 8):
            bx, by, bc = (k >> 2) & 1, (k >> 1) & 1, k & 1
            px, py, pc = x ^ bx, y ^ by, c ^ bc
            cp = pltpu.make_async_remote_copy(
                src_ref=in_ref, dst_ref=out_ref.at[me], send_sem=send_sems.at[k - 1], recv_sem=recv_sems.at[k - 1],
                device_id=(px, py, pc), device_id_type=MESH)
            cp.start()
            cps.append((cp, 4 * px + 2 * py + pc))
        for k, (cp, peer) in enumerate(cps):
            land = out_ref.at[peer]
            pltpu.make_async_remote_copy(
                src_ref=land, dst_ref=land, send_sem=send_sems.at[k], recv_sem=recv_sems.at[k],
                device_id=(x, y, c), device_id_type=MESH).wait_recv()
        for cp, _ in cps:
            cp.wait_send()

    return pl.pallas_call(
        body, name="gather_small",
        out_shape=jax.ShapeDtypeStruct((8, r, d), pack.dtype),
        in_specs=[pl.BlockSpec(memory_space=pltpu.VMEM)], out_specs=pl.BlockSpec(memory_space=pltpu.VMEM),
        scratch_shapes=[pltpu.SemaphoreType.DMA((7,)), pltpu.SemaphoreType.DMA((7,))])(pack)


def _ew_call(name, fn, ins, out_dtypes):
    shape = ins[0].shape
    cols = shape[-1]
    rows = 1
    for s_ in shape[:-1]:
        rows *= s_
    ins2 = [a.reshape(rows, cols) for a in ins]
    tr = rows
    for t in range(16, min(rows, max(16, (1 << 19) // cols)) + 1, 16):
        if rows % t == 0:
            tr = t
    no = len(out_dtypes)

    def body(*refs):
        outs = fn(*[r[...] for r in refs[:len(ins2)]])
        for ref, val in zip(refs[len(ins2):], outs):
            ref[...] = val.astype(ref.dtype)

    spec = pl.BlockSpec((tr, cols), lambda i: (i, 0))
    res = pl.pallas_call(
        body, name=name, grid=(rows // tr,), in_specs=[spec] * len(ins2), out_specs=[spec] * no,
        out_shape=[jax.ShapeDtypeStruct((rows, cols), dt) for dt in out_dtypes],
        compiler_params=_params(("parallel",)))(*ins2)
    return [r.reshape(shape) for r in res]


def _adamw_math(w, g, m, v):
    m = ADAM_B1 * m + (1.0 - ADAM_B1) * g
    v = ADAM_B2 * v + (1.0 - ADAM_B2) * (g * g)
    m_hat = m / (1.0 - ADAM_B1 ** ADAM_STEP)
    v_hat = v / (1.0 - ADAM_B2 ** ADAM_STEP)
    delta = -ADAM_LR * (m_hat / (jnp.sqrt(v_hat) + ADAM_EPS) + ADAM_WD * w)
    return delta, m, v


def _adamw_call(name, w, g, m, v):
    return _ew_call(name, _adamw_math, [w, g, m, v], [F32, F32, F32])


def _small_update_call(gathered, wp, mp, vp):
    nd, r, d = gathered.shape

    def body(g_ref, w_ref, m_ref, v_ref, gs_ref, d_ref, nm_ref, nv_ref):
        gs = g_ref[0]
        for k in range(1, nd):
            gs = gs + g_ref[k]
        gs_ref[...] = gs
        delta, m, v = _adamw_math(w_ref[...], gs[0:SMALL_ADAM_ROWS], m_ref[...], v_ref[...])
        d_ref[...] = delta
        nm_ref[...] = m
        nv_ref[...] = v

    vm = pl.BlockSpec(memory_space=pltpu.VMEM)
    ashape = jax.ShapeDtypeStruct((SMALL_ADAM_ROWS, d), F32)
    return pl.pallas_call(
        body, name="small_update", in_specs=[vm] * 4, out_specs=[vm] * 4,
        out_shape=[jax.ShapeDtypeStruct((r, d), F32), ashape, ashape, ashape],
        compiler_params=pltpu.CompilerParams(vmem_limit_bytes=VMEM_LIMIT_BYTES))(gathered, wp, mp, vp)


SMALL_NAMES = ["ffn1_norm", "mix_norm", "ffn2_norm", "final_norm", "q_latent_norm", "kv_latent_norm",
               "q_head_norm", "k_head_norm", "conv_b", "gate_a_b", "gate_x_b", "lru_lambda", "attn_out_norm",
               "lru_out_norm"]
ROW_CONV_W = 14
ROW_GATE_A = 16
ROW_GATE_X = 48
ROW_META = 80
ROW_LOSS = 96


def _row(a):
    flat = a.reshape(1, -1)
    return jnp.pad(flat, ((0, 0), (0, D_MODEL - flat.shape[1])))


def _pack_small(t, rows):
    parts = [_row(t[nm]) for nm in SMALL_NAMES]
    parts.append(t["conv_w"].reshape(2, D_MODEL))
    parts.append(t["gate_a_w"].reshape(32, D_MODEL))
    parts.append(t["gate_x_w"].reshape(32, D_MODEL))
    p = jnp.concatenate(parts, axis=0)
    return jnp.pad(p, ((0, rows - p.shape[0]), (0, 0)))


def _unpack_small(p, like):
    out = {}
    for k, nm in enumerate(SMALL_NAMES):
        out[nm] = p[k, 0:like[nm].size].reshape(like[nm].shape)
    out["gate_a_w"] = p[ROW_GATE_A:ROW_GATE_A + 32].reshape(like["gate_a_w"].shape)
    out["gate_x_w"] = p[ROW_GATE_X:ROW_GATE_X + 32].reshape(like["gate_x_w"].shape)
    return out


def _gate_dense(wg):
    w4 = wg[0].reshape(N_LRU_TILES, 2, 64, 64)
    zero = jnp.zeros((N_LRU_TILES, 64, 64), wg.dtype)
    top = jnp.concatenate([w4[:, 0], zero], axis=2)
    bot = jnp.concatenate([zero, w4[:, 1]], axis=2)
    return jnp.concatenate([top, bot], axis=1).astype(BF16)


def _gate_blocks(dw):
    return jnp.stack([dw[:, 0:64, 0:64], dw[:, 64:128, 64:128]], axis=1).reshape(8, 64, 64)


BIG_NAMES = ["ffn1_w_gate", "ffn1_w_up", "ffn1_w_down", "w_in", "w_uq", "w_uk", "w_uv", "w_out", "ffn2_w_gate",
             "ffn2_w_up", "ffn2_w_down"]
WEIGHT_NAMES = ["meta_tokens", "ffn1_norm", "ffn1_w_gate", "ffn1_w_up", "ffn1_w_down", "mix_norm", "w_in",
                "q_latent_norm", "w_uq", "kv_latent_norm", "w_uk", "w_uv", "q_head_norm", "k_head_norm", "conv_w",
                "conv_b", "gate_a_w", "gate_a_b", "gate_x_w", "gate_x_b", "lru_lambda", "attn_out_norm",
                "lru_out_norm", "w_out", "ffn2_norm", "ffn2_w_gate", "ffn2_w_up", "ffn2_w_down", "final_norm"]


def _full_weights(p, gathered, small):
    w = {nm: p[nm] for nm in SMALL_NAMES}
    for nm in ("ffn1_w_gate", "ffn1_w_up", "ffn1_w_down", "ffn2_w_gate", "ffn2_w_up", "ffn2_w_down"):
        w[nm] = gathered[nm]
    win = gathered["w_in"].transpose(1, 0, 2).reshape(D_MODEL, IN_WIDTH)
    w["w_in"] = jnp.concatenate([win[:, 0:Z_KR + D_ROPE], jnp.zeros((D_MODEL, 128 - D_ROPE), BF16),
                                 win[:, Z_KR + D_ROPE:]], axis=1)
    wuq = jnp.pad(gathered["w_uq"], ((0, 0), (0, 0), (0, D_QKP - D_QK)))
    w["w_uq"] = wuq.transpose(1, 0, 2).reshape(Q_RANK, HEADS * D_QKP)
    w["w_uk"] = gathered["w_uk"].transpose(1, 0, 2).reshape(KV_RANK, HEADS * D_NOPE)
    w["w_uv"] = gathered["w_uv"].transpose(1, 0, 2).reshape(KV_RANK, HEADS * D_V)
    w["w_out"] = gathered["w_out"].reshape(D_MODEL, D_MODEL)
    w["q_head_norm"] = jnp.pad(p["q_head_norm"], ((0, 0), (0, D_QKP - D_QK)))
    w["k_head_norm"] = jnp.pad(p["k_head_norm"], ((0, 0), (0, D_QKP - D_QK)))
    w["conv_w"] = small[:, N_META:N_META + 2, :].reshape(N_SHARD, CONV_K, LRU_TILE)
    w["gate_a_w"] = _gate_dense(p["gate_a_w"])
    w["gate_x_w"] = _gate_dense(p["gate_x_w"])
    meta = small[:, 0:N_META, :].transpose(1, 0, 2).reshape(N_META, D_MODEL)
    return w, meta


def _shard_grads(g):
    out = {nm: g[nm] for nm in ("ffn1_w_gate", "ffn1_w_up", "ffn1_w_down", "ffn2_w_gate", "ffn2_w_up",
                                "ffn2_w_down")}
    win = jnp.concatenate([g["w_in"][:, 0:Z_KR + D_ROPE], g["w_in"][:, Z_MLA:]], axis=1)
    out["w_in"] = win.reshape(D_MODEL, N_SHARD, IN_WIDTH // N_SHARD).transpose(1, 0, 2)
    out["w_uq"] = g["w_uq"].reshape(Q_RANK, HEADS, D_QKP)[:, :, 0:D_QK].transpose(1, 0, 2)
    out["w_uk"] = g["w_uk"].reshape(KV_RANK, HEADS, D_NOPE).transpose(1, 0, 2)
    out["w_uv"] = g["w_uv"].reshape(KV_RANK, HEADS, D_V).transpose(1, 0, 2)
    out["w_out"] = g["w_out"].reshape(N_SHARD, D_MODEL // N_SHARD, D_MODEL)
    return out


def kernel(x, meta_tokens, ffn1_norm, ffn1_w_gate, ffn1_w_up, ffn1_w_down, mix_norm, w_in, q_latent_norm, w_uq, kv_latent_norm, w_uk, w_uv, q_head_norm, k_head_norm, conv_w, conv_b, gate_a_w, gate_a_b, gate_x_w, gate_x_b, lru_lambda, attn_out_norm, lru_out_norm, w_out, ffn2_norm, ffn2_w_gate, ffn2_w_up, ffn2_w_down, final_norm, loss_target, m_meta_tokens, m_ffn1_norm, m_ffn1_w_gate, m_ffn1_w_up, m_ffn1_w_down, m_mix_norm, m_w_in, m_q_latent_norm, m_w_uq, m_kv_latent_norm, m_w_uk, m_w_uv, m_q_head_norm, m_k_head_norm, m_conv_w, m_conv_b, m_gate_a_w, m_gate_a_b, m_gate_x_w, m_gate_x_b, m_lru_lambda, m_attn_out_norm, m_lru_out_norm, m_w_out, m_ffn2_norm, m_ffn2_w_gate, m_ffn2_w_up, m_ffn2_w_down, m_final_norm, v_meta_tokens, v_ffn1_norm, v_ffn1_w_gate, v_ffn1_w_up, v_ffn1_w_down, v_mix_norm, v_w_in, v_q_latent_norm, v_w_uq, v_kv_latent_norm, v_w_uk, v_w_uv, v_q_head_norm, v_k_head_norm, v_conv_w, v_conv_b, v_gate_a_w, v_gate_a_b, v_gate_x_w, v_gate_x_b, v_lru_lambda, v_attn_out_norm, v_lru_out_norm, v_w_out, v_ffn2_norm, v_ffn2_w_gate, v_ffn2_w_up, v_ffn2_w_down, v_final_norm):
    args = locals()
    p = {nm: args[nm] for nm in WEIGHT_NAMES}
    mom = {nm: args["m_" + nm] for nm in WEIGHT_NAMES}
    var = {nm: args["v_" + nm] for nm in WEIGHT_NAMES}
    nb, seq, d = x.shape
    lp = CHUNK + seq
    xi, yi, ci = lax.axis_index("x"), lax.axis_index("y"), lax.axis_index("c")
    chip = 2 * xi + yi

    shards = [p[nm][0].astype(BF16) for nm in BIG_NAMES]
    small_shard = jnp.concatenate(
        [meta_tokens, conv_w[0].reshape(2, 2 * LRU_TILE), jnp.zeros((14, 2 * LRU_TILE), F32)], axis=0)
    gathered = _gather_call(shards + [small_shard])
    w, meta = _full_weights(p, dict(zip(BIG_NAMES, gathered[:-1])), gathered[-1])

    h0 = jnp.concatenate(
        [jnp.zeros((nb, PAD_ROWS, d), F32), jnp.broadcast_to(meta[None], (nb, N_META, d)), x], axis=1)
    target = jnp.pad(loss_target, ((0, 0), (CHUNK, 0), (0, 0)))
    loss_part, dh0, g = _local_step(h0.reshape(nb * lp, d), target.reshape(nb * lp, d), w, nb, lp)
    dh0 = dh0.reshape(nb, lp, d)
    grad_x = dh0[:, CHUNK:, :]

    gs = {nm: g[nm] for nm in SMALL_NAMES}
    gs["q_head_norm"] = g["q_head_norm"][:, 0:D_QK]
    gs["k_head_norm"] = g["k_head_norm"][:, 0:D_QK]
    for nm in ("conv_b", "gate_a_b", "gate_x_b", "lru_lambda"):
        gs[nm] = g[nm].reshape(1, LRU_W)
    gs["conv_w"] = g["conv_w"].transpose(1, 0, 2).reshape(CONV_K, LRU_W)
    gs["gate_a_w"] = _gate_blocks(g["gate_a_w"])
    gs["gate_x_w"] = _gate_blocks(g["gate_x_w"])
    pack = _pack_small(gs, ROW_META)
    pack = jnp.concatenate([pack, jnp.sum(dh0[:, PAD_ROWS:CHUNK, :], axis=0), _row(loss_part[:, 0:1]),
                            jnp.zeros((SMALL_ROWS - ROW_LOSS - 1, D_MODEL), F32)], axis=0)
    small_like = {nm: p[nm] for nm in SMALL_NAMES + ["gate_a_w", "gate_x_w"]}

    def pack_w(t):
        tt = {nm: t[nm] for nm in SMALL_NAMES + ["gate_a_w", "gate_x_w"]}
        tt["conv_w"] = jnp.zeros((CONV_K, LRU_W), F32)
        return _pack_small(tt, SMALL_ADAM_ROWS)

    gsum, dsm, msm, vsm = _small_update_call(_gather_small_call(pack), pack_w(p), pack_w(mom), pack_w(var))
    grads = _unpack_small(gsum, small_like)
    delta = _unpack_small(dsm, small_like)
    new_m = _unpack_small(msm, small_like)
    new_v = _unpack_small(vsm, small_like)
    loss = gsum[ROW_LOSS, 0]
    gmeta = gsum[ROW_META:ROW_META + N_META].reshape(N_META, N_SHARD, D_MODEL // N_SHARD)
    grads["meta_tokens"] = lax.dynamic_index_in_dim(gmeta, chip, axis=1, keepdims=False)
    gconv = gsum[ROW_CONV_W:ROW_CONV_W + 2].reshape(CONV_K, N_SHARD, LRU_TILE)
    grads["conv_w"] = lax.dynamic_index_in_dim(gconv, chip, axis=1, keepdims=False)[None]
    for nm in ("meta_tokens", "conv_w"):
        delta[nm], new_m[nm], new_v[nm] = _adamw_call("adamw_" + nm, p[nm], grads[nm], mom[nm], var[nm])

    gb = _shard_grads(g)
    glist = [gb[nm] for nm in BIG_NAMES]
    from_pair = _reduce_pair_call(glist)
    chip_f32, chip_bf16 = [], []
    for nm, full, got in zip(BIG_NAMES, glist, from_pair):
        half = full.shape[1] // 2
        mine = lax.dynamic_slice_in_dim(full, ci * half, half, axis=1)
        t32, t16 = _ew_call("pair_sum_" + nm, lambda a, b: (a + b, a + b), [mine, got], [F32, BF16])
        chip_f32.append(t32)
        chip_bf16.append(t16)
    from_chips = _reduce_chips_call(chip_bf16)
    halves = []
    for nm, t32, got in zip(BIG_NAMES, chip_f32, from_chips):
        own = lax.dynamic_index_in_dim(t32, chip, axis=0, keepdims=False)
        halves.append(_ew_call(
            "chip_sum_" + nm,
            lambda a, b0, b1, b2: (a + b0.astype(F32) + b1.astype(F32) + b2.astype(F32),),
            [own, got[0], got[1], got[2]], [F32])[0])
    shard_grads = _share_pair_call(halves)
    for nm, gsh in zip(BIG_NAMES, shard_grads):
        grads[nm] = gsh[None]
        delta[nm], new_m[nm], new_v[nm] = _adamw_call("adamw_" + nm, p[nm], grads[nm], mom[nm], var[nm])

    return (loss, grad_x, *[grads[nm] for nm in WEIGHT_NAMES], *[delta[nm] for nm in WEIGHT_NAMES],
            *[new_m[nm] for nm in WEIGHT_NAMES], *[new_v[nm] for nm in WEIGHT_NAMES])
```

```python
import functools
import math

import jax
import jax.numpy as jnp
from jax import lax
from jax.experimental import pallas as pl
from jax.experimental.pallas import tpu as pltpu

F32 = jnp.float32
BF16 = jnp.bfloat16
MESH = pl.DeviceIdType.MESH

D_MODEL = 1024
N_META = 16
CHUNK = 64
PAD_ROWS = CHUNK - N_META
HEADS = 4
D_NOPE = 128
D_ROPE = 64
D_QK = D_NOPE + D_ROPE
D_QKP = 256
D_V = 128
KV_RANK = 256
Q_RANK = 384
MLA_W = HEADS * D_V
LRU_W = 512
LRU_TILE = 128
N_LRU_TILES = LRU_W // LRU_TILE
CONV_K = 4
C_RGLRU = 8.0
ROPE_THETA = 10000.0
D_FF = 2816
N_SHARD = 4
EPS = 1e-6
NEG_INF = -1e30
Z_KR = Q_RANK + KV_RANK
Z_MLA = Z_KR + 128
Z_U = Z_MLA
Z_G = Z_U + LRU_W
Z_W = Z_G + LRU_W
IN_WIDTH = Q_RANK + KV_RANK + D_ROPE + 2 * LRU_W

ADAM_LR = 0.001
ADAM_B1 = 0.9
ADAM_B2 = 0.999
ADAM_EPS = 1e-08
ADAM_WD = 0.01
ADAM_STEP = 10

VMEM_LIMIT_BYTES = 56 * 1024 * 1024
SMALL_ROWS = 104
SMALL_ADAM_ROWS = 80


def _params(sem):
    return pltpu.CompilerParams(dimension_semantics=sem, vmem_limit_bytes=VMEM_LIMIT_BYTES)


def _row_tile(rows, target):
    best = 16
    for t in range(16, min(rows, target) + 1, 16):
        if rows % t == 0:
            best = t
    return best


def _col_tile(cols, target):
    best = cols
    for t in range(128, min(cols, target) + 1, 128):
        if cols % t == 0:
            best = t
    return best


def _dot(a, b):
    return jnp.dot(a, b, preferred_element_type=F32)


def _dot_nt(a, b):
    return lax.dot_general(a, b, (((1,), (1,)), ((), ())), preferred_element_type=F32)


def _dot_tn(a, b):
    return lax.dot_general(a, b, (((0,), (0,)), ((), ())), preferred_element_type=F32)


def _rms(x, n):
    return lax.rsqrt(jnp.sum(x * x, axis=-1, keepdims=True) * (1.0 / n) + EPS)


def _rms_bwd(dn, nrm, r, n):
    return r * (dn - nrm * (jnp.sum(dn * nrm, axis=-1, keepdims=True) * (1.0 / n)))


def _gelu(x):
    k = math.sqrt(2.0 / math.pi)
    t = jnp.tanh(k * (x + 0.044715 * x * x * x))
    return 0.5 * x * (1.0 + t), t


def _gelu_grad(x, t):
    k = math.sqrt(2.0 / math.pi)
    return 0.5 * (1.0 + t) + 0.5 * x * (1.0 - t * t) * k * (1.0 + 3.0 * 0.044715 * x * x)


def _neg_expm1(x):
    series = -x * (1.0 + x * (0.5 + x * (1.0 / 6 + x * (1.0 / 24 + x * (1.0 / 120 + x * (1.0 / 720))))))
    return jnp.where(x > -0.25, series, 1.0 - jnp.exp(x))


def _softplus_neg(lam):
    e = jnp.exp(-jnp.abs(lam))
    log1p = jnp.where(e < 0.01, e * (1.0 - e * (0.5 - e * (1.0 / 3 - e * 0.25))), jnp.log(1.0 + e))
    return jnp.maximum(-lam, 0.0) + log1p


def _rope(t, c, s1, s2):
    return t * c + pltpu.roll(t, 96, 1) * s1 + pltpu.roll(t, 32, 1) * s2


def _rope_t(d, c, s1, s2):
    return d * c + pltpu.roll(d * s1, 32, 1) + pltpu.roll(d * s2, 96, 1)


def _rope_tables(lp):
    pos = (jnp.arange(lp, dtype=jnp.int32) - PAD_ROWS).astype(F32)
    inv_freq = ROPE_THETA ** (-jnp.arange(0, D_ROPE // 2, dtype=F32) / (D_ROPE // 2))
    ang = pos[:, None] * inv_freq[None, :]
    cos, sin = jnp.cos(ang), jnp.sin(ang)
    z = jnp.zeros_like(cos)
    return (jnp.concatenate([cos, cos, z, z], 1), jnp.concatenate([-sin, z, z, z], 1),
            jnp.concatenate([z, sin, z, z], 1))


def _rmsnorm_call(name, h, g, tm):
    rows, d = h.shape

    def body(h_ref, g_ref, o_ref):
        x = h_ref[...]
        o_ref[...] = (x * _rms(x, d) * g_ref[...]).astype(BF16)

    return pl.pallas_call(
        body, name=name, grid=(rows // tm,),
        in_specs=[pl.BlockSpec((tm, d), lambda i: (i, 0)), pl.BlockSpec((1, d), lambda i: (0, 0))],
        out_specs=pl.BlockSpec((tm, d), lambda i: (i, 0)),
        out_shape=jax.ShapeDtypeStruct((rows, d), BF16),
        compiler_params=_params(("parallel",)))(h, g)


def _ffn_up_call(name, u, wg, wu, tm):
    rows, d = u.shape
    ns, fs, _ = wg.shape

    def body(u_ref, wg_ref, wu_ref, g_ref, p_ref, a_ref):
        uu = u_ref[...]
        g = _dot_nt(uu, wg_ref[0])
        p = _dot_nt(uu, wu_ref[0])
        g_ref[0] = g.astype(BF16)
        p_ref[0] = p.astype(BF16)
        a_ref[0] = (g * jax.nn.sigmoid(g) * p).astype(BF16)

    wspec = pl.BlockSpec((1, fs, d), lambda s, i: (s, 0, 0))
    ospec = pl.BlockSpec((1, tm, fs), lambda s, i: (s, i, 0))
    oshape = jax.ShapeDtypeStruct((ns, rows, fs), BF16)
    return pl.pallas_call(
        body, name=name, grid=(ns, rows // tm),
        in_specs=[pl.BlockSpec((tm, d), lambda s, i: (i, 0)), wspec, wspec],
        out_specs=[ospec, ospec, ospec], out_shape=[oshape, oshape, oshape],
        compiler_params=_params(("parallel", "parallel")))(u, wg, wu)


def _ffn_down_call(name, a, wd, h, tm):
    rows, d = h.shape
    ns, _, fs = a.shape

    def body(a_ref, wd_ref, h_ref, o_ref):
        acc = h_ref[...]
        for s in range(ns):
            acc = acc + 0.5 * _dot(a_ref[s], wd_ref[s])
        o_ref[...] = acc

    return pl.pallas_call(
        body, name=name, grid=(rows // tm,),
        in_specs=[pl.BlockSpec((ns, tm, fs), lambda i: (0, i, 0)),
                  pl.BlockSpec((ns, fs, d), lambda i: (0, 0, 0)),
                  pl.BlockSpec((tm, d), lambda i: (i, 0))],
        out_specs=pl.BlockSpec((tm, d), lambda i: (i, 0)),
        out_shape=jax.ShapeDtypeStruct((rows, d), F32),
        compiler_params=_params(("parallel",)))(a, wd, h)


def _mm_call(name, a, bt, tm, out_dtype):
    rows, k = a.shape
    n = bt.shape[0]

    def body(a_ref, b_ref, o_ref):
        o_ref[...] = _dot_nt(a_ref[...], b_ref[...]).astype(out_dtype)

    return pl.pallas_call(
        body, name=name, grid=(rows // tm,),
        in_specs=[pl.BlockSpec((tm, k), lambda i: (i, 0)), pl.BlockSpec((n, k), lambda i: (0, 0))],
        out_specs=pl.BlockSpec((tm, n), lambda i: (i, 0)),
        out_shape=jax.ShapeDtypeStruct((rows, n), out_dtype),
        compiler_params=_params(("parallel",)))(a, bt)


def _mla_heads(z, gql, gkvl, wuq, wuk, wuv):
    cq = z[:, 0:Q_RANK]
    ckv = z[:, Q_RANK:Z_KR]
    kr = z[:, Z_KR:Z_MLA]
    rq = _rms(cq, Q_RANK)
    nq = cq * rq
    cqn = (nq * gql).astype(BF16)
    rkv = _rms(ckv, KV_RANK)
    nkv = ckv * rkv
    ckvn = (nkv * gkvl).astype(BF16)
    qraw = _dot_nt(cqn, wuq)
    knope = _dot(ckvn, wuk)
    v = _dot(ckvn, wuv)
    skr = jnp.sum(kr * kr, axis=-1, keepdims=True)
    heads = []
    for hd in range(HEADS):
        qh = qraw[:, hd * D_QKP:(hd + 1) * D_QKP]
        rqh = lax.rsqrt(jnp.sum(qh * qh, axis=-1, keepdims=True) * (1.0 / D_QK) + EPS)
        kn = knope[:, hd * D_NOPE:(hd + 1) * D_NOPE]
        rkh = lax.rsqrt((jnp.sum(kn * kn, axis=-1, keepdims=True) + skr) * (1.0 / D_QK) + EPS)
        heads.append((qh * rqh, rqh, kn * rkh, kr * rkh, rkh))
    return dict(rq=rq, nq=nq, cqn=cqn, rkv=rkv, nkv=nkv, ckvn=ckvn, v=v, heads=heads)


def _mla_prep_call(z, gql, gkvl, gqh, gkh, wuq, wuk, wuv, tabs, lp, tm):
    rows = z.shape[0]
    tpe = lp // tm

    def body(z_ref, gql_ref, gkvl_ref, gqh_ref, gkh_ref, wuq_ref, wuk_ref, wuv_ref, c_ref, s1_ref, s2_ref,
             q_ref, k_ref, v_ref, cqn_ref, ckvn_ref):
        m = _mla_heads(z_ref[...], gql_ref[...], gkvl_ref[...], wuq_ref[...], wuk_ref[...], wuv_ref[...])
        c, s1, s2 = c_ref[...], s1_ref[...], s2_ref[...]
        gq, gk = gqh_ref[...], gkh_ref[...]
        for hd in range(HEADS):
            qn, _, knn, krn, _ = m["heads"][hd]
            qg = qn * gq
            q_ref[hd, :, 0:D_NOPE] = qg[:, 0:D_NOPE].astype(BF16)
            q_ref[hd, :, D_NOPE:D_QKP] = _rope(qg[:, D_NOPE:D_QKP], c, s1, s2).astype(BF16)
            k_ref[hd, :, 0:D_NOPE] = (knn * gk[:, 0:D_NOPE]).astype(BF16)
            k_ref[hd, :, D_NOPE:D_QKP] = _rope(krn * gk[:, D_NOPE:D_QKP], c, s1, s2).astype(BF16)
            v_ref[hd] = m["v"][:, hd * D_V:(hd + 1) * D_V].astype(BF16)
        cqn_ref[...] = m["cqn"]
        ckvn_ref[...] = m["ckvn"]

    def const(shape):
        return pl.BlockSpec(shape, lambda i: tuple(0 for _ in shape))

    tab = pl.BlockSpec((tm, 128), lambda i: (i % tpe, 0))
    return pl.pallas_call(
        body, name="mla_prep", grid=(rows // tm,),
        in_specs=[pl.BlockSpec((tm, Z_MLA), lambda i: (i, 0)), const((1, Q_RANK)), const((1, KV_RANK)),
                  const((1, D_QKP)), const((1, D_QKP)), const((HEADS * D_QKP, Q_RANK)),
                  const((KV_RANK, HEADS * D_NOPE)), const((KV_RANK, HEADS * D_V)), tab, tab, tab],
        out_specs=[pl.BlockSpec((HEADS, tm, D_QKP), lambda i: (0, i, 0)),
                   pl.BlockSpec((HEADS, tm, D_QKP), lambda i: (0, i, 0)),
                   pl.BlockSpec((HEADS, tm, D_V), lambda i: (0, i, 0)),
                   pl.BlockSpec((tm, Q_RANK), lambda i: (i, 0)),
                   pl.BlockSpec((tm, KV_RANK), lambda i: (i, 0))],
        out_shape=[jax.ShapeDtypeStruct((HEADS, rows, D_QKP), BF16),
                   jax.ShapeDtypeStruct((HEADS, rows, D_QKP), BF16),
                   jax.ShapeDtypeStruct((HEADS, rows, D_V), BF16),
                   jax.ShapeDtypeStruct((rows, Q_RANK), BF16),
                   jax.ShapeDtypeStruct((rows, KV_RANK), BF16)],
        compiler_params=_params(("parallel",)))(z, gql, gkvl, gqh, gkh, wuq, wuk, wuv, *tabs)


def _q_block(lp):
    chunks = lp // CHUNK
    best = 1
    for g in range(1, 5):
        if chunks % g == 0:
            best = g
    return best * CHUNK


def _attn_mask(j0, qb, ext):
    qrow = j0 + lax.broadcasted_iota(jnp.int32, (qb, ext), 0)
    krow = lax.broadcasted_iota(jnp.int32, (qb, ext), 1)
    shift = CHUNK.bit_length() - 1
    return (jnp.right_shift(krow, shift) <= jnp.right_shift(qrow, shift)) & (krow >= PAD_ROWS)


def _attn_fwd_call(q, k, v, nb, lp):
    rows = nb * lp
    qb = _q_block(lp)
    scale = 1.0 / math.sqrt(D_QK)

    def body(q_ref, k_ref, v_ref, o_ref, lse_ref):
        for j in range(lp // qb):
            j0, ext = j * qb, (j + 1) * qb
            s = _dot_nt(q_ref[0, j0:ext, :], k_ref[0, 0:ext, :]) * scale
            s = jnp.where(_attn_mask(j0, qb, ext), s, NEG_INF)
            mx = jnp.max(s, axis=-1, keepdims=True)
            p = jnp.exp(s - mx)
            l = jnp.sum(p, axis=-1, keepdims=True)
            o = _dot(p.astype(BF16), v_ref[0, 0:ext, :])
            o_ref[j0:ext, :] = o / l
            lse_ref[0, j0:ext, :] = mx + jnp.log(l)

    return pl.pallas_call(
        body, name="attn_fwd", grid=(nb, HEADS),
        in_specs=[pl.BlockSpec((1, lp, D_QKP), lambda b, h: (h, b, 0)),
                  pl.BlockSpec((1, lp, D_QKP), lambda b, h: (h, b, 0)),
                  pl.BlockSpec((1, lp, D_V), lambda b, h: (h, b, 0))],
        out_specs=[pl.BlockSpec((lp, D_V), lambda b, h: (b, h)),
                   pl.BlockSpec((1, lp, 1), lambda b, h: (h, b, 0))],
        out_shape=[jax.ShapeDtypeStruct((rows, MLA_W), F32),
                   jax.ShapeDtypeStruct((HEADS, rows, 1), F32)],
        compiler_params=_params(("parallel", "parallel")))(q, k, v)


def _attn_bwd_call(q, k, v, o, lse, do, nb, lp):
    rows = nb * lp
    qb = _q_block(lp)
    scale = 1.0 / math.sqrt(D_QK)

    def body(q_ref, k_ref, v_ref, o_ref, lse_ref, do_ref, dq_ref, dk_ref, dv_ref, dk_acc, dv_acc):
        dk_acc[...] = jnp.zeros_like(dk_acc)
        dv_acc[...] = jnp.zeros_like(dv_acc)
        for j in range(lp // qb):
            j0, ext = j * qb, (j + 1) * qb
            qj = q_ref[0, j0:ext, :]
            doj = do_ref[j0:ext, :]
            delta = jnp.sum(doj * o_ref[j0:ext, :], axis=-1, keepdims=True)
            dob = doj.astype(BF16)
            kk = k_ref[0, 0:ext, :]
            s = _dot_nt(qj, kk) * scale
            p = jnp.where(_attn_mask(j0, qb, ext), jnp.exp(s - lse_ref[0, j0:ext, :]), 0.0)
            dv_acc[0:ext, :] += _dot_tn(p.astype(BF16), dob)
            dp = _dot_nt(dob, v_ref[0, 0:ext, :])
            ds = (p * (dp - delta) * scale).astype(BF16)
            dq_ref[0, j0:ext, :] = _dot(ds, kk).astype(BF16)
            dk_acc[0:ext, :] += _dot_tn(ds, qj)
        dk_ref[0] = dk_acc[...].astype(BF16)
        dv_ref[0] = dv_acc[...].astype(BF16)

    qspec = pl.BlockSpec((1, lp, D_QKP), lambda b, h: (h, b, 0))
    vspec = pl.BlockSpec((1, lp, D_V), lambda b, h: (h, b, 0))
    ospec = pl.BlockSpec((lp, D_V), lambda b, h: (b, h))
    return pl.pallas_call(
        body, name="attn_bwd", grid=(nb, HEADS),
        in_specs=[qspec, qspec, vspec, ospec, pl.BlockSpec((1, lp, 1), lambda b, h: (h, b, 0)), ospec],
        out_specs=[qspec, qspec, vspec],
        out_shape=[jax.ShapeDtypeStruct((HEADS, rows, D_QKP), BF16),
                   jax.ShapeDtypeStruct((HEADS, rows, D_QKP), BF16),
                   jax.ShapeDtypeStruct((HEADS, rows, D_V), BF16)],
        scratch_shapes=[pltpu.VMEM((lp, D_QKP), F32), pltpu.VMEM((lp, D_V), F32)],
        compiler_params=_params(("parallel", "parallel")))(q, k, v, o, lse, do)


def _lru_gates(u, cw, cb, wa, ba, wx, bx, lam, lp):
    xc = (cw[3:4, :] * u + cw[2:3, :] * pltpu.roll(u, 1, 0) + cw[1:2, :] * pltpu.roll(u, 2, 0)
          + cw[0:1, :] * pltpu.roll(u, 3, 0) + cb)
    xcb = xc.astype(BF16)
    r = jax.nn.sigmoid(_dot(xcb, wa) + ba)
    i = jax.nn.sigmoid(_dot(xcb, wx) + bx)
    sp = _softplus_neg(lam)
    la = -C_RGLRU * r * sp
    a = jnp.exp(la)
    mult = jnp.sqrt(_neg_expm1(2.0 * la))
    row = lax.broadcasted_iota(jnp.int32, (lp, LRU_TILE), 0)
    first = row == PAD_ROWS
    valid = row >= PAD_ROWS
    mult_eff = jnp.where(first, 1.0, mult)
    return dict(xc=xc, xcb=xcb, r=r, i=i, sp=sp, la=la, a=a, mult=mult, mult_eff=mult_eff, first=first, valid=valid)


def _select_rows(rows8):
    n = rows8[0].shape[1]
    sub = lax.broadcasted_iota(jnp.int32, (8, n), 0)
    out = jnp.broadcast_to(rows8[0], (8, n))
    for j in range(1, 8):
        out = jnp.where(sub == j, jnp.broadcast_to(rows8[j], (8, n)), out)
    return out


def _lru_specs(lp):
    seq = lambda col0: pl.BlockSpec((lp, LRU_TILE), lambda t, b: (b, col0 + t))
    cw = pl.BlockSpec((1, CONV_K, LRU_TILE), lambda t, b: (t, 0, 0))
    vec = pl.BlockSpec((1, LRU_TILE), lambda t, b: (0, t))
    mat = pl.BlockSpec((1, LRU_TILE, LRU_TILE), lambda t, b: (t, 0, 0))
    return seq, cw, vec, mat


def _lru_fwd_call(z, cw, cb, wa, ba, wx, bx, lam, nb, lp):
    rows = nb * lp
    seq, cwspec, vec, mat = _lru_specs(lp)

    def body(u_ref, g_ref, cw_ref, cb_ref, wa_ref, ba_ref, wx_ref, bx_ref, lam_ref, y_ref, hs_ref, hp_ref, a_s, b_s):
        m = _lru_gates(u_ref[...], cw_ref[0], cb_ref[...], wa_ref[0], ba_ref[...], wx_ref[0], bx_ref[...],
                       lam_ref[...], lp)
        a_s[...] = jnp.where(m["valid"], m["a"], 0.0)
        b_s[...] = jnp.where(m["valid"], m["mult_eff"] * (m["i"] * m["xc"]), 0.0)

        def group(gi, h):
            r0 = pl.multiple_of(gi * 8, 8)
            a8 = a_s[pl.ds(r0, 8), :]
            b8 = b_s[pl.ds(r0, 8), :]
            prev, cur = [], []
            for j in range(8):
                prev.append(h)
                h = a8[j:j + 1, :] * h + b8[j:j + 1, :]
                cur.append(h)
            hs_ref[pl.ds(r0, 8), :] = _select_rows(cur)
            hp_ref[pl.ds(r0, 8), :] = _select_rows(prev)
            return h

        lax.fori_loop(0, lp // 8, group, jnp.zeros((1, LRU_TILE), F32))
        gl, _ = _gelu(g_ref[...])
        y_ref[...] = hs_ref[...] * gl

    oshape = jax.ShapeDtypeStruct((rows, LRU_W), F32)
    return pl.pallas_call(
        body, name="lru_fwd", grid=(N_LRU_TILES, nb),
        in_specs=[seq(Z_U // LRU_TILE), seq(Z_G // LRU_TILE), cwspec, vec, mat, vec, mat, vec, vec],
        out_specs=[seq(0), seq(0), seq(0)], out_shape=[oshape, oshape, oshape],
        scratch_shapes=[pltpu.VMEM((lp, LRU_TILE), F32), pltpu.VMEM((lp, LRU_TILE), F32)],
        compiler_params=_params(("parallel", "parallel")))(z, z, cw, cb, wa, ba, wx, bx, lam)


def _lru_bwd_call(z, hs, hp, dy, cw, cb, wa, ba, wx, bx, lam, nb, lp):
    rows = nb * lp
    seq, cwspec, vec, mat = _lru_specs(lp)

    def body(u_ref, g_ref, hs_ref, hp_ref, dy_ref, cw_ref, cb_ref, wa_ref, ba_ref, wx_ref, bx_ref, lam_ref,
             du_ref, dg_ref, dcw_ref, dcb_ref, dwa_ref, dba_ref, dwx_ref, dbx_ref, dlam_ref, an_s, d_s):
        b_idx = pl.program_id(1)
        u = u_ref[...]
        cw = cw_ref[0]
        wa, wx = wa_ref[0], wx_ref[0]
        lam = lam_ref[...]
        m = _lru_gates(u, cw, cb_ref[...], wa, ba_ref[...], wx, bx_ref[...], lam, lp)
        gate = g_ref[...]
        gl, th = _gelu(gate)
        dy = dy_ref[...]
        dg_ref[...] = (dy * hs_ref[...] * _gelu_grad(gate, th)).astype(BF16)
        a_eff = jnp.where(m["valid"], m["a"], 0.0)
        an_s[...] = pltpu.roll(a_eff, lp - 1, 0)
        d_s[...] = dy * gl

        def group(gi, carry):
            r0 = pl.multiple_of((lp // 8 - 1 - gi) * 8, 8)
            a8 = an_s[pl.ds(r0, 8), :]
            d8 = d_s[pl.ds(r0, 8), :]
            cur = [None] * 8
            for j in range(7, -1, -1):
                carry = d8[j:j + 1, :] + a8[j:j + 1, :] * carry
                cur[j] = carry
            d_s[pl.ds(r0, 8), :] = _select_rows(cur)
            return carry

        lax.fori_loop(0, lp // 8, group, jnp.zeros((1, LRU_TILE), F32))
        ds = d_s[...]
        xc, r, i = m["xc"], m["r"], m["i"]
        da = ds * hp_ref[...]
        db = jnp.where(m["valid"], ds, 0.0)
        di = db * m["mult_eff"] * xc
        dxc = db * m["mult_eff"] * i
        live = m["valid"] & jnp.logical_not(m["first"])
        e2 = jnp.exp(2.0 * m["la"])
        dm = jnp.where(live, db * i * xc, 0.0)
        dla = da * m["a"] + jnp.where(live, dm * (-e2 / m["mult"]), 0.0)
        dr = dla * (-C_RGLRU * m["sp"])
        dsp = jnp.sum(dla * (-C_RGLRU * r), axis=0, keepdims=True)
        dpr = (dr * r * (1.0 - r))
        dpi = (di * i * (1.0 - i))
        dprb, dpib = dpr.astype(BF16), dpi.astype(BF16)
        dxc = dxc + _dot_nt(dprb, wa) + _dot_nt(dpib, wx)
        du = (cw[3:4, :] * dxc + cw[2:3, :] * pltpu.roll(dxc, lp - 1, 0) + cw[1:2, :] * pltpu.roll(dxc, lp - 2, 0)
              + cw[0:1, :] * pltpu.roll(dxc, lp - 3, 0))
        du_ref[...] = jnp.where(m["valid"], du, 0.0).astype(BF16)
        tap = lax.broadcasted_iota(jnp.int32, (CONV_K, LRU_TILE), 0)
        dcw = jnp.zeros((CONV_K, LRU_TILE), F32)
        for kk in range(CONV_K):
            shifted = u if kk == CONV_K - 1 else pltpu.roll(u, CONV_K - 1 - kk, 0)
            dcw = jnp.where(tap == kk, jnp.sum(dxc * shifted, axis=0, keepdims=True), dcw)
        parts = [(dcw_ref, dcw[None]), (dcb_ref, jnp.sum(dxc, axis=0, keepdims=True)[None]),
                 (dwa_ref, _dot_tn(m["xcb"], dprb)[None]), (dba_ref, jnp.sum(dpr, axis=0, keepdims=True)[None]),
                 (dwx_ref, _dot_tn(m["xcb"], dpib)[None]), (dbx_ref, jnp.sum(dpi, axis=0, keepdims=True)[None]),
                 (dlam_ref, (dsp * (-jax.nn.sigmoid(-lam)))[None])]

        @pl.when(b_idx == 0)
        def _():
            for ref, val in parts:
                ref[...] = val

        @pl.when(b_idx != 0)
        def _():
            for ref, val in parts:
                ref[...] += val

    bshape = jax.ShapeDtypeStruct((rows, LRU_W), BF16)
    vec3 = pl.BlockSpec((1, 1, LRU_TILE), lambda t, b: (t, 0, 0))
    vshape = jax.ShapeDtypeStruct((N_LRU_TILES, 1, LRU_TILE), F32)
    mshape = jax.ShapeDtypeStruct((N_LRU_TILES, LRU_TILE, LRU_TILE), F32)
    return pl.pallas_call(
        body, name="lru_bwd", grid=(N_LRU_TILES, nb),
        in_specs=[seq(Z_U // LRU_TILE), seq(Z_G // LRU_TILE), seq(0), seq(0), seq(0), cwspec, vec, mat, vec, mat,
                  vec, vec],
        out_specs=[seq(0), seq(0), cwspec, vec3, mat, vec3, mat, vec3, vec3],
        out_shape=[bshape, bshape, jax.ShapeDtypeStruct((N_LRU_TILES, CONV_K, LRU_TILE), F32), vshape, mshape,
                   vshape, mshape, vshape, vshape],
        scratch_shapes=[pltpu.VMEM((lp, LRU_TILE), F32), pltpu.VMEM((lp, LRU_TILE), F32)],
        compiler_params=_params(("parallel", "arbitrary")))(z, z, hs, hp, dy, cw, cb, wa, ba, wx, bx, lam)


def _mix_out_call(ya, yl, ga, gl, wout, h, tm):
    rows, d = h.shape

    def body(ya_ref, yl_ref, ga_ref, gl_ref, w_ref, h_ref, y_ref, o_ref):
        a = ya_ref[...]
        l = yl_ref[...]
        an = (a * _rms(a, MLA_W) * ga_ref[...]).astype(BF16)
        ln = (l * _rms(l, LRU_W) * gl_ref[...]).astype(BF16)
        y_ref[:, 0:MLA_W] = an
        y_ref[:, MLA_W:MLA_W + LRU_W] = ln
        o_ref[...] = h_ref[...] + _dot(an, w_ref[0:MLA_W, :]) + _dot(ln, w_ref[MLA_W:MLA_W + LRU_W, :])

    half = pl.BlockSpec((tm, MLA_W), lambda i: (i, 0))
    g = pl.BlockSpec((1, MLA_W), lambda i: (0, 0))
    full = pl.BlockSpec((tm, d), lambda i: (i, 0))
    return pl.pallas_call(
        body, name="mix_out", grid=(rows // tm,),
        in_specs=[half, half, g, g, pl.BlockSpec((MLA_W + LRU_W, d), lambda i: (0, 0)), full],
        out_specs=[full, full],
        out_shape=[jax.ShapeDtypeStruct((rows, MLA_W + LRU_W), BF16), jax.ShapeDtypeStruct((rows, d), F32)],
        compiler_params=_params(("parallel",)))(ya, yl, ga, gl, wout, h)


def _mix_out_bwd_call(dhb, wout, ya, yl, ga, gl, tm):
    rows = ya.shape[0]
    d = dhb.shape[1]

    def body(dh_ref, w_ref, ya_ref, yl_ref, ga_ref, gl_ref, dya_ref, dyl_ref, dga_ref, dgl_ref):
        dy = _dot_nt(dh_ref[...], w_ref[...])
        outs = []
        for val, g_ref, lo, out_ref in ((ya_ref[...], ga_ref, 0, dya_ref), (yl_ref[...], gl_ref, MLA_W, dyl_ref)):
            r = _rms(val, MLA_W)
            n = val * r
            dyn = dy[:, lo:lo + MLA_W]
            out_ref[...] = _rms_bwd(dyn * g_ref[...], n, r, MLA_W)
            outs.append(jnp.sum(dyn * n, axis=0, keepdims=True))

        @pl.when(pl.program_id(0) == 0)
        def _():
            dga_ref[...] = outs[0]
            dgl_ref[...] = outs[1]

        @pl.when(pl.program_id(0) != 0)
        def _():
            dga_ref[...] += outs[0]
            dgl_ref[...] += outs[1]

    half = pl.BlockSpec((tm, MLA_W), lambda i: (i, 0))
    g = pl.BlockSpec((1, MLA_W), lambda i: (0, 0))
    return pl.pallas_call(
        body, name="mix_out_bwd", grid=(rows // tm,),
        in_specs=[pl.BlockSpec((tm, d), lambda i: (i, 0)), pl.BlockSpec((MLA_W + LRU_W, d), lambda i: (0, 0)),
                  half, half, g, g],
        out_specs=[half, half, g, g],
        out_shape=[jax.ShapeDtypeStruct((rows, MLA_W), F32), jax.ShapeDtypeStruct((rows, LRU_W), F32),
                   jax.ShapeDtypeStruct((1, MLA_W), F32), jax.ShapeDtypeStruct((1, LRU_W), F32)],
        compiler_params=_params(("arbitrary",)))(dhb, wout, ya, yl, ga, gl)


def _final_call(h, g, target, lp, tm):
    rows, d = h.shape
    tpe = lp // tm

    def body(h_ref, g_ref, t_ref, dh_ref, dhb_ref, dg_ref, loss_ref):
        i = pl.program_id(0)
        x = h_ref[...]
        g = g_ref[...]
        r = _rms(x, d)
        n = x * r
        row = (i % tpe) * tm + lax.broadcasted_iota(jnp.int32, (tm, 1), 0)
        err = jnp.where(row >= CHUNK, n * g - t_ref[...], 0.0)
        dout = err * (1.0 / d)
        dh = _rms_bwd(dout * g, n, r, d)
        dh_ref[...] = dh
        dhb_ref[...] = dh.astype(BF16)
        dg = jnp.sum(dout * n, axis=0, keepdims=True)
        part = jnp.sum(jnp.sum(err * err, axis=1, keepdims=True), axis=0, keepdims=True) * (0.5 / d)
        loss = jnp.broadcast_to(part, (1, 128))

        @pl.when(i == 0)
        def _():
            dg_ref[...] = dg
            loss_ref[...] = loss

        @pl.when(i != 0)
        def _():
            dg_ref[...] += dg
            loss_ref[...] += loss

    full = pl.BlockSpec((tm, d), lambda i: (i, 0))
    return pl.pallas_call(
        body, name="final_loss", grid=(rows // tm,),
        in_specs=[full, pl.BlockSpec((1, d), lambda i: (0, 0)), full],
        out_specs=[full, full, pl.BlockSpec((1, d), lambda i: (0, 0)), pl.BlockSpec((1, 128), lambda i: (0, 0))],
        out_shape=[jax.ShapeDtypeStruct((rows, d), F32), jax.ShapeDtypeStruct((rows, d), BF16),
                   jax.ShapeDtypeStruct((1, d), F32), jax.ShapeDtypeStruct((1, 128), F32)],
        compiler_params=_params(("arbitrary",)))(h, g, target)


def _ffn_dact_call(name, dhb, wd, gate, up, tm):
    rows, d = dhb.shape
    ns, fs, _ = wd.shape

    def body(dh_ref, wd_ref, g_ref, p_ref, dg_ref, dp_ref):
        da = 0.5 * _dot_nt(dh_ref[...], wd_ref[0])
        g = g_ref[0].astype(F32)
        p = p_ref[0].astype(F32)
        sg = jax.nn.sigmoid(g)
        dg_ref[0] = (da * p * sg * (1.0 + g * (1.0 - sg))).astype(BF16)
        dp_ref[0] = (da * g * sg).astype(BF16)

    aspec = pl.BlockSpec((1, tm, fs), lambda s, i: (s, i, 0))
    oshape = jax.ShapeDtypeStruct((ns, rows, fs), BF16)
    return pl.pallas_call(
        body, name=name, grid=(ns, rows // tm),
        in_specs=[pl.BlockSpec((tm, d), lambda s, i: (i, 0)), pl.BlockSpec((1, fs, d), lambda s, i: (s, 0, 0)),
                  aspec, aspec],
        out_specs=[aspec, aspec], out_shape=[oshape, oshape],
        compiler_params=_params(("parallel", "parallel")))(dhb, wd, gate, up)


def _norm_in_bwd_call(name, pieces, h, g, dres, tm):
    rows, d = h.shape
    npc = len(pieces)

    def body(*refs):
        d_refs = refs[0:2 * npc:2]
        w_refs = refs[1:2 * npc:2]
        h_ref, g_ref, dres_ref, dh_ref, dhb_ref, dg_ref = refs[2 * npc:]
        du = jnp.zeros((tm, d), F32)
        for d_ref, w_ref in zip(d_refs, w_refs):
            if len(d_ref.shape) == 3:
                for s in range(d_ref.shape[0]):
                    du = du + _dot(d_ref[s], w_ref[s])
            else:
                du = du + _dot(d_ref[...], w_ref[...])
        x = h_ref[...]
        r = _rms(x, d)
        n = x * r
        dh = dres_ref[...] + _rms_bwd(du * g_ref[...], n, r, d)
        dh_ref[...] = dh
        dhb_ref[...] = dh.astype(BF16)
        dg = jnp.sum(du * n, axis=0, keepdims=True)

        @pl.when(pl.program_id(0) == 0)
        def _():
            dg_ref[...] = dg

        @pl.when(pl.program_id(0) != 0)
        def _():
            dg_ref[...] += dg

    in_specs, args = [], []
    for dd, w in pieces:
        if dd.ndim == 3:
            in_specs.append(pl.BlockSpec((dd.shape[0], tm, dd.shape[2]), lambda i: (0, i, 0)))
            in_specs.append(pl.BlockSpec(w.shape, lambda i: (0, 0, 0)))
        else:
            in_specs.append(pl.BlockSpec((tm, dd.shape[1]), lambda i: (i, 0)))
            in_specs.append(pl.BlockSpec(w.shape, lambda i: (0, 0)))
        args += [dd, w]
    full = pl.BlockSpec((tm, d), lambda i: (i, 0))
    gspec = pl.BlockSpec((1, d), lambda i: (0, 0))
    return pl.pallas_call(
        body, name=name, grid=(rows // tm,),
        in_specs=in_specs + [full, gspec, full],
        out_specs=[full, full, gspec],
        out_shape=[jax.ShapeDtypeStruct((rows, d), F32), jax.ShapeDtypeStruct((rows, d), BF16),
                   jax.ShapeDtypeStruct((1, d), F32)],
        compiler_params=_params(("arbitrary",)))(*args, h, g, dres)


def _wgrad_call(name, a, b, scale=1.0):
    a3, b3 = a.ndim == 3, b.ndim == 3
    ns = a.shape[0] if a3 else (b.shape[0] if b3 else 1)
    rows, m = a.shape[-2:]
    n = b.shape[-1]
    tmm = m if a3 else _col_tile(m, 256)

    def body(a_ref, b_ref, o_ref):
        av = a_ref[0] if a3 else a_ref[...]
        bv = b_ref[0] if b3 else b_ref[...]
        res = _dot_tn(av, bv)
        if scale != 1.0:
            res = res * scale
        if a3 or b3:
            o_ref[0] = res
        else:
            o_ref[...] = res

    aspec = (pl.BlockSpec((1, rows, tmm), lambda s, j: (s, 0, j)) if a3
             else pl.BlockSpec((rows, tmm), lambda s, j: (0, j)))
    bspec = (pl.BlockSpec((1, rows, n), lambda s, j: (s, 0, 0)) if b3
             else pl.BlockSpec((rows, n), lambda s, j: (0, 0)))
    if a3 or b3:
        ospec = pl.BlockSpec((1, tmm, n), lambda s, j: (s, j, 0))
        oshape = jax.ShapeDtypeStruct((ns, m, n), F32)
    else:
        ospec = pl.BlockSpec((tmm, n), lambda s, j: (j, 0))
        oshape = jax.ShapeDtypeStruct((m, n), F32)
    return pl.pallas_call(
        body, name=name, grid=(ns, m // tmm), in_specs=[aspec, bspec], out_specs=ospec, out_shape=oshape,
        compiler_params=_params(("parallel", "parallel")))(a, b)


def _mla_prep_bwd_call(z, dq, dk, dv, gql, gkvl, gqh, gkh, wuq, wuk, wuv, tabs, lp, tm):
    rows = z.shape[0]
    tpe = lp // tm

    def body(z_ref, dq_ref, dk_ref, dv_ref, gql_ref, gkvl_ref, gqh_ref, gkh_ref, wuq_ref, wuk_ref, wuv_ref,
             c_ref, s1_ref, s2_ref, dz_ref, dqp_ref, dkn_ref, dvv_ref, dgql_ref, dgkvl_ref, dgqh_ref, dgkh_ref):
        gql, gkvl = gql_ref[...], gkvl_ref[...]
        gq, gk = gqh_ref[...], gkh_ref[...]
        wuq, wuk, wuv = wuq_ref[...], wuk_ref[...], wuv_ref[...]
        m = _mla_heads(z_ref[...], gql, gkvl, wuq, wuk, wuv)
        c, s1, s2 = c_ref[...], s1_ref[...], s2_ref[...]
        dgq = jnp.zeros((1, D_QKP), F32)
        dgk = jnp.zeros((1, D_QKP), F32)
        dkr = jnp.zeros((tm, D_QKP - D_NOPE), F32)
        for hd in range(HEADS):
            qn, rqh, knn, krn, rkh = m["heads"][hd]
            dqg = jnp.concatenate([dq_ref[hd, :, 0:D_NOPE].astype(F32),
                                   _rope_t(dq_ref[hd, :, D_NOPE:D_QKP].astype(F32), c, s1, s2)], axis=1)
            dgq = dgq + jnp.sum(dqg * qn, axis=0, keepdims=True)
            dqn = dqg * gq
            dqr = rqh * (dqn - qn * (jnp.sum(dqn * qn, axis=-1, keepdims=True) * (1.0 / D_QK)))
            dqp_ref[:, hd * D_QKP:(hd + 1) * D_QKP] = dqr.astype(BF16)
            kn_full = jnp.concatenate([knn, krn], axis=1)
            dkg = jnp.concatenate([dk_ref[hd, :, 0:D_NOPE].astype(F32),
                                   _rope_t(dk_ref[hd, :, D_NOPE:D_QKP].astype(F32), c, s1, s2)], axis=1)
            dgk = dgk + jnp.sum(dkg * kn_full, axis=0, keepdims=True)
            dkn = dkg * gk
            dkraw = rkh * (dkn - kn_full * (jnp.sum(dkn * kn_full, axis=-1, keepdims=True) * (1.0 / D_QK)))
            dkn_ref[:, hd * D_NOPE:(hd + 1) * D_NOPE] = dkraw[:, 0:D_NOPE].astype(BF16)
            dkr = dkr + dkraw[:, D_NOPE:D_QKP]
            dvv_ref[:, hd * D_V:(hd + 1) * D_V] = dv_ref[hd]
        dcqn = _dot(dqp_ref[...], wuq)
        dckvn = _dot_nt(dkn_ref[...], wuk) + _dot_nt(dvv_ref[...], wuv)
        dz_ref[:, 0:Q_RANK] = _rms_bwd(dcqn * gql, m["nq"], m["rq"], Q_RANK).astype(BF16)
        dz_ref[:, Q_RANK:Z_KR] = _rms_bwd(dckvn * gkvl, m["nkv"], m["rkv"], KV_RANK).astype(BF16)
        dz_ref[:, Z_KR:Z_MLA] = dkr.astype(BF16)
        parts = [(dgql_ref, jnp.sum(dcqn * m["nq"], axis=0, keepdims=True)),
                 (dgkvl_ref, jnp.sum(dckvn * m["nkv"], axis=0, keepdims=True)), (dgqh_ref, dgq), (dgkh_ref, dgk)]

        @pl.when(pl.program_id(0) == 0)
        def _():
            for ref, val in parts:
                ref[...] = val

        @pl.when(pl.program_id(0) != 0)
        def _():
            for ref, val in parts:
                ref[...] += val

    def const(shape):
        return pl.BlockSpec(shape, lambda i: tuple(0 for _ in shape))

    tab = pl.BlockSpec((tm, 128), lambda i: (i % tpe, 0))
    hq = pl.BlockSpec((HEADS, tm, D_QKP), lambda i: (0, i, 0))
    hv = pl.BlockSpec((HEADS, tm, D_V), lambda i: (0, i, 0))

    def rowspec(n):
        return pl.BlockSpec((tm, n), lambda i: (i, 0))

    return pl.pallas_call(
        body, name="mla_prep_bwd", grid=(rows // tm,),
        in_specs=[rowspec(Z_MLA), hq, hq, hv, const((1, Q_RANK)), const((1, KV_RANK)), const((1, D_QKP)),
                  const((1, D_QKP)), const((HEADS * D_QKP, Q_RANK)), const((KV_RANK, HEADS * D_NOPE)),
                  const((KV_RANK, HEADS * D_V)), tab, tab, tab],
        out_specs=[rowspec(Z_MLA), rowspec(HEADS * D_QKP), rowspec(HEADS * D_NOPE), rowspec(HEADS * D_V),
                   const((1, Q_RANK)), const((1, KV_RANK)), const((1, D_QKP)), const((1, D_QKP))],
        out_shape=[jax.ShapeDtypeStruct((rows, Z_MLA), BF16), jax.ShapeDtypeStruct((rows, HEADS * D_QKP), BF16),
                   jax.ShapeDtypeStruct((rows, HEADS * D_NOPE), BF16), jax.ShapeDtypeStruct((rows, HEADS * D_V), BF16),
                   jax.ShapeDtypeStruct((1, Q_RANK), F32), jax.ShapeDtypeStruct((1, KV_RANK), F32),
                   jax.ShapeDtypeStruct((1, D_QKP), F32), jax.ShapeDtypeStruct((1, D_QKP), F32)],
        compiler_params=_params(("arbitrary",)))(z, dq, dk, dv, gql, gkvl, gqh, gkh, wuq, wuk, wuv, *tabs)


def _local_step(h0, target, w, nb, lp):
    tm = _row_tile(lp, 1024)
    te = _row_tile(lp, 512)
    tabs = _rope_tables(lp)
    g = {}

    def ffn_fwd(tag, h):
        u = _rmsnorm_call(tag + "_norm", h, w[tag + "_norm"], te)
        gate, up, act = _ffn_up_call(tag + "_up", u, w[tag + "_w_gate"], w[tag + "_w_up"], tm)
        return _ffn_down_call(tag + "_down", act, w[tag + "_w_down"], h, te), (u, gate, up, act)

    def ffn_bwd(tag, h, saved, dh, dhb):
        u, gate, up, act = saved
        dgate, dup = _ffn_dact_call(tag + "_dact", dhb, w[tag + "_w_down"], gate, up, tm)
        g[tag + "_w_down"] = _wgrad_call(tag + "_dwd", act, dhb, 0.5)
        g[tag + "_w_gate"] = _wgrad_call(tag + "_dwg", dgate, u)
        g[tag + "_w_up"] = _wgrad_call(tag + "_dwu", dup, u)
        dh_in, dhb_in, g[tag + "_norm"] = _norm_in_bwd_call(
            tag + "_din", [(dgate, w[tag + "_w_gate"]), (dup, w[tag + "_w_up"])], h, w[tag + "_norm"], dh, te)
        return dh_in, dhb_in

    h1, s1 = ffn_fwd("ffn1", h0)
    un = _rmsnorm_call("mix_norm", h1, w["mix_norm"], te)
    z = _mm_call("mix_in", un, w["w_in"], tm, F32)
    mla_w = (w["q_latent_norm"], w["kv_latent_norm"], w["q_head_norm"], w["k_head_norm"], w["w_uq"], w["w_uk"],
             w["w_uv"])
    q, k, v, cqn, ckvn = _mla_prep_call(z, *mla_w, tabs, lp, te)
    o, lse = _attn_fwd_call(q, k, v, nb, lp)
    lru_w = (w["conv_w"], w["conv_b"], w["gate_a_w"], w["gate_a_b"], w["gate_x_w"], w["gate_x_b"], w["lru_lambda"])
    yl, hs, hp = _lru_fwd_call(z, *lru_w, nb, lp)
    y, h2 = _mix_out_call(o, yl, w["attn_out_norm"], w["lru_out_norm"], w["w_out"], h1, te)
    h3, s2 = ffn_fwd("ffn2", h2)
    dh3, dh3b, g["final_norm"], loss = _final_call(h3, w["final_norm"], target, lp, te)

    dh2, dh2b = ffn_bwd("ffn2", h2, s2, dh3, dh3b)
    g["w_out"] = _wgrad_call("dw_out", y, dh2b)
    dya, dyl, g["attn_out_norm"], g["lru_out_norm"] = _mix_out_bwd_call(
        dh2b, w["w_out"], o, yl, w["attn_out_norm"], w["lru_out_norm"], te)
    dq, dk, dv = _attn_bwd_call(q, k, v, o, lse, dya, nb, lp)
    (dz_mla, dqp, dkn, dvv, g["q_latent_norm"], g["kv_latent_norm"], g["q_head_norm"],
     g["k_head_norm"]) = _mla_prep_bwd_call(z, dq, dk, dv, *mla_w, tabs, lp, te)
    g["w_uq"] = _wgrad_call("dw_uq", dqp, cqn)
    g["w_uk"] = _wgrad_call("dw_uk", ckvn, dkn)
    g["w_uv"] = _wgrad_call("dw_uv", ckvn, dvv)
    (du, dgt, g["conv_w"], g["conv_b"], g["gate_a_w"], g["gate_a_b"], g["gate_x_w"], g["gate_x_b"],
     g["lru_lambda"]) = _lru_bwd_call(z, hs, hp, dyl, *lru_w, nb, lp)
    win = w["w_in"]
    g["w_in"] = jnp.concatenate(
        [_wgrad_call("dw_in_mla", dz_mla, un), _wgrad_call("dw_in_u", du, un), _wgrad_call("dw_in_g", dgt, un)],
        axis=0)
    dh1, dh1b, g["mix_norm"] = _norm_in_bwd_call(
        "mix_din", [(dz_mla, win[0:Z_MLA]), (du, win[Z_U:Z_G]), (dgt, win[Z_G:Z_W])], h1, w["mix_norm"], dh2, te)
    dh0, _ = ffn_bwd("ffn1", h0, s1, dh1, dh1b)
    return loss, dh0, g


def _place():
    x, y, c = lax.axis_index("x"), lax.axis_index("y"), lax.axis_index("c")
    return x, y, c, [(1 - x, y), (x, 1 - y), (1 - x, 1 - y)]


def _any_specs(n):
    return [pl.BlockSpec(memory_space=pl.ANY)] * n


def _remote(src, dst, sems, k, dev):
    send_sems, recv_sems = sems
    return pltpu.make_async_remote_copy(src_ref=src, dst_ref=dst, send_sem=send_sems.at[k], recv_sem=recv_sems.at[k],
                                        device_id=dev, device_id_type=MESH)


EW_VMEM_BYTES = 24 * 1024 * 1024


def _fit_rows(rows, cols, blocks):
    return _row_tile(rows, max(16, int(EW_VMEM_BYTES // (8 * blocks)) // cols))


class _Geom:
    def __init__(self, n0, n1, blocks=1.0):
        self.n0, self.n1 = n0, n1
        self.axis = 0 if n0 % 32 == 0 else 1
        self.h0, self.h1 = (n0 // 2, n1) if self.axis == 0 else (n0, n1 // 2)
        self.tr = _fit_rows(self.h0, self.h1, blocks)
        self.nblk = self.h0 // self.tr

    def half_ref(self, ref, lead, idx):
        if self.axis == 0:
            return ref.at[(*lead, pl.ds(idx * self.h0, self.h0))]
        return ref.at[(*lead, slice(None), pl.ds(idx * self.h1, self.h1))]

    def half_block(self, lead, i, idx):
        return (*lead, idx * self.nblk + i, 0) if self.axis == 0 else (*lead, i, idx)


def _gather_call(bufs):
    n = len(bufs)
    geoms = [_Geom(*b.shape[1:]) for b in bufs]

    def body(*refs):
        outs, sems = refs[n:2 * n], refs[2 * n:]
        x, y, c, chips = _place()
        s = 2 * x + y
        sib = (x, y, 1 - c)
        started = []
        for a in range(n):
            mine = geoms[a].half_ref(outs[a], (s,), c)
            for j, (cx, cy) in enumerate(chips):
                cp = _remote(mine, mine, sems, 6 * a + j, (cx, cy, c))
                cp.start()
                started.append(cp)
        for a in range(n):
            for j, (cx, cy) in enumerate(chips):
                land = geoms[a].half_ref(outs[a], (2 * cx + cy,), c)
                _remote(land, land, sems, 6 * a + j, sib).wait_recv()
                cp = _remote(land, land, sems, 6 * a + 3 + j, sib)
                cp.start()
                started.append(cp)
        for a in range(n):
            for j, (cx, cy) in enumerate(chips):
                land = geoms[a].half_ref(outs[a], (2 * cx + cy,), 1 - c)
                _remote(land, land, sems, 6 * a + 3 + j, sib).wait_recv()
        for cp in started:
            cp.wait_send()

    return pl.pallas_call(
        body, name="gather_weights",
        out_shape=[jax.ShapeDtypeStruct(b.shape, b.dtype) for b in bufs],
        in_specs=_any_specs(n), out_specs=_any_specs(n), input_output_aliases={a: a for a in range(n)},
        scratch_shapes=[pltpu.SemaphoreType.DMA((6 * n,)), pltpu.SemaphoreType.DMA((6 * n,))])(*bufs)


def _reduce_pair_call(grads):
    n = len(grads)
    geoms = [_Geom(*a.shape[1:]) for a in grads]

    def body(*refs):
        ins, outs, sems = refs[:n], refs[n:2 * n], refs[2 * n:]
        x, y, c, _ = _place()
        sib = (x, y, 1 - c)
        cps = []
        for a in range(n):
            cp = _remote(geoms[a].half_ref(ins[a], (slice(None),), 1 - c), outs[a], sems, a, sib)
            cp.start()
            cps.append(cp)
        for cp in cps:
            cp.wait_recv()
        for cp in cps:
            cp.wait_send()

    return pl.pallas_call(
        body, name="reduce_pair",
        out_shape=[jax.ShapeDtypeStruct((N_SHARD, g.h0, g.h1), a.dtype) for a, g in zip(grads, geoms)],
        in_specs=_any_specs(n), out_specs=_any_specs(n),
        scratch_shapes=[pltpu.SemaphoreType.DMA((n,)), pltpu.SemaphoreType.DMA((n,))])(*grads)


def _reduce_chips_call(parts):
    n = len(parts)

    def body(*refs):
        ins, outs, sems = refs[:n], refs[n:2 * n], refs[2 * n:]
        x, y, c, chips = _place()
        cps = []
        for a in range(n):
            for j, (cx, cy) in enumerate(chips):
                cp = _remote(ins[a].at[2 * cx + cy], outs[a].at[j], sems, 3 * a + j, (cx, cy, c))
                cp.start()
                cps.append(cp)
        for cp in cps:
            cp.wait_recv()
        for cp in cps:
            cp.wait_send()

    return pl.pallas_call(
        body, name="reduce_chips",
        out_shape=[jax.ShapeDtypeStruct((3,) + a.shape[1:], a.dtype) for a in parts],
        in_specs=_any_specs(n), out_specs=_any_specs(n),
        scratch_shapes=[pltpu.SemaphoreType.DMA((3 * n,)), pltpu.SemaphoreType.DMA((3 * n,))])(*parts)


def _share_pair_call(bufs):
    n = len(bufs)
    geoms = [_Geom(*b.shape) for b in bufs]

    def body(*refs):
        outs, sems = refs[n:2 * n], refs[2 * n:]
        x, y, c, _ = _place()
        sib = (x, y, 1 - c)
        cps = []
        for a in range(n):
            mine = geoms[a].half_ref(outs[a], (), c)
            cp = _remote(mine, mine, sems, a, sib)
            cp.start()
            cps.append(cp)
        for a in range(n):
            land = geoms[a].half_ref(outs[a], (), 1 - c)
            _remote(land, land, sems, a, sib).wait_recv()
        for cp in cps:
            cp.wait_send()

    return pl.pallas_call(
        body, name="share_pair",
        out_shape=[jax.ShapeDtypeStruct(b.shape, b.dtype) for b in bufs],
        in_specs=_any_specs(n), out_specs=_any_specs(n), input_output_aliases={a: a for a in range(n)},
        scratch_shapes=[pltpu.SemaphoreType.DMA((n,)), pltpu.SemaphoreType.DMA((n,))])(*bufs)


def _gather_small_call(pack):
    r, d = pack.shape

    def body(in_ref, out_ref, send_sems, recv_sems):
        x, y, c, _ = _place()
        me = 4 * x + 2 * y + c
        out_ref[me] = in_ref[...]
        cps = []
        for k in range(1, 8):
            bx, by, bc = (k >> 2) & 1, (k >> 1) & 1, k & 1
            px, py, pc = x ^ bx, y ^ by, c ^ bc
            cp = pltpu.make_async_remote_copy(
                src_ref=in_ref, dst_ref=out_ref.at[me], send_sem=send_sems.at[k - 1], recv_sem=recv_sems.at[k - 1],
                device_id=(px, py, pc), device_id_type=MESH)
            cp.start()
            cps.append((cp, 4 * px + 2 * py + pc))
        for k, (cp, peer) in enumerate(cps):
            land = out_ref.at[peer]
            pltpu.make_async_remote_copy(
                src_ref=land, dst_ref=land, send_sem=send_sems.at[k], recv_sem=recv_sems.at[k],
                device_id=(x, y, c), device_id_type=MESH).wait_recv()
        for cp, _ in cps:
            cp.wait_send()

    return pl.pallas_call(
        body, name="gather_small",
        out_shape=jax.ShapeDtypeStruct((8, r, d), pack.dtype),
        in_specs=[pl.BlockSpec(memory_space=pltpu.VMEM)], out_specs=pl.BlockSpec(memory_space=pltpu.VMEM),
        scratch_shapes=[pltpu.SemaphoreType.DMA((7,)), pltpu.SemaphoreType.DMA((7,))])(pack)


def _ew_call(name, fn, ins, out_dtypes):
    shape = ins[0].shape
    cols = shape[-1]
    rows = 1
    for s_ in shape[:-1]:
        rows *= s_
    ins2 = [a.reshape(rows, cols) for a in ins]
    tr = rows
    for t in range(16, min(rows, max(16, (1 << 19) // cols)) + 1, 16):
        if rows % t == 0:
            tr = t
    no = len(out_dtypes)

    def body(*refs):
        outs = fn(*[r[...] for r in refs[:len(ins2)]])
        for ref, val in zip(refs[len(ins2):], outs):
            ref[...] = val.astype(ref.dtype)

    spec = pl.BlockSpec((tr, cols), lambda i: (i, 0))
    res = pl.pallas_call(
        body, name=name, grid=(rows // tr,), in_specs=[spec] * len(ins2), out_specs=[spec] * no,
        out_shape=[jax.ShapeDtypeStruct((rows, cols), dt) for dt in out_dtypes],
        compiler_params=_params(("parallel",)))(*ins2)
    return [r.reshape(shape) for r in res]


def _adamw_math(w, g, m, v):
    m = ADAM_B1 * m + (1.0 - ADAM_B1) * g
    v = ADAM_B2 * v + (1.0 - ADAM_B2) * (g * g)
    m_hat = m / (1.0 - ADAM_B1 ** ADAM_STEP)
    v_hat = v / (1.0 - ADAM_B2 ** ADAM_STEP)
    delta = -ADAM_LR * (m_hat / (jnp.sqrt(v_hat) + ADAM_EPS) + ADAM_WD * w)
    return delta, m, v


def _adamw_call(name, w, g, m, v):
    return _ew_call(name, _adamw_math, [w, g, m, v], [F32, F32, F32])


def _tiled_call(name, fn, place, grid, in_items, out_items):
    ni = len(in_items)

    def body(place_ref, *refs):
        vals = fn(*[r[...] for r in refs[:ni]])
        for ref, val in zip(refs[ni:], vals):
            ref[...] = val.astype(ref.dtype)

    spec = pltpu.PrefetchScalarGridSpec(
        num_scalar_prefetch=1, grid=grid,
        in_specs=[pl.BlockSpec(blk, imap) for _, blk, imap in in_items],
        out_specs=[pl.BlockSpec(blk, imap) for _, _, blk, imap in out_items])
    return pl.pallas_call(
        body, name=name, grid_spec=spec,
        out_shape=[jax.ShapeDtypeStruct(shp, dt) for shp, dt, _, _ in out_items],
        compiler_params=_params(("arbitrary",) * len(grid)))(place, *[a for a, _, _ in in_items])


def _cast_call(name, place, shards):
    n0, n1 = shards[0].shape
    tr = _fit_rows(n0, n1, 1.5 * len(shards))
    ins = [(a, (tr, n1), lambda i, p: (i, 0)) for a in shards]
    outs = [((N_SHARD, n0, n1), BF16, (1, tr, n1), lambda i, p: (p[0], i, 0)) for _ in shards]
    return _tiled_call(name, lambda *v: [x[None] for x in v], place, (n0 // tr,), ins, outs)


def _pair_sum_call(name, place, fulls, gots):
    k = len(fulls)
    g = _Geom(*fulls[0].shape[1:], blocks=2.5 * k)
    blk = (1, g.tr, g.h1)
    ins = [(a, blk, lambda s, i, p: g.half_block((s,), i, p[1])) for a in fulls]
    ins += [(a, blk, lambda s, i, p: (s, i, 0)) for a in gots]
    outs = [((N_SHARD, g.h0, g.h1), BF16, blk, lambda s, i, p: (s, i, 0)) for _ in fulls]
    return _tiled_call(name, lambda *v: [v[j] + v[k + j] for j in range(k)], place, (N_SHARD, g.nblk), ins, outs)


def _chip_sum_call(name, place, fulls, gots, recvs):
    k = len(fulls)
    g = _Geom(*fulls[0].shape[1:], blocks=4.5 * k)
    blk = (1, g.tr, g.h1)
    ins = [(a, blk, lambda i, p: g.half_block((p[0],), i, p[1])) for a in fulls]
    ins += [(a, blk, lambda i, p: (p[0], i, 0)) for a in gots]
    ins += [(a, (3, g.tr, g.h1), lambda i, p: (0, i, 0)) for a in recvs]
    outs = [((g.n0, g.n1), F32, (g.tr, g.h1), lambda i, p: g.half_block((), i, p[1])) for _ in fulls]

    def fn(*v):
        res = []
        for j in range(k):
            r = v[2 * k + j].astype(F32)
            res.append(v[j][0] + v[k + j][0] + r[0] + r[1] + r[2])
        return res

    return _tiled_call(name, fn, place, (g.nblk,), ins, outs)


def _adamw_group_call(name, place, ws, gs, ms, vs):
    k = len(ws)
    n0, n1 = ws[0].shape
    tr = _fit_rows(n0, n1, 8 * k)
    blk = (tr, n1)
    imap = lambda i, p: (i, 0)
    ins = [(a, blk, imap) for a in list(ws) + list(gs) + list(ms) + list(vs)]
    outs = [((n0, n1), F32, blk, imap) for _ in range(4 * k)]

    def fn(*v):
        res = []
        for j in range(k):
            delta, m, vv = _adamw_math(v[j], v[k + j], v[2 * k + j], v[3 * k + j])
            res += [v[k + j], delta, m, vv]
        return res

    flat = _tiled_call(name, fn, place, (n0 // tr,), ins, outs)
    return [flat[4 * j:4 * j + 4] for j in range(k)]


def _small_update_call(gathered, wp, mp, vp):
    nd, r, d = gathered.shape

    def body(g_ref, w_ref, m_ref, v_ref, gs_ref, d_ref, nm_ref, nv_ref):
        gs = g_ref[0]
        for k in range(1, nd):
            gs = gs + g_ref[k]
        gs_ref[...] = gs
        delta, m, v = _adamw_math(w_ref[...], gs[0:SMALL_ADAM_ROWS], m_ref[...], v_ref[...])
        d_ref[...] = delta
        nm_ref[...] = m
        nv_ref[...] = v

    vm = pl.BlockSpec(memory_space=pltpu.VMEM)
    ashape = jax.ShapeDtypeStruct((SMALL_ADAM_ROWS, d), F32)
    return pl.pallas_call(
        body, name="small_update", in_specs=[vm] * 4, out_specs=[vm] * 4,
        out_shape=[jax.ShapeDtypeStruct((r, d), F32), ashape, ashape, ashape],
        compiler_params=pltpu.CompilerParams(vmem_limit_bytes=VMEM_LIMIT_BYTES))(gathered, wp, mp, vp)


SMALL_NAMES = ["ffn1_norm", "mix_norm", "ffn2_norm", "final_norm", "q_latent_norm", "kv_latent_norm",
               "q_head_norm", "k_head_norm", "conv_b", "gate_a_b", "gate_x_b", "lru_lambda", "attn_out_norm",
               "lru_out_norm"]
ROW_CONV_W = 14
ROW_GATE_A = 16
ROW_GATE_X = 48
ROW_META = 80
ROW_LOSS = 96


def _row(a):
    flat = a.reshape(1, -1)
    return jnp.pad(flat, ((0, 0), (0, D_MODEL - flat.shape[1])))


def _pack_small(t, rows):
    parts = [_row(t[nm]) for nm in SMALL_NAMES]
    parts.append(t["conv_w"].reshape(2, D_MODEL))
    parts.append(t["gate_a_w"].reshape(32, D_MODEL))
    parts.append(t["gate_x_w"].reshape(32, D_MODEL))
    p = jnp.concatenate(parts, axis=0)
    return jnp.pad(p, ((0, rows - p.shape[0]), (0, 0)))


def _unpack_small(p, like):
    out = {}
    for k, nm in enumerate(SMALL_NAMES):
        out[nm] = p[k, 0:like[nm].size].reshape(like[nm].shape)
    out["gate_a_w"] = p[ROW_GATE_A:ROW_GATE_A + 32].reshape(like["gate_a_w"].shape)
    out["gate_x_w"] = p[ROW_GATE_X:ROW_GATE_X + 32].reshape(like["gate_x_w"].shape)
    return out


def _gate_dense(wg):
    w4 = wg[0].reshape(N_LRU_TILES, 2, 64, 64)
    zero = jnp.zeros((N_LRU_TILES, 64, 64), wg.dtype)
    top = jnp.concatenate([w4[:, 0], zero], axis=2)
    bot = jnp.concatenate([zero, w4[:, 1]], axis=2)
    return jnp.concatenate([top, bot], axis=1).astype(BF16)


def _gate_blocks(dw):
    return jnp.stack([dw[:, 0:64, 0:64], dw[:, 64:128, 64:128]], axis=1).reshape(8, 64, 64)


BIG_NAMES = ["ffn1_w_gate", "ffn1_w_up", "ffn1_w_down", "w_in", "w_uq", "w_uk", "w_uv", "w_out", "ffn2_w_gate",
             "ffn2_w_up", "ffn2_w_down"]
BIG_GROUPS = [["ffn1_w_gate", "ffn1_w_up", "ffn1_w_down", "ffn2_w_gate", "ffn2_w_up", "ffn2_w_down"], ["w_in"],
              ["w_uq"], ["w_uk", "w_uv"], ["w_out"]]
TRANSPOSED = ("ffn1_w_gate", "ffn1_w_up", "ffn2_w_gate", "ffn2_w_up", "w_in", "w_uq")


def _to2d(nm, a):
    return a[0].T if nm in TRANSPOSED else a[0]


def _from2d(nm, a):
    return (a.T if nm in TRANSPOSED else a)[None]


WEIGHT_NAMES = ["meta_tokens", "ffn1_norm", "ffn1_w_gate", "ffn1_w_up", "ffn1_w_down", "mix_norm", "w_in",
                "q_latent_norm", "w_uq", "kv_latent_norm", "w_uk", "w_uv", "q_head_norm", "k_head_norm", "conv_w",
                "conv_b", "gate_a_w", "gate_a_b", "gate_x_w", "gate_x_b", "lru_lambda", "attn_out_norm",
                "lru_out_norm", "w_out", "ffn2_norm", "ffn2_w_gate", "ffn2_w_up", "ffn2_w_down", "final_norm"]


def _full_weights(p, gathered, small):
    w = {nm: p[nm] for nm in SMALL_NAMES}
    for nm in ("ffn1_w_gate", "ffn1_w_up", "ffn1_w_down", "ffn2_w_gate", "ffn2_w_up", "ffn2_w_down"):
        w[nm] = gathered[nm]
    win = gathered["w_in"].reshape(IN_WIDTH, D_MODEL)
    w["w_in"] = jnp.concatenate([win[0:Z_KR + D_ROPE], jnp.zeros((128 - D_ROPE, D_MODEL), BF16),
                                 win[Z_KR + D_ROPE:]], axis=0)
    w["w_uq"] = jnp.pad(gathered["w_uq"], ((0, 0), (0, D_QKP - D_QK), (0, 0))).reshape(HEADS * D_QKP, Q_RANK)
    w["w_uk"] = gathered["w_uk"].transpose(1, 0, 2).reshape(KV_RANK, HEADS * D_NOPE)
    w["w_uv"] = gathered["w_uv"].transpose(1, 0, 2).reshape(KV_RANK, HEADS * D_V)
    w["w_out"] = gathered["w_out"].reshape(D_MODEL, D_MODEL)
    w["q_head_norm"] = jnp.pad(p["q_head_norm"], ((0, 0), (0, D_QKP - D_QK)))
    w["k_head_norm"] = jnp.pad(p["k_head_norm"], ((0, 0), (0, D_QKP - D_QK)))
    w["conv_w"] = small[:, N_META:N_META + 2, :].reshape(N_SHARD, CONV_K, LRU_TILE)
    w["gate_a_w"] = _gate_dense(p["gate_a_w"])
    w["gate_x_w"] = _gate_dense(p["gate_x_w"])
    meta = small[:, 0:N_META, :].transpose(1, 0, 2).reshape(N_META, D_MODEL)
    return w, meta


def _shard_grads(g):
    out = {nm: g[nm] for nm in ("ffn1_w_gate", "ffn1_w_up", "ffn1_w_down", "ffn2_w_gate", "ffn2_w_up",
                                "ffn2_w_down")}
    win = jnp.concatenate([g["w_in"][0:Z_KR + D_ROPE], g["w_in"][Z_MLA:]], axis=0)
    out["w_in"] = win.reshape(N_SHARD, IN_WIDTH // N_SHARD, D_MODEL)
    out["w_uq"] = g["w_uq"].reshape(HEADS, D_QKP, Q_RANK)[:, 0:D_QK, :]
    out["w_uk"] = g["w_uk"].reshape(KV_RANK, HEADS, D_NOPE).transpose(1, 0, 2)
    out["w_uv"] = g["w_uv"].reshape(KV_RANK, HEADS, D_V).transpose(1, 0, 2)
    out["w_out"] = g["w_out"].reshape(N_SHARD, D_MODEL // N_SHARD, D_MODEL)
    return out


def kernel(x, meta_tokens, ffn1_norm, ffn1_w_gate, ffn1_w_up, ffn1_w_down, mix_norm, w_in, q_latent_norm, w_uq, kv_latent_norm, w_uk, w_uv, q_head_norm, k_head_norm, conv_w, conv_b, gate_a_w, gate_a_b, gate_x_w, gate_x_b, lru_lambda, attn_out_norm, lru_out_norm, w_out, ffn2_norm, ffn2_w_gate, ffn2_w_up, ffn2_w_down, final_norm, loss_target, m_meta_tokens, m_ffn1_norm, m_ffn1_w_gate, m_ffn1_w_up, m_ffn1_w_down, m_mix_norm, m_w_in, m_q_latent_norm, m_w_uq, m_kv_latent_norm, m_w_uk, m_w_uv, m_q_head_norm, m_k_head_norm, m_conv_w, m_conv_b, m_gate_a_w, m_gate_a_b, m_gate_x_w, m_gate_x_b, m_lru_lambda, m_attn_out_norm, m_lru_out_norm, m_w_out, m_ffn2_norm, m_ffn2_w_gate, m_ffn2_w_up, m_ffn2_w_down, m_final_norm, v_meta_tokens, v_ffn1_norm, v_ffn1_w_gate, v_ffn1_w_up, v_ffn1_w_down, v_mix_norm, v_w_in, v_q_latent_norm, v_w_uq, v_kv_latent_norm, v_w_uk, v_w_uv, v_q_head_norm, v_k_head_norm, v_conv_w, v_conv_b, v_gate_a_w, v_gate_a_b, v_gate_x_w, v_gate_x_b, v_lru_lambda, v_attn_out_norm, v_lru_out_norm, v_w_out, v_ffn2_norm, v_ffn2_w_gate, v_ffn2_w_up, v_ffn2_w_down, v_final_norm):
    args = locals()
    p = {nm: args[nm] for nm in WEIGHT_NAMES}
    mom = {nm: args["m_" + nm] for nm in WEIGHT_NAMES}
    var = {nm: args["v_" + nm] for nm in WEIGHT_NAMES}
    nb, seq, d = x.shape
    lp = CHUNK + seq
    xi, yi, ci = lax.axis_index("x"), lax.axis_index("y"), lax.axis_index("c")
    chip = 2 * xi + yi

    place = jnp.stack([chip, ci]).astype(jnp.int32)
    p2 = {nm: _to2d(nm, p[nm]) for nm in BIG_NAMES}
    m2 = {nm: _to2d(nm, mom[nm]) for nm in BIG_NAMES}
    v2 = {nm: _to2d(nm, var[nm]) for nm in BIG_NAMES}

    slots = {}
    for grp in BIG_GROUPS:
        for nm, buf in zip(grp, _cast_call("cast_" + grp[0], place, [p2[nm] for nm in grp])):
            slots[nm] = buf
    small_shard = jnp.concatenate(
        [meta_tokens, conv_w[0].reshape(2, 2 * LRU_TILE), jnp.zeros((14, 2 * LRU_TILE), F32)], axis=0)
    small_slots = lax.dynamic_update_slice(jnp.zeros((N_SHARD,) + small_shard.shape, F32), small_shard[None],
                                           (chip, 0, 0))
    gathered = _gather_call([slots[nm] for nm in BIG_NAMES] + [small_slots])
    w, meta = _full_weights(p, dict(zip(BIG_NAMES, gathered[:-1])), gathered[-1])

    h0 = jnp.concatenate(
        [jnp.zeros((nb, PAD_ROWS, d), F32), jnp.broadcast_to(meta[None], (nb, N_META, d)), x], axis=1)
    target = jnp.pad(loss_target, ((0, 0), (CHUNK, 0), (0, 0)))
    loss_part, dh0, g = _local_step(h0.reshape(nb * lp, d), target.reshape(nb * lp, d), w, nb, lp)
    dh0 = dh0.reshape(nb, lp, d)
    grad_x = dh0[:, CHUNK:, :]

    gs = {nm: g[nm] for nm in SMALL_NAMES}
    gs["q_head_norm"] = g["q_head_norm"][:, 0:D_QK]
    gs["k_head_norm"] = g["k_head_norm"][:, 0:D_QK]
    for nm in ("conv_b", "gate_a_b", "gate_x_b", "lru_lambda"):
        gs[nm] = g[nm].reshape(1, LRU_W)
    gs["conv_w"] = g["conv_w"].transpose(1, 0, 2).reshape(CONV_K, LRU_W)
    gs["gate_a_w"] = _gate_blocks(g["gate_a_w"])
    gs["gate_x_w"] = _gate_blocks(g["gate_x_w"])
    pack = _pack_small(gs, ROW_META)
    pack = jnp.concatenate([pack, jnp.sum(dh0[:, PAD_ROWS:CHUNK, :], axis=0), _row(loss_part[:, 0:1]),
                            jnp.zeros((SMALL_ROWS - ROW_LOSS - 1, D_MODEL), F32)], axis=0)
    small_like = {nm: p[nm] for nm in SMALL_NAMES + ["gate_a_w", "gate_x_w"]}

    def pack_w(t):
        tt = {nm: t[nm] for nm in SMALL_NAMES + ["gate_a_w", "gate_x_w"]}
        tt["conv_w"] = jnp.zeros((CONV_K, LRU_W), F32)
        return _pack_small(tt, SMALL_ADAM_ROWS)

    gsum, dsm, msm, vsm = _small_update_call(_gather_small_call(pack), pack_w(p), pack_w(mom), pack_w(var))
    grads = _unpack_small(gsum, small_like)
    delta = _unpack_small(dsm, small_like)
    new_m = _unpack_small(msm, small_like)
    new_v = _unpack_small(vsm, small_like)
    loss = gsum[ROW_LOSS, 0]
    gmeta = gsum[ROW_META:ROW_META + N_META].reshape(N_META, N_SHARD, D_MODEL // N_SHARD)
    grads["meta_tokens"] = lax.dynamic_index_in_dim(gmeta, chip, axis=1, keepdims=False)
    gconv = gsum[ROW_CONV_W:ROW_CONV_W + 2].reshape(CONV_K, N_SHARD, LRU_TILE)
    grads["conv_w"] = lax.dynamic_index_in_dim(gconv, chip, axis=1, keepdims=False)[None]
    for nm in ("meta_tokens", "conv_w"):
        delta[nm], new_m[nm], new_v[nm] = _adamw_call("adamw_" + nm, p[nm], grads[nm], mom[nm], var[nm])

    gb = _shard_grads(g)
    from_pair = dict(zip(BIG_NAMES, _reduce_pair_call([gb[nm] for nm in BIG_NAMES])))
    chip_bf16 = {}
    for grp in BIG_GROUPS:
        sums = _pair_sum_call("pair_sum_" + grp[0], place, [gb[nm] for nm in grp], [from_pair[nm] for nm in grp])
        chip_bf16.update(zip(grp, sums))
    from_chips = dict(zip(BIG_NAMES, _reduce_chips_call([chip_bf16[nm] for nm in BIG_NAMES])))
    mine = {}
    for grp in BIG_GROUPS:
        sums = _chip_sum_call("chip_sum_" + grp[0], place, [gb[nm] for nm in grp], [from_pair[nm] for nm in grp],
                              [from_chips[nm] for nm in grp])
        mine.update(zip(grp, sums))
    shard_grads = dict(zip(BIG_NAMES, _share_pair_call([mine[nm] for nm in BIG_NAMES])))
    for grp in BIG_GROUPS:
        res = _adamw_group_call("adamw_" + grp[0], place, [p2[nm] for nm in grp], [shard_grads[nm] for nm in grp],
                                [m2[nm] for nm in grp], [v2[nm] for nm in grp])
        for nm, (gg, dd, mm, vv) in zip(grp, res):
            grads[nm], delta[nm], new_m[nm], new_v[nm] = (_from2d(nm, t) for t in (gg, dd, mm, vv))

    return (loss, grad_x, *[grads[nm] for nm in WEIGHT_NAMES], *[delta[nm] for nm in WEIGHT_NAMES],
            *[new_m[nm] for nm in WEIGHT_NAMES], *[new_v[nm] for nm in WEIGHT_NAMES])
```

```python
import functools
import math

import jax
import jax.numpy as jnp
from jax import lax
from jax.experimental import pallas as pl
from jax.experimental.pallas import tpu as pltpu

F32 = jnp.float32
BF16 = jnp.bfloat16
MESH = pl.DeviceIdType.MESH

D_MODEL = 1024
N_META = 16
CHUNK = 64
PAD_ROWS = CHUNK - N_META
HEADS = 4
D_NOPE = 128
D_ROPE = 64
D_QK = D_NOPE + D_ROPE
D_QKP = 256
D_V = 128
KV_RANK = 256
Q_RANK = 384
MLA_W = HEADS * D_V
LRU_W = 512
LRU_TILE = 128
N_LRU_TILES = LRU_W // LRU_TILE
CONV_K = 4
C_RGLRU = 8.0
ROPE_THETA = 10000.0
D_FF = 2816
N_SHARD = 4
EPS = 1e-6
NEG_INF = -1e30
Z_KR = Q_RANK + KV_RANK
Z_MLA = Z_KR + 128
Z_U = Z_MLA
Z_G = Z_U + LRU_W
Z_W = Z_G + LRU_W
IN_WIDTH = Q_RANK + KV_RANK + D_ROPE + 2 * LRU_W

ADAM_LR = 0.001
ADAM_B1 = 0.9
ADAM_B2 = 0.999
ADAM_EPS = 1e-08
ADAM_WD = 0.01
ADAM_STEP = 10

VMEM_LIMIT_BYTES = 56 * 1024 * 1024
SMALL_ROWS = 104
SMALL_ADAM_ROWS = 80


def _params(sem):
    return pltpu.CompilerParams(dimension_semantics=sem, vmem_limit_bytes=VMEM_LIMIT_BYTES)


def _row_tile(rows, target):
    best = 16
    for t in range(16, min(rows, target) + 1, 16):
        if rows % t == 0:
            best = t
    return best


def _col_tile(cols, target):
    best = cols
    for t in range(128, min(cols, target) + 1, 128):
        if cols % t == 0:
            best = t
    return best


def _dot(a, b):
    return jnp.dot(a, b, preferred_element_type=F32)


def _dot_nt(a, b):
    return lax.dot_general(a, b, (((1,), (1,)), ((), ())), preferred_element_type=F32)


def _dot_tn(a, b):
    return lax.dot_general(a, b, (((0,), (0,)), ((), ())), preferred_element_type=F32)


def _rms(x, n):
    return lax.rsqrt(jnp.sum(x * x, axis=-1, keepdims=True) * (1.0 / n) + EPS)


def _rms_bwd(dn, nrm, r, n):
    return r * (dn - nrm * (jnp.sum(dn * nrm, axis=-1, keepdims=True) * (1.0 / n)))


def _gelu(x):
    k = math.sqrt(2.0 / math.pi)
    t = jnp.tanh(k * (x + 0.044715 * x * x * x))
    return 0.5 * x * (1.0 + t), t


def _gelu_grad(x, t):
    k = math.sqrt(2.0 / math.pi)
    return 0.5 * (1.0 + t) + 0.5 * x * (1.0 - t * t) * k * (1.0 + 3.0 * 0.044715 * x * x)


def _neg_expm1(x):
    series = -x * (1.0 + x * (0.5 + x * (1.0 / 6 + x * (1.0 / 24 + x * (1.0 / 120 + x * (1.0 / 720))))))
    return jnp.where(x > -0.25, series, 1.0 - jnp.exp(x))


def _softplus_neg(lam):
    e = jnp.exp(-jnp.abs(lam))
    log1p = jnp.where(e < 0.01, e * (1.0 - e * (0.5 - e * (1.0 / 3 - e * 0.25))), jnp.log(1.0 + e))
    return jnp.maximum(-lam, 0.0) + log1p


def _rope(t, c, s1, s2):
    return t * c + pltpu.roll(t, 96, 1) * s1 + pltpu.roll(t, 32, 1) * s2


def _rope_t(d, c, s1, s2):
    return d * c + pltpu.roll(d * s1, 32, 1) + pltpu.roll(d * s2, 96, 1)


def _rope_tables(lp):
    pos = (jnp.arange(lp, dtype=jnp.int32) - PAD_ROWS).astype(F32)
    inv_freq = ROPE_THETA ** (-jnp.arange(0, D_ROPE // 2, dtype=F32) / (D_ROPE // 2))
    ang = pos[:, None] * inv_freq[None, :]
    cos, sin = jnp.cos(ang), jnp.sin(ang)
    z = jnp.zeros_like(cos)
    return (jnp.concatenate([cos, cos, z, z], 1), jnp.concatenate([-sin, z, z, z], 1),
            jnp.concatenate([z, sin, z, z], 1))


def _rmsnorm_call(name, h, g, tm):
    rows, d = h.shape

    def body(h_ref, g_ref, o_ref):
        x = h_ref[...]
        o_ref[...] = (x * _rms(x, d) * g_ref[...]).astype(BF16)

    return pl.pallas_call(
        body, name=name, grid=(rows // tm,),
        in_specs=[pl.BlockSpec((tm, d), lambda i: (i, 0)), pl.BlockSpec((1, d), lambda i: (0, 0))],
        out_specs=pl.BlockSpec((tm, d), lambda i: (i, 0)),
        out_shape=jax.ShapeDtypeStruct((rows, d), BF16),
        compiler_params=_params(("parallel",)))(h, g)


def _ffn_up_call(name, u, wg, wu, tm, comm=None):
    rows, d = u.shape
    ns, fs, _ = wg.shape

    def body(u_ref, wg_ref, wu_ref, g_ref, p_ref, a_ref):
        uu = u_ref[...]
        g = _dot_nt(uu, wg_ref[0])
        p = _dot_nt(uu, wu_ref[0])
        g_ref[0] = g.astype(BF16)
        p_ref[0] = p.astype(BF16)
        a_ref[0] = (g * jax.nn.sigmoid(g) * p).astype(BF16)

    wspec = pl.BlockSpec((1, fs, d), lambda s, i: (s, 0, 0))
    ospec = pl.BlockSpec((1, tm, fs), lambda s, i: (s, i, 0))
    oshape = jax.ShapeDtypeStruct((ns, rows, fs), BF16)
    return _hosted_call(
        body, name=name, grid=(ns, rows // tm),
        in_specs=[pl.BlockSpec((tm, d), lambda s, i: (i, 0)), wspec, wspec],
        out_specs=[ospec, ospec, ospec], out_shape=[oshape, oshape, oshape],
        dims=("parallel", "parallel"), args=(u, wg, wu), comm=comm)


def _ffn_down_call(name, a, wd, h, tm):
    rows, d = h.shape
    ns, _, fs = a.shape

    def body(a_ref, wd_ref, h_ref, o_ref):
        acc = h_ref[...]
        for s in range(ns):
            acc = acc + 0.5 * _dot(a_ref[s], wd_ref[s])
        o_ref[...] = acc

    return pl.pallas_call(
        body, name=name, grid=(rows // tm,),
        in_specs=[pl.BlockSpec((ns, tm, fs), lambda i: (0, i, 0)),
                  pl.BlockSpec((ns, fs, d), lambda i: (0, 0, 0)),
                  pl.BlockSpec((tm, d), lambda i: (i, 0))],
        out_specs=pl.BlockSpec((tm, d), lambda i: (i, 0)),
        out_shape=jax.ShapeDtypeStruct((rows, d), F32),
        compiler_params=_params(("parallel",)))(a, wd, h)


def _mm_call(name, a, bt, tm, out_dtype):
    rows, k = a.shape
    n = bt.shape[0]

    def body(a_ref, b_ref, o_ref):
        o_ref[...] = _dot_nt(a_ref[...], b_ref[...]).astype(out_dtype)

    return pl.pallas_call(
        body, name=name, grid=(rows // tm,),
        in_specs=[pl.BlockSpec((tm, k), lambda i: (i, 0)), pl.BlockSpec((n, k), lambda i: (0, 0))],
        out_specs=pl.BlockSpec((tm, n), lambda i: (i, 0)),
        out_shape=jax.ShapeDtypeStruct((rows, n), out_dtype),
        compiler_params=_params(("parallel",)))(a, bt)


def _mla_heads(z, gql, gkvl, wuq, wuk, wuv):
    cq = z[:, 0:Q_RANK]
    ckv = z[:, Q_RANK:Z_KR]
    kr = z[:, Z_KR:Z_MLA]
    rq = _rms(cq, Q_RANK)
    nq = cq * rq
    cqn = (nq * gql).astype(BF16)
    rkv = _rms(ckv, KV_RANK)
    nkv = ckv * rkv
    ckvn = (nkv * gkvl).astype(BF16)
    qraw = _dot_nt(cqn, wuq)
    knope = _dot(ckvn, wuk)
    v = _dot(ckvn, wuv)
    skr = jnp.sum(kr * kr, axis=-1, keepdims=True)
    heads = []
    for hd in range(HEADS):
        qh = qraw[:, hd * D_QKP:(hd + 1) * D_QKP]
        rqh = lax.rsqrt(jnp.sum(qh * qh, axis=-1, keepdims=True) * (1.0 / D_QK) + EPS)
        kn = knope[:, hd * D_NOPE:(hd + 1) * D_NOPE]
        rkh = lax.rsqrt((jnp.sum(kn * kn, axis=-1, keepdims=True) + skr) * (1.0 / D_QK) + EPS)
        heads.append((qh * rqh, rqh, kn * rkh, kr * rkh, rkh))
    return dict(rq=rq, nq=nq, cqn=cqn, rkv=rkv, nkv=nkv, ckvn=ckvn, v=v, heads=heads)


def _mla_prep_call(z, gql, gkvl, gqh, gkh, wuq, wuk, wuv, tabs, lp, tm):
    rows = z.shape[0]
    tpe = lp // tm

    def body(z_ref, gql_ref, gkvl_ref, gqh_ref, gkh_ref, wuq_ref, wuk_ref, wuv_ref, c_ref, s1_ref, s2_ref,
             q_ref, k_ref, v_ref, cqn_ref, ckvn_ref):
        m = _mla_heads(z_ref[...], gql_ref[...], gkvl_ref[...], wuq_ref[...], wuk_ref[...], wuv_ref[...])
        c, s1, s2 = c_ref[...], s1_ref[...], s2_ref[...]
        gq, gk = gqh_ref[...], gkh_ref[...]
        for hd in range(HEADS):
            qn, _, knn, krn, _ = m["heads"][hd]
            qg = qn * gq
            q_ref[hd, :, 0:D_NOPE] = qg[:, 0:D_NOPE].astype(BF16)
            q_ref[hd, :, D_NOPE:D_QKP] = _rope(qg[:, D_NOPE:D_QKP], c, s1, s2).astype(BF16)
            k_ref[hd, :, 0:D_NOPE] = (knn * gk[:, 0:D_NOPE]).astype(BF16)
            k_ref[hd, :, D_NOPE:D_QKP] = _rope(krn * gk[:, D_NOPE:D_QKP], c, s1, s2).astype(BF16)
            v_ref[hd] = m["v"][:, hd * D_V:(hd + 1) * D_V].astype(BF16)
        cqn_ref[...] = m["cqn"]
        ckvn_ref[...] = m["ckvn"]

    def const(shape):
        return pl.BlockSpec(shape, lambda i: tuple(0 for _ in shape))

    tab = pl.BlockSpec((tm, 128), lambda i: (i % tpe, 0))
    return pl.pallas_call(
        body, name="mla_prep", grid=(rows // tm,),
        in_specs=[pl.BlockSpec((tm, Z_MLA), lambda i: (i, 0)), const((1, Q_RANK)), const((1, KV_RANK)),
                  const((1, D_QKP)), const((1, D_QKP)), const((HEADS * D_QKP, Q_RANK)),
                  const((KV_RANK, HEADS * D_NOPE)), const((KV_RANK, HEADS * D_V)), tab, tab, tab],
        out_specs=[pl.BlockSpec((HEADS, tm, D_QKP), lambda i: (0, i, 0)),
                   pl.BlockSpec((HEADS, tm, D_QKP), lambda i: (0, i, 0)),
                   pl.BlockSpec((HEADS, tm, D_V), lambda i: (0, i, 0)),
                   pl.BlockSpec((tm, Q_RANK), lambda i: (i, 0)),
                   pl.BlockSpec((tm, KV_RANK), lambda i: (i, 0))],
        out_shape=[jax.ShapeDtypeStruct((HEADS, rows, D_QKP), BF16),
                   jax.ShapeDtypeStruct((HEADS, rows, D_QKP), BF16),
                   jax.ShapeDtypeStruct((HEADS, rows, D_V), BF16),
                   jax.ShapeDtypeStruct((rows, Q_RANK), BF16),
                   jax.ShapeDtypeStruct((rows, KV_RANK), BF16)],
        compiler_params=_params(("parallel",)))(z, gql, gkvl, gqh, gkh, wuq, wuk, wuv, *tabs)


def _q_block(lp):
    chunks = lp // CHUNK
    best = 1
    for g in range(1, 5):
        if chunks % g == 0:
            best = g
    return best * CHUNK


def _attn_mask(j0, qb, ext):
    qrow = j0 + lax.broadcasted_iota(jnp.int32, (qb, ext), 0)
    krow = lax.broadcasted_iota(jnp.int32, (qb, ext), 1)
    shift = CHUNK.bit_length() - 1
    return (jnp.right_shift(krow, shift) <= jnp.right_shift(qrow, shift)) & (krow >= PAD_ROWS)


def _attn_fwd_call(q, k, v, nb, lp, comm=None):
    rows = nb * lp
    qb = _q_block(lp)
    scale = 1.0 / math.sqrt(D_QK)

    def body(q_ref, k_ref, v_ref, o_ref, lse_ref):
        for j in range(lp // qb):
            j0, ext = j * qb, (j + 1) * qb
            s = _dot_nt(q_ref[0, j0:ext, :], k_ref[0, 0:ext, :]) * scale
            s = jnp.where(_attn_mask(j0, qb, ext), s, NEG_INF)
            mx = jnp.max(s, axis=-1, keepdims=True)
            p = jnp.exp(s - mx)
            l = jnp.sum(p, axis=-1, keepdims=True)
            o = _dot(p.astype(BF16), v_ref[0, 0:ext, :])
            o_ref[j0:ext, :] = o / l
            lse_ref[0, j0:ext, :] = mx + jnp.log(l)

    return _hosted_call(
        body, name="attn_fwd", grid=(nb, HEADS),
        in_specs=[pl.BlockSpec((1, lp, D_QKP), lambda b, h: (h, b, 0)),
                  pl.BlockSpec((1, lp, D_QKP), lambda b, h: (h, b, 0)),
                  pl.BlockSpec((1, lp, D_V), lambda b, h: (h, b, 0))],
        out_specs=[pl.BlockSpec((lp, D_V), lambda b, h: (b, h)),
                   pl.BlockSpec((1, lp, 1), lambda b, h: (h, b, 0))],
        out_shape=[jax.ShapeDtypeStruct((rows, MLA_W), F32),
                   jax.ShapeDtypeStruct((HEADS, rows, 1), F32)],
        dims=("parallel", "parallel"), args=(q, k, v), comm=comm)


def _attn_bwd_call(q, k, v, o, lse, do, nb, lp):
    rows = nb * lp
    qb = _q_block(lp)
    scale = 1.0 / math.sqrt(D_QK)

    def body(q_ref, k_ref, v_ref, o_ref, lse_ref, do_ref, dq_ref, dk_ref, dv_ref, dk_acc, dv_acc):
        dk_acc[...] = jnp.zeros_like(dk_acc)
        dv_acc[...] = jnp.zeros_like(dv_acc)
        for j in range(lp // qb):
            j0, ext = j * qb, (j + 1) * qb
            qj = q_ref[0, j0:ext, :]
            doj = do_ref[j0:ext, :]
            delta = jnp.sum(doj * o_ref[j0:ext, :], axis=-1, keepdims=True)
            dob = doj.astype(BF16)
            kk = k_ref[0, 0:ext, :]
            s = _dot_nt(qj, kk) * scale
            p = jnp.where(_attn_mask(j0, qb, ext), jnp.exp(s - lse_ref[0, j0:ext, :]), 0.0)
            dv_acc[0:ext, :] += _dot_tn(p.astype(BF16), dob)
            dp = _dot_nt(dob, v_ref[0, 0:ext, :])
            ds = (p * (dp - delta) * scale).astype(BF16)
            dq_ref[0, j0:ext, :] = _dot(ds, kk).astype(BF16)
            dk_acc[0:ext, :] += _dot_tn(ds, qj)
        dk_ref[0] = dk_acc[...].astype(BF16)
        dv_ref[0] = dv_acc[...].astype(BF16)

    qspec = pl.BlockSpec((1, lp, D_QKP), lambda b, h: (h, b, 0))
    vspec = pl.BlockSpec((1, lp, D_V), lambda b, h: (h, b, 0))
    ospec = pl.BlockSpec((lp, D_V), lambda b, h: (b, h))
    return pl.pallas_call(
        body, name="attn_bwd", grid=(nb, HEADS),
        in_specs=[qspec, qspec, vspec, ospec, pl.BlockSpec((1, lp, 1), lambda b, h: (h, b, 0)), ospec],
        out_specs=[qspec, qspec, vspec],
        out_shape=[jax.ShapeDtypeStruct((HEADS, rows, D_QKP), BF16),
                   jax.ShapeDtypeStruct((HEADS, rows, D_QKP), BF16),
                   jax.ShapeDtypeStruct((HEADS, rows, D_V), BF16)],
        scratch_shapes=[pltpu.VMEM((lp, D_QKP), F32), pltpu.VMEM((lp, D_V), F32)],
        compiler_params=_params(("parallel", "parallel")))(q, k, v, o, lse, do)


def _lru_gates(u, cw, cb, wa, ba, wx, bx, lam, lp):
    xc = (cw[3:4, :] * u + cw[2:3, :] * pltpu.roll(u, 1, 0) + cw[1:2, :] * pltpu.roll(u, 2, 0)
          + cw[0:1, :] * pltpu.roll(u, 3, 0) + cb)
    xcb = xc.astype(BF16)
    r = jax.nn.sigmoid(_dot(xcb, wa) + ba)
    i = jax.nn.sigmoid(_dot(xcb, wx) + bx)
    sp = _softplus_neg(lam)
    la = -C_RGLRU * r * sp
    a = jnp.exp(la)
    mult = jnp.sqrt(_neg_expm1(2.0 * la))
    row = lax.broadcasted_iota(jnp.int32, (lp, LRU_TILE), 0)
    first = row == PAD_ROWS
    valid = row >= PAD_ROWS
    mult_eff = jnp.where(first, 1.0, mult)
    return dict(xc=xc, xcb=xcb, r=r, i=i, sp=sp, la=la, a=a, mult=mult, mult_eff=mult_eff, first=first, valid=valid)


def _select_rows(rows8):
    n = rows8[0].shape[1]
    sub = lax.broadcasted_iota(jnp.int32, (8, n), 0)
    out = jnp.broadcast_to(rows8[0], (8, n))
    for j in range(1, 8):
        out = jnp.where(sub == j, jnp.broadcast_to(rows8[j], (8, n)), out)
    return out


def _lru_specs(lp):
    seq = lambda col0: pl.BlockSpec((lp, LRU_TILE), lambda t, b: (b, col0 + t))
    cw = pl.BlockSpec((1, CONV_K, LRU_TILE), lambda t, b: (t, 0, 0))
    vec = pl.BlockSpec((1, LRU_TILE), lambda t, b: (0, t))
    mat = pl.BlockSpec((1, LRU_TILE, LRU_TILE), lambda t, b: (t, 0, 0))
    return seq, cw, vec, mat


def _lru_fwd_call(z, cw, cb, wa, ba, wx, bx, lam, nb, lp, comm=None):
    rows = nb * lp
    seq, cwspec, vec, mat = _lru_specs(lp)

    def body(u_ref, g_ref, cw_ref, cb_ref, wa_ref, ba_ref, wx_ref, bx_ref, lam_ref, y_ref, hs_ref, hp_ref, a_s, b_s):
        m = _lru_gates(u_ref[...], cw_ref[0], cb_ref[...], wa_ref[0], ba_ref[...], wx_ref[0], bx_ref[...],
                       lam_ref[...], lp)
        a_s[...] = jnp.where(m["valid"], m["a"], 0.0)
        b_s[...] = jnp.where(m["valid"], m["mult_eff"] * (m["i"] * m["xc"]), 0.0)

        def group(gi, h):
            r0 = pl.multiple_of(gi * 8, 8)
            a8 = a_s[pl.ds(r0, 8), :]
            b8 = b_s[pl.ds(r0, 8), :]
            prev, cur = [], []
            for j in range(8):
                prev.append(h)
                h = a8[j:j + 1, :] * h + b8[j:j + 1, :]
                cur.append(h)
            hs_ref[pl.ds(r0, 8), :] = _select_rows(cur)
            hp_ref[pl.ds(r0, 8), :] = _select_rows(prev)
            return h

        lax.fori_loop(0, lp // 8, group, jnp.zeros((1, LRU_TILE), F32))
        gl, _ = _gelu(g_ref[...])
        y_ref[...] = hs_ref[...] * gl

    oshape = jax.ShapeDtypeStruct((rows, LRU_W), F32)
    return _hosted_call(
        body, name="lru_fwd", grid=(N_LRU_TILES, nb),
        in_specs=[seq(Z_U // LRU_TILE), seq(Z_G // LRU_TILE), cwspec, vec, mat, vec, mat, vec, vec],
        out_specs=[seq(0), seq(0), seq(0)], out_shape=[oshape, oshape, oshape],
        scratch_shapes=[pltpu.VMEM((lp, LRU_TILE), F32), pltpu.VMEM((lp, LRU_TILE), F32)],
        dims=("parallel", "parallel"), args=(z, z, cw, cb, wa, ba, wx, bx, lam), comm=comm)


def _lru_bwd_call(z, hs, hp, dy, cw, cb, wa, ba, wx, bx, lam, nb, lp, comm=None):
    rows = nb * lp
    seq, cwspec, vec, mat = _lru_specs(lp)

    def body(u_ref, g_ref, hs_ref, hp_ref, dy_ref, cw_ref, cb_ref, wa_ref, ba_ref, wx_ref, bx_ref, lam_ref,
             du_ref, dg_ref, dcw_ref, dcb_ref, dwa_ref, dba_ref, dwx_ref, dbx_ref, dlam_ref, an_s, d_s):
        b_idx = pl.program_id(1)
        u = u_ref[...]
        cw = cw_ref[0]
        wa, wx = wa_ref[0], wx_ref[0]
        lam = lam_ref[...]
        m = _lru_gates(u, cw, cb_ref[...], wa, ba_ref[...], wx, bx_ref[...], lam, lp)
        gate = g_ref[...]
        gl, th = _gelu(gate)
        dy = dy_ref[...]
        dg_ref[...] = (dy * hs_ref[...] * _gelu_grad(gate, th)).astype(BF16)
        a_eff = jnp.where(m["valid"], m["a"], 0.0)
        an_s[...] = pltpu.roll(a_eff, lp - 1, 0)
        d_s[...] = dy * gl

        def group(gi, carry):
            r0 = pl.multiple_of((lp // 8 - 1 - gi) * 8, 8)
            a8 = an_s[pl.ds(r0, 8), :]
            d8 = d_s[pl.ds(r0, 8), :]
            cur = [None] * 8
            for j in range(7, -1, -1):
                carry = d8[j:j + 1, :] + a8[j:j + 1, :] * carry
                cur[j] = carry
            d_s[pl.ds(r0, 8), :] = _select_rows(cur)
            return carry

        lax.fori_loop(0, lp // 8, group, jnp.zeros((1, LRU_TILE), F32))
        ds = d_s[...]
        xc, r, i = m["xc"], m["r"], m["i"]
        da = ds * hp_ref[...]
        db = jnp.where(m["valid"], ds, 0.0)
        di = db * m["mult_eff"] * xc
        dxc = db * m["mult_eff"] * i
        live = m["valid"] & jnp.logical_not(m["first"])
        e2 = jnp.exp(2.0 * m["la"])
        dm = jnp.where(live, db * i * xc, 0.0)
        dla = da * m["a"] + jnp.where(live, dm * (-e2 / m["mult"]), 0.0)
        dr = dla * (-C_RGLRU * m["sp"])
        dsp = jnp.sum(dla * (-C_RGLRU * r), axis=0, keepdims=True)
        dpr = (dr * r * (1.0 - r))
        dpi = (di * i * (1.0 - i))
        dprb, dpib = dpr.astype(BF16), dpi.astype(BF16)
        dxc = dxc + _dot_nt(dprb, wa) + _dot_nt(dpib, wx)
        du = (cw[3:4, :] * dxc + cw[2:3, :] * pltpu.roll(dxc, lp - 1, 0) + cw[1:2, :] * pltpu.roll(dxc, lp - 2, 0)
              + cw[0:1, :] * pltpu.roll(dxc, lp - 3, 0))
        du_ref[...] = jnp.where(m["valid"], du, 0.0).astype(BF16)
        tap = lax.broadcasted_iota(jnp.int32, (CONV_K, LRU_TILE), 0)
        dcw = jnp.zeros((CONV_K, LRU_TILE), F32)
        for kk in range(CONV_K):
            shifted = u if kk == CONV_K - 1 else pltpu.roll(u, CONV_K - 1 - kk, 0)
            dcw = jnp.where(tap == kk, jnp.sum(dxc * shifted, axis=0, keepdims=True), dcw)
        parts = [(dcw_ref, dcw[None]), (dcb_ref, jnp.sum(dxc, axis=0, keepdims=True)[None]),
                 (dwa_ref, _dot_tn(m["xcb"], dprb)[None]), (dba_ref, jnp.sum(dpr, axis=0, keepdims=True)[None]),
                 (dwx_ref, _dot_tn(m["xcb"], dpib)[None]), (dbx_ref, jnp.sum(dpi, axis=0, keepdims=True)[None]),
                 (dlam_ref, (dsp * (-jax.nn.sigmoid(-lam)))[None])]

        @pl.when(b_idx == 0)
        def _():
            for ref, val in parts:
                ref[...] = val

        @pl.when(b_idx != 0)
        def _():
            for ref, val in parts:
                ref[...] += val

    bshape = jax.ShapeDtypeStruct((rows, LRU_W), BF16)
    vec3 = pl.BlockSpec((1, 1, LRU_TILE), lambda t, b: (t, 0, 0))
    vshape = jax.ShapeDtypeStruct((N_LRU_TILES, 1, LRU_TILE), F32)
    mshape = jax.ShapeDtypeStruct((N_LRU_TILES, LRU_TILE, LRU_TILE), F32)
    return _hosted_call(
        body, name="lru_bwd", grid=(N_LRU_TILES, nb),
        in_specs=[seq(Z_U // LRU_TILE), seq(Z_G // LRU_TILE), seq(0), seq(0), seq(0), cwspec, vec, mat, vec, mat,
                  vec, vec],
        out_specs=[seq(0), seq(0), cwspec, vec3, mat, vec3, mat, vec3, vec3],
        out_shape=[bshape, bshape, jax.ShapeDtypeStruct((N_LRU_TILES, CONV_K, LRU_TILE), F32), vshape, mshape,
                   vshape, mshape, vshape, vshape],
        scratch_shapes=[pltpu.VMEM((lp, LRU_TILE), F32), pltpu.VMEM((lp, LRU_TILE), F32)],
        dims=("parallel", "arbitrary"), args=(z, z, hs, hp, dy, cw, cb, wa, ba, wx, bx, lam), comm=comm)


def _mix_out_call(ya, yl, ga, gl, wout, h, tm):
    rows, d = h.shape

    def body(ya_ref, yl_ref, ga_ref, gl_ref, w_ref, h_ref, y_ref, o_ref):
        a = ya_ref[...]
        l = yl_ref[...]
        an = (a * _rms(a, MLA_W) * ga_ref[...]).astype(BF16)
        ln = (l * _rms(l, LRU_W) * gl_ref[...]).astype(BF16)
        y_ref[:, 0:MLA_W] = an
        y_ref[:, MLA_W:MLA_W + LRU_W] = ln
        o_ref[...] = h_ref[...] + _dot(an, w_ref[0:MLA_W, :]) + _dot(ln, w_ref[MLA_W:MLA_W + LRU_W, :])

    half = pl.BlockSpec((tm, MLA_W), lambda i: (i, 0))
    g = pl.BlockSpec((1, MLA_W), lambda i: (0, 0))
    full = pl.BlockSpec((tm, d), lambda i: (i, 0))
    return pl.pallas_call(
        body, name="mix_out", grid=(rows // tm,),
        in_specs=[half, half, g, g, pl.BlockSpec((MLA_W + LRU_W, d), lambda i: (0, 0)), full],
        out_specs=[full, full],
        out_shape=[jax.ShapeDtypeStruct((rows, MLA_W + LRU_W), BF16), jax.ShapeDtypeStruct((rows, d), F32)],
        compiler_params=_params(("parallel",)))(ya, yl, ga, gl, wout, h)


def _mix_out_bwd_call(dhb, wout, ya, yl, ga, gl, tm):
    rows = ya.shape[0]
    d = dhb.shape[1]

    def body(dh_ref, w_ref, ya_ref, yl_ref, ga_ref, gl_ref, dya_ref, dyl_ref, dga_ref, dgl_ref):
        dy = _dot_nt(dh_ref[...], w_ref[...])
        outs = []
        for val, g_ref, lo, out_ref in ((ya_ref[...], ga_ref, 0, dya_ref), (yl_ref[...], gl_ref, MLA_W, dyl_ref)):
            r = _rms(val, MLA_W)
            n = val * r
            dyn = dy[:, lo:lo + MLA_W]
            out_ref[...] = _rms_bwd(dyn * g_ref[...], n, r, MLA_W)
            outs.append(jnp.sum(dyn * n, axis=0, keepdims=True))

        @pl.when(pl.program_id(0) == 0)
        def _():
            dga_ref[...] = outs[0]
            dgl_ref[...] = outs[1]

        @pl.when(pl.program_id(0) != 0)
        def _():
            dga_ref[...] += outs[0]
            dgl_ref[...] += outs[1]

    half = pl.BlockSpec((tm, MLA_W), lambda i: (i, 0))
    g = pl.BlockSpec((1, MLA_W), lambda i: (0, 0))
    return pl.pallas_call(
        body, name="mix_out_bwd", grid=(rows // tm,),
        in_specs=[pl.BlockSpec((tm, d), lambda i: (i, 0)), pl.BlockSpec((MLA_W + LRU_W, d), lambda i: (0, 0)),
                  half, half, g, g],
        out_specs=[half, half, g, g],
        out_shape=[jax.ShapeDtypeStruct((rows, MLA_W), F32), jax.ShapeDtypeStruct((rows, LRU_W), F32),
                   jax.ShapeDtypeStruct((1, MLA_W), F32), jax.ShapeDtypeStruct((1, LRU_W), F32)],
        compiler_params=_params(("arbitrary",)))(dhb, wout, ya, yl, ga, gl)


def _final_call(h, g, target, lp, tm):
    rows, d = h.shape
    tpe = lp // tm

    def body(h_ref, g_ref, t_ref, dh_ref, dhb_ref, dg_ref, loss_ref):
        i = pl.program_id(0)
        x = h_ref[...]
        g = g_ref[...]
        r = _rms(x, d)
        n = x * r
        row = (i % tpe) * tm + lax.broadcasted_iota(jnp.int32, (tm, 1), 0)
        err = jnp.where(row >= CHUNK, n * g - t_ref[...], 0.0)
        dout = err * (1.0 / d)
        dh = _rms_bwd(dout * g, n, r, d)
        dh_ref[...] = dh
        dhb_ref[...] = dh.astype(BF16)
        dg = jnp.sum(dout * n, axis=0, keepdims=True)
        part = jnp.sum(jnp.sum(err * err, axis=1, keepdims=True), axis=0, keepdims=True) * (0.5 / d)
        loss = jnp.broadcast_to(part, (1, 128))

        @pl.when(i == 0)
        def _():
            dg_ref[...] = dg
            loss_ref[...] = loss

        @pl.when(i != 0)
        def _():
            dg_ref[...] += dg
            loss_ref[...] += loss

    full = pl.BlockSpec((tm, d), lambda i: (i, 0))
    return pl.pallas_call(
        body, name="final_loss", grid=(rows // tm,),
        in_specs=[full, pl.BlockSpec((1, d), lambda i: (0, 0)), full],
        out_specs=[full, full, pl.BlockSpec((1, d), lambda i: (0, 0)), pl.BlockSpec((1, 128), lambda i: (0, 0))],
        out_shape=[jax.ShapeDtypeStruct((rows, d), F32), jax.ShapeDtypeStruct((rows, d), BF16),
                   jax.ShapeDtypeStruct((1, d), F32), jax.ShapeDtypeStruct((1, 128), F32)],
        compiler_params=_params(("arbitrary",)))(h, g, target)


def _ffn_dact_call(name, dhb, wd, gate, up, tm, comm=None):
    rows, d = dhb.shape
    ns, fs, _ = wd.shape

    def body(dh_ref, wd_ref, g_ref, p_ref, dg_ref, dp_ref):
        da = 0.5 * _dot_nt(dh_ref[...], wd_ref[0])
        g = g_ref[0].astype(F32)
        p = p_ref[0].astype(F32)
        sg = jax.nn.sigmoid(g)
        dg_ref[0] = (da * p * sg * (1.0 + g * (1.0 - sg))).astype(BF16)
        dp_ref[0] = (da * g * sg).astype(BF16)

    aspec = pl.BlockSpec((1, tm, fs), lambda s, i: (s, i, 0))
    oshape = jax.ShapeDtypeStruct((ns, rows, fs), BF16)
    return _hosted_call(
        body, name=name, grid=(ns, rows // tm),
        in_specs=[pl.BlockSpec((tm, d), lambda s, i: (i, 0)), pl.BlockSpec((1, fs, d), lambda s, i: (s, 0, 0)),
                  aspec, aspec],
        out_specs=[aspec, aspec], out_shape=[oshape, oshape],
        dims=("parallel", "parallel"), args=(dhb, wd, gate, up), comm=comm)


def _norm_in_bwd_call(name, pieces, h, g, dres, tm, comm=None):
    rows, d = h.shape
    npc = len(pieces)

    def body(*refs):
        d_refs = refs[0:2 * npc:2]
        w_refs = refs[1:2 * npc:2]
        h_ref, g_ref, dres_ref, dh_ref, dhb_ref, dg_ref = refs[2 * npc:]
        du = jnp.zeros((tm, d), F32)
        for d_ref, w_ref in zip(d_refs, w_refs):
            if len(d_ref.shape) == 3:
                for s in range(d_ref.shape[0]):
                    du = du + _dot(d_ref[s], w_ref[s])
            else:
                du = du + _dot(d_ref[...], w_ref[...])
        x = h_ref[...]
        r = _rms(x, d)
        n = x * r
        dh = dres_ref[...] + _rms_bwd(du * g_ref[...], n, r, d)
        dh_ref[...] = dh
        dhb_ref[...] = dh.astype(BF16)
        dg = jnp.sum(du * n, axis=0, keepdims=True)

        @pl.when(pl.program_id(0) == 0)
        def _():
            dg_ref[...] = dg

        @pl.when(pl.program_id(0) != 0)
        def _():
            dg_ref[...] += dg

    in_specs, args = [], []
    for dd, w in pieces:
        if dd.ndim == 3:
            in_specs.append(pl.BlockSpec((dd.shape[0], tm, dd.shape[2]), lambda i: (0, i, 0)))
            in_specs.append(pl.BlockSpec(w.shape, lambda i: (0, 0, 0)))
        else:
            in_specs.append(pl.BlockSpec((tm, dd.shape[1]), lambda i: (i, 0)))
            in_specs.append(pl.BlockSpec(w.shape, lambda i: (0, 0)))
        args += [dd, w]
    full = pl.BlockSpec((tm, d), lambda i: (i, 0))
    gspec = pl.BlockSpec((1, d), lambda i: (0, 0))
    return _hosted_call(
        body, name=name, grid=(rows // tm,),
        in_specs=in_specs + [full, gspec, full],
        out_specs=[full, full, gspec],
        out_shape=[jax.ShapeDtypeStruct((rows, d), F32), jax.ShapeDtypeStruct((rows, d), BF16),
                   jax.ShapeDtypeStruct((1, d), F32)],
        args=(*args, h, g, dres), comm=comm)


def _wgrad_call(name, a, b, scale=1.0):
    a3, b3 = a.ndim == 3, b.ndim == 3
    ns = a.shape[0] if a3 else (b.shape[0] if b3 else 1)
    rows, m = a.shape[-2:]
    n = b.shape[-1]
    tmm = m if a3 else _col_tile(m, 256)

    def body(a_ref, b_ref, o_ref):
        av = a_ref[0] if a3 else a_ref[...]
        bv = b_ref[0] if b3 else b_ref[...]
        res = _dot_tn(av, bv)
        if scale != 1.0:
            res = res * scale
        if a3 or b3:
            o_ref[0] = res
        else:
            o_ref[...] = res

    aspec = (pl.BlockSpec((1, rows, tmm), lambda s, j: (s, 0, j)) if a3
             else pl.BlockSpec((rows, tmm), lambda s, j: (0, j)))
    bspec = (pl.BlockSpec((1, rows, n), lambda s, j: (s, 0, 0)) if b3
             else pl.BlockSpec((rows, n), lambda s, j: (0, 0)))
    if a3 or b3:
        ospec = pl.BlockSpec((1, tmm, n), lambda s, j: (s, j, 0))
        oshape = jax.ShapeDtypeStruct((ns, m, n), F32)
    else:
        ospec = pl.BlockSpec((tmm, n), lambda s, j: (j, 0))
        oshape = jax.ShapeDtypeStruct((m, n), F32)
    return pl.pallas_call(
        body, name=name, grid=(ns, m // tmm), in_specs=[aspec, bspec], out_specs=ospec, out_shape=oshape,
        compiler_params=_params(("parallel", "parallel")))(a, b)


def _mla_prep_bwd_call(z, dq, dk, dv, gql, gkvl, gqh, gkh, wuq, wuk, wuv, tabs, lp, tm):
    rows = z.shape[0]
    tpe = lp // tm

    def body(z_ref, dq_ref, dk_ref, dv_ref, gql_ref, gkvl_ref, gqh_ref, gkh_ref, wuq_ref, wuk_ref, wuv_ref,
             c_ref, s1_ref, s2_ref, dz_ref, dqp_ref, dkn_ref, dvv_ref, dgql_ref, dgkvl_ref, dgqh_ref, dgkh_ref):
        gql, gkvl = gql_ref[...], gkvl_ref[...]
        gq, gk = gqh_ref[...], gkh_ref[...]
        wuq, wuk, wuv = wuq_ref[...], wuk_ref[...], wuv_ref[...]
        m = _mla_heads(z_ref[...], gql, gkvl, wuq, wuk, wuv)
        c, s1, s2 = c_ref[...], s1_ref[...], s2_ref[...]
        dgq = jnp.zeros((1, D_QKP), F32)
        dgk = jnp.zeros((1, D_QKP), F32)
        dkr = jnp.zeros((tm, D_QKP - D_NOPE), F32)
        for hd in range(HEADS):
            qn, rqh, knn, krn, rkh = m["heads"][hd]
            dqg = jnp.concatenate([dq_ref[hd, :, 0:D_NOPE].astype(F32),
                                   _rope_t(dq_ref[hd, :, D_NOPE:D_QKP].astype(F32), c, s1, s2)], axis=1)
            dgq = dgq + jnp.sum(dqg * qn, axis=0, keepdims=True)
            dqn = dqg * gq
            dqr = rqh * (dqn - qn * (jnp.sum(dqn * qn, axis=-1, keepdims=True) * (1.0 / D_QK)))
            dqp_ref[:, hd * D_QKP:(hd + 1) * D_QKP] = dqr.astype(BF16)
            kn_full = jnp.concatenate([knn, krn], axis=1)
            dkg = jnp.concatenate([dk_ref[hd, :, 0:D_NOPE].astype(F32),
                                   _rope_t(dk_ref[hd, :, D_NOPE:D_QKP].astype(F32), c, s1, s2)], axis=1)
            dgk = dgk + jnp.sum(dkg * kn_full, axis=0, keepdims=True)
            dkn = dkg * gk
            dkraw = rkh * (dkn - kn_full * (jnp.sum(dkn * kn_full, axis=-1, keepdims=True) * (1.0 / D_QK)))
            dkn_ref[:, hd * D_NOPE:(hd + 1) * D_NOPE] = dkraw[:, 0:D_NOPE].astype(BF16)
            dkr = dkr + dkraw[:, D_NOPE:D_QKP]
            dvv_ref[:, hd * D_V:(hd + 1) * D_V] = dv_ref[hd]
        dcqn = _dot(dqp_ref[...], wuq)
        dckvn = _dot_nt(dkn_ref[...], wuk) + _dot_nt(dvv_ref[...], wuv)
        dz_ref[:, 0:Q_RANK] = _rms_bwd(dcqn * gql, m["nq"], m["rq"], Q_RANK).astype(BF16)
        dz_ref[:, Q_RANK:Z_KR] = _rms_bwd(dckvn * gkvl, m["nkv"], m["rkv"], KV_RANK).astype(BF16)
        dz_ref[:, Z_KR:Z_MLA] = dkr.astype(BF16)
        parts = [(dgql_ref, jnp.sum(dcqn * m["nq"], axis=0, keepdims=True)),
                 (dgkvl_ref, jnp.sum(dckvn * m["nkv"], axis=0, keepdims=True)), (dgqh_ref, dgq), (dgkh_ref, dgk)]

        @pl.when(pl.program_id(0) == 0)
        def _():
            for ref, val in parts:
                ref[...] = val

        @pl.when(pl.program_id(0) != 0)
        def _():
            for ref, val in parts:
                ref[...] += val

    def const(shape):
        return pl.BlockSpec(shape, lambda i: tuple(0 for _ in shape))

    tab = pl.BlockSpec((tm, 128), lambda i: (i % tpe, 0))
    hq = pl.BlockSpec((HEADS, tm, D_QKP), lambda i: (0, i, 0))
    hv = pl.BlockSpec((HEADS, tm, D_V), lambda i: (0, i, 0))

    def rowspec(n):
        return pl.BlockSpec((tm, n), lambda i: (i, 0))

    return pl.pallas_call(
        body, name="mla_prep_bwd", grid=(rows // tm,),
        in_specs=[rowspec(Z_MLA), hq, hq, hv, const((1, Q_RANK)), const((1, KV_RANK)), const((1, D_QKP)),
                  const((1, D_QKP)), const((HEADS * D_QKP, Q_RANK)), const((KV_RANK, HEADS * D_NOPE)),
                  const((KV_RANK, HEADS * D_V)), tab, tab, tab],
        out_specs=[rowspec(Z_MLA), rowspec(HEADS * D_QKP), rowspec(HEADS * D_NOPE), rowspec(HEADS * D_V),
                   const((1, Q_RANK)), const((1, KV_RANK)), const((1, D_QKP)), const((1, D_QKP))],
        out_shape=[jax.ShapeDtypeStruct((rows, Z_MLA), BF16), jax.ShapeDtypeStruct((rows, HEADS * D_QKP), BF16),
                   jax.ShapeDtypeStruct((rows, HEADS * D_NOPE), BF16), jax.ShapeDtypeStruct((rows, HEADS * D_V), BF16),
                   jax.ShapeDtypeStruct((1, Q_RANK), F32), jax.ShapeDtypeStruct((1, KV_RANK), F32),
                   jax.ShapeDtypeStruct((1, D_QKP), F32), jax.ShapeDtypeStruct((1, D_QKP), F32)],
        compiler_params=_params(("arbitrary",)))(z, dq, dk, dv, gql, gkvl, gqh, gkh, wuq, wuk, wuv, *tabs)


def _local_step(h0, target, w, nb, lp, sched=None):
    tm = _row_tile(lp, 1024)
    te = _row_tile(lp, 512)
    tabs = _rope_tables(lp)
    g = {}
    if sched is None:
        host = lambda stage: None
    else:
        sched.g = g
        host = sched.host

    def ffn_fwd(tag, h):
        u = _rmsnorm_call(tag + "_norm", h, w[tag + "_norm"], te)
        gate, up, act = _ffn_up_call(tag + "_up", u, w[tag + "_w_gate"], w[tag + "_w_up"], tm, host(tag + "_up"))
        return _ffn_down_call(tag + "_down", act, w[tag + "_w_down"], h, te), (u, gate, up, act)

    def ffn_bwd(tag, h, saved, dh, dhb):
        u, gate, up, act = saved
        dgate, dup = _ffn_dact_call(tag + "_dact", dhb, w[tag + "_w_down"], gate, up, tm, host(tag + "_dact"))
        g[tag + "_w_down"] = _wgrad_call(tag + "_dwd", act, dhb, 0.5)
        g[tag + "_w_gate"] = _wgrad_call(tag + "_dwg", dgate, u)
        g[tag + "_w_up"] = _wgrad_call(tag + "_dwu", dup, u)
        dh_in, dhb_in, g[tag + "_norm"] = _norm_in_bwd_call(
            tag + "_din", [(dgate, w[tag + "_w_gate"]), (dup, w[tag + "_w_up"])], h, w[tag + "_norm"], dh, te,
            host(tag + "_din"))
        return dh_in, dhb_in

    h1, s1 = ffn_fwd("ffn1", h0)
    un = _rmsnorm_call("mix_norm", h1, w["mix_norm"], te)
    z = _mm_call("mix_in", un, w["w_in"], tm, F32)
    mla_w = (w["q_latent_norm"], w["kv_latent_norm"], w["q_head_norm"], w["k_head_norm"], w["w_uq"], w["w_uk"],
             w["w_uv"])
    q, k, v, cqn, ckvn = _mla_prep_call(z, *mla_w, tabs, lp, te)
    o, lse = _attn_fwd_call(q, k, v, nb, lp, host("attn_fwd"))
    lru_w = (w["conv_w"], w["conv_b"], w["gate_a_w"], w["gate_a_b"], w["gate_x_w"], w["gate_x_b"], w["lru_lambda"])
    yl, hs, hp = _lru_fwd_call(z, *lru_w, nb, lp, host("lru_fwd"))
    y, h2 = _mix_out_call(o, yl, w["attn_out_norm"], w["lru_out_norm"], w["w_out"], h1, te)
    h3, s2 = ffn_fwd("ffn2", h2)
    dh3, dh3b, g["final_norm"], loss = _final_call(h3, w["final_norm"], target, lp, te)

    dh2, dh2b = ffn_bwd("ffn2", h2, s2, dh3, dh3b)
    g["w_out"] = _wgrad_call("dw_out", y, dh2b)
    dya, dyl, g["attn_out_norm"], g["lru_out_norm"] = _mix_out_bwd_call(
        dh2b, w["w_out"], o, yl, w["attn_out_norm"], w["lru_out_norm"], te)
    dq, dk, dv = _attn_bwd_call(q, k, v, o, lse, dya, nb, lp)
    (dz_mla, dqp, dkn, dvv, g["q_latent_norm"], g["kv_latent_norm"], g["q_head_norm"],
     g["k_head_norm"]) = _mla_prep_bwd_call(z, dq, dk, dv, *mla_w, tabs, lp, te)
    g["w_uq"] = _wgrad_call("dw_uq", dqp, cqn)
    g["w_uk"] = _wgrad_call("dw_uk", ckvn, dkn)
    g["w_uv"] = _wgrad_call("dw_uv", ckvn, dvv)
    (du, dgt, g["conv_w"], g["conv_b"], g["gate_a_w"], g["gate_a_b"], g["gate_x_w"], g["gate_x_b"],
     g["lru_lambda"]) = _lru_bwd_call(z, hs, hp, dyl, *lru_w, nb, lp, host("lru_bwd"))
    win = w["w_in"]
    g["w_in"] = jnp.concatenate(
        [_wgrad_call("dw_in_mla", dz_mla, un), _wgrad_call("dw_in_u", du, un), _wgrad_call("dw_in_g", dgt, un)],
        axis=0)
    dh1, dh1b, g["mix_norm"] = _norm_in_bwd_call(
        "mix_din", [(dz_mla, win[0:Z_MLA]), (du, win[Z_U:Z_G]), (dgt, win[Z_G:Z_W])], h1, w["mix_norm"], dh2, te,
        host("mix_din"))
    dh0, _ = ffn_bwd("ffn1", h0, s1, dh1, dh1b)
    return loss, dh0, g


def _place():
    x, y, c = lax.axis_index("x"), lax.axis_index("y"), lax.axis_index("c")
    return x, y, c, [(1 - x, y), (x, 1 - y), (1 - x, 1 - y)]


def _any_specs(n):
    return [pl.BlockSpec(memory_space=pl.ANY)] * n


def _remote(src, dst, sems, k, dev):
    send_sems, recv_sems = sems
    return pltpu.make_async_remote_copy(src_ref=src, dst_ref=dst, send_sem=send_sems.at[k], recv_sem=recv_sems.at[k],
                                        device_id=dev, device_id_type=MESH)


EW_VMEM_BYTES = 24 * 1024 * 1024


def _fit_rows(rows, cols, blocks):
    return _row_tile(rows, max(16, int(EW_VMEM_BYTES // (8 * blocks)) // cols))


class _Geom:
    def __init__(self, n0, n1, blocks=1.0):
        self.n0, self.n1 = n0, n1
        self.axis = 0 if n0 % 32 == 0 else 1
        self.h0, self.h1 = (n0 // 2, n1) if self.axis == 0 else (n0, n1 // 2)
        self.tr = _fit_rows(self.h0, self.h1, blocks)
        self.nblk = self.h0 // self.tr

    def half_ref(self, ref, lead, idx):
        if self.axis == 0:
            return ref.at[(*lead, pl.ds(idx * self.h0, self.h0))]
        return ref.at[(*lead, slice(None), pl.ds(idx * self.h1, self.h1))]

    def half_block(self, lead, i, idx):
        return (*lead, idx * self.nblk + i, 0) if self.axis == 0 else (*lead, i, idx)


class _Comm:
    def __init__(self, ins, out_shapes, aliases, n_sems, start, finish, deliver):
        self.ins, self.out_shapes, self.aliases, self.n_sems = list(ins), list(out_shapes), dict(aliases), n_sems
        self.start, self.finish, self.deliver = start, finish, deliver

    def scratch(self):
        return [pltpu.SemaphoreType.DMA((self.n_sems,)), pltpu.SemaphoreType.DMA((self.n_sems,))]


def _comm_call(name, comm):
    n_in = len(comm.ins)

    def body(*refs):
        ins, outs, sems = refs[:n_in], refs[n_in:-2], refs[-2:]
        comm.start(ins, outs, sems)
        comm.finish(ins, outs, sems)

    res = pl.pallas_call(
        body, name=name, out_shape=comm.out_shapes, in_specs=_any_specs(n_in),
        out_specs=_any_specs(len(comm.out_shapes)), input_output_aliases=comm.aliases,
        scratch_shapes=comm.scratch())(*comm.ins)
    return comm.deliver(list(res))


def _hosted_call(body, *, name, grid, in_specs, out_specs, out_shape, args, scratch_shapes=(), dims=None, comm=None):
    in_specs, out_specs, out_shape = list(in_specs), list(out_specs), list(out_shape)
    if comm is None:
        return pl.pallas_call(
            body, name=name, grid=grid, in_specs=in_specs, out_specs=out_specs, out_shape=out_shape,
            scratch_shapes=list(scratch_shapes),
            compiler_params=_params(dims or ("arbitrary",) * len(grid)))(*args)
    n_in, n_out, n_ci, n_co = len(in_specs), len(out_specs), len(comm.ins), len(comm.out_shapes)

    def wrapped(*refs):
        ins, cins = refs[:n_in], refs[n_in:n_in + n_ci]
        outs = refs[n_in + n_ci:n_in + n_ci + n_out]
        couts = refs[n_in + n_ci + n_out:n_in + n_ci + n_out + n_co]
        scratch, sems = refs[n_in + n_ci + n_out + n_co:-2], refs[-2:]
        first = functools.reduce(jnp.logical_and, [pl.program_id(k) == 0 for k in range(len(grid))])
        last = functools.reduce(jnp.logical_and, [pl.program_id(k) == grid[k] - 1 for k in range(len(grid))])

        @pl.when(first)
        def _():
            comm.start(cins, couts, sems)

        body(*ins, *outs, *scratch)

        @pl.when(last)
        def _():
            comm.finish(cins, couts, sems)

    res = pl.pallas_call(
        wrapped, name=name, grid=grid, in_specs=in_specs + _any_specs(n_ci), out_specs=out_specs + _any_specs(n_co),
        out_shape=out_shape + comm.out_shapes,
        input_output_aliases={n_in + i: n_out + o for i, o in comm.aliases.items()},
        scratch_shapes=list(scratch_shapes) + comm.scratch(),
        compiler_params=_params(("arbitrary",) * len(grid)))(*args, *comm.ins)
    comm.deliver(list(res[n_out:]))
    return list(res[:n_out])


def _gather_comm(bufs, deliver):
    n = len(bufs)
    geoms = [_Geom(*b.shape[1:]) for b in bufs]

    def first(outs, sems):
        x, y, c, chips = _place()
        cps = []
        for a in range(n):
            mine = geoms[a].half_ref(outs[a], (2 * x + y,), c)
            cps += [_remote(mine, mine, sems, 6 * a + j, (cx, cy, c)) for j, (cx, cy) in enumerate(chips)]
        return cps

    def start(ins, outs, sems):
        for cp in first(outs, sems):
            cp.start()

    def finish(ins, outs, sems):
        x, y, c, chips = _place()
        sib = (x, y, 1 - c)
        passed = []
        for a in range(n):
            for j, (cx, cy) in enumerate(chips):
                land = geoms[a].half_ref(outs[a], (2 * cx + cy,), c)
                _remote(land, land, sems, 6 * a + j, sib).wait_recv()
                cp = _remote(land, land, sems, 6 * a + 3 + j, sib)
                cp.start()
                passed.append(cp)
        for a in range(n):
            for j, (cx, cy) in enumerate(chips):
                land = geoms[a].half_ref(outs[a], (2 * cx + cy,), 1 - c)
                _remote(land, land, sems, 6 * a + 3 + j, sib).wait_recv()
        for cp in first(outs, sems) + passed:
            cp.wait_send()

    return _Comm(bufs, [jax.ShapeDtypeStruct(b.shape, b.dtype) for b in bufs], {a: a for a in range(n)}, 6 * n,
                 start, finish, deliver)


def _reduce_pair_comm(grads, deliver):
    n = len(grads)
    geoms = [_Geom(*a.shape[1:]) for a in grads]

    def copies(ins, outs, sems):
        x, y, c, _ = _place()
        return [_remote(geoms[a].half_ref(ins[a], (slice(None),), 1 - c), outs[a], sems, a, (x, y, 1 - c))
                for a in range(n)]

    def start(ins, outs, sems):
        for cp in copies(ins, outs, sems):
            cp.start()

    def finish(ins, outs, sems):
        cps = copies(ins, outs, sems)
        for cp in cps:
            cp.wait_recv()
        for cp in cps:
            cp.wait_send()

    shapes = [jax.ShapeDtypeStruct((N_SHARD, g.h0, g.h1), a.dtype) for a, g in zip(grads, geoms)]
    return _Comm(grads, shapes, {}, n, start, finish, deliver)


def _reduce_chips_comm(parts, deliver):
    n = len(parts)

    def copies(ins, outs, sems):
        x, y, c, chips = _place()
        return [_remote(ins[a].at[2 * cx + cy], outs[a].at[j], sems, 3 * a + j, (cx, cy, c))
                for a in range(n) for j, (cx, cy) in enumerate(chips)]

    def start(ins, outs, sems):
        for cp in copies(ins, outs, sems):
            cp.start()

    def finish(ins, outs, sems):
        cps = copies(ins, outs, sems)
        for cp in cps:
            cp.wait_recv()
        for cp in cps:
            cp.wait_send()

    shapes = [jax.ShapeDtypeStruct((3,) + a.shape[1:], a.dtype) for a in parts]
    return _Comm(parts, shapes, {}, 3 * n, start, finish, deliver)


def _share_pair_comm(bufs, deliver):
    n = len(bufs)
    geoms = [_Geom(*b.shape) for b in bufs]

    def copies(outs, sems):
        x, y, c, _ = _place()
        cps = []
        for a in range(n):
            mine = geoms[a].half_ref(outs[a], (), c)
            cps.append(_remote(mine, mine, sems, a, (x, y, 1 - c)))
        return cps

    def start(ins, outs, sems):
        for cp in copies(outs, sems):
            cp.start()

    def finish(ins, outs, sems):
        x, y, c, _ = _place()
        for a in range(n):
            land = geoms[a].half_ref(outs[a], (), 1 - c)
            _remote(land, land, sems, a, (x, y, 1 - c)).wait_recv()
        for cp in copies(outs, sems):
            cp.wait_send()

    return _Comm(bufs, [jax.ShapeDtypeStruct(b.shape, b.dtype) for b in bufs], {a: a for a in range(n)}, n,
                 start, finish, deliver)


def _gather_small_call(pack):
    r, d = pack.shape

    def body(in_ref, out_ref, send_sems, recv_sems):
        x, y, c, _ = _place()
        me = 4 * x + 2 * y + c
        out_ref[me] = in_ref[...]
        cps = []
        for k in range(1, 8):
            bx, by, bc = (k >> 2) & 1, (k >> 1) & 1, k & 1
            px, py, pc = x ^ bx, y ^ by, c ^ bc
            cp = pltpu.make_async_remote_copy(
                src_ref=in_ref, dst_ref=out_ref.at[me], send_sem=send_sems.at[k - 1], recv_sem=recv_sems.at[k - 1],
                device_id=(px, py, pc), device_id_type=MESH)
            cp.start()
            cps.append((cp, 4 * px + 2 * py + pc))
        for k, (cp, peer) in enumerate(cps):
            land = out_ref.at[peer]
            pltpu.make_async_remote_copy(
                src_ref=land, dst_ref=land, send_sem=send_sems.at[k], recv_sem=recv_sems.at[k],
                device_id=(x, y, c), device_id_type=MESH).wait_recv()
        for cp, _ in cps:
            cp.wait_send()

    return pl.pallas_call(
        body, name="gather_small",
        out_shape=jax.ShapeDtypeStruct((8, r, d), pack.dtype),
        in_specs=[pl.BlockSpec(memory_space=pltpu.VMEM)], out_specs=pl.BlockSpec(memory_space=pltpu.VMEM),
        scratch_shapes=[pltpu.SemaphoreType.DMA((7,)), pltpu.SemaphoreType.DMA((7,))])(pack)


def _ew_call(name, fn, ins, out_dtypes):
    shape = ins[0].shape
    cols = shape[-1]
    rows = 1
    for s_ in shape[:-1]:
        rows *= s_
    ins2 = [a.reshape(rows, cols) for a in ins]
    tr = rows
    for t in range(16, min(rows, max(16, (1 << 19) // cols)) + 1, 16):
        if rows % t == 0:
            tr = t
    no = len(out_dtypes)

    def body(*refs):
        outs = fn(*[r[...] for r in refs[:len(ins2)]])
        for ref, val in zip(refs[len(ins2):], outs):
            ref[...] = val.astype(ref.dtype)

    spec = pl.BlockSpec((tr, cols), lambda i: (i, 0))
    res = pl.pallas_call(
        body, name=name, grid=(rows // tr,), in_specs=[spec] * len(ins2), out_specs=[spec] * no,
        out_shape=[jax.ShapeDtypeStruct((rows, cols), dt) for dt in out_dtypes],
        compiler_params=_params(("parallel",)))(*ins2)
    return [r.reshape(shape) for r in res]


def _adamw_math(w, g, m, v):
    m = ADAM_B1 * m + (1.0 - ADAM_B1) * g
    v = ADAM_B2 * v + (1.0 - ADAM_B2) * (g * g)
    m_hat = m / (1.0 - ADAM_B1 ** ADAM_STEP)
    v_hat = v / (1.0 - ADAM_B2 ** ADAM_STEP)
    delta = -ADAM_LR * (m_hat / (jnp.sqrt(v_hat) + ADAM_EPS) + ADAM_WD * w)
    return delta, m, v


def _adamw_call(name, w, g, m, v):
    return _ew_call(name, _adamw_math, [w, g, m, v], [F32, F32, F32])


def _tiled_call(name, fn, place, grid, in_items, out_items):
    ni = len(in_items)

    def body(place_ref, *refs):
        vals = fn(*[r[...] for r in refs[:ni]])
        for ref, val in zip(refs[ni:], vals):
            ref[...] = val.astype(ref.dtype)

    spec = pltpu.PrefetchScalarGridSpec(
        num_scalar_prefetch=1, grid=grid,
        in_specs=[pl.BlockSpec(blk, imap) for _, blk, imap in in_items],
        out_specs=[pl.BlockSpec(blk, imap) for _, _, blk, imap in out_items])
    return pl.pallas_call(
        body, name=name, grid_spec=spec,
        out_shape=[jax.ShapeDtypeStruct(shp, dt) for shp, dt, _, _ in out_items],
        compiler_params=_params(("arbitrary",) * len(grid)))(place, *[a for a, _, _ in in_items])


def _cast_call(name, place, shards):
    n0, n1 = shards[0].shape
    tr = _fit_rows(n0, n1, 1.5 * len(shards))
    ins = [(a, (tr, n1), lambda i, p: (i, 0)) for a in shards]
    outs = [((N_SHARD, n0, n1), BF16, (1, tr, n1), lambda i, p: (p[0], i, 0)) for _ in shards]
    return _tiled_call(name, lambda *v: [x[None] for x in v], place, (n0 // tr,), ins, outs)


def _pair_sum_call(name, place, fulls, gots):
    k = len(fulls)
    g = _Geom(*fulls[0].shape[1:], blocks=2.5 * k)
    blk = (1, g.tr, g.h1)
    ins = [(a, blk, lambda s, i, p: g.half_block((s,), i, p[1])) for a in fulls]
    ins += [(a, blk, lambda s, i, p: (s, i, 0)) for a in gots]
    outs = [((N_SHARD, g.h0, g.h1), BF16, blk, lambda s, i, p: (s, i, 0)) for _ in fulls]
    return _tiled_call(name, lambda *v: [v[j] + v[k + j] for j in range(k)], place, (N_SHARD, g.nblk), ins, outs)


def _chip_sum_call(name, place, fulls, gots, recvs):
    k = len(fulls)
    g = _Geom(*fulls[0].shape[1:], blocks=4.5 * k)
    blk = (1, g.tr, g.h1)
    ins = [(a, blk, lambda i, p: g.half_block((p[0],), i, p[1])) for a in fulls]
    ins += [(a, blk, lambda i, p: (p[0], i, 0)) for a in gots]
    ins += [(a, (3, g.tr, g.h1), lambda i, p: (0, i, 0)) for a in recvs]
    outs = [((g.n0, g.n1), F32, (g.tr, g.h1), lambda i, p: g.half_block((), i, p[1])) for _ in fulls]

    def fn(*v):
        res = []
        for j in range(k):
            r = v[2 * k + j].astype(F32)
            res.append(v[j][0] + v[k + j][0] + r[0] + r[1] + r[2])
        return res

    return _tiled_call(name, fn, place, (g.nblk,), ins, outs)


def _adamw_group_call(name, ws, gs, ms, vs, comm=None):
    k = len(ws)
    n0, n1 = ws[0].shape
    tr = _fit_rows(n0, n1, 8 * k)
    spec = pl.BlockSpec((tr, n1), lambda i: (i, 0))

    def body(*refs):
        for j in range(k):
            g = refs[k + j][...]
            delta, m, vv = _adamw_math(refs[j][...], g, refs[2 * k + j][...], refs[3 * k + j][...])
            for ref, val in zip(refs[4 * k + 4 * j:4 * k + 4 * j + 4], (g, delta, m, vv)):
                ref[...] = val

    flat = _hosted_call(
        body, name=name, grid=(n0 // tr,), in_specs=[spec] * (4 * k), out_specs=[spec] * (4 * k),
        out_shape=[jax.ShapeDtypeStruct((n0, n1), F32)] * (4 * k), dims=("parallel",),
        args=(*ws, *gs, *ms, *vs), comm=comm)
    return [flat[4 * j:4 * j + 4] for j in range(k)]


def _small_update_call(gathered, wp, mp, vp):
    nd, r, d = gathered.shape

    def body(g_ref, w_ref, m_ref, v_ref, gs_ref, d_ref, nm_ref, nv_ref):
        gs = g_ref[0]
        for k in range(1, nd):
            gs = gs + g_ref[k]
        gs_ref[...] = gs
        delta, m, v = _adamw_math(w_ref[...], gs[0:SMALL_ADAM_ROWS], m_ref[...], v_ref[...])
        d_ref[...] = delta
        nm_ref[...] = m
        nv_ref[...] = v

    vm = pl.BlockSpec(memory_space=pltpu.VMEM)
    ashape = jax.ShapeDtypeStruct((SMALL_ADAM_ROWS, d), F32)
    return pl.pallas_call(
        body, name="small_update", in_specs=[vm] * 4, out_specs=[vm] * 4,
        out_shape=[jax.ShapeDtypeStruct((r, d), F32), ashape, ashape, ashape],
        compiler_params=pltpu.CompilerParams(vmem_limit_bytes=VMEM_LIMIT_BYTES))(gathered, wp, mp, vp)


SMALL_NAMES = ["ffn1_norm", "mix_norm", "ffn2_norm", "final_norm", "q_latent_norm", "kv_latent_norm",
               "q_head_norm", "k_head_norm", "conv_b", "gate_a_b", "gate_x_b", "lru_lambda", "attn_out_norm",
               "lru_out_norm"]
ROW_CONV_W = 14
ROW_GATE_A = 16
ROW_GATE_X = 48
ROW_META = 80
ROW_LOSS = 96


def _row(a):
    flat = a.reshape(1, -1)
    return jnp.pad(flat, ((0, 0), (0, D_MODEL - flat.shape[1])))


def _pack_small(t, rows):
    parts = [_row(t[nm]) for nm in SMALL_NAMES]
    parts.append(t["conv_w"].reshape(2, D_MODEL))
    parts.append(t["gate_a_w"].reshape(32, D_MODEL))
    parts.append(t["gate_x_w"].reshape(32, D_MODEL))
    p = jnp.concatenate(parts, axis=0)
    return jnp.pad(p, ((0, rows - p.shape[0]), (0, 0)))


def _unpack_small(p, like):
    out = {}
    for k, nm in enumerate(SMALL_NAMES):
        out[nm] = p[k, 0:like[nm].size].reshape(like[nm].shape)
    out["gate_a_w"] = p[ROW_GATE_A:ROW_GATE_A + 32].reshape(like["gate_a_w"].shape)
    out["gate_x_w"] = p[ROW_GATE_X:ROW_GATE_X + 32].reshape(like["gate_x_w"].shape)
    return out


def _gate_dense(wg):
    w4 = wg[0].reshape(N_LRU_TILES, 2, 64, 64)
    zero = jnp.zeros((N_LRU_TILES, 64, 64), wg.dtype)
    top = jnp.concatenate([w4[:, 0], zero], axis=2)
    bot = jnp.concatenate([zero, w4[:, 1]], axis=2)
    return jnp.concatenate([top, bot], axis=1).astype(BF16)


def _gate_blocks(dw):
    return jnp.stack([dw[:, 0:64, 0:64], dw[:, 64:128, 64:128]], axis=1).reshape(8, 64, 64)


BIG_NAMES = ["ffn1_w_gate", "ffn1_w_up", "ffn1_w_down", "w_in", "w_uq", "w_uk", "w_uv", "w_out", "ffn2_w_gate",
             "ffn2_w_up", "ffn2_w_down"]
BIG_GROUPS = [["ffn1_w_gate", "ffn1_w_up", "ffn1_w_down", "ffn2_w_gate", "ffn2_w_up", "ffn2_w_down"], ["w_in"],
              ["w_uq"], ["w_uk", "w_uv"], ["w_out"]]
TRANSPOSED = ("ffn1_w_gate", "ffn1_w_up", "ffn2_w_gate", "ffn2_w_up", "w_in", "w_uq")


def _to2d(nm, a):
    return a[0].T if nm in TRANSPOSED else a[0]


def _from2d(nm, a):
    return (a.T if nm in TRANSPOSED else a)[None]


WEIGHT_NAMES = ["meta_tokens", "ffn1_norm", "ffn1_w_gate", "ffn1_w_up", "ffn1_w_down", "mix_norm", "w_in",
                "q_latent_norm", "w_uq", "kv_latent_norm", "w_uk", "w_uv", "q_head_norm", "k_head_norm", "conv_w",
                "conv_b", "gate_a_w", "gate_a_b", "gate_x_w", "gate_x_b", "lru_lambda", "attn_out_norm",
                "lru_out_norm", "w_out", "ffn2_norm", "ffn2_w_gate", "ffn2_w_up", "ffn2_w_down", "final_norm"]


def _weight_from(nm, slots):
    if nm == "w_in":
        win = slots.reshape(IN_WIDTH, D_MODEL)
        return jnp.concatenate([win[0:Z_KR + D_ROPE], jnp.zeros((128 - D_ROPE, D_MODEL), BF16),
                                win[Z_KR + D_ROPE:]], axis=0)
    if nm == "w_uq":
        return jnp.pad(slots, ((0, 0), (0, D_QKP - D_QK), (0, 0))).reshape(HEADS * D_QKP, Q_RANK)
    if nm in ("w_uk", "w_uv"):
        return slots.transpose(1, 0, 2).reshape(KV_RANK, HEADS * D_NOPE)
    if nm == "w_out":
        return slots.reshape(D_MODEL, D_MODEL)
    return slots


def _small_weights(p, small):
    w = {nm: p[nm] for nm in SMALL_NAMES}
    w["q_head_norm"] = jnp.pad(p["q_head_norm"], ((0, 0), (0, D_QKP - D_QK)))
    w["k_head_norm"] = jnp.pad(p["k_head_norm"], ((0, 0), (0, D_QKP - D_QK)))
    w["conv_w"] = small[:, N_META:N_META + 2, :].reshape(N_SHARD, CONV_K, LRU_TILE)
    w["gate_a_w"] = _gate_dense(p["gate_a_w"])
    w["gate_x_w"] = _gate_dense(p["gate_x_w"])
    meta = small[:, 0:N_META, :].transpose(1, 0, 2).reshape(N_META, D_MODEL)
    return w, meta


def _full_weights(p, gathered, small):
    w, meta = _small_weights(p, small)
    w.update({nm: _weight_from(nm, gathered[nm]) for nm in BIG_NAMES})
    return w, meta


def _shard_grad(nm, g):
    if nm == "w_in":
        return jnp.concatenate([g[0:Z_KR + D_ROPE], g[Z_MLA:]], axis=0).reshape(N_SHARD, IN_WIDTH // N_SHARD, D_MODEL)
    if nm == "w_uq":
        return g.reshape(HEADS, D_QKP, Q_RANK)[:, 0:D_QK, :]
    if nm in ("w_uk", "w_uv"):
        return g.reshape(KV_RANK, HEADS, D_NOPE).transpose(1, 0, 2)
    if nm == "w_out":
        return g.reshape(N_SHARD, D_MODEL // N_SHARD, D_MODEL)
    return g


def _shard_grads(g):
    return {nm: _shard_grad(nm, g[nm]) for nm in BIG_NAMES}


GATHER_FIRST = ["ffn1_w_gate", "ffn1_w_up", "ffn1_w_down"]
GATHER_AT = {"ffn1_up": ["w_in", "w_uq", "w_uk", "w_uv", "w_out"], "attn_fwd": ["ffn2_w_down"],
             "lru_fwd": ["ffn2_w_gate", "ffn2_w_up"]}
REDUCE_PLAN = [(["ffn2_w_gate", "ffn2_w_up", "ffn2_w_down"], "ffn2_din", "lru_bwd"),
               (["w_out", "w_uq", "w_uk", "w_uv", "w_in"], "mix_din", "ffn1_dact"),
               (["ffn1_w_gate", "ffn1_w_up", "ffn1_w_down"], "ffn1_din", None)]


def _same_shape_groups(names):
    return [[nm for nm in grp if nm in names] for grp in BIG_GROUPS if any(nm in names for nm in grp)]


class _Sched:
    def __init__(self, place, w, slots):
        self.place, self.w, self.slots = place, w, slots
        self.g = None
        self.sharded, self.from_pair, self.chip_bf16, self.from_chips = {}, {}, {}, {}

    def host(self, stage):
        if stage in GATHER_AT:
            return self.gather(GATHER_AT[stage])
        for names, pair_at, chips_at in REDUCE_PLAN:
            if stage == pair_at:
                return self.pair(names)
            if stage == chips_at:
                return self.chips(names)
        return None

    def gather(self, names):
        def deliver(outs):
            self.w.update({nm: _weight_from(nm, o) for nm, o in zip(names, outs)})
            return outs

        return _gather_comm([self.slots[nm] for nm in names], deliver)

    def pair(self, names):
        self.sharded.update({nm: _shard_grad(nm, self.g[nm]) for nm in names})

        def deliver(outs):
            self.from_pair.update(zip(names, outs))
            for grp in _same_shape_groups(names):
                sums = _pair_sum_call("pair_sum_" + grp[0], self.place, [self.sharded[nm] for nm in grp],
                                      [self.from_pair[nm] for nm in grp])
                self.chip_bf16.update(zip(grp, sums))
            return outs

        return _reduce_pair_comm([self.sharded[nm] for nm in names], deliver)

    def chips(self, names):
        def deliver(outs):
            self.from_chips.update(zip(names, outs))
            return outs

        return _reduce_chips_comm([self.chip_bf16[nm] for nm in names], deliver)

    def chip_sums(self, names):
        out = {}
        for grp in _same_shape_groups(names):
            sums = _chip_sum_call("chip_sum_" + grp[0], self.place, [self.sharded[nm] for nm in grp],
                                  [self.from_pair[nm] for nm in grp], [self.from_chips[nm] for nm in grp])
            out.update(zip(grp, sums))
        return out


def kernel(x, meta_tokens, ffn1_norm, ffn1_w_gate, ffn1_w_up, ffn1_w_down, mix_norm, w_in, q_latent_norm, w_uq, kv_latent_norm, w_uk, w_uv, q_head_norm, k_head_norm, conv_w, conv_b, gate_a_w, gate_a_b, gate_x_w, gate_x_b, lru_lambda, attn_out_norm, lru_out_norm, w_out, ffn2_norm, ffn2_w_gate, ffn2_w_up, ffn2_w_down, final_norm, loss_target, m_meta_tokens, m_ffn1_norm, m_ffn1_w_gate, m_ffn1_w_up, m_ffn1_w_down, m_mix_norm, m_w_in, m_q_latent_norm, m_w_uq, m_kv_latent_norm, m_w_uk, m_w_uv, m_q_head_norm, m_k_head_norm, m_conv_w, m_conv_b, m_gate_a_w, m_gate_a_b, m_gate_x_w, m_gate_x_b, m_lru_lambda, m_attn_out_norm, m_lru_out_norm, m_w_out, m_ffn2_norm, m_ffn2_w_gate, m_ffn2_w_up, m_ffn2_w_down, m_final_norm, v_meta_tokens, v_ffn1_norm, v_ffn1_w_gate, v_ffn1_w_up, v_ffn1_w_down, v_mix_norm, v_w_in, v_q_latent_norm, v_w_uq, v_kv_latent_norm, v_w_uk, v_w_uv, v_q_head_norm, v_k_head_norm, v_conv_w, v_conv_b, v_gate_a_w, v_gate_a_b, v_gate_x_w, v_gate_x_b, v_lru_lambda, v_attn_out_norm, v_lru_out_norm, v_w_out, v_ffn2_norm, v_ffn2_w_gate, v_ffn2_w_up, v_ffn2_w_down, v_final_norm):
    args = locals()
    p = {nm: args[nm] for nm in WEIGHT_NAMES}
    mom = {nm: args["m_" + nm] for nm in WEIGHT_NAMES}
    var = {nm: args["v_" + nm] for nm in WEIGHT_NAMES}
    nb, seq, d = x.shape
    lp = CHUNK + seq
    xi, yi, ci = lax.axis_index("x"), lax.axis_index("y"), lax.axis_index("c")
    chip = 2 * xi + yi

    place = jnp.stack([chip, ci]).astype(jnp.int32)
    p2 = {nm: _to2d(nm, p[nm]) for nm in BIG_NAMES}
    m2 = {nm: _to2d(nm, mom[nm]) for nm in BIG_NAMES}
    v2 = {nm: _to2d(nm, var[nm]) for nm in BIG_NAMES}

    slots = {}
    for grp in BIG_GROUPS:
        for nm, buf in zip(grp, _cast_call("cast_" + grp[0], place, [p2[nm] for nm in grp])):
            slots[nm] = buf
    small_shard = jnp.concatenate(
        [meta_tokens, conv_w[0].reshape(2, 2 * LRU_TILE), jnp.zeros((14, 2 * LRU_TILE), F32)], axis=0)
    small_slots = lax.dynamic_update_slice(jnp.zeros((N_SHARD,) + small_shard.shape, F32), small_shard[None],
                                           (chip, 0, 0))
    first = _comm_call("gather_first", _gather_comm([slots[nm] for nm in GATHER_FIRST] + [small_slots], lambda o: o))
    w, meta = _small_weights(p, first[-1])
    w.update({nm: _weight_from(nm, o) for nm, o in zip(GATHER_FIRST, first[:-1])})
    sched = _Sched(place, w, slots)

    h0 = jnp.concatenate(
        [jnp.zeros((nb, PAD_ROWS, d), F32), jnp.broadcast_to(meta[None], (nb, N_META, d)), x], axis=1)
    target = jnp.pad(loss_target, ((0, 0), (CHUNK, 0), (0, 0)))
    loss_part, dh0, g = _local_step(h0.reshape(nb * lp, d), target.reshape(nb * lp, d), w, nb, lp, sched)
    dh0 = dh0.reshape(nb, lp, d)
    grad_x = dh0[:, CHUNK:, :]

    gs = {nm: g[nm] for nm in SMALL_NAMES}
    gs["q_head_norm"] = g["q_head_norm"][:, 0:D_QK]
    gs["k_head_norm"] = g["k_head_norm"][:, 0:D_QK]
    for nm in ("conv_b", "gate_a_b", "gate_x_b", "lru_lambda"):
        gs[nm] = g[nm].reshape(1, LRU_W)
    gs["conv_w"] = g["conv_w"].transpose(1, 0, 2).reshape(CONV_K, LRU_W)
    gs["gate_a_w"] = _gate_blocks(g["gate_a_w"])
    gs["gate_x_w"] = _gate_blocks(g["gate_x_w"])
    pack = _pack_small(gs, ROW_META)
    pack = jnp.concatenate([pack, jnp.sum(dh0[:, PAD_ROWS:CHUNK, :], axis=0), _row(loss_part[:, 0:1]),
                            jnp.zeros((SMALL_ROWS - ROW_LOSS - 1, D_MODEL), F32)], axis=0)
    small_like = {nm: p[nm] for nm in SMALL_NAMES + ["gate_a_w", "gate_x_w"]}

    def pack_w(t):
        tt = {nm: t[nm] for nm in SMALL_NAMES + ["gate_a_w", "gate_x_w"]}
        tt["conv_w"] = jnp.zeros((CONV_K, LRU_W), F32)
        return _pack_small(tt, SMALL_ADAM_ROWS)

    gsum, dsm, msm, vsm = _small_update_call(_gather_small_call(pack), pack_w(p), pack_w(mom), pack_w(var))
    grads = _unpack_small(gsum, small_like)
    delta = _unpack_small(dsm, small_like)
    new_m = _unpack_small(msm, small_like)
    new_v = _unpack_small(vsm, small_like)
    loss = gsum[ROW_LOSS, 0]
    gmeta = gsum[ROW_META:ROW_META + N_META].reshape(N_META, N_SHARD, D_MODEL // N_SHARD)
    grads["meta_tokens"] = lax.dynamic_index_in_dim(gmeta, chip, axis=1, keepdims=False)
    gconv = gsum[ROW_CONV_W:ROW_CONV_W + 2].reshape(CONV_K, N_SHARD, LRU_TILE)
    grads["conv_w"] = lax.dynamic_index_in_dim(gconv, chip, axis=1, keepdims=False)[None]
    for nm in ("meta_tokens", "conv_w"):
        delta[nm], new_m[nm], new_v[nm] = _adamw_call("adamw_" + nm, p[nm], grads[nm], mom[nm], var[nm])

    done = REDUCE_PLAN[0][0] + REDUCE_PLAN[1][0]
    last = REDUCE_PLAN[2][0]
    mine = sched.chip_sums(done)
    shard_grads = dict(zip(done, _comm_call("share_pair", _share_pair_comm([mine[nm] for nm in done], lambda o: o))))

    def adamw(names, comm=None):
        res = _adamw_group_call("adamw_" + names[0], [p2[nm] for nm in names], [shard_grads[nm] for nm in names],
                                [m2[nm] for nm in names], [v2[nm] for nm in names], comm)
        for nm, (gg, dd, mm, vv) in zip(names, res):
            grads[nm], delta[nm], new_m[nm], new_v[nm] = (_from2d(nm, t) for t in (gg, dd, mm, vv))

    groups = _same_shape_groups(done)
    adamw(groups[0], sched.chips(last))
    for grp in groups[1:]:
        adamw(grp)
    mine = sched.chip_sums(last)
    shard_grads = dict(zip(last, _comm_call("share_last", _share_pair_comm([mine[nm] for nm in last], lambda o: o))))
    adamw(last)

    return (loss, grad_x, *[grads[nm] for nm in WEIGHT_NAMES], *[delta[nm] for nm in WEIGHT_NAMES],
            *[new_m[nm] for nm in WEIGHT_NAMES], *[new_v[nm] for nm in WEIGHT_NAMES])
```

```python
import functools
import math

import jax
import jax.numpy as jnp
from jax import lax
from jax.experimental import pallas as pl
from jax.experimental.pallas import tpu as pltpu

F32 = jnp.float32
BF16 = jnp.bfloat16
MESH = pl.DeviceIdType.MESH

D_MODEL = 1024
N_META = 16
CHUNK = 64
PAD_ROWS = CHUNK - N_META
HEADS = 4
D_NOPE = 128
D_ROPE = 64
D_QK = D_NOPE + D_ROPE
D_QKP = 256
D_V = 128
KV_RANK = 256
Q_RANK = 384
MLA_W = HEADS * D_V
LRU_W = 512
LRU_TILE = 128
N_LRU_TILES = LRU_W // LRU_TILE
CONV_K = 4
C_RGLRU = 8.0
ROPE_THETA = 10000.0
D_FF = 2816
N_SHARD = 4
EPS = 1e-6
NEG_INF = -1e30
Z_KR = Q_RANK + KV_RANK
Z_MLA = Z_KR + 128
Z_U = Z_MLA
Z_G = Z_U + LRU_W
Z_W = Z_G + LRU_W
IN_WIDTH = Q_RANK + KV_RANK + D_ROPE + 2 * LRU_W

ADAM_LR = 0.001
ADAM_B1 = 0.9
ADAM_B2 = 0.999
ADAM_EPS = 1e-08
ADAM_WD = 0.01
ADAM_STEP = 10

VMEM_LIMIT_BYTES = 56 * 1024 * 1024
SMALL_ROWS = 104
SMALL_ADAM_ROWS = 80


def _params(sem):
    return pltpu.CompilerParams(dimension_semantics=sem, vmem_limit_bytes=VMEM_LIMIT_BYTES)


def _row_tile(rows, target):
    best = 16
    for t in range(16, min(rows, target) + 1, 16):
        if rows % t == 0:
            best = t
    return best


def _col_tile(cols, target):
    best = cols
    for t in range(128, min(cols, target) + 1, 128):
        if cols % t == 0:
            best = t
    return best


def _dot(a, b):
    return jnp.dot(a, b, preferred_element_type=F32)


def _dot_nt(a, b):
    return lax.dot_general(a, b, (((1,), (1,)), ((), ())), preferred_element_type=F32)


def _dot_tn(a, b):
    return lax.dot_general(a, b, (((0,), (0,)), ((), ())), preferred_element_type=F32)


def _rms(x, n):
    return lax.rsqrt(jnp.sum(x * x, axis=-1, keepdims=True) * (1.0 / n) + EPS)


def _rms_bwd(dn, nrm, r, n):
    return r * (dn - nrm * (jnp.sum(dn * nrm, axis=-1, keepdims=True) * (1.0 / n)))


def _gelu(x):
    k = math.sqrt(2.0 / math.pi)
    t = jnp.tanh(k * (x + 0.044715 * x * x * x))
    return 0.5 * x * (1.0 + t), t


def _gelu_grad(x, t):
    k = math.sqrt(2.0 / math.pi)
    return 0.5 * (1.0 + t) + 0.5 * x * (1.0 - t * t) * k * (1.0 + 3.0 * 0.044715 * x * x)


def _neg_expm1(x):
    series = -x * (1.0 + x * (0.5 + x * (1.0 / 6 + x * (1.0 / 24 + x * (1.0 / 120 + x * (1.0 / 720))))))
    return jnp.where(x > -0.25, series, 1.0 - jnp.exp(x))


def _softplus_neg(lam):
    e = jnp.exp(-jnp.abs(lam))
    log1p = jnp.where(e < 0.01, e * (1.0 - e * (0.5 - e * (1.0 / 3 - e * 0.25))), jnp.log(1.0 + e))
    return jnp.maximum(-lam, 0.0) + log1p


def _rope(t, c, s1, s2):
    return t * c + pltpu.roll(t, 96, 1) * s1 + pltpu.roll(t, 32, 1) * s2


def _rope_t(d, c, s1, s2):
    return d * c + pltpu.roll(d * s1, 32, 1) + pltpu.roll(d * s2, 96, 1)


def _rope_tables(lp):
    pos = (jnp.arange(lp, dtype=jnp.int32) - PAD_ROWS).astype(F32)
    inv_freq = ROPE_THETA ** (-jnp.arange(0, D_ROPE // 2, dtype=F32) / (D_ROPE // 2))
    ang = pos[:, None] * inv_freq[None, :]
    cos, sin = jnp.cos(ang), jnp.sin(ang)
    z = jnp.zeros_like(cos)
    return (jnp.concatenate([cos, cos, z, z], 1), jnp.concatenate([-sin, z, z, z], 1),
            jnp.concatenate([z, sin, z, z], 1))


def _rmsnorm_call(name, h, g, tm):
    rows, d = h.shape

    def body(h_ref, g_ref, o_ref):
        x = h_ref[...]
        o_ref[...] = (x * _rms(x, d) * g_ref[...]).astype(BF16)

    return pl.pallas_call(
        body, name=name, grid=(rows // tm,),
        in_specs=[pl.BlockSpec((tm, d), lambda i: (i, 0)), pl.BlockSpec((1, d), lambda i: (0, 0))],
        out_specs=pl.BlockSpec((tm, d), lambda i: (i, 0)),
        out_shape=jax.ShapeDtypeStruct((rows, d), BF16),
        compiler_params=_params(("parallel",)))(h, g)


def _ffn_up_call(name, u, wg, wu, tm, comm=None):
    rows, d = u.shape
    ns, fs, _ = wg.shape

    def body(u_ref, wg_ref, wu_ref, g_ref, p_ref, a_ref):
        uu = u_ref[...]
        g = _dot_nt(uu, wg_ref[0])
        p = _dot_nt(uu, wu_ref[0])
        g_ref[0] = g.astype(BF16)
        p_ref[0] = p.astype(BF16)
        a_ref[0] = (g * jax.nn.sigmoid(g) * p).astype(BF16)

    wspec = pl.BlockSpec((1, fs, d), lambda s, i: (s, 0, 0))
    ospec = pl.BlockSpec((1, tm, fs), lambda s, i: (s, i, 0))
    oshape = jax.ShapeDtypeStruct((ns, rows, fs), BF16)
    return _hosted_call(
        body, name=name, grid=(ns, rows // tm),
        in_specs=[pl.BlockSpec((tm, d), lambda s, i: (i, 0)), wspec, wspec],
        out_specs=[ospec, ospec, ospec], out_shape=[oshape, oshape, oshape],
        dims=("parallel", "parallel"), args=(u, wg, wu), comm=comm)


def _ffn_down_call(name, a, wd, h, tm):
    rows, d = h.shape
    ns, _, fs = a.shape

    def body(a_ref, wd_ref, h_ref, o_ref):
        acc = h_ref[...]
        for s in range(ns):
            acc = acc + 0.5 * _dot(a_ref[s], wd_ref[s])
        o_ref[...] = acc

    return pl.pallas_call(
        body, name=name, grid=(rows // tm,),
        in_specs=[pl.BlockSpec((ns, tm, fs), lambda i: (0, i, 0)),
                  pl.BlockSpec((ns, fs, d), lambda i: (0, 0, 0)),
                  pl.BlockSpec((tm, d), lambda i: (i, 0))],
        out_specs=pl.BlockSpec((tm, d), lambda i: (i, 0)),
        out_shape=jax.ShapeDtypeStruct((rows, d), F32),
        compiler_params=_params(("parallel",)))(a, wd, h)


def _mm_call(name, a, bt, tm, out_dtype):
    rows, k = a.shape
    n = bt.shape[0]

    def body(a_ref, b_ref, o_ref):
        o_ref[...] = _dot_nt(a_ref[...], b_ref[...]).astype(out_dtype)

    return pl.pallas_call(
        body, name=name, grid=(rows // tm,),
        in_specs=[pl.BlockSpec((tm, k), lambda i: (i, 0)), pl.BlockSpec((n, k), lambda i: (0, 0))],
        out_specs=pl.BlockSpec((tm, n), lambda i: (i, 0)),
        out_shape=jax.ShapeDtypeStruct((rows, n), out_dtype),
        compiler_params=_params(("parallel",)))(a, bt)


def _mla_heads(z, gql, gkvl, wuq, wuk, wuv):
    cq = z[:, 0:Q_RANK]
    ckv = z[:, Q_RANK:Z_KR]
    kr = z[:, Z_KR:Z_MLA]
    rq = _rms(cq, Q_RANK)
    nq = cq * rq
    cqn = (nq * gql).astype(BF16)
    rkv = _rms(ckv, KV_RANK)
    nkv = ckv * rkv
    ckvn = (nkv * gkvl).astype(BF16)
    qraw = _dot_nt(cqn, wuq)
    knope = _dot(ckvn, wuk)
    v = _dot(ckvn, wuv)
    skr = jnp.sum(kr * kr, axis=-1, keepdims=True)
    heads = []
    for hd in range(HEADS):
        qh = qraw[:, hd * D_QKP:(hd + 1) * D_QKP]
        rqh = lax.rsqrt(jnp.sum(qh * qh, axis=-1, keepdims=True) * (1.0 / D_QK) + EPS)
        kn = knope[:, hd * D_NOPE:(hd + 1) * D_NOPE]
        rkh = lax.rsqrt((jnp.sum(kn * kn, axis=-1, keepdims=True) + skr) * (1.0 / D_QK) + EPS)
        heads.append((qh * rqh, rqh, kn * rkh, kr * rkh, rkh))
    return dict(rq=rq, nq=nq, cqn=cqn, rkv=rkv, nkv=nkv, ckvn=ckvn, v=v, heads=heads)


def _mla_prep_call(z, gql, gkvl, gqh, gkh, wuq, wuk, wuv, tabs, lp, tm):
    rows = z.shape[0]
    tpe = lp // tm

    def body(z_ref, gql_ref, gkvl_ref, gqh_ref, gkh_ref, wuq_ref, wuk_ref, wuv_ref, c_ref, s1_ref, s2_ref,
             q_ref, k_ref, v_ref, cqn_ref, ckvn_ref):
        m = _mla_heads(z_ref[...], gql_ref[...], gkvl_ref[...], wuq_ref[...], wuk_ref[...], wuv_ref[...])
        c, s1, s2 = c_ref[...], s1_ref[...], s2_ref[...]
        gq, gk = gqh_ref[...], gkh_ref[...]
        row = (pl.program_id(0) % tpe) * tm + lax.broadcasted_iota(jnp.int32, (tm, 1), 0)
        spare = (lax.broadcasted_iota(jnp.int32, (1, D_QKP - D_NOPE), 1) == D_ROPE).astype(F32)
        kmask = jnp.where(row < PAD_ROWS, NEG_INF * math.sqrt(D_QK), 0.0) * spare
        for hd in range(HEADS):
            qn, _, knn, krn, _ = m["heads"][hd]
            qg = qn * gq
            q_ref[hd, :, 0:D_NOPE] = qg[:, 0:D_NOPE].astype(BF16)
            q_ref[hd, :, D_NOPE:D_QKP] = (_rope(qg[:, D_NOPE:D_QKP], c, s1, s2) + spare).astype(BF16)
            k_ref[hd, :, 0:D_NOPE] = (knn * gk[:, 0:D_NOPE]).astype(BF16)
            k_ref[hd, :, D_NOPE:D_QKP] = (_rope(krn * gk[:, D_NOPE:D_QKP], c, s1, s2) + kmask).astype(BF16)
            v_ref[hd] = m["v"][:, hd * D_V:(hd + 1) * D_V].astype(BF16)
        cqn_ref[...] = m["cqn"]
        ckvn_ref[...] = m["ckvn"]

    def const(shape):
        return pl.BlockSpec(shape, lambda i: tuple(0 for _ in shape))

    tab = pl.BlockSpec((tm, 128), lambda i: (i % tpe, 0))
    return pl.pallas_call(
        body, name="mla_prep", grid=(rows // tm,),
        in_specs=[pl.BlockSpec((tm, Z_MLA), lambda i: (i, 0)), const((1, Q_RANK)), const((1, KV_RANK)),
                  const((1, D_QKP)), const((1, D_QKP)), const((HEADS * D_QKP, Q_RANK)),
                  const((KV_RANK, HEADS * D_NOPE)), const((KV_RANK, HEADS * D_V)), tab, tab, tab],
        out_specs=[pl.BlockSpec((HEADS, tm, D_QKP), lambda i: (0, i, 0)),
                   pl.BlockSpec((HEADS, tm, D_QKP), lambda i: (0, i, 0)),
                   pl.BlockSpec((HEADS, tm, D_V), lambda i: (0, i, 0)),
                   pl.BlockSpec((tm, Q_RANK), lambda i: (i, 0)),
                   pl.BlockSpec((tm, KV_RANK), lambda i: (i, 0))],
        out_shape=[jax.ShapeDtypeStruct((HEADS, rows, D_QKP), BF16),
                   jax.ShapeDtypeStruct((HEADS, rows, D_QKP), BF16),
                   jax.ShapeDtypeStruct((HEADS, rows, D_V), BF16),
                   jax.ShapeDtypeStruct((rows, Q_RANK), BF16),
                   jax.ShapeDtypeStruct((rows, KV_RANK), BF16)],
        compiler_params=_params(("parallel",)))(z, gql, gkvl, gqh, gkh, wuq, wuk, wuv, *tabs)


def _q_block(lp):
    chunks = lp // CHUNK
    best = 1
    for g in range(1, 5):
        if chunks % g == 0:
            best = g
    return best * CHUNK


def _diag_bias(qb):
    shift = CHUNK.bit_length() - 1
    r = jnp.right_shift(lax.broadcasted_iota(jnp.int32, (qb, qb), 0), shift)
    c = jnp.right_shift(lax.broadcasted_iota(jnp.int32, (qb, qb), 1), shift)
    return jnp.where(c <= r, 0.0, NEG_INF)


def _attn_fwd_call(q, k, v, nb, lp, comm=None):
    rows = nb * lp
    qb = _q_block(lp)
    scale = 1.0 / math.sqrt(D_QK)

    def body(q_ref, k_ref, v_ref, o_ref, lse_ref):
        dbias = _diag_bias(qb)
        for j in range(lp // qb):
            j0, ext = j * qb, (j + 1) * qb
            qj = q_ref[0, j0:ext, :]
            sd = _dot_nt(qj, k_ref[0, j0:ext, :]) * scale + dbias
            mx = jnp.max(sd, axis=-1, keepdims=True)
            if j > 0:
                so = _dot_nt(qj, k_ref[0, 0:j0, :]) * scale
                mx = jnp.maximum(mx, jnp.max(so, axis=-1, keepdims=True))
            pd = jnp.exp(sd - mx)
            l = jnp.sum(pd, axis=-1, keepdims=True)
            o = _dot(pd.astype(BF16), v_ref[0, j0:ext, :])
            if j > 0:
                po = jnp.exp(so - mx)
                l = l + jnp.sum(po, axis=-1, keepdims=True)
                o = o + _dot(po.astype(BF16), v_ref[0, 0:j0, :])
            o_ref[j0:ext, :] = o / l
            lse_ref[0, j0:ext, :] = mx + jnp.log(l)

    return _hosted_call(
        body, name="attn_fwd", grid=(nb, HEADS),
        in_specs=[pl.BlockSpec((1, lp, D_QKP), lambda b, h: (h, b, 0)),
                  pl.BlockSpec((1, lp, D_QKP), lambda b, h: (h, b, 0)),
                  pl.BlockSpec((1, lp, D_V), lambda b, h: (h, b, 0))],
        out_specs=[pl.BlockSpec((lp, D_V), lambda b, h: (b, h)),
                   pl.BlockSpec((1, lp, 1), lambda b, h: (h, b, 0))],
        out_shape=[jax.ShapeDtypeStruct((rows, MLA_W), F32),
                   jax.ShapeDtypeStruct((HEADS, rows, 1), F32)],
        dims=("parallel", "parallel"), args=(q, k, v), comm=comm)


def _attn_bwd_call(q, k, v, o, lse, do, nb, lp):
    rows = nb * lp
    qb = _q_block(lp)
    scale = 1.0 / math.sqrt(D_QK)

    def body(q_ref, k_ref, v_ref, o_ref, lse_ref, do_ref, dq_ref, dk_ref, dv_ref, dk_acc, dv_acc):
        dk_acc[...] = jnp.zeros_like(dk_acc)
        dv_acc[...] = jnp.zeros_like(dv_acc)
        dbias = _diag_bias(qb)
        for j in range(lp // qb):
            j0, ext = j * qb, (j + 1) * qb
            qj = q_ref[0, j0:ext, :]
            doj = do_ref[j0:ext, :]
            delta = jnp.sum(doj * o_ref[j0:ext, :], axis=-1, keepdims=True)
            dob = doj.astype(BF16)
            lse = lse_ref[0, j0:ext, :]
            dq = jnp.zeros((qb, D_QKP), F32)
            for lo, hi, bias in ((j0, ext, dbias), (0, j0, None)):
                if hi == lo:
                    continue
                kk = k_ref[0, lo:hi, :]
                s = _dot_nt(qj, kk) * scale
                p = jnp.exp((s if bias is None else s + bias) - lse)
                dv_acc[lo:hi, :] += _dot_tn(p.astype(BF16), dob)
                dp = _dot_nt(dob, v_ref[0, lo:hi, :])
                ds = (p * (dp - delta) * scale).astype(BF16)
                dq = dq + _dot(ds, kk)
                dk_acc[lo:hi, :] += _dot_tn(ds, qj)
            dq_ref[0, j0:ext, :] = dq.astype(BF16)
        dk_ref[0] = dk_acc[...].astype(BF16)
        dv_ref[0] = dv_acc[...].astype(BF16)

    qspec = pl.BlockSpec((1, lp, D_QKP), lambda b, h: (h, b, 0))
    vspec = pl.BlockSpec((1, lp, D_V), lambda b, h: (h, b, 0))
    ospec = pl.BlockSpec((lp, D_V), lambda b, h: (b, h))
    return pl.pallas_call(
        body, name="attn_bwd", grid=(nb, HEADS),
        in_specs=[qspec, qspec, vspec, ospec, pl.BlockSpec((1, lp, 1), lambda b, h: (h, b, 0)), ospec],
        out_specs=[qspec, qspec, vspec],
        out_shape=[jax.ShapeDtypeStruct((HEADS, rows, D_QKP), BF16),
                   jax.ShapeDtypeStruct((HEADS, rows, D_QKP), BF16),
                   jax.ShapeDtypeStruct((HEADS, rows, D_V), BF16)],
        scratch_shapes=[pltpu.VMEM((lp, D_QKP), F32), pltpu.VMEM((lp, D_V), F32)],
        compiler_params=_params(("parallel", "parallel")))(q, k, v, o, lse, do)


def _lru_gates(u, cw, cb, wa, ba, wx, bx, lam, lp):
    xc = (cw[3:4, :] * u + cw[2:3, :] * pltpu.roll(u, 1, 0) + cw[1:2, :] * pltpu.roll(u, 2, 0)
          + cw[0:1, :] * pltpu.roll(u, 3, 0) + cb)
    xcb = xc.astype(BF16)
    r = jax.nn.sigmoid(_dot(xcb, wa) + ba)
    i = jax.nn.sigmoid(_dot(xcb, wx) + bx)
    sp = _softplus_neg(lam)
    la = -C_RGLRU * r * sp
    a = jnp.exp(la)
    mult = jnp.sqrt(_neg_expm1(2.0 * la))
    row = lax.broadcasted_iota(jnp.int32, (lp, LRU_TILE), 0)
    first = row == PAD_ROWS
    valid = row >= PAD_ROWS
    mult_eff = jnp.where(first, 1.0, mult)
    return dict(xc=xc, xcb=xcb, r=r, i=i, sp=sp, la=la, a=a, mult=mult, mult_eff=mult_eff, first=first, valid=valid)


def _scan_rows(a, b, a_s, b_s, out_ref, lp, reverse):
    sub = lax.broadcasted_iota(jnp.int32, (lp, LRU_TILE), 0) & 7
    for dist in (1, 2, 4):
        shift = lp - dist if reverse else dist
        keep = (sub + dist <= 7) if reverse else (sub >= dist)
        a_sh = pltpu.roll(a, shift, 0)
        b_sh = pltpu.roll(b, shift, 0)
        b = jnp.where(keep, a * b_sh + b, b)
        a = jnp.where(keep, a * a_sh, a)
    a_s[...] = a
    b_s[...] = b
    n_groups = lp // 8
    edge = 0 if reverse else 7

    def group(gi, carry):
        r0 = pl.multiple_of(((n_groups - 1 - gi) if reverse else gi) * 8, 8)
        a8 = a_s[pl.ds(r0, 8), :]
        b8 = b_s[pl.ds(r0, 8), :]
        out_ref[pl.ds(r0, 8), :] = a8 * carry + b8
        return a8[edge:edge + 1, :] * carry + b8[edge:edge + 1, :]

    lax.fori_loop(0, n_groups, group, jnp.zeros((1, LRU_TILE), F32), unroll=4)


def _lru_specs(lp):
    seq = lambda col0: pl.BlockSpec((lp, LRU_TILE), lambda t, b: (b, col0 + t))
    cw = pl.BlockSpec((1, CONV_K, LRU_TILE), lambda t, b: (t, 0, 0))
    vec = pl.BlockSpec((1, LRU_TILE), lambda t, b: (0, t))
    mat = pl.BlockSpec((1, LRU_TILE, LRU_TILE), lambda t, b: (t, 0, 0))
    return seq, cw, vec, mat


def _lru_fwd_call(z, cw, cb, wa, ba, wx, bx, lam, nb, lp, comm=None):
    rows = nb * lp
    seq, cwspec, vec, mat = _lru_specs(lp)

    def body(u_ref, g_ref, cw_ref, cb_ref, wa_ref, ba_ref, wx_ref, bx_ref, lam_ref, y_ref, hs_ref, a_s, b_s):
        m = _lru_gates(u_ref[...], cw_ref[0], cb_ref[...], wa_ref[0], ba_ref[...], wx_ref[0], bx_ref[...],
                       lam_ref[...], lp)
        a = jnp.where(m["valid"], m["a"], 0.0)
        b = jnp.where(m["valid"], m["mult_eff"] * (m["i"] * m["xc"]), 0.0)
        _scan_rows(a, b, a_s, b_s, hs_ref, lp, reverse=False)
        gl, _ = _gelu(g_ref[...])
        y_ref[...] = hs_ref[...] * gl

    oshape = jax.ShapeDtypeStruct((rows, LRU_W), F32)
    return _hosted_call(
        body, name="lru_fwd", grid=(N_LRU_TILES, nb),
        in_specs=[seq(Z_U // LRU_TILE), seq(Z_G // LRU_TILE), cwspec, vec, mat, vec, mat, vec, vec],
        out_specs=[seq(0), seq(0)], out_shape=[oshape, oshape],
        scratch_shapes=[pltpu.VMEM((lp, LRU_TILE), F32), pltpu.VMEM((lp, LRU_TILE), F32)],
        dims=("parallel", "parallel"), args=(z, z, cw, cb, wa, ba, wx, bx, lam), comm=comm)


def _lru_bwd_call(z, hs, dy, cw, cb, wa, ba, wx, bx, lam, nb, lp, comm=None):
    rows = nb * lp
    seq, cwspec, vec, mat = _lru_specs(lp)

    def body(u_ref, g_ref, hs_ref, dy_ref, cw_ref, cb_ref, wa_ref, ba_ref, wx_ref, bx_ref, lam_ref,
             du_ref, dg_ref, dcw_ref, dcb_ref, dwa_ref, dba_ref, dwx_ref, dbx_ref, dlam_ref, a_s, b_s, d_s):
        b_idx = pl.program_id(1)
        u = u_ref[...]
        cw = cw_ref[0]
        wa, wx = wa_ref[0], wx_ref[0]
        lam = lam_ref[...]
        m = _lru_gates(u, cw, cb_ref[...], wa, ba_ref[...], wx, bx_ref[...], lam, lp)
        gate = g_ref[...]
        gl, th = _gelu(gate)
        dy = dy_ref[...]
        hs = hs_ref[...]
        dg_ref[...] = (dy * hs * _gelu_grad(gate, th)).astype(BF16)
        a_eff = jnp.where(m["valid"], m["a"], 0.0)
        _scan_rows(pltpu.roll(a_eff, lp - 1, 0), dy * gl, a_s, b_s, d_s, lp, reverse=True)
        ds = d_s[...]
        xc, r, i = m["xc"], m["r"], m["i"]
        row = lax.broadcasted_iota(jnp.int32, (lp, LRU_TILE), 0)
        da = ds * jnp.where(row >= 1, pltpu.roll(hs, 1, 0), 0.0)
        db = jnp.where(m["valid"], ds, 0.0)
        di = db * m["mult_eff"] * xc
        dxc = db * m["mult_eff"] * i
        live = m["valid"] & jnp.logical_not(m["first"])
        e2 = jnp.exp(2.0 * m["la"])
        dm = jnp.where(live, db * i * xc, 0.0)
        dla = da * m["a"] + jnp.where(live, dm * (-e2 / m["mult"]), 0.0)
        dr = dla * (-C_RGLRU * m["sp"])
        dsp = jnp.sum(dla * (-C_RGLRU * r), axis=0, keepdims=True)
        dpr = (dr * r * (1.0 - r))
        dpi = (di * i * (1.0 - i))
        dprb, dpib = dpr.astype(BF16), dpi.astype(BF16)
        dxc = dxc + _dot_nt(dprb, wa) + _dot_nt(dpib, wx)
        du = (cw[3:4, :] * dxc + cw[2:3, :] * pltpu.roll(dxc, lp - 1, 0) + cw[1:2, :] * pltpu.roll(dxc, lp - 2, 0)
              + cw[0:1, :] * pltpu.roll(dxc, lp - 3, 0))
        du_ref[...] = jnp.where(m["valid"], du, 0.0).astype(BF16)
        tap = lax.broadcasted_iota(jnp.int32, (CONV_K, LRU_TILE), 0)
        dcw = jnp.zeros((CONV_K, LRU_TILE), F32)
        for kk in range(CONV_K):
            shifted = u if kk == CONV_K - 1 else pltpu.roll(u, CONV_K - 1 - kk, 0)
            dcw = jnp.where(tap == kk, jnp.sum(dxc * shifted, axis=0, keepdims=True), dcw)
        parts = [(dcw_ref, dcw[None]), (dcb_ref, jnp.sum(dxc, axis=0, keepdims=True)[None]),
                 (dwa_ref, _dot_tn(m["xcb"], dprb)[None]), (dba_ref, jnp.sum(dpr, axis=0, keepdims=True)[None]),
                 (dwx_ref, _dot_tn(m["xcb"], dpib)[None]), (dbx_ref, jnp.sum(dpi, axis=0, keepdims=True)[None]),
                 (dlam_ref, (dsp * (-jax.nn.sigmoid(-lam)))[None])]

        @pl.when(b_idx == 0)
        def _():
            for ref, val in parts:
                ref[...] = val

        @pl.when(b_idx != 0)
        def _():
            for ref, val in parts:
                ref[...] += val

    bshape = jax.ShapeDtypeStruct((rows, LRU_W), BF16)
    vec3 = pl.BlockSpec((1, 1, LRU_TILE), lambda t, b: (t, 0, 0))
    vshape = jax.ShapeDtypeStruct((N_LRU_TILES, 1, LRU_TILE), F32)
    mshape = jax.ShapeDtypeStruct((N_LRU_TILES, LRU_TILE, LRU_TILE), F32)
    return _hosted_call(
        body, name="lru_bwd", grid=(N_LRU_TILES, nb),
        in_specs=[seq(Z_U // LRU_TILE), seq(Z_G // LRU_TILE), seq(0), seq(0), cwspec, vec, mat, vec, mat, vec, vec],
        out_specs=[seq(0), seq(0), cwspec, vec3, mat, vec3, mat, vec3, vec3],
        out_shape=[bshape, bshape, jax.ShapeDtypeStruct((N_LRU_TILES, CONV_K, LRU_TILE), F32), vshape, mshape,
                   vshape, mshape, vshape, vshape],
        scratch_shapes=[pltpu.VMEM((lp, LRU_TILE), F32)] * 3,
        dims=("parallel", "arbitrary"), args=(z, z, hs, dy, cw, cb, wa, ba, wx, bx, lam), comm=comm)


def _mix_out_call(ya, yl, ga, gl, wout, h, tm):
    rows, d = h.shape

    def body(ya_ref, yl_ref, ga_ref, gl_ref, w_ref, h_ref, y_ref, o_ref):
        a = ya_ref[...]
        l = yl_ref[...]
        an = (a * _rms(a, MLA_W) * ga_ref[...]).astype(BF16)
        ln = (l * _rms(l, LRU_W) * gl_ref[...]).astype(BF16)
        y_ref[:, 0:MLA_W] = an
        y_ref[:, MLA_W:MLA_W + LRU_W] = ln
        o_ref[...] = h_ref[...] + _dot(an, w_ref[0:MLA_W, :]) + _dot(ln, w_ref[MLA_W:MLA_W + LRU_W, :])

    half = pl.BlockSpec((tm, MLA_W), lambda i: (i, 0))
    g = pl.BlockSpec((1, MLA_W), lambda i: (0, 0))
    full = pl.BlockSpec((tm, d), lambda i: (i, 0))
    return pl.pallas_call(
        body, name="mix_out", grid=(rows // tm,),
        in_specs=[half, half, g, g, pl.BlockSpec((MLA_W + LRU_W, d), lambda i: (0, 0)), full],
        out_specs=[full, full],
        out_shape=[jax.ShapeDtypeStruct((rows, MLA_W + LRU_W), BF16), jax.ShapeDtypeStruct((rows, d), F32)],
        compiler_params=_params(("parallel",)))(ya, yl, ga, gl, wout, h)


def _mix_out_bwd_call(dhb, wout, ya, yl, ga, gl, tm):
    rows = ya.shape[0]
    d = dhb.shape[1]

    def body(dh_ref, w_ref, ya_ref, yl_ref, ga_ref, gl_ref, dya_ref, dyl_ref, dga_ref, dgl_ref):
        dy = _dot_nt(dh_ref[...], w_ref[...])
        outs = []
        for val, g_ref, lo, out_ref in ((ya_ref[...], ga_ref, 0, dya_ref), (yl_ref[...], gl_ref, MLA_W, dyl_ref)):
            r = _rms(val, MLA_W)
            n = val * r
            dyn = dy[:, lo:lo + MLA_W]
            out_ref[...] = _rms_bwd(dyn * g_ref[...], n, r, MLA_W)
            outs.append(jnp.sum(dyn * n, axis=0, keepdims=True))

        @pl.when(pl.program_id(0) == 0)
        def _():
            dga_ref[...] = outs[0]
            dgl_ref[...] = outs[1]

        @pl.when(pl.program_id(0) != 0)
        def _():
            dga_ref[...] += outs[0]
            dgl_ref[...] += outs[1]

    half = pl.BlockSpec((tm, MLA_W), lambda i: (i, 0))
    g = pl.BlockSpec((1, MLA_W), lambda i: (0, 0))
    return pl.pallas_call(
        body, name="mix_out_bwd", grid=(rows // tm,),
        in_specs=[pl.BlockSpec((tm, d), lambda i: (i, 0)), pl.BlockSpec((MLA_W + LRU_W, d), lambda i: (0, 0)),
                  half, half, g, g],
        out_specs=[half, half, g, g],
        out_shape=[jax.ShapeDtypeStruct((rows, MLA_W), F32), jax.ShapeDtypeStruct((rows, LRU_W), F32),
                   jax.ShapeDtypeStruct((1, MLA_W), F32), jax.ShapeDtypeStruct((1, LRU_W), F32)],
        compiler_params=_params(("arbitrary",)))(dhb, wout, ya, yl, ga, gl)


def _final_call(h, g, target, lp, tm):
    rows, d = h.shape
    tpe = lp // tm

    def body(h_ref, g_ref, t_ref, dh_ref, dhb_ref, dg_ref, loss_ref):
        i = pl.program_id(0)
        x = h_ref[...]
        g = g_ref[...]
        r = _rms(x, d)
        n = x * r
        row = (i % tpe) * tm + lax.broadcasted_iota(jnp.int32, (tm, 1), 0)
        err = jnp.where(row >= CHUNK, n * g - t_ref[...], 0.0)
        dout = err * (1.0 / d)
        dh = _rms_bwd(dout * g, n, r, d)
        dh_ref[...] = dh
        dhb_ref[...] = dh.astype(BF16)
        dg = jnp.sum(dout * n, axis=0, keepdims=True)
        part = jnp.sum(jnp.sum(err * err, axis=1, keepdims=True), axis=0, keepdims=True) * (0.5 / d)
        loss = jnp.broadcast_to(part, (1, 128))

        @pl.when(i == 0)
        def _():
            dg_ref[...] = dg
            loss_ref[...] = loss

        @pl.when(i != 0)
        def _():
            dg_ref[...] += dg
            loss_ref[...] += loss

    full = pl.BlockSpec((tm, d), lambda i: (i, 0))
    return pl.pallas_call(
        body, name="final_loss", grid=(rows // tm,),
        in_specs=[full, pl.BlockSpec((1, d), lambda i: (0, 0)), full],
        out_specs=[full, full, pl.BlockSpec((1, d), lambda i: (0, 0)), pl.BlockSpec((1, 128), lambda i: (0, 0))],
        out_shape=[jax.ShapeDtypeStruct((rows, d), F32), jax.ShapeDtypeStruct((rows, d), BF16),
                   jax.ShapeDtypeStruct((1, d), F32), jax.ShapeDtypeStruct((1, 128), F32)],
        compiler_params=_params(("arbitrary",)))(h, g, target)


def _ffn_dact_call(name, dhb, wd, gate, up, tm, comm=None):
    rows, d = dhb.shape
    ns, fs, _ = wd.shape

    def body(dh_ref, wd_ref, g_ref, p_ref, dg_ref, dp_ref):
        da = 0.5 * _dot_nt(dh_ref[...], wd_ref[0])
        g = g_ref[0].astype(F32)
        p = p_ref[0].astype(F32)
        sg = jax.nn.sigmoid(g)
        dg_ref[0] = (da * p * sg * (1.0 + g * (1.0 - sg))).astype(BF16)
        dp_ref[0] = (da * g * sg).astype(BF16)

    aspec = pl.BlockSpec((1, tm, fs), lambda s, i: (s, i, 0))
    oshape = jax.ShapeDtypeStruct((ns, rows, fs), BF16)
    return _hosted_call(
        body, name=name, grid=(ns, rows // tm),
        in_specs=[pl.BlockSpec((tm, d), lambda s, i: (i, 0)), pl.BlockSpec((1, fs, d), lambda s, i: (s, 0, 0)),
                  aspec, aspec],
        out_specs=[aspec, aspec], out_shape=[oshape, oshape],
        dims=("parallel", "parallel"), args=(dhb, wd, gate, up), comm=comm)


def _norm_in_bwd_call(name, pieces, h, g, dres, tm, comm=None):
    rows, d = h.shape
    npc = len(pieces)

    def body(*refs):
        d_refs = refs[0:2 * npc:2]
        w_refs = refs[1:2 * npc:2]
        h_ref, g_ref, dres_ref, dh_ref, dhb_ref, dg_ref = refs[2 * npc:]
        du = jnp.zeros((tm, d), F32)
        for d_ref, w_ref in zip(d_refs, w_refs):
            if len(d_ref.shape) == 3:
                for s in range(d_ref.shape[0]):
                    du = du + _dot(d_ref[s], w_ref[s])
            else:
                du = du + _dot(d_ref[...], w_ref[...])
        x = h_ref[...]
        r = _rms(x, d)
        n = x * r
        dh = dres_ref[...] + _rms_bwd(du * g_ref[...], n, r, d)
        dh_ref[...] = dh
        dhb_ref[...] = dh.astype(BF16)
        dg = jnp.sum(du * n, axis=0, keepdims=True)

        @pl.when(pl.program_id(0) == 0)
        def _():
            dg_ref[...] = dg

        @pl.when(pl.program_id(0) != 0)
        def _():
            dg_ref[...] += dg

    in_specs, args = [], []
    for dd, w in pieces:
        if dd.ndim == 3:
            in_specs.append(pl.BlockSpec((dd.shape[0], tm, dd.shape[2]), lambda i: (0, i, 0)))
            in_specs.append(pl.BlockSpec(w.shape, lambda i: (0, 0, 0)))
        else:
            in_specs.append(pl.BlockSpec((tm, dd.shape[1]), lambda i: (i, 0)))
            in_specs.append(pl.BlockSpec(w.shape, lambda i: (0, 0)))
        args += [dd, w]
    full = pl.BlockSpec((tm, d), lambda i: (i, 0))
    gspec = pl.BlockSpec((1, d), lambda i: (0, 0))
    return _hosted_call(
        body, name=name, grid=(rows // tm,),
        in_specs=in_specs + [full, gspec, full],
        out_specs=[full, full, gspec],
        out_shape=[jax.ShapeDtypeStruct((rows, d), F32), jax.ShapeDtypeStruct((rows, d), BF16),
                   jax.ShapeDtypeStruct((1, d), F32)],
        args=(*args, h, g, dres), comm=comm)


def _wgrad_call(name, a, b, scale=1.0):
    a3, b3 = a.ndim == 3, b.ndim == 3
    ns = a.shape[0] if a3 else (b.shape[0] if b3 else 1)
    rows, m = a.shape[-2:]
    n = b.shape[-1]
    tmm = m if a3 else _col_tile(m, 256)

    def body(a_ref, b_ref, o_ref):
        av = a_ref[0] if a3 else a_ref[...]
        bv = b_ref[0] if b3 else b_ref[...]
        res = _dot_tn(av, bv)
        if scale != 1.0:
            res = res * scale
        if a3 or b3:
            o_ref[0] = res
        else:
            o_ref[...] = res

    aspec = (pl.BlockSpec((1, rows, tmm), lambda s, j: (s, 0, j)) if a3
             else pl.BlockSpec((rows, tmm), lambda s, j: (0, j)))
    bspec = (pl.BlockSpec((1, rows, n), lambda s, j: (s, 0, 0)) if b3
             else pl.BlockSpec((rows, n), lambda s, j: (0, 0)))
    if a3 or b3:
        ospec = pl.BlockSpec((1, tmm, n), lambda s, j: (s, j, 0))
        oshape = jax.ShapeDtypeStruct((ns, m, n), F32)
    else:
        ospec = pl.BlockSpec((tmm, n), lambda s, j: (j, 0))
        oshape = jax.ShapeDtypeStruct((m, n), F32)
    return pl.pallas_call(
        body, name=name, grid=(ns, m // tmm), in_specs=[aspec, bspec], out_specs=ospec, out_shape=oshape,
        compiler_params=_params(("parallel", "parallel")))(a, b)


def _mla_prep_bwd_call(z, dq, dk, dv, gql, gkvl, gqh, gkh, wuq, wuk, wuv, tabs, lp, tm):
    rows = z.shape[0]
    tpe = lp // tm

    def body(z_ref, dq_ref, dk_ref, dv_ref, gql_ref, gkvl_ref, gqh_ref, gkh_ref, wuq_ref, wuk_ref, wuv_ref,
             c_ref, s1_ref, s2_ref, dz_ref, dqp_ref, dkn_ref, dvv_ref, dgql_ref, dgkvl_ref, dgqh_ref, dgkh_ref):
        gql, gkvl = gql_ref[...], gkvl_ref[...]
        gq, gk = gqh_ref[...], gkh_ref[...]
        wuq, wuk, wuv = wuq_ref[...], wuk_ref[...], wuv_ref[...]
        m = _mla_heads(z_ref[...], gql, gkvl, wuq, wuk, wuv)
        c, s1, s2 = c_ref[...], s1_ref[...], s2_ref[...]
        dgq = jnp.zeros((1, D_QKP), F32)
        dgk = jnp.zeros((1, D_QKP), F32)
        dkr = jnp.zeros((tm, D_QKP - D_NOPE), F32)
        for hd in range(HEADS):
            qn, rqh, knn, krn, rkh = m["heads"][hd]
            dqg = jnp.concatenate([dq_ref[hd, :, 0:D_NOPE].astype(F32),
                                   _rope_t(dq_ref[hd, :, D_NOPE:D_QKP].astype(F32), c, s1, s2)], axis=1)
            dgq = dgq + jnp.sum(dqg * qn, axis=0, keepdims=True)
            dqn = dqg * gq
            dqr = rqh * (dqn - qn * (jnp.sum(dqn * qn, axis=-1, keepdims=True) * (1.0 / D_QK)))
            dqp_ref[:, hd * D_QKP:(hd + 1) * D_QKP] = dqr.astype(BF16)
            kn_full = jnp.concatenate([knn, krn], axis=1)
            dkg = jnp.concatenate([dk_ref[hd, :, 0:D_NOPE].astype(F32),
                                   _rope_t(dk_ref[hd, :, D_NOPE:D_QKP].astype(F32), c, s1, s2)], axis=1)
            dgk = dgk + jnp.sum(dkg * kn_full, axis=0, keepdims=True)
            dkn = dkg * gk
            dkraw = rkh * (dkn - kn_full * (jnp.sum(dkn * kn_full, axis=-1, keepdims=True) * (1.0 / D_QK)))
            dkn_ref[:, hd * D_NOPE:(hd + 1) * D_NOPE] = dkraw[:, 0:D_NOPE].astype(BF16)
            dkr = dkr + dkraw[:, D_NOPE:D_QKP]
            dvv_ref[:, hd * D_V:(hd + 1) * D_V] = dv_ref[hd]
        dcqn = _dot(dqp_ref[...], wuq)
        dckvn = _dot_nt(dkn_ref[...], wuk) + _dot_nt(dvv_ref[...], wuv)
        dz_ref[:, 0:Q_RANK] = _rms_bwd(dcqn * gql, m["nq"], m["rq"], Q_RANK).astype(BF16)
        dz_ref[:, Q_RANK:Z_KR] = _rms_bwd(dckvn * gkvl, m["nkv"], m["rkv"], KV_RANK).astype(BF16)
        dz_ref[:, Z_KR:Z_MLA] = dkr.astype(BF16)
        parts = [(dgql_ref, jnp.sum(dcqn * m["nq"], axis=0, keepdims=True)),
                 (dgkvl_ref, jnp.sum(dckvn * m["nkv"], axis=0, keepdims=True)), (dgqh_ref, dgq), (dgkh_ref, dgk)]

        @pl.when(pl.program_id(0) == 0)
        def _():
            for ref, val in parts:
                ref[...] = val

        @pl.when(pl.program_id(0) != 0)
        def _():
            for ref, val in parts:
                ref[...] += val

    def const(shape):
        return pl.BlockSpec(shape, lambda i: tuple(0 for _ in shape))

    tab = pl.BlockSpec((tm, 128), lambda i: (i % tpe, 0))
    hq = pl.BlockSpec((HEADS, tm, D_QKP), lambda i: (0, i, 0))
    hv = pl.BlockSpec((HEADS, tm, D_V), lambda i: (0, i, 0))

    def rowspec(n):
        return pl.BlockSpec((tm, n), lambda i: (i, 0))

    return pl.pallas_call(
        body, name="mla_prep_bwd", grid=(rows // tm,),
        in_specs=[rowspec(Z_MLA), hq, hq, hv, const((1, Q_RANK)), const((1, KV_RANK)), const((1, D_QKP)),
                  const((1, D_QKP)), const((HEADS * D_QKP, Q_RANK)), const((KV_RANK, HEADS * D_NOPE)),
                  const((KV_RANK, HEADS * D_V)), tab, tab, tab],
        out_specs=[rowspec(Z_MLA), rowspec(HEADS * D_QKP), rowspec(HEADS * D_NOPE), rowspec(HEADS * D_V),
                   const((1, Q_RANK)), const((1, KV_RANK)), const((1, D_QKP)), const((1, D_QKP))],
        out_shape=[jax.ShapeDtypeStruct((rows, Z_MLA), BF16), jax.ShapeDtypeStruct((rows, HEADS * D_QKP), BF16),
                   jax.ShapeDtypeStruct((rows, HEADS * D_NOPE), BF16), jax.ShapeDtypeStruct((rows, HEADS * D_V), BF16),
                   jax.ShapeDtypeStruct((1, Q_RANK), F32), jax.ShapeDtypeStruct((1, KV_RANK), F32),
                   jax.ShapeDtypeStruct((1, D_QKP), F32), jax.ShapeDtypeStruct((1, D_QKP), F32)],
        compiler_params=_params(("arbitrary",)))(z, dq, dk, dv, gql, gkvl, gqh, gkh, wuq, wuk, wuv, *tabs)


def _local_step(h0, target, w, nb, lp, sched=None):
    tm = _row_tile(lp, 1024)
    te = _row_tile(lp, 512)
    tabs = _rope_tables(lp)
    g = {}
    if sched is None:
        host = lambda stage: None
    else:
        sched.g = g
        host = sched.host

    def ffn_fwd(tag, h):
        u = _rmsnorm_call(tag + "_norm", h, w[tag + "_norm"], te)
        gate, up, act = _ffn_up_call(tag + "_up", u, w[tag + "_w_gate"], w[tag + "_w_up"], tm, host(tag + "_up"))
        return _ffn_down_call(tag + "_down", act, w[tag + "_w_down"], h, te), (u, gate, up, act)

    def ffn_bwd(tag, h, saved, dh, dhb):
        u, gate, up, act = saved
        dgate, dup = _ffn_dact_call(tag + "_dact", dhb, w[tag + "_w_down"], gate, up, tm, host(tag + "_dact"))
        g[tag + "_w_down"] = _wgrad_call(tag + "_dwd", act, dhb, 0.5)
        g[tag + "_w_gate"] = _wgrad_call(tag + "_dwg", dgate, u)
        g[tag + "_w_up"] = _wgrad_call(tag + "_dwu", dup, u)
        dh_in, dhb_in, g[tag + "_norm"] = _norm_in_bwd_call(
            tag + "_din", [(dgate, w[tag + "_w_gate"]), (dup, w[tag + "_w_up"])], h, w[tag + "_norm"], dh, te,
            host(tag + "_din"))
        return dh_in, dhb_in

    h1, s1 = ffn_fwd("ffn1", h0)
    un = _rmsnorm_call("mix_norm", h1, w["mix_norm"], te)
    z = _mm_call("mix_in", un, w["w_in"], tm, F32)
    mla_w = (w["q_latent_norm"], w["kv_latent_norm"], w["q_head_norm"], w["k_head_norm"], w["w_uq"], w["w_uk"],
             w["w_uv"])
    q, k, v, cqn, ckvn = _mla_prep_call(z, *mla_w, tabs, lp, te)
    o, lse = _attn_fwd_call(q, k, v, nb, lp, host("attn_fwd"))
    lru_w = (w["conv_w"], w["conv_b"], w["gate_a_w"], w["gate_a_b"], w["gate_x_w"], w["gate_x_b"], w["lru_lambda"])
    yl, hs = _lru_fwd_call(z, *lru_w, nb, lp, host("lru_fwd"))
    y, h2 = _mix_out_call(o, yl, w["attn_out_norm"], w["lru_out_norm"], w["w_out"], h1, te)
    h3, s2 = ffn_fwd("ffn2", h2)
    dh3, dh3b, g["final_norm"], loss = _final_call(h3, w["final_norm"], target, lp, te)

    dh2, dh2b = ffn_bwd("ffn2", h2, s2, dh3, dh3b)
    g["w_out"] = _wgrad_call("dw_out", y, dh2b)
    dya, dyl, g["attn_out_norm"], g["lru_out_norm"] = _mix_out_bwd_call(
        dh2b, w["w_out"], o, yl, w["attn_out_norm"], w["lru_out_norm"], te)
    dq, dk, dv = _attn_bwd_call(q, k, v, o, lse, dya, nb, lp)
    (dz_mla, dqp, dkn, dvv, g["q_latent_norm"], g["kv_latent_norm"], g["q_head_norm"],
     g["k_head_norm"]) = _mla_prep_bwd_call(z, dq, dk, dv, *mla_w, tabs, lp, te)
    g["w_uq"] = _wgrad_call("dw_uq", dqp, cqn)
    g["w_uk"] = _wgrad_call("dw_uk", ckvn, dkn)
    g["w_uv"] = _wgrad_call("dw_uv", ckvn, dvv)
    (du, dgt, g["conv_w"], g["conv_b"], g["gate_a_w"], g["gate_a_b"], g["gate_x_w"], g["gate_x_b"],
     g["lru_lambda"]) = _lru_bwd_call(z, hs, dyl, *lru_w, nb, lp, host("lru_bwd"))
    win = w["w_in"]
    g["w_in"] = jnp.concatenate(
        [_wgrad_call("dw_in_mla", dz_mla, un), _wgrad_call("dw_in_u", du, un), _wgrad_call("dw_in_g", dgt, un)],
        axis=0)
    dh1, dh1b, g["mix_norm"] = _norm_in_bwd_call(
        "mix_din", [(dz_mla, win[0:Z_MLA]), (du, win[Z_U:Z_G]), (dgt, win[Z_G:Z_W])], h1, w["mix_norm"], dh2, te,
        host("mix_din"))
    dh0, _ = ffn_bwd("ffn1", h0, s1, dh1, dh1b)
    return loss, dh0, g


def _place():
    x, y, c = lax.axis_index("x"), lax.axis_index("y"), lax.axis_index("c")
    return x, y, c, [(1 - x, y), (x, 1 - y), (1 - x, 1 - y)]


def _any_specs(n):
    return [pl.BlockSpec(memory_space=pl.ANY)] * n


def _remote(src, dst, sems, k, dev):
    send_sems, recv_sems = sems
    return pltpu.make_async_remote_copy(src_ref=src, dst_ref=dst, send_sem=send_sems.at[k], recv_sem=recv_sems.at[k],
                                        device_id=dev, device_id_type=MESH)


EW_VMEM_BYTES = 24 * 1024 * 1024


def _fit_rows(rows, cols, blocks):
    return _row_tile(rows, max(16, int(EW_VMEM_BYTES // (8 * blocks)) // cols))


class _Geom:
    def __init__(self, n0, n1, blocks=1.0):
        self.n0, self.n1 = n0, n1
        self.axis = 0 if n0 % 32 == 0 else 1
        self.h0, self.h1 = (n0 // 2, n1) if self.axis == 0 else (n0, n1 // 2)
        self.tr = _fit_rows(self.h0, self.h1, blocks)
        self.nblk = self.h0 // self.tr

    def half_ref(self, ref, lead, idx):
        if self.axis == 0:
            return ref.at[(*lead, pl.ds(idx * self.h0, self.h0))]
        return ref.at[(*lead, slice(None), pl.ds(idx * self.h1, self.h1))]

    def half_block(self, lead, i, idx):
        return (*lead, idx * self.nblk + i, 0) if self.axis == 0 else (*lead, i, idx)


class _Comm:
    def __init__(self, ins, out_shapes, aliases, n_sems, start, finish, deliver):
        self.ins, self.out_shapes, self.aliases, self.n_sems = list(ins), list(out_shapes), dict(aliases), n_sems
        self.start, self.finish, self.deliver = start, finish, deliver

    def scratch(self):
        return [pltpu.SemaphoreType.DMA((self.n_sems,)), pltpu.SemaphoreType.DMA((self.n_sems,))]


def _comm_call(name, comm):
    n_in = len(comm.ins)

    def body(*refs):
        ins, outs, sems = refs[:n_in], refs[n_in:-2], refs[-2:]
        comm.start(ins, outs, sems)
        comm.finish(ins, outs, sems)

    res = pl.pallas_call(
        body, name=name, out_shape=comm.out_shapes, in_specs=_any_specs(n_in),
        out_specs=_any_specs(len(comm.out_shapes)), input_output_aliases=comm.aliases,
        scratch_shapes=comm.scratch())(*comm.ins)
    return comm.deliver(list(res))


def _hosted_call(body, *, name, grid, in_specs, out_specs, out_shape, args, scratch_shapes=(), dims=None, comm=None):
    in_specs, out_specs, out_shape = list(in_specs), list(out_specs), list(out_shape)
    if comm is None:
        return pl.pallas_call(
            body, name=name, grid=grid, in_specs=in_specs, out_specs=out_specs, out_shape=out_shape,
            scratch_shapes=list(scratch_shapes),
            compiler_params=_params(dims or ("arbitrary",) * len(grid)))(*args)
    n_in, n_out, n_ci, n_co = len(in_specs), len(out_specs), len(comm.ins), len(comm.out_shapes)

    def wrapped(*refs):
        ins, cins = refs[:n_in], refs[n_in:n_in + n_ci]
        outs = refs[n_in + n_ci:n_in + n_ci + n_out]
        couts = refs[n_in + n_ci + n_out:n_in + n_ci + n_out + n_co]
        scratch, sems = refs[n_in + n_ci + n_out + n_co:-2], refs[-2:]
        first = functools.reduce(jnp.logical_and, [pl.program_id(k) == 0 for k in range(len(grid))])
        last = functools.reduce(jnp.logical_and, [pl.program_id(k) == grid[k] - 1 for k in range(len(grid))])

        @pl.when(first)
        def _():
            comm.start(cins, couts, sems)

        body(*ins, *outs, *scratch)

        @pl.when(last)
        def _():
            comm.finish(cins, couts, sems)

    res = pl.pallas_call(
        wrapped, name=name, grid=grid, in_specs=in_specs + _any_specs(n_ci), out_specs=out_specs + _any_specs(n_co),
        out_shape=out_shape + comm.out_shapes,
        input_output_aliases={n_in + i: n_out + o for i, o in comm.aliases.items()},
        scratch_shapes=list(scratch_shapes) + comm.scratch(),
        compiler_params=_params(("arbitrary",) * len(grid)))(*args, *comm.ins)
    comm.deliver(list(res[n_out:]))
    return list(res[:n_out])


def _gather_comm(bufs, deliver):
    n = len(bufs)
    geoms = [_Geom(*b.shape[1:]) for b in bufs]

    def first(outs, sems):
        x, y, c, chips = _place()
        cps = []
        for a in range(n):
            mine = geoms[a].half_ref(outs[a], (2 * x + y,), c)
            cps += [_remote(mine, mine, sems, 6 * a + j, (cx, cy, c)) for j, (cx, cy) in enumerate(chips)]
        return cps

    def start(ins, outs, sems):
        for cp in first(outs, sems):
            cp.start()

    def finish(ins, outs, sems):
        x, y, c, chips = _place()
        sib = (x, y, 1 - c)
        passed = []
        for a in range(n):
            for j, (cx, cy) in enumerate(chips):
                land = geoms[a].half_ref(outs[a], (2 * cx + cy,), c)
                _remote(land, land, sems, 6 * a + j, sib).wait_recv()
                cp = _remote(land, land, sems, 6 * a + 3 + j, sib)
                cp.start()
                passed.append(cp)
        for a in range(n):
            for j, (cx, cy) in enumerate(chips):
                land = geoms[a].half_ref(outs[a], (2 * cx + cy,), 1 - c)
                _remote(land, land, sems, 6 * a + 3 + j, sib).wait_recv()
        for cp in first(outs, sems) + passed:
            cp.wait_send()

    return _Comm(bufs, [jax.ShapeDtypeStruct(b.shape, b.dtype) for b in bufs], {a: a for a in range(n)}, 6 * n,
                 start, finish, deliver)


def _reduce_pair_comm(grads, deliver):
    n = len(grads)
    geoms = [_Geom(*a.shape[1:]) for a in grads]

    def copies(ins, outs, sems):
        x, y, c, _ = _place()
        return [_remote(geoms[a].half_ref(ins[a], (slice(None),), 1 - c), outs[a], sems, a, (x, y, 1 - c))
                for a in range(n)]

    def start(ins, outs, sems):
        for cp in copies(ins, outs, sems):
            cp.start()

    def finish(ins, outs, sems):
        cps = copies(ins, outs, sems)
        for cp in cps:
            cp.wait_recv()
        for cp in cps:
            cp.wait_send()

    shapes = [jax.ShapeDtypeStruct((N_SHARD, g.h0, g.h1), a.dtype) for a, g in zip(grads, geoms)]
    return _Comm(grads, shapes, {}, n, start, finish, deliver)


def _reduce_chips_comm(parts, deliver):
    n = len(parts)

    def copies(ins, outs, sems):
        x, y, c, chips = _place()
        return [_remote(ins[a].at[2 * cx + cy], outs[a].at[j], sems, 3 * a + j, (cx, cy, c))
                for a in range(n) for j, (cx, cy) in enumerate(chips)]

    def start(ins, outs, sems):
        for cp in copies(ins, outs, sems):
            cp.start()

    def finish(ins, outs, sems):
        cps = copies(ins, outs, sems)
        for cp in cps:
            cp.wait_recv()
        for cp in cps:
            cp.wait_send()

    shapes = [jax.ShapeDtypeStruct((3,) + a.shape[1:], a.dtype) for a in parts]
    return _Comm(parts, shapes, {}, 3 * n, start, finish, deliver)


def _share_pair_comm(bufs, deliver):
    n = len(bufs)
    geoms = [_Geom(*b.shape) for b in bufs]

    def copies(outs, sems):
        x, y, c, _ = _place()
        cps = []
        for a in range(n):
            mine = geoms[a].half_ref(outs[a], (), c)
            cps.append(_remote(mine, mine, sems, a, (x, y, 1 - c)))
        return cps

    def start(ins, outs, sems):
        for cp in copies(outs, sems):
            cp.start()

    def finish(ins, outs, sems):
        x, y, c, _ = _place()
        for a in range(n):
            land = geoms[a].half_ref(outs[a], (), 1 - c)
            _remote(land, land, sems, a, (x, y, 1 - c)).wait_recv()
        for cp in copies(outs, sems):
            cp.wait_send()

    return _Comm(bufs, [jax.ShapeDtypeStruct(b.shape, b.dtype) for b in bufs], {a: a for a in range(n)}, n,
                 start, finish, deliver)


def _gather_small_call(pack):
    r, d = pack.shape

    def body(in_ref, out_ref, send_sems, recv_sems):
        x, y, c, _ = _place()
        me = 4 * x + 2 * y + c
        out_ref[me] = in_ref[...]
        cps = []
        for k in range(1, 8):
            bx, by, bc = (k >> 2) & 1, (k >> 1) & 1, k & 1
            px, py, pc = x ^ bx, y ^ by, c ^ bc
            cp = pltpu.make_async_remote_copy(
                src_ref=in_ref, dst_ref=out_ref.at[me], send_sem=send_sems.at[k - 1], recv_sem=recv_sems.at[k - 1],
                device_id=(px, py, pc), device_id_type=MESH)
            cp.start()
            cps.append((cp, 4 * px + 2 * py + pc))
        for k, (cp, peer) in enumerate(cps):
            land = out_ref.at[peer]
            pltpu.make_async_remote_copy(
                src_ref=land, dst_ref=land, send_sem=send_sems.at[k], recv_sem=recv_sems.at[k],
                device_id=(x, y, c), device_id_type=MESH).wait_recv()
        for cp, _ in cps:
            cp.wait_send()

    return pl.pallas_call(
        body, name="gather_small",
        out_shape=jax.ShapeDtypeStruct((8, r, d), pack.dtype),
        in_specs=[pl.BlockSpec(memory_space=pltpu.VMEM)], out_specs=pl.BlockSpec(memory_space=pltpu.VMEM),
        scratch_shapes=[pltpu.SemaphoreType.DMA((7,)), pltpu.SemaphoreType.DMA((7,))])(pack)


def _ew_call(name, fn, ins, out_dtypes):
    shape = ins[0].shape
    cols = shape[-1]
    rows = 1
    for s_ in shape[:-1]:
        rows *= s_
    ins2 = [a.reshape(rows, cols) for a in ins]
    tr = rows
    for t in range(16, min(rows, max(16, (1 << 19) // cols)) + 1, 16):
        if rows % t == 0:
            tr = t
    no = len(out_dtypes)

    def body(*refs):
        outs = fn(*[r[...] for r in refs[:len(ins2)]])
        for ref, val in zip(refs[len(ins2):], outs):
            ref[...] = val.astype(ref.dtype)

    spec = pl.BlockSpec((tr, cols), lambda i: (i, 0))
    res = pl.pallas_call(
        body, name=name, grid=(rows // tr,), in_specs=[spec] * len(ins2), out_specs=[spec] * no,
        out_shape=[jax.ShapeDtypeStruct((rows, cols), dt) for dt in out_dtypes],
        compiler_params=_params(("parallel",)))(*ins2)
    return [r.reshape(shape) for r in res]


def _adamw_math(w, g, m, v):
    m = ADAM_B1 * m + (1.0 - ADAM_B1) * g
    v = ADAM_B2 * v + (1.0 - ADAM_B2) * (g * g)
    m_hat = m / (1.0 - ADAM_B1 ** ADAM_STEP)
    v_hat = v / (1.0 - ADAM_B2 ** ADAM_STEP)
    delta = -ADAM_LR * (m_hat / (jnp.sqrt(v_hat) + ADAM_EPS) + ADAM_WD * w)
    return delta, m, v


def _adamw_call(name, w, g, m, v):
    return _ew_call(name, _adamw_math, [w, g, m, v], [F32, F32, F32])


def _tiled_call(name, fn, place, grid, in_items, out_items):
    ni = len(in_items)

    def body(place_ref, *refs):
        vals = fn(*[r[...] for r in refs[:ni]])
        for ref, val in zip(refs[ni:], vals):
            ref[...] = val.astype(ref.dtype)

    spec = pltpu.PrefetchScalarGridSpec(
        num_scalar_prefetch=1, grid=grid,
        in_specs=[pl.BlockSpec(blk, imap) for _, blk, imap in in_items],
        out_specs=[pl.BlockSpec(blk, imap) for _, _, blk, imap in out_items])
    return pl.pallas_call(
        body, name=name, grid_spec=spec,
        out_shape=[jax.ShapeDtypeStruct(shp, dt) for shp, dt, _, _ in out_items],
        compiler_params=_params(("arbitrary",) * len(grid)))(place, *[a for a, _, _ in in_items])


def _cast_call(name, place, shards):
    n0, n1 = shards[0].shape
    tr = _fit_rows(n0, n1, 1.5 * len(shards))
    ins = [(a, (tr, n1), lambda i, p: (i, 0)) for a in shards]
    outs = [((N_SHARD, n0, n1), BF16, (1, tr, n1), lambda i, p: (p[0], i, 0)) for _ in shards]
    return _tiled_call(name, lambda *v: [x[None] for x in v], place, (n0 // tr,), ins, outs)


def _pair_sum_call(name, place, fulls, gots):
    k = len(fulls)
    g = _Geom(*fulls[0].shape[1:], blocks=2.5 * k)
    blk = (1, g.tr, g.h1)
    ins = [(a, blk, lambda s, i, p: g.half_block((s,), i, p[1])) for a in fulls]
    ins += [(a, blk, lambda s, i, p: (s, i, 0)) for a in gots]
    outs = [((N_SHARD, g.h0, g.h1), BF16, blk, lambda s, i, p: (s, i, 0)) for _ in fulls]
    return _tiled_call(name, lambda *v: [v[j] + v[k + j] for j in range(k)], place, (N_SHARD, g.nblk), ins, outs)


def _chip_sum_call(name, place, fulls, gots, recvs):
    k = len(fulls)
    g = _Geom(*fulls[0].shape[1:], blocks=4.5 * k)
    blk = (1, g.tr, g.h1)
    ins = [(a, blk, lambda i, p: g.half_block((p[0],), i, p[1])) for a in fulls]
    ins += [(a, blk, lambda i, p: (p[0], i, 0)) for a in gots]
    ins += [(a, (3, g.tr, g.h1), lambda i, p: (0, i, 0)) for a in recvs]
    outs = [((g.n0, g.n1), F32, (g.tr, g.h1), lambda i, p: g.half_block((), i, p[1])) for _ in fulls]

    def fn(*v):
        res = []
        for j in range(k):
            r = v[2 * k + j].astype(F32)
            res.append(v[j][0] + v[k + j][0] + r[0] + r[1] + r[2])
        return res

    return _tiled_call(name, fn, place, (g.nblk,), ins, outs)


def _adamw_group_call(name, ws, gs, ms, vs, comm=None):
    k = len(ws)
    n0, n1 = ws[0].shape
    tr = _fit_rows(n0, n1, 8 * k)
    spec = pl.BlockSpec((tr, n1), lambda i: (i, 0))

    def body(*refs):
        for j in range(k):
            g = refs[k + j][...]
            delta, m, vv = _adamw_math(refs[j][...], g, refs[2 * k + j][...], refs[3 * k + j][...])
            for ref, val in zip(refs[4 * k + 4 * j:4 * k + 4 * j + 4], (g, delta, m, vv)):
                ref[...] = val

    flat = _hosted_call(
        body, name=name, grid=(n0 // tr,), in_specs=[spec] * (4 * k), out_specs=[spec] * (4 * k),
        out_shape=[jax.ShapeDtypeStruct((n0, n1), F32)] * (4 * k), dims=("parallel",),
        args=(*ws, *gs, *ms, *vs), comm=comm)
    return [flat[4 * j:4 * j + 4] for j in range(k)]


def _small_update_call(gathered, wp, mp, vp):
    nd, r, d = gathered.shape

    def body(g_ref, w_ref, m_ref, v_ref, gs_ref, d_ref, nm_ref, nv_ref):
        gs = g_ref[0]
        for k in range(1, nd):
            gs = gs + g_ref[k]
        gs_ref[...] = gs
        delta, m, v = _adamw_math(w_ref[...], gs[0:SMALL_ADAM_ROWS], m_ref[...], v_ref[...])
        d_ref[...] = delta
        nm_ref[...] = m
        nv_ref[...] = v

    vm = pl.BlockSpec(memory_space=pltpu.VMEM)
    ashape = jax.ShapeDtypeStruct((SMALL_ADAM_ROWS, d), F32)
    return pl.pallas_call(
        body, name="small_update", in_specs=[vm] * 4, out_specs=[vm] * 4,
        out_shape=[jax.ShapeDtypeStruct((r, d), F32), ashape, ashape, ashape],
        compiler_params=pltpu.CompilerParams(vmem_limit_bytes=VMEM_LIMIT_BYTES))(gathered, wp, mp, vp)


SMALL_NAMES = ["ffn1_norm", "mix_norm", "ffn2_norm", "final_norm", "q_latent_norm", "kv_latent_norm",
               "q_head_norm", "k_head_norm", "conv_b", "gate_a_b", "gate_x_b", "lru_lambda", "attn_out_norm",
               "lru_out_norm"]
ROW_CONV_W = 14
ROW_GATE_A = 16
ROW_GATE_X = 48
ROW_META = 80
ROW_LOSS = 96


def _row(a):
    flat = a.reshape(1, -1)
    return jnp.pad(flat, ((0, 0), (0, D_MODEL - flat.shape[1])))


def _pack_small(t, rows):
    parts = [_row(t[nm]) for nm in SMALL_NAMES]
    parts.append(t["conv_w"].reshape(2, D_MODEL))
    parts.append(t["gate_a_w"].reshape(32, D_MODEL))
    parts.append(t["gate_x_w"].reshape(32, D_MODEL))
    p = jnp.concatenate(parts, axis=0)
    return jnp.pad(p, ((0, rows - p.shape[0]), (0, 0)))


def _unpack_small(p, like):
    out = {}
    for k, nm in enumerate(SMALL_NAMES):
        out[nm] = p[k, 0:like[nm].size].reshape(like[nm].shape)
    out["gate_a_w"] = p[ROW_GATE_A:ROW_GATE_A + 32].reshape(like["gate_a_w"].shape)
    out["gate_x_w"] = p[ROW_GATE_X:ROW_GATE_X + 32].reshape(like["gate_x_w"].shape)
    return out


def _gate_dense(wg):
    w4 = wg[0].reshape(N_LRU_TILES, 2, 64, 64)
    zero = jnp.zeros((N_LRU_TILES, 64, 64), wg.dtype)
    top = jnp.concatenate([w4[:, 0], zero], axis=2)
    bot = jnp.concatenate([zero, w4[:, 1]], axis=2)
    return jnp.concatenate([top, bot], axis=1).astype(BF16)


def _gate_blocks(dw):
    return jnp.stack([dw[:, 0:64, 0:64], dw[:, 64:128, 64:128]], axis=1).reshape(8, 64, 64)


BIG_NAMES = ["ffn1_w_gate", "ffn1_w_up", "ffn1_w_down", "w_in", "w_uq", "w_uk", "w_uv", "w_out", "ffn2_w_gate",
             "ffn2_w_up", "ffn2_w_down"]
BIG_GROUPS = [["ffn1_w_gate", "ffn1_w_up", "ffn1_w_down", "ffn2_w_gate", "ffn2_w_up", "ffn2_w_down"], ["w_in"],
              ["w_uq"], ["w_uk", "w_uv"], ["w_out"]]
TRANSPOSED = ("ffn1_w_gate", "ffn1_w_up", "ffn2_w_gate", "ffn2_w_up", "w_in", "w_uq")


def _to2d(nm, a):
    return a[0].T if nm in TRANSPOSED else a[0]


def _from2d(nm, a):
    return (a.T if nm in TRANSPOSED else a)[None]


WEIGHT_NAMES = ["meta_tokens", "ffn1_norm", "ffn1_w_gate", "ffn1_w_up", "ffn1_w_down", "mix_norm", "w_in",
                "q_latent_norm", "w_uq", "kv_latent_norm", "w_uk", "w_uv", "q_head_norm", "k_head_norm", "conv_w",
                "conv_b", "gate_a_w", "gate_a_b", "gate_x_w", "gate_x_b", "lru_lambda", "attn_out_norm",
                "lru_out_norm", "w_out", "ffn2_norm", "ffn2_w_gate", "ffn2_w_up", "ffn2_w_down", "final_norm"]


def _weight_from(nm, slots):
    if nm == "w_in":
        win = slots.reshape(IN_WIDTH, D_MODEL)
        return jnp.concatenate([win[0:Z_KR + D_ROPE], jnp.zeros((128 - D_ROPE, D_MODEL), BF16),
                                win[Z_KR + D_ROPE:]], axis=0)
    if nm == "w_uq":
        return jnp.pad(slots, ((0, 0), (0, D_QKP - D_QK), (0, 0))).reshape(HEADS * D_QKP, Q_RANK)
    if nm in ("w_uk", "w_uv"):
        return slots.transpose(1, 0, 2).reshape(KV_RANK, HEADS * D_NOPE)
    if nm == "w_out":
        return slots.reshape(D_MODEL, D_MODEL)
    return slots


def _small_weights(p, small):
    w = {nm: p[nm] for nm in SMALL_NAMES}
    w["q_head_norm"] = jnp.pad(p["q_head_norm"], ((0, 0), (0, D_QKP - D_QK)))
    w["k_head_norm"] = jnp.pad(p["k_head_norm"], ((0, 0), (0, D_QKP - D_QK)))
    w["conv_w"] = small[:, N_META:N_META + 2, :].reshape(N_SHARD, CONV_K, LRU_TILE)
    w["gate_a_w"] = _gate_dense(p["gate_a_w"])
    w["gate_x_w"] = _gate_dense(p["gate_x_w"])
    meta = small[:, 0:N_META, :].transpose(1, 0, 2).reshape(N_META, D_MODEL)
    return w, meta


def _full_weights(p, gathered, small):
    w, meta = _small_weights(p, small)
    w.update({nm: _weight_from(nm, gathered[nm]) for nm in BIG_NAMES})
    return w, meta


def _shard_grad(nm, g):
    if nm == "w_in":
        return jnp.concatenate([g[0:Z_KR + D_ROPE], g[Z_MLA:]], axis=0).reshape(N_SHARD, IN_WIDTH // N_SHARD, D_MODEL)
    if nm == "w_uq":
        return g.reshape(HEADS, D_QKP, Q_RANK)[:, 0:D_QK, :]
    if nm in ("w_uk", "w_uv"):
        return g.reshape(KV_RANK, HEADS, D_NOPE).transpose(1, 0, 2)
    if nm == "w_out":
        return g.reshape(N_SHARD, D_MODEL // N_SHARD, D_MODEL)
    return g


def _shard_grads(g):
    return {nm: _shard_grad(nm, g[nm]) for nm in BIG_NAMES}


GATHER_FIRST = ["ffn1_w_gate", "ffn1_w_up", "ffn1_w_down"]
GATHER_AT = {"ffn1_up": ["w_in", "w_uq", "w_uk", "w_uv", "w_out"], "attn_fwd": ["ffn2_w_down"],
             "lru_fwd": ["ffn2_w_gate", "ffn2_w_up"]}
REDUCE_PLAN = [(["ffn2_w_gate", "ffn2_w_up", "ffn2_w_down"], "ffn2_din", "lru_bwd"),
               (["w_out", "w_uq", "w_uk", "w_uv", "w_in"], "mix_din", "ffn1_dact"),
               (["ffn1_w_gate", "ffn1_w_up", "ffn1_w_down"], "ffn1_din", None)]


def _same_shape_groups(names):
    return [[nm for nm in grp if nm in names] for grp in BIG_GROUPS if any(nm in names for nm in grp)]


class _Sched:
    def __init__(self, place, w, slots):
        self.place, self.w, self.slots = place, w, slots
        self.g = None
        self.sharded, self.from_pair, self.chip_bf16, self.from_chips = {}, {}, {}, {}

    def host(self, stage):
        if stage in GATHER_AT:
            return self.gather(GATHER_AT[stage])
        for names, pair_at, chips_at in REDUCE_PLAN:
            if stage == pair_at:
                return self.pair(names)
            if stage == chips_at:
                return self.chips(names)
        return None

    def gather(self, names):
        def deliver(outs):
            self.w.update({nm: _weight_from(nm, o) for nm, o in zip(names, outs)})
            return outs

        return _gather_comm([self.slots[nm] for nm in names], deliver)

    def pair(self, names):
        self.sharded.update({nm: _shard_grad(nm, self.g[nm]) for nm in names})

        def deliver(outs):
            self.from_pair.update(zip(names, outs))
            for grp in _same_shape_groups(names):
                sums = _pair_sum_call("pair_sum_" + grp[0], self.place, [self.sharded[nm] for nm in grp],
                                      [self.from_pair[nm] for nm in grp])
                self.chip_bf16.update(zip(grp, sums))
            return outs

        return _reduce_pair_comm([self.sharded[nm] for nm in names], deliver)

    def chips(self, names):
        def deliver(outs):
            self.from_chips.update(zip(names, outs))
            return outs

        return _reduce_chips_comm([self.chip_bf16[nm] for nm in names], deliver)

    def chip_sums(self, names):
        out = {}
        for grp in _same_shape_groups(names):
            sums = _chip_sum_call("chip_sum_" + grp[0], self.place, [self.sharded[nm] for nm in grp],
                                  [self.from_pair[nm] for nm in grp], [self.from_chips[nm] for nm in grp])
            out.update(zip(grp, sums))
        return out


def kernel(x, meta_tokens, ffn1_norm, ffn1_w_gate, ffn1_w_up, ffn1_w_down, mix_norm, w_in, q_latent_norm, w_uq, kv_latent_norm, w_uk, w_uv, q_head_norm, k_head_norm, conv_w, conv_b, gate_a_w, gate_a_b, gate_x_w, gate_x_b, lru_lambda, attn_out_norm, lru_out_norm, w_out, ffn2_norm, ffn2_w_gate, ffn2_w_up, ffn2_w_down, final_norm, loss_target, m_meta_tokens, m_ffn1_norm, m_ffn1_w_gate, m_ffn1_w_up, m_ffn1_w_down, m_mix_norm, m_w_in, m_q_latent_norm, m_w_uq, m_kv_latent_norm, m_w_uk, m_w_uv, m_q_head_norm, m_k_head_norm, m_conv_w, m_conv_b, m_gate_a_w, m_gate_a_b, m_gate_x_w, m_gate_x_b, m_lru_lambda, m_attn_out_norm, m_lru_out_norm, m_w_out, m_ffn2_norm, m_ffn2_w_gate, m_ffn2_w_up, m_ffn2_w_down, m_final_norm, v_meta_tokens, v_ffn1_norm, v_ffn1_w_gate, v_ffn1_w_up, v_ffn1_w_down, v_mix_norm, v_w_in, v_q_latent_norm, v_w_uq, v_kv_latent_norm, v_w_uk, v_w_uv, v_q_head_norm, v_k_head_norm, v_conv_w, v_conv_b, v_gate_a_w, v_gate_a_b, v_gate_x_w, v_gate_x_b, v_lru_lambda, v_attn_out_norm, v_lru_out_norm, v_w_out, v_ffn2_norm, v_ffn2_w_gate, v_ffn2_w_up, v_ffn2_w_down, v_final_norm):
    args = locals()
    p = {nm: args[nm] for nm in WEIGHT_NAMES}
    mom = {nm: args["m_" + nm] for nm in WEIGHT_NAMES}
    var = {nm: args["v_" + nm] for nm in WEIGHT_NAMES}
    nb, seq, d = x.shape
    lp = CHUNK + seq
    xi, yi, ci = lax.axis_index("x"), lax.axis_index("y"), lax.axis_index("c")
    chip = 2 * xi + yi

    place = jnp.stack([chip, ci]).astype(jnp.int32)
    p2 = {nm: _to2d(nm, p[nm]) for nm in BIG_NAMES}
    m2 = {nm: _to2d(nm, mom[nm]) for nm in BIG_NAMES}
    v2 = {nm: _to2d(nm, var[nm]) for nm in BIG_NAMES}

    slots = {}
    for grp in BIG_GROUPS:
        for nm, buf in zip(grp, _cast_call("cast_" + grp[0], place, [p2[nm] for nm in grp])):
            slots[nm] = buf
    small_shard = jnp.concatenate(
        [meta_tokens, conv_w[0].reshape(2, 2 * LRU_TILE), jnp.zeros((14, 2 * LRU_TILE), F32)], axis=0)
    small_slots = lax.dynamic_update_slice(jnp.zeros((N_SHARD,) + small_shard.shape, F32), small_shard[None],
                                           (chip, 0, 0))
    first = _comm_call("gather_first", _gather_comm([slots[nm] for nm in GATHER_FIRST] + [small_slots], lambda o: o))
    w, meta = _small_weights(p, first[-1])
    w.update({nm: _weight_from(nm, o) for nm, o in zip(GATHER_FIRST, first[:-1])})
    sched = _Sched(place, w, slots)

    h0 = jnp.concatenate(
        [jnp.zeros((nb, PAD_ROWS, d), F32), jnp.broadcast_to(meta[None], (nb, N_META, d)), x], axis=1)
    target = jnp.pad(loss_target, ((0, 0), (CHUNK, 0), (0, 0)))
    loss_part, dh0, g = _local_step(h0.reshape(nb * lp, d), target.reshape(nb * lp, d), w, nb, lp, sched)
    dh0 = dh0.reshape(nb, lp, d)
    grad_x = dh0[:, CHUNK:, :]

    gs = {nm: g[nm] for nm in SMALL_NAMES}
    gs["q_head_norm"] = g["q_head_norm"][:, 0:D_QK]
    gs["k_head_norm"] = g["k_head_norm"][:, 0:D_QK]
    for nm in ("conv_b", "gate_a_b", "gate_x_b", "lru_lambda"):
        gs[nm] = g[nm].reshape(1, LRU_W)
    gs["conv_w"] = g["conv_w"].transpose(1, 0, 2).reshape(CONV_K, LRU_W)
    gs["gate_a_w"] = _gate_blocks(g["gate_a_w"])
    gs["gate_x_w"] = _gate_blocks(g["gate_x_w"])
    pack = _pack_small(gs, ROW_META)
    pack = jnp.concatenate([pack, jnp.sum(dh0[:, PAD_ROWS:CHUNK, :], axis=0), _row(loss_part[:, 0:1]),
                            jnp.zeros((SMALL_ROWS - ROW_LOSS - 1, D_MODEL), F32)], axis=0)
    small_like = {nm: p[nm] for nm in SMALL_NAMES + ["gate_a_w", "gate_x_w"]}

    def pack_w(t):
        tt = {nm: t[nm] for nm in SMALL_NAMES + ["gate_a_w", "gate_x_w"]}
        tt["conv_w"] = jnp.zeros((CONV_K, LRU_W), F32)
        return _pack_small(tt, SMALL_ADAM_ROWS)

    gsum, dsm, msm, vsm = _small_update_call(_gather_small_call(pack), pack_w(p), pack_w(mom), pack_w(var))
    grads = _unpack_small(gsum, small_like)
    delta = _unpack_small(dsm, small_like)
    new_m = _unpack_small(msm, small_like)
    new_v = _unpack_small(vsm, small_like)
    loss = gsum[ROW_LOSS, 0]
    gmeta = gsum[ROW_META:ROW_META + N_META].reshape(N_META, N_SHARD, D_MODEL // N_SHARD)
    grads["meta_tokens"] = lax.dynamic_index_in_dim(gmeta, chip, axis=1, keepdims=False)
    gconv = gsum[ROW_CONV_W:ROW_CONV_W + 2].reshape(CONV_K, N_SHARD, LRU_TILE)
    grads["conv_w"] = lax.dynamic_index_in_dim(gconv, chip, axis=1, keepdims=False)[None]
    for nm in ("meta_tokens", "conv_w"):
        delta[nm], new_m[nm], new_v[nm] = _adamw_call("adamw_" + nm, p[nm], grads[nm], mom[nm], var[nm])

    done = REDUCE_PLAN[0][0] + REDUCE_PLAN[1][0]
    last = REDUCE_PLAN[2][0]
    mine = sched.chip_sums(done)
    shard_grads = dict(zip(done, _comm_call("share_pair", _share_pair_comm([mine[nm] for nm in done], lambda o: o))))

    def adamw(names, comm=None):
        res = _adamw_group_call("adamw_" + names[0], [p2[nm] for nm in names], [shard_grads[nm] for nm in names],
                                [m2[nm] for nm in names], [v2[nm] for nm in names], comm)
        for nm, (gg, dd, mm, vv) in zip(names, res):
            grads[nm], delta[nm], new_m[nm], new_v[nm] = (_from2d(nm, t) for t in (gg, dd, mm, vv))

    groups = _same_shape_groups(done)
    adamw(groups[0], sched.chips(last))
    for grp in groups[1:]:
        adamw(grp)
    mine = sched.chip_sums(last)
    shard_grads = dict(zip(last, _comm_call("share_last", _share_pair_comm([mine[nm] for nm in last], lambda o: o))))
    adamw(last)

    return (loss, grad_x, *[grads[nm] for nm in WEIGHT_NAMES], *[delta[nm] for nm in WEIGHT_NAMES],
            *[new_m[nm] for nm in WEIGHT_NAMES], *[new_v[nm] for nm in WEIGHT_NAMES])
```

```python
import functools
import math

import jax
import jax.numpy as jnp
from jax import lax
from jax.experimental import pallas as pl
from jax.experimental.pallas import tpu as pltpu

F32 = jnp.float32
BF16 = jnp.bfloat16
MESH = pl.DeviceIdType.MESH

D_MODEL = 1024
N_META = 16
CHUNK = 64
PAD_ROWS = CHUNK - N_META
HEADS = 4
D_NOPE = 128
D_ROPE = 64
D_QK = D_NOPE + D_ROPE
D_QKP = 256
D_V = 128
KV_RANK = 256
Q_RANK = 384
MLA_W = HEADS * D_V
LRU_W = 512
LRU_TILE = 128
N_LRU_TILES = LRU_W // LRU_TILE
CONV_K = 4
C_RGLRU = 8.0
ROPE_THETA = 10000.0
D_FF = 2816
N_SHARD = 4
EPS = 1e-6
NEG_INF = -1e30
Z_KR = Q_RANK + KV_RANK
Z_MLA = Z_KR + 128
Z_U = Z_MLA
Z_G = Z_U + LRU_W
Z_W = Z_G + LRU_W
IN_WIDTH = Q_RANK + KV_RANK + D_ROPE + 2 * LRU_W

ADAM_LR = 0.001
ADAM_B1 = 0.9
ADAM_B2 = 0.999
ADAM_EPS = 1e-08
ADAM_WD = 0.01
ADAM_STEP = 10

VMEM_LIMIT_BYTES = 56 * 1024 * 1024
SMALL_ROWS = 104
SMALL_ADAM_ROWS = 80


def _params(sem):
    return pltpu.CompilerParams(dimension_semantics=sem, vmem_limit_bytes=VMEM_LIMIT_BYTES)


def _row_tile(rows, target):
    best = 16
    for t in range(16, min(rows, target) + 1, 16):
        if rows % t == 0:
            best = t
    return best


def _col_tile(cols, target):
    best = cols
    for t in range(128, min(cols, target) + 1, 128):
        if cols % t == 0:
            best = t
    return best


def _dot(a, b):
    return jnp.dot(a, b, preferred_element_type=F32)


def _dot_nt(a, b):
    return lax.dot_general(a, b, (((1,), (1,)), ((), ())), preferred_element_type=F32)


def _dot_tn(a, b):
    return lax.dot_general(a, b, (((0,), (0,)), ((), ())), preferred_element_type=F32)


def _rms(x, n):
    return lax.rsqrt(jnp.sum(x * x, axis=-1, keepdims=True) * (1.0 / n) + EPS)


def _rms_bwd(dn, nrm, r, n):
    return r * (dn - nrm * (jnp.sum(dn * nrm, axis=-1, keepdims=True) * (1.0 / n)))


def _gelu(x):
    k = math.sqrt(2.0 / math.pi)
    t = jnp.tanh(k * (x + 0.044715 * x * x * x))
    return 0.5 * x * (1.0 + t), t


def _gelu_grad(x, t):
    k = math.sqrt(2.0 / math.pi)
    return 0.5 * (1.0 + t) + 0.5 * x * (1.0 - t * t) * k * (1.0 + 3.0 * 0.044715 * x * x)


def _neg_expm1(x):
    series = -x * (1.0 + x * (0.5 + x * (1.0 / 6 + x * (1.0 / 24 + x * (1.0 / 120 + x * (1.0 / 720))))))
    return jnp.where(x > -0.25, series, 1.0 - jnp.exp(x))


def _softplus_neg(lam):
    e = jnp.exp(-jnp.abs(lam))
    log1p = jnp.where(e < 0.01, e * (1.0 - e * (0.5 - e * (1.0 / 3 - e * 0.25))), jnp.log(1.0 + e))
    return jnp.maximum(-lam, 0.0) + log1p


def _rope(t, c, s1, s2):
    return t * c + pltpu.roll(t, 96, 1) * s1 + pltpu.roll(t, 32, 1) * s2


def _rope_t(d, c, s1, s2):
    return d * c + pltpu.roll(d * s1, 32, 1) + pltpu.roll(d * s2, 96, 1)


def _rope_tables(lp):
    pos = (jnp.arange(lp, dtype=jnp.int32) - PAD_ROWS).astype(F32)
    inv_freq = ROPE_THETA ** (-jnp.arange(0, D_ROPE // 2, dtype=F32) / (D_ROPE // 2))
    ang = pos[:, None] * inv_freq[None, :]
    cos, sin = jnp.cos(ang), jnp.sin(ang)
    z = jnp.zeros_like(cos)
    return (jnp.concatenate([cos, cos, z, z], 1), jnp.concatenate([-sin, z, z, z], 1),
            jnp.concatenate([z, sin, z, z], 1))


def _rmsnorm_call(name, h, g, tm):
    rows, d = h.shape

    def body(h_ref, g_ref, o_ref):
        x = h_ref[...]
        o_ref[...] = (x * _rms(x, d) * g_ref[...]).astype(BF16)

    return pl.pallas_call(
        body, name=name, grid=(rows // tm,),
        in_specs=[pl.BlockSpec((tm, d), lambda i: (i, 0)), pl.BlockSpec((1, d), lambda i: (0, 0))],
        out_specs=pl.BlockSpec((tm, d), lambda i: (i, 0)),
        out_shape=jax.ShapeDtypeStruct((rows, d), BF16),
        compiler_params=_params(("parallel",)))(h, g)


def _ffn_up_call(name, u, wg, wu, tm, comm=None):
    rows, d = u.shape
    ns, fs, _ = wg.shape

    def body(u_ref, wg_ref, wu_ref, g_ref, p_ref, a_ref):
        uu = u_ref[...]
        g = _dot_nt(uu, wg_ref[0])
        p = _dot_nt(uu, wu_ref[0])
        g_ref[0] = g.astype(BF16)
        p_ref[0] = p.astype(BF16)
        a_ref[0] = (g * jax.nn.sigmoid(g) * p).astype(BF16)

    wspec = pl.BlockSpec((1, fs, d), lambda s, i: (s, 0, 0))
    ospec = pl.BlockSpec((1, tm, fs), lambda s, i: (s, i, 0))
    oshape = jax.ShapeDtypeStruct((ns, rows, fs), BF16)
    return _hosted_call(
        body, name=name, grid=(ns, rows // tm),
        in_specs=[pl.BlockSpec((tm, d), lambda s, i: (i, 0)), wspec, wspec],
        out_specs=[ospec, ospec, ospec], out_shape=[oshape, oshape, oshape],
        dims=("parallel", "parallel"), args=(u, wg, wu), comm=comm)


def _ffn_down_call(name, a, wd, h, tm):
    rows, d = h.shape
    ns, _, fs = a.shape

    def body(a_ref, wd_ref, h_ref, o_ref):
        acc = h_ref[...]
        for s in range(ns):
            acc = acc + 0.5 * _dot(a_ref[s], wd_ref[s])
        o_ref[...] = acc

    return pl.pallas_call(
        body, name=name, grid=(rows // tm,),
        in_specs=[pl.BlockSpec((ns, tm, fs), lambda i: (0, i, 0)),
                  pl.BlockSpec((ns, fs, d), lambda i: (0, 0, 0)),
                  pl.BlockSpec((tm, d), lambda i: (i, 0))],
        out_specs=pl.BlockSpec((tm, d), lambda i: (i, 0)),
        out_shape=jax.ShapeDtypeStruct((rows, d), F32),
        compiler_params=_params(("parallel",)))(a, wd, h)


def _mm_call(name, a, bt, tm, out_dtype):
    rows, k = a.shape
    n = bt.shape[0]

    def body(a_ref, b_ref, o_ref):
        o_ref[...] = _dot_nt(a_ref[...], b_ref[...]).astype(out_dtype)

    return pl.pallas_call(
        body, name=name, grid=(rows // tm,),
        in_specs=[pl.BlockSpec((tm, k), lambda i: (i, 0)), pl.BlockSpec((n, k), lambda i: (0, 0))],
        out_specs=pl.BlockSpec((tm, n), lambda i: (i, 0)),
        out_shape=jax.ShapeDtypeStruct((rows, n), out_dtype),
        compiler_params=_params(("parallel",)))(a, bt)


def _mla_heads(z, gql, gkvl, wuq, wuk, wuv):
    cq = z[:, 0:Q_RANK]
    ckv = z[:, Q_RANK:Z_KR]
    kr = z[:, Z_KR:Z_MLA]
    rq = _rms(cq, Q_RANK)
    nq = cq * rq
    cqn = (nq * gql).astype(BF16)
    rkv = _rms(ckv, KV_RANK)
    nkv = ckv * rkv
    ckvn = (nkv * gkvl).astype(BF16)
    qraw = _dot_nt(cqn, wuq)
    knope = _dot(ckvn, wuk)
    v = _dot(ckvn, wuv)
    skr = jnp.sum(kr * kr, axis=-1, keepdims=True)
    heads = []
    for hd in range(HEADS):
        qh = qraw[:, hd * D_QKP:(hd + 1) * D_QKP]
        rqh = lax.rsqrt(jnp.sum(qh * qh, axis=-1, keepdims=True) * (1.0 / D_QK) + EPS)
        kn = knope[:, hd * D_NOPE:(hd + 1) * D_NOPE]
        rkh = lax.rsqrt((jnp.sum(kn * kn, axis=-1, keepdims=True) + skr) * (1.0 / D_QK) + EPS)
        heads.append((qh * rqh, rqh, kn * rkh, kr * rkh, rkh))
    return dict(rq=rq, nq=nq, cqn=cqn, rkv=rkv, nkv=nkv, ckvn=ckvn, v=v, heads=heads)


def _mla_prep_call(z, gql, gkvl, gqh, gkh, wuq, wuk, wuv, tabs, lp, tm):
    rows = z.shape[0]
    tpe = lp // tm

    def body(z_ref, gql_ref, gkvl_ref, gqh_ref, gkh_ref, wuq_ref, wuk_ref, wuv_ref, c_ref, s1_ref, s2_ref,
             q_ref, k_ref, v_ref, cqn_ref, ckvn_ref):
        m = _mla_heads(z_ref[...], gql_ref[...], gkvl_ref[...], wuq_ref[...], wuk_ref[...], wuv_ref[...])
        c, s1, s2 = c_ref[...], s1_ref[...], s2_ref[...]
        gq, gk = gqh_ref[...], gkh_ref[...]
        row = (pl.program_id(0) % tpe) * tm + lax.broadcasted_iota(jnp.int32, (tm, 1), 0)
        spare = (lax.broadcasted_iota(jnp.int32, (1, D_QKP - D_NOPE), 1) == D_ROPE).astype(F32)
        kmask = jnp.where(row < PAD_ROWS, NEG_INF * math.sqrt(D_QK), 0.0) * spare
        for hd in range(HEADS):
            qn, _, knn, krn, _ = m["heads"][hd]
            qg = qn * gq
            q_ref[hd, :, 0:D_NOPE] = qg[:, 0:D_NOPE].astype(BF16)
            q_ref[hd, :, D_NOPE:D_QKP] = (_rope(qg[:, D_NOPE:D_QKP], c, s1, s2) + spare).astype(BF16)
            k_ref[hd, :, 0:D_NOPE] = (knn * gk[:, 0:D_NOPE]).astype(BF16)
            k_ref[hd, :, D_NOPE:D_QKP] = (_rope(krn * gk[:, D_NOPE:D_QKP], c, s1, s2) + kmask).astype(BF16)
            v_ref[hd] = m["v"][:, hd * D_V:(hd + 1) * D_V].astype(BF16)
        cqn_ref[...] = m["cqn"]
        ckvn_ref[...] = m["ckvn"]

    def const(shape):
        return pl.BlockSpec(shape, lambda i: tuple(0 for _ in shape))

    tab = pl.BlockSpec((tm, 128), lambda i: (i % tpe, 0))
    return pl.pallas_call(
        body, name="mla_prep", grid=(rows // tm,),
        in_specs=[pl.BlockSpec((tm, Z_MLA), lambda i: (i, 0)), const((1, Q_RANK)), const((1, KV_RANK)),
                  const((1, D_QKP)), const((1, D_QKP)), const((HEADS * D_QKP, Q_RANK)),
                  const((KV_RANK, HEADS * D_NOPE)), const((KV_RANK, HEADS * D_V)), tab, tab, tab],
        out_specs=[pl.BlockSpec((HEADS, tm, D_QKP), lambda i: (0, i, 0)),
                   pl.BlockSpec((HEADS, tm, D_QKP), lambda i: (0, i, 0)),
                   pl.BlockSpec((HEADS, tm, D_V), lambda i: (0, i, 0)),
                   pl.BlockSpec((tm, Q_RANK), lambda i: (i, 0)),
                   pl.BlockSpec((tm, KV_RANK), lambda i: (i, 0))],
        out_shape=[jax.ShapeDtypeStruct((HEADS, rows, D_QKP), BF16),
                   jax.ShapeDtypeStruct((HEADS, rows, D_QKP), BF16),
                   jax.ShapeDtypeStruct((HEADS, rows, D_V), BF16),
                   jax.ShapeDtypeStruct((rows, Q_RANK), BF16),
                   jax.ShapeDtypeStruct((rows, KV_RANK), BF16)],
        compiler_params=_params(("parallel",)))(z, gql, gkvl, gqh, gkh, wuq, wuk, wuv, *tabs)


Q_BLOCK_ROWS = 528


def _q_block(lp):
    return _row_tile(lp, Q_BLOCK_ROWS)


def _diag_bias(qb, j0):
    shift = CHUNK.bit_length() - 1
    r = jnp.right_shift(j0 + lax.broadcasted_iota(jnp.int32, (qb, qb), 0), shift)
    c = jnp.right_shift(j0 + lax.broadcasted_iota(jnp.int32, (qb, qb), 1), shift)
    return jnp.where(c <= r, 0.0, NEG_INF)


def _attn_fwd_call(q, k, v, nb, lp, comm=None):
    rows = nb * lp
    qb = _q_block(lp)
    scale = 1.0 / math.sqrt(D_QK)

    def body(q_ref, k_ref, v_ref, o_ref, lse_ref):
        for j in range(lp // qb):
            j0, ext = j * qb, (j + 1) * qb
            qj = q_ref[0, j0:ext, :]
            sd = _dot_nt(qj, k_ref[0, j0:ext, :]) * scale + _diag_bias(qb, j0)
            mx = jnp.max(sd, axis=-1, keepdims=True)
            if j > 0:
                so = _dot_nt(qj, k_ref[0, 0:j0, :]) * scale
                mx = jnp.maximum(mx, jnp.max(so, axis=-1, keepdims=True))
            pd = jnp.exp(sd - mx)
            l = jnp.sum(pd, axis=-1, keepdims=True)
            o = _dot(pd.astype(BF16), v_ref[0, j0:ext, :])
            if j > 0:
                po = jnp.exp(so - mx)
                l = l + jnp.sum(po, axis=-1, keepdims=True)
                o = o + _dot(po.astype(BF16), v_ref[0, 0:j0, :])
            o_ref[j0:ext, :] = o / l
            lse_ref[0, j0:ext, :] = mx + jnp.log(l)

    return _hosted_call(
        body, name="attn_fwd", grid=(nb, HEADS),
        in_specs=[pl.BlockSpec((1, lp, D_QKP), lambda b, h: (h, b, 0)),
                  pl.BlockSpec((1, lp, D_QKP), lambda b, h: (h, b, 0)),
                  pl.BlockSpec((1, lp, D_V), lambda b, h: (h, b, 0))],
        out_specs=[pl.BlockSpec((lp, D_V), lambda b, h: (b, h)),
                   pl.BlockSpec((1, lp, 1), lambda b, h: (h, b, 0))],
        out_shape=[jax.ShapeDtypeStruct((rows, MLA_W), F32),
                   jax.ShapeDtypeStruct((HEADS, rows, 1), F32)],
        dims=("parallel", "parallel"), args=(q, k, v), comm=comm)


def _attn_bwd_call(q, k, v, o, lse, do, nb, lp):
    rows = nb * lp
    qb = _q_block(lp)
    scale = 1.0 / math.sqrt(D_QK)

    def body(q_ref, k_ref, v_ref, o_ref, lse_ref, do_ref, dq_ref, dk_ref, dv_ref, dk_acc, dv_acc):
        dk_acc[...] = jnp.zeros_like(dk_acc)
        dv_acc[...] = jnp.zeros_like(dv_acc)
        for j in range(lp // qb):
            j0, ext = j * qb, (j + 1) * qb
            dbias = _diag_bias(qb, j0)
            qj = q_ref[0, j0:ext, :]
            doj = do_ref[j0:ext, :]
            delta = jnp.sum(doj * o_ref[j0:ext, :], axis=-1, keepdims=True)
            dob = doj.astype(BF16)
            lse = lse_ref[0, j0:ext, :]
            dq = jnp.zeros((qb, D_QKP), F32)
            for lo, hi, bias in ((j0, ext, dbias), (0, j0, None)):
                if hi == lo:
                    continue
                kk = k_ref[0, lo:hi, :]
                s = _dot_nt(qj, kk) * scale
                p = jnp.exp((s if bias is None else s + bias) - lse)
                dv_acc[lo:hi, :] += _dot_tn(p.astype(BF16), dob)
                dp = _dot_nt(dob, v_ref[0, lo:hi, :])
                ds = (p * (dp - delta) * scale).astype(BF16)
                dq = dq + _dot(ds, kk)
                dk_acc[lo:hi, :] += _dot_tn(ds, qj)
            dq_ref[0, j0:ext, :] = dq.astype(BF16)
        dk_ref[0] = dk_acc[...].astype(BF16)
        dv_ref[0] = dv_acc[...].astype(BF16)

    qspec = pl.BlockSpec((1, lp, D_QKP), lambda b, h: (h, b, 0))
    vspec = pl.BlockSpec((1, lp, D_V), lambda b, h: (h, b, 0))
    ospec = pl.BlockSpec((lp, D_V), lambda b, h: (b, h))
    return pl.pallas_call(
        body, name="attn_bwd", grid=(nb, HEADS),
        in_specs=[qspec, qspec, vspec, ospec, pl.BlockSpec((1, lp, 1), lambda b, h: (h, b, 0)), ospec],
        out_specs=[qspec, qspec, vspec],
        out_shape=[jax.ShapeDtypeStruct((HEADS, rows, D_QKP), BF16),
                   jax.ShapeDtypeStruct((HEADS, rows, D_QKP), BF16),
                   jax.ShapeDtypeStruct((HEADS, rows, D_V), BF16)],
        scratch_shapes=[pltpu.VMEM((lp, D_QKP), F32), pltpu.VMEM((lp, D_V), F32)],
        compiler_params=_params(("parallel", "parallel")))(q, k, v, o, lse, do)


def _lru_gates(u, cw, cb, wa, ba, wx, bx, lam, lp):
    xc = (cw[3:4, :] * u + cw[2:3, :] * pltpu.roll(u, 1, 0) + cw[1:2, :] * pltpu.roll(u, 2, 0)
          + cw[0:1, :] * pltpu.roll(u, 3, 0) + cb)
    xcb = xc.astype(BF16)
    r = jax.nn.sigmoid(_dot(xcb, wa) + ba)
    i = jax.nn.sigmoid(_dot(xcb, wx) + bx)
    sp = _softplus_neg(lam)
    la = -C_RGLRU * r * sp
    a = jnp.exp(la)
    mult = jnp.sqrt(_neg_expm1(2.0 * la))
    row = lax.broadcasted_iota(jnp.int32, (lp, LRU_TILE), 0)
    first = row == PAD_ROWS
    valid = row >= PAD_ROWS
    mult_eff = jnp.where(first, 1.0, mult)
    return dict(xc=xc, xcb=xcb, r=r, i=i, sp=sp, la=la, a=a, mult=mult, mult_eff=mult_eff, first=first, valid=valid)


def _scan_rows(a, b, a_s, b_s, out_ref, lp, reverse):
    sub = lax.broadcasted_iota(jnp.int32, (lp, LRU_TILE), 0) & 7
    for dist in (1, 2, 4):
        shift = lp - dist if reverse else dist
        keep = (sub + dist <= 7) if reverse else (sub >= dist)
        a_sh = pltpu.roll(a, shift, 0)
        b_sh = pltpu.roll(b, shift, 0)
        b = jnp.where(keep, a * b_sh + b, b)
        a = jnp.where(keep, a * a_sh, a)
    a_s[...] = a
    b_s[...] = b
    n_groups = lp // 8
    edge = 0 if reverse else 7

    def group(gi, carry):
        r0 = pl.multiple_of(((n_groups - 1 - gi) if reverse else gi) * 8, 8)
        a8 = a_s[pl.ds(r0, 8), :]
        b8 = b_s[pl.ds(r0, 8), :]
        out_ref[pl.ds(r0, 8), :] = a8 * carry + b8
        return a8[edge:edge + 1, :] * carry + b8[edge:edge + 1, :]

    lax.fori_loop(0, n_groups, group, jnp.zeros((1, LRU_TILE), F32), unroll=4)


def _lru_specs(lp):
    seq = lambda col0: pl.BlockSpec((lp, LRU_TILE), lambda t, b: (b, col0 + t))
    cw = pl.BlockSpec((1, CONV_K, LRU_TILE), lambda t, b: (t, 0, 0))
    vec = pl.BlockSpec((1, LRU_TILE), lambda t, b: (0, t))
    mat = pl.BlockSpec((1, LRU_TILE, LRU_TILE), lambda t, b: (t, 0, 0))
    return seq, cw, vec, mat


def _lru_fwd_call(z, cw, cb, wa, ba, wx, bx, lam, nb, lp, comm=None):
    rows = nb * lp
    seq, cwspec, vec, mat = _lru_specs(lp)

    def body(u_ref, g_ref, cw_ref, cb_ref, wa_ref, ba_ref, wx_ref, bx_ref, lam_ref, y_ref, hs_ref, a_s, b_s):
        m = _lru_gates(u_ref[...], cw_ref[0], cb_ref[...], wa_ref[0], ba_ref[...], wx_ref[0], bx_ref[...],
                       lam_ref[...], lp)
        a = jnp.where(m["valid"], m["a"], 0.0)
        b = jnp.where(m["valid"], m["mult_eff"] * (m["i"] * m["xc"]), 0.0)
        _scan_rows(a, b, a_s, b_s, hs_ref, lp, reverse=False)
        gl, _ = _gelu(g_ref[...])
        y_ref[...] = hs_ref[...] * gl

    oshape = jax.ShapeDtypeStruct((rows, LRU_W), F32)
    return _hosted_call(
        body, name="lru_fwd", grid=(N_LRU_TILES, nb),
        in_specs=[seq(Z_U // LRU_TILE), seq(Z_G // LRU_TILE), cwspec, vec, mat, vec, mat, vec, vec],
        out_specs=[seq(0), seq(0)], out_shape=[oshape, oshape],
        scratch_shapes=[pltpu.VMEM((lp, LRU_TILE), F32), pltpu.VMEM((lp, LRU_TILE), F32)],
        dims=("parallel", "parallel"), args=(z, z, cw, cb, wa, ba, wx, bx, lam), comm=comm)


def _lru_bwd_call(z, hs, dy, cw, cb, wa, ba, wx, bx, lam, nb, lp, comm=None):
    rows = nb * lp
    seq, cwspec, vec, mat = _lru_specs(lp)

    def body(u_ref, g_ref, hs_ref, dy_ref, cw_ref, cb_ref, wa_ref, ba_ref, wx_ref, bx_ref, lam_ref,
             du_ref, dg_ref, dcw_ref, dcb_ref, dwa_ref, dba_ref, dwx_ref, dbx_ref, dlam_ref, a_s, b_s, d_s):
        b_idx = pl.program_id(1)
        u = u_ref[...]
        cw = cw_ref[0]
        wa, wx = wa_ref[0], wx_ref[0]
        lam = lam_ref[...]
        m = _lru_gates(u, cw, cb_ref[...], wa, ba_ref[...], wx, bx_ref[...], lam, lp)
        gate = g_ref[...]
        gl, th = _gelu(gate)
        dy = dy_ref[...]
        hs = hs_ref[...]
        dg_ref[...] = (dy * hs * _gelu_grad(gate, th)).astype(BF16)
        a_eff = jnp.where(m["valid"], m["a"], 0.0)
        _scan_rows(pltpu.roll(a_eff, lp - 1, 0), dy * gl, a_s, b_s, d_s, lp, reverse=True)
        ds = d_s[...]
        xc, r, i = m["xc"], m["r"], m["i"]
        row = lax.broadcasted_iota(jnp.int32, (lp, LRU_TILE), 0)
        da = ds * jnp.where(row >= 1, pltpu.roll(hs, 1, 0), 0.0)
        db = jnp.where(m["valid"], ds, 0.0)
        di = db * m["mult_eff"] * xc
        dxc = db * m["mult_eff"] * i
        live = m["valid"] & jnp.logical_not(m["first"])
        e2 = jnp.exp(2.0 * m["la"])
        dm = jnp.where(live, db * i * xc, 0.0)
        dla = da * m["a"] + jnp.where(live, dm * (-e2 / m["mult"]), 0.0)
        dr = dla * (-C_RGLRU * m["sp"])
        dsp = jnp.sum(dla * (-C_RGLRU * r), axis=0, keepdims=True)
        dpr = (dr * r * (1.0 - r))
        dpi = (di * i * (1.0 - i))
        dprb, dpib = dpr.astype(BF16), dpi.astype(BF16)
        dxc = dxc + _dot_nt(dprb, wa) + _dot_nt(dpib, wx)
        du = (cw[3:4, :] * dxc + cw[2:3, :] * pltpu.roll(dxc, lp - 1, 0) + cw[1:2, :] * pltpu.roll(dxc, lp - 2, 0)
              + cw[0:1, :] * pltpu.roll(dxc, lp - 3, 0))
        du_ref[...] = jnp.where(m["valid"], du, 0.0).astype(BF16)
        tap = lax.broadcasted_iota(jnp.int32, (CONV_K, LRU_TILE), 0)
        dcw = jnp.zeros((CONV_K, LRU_TILE), F32)
        for kk in range(CONV_K):
            shifted = u if kk == CONV_K - 1 else pltpu.roll(u, CONV_K - 1 - kk, 0)
            dcw = jnp.where(tap == kk, jnp.sum(dxc * shifted, axis=0, keepdims=True), dcw)
        parts = [(dcw_ref, dcw[None]), (dcb_ref, jnp.sum(dxc, axis=0, keepdims=True)[None]),
                 (dwa_ref, _dot_tn(m["xcb"], dprb)[None]), (dba_ref, jnp.sum(dpr, axis=0, keepdims=True)[None]),
                 (dwx_ref, _dot_tn(m["xcb"], dpib)[None]), (dbx_ref, jnp.sum(dpi, axis=0, keepdims=True)[None]),
                 (dlam_ref, (dsp * (-jax.nn.sigmoid(-lam)))[None])]

        @pl.when(b_idx == 0)
        def _():
            for ref, val in parts:
                ref[...] = val

        @pl.when(b_idx != 0)
        def _():
            for ref, val in parts:
                ref[...] += val

    bshape = jax.ShapeDtypeStruct((rows, LRU_W), BF16)
    vec3 = pl.BlockSpec((1, 1, LRU_TILE), lambda t, b: (t, 0, 0))
    vshape = jax.ShapeDtypeStruct((N_LRU_TILES, 1, LRU_TILE), F32)
    mshape = jax.ShapeDtypeStruct((N_LRU_TILES, LRU_TILE, LRU_TILE), F32)
    return _hosted_call(
        body, name="lru_bwd", grid=(N_LRU_TILES, nb),
        in_specs=[seq(Z_U // LRU_TILE), seq(Z_G // LRU_TILE), seq(0), seq(0), cwspec, vec, mat, vec, mat, vec, vec],
        out_specs=[seq(0), seq(0), cwspec, vec3, mat, vec3, mat, vec3, vec3],
        out_shape=[bshape, bshape, jax.ShapeDtypeStruct((N_LRU_TILES, CONV_K, LRU_TILE), F32), vshape, mshape,
                   vshape, mshape, vshape, vshape],
        scratch_shapes=[pltpu.VMEM((lp, LRU_TILE), F32)] * 3,
        dims=("parallel", "arbitrary"), args=(z, z, hs, dy, cw, cb, wa, ba, wx, bx, lam), comm=comm)


def _mix_out_call(ya, yl, ga, gl, wout, h, tm):
    rows, d = h.shape

    def body(ya_ref, yl_ref, ga_ref, gl_ref, w_ref, h_ref, y_ref, o_ref):
        a = ya_ref[...]
        l = yl_ref[...]
        an = (a * _rms(a, MLA_W) * ga_ref[...]).astype(BF16)
        ln = (l * _rms(l, LRU_W) * gl_ref[...]).astype(BF16)
        y_ref[:, 0:MLA_W] = an
        y_ref[:, MLA_W:MLA_W + LRU_W] = ln
        o_ref[...] = h_ref[...] + _dot(an, w_ref[0:MLA_W, :]) + _dot(ln, w_ref[MLA_W:MLA_W + LRU_W, :])

    half = pl.BlockSpec((tm, MLA_W), lambda i: (i, 0))
    g = pl.BlockSpec((1, MLA_W), lambda i: (0, 0))
    full = pl.BlockSpec((tm, d), lambda i: (i, 0))
    return pl.pallas_call(
        body, name="mix_out", grid=(rows // tm,),
        in_specs=[half, half, g, g, pl.BlockSpec((MLA_W + LRU_W, d), lambda i: (0, 0)), full],
        out_specs=[full, full],
        out_shape=[jax.ShapeDtypeStruct((rows, MLA_W + LRU_W), BF16), jax.ShapeDtypeStruct((rows, d), F32)],
        compiler_params=_params(("parallel",)))(ya, yl, ga, gl, wout, h)


def _mix_out_bwd_call(dhb, wout, ya, yl, ga, gl, tm):
    rows = ya.shape[0]
    d = dhb.shape[1]

    def body(dh_ref, w_ref, ya_ref, yl_ref, ga_ref, gl_ref, dya_ref, dyl_ref, dga_ref, dgl_ref):
        dy = _dot_nt(dh_ref[...], w_ref[...])
        outs = []
        for val, g_ref, lo, out_ref in ((ya_ref[...], ga_ref, 0, dya_ref), (yl_ref[...], gl_ref, MLA_W, dyl_ref)):
            r = _rms(val, MLA_W)
            n = val * r
            dyn = dy[:, lo:lo + MLA_W]
            out_ref[...] = _rms_bwd(dyn * g_ref[...], n, r, MLA_W)
            outs.append(jnp.sum(dyn * n, axis=0, keepdims=True))

        @pl.when(pl.program_id(0) == 0)
        def _():
            dga_ref[...] = outs[0]
            dgl_ref[...] = outs[1]

        @pl.when(pl.program_id(0) != 0)
        def _():
            dga_ref[...] += outs[0]
            dgl_ref[...] += outs[1]

    half = pl.BlockSpec((tm, MLA_W), lambda i: (i, 0))
    g = pl.BlockSpec((1, MLA_W), lambda i: (0, 0))
    return pl.pallas_call(
        body, name="mix_out_bwd", grid=(rows // tm,),
        in_specs=[pl.BlockSpec((tm, d), lambda i: (i, 0)), pl.BlockSpec((MLA_W + LRU_W, d), lambda i: (0, 0)),
                  half, half, g, g],
        out_specs=[half, half, g, g],
        out_shape=[jax.ShapeDtypeStruct((rows, MLA_W), F32), jax.ShapeDtypeStruct((rows, LRU_W), F32),
                   jax.ShapeDtypeStruct((1, MLA_W), F32), jax.ShapeDtypeStruct((1, LRU_W), F32)],
        compiler_params=_params(("arbitrary",)))(dhb, wout, ya, yl, ga, gl)


def _final_call(h, g, target, lp, tm):
    rows, d = h.shape
    tpe = lp // tm

    def body(h_ref, g_ref, t_ref, dh_ref, dhb_ref, dg_ref, loss_ref):
        i = pl.program_id(0)
        x = h_ref[...]
        g = g_ref[...]
        r = _rms(x, d)
        n = x * r
        row = (i % tpe) * tm + lax.broadcasted_iota(jnp.int32, (tm, 1), 0)
        err = jnp.where(row >= CHUNK, n * g - t_ref[...], 0.0)
        dout = err * (1.0 / d)
        dh = _rms_bwd(dout * g, n, r, d)
        dh_ref[...] = dh
        dhb_ref[...] = dh.astype(BF16)
        dg = jnp.sum(dout * n, axis=0, keepdims=True)
        part = jnp.sum(jnp.sum(err * err, axis=1, keepdims=True), axis=0, keepdims=True) * (0.5 / d)
        loss = jnp.broadcast_to(part, (1, 128))

        @pl.when(i == 0)
        def _():
            dg_ref[...] = dg
            loss_ref[...] = loss

        @pl.when(i != 0)
        def _():
            dg_ref[...] += dg
            loss_ref[...] += loss

    full = pl.BlockSpec((tm, d), lambda i: (i, 0))
    return pl.pallas_call(
        body, name="final_loss", grid=(rows // tm,),
        in_specs=[full, pl.BlockSpec((1, d), lambda i: (0, 0)), full],
        out_specs=[full, full, pl.BlockSpec((1, d), lambda i: (0, 0)), pl.BlockSpec((1, 128), lambda i: (0, 0))],
        out_shape=[jax.ShapeDtypeStruct((rows, d), F32), jax.ShapeDtypeStruct((rows, d), BF16),
                   jax.ShapeDtypeStruct((1, d), F32), jax.ShapeDtypeStruct((1, 128), F32)],
        compiler_params=_params(("arbitrary",)))(h, g, target)


def _ffn_dact_call(name, dhb, wd, gate, up, tm, comm=None):
    rows, d = dhb.shape
    ns, fs, _ = wd.shape

    nsub = 2 if tm % 32 == 0 else 1
    sub = tm // nsub

    def body(dh_ref, wd_ref, g_ref, p_ref, dg_ref, dp_ref):
        wd = wd_ref[0]
        for r in range(nsub):
            rs = slice(r * sub, (r + 1) * sub)
            da = 0.5 * _dot_nt(dh_ref[rs, :], wd)
            g = g_ref[0, rs, :].astype(F32)
            p = p_ref[0, rs, :].astype(F32)
            sg = jax.nn.sigmoid(g)
            dg_ref[0, rs, :] = (da * p * sg * (1.0 + g * (1.0 - sg))).astype(BF16)
            dp_ref[0, rs, :] = (da * g * sg).astype(BF16)

    aspec = pl.BlockSpec((1, tm, fs), lambda s, i: (s, i, 0))
    oshape = jax.ShapeDtypeStruct((ns, rows, fs), BF16)
    return _hosted_call(
        body, name=name, grid=(ns, rows // tm),
        in_specs=[pl.BlockSpec((tm, d), lambda s, i: (i, 0)), pl.BlockSpec((1, fs, d), lambda s, i: (s, 0, 0)),
                  aspec, aspec],
        out_specs=[aspec, aspec], out_shape=[oshape, oshape],
        dims=("parallel", "parallel"), args=(dhb, wd, gate, up), comm=comm)


def _norm_in_bwd_call(name, pieces, h, g, dres, tm, comm=None):
    rows, d = h.shape
    npc = len(pieces)

    def body(*refs):
        d_refs = refs[0:2 * npc:2]
        w_refs = refs[1:2 * npc:2]
        h_ref, g_ref, dres_ref, dh_ref, dhb_ref, dg_ref = refs[2 * npc:]
        du = jnp.zeros((tm, d), F32)
        for d_ref, w_ref in zip(d_refs, w_refs):
            if len(d_ref.shape) == 3:
                for s in range(d_ref.shape[0]):
                    du = du + _dot(d_ref[s], w_ref[s])
            else:
                du = du + _dot(d_ref[...], w_ref[...])
        x = h_ref[...]
        r = _rms(x, d)
        n = x * r
        dh = dres_ref[...] + _rms_bwd(du * g_ref[...], n, r, d)
        dh_ref[...] = dh
        dhb_ref[...] = dh.astype(BF16)
        dg = jnp.sum(du * n, axis=0, keepdims=True)

        @pl.when(pl.program_id(0) == 0)
        def _():
            dg_ref[...] = dg

        @pl.when(pl.program_id(0) != 0)
        def _():
            dg_ref[...] += dg

    in_specs, args = [], []
    for dd, w in pieces:
        if dd.ndim == 3:
            in_specs.append(pl.BlockSpec((dd.shape[0], tm, dd.shape[2]), lambda i: (0, i, 0)))
            in_specs.append(pl.BlockSpec(w.shape, lambda i: (0, 0, 0)))
        else:
            in_specs.append(pl.BlockSpec((tm, dd.shape[1]), lambda i: (i, 0)))
            in_specs.append(pl.BlockSpec(w.shape, lambda i: (0, 0)))
        args += [dd, w]
    full = pl.BlockSpec((tm, d), lambda i: (i, 0))
    gspec = pl.BlockSpec((1, d), lambda i: (0, 0))
    return _hosted_call(
        body, name=name, grid=(rows // tm,),
        in_specs=in_specs + [full, gspec, full],
        out_specs=[full, full, gspec],
        out_shape=[jax.ShapeDtypeStruct((rows, d), F32), jax.ShapeDtypeStruct((rows, d), BF16),
                   jax.ShapeDtypeStruct((1, d), F32)],
        args=(*args, h, g, dres), comm=comm)


def _wgrad_call(name, a, b, scale=1.0):
    a3, b3 = a.ndim == 3, b.ndim == 3
    ns = a.shape[0] if a3 else (b.shape[0] if b3 else 1)
    rows, m = a.shape[-2:]
    n = b.shape[-1]
    tmm = m if a3 else _col_tile(m, 256)

    def body(a_ref, b_ref, o_ref):
        av = a_ref[0] if a3 else a_ref[...]
        bv = b_ref[0] if b3 else b_ref[...]
        res = _dot_tn(av, bv)
        if scale != 1.0:
            res = res * scale
        if a3 or b3:
            o_ref[0] = res
        else:
            o_ref[...] = res

    aspec = (pl.BlockSpec((1, rows, tmm), lambda s, j: (s, 0, j)) if a3
             else pl.BlockSpec((rows, tmm), lambda s, j: (0, j)))
    bspec = (pl.BlockSpec((1, rows, n), lambda s, j: (s, 0, 0)) if b3
             else pl.BlockSpec((rows, n), lambda s, j: (0, 0)))
    if a3 or b3:
        ospec = pl.BlockSpec((1, tmm, n), lambda s, j: (s, j, 0))
        oshape = jax.ShapeDtypeStruct((ns, m, n), F32)
    else:
        ospec = pl.BlockSpec((tmm, n), lambda s, j: (j, 0))
        oshape = jax.ShapeDtypeStruct((m, n), F32)
    return pl.pallas_call(
        body, name=name, grid=(ns, m // tmm), in_specs=[aspec, bspec], out_specs=ospec, out_shape=oshape,
        compiler_params=_params(("parallel", "parallel")))(a, b)


def _mla_prep_bwd_call(z, dq, dk, dv, gql, gkvl, gqh, gkh, wuq, wuk, wuv, tabs, lp, tm):
    rows = z.shape[0]
    tpe = lp // tm

    def body(z_ref, dq_ref, dk_ref, dv_ref, gql_ref, gkvl_ref, gqh_ref, gkh_ref, wuq_ref, wuk_ref, wuv_ref,
             c_ref, s1_ref, s2_ref, dz_ref, dqp_ref, dkn_ref, dvv_ref, dgql_ref, dgkvl_ref, dgqh_ref, dgkh_ref):
        gql, gkvl = gql_ref[...], gkvl_ref[...]
        gq, gk = gqh_ref[...], gkh_ref[...]
        wuq, wuk, wuv = wuq_ref[...], wuk_ref[...], wuv_ref[...]
        m = _mla_heads(z_ref[...], gql, gkvl, wuq, wuk, wuv)
        c, s1, s2 = c_ref[...], s1_ref[...], s2_ref[...]
        dgq = jnp.zeros((1, D_QKP), F32)
        dgk = jnp.zeros((1, D_QKP), F32)
        dkr = jnp.zeros((tm, D_QKP - D_NOPE), F32)
        for hd in range(HEADS):
            qn, rqh, knn, krn, rkh = m["heads"][hd]
            dqg = jnp.concatenate([dq_ref[hd, :, 0:D_NOPE].astype(F32),
                                   _rope_t(dq_ref[hd, :, D_NOPE:D_QKP].astype(F32), c, s1, s2)], axis=1)
            dgq = dgq + jnp.sum(dqg * qn, axis=0, keepdims=True)
            dqn = dqg * gq
            dqr = rqh * (dqn - qn * (jnp.sum(dqn * qn, axis=-1, keepdims=True) * (1.0 / D_QK)))
            dqp_ref[:, hd * D_QKP:(hd + 1) * D_QKP] = dqr.astype(BF16)
            kn_full = jnp.concatenate([knn, krn], axis=1)
            dkg = jnp.concatenate([dk_ref[hd, :, 0:D_NOPE].astype(F32),
                                   _rope_t(dk_ref[hd, :, D_NOPE:D_QKP].astype(F32), c, s1, s2)], axis=1)
            dgk = dgk + jnp.sum(dkg * kn_full, axis=0, keepdims=True)
            dkn = dkg * gk
            dkraw = rkh * (dkn - kn_full * (jnp.sum(dkn * kn_full, axis=-1, keepdims=True) * (1.0 / D_QK)))
            dkn_ref[:, hd * D_NOPE:(hd + 1) * D_NOPE] = dkraw[:, 0:D_NOPE].astype(BF16)
            dkr = dkr + dkraw[:, D_NOPE:D_QKP]
            dvv_ref[:, hd * D_V:(hd + 1) * D_V] = dv_ref[hd]
        dcqn = _dot(dqp_ref[...], wuq)
        dckvn = _dot_nt(dkn_ref[...], wuk) + _dot_nt(dvv_ref[...], wuv)
        dz_ref[:, 0:Q_RANK] = _rms_bwd(dcqn * gql, m["nq"], m["rq"], Q_RANK).astype(BF16)
        dz_ref[:, Q_RANK:Z_KR] = _rms_bwd(dckvn * gkvl, m["nkv"], m["rkv"], KV_RANK).astype(BF16)
        dz_ref[:, Z_KR:Z_MLA] = dkr.astype(BF16)
        parts = [(dgql_ref, jnp.sum(dcqn * m["nq"], axis=0, keepdims=True)),
                 (dgkvl_ref, jnp.sum(dckvn * m["nkv"], axis=0, keepdims=True)), (dgqh_ref, dgq), (dgkh_ref, dgk)]

        @pl.when(pl.program_id(0) == 0)
        def _():
            for ref, val in parts:
                ref[...] = val

        @pl.when(pl.program_id(0) != 0)
        def _():
            for ref, val in parts:
                ref[...] += val

    def const(shape):
        return pl.BlockSpec(shape, lambda i: tuple(0 for _ in shape))

    tab = pl.BlockSpec((tm, 128), lambda i: (i % tpe, 0))
    hq = pl.BlockSpec((HEADS, tm, D_QKP), lambda i: (0, i, 0))
    hv = pl.BlockSpec((HEADS, tm, D_V), lambda i: (0, i, 0))

    def rowspec(n):
        return pl.BlockSpec((tm, n), lambda i: (i, 0))

    return pl.pallas_call(
        body, name="mla_prep_bwd", grid=(rows // tm,),
        in_specs=[rowspec(Z_MLA), hq, hq, hv, const((1, Q_RANK)), const((1, KV_RANK)), const((1, D_QKP)),
                  const((1, D_QKP)), const((HEADS * D_QKP, Q_RANK)), const((KV_RANK, HEADS * D_NOPE)),
                  const((KV_RANK, HEADS * D_V)), tab, tab, tab],
        out_specs=[rowspec(Z_MLA), rowspec(HEADS * D_QKP), rowspec(HEADS * D_NOPE), rowspec(HEADS * D_V),
                   const((1, Q_RANK)), const((1, KV_RANK)), const((1, D_QKP)), const((1, D_QKP))],
        out_shape=[jax.ShapeDtypeStruct((rows, Z_MLA), BF16), jax.ShapeDtypeStruct((rows, HEADS * D_QKP), BF16),
                   jax.ShapeDtypeStruct((rows, HEADS * D_NOPE), BF16), jax.ShapeDtypeStruct((rows, HEADS * D_V), BF16),
                   jax.ShapeDtypeStruct((1, Q_RANK), F32), jax.ShapeDtypeStruct((1, KV_RANK), F32),
                   jax.ShapeDtypeStruct((1, D_QKP), F32), jax.ShapeDtypeStruct((1, D_QKP), F32)],
        compiler_params=_params(("arbitrary",)))(z, dq, dk, dv, gql, gkvl, gqh, gkh, wuq, wuk, wuv, *tabs)


def _local_step(h0, target, w, nb, lp, sched=None):
    tm = _row_tile(lp, 1024)
    te = _row_tile(lp, 512)
    tabs = _rope_tables(lp)
    g = {}
    if sched is None:
        host = lambda stage: None
    else:
        sched.g = g
        host = sched.host

    def ffn_fwd(tag, h):
        u = _rmsnorm_call(tag + "_norm", h, w[tag + "_norm"], te)
        gate, up, act = _ffn_up_call(tag + "_up", u, w[tag + "_w_gate"], w[tag + "_w_up"], tm, host(tag + "_up"))
        return _ffn_down_call(tag + "_down", act, w[tag + "_w_down"], h, te), (u, gate, up, act)

    def ffn_bwd(tag, h, saved, dh, dhb):
        u, gate, up, act = saved
        dgate, dup = _ffn_dact_call(tag + "_dact", dhb, w[tag + "_w_down"], gate, up, tm, host(tag + "_dact"))
        g[tag + "_w_down"] = _wgrad_call(tag + "_dwd", act, dhb, 0.5)
        g[tag + "_w_gate"] = _wgrad_call(tag + "_dwg", dgate, u)
        g[tag + "_w_up"] = _wgrad_call(tag + "_dwu", dup, u)
        dh_in, dhb_in, g[tag + "_norm"] = _norm_in_bwd_call(
            tag + "_din", [(dgate, w[tag + "_w_gate"]), (dup, w[tag + "_w_up"])], h, w[tag + "_norm"], dh, te,
            host(tag + "_din"))
        return dh_in, dhb_in

    h1, s1 = ffn_fwd("ffn1", h0)
    un = _rmsnorm_call("mix_norm", h1, w["mix_norm"], te)
    z = _mm_call("mix_in", un, w["w_in"], tm, F32)
    mla_w = (w["q_latent_norm"], w["kv_latent_norm"], w["q_head_norm"], w["k_head_norm"], w["w_uq"], w["w_uk"],
             w["w_uv"])
    q, k, v, cqn, ckvn = _mla_prep_call(z, *mla_w, tabs, lp, te)
    o, lse = _attn_fwd_call(q, k, v, nb, lp, host("attn_fwd"))
    lru_w = (w["conv_w"], w["conv_b"], w["gate_a_w"], w["gate_a_b"], w["gate_x_w"], w["gate_x_b"], w["lru_lambda"])
    yl, hs = _lru_fwd_call(z, *lru_w, nb, lp, host("lru_fwd"))
    y, h2 = _mix_out_call(o, yl, w["attn_out_norm"], w["lru_out_norm"], w["w_out"], h1, te)
    h3, s2 = ffn_fwd("ffn2", h2)
    dh3, dh3b, g["final_norm"], loss = _final_call(h3, w["final_norm"], target, lp, te)

    dh2, dh2b = ffn_bwd("ffn2", h2, s2, dh3, dh3b)
    g["w_out"] = _wgrad_call("dw_out", y, dh2b)
    dya, dyl, g["attn_out_norm"], g["lru_out_norm"] = _mix_out_bwd_call(
        dh2b, w["w_out"], o, yl, w["attn_out_norm"], w["lru_out_norm"], te)
    dq, dk, dv = _attn_bwd_call(q, k, v, o, lse, dya, nb, lp)
    (dz_mla, dqp, dkn, dvv, g["q_latent_norm"], g["kv_latent_norm"], g["q_head_norm"],
     g["k_head_norm"]) = _mla_prep_bwd_call(z, dq, dk, dv, *mla_w, tabs, lp, te)
    g["w_uq"] = _wgrad_call("dw_uq", dqp, cqn)
    g["w_uk"] = _wgrad_call("dw_uk", ckvn, dkn)
    g["w_uv"] = _wgrad_call("dw_uv", ckvn, dvv)
    (du, dgt, g["conv_w"], g["conv_b"], g["gate_a_w"], g["gate_a_b"], g["gate_x_w"], g["gate_x_b"],
     g["lru_lambda"]) = _lru_bwd_call(z, hs, dyl, *lru_w, nb, lp, host("lru_bwd"))
    win = w["w_in"]
    g["w_in"] = jnp.concatenate(
        [_wgrad_call("dw_in_mla", dz_mla, un), _wgrad_call("dw_in_u", du, un), _wgrad_call("dw_in_g", dgt, un)],
        axis=0)
    dh1, dh1b, g["mix_norm"] = _norm_in_bwd_call(
        "mix_din", [(dz_mla, win[0:Z_MLA]), (du, win[Z_U:Z_G]), (dgt, win[Z_G:Z_W])], h1, w["mix_norm"], dh2, te,
        host("mix_din"))
    dh0, _ = ffn_bwd("ffn1", h0, s1, dh1, dh1b)
    return loss, dh0, g


def _place():
    x, y, c = lax.axis_index("x"), lax.axis_index("y"), lax.axis_index("c")
    return x, y, c, [(1 - x, y), (x, 1 - y), (1 - x, 1 - y)]


def _any_specs(n):
    return [pl.BlockSpec(memory_space=pl.ANY)] * n


def _remote(src, dst, sems, k, dev):
    send_sems, recv_sems = sems
    return pltpu.make_async_remote_copy(src_ref=src, dst_ref=dst, send_sem=send_sems.at[k], recv_sem=recv_sems.at[k],
                                        device_id=dev, device_id_type=MESH)


EW_VMEM_BYTES = 24 * 1024 * 1024


def _fit_rows(rows, cols, blocks):
    return _row_tile(rows, max(16, int(EW_VMEM_BYTES // (8 * blocks)) // cols))


class _Geom:
    def __init__(self, n0, n1, blocks=1.0):
        self.n0, self.n1 = n0, n1
        self.axis = 0 if n0 % 32 == 0 else 1
        self.h0, self.h1 = (n0 // 2, n1) if self.axis == 0 else (n0, n1 // 2)
        self.tr = _fit_rows(self.h0, self.h1, blocks)
        self.nblk = self.h0 // self.tr

    def half_ref(self, ref, lead, idx):
        if self.axis == 0:
            return ref.at[(*lead, pl.ds(idx * self.h0, self.h0))]
        return ref.at[(*lead, slice(None), pl.ds(idx * self.h1, self.h1))]

    def half_block(self, lead, i, idx):
        return (*lead, idx * self.nblk + i, 0) if self.axis == 0 else (*lead, i, idx)


class _Comm:
    def __init__(self, ins, out_shapes, aliases, n_sems, start, finish, deliver):
        self.ins, self.out_shapes, self.aliases, self.n_sems = list(ins), list(out_shapes), dict(aliases), n_sems
        self.start, self.finish, self.deliver = start, finish, deliver

    def scratch(self):
        return [pltpu.SemaphoreType.DMA((self.n_sems,)), pltpu.SemaphoreType.DMA((self.n_sems,))]


def _comm_call(name, comm):
    n_in = len(comm.ins)

    def body(*refs):
        ins, outs, sems = refs[:n_in], refs[n_in:-2], refs[-2:]
        comm.start(ins, outs, sems)
        comm.finish(ins, outs, sems)

    res = pl.pallas_call(
        body, name=name, out_shape=comm.out_shapes, in_specs=_any_specs(n_in),
        out_specs=_any_specs(len(comm.out_shapes)), input_output_aliases=comm.aliases,
        scratch_shapes=comm.scratch())(*comm.ins)
    return comm.deliver(list(res))


def _hosted_call(body, *, name, grid, in_specs, out_specs, out_shape, args, scratch_shapes=(), dims=None, comm=None):
    in_specs, out_specs, out_shape = list(in_specs), list(out_specs), list(out_shape)
    if comm is None:
        return pl.pallas_call(
            body, name=name, grid=grid, in_specs=in_specs, out_specs=out_specs, out_shape=out_shape,
            scratch_shapes=list(scratch_shapes),
            compiler_params=_params(dims or ("arbitrary",) * len(grid)))(*args)
    n_in, n_out, n_ci, n_co = len(in_specs), len(out_specs), len(comm.ins), len(comm.out_shapes)

    def wrapped(*refs):
        ins, cins = refs[:n_in], refs[n_in:n_in + n_ci]
        outs = refs[n_in + n_ci:n_in + n_ci + n_out]
        couts = refs[n_in + n_ci + n_out:n_in + n_ci + n_out + n_co]
        scratch, sems = refs[n_in + n_ci + n_out + n_co:-2], refs[-2:]
        first = functools.reduce(jnp.logical_and, [pl.program_id(k) == 0 for k in range(len(grid))])
        last = functools.reduce(jnp.logical_and, [pl.program_id(k) == grid[k] - 1 for k in range(len(grid))])

        @pl.when(first)
        def _():
            comm.start(cins, couts, sems)

        body(*ins, *outs, *scratch)

        @pl.when(last)
        def _():
            comm.finish(cins, couts, sems)

    res = pl.pallas_call(
        wrapped, name=name, grid=grid, in_specs=in_specs + _any_specs(n_ci), out_specs=out_specs + _any_specs(n_co),
        out_shape=out_shape + comm.out_shapes,
        input_output_aliases={n_in + i: n_out + o for i, o in comm.aliases.items()},
        scratch_shapes=list(scratch_shapes) + comm.scratch(),
        compiler_params=_params(("arbitrary",) * len(grid)))(*args, *comm.ins)
    comm.deliver(list(res[n_out:]))
    return list(res[:n_out])


def _gather_comm(bufs, deliver):
    n = len(bufs)
    geoms = [_Geom(*b.shape[1:]) for b in bufs]

    def first(outs, sems):
        x, y, c, chips = _place()
        cps = []
        for a in range(n):
            mine = geoms[a].half_ref(outs[a], (2 * x + y,), c)
            cps += [_remote(mine, mine, sems, 6 * a + j, (cx, cy, c)) for j, (cx, cy) in enumerate(chips)]
        return cps

    def start(ins, outs, sems):
        for cp in first(outs, sems):
            cp.start()

    def finish(ins, outs, sems):
        x, y, c, chips = _place()
        sib = (x, y, 1 - c)
        passed = []
        for a in range(n):
            for j, (cx, cy) in enumerate(chips):
                land = geoms[a].half_ref(outs[a], (2 * cx + cy,), c)
                _remote(land, land, sems, 6 * a + j, sib).wait_recv()
                cp = _remote(land, land, sems, 6 * a + 3 + j, sib)
                cp.start()
                passed.append(cp)
        for a in range(n):
            for j, (cx, cy) in enumerate(chips):
                land = geoms[a].half_ref(outs[a], (2 * cx + cy,), 1 - c)
                _remote(land, land, sems, 6 * a + 3 + j, sib).wait_recv()
        for cp in first(outs, sems) + passed:
            cp.wait_send()

    return _Comm(bufs, [jax.ShapeDtypeStruct(b.shape, b.dtype) for b in bufs], {a: a for a in range(n)}, 6 * n,
                 start, finish, deliver)


def _reduce_pair_comm(grads, deliver):
    n = len(grads)
    geoms = [_Geom(*a.shape[1:]) for a in grads]

    def copies(ins, outs, sems):
        x, y, c, _ = _place()
        return [_remote(geoms[a].half_ref(ins[a], (slice(None),), 1 - c), outs[a], sems, a, (x, y, 1 - c))
                for a in range(n)]

    def start(ins, outs, sems):
        for cp in copies(ins, outs, sems):
            cp.start()

    def finish(ins, outs, sems):
        cps = copies(ins, outs, sems)
        for cp in cps:
            cp.wait_recv()
        for cp in cps:
            cp.wait_send()

    shapes = [jax.ShapeDtypeStruct((N_SHARD, g.h0, g.h1), a.dtype) for a, g in zip(grads, geoms)]
    return _Comm(grads, shapes, {}, n, start, finish, deliver)


def _reduce_chips_comm(parts, deliver):
    n = len(parts)

    def copies(ins, outs, sems):
        x, y, c, chips = _place()
        return [_remote(ins[a].at[2 * cx + cy], outs[a].at[j], sems, 3 * a + j, (cx, cy, c))
                for a in range(n) for j, (cx, cy) in enumerate(chips)]

    def start(ins, outs, sems):
        for cp in copies(ins, outs, sems):
            cp.start()

    def finish(ins, outs, sems):
        cps = copies(ins, outs, sems)
        for cp in cps:
            cp.wait_recv()
        for cp in cps:
            cp.wait_send()

    shapes = [jax.ShapeDtypeStruct((3,) + a.shape[1:], a.dtype) for a in parts]
    return _Comm(parts, shapes, {}, 3 * n, start, finish, deliver)


def _share_pair_comm(bufs, deliver):
    n = len(bufs)
    geoms = [_Geom(*b.shape) for b in bufs]

    def copies(outs, sems):
        x, y, c, _ = _place()
        cps = []
        for a in range(n):
            mine = geoms[a].half_ref(outs[a], (), c)
            cps.append(_remote(mine, mine, sems, a, (x, y, 1 - c)))
        return cps

    def start(ins, outs, sems):
        for cp in copies(outs, sems):
            cp.start()

    def finish(ins, outs, sems):
        x, y, c, _ = _place()
        for a in range(n):
            land = geoms[a].half_ref(outs[a], (), 1 - c)
            _remote(land, land, sems, a, (x, y, 1 - c)).wait_recv()
        for cp in copies(outs, sems):
            cp.wait_send()

    return _Comm(bufs, [jax.ShapeDtypeStruct(b.shape, b.dtype) for b in bufs], {a: a for a in range(n)}, n,
                 start, finish, deliver)


def _gather_small_call(pack):
    r, d = pack.shape

    def body(in_ref, out_ref, send_sems, recv_sems):
        x, y, c, _ = _place()
        me = 4 * x + 2 * y + c
        out_ref[me] = in_ref[...]
        cps = []
        for k in range(1, 8):
            bx, by, bc = (k >> 2) & 1, (k >> 1) & 1, k & 1
            px, py, pc = x ^ bx, y ^ by, c ^ bc
            cp = pltpu.make_async_remote_copy(
                src_ref=in_ref, dst_ref=out_ref.at[me], send_sem=send_sems.at[k - 1], recv_sem=recv_sems.at[k - 1],
                device_id=(px, py, pc), device_id_type=MESH)
            cp.start()
            cps.append((cp, 4 * px + 2 * py + pc))
        for k, (cp, peer) in enumerate(cps):
            land = out_ref.at[peer]
            pltpu.make_async_remote_copy(
                src_ref=land, dst_ref=land, send_sem=send_sems.at[k], recv_sem=recv_sems.at[k],
                device_id=(x, y, c), device_id_type=MESH).wait_recv()
        for cp, _ in cps:
            cp.wait_send()

    return pl.pallas_call(
        body, name="gather_small",
        out_shape=jax.ShapeDtypeStruct((8, r, d), pack.dtype),
        in_specs=[pl.BlockSpec(memory_space=pltpu.VMEM)], out_specs=pl.BlockSpec(memory_space=pltpu.VMEM),
        scratch_shapes=[pltpu.SemaphoreType.DMA((7,)), pltpu.SemaphoreType.DMA((7,))])(pack)


def _ew_call(name, fn, ins, out_dtypes):
    shape = ins[0].shape
    cols = shape[-1]
    rows = 1
    for s_ in shape[:-1]:
        rows *= s_
    ins2 = [a.reshape(rows, cols) for a in ins]
    tr = rows
    for t in range(16, min(rows, max(16, (1 << 19) // cols)) + 1, 16):
        if rows % t == 0:
            tr = t
    no = len(out_dtypes)

    def body(*refs):
        outs = fn(*[r[...] for r in refs[:len(ins2)]])
        for ref, val in zip(refs[len(ins2):], outs):
            ref[...] = val.astype(ref.dtype)

    spec = pl.BlockSpec((tr, cols), lambda i: (i, 0))
    res = pl.pallas_call(
        body, name=name, grid=(rows // tr,), in_specs=[spec] * len(ins2), out_specs=[spec] * no,
        out_shape=[jax.ShapeDtypeStruct((rows, cols), dt) for dt in out_dtypes],
        compiler_params=_params(("parallel",)))(*ins2)
    return [r.reshape(shape) for r in res]


def _adamw_math(w, g, m, v):
    m = ADAM_B1 * m + (1.0 - ADAM_B1) * g
    v = ADAM_B2 * v + (1.0 - ADAM_B2) * (g * g)
    m_hat = m / (1.0 - ADAM_B1 ** ADAM_STEP)
    v_hat = v / (1.0 - ADAM_B2 ** ADAM_STEP)
    delta = -ADAM_LR * (m_hat / (jnp.sqrt(v_hat) + ADAM_EPS) + ADAM_WD * w)
    return delta, m, v


def _adamw_call(name, w, g, m, v):
    return _ew_call(name, _adamw_math, [w, g, m, v], [F32, F32, F32])


def _tiled_call(name, fn, place, grid, in_items, out_items):
    ni = len(in_items)

    def body(place_ref, *refs):
        vals = fn(*[r[...] for r in refs[:ni]])
        for ref, val in zip(refs[ni:], vals):
            ref[...] = val.astype(ref.dtype)

    spec = pltpu.PrefetchScalarGridSpec(
        num_scalar_prefetch=1, grid=grid,
        in_specs=[pl.BlockSpec(blk, imap) for _, blk, imap in in_items],
        out_specs=[pl.BlockSpec(blk, imap) for _, _, blk, imap in out_items])
    return pl.pallas_call(
        body, name=name, grid_spec=spec,
        out_shape=[jax.ShapeDtypeStruct(shp, dt) for shp, dt, _, _ in out_items],
        compiler_params=_params(("arbitrary",) * len(grid)))(place, *[a for a, _, _ in in_items])


def _cast_call(name, place, shards):
    n0, n1 = shards[0].shape
    tr = _fit_rows(n0, n1, 1.5 * len(shards))
    ins = [(a, (tr, n1), lambda i, p: (i, 0)) for a in shards]
    outs = [((N_SHARD, n0, n1), BF16, (1, tr, n1), lambda i, p: (p[0], i, 0)) for _ in shards]
    return _tiled_call(name, lambda *v: [x[None] for x in v], place, (n0 // tr,), ins, outs)


def _pair_sum_call(name, place, fulls, gots):
    k = len(fulls)
    g = _Geom(*fulls[0].shape[1:], blocks=2.5 * k)
    blk = (1, g.tr, g.h1)
    ins = [(a, blk, lambda s, i, p: g.half_block((s,), i, p[1])) for a in fulls]
    ins += [(a, blk, lambda s, i, p: (s, i, 0)) for a in gots]
    outs = [((N_SHARD, g.h0, g.h1), BF16, blk, lambda s, i, p: (s, i, 0)) for _ in fulls]
    return _tiled_call(name, lambda *v: [v[j] + v[k + j] for j in range(k)], place, (N_SHARD, g.nblk), ins, outs)


def _chip_sum_call(name, place, fulls, gots, recvs):
    k = len(fulls)
    g = _Geom(*fulls[0].shape[1:], blocks=4.5 * k)
    blk = (1, g.tr, g.h1)
    ins = [(a, blk, lambda i, p: g.half_block((p[0],), i, p[1])) for a in fulls]
    ins += [(a, blk, lambda i, p: (p[0], i, 0)) for a in gots]
    ins += [(a, (3, g.tr, g.h1), lambda i, p: (0, i, 0)) for a in recvs]
    outs = [((g.n0, g.n1), F32, (g.tr, g.h1), lambda i, p: g.half_block((), i, p[1])) for _ in fulls]

    def fn(*v):
        res = []
        for j in range(k):
            r = v[2 * k + j].astype(F32)
            res.append(v[j][0] + v[k + j][0] + r[0] + r[1] + r[2])
        return res

    return _tiled_call(name, fn, place, (g.nblk,), ins, outs)


def _adamw_group_call(name, ws, gs, ms, vs, comm=None):
    k = len(ws)
    n0, n1 = ws[0].shape
    tr = _fit_rows(n0, n1, 8 * k)
    spec = pl.BlockSpec((tr, n1), lambda i: (i, 0))

    def body(*refs):
        for j in range(k):
            g = refs[k + j][...]
            delta, m, vv = _adamw_math(refs[j][...], g, refs[2 * k + j][...], refs[3 * k + j][...])
            for ref, val in zip(refs[4 * k + 4 * j:4 * k + 4 * j + 4], (g, delta, m, vv)):
                ref[...] = val

    flat = _hosted_call(
        body, name=name, grid=(n0 // tr,), in_specs=[spec] * (4 * k), out_specs=[spec] * (4 * k),
        out_shape=[jax.ShapeDtypeStruct((n0, n1), F32)] * (4 * k), dims=("parallel",),
        args=(*ws, *gs, *ms, *vs), comm=comm)
    return [flat[4 * j:4 * j + 4] for j in range(k)]


def _small_update_call(gathered, wp, mp, vp):
    nd, r, d = gathered.shape

    def body(g_ref, w_ref, m_ref, v_ref, gs_ref, d_ref, nm_ref, nv_ref):
        gs = g_ref[0]
        for k in range(1, nd):
            gs = gs + g_ref[k]
        gs_ref[...] = gs
        delta, m, v = _adamw_math(w_ref[...], gs[0:SMALL_ADAM_ROWS], m_ref[...], v_ref[...])
        d_ref[...] = delta
        nm_ref[...] = m
        nv_ref[...] = v

    vm = pl.BlockSpec(memory_space=pltpu.VMEM)
    ashape = jax.ShapeDtypeStruct((SMALL_ADAM_ROWS, d), F32)
    return pl.pallas_call(
        body, name="small_update", in_specs=[vm] * 4, out_specs=[vm] * 4,
        out_shape=[jax.ShapeDtypeStruct((r, d), F32), ashape, ashape, ashape],
        compiler_params=pltpu.CompilerParams(vmem_limit_bytes=VMEM_LIMIT_BYTES))(gathered, wp, mp, vp)


SMALL_NAMES = ["ffn1_norm", "mix_norm", "ffn2_norm", "final_norm", "q_latent_norm", "kv_latent_norm",
               "q_head_norm", "k_head_norm", "conv_b", "gate_a_b", "gate_x_b", "lru_lambda", "attn_out_norm",
               "lru_out_norm"]
ROW_CONV_W = 14
ROW_GATE_A = 16
ROW_GATE_X = 48
ROW_META = 80
ROW_LOSS = 96


def _row(a):
    flat = a.reshape(1, -1)
    return jnp.pad(flat, ((0, 0), (0, D_MODEL - flat.shape[1])))


def _pack_small(t, rows):
    parts = [_row(t[nm]) for nm in SMALL_NAMES]
    parts.append(t["conv_w"].reshape(2, D_MODEL))
    parts.append(t["gate_a_w"].reshape(32, D_MODEL))
    parts.append(t["gate_x_w"].reshape(32, D_MODEL))
    p = jnp.concatenate(parts, axis=0)
    return jnp.pad(p, ((0, rows - p.shape[0]), (0, 0)))


def _unpack_small(p, like):
    out = {}
    for k, nm in enumerate(SMALL_NAMES):
        out[nm] = p[k, 0:like[nm].size].reshape(like[nm].shape)
    out["gate_a_w"] = p[ROW_GATE_A:ROW_GATE_A + 32].reshape(like["gate_a_w"].shape)
    out["gate_x_w"] = p[ROW_GATE_X:ROW_GATE_X + 32].reshape(like["gate_x_w"].shape)
    return out


def _gate_dense(wg):
    w4 = wg[0].reshape(N_LRU_TILES, 2, 64, 64)
    zero = jnp.zeros((N_LRU_TILES, 64, 64), wg.dtype)
    top = jnp.concatenate([w4[:, 0], zero], axis=2)
    bot = jnp.concatenate([zero, w4[:, 1]], axis=2)
    return jnp.concatenate([top, bot], axis=1).astype(BF16)


def _gate_blocks(dw):
    return jnp.stack([dw[:, 0:64, 0:64], dw[:, 64:128, 64:128]], axis=1).reshape(8, 64, 64)


BIG_NAMES = ["ffn1_w_gate", "ffn1_w_up", "ffn1_w_down", "w_in", "w_uq", "w_uk", "w_uv", "w_out", "ffn2_w_gate",
             "ffn2_w_up", "ffn2_w_down"]
BIG_GROUPS = [["ffn1_w_gate", "ffn1_w_up", "ffn1_w_down", "ffn2_w_gate", "ffn2_w_up", "ffn2_w_down"], ["w_in"],
              ["w_uq"], ["w_uk", "w_uv"], ["w_out"]]
TRANSPOSED = ("ffn1_w_gate", "ffn1_w_up", "ffn2_w_gate", "ffn2_w_up", "w_in", "w_uq")


def _to2d(nm, a):
    return a[0].T if nm in TRANSPOSED else a[0]


def _from2d(nm, a):
    return (a.T if nm in TRANSPOSED else a)[None]


WEIGHT_NAMES = ["meta_tokens", "ffn1_norm", "ffn1_w_gate", "ffn1_w_up", "ffn1_w_down", "mix_norm", "w_in",
                "q_latent_norm", "w_uq", "kv_latent_norm", "w_uk", "w_uv", "q_head_norm", "k_head_norm", "conv_w",
                "conv_b", "gate_a_w", "gate_a_b", "gate_x_w", "gate_x_b", "lru_lambda", "attn_out_norm",
                "lru_out_norm", "w_out", "ffn2_norm", "ffn2_w_gate", "ffn2_w_up", "ffn2_w_down", "final_norm"]


def _weight_from(nm, slots):
    if nm == "w_in":
        win = slots.reshape(IN_WIDTH, D_MODEL)
        return jnp.concatenate([win[0:Z_KR + D_ROPE], jnp.zeros((128 - D_ROPE, D_MODEL), BF16),
                                win[Z_KR + D_ROPE:]], axis=0)
    if nm == "w_uq":
        return jnp.pad(slots, ((0, 0), (0, D_QKP - D_QK), (0, 0))).reshape(HEADS * D_QKP, Q_RANK)
    if nm in ("w_uk", "w_uv"):
        return slots.transpose(1, 0, 2).reshape(KV_RANK, HEADS * D_NOPE)
    if nm == "w_out":
        return slots.reshape(D_MODEL, D_MODEL)
    return slots


def _small_weights(p, small):
    w = {nm: p[nm] for nm in SMALL_NAMES}
    w["q_head_norm"] = jnp.pad(p["q_head_norm"], ((0, 0), (0, D_QKP - D_QK)))
    w["k_head_norm"] = jnp.pad(p["k_head_norm"], ((0, 0), (0, D_QKP - D_QK)))
    w["conv_w"] = small[:, N_META:N_META + 2, :].reshape(N_SHARD, CONV_K, LRU_TILE)
    w["gate_a_w"] = _gate_dense(p["gate_a_w"])
    w["gate_x_w"] = _gate_dense(p["gate_x_w"])
    meta = small[:, 0:N_META, :].transpose(1, 0, 2).reshape(N_META, D_MODEL)
    return w, meta


def _full_weights(p, gathered, small):
    w, meta = _small_weights(p, small)
    w.update({nm: _weight_from(nm, gathered[nm]) for nm in BIG_NAMES})
    return w, meta


def _shard_grad(nm, g):
    if nm == "w_in":
        return jnp.concatenate([g[0:Z_KR + D_ROPE], g[Z_MLA:]], axis=0).reshape(N_SHARD, IN_WIDTH // N_SHARD, D_MODEL)
    if nm == "w_uq":
        return g.reshape(HEADS, D_QKP, Q_RANK)[:, 0:D_QK, :]
    if nm in ("w_uk", "w_uv"):
        return g.reshape(KV_RANK, HEADS, D_NOPE).transpose(1, 0, 2)
    if nm == "w_out":
        return g.reshape(N_SHARD, D_MODEL // N_SHARD, D_MODEL)
    return g


def _shard_grads(g):
    return {nm: _shard_grad(nm, g[nm]) for nm in BIG_NAMES}


GATHER_FIRST = ["ffn1_w_gate", "ffn1_w_up", "ffn1_w_down"]
GATHER_AT = {"ffn1_up": ["w_in", "w_uq", "w_uk", "w_uv", "w_out"], "attn_fwd": ["ffn2_w_down"],
             "lru_fwd": ["ffn2_w_gate", "ffn2_w_up"]}
REDUCE_PLAN = [(["ffn2_w_gate", "ffn2_w_up", "ffn2_w_down"], "ffn2_din", "lru_bwd"),
               (["w_out", "w_uq", "w_uk", "w_uv", "w_in"], "mix_din", "ffn1_dact"),
               (["ffn1_w_gate", "ffn1_w_up", "ffn1_w_down"], "ffn1_din", None)]


def _same_shape_groups(names):
    return [[nm for nm in grp if nm in names] for grp in BIG_GROUPS if any(nm in names for nm in grp)]


class _Sched:
    def __init__(self, place, w, slots):
        self.place, self.w, self.slots = place, w, slots
        self.g = None
        self.sharded, self.from_pair, self.chip_bf16, self.from_chips = {}, {}, {}, {}

    def host(self, stage):
        if stage in GATHER_AT:
            return self.gather(GATHER_AT[stage])
        for names, pair_at, chips_at in REDUCE_PLAN:
            if stage == pair_at:
                return self.pair(names)
            if stage == chips_at:
                return self.chips(names)
        return None

    def gather(self, names):
        def deliver(outs):
            self.w.update({nm: _weight_from(nm, o) for nm, o in zip(names, outs)})
            return outs

        return _gather_comm([self.slots[nm] for nm in names], deliver)

    def pair(self, names):
        self.sharded.update({nm: _shard_grad(nm, self.g[nm]) for nm in names})

        def deliver(outs):
            self.from_pair.update(zip(names, outs))
            for grp in _same_shape_groups(names):
                sums = _pair_sum_call("pair_sum_" + grp[0], self.place, [self.sharded[nm] for nm in grp],
                                      [self.from_pair[nm] for nm in grp])
                self.chip_bf16.update(zip(grp, sums))
            return outs

        return _reduce_pair_comm([self.sharded[nm] for nm in names], deliver)

    def chips(self, names):
        def deliver(outs):
            self.from_chips.update(zip(names, outs))
            return outs

        return _reduce_chips_comm([self.chip_bf16[nm] for nm in names], deliver)

    def chip_sums(self, names):
        out = {}
        for grp in _same_shape_groups(names):
            sums = _chip_sum_call("chip_sum_" + grp[0], self.place, [self.sharded[nm] for nm in grp],
                                  [self.from_pair[nm] for nm in grp], [self.from_chips[nm] for nm in grp])
            out.update(zip(grp, sums))
        return out


def kernel(x, meta_tokens, ffn1_norm, ffn1_w_gate, ffn1_w_up, ffn1_w_down, mix_norm, w_in, q_latent_norm, w_uq, kv_latent_norm, w_uk, w_uv, q_head_norm, k_head_norm, conv_w, conv_b, gate_a_w, gate_a_b, gate_x_w, gate_x_b, lru_lambda, attn_out_norm, lru_out_norm, w_out, ffn2_norm, ffn2_w_gate, ffn2_w_up, ffn2_w_down, final_norm, loss_target, m_meta_tokens, m_ffn1_norm, m_ffn1_w_gate, m_ffn1_w_up, m_ffn1_w_down, m_mix_norm, m_w_in, m_q_latent_norm, m_w_uq, m_kv_latent_norm, m_w_uk, m_w_uv, m_q_head_norm, m_k_head_norm, m_conv_w, m_conv_b, m_gate_a_w, m_gate_a_b, m_gate_x_w, m_gate_x_b, m_lru_lambda, m_attn_out_norm, m_lru_out_norm, m_w_out, m_ffn2_norm, m_ffn2_w_gate, m_ffn2_w_up, m_ffn2_w_down, m_final_norm, v_meta_tokens, v_ffn1_norm, v_ffn1_w_gate, v_ffn1_w_up, v_ffn1_w_down, v_mix_norm, v_w_in, v_q_latent_norm, v_w_uq, v_kv_latent_norm, v_w_uk, v_w_uv, v_q_head_norm, v_k_head_norm, v_conv_w, v_conv_b, v_gate_a_w, v_gate_a_b, v_gate_x_w, v_gate_x_b, v_lru_lambda, v_attn_out_norm, v_lru_out_norm, v_w_out, v_ffn2_norm, v_ffn2_w_gate, v_ffn2_w_up, v_ffn2_w_down, v_final_norm):
    args = locals()
    p = {nm: args[nm] for nm in WEIGHT_NAMES}
    mom = {nm: args["m_" + nm] for nm in WEIGHT_NAMES}
    var = {nm: args["v_" + nm] for nm in WEIGHT_NAMES}
    nb, seq, d = x.shape
    lp = CHUNK + seq
    xi, yi, ci = lax.axis_index("x"), lax.axis_index("y"), lax.axis_index("c")
    chip = 2 * xi + yi

    place = jnp.stack([chip, ci]).astype(jnp.int32)
    p2 = {nm: _to2d(nm, p[nm]) for nm in BIG_NAMES}
    m2 = {nm: _to2d(nm, mom[nm]) for nm in BIG_NAMES}
    v2 = {nm: _to2d(nm, var[nm]) for nm in BIG_NAMES}

    slots = {}
    for grp in BIG_GROUPS:
        for nm, buf in zip(grp, _cast_call("cast_" + grp[0], place, [p2[nm] for nm in grp])):
            slots[nm] = buf
    small_shard = jnp.concatenate(
        [meta_tokens, conv_w[0].reshape(2, 2 * LRU_TILE), jnp.zeros((14, 2 * LRU_TILE), F32)], axis=0)
    small_slots = lax.dynamic_update_slice(jnp.zeros((N_SHARD,) + small_shard.shape, F32), small_shard[None],
                                           (chip, 0, 0))
    first = _comm_call("gather_first", _gather_comm([slots[nm] for nm in GATHER_FIRST] + [small_slots], lambda o: o))
    w, meta = _small_weights(p, first[-1])
    w.update({nm: _weight_from(nm, o) for nm, o in zip(GATHER_FIRST, first[:-1])})
    sched = _Sched(place, w, slots)

    h0 = jnp.concatenate(
        [jnp.zeros((nb, PAD_ROWS, d), F32), jnp.broadcast_to(meta[None], (nb, N_META, d)), x], axis=1)
    target = jnp.pad(loss_target, ((0, 0), (CHUNK, 0), (0, 0)))
    loss_part, dh0, g = _local_step(h0.reshape(nb * lp, d), target.reshape(nb * lp, d), w, nb, lp, sched)
    dh0 = dh0.reshape(nb, lp, d)
    grad_x = dh0[:, CHUNK:, :]

    gs = {nm: g[nm] for nm in SMALL_NAMES}
    gs["q_head_norm"] = g["q_head_norm"][:, 0:D_QK]
    gs["k_head_norm"] = g["k_head_norm"][:, 0:D_QK]
    for nm in ("conv_b", "gate_a_b", "gate_x_b", "lru_lambda"):
        gs[nm] = g[nm].reshape(1, LRU_W)
    gs["conv_w"] = g["conv_w"].transpose(1, 0, 2).reshape(CONV_K, LRU_W)
    gs["gate_a_w"] = _gate_blocks(g["gate_a_w"])
    gs["gate_x_w"] = _gate_blocks(g["gate_x_w"])
    pack = _pack_small(gs, ROW_META)
    pack = jnp.concatenate([pack, jnp.sum(dh0[:, PAD_ROWS:CHUNK, :], axis=0), _row(loss_part[:, 0:1]),
                            jnp.zeros((SMALL_ROWS - ROW_LOSS - 1, D_MODEL), F32)], axis=0)
    small_like = {nm: p[nm] for nm in SMALL_NAMES + ["gate_a_w", "gate_x_w"]}

    def pack_w(t):
        tt = {nm: t[nm] for nm in SMALL_NAMES + ["gate_a_w", "gate_x_w"]}
        tt["conv_w"] = jnp.zeros((CONV_K, LRU_W), F32)
        return _pack_small(tt, SMALL_ADAM_ROWS)

    gsum, dsm, msm, vsm = _small_update_call(_gather_small_call(pack), pack_w(p), pack_w(mom), pack_w(var))
    grads = _unpack_small(gsum, small_like)
    delta = _unpack_small(dsm, small_like)
    new_m = _unpack_small(msm, small_like)
    new_v = _unpack_small(vsm, small_like)
    loss = gsum[ROW_LOSS, 0]
    gmeta = gsum[ROW_META:ROW_META + N_META].reshape(N_META, N_SHARD, D_MODEL // N_SHARD)
    grads["meta_tokens"] = lax.dynamic_index_in_dim(gmeta, chip, axis=1, keepdims=False)
    gconv = gsum[ROW_CONV_W:ROW_CONV_W + 2].reshape(CONV_K, N_SHARD, LRU_TILE)
    grads["conv_w"] = lax.dynamic_index_in_dim(gconv, chip, axis=1, keepdims=False)[None]
    for nm in ("meta_tokens", "conv_w"):
        delta[nm], new_m[nm], new_v[nm] = _adamw_call("adamw_" + nm, p[nm], grads[nm], mom[nm], var[nm])

    done = REDUCE_PLAN[0][0] + REDUCE_PLAN[1][0]
    last = REDUCE_PLAN[2][0]
    mine = sched.chip_sums(done)
    shard_grads = dict(zip(done, _comm_call("share_pair", _share_pair_comm([mine[nm] for nm in done], lambda o: o))))

    def adamw(names, comm=None):
        res = _adamw_group_call("adamw_" + names[0], [p2[nm] for nm in names], [shard_grads[nm] for nm in names],
                                [m2[nm] for nm in names], [v2[nm] for nm in names], comm)
        for nm, (gg, dd, mm, vv) in zip(names, res):
            grads[nm], delta[nm], new_m[nm], new_v[nm] = (_from2d(nm, t) for t in (gg, dd, mm, vv))

    groups = _same_shape_groups(done)
    adamw(groups[0], sched.chips(last))
    for grp in groups[1:]:
        adamw(grp)
    mine = sched.chip_sums(last)
    shard_grads = dict(zip(last, _comm_call("share_last", _share_pair_comm([mine[nm] for nm in last], lambda o: o))))
    adamw(last)

    return (loss, grad_x, *[grads[nm] for nm in WEIGHT_NAMES], *[delta[nm] for nm in WEIGHT_NAMES],
            *[new_m[nm] for nm in WEIGHT_NAMES], *[new_v[nm] for nm in WEIGHT_NAMES])
```

```python
import functools
import math

import jax
import jax.numpy as jnp
from jax import lax
from jax.experimental import pallas as pl
from jax.experimental.pallas import tpu as pltpu

F32 = jnp.float32
BF16 = jnp.bfloat16
MESH = pl.DeviceIdType.MESH

D_MODEL = 1024
N_META = 16
CHUNK = 64
PAD_ROWS = CHUNK - N_META
HEADS = 4
D_NOPE = 128
D_ROPE = 64
D_QK = D_NOPE + D_ROPE
D_QKP = 256
D_V = 128
KV_RANK = 256
Q_RANK = 384
MLA_W = HEADS * D_V
LRU_W = 512
LRU_TILE = 128
N_LRU_TILES = LRU_W // LRU_TILE
CONV_K = 4
C_RGLRU = 8.0
ROPE_THETA = 10000.0
D_FF = 2816
N_SHARD = 4
EPS = 1e-6
NEG_INF = -1e30
Z_KR = Q_RANK + KV_RANK
Z_MLA = Z_KR + 128
Z_U = Z_MLA
Z_G = Z_U + LRU_W
Z_W = Z_G + LRU_W
IN_WIDTH = Q_RANK + KV_RANK + D_ROPE + 2 * LRU_W

ADAM_LR = 0.001
ADAM_B1 = 0.9
ADAM_B2 = 0.999
ADAM_EPS = 1e-08
ADAM_WD = 0.01
ADAM_STEP = 10

VMEM_LIMIT_BYTES = 56 * 1024 * 1024
SMALL_ROWS = 104
SMALL_ADAM_ROWS = 80


def _params(sem):
    return pltpu.CompilerParams(dimension_semantics=sem, vmem_limit_bytes=VMEM_LIMIT_BYTES)


def _row_tile(rows, target):
    best = 16
    for t in range(16, min(rows, target) + 1, 16):
        if rows % t == 0:
            best = t
    return best


def _col_tile(cols, target):
    best = cols
    for t in range(128, min(cols, target) + 1, 128):
        if cols % t == 0:
            best = t
    return best


def _dot(a, b):
    return jnp.dot(a, b, preferred_element_type=F32)


def _dot_nt(a, b):
    return lax.dot_general(a, b, (((1,), (1,)), ((), ())), preferred_element_type=F32)


def _dot_tn(a, b):
    return lax.dot_general(a, b, (((0,), (0,)), ((), ())), preferred_element_type=F32)


def _rms(x, n):
    return lax.rsqrt(jnp.sum(x * x, axis=-1, keepdims=True) * (1.0 / n) + EPS)


def _rms_bwd(dn, nrm, r, n):
    return r * (dn - nrm * (jnp.sum(dn * nrm, axis=-1, keepdims=True) * (1.0 / n)))


def _gelu(x):
    k = math.sqrt(2.0 / math.pi)
    t = jnp.tanh(k * (x + 0.044715 * x * x * x))
    return 0.5 * x * (1.0 + t), t


def _gelu_grad(x, t):
    k = math.sqrt(2.0 / math.pi)
    return 0.5 * (1.0 + t) + 0.5 * x * (1.0 - t * t) * k * (1.0 + 3.0 * 0.044715 * x * x)


def _neg_expm1(x):
    series = -x * (1.0 + x * (0.5 + x * (1.0 / 6 + x * (1.0 / 24 + x * (1.0 / 120 + x * (1.0 / 720))))))
    return jnp.where(x > -0.25, series, 1.0 - jnp.exp(x))


def _softplus_neg(lam):
    e = jnp.exp(-jnp.abs(lam))
    log1p = jnp.where(e < 0.01, e * (1.0 - e * (0.5 - e * (1.0 / 3 - e * 0.25))), jnp.log(1.0 + e))
    return jnp.maximum(-lam, 0.0) + log1p


def _rope(t, c, s1, s2):
    return t * c + pltpu.roll(t, 96, 1) * s1 + pltpu.roll(t, 32, 1) * s2


def _rope_t(d, c, s1, s2):
    return d * c + pltpu.roll(d * s1, 32, 1) + pltpu.roll(d * s2, 96, 1)


def _rope_tables(lp):
    pos = (jnp.arange(lp, dtype=jnp.int32) - PAD_ROWS).astype(F32)
    inv_freq = ROPE_THETA ** (-jnp.arange(0, D_ROPE // 2, dtype=F32) / (D_ROPE // 2))
    ang = pos[:, None] * inv_freq[None, :]
    cos, sin = jnp.cos(ang), jnp.sin(ang)
    z = jnp.zeros_like(cos)
    return (jnp.concatenate([cos, cos, z, z], 1), jnp.concatenate([-sin, z, z, z], 1),
            jnp.concatenate([z, sin, z, z], 1))


def _rmsnorm_call(name, h, g, tm):
    rows, d = h.shape

    def body(h_ref, g_ref, o_ref):
        x = h_ref[...]
        o_ref[...] = (x * _rms(x, d) * g_ref[...]).astype(BF16)

    return pl.pallas_call(
        body, name=name, grid=(rows // tm,),
        in_specs=[pl.BlockSpec((tm, d), lambda i: (i, 0)), pl.BlockSpec((1, d), lambda i: (0, 0))],
        out_specs=pl.BlockSpec((tm, d), lambda i: (i, 0)),
        out_shape=jax.ShapeDtypeStruct((rows, d), BF16),
        compiler_params=_params(("parallel",)))(h, g)


def _ffn_up_call(name, u, wg, wu, tm, comm=None):
    rows, d = u.shape
    ns, fs, _ = wg.shape

    def body(u_ref, wg_ref, wu_ref, g_ref, p_ref, a_ref):
        uu = u_ref[...]
        g = _dot_nt(uu, wg_ref[0])
        p = _dot_nt(uu, wu_ref[0])
        g_ref[0] = g.astype(BF16)
        p_ref[0] = p.astype(BF16)
        a_ref[0] = (g * jax.nn.sigmoid(g) * p).astype(BF16)

    wspec = pl.BlockSpec((1, fs, d), lambda s, i: (s, 0, 0))
    ospec = pl.BlockSpec((1, tm, fs), lambda s, i: (s, i, 0))
    oshape = jax.ShapeDtypeStruct((ns, rows, fs), BF16)
    return _hosted_call(
        body, name=name, grid=(ns, rows // tm),
        in_specs=[pl.BlockSpec((tm, d), lambda s, i: (i, 0)), wspec, wspec],
        out_specs=[ospec, ospec, ospec], out_shape=[oshape, oshape, oshape],
        dims=("parallel", "parallel"), args=(u, wg, wu), comm=comm)


def _ffn_down_call(name, a, wd, h, tm):
    rows, d = h.shape
    ns, _, fs = a.shape

    def body(a_ref, wd_ref, h_ref, o_ref):
        acc = h_ref[...]
        for s in range(ns):
            acc = acc + 0.5 * _dot(a_ref[s], wd_ref[s])
        o_ref[...] = acc

    return pl.pallas_call(
        body, name=name, grid=(rows // tm,),
        in_specs=[pl.BlockSpec((ns, tm, fs), lambda i: (0, i, 0)),
                  pl.BlockSpec((ns, fs, d), lambda i: (0, 0, 0)),
                  pl.BlockSpec((tm, d), lambda i: (i, 0))],
        out_specs=pl.BlockSpec((tm, d), lambda i: (i, 0)),
        out_shape=jax.ShapeDtypeStruct((rows, d), F32),
        compiler_params=_params(("parallel",)))(a, wd, h)


def _mm_call(name, a, bt, tm, out_dtype):
    rows, k = a.shape
    n = bt.shape[0]

    def body(a_ref, b_ref, o_ref):
        o_ref[...] = _dot_nt(a_ref[...], b_ref[...]).astype(out_dtype)

    return pl.pallas_call(
        body, name=name, grid=(rows // tm,),
        in_specs=[pl.BlockSpec((tm, k), lambda i: (i, 0)), pl.BlockSpec((n, k), lambda i: (0, 0))],
        out_specs=pl.BlockSpec((tm, n), lambda i: (i, 0)),
        out_shape=jax.ShapeDtypeStruct((rows, n), out_dtype),
        compiler_params=_params(("parallel",)))(a, bt)


def _mla_heads(z, gql, gkvl, wuq, wuk, wuv):
    cq = z[:, 0:Q_RANK]
    ckv = z[:, Q_RANK:Z_KR]
    kr = z[:, Z_KR:Z_MLA]
    rq = _rms(cq, Q_RANK)
    nq = cq * rq
    cqn = (nq * gql).astype(BF16)
    rkv = _rms(ckv, KV_RANK)
    nkv = ckv * rkv
    ckvn = (nkv * gkvl).astype(BF16)
    qraw = _dot_nt(cqn, wuq)
    knope = _dot(ckvn, wuk)
    v = _dot(ckvn, wuv)
    skr = jnp.sum(kr * kr, axis=-1, keepdims=True)
    heads = []
    for hd in range(HEADS):
        qh = qraw[:, hd * D_QKP:(hd + 1) * D_QKP]
        rqh = lax.rsqrt(jnp.sum(qh * qh, axis=-1, keepdims=True) * (1.0 / D_QK) + EPS)
        kn = knope[:, hd * D_NOPE:(hd + 1) * D_NOPE]
        rkh = lax.rsqrt((jnp.sum(kn * kn, axis=-1, keepdims=True) + skr) * (1.0 / D_QK) + EPS)
        heads.append((qh * rqh, rqh, kn * rkh, kr * rkh, rkh))
    return dict(rq=rq, nq=nq, cqn=cqn, rkv=rkv, nkv=nkv, ckvn=ckvn, v=v, heads=heads)


def _mla_prep_call(z, gql, gkvl, gqh, gkh, wuq, wuk, wuv, tabs, lp, tm):
    rows = z.shape[0]
    tpe = lp // tm

    def body(z_ref, gql_ref, gkvl_ref, gqh_ref, gkh_ref, wuq_ref, wuk_ref, wuv_ref, c_ref, s1_ref, s2_ref,
             q_ref, k_ref, v_ref, cqn_ref, ckvn_ref):
        m = _mla_heads(z_ref[...], gql_ref[...], gkvl_ref[...], wuq_ref[...], wuk_ref[...], wuv_ref[...])
        c, s1, s2 = c_ref[...], s1_ref[...], s2_ref[...]
        gq, gk = gqh_ref[...], gkh_ref[...]
        row = (pl.program_id(0) % tpe) * tm + lax.broadcasted_iota(jnp.int32, (tm, 1), 0)
        spare = (lax.broadcasted_iota(jnp.int32, (1, D_QKP - D_NOPE), 1) == D_ROPE).astype(F32)
        kmask = jnp.where(row < PAD_ROWS, NEG_INF * math.sqrt(D_QK), 0.0) * spare
        for hd in range(HEADS):
            qn, _, knn, krn, _ = m["heads"][hd]
            qg = qn * gq
            q_ref[hd, :, 0:D_NOPE] = qg[:, 0:D_NOPE].astype(BF16)
            q_ref[hd, :, D_NOPE:D_QKP] = (_rope(qg[:, D_NOPE:D_QKP], c, s1, s2) + spare).astype(BF16)
            k_ref[hd, :, 0:D_NOPE] = (knn * gk[:, 0:D_NOPE]).astype(BF16)
            k_ref[hd, :, D_NOPE:D_QKP] = (_rope(krn * gk[:, D_NOPE:D_QKP], c, s1, s2) + kmask).astype(BF16)
            v_ref[hd] = m["v"][:, hd * D_V:(hd + 1) * D_V].astype(BF16)
        cqn_ref[...] = m["cqn"]
        ckvn_ref[...] = m["ckvn"]

    def const(shape):
        return pl.BlockSpec(shape, lambda i: tuple(0 for _ in shape))

    tab = pl.BlockSpec((tm, 128), lambda i: (i % tpe, 0))
    return pl.pallas_call(
        body, name="mla_prep", grid=(rows // tm,),
        in_specs=[pl.BlockSpec((tm, Z_MLA), lambda i: (i, 0)), const((1, Q_RANK)), const((1, KV_RANK)),
                  const((1, D_QKP)), const((1, D_QKP)), const((HEADS * D_QKP, Q_RANK)),
                  const((KV_RANK, HEADS * D_NOPE)), const((KV_RANK, HEADS * D_V)), tab, tab, tab],
        out_specs=[pl.BlockSpec((HEADS, tm, D_QKP), lambda i: (0, i, 0)),
                   pl.BlockSpec((HEADS, tm, D_QKP), lambda i: (0, i, 0)),
                   pl.BlockSpec((HEADS, tm, D_V), lambda i: (0, i, 0)),
                   pl.BlockSpec((tm, Q_RANK), lambda i: (i, 0)),
                   pl.BlockSpec((tm, KV_RANK), lambda i: (i, 0))],
        out_shape=[jax.ShapeDtypeStruct((HEADS, rows, D_QKP), BF16),
                   jax.ShapeDtypeStruct((HEADS, rows, D_QKP), BF16),
                   jax.ShapeDtypeStruct((HEADS, rows, D_V), BF16),
                   jax.ShapeDtypeStruct((rows, Q_RANK), BF16),
                   jax.ShapeDtypeStruct((rows, KV_RANK), BF16)],
        compiler_params=_params(("parallel",)))(z, gql, gkvl, gqh, gkh, wuq, wuk, wuv, *tabs)


Q_BLOCK_ROWS = 528


def _q_block(lp):
    return _row_tile(lp, Q_BLOCK_ROWS)


def _key_end(ext, lp):
    return min(lp, -(-ext // CHUNK) * CHUNK)


def _diag_bias(qb, j0, nk):
    shift = CHUNK.bit_length() - 1
    r = jnp.right_shift(j0 + lax.broadcasted_iota(jnp.int32, (qb, nk), 0), shift)
    c = jnp.right_shift(j0 + lax.broadcasted_iota(jnp.int32, (qb, nk), 1), shift)
    return jnp.where(c <= r, 0.0, NEG_INF)


def _attn_fwd_call(q, k, v, nb, lp, comm=None):
    rows = nb * lp
    qb = _q_block(lp)
    scale = 1.0 / math.sqrt(D_QK)

    def body(q_ref, k_ref, v_ref, o_ref, lse_ref):
        for j in range(lp // qb):
            j0, ext = j * qb, (j + 1) * qb
            kend = _key_end(ext, lp)
            qj = q_ref[0, j0:ext, :]
            sd = _dot_nt(qj, k_ref[0, j0:kend, :]) * scale + _diag_bias(qb, j0, kend - j0)
            mx = jnp.max(sd, axis=-1, keepdims=True)
            if j > 0:
                so = _dot_nt(qj, k_ref[0, 0:j0, :]) * scale
                mx = jnp.maximum(mx, jnp.max(so, axis=-1, keepdims=True))
            pd = jnp.exp(sd - mx)
            l = jnp.sum(pd, axis=-1, keepdims=True)
            o = _dot(pd.astype(BF16), v_ref[0, j0:kend, :])
            if j > 0:
                po = jnp.exp(so - mx)
                l = l + jnp.sum(po, axis=-1, keepdims=True)
                o = o + _dot(po.astype(BF16), v_ref[0, 0:j0, :])
            o_ref[j0:ext, :] = o / l
            lse_ref[0, j0:ext, :] = mx + jnp.log(l)

    return _hosted_call(
        body, name="attn_fwd", grid=(nb, HEADS),
        in_specs=[pl.BlockSpec((1, lp, D_QKP), lambda b, h: (h, b, 0)),
                  pl.BlockSpec((1, lp, D_QKP), lambda b, h: (h, b, 0)),
                  pl.BlockSpec((1, lp, D_V), lambda b, h: (h, b, 0))],
        out_specs=[pl.BlockSpec((lp, D_V), lambda b, h: (b, h)),
                   pl.BlockSpec((1, lp, 1), lambda b, h: (h, b, 0))],
        out_shape=[jax.ShapeDtypeStruct((rows, MLA_W), F32),
                   jax.ShapeDtypeStruct((HEADS, rows, 1), F32)],
        dims=("parallel", "parallel"), args=(q, k, v), comm=comm)


def _attn_bwd_call(q, k, v, o, lse, do, nb, lp):
    rows = nb * lp
    qb = _q_block(lp)
    scale = 1.0 / math.sqrt(D_QK)

    def body(q_ref, k_ref, v_ref, o_ref, lse_ref, do_ref, dq_ref, dk_ref, dv_ref, dk_acc, dv_acc):
        dk_acc[...] = jnp.zeros_like(dk_acc)
        dv_acc[...] = jnp.zeros_like(dv_acc)
        for j in range(lp // qb):
            j0, ext = j * qb, (j + 1) * qb
            kend = _key_end(ext, lp)
            dbias = _diag_bias(qb, j0, kend - j0)
            qj = q_ref[0, j0:ext, :]
            doj = do_ref[j0:ext, :]
            delta = jnp.sum(doj * o_ref[j0:ext, :], axis=-1, keepdims=True)
            dob = doj.astype(BF16)
            lse = lse_ref[0, j0:ext, :]
            dq = jnp.zeros((qb, D_QKP), F32)
            for lo, hi, bias in ((j0, kend, dbias), (0, j0, None)):
                if hi == lo:
                    continue
                kk = k_ref[0, lo:hi, :]
                s = _dot_nt(qj, kk) * scale
                p = jnp.exp((s if bias is None else s + bias) - lse)
                dv_acc[lo:hi, :] += _dot_tn(p.astype(BF16), dob)
                dp = _dot_nt(dob, v_ref[0, lo:hi, :])
                ds = (p * (dp - delta) * scale).astype(BF16)
                dq = dq + _dot(ds, kk)
                dk_acc[lo:hi, :] += _dot_tn(ds, qj)
            dq_ref[0, j0:ext, :] = dq.astype(BF16)
        dk_ref[0] = dk_acc[...].astype(BF16)
        dv_ref[0] = dv_acc[...].astype(BF16)

    qspec = pl.BlockSpec((1, lp, D_QKP), lambda b, h: (h, b, 0))
    vspec = pl.BlockSpec((1, lp, D_V), lambda b, h: (h, b, 0))
    ospec = pl.BlockSpec((lp, D_V), lambda b, h: (b, h))
    return pl.pallas_call(
        body, name="attn_bwd", grid=(nb, HEADS),
        in_specs=[qspec, qspec, vspec, ospec, pl.BlockSpec((1, lp, 1), lambda b, h: (h, b, 0)), ospec],
        out_specs=[qspec, qspec, vspec],
        out_shape=[jax.ShapeDtypeStruct((HEADS, rows, D_QKP), BF16),
                   jax.ShapeDtypeStruct((HEADS, rows, D_QKP), BF16),
                   jax.ShapeDtypeStruct((HEADS, rows, D_V), BF16)],
        scratch_shapes=[pltpu.VMEM((lp, D_QKP), F32), pltpu.VMEM((lp, D_V), F32)],
        compiler_params=_params(("parallel", "parallel")))(q, k, v, o, lse, do)


def _lru_gates(u, cw, cb, wa, ba, wx, bx, lam, lp):
    xc = (cw[3:4, :] * u + cw[2:3, :] * pltpu.roll(u, 1, 0) + cw[1:2, :] * pltpu.roll(u, 2, 0)
          + cw[0:1, :] * pltpu.roll(u, 3, 0) + cb)
    xcb = xc.astype(BF16)
    r = jax.nn.sigmoid(_dot(xcb, wa) + ba)
    i = jax.nn.sigmoid(_dot(xcb, wx) + bx)
    sp = _softplus_neg(lam)
    la = -C_RGLRU * r * sp
    a = jnp.exp(la)
    mult = jnp.sqrt(_neg_expm1(2.0 * la))
    row = lax.broadcasted_iota(jnp.int32, (lp, LRU_TILE), 0)
    first = row == PAD_ROWS
    valid = row >= PAD_ROWS
    mult_eff = jnp.where(first, 1.0, mult)
    return dict(xc=xc, xcb=xcb, r=r, i=i, sp=sp, la=la, a=a, mult=mult, mult_eff=mult_eff, first=first, valid=valid)


def _scan_rows(a, b, a_s, b_s, out_ref, lp, reverse):
    sub = lax.broadcasted_iota(jnp.int32, (lp, LRU_TILE), 0) & 7
    for dist in (1, 2, 4):
        shift = lp - dist if reverse else dist
        keep = (sub + dist <= 7) if reverse else (sub >= dist)
        a_sh = pltpu.roll(a, shift, 0)
        b_sh = pltpu.roll(b, shift, 0)
        b = jnp.where(keep, a * b_sh + b, b)
        a = jnp.where(keep, a * a_sh, a)
    a_s[...] = a
    b_s[...] = b
    n_groups = lp // 8
    edge = 0 if reverse else 7

    def group(gi, carry):
        r0 = pl.multiple_of(((n_groups - 1 - gi) if reverse else gi) * 8, 8)
        a8 = a_s[pl.ds(r0, 8), :]
        b8 = b_s[pl.ds(r0, 8), :]
        out_ref[pl.ds(r0, 8), :] = a8 * carry + b8
        return a8[edge:edge + 1, :] * carry + b8[edge:edge + 1, :]

    lax.fori_loop(0, n_groups, group, jnp.zeros((1, LRU_TILE), F32), unroll=4)


def _lru_specs(lp):
    seq = lambda col0: pl.BlockSpec((lp, LRU_TILE), lambda t, b: (b, col0 + t))
    cw = pl.BlockSpec((1, CONV_K, LRU_TILE), lambda t, b: (t, 0, 0))
    vec = pl.BlockSpec((1, LRU_TILE), lambda t, b: (0, t))
    mat = pl.BlockSpec((1, LRU_TILE, LRU_TILE), lambda t, b: (t, 0, 0))
    return seq, cw, vec, mat


def _lru_fwd_call(z, cw, cb, wa, ba, wx, bx, lam, nb, lp, comm=None):
    rows = nb * lp
    seq, cwspec, vec, mat = _lru_specs(lp)

    def body(u_ref, g_ref, cw_ref, cb_ref, wa_ref, ba_ref, wx_ref, bx_ref, lam_ref, y_ref, hs_ref, a_s, b_s):
        m = _lru_gates(u_ref[...], cw_ref[0], cb_ref[...], wa_ref[0], ba_ref[...], wx_ref[0], bx_ref[...],
                       lam_ref[...], lp)
        a = jnp.where(m["valid"], m["a"], 0.0)
        b = jnp.where(m["valid"], m["mult_eff"] * (m["i"] * m["xc"]), 0.0)
        _scan_rows(a, b, a_s, b_s, hs_ref, lp, reverse=False)
        gl, _ = _gelu(g_ref[...])
        y_ref[...] = hs_ref[...] * gl

    oshape = jax.ShapeDtypeStruct((rows, LRU_W), F32)
    return _hosted_call(
        body, name="lru_fwd", grid=(N_LRU_TILES, nb),
        in_specs=[seq(Z_U // LRU_TILE), seq(Z_G // LRU_TILE), cwspec, vec, mat, vec, mat, vec, vec],
        out_specs=[seq(0), seq(0)], out_shape=[oshape, oshape],
        scratch_shapes=[pltpu.VMEM((lp, LRU_TILE), F32), pltpu.VMEM((lp, LRU_TILE), F32)],
        dims=("parallel", "parallel"), args=(z, z, cw, cb, wa, ba, wx, bx, lam), comm=comm)


def _lru_bwd_call(z, hs, dy, cw, cb, wa, ba, wx, bx, lam, nb, lp, comm=None):
    rows = nb * lp
    seq, cwspec, vec, mat = _lru_specs(lp)

    def body(u_ref, g_ref, hs_ref, dy_ref, cw_ref, cb_ref, wa_ref, ba_ref, wx_ref, bx_ref, lam_ref,
             du_ref, dg_ref, dcw_ref, dcb_ref, dwa_ref, dba_ref, dwx_ref, dbx_ref, dlam_ref, a_s, b_s, d_s):
        b_idx = pl.program_id(1)
        u = u_ref[...]
        cw = cw_ref[0]
        wa, wx = wa_ref[0], wx_ref[0]
        lam = lam_ref[...]
        m = _lru_gates(u, cw, cb_ref[...], wa, ba_ref[...], wx, bx_ref[...], lam, lp)
        gate = g_ref[...]
        gl, th = _gelu(gate)
        dy = dy_ref[...]
        hs = hs_ref[...]
        dg_ref[...] = (dy * hs * _gelu_grad(gate, th)).astype(BF16)
        a_eff = jnp.where(m["valid"], m["a"], 0.0)
        _scan_rows(pltpu.roll(a_eff, lp - 1, 0), dy * gl, a_s, b_s, d_s, lp, reverse=True)
        ds = d_s[...]
        xc, r, i = m["xc"], m["r"], m["i"]
        row = lax.broadcasted_iota(jnp.int32, (lp, LRU_TILE), 0)
        da = ds * jnp.where(row >= 1, pltpu.roll(hs, 1, 0), 0.0)
        db = jnp.where(m["valid"], ds, 0.0)
        di = db * m["mult_eff"] * xc
        dxc = db * m["mult_eff"] * i
        live = m["valid"] & jnp.logical_not(m["first"])
        e2 = jnp.exp(2.0 * m["la"])
        dm = jnp.where(live, db * i * xc, 0.0)
        dla = da * m["a"] + jnp.where(live, dm * (-e2 / m["mult"]), 0.0)
        dr = dla * (-C_RGLRU * m["sp"])
        dsp = jnp.sum(dla * (-C_RGLRU * r), axis=0, keepdims=True)
        dpr = (dr * r * (1.0 - r))
        dpi = (di * i * (1.0 - i))
        dprb, dpib = dpr.astype(BF16), dpi.astype(BF16)
        dxc = dxc + _dot_nt(dprb, wa) + _dot_nt(dpib, wx)
        du = (cw[3:4, :] * dxc + cw[2:3, :] * pltpu.roll(dxc, lp - 1, 0) + cw[1:2, :] * pltpu.roll(dxc, lp - 2, 0)
              + cw[0:1, :] * pltpu.roll(dxc, lp - 3, 0))
        du_ref[...] = jnp.where(m["valid"], du, 0.0).astype(BF16)
        tap = lax.broadcasted_iota(jnp.int32, (CONV_K, LRU_TILE), 0)
        dcw = jnp.zeros((CONV_K, LRU_TILE), F32)
        for kk in range(CONV_K):
            shifted = u if kk == CONV_K - 1 else pltpu.roll(u, CONV_K - 1 - kk, 0)
            dcw = jnp.where(tap == kk, jnp.sum(dxc * shifted, axis=0, keepdims=True), dcw)
        parts = [(dcw_ref, dcw[None]), (dcb_ref, jnp.sum(dxc, axis=0, keepdims=True)[None]),
                 (dwa_ref, _dot_tn(m["xcb"], dprb)[None]), (dba_ref, jnp.sum(dpr, axis=0, keepdims=True)[None]),
                 (dwx_ref, _dot_tn(m["xcb"], dpib)[None]), (dbx_ref, jnp.sum(dpi, axis=0, keepdims=True)[None]),
                 (dlam_ref, (dsp * (-jax.nn.sigmoid(-lam)))[None])]

        @pl.when(b_idx == 0)
        def _():
            for ref, val in parts:
                ref[...] = val

        @pl.when(b_idx != 0)
        def _():
            for ref, val in parts:
                ref[...] += val

    bshape = jax.ShapeDtypeStruct((rows, LRU_W), BF16)
    vec3 = pl.BlockSpec((1, 1, LRU_TILE), lambda t, b: (t, 0, 0))
    vshape = jax.ShapeDtypeStruct((N_LRU_TILES, 1, LRU_TILE), F32)
    mshape = jax.ShapeDtypeStruct((N_LRU_TILES, LRU_TILE, LRU_TILE), F32)
    return _hosted_call(
        body, name="lru_bwd", grid=(N_LRU_TILES, nb),
        in_specs=[seq(Z_U // LRU_TILE), seq(Z_G // LRU_TILE), seq(0), seq(0), cwspec, vec, mat, vec, mat, vec, vec],
        out_specs=[seq(0), seq(0), cwspec, vec3, mat, vec3, mat, vec3, vec3],
        out_shape=[bshape, bshape, jax.ShapeDtypeStruct((N_LRU_TILES, CONV_K, LRU_TILE), F32), vshape, mshape,
                   vshape, mshape, vshape, vshape],
        scratch_shapes=[pltpu.VMEM((lp, LRU_TILE), F32)] * 3,
        dims=("parallel", "arbitrary"), args=(z, z, hs, dy, cw, cb, wa, ba, wx, bx, lam), comm=comm)


def _mix_out_call(ya, yl, ga, gl, wout, h, tm):
    rows, d = h.shape

    def body(ya_ref, yl_ref, ga_ref, gl_ref, w_ref, h_ref, y_ref, o_ref):
        a = ya_ref[...]
        l = yl_ref[...]
        an = (a * _rms(a, MLA_W) * ga_ref[...]).astype(BF16)
        ln = (l * _rms(l, LRU_W) * gl_ref[...]).astype(BF16)
        y_ref[:, 0:MLA_W] = an
        y_ref[:, MLA_W:MLA_W + LRU_W] = ln
        o_ref[...] = h_ref[...] + _dot(an, w_ref[0:MLA_W, :]) + _dot(ln, w_ref[MLA_W:MLA_W + LRU_W, :])

    half = pl.BlockSpec((tm, MLA_W), lambda i: (i, 0))
    g = pl.BlockSpec((1, MLA_W), lambda i: (0, 0))
    full = pl.BlockSpec((tm, d), lambda i: (i, 0))
    return pl.pallas_call(
        body, name="mix_out", grid=(rows // tm,),
        in_specs=[half, half, g, g, pl.BlockSpec((MLA_W + LRU_W, d), lambda i: (0, 0)), full],
        out_specs=[full, full],
        out_shape=[jax.ShapeDtypeStruct((rows, MLA_W + LRU_W), BF16), jax.ShapeDtypeStruct((rows, d), F32)],
        compiler_params=_params(("parallel",)))(ya, yl, ga, gl, wout, h)


def _mix_out_bwd_call(dhb, wout, ya, yl, ga, gl, tm):
    rows = ya.shape[0]
    d = dhb.shape[1]

    def body(dh_ref, w_ref, ya_ref, yl_ref, ga_ref, gl_ref, dya_ref, dyl_ref, dga_ref, dgl_ref):
        dy = _dot_nt(dh_ref[...], w_ref[...])
        outs = []
        for val, g_ref, lo, out_ref in ((ya_ref[...], ga_ref, 0, dya_ref), (yl_ref[...], gl_ref, MLA_W, dyl_ref)):
            r = _rms(val, MLA_W)
            n = val * r
            dyn = dy[:, lo:lo + MLA_W]
            out_ref[...] = _rms_bwd(dyn * g_ref[...], n, r, MLA_W)
            outs.append(jnp.sum(dyn * n, axis=0, keepdims=True))

        @pl.when(pl.program_id(0) == 0)
        def _():
            dga_ref[...] = outs[0]
            dgl_ref[...] = outs[1]

        @pl.when(pl.program_id(0) != 0)
        def _():
            dga_ref[...] += outs[0]
            dgl_ref[...] += outs[1]

    half = pl.BlockSpec((tm, MLA_W), lambda i: (i, 0))
    g = pl.BlockSpec((1, MLA_W), lambda i: (0, 0))
    return pl.pallas_call(
        body, name="mix_out_bwd", grid=(rows // tm,),
        in_specs=[pl.BlockSpec((tm, d), lambda i: (i, 0)), pl.BlockSpec((MLA_W + LRU_W, d), lambda i: (0, 0)),
                  half, half, g, g],
        out_specs=[half, half, g, g],
        out_shape=[jax.ShapeDtypeStruct((rows, MLA_W), F32), jax.ShapeDtypeStruct((rows, LRU_W), F32),
                   jax.ShapeDtypeStruct((1, MLA_W), F32), jax.ShapeDtypeStruct((1, LRU_W), F32)],
        compiler_params=_params(("arbitrary",)))(dhb, wout, ya, yl, ga, gl)


def _final_call(h, g, target, lp, tm):
    rows, d = h.shape
    tpe = lp // tm

    def body(h_ref, g_ref, t_ref, dh_ref, dhb_ref, dg_ref, loss_ref):
        i = pl.program_id(0)
        x = h_ref[...]
        g = g_ref[...]
        r = _rms(x, d)
        n = x * r
        row = (i % tpe) * tm + lax.broadcasted_iota(jnp.int32, (tm, 1), 0)
        err = jnp.where(row >= CHUNK, n * g - t_ref[...], 0.0)
        dout = err * (1.0 / d)
        dh = _rms_bwd(dout * g, n, r, d)
        dh_ref[...] = dh
        dhb_ref[...] = dh.astype(BF16)
        dg = jnp.sum(dout * n, axis=0, keepdims=True)
        part = jnp.sum(jnp.sum(err * err, axis=1, keepdims=True), axis=0, keepdims=True) * (0.5 / d)
        loss = jnp.broadcast_to(part, (1, 128))

        @pl.when(i == 0)
        def _():
            dg_ref[...] = dg
            loss_ref[...] = loss

        @pl.when(i != 0)
        def _():
            dg_ref[...] += dg
            loss_ref[...] += loss

    full = pl.BlockSpec((tm, d), lambda i: (i, 0))
    return pl.pallas_call(
        body, name="final_loss", grid=(rows // tm,),
        in_specs=[full, pl.BlockSpec((1, d), lambda i: (0, 0)), full],
        out_specs=[full, full, pl.BlockSpec((1, d), lambda i: (0, 0)), pl.BlockSpec((1, 128), lambda i: (0, 0))],
        out_shape=[jax.ShapeDtypeStruct((rows, d), F32), jax.ShapeDtypeStruct((rows, d), BF16),
                   jax.ShapeDtypeStruct((1, d), F32), jax.ShapeDtypeStruct((1, 128), F32)],
        compiler_params=_params(("arbitrary",)))(h, g, target)


def _ffn_dact_call(name, dhb, wd, gate, up, tm, comm=None):
    rows, d = dhb.shape
    ns, fs, _ = wd.shape

    nsub = 2 if tm % 32 == 0 else 1
    sub = tm // nsub

    def body(dh_ref, wd_ref, g_ref, p_ref, dg_ref, dp_ref):
        wd = wd_ref[0]
        for r in range(nsub):
            rs = slice(r * sub, (r + 1) * sub)
            da = 0.5 * _dot_nt(dh_ref[rs, :], wd)
            g = g_ref[0, rs, :].astype(F32)
            p = p_ref[0, rs, :].astype(F32)
            sg = jax.nn.sigmoid(g)
            dg_ref[0, rs, :] = (da * p * sg * (1.0 + g * (1.0 - sg))).astype(BF16)
            dp_ref[0, rs, :] = (da * g * sg).astype(BF16)

    aspec = pl.BlockSpec((1, tm, fs), lambda s, i: (s, i, 0))
    oshape = jax.ShapeDtypeStruct((ns, rows, fs), BF16)
    return _hosted_call(
        body, name=name, grid=(ns, rows // tm),
        in_specs=[pl.BlockSpec((tm, d), lambda s, i: (i, 0)), pl.BlockSpec((1, fs, d), lambda s, i: (s, 0, 0)),
                  aspec, aspec],
        out_specs=[aspec, aspec], out_shape=[oshape, oshape],
        dims=("parallel", "parallel"), args=(dhb, wd, gate, up), comm=comm)


def _norm_in_bwd_call(name, pieces, h, g, dres, tm, comm=None):
    rows, d = h.shape
    npc = len(pieces)

    def body(*refs):
        d_refs = refs[0:2 * npc:2]
        w_refs = refs[1:2 * npc:2]
        h_ref, g_ref, dres_ref, dh_ref, dhb_ref, dg_ref = refs[2 * npc:]
        du = jnp.zeros((tm, d), F32)
        for d_ref, w_ref in zip(d_refs, w_refs):
            if len(d_ref.shape) == 3:
                for s in range(d_ref.shape[0]):
                    du = du + _dot(d_ref[s], w_ref[s])
            else:
                du = du + _dot(d_ref[...], w_ref[...])
        x = h_ref[...]
        r = _rms(x, d)
        n = x * r
        dh = dres_ref[...] + _rms_bwd(du * g_ref[...], n, r, d)
        dh_ref[...] = dh
        dhb_ref[...] = dh.astype(BF16)
        dg = jnp.sum(du * n, axis=0, keepdims=True)

        @pl.when(pl.program_id(0) == 0)
        def _():
            dg_ref[...] = dg

        @pl.when(pl.program_id(0) != 0)
        def _():
            dg_ref[...] += dg

    in_specs, args = [], []
    for dd, w in pieces:
        if dd.ndim == 3:
            in_specs.append(pl.BlockSpec((dd.shape[0], tm, dd.shape[2]), lambda i: (0, i, 0)))
            in_specs.append(pl.BlockSpec(w.shape, lambda i: (0, 0, 0)))
        else:
            in_specs.append(pl.BlockSpec((tm, dd.shape[1]), lambda i: (i, 0)))
            in_specs.append(pl.BlockSpec(w.shape, lambda i: (0, 0)))
        args += [dd, w]
    full = pl.BlockSpec((tm, d), lambda i: (i, 0))
    gspec = pl.BlockSpec((1, d), lambda i: (0, 0))
    return _hosted_call(
        body, name=name, grid=(rows // tm,),
        in_specs=in_specs + [full, gspec, full],
        out_specs=[full, full, gspec],
        out_shape=[jax.ShapeDtypeStruct((rows, d), F32), jax.ShapeDtypeStruct((rows, d), BF16),
                   jax.ShapeDtypeStruct((1, d), F32)],
        args=(*args, h, g, dres), comm=comm)


def _wgrad_call(name, a, b, scale=1.0):
    a3, b3 = a.ndim == 3, b.ndim == 3
    ns = a.shape[0] if a3 else (b.shape[0] if b3 else 1)
    rows, m = a.shape[-2:]
    n = b.shape[-1]
    tmm = m if a3 else _col_tile(m, 256)

    def body(a_ref, b_ref, o_ref):
        av = a_ref[0] if a3 else a_ref[...]
        bv = b_ref[0] if b3 else b_ref[...]
        res = _dot_tn(av, bv)
        if scale != 1.0:
            res = res * scale
        if a3 or b3:
            o_ref[0] = res
        else:
            o_ref[...] = res

    aspec = (pl.BlockSpec((1, rows, tmm), lambda s, j: (s, 0, j)) if a3
             else pl.BlockSpec((rows, tmm), lambda s, j: (0, j)))
    bspec = (pl.BlockSpec((1, rows, n), lambda s, j: (s, 0, 0)) if b3
             else pl.BlockSpec((rows, n), lambda s, j: (0, 0)))
    if a3 or b3:
        ospec = pl.BlockSpec((1, tmm, n), lambda s, j: (s, j, 0))
        oshape = jax.ShapeDtypeStruct((ns, m, n), F32)
    else:
        ospec = pl.BlockSpec((tmm, n), lambda s, j: (j, 0))
        oshape = jax.ShapeDtypeStruct((m, n), F32)
    return pl.pallas_call(
        body, name=name, grid=(ns, m // tmm), in_specs=[aspec, bspec], out_specs=ospec, out_shape=oshape,
        compiler_params=_params(("parallel", "parallel")))(a, b)


def _mla_prep_bwd_call(z, dq, dk, dv, gql, gkvl, gqh, gkh, wuq, wuk, wuv, tabs, lp, tm):
    rows = z.shape[0]
    tpe = lp // tm

    def body(z_ref, dq_ref, dk_ref, dv_ref, gql_ref, gkvl_ref, gqh_ref, gkh_ref, wuq_ref, wuk_ref, wuv_ref,
             c_ref, s1_ref, s2_ref, dz_ref, dqp_ref, dkn_ref, dvv_ref, dgql_ref, dgkvl_ref, dgqh_ref, dgkh_ref):
        gql, gkvl = gql_ref[...], gkvl_ref[...]
        gq, gk = gqh_ref[...], gkh_ref[...]
        wuq, wuk, wuv = wuq_ref[...], wuk_ref[...], wuv_ref[...]
        m = _mla_heads(z_ref[...], gql, gkvl, wuq, wuk, wuv)
        c, s1, s2 = c_ref[...], s1_ref[...], s2_ref[...]
        dgq = jnp.zeros((1, D_QKP), F32)
        dgk = jnp.zeros((1, D_QKP), F32)
        dkr = jnp.zeros((tm, D_QKP - D_NOPE), F32)
        for hd in range(HEADS):
            qn, rqh, knn, krn, rkh = m["heads"][hd]
            dqg = jnp.concatenate([dq_ref[hd, :, 0:D_NOPE].astype(F32),
                                   _rope_t(dq_ref[hd, :, D_NOPE:D_QKP].astype(F32), c, s1, s2)], axis=1)
            dgq = dgq + jnp.sum(dqg * qn, axis=0, keepdims=True)
            dqn = dqg * gq
            dqr = rqh * (dqn - qn * (jnp.sum(dqn * qn, axis=-1, keepdims=True) * (1.0 / D_QK)))
            dqp_ref[:, hd * D_QKP:(hd + 1) * D_QKP] = dqr.astype(BF16)
            kn_full = jnp.concatenate([knn, krn], axis=1)
            dkg = jnp.concatenate([dk_ref[hd, :, 0:D_NOPE].astype(F32),
                                   _rope_t(dk_ref[hd, :, D_NOPE:D_QKP].astype(F32), c, s1, s2)], axis=1)
            dgk = dgk + jnp.sum(dkg * kn_full, axis=0, keepdims=True)
            dkn = dkg * gk
            dkraw = rkh * (dkn - kn_full * (jnp.sum(dkn * kn_full, axis=-1, keepdims=True) * (1.0 / D_QK)))
            dkn_ref[:, hd * D_NOPE:(hd + 1) * D_NOPE] = dkraw[:, 0:D_NOPE].astype(BF16)
            dkr = dkr + dkraw[:, D_NOPE:D_QKP]
            dvv_ref[:, hd * D_V:(hd + 1) * D_V] = dv_ref[hd]
        dcqn = _dot(dqp_ref[...], wuq)
        dckvn = _dot_nt(dkn_ref[...], wuk) + _dot_nt(dvv_ref[...], wuv)
        dz_ref[:, 0:Q_RANK] = _rms_bwd(dcqn * gql, m["nq"], m["rq"], Q_RANK).astype(BF16)
        dz_ref[:, Q_RANK:Z_KR] = _rms_bwd(dckvn * gkvl, m["nkv"], m["rkv"], KV_RANK).astype(BF16)
        dz_ref[:, Z_KR:Z_MLA] = dkr.astype(BF16)
        parts = [(dgql_ref, jnp.sum(dcqn * m["nq"], axis=0, keepdims=True)),
                 (dgkvl_ref, jnp.sum(dckvn * m["nkv"], axis=0, keepdims=True)), (dgqh_ref, dgq), (dgkh_ref, dgk)]

        @pl.when(pl.program_id(0) == 0)
        def _():
            for ref, val in parts:
                ref[...] = val

        @pl.when(pl.program_id(0) != 0)
        def _():
            for ref, val in parts:
                ref[...] += val

    def const(shape):
        return pl.BlockSpec(shape, lambda i: tuple(0 for _ in shape))

    tab = pl.BlockSpec((tm, 128), lambda i: (i % tpe, 0))
    hq = pl.BlockSpec((HEADS, tm, D_QKP), lambda i: (0, i, 0))
    hv = pl.BlockSpec((HEADS, tm, D_V), lambda i: (0, i, 0))

    def rowspec(n):
        return pl.BlockSpec((tm, n), lambda i: (i, 0))

    return pl.pallas_call(
        body, name="mla_prep_bwd", grid=(rows // tm,),
        in_specs=[rowspec(Z_MLA), hq, hq, hv, const((1, Q_RANK)), const((1, KV_RANK)), const((1, D_QKP)),
                  const((1, D_QKP)), const((HEADS * D_QKP, Q_RANK)), const((KV_RANK, HEADS * D_NOPE)),
                  const((KV_RANK, HEADS * D_V)), tab, tab, tab],
        out_specs=[rowspec(Z_MLA), rowspec(HEADS * D_QKP), rowspec(HEADS * D_NOPE), rowspec(HEADS * D_V),
                   const((1, Q_RANK)), const((1, KV_RANK)), const((1, D_QKP)), const((1, D_QKP))],
        out_shape=[jax.ShapeDtypeStruct((rows, Z_MLA), BF16), jax.ShapeDtypeStruct((rows, HEADS * D_QKP), BF16),
                   jax.ShapeDtypeStruct((rows, HEADS * D_NOPE), BF16), jax.ShapeDtypeStruct((rows, HEADS * D_V), BF16),
                   jax.ShapeDtypeStruct((1, Q_RANK), F32), jax.ShapeDtypeStruct((1, KV_RANK), F32),
                   jax.ShapeDtypeStruct((1, D_QKP), F32), jax.ShapeDtypeStruct((1, D_QKP), F32)],
        compiler_params=_params(("arbitrary",)))(z, dq, dk, dv, gql, gkvl, gqh, gkh, wuq, wuk, wuv, *tabs)


def _local_step(h0, target, w, nb, lp, sched=None):
    tm = _row_tile(lp, 1024)
    te = _row_tile(lp, 512)
    tabs = _rope_tables(lp)
    g = {}
    if sched is None:
        host = lambda stage: None
    else:
        sched.g = g
        host = sched.host

    def ffn_fwd(tag, h):
        u = _rmsnorm_call(tag + "_norm", h, w[tag + "_norm"], te)
        gate, up, act = _ffn_up_call(tag + "_up", u, w[tag + "_w_gate"], w[tag + "_w_up"], tm, host(tag + "_up"))
        return _ffn_down_call(tag + "_down", act, w[tag + "_w_down"], h, te), (u, gate, up, act)

    def ffn_bwd(tag, h, saved, dh, dhb):
        u, gate, up, act = saved
        dgate, dup = _ffn_dact_call(tag + "_dact", dhb, w[tag + "_w_down"], gate, up, tm, host(tag + "_dact"))
        g[tag + "_w_down"] = _wgrad_call(tag + "_dwd", act, dhb, 0.5)
        g[tag + "_w_gate"] = _wgrad_call(tag + "_dwg", dgate, u)
        g[tag + "_w_up"] = _wgrad_call(tag + "_dwu", dup, u)
        dh_in, dhb_in, g[tag + "_norm"] = _norm_in_bwd_call(
            tag + "_din", [(dgate, w[tag + "_w_gate"]), (dup, w[tag + "_w_up"])], h, w[tag + "_norm"], dh, te,
            host(tag + "_din"))
        return dh_in, dhb_in

    h1, s1 = ffn_fwd("ffn1", h0)
    un = _rmsnorm_call("mix_norm", h1, w["mix_norm"], te)
    z = _mm_call("mix_in", un, w["w_in"], tm, F32)
    mla_w = (w["q_latent_norm"], w["kv_latent_norm"], w["q_head_norm"], w["k_head_norm"], w["w_uq"], w["w_uk"],
             w["w_uv"])
    q, k, v, cqn, ckvn = _mla_prep_call(z, *mla_w, tabs, lp, te)
    o, lse = _attn_fwd_call(q, k, v, nb, lp, host("attn_fwd"))
    lru_w = (w["conv_w"], w["conv_b"], w["gate_a_w"], w["gate_a_b"], w["gate_x_w"], w["gate_x_b"], w["lru_lambda"])
    yl, hs = _lru_fwd_call(z, *lru_w, nb, lp, host("lru_fwd"))
    y, h2 = _mix_out_call(o, yl, w["attn_out_norm"], w["lru_out_norm"], w["w_out"], h1, te)
    h3, s2 = ffn_fwd("ffn2", h2)
    dh3, dh3b, g["final_norm"], loss = _final_call(h3, w["final_norm"], target, lp, te)

    dh2, dh2b = ffn_bwd("ffn2", h2, s2, dh3, dh3b)
    g["w_out"] = _wgrad_call("dw_out", y, dh2b)
    dya, dyl, g["attn_out_norm"], g["lru_out_norm"] = _mix_out_bwd_call(
        dh2b, w["w_out"], o, yl, w["attn_out_norm"], w["lru_out_norm"], te)
    dq, dk, dv = _attn_bwd_call(q, k, v, o, lse, dya, nb, lp)
    (dz_mla, dqp, dkn, dvv, g["q_latent_norm"], g["kv_latent_norm"], g["q_head_norm"],
     g["k_head_norm"]) = _mla_prep_bwd_call(z, dq, dk, dv, *mla_w, tabs, lp, te)
    g["w_uq"] = _wgrad_call("dw_uq", dqp, cqn)
    g["w_uk"] = _wgrad_call("dw_uk", ckvn, dkn)
    g["w_uv"] = _wgrad_call("dw_uv", ckvn, dvv)
    (du, dgt, g["conv_w"], g["conv_b"], g["gate_a_w"], g["gate_a_b"], g["gate_x_w"], g["gate_x_b"],
     g["lru_lambda"]) = _lru_bwd_call(z, hs, dyl, *lru_w, nb, lp, host("lru_bwd"))
    win = w["w_in"]
    g["w_in"] = jnp.concatenate(
        [_wgrad_call("dw_in_mla", dz_mla, un), _wgrad_call("dw_in_u", du, un), _wgrad_call("dw_in_g", dgt, un)],
        axis=0)
    dh1, dh1b, g["mix_norm"] = _norm_in_bwd_call(
        "mix_din", [(dz_mla, win[0:Z_MLA]), (du, win[Z_U:Z_G]), (dgt, win[Z_G:Z_W])], h1, w["mix_norm"], dh2, te,
        host("mix_din"))
    dh0, _ = ffn_bwd("ffn1", h0, s1, dh1, dh1b)
    return loss, dh0, g


def _place():
    x, y, c = lax.axis_index("x"), lax.axis_index("y"), lax.axis_index("c")
    return x, y, c, [(1 - x, y), (x, 1 - y), (1 - x, 1 - y)]


def _any_specs(n):
    return [pl.BlockSpec(memory_space=pl.ANY)] * n


def _remote(src, dst, sems, k, dev):
    send_sems, recv_sems = sems
    return pltpu.make_async_remote_copy(src_ref=src, dst_ref=dst, send_sem=send_sems.at[k], recv_sem=recv_sems.at[k],
                                        device_id=dev, device_id_type=MESH)


EW_VMEM_BYTES = 24 * 1024 * 1024


def _fit_rows(rows, cols, blocks):
    return _row_tile(rows, max(16, int(EW_VMEM_BYTES // (8 * blocks)) // cols))


class _Geom:
    def __init__(self, n0, n1, blocks=1.0):
        self.n0, self.n1 = n0, n1
        self.axis = 0 if n0 % 32 == 0 else 1
        self.h0, self.h1 = (n0 // 2, n1) if self.axis == 0 else (n0, n1 // 2)
        self.tr = _fit_rows(self.h0, self.h1, blocks)
        self.nblk = self.h0 // self.tr

    def half_ref(self, ref, lead, idx):
        if self.axis == 0:
            return ref.at[(*lead, pl.ds(idx * self.h0, self.h0))]
        return ref.at[(*lead, slice(None), pl.ds(idx * self.h1, self.h1))]

    def half_block(self, lead, i, idx):
        return (*lead, idx * self.nblk + i, 0) if self.axis == 0 else (*lead, i, idx)


class _Comm:
    def __init__(self, ins, out_shapes, aliases, n_sems, start, finish, deliver):
        self.ins, self.out_shapes, self.aliases, self.n_sems = list(ins), list(out_shapes), dict(aliases), n_sems
        self.start, self.finish, self.deliver = start, finish, deliver

    def scratch(self):
        return [pltpu.SemaphoreType.DMA((self.n_sems,)), pltpu.SemaphoreType.DMA((self.n_sems,))]


def _comm_call(name, comm):
    n_in = len(comm.ins)

    def body(*refs):
        ins, outs, sems = refs[:n_in], refs[n_in:-2], refs[-2:]
        comm.start(ins, outs, sems)
        comm.finish(ins, outs, sems)

    res = pl.pallas_call(
        body, name=name, out_shape=comm.out_shapes, in_specs=_any_specs(n_in),
        out_specs=_any_specs(len(comm.out_shapes)), input_output_aliases=comm.aliases,
        scratch_shapes=comm.scratch())(*comm.ins)
    return comm.deliver(list(res))


def _hosted_call(body, *, name, grid, in_specs, out_specs, out_shape, args, scratch_shapes=(), dims=None, comm=None):
    in_specs, out_specs, out_shape = list(in_specs), list(out_specs), list(out_shape)
    if comm is None:
        return pl.pallas_call(
            body, name=name, grid=grid, in_specs=in_specs, out_specs=out_specs, out_shape=out_shape,
            scratch_shapes=list(scratch_shapes),
            compiler_params=_params(dims or ("arbitrary",) * len(grid)))(*args)
    n_in, n_out, n_ci, n_co = len(in_specs), len(out_specs), len(comm.ins), len(comm.out_shapes)

    def wrapped(*refs):
        ins, cins = refs[:n_in], refs[n_in:n_in + n_ci]
        outs = refs[n_in + n_ci:n_in + n_ci + n_out]
        couts = refs[n_in + n_ci + n_out:n_in + n_ci + n_out + n_co]
        scratch, sems = refs[n_in + n_ci + n_out + n_co:-2], refs[-2:]
        first = functools.reduce(jnp.logical_and, [pl.program_id(k) == 0 for k in range(len(grid))])
        last = functools.reduce(jnp.logical_and, [pl.program_id(k) == grid[k] - 1 for k in range(len(grid))])

        @pl.when(first)
        def _():
            comm.start(cins, couts, sems)

        body(*ins, *outs, *scratch)

        @pl.when(last)
        def _():
            comm.finish(cins, couts, sems)

    res = pl.pallas_call(
        wrapped, name=name, grid=grid, in_specs=in_specs + _any_specs(n_ci), out_specs=out_specs + _any_specs(n_co),
        out_shape=out_shape + comm.out_shapes,
        input_output_aliases={n_in + i: n_out + o for i, o in comm.aliases.items()},
        scratch_shapes=list(scratch_shapes) + comm.scratch(),
        compiler_params=_params(("arbitrary",) * len(grid)))(*args, *comm.ins)
    comm.deliver(list(res[n_out:]))
    return list(res[:n_out])


def _gather_comm(bufs, deliver):
    n = len(bufs)
    geoms = [_Geom(*b.shape[1:]) for b in bufs]

    def first(outs, sems):
        x, y, c, chips = _place()
        cps = []
        for a in range(n):
            mine = geoms[a].half_ref(outs[a], (2 * x + y,), c)
            cps += [_remote(mine, mine, sems, 6 * a + j, (cx, cy, c)) for j, (cx, cy) in enumerate(chips)]
        return cps

    def start(ins, outs, sems):
        for cp in first(outs, sems):
            cp.start()

    def finish(ins, outs, sems):
        x, y, c, chips = _place()
        sib = (x, y, 1 - c)
        passed = []
        for a in range(n):
            for j, (cx, cy) in enumerate(chips):
                land = geoms[a].half_ref(outs[a], (2 * cx + cy,), c)
                _remote(land, land, sems, 6 * a + j, sib).wait_recv()
                cp = _remote(land, land, sems, 6 * a + 3 + j, sib)
                cp.start()
                passed.append(cp)
        for a in range(n):
            for j, (cx, cy) in enumerate(chips):
                land = geoms[a].half_ref(outs[a], (2 * cx + cy,), 1 - c)
                _remote(land, land, sems, 6 * a + 3 + j, sib).wait_recv()
        for cp in first(outs, sems) + passed:
            cp.wait_send()

    return _Comm(bufs, [jax.ShapeDtypeStruct(b.shape, b.dtype) for b in bufs], {a: a for a in range(n)}, 6 * n,
                 start, finish, deliver)


def _reduce_pair_comm(grads, deliver):
    n = len(grads)
    geoms = [_Geom(*a.shape[1:]) for a in grads]

    def copies(ins, outs, sems):
        x, y, c, _ = _place()
        return [_remote(geoms[a].half_ref(ins[a], (slice(None),), 1 - c), outs[a], sems, a, (x, y, 1 - c))
                for a in range(n)]

    def start(ins, outs, sems):
        for cp in copies(ins, outs, sems):
            cp.start()

    def finish(ins, outs, sems):
        cps = copies(ins, outs, sems)
        for cp in cps:
            cp.wait_recv()
        for cp in cps:
            cp.wait_send()

    shapes = [jax.ShapeDtypeStruct((N_SHARD, g.h0, g.h1), a.dtype) for a, g in zip(grads, geoms)]
    return _Comm(grads, shapes, {}, n, start, finish, deliver)


def _reduce_chips_comm(parts, deliver):
    n = len(parts)

    def copies(ins, outs, sems):
        x, y, c, chips = _place()
        return [_remote(ins[a].at[2 * cx + cy], outs[a].at[j], sems, 3 * a + j, (cx, cy, c))
                for a in range(n) for j, (cx, cy) in enumerate(chips)]

    def start(ins, outs, sems):
        for cp in copies(ins, outs, sems):
            cp.start()

    def finish(ins, outs, sems):
        cps = copies(ins, outs, sems)
        for cp in cps:
            cp.wait_recv()
        for cp in cps:
            cp.wait_send()

    shapes = [jax.ShapeDtypeStruct((3,) + a.shape[1:], a.dtype) for a in parts]
    return _Comm(parts, shapes, {}, 3 * n, start, finish, deliver)


def _share_pair_comm(bufs, deliver):
    n = len(bufs)
    geoms = [_Geom(*b.shape) for b in bufs]

    def copies(outs, sems):
        x, y, c, _ = _place()
        cps = []
        for a in range(n):
            mine = geoms[a].half_ref(outs[a], (), c)
            cps.append(_remote(mine, mine, sems, a, (x, y, 1 - c)))
        return cps

    def start(ins, outs, sems):
        for cp in copies(outs, sems):
            cp.start()

    def finish(ins, outs, sems):
        x, y, c, _ = _place()
        for a in range(n):
            land = geoms[a].half_ref(outs[a], (), 1 - c)
            _remote(land, land, sems, a, (x, y, 1 - c)).wait_recv()
        for cp in copies(outs, sems):
            cp.wait_send()

    return _Comm(bufs, [jax.ShapeDtypeStruct(b.shape, b.dtype) for b in bufs], {a: a for a in range(n)}, n,
                 start, finish, deliver)


def _gather_small_call(pack):
    r, d = pack.shape

    def body(in_ref, out_ref, send_sems, recv_sems):
        x, y, c, _ = _place()
        me = 4 * x + 2 * y + c
        out_ref[me] = in_ref[...]
        cps = []
        for k in range(1, 8):
            bx, by, bc = (k >> 2) & 1, (k >> 1) & 1, k & 1
            px, py, pc = x ^ bx, y ^ by, c ^ bc
            cp = pltpu.make_async_remote_copy(
                src_ref=in_ref, dst_ref=out_ref.at[me], send_sem=send_sems.at[k - 1], recv_sem=recv_sems.at[k - 1],
                device_id=(px, py, pc), device_id_type=MESH)
            cp.start()
            cps.append((cp, 4 * px + 2 * py + pc))
        for k, (cp, peer) in enumerate(cps):
            land = out_ref.at[peer]
            pltpu.make_async_remote_copy(
                src_ref=land, dst_ref=land, send_sem=send_sems.at[k], recv_sem=recv_sems.at[k],
                device_id=(x, y, c), device_id_type=MESH).wait_recv()
        for cp, _ in cps:
            cp.wait_send()

    return pl.pallas_call(
        body, name="gather_small",
        out_shape=jax.ShapeDtypeStruct((8, r, d), pack.dtype),
        in_specs=[pl.BlockSpec(memory_space=pltpu.VMEM)], out_specs=pl.BlockSpec(memory_space=pltpu.VMEM),
        scratch_shapes=[pltpu.SemaphoreType.DMA((7,)), pltpu.SemaphoreType.DMA((7,))])(pack)


def _ew_call(name, fn, ins, out_dtypes):
    shape = ins[0].shape
    cols = shape[-1]
    rows = 1
    for s_ in shape[:-1]:
        rows *= s_
    ins2 = [a.reshape(rows, cols) for a in ins]
    tr = rows
    for t in range(16, min(rows, max(16, (1 << 19) // cols)) + 1, 16):
        if rows % t == 0:
            tr = t
    no = len(out_dtypes)

    def body(*refs):
        outs = fn(*[r[...] for r in refs[:len(ins2)]])
        for ref, val in zip(refs[len(ins2):], outs):
            ref[...] = val.astype(ref.dtype)

    spec = pl.BlockSpec((tr, cols), lambda i: (i, 0))
    res = pl.pallas_call(
        body, name=name, grid=(rows // tr,), in_specs=[spec] * len(ins2), out_specs=[spec] * no,
        out_shape=[jax.ShapeDtypeStruct((rows, cols), dt) for dt in out_dtypes],
        compiler_params=_params(("parallel",)))(*ins2)
    return [r.reshape(shape) for r in res]


def _adamw_math(w, g, m, v):
    m = ADAM_B1 * m + (1.0 - ADAM_B1) * g
    v = ADAM_B2 * v + (1.0 - ADAM_B2) * (g * g)
    m_hat = m / (1.0 - ADAM_B1 ** ADAM_STEP)
    v_hat = v / (1.0 - ADAM_B2 ** ADAM_STEP)
    delta = -ADAM_LR * (m_hat / (jnp.sqrt(v_hat) + ADAM_EPS) + ADAM_WD * w)
    return delta, m, v


def _adamw_call(name, w, g, m, v):
    return _ew_call(name, _adamw_math, [w, g, m, v], [F32, F32, F32])


def _tiled_call(name, fn, place, grid, in_items, out_items):
    ni = len(in_items)

    def body(place_ref, *refs):
        vals = fn(*[r[...] for r in refs[:ni]])
        for ref, val in zip(refs[ni:], vals):
            ref[...] = val.astype(ref.dtype)

    spec = pltpu.PrefetchScalarGridSpec(
        num_scalar_prefetch=1, grid=grid,
        in_specs=[pl.BlockSpec(blk, imap) for _, blk, imap in in_items],
        out_specs=[pl.BlockSpec(blk, imap) for _, _, blk, imap in out_items])
    return pl.pallas_call(
        body, name=name, grid_spec=spec,
        out_shape=[jax.ShapeDtypeStruct(shp, dt) for shp, dt, _, _ in out_items],
        compiler_params=_params(("arbitrary",) * len(grid)))(place, *[a for a, _, _ in in_items])


def _cast_call(name, place, shards):
    n0, n1 = shards[0].shape
    tr = _fit_rows(n0, n1, 1.5 * len(shards))
    ins = [(a, (tr, n1), lambda i, p: (i, 0)) for a in shards]
    outs = [((N_SHARD, n0, n1), BF16, (1, tr, n1), lambda i, p: (p[0], i, 0)) for _ in shards]
    return _tiled_call(name, lambda *v: [x[None] for x in v], place, (n0 // tr,), ins, outs)


def _pair_sum_call(name, place, fulls, gots):
    k = len(fulls)
    g = _Geom(*fulls[0].shape[1:], blocks=2.5 * k)
    blk = (1, g.tr, g.h1)
    ins = [(a, blk, lambda s, i, p: g.half_block((s,), i, p[1])) for a in fulls]
    ins += [(a, blk, lambda s, i, p: (s, i, 0)) for a in gots]
    outs = [((N_SHARD, g.h0, g.h1), BF16, blk, lambda s, i, p: (s, i, 0)) for _ in fulls]
    return _tiled_call(name, lambda *v: [v[j] + v[k + j] for j in range(k)], place, (N_SHARD, g.nblk), ins, outs)


def _chip_sum_call(name, place, fulls, gots, recvs):
    k = len(fulls)
    g = _Geom(*fulls[0].shape[1:], blocks=4.5 * k)
    blk = (1, g.tr, g.h1)
    ins = [(a, blk, lambda i, p: g.half_block((p[0],), i, p[1])) for a in fulls]
    ins += [(a, blk, lambda i, p: (p[0], i, 0)) for a in gots]
    ins += [(a, (3, g.tr, g.h1), lambda i, p: (0, i, 0)) for a in recvs]
    outs = [((g.n0, g.n1), F32, (g.tr, g.h1), lambda i, p: g.half_block((), i, p[1])) for _ in fulls]

    def fn(*v):
        res = []
        for j in range(k):
            r = v[2 * k + j].astype(F32)
            res.append(v[j][0] + v[k + j][0] + r[0] + r[1] + r[2])
        return res

    return _tiled_call(name, fn, place, (g.nblk,), ins, outs)


def _adamw_group_call(name, ws, gs, ms, vs, comm=None):
    k = len(ws)
    n0, n1 = ws[0].shape
    tr = _fit_rows(n0, n1, 8 * k)
    spec = pl.BlockSpec((tr, n1), lambda i: (i, 0))

    def body(*refs):
        for j in range(k):
            g = refs[k + j][...]
            delta, m, vv = _adamw_math(refs[j][...], g, refs[2 * k + j][...], refs[3 * k + j][...])
            for ref, val in zip(refs[4 * k + 4 * j:4 * k + 4 * j + 4], (g, delta, m, vv)):
                ref[...] = val

    flat = _hosted_call(
        body, name=name, grid=(n0 // tr,), in_specs=[spec] * (4 * k), out_specs=[spec] * (4 * k),
        out_shape=[jax.ShapeDtypeStruct((n0, n1), F32)] * (4 * k), dims=("parallel",),
        args=(*ws, *gs, *ms, *vs), comm=comm)
    return [flat[4 * j:4 * j + 4] for j in range(k)]


def _small_update_call(gathered, wp, mp, vp):
    nd, r, d = gathered.shape

    def body(g_ref, w_ref, m_ref, v_ref, gs_ref, d_ref, nm_ref, nv_ref):
        gs = g_ref[0]
        for k in range(1, nd):
            gs = gs + g_ref[k]
        gs_ref[...] = gs
        delta, m, v = _adamw_math(w_ref[...], gs[0:SMALL_ADAM_ROWS], m_ref[...], v_ref[...])
        d_ref[...] = delta
        nm_ref[...] = m
        nv_ref[...] = v

    vm = pl.BlockSpec(memory_space=pltpu.VMEM)
    ashape = jax.ShapeDtypeStruct((SMALL_ADAM_ROWS, d), F32)
    return pl.pallas_call(
        body, name="small_update", in_specs=[vm] * 4, out_specs=[vm] * 4,
        out_shape=[jax.ShapeDtypeStruct((r, d), F32), ashape, ashape, ashape],
        compiler_params=pltpu.CompilerParams(vmem_limit_bytes=VMEM_LIMIT_BYTES))(gathered, wp, mp, vp)


SMALL_NAMES = ["ffn1_norm", "mix_norm", "ffn2_norm", "final_norm", "q_latent_norm", "kv_latent_norm",
               "q_head_norm", "k_head_norm", "conv_b", "gate_a_b", "gate_x_b", "lru_lambda", "attn_out_norm",
               "lru_out_norm"]
ROW_CONV_W = 14
ROW_GATE_A = 16
ROW_GATE_X = 48
ROW_META = 80
ROW_LOSS = 96


def _row(a):
    flat = a.reshape(1, -1)
    return jnp.pad(flat, ((0, 0), (0, D_MODEL - flat.shape[1])))


def _pack_small(t, rows):
    parts = [_row(t[nm]) for nm in SMALL_NAMES]
    parts.append(t["conv_w"].reshape(2, D_MODEL))
    parts.append(t["gate_a_w"].reshape(32, D_MODEL))
    parts.append(t["gate_x_w"].reshape(32, D_MODEL))
    p = jnp.concatenate(parts, axis=0)
    return jnp.pad(p, ((0, rows - p.shape[0]), (0, 0)))


def _unpack_small(p, like):
    out = {}
    for k, nm in enumerate(SMALL_NAMES):
        out[nm] = p[k, 0:like[nm].size].reshape(like[nm].shape)
    out["gate_a_w"] = p[ROW_GATE_A:ROW_GATE_A + 32].reshape(like["gate_a_w"].shape)
    out["gate_x_w"] = p[ROW_GATE_X:ROW_GATE_X + 32].reshape(like["gate_x_w"].shape)
    return out


def _gate_dense(wg):
    w4 = wg[0].reshape(N_LRU_TILES, 2, 64, 64)
    zero = jnp.zeros((N_LRU_TILES, 64, 64), wg.dtype)
    top = jnp.concatenate([w4[:, 0], zero], axis=2)
    bot = jnp.concatenate([zero, w4[:, 1]], axis=2)
    return jnp.concatenate([top, bot], axis=1).astype(BF16)


def _gate_blocks(dw):
    return jnp.stack([dw[:, 0:64, 0:64], dw[:, 64:128, 64:128]], axis=1).reshape(8, 64, 64)


BIG_NAMES = ["ffn1_w_gate", "ffn1_w_up", "ffn1_w_down", "w_in", "w_uq", "w_uk", "w_uv", "w_out", "ffn2_w_gate",
             "ffn2_w_up", "ffn2_w_down"]
BIG_GROUPS = [["ffn1_w_gate", "ffn1_w_up", "ffn1_w_down", "ffn2_w_gate", "ffn2_w_up", "ffn2_w_down"], ["w_in"],
              ["w_uq"], ["w_uk", "w_uv"], ["w_out"]]
TRANSPOSED = ("ffn1_w_gate", "ffn1_w_up", "ffn2_w_gate", "ffn2_w_up", "w_in", "w_uq")


def _to2d(nm, a):
    return a[0].T if nm in TRANSPOSED else a[0]


def _from2d(nm, a):
    return (a.T if nm in TRANSPOSED else a)[None]


WEIGHT_NAMES = ["meta_tokens", "ffn1_norm", "ffn1_w_gate", "ffn1_w_up", "ffn1_w_down", "mix_norm", "w_in",
                "q_latent_norm", "w_uq", "kv_latent_norm", "w_uk", "w_uv", "q_head_norm", "k_head_norm", "conv_w",
                "conv_b", "gate_a_w", "gate_a_b", "gate_x_w", "gate_x_b", "lru_lambda", "attn_out_norm",
                "lru_out_norm", "w_out", "ffn2_norm", "ffn2_w_gate", "ffn2_w_up", "ffn2_w_down", "final_norm"]


def _weight_from(nm, slots):
    if nm == "w_in":
        win = slots.reshape(IN_WIDTH, D_MODEL)
        return jnp.concatenate([win[0:Z_KR + D_ROPE], jnp.zeros((128 - D_ROPE, D_MODEL), BF16),
                                win[Z_KR + D_ROPE:]], axis=0)
    if nm == "w_uq":
        return jnp.pad(slots, ((0, 0), (0, D_QKP - D_QK), (0, 0))).reshape(HEADS * D_QKP, Q_RANK)
    if nm in ("w_uk", "w_uv"):
        return slots.transpose(1, 0, 2).reshape(KV_RANK, HEADS * D_NOPE)
    if nm == "w_out":
        return slots.reshape(D_MODEL, D_MODEL)
    return slots


def _small_weights(p, small):
    w = {nm: p[nm] for nm in SMALL_NAMES}
    w["q_head_norm"] = jnp.pad(p["q_head_norm"], ((0, 0), (0, D_QKP - D_QK)))
    w["k_head_norm"] = jnp.pad(p["k_head_norm"], ((0, 0), (0, D_QKP - D_QK)))
    w["conv_w"] = small[:, N_META:N_META + 2, :].reshape(N_SHARD, CONV_K, LRU_TILE)
    w["gate_a_w"] = _gate_dense(p["gate_a_w"])
    w["gate_x_w"] = _gate_dense(p["gate_x_w"])
    meta = small[:, 0:N_META, :].transpose(1, 0, 2).reshape(N_META, D_MODEL)
    return w, meta


def _full_weights(p, gathered, small):
    w, meta = _small_weights(p, small)
    w.update({nm: _weight_from(nm, gathered[nm]) for nm in BIG_NAMES})
    return w, meta


def _shard_grad(nm, g):
    if nm == "w_in":
        return jnp.concatenate([g[0:Z_KR + D_ROPE], g[Z_MLA:]], axis=0).reshape(N_SHARD, IN_WIDTH // N_SHARD, D_MODEL)
    if nm == "w_uq":
        return g.reshape(HEADS, D_QKP, Q_RANK)[:, 0:D_QK, :]
    if nm in ("w_uk", "w_uv"):
        return g.reshape(KV_RANK, HEADS, D_NOPE).transpose(1, 0, 2)
    if nm == "w_out":
        return g.reshape(N_SHARD, D_MODEL // N_SHARD, D_MODEL)
    return g


def _shard_grads(g):
    return {nm: _shard_grad(nm, g[nm]) for nm in BIG_NAMES}


GATHER_FIRST = ["ffn1_w_gate", "ffn1_w_up", "ffn1_w_down"]
GATHER_AT = {"ffn1_up": ["w_in", "w_uq", "w_uk", "w_uv", "w_out"], "attn_fwd": ["ffn2_w_down"],
             "lru_fwd": ["ffn2_w_gate", "ffn2_w_up"]}
REDUCE_PLAN = [(["ffn2_w_gate", "ffn2_w_up", "ffn2_w_down"], "ffn2_din", "lru_bwd"),
               (["w_out", "w_uq", "w_uk", "w_uv", "w_in"], "mix_din", "ffn1_dact"),
               (["ffn1_w_gate", "ffn1_w_up", "ffn1_w_down"], "ffn1_din", None)]


def _same_shape_groups(names):
    return [[nm for nm in grp if nm in names] for grp in BIG_GROUPS if any(nm in names for nm in grp)]


class _Sched:
    def __init__(self, place, w, slots):
        self.place, self.w, self.slots = place, w, slots
        self.g = None
        self.sharded, self.from_pair, self.chip_bf16, self.from_chips = {}, {}, {}, {}

    def host(self, stage):
        if stage in GATHER_AT:
            return self.gather(GATHER_AT[stage])
        for names, pair_at, chips_at in REDUCE_PLAN:
            if stage == pair_at:
                return self.pair(names)
            if stage == chips_at:
                return self.chips(names)
        return None

    def gather(self, names):
        def deliver(outs):
            self.w.update({nm: _weight_from(nm, o) for nm, o in zip(names, outs)})
            return outs

        return _gather_comm([self.slots[nm] for nm in names], deliver)

    def pair(self, names):
        self.sharded.update({nm: _shard_grad(nm, self.g[nm]) for nm in names})

        def deliver(outs):
            self.from_pair.update(zip(names, outs))
            for grp in _same_shape_groups(names):
                sums = _pair_sum_call("pair_sum_" + grp[0], self.place, [self.sharded[nm] for nm in grp],
                                      [self.from_pair[nm] for nm in grp])
                self.chip_bf16.update(zip(grp, sums))
            return outs

        return _reduce_pair_comm([self.sharded[nm] for nm in names], deliver)

    def chips(self, names):
        def deliver(outs):
            self.from_chips.update(zip(names, outs))
            return outs

        return _reduce_chips_comm([self.chip_bf16[nm] for nm in names], deliver)

    def chip_sums(self, names):
        out = {}
        for grp in _same_shape_groups(names):
            sums = _chip_sum_call("chip_sum_" + grp[0], self.place, [self.sharded[nm] for nm in grp],
                                  [self.from_pair[nm] for nm in grp], [self.from_chips[nm] for nm in grp])
            out.update(zip(grp, sums))
        return out


def kernel(x, meta_tokens, ffn1_norm, ffn1_w_gate, ffn1_w_up, ffn1_w_down, mix_norm, w_in, q_latent_norm, w_uq, kv_latent_norm, w_uk, w_uv, q_head_norm, k_head_norm, conv_w, conv_b, gate_a_w, gate_a_b, gate_x_w, gate_x_b, lru_lambda, attn_out_norm, lru_out_norm, w_out, ffn2_norm, ffn2_w_gate, ffn2_w_up, ffn2_w_down, final_norm, loss_target, m_meta_tokens, m_ffn1_norm, m_ffn1_w_gate, m_ffn1_w_up, m_ffn1_w_down, m_mix_norm, m_w_in, m_q_latent_norm, m_w_uq, m_kv_latent_norm, m_w_uk, m_w_uv, m_q_head_norm, m_k_head_norm, m_conv_w, m_conv_b, m_gate_a_w, m_gate_a_b, m_gate_x_w, m_gate_x_b, m_lru_lambda, m_attn_out_norm, m_lru_out_norm, m_w_out, m_ffn2_norm, m_ffn2_w_gate, m_ffn2_w_up, m_ffn2_w_down, m_final_norm, v_meta_tokens, v_ffn1_norm, v_ffn1_w_gate, v_ffn1_w_up, v_ffn1_w_down, v_mix_norm, v_w_in, v_q_latent_norm, v_w_uq, v_kv_latent_norm, v_w_uk, v_w_uv, v_q_head_norm, v_k_head_norm, v_conv_w, v_conv_b, v_gate_a_w, v_gate_a_b, v_gate_x_w, v_gate_x_b, v_lru_lambda, v_attn_out_norm, v_lru_out_norm, v_w_out, v_ffn2_norm, v_ffn2_w_gate, v_ffn2_w_up, v_ffn2_w_down, v_final_norm):
    args = locals()
    p = {nm: args[nm] for nm in WEIGHT_NAMES}
    mom = {nm: args["m_" + nm] for nm in WEIGHT_NAMES}
    var = {nm: args["v_" + nm] for nm in WEIGHT_NAMES}
    nb, seq, d = x.shape
    lp = CHUNK + seq
    xi, yi, ci = lax.axis_index("x"), lax.axis_index("y"), lax.axis_index("c")
    chip = 2 * xi + yi

    place = jnp.stack([chip, ci]).astype(jnp.int32)
    p2 = {nm: _to2d(nm, p[nm]) for nm in BIG_NAMES}
    m2 = {nm: _to2d(nm, mom[nm]) for nm in BIG_NAMES}
    v2 = {nm: _to2d(nm, var[nm]) for nm in BIG_NAMES}

    slots = {}
    for grp in BIG_GROUPS:
        for nm, buf in zip(grp, _cast_call("cast_" + grp[0], place, [p2[nm] for nm in grp])):
            slots[nm] = buf
    small_shard = jnp.concatenate(
        [meta_tokens, conv_w[0].reshape(2, 2 * LRU_TILE), jnp.zeros((14, 2 * LRU_TILE), F32)], axis=0)
    small_slots = lax.dynamic_update_slice(jnp.zeros((N_SHARD,) + small_shard.shape, F32), small_shard[None],
                                           (chip, 0, 0))
    first = _comm_call("gather_first", _gather_comm([slots[nm] for nm in GATHER_FIRST] + [small_slots], lambda o: o))
    w, meta = _small_weights(p, first[-1])
    w.update({nm: _weight_from(nm, o) for nm, o in zip(GATHER_FIRST, first[:-1])})
    sched = _Sched(place, w, slots)

    h0 = jnp.concatenate(
        [jnp.zeros((nb, PAD_ROWS, d), F32), jnp.broadcast_to(meta[None], (nb, N_META, d)), x], axis=1)
    target = jnp.pad(loss_target, ((0, 0), (CHUNK, 0), (0, 0)))
    loss_part, dh0, g = _local_step(h0.reshape(nb * lp, d), target.reshape(nb * lp, d), w, nb, lp, sched)
    dh0 = dh0.reshape(nb, lp, d)
    grad_x = dh0[:, CHUNK:, :]

    gs = {nm: g[nm] for nm in SMALL_NAMES}
    gs["q_head_norm"] = g["q_head_norm"][:, 0:D_QK]
    gs["k_head_norm"] = g["k_head_norm"][:, 0:D_QK]
    for nm in ("conv_b", "gate_a_b", "gate_x_b", "lru_lambda"):
        gs[nm] = g[nm].reshape(1, LRU_W)
    gs["conv_w"] = g["conv_w"].transpose(1, 0, 2).reshape(CONV_K, LRU_W)
    gs["gate_a_w"] = _gate_blocks(g["gate_a_w"])
    gs["gate_x_w"] = _gate_blocks(g["gate_x_w"])
    pack = _pack_small(gs, ROW_META)
    pack = jnp.concatenate([pack, jnp.sum(dh0[:, PAD_ROWS:CHUNK, :], axis=0), _row(loss_part[:, 0:1]),
                            jnp.zeros((SMALL_ROWS - ROW_LOSS - 1, D_MODEL), F32)], axis=0)
    small_like = {nm: p[nm] for nm in SMALL_NAMES + ["gate_a_w", "gate_x_w"]}

    def pack_w(t):
        tt = {nm: t[nm] for nm in SMALL_NAMES + ["gate_a_w", "gate_x_w"]}
        tt["conv_w"] = jnp.zeros((CONV_K, LRU_W), F32)
        return _pack_small(tt, SMALL_ADAM_ROWS)

    gsum, dsm, msm, vsm = _small_update_call(_gather_small_call(pack), pack_w(p), pack_w(mom), pack_w(var))
    grads = _unpack_small(gsum, small_like)
    delta = _unpack_small(dsm, small_like)
    new_m = _unpack_small(msm, small_like)
    new_v = _unpack_small(vsm, small_like)
    loss = gsum[ROW_LOSS, 0]
    gmeta = gsum[ROW_META:ROW_META + N_META].reshape(N_META, N_SHARD, D_MODEL // N_SHARD)
    grads["meta_tokens"] = lax.dynamic_index_in_dim(gmeta, chip, axis=1, keepdims=False)
    gconv = gsum[ROW_CONV_W:ROW_CONV_W + 2].reshape(CONV_K, N_SHARD, LRU_TILE)
    grads["conv_w"] = lax.dynamic_index_in_dim(gconv, chip, axis=1, keepdims=False)[None]
    for nm in ("meta_tokens", "conv_w"):
        delta[nm], new_m[nm], new_v[nm] = _adamw_call("adamw_" + nm, p[nm], grads[nm], mom[nm], var[nm])

    done = REDUCE_PLAN[0][0] + REDUCE_PLAN[1][0]
    last = REDUCE_PLAN[2][0]
    mine = sched.chip_sums(done)
    shard_grads = dict(zip(done, _comm_call("share_pair", _share_pair_comm([mine[nm] for nm in done], lambda o: o))))

    def adamw(names, comm=None):
        res = _adamw_group_call("adamw_" + names[0], [p2[nm] for nm in names], [shard_grads[nm] for nm in names],
                                [m2[nm] for nm in names], [v2[nm] for nm in names], comm)
        for nm, (gg, dd, mm, vv) in zip(names, res):
            grads[nm], delta[nm], new_m[nm], new_v[nm] = (_from2d(nm, t) for t in (gg, dd, mm, vv))

    groups = _same_shape_groups(done)
    adamw(groups[0], sched.chips(last))
    for grp in groups[1:]:
        adamw(grp)
    mine = sched.chip_sums(last)
    shard_grads = dict(zip(last, _comm_call("share_last", _share_pair_comm([mine[nm] for nm in last], lambda o: o))))
    adamw(last)

    return (loss, grad_x, *[grads[nm] for nm in WEIGHT_NAMES], *[delta[nm] for nm in WEIGHT_NAMES],
            *[new_m[nm] for nm in WEIGHT_NAMES], *[new_v[nm] for nm in WEIGHT_NAMES])
```

```python
import functools
import math

import jax
import jax.numpy as jnp
from jax import lax
from jax.experimental import pallas as pl
from jax.experimental.pallas import tpu as pltpu

F32 = jnp.float32
BF16 = jnp.bfloat16
MESH = pl.DeviceIdType.MESH

D_MODEL = 1024
N_META = 16
CHUNK = 64
PAD_ROWS = CHUNK - N_META
HEADS = 4
D_NOPE = 128
D_ROPE = 64
D_QK = D_NOPE + D_ROPE
D_QKP = 256
D_V = 128
KV_RANK = 256
Q_RANK = 384
MLA_W = HEADS * D_V
LRU_W = 512
LRU_TILE = 128
N_LRU_TILES = LRU_W // LRU_TILE
CONV_K = 4
C_RGLRU = 8.0
ROPE_THETA = 10000.0
D_FF = 2816
N_SHARD = 4
EPS = 1e-6
NEG_INF = -1e30
Z_KR = Q_RANK + KV_RANK
Z_MLA = Z_KR + 128
Z_U = Z_MLA
Z_G = Z_U + LRU_W
Z_W = Z_G + LRU_W
IN_WIDTH = Q_RANK + KV_RANK + D_ROPE + 2 * LRU_W

ADAM_LR = 0.001
ADAM_B1 = 0.9
ADAM_B2 = 0.999
ADAM_EPS = 1e-08
ADAM_WD = 0.01
ADAM_STEP = 10

VMEM_LIMIT_BYTES = 56 * 1024 * 1024
SMALL_ROWS = 104
SMALL_ADAM_ROWS = 80


def _params(sem):
    return pltpu.CompilerParams(dimension_semantics=sem, vmem_limit_bytes=VMEM_LIMIT_BYTES)


def _row_tile(rows, target):
    best = 16
    for t in range(16, min(rows, target) + 1, 16):
        if rows % t == 0:
            best = t
    return best


def _col_tile(cols, target):
    best = cols
    for t in range(128, min(cols, target) + 1, 128):
        if cols % t == 0:
            best = t
    return best


def _dot(a, b):
    return jnp.dot(a, b, preferred_element_type=F32)


def _dot_nt(a, b):
    return lax.dot_general(a, b, (((1,), (1,)), ((), ())), preferred_element_type=F32)


def _dot_tn(a, b):
    return lax.dot_general(a, b, (((0,), (0,)), ((), ())), preferred_element_type=F32)


def _rms(x, n):
    return lax.rsqrt(jnp.sum(x * x, axis=-1, keepdims=True) * (1.0 / n) + EPS)


def _rms_bwd(dn, nrm, r, n):
    return r * (dn - nrm * (jnp.sum(dn * nrm, axis=-1, keepdims=True) * (1.0 / n)))


def _gelu(x):
    k = math.sqrt(2.0 / math.pi)
    t = jnp.tanh(k * (x + 0.044715 * x * x * x))
    return 0.5 * x * (1.0 + t), t


def _gelu_grad(x, t):
    k = math.sqrt(2.0 / math.pi)
    return 0.5 * (1.0 + t) + 0.5 * x * (1.0 - t * t) * k * (1.0 + 3.0 * 0.044715 * x * x)


def _neg_expm1(x):
    series = -x * (1.0 + x * (0.5 + x * (1.0 / 6 + x * (1.0 / 24 + x * (1.0 / 120 + x * (1.0 / 720))))))
    return jnp.where(x > -0.25, series, 1.0 - jnp.exp(x))


def _softplus_neg(lam):
    e = jnp.exp(-jnp.abs(lam))
    log1p = jnp.where(e < 0.01, e * (1.0 - e * (0.5 - e * (1.0 / 3 - e * 0.25))), jnp.log(1.0 + e))
    return jnp.maximum(-lam, 0.0) + log1p


def _rope(t, c, s1, s2):
    return t * c + pltpu.roll(t, 96, 1) * s1 + pltpu.roll(t, 32, 1) * s2


def _rope_t(d, c, s1, s2):
    return d * c + pltpu.roll(d * s1, 32, 1) + pltpu.roll(d * s2, 96, 1)


def _rope_tables(lp):
    pos = (jnp.arange(lp, dtype=jnp.int32) - PAD_ROWS).astype(F32)
    inv_freq = ROPE_THETA ** (-jnp.arange(0, D_ROPE // 2, dtype=F32) / (D_ROPE // 2))
    ang = pos[:, None] * inv_freq[None, :]
    cos, sin = jnp.cos(ang), jnp.sin(ang)
    z = jnp.zeros_like(cos)
    return (jnp.concatenate([cos, cos, z, z], 1), jnp.concatenate([-sin, z, z, z], 1),
            jnp.concatenate([z, sin, z, z], 1))


def _rmsnorm_call(name, h, g, tm):
    rows, d = h.shape

    def body(h_ref, g_ref, o_ref):
        x = h_ref[...]
        o_ref[...] = (x * _rms(x, d) * g_ref[...]).astype(BF16)

    return pl.pallas_call(
        body, name=name, grid=(rows // tm,),
        in_specs=[pl.BlockSpec((tm, d), lambda i: (i, 0)), pl.BlockSpec((1, d), lambda i: (0, 0))],
        out_specs=pl.BlockSpec((tm, d), lambda i: (i, 0)),
        out_shape=jax.ShapeDtypeStruct((rows, d), BF16),
        compiler_params=_params(("parallel",)))(h, g)


def _ffn_up_call(name, u, wg, wu, tm, comm=None):
    rows, d = u.shape
    ns, fs, _ = wg.shape

    def body(u_ref, wg_ref, wu_ref, g_ref, p_ref, a_ref):
        uu = u_ref[...]
        g = _dot_nt(uu, wg_ref[0])
        p = _dot_nt(uu, wu_ref[0])
        g_ref[0] = g.astype(BF16)
        p_ref[0] = p.astype(BF16)
        a_ref[0] = (g * jax.nn.sigmoid(g) * p).astype(BF16)

    wspec = pl.BlockSpec((1, fs, d), lambda s, i: (s, 0, 0))
    ospec = pl.BlockSpec((1, tm, fs), lambda s, i: (s, i, 0))
    oshape = jax.ShapeDtypeStruct((ns, rows, fs), BF16)
    return _hosted_call(
        body, name=name, grid=(ns, rows // tm),
        in_specs=[pl.BlockSpec((tm, d), lambda s, i: (i, 0)), wspec, wspec],
        out_specs=[ospec, ospec, ospec], out_shape=[oshape, oshape, oshape],
        dims=("parallel", "parallel"), args=(u, wg, wu), comm=comm)


def _ffn_gate_call(name, u, wg, tm, comm=None):
    rows, d = u.shape
    ns, fs, _ = wg.shape

    def body(u_ref, wg_ref, g_ref):
        g_ref[0] = _dot_nt(u_ref[...], wg_ref[0]).astype(BF16)

    return _hosted_call(
        body, name=name, grid=(ns, rows // tm),
        in_specs=[pl.BlockSpec((tm, d), lambda s, i: (i, 0)), pl.BlockSpec((1, fs, d), lambda s, i: (s, 0, 0))],
        out_specs=[pl.BlockSpec((1, tm, fs), lambda s, i: (s, i, 0))],
        out_shape=[jax.ShapeDtypeStruct((ns, rows, fs), BF16)],
        dims=("parallel", "parallel"), args=(u, wg), comm=comm)[0]


def _ffn_upact_call(name, u, wu, gate, tm, comm=None):
    rows, d = u.shape
    ns, fs, _ = wu.shape

    def body(u_ref, wu_ref, g_ref, p_ref, a_ref):
        p = _dot_nt(u_ref[...], wu_ref[0])
        g = g_ref[0].astype(F32)
        p_ref[0] = p.astype(BF16)
        a_ref[0] = (g * jax.nn.sigmoid(g) * p).astype(BF16)

    ospec = pl.BlockSpec((1, tm, fs), lambda s, i: (s, i, 0))
    oshape = jax.ShapeDtypeStruct((ns, rows, fs), BF16)
    return _hosted_call(
        body, name=name, grid=(ns, rows // tm),
        in_specs=[pl.BlockSpec((tm, d), lambda s, i: (i, 0)), pl.BlockSpec((1, fs, d), lambda s, i: (s, 0, 0)), ospec],
        out_specs=[ospec, ospec], out_shape=[oshape, oshape],
        dims=("parallel", "parallel"), args=(u, wu, gate), comm=comm)


def _ffn_down_call(name, a, wd, h, tm, comm=None):
    rows, d = h.shape
    ns, _, fs = a.shape

    def body(a_ref, wd_ref, h_ref, o_ref):
        acc = h_ref[...]
        for s in range(ns):
            acc = acc + 0.5 * _dot(a_ref[s], wd_ref[s])
        o_ref[...] = acc

    return _hosted_call(
        body, name=name, grid=(rows // tm,),
        in_specs=[pl.BlockSpec((ns, tm, fs), lambda i: (0, i, 0)),
                  pl.BlockSpec((ns, fs, d), lambda i: (0, 0, 0)),
                  pl.BlockSpec((tm, d), lambda i: (i, 0))],
        out_specs=[pl.BlockSpec((tm, d), lambda i: (i, 0))],
        out_shape=[jax.ShapeDtypeStruct((rows, d), F32)],
        dims=("parallel",), args=(a, wd, h), comm=comm)[0]


def _mm_call(name, a, bt, tm, out_dtype):
    rows, k = a.shape
    n = bt.shape[0]

    def body(a_ref, b_ref, o_ref):
        o_ref[...] = _dot_nt(a_ref[...], b_ref[...]).astype(out_dtype)

    return pl.pallas_call(
        body, name=name, grid=(rows // tm,),
        in_specs=[pl.BlockSpec((tm, k), lambda i: (i, 0)), pl.BlockSpec((n, k), lambda i: (0, 0))],
        out_specs=pl.BlockSpec((tm, n), lambda i: (i, 0)),
        out_shape=jax.ShapeDtypeStruct((rows, n), out_dtype),
        compiler_params=_params(("parallel",)))(a, bt)


def _mla_heads(z, gql, gkvl, wuq, wuk, wuv):
    cq = z[:, 0:Q_RANK]
    ckv = z[:, Q_RANK:Z_KR]
    kr = z[:, Z_KR:Z_MLA]
    rq = _rms(cq, Q_RANK)
    nq = cq * rq
    cqn = (nq * gql).astype(BF16)
    rkv = _rms(ckv, KV_RANK)
    nkv = ckv * rkv
    ckvn = (nkv * gkvl).astype(BF16)
    qraw = _dot_nt(cqn, wuq)
    knope = _dot(ckvn, wuk)
    v = _dot(ckvn, wuv)
    skr = jnp.sum(kr * kr, axis=-1, keepdims=True)
    heads = []
    for hd in range(HEADS):
        qh = qraw[:, hd * D_QKP:(hd + 1) * D_QKP]
        rqh = lax.rsqrt(jnp.sum(qh * qh, axis=-1, keepdims=True) * (1.0 / D_QK) + EPS)
        kn = knope[:, hd * D_NOPE:(hd + 1) * D_NOPE]
        rkh = lax.rsqrt((jnp.sum(kn * kn, axis=-1, keepdims=True) + skr) * (1.0 / D_QK) + EPS)
        heads.append((qh * rqh, rqh, kn * rkh, kr * rkh, rkh))
    return dict(rq=rq, nq=nq, cqn=cqn, rkv=rkv, nkv=nkv, ckvn=ckvn, v=v, heads=heads)


def _mla_prep_call(z, gql, gkvl, gqh, gkh, wuq, wuk, wuv, tabs, lp, tm):
    rows = z.shape[0]
    tpe = lp // tm

    def body(z_ref, gql_ref, gkvl_ref, gqh_ref, gkh_ref, wuq_ref, wuk_ref, wuv_ref, c_ref, s1_ref, s2_ref,
             q_ref, k_ref, v_ref, cqn_ref, ckvn_ref):
        m = _mla_heads(z_ref[...], gql_ref[...], gkvl_ref[...], wuq_ref[...], wuk_ref[...], wuv_ref[...])
        c, s1, s2 = c_ref[...], s1_ref[...], s2_ref[...]
        gq, gk = gqh_ref[...], gkh_ref[...]
        row = (pl.program_id(0) % tpe) * tm + lax.broadcasted_iota(jnp.int32, (tm, 1), 0)
        spare = (lax.broadcasted_iota(jnp.int32, (1, D_QKP - D_NOPE), 1) == D_ROPE).astype(F32)
        kmask = jnp.where(row < PAD_ROWS, NEG_INF * math.sqrt(D_QK), 0.0) * spare
        for hd in range(HEADS):
            qn, _, knn, krn, _ = m["heads"][hd]
            qg = qn * gq
            q_ref[hd, :, 0:D_NOPE] = qg[:, 0:D_NOPE].astype(BF16)
            q_ref[hd, :, D_NOPE:D_QKP] = (_rope(qg[:, D_NOPE:D_QKP], c, s1, s2) + spare).astype(BF16)
            k_ref[hd, :, 0:D_NOPE] = (knn * gk[:, 0:D_NOPE]).astype(BF16)
            k_ref[hd, :, D_NOPE:D_QKP] = (_rope(krn * gk[:, D_NOPE:D_QKP], c, s1, s2) + kmask).astype(BF16)
            v_ref[hd] = m["v"][:, hd * D_V:(hd + 1) * D_V].astype(BF16)
        cqn_ref[...] = m["cqn"]
        ckvn_ref[...] = m["ckvn"]

    def const(shape):
        return pl.BlockSpec(shape, lambda i: tuple(0 for _ in shape))

    tab = pl.BlockSpec((tm, 128), lambda i: (i % tpe, 0))
    return pl.pallas_call(
        body, name="mla_prep", grid=(rows // tm,),
        in_specs=[pl.BlockSpec((tm, Z_MLA), lambda i: (i, 0)), const((1, Q_RANK)), const((1, KV_RANK)),
                  const((1, D_QKP)), const((1, D_QKP)), const((HEADS * D_QKP, Q_RANK)),
                  const((KV_RANK, HEADS * D_NOPE)), const((KV_RANK, HEADS * D_V)), tab, tab, tab],
        out_specs=[pl.BlockSpec((HEADS, tm, D_QKP), lambda i: (0, i, 0)),
                   pl.BlockSpec((HEADS, tm, D_QKP), lambda i: (0, i, 0)),
                   pl.BlockSpec((HEADS, tm, D_V), lambda i: (0, i, 0)),
                   pl.BlockSpec((tm, Q_RANK), lambda i: (i, 0)),
                   pl.BlockSpec((tm, KV_RANK), lambda i: (i, 0))],
        out_shape=[jax.ShapeDtypeStruct((HEADS, rows, D_QKP), BF16),
                   jax.ShapeDtypeStruct((HEADS, rows, D_QKP), BF16),
                   jax.ShapeDtypeStruct((HEADS, rows, D_V), BF16),
                   jax.ShapeDtypeStruct((rows, Q_RANK), BF16),
                   jax.ShapeDtypeStruct((rows, KV_RANK), BF16)],
        compiler_params=_params(("parallel",)))(z, gql, gkvl, gqh, gkh, wuq, wuk, wuv, *tabs)


Q_BLOCK_ROWS = 528


def _q_block(lp):
    return _row_tile(lp, Q_BLOCK_ROWS)


def _key_end(ext, lp):
    return min(lp, -(-ext // CHUNK) * CHUNK)


def _diag_bias(qb, j0, nk):
    shift = CHUNK.bit_length() - 1
    r = jnp.right_shift(j0 + lax.broadcasted_iota(jnp.int32, (qb, nk), 0), shift)
    c = jnp.right_shift(j0 + lax.broadcasted_iota(jnp.int32, (qb, nk), 1), shift)
    return jnp.where(c <= r, 0.0, NEG_INF)


def _attn_fwd_call(q, k, v, nb, lp, comm=None):
    rows = nb * lp
    qb = _q_block(lp)
    scale = 1.0 / math.sqrt(D_QK)

    def body(q_ref, k_ref, v_ref, o_ref, lse_ref):
        for j in range(lp // qb):
            j0, ext = j * qb, (j + 1) * qb
            kend = _key_end(ext, lp)
            qj = q_ref[0, j0:ext, :]
            sd = _dot_nt(qj, k_ref[0, j0:kend, :]) * scale + _diag_bias(qb, j0, kend - j0)
            mx = jnp.max(sd, axis=-1, keepdims=True)
            if j > 0:
                so = _dot_nt(qj, k_ref[0, 0:j0, :]) * scale
                mx = jnp.maximum(mx, jnp.max(so, axis=-1, keepdims=True))
            pd = jnp.exp(sd - mx)
            l = jnp.sum(pd, axis=-1, keepdims=True)
            o = _dot(pd.astype(BF16), v_ref[0, j0:kend, :])
            if j > 0:
                po = jnp.exp(so - mx)
                l = l + jnp.sum(po, axis=-1, keepdims=True)
                o = o + _dot(po.astype(BF16), v_ref[0, 0:j0, :])
            o_ref[j0:ext, :] = o / l
            lse_ref[0, j0:ext, :] = mx + jnp.log(l)

    return _hosted_call(
        body, name="attn_fwd", grid=(nb, HEADS),
        in_specs=[pl.BlockSpec((1, lp, D_QKP), lambda b, h: (h, b, 0)),
                  pl.BlockSpec((1, lp, D_QKP), lambda b, h: (h, b, 0)),
                  pl.BlockSpec((1, lp, D_V), lambda b, h: (h, b, 0))],
        out_specs=[pl.BlockSpec((lp, D_V), lambda b, h: (b, h)),
                   pl.BlockSpec((1, lp, 1), lambda b, h: (h, b, 0))],
        out_shape=[jax.ShapeDtypeStruct((rows, MLA_W), F32),
                   jax.ShapeDtypeStruct((HEADS, rows, 1), F32)],
        dims=("parallel", "parallel"), args=(q, k, v), comm=comm)


def _attn_bwd_call(q, k, v, o, lse, do, nb, lp):
    rows = nb * lp
    qb = _q_block(lp)
    scale = 1.0 / math.sqrt(D_QK)

    def body(q_ref, k_ref, v_ref, o_ref, lse_ref, do_ref, dq_ref, dk_ref, dv_ref, dk_acc, dv_acc):
        dk_acc[...] = jnp.zeros_like(dk_acc)
        dv_acc[...] = jnp.zeros_like(dv_acc)
        for j in range(lp // qb):
            j0, ext = j * qb, (j + 1) * qb
            kend = _key_end(ext, lp)
            dbias = _diag_bias(qb, j0, kend - j0)
            qj = q_ref[0, j0:ext, :]
            doj = do_ref[j0:ext, :]
            delta = jnp.sum(doj * o_ref[j0:ext, :], axis=-1, keepdims=True)
            dob = doj.astype(BF16)
            lse = lse_ref[0, j0:ext, :]
            dq = jnp.zeros((qb, D_QKP), F32)
            for lo, hi, bias in ((j0, kend, dbias), (0, j0, None)):
                if hi == lo:
                    continue
                kk = k_ref[0, lo:hi, :]
                s = _dot_nt(qj, kk) * scale
                p = jnp.exp((s if bias is None else s + bias) - lse)
                dv_acc[lo:hi, :] += _dot_tn(p.astype(BF16), dob)
                dp = _dot_nt(dob, v_ref[0, lo:hi, :])
                ds = (p * (dp - delta) * scale).astype(BF16)
                dq = dq + _dot(ds, kk)
                dk_acc[lo:hi, :] += _dot_tn(ds, qj)
            dq_ref[0, j0:ext, :] = dq.astype(BF16)
        dk_ref[0] = dk_acc[...].astype(BF16)
        dv_ref[0] = dv_acc[...].astype(BF16)

    qspec = pl.BlockSpec((1, lp, D_QKP), lambda b, h: (h, b, 0))
    vspec = pl.BlockSpec((1, lp, D_V), lambda b, h: (h, b, 0))
    ospec = pl.BlockSpec((lp, D_V), lambda b, h: (b, h))
    return pl.pallas_call(
        body, name="attn_bwd", grid=(nb, HEADS),
        in_specs=[qspec, qspec, vspec, ospec, pl.BlockSpec((1, lp, 1), lambda b, h: (h, b, 0)), ospec],
        out_specs=[qspec, qspec, vspec],
        out_shape=[jax.ShapeDtypeStruct((HEADS, rows, D_QKP), BF16),
                   jax.ShapeDtypeStruct((HEADS, rows, D_QKP), BF16),
                   jax.ShapeDtypeStruct((HEADS, rows, D_V), BF16)],
        scratch_shapes=[pltpu.VMEM((lp, D_QKP), F32), pltpu.VMEM((lp, D_V), F32)],
        compiler_params=_params(("parallel", "parallel")))(q, k, v, o, lse, do)


def _lru_gates(u, cw, cb, wa, ba, wx, bx, lam, lp):
    xc = (cw[3:4, :] * u + cw[2:3, :] * pltpu.roll(u, 1, 0) + cw[1:2, :] * pltpu.roll(u, 2, 0)
          + cw[0:1, :] * pltpu.roll(u, 3, 0) + cb)
    xcb = xc.astype(BF16)
    r = jax.nn.sigmoid(_dot(xcb, wa) + ba)
    i = jax.nn.sigmoid(_dot(xcb, wx) + bx)
    sp = _softplus_neg(lam)
    la = -C_RGLRU * r * sp
    a = jnp.exp(la)
    mult = jnp.sqrt(_neg_expm1(2.0 * la))
    row = lax.broadcasted_iota(jnp.int32, (lp, LRU_TILE), 0)
    first = row == PAD_ROWS
    valid = row >= PAD_ROWS
    mult_eff = jnp.where(first, 1.0, mult)
    return dict(xc=xc, xcb=xcb, r=r, i=i, sp=sp, la=la, a=a, mult=mult, mult_eff=mult_eff, first=first, valid=valid)


def _scan_rows(a, b, a_s, b_s, out_ref, lp, reverse):
    sub = lax.broadcasted_iota(jnp.int32, (lp, LRU_TILE), 0) & 7
    for dist in (1, 2, 4):
        shift = lp - dist if reverse else dist
        keep = (sub + dist <= 7) if reverse else (sub >= dist)
        a_sh = pltpu.roll(a, shift, 0)
        b_sh = pltpu.roll(b, shift, 0)
        b = jnp.where(keep, a * b_sh + b, b)
        a = jnp.where(keep, a * a_sh, a)
    a_s[...] = a
    b_s[...] = b
    n_groups = lp // 8
    edge = 0 if reverse else 7

    def group(gi, carry):
        r0 = pl.multiple_of(((n_groups - 1 - gi) if reverse else gi) * 8, 8)
        a8 = a_s[pl.ds(r0, 8), :]
        b8 = b_s[pl.ds(r0, 8), :]
        out_ref[pl.ds(r0, 8), :] = a8 * carry + b8
        return a8[edge:edge + 1, :] * carry + b8[edge:edge + 1, :]

    lax.fori_loop(0, n_groups, group, jnp.zeros((1, LRU_TILE), F32), unroll=4)


def _lru_specs(lp):
    seq = lambda col0: pl.BlockSpec((lp, LRU_TILE), lambda t, b: (b, col0 + t))
    cw = pl.BlockSpec((1, CONV_K, LRU_TILE), lambda t, b: (t, 0, 0))
    vec = pl.BlockSpec((1, LRU_TILE), lambda t, b: (0, t))
    mat = pl.BlockSpec((1, LRU_TILE, LRU_TILE), lambda t, b: (t, 0, 0))
    return seq, cw, vec, mat


def _lru_fwd_call(z, cw, cb, wa, ba, wx, bx, lam, nb, lp, comm=None):
    rows = nb * lp
    seq, cwspec, vec, mat = _lru_specs(lp)

    def body(u_ref, g_ref, cw_ref, cb_ref, wa_ref, ba_ref, wx_ref, bx_ref, lam_ref, y_ref, hs_ref, a_s, b_s):
        m = _lru_gates(u_ref[...], cw_ref[0], cb_ref[...], wa_ref[0], ba_ref[...], wx_ref[0], bx_ref[...],
                       lam_ref[...], lp)
        a = jnp.where(m["valid"], m["a"], 0.0)
        b = jnp.where(m["valid"], m["mult_eff"] * (m["i"] * m["xc"]), 0.0)
        _scan_rows(a, b, a_s, b_s, hs_ref, lp, reverse=False)
        gl, _ = _gelu(g_ref[...])
        y_ref[...] = hs_ref[...] * gl

    oshape = jax.ShapeDtypeStruct((rows, LRU_W), F32)
    return _hosted_call(
        body, name="lru_fwd", grid=(N_LRU_TILES, nb),
        in_specs=[seq(Z_U // LRU_TILE), seq(Z_G // LRU_TILE), cwspec, vec, mat, vec, mat, vec, vec],
        out_specs=[seq(0), seq(0)], out_shape=[oshape, oshape],
        scratch_shapes=[pltpu.VMEM((lp, LRU_TILE), F32), pltpu.VMEM((lp, LRU_TILE), F32)],
        dims=("parallel", "parallel"), args=(z, z, cw, cb, wa, ba, wx, bx, lam), comm=comm)


def _lru_bwd_call(z, hs, dy, cw, cb, wa, ba, wx, bx, lam, nb, lp, comm=None):
    rows = nb * lp
    seq, cwspec, vec, mat = _lru_specs(lp)

    def body(u_ref, g_ref, hs_ref, dy_ref, cw_ref, cb_ref, wa_ref, ba_ref, wx_ref, bx_ref, lam_ref,
             du_ref, dg_ref, dcw_ref, dcb_ref, dwa_ref, dba_ref, dwx_ref, dbx_ref, dlam_ref, a_s, b_s, d_s):
        b_idx = pl.program_id(1)
        u = u_ref[...]
        cw = cw_ref[0]
        wa, wx = wa_ref[0], wx_ref[0]
        lam = lam_ref[...]
        m = _lru_gates(u, cw, cb_ref[...], wa, ba_ref[...], wx, bx_ref[...], lam, lp)
        gate = g_ref[...]
        gl, th = _gelu(gate)
        dy = dy_ref[...]
        hs = hs_ref[...]
        dg_ref[...] = (dy * hs * _gelu_grad(gate, th)).astype(BF16)
        a_eff = jnp.where(m["valid"], m["a"], 0.0)
        _scan_rows(pltpu.roll(a_eff, lp - 1, 0), dy * gl, a_s, b_s, d_s, lp, reverse=True)
        ds = d_s[...]
        xc, r, i = m["xc"], m["r"], m["i"]
        row = lax.broadcasted_iota(jnp.int32, (lp, LRU_TILE), 0)
        da = ds * jnp.where(row >= 1, pltpu.roll(hs, 1, 0), 0.0)
        db = jnp.where(m["valid"], ds, 0.0)
        di = db * m["mult_eff"] * xc
        dxc = db * m["mult_eff"] * i
        live = m["valid"] & jnp.logical_not(m["first"])
        e2 = jnp.exp(2.0 * m["la"])
        dm = jnp.where(live, db * i * xc, 0.0)
        dla = da * m["a"] + jnp.where(live, dm * (-e2 / m["mult"]), 0.0)
        dr = dla * (-C_RGLRU * m["sp"])
        dsp = jnp.sum(dla * (-C_RGLRU * r), axis=0, keepdims=True)
        dpr = (dr * r * (1.0 - r))
        dpi = (di * i * (1.0 - i))
        dprb, dpib = dpr.astype(BF16), dpi.astype(BF16)
        dxc = dxc + _dot_nt(dprb, wa) + _dot_nt(dpib, wx)
        du = (cw[3:4, :] * dxc + cw[2:3, :] * pltpu.roll(dxc, lp - 1, 0) + cw[1:2, :] * pltpu.roll(dxc, lp - 2, 0)
              + cw[0:1, :] * pltpu.roll(dxc, lp - 3, 0))
        du_ref[...] = jnp.where(m["valid"], du, 0.0).astype(BF16)
        tap = lax.broadcasted_iota(jnp.int32, (CONV_K, LRU_TILE), 0)
        dcw = jnp.zeros((CONV_K, LRU_TILE), F32)
        for kk in range(CONV_K):
            shifted = u if kk == CONV_K - 1 else pltpu.roll(u, CONV_K - 1 - kk, 0)
            dcw = jnp.where(tap == kk, jnp.sum(dxc * shifted, axis=0, keepdims=True), dcw)
        parts = [(dcw_ref, dcw[None]), (dcb_ref, jnp.sum(dxc, axis=0, keepdims=True)[None]),
                 (dwa_ref, _dot_tn(m["xcb"], dprb)[None]), (dba_ref, jnp.sum(dpr, axis=0, keepdims=True)[None]),
                 (dwx_ref, _dot_tn(m["xcb"], dpib)[None]), (dbx_ref, jnp.sum(dpi, axis=0, keepdims=True)[None]),
                 (dlam_ref, (dsp * (-jax.nn.sigmoid(-lam)))[None])]

        @pl.when(b_idx == 0)
        def _():
            for ref, val in parts:
                ref[...] = val

        @pl.when(b_idx != 0)
        def _():
            for ref, val in parts:
                ref[...] += val

    bshape = jax.ShapeDtypeStruct((rows, LRU_W), BF16)
    vec3 = pl.BlockSpec((1, 1, LRU_TILE), lambda t, b: (t, 0, 0))
    vshape = jax.ShapeDtypeStruct((N_LRU_TILES, 1, LRU_TILE), F32)
    mshape = jax.ShapeDtypeStruct((N_LRU_TILES, LRU_TILE, LRU_TILE), F32)
    return _hosted_call(
        body, name="lru_bwd", grid=(N_LRU_TILES, nb),
        in_specs=[seq(Z_U // LRU_TILE), seq(Z_G // LRU_TILE), seq(0), seq(0), cwspec, vec, mat, vec, mat, vec, vec],
        out_specs=[seq(0), seq(0), cwspec, vec3, mat, vec3, mat, vec3, vec3],
        out_shape=[bshape, bshape, jax.ShapeDtypeStruct((N_LRU_TILES, CONV_K, LRU_TILE), F32), vshape, mshape,
                   vshape, mshape, vshape, vshape],
        scratch_shapes=[pltpu.VMEM((lp, LRU_TILE), F32)] * 3,
        dims=("parallel", "arbitrary"), args=(z, z, hs, dy, cw, cb, wa, ba, wx, bx, lam), comm=comm)


def _mix_out_call(ya, yl, ga, gl, wout, h, tm):
    rows, d = h.shape

    def body(ya_ref, yl_ref, ga_ref, gl_ref, w_ref, h_ref, y_ref, o_ref):
        a = ya_ref[...]
        l = yl_ref[...]
        an = (a * _rms(a, MLA_W) * ga_ref[...]).astype(BF16)
        ln = (l * _rms(l, LRU_W) * gl_ref[...]).astype(BF16)
        y_ref[:, 0:MLA_W] = an
        y_ref[:, MLA_W:MLA_W + LRU_W] = ln
        o_ref[...] = h_ref[...] + _dot(an, w_ref[0:MLA_W, :]) + _dot(ln, w_ref[MLA_W:MLA_W + LRU_W, :])

    half = pl.BlockSpec((tm, MLA_W), lambda i: (i, 0))
    g = pl.BlockSpec((1, MLA_W), lambda i: (0, 0))
    full = pl.BlockSpec((tm, d), lambda i: (i, 0))
    return pl.pallas_call(
        body, name="mix_out", grid=(rows // tm,),
        in_specs=[half, half, g, g, pl.BlockSpec((MLA_W + LRU_W, d), lambda i: (0, 0)), full],
        out_specs=[full, full],
        out_shape=[jax.ShapeDtypeStruct((rows, MLA_W + LRU_W), BF16), jax.ShapeDtypeStruct((rows, d), F32)],
        compiler_params=_params(("parallel",)))(ya, yl, ga, gl, wout, h)


def _mix_out_bwd_call(dhb, wout, ya, yl, ga, gl, tm):
    rows = ya.shape[0]
    d = dhb.shape[1]

    def body(dh_ref, w_ref, ya_ref, yl_ref, ga_ref, gl_ref, dya_ref, dyl_ref, dga_ref, dgl_ref):
        dy = _dot_nt(dh_ref[...], w_ref[...])
        outs = []
        for val, g_ref, lo, out_ref in ((ya_ref[...], ga_ref, 0, dya_ref), (yl_ref[...], gl_ref, MLA_W, dyl_ref)):
            r = _rms(val, MLA_W)
            n = val * r
            dyn = dy[:, lo:lo + MLA_W]
            out_ref[...] = _rms_bwd(dyn * g_ref[...], n, r, MLA_W)
            outs.append(jnp.sum(dyn * n, axis=0, keepdims=True))

        @pl.when(pl.program_id(0) == 0)
        def _():
            dga_ref[...] = outs[0]
            dgl_ref[...] = outs[1]

        @pl.when(pl.program_id(0) != 0)
        def _():
            dga_ref[...] += outs[0]
            dgl_ref[...] += outs[1]

    half = pl.BlockSpec((tm, MLA_W), lambda i: (i, 0))
    g = pl.BlockSpec((1, MLA_W), lambda i: (0, 0))
    return pl.pallas_call(
        body, name="mix_out_bwd", grid=(rows // tm,),
        in_specs=[pl.BlockSpec((tm, d), lambda i: (i, 0)), pl.BlockSpec((MLA_W + LRU_W, d), lambda i: (0, 0)),
                  half, half, g, g],
        out_specs=[half, half, g, g],
        out_shape=[jax.ShapeDtypeStruct((rows, MLA_W), F32), jax.ShapeDtypeStruct((rows, LRU_W), F32),
                   jax.ShapeDtypeStruct((1, MLA_W), F32), jax.ShapeDtypeStruct((1, LRU_W), F32)],
        compiler_params=_params(("arbitrary",)))(dhb, wout, ya, yl, ga, gl)


def _final_call(h, g, target, lp, tm):
    rows, d = h.shape
    tpe = lp // tm

    def body(h_ref, g_ref, t_ref, dh_ref, dhb_ref, dg_ref, loss_ref):
        i = pl.program_id(0)
        x = h_ref[...]
        g = g_ref[...]
        r = _rms(x, d)
        n = x * r
        row = (i % tpe) * tm + lax.broadcasted_iota(jnp.int32, (tm, 1), 0)
        err = jnp.where(row >= CHUNK, n * g - t_ref[...], 0.0)
        dout = err * (1.0 / d)
        dh = _rms_bwd(dout * g, n, r, d)
        dh_ref[...] = dh
        dhb_ref[...] = dh.astype(BF16)
        dg = jnp.sum(dout * n, axis=0, keepdims=True)
        part = jnp.sum(jnp.sum(err * err, axis=1, keepdims=True), axis=0, keepdims=True) * (0.5 / d)
        loss = jnp.broadcast_to(part, (1, 128))

        @pl.when(i == 0)
        def _():
            dg_ref[...] = dg
            loss_ref[...] = loss

        @pl.when(i != 0)
        def _():
            dg_ref[...] += dg
            loss_ref[...] += loss

    full = pl.BlockSpec((tm, d), lambda i: (i, 0))
    return pl.pallas_call(
        body, name="final_loss", grid=(rows // tm,),
        in_specs=[full, pl.BlockSpec((1, d), lambda i: (0, 0)), full],
        out_specs=[full, full, pl.BlockSpec((1, d), lambda i: (0, 0)), pl.BlockSpec((1, 128), lambda i: (0, 0))],
        out_shape=[jax.ShapeDtypeStruct((rows, d), F32), jax.ShapeDtypeStruct((rows, d), BF16),
                   jax.ShapeDtypeStruct((1, d), F32), jax.ShapeDtypeStruct((1, 128), F32)],
        compiler_params=_params(("arbitrary",)))(h, g, target)


def _ffn_dact_call(name, dhb, wd, gate, up, tm, comm=None):
    rows, d = dhb.shape
    ns, fs, _ = wd.shape

    nsub = 2 if tm % 32 == 0 else 1
    sub = tm // nsub

    def body(dh_ref, wd_ref, g_ref, p_ref, dg_ref, dp_ref):
        wd = wd_ref[0]
        for r in range(nsub):
            rs = slice(r * sub, (r + 1) * sub)
            da = 0.5 * _dot_nt(dh_ref[rs, :], wd)
            g = g_ref[0, rs, :].astype(F32)
            p = p_ref[0, rs, :].astype(F32)
            sg = jax.nn.sigmoid(g)
            dg_ref[0, rs, :] = (da * p * sg * (1.0 + g * (1.0 - sg))).astype(BF16)
            dp_ref[0, rs, :] = (da * g * sg).astype(BF16)

    aspec = pl.BlockSpec((1, tm, fs), lambda s, i: (s, i, 0))
    oshape = jax.ShapeDtypeStruct((ns, rows, fs), BF16)
    return _hosted_call(
        body, name=name, grid=(ns, rows // tm),
        in_specs=[pl.BlockSpec((tm, d), lambda s, i: (i, 0)), pl.BlockSpec((1, fs, d), lambda s, i: (s, 0, 0)),
                  aspec, aspec],
        out_specs=[aspec, aspec], out_shape=[oshape, oshape],
        dims=("parallel", "parallel"), args=(dhb, wd, gate, up), comm=comm)


def _norm_in_bwd_call(name, pieces, h, g, dres, tm, comm=None):
    rows, d = h.shape
    npc = len(pieces)

    def body(*refs):
        d_refs = refs[0:2 * npc:2]
        w_refs = refs[1:2 * npc:2]
        h_ref, g_ref, dres_ref, dh_ref, dhb_ref, dg_ref = refs[2 * npc:]
        du = jnp.zeros((tm, d), F32)
        for d_ref, w_ref in zip(d_refs, w_refs):
            if len(d_ref.shape) == 3:
                for s in range(d_ref.shape[0]):
                    du = du + _dot(d_ref[s], w_ref[s])
            else:
                du = du + _dot(d_ref[...], w_ref[...])
        x = h_ref[...]
        r = _rms(x, d)
        n = x * r
        dh = dres_ref[...] + _rms_bwd(du * g_ref[...], n, r, d)
        dh_ref[...] = dh
        dhb_ref[...] = dh.astype(BF16)
        dg = jnp.sum(du * n, axis=0, keepdims=True)

        @pl.when(pl.program_id(0) == 0)
        def _():
            dg_ref[...] = dg

        @pl.when(pl.program_id(0) != 0)
        def _():
            dg_ref[...] += dg

    in_specs, args = [], []
    for dd, w in pieces:
        if dd.ndim == 3:
            in_specs.append(pl.BlockSpec((dd.shape[0], tm, dd.shape[2]), lambda i: (0, i, 0)))
            in_specs.append(pl.BlockSpec(w.shape, lambda i: (0, 0, 0)))
        else:
            in_specs.append(pl.BlockSpec((tm, dd.shape[1]), lambda i: (i, 0)))
            in_specs.append(pl.BlockSpec(w.shape, lambda i: (0, 0)))
        args += [dd, w]
    full = pl.BlockSpec((tm, d), lambda i: (i, 0))
    gspec = pl.BlockSpec((1, d), lambda i: (0, 0))
    return _hosted_call(
        body, name=name, grid=(rows // tm,),
        in_specs=in_specs + [full, gspec, full],
        out_specs=[full, full, gspec],
        out_shape=[jax.ShapeDtypeStruct((rows, d), F32), jax.ShapeDtypeStruct((rows, d), BF16),
                   jax.ShapeDtypeStruct((1, d), F32)],
        args=(*args, h, g, dres), comm=comm)


def _wgrad_call(name, a, b, scale=1.0, comm=None):
    a3, b3 = a.ndim == 3, b.ndim == 3
    ns = a.shape[0] if a3 else (b.shape[0] if b3 else 1)
    rows, m = a.shape[-2:]
    n = b.shape[-1]
    tmm = m if a3 else _col_tile(m, 256)

    def body(a_ref, b_ref, o_ref):
        av = a_ref[0] if a3 else a_ref[...]
        bv = b_ref[0] if b3 else b_ref[...]
        res = _dot_tn(av, bv)
        if scale != 1.0:
            res = res * scale
        if a3 or b3:
            o_ref[0] = res
        else:
            o_ref[...] = res

    aspec = (pl.BlockSpec((1, rows, tmm), lambda s, j: (s, 0, j)) if a3
             else pl.BlockSpec((rows, tmm), lambda s, j: (0, j)))
    bspec = (pl.BlockSpec((1, rows, n), lambda s, j: (s, 0, 0)) if b3
             else pl.BlockSpec((rows, n), lambda s, j: (0, 0)))
    if a3 or b3:
        ospec = pl.BlockSpec((1, tmm, n), lambda s, j: (s, j, 0))
        oshape = jax.ShapeDtypeStruct((ns, m, n), F32)
    else:
        ospec = pl.BlockSpec((tmm, n), lambda s, j: (j, 0))
        oshape = jax.ShapeDtypeStruct((m, n), F32)
    return _hosted_call(
        body, name=name, grid=(ns, m // tmm), in_specs=[aspec, bspec], out_specs=[ospec], out_shape=[oshape],
        dims=("parallel", "parallel"), args=(a, b), comm=comm)[0]


def _mla_prep_bwd_call(z, dq, dk, dv, gql, gkvl, gqh, gkh, wuq, wuk, wuv, tabs, lp, tm):
    rows = z.shape[0]
    tpe = lp // tm

    def body(z_ref, dq_ref, dk_ref, dv_ref, gql_ref, gkvl_ref, gqh_ref, gkh_ref, wuq_ref, wuk_ref, wuv_ref,
             c_ref, s1_ref, s2_ref, dz_ref, dqp_ref, dkn_ref, dvv_ref, dgql_ref, dgkvl_ref, dgqh_ref, dgkh_ref):
        gql, gkvl = gql_ref[...], gkvl_ref[...]
        gq, gk = gqh_ref[...], gkh_ref[...]
        wuq, wuk, wuv = wuq_ref[...], wuk_ref[...], wuv_ref[...]
        m = _mla_heads(z_ref[...], gql, gkvl, wuq, wuk, wuv)
        c, s1, s2 = c_ref[...], s1_ref[...], s2_ref[...]
        dgq = jnp.zeros((1, D_QKP), F32)
        dgk = jnp.zeros((1, D_QKP), F32)
        dkr = jnp.zeros((tm, D_QKP - D_NOPE), F32)
        for hd in range(HEADS):
            qn, rqh, knn, krn, rkh = m["heads"][hd]
            dqg = jnp.concatenate([dq_ref[hd, :, 0:D_NOPE].astype(F32),
                                   _rope_t(dq_ref[hd, :, D_NOPE:D_QKP].astype(F32), c, s1, s2)], axis=1)
            dgq = dgq + jnp.sum(dqg * qn, axis=0, keepdims=True)
            dqn = dqg * gq
            dqr = rqh * (dqn - qn * (jnp.sum(dqn * qn, axis=-1, keepdims=True) * (1.0 / D_QK)))
            dqp_ref[:, hd * D_QKP:(hd + 1) * D_QKP] = dqr.astype(BF16)
            kn_full = jnp.concatenate([knn, krn], axis=1)
            dkg = jnp.concatenate([dk_ref[hd, :, 0:D_NOPE].astype(F32),
                                   _rope_t(dk_ref[hd, :, D_NOPE:D_QKP].astype(F32), c, s1, s2)], axis=1)
            dgk = dgk + jnp.sum(dkg * kn_full, axis=0, keepdims=True)
            dkn = dkg * gk
            dkraw = rkh * (dkn - kn_full * (jnp.sum(dkn * kn_full, axis=-1, keepdims=True) * (1.0 / D_QK)))
            dkn_ref[:, hd * D_NOPE:(hd + 1) * D_NOPE] = dkraw[:, 0:D_NOPE].astype(BF16)
            dkr = dkr + dkraw[:, D_NOPE:D_QKP]
            dvv_ref[:, hd * D_V:(hd + 1) * D_V] = dv_ref[hd]
        dcqn = _dot(dqp_ref[...], wuq)
        dckvn = _dot_nt(dkn_ref[...], wuk) + _dot_nt(dvv_ref[...], wuv)
        dz_ref[:, 0:Q_RANK] = _rms_bwd(dcqn * gql, m["nq"], m["rq"], Q_RANK).astype(BF16)
        dz_ref[:, Q_RANK:Z_KR] = _rms_bwd(dckvn * gkvl, m["nkv"], m["rkv"], KV_RANK).astype(BF16)
        dz_ref[:, Z_KR:Z_MLA] = dkr.astype(BF16)
        parts = [(dgql_ref, jnp.sum(dcqn * m["nq"], axis=0, keepdims=True)),
                 (dgkvl_ref, jnp.sum(dckvn * m["nkv"], axis=0, keepdims=True)), (dgqh_ref, dgq), (dgkh_ref, dgk)]

        @pl.when(pl.program_id(0) == 0)
        def _():
            for ref, val in parts:
                ref[...] = val

        @pl.when(pl.program_id(0) != 0)
        def _():
            for ref, val in parts:
                ref[...] += val

    def const(shape):
        return pl.BlockSpec(shape, lambda i: tuple(0 for _ in shape))

    tab = pl.BlockSpec((tm, 128), lambda i: (i % tpe, 0))
    hq = pl.BlockSpec((HEADS, tm, D_QKP), lambda i: (0, i, 0))
    hv = pl.BlockSpec((HEADS, tm, D_V), lambda i: (0, i, 0))

    def rowspec(n):
        return pl.BlockSpec((tm, n), lambda i: (i, 0))

    return pl.pallas_call(
        body, name="mla_prep_bwd", grid=(rows // tm,),
        in_specs=[rowspec(Z_MLA), hq, hq, hv, const((1, Q_RANK)), const((1, KV_RANK)), const((1, D_QKP)),
                  const((1, D_QKP)), const((HEADS * D_QKP, Q_RANK)), const((KV_RANK, HEADS * D_NOPE)),
                  const((KV_RANK, HEADS * D_V)), tab, tab, tab],
        out_specs=[rowspec(Z_MLA), rowspec(HEADS * D_QKP), rowspec(HEADS * D_NOPE), rowspec(HEADS * D_V),
                   const((1, Q_RANK)), const((1, KV_RANK)), const((1, D_QKP)), const((1, D_QKP))],
        out_shape=[jax.ShapeDtypeStruct((rows, Z_MLA), BF16), jax.ShapeDtypeStruct((rows, HEADS * D_QKP), BF16),
                   jax.ShapeDtypeStruct((rows, HEADS * D_NOPE), BF16), jax.ShapeDtypeStruct((rows, HEADS * D_V), BF16),
                   jax.ShapeDtypeStruct((1, Q_RANK), F32), jax.ShapeDtypeStruct((1, KV_RANK), F32),
                   jax.ShapeDtypeStruct((1, D_QKP), F32), jax.ShapeDtypeStruct((1, D_QKP), F32)],
        compiler_params=_params(("arbitrary",)))(z, dq, dk, dv, gql, gkvl, gqh, gkh, wuq, wuk, wuv, *tabs)


def _local_step(h0, target, w, nb, lp, sched=None):
    tm = _row_tile(lp, 1024)
    te = _row_tile(lp, 512)
    tabs = _rope_tables(lp)
    g = {}
    if sched is None:
        host = lambda stage: None
    else:
        sched.g = g
        host = sched.host

    def ffn_fwd(tag, h, split):
        u = _rmsnorm_call(tag + "_norm", h, w[tag + "_norm"], te)
        if split:
            gate = _ffn_gate_call(tag + "_gate", u, w[tag + "_w_gate"], tm, host(tag + "_gate"))
            up, act = _ffn_upact_call(tag + "_upact", u, w[tag + "_w_up"], gate, tm, host(tag + "_upact"))
        else:
            gate, up, act = _ffn_up_call(tag + "_up", u, w[tag + "_w_gate"], w[tag + "_w_up"], tm, host(tag + "_up"))
        return _ffn_down_call(tag + "_down", act, w[tag + "_w_down"], h, te, host(tag + "_down")), (u, gate, up, act)

    def ffn_bwd(tag, h, saved, dh, dhb):
        u, gate, up, act = saved
        dgate, dup = _ffn_dact_call(tag + "_dact", dhb, w[tag + "_w_down"], gate, up, tm, host(tag + "_dact"))
        g[tag + "_w_down"] = _wgrad_call(tag + "_dwd", act, dhb, 0.5, host(tag + "_dwd"))
        g[tag + "_w_gate"] = _wgrad_call(tag + "_dwg", dgate, u, 1.0, host(tag + "_dwg"))
        g[tag + "_w_up"] = _wgrad_call(tag + "_dwu", dup, u, 1.0, host(tag + "_dwu"))
        dh_in, dhb_in, g[tag + "_norm"] = _norm_in_bwd_call(
            tag + "_din", [(dgate, w[tag + "_w_gate"]), (dup, w[tag + "_w_up"])], h, w[tag + "_norm"], dh, te,
            host(tag + "_din"))
        return dh_in, dhb_in

    h1, s1 = ffn_fwd("ffn1", h0, True)
    un = _rmsnorm_call("mix_norm", h1, w["mix_norm"], te)
    z = _mm_call("mix_in", un, w["w_in"], tm, F32)
    mla_w = (w["q_latent_norm"], w["kv_latent_norm"], w["q_head_norm"], w["k_head_norm"], w["w_uq"], w["w_uk"],
             w["w_uv"])
    q, k, v, cqn, ckvn = _mla_prep_call(z, *mla_w, tabs, lp, te)
    o, lse = _attn_fwd_call(q, k, v, nb, lp, host("attn_fwd"))
    lru_w = (w["conv_w"], w["conv_b"], w["gate_a_w"], w["gate_a_b"], w["gate_x_w"], w["gate_x_b"], w["lru_lambda"])
    yl, hs = _lru_fwd_call(z, *lru_w, nb, lp, host("lru_fwd"))
    y, h2 = _mix_out_call(o, yl, w["attn_out_norm"], w["lru_out_norm"], w["w_out"], h1, te)
    h3, s2 = ffn_fwd("ffn2", h2, False)
    dh3, dh3b, g["final_norm"], loss = _final_call(h3, w["final_norm"], target, lp, te)
    g["loss"] = loss

    dh2, dh2b = ffn_bwd("ffn2", h2, s2, dh3, dh3b)
    g["w_out"] = _wgrad_call("dw_out", y, dh2b)
    dya, dyl, g["attn_out_norm"], g["lru_out_norm"] = _mix_out_bwd_call(
        dh2b, w["w_out"], o, yl, w["attn_out_norm"], w["lru_out_norm"], te)
    dq, dk, dv = _attn_bwd_call(q, k, v, o, lse, dya, nb, lp)
    (dz_mla, dqp, dkn, dvv, g["q_latent_norm"], g["kv_latent_norm"], g["q_head_norm"],
     g["k_head_norm"]) = _mla_prep_bwd_call(z, dq, dk, dv, *mla_w, tabs, lp, te)
    g["w_uq"] = _wgrad_call("dw_uq", dqp, cqn)
    g["w_uk"] = _wgrad_call("dw_uk", ckvn, dkn)
    g["w_uv"] = _wgrad_call("dw_uv", ckvn, dvv)
    (du, dgt, g["conv_w"], g["conv_b"], g["gate_a_w"], g["gate_a_b"], g["gate_x_w"], g["gate_x_b"],
     g["lru_lambda"]) = _lru_bwd_call(z, hs, dyl, *lru_w, nb, lp, host("lru_bwd"))
    win = w["w_in"]
    g["w_in"] = jnp.concatenate(
        [_wgrad_call("dw_in_mla", dz_mla, un), _wgrad_call("dw_in_u", du, un), _wgrad_call("dw_in_g", dgt, un)],
        axis=0)
    dh1, dh1b, g["mix_norm"] = _norm_in_bwd_call(
        "mix_din", [(dz_mla, win[0:Z_MLA]), (du, win[Z_U:Z_G]), (dgt, win[Z_G:Z_W])], h1, w["mix_norm"], dh2, te,
        host("mix_din"))
    dh0, _ = ffn_bwd("ffn1", h0, s1, dh1, dh1b)
    return loss, dh0, g


def _place():
    x, y, c = lax.axis_index("x"), lax.axis_index("y"), lax.axis_index("c")
    return x, y, c, [(1 - x, y), (x, 1 - y), (1 - x, 1 - y)]


def _any_specs(n):
    return [pl.BlockSpec(memory_space=pl.ANY)] * n


def _remote(src, dst, sems, k, dev):
    send_sems, recv_sems, base = sems
    return pltpu.make_async_remote_copy(src_ref=src, dst_ref=dst, send_sem=send_sems.at[base + k],
                                        recv_sem=recv_sems.at[base + k], device_id=dev, device_id_type=MESH)


EW_VMEM_BYTES = 24 * 1024 * 1024


def _fit_rows(rows, cols, blocks):
    return _row_tile(rows, max(16, int(EW_VMEM_BYTES // (8 * blocks)) // cols))


class _Geom:
    def __init__(self, n0, n1, blocks=1.0):
        self.n0, self.n1 = n0, n1
        self.axis = 0 if n0 % 32 == 0 else 1
        self.h0, self.h1 = (n0 // 2, n1) if self.axis == 0 else (n0, n1 // 2)
        self.tr = _fit_rows(self.h0, self.h1, blocks)
        self.nblk = self.h0 // self.tr

    def half_ref(self, ref, lead, idx):
        if self.axis == 0:
            return ref.at[(*lead, pl.ds(idx * self.h0, self.h0))]
        return ref.at[(*lead, slice(None), pl.ds(idx * self.h1, self.h1))]

    def half_block(self, lead, i, idx):
        return (*lead, idx * self.nblk + i, 0) if self.axis == 0 else (*lead, i, idx)


class _Comm:
    def __init__(self, ins, out_shapes, aliases, n_sems, start, finish, deliver):
        self.ins, self.out_shapes, self.aliases, self.n_sems = list(ins), list(out_shapes), dict(aliases), n_sems
        self.start, self.finish, self.deliver = start, finish, deliver

    def scratch(self):
        return [pltpu.SemaphoreType.DMA((self.n_sems,)), pltpu.SemaphoreType.DMA((self.n_sems,))]


def _comm_call(name, comm):
    n_in = len(comm.ins)

    def body(*refs):
        ins, outs, sems = refs[:n_in], refs[n_in:-2], (*refs[-2:], 0)
        comm.start(ins, outs, sems)
        comm.finish(ins, outs, sems)

    res = pl.pallas_call(
        body, name=name, out_shape=comm.out_shapes, in_specs=_any_specs(n_in),
        out_specs=_any_specs(len(comm.out_shapes)), input_output_aliases=comm.aliases,
        scratch_shapes=comm.scratch())(*comm.ins)
    return comm.deliver(list(res))


def _hosted_call(body, *, name, grid, in_specs, out_specs, out_shape, args, scratch_shapes=(), dims=None, comm=None):
    in_specs, out_specs, out_shape = list(in_specs), list(out_specs), list(out_shape)
    if comm is None:
        return pl.pallas_call(
            body, name=name, grid=grid, in_specs=in_specs, out_specs=out_specs, out_shape=out_shape,
            scratch_shapes=list(scratch_shapes),
            compiler_params=_params(dims or ("arbitrary",) * len(grid)))(*args)
    n_in, n_out, n_ci, n_co = len(in_specs), len(out_specs), len(comm.ins), len(comm.out_shapes)

    def wrapped(*refs):
        ins, cins = refs[:n_in], refs[n_in:n_in + n_ci]
        outs = refs[n_in + n_ci:n_in + n_ci + n_out]
        couts = refs[n_in + n_ci + n_out:n_in + n_ci + n_out + n_co]
        scratch, sems = refs[n_in + n_ci + n_out + n_co:-2], (*refs[-2:], 0)
        first = functools.reduce(jnp.logical_and, [pl.program_id(k) == 0 for k in range(len(grid))])
        last = functools.reduce(jnp.logical_and, [pl.program_id(k) == grid[k] - 1 for k in range(len(grid))])

        @pl.when(first)
        def _():
            comm.start(cins, couts, sems)

        body(*ins, *outs, *scratch)

        @pl.when(last)
        def _():
            comm.finish(cins, couts, sems)

    res = pl.pallas_call(
        wrapped, name=name, grid=grid, in_specs=in_specs + _any_specs(n_ci), out_specs=out_specs + _any_specs(n_co),
        out_shape=out_shape + comm.out_shapes,
        input_output_aliases={n_in + i: n_out + o for i, o in comm.aliases.items()},
        scratch_shapes=list(scratch_shapes) + comm.scratch(),
        compiler_params=_params(("arbitrary",) * len(grid)))(*args, *comm.ins)
    comm.deliver(list(res[n_out:]))
    return list(res[:n_out])


def _gather_comm(bufs, deliver):
    n = len(bufs)
    geoms = [_Geom(*b.shape[1:]) for b in bufs]

    def first(outs, sems):
        x, y, c, chips = _place()
        cps = []
        for a in range(n):
            mine = geoms[a].half_ref(outs[a], (2 * x + y,), c)
            cps += [_remote(mine, mine, sems, 6 * a + j, (cx, cy, c)) for j, (cx, cy) in enumerate(chips)]
        return cps

    def start(ins, outs, sems):
        for cp in first(outs, sems):
            cp.start()

    def finish(ins, outs, sems):
        x, y, c, chips = _place()
        sib = (x, y, 1 - c)
        passed = []
        for a in range(n):
            for j, (cx, cy) in enumerate(chips):
                land = geoms[a].half_ref(outs[a], (2 * cx + cy,), c)
                _remote(land, land, sems, 6 * a + j, sib).wait_recv()
                cp = _remote(land, land, sems, 6 * a + 3 + j, sib)
                cp.start()
                passed.append(cp)
        for a in range(n):
            for j, (cx, cy) in enumerate(chips):
                land = geoms[a].half_ref(outs[a], (2 * cx + cy,), 1 - c)
                _remote(land, land, sems, 6 * a + 3 + j, sib).wait_recv()
        for cp in first(outs, sems) + passed:
            cp.wait_send()

    return _Comm(bufs, [jax.ShapeDtypeStruct(b.shape, b.dtype) for b in bufs], {a: a for a in range(n)}, 6 * n,
                 start, finish, deliver)


def _reduce_pair_comm(grads, deliver):
    n = len(grads)
    geoms = [_Geom(*a.shape[1:]) for a in grads]

    def copies(ins, outs, sems):
        x, y, c, _ = _place()
        return [_remote(geoms[a].half_ref(ins[a], (slice(None),), 1 - c), outs[a], sems, a, (x, y, 1 - c))
                for a in range(n)]

    def start(ins, outs, sems):
        for cp in copies(ins, outs, sems):
            cp.start()

    def finish(ins, outs, sems):
        cps = copies(ins, outs, sems)
        for cp in cps:
            cp.wait_recv()
        for cp in cps:
            cp.wait_send()

    shapes = [jax.ShapeDtypeStruct((N_SHARD, g.h0, g.h1), a.dtype) for a, g in zip(grads, geoms)]
    return _Comm(grads, shapes, {}, n, start, finish, deliver)


def _reduce_chips_comm(parts, deliver):
    n = len(parts)

    def copies(ins, outs, sems):
        x, y, c, chips = _place()
        return [_remote(ins[a].at[2 * cx + cy], outs[a].at[j], sems, 3 * a + j, (cx, cy, c))
                for a in range(n) for j, (cx, cy) in enumerate(chips)]

    def start(ins, outs, sems):
        for cp in copies(ins, outs, sems):
            cp.start()

    def finish(ins, outs, sems):
        cps = copies(ins, outs, sems)
        for cp in cps:
            cp.wait_recv()
        for cp in cps:
            cp.wait_send()

    shapes = [jax.ShapeDtypeStruct((3,) + a.shape[1:], a.dtype) for a in parts]
    return _Comm(parts, shapes, {}, 3 * n, start, finish, deliver)


def _share_pair_comm(bufs, deliver):
    n = len(bufs)
    geoms = [_Geom(*b.shape) for b in bufs]

    def copies(outs, sems):
        x, y, c, _ = _place()
        cps = []
        for a in range(n):
            mine = geoms[a].half_ref(outs[a], (), c)
            cps.append(_remote(mine, mine, sems, a, (x, y, 1 - c)))
        return cps

    def start(ins, outs, sems):
        for cp in copies(outs, sems):
            cp.start()

    def finish(ins, outs, sems):
        x, y, c, _ = _place()
        for a in range(n):
            land = geoms[a].half_ref(outs[a], (), 1 - c)
            _remote(land, land, sems, a, (x, y, 1 - c)).wait_recv()
        for cp in copies(outs, sems):
            cp.wait_send()

    return _Comm(bufs, [jax.ShapeDtypeStruct(b.shape, b.dtype) for b in bufs], {a: a for a in range(n)}, n,
                 start, finish, deliver)


def _small_comm(pack, deliver):
    r, d = pack.shape

    def copies(ins, outs, sems):
        x, y, c, _ = _place()
        cps = []
        for k in range(1, 8):
            peer = (x ^ ((k >> 2) & 1), y ^ ((k >> 1) & 1), c ^ (k & 1))
            cps.append(_remote(ins[0], outs[0].at[4 * x + 2 * y + c], sems, k - 1, peer))
        return cps

    def start(ins, outs, sems):
        for cp in copies(ins, outs, sems):
            cp.start()

    def finish(ins, outs, sems):
        cps = copies(ins, outs, sems)
        for cp in cps:
            cp.wait_recv()
        for cp in cps:
            cp.wait_send()

    return _Comm([pack], [jax.ShapeDtypeStruct((8, r, d), pack.dtype)], {}, 7, start, finish, deliver)


def _join_comms(comms):
    comms = [c for c in comms if c is not None]
    if len(comms) <= 1:
        return comms[0] if comms else None
    ins, out_shapes, aliases, spans, n_sems = [], [], {}, [], 0
    for c in comms:
        aliases.update({len(ins) + i: len(out_shapes) + o for i, o in c.aliases.items()})
        spans.append((len(ins), len(ins) + len(c.ins), len(out_shapes), len(out_shapes) + len(c.out_shapes), n_sems))
        ins += c.ins
        out_shapes += c.out_shapes
        n_sems += c.n_sems

    def run(which):
        def go(all_ins, all_outs, sems):
            for c, (i0, i1, o0, o1, base) in zip(comms, spans):
                getattr(c, which)(all_ins[i0:i1], all_outs[o0:o1], (sems[0], sems[1], sems[2] + base))
        return go

    def deliver(outs):
        for c, (_, _, o0, o1, _) in zip(comms, spans):
            c.deliver(outs[o0:o1])
        return outs

    return _Comm(ins, out_shapes, aliases, n_sems, run("start"), run("finish"), deliver)


def _ew_call(name, fn, ins, out_dtypes):
    shape = ins[0].shape
    cols = shape[-1]
    rows = 1
    for s_ in shape[:-1]:
        rows *= s_
    ins2 = [a.reshape(rows, cols) for a in ins]
    tr = rows
    for t in range(16, min(rows, max(16, (1 << 19) // cols)) + 1, 16):
        if rows % t == 0:
            tr = t
    no = len(out_dtypes)

    def body(*refs):
        outs = fn(*[r[...] for r in refs[:len(ins2)]])
        for ref, val in zip(refs[len(ins2):], outs):
            ref[...] = val.astype(ref.dtype)

    spec = pl.BlockSpec((tr, cols), lambda i: (i, 0))
    res = pl.pallas_call(
        body, name=name, grid=(rows // tr,), in_specs=[spec] * len(ins2), out_specs=[spec] * no,
        out_shape=[jax.ShapeDtypeStruct((rows, cols), dt) for dt in out_dtypes],
        compiler_params=_params(("parallel",)))(*ins2)
    return [r.reshape(shape) for r in res]


def _adamw_math(w, g, m, v):
    m = ADAM_B1 * m + (1.0 - ADAM_B1) * g
    v = ADAM_B2 * v + (1.0 - ADAM_B2) * (g * g)
    m_hat = m / (1.0 - ADAM_B1 ** ADAM_STEP)
    v_hat = v / (1.0 - ADAM_B2 ** ADAM_STEP)
    delta = -ADAM_LR * (m_hat / (jnp.sqrt(v_hat) + ADAM_EPS) + ADAM_WD * w)
    return delta, m, v


def _adamw_call(name, w, g, m, v):
    return _ew_call(name, _adamw_math, [w, g, m, v], [F32, F32, F32])


def _tiled_call(name, fn, place, grid, in_items, out_items):
    ni = len(in_items)

    def body(place_ref, *refs):
        vals = fn(*[r[...] for r in refs[:ni]])
        for ref, val in zip(refs[ni:], vals):
            ref[...] = val.astype(ref.dtype)

    spec = pltpu.PrefetchScalarGridSpec(
        num_scalar_prefetch=1, grid=grid,
        in_specs=[pl.BlockSpec(blk, imap) for _, blk, imap in in_items],
        out_specs=[pl.BlockSpec(blk, imap) for _, _, blk, imap in out_items])
    return pl.pallas_call(
        body, name=name, grid_spec=spec,
        out_shape=[jax.ShapeDtypeStruct(shp, dt) for shp, dt, _, _ in out_items],
        compiler_params=_params(("arbitrary",) * len(grid)))(place, *[a for a, _, _ in in_items])


def _cast_call(name, place, shards):
    n0, n1 = shards[0].shape
    tr = _fit_rows(n0, n1, 1.5 * len(shards))
    ins = [(a, (tr, n1), lambda i, p: (i, 0)) for a in shards]
    outs = [((N_SHARD, n0, n1), BF16, (1, tr, n1), lambda i, p: (p[0], i, 0)) for _ in shards]
    return _tiled_call(name, lambda *v: [x[None] for x in v], place, (n0 // tr,), ins, outs)


def _pair_sum_call(name, place, fulls, gots):
    k = len(fulls)
    g = _Geom(*fulls[0].shape[1:], blocks=2.5 * k)
    blk = (1, g.tr, g.h1)
    ins = [(a, blk, lambda s, i, p: g.half_block((s,), i, p[1])) for a in fulls]
    ins += [(a, blk, lambda s, i, p: (s, i, 0)) for a in gots]
    outs = [((N_SHARD, g.h0, g.h1), BF16, blk, lambda s, i, p: (s, i, 0)) for _ in fulls]
    return _tiled_call(name, lambda *v: [v[j] + v[k + j] for j in range(k)], place, (N_SHARD, g.nblk), ins, outs)


def _chip_sum_call(name, place, fulls, gots, recvs):
    k = len(fulls)
    g = _Geom(*fulls[0].shape[1:], blocks=4.5 * k)
    blk = (1, g.tr, g.h1)
    ins = [(a, blk, lambda i, p: g.half_block((p[0],), i, p[1])) for a in fulls]
    ins += [(a, blk, lambda i, p: (p[0], i, 0)) for a in gots]
    ins += [(a, (3, g.tr, g.h1), lambda i, p: (0, i, 0)) for a in recvs]
    outs = [((g.n0, g.n1), F32, (g.tr, g.h1), lambda i, p: g.half_block((), i, p[1])) for _ in fulls]

    def fn(*v):
        res = []
        for j in range(k):
            r = v[2 * k + j].astype(F32)
            res.append(v[j][0] + v[k + j][0] + r[0] + r[1] + r[2])
        return res

    return _tiled_call(name, fn, place, (g.nblk,), ins, outs)


def _adamw_group_call(name, ws, gs, ms, vs, comm=None):
    k = len(ws)
    n0, n1 = ws[0].shape
    tr = _fit_rows(n0, n1, 8 * k)
    spec = pl.BlockSpec((tr, n1), lambda i: (i, 0))

    def body(*refs):
        for j in range(k):
            g = refs[k + j][...]
            delta, m, vv = _adamw_math(refs[j][...], g, refs[2 * k + j][...], refs[3 * k + j][...])
            for ref, val in zip(refs[4 * k + 4 * j:4 * k + 4 * j + 4], (g, delta, m, vv)):
                ref[...] = val

    flat = _hosted_call(
        body, name=name, grid=(n0 // tr,), in_specs=[spec] * (4 * k), out_specs=[spec] * (4 * k),
        out_shape=[jax.ShapeDtypeStruct((n0, n1), F32)] * (4 * k), dims=("parallel",),
        args=(*ws, *gs, *ms, *vs), comm=comm)
    return [flat[4 * j:4 * j + 4] for j in range(k)]


def _small_update_call(me, early, own_early, late, own_late, wp, mp, vp):
    nd, r, d = early.shape

    def body(me_ref, e_ref, oe_ref, l_ref, ol_ref, w_ref, m_ref, v_ref, gs_ref, d_ref, nm_ref, nv_ref):
        mine = me_ref[0]

        def total(g_ref, own_ref):
            acc = None
            for k in range(nd):
                part = jnp.where(mine == k, own_ref[...], g_ref[k])
                acc = part if acc is None else acc + part
            return acc

        gs = total(e_ref, oe_ref)
        ls = total(l_ref, ol_ref)
        gs_ref[...] = gs
        first = gs[0:8] + ls[0:8]
        gs_ref[0:8, :] = first
        gs_ref[ROW_META:ROW_META + N_META, :] = gs[ROW_META:ROW_META + N_META] + ls[8:8 + N_META]
        grads = jnp.concatenate([first, gs[8:SMALL_ADAM_ROWS]], axis=0)
        delta, m, v = _adamw_math(w_ref[...], grads, m_ref[...], v_ref[...])
        d_ref[...] = delta
        nm_ref[...] = m
        nv_ref[...] = v

    vm = pl.BlockSpec(memory_space=pltpu.VMEM)
    ashape = jax.ShapeDtypeStruct((SMALL_ADAM_ROWS, d), F32)
    return pl.pallas_call(
        body, name="small_update", in_specs=[pl.BlockSpec(memory_space=pltpu.SMEM)] + [vm] * 7, out_specs=[vm] * 4,
        out_shape=[jax.ShapeDtypeStruct((r, d), F32), ashape, ashape, ashape],
        compiler_params=pltpu.CompilerParams(vmem_limit_bytes=VMEM_LIMIT_BYTES))(
            me, early, own_early, late, own_late, wp, mp, vp)


SMALL_NAMES = ["ffn1_norm", "mix_norm", "ffn2_norm", "final_norm", "q_latent_norm", "kv_latent_norm",
               "q_head_norm", "k_head_norm", "conv_b", "gate_a_b", "gate_x_b", "lru_lambda", "attn_out_norm",
               "lru_out_norm"]
ROW_CONV_W = 14
ROW_GATE_A = 16
ROW_GATE_X = 48
ROW_META = 80
ROW_LOSS = 96


def _row(a):
    flat = a.reshape(1, -1)
    return jnp.pad(flat, ((0, 0), (0, D_MODEL - flat.shape[1])))


def _pack_small(t, rows):
    parts = [_row(t[nm]) for nm in SMALL_NAMES]
    parts.append(t["conv_w"].reshape(2, D_MODEL))
    parts.append(t["gate_a_w"].reshape(32, D_MODEL))
    parts.append(t["gate_x_w"].reshape(32, D_MODEL))
    p = jnp.concatenate(parts, axis=0)
    return jnp.pad(p, ((0, rows - p.shape[0]), (0, 0)))


def _early_pack(g):
    gs = {nm: g.get(nm, jnp.zeros((1, D_MODEL), F32)) for nm in SMALL_NAMES}
    gs["q_head_norm"] = g["q_head_norm"][:, 0:D_QK]
    gs["k_head_norm"] = g["k_head_norm"][:, 0:D_QK]
    for nm in ("conv_b", "gate_a_b", "gate_x_b", "lru_lambda"):
        gs[nm] = g[nm].reshape(1, LRU_W)
    gs["conv_w"] = g["conv_w"].transpose(1, 0, 2).reshape(CONV_K, LRU_W)
    gs["gate_a_w"] = _gate_blocks(g["gate_a_w"])
    gs["gate_x_w"] = _gate_blocks(g["gate_x_w"])
    return jnp.concatenate([_pack_small(gs, ROW_META), jnp.zeros((N_META, D_MODEL), F32), _row(g["loss"][:, 0:1]),
                            jnp.zeros((SMALL_ROWS - ROW_LOSS - 1, D_MODEL), F32)], axis=0)


def _unpack_small(p, like):
    out = {}
    for k, nm in enumerate(SMALL_NAMES):
        out[nm] = p[k, 0:like[nm].size].reshape(like[nm].shape)
    out["gate_a_w"] = p[ROW_GATE_A:ROW_GATE_A + 32].reshape(like["gate_a_w"].shape)
    out["gate_x_w"] = p[ROW_GATE_X:ROW_GATE_X + 32].reshape(like["gate_x_w"].shape)
    return out


def _gate_dense(wg):
    w4 = wg[0].reshape(N_LRU_TILES, 2, 64, 64)
    zero = jnp.zeros((N_LRU_TILES, 64, 64), wg.dtype)
    top = jnp.concatenate([w4[:, 0], zero], axis=2)
    bot = jnp.concatenate([zero, w4[:, 1]], axis=2)
    return jnp.concatenate([top, bot], axis=1).astype(BF16)


def _gate_blocks(dw):
    return jnp.stack([dw[:, 0:64, 0:64], dw[:, 64:128, 64:128]], axis=1).reshape(8, 64, 64)


BIG_NAMES = ["ffn1_w_gate", "ffn1_w_up", "ffn1_w_down", "w_in", "w_uq", "w_uk", "w_uv", "w_out", "ffn2_w_gate",
             "ffn2_w_up", "ffn2_w_down"]
BIG_GROUPS = [["ffn1_w_gate", "ffn1_w_up", "ffn1_w_down", "ffn2_w_gate", "ffn2_w_up", "ffn2_w_down"], ["w_in"],
              ["w_uq"], ["w_uk", "w_uv"], ["w_out"]]
TRANSPOSED = ("ffn1_w_gate", "ffn1_w_up", "ffn2_w_gate", "ffn2_w_up", "w_in", "w_uq")


def _to2d(nm, a):
    return a[0].T if nm in TRANSPOSED else a[0]


def _from2d(nm, a):
    return (a.T if nm in TRANSPOSED else a)[None]


WEIGHT_NAMES = ["meta_tokens", "ffn1_norm", "ffn1_w_gate", "ffn1_w_up", "ffn1_w_down", "mix_norm", "w_in",
                "q_latent_norm", "w_uq", "kv_latent_norm", "w_uk", "w_uv", "q_head_norm", "k_head_norm", "conv_w",
                "conv_b", "gate_a_w", "gate_a_b", "gate_x_w", "gate_x_b", "lru_lambda", "attn_out_norm",
                "lru_out_norm", "w_out", "ffn2_norm", "ffn2_w_gate", "ffn2_w_up", "ffn2_w_down", "final_norm"]


def _weight_from(nm, slots):
    if nm == "w_in":
        win = slots.reshape(IN_WIDTH, D_MODEL)
        return jnp.concatenate([win[0:Z_KR + D_ROPE], jnp.zeros((128 - D_ROPE, D_MODEL), BF16),
                                win[Z_KR + D_ROPE:]], axis=0)
    if nm == "w_uq":
        return jnp.pad(slots, ((0, 0), (0, D_QKP - D_QK), (0, 0))).reshape(HEADS * D_QKP, Q_RANK)
    if nm in ("w_uk", "w_uv"):
        return slots.transpose(1, 0, 2).reshape(KV_RANK, HEADS * D_NOPE)
    if nm == "w_out":
        return slots.reshape(D_MODEL, D_MODEL)
    return slots


def _small_weights(p, small):
    w = {nm: p[nm] for nm in SMALL_NAMES}
    w["q_head_norm"] = jnp.pad(p["q_head_norm"], ((0, 0), (0, D_QKP - D_QK)))
    w["k_head_norm"] = jnp.pad(p["k_head_norm"], ((0, 0), (0, D_QKP - D_QK)))
    w["conv_w"] = small[:, N_META:N_META + 2, :].reshape(N_SHARD, CONV_K, LRU_TILE)
    w["gate_a_w"] = _gate_dense(p["gate_a_w"])
    w["gate_x_w"] = _gate_dense(p["gate_x_w"])
    meta = small[:, 0:N_META, :].transpose(1, 0, 2).reshape(N_META, D_MODEL)
    return w, meta


def _full_weights(p, gathered, small):
    w, meta = _small_weights(p, small)
    w.update({nm: _weight_from(nm, gathered[nm]) for nm in BIG_NAMES})
    return w, meta


def _shard_grad(nm, g):
    if nm == "w_in":
        return jnp.concatenate([g[0:Z_KR + D_ROPE], g[Z_MLA:]], axis=0).reshape(N_SHARD, IN_WIDTH // N_SHARD, D_MODEL)
    if nm == "w_uq":
        return g.reshape(HEADS, D_QKP, Q_RANK)[:, 0:D_QK, :]
    if nm in ("w_uk", "w_uv"):
        return g.reshape(KV_RANK, HEADS, D_NOPE).transpose(1, 0, 2)
    if nm == "w_out":
        return g.reshape(N_SHARD, D_MODEL // N_SHARD, D_MODEL)
    return g


def _shard_grads(g):
    return {nm: _shard_grad(nm, g[nm]) for nm in BIG_NAMES}


GATHER_FIRST = ["ffn1_w_gate"]
GATHER_AT = {"ffn1_gate": ["ffn1_w_up"], "ffn1_upact": ["ffn1_w_down"],
             "ffn1_down": ["w_in", "w_uq", "w_uk", "w_uv", "w_out"], "attn_fwd": ["ffn2_w_down"],
             "lru_fwd": ["ffn2_w_gate", "ffn2_w_up"]}
REDUCE_PLAN = [(["ffn2_w_gate", "ffn2_w_up", "ffn2_w_down"], "ffn2_din", "lru_bwd"),
               (["w_out", "w_uq", "w_uk", "w_uv", "w_in"], "mix_din", "ffn1_dact"),
               (["ffn1_w_down"], "ffn1_dwg", "ffn1_dwu"),
               (["ffn1_w_gate"], "ffn1_dwu", "ffn1_din"),
               (["ffn1_w_up"], "ffn1_din", None)]
SMALL_EARLY_AT = "ffn1_dact"
LATE_ROWS = 24


def _same_shape_groups(names):
    return [[nm for nm in grp if nm in names] for grp in BIG_GROUPS if any(nm in names for nm in grp)]


class _Sched:
    def __init__(self, place, w, slots):
        self.place, self.w, self.slots = place, w, slots
        self.g = None
        self.sharded, self.from_pair, self.chip_bf16, self.from_chips = {}, {}, {}, {}
        self.early = self.early_all = None

    def host(self, stage):
        comms = []
        if stage in GATHER_AT:
            comms.append(self.gather(GATHER_AT[stage]))
        for names, pair_at, chips_at in REDUCE_PLAN:
            if stage == chips_at:
                comms.append(self.chips(names))
            if stage == pair_at:
                comms.append(self.pair(names))
        if stage == SMALL_EARLY_AT:
            comms.append(self.small_early())
        return _join_comms(comms)

    def small_early(self):
        self.early = _early_pack(self.g)

        def deliver(outs):
            self.early_all = outs[0]
            return outs

        return _small_comm(self.early, deliver)

    def gather(self, names):
        def deliver(outs):
            self.w.update({nm: _weight_from(nm, o) for nm, o in zip(names, outs)})
            return outs

        return _gather_comm([self.slots[nm] for nm in names], deliver)

    def pair(self, names):
        self.sharded.update({nm: _shard_grad(nm, self.g[nm]) for nm in names})

        def deliver(outs):
            self.from_pair.update(zip(names, outs))
            for grp in _same_shape_groups(names):
                sums = _pair_sum_call("pair_sum_" + grp[0], self.place, [self.sharded[nm] for nm in grp],
                                      [self.from_pair[nm] for nm in grp])
                self.chip_bf16.update(zip(grp, sums))
            return outs

        return _reduce_pair_comm([self.sharded[nm] for nm in names], deliver)

    def chips(self, names):
        def deliver(outs):
            self.from_chips.update(zip(names, outs))
            return outs

        return _reduce_chips_comm([self.chip_bf16[nm] for nm in names], deliver)

    def chip_sums(self, names):
        out = {}
        for grp in _same_shape_groups(names):
            sums = _chip_sum_call("chip_sum_" + grp[0], self.place, [self.sharded[nm] for nm in grp],
                                  [self.from_pair[nm] for nm in grp], [self.from_chips[nm] for nm in grp])
            out.update(zip(grp, sums))
        return out


def kernel(x, meta_tokens, ffn1_norm, ffn1_w_gate, ffn1_w_up, ffn1_w_down, mix_norm, w_in, q_latent_norm, w_uq, kv_latent_norm, w_uk, w_uv, q_head_norm, k_head_norm, conv_w, conv_b, gate_a_w, gate_a_b, gate_x_w, gate_x_b, lru_lambda, attn_out_norm, lru_out_norm, w_out, ffn2_norm, ffn2_w_gate, ffn2_w_up, ffn2_w_down, final_norm, loss_target, m_meta_tokens, m_ffn1_norm, m_ffn1_w_gate, m_ffn1_w_up, m_ffn1_w_down, m_mix_norm, m_w_in, m_q_latent_norm, m_w_uq, m_kv_latent_norm, m_w_uk, m_w_uv, m_q_head_norm, m_k_head_norm, m_conv_w, m_conv_b, m_gate_a_w, m_gate_a_b, m_gate_x_w, m_gate_x_b, m_lru_lambda, m_attn_out_norm, m_lru_out_norm, m_w_out, m_ffn2_norm, m_ffn2_w_gate, m_ffn2_w_up, m_ffn2_w_down, m_final_norm, v_meta_tokens, v_ffn1_norm, v_ffn1_w_gate, v_ffn1_w_up, v_ffn1_w_down, v_mix_norm, v_w_in, v_q_latent_norm, v_w_uq, v_kv_latent_norm, v_w_uk, v_w_uv, v_q_head_norm, v_k_head_norm, v_conv_w, v_conv_b, v_gate_a_w, v_gate_a_b, v_gate_x_w, v_gate_x_b, v_lru_lambda, v_attn_out_norm, v_lru_out_norm, v_w_out, v_ffn2_norm, v_ffn2_w_gate, v_ffn2_w_up, v_ffn2_w_down, v_final_norm):
    args = locals()
    p = {nm: args[nm] for nm in WEIGHT_NAMES}
    mom = {nm: args["m_" + nm] for nm in WEIGHT_NAMES}
    var = {nm: args["v_" + nm] for nm in WEIGHT_NAMES}
    nb, seq, d = x.shape
    lp = CHUNK + seq
    xi, yi, ci = lax.axis_index("x"), lax.axis_index("y"), lax.axis_index("c")
    chip = 2 * xi + yi

    place = jnp.stack([chip, ci]).astype(jnp.int32)
    p2 = {nm: _to2d(nm, p[nm]) for nm in BIG_NAMES}
    m2 = {nm: _to2d(nm, mom[nm]) for nm in BIG_NAMES}
    v2 = {nm: _to2d(nm, var[nm]) for nm in BIG_NAMES}

    slots = {}
    for grp in BIG_GROUPS:
        for nm, buf in zip(grp, _cast_call("cast_" + grp[0], place, [p2[nm] for nm in grp])):
            slots[nm] = buf
    small_shard = jnp.concatenate(
        [meta_tokens, conv_w[0].reshape(2, 2 * LRU_TILE), jnp.zeros((14, 2 * LRU_TILE), F32)], axis=0)
    small_slots = lax.dynamic_update_slice(jnp.zeros((N_SHARD,) + small_shard.shape, F32), small_shard[None],
                                           (chip, 0, 0))
    first = _comm_call("gather_first", _gather_comm([slots[nm] for nm in GATHER_FIRST] + [small_slots], lambda o: o))
    w, meta = _small_weights(p, first[-1])
    w.update({nm: _weight_from(nm, o) for nm, o in zip(GATHER_FIRST, first[:-1])})
    sched = _Sched(place, w, slots)

    h0 = jnp.concatenate(
        [jnp.zeros((nb, PAD_ROWS, d), F32), jnp.broadcast_to(meta[None], (nb, N_META, d)), x], axis=1)
    target = jnp.pad(loss_target, ((0, 0), (CHUNK, 0), (0, 0)))
    loss_part, dh0, g = _local_step(h0.reshape(nb * lp, d), target.reshape(nb * lp, d), w, nb, lp, sched)
    dh0 = dh0.reshape(nb, lp, d)
    grad_x = dh0[:, CHUNK:, :]

    late = jnp.concatenate([g["ffn1_norm"], jnp.zeros((7, D_MODEL), F32), jnp.sum(dh0[:, PAD_ROWS:CHUNK, :], axis=0)],
                           axis=0)
    late_all = _comm_call("gather_late", _small_comm(late, lambda o: o))[0]
    small_like = {nm: p[nm] for nm in SMALL_NAMES + ["gate_a_w", "gate_x_w"]}

    def pack_w(t):
        tt = {nm: t[nm] for nm in SMALL_NAMES + ["gate_a_w", "gate_x_w"]}
        tt["conv_w"] = jnp.zeros((CONV_K, LRU_W), F32)
        return _pack_small(tt, SMALL_ADAM_ROWS)

    me = (4 * xi + 2 * yi + ci).astype(jnp.int32).reshape(1)
    gsum, dsm, msm, vsm = _small_update_call(me, sched.early_all, sched.early, late_all, late, pack_w(p),
                                             pack_w(mom), pack_w(var))
    grads = _unpack_small(gsum, small_like)
    delta = _unpack_small(dsm, small_like)
    new_m = _unpack_small(msm, small_like)
    new_v = _unpack_small(vsm, small_like)
    loss = gsum[ROW_LOSS, 0]
    gmeta = gsum[ROW_META:ROW_META + N_META].reshape(N_META, N_SHARD, D_MODEL // N_SHARD)
    grads["meta_tokens"] = lax.dynamic_index_in_dim(gmeta, chip, axis=1, keepdims=False)
    gconv = gsum[ROW_CONV_W:ROW_CONV_W + 2].reshape(CONV_K, N_SHARD, LRU_TILE)
    grads["conv_w"] = lax.dynamic_index_in_dim(gconv, chip, axis=1, keepdims=False)[None]
    for nm in ("meta_tokens", "conv_w"):
        delta[nm], new_m[nm], new_v[nm] = _adamw_call("adamw_" + nm, p[nm], grads[nm], mom[nm], var[nm])

    done = [nm for names, _, chips_at in REDUCE_PLAN if chips_at is not None for nm in names]
    last = [nm for names, _, chips_at in REDUCE_PLAN if chips_at is None for nm in names]
    mine = sched.chip_sums(done)
    shard_grads = dict(zip(done, _comm_call("share_pair", _share_pair_comm([mine[nm] for nm in done], lambda o: o))))

    def adamw(names, comm=None):
        res = _adamw_group_call("adamw_" + names[0], [p2[nm] for nm in names], [shard_grads[nm] for nm in names],
                                [m2[nm] for nm in names], [v2[nm] for nm in names], comm)
        for nm, (gg, dd, mm, vv) in zip(names, res):
            grads[nm], delta[nm], new_m[nm], new_v[nm] = (_from2d(nm, t) for t in (gg, dd, mm, vv))

    groups = _same_shape_groups(done)
    adamw(groups[0], sched.chips(last))
    for grp in groups[1:]:
        adamw(grp)
    mine = sched.chip_sums(last)
    shard_grads = dict(zip(last, _comm_call("share_last", _share_pair_comm([mine[nm] for nm in last], lambda o: o))))
    adamw(last)

    return (loss, grad_x, *[grads[nm] for nm in WEIGHT_NAMES], *[delta[nm] for nm in WEIGHT_NAMES],
            *[new_m[nm] for nm in WEIGHT_NAMES], *[new_v[nm] for nm in WEIGHT_NAMES])
```

```python
import functools
import math

import jax
import jax.numpy as jnp
import numpy as np
from jax import lax
from jax.experimental import pallas as pl
from jax.experimental.pallas import tpu as pltpu

F32 = jnp.float32
BF16 = jnp.bfloat16
MESH = pl.DeviceIdType.MESH

D_MODEL = 1024
N_META = 16
CHUNK = 64
PAD_ROWS = CHUNK - N_META
HEADS = 4
D_NOPE = 128
D_ROPE = 64
D_QK = D_NOPE + D_ROPE
D_QKP = 256
D_V = 128
KV_RANK = 256
Q_RANK = 384
MLA_W = HEADS * D_V
LRU_W = 512
LRU_TILE = 128
N_LRU_TILES = LRU_W // LRU_TILE
CONV_K = 4
C_RGLRU = 8.0
ROPE_THETA = 10000.0
D_FF = 2816
N_SHARD = 4
EPS = 1e-6
NEG_INF = -1e30
Z_KR = Q_RANK + KV_RANK
Z_MLA = Z_KR + 128
Z_U = Z_MLA
Z_G = Z_U + LRU_W
Z_W = Z_G + LRU_W
IN_WIDTH = Q_RANK + KV_RANK + D_ROPE + 2 * LRU_W

ADAM_LR = 0.001
ADAM_B1 = 0.9
ADAM_B2 = 0.999
ADAM_EPS = 1e-08
ADAM_WD = 0.01
ADAM_STEP = 10

VMEM_LIMIT_BYTES = 56 * 1024 * 1024
SMALL_ROWS = 104
SMALL_ADAM_ROWS = 80


def _params(sem):
    return pltpu.CompilerParams(dimension_semantics=sem, vmem_limit_bytes=VMEM_LIMIT_BYTES)


def _row_tile(rows, target):
    best = 16
    for t in range(16, min(rows, target) + 1, 16):
        if rows % t == 0:
            best = t
    return best


def _col_tile(cols, target):
    best = cols
    for t in range(128, min(cols, target) + 1, 128):
        if cols % t == 0:
            best = t
    return best


def _dot(a, b):
    return jnp.dot(a, b, preferred_element_type=F32)


def _dot_nt(a, b):
    return lax.dot_general(a, b, (((1,), (1,)), ((), ())), preferred_element_type=F32)


def _dot_tn(a, b):
    return lax.dot_general(a, b, (((0,), (0,)), ((), ())), preferred_element_type=F32)


def _rms(x, n):
    return lax.rsqrt(jnp.sum(x * x, axis=-1, keepdims=True) * (1.0 / n) + EPS)


def _rms_bwd(dn, nrm, r, n):
    return r * (dn - nrm * (jnp.sum(dn * nrm, axis=-1, keepdims=True) * (1.0 / n)))


def _gelu(x):
    k = math.sqrt(2.0 / math.pi)
    t = jnp.tanh(k * (x + 0.044715 * x * x * x))
    return 0.5 * x * (1.0 + t), t


def _gelu_grad(x, t):
    k = math.sqrt(2.0 / math.pi)
    return 0.5 * (1.0 + t) + 0.5 * x * (1.0 - t * t) * k * (1.0 + 3.0 * 0.044715 * x * x)


def _sigmoid(x):
    return 0.5 + 0.5 * jnp.tanh(0.5 * x)


def _softplus_neg(lam):
    e = jnp.exp(-jnp.abs(lam))
    log1p = jnp.where(e < 0.01, e * (1.0 - e * (0.5 - e * (1.0 / 3 - e * 0.25))), jnp.log(1.0 + e))
    return jnp.maximum(-lam, 0.0) + log1p


def _rope(t, c, s1, s2):
    return t * c + pltpu.roll(t, 96, 1) * s1 + pltpu.roll(t, 32, 1) * s2


def _rope_t(d, c, s1, s2):
    return d * c + pltpu.roll(d * s1, 32, 1) + pltpu.roll(d * s2, 96, 1)


def _rope_tables(lp):
    pos = (np.arange(lp, dtype=np.int32) - PAD_ROWS).astype(np.float32)
    inv_freq = (ROPE_THETA ** (-np.arange(0, D_ROPE // 2, dtype=np.float32) / (D_ROPE // 2))).astype(np.float32)
    ang = (pos[:, None] * inv_freq[None, :]).astype(np.float32).astype(np.float64)
    cos, sin = np.cos(ang).astype(np.float32), np.sin(ang).astype(np.float32)
    z = np.zeros_like(cos)
    return (jnp.asarray(np.concatenate([cos, cos, z, z], 1)), jnp.asarray(np.concatenate([-sin, z, z, z], 1)),
            jnp.asarray(np.concatenate([z, sin, z, z], 1)))


def _rmsnorm_call(name, h, g, tm):
    rows, d = h.shape

    def body(h_ref, g_ref, o_ref):
        x = h_ref[...]
        o_ref[...] = (x * _rms(x, d) * g_ref[...]).astype(BF16)

    return pl.pallas_call(
        body, name=name, grid=(rows // tm,),
        in_specs=[pl.BlockSpec((tm, d), lambda i: (i, 0)), pl.BlockSpec((1, d), lambda i: (0, 0))],
        out_specs=pl.BlockSpec((tm, d), lambda i: (i, 0)),
        out_shape=jax.ShapeDtypeStruct((rows, d), BF16),
        compiler_params=_params(("parallel",)))(h, g)


def _ffn_up_call(name, u, wg, wu, tm, comm=None):
    rows, d = u.shape
    ns, fs, _ = wg.shape

    def body(u_ref, wg_ref, wu_ref, g_ref, p_ref, a_ref):
        uu = u_ref[...]
        g = _dot_nt(uu, wg_ref[0])
        p = _dot_nt(uu, wu_ref[0])
        g_ref[0] = g.astype(BF16)
        p_ref[0] = p.astype(BF16)
        a_ref[0] = (g * jax.nn.sigmoid(g) * p).astype(BF16)

    wspec = pl.BlockSpec((1, fs, d), lambda s, i: (s, 0, 0))
    ospec = pl.BlockSpec((1, tm, fs), lambda s, i: (s, i, 0))
    oshape = jax.ShapeDtypeStruct((ns, rows, fs), BF16)
    return _hosted_call(
        body, name=name, grid=(ns, rows // tm),
        in_specs=[pl.BlockSpec((tm, d), lambda s, i: (i, 0)), wspec, wspec],
        out_specs=[ospec, ospec, ospec], out_shape=[oshape, oshape, oshape],
        dims=("parallel", "parallel"), args=(u, wg, wu), comm=comm)


def _ffn_gate_call(name, u, wg, tm, comm=None):
    rows, d = u.shape
    ns, fs, _ = wg.shape

    def body(u_ref, wg_ref, g_ref):
        g_ref[0] = _dot_nt(u_ref[...], wg_ref[0]).astype(BF16)

    return _hosted_call(
        body, name=name, grid=(ns, rows // tm),
        in_specs=[pl.BlockSpec((tm, d), lambda s, i: (i, 0)), pl.BlockSpec((1, fs, d), lambda s, i: (s, 0, 0))],
        out_specs=[pl.BlockSpec((1, tm, fs), lambda s, i: (s, i, 0))],
        out_shape=[jax.ShapeDtypeStruct((ns, rows, fs), BF16)],
        dims=("parallel", "parallel"), args=(u, wg), comm=comm)[0]


def _ffn_upact_call(name, u, wu, gate, tm, comm=None):
    rows, d = u.shape
    ns, fs, _ = wu.shape

    def body(u_ref, wu_ref, g_ref, p_ref, a_ref):
        p = _dot_nt(u_ref[...], wu_ref[0])
        g = g_ref[0].astype(F32)
        p_ref[0] = p.astype(BF16)
        a_ref[0] = (g * jax.nn.sigmoid(g) * p).astype(BF16)

    ospec = pl.BlockSpec((1, tm, fs), lambda s, i: (s, i, 0))
    oshape = jax.ShapeDtypeStruct((ns, rows, fs), BF16)
    return _hosted_call(
        body, name=name, grid=(ns, rows // tm),
        in_specs=[pl.BlockSpec((tm, d), lambda s, i: (i, 0)), pl.BlockSpec((1, fs, d), lambda s, i: (s, 0, 0)), ospec],
        out_specs=[ospec, ospec], out_shape=[oshape, oshape],
        dims=("parallel", "parallel"), args=(u, wu, gate), comm=comm)


def _ffn_down_call(name, a, wd, h, tm, comm=None):
    rows, d = h.shape
    ns, _, fs = a.shape

    def body(a_ref, wd_ref, h_ref, o_ref):
        acc = h_ref[...]
        for s in range(ns):
            acc = acc + 0.5 * _dot(a_ref[s], wd_ref[s])
        o_ref[...] = acc

    return _hosted_call(
        body, name=name, grid=(rows // tm,),
        in_specs=[pl.BlockSpec((ns, tm, fs), lambda i: (0, i, 0)),
                  pl.BlockSpec((ns, fs, d), lambda i: (0, 0, 0)),
                  pl.BlockSpec((tm, d), lambda i: (i, 0))],
        out_specs=[pl.BlockSpec((tm, d), lambda i: (i, 0))],
        out_shape=[jax.ShapeDtypeStruct((rows, d), F32)],
        dims=("parallel",), args=(a, wd, h), comm=comm)[0]


def _mm_call(name, a, bt, tm, out_dtype):
    rows, k = a.shape
    n = bt.shape[0]

    def body(a_ref, b_ref, o_ref):
        o_ref[...] = _dot_nt(a_ref[...], b_ref[...]).astype(out_dtype)

    return pl.pallas_call(
        body, name=name, grid=(rows // tm,),
        in_specs=[pl.BlockSpec((tm, k), lambda i: (i, 0)), pl.BlockSpec((n, k), lambda i: (0, 0))],
        out_specs=pl.BlockSpec((tm, n), lambda i: (i, 0)),
        out_shape=jax.ShapeDtypeStruct((rows, n), out_dtype),
        compiler_params=_params(("parallel",)))(a, bt)


def _mla_heads(z, gql, gkvl, wuq, wuk, wuv):
    cq = z[:, 0:Q_RANK]
    ckv = z[:, Q_RANK:Z_KR]
    kr = z[:, Z_KR:Z_MLA]
    rq = _rms(cq, Q_RANK)
    nq = cq * rq
    cqn = (nq * gql).astype(BF16)
    rkv = _rms(ckv, KV_RANK)
    nkv = ckv * rkv
    ckvn = (nkv * gkvl).astype(BF16)
    qraw = _dot_nt(cqn, wuq)
    knope = _dot(ckvn, wuk)
    v = _dot(ckvn, wuv)
    skr = jnp.sum(kr * kr, axis=-1, keepdims=True)
    heads = []
    for hd in range(HEADS):
        qh = qraw[:, hd * D_QKP:(hd + 1) * D_QKP]
        rqh = lax.rsqrt(jnp.sum(qh * qh, axis=-1, keepdims=True) * (1.0 / D_QK) + EPS)
        kn = knope[:, hd * D_NOPE:(hd + 1) * D_NOPE]
        rkh = lax.rsqrt((jnp.sum(kn * kn, axis=-1, keepdims=True) + skr) * (1.0 / D_QK) + EPS)
        heads.append((qh * rqh, rqh, kn * rkh, kr * rkh, rkh))
    return dict(rq=rq, nq=nq, cqn=cqn, rkv=rkv, nkv=nkv, ckvn=ckvn, v=v, heads=heads)


def _mla_prep_call(z, gql, gkvl, gqh, gkh, wuq, wuk, wuv, tabs, lp, tm):
    rows = z.shape[0]
    tpe = lp // tm

    def body(z_ref, gql_ref, gkvl_ref, gqh_ref, gkh_ref, wuq_ref, wuk_ref, wuv_ref, c_ref, s1_ref, s2_ref,
             q_ref, k_ref, v_ref, cqn_ref, ckvn_ref):
        m = _mla_heads(z_ref[...], gql_ref[...], gkvl_ref[...], wuq_ref[...], wuk_ref[...], wuv_ref[...])
        c, s1, s2 = c_ref[...], s1_ref[...], s2_ref[...]
        gq, gk = gqh_ref[...], gkh_ref[...]
        row = (pl.program_id(0) % tpe) * tm + lax.broadcasted_iota(jnp.int32, (tm, 1), 0)
        spare = (lax.broadcasted_iota(jnp.int32, (1, D_QKP - D_NOPE), 1) == D_ROPE).astype(F32)
        kmask = jnp.where(row < PAD_ROWS, NEG_INF * math.sqrt(D_QK), 0.0) * spare
        for hd in range(HEADS):
            qn, _, knn, krn, _ = m["heads"][hd]
            qg = qn * gq
            q_ref[hd, :, 0:D_NOPE] = qg[:, 0:D_NOPE].astype(BF16)
            q_ref[hd, :, D_NOPE:D_QKP] = (_rope(qg[:, D_NOPE:D_QKP], c, s1, s2) + spare).astype(BF16)
            k_ref[hd, :, 0:D_NOPE] = (knn * gk[:, 0:D_NOPE]).astype(BF16)
            k_ref[hd, :, D_NOPE:D_QKP] = (_rope(krn * gk[:, D_NOPE:D_QKP], c, s1, s2) + kmask).astype(BF16)
            v_ref[hd] = m["v"][:, hd * D_V:(hd + 1) * D_V].astype(BF16)
        cqn_ref[...] = m["cqn"]
        ckvn_ref[...] = m["ckvn"]

    def const(shape):
        return pl.BlockSpec(shape, lambda i: tuple(0 for _ in shape))

    tab = pl.BlockSpec((tm, 128), lambda i: (i % tpe, 0))
    return pl.pallas_call(
        body, name="mla_prep", grid=(rows // tm,),
        in_specs=[pl.BlockSpec((tm, Z_MLA), lambda i: (i, 0)), const((1, Q_RANK)), const((1, KV_RANK)),
                  const((1, D_QKP)), const((1, D_QKP)), const((HEADS * D_QKP, Q_RANK)),
                  const((KV_RANK, HEADS * D_NOPE)), const((KV_RANK, HEADS * D_V)), tab, tab, tab],
        out_specs=[pl.BlockSpec((HEADS, tm, D_QKP), lambda i: (0, i, 0)),
                   pl.BlockSpec((HEADS, tm, D_QKP), lambda i: (0, i, 0)),
                   pl.BlockSpec((HEADS, tm, D_V), lambda i: (0, i, 0)),
                   pl.BlockSpec((tm, Q_RANK), lambda i: (i, 0)),
                   pl.BlockSpec((tm, KV_RANK), lambda i: (i, 0))],
        out_shape=[jax.ShapeDtypeStruct((HEADS, rows, D_QKP), BF16),
                   jax.ShapeDtypeStruct((HEADS, rows, D_QKP), BF16),
                   jax.ShapeDtypeStruct((HEADS, rows, D_V), BF16),
                   jax.ShapeDtypeStruct((rows, Q_RANK), BF16),
                   jax.ShapeDtypeStruct((rows, KV_RANK), BF16)],
        compiler_params=_params(("parallel",)))(z, gql, gkvl, gqh, gkh, wuq, wuk, wuv, *tabs)


Q_BLOCK_ROWS = 528


def _q_block(lp):
    return _row_tile(lp, Q_BLOCK_ROWS)


def _key_end(ext, lp):
    return min(lp, -(-ext // CHUNK) * CHUNK)


def _diag_bias(qb, j0, nk):
    shift = CHUNK.bit_length() - 1
    r = jnp.right_shift(j0 + lax.broadcasted_iota(jnp.int32, (qb, nk), 0), shift)
    c = jnp.right_shift(j0 + lax.broadcasted_iota(jnp.int32, (qb, nk), 1), shift)
    return jnp.where(c <= r, 0.0, NEG_INF)


def _attn_fwd_call(q, k, v, nb, lp, comm=None):
    rows = nb * lp
    qb = _q_block(lp)
    scale = 1.0 / math.sqrt(D_QK)

    def body(q_ref, k_ref, v_ref, o_ref, lse_ref):
        for j in range(lp // qb):
            j0, ext = j * qb, (j + 1) * qb
            kend = _key_end(ext, lp)
            qj = q_ref[0, j0:ext, :]
            sd = _dot_nt(qj, k_ref[0, j0:kend, :]) * scale + _diag_bias(qb, j0, kend - j0)
            mx = jnp.max(sd, axis=-1, keepdims=True)
            if j > 0:
                so = _dot_nt(qj, k_ref[0, 0:j0, :]) * scale
                mx = jnp.maximum(mx, jnp.max(so, axis=-1, keepdims=True))
            pd = jnp.exp(sd - mx)
            l = jnp.sum(pd, axis=-1, keepdims=True)
            o = _dot(pd.astype(BF16), v_ref[0, j0:kend, :])
            if j > 0:
                po = jnp.exp(so - mx)
                l = l + jnp.sum(po, axis=-1, keepdims=True)
                o = o + _dot(po.astype(BF16), v_ref[0, 0:j0, :])
            o_ref[j0:ext, :] = o / l
            lse_ref[0, j0:ext, :] = mx + jnp.log(l)

    return _hosted_call(
        body, name="attn_fwd", grid=(nb, HEADS),
        in_specs=[pl.BlockSpec((1, lp, D_QKP), lambda b, h: (h, b, 0)),
                  pl.BlockSpec((1, lp, D_QKP), lambda b, h: (h, b, 0)),
                  pl.BlockSpec((1, lp, D_V), lambda b, h: (h, b, 0))],
        out_specs=[pl.BlockSpec((lp, D_V), lambda b, h: (b, h)),
                   pl.BlockSpec((1, lp, 1), lambda b, h: (h, b, 0))],
        out_shape=[jax.ShapeDtypeStruct((rows, MLA_W), F32),
                   jax.ShapeDtypeStruct((HEADS, rows, 1), F32)],
        dims=("parallel", "parallel"), args=(q, k, v), comm=comm)


def _attn_bwd_call(q, k, v, o, lse, do, nb, lp):
    rows = nb * lp
    qb = _q_block(lp)
    scale = 1.0 / math.sqrt(D_QK)

    def body(q_ref, k_ref, v_ref, o_ref, lse_ref, do_ref, dq_ref, dk_ref, dv_ref, dk_acc, dv_acc):
        dk_acc[...] = jnp.zeros_like(dk_acc)
        dv_acc[...] = jnp.zeros_like(dv_acc)
        for j in range(lp // qb):
            j0, ext = j * qb, (j + 1) * qb
            kend = _key_end(ext, lp)
            dbias = _diag_bias(qb, j0, kend - j0)
            qj = q_ref[0, j0:ext, :]
            doj = do_ref[j0:ext, :]
            delta = jnp.sum(doj * o_ref[j0:ext, :], axis=-1, keepdims=True)
            dob = doj.astype(BF16)
            lse = lse_ref[0, j0:ext, :]
            dq = jnp.zeros((qb, D_QKP), F32)
            for lo, hi, bias in ((j0, kend, dbias), (0, j0, None)):
                if hi == lo:
                    continue
                kk = k_ref[0, lo:hi, :]
                s = _dot_nt(qj, kk) * scale
                p = jnp.exp((s if bias is None else s + bias) - lse)
                dv_acc[lo:hi, :] += _dot_tn(p.astype(BF16), dob)
                dp = _dot_nt(dob, v_ref[0, lo:hi, :])
                ds = (p * (dp - delta) * scale).astype(BF16)
                dq = dq + _dot(ds, kk)
                dk_acc[lo:hi, :] += _dot_tn(ds, qj)
            dq_ref[0, j0:ext, :] = dq.astype(BF16)
        dk_ref[0] = dk_acc[...].astype(BF16)
        dv_ref[0] = dv_acc[...].astype(BF16)

    qspec = pl.BlockSpec((1, lp, D_QKP), lambda b, h: (h, b, 0))
    vspec = pl.BlockSpec((1, lp, D_V), lambda b, h: (h, b, 0))
    ospec = pl.BlockSpec((lp, D_V), lambda b, h: (b, h))
    return pl.pallas_call(
        body, name="attn_bwd", grid=(nb, HEADS),
        in_specs=[qspec, qspec, vspec, ospec, pl.BlockSpec((1, lp, 1), lambda b, h: (h, b, 0)), ospec],
        out_specs=[qspec, qspec, vspec],
        out_shape=[jax.ShapeDtypeStruct((HEADS, rows, D_QKP), BF16),
                   jax.ShapeDtypeStruct((HEADS, rows, D_QKP), BF16),
                   jax.ShapeDtypeStruct((HEADS, rows, D_V), BF16)],
        scratch_shapes=[pltpu.VMEM((lp, D_QKP), F32), pltpu.VMEM((lp, D_V), F32)],
        compiler_params=_params(("parallel", "parallel")))(q, k, v, o, lse, do)


def _lru_gates(u, cw, cb, wa, ba, wx, bx, lam, lp):
    xc = (cw[3:4, :] * u + cw[2:3, :] * pltpu.roll(u, 1, 0) + cw[1:2, :] * pltpu.roll(u, 2, 0)
          + cw[0:1, :] * pltpu.roll(u, 3, 0) + cb)
    xcb = xc.astype(BF16)
    r = _sigmoid(_dot(xcb, wa) + ba)
    i = _sigmoid(_dot(xcb, wx) + bx)
    sp = _softplus_neg(lam)
    la = -C_RGLRU * r * sp
    a = jnp.exp(la)
    x2 = 2.0 * la
    e2 = a * a
    m2 = jnp.maximum(jnp.where(x2 > -0.01, -x2 * (1.0 + 0.5 * x2), 1.0 - e2), 1e-30)
    rs = lax.rsqrt(m2)
    row = lax.broadcasted_iota(jnp.int32, (lp, LRU_TILE), 0)
    first = row == PAD_ROWS
    valid = row >= PAD_ROWS
    mult_eff = jnp.where(first, 1.0, m2 * rs)
    return dict(xc=xc, xcb=xcb, r=r, i=i, sp=sp, a=a, e2=e2, rs=rs, mult_eff=mult_eff, first=first, valid=valid)


def _scan_rows(a, b, a_s, b_s, out_ref, lp, reverse):
    sub = lax.broadcasted_iota(jnp.int32, (lp, LRU_TILE), 0) & 7
    for dist in (1, 2, 4):
        shift = lp - dist if reverse else dist
        keep = (sub + dist <= 7) if reverse else (sub >= dist)
        a_sh = pltpu.roll(a, shift, 0)
        b_sh = pltpu.roll(b, shift, 0)
        b = jnp.where(keep, a * b_sh + b, b)
        a = jnp.where(keep, a * a_sh, a)
    a_s[...] = a
    b_s[...] = b
    n_groups = lp // 8
    edge = 0 if reverse else 7

    def group(gi, carry):
        r0 = pl.multiple_of(((n_groups - 1 - gi) if reverse else gi) * 8, 8)
        a8 = a_s[pl.ds(r0, 8), :]
        b8 = b_s[pl.ds(r0, 8), :]
        out_ref[pl.ds(r0, 8), :] = a8 * carry + b8
        return a8[edge:edge + 1, :] * carry + b8[edge:edge + 1, :]

    lax.fori_loop(0, n_groups, group, jnp.zeros((1, LRU_TILE), F32), unroll=4)


def _lru_specs(lp):
    seq = lambda col0: pl.BlockSpec((lp, LRU_TILE), lambda t, b: (b, col0 + t))
    cw = pl.BlockSpec((1, CONV_K, LRU_TILE), lambda t, b: (t, 0, 0))
    vec = pl.BlockSpec((1, LRU_TILE), lambda t, b: (0, t))
    mat = pl.BlockSpec((1, LRU_TILE, LRU_TILE), lambda t, b: (t, 0, 0))
    return seq, cw, vec, mat


def _lru_fwd_call(z, cw, cb, wa, ba, wx, bx, lam, nb, lp, comm=None):
    rows = nb * lp
    seq, cwspec, vec, mat = _lru_specs(lp)

    def body(u_ref, g_ref, cw_ref, cb_ref, wa_ref, ba_ref, wx_ref, bx_ref, lam_ref, y_ref, hs_ref, a_s, b_s):
        m = _lru_gates(u_ref[...], cw_ref[0], cb_ref[...], wa_ref[0], ba_ref[...], wx_ref[0], bx_ref[...],
                       lam_ref[...], lp)
        a = jnp.where(m["valid"], m["a"], 0.0)
        b = jnp.where(m["valid"], m["mult_eff"] * (m["i"] * m["xc"]), 0.0)
        _scan_rows(a, b, a_s, b_s, hs_ref, lp, reverse=False)
        gl, _ = _gelu(g_ref[...])
        y_ref[...] = hs_ref[...] * gl

    oshape = jax.ShapeDtypeStruct((rows, LRU_W), F32)
    return _hosted_call(
        body, name="lru_fwd", grid=(N_LRU_TILES, nb),
        in_specs=[seq(Z_U // LRU_TILE), seq(Z_G // LRU_TILE), cwspec, vec, mat, vec, mat, vec, vec],
        out_specs=[seq(0), seq(0)], out_shape=[oshape, oshape],
        scratch_shapes=[pltpu.VMEM((lp, LRU_TILE), F32), pltpu.VMEM((lp, LRU_TILE), F32)],
        dims=("parallel", "parallel"), args=(z, z, cw, cb, wa, ba, wx, bx, lam), comm=comm)


def _lru_bwd_call(z, hs, dy, cw, cb, wa, ba, wx, bx, lam, nb, lp, comm=None):
    rows = nb * lp
    seq, cwspec, vec, mat = _lru_specs(lp)

    def body(u_ref, g_ref, hs_ref, dy_ref, cw_ref, cb_ref, wa_ref, ba_ref, wx_ref, bx_ref, lam_ref,
             du_ref, dg_ref, dcw_ref, dcb_ref, dwa_ref, dba_ref, dwx_ref, dbx_ref, dlam_ref, a_s, b_s, d_s):
        b_idx = pl.program_id(1)
        u = u_ref[...]
        cw = cw_ref[0]
        wa, wx = wa_ref[0], wx_ref[0]
        lam = lam_ref[...]
        m = _lru_gates(u, cw, cb_ref[...], wa, ba_ref[...], wx, bx_ref[...], lam, lp)
        gate = g_ref[...]
        gl, th = _gelu(gate)
        dy = dy_ref[...]
        hs = hs_ref[...]
        dg_ref[...] = (dy * hs * _gelu_grad(gate, th)).astype(BF16)
        a_eff = jnp.where(m["valid"], m["a"], 0.0)
        _scan_rows(pltpu.roll(a_eff, lp - 1, 0), dy * gl, a_s, b_s, d_s, lp, reverse=True)
        ds = d_s[...]
        xc, r, i = m["xc"], m["r"], m["i"]
        row = lax.broadcasted_iota(jnp.int32, (lp, LRU_TILE), 0)
        da = ds * jnp.where(row >= 1, pltpu.roll(hs, 1, 0), 0.0)
        db = jnp.where(m["valid"], ds, 0.0)
        di = db * m["mult_eff"] * xc
        dxc = db * m["mult_eff"] * i
        live = m["valid"] & jnp.logical_not(m["first"])
        dm = jnp.where(live, db * i * xc, 0.0)
        dla = da * m["a"] - dm * (m["e2"] * m["rs"])
        dr = dla * (-C_RGLRU * m["sp"])
        dsp = jnp.sum(dla * (-C_RGLRU * r), axis=0, keepdims=True)
        dpr = (dr * r * (1.0 - r))
        dpi = (di * i * (1.0 - i))
        dprb, dpib = dpr.astype(BF16), dpi.astype(BF16)
        dxc = dxc + _dot_nt(dprb, wa) + _dot_nt(dpib, wx)
        du = (cw[3:4, :] * dxc + cw[2:3, :] * pltpu.roll(dxc, lp - 1, 0) + cw[1:2, :] * pltpu.roll(dxc, lp - 2, 0)
              + cw[0:1, :] * pltpu.roll(dxc, lp - 3, 0))
        du_ref[...] = jnp.where(m["valid"], du, 0.0).astype(BF16)
        tap = lax.broadcasted_iota(jnp.int32, (CONV_K, LRU_TILE), 0)
        dcw = jnp.zeros((CONV_K, LRU_TILE), F32)
        for kk in range(CONV_K):
            shifted = u if kk == CONV_K - 1 else pltpu.roll(u, CONV_K - 1 - kk, 0)
            dcw = jnp.where(tap == kk, jnp.sum(dxc * shifted, axis=0, keepdims=True), dcw)
        parts = [(dcw_ref, dcw[None]), (dcb_ref, jnp.sum(dxc, axis=0, keepdims=True)[None]),
                 (dwa_ref, _dot_tn(m["xcb"], dprb)[None]), (dba_ref, jnp.sum(dpr, axis=0, keepdims=True)[None]),
                 (dwx_ref, _dot_tn(m["xcb"], dpib)[None]), (dbx_ref, jnp.sum(dpi, axis=0, keepdims=True)[None]),
                 (dlam_ref, (dsp * (-jax.nn.sigmoid(-lam)))[None])]

        @pl.when(b_idx == 0)
        def _():
            for ref, val in parts:
                ref[...] = val

        @pl.when(b_idx != 0)
        def _():
            for ref, val in parts:
                ref[...] += val

    bshape = jax.ShapeDtypeStruct((rows, LRU_W), BF16)
    vec3 = pl.BlockSpec((1, 1, LRU_TILE), lambda t, b: (t, 0, 0))
    vshape = jax.ShapeDtypeStruct((N_LRU_TILES, 1, LRU_TILE), F32)
    mshape = jax.ShapeDtypeStruct((N_LRU_TILES, LRU_TILE, LRU_TILE), F32)
    return _hosted_call(
        body, name="lru_bwd", grid=(N_LRU_TILES, nb),
        in_specs=[seq(Z_U // LRU_TILE), seq(Z_G // LRU_TILE), seq(0), seq(0), cwspec, vec, mat, vec, mat, vec, vec],
        out_specs=[seq(0), seq(0), cwspec, vec3, mat, vec3, mat, vec3, vec3],
        out_shape=[bshape, bshape, jax.ShapeDtypeStruct((N_LRU_TILES, CONV_K, LRU_TILE), F32), vshape, mshape,
                   vshape, mshape, vshape, vshape],
        scratch_shapes=[pltpu.VMEM((lp, LRU_TILE), F32)] * 3,
        dims=("parallel", "arbitrary"), args=(z, z, hs, dy, cw, cb, wa, ba, wx, bx, lam), comm=comm)


def _mix_out_call(ya, yl, ga, gl, wout, h, tm):
    rows, d = h.shape

    def body(ya_ref, yl_ref, ga_ref, gl_ref, w_ref, h_ref, y_ref, o_ref):
        a = ya_ref[...]
        l = yl_ref[...]
        an = (a * _rms(a, MLA_W) * ga_ref[...]).astype(BF16)
        ln = (l * _rms(l, LRU_W) * gl_ref[...]).astype(BF16)
        y_ref[:, 0:MLA_W] = an
        y_ref[:, MLA_W:MLA_W + LRU_W] = ln
        o_ref[...] = h_ref[...] + _dot(an, w_ref[0:MLA_W, :]) + _dot(ln, w_ref[MLA_W:MLA_W + LRU_W, :])

    half = pl.BlockSpec((tm, MLA_W), lambda i: (i, 0))
    g = pl.BlockSpec((1, MLA_W), lambda i: (0, 0))
    full = pl.BlockSpec((tm, d), lambda i: (i, 0))
    return pl.pallas_call(
        body, name="mix_out", grid=(rows // tm,),
        in_specs=[half, half, g, g, pl.BlockSpec((MLA_W + LRU_W, d), lambda i: (0, 0)), full],
        out_specs=[full, full],
        out_shape=[jax.ShapeDtypeStruct((rows, MLA_W + LRU_W), BF16), jax.ShapeDtypeStruct((rows, d), F32)],
        compiler_params=_params(("parallel",)))(ya, yl, ga, gl, wout, h)


def _mix_out_bwd_call(dhb, wout, ya, yl, ga, gl, tm):
    rows = ya.shape[0]
    d = dhb.shape[1]

    def body(dh_ref, w_ref, ya_ref, yl_ref, ga_ref, gl_ref, dya_ref, dyl_ref, dga_ref, dgl_ref):
        dy = _dot_nt(dh_ref[...], w_ref[...])
        outs = []
        for val, g_ref, lo, out_ref in ((ya_ref[...], ga_ref, 0, dya_ref), (yl_ref[...], gl_ref, MLA_W, dyl_ref)):
            r = _rms(val, MLA_W)
            n = val * r
            dyn = dy[:, lo:lo + MLA_W]
            out_ref[...] = _rms_bwd(dyn * g_ref[...], n, r, MLA_W)
            outs.append(jnp.sum(dyn * n, axis=0, keepdims=True))

        @pl.when(pl.program_id(0) == 0)
        def _():
            dga_ref[...] = outs[0]
            dgl_ref[...] = outs[1]

        @pl.when(pl.program_id(0) != 0)
        def _():
            dga_ref[...] += outs[0]
            dgl_ref[...] += outs[1]

    half = pl.BlockSpec((tm, MLA_W), lambda i: (i, 0))
    g = pl.BlockSpec((1, MLA_W), lambda i: (0, 0))
    return pl.pallas_call(
        body, name="mix_out_bwd", grid=(rows // tm,),
        in_specs=[pl.BlockSpec((tm, d), lambda i: (i, 0)), pl.BlockSpec((MLA_W + LRU_W, d), lambda i: (0, 0)),
                  half, half, g, g],
        out_specs=[half, half, g, g],
        out_shape=[jax.ShapeDtypeStruct((rows, MLA_W), F32), jax.ShapeDtypeStruct((rows, LRU_W), F32),
                   jax.ShapeDtypeStruct((1, MLA_W), F32), jax.ShapeDtypeStruct((1, LRU_W), F32)],
        compiler_params=_params(("arbitrary",)))(dhb, wout, ya, yl, ga, gl)


def _final_call(h, g, target, lp, tm):
    rows, d = h.shape
    tpe = lp // tm

    def body(h_ref, g_ref, t_ref, dh_ref, dhb_ref, dg_ref, loss_ref):
        i = pl.program_id(0)
        x = h_ref[...]
        g = g_ref[...]
        r = _rms(x, d)
        n = x * r
        row = (i % tpe) * tm + lax.broadcasted_iota(jnp.int32, (tm, 1), 0)
        err = jnp.where(row >= CHUNK, n * g - t_ref[...], 0.0)
        dout = err * (1.0 / d)
        dh = _rms_bwd(dout * g, n, r, d)
        dh_ref[...] = dh
        dhb_ref[...] = dh.astype(BF16)
        dg = jnp.sum(dout * n, axis=0, keepdims=True)
        part = jnp.sum(jnp.sum(err * err, axis=1, keepdims=True), axis=0, keepdims=True) * (0.5 / d)
        loss = jnp.broadcast_to(part, (1, 128))

        @pl.when(i == 0)
        def _():
            dg_ref[...] = dg
            loss_ref[...] = loss

        @pl.when(i != 0)
        def _():
            dg_ref[...] += dg
            loss_ref[...] += loss

    full = pl.BlockSpec((tm, d), lambda i: (i, 0))
    return pl.pallas_call(
        body, name="final_loss", grid=(rows // tm,),
        in_specs=[full, pl.BlockSpec((1, d), lambda i: (0, 0)), full],
        out_specs=[full, full, pl.BlockSpec((1, d), lambda i: (0, 0)), pl.BlockSpec((1, 128), lambda i: (0, 0))],
        out_shape=[jax.ShapeDtypeStruct((rows, d), F32), jax.ShapeDtypeStruct((rows, d), BF16),
                   jax.ShapeDtypeStruct((1, d), F32), jax.ShapeDtypeStruct((1, 128), F32)],
        compiler_params=_params(("arbitrary",)))(h, g, target)


def _ffn_dact_call(name, dhb, wd, gate, up, tm, comm=None):
    rows, d = dhb.shape
    ns, fs, _ = wd.shape

    nsub = 2 if tm % 32 == 0 else 1
    sub = tm // nsub

    def body(dh_ref, wd_ref, g_ref, p_ref, dg_ref, dp_ref):
        wd = wd_ref[0]
        for r in range(nsub):
            rs = slice(r * sub, (r + 1) * sub)
            da = 0.5 * _dot_nt(dh_ref[rs, :], wd)
            g = g_ref[0, rs, :].astype(F32)
            p = p_ref[0, rs, :].astype(F32)
            sg = jax.nn.sigmoid(g)
            dg_ref[0, rs, :] = (da * p * sg * (1.0 + g * (1.0 - sg))).astype(BF16)
            dp_ref[0, rs, :] = (da * g * sg).astype(BF16)

    aspec = pl.BlockSpec((1, tm, fs), lambda s, i: (s, i, 0))
    oshape = jax.ShapeDtypeStruct((ns, rows, fs), BF16)
    return _hosted_call(
        body, name=name, grid=(ns, rows // tm),
        in_specs=[pl.BlockSpec((tm, d), lambda s, i: (i, 0)), pl.BlockSpec((1, fs, d), lambda s, i: (s, 0, 0)),
                  aspec, aspec],
        out_specs=[aspec, aspec], out_shape=[oshape, oshape],
        dims=("parallel", "parallel"), args=(dhb, wd, gate, up), comm=comm)


def _norm_in_bwd_call(name, pieces, h, g, dres, tm, comm=None):
    rows, d = h.shape
    npc = len(pieces)

    def body(*refs):
        d_refs = refs[0:2 * npc:2]
        w_refs = refs[1:2 * npc:2]
        h_ref, g_ref, dres_ref, dh_ref, dhb_ref, dg_ref = refs[2 * npc:]
        du = jnp.zeros((tm, d), F32)
        for d_ref, w_ref in zip(d_refs, w_refs):
            if len(d_ref.shape) == 3:
                for s in range(d_ref.shape[0]):
                    du = du + _dot(d_ref[s], w_ref[s])
            else:
                du = du + _dot(d_ref[...], w_ref[...])
        x = h_ref[...]
        r = _rms(x, d)
        n = x * r
        dh = dres_ref[...] + _rms_bwd(du * g_ref[...], n, r, d)
        dh_ref[...] = dh
        dhb_ref[...] = dh.astype(BF16)
        dg = jnp.sum(du * n, axis=0, keepdims=True)

        @pl.when(pl.program_id(0) == 0)
        def _():
            dg_ref[...] = dg

        @pl.when(pl.program_id(0) != 0)
        def _():
            dg_ref[...] += dg

    in_specs, args = [], []
    for dd, w in pieces:
        if dd.ndim == 3:
            in_specs.append(pl.BlockSpec((dd.shape[0], tm, dd.shape[2]), lambda i: (0, i, 0)))
            in_specs.append(pl.BlockSpec(w.shape, lambda i: (0, 0, 0)))
        else:
            in_specs.append(pl.BlockSpec((tm, dd.shape[1]), lambda i: (i, 0)))
            in_specs.append(pl.BlockSpec(w.shape, lambda i: (0, 0)))
        args += [dd, w]
    full = pl.BlockSpec((tm, d), lambda i: (i, 0))
    gspec = pl.BlockSpec((1, d), lambda i: (0, 0))
    return _hosted_call(
        body, name=name, grid=(rows // tm,),
        in_specs=in_specs + [full, gspec, full],
        out_specs=[full, full, gspec],
        out_shape=[jax.ShapeDtypeStruct((rows, d), F32), jax.ShapeDtypeStruct((rows, d), BF16),
                   jax.ShapeDtypeStruct((1, d), F32)],
        args=(*args, h, g, dres), comm=comm)


def _wgrad_call(name, a, b, scale=1.0, comm=None):
    a3, b3 = a.ndim == 3, b.ndim == 3
    ns = a.shape[0] if a3 else (b.shape[0] if b3 else 1)
    rows, m = a.shape[-2:]
    n = b.shape[-1]
    tmm = m if a3 else _col_tile(m, 256)

    def body(a_ref, b_ref, o_ref):
        av = a_ref[0] if a3 else a_ref[...]
        bv = b_ref[0] if b3 else b_ref[...]
        res = _dot_tn(av, bv)
        if scale != 1.0:
            res = res * scale
        if a3 or b3:
            o_ref[0] = res
        else:
            o_ref[...] = res

    aspec = (pl.BlockSpec((1, rows, tmm), lambda s, j: (s, 0, j)) if a3
             else pl.BlockSpec((rows, tmm), lambda s, j: (0, j)))
    bspec = (pl.BlockSpec((1, rows, n), lambda s, j: (s, 0, 0)) if b3
             else pl.BlockSpec((rows, n), lambda s, j: (0, 0)))
    if a3 or b3:
        ospec = pl.BlockSpec((1, tmm, n), lambda s, j: (s, j, 0))
        oshape = jax.ShapeDtypeStruct((ns, m, n), F32)
    else:
        ospec = pl.BlockSpec((tmm, n), lambda s, j: (j, 0))
        oshape = jax.ShapeDtypeStruct((m, n), F32)
    return _hosted_call(
        body, name=name, grid=(ns, m // tmm), in_specs=[aspec, bspec], out_specs=[ospec], out_shape=[oshape],
        dims=("parallel", "parallel"), args=(a, b), comm=comm)[0]


def _mla_prep_bwd_call(z, dq, dk, dv, gql, gkvl, gqh, gkh, wuq, wuk, wuv, tabs, lp, tm):
    rows = z.shape[0]
    tpe = lp // tm

    def body(z_ref, dq_ref, dk_ref, dv_ref, gql_ref, gkvl_ref, gqh_ref, gkh_ref, wuq_ref, wuk_ref, wuv_ref,
             c_ref, s1_ref, s2_ref, dz_ref, dqp_ref, dkn_ref, dvv_ref, dgql_ref, dgkvl_ref, dgqh_ref, dgkh_ref):
        gql, gkvl = gql_ref[...], gkvl_ref[...]
        gq, gk = gqh_ref[...], gkh_ref[...]
        wuq, wuk, wuv = wuq_ref[...], wuk_ref[...], wuv_ref[...]
        m = _mla_heads(z_ref[...], gql, gkvl, wuq, wuk, wuv)
        c, s1, s2 = c_ref[...], s1_ref[...], s2_ref[...]
        dgq = jnp.zeros((1, D_QKP), F32)
        dgk = jnp.zeros((1, D_QKP), F32)
        dkr = jnp.zeros((tm, D_QKP - D_NOPE), F32)
        for hd in range(HEADS):
            qn, rqh, knn, krn, rkh = m["heads"][hd]
            dqg = jnp.concatenate([dq_ref[hd, :, 0:D_NOPE].astype(F32),
                                   _rope_t(dq_ref[hd, :, D_NOPE:D_QKP].astype(F32), c, s1, s2)], axis=1)
            dgq = dgq + jnp.sum(dqg * qn, axis=0, keepdims=True)
            dqn = dqg * gq
            dqr = rqh * (dqn - qn * (jnp.sum(dqn * qn, axis=-1, keepdims=True) * (1.0 / D_QK)))
            dqp_ref[:, hd * D_QKP:(hd + 1) * D_QKP] = dqr.astype(BF16)
            kn_full = jnp.concatenate([knn, krn], axis=1)
            dkg = jnp.concatenate([dk_ref[hd, :, 0:D_NOPE].astype(F32),
                                   _rope_t(dk_ref[hd, :, D_NOPE:D_QKP].astype(F32), c, s1, s2)], axis=1)
            dgk = dgk + jnp.sum(dkg * kn_full, axis=0, keepdims=True)
            dkn = dkg * gk
            dkraw = rkh * (dkn - kn_full * (jnp.sum(dkn * kn_full, axis=-1, keepdims=True) * (1.0 / D_QK)))
            dkn_ref[:, hd * D_NOPE:(hd + 1) * D_NOPE] = dkraw[:, 0:D_NOPE].astype(BF16)
            dkr = dkr + dkraw[:, D_NOPE:D_QKP]
            dvv_ref[:, hd * D_V:(hd + 1) * D_V] = dv_ref[hd]
        dcqn = _dot(dqp_ref[...], wuq)
        dckvn = _dot_nt(dkn_ref[...], wuk) + _dot_nt(dvv_ref[...], wuv)
        dz_ref[:, 0:Q_RANK] = _rms_bwd(dcqn * gql, m["nq"], m["rq"], Q_RANK).astype(BF16)
        dz_ref[:, Q_RANK:Z_KR] = _rms_bwd(dckvn * gkvl, m["nkv"], m["rkv"], KV_RANK).astype(BF16)
        dz_ref[:, Z_KR:Z_MLA] = dkr.astype(BF16)
        parts = [(dgql_ref, jnp.sum(dcqn * m["nq"], axis=0, keepdims=True)),
                 (dgkvl_ref, jnp.sum(dckvn * m["nkv"], axis=0, keepdims=True)), (dgqh_ref, dgq), (dgkh_ref, dgk)]

        @pl.when(pl.program_id(0) == 0)
        def _():
            for ref, val in parts:
                ref[...] = val

        @pl.when(pl.program_id(0) != 0)
        def _():
            for ref, val in parts:
                ref[...] += val

    def const(shape):
        return pl.BlockSpec(shape, lambda i: tuple(0 for _ in shape))

    tab = pl.BlockSpec((tm, 128), lambda i: (i % tpe, 0))
    hq = pl.BlockSpec((HEADS, tm, D_QKP), lambda i: (0, i, 0))
    hv = pl.BlockSpec((HEADS, tm, D_V), lambda i: (0, i, 0))

    def rowspec(n):
        return pl.BlockSpec((tm, n), lambda i: (i, 0))

    return pl.pallas_call(
        body, name="mla_prep_bwd", grid=(rows // tm,),
        in_specs=[rowspec(Z_MLA), hq, hq, hv, const((1, Q_RANK)), const((1, KV_RANK)), const((1, D_QKP)),
                  const((1, D_QKP)), const((HEADS * D_QKP, Q_RANK)), const((KV_RANK, HEADS * D_NOPE)),
                  const((KV_RANK, HEADS * D_V)), tab, tab, tab],
        out_specs=[rowspec(Z_MLA), rowspec(HEADS * D_QKP), rowspec(HEADS * D_NOPE), rowspec(HEADS * D_V),
                   const((1, Q_RANK)), const((1, KV_RANK)), const((1, D_QKP)), const((1, D_QKP))],
        out_shape=[jax.ShapeDtypeStruct((rows, Z_MLA), BF16), jax.ShapeDtypeStruct((rows, HEADS * D_QKP), BF16),
                   jax.ShapeDtypeStruct((rows, HEADS * D_NOPE), BF16), jax.ShapeDtypeStruct((rows, HEADS * D_V), BF16),
                   jax.ShapeDtypeStruct((1, Q_RANK), F32), jax.ShapeDtypeStruct((1, KV_RANK), F32),
                   jax.ShapeDtypeStruct((1, D_QKP), F32), jax.ShapeDtypeStruct((1, D_QKP), F32)],
        compiler_params=_params(("arbitrary",)))(z, dq, dk, dv, gql, gkvl, gqh, gkh, wuq, wuk, wuv, *tabs)


def _local_step(h0, target, w, nb, lp, sched=None):
    tm = _row_tile(lp, 1024)
    te = _row_tile(lp, 512)
    tabs = _rope_tables(lp)
    g = {}
    if sched is None:
        host = lambda stage: None
    else:
        sched.g = g
        host = sched.host

    def ffn_fwd(tag, h, split):
        u = _rmsnorm_call(tag + "_norm", h, w[tag + "_norm"], te)
        if split:
            gate = _ffn_gate_call(tag + "_gate", u, w[tag + "_w_gate"], tm, host(tag + "_gate"))
            up, act = _ffn_upact_call(tag + "_upact", u, w[tag + "_w_up"], gate, tm, host(tag + "_upact"))
        else:
            gate, up, act = _ffn_up_call(tag + "_up", u, w[tag + "_w_gate"], w[tag + "_w_up"], tm, host(tag + "_up"))
        return _ffn_down_call(tag + "_down", act, w[tag + "_w_down"], h, te, host(tag + "_down")), (u, gate, up, act)

    def ffn_bwd(tag, h, saved, dh, dhb):
        u, gate, up, act = saved
        dgate, dup = _ffn_dact_call(tag + "_dact", dhb, w[tag + "_w_down"], gate, up, tm, host(tag + "_dact"))
        g[tag + "_w_down"] = _wgrad_call(tag + "_dwd", act, dhb, 0.5, host(tag + "_dwd"))
        g[tag + "_w_gate"] = _wgrad_call(tag + "_dwg", dgate, u, 1.0, host(tag + "_dwg"))
        g[tag + "_w_up"] = _wgrad_call(tag + "_dwu", dup, u, 1.0, host(tag + "_dwu"))
        dh_in, dhb_in, g[tag + "_norm"] = _norm_in_bwd_call(
            tag + "_din", [(dgate, w[tag + "_w_gate"]), (dup, w[tag + "_w_up"])], h, w[tag + "_norm"], dh, te,
            host(tag + "_din"))
        return dh_in, dhb_in

    h1, s1 = ffn_fwd("ffn1", h0, True)
    un = _rmsnorm_call("mix_norm", h1, w["mix_norm"], te)
    z = _mm_call("mix_in", un, w["w_in"], tm, F32)
    mla_w = (w["q_latent_norm"], w["kv_latent_norm"], w["q_head_norm"], w["k_head_norm"], w["w_uq"], w["w_uk"],
             w["w_uv"])
    q, k, v, cqn, ckvn = _mla_prep_call(z, *mla_w, tabs, lp, te)
    o, lse = _attn_fwd_call(q, k, v, nb, lp, host("attn_fwd"))
    lru_w = (w["conv_w"], w["conv_b"], w["gate_a_w"], w["gate_a_b"], w["gate_x_w"], w["gate_x_b"], w["lru_lambda"])
    yl, hs = _lru_fwd_call(z, *lru_w, nb, lp, host("lru_fwd"))
    y, h2 = _mix_out_call(o, yl, w["attn_out_norm"], w["lru_out_norm"], w["w_out"], h1, te)
    h3, s2 = ffn_fwd("ffn2", h2, False)
    dh3, dh3b, g["final_norm"], loss = _final_call(h3, w["final_norm"], target, lp, te)
    g["loss"] = loss

    dh2, dh2b = ffn_bwd("ffn2", h2, s2, dh3, dh3b)
    g["w_out"] = _wgrad_call("dw_out", y, dh2b)
    dya, dyl, g["attn_out_norm"], g["lru_out_norm"] = _mix_out_bwd_call(
        dh2b, w["w_out"], o, yl, w["attn_out_norm"], w["lru_out_norm"], te)
    dq, dk, dv = _attn_bwd_call(q, k, v, o, lse, dya, nb, lp)
    (dz_mla, dqp, dkn, dvv, g["q_latent_norm"], g["kv_latent_norm"], g["q_head_norm"],
     g["k_head_norm"]) = _mla_prep_bwd_call(z, dq, dk, dv, *mla_w, tabs, lp, te)
    g["w_uq"] = _wgrad_call("dw_uq", dqp, cqn)
    g["w_uk"] = _wgrad_call("dw_uk", ckvn, dkn)
    g["w_uv"] = _wgrad_call("dw_uv", ckvn, dvv)
    (du, dgt, g["conv_w"], g["conv_b"], g["gate_a_w"], g["gate_a_b"], g["gate_x_w"], g["gate_x_b"],
     g["lru_lambda"]) = _lru_bwd_call(z, hs, dyl, *lru_w, nb, lp, host("lru_bwd"))
    win = w["w_in"]
    g["w_in"] = jnp.concatenate(
        [_wgrad_call("dw_in_mla", dz_mla, un), _wgrad_call("dw_in_u", du, un), _wgrad_call("dw_in_g", dgt, un)],
        axis=0)
    dh1, dh1b, g["mix_norm"] = _norm_in_bwd_call(
        "mix_din", [(dz_mla, win[0:Z_MLA]), (du, win[Z_U:Z_G]), (dgt, win[Z_G:Z_W])], h1, w["mix_norm"], dh2, te,
        host("mix_din"))
    dh0, _ = ffn_bwd("ffn1", h0, s1, dh1, dh1b)
    return loss, dh0, g


def _place():
    x, y, c = lax.axis_index("x"), lax.axis_index("y"), lax.axis_index("c")
    return x, y, c, [(1 - x, y), (x, 1 - y), (1 - x, 1 - y)]


def _any_specs(n):
    return [pl.BlockSpec(memory_space=pl.ANY)] * n


def _remote(src, dst, sems, k, dev):
    send_sems, recv_sems, base = sems
    return pltpu.make_async_remote_copy(src_ref=src, dst_ref=dst, send_sem=send_sems.at[base + k],
                                        recv_sem=recv_sems.at[base + k], device_id=dev, device_id_type=MESH)


EW_VMEM_BYTES = 24 * 1024 * 1024


def _fit_rows(rows, cols, blocks):
    return _row_tile(rows, max(16, int(EW_VMEM_BYTES // (8 * blocks)) // cols))


class _Geom:
    def __init__(self, n0, n1, blocks=1.0):
        self.n0, self.n1 = n0, n1
        self.axis = 0 if n0 % 32 == 0 else 1
        self.h0, self.h1 = (n0 // 2, n1) if self.axis == 0 else (n0, n1 // 2)
        self.tr = _fit_rows(self.h0, self.h1, blocks)
        self.nblk = self.h0 // self.tr

    def half_ref(self, ref, lead, idx):
        if self.axis == 0:
            return ref.at[(*lead, pl.ds(idx * self.h0, self.h0))]
        return ref.at[(*lead, slice(None), pl.ds(idx * self.h1, self.h1))]

    def half_block(self, lead, i, idx):
        return (*lead, idx * self.nblk + i, 0) if self.axis == 0 else (*lead, i, idx)


class _Comm:
    def __init__(self, ins, out_shapes, aliases, n_sems, start, finish, deliver):
        self.ins, self.out_shapes, self.aliases, self.n_sems = list(ins), list(out_shapes), dict(aliases), n_sems
        self.start, self.finish, self.deliver = start, finish, deliver

    def scratch(self):
        return [pltpu.SemaphoreType.DMA((self.n_sems,)), pltpu.SemaphoreType.DMA((self.n_sems,))]


def _comm_call(name, comm):
    n_in = len(comm.ins)

    def body(*refs):
        ins, outs, sems = refs[:n_in], refs[n_in:-2], (*refs[-2:], 0)
        comm.start(ins, outs, sems)
        comm.finish(ins, outs, sems)

    res = pl.pallas_call(
        body, name=name, out_shape=comm.out_shapes, in_specs=_any_specs(n_in),
        out_specs=_any_specs(len(comm.out_shapes)), input_output_aliases=comm.aliases,
        scratch_shapes=comm.scratch())(*comm.ins)
    return comm.deliver(list(res))


def _hosted_call(body, *, name, grid, in_specs, out_specs, out_shape, args, scratch_shapes=(), dims=None, comm=None):
    in_specs, out_specs, out_shape = list(in_specs), list(out_specs), list(out_shape)
    if comm is None:
        return pl.pallas_call(
            body, name=name, grid=grid, in_specs=in_specs, out_specs=out_specs, out_shape=out_shape,
            scratch_shapes=list(scratch_shapes),
            compiler_params=_params(dims or ("arbitrary",) * len(grid)))(*args)
    n_in, n_out, n_ci, n_co = len(in_specs), len(out_specs), len(comm.ins), len(comm.out_shapes)

    def wrapped(*refs):
        ins, cins = refs[:n_in], refs[n_in:n_in + n_ci]
        outs = refs[n_in + n_ci:n_in + n_ci + n_out]
        couts = refs[n_in + n_ci + n_out:n_in + n_ci + n_out + n_co]
        scratch, sems = refs[n_in + n_ci + n_out + n_co:-2], (*refs[-2:], 0)
        first = functools.reduce(jnp.logical_and, [pl.program_id(k) == 0 for k in range(len(grid))])
        last = functools.reduce(jnp.logical_and, [pl.program_id(k) == grid[k] - 1 for k in range(len(grid))])

        @pl.when(first)
        def _():
            comm.start(cins, couts, sems)

        body(*ins, *outs, *scratch)

        @pl.when(last)
        def _():
            comm.finish(cins, couts, sems)

    res = pl.pallas_call(
        wrapped, name=name, grid=grid, in_specs=in_specs + _any_specs(n_ci), out_specs=out_specs + _any_specs(n_co),
        out_shape=out_shape + comm.out_shapes,
        input_output_aliases={n_in + i: n_out + o for i, o in comm.aliases.items()},
        scratch_shapes=list(scratch_shapes) + comm.scratch(),
        compiler_params=_params(("arbitrary",) * len(grid)))(*args, *comm.ins)
    comm.deliver(list(res[n_out:]))
    return list(res[:n_out])


def _gather_comm(bufs, deliver):
    n = len(bufs)
    geoms = [_Geom(*b.shape[1:]) for b in bufs]

    def first(outs, sems):
        x, y, c, chips = _place()
        cps = []
        for a in range(n):
            mine = geoms[a].half_ref(outs[a], (2 * x + y,), c)
            cps += [_remote(mine, mine, sems, 6 * a + j, (cx, cy, c)) for j, (cx, cy) in enumerate(chips)]
        return cps

    def start(ins, outs, sems):
        for cp in first(outs, sems):
            cp.start()

    def finish(ins, outs, sems):
        x, y, c, chips = _place()
        sib = (x, y, 1 - c)
        passed = []
        for a in range(n):
            for j, (cx, cy) in enumerate(chips):
                land = geoms[a].half_ref(outs[a], (2 * cx + cy,), c)
                _remote(land, land, sems, 6 * a + j, sib).wait_recv()
                cp = _remote(land, land, sems, 6 * a + 3 + j, sib)
                cp.start()
                passed.append(cp)
        for a in range(n):
            for j, (cx, cy) in enumerate(chips):
                land = geoms[a].half_ref(outs[a], (2 * cx + cy,), 1 - c)
                _remote(land, land, sems, 6 * a + 3 + j, sib).wait_recv()
        for cp in first(outs, sems) + passed:
            cp.wait_send()

    return _Comm(bufs, [jax.ShapeDtypeStruct(b.shape, b.dtype) for b in bufs], {a: a for a in range(n)}, 6 * n,
                 start, finish, deliver)


def _reduce_pair_comm(grads, deliver):
    n = len(grads)
    geoms = [_Geom(*a.shape[1:]) for a in grads]

    def copies(ins, outs, sems):
        x, y, c, _ = _place()
        return [_remote(geoms[a].half_ref(ins[a], (slice(None),), 1 - c), outs[a], sems, a, (x, y, 1 - c))
                for a in range(n)]

    def start(ins, outs, sems):
        for cp in copies(ins, outs, sems):
            cp.start()

    def finish(ins, outs, sems):
        cps = copies(ins, outs, sems)
        for cp in cps:
            cp.wait_recv()
        for cp in cps:
            cp.wait_send()

    shapes = [jax.ShapeDtypeStruct((N_SHARD, g.h0, g.h1), a.dtype) for a, g in zip(grads, geoms)]
    return _Comm(grads, shapes, {}, n, start, finish, deliver)


def _reduce_chips_comm(parts, deliver):
    n = len(parts)

    def copies(ins, outs, sems):
        x, y, c, chips = _place()
        return [_remote(ins[a].at[2 * cx + cy], outs[a].at[j], sems, 3 * a + j, (cx, cy, c))
                for a in range(n) for j, (cx, cy) in enumerate(chips)]

    def start(ins, outs, sems):
        for cp in copies(ins, outs, sems):
            cp.start()

    def finish(ins, outs, sems):
        cps = copies(ins, outs, sems)
        for cp in cps:
            cp.wait_recv()
        for cp in cps:
            cp.wait_send()

    shapes = [jax.ShapeDtypeStruct((3,) + a.shape[1:], a.dtype) for a in parts]
    return _Comm(parts, shapes, {}, 3 * n, start, finish, deliver)


def _share_pair_comm(bufs, deliver):
    n = len(bufs)
    geoms = [_Geom(*b.shape) for b in bufs]

    def copies(outs, sems):
        x, y, c, _ = _place()
        cps = []
        for a in range(n):
            mine = geoms[a].half_ref(outs[a], (), c)
            cps.append(_remote(mine, mine, sems, a, (x, y, 1 - c)))
        return cps

    def start(ins, outs, sems):
        for cp in copies(outs, sems):
            cp.start()

    def finish(ins, outs, sems):
        x, y, c, _ = _place()
        for a in range(n):
            land = geoms[a].half_ref(outs[a], (), 1 - c)
            _remote(land, land, sems, a, (x, y, 1 - c)).wait_recv()
        for cp in copies(outs, sems):
            cp.wait_send()

    return _Comm(bufs, [jax.ShapeDtypeStruct(b.shape, b.dtype) for b in bufs], {a: a for a in range(n)}, n,
                 start, finish, deliver)


def _small_comm(pack, deliver):
    r, d = pack.shape

    def copies(ins, outs, sems):
        x, y, c, _ = _place()
        cps = []
        for k in range(1, 8):
            peer = (x ^ ((k >> 2) & 1), y ^ ((k >> 1) & 1), c ^ (k & 1))
            cps.append(_remote(ins[0], outs[0].at[4 * x + 2 * y + c], sems, k - 1, peer))
        return cps

    def start(ins, outs, sems):
        for cp in copies(ins, outs, sems):
            cp.start()

    def finish(ins, outs, sems):
        cps = copies(ins, outs, sems)
        for cp in cps:
            cp.wait_recv()
        for cp in cps:
            cp.wait_send()

    return _Comm([pack], [jax.ShapeDtypeStruct((8, r, d), pack.dtype)], {}, 7, start, finish, deliver)


def _join_comms(comms):
    comms = [c for c in comms if c is not None]
    if len(comms) <= 1:
        return comms[0] if comms else None
    ins, out_shapes, aliases, spans, n_sems = [], [], {}, [], 0
    for c in comms:
        aliases.update({len(ins) + i: len(out_shapes) + o for i, o in c.aliases.items()})
        spans.append((len(ins), len(ins) + len(c.ins), len(out_shapes), len(out_shapes) + len(c.out_shapes), n_sems))
        ins += c.ins
        out_shapes += c.out_shapes
        n_sems += c.n_sems

    def run(which):
        def go(all_ins, all_outs, sems):
            for c, (i0, i1, o0, o1, base) in zip(comms, spans):
                getattr(c, which)(all_ins[i0:i1], all_outs[o0:o1], (sems[0], sems[1], sems[2] + base))
        return go

    def deliver(outs):
        for c, (_, _, o0, o1, _) in zip(comms, spans):
            c.deliver(outs[o0:o1])
        return outs

    return _Comm(ins, out_shapes, aliases, n_sems, run("start"), run("finish"), deliver)


def _ew_call(name, fn, ins, out_dtypes):
    shape = ins[0].shape
    cols = shape[-1]
    rows = 1
    for s_ in shape[:-1]:
        rows *= s_
    ins2 = [a.reshape(rows, cols) for a in ins]
    tr = rows
    for t in range(16, min(rows, max(16, (1 << 19) // cols)) + 1, 16):
        if rows % t == 0:
            tr = t
    no = len(out_dtypes)

    def body(*refs):
        outs = fn(*[r[...] for r in refs[:len(ins2)]])
        for ref, val in zip(refs[len(ins2):], outs):
            ref[...] = val.astype(ref.dtype)

    spec = pl.BlockSpec((tr, cols), lambda i: (i, 0))
    res = pl.pallas_call(
        body, name=name, grid=(rows // tr,), in_specs=[spec] * len(ins2), out_specs=[spec] * no,
        out_shape=[jax.ShapeDtypeStruct((rows, cols), dt) for dt in out_dtypes],
        compiler_params=_params(("parallel",)))(*ins2)
    return [r.reshape(shape) for r in res]


def _adamw_math(w, g, m, v):
    m = ADAM_B1 * m + (1.0 - ADAM_B1) * g
    v = ADAM_B2 * v + (1.0 - ADAM_B2) * (g * g)
    m_hat = m / (1.0 - ADAM_B1 ** ADAM_STEP)
    v_hat = v / (1.0 - ADAM_B2 ** ADAM_STEP)
    delta = -ADAM_LR * (m_hat / (jnp.sqrt(v_hat) + ADAM_EPS) + ADAM_WD * w)
    return delta, m, v


def _adamw_call(name, w, g, m, v):
    return _ew_call(name, _adamw_math, [w, g, m, v], [F32, F32, F32])


def _tiled_call(name, fn, place, grid, in_items, out_items):
    ni = len(in_items)

    def body(place_ref, *refs):
        vals = fn(*[r[...] for r in refs[:ni]])
        for ref, val in zip(refs[ni:], vals):
            ref[...] = val.astype(ref.dtype)

    spec = pltpu.PrefetchScalarGridSpec(
        num_scalar_prefetch=1, grid=grid,
        in_specs=[pl.BlockSpec(blk, imap) for _, blk, imap in in_items],
        out_specs=[pl.BlockSpec(blk, imap) for _, _, blk, imap in out_items])
    return pl.pallas_call(
        body, name=name, grid_spec=spec,
        out_shape=[jax.ShapeDtypeStruct(shp, dt) for shp, dt, _, _ in out_items],
        compiler_params=_params(("arbitrary",) * len(grid)))(place, *[a for a, _, _ in in_items])


def _cast_call(name, place, shards):
    n0, n1 = shards[0].shape
    tr = _fit_rows(n0, n1, 1.5 * len(shards))
    ins = [(a, (tr, n1), lambda i, p: (i, 0)) for a in shards]
    outs = [((N_SHARD, n0, n1), BF16, (1, tr, n1), lambda i, p: (p[0], i, 0)) for _ in shards]
    return _tiled_call(name, lambda *v: [x[None] for x in v], place, (n0 // tr,), ins, outs)


def _pair_sum_call(name, place, fulls, gots):
    k = len(fulls)
    g = _Geom(*fulls[0].shape[1:], blocks=2.5 * k)
    blk = (1, g.tr, g.h1)
    ins = [(a, blk, lambda s, i, p: g.half_block((s,), i, p[1])) for a in fulls]
    ins += [(a, blk, lambda s, i, p: (s, i, 0)) for a in gots]
    outs = [((N_SHARD, g.h0, g.h1), BF16, blk, lambda s, i, p: (s, i, 0)) for _ in fulls]
    return _tiled_call(name, lambda *v: [v[j] + v[k + j] for j in range(k)], place, (N_SHARD, g.nblk), ins, outs)


def _chip_sum_call(name, place, fulls, gots, recvs):
    k = len(fulls)
    g = _Geom(*fulls[0].shape[1:], blocks=4.5 * k)
    blk = (1, g.tr, g.h1)
    ins = [(a, blk, lambda i, p: g.half_block((p[0],), i, p[1])) for a in fulls]
    ins += [(a, blk, lambda i, p: (p[0], i, 0)) for a in gots]
    ins += [(a, (3, g.tr, g.h1), lambda i, p: (0, i, 0)) for a in recvs]
    outs = [((g.n0, g.n1), F32, (g.tr, g.h1), lambda i, p: g.half_block((), i, p[1])) for _ in fulls]

    def fn(*v):
        res = []
        for j in range(k):
            r = v[2 * k + j].astype(F32)
            res.append(v[j][0] + v[k + j][0] + r[0] + r[1] + r[2])
        return res

    return _tiled_call(name, fn, place, (g.nblk,), ins, outs)


def _adamw_group_call(name, ws, gs, ms, vs, comm=None):
    k = len(ws)
    n0, n1 = ws[0].shape
    tr = _fit_rows(n0, n1, 8 * k)
    spec = pl.BlockSpec((tr, n1), lambda i: (i, 0))

    def body(*refs):
        for j in range(k):
            g = refs[k + j][...]
            delta, m, vv = _adamw_math(refs[j][...], g, refs[2 * k + j][...], refs[3 * k + j][...])
            for ref, val in zip(refs[4 * k + 4 * j:4 * k + 4 * j + 4], (g, delta, m, vv)):
                ref[...] = val

    flat = _hosted_call(
        body, name=name, grid=(n0 // tr,), in_specs=[spec] * (4 * k), out_specs=[spec] * (4 * k),
        out_shape=[jax.ShapeDtypeStruct((n0, n1), F32)] * (4 * k), dims=("parallel",),
        args=(*ws, *gs, *ms, *vs), comm=comm)
    return [flat[4 * j:4 * j + 4] for j in range(k)]


def _small_update_call(me, early, own_early, late, own_late, wp, mp, vp):
    nd, r, d = early.shape

    def body(me_ref, e_ref, oe_ref, l_ref, ol_ref, w_ref, m_ref, v_ref, gs_ref, d_ref, nm_ref, nv_ref):
        mine = me_ref[0]

        def total(g_ref, own_ref):
            acc = None
            for k in range(nd):
                part = jnp.where(mine == k, own_ref[...], g_ref[k])
                acc = part if acc is None else acc + part
            return acc

        gs = total(e_ref, oe_ref)
        ls = total(l_ref, ol_ref)
        gs_ref[...] = gs
        first = gs[0:8] + ls[0:8]
        gs_ref[0:8, :] = first
        gs_ref[ROW_META:ROW_META + N_META, :] = gs[ROW_META:ROW_META + N_META] + ls[8:8 + N_META]
        grads = jnp.concatenate([first, gs[8:SMALL_ADAM_ROWS]], axis=0)
        delta, m, v = _adamw_math(w_ref[...], grads, m_ref[...], v_ref[...])
        d_ref[...] = delta
        nm_ref[...] = m
        nv_ref[...] = v

    vm = pl.BlockSpec(memory_space=pltpu.VMEM)
    ashape = jax.ShapeDtypeStruct((SMALL_ADAM_ROWS, d), F32)
    return pl.pallas_call(
        body, name="small_update", in_specs=[pl.BlockSpec(memory_space=pltpu.SMEM)] + [vm] * 7, out_specs=[vm] * 4,
        out_shape=[jax.ShapeDtypeStruct((r, d), F32), ashape, ashape, ashape],
        compiler_params=pltpu.CompilerParams(vmem_limit_bytes=VMEM_LIMIT_BYTES))(
            me, early, own_early, late, own_late, wp, mp, vp)


SMALL_NAMES = ["ffn1_norm", "mix_norm", "ffn2_norm", "final_norm", "q_latent_norm", "kv_latent_norm",
               "q_head_norm", "k_head_norm", "conv_b", "gate_a_b", "gate_x_b", "lru_lambda", "attn_out_norm",
               "lru_out_norm"]
ROW_CONV_W = 14
ROW_GATE_A = 16
ROW_GATE_X = 48
ROW_META = 80
ROW_LOSS = 96


def _row(a):
    flat = a.reshape(1, -1)
    return jnp.pad(flat, ((0, 0), (0, D_MODEL - flat.shape[1])))


def _pack_small(t, rows):
    parts = [_row(t[nm]) for nm in SMALL_NAMES]
    parts.append(t["conv_w"].reshape(2, D_MODEL))
    parts.append(t["gate_a_w"].reshape(32, D_MODEL))
    parts.append(t["gate_x_w"].reshape(32, D_MODEL))
    p = jnp.concatenate(parts, axis=0)
    return jnp.pad(p, ((0, rows - p.shape[0]), (0, 0)))


def _early_pack(g):
    gs = {nm: g.get(nm, jnp.zeros((1, D_MODEL), F32)) for nm in SMALL_NAMES}
    gs["q_head_norm"] = g["q_head_norm"][:, 0:D_QK]
    gs["k_head_norm"] = g["k_head_norm"][:, 0:D_QK]
    for nm in ("conv_b", "gate_a_b", "gate_x_b", "lru_lambda"):
        gs[nm] = g[nm].reshape(1, LRU_W)
    gs["conv_w"] = g["conv_w"].transpose(1, 0, 2).reshape(CONV_K, LRU_W)
    gs["gate_a_w"] = _gate_blocks(g["gate_a_w"])
    gs["gate_x_w"] = _gate_blocks(g["gate_x_w"])
    return jnp.concatenate([_pack_small(gs, ROW_META), jnp.zeros((N_META, D_MODEL), F32), _row(g["loss"][:, 0:1]),
                            jnp.zeros((SMALL_ROWS - ROW_LOSS - 1, D_MODEL), F32)], axis=0)


def _unpack_small(p, like):
    out = {}
    for k, nm in enumerate(SMALL_NAMES):
        out[nm] = p[k, 0:like[nm].size].reshape(like[nm].shape)
    out["gate_a_w"] = p[ROW_GATE_A:ROW_GATE_A + 32].reshape(like["gate_a_w"].shape)
    out["gate_x_w"] = p[ROW_GATE_X:ROW_GATE_X + 32].reshape(like["gate_x_w"].shape)
    return out


def _gate_dense(wg):
    w4 = wg[0].reshape(N_LRU_TILES, 2, 64, 64)
    zero = jnp.zeros((N_LRU_TILES, 64, 64), wg.dtype)
    top = jnp.concatenate([w4[:, 0], zero], axis=2)
    bot = jnp.concatenate([zero, w4[:, 1]], axis=2)
    return jnp.concatenate([top, bot], axis=1).astype(BF16)


def _gate_blocks(dw):
    return jnp.stack([dw[:, 0:64, 0:64], dw[:, 64:128, 64:128]], axis=1).reshape(8, 64, 64)


BIG_NAMES = ["ffn1_w_gate", "ffn1_w_up", "ffn1_w_down", "w_in", "w_uq", "w_uk", "w_uv", "w_out", "ffn2_w_gate",
             "ffn2_w_up", "ffn2_w_down"]
BIG_GROUPS = [["ffn1_w_gate", "ffn1_w_up", "ffn1_w_down", "ffn2_w_gate", "ffn2_w_up", "ffn2_w_down"], ["w_in"],
              ["w_uq"], ["w_uk", "w_uv"], ["w_out"]]
TRANSPOSED = ("ffn1_w_gate", "ffn1_w_up", "ffn2_w_gate", "ffn2_w_up", "w_in", "w_uq")


def _to2d(nm, a):
    return a[0].T if nm in TRANSPOSED else a[0]


def _from2d(nm, a):
    return (a.T if nm in TRANSPOSED else a)[None]


WEIGHT_NAMES = ["meta_tokens", "ffn1_norm", "ffn1_w_gate", "ffn1_w_up", "ffn1_w_down", "mix_norm", "w_in",
                "q_latent_norm", "w_uq", "kv_latent_norm", "w_uk", "w_uv", "q_head_norm", "k_head_norm", "conv_w",
                "conv_b", "gate_a_w", "gate_a_b", "gate_x_w", "gate_x_b", "lru_lambda", "attn_out_norm",
                "lru_out_norm", "w_out", "ffn2_norm", "ffn2_w_gate", "ffn2_w_up", "ffn2_w_down", "final_norm"]


def _weight_from(nm, slots):
    if nm == "w_in":
        win = slots.reshape(IN_WIDTH, D_MODEL)
        return jnp.concatenate([win[0:Z_KR + D_ROPE], jnp.zeros((128 - D_ROPE, D_MODEL), BF16),
                                win[Z_KR + D_ROPE:]], axis=0)
    if nm == "w_uq":
        return jnp.pad(slots, ((0, 0), (0, D_QKP - D_QK), (0, 0))).reshape(HEADS * D_QKP, Q_RANK)
    if nm in ("w_uk", "w_uv"):
        return slots.transpose(1, 0, 2).reshape(KV_RANK, HEADS * D_NOPE)
    if nm == "w_out":
        return slots.reshape(D_MODEL, D_MODEL)
    return slots


def _small_weights(p, small):
    w = {nm: p[nm] for nm in SMALL_NAMES}
    w["q_head_norm"] = jnp.pad(p["q_head_norm"], ((0, 0), (0, D_QKP - D_QK)))
    w["k_head_norm"] = jnp.pad(p["k_head_norm"], ((0, 0), (0, D_QKP - D_QK)))
    w["conv_w"] = small[:, N_META:N_META + 2, :].reshape(N_SHARD, CONV_K, LRU_TILE)
    w["gate_a_w"] = _gate_dense(p["gate_a_w"])
    w["gate_x_w"] = _gate_dense(p["gate_x_w"])
    meta = small[:, 0:N_META, :].transpose(1, 0, 2).reshape(N_META, D_MODEL)
    return w, meta


def _full_weights(p, gathered, small):
    w, meta = _small_weights(p, small)
    w.update({nm: _weight_from(nm, gathered[nm]) for nm in BIG_NAMES})
    return w, meta


def _shard_grad(nm, g):
    if nm == "w_in":
        return jnp.concatenate([g[0:Z_KR + D_ROPE], g[Z_MLA:]], axis=0).reshape(N_SHARD, IN_WIDTH // N_SHARD, D_MODEL)
    if nm == "w_uq":
        return g.reshape(HEADS, D_QKP, Q_RANK)[:, 0:D_QK, :]
    if nm in ("w_uk", "w_uv"):
        return g.reshape(KV_RANK, HEADS, D_NOPE).transpose(1, 0, 2)
    if nm == "w_out":
        return g.reshape(N_SHARD, D_MODEL // N_SHARD, D_MODEL)
    return g


def _shard_grads(g):
    return {nm: _shard_grad(nm, g[nm]) for nm in BIG_NAMES}


GATHER_FIRST = ["ffn1_w_gate"]
GATHER_AT = {"ffn1_gate": ["ffn1_w_up"], "ffn1_upact": ["ffn1_w_down"],
             "ffn1_down": ["w_in", "w_uq", "w_uk", "w_uv", "w_out"], "attn_fwd": ["ffn2_w_down"],
             "lru_fwd": ["ffn2_w_gate", "ffn2_w_up"]}
REDUCE_PLAN = [(["ffn2_w_gate", "ffn2_w_up", "ffn2_w_down"], "ffn2_din", "lru_bwd"),
               (["w_out", "w_uq", "w_uk", "w_uv", "w_in"], "mix_din", "ffn1_dact"),
               (["ffn1_w_down"], "ffn1_dwg", "ffn1_dwu"),
               (["ffn1_w_gate"], "ffn1_dwu", "ffn1_din"),
               (["ffn1_w_up"], "ffn1_din", None)]
SMALL_EARLY_AT = "mix_din"


def _same_shape_groups(names):
    return [[nm for nm in grp if nm in names] for grp in BIG_GROUPS if any(nm in names for nm in grp)]


class _Sched:
    def __init__(self, place, w, slots):
        self.place, self.w, self.slots = place, w, slots
        self.g = None
        self.sharded, self.from_pair, self.chip_bf16, self.from_chips = {}, {}, {}, {}
        self.early = self.early_all = None

    def host(self, stage):
        comms = []
        if stage in GATHER_AT:
            comms.append(self.gather(GATHER_AT[stage]))
        for names, pair_at, chips_at in REDUCE_PLAN:
            if stage == chips_at:
                comms.append(self.chips(names))
            if stage == pair_at:
                comms.append(self.pair(names))
        if stage == SMALL_EARLY_AT:
            comms.append(self.small_early())
        return _join_comms(comms)

    def small_early(self):
        self.early = _early_pack(self.g)

        def deliver(outs):
            self.early_all = outs[0]
            return outs

        return _small_comm(self.early, deliver)

    def gather(self, names):
        def deliver(outs):
            self.w.update({nm: _weight_from(nm, o) for nm, o in zip(names, outs)})
            return outs

        return _gather_comm([self.slots[nm] for nm in names], deliver)

    def pair(self, names):
        self.sharded.update({nm: _shard_grad(nm, self.g[nm]) for nm in names})

        def deliver(outs):
            self.from_pair.update(zip(names, outs))
            for grp in _same_shape_groups(names):
                sums = _pair_sum_call("pair_sum_" + grp[0], self.place, [self.sharded[nm] for nm in grp],
                                      [self.from_pair[nm] for nm in grp])
                self.chip_bf16.update(zip(grp, sums))
            return outs

        return _reduce_pair_comm([self.sharded[nm] for nm in names], deliver)

    def chips(self, names):
        def deliver(outs):
            self.from_chips.update(zip(names, outs))
            return outs

        return _reduce_chips_comm([self.chip_bf16[nm] for nm in names], deliver)

    def chip_sums(self, names):
        out = {}
        for grp in _same_shape_groups(names):
            sums = _chip_sum_call("chip_sum_" + grp[0], self.place, [self.sharded[nm] for nm in grp],
                                  [self.from_pair[nm] for nm in grp], [self.from_chips[nm] for nm in grp])
            out.update(zip(grp, sums))
        return out


def kernel(x, meta_tokens, ffn1_norm, ffn1_w_gate, ffn1_w_up, ffn1_w_down, mix_norm, w_in, q_latent_norm, w_uq, kv_latent_norm, w_uk, w_uv, q_head_norm, k_head_norm, conv_w, conv_b, gate_a_w, gate_a_b, gate_x_w, gate_x_b, lru_lambda, attn_out_norm, lru_out_norm, w_out, ffn2_norm, ffn2_w_gate, ffn2_w_up, ffn2_w_down, final_norm, loss_target, m_meta_tokens, m_ffn1_norm, m_ffn1_w_gate, m_ffn1_w_up, m_ffn1_w_down, m_mix_norm, m_w_in, m_q_latent_norm, m_w_uq, m_kv_latent_norm, m_w_uk, m_w_uv, m_q_head_norm, m_k_head_norm, m_conv_w, m_conv_b, m_gate_a_w, m_gate_a_b, m_gate_x_w, m_gate_x_b, m_lru_lambda, m_attn_out_norm, m_lru_out_norm, m_w_out, m_ffn2_norm, m_ffn2_w_gate, m_ffn2_w_up, m_ffn2_w_down, m_final_norm, v_meta_tokens, v_ffn1_norm, v_ffn1_w_gate, v_ffn1_w_up, v_ffn1_w_down, v_mix_norm, v_w_in, v_q_latent_norm, v_w_uq, v_kv_latent_norm, v_w_uk, v_w_uv, v_q_head_norm, v_k_head_norm, v_conv_w, v_conv_b, v_gate_a_w, v_gate_a_b, v_gate_x_w, v_gate_x_b, v_lru_lambda, v_attn_out_norm, v_lru_out_norm, v_w_out, v_ffn2_norm, v_ffn2_w_gate, v_ffn2_w_up, v_ffn2_w_down, v_final_norm):
    args = locals()
    p = {nm: args[nm] for nm in WEIGHT_NAMES}
    mom = {nm: args["m_" + nm] for nm in WEIGHT_NAMES}
    var = {nm: args["v_" + nm] for nm in WEIGHT_NAMES}
    nb, seq, d = x.shape
    lp = CHUNK + seq
    xi, yi, ci = lax.axis_index("x"), lax.axis_index("y"), lax.axis_index("c")
    chip = 2 * xi + yi

    place = jnp.stack([chip, ci]).astype(jnp.int32)
    p2 = {nm: _to2d(nm, p[nm]) for nm in BIG_NAMES}
    m2 = {nm: _to2d(nm, mom[nm]) for nm in BIG_NAMES}
    v2 = {nm: _to2d(nm, var[nm]) for nm in BIG_NAMES}

    slots = {}
    for grp in BIG_GROUPS:
        for nm, buf in zip(grp, _cast_call("cast_" + grp[0], place, [p2[nm] for nm in grp])):
            slots[nm] = buf
    small_shard = jnp.concatenate(
        [meta_tokens, conv_w[0].reshape(2, 2 * LRU_TILE), jnp.zeros((14, 2 * LRU_TILE), F32)], axis=0)
    small_slots = lax.dynamic_update_slice(jnp.zeros((N_SHARD,) + small_shard.shape, F32), small_shard[None],
                                           (chip, 0, 0))
    first = _comm_call("gather_first", _gather_comm([slots[nm] for nm in GATHER_FIRST] + [small_slots], lambda o: o))
    w, meta = _small_weights(p, first[-1])
    w.update({nm: _weight_from(nm, o) for nm, o in zip(GATHER_FIRST, first[:-1])})
    sched = _Sched(place, w, slots)

    h0 = jnp.concatenate(
        [jnp.zeros((nb, PAD_ROWS, d), F32), jnp.broadcast_to(meta[None], (nb, N_META, d)), x], axis=1)
    target = jnp.pad(loss_target, ((0, 0), (CHUNK, 0), (0, 0)))
    loss_part, dh0, g = _local_step(h0.reshape(nb * lp, d), target.reshape(nb * lp, d), w, nb, lp, sched)
    dh0 = dh0.reshape(nb, lp, d)
    grad_x = dh0[:, CHUNK:, :]

    late = jnp.concatenate([g["ffn1_norm"], g["mix_norm"], jnp.zeros((6, D_MODEL), F32),
                            jnp.sum(dh0[:, PAD_ROWS:CHUNK, :], axis=0)], axis=0)
    late_all = _comm_call("gather_late", _small_comm(late, lambda o: o))[0]
    small_like = {nm: p[nm] for nm in SMALL_NAMES + ["gate_a_w", "gate_x_w"]}

    def pack_w(t):
        tt = {nm: t[nm] for nm in SMALL_NAMES + ["gate_a_w", "gate_x_w"]}
        tt["conv_w"] = jnp.zeros((CONV_K, LRU_W), F32)
        return _pack_small(tt, SMALL_ADAM_ROWS)

    me = (4 * xi + 2 * yi + ci).astype(jnp.int32).reshape(1)
    gsum, dsm, msm, vsm = _small_update_call(me, sched.early_all, sched.early, late_all, late, pack_w(p),
                                             pack_w(mom), pack_w(var))
    grads = _unpack_small(gsum, small_like)
    delta = _unpack_small(dsm, small_like)
    new_m = _unpack_small(msm, small_like)
    new_v = _unpack_small(vsm, small_like)
    loss = gsum[ROW_LOSS, 0]
    gmeta = gsum[ROW_META:ROW_META + N_META].reshape(N_META, N_SHARD, D_MODEL // N_SHARD)
    grads["meta_tokens"] = lax.dynamic_index_in_dim(gmeta, chip, axis=1, keepdims=False)
    gconv = gsum[ROW_CONV_W:ROW_CONV_W + 2].reshape(CONV_K, N_SHARD, LRU_TILE)
    grads["conv_w"] = lax.dynamic_index_in_dim(gconv, chip, axis=1, keepdims=False)[None]
    for nm in ("meta_tokens", "conv_w"):
        delta[nm], new_m[nm], new_v[nm] = _adamw_call("adamw_" + nm, p[nm], grads[nm], mom[nm], var[nm])

    done = [nm for names, _, chips_at in REDUCE_PLAN if chips_at is not None for nm in names]
    last = [nm for names, _, chips_at in REDUCE_PLAN if chips_at is None for nm in names]
    mine = sched.chip_sums(done)
    shard_grads = {}
    share = _share_pair_comm([mine[nm] for nm in done], lambda o: shard_grads.update(zip(done, o)))
    _comm_call("share_pair", _join_comms([share, sched.chips(last)]))

    def adamw(names):
        res = _adamw_group_call("adamw_" + names[0], [p2[nm] for nm in names], [shard_grads[nm] for nm in names],
                                [m2[nm] for nm in names], [v2[nm] for nm in names])
        for nm, (gg, dd, mm, vv) in zip(names, res):
            grads[nm], delta[nm], new_m[nm], new_v[nm] = (_from2d(nm, t) for t in (gg, dd, mm, vv))

    for grp in _same_shape_groups(done):
        adamw(grp)
    mine = sched.chip_sums(last)
    shard_grads = dict(zip(last, _comm_call("share_last", _share_pair_comm([mine[nm] for nm in last], lambda o: o))))
    adamw(last)

    return (loss, grad_x, *[grads[nm] for nm in WEIGHT_NAMES], *[delta[nm] for nm in WEIGHT_NAMES],
            *[new_m[nm] for nm in WEIGHT_NAMES], *[new_v[nm] for nm in WEIGHT_NAMES])
```

```python
import functools
import math

import jax
import jax.numpy as jnp
import numpy as np
from jax import lax
from jax.experimental import pallas as pl
from jax.experimental.pallas import tpu as pltpu

F32 = jnp.float32
BF16 = jnp.bfloat16
MESH = pl.DeviceIdType.MESH

D_MODEL = 1024
N_META = 16
CHUNK = 64
PAD_ROWS = CHUNK - N_META
HEADS = 4
D_NOPE = 128
D_ROPE = 64
D_QK = D_NOPE + D_ROPE
D_QKP = 256
D_V = 128
KV_RANK = 256
Q_RANK = 384
MLA_W = HEADS * D_V
LRU_W = 512
LRU_TILE = 128
N_LRU_TILES = LRU_W // LRU_TILE
CONV_K = 4
C_RGLRU = 8.0
ROPE_THETA = 10000.0
D_FF = 2816
N_SHARD = 4
EPS = 1e-6
NEG_INF = -1e30
Z_KR = Q_RANK + KV_RANK
Z_MLA = Z_KR + 128
Z_U = Z_MLA
Z_G = Z_U + LRU_W
Z_W = Z_G + LRU_W
IN_WIDTH = Q_RANK + KV_RANK + D_ROPE + 2 * LRU_W

ADAM_LR = 0.001
ADAM_B1 = 0.9
ADAM_B2 = 0.999
ADAM_EPS = 1e-08
ADAM_WD = 0.01
ADAM_STEP = 10

VMEM_LIMIT_BYTES = 56 * 1024 * 1024
SMALL_ROWS = 104
SMALL_ADAM_ROWS = 80


def _params(sem):
    return pltpu.CompilerParams(dimension_semantics=sem, vmem_limit_bytes=VMEM_LIMIT_BYTES)


def _row_tile(rows, target):
    best = 16
    for t in range(16, min(rows, target) + 1, 16):
        if rows % t == 0:
            best = t
    return best


def _col_tile(cols, target):
    best = cols
    for t in range(128, min(cols, target) + 1, 128):
        if cols % t == 0:
            best = t
    return best


def _dot(a, b):
    return jnp.dot(a, b, preferred_element_type=F32)


def _dot_nt(a, b):
    return lax.dot_general(a, b, (((1,), (1,)), ((), ())), preferred_element_type=F32)


def _dot_tn(a, b):
    return lax.dot_general(a, b, (((0,), (0,)), ((), ())), preferred_element_type=F32)


def _rms(x, n):
    return lax.rsqrt(jnp.sum(x * x, axis=-1, keepdims=True) * (1.0 / n) + EPS)


def _rms_bwd(dn, nrm, r, n):
    return r * (dn - nrm * (jnp.sum(dn * nrm, axis=-1, keepdims=True) * (1.0 / n)))


def _gelu(x):
    k = math.sqrt(2.0 / math.pi)
    t = jnp.tanh(k * (x + 0.044715 * x * x * x))
    return 0.5 * x * (1.0 + t), t


def _gelu_grad(x, t):
    k = math.sqrt(2.0 / math.pi)
    return 0.5 * (1.0 + t) + 0.5 * x * (1.0 - t * t) * k * (1.0 + 3.0 * 0.044715 * x * x)


def _sigmoid(x):
    return 0.5 + 0.5 * jnp.tanh(0.5 * x)


def _softplus_neg(lam):
    e = jnp.exp(-jnp.abs(lam))
    log1p = jnp.where(e < 0.01, e * (1.0 - e * (0.5 - e * (1.0 / 3 - e * 0.25))), jnp.log(1.0 + e))
    return jnp.maximum(-lam, 0.0) + log1p


def _rope(t, c, s1, s2):
    return t * c + pltpu.roll(t, 96, 1) * s1 + pltpu.roll(t, 32, 1) * s2


def _rope_t(d, c, s1, s2):
    return d * c + pltpu.roll(d * s1, 32, 1) + pltpu.roll(d * s2, 96, 1)


def _rope_tables(lp):
    pos = (np.arange(lp, dtype=np.int32) - PAD_ROWS).astype(np.float32)
    inv_freq = (ROPE_THETA ** (-np.arange(0, D_ROPE // 2, dtype=np.float32) / (D_ROPE // 2))).astype(np.float32)
    ang = (pos[:, None] * inv_freq[None, :]).astype(np.float32).astype(np.float64)
    cos, sin = np.cos(ang).astype(np.float32), np.sin(ang).astype(np.float32)
    z = np.zeros_like(cos)
    return (jnp.asarray(np.concatenate([cos, cos, z, z], 1)), jnp.asarray(np.concatenate([-sin, z, z, z], 1)),
            jnp.asarray(np.concatenate([z, sin, z, z], 1)))


def _rmsnorm_call(name, h, g, tm):
    rows, d = h.shape

    def body(h_ref, g_ref, o_ref):
        x = h_ref[...]
        o_ref[...] = (x * _rms(x, d) * g_ref[...]).astype(BF16)

    return pl.pallas_call(
        body, name=name, grid=(rows // tm,),
        in_specs=[pl.BlockSpec((tm, d), lambda i: (i, 0)), pl.BlockSpec((1, d), lambda i: (0, 0))],
        out_specs=pl.BlockSpec((tm, d), lambda i: (i, 0)),
        out_shape=jax.ShapeDtypeStruct((rows, d), BF16),
        compiler_params=_params(("parallel",)))(h, g)


def _ffn_up_call(name, u, wg, wu, tm, comm=None):
    rows, d = u.shape
    ns, fs, _ = wg.shape

    def body(u_ref, wg_ref, wu_ref, g_ref, p_ref, a_ref):
        uu = u_ref[...]
        g = _dot_nt(uu, wg_ref[0])
        p = _dot_nt(uu, wu_ref[0])
        g_ref[0] = g.astype(BF16)
        p_ref[0] = p.astype(BF16)
        a_ref[0] = (g * jax.nn.sigmoid(g) * p).astype(BF16)

    wspec = pl.BlockSpec((1, fs, d), lambda s, i: (s, 0, 0))
    ospec = pl.BlockSpec((1, tm, fs), lambda s, i: (s, i, 0))
    oshape = jax.ShapeDtypeStruct((ns, rows, fs), BF16)
    return _hosted_call(
        body, name=name, grid=(ns, rows // tm),
        in_specs=[pl.BlockSpec((tm, d), lambda s, i: (i, 0)), wspec, wspec],
        out_specs=[ospec, ospec, ospec], out_shape=[oshape, oshape, oshape],
        dims=("parallel", "parallel"), args=(u, wg, wu), comm=comm)


def _ffn_gate_call(name, u, wg, tm, comm=None):
    rows, d = u.shape
    ns, fs, _ = wg.shape

    def body(u_ref, wg_ref, g_ref):
        g_ref[0] = _dot_nt(u_ref[...], wg_ref[0]).astype(BF16)

    return _hosted_call(
        body, name=name, grid=(ns, rows // tm),
        in_specs=[pl.BlockSpec((tm, d), lambda s, i: (i, 0)), pl.BlockSpec((1, fs, d), lambda s, i: (s, 0, 0))],
        out_specs=[pl.BlockSpec((1, tm, fs), lambda s, i: (s, i, 0))],
        out_shape=[jax.ShapeDtypeStruct((ns, rows, fs), BF16)],
        dims=("parallel", "parallel"), args=(u, wg), comm=comm)[0]


def _ffn_upact_call(name, u, wu, gate, tm, comm=None):
    rows, d = u.shape
    ns, fs, _ = wu.shape

    def body(u_ref, wu_ref, g_ref, p_ref, a_ref):
        p = _dot_nt(u_ref[...], wu_ref[0])
        g = g_ref[0].astype(F32)
        p_ref[0] = p.astype(BF16)
        a_ref[0] = (g * jax.nn.sigmoid(g) * p).astype(BF16)

    ospec = pl.BlockSpec((1, tm, fs), lambda s, i: (s, i, 0))
    oshape = jax.ShapeDtypeStruct((ns, rows, fs), BF16)
    return _hosted_call(
        body, name=name, grid=(ns, rows // tm),
        in_specs=[pl.BlockSpec((tm, d), lambda s, i: (i, 0)), pl.BlockSpec((1, fs, d), lambda s, i: (s, 0, 0)), ospec],
        out_specs=[ospec, ospec], out_shape=[oshape, oshape],
        dims=("parallel", "parallel"), args=(u, wu, gate), comm=comm)


def _ffn_down_call(name, a, wd, h, tm, comm=None):
    rows, d = h.shape
    ns, _, fs = a.shape

    def body(a_ref, wd_ref, h_ref, o_ref):
        acc = h_ref[...]
        for s in range(ns):
            acc = acc + 0.5 * _dot(a_ref[s], wd_ref[s])
        o_ref[...] = acc

    return _hosted_call(
        body, name=name, grid=(rows // tm,),
        in_specs=[pl.BlockSpec((ns, tm, fs), lambda i: (0, i, 0)),
                  pl.BlockSpec((ns, fs, d), lambda i: (0, 0, 0)),
                  pl.BlockSpec((tm, d), lambda i: (i, 0))],
        out_specs=[pl.BlockSpec((tm, d), lambda i: (i, 0))],
        out_shape=[jax.ShapeDtypeStruct((rows, d), F32)],
        dims=("parallel",), args=(a, wd, h), comm=comm)[0]


def _mm_call(name, a, bt, tm, out_dtype):
    rows, k = a.shape
    n = bt.shape[0]

    def body(a_ref, b_ref, o_ref):
        o_ref[...] = _dot_nt(a_ref[...], b_ref[...]).astype(out_dtype)

    return pl.pallas_call(
        body, name=name, grid=(rows // tm,),
        in_specs=[pl.BlockSpec((tm, k), lambda i: (i, 0)), pl.BlockSpec((n, k), lambda i: (0, 0))],
        out_specs=pl.BlockSpec((tm, n), lambda i: (i, 0)),
        out_shape=jax.ShapeDtypeStruct((rows, n), out_dtype),
        compiler_params=_params(("parallel",)))(a, bt)


def _mla_heads(z, gql, gkvl, wuq, wuk, wuv):
    cq = z[:, 0:Q_RANK]
    ckv = z[:, Q_RANK:Z_KR]
    kr = z[:, Z_KR:Z_MLA]
    rq = _rms(cq, Q_RANK)
    nq = cq * rq
    cqn = (nq * gql).astype(BF16)
    rkv = _rms(ckv, KV_RANK)
    nkv = ckv * rkv
    ckvn = (nkv * gkvl).astype(BF16)
    qraw = _dot_nt(cqn, wuq)
    knope = _dot(ckvn, wuk)
    v = _dot(ckvn, wuv)
    skr = jnp.sum(kr * kr, axis=-1, keepdims=True)
    heads = []
    for hd in range(HEADS):
        qh = qraw[:, hd * D_QKP:(hd + 1) * D_QKP]
        rqh = lax.rsqrt(jnp.sum(qh * qh, axis=-1, keepdims=True) * (1.0 / D_QK) + EPS)
        kn = knope[:, hd * D_NOPE:(hd + 1) * D_NOPE]
        rkh = lax.rsqrt((jnp.sum(kn * kn, axis=-1, keepdims=True) + skr) * (1.0 / D_QK) + EPS)
        heads.append((qh * rqh, rqh, kn * rkh, kr * rkh, rkh))
    return dict(rq=rq, nq=nq, cqn=cqn, rkv=rkv, nkv=nkv, ckvn=ckvn, v=v, heads=heads)


def _mla_prep_call(z, gql, gkvl, gqh, gkh, wuq, wuk, wuv, tabs, lp, tm):
    rows = z.shape[0]
    tpe = lp // tm

    def body(z_ref, gql_ref, gkvl_ref, gqh_ref, gkh_ref, wuq_ref, wuk_ref, wuv_ref, c_ref, s1_ref, s2_ref,
             q_ref, k_ref, v_ref, cqn_ref, ckvn_ref):
        m = _mla_heads(z_ref[...], gql_ref[...], gkvl_ref[...], wuq_ref[...], wuk_ref[...], wuv_ref[...])
        c, s1, s2 = c_ref[...], s1_ref[...], s2_ref[...]
        gq, gk = gqh_ref[...], gkh_ref[...]
        row = (pl.program_id(0) % tpe) * tm + lax.broadcasted_iota(jnp.int32, (tm, 1), 0)
        spare = (lax.broadcasted_iota(jnp.int32, (1, D_QKP - D_NOPE), 1) == D_ROPE).astype(F32)
        kmask = jnp.where(row < PAD_ROWS, NEG_INF * math.sqrt(D_QK), 0.0) * spare
        for hd in range(HEADS):
            qn, _, knn, krn, _ = m["heads"][hd]
            qg = qn * gq
            q_ref[hd, :, 0:D_NOPE] = qg[:, 0:D_NOPE].astype(BF16)
            q_ref[hd, :, D_NOPE:D_QKP] = (_rope(qg[:, D_NOPE:D_QKP], c, s1, s2) + spare).astype(BF16)
            k_ref[hd, :, 0:D_NOPE] = (knn * gk[:, 0:D_NOPE]).astype(BF16)
            k_ref[hd, :, D_NOPE:D_QKP] = (_rope(krn * gk[:, D_NOPE:D_QKP], c, s1, s2) + kmask).astype(BF16)
            v_ref[hd] = m["v"][:, hd * D_V:(hd + 1) * D_V].astype(BF16)
        cqn_ref[...] = m["cqn"]
        ckvn_ref[...] = m["ckvn"]

    def const(shape):
        return pl.BlockSpec(shape, lambda i: tuple(0 for _ in shape))

    tab = pl.BlockSpec((tm, 128), lambda i: (i % tpe, 0))
    return pl.pallas_call(
        body, name="mla_prep", grid=(rows // tm,),
        in_specs=[pl.BlockSpec((tm, Z_MLA), lambda i: (i, 0)), const((1, Q_RANK)), const((1, KV_RANK)),
                  const((1, D_QKP)), const((1, D_QKP)), const((HEADS * D_QKP, Q_RANK)),
                  const((KV_RANK, HEADS * D_NOPE)), const((KV_RANK, HEADS * D_V)), tab, tab, tab],
        out_specs=[pl.BlockSpec((HEADS, tm, D_QKP), lambda i: (0, i, 0)),
                   pl.BlockSpec((HEADS, tm, D_QKP), lambda i: (0, i, 0)),
                   pl.BlockSpec((HEADS, tm, D_V), lambda i: (0, i, 0)),
                   pl.BlockSpec((tm, Q_RANK), lambda i: (i, 0)),
                   pl.BlockSpec((tm, KV_RANK), lambda i: (i, 0))],
        out_shape=[jax.ShapeDtypeStruct((HEADS, rows, D_QKP), BF16),
                   jax.ShapeDtypeStruct((HEADS, rows, D_QKP), BF16),
                   jax.ShapeDtypeStruct((HEADS, rows, D_V), BF16),
                   jax.ShapeDtypeStruct((rows, Q_RANK), BF16),
                   jax.ShapeDtypeStruct((rows, KV_RANK), BF16)],
        compiler_params=_params(("parallel",)))(z, gql, gkvl, gqh, gkh, wuq, wuk, wuv, *tabs)


Q_BLOCK_ROWS = 528


def _q_block(lp):
    return _row_tile(lp, Q_BLOCK_ROWS)


def _key_end(ext, lp):
    return min(lp, -(-ext // CHUNK) * CHUNK)


def _diag_bias(qb, j0, nk):
    shift = CHUNK.bit_length() - 1
    r = jnp.right_shift(j0 + lax.broadcasted_iota(jnp.int32, (qb, nk), 0), shift)
    c = jnp.right_shift(j0 + lax.broadcasted_iota(jnp.int32, (qb, nk), 1), shift)
    return jnp.where(c <= r, 0.0, NEG_INF)


def _attn_fwd_call(q, k, v, nb, lp, comm=None):
    rows = nb * lp
    qb = _q_block(lp)
    scale = 1.0 / math.sqrt(D_QK)

    def body(q_ref, k_ref, v_ref, o_ref, lse_ref):
        for j in range(lp // qb):
            j0, ext = j * qb, (j + 1) * qb
            kend = _key_end(ext, lp)
            qj = q_ref[0, j0:ext, :]
            sd = _dot_nt(qj, k_ref[0, j0:kend, :]) * scale + _diag_bias(qb, j0, kend - j0)
            mx = jnp.max(sd, axis=-1, keepdims=True)
            if j > 0:
                so = _dot_nt(qj, k_ref[0, 0:j0, :]) * scale
                mx = jnp.maximum(mx, jnp.max(so, axis=-1, keepdims=True))
            pd = jnp.exp(sd - mx)
            l = jnp.sum(pd, axis=-1, keepdims=True)
            o = _dot(pd.astype(BF16), v_ref[0, j0:kend, :])
            if j > 0:
                po = jnp.exp(so - mx)
                l = l + jnp.sum(po, axis=-1, keepdims=True)
                o = o + _dot(po.astype(BF16), v_ref[0, 0:j0, :])
            o_ref[j0:ext, :] = o / l
            lse_ref[0, j0:ext, :] = mx + jnp.log(l)

    return _hosted_call(
        body, name="attn_fwd", grid=(nb, HEADS),
        in_specs=[pl.BlockSpec((1, lp, D_QKP), lambda b, h: (h, b, 0)),
                  pl.BlockSpec((1, lp, D_QKP), lambda b, h: (h, b, 0)),
                  pl.BlockSpec((1, lp, D_V), lambda b, h: (h, b, 0))],
        out_specs=[pl.BlockSpec((lp, D_V), lambda b, h: (b, h)),
                   pl.BlockSpec((1, lp, 1), lambda b, h: (h, b, 0))],
        out_shape=[jax.ShapeDtypeStruct((rows, MLA_W), F32),
                   jax.ShapeDtypeStruct((HEADS, rows, 1), F32)],
        dims=("parallel", "parallel"), args=(q, k, v), comm=comm)


def _attn_bwd_call(q, k, v, o, lse, do, nb, lp, comm=None):
    rows = nb * lp
    qb = _q_block(lp)
    scale = 1.0 / math.sqrt(D_QK)

    def body(q_ref, k_ref, v_ref, o_ref, lse_ref, do_ref, dq_ref, dk_ref, dv_ref, dk_acc, dv_acc):
        dk_acc[...] = jnp.zeros_like(dk_acc)
        dv_acc[...] = jnp.zeros_like(dv_acc)
        for j in range(lp // qb):
            j0, ext = j * qb, (j + 1) * qb
            kend = _key_end(ext, lp)
            dbias = _diag_bias(qb, j0, kend - j0)
            qj = q_ref[0, j0:ext, :]
            doj = do_ref[j0:ext, :]
            delta = jnp.sum(doj * o_ref[j0:ext, :], axis=-1, keepdims=True)
            dob = doj.astype(BF16)
            lse = lse_ref[0, j0:ext, :]
            dq = jnp.zeros((qb, D_QKP), F32)
            for lo, hi, bias in ((j0, kend, dbias), (0, j0, None)):
                if hi == lo:
                    continue
                kk = k_ref[0, lo:hi, :]
                s = _dot_nt(qj, kk) * scale
                p = jnp.exp((s if bias is None else s + bias) - lse)
                dv_acc[lo:hi, :] += _dot_tn(p.astype(BF16), dob)
                dp = _dot_nt(dob, v_ref[0, lo:hi, :])
                ds = (p * (dp - delta) * scale).astype(BF16)
                dq = dq + _dot(ds, kk)
                dk_acc[lo:hi, :] += _dot_tn(ds, qj)
            dq_ref[0, j0:ext, :] = dq.astype(BF16)
        dk_ref[0] = dk_acc[...].astype(BF16)
        dv_ref[0] = dv_acc[...].astype(BF16)

    qspec = pl.BlockSpec((1, lp, D_QKP), lambda b, h: (h, b, 0))
    vspec = pl.BlockSpec((1, lp, D_V), lambda b, h: (h, b, 0))
    ospec = pl.BlockSpec((lp, D_V), lambda b, h: (b, h))
    return _hosted_call(
        body, name="attn_bwd", grid=(nb, HEADS),
        in_specs=[qspec, qspec, vspec, ospec, pl.BlockSpec((1, lp, 1), lambda b, h: (h, b, 0)), ospec],
        out_specs=[qspec, qspec, vspec],
        out_shape=[jax.ShapeDtypeStruct((HEADS, rows, D_QKP), BF16),
                   jax.ShapeDtypeStruct((HEADS, rows, D_QKP), BF16),
                   jax.ShapeDtypeStruct((HEADS, rows, D_V), BF16)],
        scratch_shapes=[pltpu.VMEM((lp, D_QKP), F32), pltpu.VMEM((lp, D_V), F32)],
        dims=("parallel", "parallel"), args=(q, k, v, o, lse, do), comm=comm)


def _lru_gates(u, cw, cb, wa, ba, wx, bx, lam, lp):
    xc = (cw[3:4, :] * u + cw[2:3, :] * pltpu.roll(u, 1, 0) + cw[1:2, :] * pltpu.roll(u, 2, 0)
          + cw[0:1, :] * pltpu.roll(u, 3, 0) + cb)
    xcb = xc.astype(BF16)
    r = _sigmoid(_dot(xcb, wa) + ba)
    i = _sigmoid(_dot(xcb, wx) + bx)
    sp = _softplus_neg(lam)
    la = -C_RGLRU * r * sp
    a = jnp.exp(la)
    x2 = 2.0 * la
    e2 = a * a
    m2 = jnp.maximum(jnp.where(x2 > -0.01, -x2 * (1.0 + 0.5 * x2), 1.0 - e2), 1e-30)
    rs = lax.rsqrt(m2)
    row = lax.broadcasted_iota(jnp.int32, (lp, LRU_TILE), 0)
    first = row == PAD_ROWS
    valid = row >= PAD_ROWS
    mult_eff = jnp.where(first, 1.0, m2 * rs)
    return dict(xc=xc, xcb=xcb, r=r, i=i, sp=sp, a=a, e2=e2, rs=rs, mult_eff=mult_eff, first=first, valid=valid)


def _scan_rows(a, b, a_s, b_s, out_ref, lp, reverse):
    sub = lax.broadcasted_iota(jnp.int32, (lp, LRU_TILE), 0) & 7
    for dist in (1, 2, 4):
        shift = lp - dist if reverse else dist
        keep = (sub + dist <= 7) if reverse else (sub >= dist)
        a_sh = pltpu.roll(a, shift, 0)
        b_sh = pltpu.roll(b, shift, 0)
        b = jnp.where(keep, a * b_sh + b, b)
        a = jnp.where(keep, a * a_sh, a)
    a_s[...] = a
    b_s[...] = b
    n_groups = lp // 8
    edge = 0 if reverse else 7

    def group(gi, carry):
        r0 = pl.multiple_of(((n_groups - 1 - gi) if reverse else gi) * 8, 8)
        a8 = a_s[pl.ds(r0, 8), :]
        b8 = b_s[pl.ds(r0, 8), :]
        out_ref[pl.ds(r0, 8), :] = a8 * carry + b8
        return a8[edge:edge + 1, :] * carry + b8[edge:edge + 1, :]

    lax.fori_loop(0, n_groups, group, jnp.zeros((1, LRU_TILE), F32), unroll=4)


def _lru_specs(lp):
    seq = lambda col0: pl.BlockSpec((lp, LRU_TILE), lambda t, b: (b, col0 + t))
    cw = pl.BlockSpec((1, CONV_K, LRU_TILE), lambda t, b: (t, 0, 0))
    vec = pl.BlockSpec((1, LRU_TILE), lambda t, b: (0, t))
    mat = pl.BlockSpec((1, LRU_TILE, LRU_TILE), lambda t, b: (t, 0, 0))
    return seq, cw, vec, mat


def _lru_fwd_call(z, cw, cb, wa, ba, wx, bx, lam, nb, lp, comm=None):
    rows = nb * lp
    seq, cwspec, vec, mat = _lru_specs(lp)

    def body(u_ref, g_ref, cw_ref, cb_ref, wa_ref, ba_ref, wx_ref, bx_ref, lam_ref, y_ref, hs_ref, a_s, b_s):
        m = _lru_gates(u_ref[...], cw_ref[0], cb_ref[...], wa_ref[0], ba_ref[...], wx_ref[0], bx_ref[...],
                       lam_ref[...], lp)
        a = jnp.where(m["valid"], m["a"], 0.0)
        b = jnp.where(m["valid"], m["mult_eff"] * (m["i"] * m["xc"]), 0.0)
        _scan_rows(a, b, a_s, b_s, hs_ref, lp, reverse=False)
        gl, _ = _gelu(g_ref[...])
        y_ref[...] = hs_ref[...] * gl

    oshape = jax.ShapeDtypeStruct((rows, LRU_W), F32)
    return _hosted_call(
        body, name="lru_fwd", grid=(N_LRU_TILES, nb),
        in_specs=[seq(Z_U // LRU_TILE), seq(Z_G // LRU_TILE), cwspec, vec, mat, vec, mat, vec, vec],
        out_specs=[seq(0), seq(0)], out_shape=[oshape, oshape],
        scratch_shapes=[pltpu.VMEM((lp, LRU_TILE), F32), pltpu.VMEM((lp, LRU_TILE), F32)],
        dims=("parallel", "parallel"), args=(z, z, cw, cb, wa, ba, wx, bx, lam), comm=comm)


def _lru_bwd_call(z, hs, dy, cw, cb, wa, ba, wx, bx, lam, nb, lp, comm=None):
    rows = nb * lp
    seq, cwspec, vec, mat = _lru_specs(lp)

    def body(u_ref, g_ref, hs_ref, dy_ref, cw_ref, cb_ref, wa_ref, ba_ref, wx_ref, bx_ref, lam_ref,
             du_ref, dg_ref, dcw_ref, dcb_ref, dwa_ref, dba_ref, dwx_ref, dbx_ref, dlam_ref, a_s, b_s, d_s):
        b_idx = pl.program_id(1)
        u = u_ref[...]
        cw = cw_ref[0]
        wa, wx = wa_ref[0], wx_ref[0]
        lam = lam_ref[...]
        m = _lru_gates(u, cw, cb_ref[...], wa, ba_ref[...], wx, bx_ref[...], lam, lp)
        gate = g_ref[...]
        gl, th = _gelu(gate)
        dy = dy_ref[...]
        hs = hs_ref[...]
        dg_ref[...] = (dy * hs * _gelu_grad(gate, th)).astype(BF16)
        a_eff = jnp.where(m["valid"], m["a"], 0.0)
        _scan_rows(pltpu.roll(a_eff, lp - 1, 0), dy * gl, a_s, b_s, d_s, lp, reverse=True)
        ds = d_s[...]
        xc, r, i = m["xc"], m["r"], m["i"]
        row = lax.broadcasted_iota(jnp.int32, (lp, LRU_TILE), 0)
        da = ds * jnp.where(row >= 1, pltpu.roll(hs, 1, 0), 0.0)
        db = jnp.where(m["valid"], ds, 0.0)
        di = db * m["mult_eff"] * xc
        dxc = db * m["mult_eff"] * i
        live = m["valid"] & jnp.logical_not(m["first"])
        dm = jnp.where(live, db * i * xc, 0.0)
        dla = da * m["a"] - dm * (m["e2"] * m["rs"])
        dr = dla * (-C_RGLRU * m["sp"])
        dsp = jnp.sum(dla * (-C_RGLRU * r), axis=0, keepdims=True)
        dpr = (dr * r * (1.0 - r))
        dpi = (di * i * (1.0 - i))
        dprb, dpib = dpr.astype(BF16), dpi.astype(BF16)
        dxc = dxc + _dot_nt(dprb, wa) + _dot_nt(dpib, wx)
        du = (cw[3:4, :] * dxc + cw[2:3, :] * pltpu.roll(dxc, lp - 1, 0) + cw[1:2, :] * pltpu.roll(dxc, lp - 2, 0)
              + cw[0:1, :] * pltpu.roll(dxc, lp - 3, 0))
        du_ref[...] = jnp.where(m["valid"], du, 0.0).astype(BF16)
        tap = lax.broadcasted_iota(jnp.int32, (CONV_K, LRU_TILE), 0)
        dcw = jnp.zeros((CONV_K, LRU_TILE), F32)
        for kk in range(CONV_K):
            shifted = u if kk == CONV_K - 1 else pltpu.roll(u, CONV_K - 1 - kk, 0)
            dcw = jnp.where(tap == kk, jnp.sum(dxc * shifted, axis=0, keepdims=True), dcw)
        parts = [(dcw_ref, dcw[None]), (dcb_ref, jnp.sum(dxc, axis=0, keepdims=True)[None]),
                 (dwa_ref, _dot_tn(m["xcb"], dprb)[None]), (dba_ref, jnp.sum(dpr, axis=0, keepdims=True)[None]),
                 (dwx_ref, _dot_tn(m["xcb"], dpib)[None]), (dbx_ref, jnp.sum(dpi, axis=0, keepdims=True)[None]),
                 (dlam_ref, (dsp * (-jax.nn.sigmoid(-lam)))[None])]

        @pl.when(b_idx == 0)
        def _():
            for ref, val in parts:
                ref[...] = val

        @pl.when(b_idx != 0)
        def _():
            for ref, val in parts:
                ref[...] += val

    bshape = jax.ShapeDtypeStruct((rows, LRU_W), BF16)
    vec3 = pl.BlockSpec((1, 1, LRU_TILE), lambda t, b: (t, 0, 0))
    vshape = jax.ShapeDtypeStruct((N_LRU_TILES, 1, LRU_TILE), F32)
    mshape = jax.ShapeDtypeStruct((N_LRU_TILES, LRU_TILE, LRU_TILE), F32)
    return _hosted_call(
        body, name="lru_bwd", grid=(N_LRU_TILES, nb),
        in_specs=[seq(Z_U // LRU_TILE), seq(Z_G // LRU_TILE), seq(0), seq(0), cwspec, vec, mat, vec, mat, vec, vec],
        out_specs=[seq(0), seq(0), cwspec, vec3, mat, vec3, mat, vec3, vec3],
        out_shape=[bshape, bshape, jax.ShapeDtypeStruct((N_LRU_TILES, CONV_K, LRU_TILE), F32), vshape, mshape,
                   vshape, mshape, vshape, vshape],
        scratch_shapes=[pltpu.VMEM((lp, LRU_TILE), F32)] * 3,
        dims=("parallel", "arbitrary"), args=(z, z, hs, dy, cw, cb, wa, ba, wx, bx, lam), comm=comm)


def _mix_out_call(ya, yl, ga, gl, wout, h, tm):
    rows, d = h.shape

    def body(ya_ref, yl_ref, ga_ref, gl_ref, w_ref, h_ref, y_ref, o_ref):
        a = ya_ref[...]
        l = yl_ref[...]
        an = (a * _rms(a, MLA_W) * ga_ref[...]).astype(BF16)
        ln = (l * _rms(l, LRU_W) * gl_ref[...]).astype(BF16)
        y_ref[:, 0:MLA_W] = an
        y_ref[:, MLA_W:MLA_W + LRU_W] = ln
        o_ref[...] = h_ref[...] + _dot(an, w_ref[0:MLA_W, :]) + _dot(ln, w_ref[MLA_W:MLA_W + LRU_W, :])

    half = pl.BlockSpec((tm, MLA_W), lambda i: (i, 0))
    g = pl.BlockSpec((1, MLA_W), lambda i: (0, 0))
    full = pl.BlockSpec((tm, d), lambda i: (i, 0))
    return pl.pallas_call(
        body, name="mix_out", grid=(rows // tm,),
        in_specs=[half, half, g, g, pl.BlockSpec((MLA_W + LRU_W, d), lambda i: (0, 0)), full],
        out_specs=[full, full],
        out_shape=[jax.ShapeDtypeStruct((rows, MLA_W + LRU_W), BF16), jax.ShapeDtypeStruct((rows, d), F32)],
        compiler_params=_params(("parallel",)))(ya, yl, ga, gl, wout, h)


def _mix_out_bwd_call(dhb, wout, ya, yl, ga, gl, tm):
    rows = ya.shape[0]
    d = dhb.shape[1]

    def body(dh_ref, w_ref, ya_ref, yl_ref, ga_ref, gl_ref, dya_ref, dyl_ref, dga_ref, dgl_ref):
        dy = _dot_nt(dh_ref[...], w_ref[...])
        outs = []
        for val, g_ref, lo, out_ref in ((ya_ref[...], ga_ref, 0, dya_ref), (yl_ref[...], gl_ref, MLA_W, dyl_ref)):
            r = _rms(val, MLA_W)
            n = val * r
            dyn = dy[:, lo:lo + MLA_W]
            out_ref[...] = _rms_bwd(dyn * g_ref[...], n, r, MLA_W)
            outs.append(jnp.sum(dyn * n, axis=0, keepdims=True))

        @pl.when(pl.program_id(0) == 0)
        def _():
            dga_ref[...] = outs[0]
            dgl_ref[...] = outs[1]

        @pl.when(pl.program_id(0) != 0)
        def _():
            dga_ref[...] += outs[0]
            dgl_ref[...] += outs[1]

    half = pl.BlockSpec((tm, MLA_W), lambda i: (i, 0))
    g = pl.BlockSpec((1, MLA_W), lambda i: (0, 0))
    return pl.pallas_call(
        body, name="mix_out_bwd", grid=(rows // tm,),
        in_specs=[pl.BlockSpec((tm, d), lambda i: (i, 0)), pl.BlockSpec((MLA_W + LRU_W, d), lambda i: (0, 0)),
                  half, half, g, g],
        out_specs=[half, half, g, g],
        out_shape=[jax.ShapeDtypeStruct((rows, MLA_W), F32), jax.ShapeDtypeStruct((rows, LRU_W), F32),
                   jax.ShapeDtypeStruct((1, MLA_W), F32), jax.ShapeDtypeStruct((1, LRU_W), F32)],
        compiler_params=_params(("arbitrary",)))(dhb, wout, ya, yl, ga, gl)


def _final_call(h, g, target, lp, tm):
    rows, d = h.shape
    tpe = lp // tm

    def body(h_ref, g_ref, t_ref, dh_ref, dhb_ref, dg_ref, loss_ref):
        i = pl.program_id(0)
        x = h_ref[...]
        g = g_ref[...]
        r = _rms(x, d)
        n = x * r
        row = (i % tpe) * tm + lax.broadcasted_iota(jnp.int32, (tm, 1), 0)
        err = jnp.where(row >= CHUNK, n * g - t_ref[...], 0.0)
        dout = err * (1.0 / d)
        dh = _rms_bwd(dout * g, n, r, d)
        dh_ref[...] = dh
        dhb_ref[...] = dh.astype(BF16)
        dg = jnp.sum(dout * n, axis=0, keepdims=True)
        part = jnp.sum(jnp.sum(err * err, axis=1, keepdims=True), axis=0, keepdims=True) * (0.5 / d)
        loss = jnp.broadcast_to(part, (1, 128))

        @pl.when(i == 0)
        def _():
            dg_ref[...] = dg
            loss_ref[...] = loss

        @pl.when(i != 0)
        def _():
            dg_ref[...] += dg
            loss_ref[...] += loss

    full = pl.BlockSpec((tm, d), lambda i: (i, 0))
    return pl.pallas_call(
        body, name="final_loss", grid=(rows // tm,),
        in_specs=[full, pl.BlockSpec((1, d), lambda i: (0, 0)), full],
        out_specs=[full, full, pl.BlockSpec((1, d), lambda i: (0, 0)), pl.BlockSpec((1, 128), lambda i: (0, 0))],
        out_shape=[jax.ShapeDtypeStruct((rows, d), F32), jax.ShapeDtypeStruct((rows, d), BF16),
                   jax.ShapeDtypeStruct((1, d), F32), jax.ShapeDtypeStruct((1, 128), F32)],
        compiler_params=_params(("arbitrary",)))(h, g, target)


def _ffn_dact_call(name, dhb, wd, gate, up, tm, comm=None):
    rows, d = dhb.shape
    ns, fs, _ = wd.shape

    nsub = 2 if tm % 32 == 0 else 1
    sub = tm // nsub

    def body(dh_ref, wd_ref, g_ref, p_ref, dg_ref, dp_ref):
        wd = wd_ref[0]
        for r in range(nsub):
            rs = slice(r * sub, (r + 1) * sub)
            da = 0.5 * _dot_nt(dh_ref[rs, :], wd)
            g = g_ref[0, rs, :].astype(F32)
            p = p_ref[0, rs, :].astype(F32)
            sg = jax.nn.sigmoid(g)
            dg_ref[0, rs, :] = (da * p * sg * (1.0 + g * (1.0 - sg))).astype(BF16)
            dp_ref[0, rs, :] = (da * g * sg).astype(BF16)

    aspec = pl.BlockSpec((1, tm, fs), lambda s, i: (s, i, 0))
    oshape = jax.ShapeDtypeStruct((ns, rows, fs), BF16)
    return _hosted_call(
        body, name=name, grid=(ns, rows // tm),
        in_specs=[pl.BlockSpec((tm, d), lambda s, i: (i, 0)), pl.BlockSpec((1, fs, d), lambda s, i: (s, 0, 0)),
                  aspec, aspec],
        out_specs=[aspec, aspec], out_shape=[oshape, oshape],
        dims=("parallel", "parallel"), args=(dhb, wd, gate, up), comm=comm)


def _norm_in_bwd_call(name, pieces, h, g, dres, tm, comm=None):
    rows, d = h.shape
    npc = len(pieces)

    def body(*refs):
        d_refs = refs[0:2 * npc:2]
        w_refs = refs[1:2 * npc:2]
        h_ref, g_ref, dres_ref, dh_ref, dhb_ref, dg_ref = refs[2 * npc:]
        du = jnp.zeros((tm, d), F32)
        for d_ref, w_ref in zip(d_refs, w_refs):
            if len(d_ref.shape) == 3:
                for s in range(d_ref.shape[0]):
                    du = du + _dot(d_ref[s], w_ref[s])
            else:
                du = du + _dot(d_ref[...], w_ref[...])
        x = h_ref[...]
        r = _rms(x, d)
        n = x * r
        dh = dres_ref[...] + _rms_bwd(du * g_ref[...], n, r, d)
        dh_ref[...] = dh
        dhb_ref[...] = dh.astype(BF16)
        dg = jnp.sum(du * n, axis=0, keepdims=True)

        @pl.when(pl.program_id(0) == 0)
        def _():
            dg_ref[...] = dg

        @pl.when(pl.program_id(0) != 0)
        def _():
            dg_ref[...] += dg

    in_specs, args = [], []
    for dd, w in pieces:
        if dd.ndim == 3:
            in_specs.append(pl.BlockSpec((dd.shape[0], tm, dd.shape[2]), lambda i: (0, i, 0)))
            in_specs.append(pl.BlockSpec(w.shape, lambda i: (0, 0, 0)))
        else:
            in_specs.append(pl.BlockSpec((tm, dd.shape[1]), lambda i: (i, 0)))
            in_specs.append(pl.BlockSpec(w.shape, lambda i: (0, 0)))
        args += [dd, w]
    full = pl.BlockSpec((tm, d), lambda i: (i, 0))
    gspec = pl.BlockSpec((1, d), lambda i: (0, 0))
    return _hosted_call(
        body, name=name, grid=(rows // tm,),
        in_specs=in_specs + [full, gspec, full],
        out_specs=[full, full, gspec],
        out_shape=[jax.ShapeDtypeStruct((rows, d), F32), jax.ShapeDtypeStruct((rows, d), BF16),
                   jax.ShapeDtypeStruct((1, d), F32)],
        args=(*args, h, g, dres), comm=comm)


def _wgrad_call(name, a, b, scale=1.0, comm=None):
    a3, b3 = a.ndim == 3, b.ndim == 3
    ns = a.shape[0] if a3 else (b.shape[0] if b3 else 1)
    rows, m = a.shape[-2:]
    n = b.shape[-1]
    tmm = m if a3 else _col_tile(m, 256)

    def body(a_ref, b_ref, o_ref):
        av = a_ref[0] if a3 else a_ref[...]
        bv = b_ref[0] if b3 else b_ref[...]
        res = _dot_tn(av, bv)
        if scale != 1.0:
            res = res * scale
        if a3 or b3:
            o_ref[0] = res
        else:
            o_ref[...] = res

    aspec = (pl.BlockSpec((1, rows, tmm), lambda s, j: (s, 0, j)) if a3
             else pl.BlockSpec((rows, tmm), lambda s, j: (0, j)))
    bspec = (pl.BlockSpec((1, rows, n), lambda s, j: (s, 0, 0)) if b3
             else pl.BlockSpec((rows, n), lambda s, j: (0, 0)))
    if a3 or b3:
        ospec = pl.BlockSpec((1, tmm, n), lambda s, j: (s, j, 0))
        oshape = jax.ShapeDtypeStruct((ns, m, n), F32)
    else:
        ospec = pl.BlockSpec((tmm, n), lambda s, j: (j, 0))
        oshape = jax.ShapeDtypeStruct((m, n), F32)
    return _hosted_call(
        body, name=name, grid=(ns, m // tmm), in_specs=[aspec, bspec], out_specs=[ospec], out_shape=[oshape],
        dims=("parallel", "parallel"), args=(a, b), comm=comm)[0]


def _mla_prep_bwd_call(z, dq, dk, dv, gql, gkvl, gqh, gkh, wuq, wuk, wuv, tabs, lp, tm, comm=None):
    rows = z.shape[0]
    tpe = lp // tm

    def body(z_ref, dq_ref, dk_ref, dv_ref, gql_ref, gkvl_ref, gqh_ref, gkh_ref, wuq_ref, wuk_ref, wuv_ref,
             c_ref, s1_ref, s2_ref, dz_ref, dqp_ref, dkn_ref, dvv_ref, dgql_ref, dgkvl_ref, dgqh_ref, dgkh_ref):
        gql, gkvl = gql_ref[...], gkvl_ref[...]
        gq, gk = gqh_ref[...], gkh_ref[...]
        wuq, wuk, wuv = wuq_ref[...], wuk_ref[...], wuv_ref[...]
        m = _mla_heads(z_ref[...], gql, gkvl, wuq, wuk, wuv)
        c, s1, s2 = c_ref[...], s1_ref[...], s2_ref[...]
        dgq = jnp.zeros((1, D_QKP), F32)
        dgk = jnp.zeros((1, D_QKP), F32)
        dkr = jnp.zeros((tm, D_QKP - D_NOPE), F32)
        for hd in range(HEADS):
            qn, rqh, knn, krn, rkh = m["heads"][hd]
            dqg = jnp.concatenate([dq_ref[hd, :, 0:D_NOPE].astype(F32),
                                   _rope_t(dq_ref[hd, :, D_NOPE:D_QKP].astype(F32), c, s1, s2)], axis=1)
            dgq = dgq + jnp.sum(dqg * qn, axis=0, keepdims=True)
            dqn = dqg * gq
            dqr = rqh * (dqn - qn * (jnp.sum(dqn * qn, axis=-1, keepdims=True) * (1.0 / D_QK)))
            dqp_ref[:, hd * D_QKP:(hd + 1) * D_QKP] = dqr.astype(BF16)
            kn_full = jnp.concatenate([knn, krn], axis=1)
            dkg = jnp.concatenate([dk_ref[hd, :, 0:D_NOPE].astype(F32),
                                   _rope_t(dk_ref[hd, :, D_NOPE:D_QKP].astype(F32), c, s1, s2)], axis=1)
            dgk = dgk + jnp.sum(dkg * kn_full, axis=0, keepdims=True)
            dkn = dkg * gk
            dkraw = rkh * (dkn - kn_full * (jnp.sum(dkn * kn_full, axis=-1, keepdims=True) * (1.0 / D_QK)))
            dkn_ref[:, hd * D_NOPE:(hd + 1) * D_NOPE] = dkraw[:, 0:D_NOPE].astype(BF16)
            dkr = dkr + dkraw[:, D_NOPE:D_QKP]
            dvv_ref[:, hd * D_V:(hd + 1) * D_V] = dv_ref[hd]
        dcqn = _dot(dqp_ref[...], wuq)
        dckvn = _dot_nt(dkn_ref[...], wuk) + _dot_nt(dvv_ref[...], wuv)
        dz_ref[:, 0:Q_RANK] = _rms_bwd(dcqn * gql, m["nq"], m["rq"], Q_RANK).astype(BF16)
        dz_ref[:, Q_RANK:Z_KR] = _rms_bwd(dckvn * gkvl, m["nkv"], m["rkv"], KV_RANK).astype(BF16)
        dz_ref[:, Z_KR:Z_MLA] = dkr.astype(BF16)
        parts = [(dgql_ref, jnp.sum(dcqn * m["nq"], axis=0, keepdims=True)),
                 (dgkvl_ref, jnp.sum(dckvn * m["nkv"], axis=0, keepdims=True)), (dgqh_ref, dgq), (dgkh_ref, dgk)]

        @pl.when(pl.program_id(0) == 0)
        def _():
            for ref, val in parts:
                ref[...] = val

        @pl.when(pl.program_id(0) != 0)
        def _():
            for ref, val in parts:
                ref[...] += val

    def const(shape):
        return pl.BlockSpec(shape, lambda i: tuple(0 for _ in shape))

    tab = pl.BlockSpec((tm, 128), lambda i: (i % tpe, 0))
    hq = pl.BlockSpec((HEADS, tm, D_QKP), lambda i: (0, i, 0))
    hv = pl.BlockSpec((HEADS, tm, D_V), lambda i: (0, i, 0))

    def rowspec(n):
        return pl.BlockSpec((tm, n), lambda i: (i, 0))

    return _hosted_call(
        body, name="mla_prep_bwd", grid=(rows // tm,),
        in_specs=[rowspec(Z_MLA), hq, hq, hv, const((1, Q_RANK)), const((1, KV_RANK)), const((1, D_QKP)),
                  const((1, D_QKP)), const((HEADS * D_QKP, Q_RANK)), const((KV_RANK, HEADS * D_NOPE)),
                  const((KV_RANK, HEADS * D_V)), tab, tab, tab],
        out_specs=[rowspec(Z_MLA), rowspec(HEADS * D_QKP), rowspec(HEADS * D_NOPE), rowspec(HEADS * D_V),
                   const((1, Q_RANK)), const((1, KV_RANK)), const((1, D_QKP)), const((1, D_QKP))],
        out_shape=[jax.ShapeDtypeStruct((rows, Z_MLA), BF16), jax.ShapeDtypeStruct((rows, HEADS * D_QKP), BF16),
                   jax.ShapeDtypeStruct((rows, HEADS * D_NOPE), BF16), jax.ShapeDtypeStruct((rows, HEADS * D_V), BF16),
                   jax.ShapeDtypeStruct((1, Q_RANK), F32), jax.ShapeDtypeStruct((1, KV_RANK), F32),
                   jax.ShapeDtypeStruct((1, D_QKP), F32), jax.ShapeDtypeStruct((1, D_QKP), F32)],
        args=(z, dq, dk, dv, gql, gkvl, gqh, gkh, wuq, wuk, wuv, *tabs), comm=comm)


def _local_step(h0, target, w, nb, lp, sched=None):
    tm = _row_tile(lp, 1024)
    te = _row_tile(lp, 512)
    tabs = _rope_tables(lp)
    g = {}
    if sched is None:
        host = lambda stage: None
    else:
        sched.g = g
        host = sched.host

    def ffn_fwd(tag, h, split):
        u = _rmsnorm_call(tag + "_norm", h, w[tag + "_norm"], te)
        if split:
            gate = _ffn_gate_call(tag + "_gate", u, w[tag + "_w_gate"], tm, host(tag + "_gate"))
            up, act = _ffn_upact_call(tag + "_upact", u, w[tag + "_w_up"], gate, tm, host(tag + "_upact"))
        else:
            gate, up, act = _ffn_up_call(tag + "_up", u, w[tag + "_w_gate"], w[tag + "_w_up"], tm, host(tag + "_up"))
        return _ffn_down_call(tag + "_down", act, w[tag + "_w_down"], h, te, host(tag + "_down")), (u, gate, up, act)

    def ffn_bwd(tag, h, saved, dh, dhb):
        u, gate, up, act = saved
        dgate, dup = _ffn_dact_call(tag + "_dact", dhb, w[tag + "_w_down"], gate, up, tm, host(tag + "_dact"))
        g[tag + "_w_down"] = _wgrad_call(tag + "_dwd", act, dhb, 0.5, host(tag + "_dwd"))
        g[tag + "_w_gate"] = _wgrad_call(tag + "_dwg", dgate, u, 1.0, host(tag + "_dwg"))
        g[tag + "_w_up"] = _wgrad_call(tag + "_dwu", dup, u, 1.0, host(tag + "_dwu"))
        dh_in, dhb_in, g[tag + "_norm"] = _norm_in_bwd_call(
            tag + "_din", [(dgate, w[tag + "_w_gate"]), (dup, w[tag + "_w_up"])], h, w[tag + "_norm"], dh, te,
            host(tag + "_din"))
        return dh_in, dhb_in

    h1, s1 = ffn_fwd("ffn1", h0, True)
    un = _rmsnorm_call("mix_norm", h1, w["mix_norm"], te)
    z = _mm_call("mix_in", un, w["w_in"], tm, F32)
    mla_w = (w["q_latent_norm"], w["kv_latent_norm"], w["q_head_norm"], w["k_head_norm"], w["w_uq"], w["w_uk"],
             w["w_uv"])
    q, k, v, cqn, ckvn = _mla_prep_call(z, *mla_w, tabs, lp, te)
    o, lse = _attn_fwd_call(q, k, v, nb, lp, host("attn_fwd"))
    lru_w = (w["conv_w"], w["conv_b"], w["gate_a_w"], w["gate_a_b"], w["gate_x_w"], w["gate_x_b"], w["lru_lambda"])
    yl, hs = _lru_fwd_call(z, *lru_w, nb, lp, host("lru_fwd"))
    y, h2 = _mix_out_call(o, yl, w["attn_out_norm"], w["lru_out_norm"], w["w_out"], h1, te)
    h3, s2 = ffn_fwd("ffn2", h2, False)
    dh3, dh3b, g["final_norm"], loss = _final_call(h3, w["final_norm"], target, lp, te)
    g["loss"] = loss

    dh2, dh2b = ffn_bwd("ffn2", h2, s2, dh3, dh3b)
    g["w_out"] = _wgrad_call("dw_out", y, dh2b)
    dya, dyl, g["attn_out_norm"], g["lru_out_norm"] = _mix_out_bwd_call(
        dh2b, w["w_out"], o, yl, w["attn_out_norm"], w["lru_out_norm"], te)
    dq, dk, dv = _attn_bwd_call(q, k, v, o, lse, dya, nb, lp, host("attn_bwd"))
    (dz_mla, dqp, dkn, dvv, g["q_latent_norm"], g["kv_latent_norm"], g["q_head_norm"],
     g["k_head_norm"]) = _mla_prep_bwd_call(z, dq, dk, dv, *mla_w, tabs, lp, te, host("mla_prep_bwd"))
    g["w_uq"] = _wgrad_call("dw_uq", dqp, cqn)
    g["w_uk"] = _wgrad_call("dw_uk", ckvn, dkn)
    g["w_uv"] = _wgrad_call("dw_uv", ckvn, dvv)
    (du, dgt, g["conv_w"], g["conv_b"], g["gate_a_w"], g["gate_a_b"], g["gate_x_w"], g["gate_x_b"],
     g["lru_lambda"]) = _lru_bwd_call(z, hs, dyl, *lru_w, nb, lp, host("lru_bwd"))
    win = w["w_in"]
    g["w_in"] = jnp.concatenate(
        [_wgrad_call("dw_in_mla", dz_mla, un), _wgrad_call("dw_in_u", du, un), _wgrad_call("dw_in_g", dgt, un)],
        axis=0)
    dh1, dh1b, g["mix_norm"] = _norm_in_bwd_call(
        "mix_din", [(dz_mla, win[0:Z_MLA]), (du, win[Z_U:Z_G]), (dgt, win[Z_G:Z_W])], h1, w["mix_norm"], dh2, te,
        host("mix_din"))
    dh0, _ = ffn_bwd("ffn1", h0, s1, dh1, dh1b)
    return loss, dh0, g


def _place():
    x, y, c = lax.axis_index("x"), lax.axis_index("y"), lax.axis_index("c")
    return x, y, c, [(1 - x, y), (x, 1 - y), (1 - x, 1 - y)]


def _any_specs(n):
    return [pl.BlockSpec(memory_space=pl.ANY)] * n


def _remote(src, dst, sems, k, dev):
    send_sems, recv_sems, base = sems
    return pltpu.make_async_remote_copy(src_ref=src, dst_ref=dst, send_sem=send_sems.at[base + k],
                                        recv_sem=recv_sems.at[base + k], device_id=dev, device_id_type=MESH)


EW_VMEM_BYTES = 24 * 1024 * 1024


def _fit_rows(rows, cols, blocks):
    return _row_tile(rows, max(16, int(EW_VMEM_BYTES // (8 * blocks)) // cols))


class _Geom:
    def __init__(self, n0, n1, blocks=1.0):
        self.n0, self.n1 = n0, n1
        self.axis = 0 if n0 % 32 == 0 else 1
        self.h0, self.h1 = (n0 // 2, n1) if self.axis == 0 else (n0, n1 // 2)
        self.tr = _fit_rows(self.h0, self.h1, blocks)
        self.nblk = self.h0 // self.tr

    def half_ref(self, ref, lead, idx):
        if self.axis == 0:
            return ref.at[(*lead, pl.ds(idx * self.h0, self.h0))]
        return ref.at[(*lead, slice(None), pl.ds(idx * self.h1, self.h1))]

    def half_block(self, lead, i, idx):
        return (*lead, idx * self.nblk + i, 0) if self.axis == 0 else (*lead, i, idx)


class _Comm:
    def __init__(self, ins, out_shapes, aliases, n_sems, start, finish, deliver):
        self.ins, self.out_shapes, self.aliases, self.n_sems = list(ins), list(out_shapes), dict(aliases), n_sems
        self.start, self.finish, self.deliver = start, finish, deliver

    def scratch(self):
        return [pltpu.SemaphoreType.DMA((self.n_sems,)), pltpu.SemaphoreType.DMA((self.n_sems,))]


def _comm_call(name, comm):
    n_in = len(comm.ins)

    def body(*refs):
        ins, outs, sems = refs[:n_in], refs[n_in:-2], (*refs[-2:], 0)
        comm.start(ins, outs, sems)
        comm.finish(ins, outs, sems)

    res = pl.pallas_call(
        body, name=name, out_shape=comm.out_shapes, in_specs=_any_specs(n_in),
        out_specs=_any_specs(len(comm.out_shapes)), input_output_aliases=comm.aliases,
        scratch_shapes=comm.scratch())(*comm.ins)
    return comm.deliver(list(res))


def _hosted_call(body, *, name, grid, in_specs, out_specs, out_shape, args, scratch_shapes=(), dims=None, comm=None):
    in_specs, out_specs, out_shape = list(in_specs), list(out_specs), list(out_shape)
    if comm is None:
        return pl.pallas_call(
            body, name=name, grid=grid, in_specs=in_specs, out_specs=out_specs, out_shape=out_shape,
            scratch_shapes=list(scratch_shapes),
            compiler_params=_params(dims or ("arbitrary",) * len(grid)))(*args)
    n_in, n_out, n_ci, n_co = len(in_specs), len(out_specs), len(comm.ins), len(comm.out_shapes)

    def wrapped(*refs):
        ins, cins = refs[:n_in], refs[n_in:n_in + n_ci]
        outs = refs[n_in + n_ci:n_in + n_ci + n_out]
        couts = refs[n_in + n_ci + n_out:n_in + n_ci + n_out + n_co]
        scratch, sems = refs[n_in + n_ci + n_out + n_co:-2], (*refs[-2:], 0)
        first = functools.reduce(jnp.logical_and, [pl.program_id(k) == 0 for k in range(len(grid))])
        last = functools.reduce(jnp.logical_and, [pl.program_id(k) == grid[k] - 1 for k in range(len(grid))])

        @pl.when(first)
        def _():
            comm.start(cins, couts, sems)

        body(*ins, *outs, *scratch)

        @pl.when(last)
        def _():
            comm.finish(cins, couts, sems)

    res = pl.pallas_call(
        wrapped, name=name, grid=grid, in_specs=in_specs + _any_specs(n_ci), out_specs=out_specs + _any_specs(n_co),
        out_shape=out_shape + comm.out_shapes,
        input_output_aliases={n_in + i: n_out + o for i, o in comm.aliases.items()},
        scratch_shapes=list(scratch_shapes) + comm.scratch(),
        compiler_params=_params(("arbitrary",) * len(grid)))(*args, *comm.ins)
    comm.deliver(list(res[n_out:]))
    return list(res[:n_out])


def _gather_comm(bufs, deliver):
    n = len(bufs)
    geoms = [_Geom(*b.shape[1:]) for b in bufs]

    def first(outs, sems):
        x, y, c, chips = _place()
        cps = []
        for a in range(n):
            mine = geoms[a].half_ref(outs[a], (2 * x + y,), c)
            cps += [_remote(mine, mine, sems, 6 * a + j, (cx, cy, c)) for j, (cx, cy) in enumerate(chips)]
        return cps

    def start(ins, outs, sems):
        for cp in first(outs, sems):
            cp.start()

    def finish(ins, outs, sems):
        x, y, c, chips = _place()
        sib = (x, y, 1 - c)
        passed = []
        for a in range(n):
            for j, (cx, cy) in enumerate(chips):
                land = geoms[a].half_ref(outs[a], (2 * cx + cy,), c)
                _remote(land, land, sems, 6 * a + j, sib).wait_recv()
                cp = _remote(land, land, sems, 6 * a + 3 + j, sib)
                cp.start()
                passed.append(cp)
        for a in range(n):
            for j, (cx, cy) in enumerate(chips):
                land = geoms[a].half_ref(outs[a], (2 * cx + cy,), 1 - c)
                _remote(land, land, sems, 6 * a + 3 + j, sib).wait_recv()
        for cp in first(outs, sems) + passed:
            cp.wait_send()

    return _Comm(bufs, [jax.ShapeDtypeStruct(b.shape, b.dtype) for b in bufs], {a: a for a in range(n)}, 6 * n,
                 start, finish, deliver)


def _reduce_pair_comm(grads, deliver):
    n = len(grads)
    geoms = [_Geom(*a.shape[1:]) for a in grads]

    def copies(ins, outs, sems):
        x, y, c, _ = _place()
        return [_remote(geoms[a].half_ref(ins[a], (slice(None),), 1 - c), outs[a], sems, a, (x, y, 1 - c))
                for a in range(n)]

    def start(ins, outs, sems):
        for cp in copies(ins, outs, sems):
            cp.start()

    def finish(ins, outs, sems):
        cps = copies(ins, outs, sems)
        for cp in cps:
            cp.wait_recv()
        for cp in cps:
            cp.wait_send()

    shapes = [jax.ShapeDtypeStruct((N_SHARD, g.h0, g.h1), a.dtype) for a, g in zip(grads, geoms)]
    return _Comm(grads, shapes, {}, n, start, finish, deliver)


def _reduce_chips_comm(parts, deliver):
    n = len(parts)

    def copies(ins, outs, sems):
        x, y, c, chips = _place()
        return [_remote(ins[a].at[2 * cx + cy], outs[a].at[j], sems, 3 * a + j, (cx, cy, c))
                for a in range(n) for j, (cx, cy) in enumerate(chips)]

    def start(ins, outs, sems):
        for cp in copies(ins, outs, sems):
            cp.start()

    def finish(ins, outs, sems):
        cps = copies(ins, outs, sems)
        for cp in cps:
            cp.wait_recv()
        for cp in cps:
            cp.wait_send()

    shapes = [jax.ShapeDtypeStruct((3,) + a.shape[1:], a.dtype) for a in parts]
    return _Comm(parts, shapes, {}, 3 * n, start, finish, deliver)


def _share_pair_comm(bufs, deliver):
    n = len(bufs)
    geoms = [_Geom(*b.shape) for b in bufs]

    def copies(outs, sems):
        x, y, c, _ = _place()
        cps = []
        for a in range(n):
            mine = geoms[a].half_ref(outs[a], (), c)
            cps.append(_remote(mine, mine, sems, a, (x, y, 1 - c)))
        return cps

    def start(ins, outs, sems):
        for cp in copies(outs, sems):
            cp.start()

    def finish(ins, outs, sems):
        x, y, c, _ = _place()
        for a in range(n):
            land = geoms[a].half_ref(outs[a], (), 1 - c)
            _remote(land, land, sems, a, (x, y, 1 - c)).wait_recv()
        for cp in copies(outs, sems):
            cp.wait_send()

    return _Comm(bufs, [jax.ShapeDtypeStruct(b.shape, b.dtype) for b in bufs], {a: a for a in range(n)}, n,
                 start, finish, deliver)


def _small_comm(pack, deliver):
    r, d = pack.shape

    def copies(ins, outs, sems):
        x, y, c, _ = _place()
        cps = []
        for k in range(1, 8):
            peer = (x ^ ((k >> 2) & 1), y ^ ((k >> 1) & 1), c ^ (k & 1))
            cps.append(_remote(ins[0], outs[0].at[4 * x + 2 * y + c], sems, k - 1, peer))
        return cps

    def start(ins, outs, sems):
        for cp in copies(ins, outs, sems):
            cp.start()

    def finish(ins, outs, sems):
        cps = copies(ins, outs, sems)
        for cp in cps:
            cp.wait_recv()
        for cp in cps:
            cp.wait_send()

    return _Comm([pack], [jax.ShapeDtypeStruct((8, r, d), pack.dtype)], {}, 7, start, finish, deliver)


def _join_comms(comms):
    comms = [c for c in comms if c is not None]
    if len(comms) <= 1:
        return comms[0] if comms else None
    ins, out_shapes, aliases, spans, n_sems = [], [], {}, [], 0
    for c in comms:
        aliases.update({len(ins) + i: len(out_shapes) + o for i, o in c.aliases.items()})
        spans.append((len(ins), len(ins) + len(c.ins), len(out_shapes), len(out_shapes) + len(c.out_shapes), n_sems))
        ins += c.ins
        out_shapes += c.out_shapes
        n_sems += c.n_sems

    def run(which):
        def go(all_ins, all_outs, sems):
            for c, (i0, i1, o0, o1, base) in zip(comms, spans):
                getattr(c, which)(all_ins[i0:i1], all_outs[o0:o1], (sems[0], sems[1], sems[2] + base))
        return go

    def deliver(outs):
        for c, (_, _, o0, o1, _) in zip(comms, spans):
            c.deliver(outs[o0:o1])
        return outs

    return _Comm(ins, out_shapes, aliases, n_sems, run("start"), run("finish"), deliver)


def _ew_call(name, fn, ins, out_dtypes):
    shape = ins[0].shape
    cols = shape[-1]
    rows = 1
    for s_ in shape[:-1]:
        rows *= s_
    ins2 = [a.reshape(rows, cols) for a in ins]
    tr = rows
    for t in range(16, min(rows, max(16, (1 << 19) // cols)) + 1, 16):
        if rows % t == 0:
            tr = t
    no = len(out_dtypes)

    def body(*refs):
        outs = fn(*[r[...] for r in refs[:len(ins2)]])
        for ref, val in zip(refs[len(ins2):], outs):
            ref[...] = val.astype(ref.dtype)

    spec = pl.BlockSpec((tr, cols), lambda i: (i, 0))
    res = pl.pallas_call(
        body, name=name, grid=(rows // tr,), in_specs=[spec] * len(ins2), out_specs=[spec] * no,
        out_shape=[jax.ShapeDtypeStruct((rows, cols), dt) for dt in out_dtypes],
        compiler_params=_params(("parallel",)))(*ins2)
    return [r.reshape(shape) for r in res]


def _adamw_math(w, g, m, v):
    m = ADAM_B1 * m + (1.0 - ADAM_B1) * g
    v = ADAM_B2 * v + (1.0 - ADAM_B2) * (g * g)
    m_hat = m / (1.0 - ADAM_B1 ** ADAM_STEP)
    v_hat = v / (1.0 - ADAM_B2 ** ADAM_STEP)
    delta = -ADAM_LR * (m_hat / (jnp.sqrt(v_hat) + ADAM_EPS) + ADAM_WD * w)
    return delta, m, v


def _adamw_call(name, w, g, m, v):
    return _ew_call(name, _adamw_math, [w, g, m, v], [F32, F32, F32])


def _tiled_call(name, fn, place, grid, in_items, out_items):
    ni = len(in_items)

    def body(place_ref, *refs):
        vals = fn(*[r[...] for r in refs[:ni]])
        for ref, val in zip(refs[ni:], vals):
            ref[...] = val.astype(ref.dtype)

    spec = pltpu.PrefetchScalarGridSpec(
        num_scalar_prefetch=1, grid=grid,
        in_specs=[pl.BlockSpec(blk, imap) for _, blk, imap in in_items],
        out_specs=[pl.BlockSpec(blk, imap) for _, _, blk, imap in out_items])
    return pl.pallas_call(
        body, name=name, grid_spec=spec,
        out_shape=[jax.ShapeDtypeStruct(shp, dt) for shp, dt, _, _ in out_items],
        compiler_params=_params(("arbitrary",) * len(grid)))(place, *[a for a, _, _ in in_items])


def _cast_call(name, place, shards):
    n0, n1 = shards[0].shape
    tr = _fit_rows(n0, n1, 1.5 * len(shards))
    ins = [(a, (tr, n1), lambda i, p: (i, 0)) for a in shards]
    outs = [((N_SHARD, n0, n1), BF16, (1, tr, n1), lambda i, p: (p[0], i, 0)) for _ in shards]
    return _tiled_call(name, lambda *v: [x[None] for x in v], place, (n0 // tr,), ins, outs)


def _pair_sum_call(name, place, fulls, gots):
    k = len(fulls)
    g = _Geom(*fulls[0].shape[1:], blocks=2.5 * k)
    blk = (1, g.tr, g.h1)
    ins = [(a, blk, lambda s, i, p: g.half_block((s,), i, p[1])) for a in fulls]
    ins += [(a, blk, lambda s, i, p: (s, i, 0)) for a in gots]
    outs = [((N_SHARD, g.h0, g.h1), BF16, blk, lambda s, i, p: (s, i, 0)) for _ in fulls]
    return _tiled_call(name, lambda *v: [v[j] + v[k + j] for j in range(k)], place, (N_SHARD, g.nblk), ins, outs)


def _chip_sum_call(name, place, fulls, gots, recvs):
    k = len(fulls)
    g = _Geom(*fulls[0].shape[1:], blocks=4.5 * k)
    blk = (1, g.tr, g.h1)
    ins = [(a, blk, lambda i, p: g.half_block((p[0],), i, p[1])) for a in fulls]
    ins += [(a, blk, lambda i, p: (p[0], i, 0)) for a in gots]
    ins += [(a, (3, g.tr, g.h1), lambda i, p: (0, i, 0)) for a in recvs]
    outs = [((g.n0, g.n1), F32, (g.tr, g.h1), lambda i, p: g.half_block((), i, p[1])) for _ in fulls]

    def fn(*v):
        res = []
        for j in range(k):
            r = v[2 * k + j].astype(F32)
            res.append(v[j][0] + v[k + j][0] + r[0] + r[1] + r[2])
        return res

    return _tiled_call(name, fn, place, (g.nblk,), ins, outs)


def _adamw_group_call(name, ws, gs, ms, vs, comm=None):
    k = len(ws)
    n0, n1 = ws[0].shape
    tr = _fit_rows(n0, n1, 8 * k)
    spec = pl.BlockSpec((tr, n1), lambda i: (i, 0))

    def body(*refs):
        for j in range(k):
            g = refs[k + j][...]
            delta, m, vv = _adamw_math(refs[j][...], g, refs[2 * k + j][...], refs[3 * k + j][...])
            for ref, val in zip(refs[4 * k + 4 * j:4 * k + 4 * j + 4], (g, delta, m, vv)):
                ref[...] = val

    flat = _hosted_call(
        body, name=name, grid=(n0 // tr,), in_specs=[spec] * (4 * k), out_specs=[spec] * (4 * k),
        out_shape=[jax.ShapeDtypeStruct((n0, n1), F32)] * (4 * k), dims=("parallel",),
        args=(*ws, *gs, *ms, *vs), comm=comm)
    return [flat[4 * j:4 * j + 4] for j in range(k)]


def _small_update_call(me, early, own_early, late, own_late, wp, mp, vp):
    nd, r, d = early.shape

    def body(me_ref, e_ref, oe_ref, l_ref, ol_ref, w_ref, m_ref, v_ref, gs_ref, d_ref, nm_ref, nv_ref):
        mine = me_ref[0]

        def total(g_ref, own_ref):
            acc = None
            for k in range(nd):
                part = jnp.where(mine == k, own_ref[...], g_ref[k])
                acc = part if acc is None else acc + part
            return acc

        gs = total(e_ref, oe_ref)
        ls = total(l_ref, ol_ref)
        gs_ref[...] = gs
        first = gs[0:8] + ls[0:8]
        gs_ref[0:8, :] = first
        gs_ref[ROW_META:ROW_META + N_META, :] = gs[ROW_META:ROW_META + N_META] + ls[8:8 + N_META]
        grads = jnp.concatenate([first, gs[8:SMALL_ADAM_ROWS]], axis=0)
        delta, m, v = _adamw_math(w_ref[...], grads, m_ref[...], v_ref[...])
        d_ref[...] = delta
        nm_ref[...] = m
        nv_ref[...] = v

    vm = pl.BlockSpec(memory_space=pltpu.VMEM)
    ashape = jax.ShapeDtypeStruct((SMALL_ADAM_ROWS, d), F32)
    return pl.pallas_call(
        body, name="small_update", in_specs=[pl.BlockSpec(memory_space=pltpu.SMEM)] + [vm] * 7, out_specs=[vm] * 4,
        out_shape=[jax.ShapeDtypeStruct((r, d), F32), ashape, ashape, ashape],
        compiler_params=pltpu.CompilerParams(vmem_limit_bytes=VMEM_LIMIT_BYTES))(
            me, early, own_early, late, own_late, wp, mp, vp)


SMALL_NAMES = ["ffn1_norm", "mix_norm", "ffn2_norm", "final_norm", "q_latent_norm", "kv_latent_norm",
               "q_head_norm", "k_head_norm", "conv_b", "gate_a_b", "gate_x_b", "lru_lambda", "attn_out_norm",
               "lru_out_norm"]
ROW_CONV_W = 14
ROW_GATE_A = 16
ROW_GATE_X = 48
ROW_META = 80
ROW_LOSS = 96


def _row(a):
    flat = a.reshape(1, -1)
    return jnp.pad(flat, ((0, 0), (0, D_MODEL - flat.shape[1])))


def _pack_small(t, rows):
    parts = [_row(t[nm]) for nm in SMALL_NAMES]
    parts.append(t["conv_w"].reshape(2, D_MODEL))
    parts.append(t["gate_a_w"].reshape(32, D_MODEL))
    parts.append(t["gate_x_w"].reshape(32, D_MODEL))
    p = jnp.concatenate(parts, axis=0)
    return jnp.pad(p, ((0, rows - p.shape[0]), (0, 0)))


def _early_pack(g):
    gs = {nm: g.get(nm, jnp.zeros((1, D_MODEL), F32)) for nm in SMALL_NAMES}
    gs["q_head_norm"] = g["q_head_norm"][:, 0:D_QK]
    gs["k_head_norm"] = g["k_head_norm"][:, 0:D_QK]
    for nm in ("conv_b", "gate_a_b", "gate_x_b", "lru_lambda"):
        gs[nm] = g[nm].reshape(1, LRU_W)
    gs["conv_w"] = g["conv_w"].transpose(1, 0, 2).reshape(CONV_K, LRU_W)
    gs["gate_a_w"] = _gate_blocks(g["gate_a_w"])
    gs["gate_x_w"] = _gate_blocks(g["gate_x_w"])
    return jnp.concatenate([_pack_small(gs, ROW_META), jnp.zeros((N_META, D_MODEL), F32), _row(g["loss"][:, 0:1]),
                            jnp.zeros((SMALL_ROWS - ROW_LOSS - 1, D_MODEL), F32)], axis=0)


def _unpack_small(p, like):
    out = {}
    for k, nm in enumerate(SMALL_NAMES):
        out[nm] = p[k, 0:like[nm].size].reshape(like[nm].shape)
    out["gate_a_w"] = p[ROW_GATE_A:ROW_GATE_A + 32].reshape(like["gate_a_w"].shape)
    out["gate_x_w"] = p[ROW_GATE_X:ROW_GATE_X + 32].reshape(like["gate_x_w"].shape)
    return out


def _gate_dense(wg):
    w4 = wg[0].reshape(N_LRU_TILES, 2, 64, 64)
    zero = jnp.zeros((N_LRU_TILES, 64, 64), wg.dtype)
    top = jnp.concatenate([w4[:, 0], zero], axis=2)
    bot = jnp.concatenate([zero, w4[:, 1]], axis=2)
    return jnp.concatenate([top, bot], axis=1).astype(BF16)


def _gate_blocks(dw):
    return jnp.stack([dw[:, 0:64, 0:64], dw[:, 64:128, 64:128]], axis=1).reshape(8, 64, 64)


BIG_NAMES = ["ffn1_w_gate", "ffn1_w_up", "ffn1_w_down", "w_in", "w_uq", "w_uk", "w_uv", "w_out", "ffn2_w_gate",
             "ffn2_w_up", "ffn2_w_down"]
BIG_GROUPS = [["ffn1_w_gate", "ffn1_w_up", "ffn1_w_down", "ffn2_w_gate", "ffn2_w_up", "ffn2_w_down"], ["w_in"],
              ["w_uq"], ["w_uk", "w_uv"], ["w_out"]]
TRANSPOSED = ("ffn1_w_gate", "ffn1_w_up", "ffn2_w_gate", "ffn2_w_up", "w_in", "w_uq")


def _to2d(nm, a):
    return a[0].T if nm in TRANSPOSED else a[0]


def _from2d(nm, a):
    return (a.T if nm in TRANSPOSED else a)[None]


WEIGHT_NAMES = ["meta_tokens", "ffn1_norm", "ffn1_w_gate", "ffn1_w_up", "ffn1_w_down", "mix_norm", "w_in",
                "q_latent_norm", "w_uq", "kv_latent_norm", "w_uk", "w_uv", "q_head_norm", "k_head_norm", "conv_w",
                "conv_b", "gate_a_w", "gate_a_b", "gate_x_w", "gate_x_b", "lru_lambda", "attn_out_norm",
                "lru_out_norm", "w_out", "ffn2_norm", "ffn2_w_gate", "ffn2_w_up", "ffn2_w_down", "final_norm"]


def _weight_from(nm, slots):
    if nm == "w_in":
        win = slots.reshape(IN_WIDTH, D_MODEL)
        return jnp.concatenate([win[0:Z_KR + D_ROPE], jnp.zeros((128 - D_ROPE, D_MODEL), BF16),
                                win[Z_KR + D_ROPE:]], axis=0)
    if nm == "w_uq":
        return jnp.pad(slots, ((0, 0), (0, D_QKP - D_QK), (0, 0))).reshape(HEADS * D_QKP, Q_RANK)
    if nm in ("w_uk", "w_uv"):
        return slots.transpose(1, 0, 2).reshape(KV_RANK, HEADS * D_NOPE)
    if nm == "w_out":
        return slots.reshape(D_MODEL, D_MODEL)
    return slots


def _small_weights(p, small):
    w = {nm: p[nm] for nm in SMALL_NAMES}
    w["q_head_norm"] = jnp.pad(p["q_head_norm"], ((0, 0), (0, D_QKP - D_QK)))
    w["k_head_norm"] = jnp.pad(p["k_head_norm"], ((0, 0), (0, D_QKP - D_QK)))
    w["conv_w"] = small[:, N_META:N_META + 2, :].reshape(N_SHARD, CONV_K, LRU_TILE)
    w["gate_a_w"] = _gate_dense(p["gate_a_w"])
    w["gate_x_w"] = _gate_dense(p["gate_x_w"])
    meta = small[:, 0:N_META, :].transpose(1, 0, 2).reshape(N_META, D_MODEL)
    return w, meta


def _full_weights(p, gathered, small):
    w, meta = _small_weights(p, small)
    w.update({nm: _weight_from(nm, gathered[nm]) for nm in BIG_NAMES})
    return w, meta


def _shard_grad(nm, g):
    if nm == "w_in":
        return jnp.concatenate([g[0:Z_KR + D_ROPE], g[Z_MLA:]], axis=0).reshape(N_SHARD, IN_WIDTH // N_SHARD, D_MODEL)
    if nm == "w_uq":
        return g.reshape(HEADS, D_QKP, Q_RANK)[:, 0:D_QK, :]
    if nm in ("w_uk", "w_uv"):
        return g.reshape(KV_RANK, HEADS, D_NOPE).transpose(1, 0, 2)
    if nm == "w_out":
        return g.reshape(N_SHARD, D_MODEL // N_SHARD, D_MODEL)
    return g


def _shard_grads(g):
    return {nm: _shard_grad(nm, g[nm]) for nm in BIG_NAMES}


GATHER_FIRST = ["ffn1_w_gate"]
GATHER_AT = {"ffn1_gate": ["ffn1_w_up"], "ffn1_upact": ["ffn1_w_down"],
             "ffn1_down": ["w_in", "w_uq", "w_uk", "w_uv", "w_out"], "attn_fwd": ["ffn2_w_down", "ffn2_w_gate"],
             "lru_fwd": ["ffn2_w_up"]}
PAIR_AT = [("ffn2_din", ["ffn2_w_gate", "ffn2_w_up", "ffn2_w_down"]),
           ("mix_din", ["w_out", "w_uq", "w_uk", "w_uv", "w_in"]),
           ("ffn1_dwg", ["ffn1_w_down"]), ("ffn1_dwu", ["ffn1_w_gate"]), ("ffn1_din", ["ffn1_w_up"])]
CHIPS_AT = [("attn_bwd", ["ffn2_w_down", "ffn2_w_gate"]), ("mla_prep_bwd", ["ffn2_w_up"]),
            ("ffn1_dact", ["w_out", "w_uq", "w_uk", "w_uv", "w_in"]),
            ("ffn1_dwu", ["ffn1_w_down"]), ("ffn1_din", ["ffn1_w_gate"]), (None, ["ffn1_w_up"])]
SMALL_EARLY_AT = "mix_din"


def _same_shape_groups(names):
    return [[nm for nm in grp if nm in names] for grp in BIG_GROUPS if any(nm in names for nm in grp)]


class _Sched:
    def __init__(self, place, w, slots):
        self.place, self.w, self.slots = place, w, slots
        self.g = None
        self.sharded, self.from_pair, self.chip_bf16, self.from_chips = {}, {}, {}, {}
        self.early = self.early_all = None

    def host(self, stage):
        comms = []
        if stage in GATHER_AT:
            comms.append(self.gather(GATHER_AT[stage]))
        comms += [self.chips(names) for at, names in CHIPS_AT if at == stage]
        comms += [self.pair(names) for at, names in PAIR_AT if at == stage]
        if stage == SMALL_EARLY_AT:
            comms.append(self.small_early())
        return _join_comms(comms)

    def small_early(self):
        self.early = _early_pack(self.g)

        def deliver(outs):
            self.early_all = outs[0]
            return outs

        return _small_comm(self.early, deliver)

    def gather(self, names):
        def deliver(outs):
            self.w.update({nm: _weight_from(nm, o) for nm, o in zip(names, outs)})
            return outs

        return _gather_comm([self.slots[nm] for nm in names], deliver)

    def pair(self, names):
        self.sharded.update({nm: _shard_grad(nm, self.g[nm]) for nm in names})

        def deliver(outs):
            self.from_pair.update(zip(names, outs))
            for grp in _same_shape_groups(names):
                sums = _pair_sum_call("pair_sum_" + grp[0], self.place, [self.sharded[nm] for nm in grp],
                                      [self.from_pair[nm] for nm in grp])
                self.chip_bf16.update(zip(grp, sums))
            return outs

        return _reduce_pair_comm([self.sharded[nm] for nm in names], deliver)

    def chips(self, names):
        def deliver(outs):
            self.from_chips.update(zip(names, outs))
            return outs

        return _reduce_chips_comm([self.chip_bf16[nm] for nm in names], deliver)

    def chip_sums(self, names):
        out = {}
        for grp in _same_shape_groups(names):
            sums = _chip_sum_call("chip_sum_" + grp[0], self.place, [self.sharded[nm] for nm in grp],
                                  [self.from_pair[nm] for nm in grp], [self.from_chips[nm] for nm in grp])
            out.update(zip(grp, sums))
        return out


def kernel(x, meta_tokens, ffn1_norm, ffn1_w_gate, ffn1_w_up, ffn1_w_down, mix_norm, w_in, q_latent_norm, w_uq, kv_latent_norm, w_uk, w_uv, q_head_norm, k_head_norm, conv_w, conv_b, gate_a_w, gate_a_b, gate_x_w, gate_x_b, lru_lambda, attn_out_norm, lru_out_norm, w_out, ffn2_norm, ffn2_w_gate, ffn2_w_up, ffn2_w_down, final_norm, loss_target, m_meta_tokens, m_ffn1_norm, m_ffn1_w_gate, m_ffn1_w_up, m_ffn1_w_down, m_mix_norm, m_w_in, m_q_latent_norm, m_w_uq, m_kv_latent_norm, m_w_uk, m_w_uv, m_q_head_norm, m_k_head_norm, m_conv_w, m_conv_b, m_gate_a_w, m_gate_a_b, m_gate_x_w, m_gate_x_b, m_lru_lambda, m_attn_out_norm, m_lru_out_norm, m_w_out, m_ffn2_norm, m_ffn2_w_gate, m_ffn2_w_up, m_ffn2_w_down, m_final_norm, v_meta_tokens, v_ffn1_norm, v_ffn1_w_gate, v_ffn1_w_up, v_ffn1_w_down, v_mix_norm, v_w_in, v_q_latent_norm, v_w_uq, v_kv_latent_norm, v_w_uk, v_w_uv, v_q_head_norm, v_k_head_norm, v_conv_w, v_conv_b, v_gate_a_w, v_gate_a_b, v_gate_x_w, v_gate_x_b, v_lru_lambda, v_attn_out_norm, v_lru_out_norm, v_w_out, v_ffn2_norm, v_ffn2_w_gate, v_ffn2_w_up, v_ffn2_w_down, v_final_norm):
    args = locals()
    p = {nm: args[nm] for nm in WEIGHT_NAMES}
    mom = {nm: args["m_" + nm] for nm in WEIGHT_NAMES}
    var = {nm: args["v_" + nm] for nm in WEIGHT_NAMES}
    nb, seq, d = x.shape
    lp = CHUNK + seq
    xi, yi, ci = lax.axis_index("x"), lax.axis_index("y"), lax.axis_index("c")
    chip = 2 * xi + yi

    place = jnp.stack([chip, ci]).astype(jnp.int32)
    p2 = {nm: _to2d(nm, p[nm]) for nm in BIG_NAMES}
    m2 = {nm: _to2d(nm, mom[nm]) for nm in BIG_NAMES}
    v2 = {nm: _to2d(nm, var[nm]) for nm in BIG_NAMES}

    slots = {}
    for grp in BIG_GROUPS:
        for nm, buf in zip(grp, _cast_call("cast_" + grp[0], place, [p2[nm] for nm in grp])):
            slots[nm] = buf
    small_shard = jnp.concatenate(
        [meta_tokens, conv_w[0].reshape(2, 2 * LRU_TILE), jnp.zeros((14, 2 * LRU_TILE), F32)], axis=0)
    small_slots = lax.dynamic_update_slice(jnp.zeros((N_SHARD,) + small_shard.shape, F32), small_shard[None],
                                           (chip, 0, 0))
    first = _comm_call("gather_first", _gather_comm([slots[nm] for nm in GATHER_FIRST] + [small_slots], lambda o: o))
    w, meta = _small_weights(p, first[-1])
    w.update({nm: _weight_from(nm, o) for nm, o in zip(GATHER_FIRST, first[:-1])})
    sched = _Sched(place, w, slots)

    h0 = jnp.concatenate(
        [jnp.zeros((nb, PAD_ROWS, d), F32), jnp.broadcast_to(meta[None], (nb, N_META, d)), x], axis=1)
    target = jnp.pad(loss_target, ((0, 0), (CHUNK, 0), (0, 0)))
    loss_part, dh0, g = _local_step(h0.reshape(nb * lp, d), target.reshape(nb * lp, d), w, nb, lp, sched)
    dh0 = dh0.reshape(nb, lp, d)
    grad_x = dh0[:, CHUNK:, :]

    late = jnp.concatenate([g["ffn1_norm"], g["mix_norm"], jnp.zeros((6, D_MODEL), F32),
                            jnp.sum(dh0[:, PAD_ROWS:CHUNK, :], axis=0)], axis=0)
    late_all = _comm_call("gather_late", _small_comm(late, lambda o: o))[0]
    small_like = {nm: p[nm] for nm in SMALL_NAMES + ["gate_a_w", "gate_x_w"]}

    def pack_w(t):
        tt = {nm: t[nm] for nm in SMALL_NAMES + ["gate_a_w", "gate_x_w"]}
        tt["conv_w"] = jnp.zeros((CONV_K, LRU_W), F32)
        return _pack_small(tt, SMALL_ADAM_ROWS)

    me = (4 * xi + 2 * yi + ci).astype(jnp.int32).reshape(1)
    gsum, dsm, msm, vsm = _small_update_call(me, sched.early_all, sched.early, late_all, late, pack_w(p),
                                             pack_w(mom), pack_w(var))
    grads = _unpack_small(gsum, small_like)
    delta = _unpack_small(dsm, small_like)
    new_m = _unpack_small(msm, small_like)
    new_v = _unpack_small(vsm, small_like)
    loss = gsum[ROW_LOSS, 0]
    gmeta = gsum[ROW_META:ROW_META + N_META].reshape(N_META, N_SHARD, D_MODEL // N_SHARD)
    grads["meta_tokens"] = lax.dynamic_index_in_dim(gmeta, chip, axis=1, keepdims=False)
    gconv = gsum[ROW_CONV_W:ROW_CONV_W + 2].reshape(CONV_K, N_SHARD, LRU_TILE)
    grads["conv_w"] = lax.dynamic_index_in_dim(gconv, chip, axis=1, keepdims=False)[None]
    for nm in ("meta_tokens", "conv_w"):
        delta[nm], new_m[nm], new_v[nm] = _adamw_call("adamw_" + nm, p[nm], grads[nm], mom[nm], var[nm])

    done = [nm for at, names in CHIPS_AT if at is not None for nm in names]
    last = [nm for at, names in CHIPS_AT if at is None for nm in names]
    mine = sched.chip_sums(done)
    shard_grads = {}
    share = _share_pair_comm([mine[nm] for nm in done], lambda o: shard_grads.update(zip(done, o)))
    _comm_call("share_pair", _join_comms([share, sched.chips(last)]))

    def adamw(names):
        res = _adamw_group_call("adamw_" + names[0], [p2[nm] for nm in names], [shard_grads[nm] for nm in names],
                                [m2[nm] for nm in names], [v2[nm] for nm in names])
        for nm, (gg, dd, mm, vv) in zip(names, res):
            grads[nm], delta[nm], new_m[nm], new_v[nm] = (_from2d(nm, t) for t in (gg, dd, mm, vv))

    for grp in _same_shape_groups(done):
        adamw(grp)
    mine = sched.chip_sums(last)
    shard_grads = dict(zip(last, _comm_call("share_last", _share_pair_comm([mine[nm] for nm in last], lambda o: o))))
    adamw(last)

    return (loss, grad_x, *[grads[nm] for nm in WEIGHT_NAMES], *[delta[nm] for nm in WEIGHT_NAMES],
            *[new_m[nm] for nm in WEIGHT_NAMES], *[new_v[nm] for nm in WEIGHT_NAMES])
```

```python
import functools
import math

import jax
import jax.numpy as jnp
import numpy as np
from jax import lax
from jax.experimental import pallas as pl
from jax.experimental.pallas import tpu as pltpu

F32 = jnp.float32
BF16 = jnp.bfloat16
MESH = pl.DeviceIdType.MESH

D_MODEL = 1024
N_META = 16
CHUNK = 64
PAD_ROWS = CHUNK - N_META
HEADS = 4
D_NOPE = 128
D_ROPE = 64
D_QK = D_NOPE + D_ROPE
D_QKP = 256
D_V = 128
KV_RANK = 256
Q_RANK = 384
MLA_W = HEADS * D_V
LRU_W = 512
LRU_TILE = 128
N_LRU_TILES = LRU_W // LRU_TILE
CONV_K = 4
C_RGLRU = 8.0
ROPE_THETA = 10000.0
D_FF = 2816
N_SHARD = 4
EPS = 1e-6
NEG_INF = -1e30
Z_KR = Q_RANK + KV_RANK
Z_MLA = Z_KR + 128
Z_U = Z_MLA
Z_G = Z_U + LRU_W
Z_W = Z_G + LRU_W
IN_WIDTH = Q_RANK + KV_RANK + D_ROPE + 2 * LRU_W

ADAM_LR = 0.001
ADAM_B1 = 0.9
ADAM_B2 = 0.999
ADAM_EPS = 1e-08
ADAM_WD = 0.01
ADAM_STEP = 10

VMEM_LIMIT_BYTES = 56 * 1024 * 1024
SMALL_ROWS = 104
SMALL_ADAM_ROWS = 80


def _params(sem):
    return pltpu.CompilerParams(dimension_semantics=sem, vmem_limit_bytes=VMEM_LIMIT_BYTES)


def _row_tile(rows, target):
    best = 16
    for t in range(16, min(rows, target) + 1, 16):
        if rows % t == 0:
            best = t
    return best


def _col_tile(cols, target):
    best = cols
    for t in range(128, min(cols, target) + 1, 128):
        if cols % t == 0:
            best = t
    return best


def _dot(a, b):
    return jnp.dot(a, b, preferred_element_type=F32)


def _dot_nt(a, b):
    return lax.dot_general(a, b, (((1,), (1,)), ((), ())), preferred_element_type=F32)


def _dot_tn(a, b):
    return lax.dot_general(a, b, (((0,), (0,)), ((), ())), preferred_element_type=F32)


def _rms(x, n):
    return lax.rsqrt(jnp.sum(x * x, axis=-1, keepdims=True) * (1.0 / n) + EPS)


def _rms_bwd(dn, nrm, r, n):
    return r * (dn - nrm * (jnp.sum(dn * nrm, axis=-1, keepdims=True) * (1.0 / n)))


def _gelu(x):
    k = math.sqrt(2.0 / math.pi)
    t = jnp.tanh(k * (x + 0.044715 * x * x * x))
    return 0.5 * x * (1.0 + t), t


def _gelu_grad(x, t):
    k = math.sqrt(2.0 / math.pi)
    return 0.5 * (1.0 + t) + 0.5 * x * (1.0 - t * t) * k * (1.0 + 3.0 * 0.044715 * x * x)


def _sigmoid(x):
    return 0.5 + 0.5 * jnp.tanh(0.5 * x)


def _softplus_neg(lam):
    e = jnp.exp(-jnp.abs(lam))
    log1p = jnp.where(e < 0.01, e * (1.0 - e * (0.5 - e * (1.0 / 3 - e * 0.25))), jnp.log(1.0 + e))
    return jnp.maximum(-lam, 0.0) + log1p


def _rope(t, c, s1, s2):
    return t * c + pltpu.roll(t, 96, 1) * s1 + pltpu.roll(t, 32, 1) * s2


def _rope_t(d, c, s1, s2):
    return d * c + pltpu.roll(d * s1, 32, 1) + pltpu.roll(d * s2, 96, 1)


def _rope_tables(lp):
    pos = (np.arange(lp, dtype=np.int32) - PAD_ROWS).astype(np.float32)
    inv_freq = (ROPE_THETA ** (-np.arange(0, D_ROPE // 2, dtype=np.float32) / (D_ROPE // 2))).astype(np.float32)
    ang = (pos[:, None] * inv_freq[None, :]).astype(np.float32).astype(np.float64)
    cos, sin = np.cos(ang).astype(np.float32), np.sin(ang).astype(np.float32)
    z = np.zeros_like(cos)
    return (jnp.asarray(np.concatenate([cos, cos, z, z], 1)), jnp.asarray(np.concatenate([-sin, z, z, z], 1)),
            jnp.asarray(np.concatenate([z, sin, z, z], 1)))


def _rmsnorm_call(name, h, g, tm):
    rows, d = h.shape

    def body(h_ref, g_ref, o_ref):
        x = h_ref[...]
        o_ref[...] = (x * _rms(x, d) * g_ref[...]).astype(BF16)

    return pl.pallas_call(
        body, name=name, grid=(rows // tm,),
        in_specs=[pl.BlockSpec((tm, d), lambda i: (i, 0)), pl.BlockSpec((1, d), lambda i: (0, 0))],
        out_specs=pl.BlockSpec((tm, d), lambda i: (i, 0)),
        out_shape=jax.ShapeDtypeStruct((rows, d), BF16),
        compiler_params=_params(("parallel",)))(h, g)


def _ffn_up_call(name, u, wg, wu, tm, comm=None):
    rows, d = u.shape
    ns, fs, _ = wg.shape

    def body(u_ref, wg_ref, wu_ref, g_ref, p_ref, a_ref):
        uu = u_ref[...]
        g = _dot_nt(uu, wg_ref[0])
        p = _dot_nt(uu, wu_ref[0])
        g_ref[0] = g.astype(BF16)
        p_ref[0] = p.astype(BF16)
        a_ref[0] = (g * jax.nn.sigmoid(g) * p).astype(BF16)

    wspec = pl.BlockSpec((1, fs, d), lambda s, i: (s, 0, 0))
    ospec = pl.BlockSpec((1, tm, fs), lambda s, i: (s, i, 0))
    oshape = jax.ShapeDtypeStruct((ns, rows, fs), BF16)
    return _hosted_call(
        body, name=name, grid=(ns, rows // tm),
        in_specs=[pl.BlockSpec((tm, d), lambda s, i: (i, 0)), wspec, wspec],
        out_specs=[ospec, ospec, ospec], out_shape=[oshape, oshape, oshape],
        dims=("parallel", "parallel"), args=(u, wg, wu), comm=comm)


def _ffn_gate_call(name, u, wg, tm, comm=None):
    rows, d = u.shape
    ns, fs, _ = wg.shape

    def body(u_ref, wg_ref, g_ref):
        g_ref[0] = _dot_nt(u_ref[...], wg_ref[0]).astype(BF16)

    return _hosted_call(
        body, name=name, grid=(ns, rows // tm),
        in_specs=[pl.BlockSpec((tm, d), lambda s, i: (i, 0)), pl.BlockSpec((1, fs, d), lambda s, i: (s, 0, 0))],
        out_specs=[pl.BlockSpec((1, tm, fs), lambda s, i: (s, i, 0))],
        out_shape=[jax.ShapeDtypeStruct((ns, rows, fs), BF16)],
        dims=("parallel", "parallel"), args=(u, wg), comm=comm)[0]


def _ffn_upact_call(name, u, wu, gate, tm, comm=None):
    rows, d = u.shape
    ns, fs, _ = wu.shape

    def body(u_ref, wu_ref, g_ref, p_ref, a_ref):
        p = _dot_nt(u_ref[...], wu_ref[0])
        g = g_ref[0].astype(F32)
        p_ref[0] = p.astype(BF16)
        a_ref[0] = (g * jax.nn.sigmoid(g) * p).astype(BF16)

    ospec = pl.BlockSpec((1, tm, fs), lambda s, i: (s, i, 0))
    oshape = jax.ShapeDtypeStruct((ns, rows, fs), BF16)
    return _hosted_call(
        body, name=name, grid=(ns, rows // tm),
        in_specs=[pl.BlockSpec((tm, d), lambda s, i: (i, 0)), pl.BlockSpec((1, fs, d), lambda s, i: (s, 0, 0)), ospec],
        out_specs=[ospec, ospec], out_shape=[oshape, oshape],
        dims=("parallel", "parallel"), args=(u, wu, gate), comm=comm)


def _ffn_down_call(name, a, wd, h, tm, comm=None):
    rows, d = h.shape
    ns, _, fs = a.shape

    def body(a_ref, wd_ref, h_ref, o_ref):
        acc = h_ref[...]
        for s in range(ns):
            acc = acc + 0.5 * _dot(a_ref[s], wd_ref[s])
        o_ref[...] = acc

    return _hosted_call(
        body, name=name, grid=(rows // tm,),
        in_specs=[pl.BlockSpec((ns, tm, fs), lambda i: (0, i, 0)),
                  pl.BlockSpec((ns, fs, d), lambda i: (0, 0, 0)),
                  pl.BlockSpec((tm, d), lambda i: (i, 0))],
        out_specs=[pl.BlockSpec((tm, d), lambda i: (i, 0))],
        out_shape=[jax.ShapeDtypeStruct((rows, d), F32)],
        dims=("parallel",), args=(a, wd, h), comm=comm)[0]


def _mm_call(name, a, bt, tm, out_dtype):
    rows, k = a.shape
    n = bt.shape[0]

    def body(a_ref, b_ref, o_ref):
        o_ref[...] = _dot_nt(a_ref[...], b_ref[...]).astype(out_dtype)

    return pl.pallas_call(
        body, name=name, grid=(rows // tm,),
        in_specs=[pl.BlockSpec((tm, k), lambda i: (i, 0)), pl.BlockSpec((n, k), lambda i: (0, 0))],
        out_specs=pl.BlockSpec((tm, n), lambda i: (i, 0)),
        out_shape=jax.ShapeDtypeStruct((rows, n), out_dtype),
        compiler_params=_params(("parallel",)))(a, bt)


def _mla_heads(z, gql, gkvl, wuq, wuk, wuv):
    cq = z[:, 0:Q_RANK]
    ckv = z[:, Q_RANK:Z_KR]
    kr = z[:, Z_KR:Z_MLA]
    rq = _rms(cq, Q_RANK)
    nq = cq * rq
    cqn = (nq * gql).astype(BF16)
    rkv = _rms(ckv, KV_RANK)
    nkv = ckv * rkv
    ckvn = (nkv * gkvl).astype(BF16)
    qraw = _dot_nt(cqn, wuq)
    knope = _dot(ckvn, wuk)
    v = _dot(ckvn, wuv)
    skr = jnp.sum(kr * kr, axis=-1, keepdims=True)
    heads = []
    for hd in range(HEADS):
        qh = qraw[:, hd * D_QKP:(hd + 1) * D_QKP]
        rqh = lax.rsqrt(jnp.sum(qh * qh, axis=-1, keepdims=True) * (1.0 / D_QK) + EPS)
        kn = knope[:, hd * D_NOPE:(hd + 1) * D_NOPE]
        rkh = lax.rsqrt((jnp.sum(kn * kn, axis=-1, keepdims=True) + skr) * (1.0 / D_QK) + EPS)
        heads.append((qh * rqh, rqh, kn * rkh, kr * rkh, rkh))
    return dict(rq=rq, nq=nq, cqn=cqn, rkv=rkv, nkv=nkv, ckvn=ckvn, v=v, heads=heads)


def _mla_prep_call(z, gql, gkvl, gqh, gkh, wuq, wuk, wuv, tabs, lp, tm):
    rows = z.shape[0]
    tpe = lp // tm

    def body(z_ref, gql_ref, gkvl_ref, gqh_ref, gkh_ref, wuq_ref, wuk_ref, wuv_ref, c_ref, s1_ref, s2_ref,
             q_ref, k_ref, v_ref, cqn_ref, ckvn_ref):
        m = _mla_heads(z_ref[...], gql_ref[...], gkvl_ref[...], wuq_ref[...], wuk_ref[...], wuv_ref[...])
        c, s1, s2 = c_ref[...], s1_ref[...], s2_ref[...]
        gq, gk = gqh_ref[...], gkh_ref[...]
        row = (pl.program_id(0) % tpe) * tm + lax.broadcasted_iota(jnp.int32, (tm, 1), 0)
        spare = (lax.broadcasted_iota(jnp.int32, (1, D_QKP - D_NOPE), 1) == D_ROPE).astype(F32)
        kmask = jnp.where(row < PAD_ROWS, NEG_INF * math.sqrt(D_QK), 0.0) * spare
        for hd in range(HEADS):
            qn, _, knn, krn, _ = m["heads"][hd]
            qg = qn * gq
            q_ref[hd, :, 0:D_NOPE] = qg[:, 0:D_NOPE].astype(BF16)
            q_ref[hd, :, D_NOPE:D_QKP] = (_rope(qg[:, D_NOPE:D_QKP], c, s1, s2) + spare).astype(BF16)
            k_ref[hd, :, 0:D_NOPE] = (knn * gk[:, 0:D_NOPE]).astype(BF16)
            k_ref[hd, :, D_NOPE:D_QKP] = (_rope(krn * gk[:, D_NOPE:D_QKP], c, s1, s2) + kmask).astype(BF16)
            v_ref[hd] = m["v"][:, hd * D_V:(hd + 1) * D_V].astype(BF16)
        cqn_ref[...] = m["cqn"]
        ckvn_ref[...] = m["ckvn"]

    def const(shape):
        return pl.BlockSpec(shape, lambda i: tuple(0 for _ in shape))

    tab = pl.BlockSpec((tm, 128), lambda i: (i % tpe, 0))
    return pl.pallas_call(
        body, name="mla_prep", grid=(rows // tm,),
        in_specs=[pl.BlockSpec((tm, Z_MLA), lambda i: (i, 0)), const((1, Q_RANK)), const((1, KV_RANK)),
                  const((1, D_QKP)), const((1, D_QKP)), const((HEADS * D_QKP, Q_RANK)),
                  const((KV_RANK, HEADS * D_NOPE)), const((KV_RANK, HEADS * D_V)), tab, tab, tab],
        out_specs=[pl.BlockSpec((HEADS, tm, D_QKP), lambda i: (0, i, 0)),
                   pl.BlockSpec((HEADS, tm, D_QKP), lambda i: (0, i, 0)),
                   pl.BlockSpec((HEADS, tm, D_V), lambda i: (0, i, 0)),
                   pl.BlockSpec((tm, Q_RANK), lambda i: (i, 0)),
                   pl.BlockSpec((tm, KV_RANK), lambda i: (i, 0))],
        out_shape=[jax.ShapeDtypeStruct((HEADS, rows, D_QKP), BF16),
                   jax.ShapeDtypeStruct((HEADS, rows, D_QKP), BF16),
                   jax.ShapeDtypeStruct((HEADS, rows, D_V), BF16),
                   jax.ShapeDtypeStruct((rows, Q_RANK), BF16),
                   jax.ShapeDtypeStruct((rows, KV_RANK), BF16)],
        compiler_params=_params(("parallel",)))(z, gql, gkvl, gqh, gkh, wuq, wuk, wuv, *tabs)


Q_BLOCK_ROWS = 528


def _q_block(lp):
    return _row_tile(lp, Q_BLOCK_ROWS)


def _key_end(ext, lp):
    return min(lp, -(-ext // CHUNK) * CHUNK)


def _diag_bias(qb, j0, nk):
    shift = CHUNK.bit_length() - 1
    r = jnp.right_shift(j0 + lax.broadcasted_iota(jnp.int32, (qb, nk), 0), shift)
    c = jnp.right_shift(j0 + lax.broadcasted_iota(jnp.int32, (qb, nk), 1), shift)
    return jnp.where(c <= r, 0.0, NEG_INF)


def _attn_fwd_call(q, k, v, nb, lp, comm=None):
    rows = nb * lp
    qb = _q_block(lp)
    scale = 1.0 / math.sqrt(D_QK)

    def body(q_ref, k_ref, v_ref, o_ref, lse_ref):
        for j in range(lp // qb):
            j0, ext = j * qb, (j + 1) * qb
            kend = _key_end(ext, lp)
            qj = q_ref[0, j0:ext, :]
            sd = _dot_nt(qj, k_ref[0, j0:kend, :]) * scale + _diag_bias(qb, j0, kend - j0)
            mx = jnp.max(sd, axis=-1, keepdims=True)
            if j > 0:
                so = _dot_nt(qj, k_ref[0, 0:j0, :]) * scale
                mx = jnp.maximum(mx, jnp.max(so, axis=-1, keepdims=True))
            pd = jnp.exp(sd - mx)
            l = jnp.sum(pd, axis=-1, keepdims=True)
            o = _dot(pd.astype(BF16), v_ref[0, j0:kend, :])
            if j > 0:
                po = jnp.exp(so - mx)
                l = l + jnp.sum(po, axis=-1, keepdims=True)
                o = o + _dot(po.astype(BF16), v_ref[0, 0:j0, :])
            o_ref[j0:ext, :] = o / l
            lse_ref[0, j0:ext, :] = mx + jnp.log(l)

    return _hosted_call(
        body, name="attn_fwd", grid=(nb, HEADS),
        in_specs=[pl.BlockSpec((1, lp, D_QKP), lambda b, h: (h, b, 0)),
                  pl.BlockSpec((1, lp, D_QKP), lambda b, h: (h, b, 0)),
                  pl.BlockSpec((1, lp, D_V), lambda b, h: (h, b, 0))],
        out_specs=[pl.BlockSpec((lp, D_V), lambda b, h: (b, h)),
                   pl.BlockSpec((1, lp, 1), lambda b, h: (h, b, 0))],
        out_shape=[jax.ShapeDtypeStruct((rows, MLA_W), F32),
                   jax.ShapeDtypeStruct((HEADS, rows, 1), F32)],
        dims=("parallel", "parallel"), args=(q, k, v), comm=comm)


def _attn_bwd_call(q, k, v, o, lse, do, nb, lp, comm=None):
    rows = nb * lp
    qb = _q_block(lp)
    scale = 1.0 / math.sqrt(D_QK)

    def body(q_ref, k_ref, v_ref, o_ref, lse_ref, do_ref, dq_ref, dk_ref, dv_ref, dk_acc, dv_acc):
        dk_acc[...] = jnp.zeros_like(dk_acc)
        dv_acc[...] = jnp.zeros_like(dv_acc)
        for j in range(lp // qb):
            j0, ext = j * qb, (j + 1) * qb
            kend = _key_end(ext, lp)
            dbias = _diag_bias(qb, j0, kend - j0)
            qj = q_ref[0, j0:ext, :]
            doj = do_ref[j0:ext, :]
            delta = jnp.sum(doj * o_ref[j0:ext, :], axis=-1, keepdims=True)
            dob = doj.astype(BF16)
            lse = lse_ref[0, j0:ext, :]
            dq = jnp.zeros((qb, D_QKP), F32)
            for lo, hi, bias in ((j0, kend, dbias), (0, j0, None)):
                if hi == lo:
                    continue
                kk = k_ref[0, lo:hi, :]
                s = _dot_nt(qj, kk) * scale
                p = jnp.exp((s if bias is None else s + bias) - lse)
                dv_acc[lo:hi, :] += _dot_tn(p.astype(BF16), dob)
                dp = _dot_nt(dob, v_ref[0, lo:hi, :])
                ds = (p * (dp - delta) * scale).astype(BF16)
                dq = dq + _dot(ds, kk)
                dk_acc[lo:hi, :] += _dot_tn(ds, qj)
            dq_ref[0, j0:ext, :] = dq.astype(BF16)
        dk_ref[0] = dk_acc[...].astype(BF16)
        dv_ref[0] = dv_acc[...].astype(BF16)

    qspec = pl.BlockSpec((1, lp, D_QKP), lambda b, h: (h, b, 0))
    vspec = pl.BlockSpec((1, lp, D_V), lambda b, h: (h, b, 0))
    ospec = pl.BlockSpec((lp, D_V), lambda b, h: (b, h))
    return _hosted_call(
        body, name="attn_bwd", grid=(nb, HEADS),
        in_specs=[qspec, qspec, vspec, ospec, pl.BlockSpec((1, lp, 1), lambda b, h: (h, b, 0)), ospec],
        out_specs=[qspec, qspec, vspec],
        out_shape=[jax.ShapeDtypeStruct((HEADS, rows, D_QKP), BF16),
                   jax.ShapeDtypeStruct((HEADS, rows, D_QKP), BF16),
                   jax.ShapeDtypeStruct((HEADS, rows, D_V), BF16)],
        scratch_shapes=[pltpu.VMEM((lp, D_QKP), F32), pltpu.VMEM((lp, D_V), F32)],
        dims=("parallel", "parallel"), args=(q, k, v, o, lse, do), comm=comm)


def _lru_gates(u, cw, cb, wa, ba, wx, bx, lam, lp):
    xc = (cw[3:4, :] * u + cw[2:3, :] * pltpu.roll(u, 1, 0) + cw[1:2, :] * pltpu.roll(u, 2, 0)
          + cw[0:1, :] * pltpu.roll(u, 3, 0) + cb)
    xcb = xc.astype(BF16)
    r = _sigmoid(_dot(xcb, wa) + ba)
    i = _sigmoid(_dot(xcb, wx) + bx)
    sp = _softplus_neg(lam)
    la = -C_RGLRU * r * sp
    a = jnp.exp(la)
    x2 = 2.0 * la
    e2 = a * a
    m2 = jnp.maximum(jnp.where(x2 > -0.01, -x2 * (1.0 + 0.5 * x2), 1.0 - e2), 1e-30)
    rs = lax.rsqrt(m2)
    row = lax.broadcasted_iota(jnp.int32, (lp, LRU_TILE), 0)
    first = row == PAD_ROWS
    valid = row >= PAD_ROWS
    mult_eff = jnp.where(first, 1.0, m2 * rs)
    return dict(xc=xc, xcb=xcb, r=r, i=i, sp=sp, a=a, e2=e2, rs=rs, mult_eff=mult_eff, first=first, valid=valid)


def _scan_rows(a, b, a_s, b_s, out_ref, lp, reverse):
    sub = lax.broadcasted_iota(jnp.int32, (lp, LRU_TILE), 0) & 7
    for dist in (1, 2, 4):
        shift = lp - dist if reverse else dist
        keep = (sub + dist <= 7) if reverse else (sub >= dist)
        a_sh = pltpu.roll(a, shift, 0)
        b_sh = pltpu.roll(b, shift, 0)
        b = jnp.where(keep, a * b_sh + b, b)
        a = jnp.where(keep, a * a_sh, a)
    a_s[...] = a
    b_s[...] = b
    n_groups = lp // 8
    edge = 0 if reverse else 7

    def group(gi, carry):
        r0 = pl.multiple_of(((n_groups - 1 - gi) if reverse else gi) * 8, 8)
        a8 = a_s[pl.ds(r0, 8), :]
        b8 = b_s[pl.ds(r0, 8), :]
        out_ref[pl.ds(r0, 8), :] = a8 * carry + b8
        return a8[edge:edge + 1, :] * carry + b8[edge:edge + 1, :]

    lax.fori_loop(0, n_groups, group, jnp.zeros((1, LRU_TILE), F32), unroll=4)


def _lru_specs(lp):
    seq = lambda col0: pl.BlockSpec((lp, LRU_TILE), lambda t, b: (b, col0 + t))
    cw = pl.BlockSpec((1, CONV_K, LRU_TILE), lambda t, b: (t, 0, 0))
    vec = pl.BlockSpec((1, LRU_TILE), lambda t, b: (0, t))
    mat = pl.BlockSpec((1, LRU_TILE, LRU_TILE), lambda t, b: (t, 0, 0))
    return seq, cw, vec, mat


def _lru_fwd_call(z, cw, cb, wa, ba, wx, bx, lam, nb, lp, comm=None):
    rows = nb * lp
    seq, cwspec, vec, mat = _lru_specs(lp)

    def body(u_ref, g_ref, cw_ref, cb_ref, wa_ref, ba_ref, wx_ref, bx_ref, lam_ref, y_ref, hs_ref, a_s, b_s):
        m = _lru_gates(u_ref[...], cw_ref[0], cb_ref[...], wa_ref[0], ba_ref[...], wx_ref[0], bx_ref[...],
                       lam_ref[...], lp)
        a = jnp.where(m["valid"], m["a"], 0.0)
        b = jnp.where(m["valid"], m["mult_eff"] * (m["i"] * m["xc"]), 0.0)
        _scan_rows(a, b, a_s, b_s, hs_ref, lp, reverse=False)
        gl, _ = _gelu(g_ref[...])
        y_ref[...] = hs_ref[...] * gl

    oshape = jax.ShapeDtypeStruct((rows, LRU_W), F32)
    return _hosted_call(
        body, name="lru_fwd", grid=(N_LRU_TILES, nb),
        in_specs=[seq(Z_U // LRU_TILE), seq(Z_G // LRU_TILE), cwspec, vec, mat, vec, mat, vec, vec],
        out_specs=[seq(0), seq(0)], out_shape=[oshape, oshape],
        scratch_shapes=[pltpu.VMEM((lp, LRU_TILE), F32), pltpu.VMEM((lp, LRU_TILE), F32)],
        dims=("parallel", "parallel"), args=(z, z, cw, cb, wa, ba, wx, bx, lam), comm=comm)


def _lru_bwd_call(z, hs, dy, cw, cb, wa, ba, wx, bx, lam, nb, lp, comm=None):
    rows = nb * lp
    seq, cwspec, vec, mat = _lru_specs(lp)

    def body(u_ref, g_ref, hs_ref, dy_ref, cw_ref, cb_ref, wa_ref, ba_ref, wx_ref, bx_ref, lam_ref,
             du_ref, dg_ref, dcw_ref, dcb_ref, dwa_ref, dba_ref, dwx_ref, dbx_ref, dlam_ref, a_s, b_s, d_s):
        b_idx = pl.program_id(1)
        u = u_ref[...]
        cw = cw_ref[0]
        wa, wx = wa_ref[0], wx_ref[0]
        lam = lam_ref[...]
        m = _lru_gates(u, cw, cb_ref[...], wa, ba_ref[...], wx, bx_ref[...], lam, lp)
        gate = g_ref[...]
        gl, th = _gelu(gate)
        dy = dy_ref[...]
        hs = hs_ref[...]
        dg_ref[...] = (dy * hs * _gelu_grad(gate, th)).astype(BF16)
        a_eff = jnp.where(m["valid"], m["a"], 0.0)
        _scan_rows(pltpu.roll(a_eff, lp - 1, 0), dy * gl, a_s, b_s, d_s, lp, reverse=True)
        ds = d_s[...]
        xc, r, i = m["xc"], m["r"], m["i"]
        row = lax.broadcasted_iota(jnp.int32, (lp, LRU_TILE), 0)
        da = ds * jnp.where(row >= 1, pltpu.roll(hs, 1, 0), 0.0)
        db = jnp.where(m["valid"], ds, 0.0)
        di = db * m["mult_eff"] * xc
        dxc = db * m["mult_eff"] * i
        live = m["valid"] & jnp.logical_not(m["first"])
        dm = jnp.where(live, db * i * xc, 0.0)
        dla = da * m["a"] - dm * (m["e2"] * m["rs"])
        dr = dla * (-C_RGLRU * m["sp"])
        dsp = jnp.sum(dla * (-C_RGLRU * r), axis=0, keepdims=True)
        dpr = (dr * r * (1.0 - r))
        dpi = (di * i * (1.0 - i))
        dprb, dpib = dpr.astype(BF16), dpi.astype(BF16)
        dxc = dxc + _dot_nt(dprb, wa) + _dot_nt(dpib, wx)
        du = (cw[3:4, :] * dxc + cw[2:3, :] * pltpu.roll(dxc, lp - 1, 0) + cw[1:2, :] * pltpu.roll(dxc, lp - 2, 0)
              + cw[0:1, :] * pltpu.roll(dxc, lp - 3, 0))
        du_ref[...] = jnp.where(m["valid"], du, 0.0).astype(BF16)
        tap = lax.broadcasted_iota(jnp.int32, (CONV_K, LRU_TILE), 0)
        dcw = jnp.zeros((CONV_K, LRU_TILE), F32)
        for kk in range(CONV_K):
            shifted = u if kk == CONV_K - 1 else pltpu.roll(u, CONV_K - 1 - kk, 0)
            dcw = jnp.where(tap == kk, jnp.sum(dxc * shifted, axis=0, keepdims=True), dcw)
        parts = [(dcw_ref, dcw[None]), (dcb_ref, jnp.sum(dxc, axis=0, keepdims=True)[None]),
                 (dwa_ref, _dot_tn(m["xcb"], dprb)[None]), (dba_ref, jnp.sum(dpr, axis=0, keepdims=True)[None]),
                 (dwx_ref, _dot_tn(m["xcb"], dpib)[None]), (dbx_ref, jnp.sum(dpi, axis=0, keepdims=True)[None]),
                 (dlam_ref, (dsp * (-jax.nn.sigmoid(-lam)))[None])]

        @pl.when(b_idx == 0)
        def _():
            for ref, val in parts:
                ref[...] = val

        @pl.when(b_idx != 0)
        def _():
            for ref, val in parts:
                ref[...] += val

    bshape = jax.ShapeDtypeStruct((rows, LRU_W), BF16)
    vec3 = pl.BlockSpec((1, 1, LRU_TILE), lambda t, b: (t, 0, 0))
    vshape = jax.ShapeDtypeStruct((N_LRU_TILES, 1, LRU_TILE), F32)
    mshape = jax.ShapeDtypeStruct((N_LRU_TILES, LRU_TILE, LRU_TILE), F32)
    return _hosted_call(
        body, name="lru_bwd", grid=(N_LRU_TILES, nb),
        in_specs=[seq(Z_U // LRU_TILE), seq(Z_G // LRU_TILE), seq(0), seq(0), cwspec, vec, mat, vec, mat, vec, vec],
        out_specs=[seq(0), seq(0), cwspec, vec3, mat, vec3, mat, vec3, vec3],
        out_shape=[bshape, bshape, jax.ShapeDtypeStruct((N_LRU_TILES, CONV_K, LRU_TILE), F32), vshape, mshape,
                   vshape, mshape, vshape, vshape],
        scratch_shapes=[pltpu.VMEM((lp, LRU_TILE), F32)] * 3,
        dims=("parallel", "arbitrary"), args=(z, z, hs, dy, cw, cb, wa, ba, wx, bx, lam), comm=comm)


def _mix_out_call(ya, yl, ga, gl, wout, h, tm):
    rows, d = h.shape

    def body(ya_ref, yl_ref, ga_ref, gl_ref, w_ref, h_ref, y_ref, o_ref):
        a = ya_ref[...]
        l = yl_ref[...]
        an = (a * _rms(a, MLA_W) * ga_ref[...]).astype(BF16)
        ln = (l * _rms(l, LRU_W) * gl_ref[...]).astype(BF16)
        y_ref[:, 0:MLA_W] = an
        y_ref[:, MLA_W:MLA_W + LRU_W] = ln
        o_ref[...] = h_ref[...] + _dot(an, w_ref[0:MLA_W, :]) + _dot(ln, w_ref[MLA_W:MLA_W + LRU_W, :])

    half = pl.BlockSpec((tm, MLA_W), lambda i: (i, 0))
    g = pl.BlockSpec((1, MLA_W), lambda i: (0, 0))
    full = pl.BlockSpec((tm, d), lambda i: (i, 0))
    return pl.pallas_call(
        body, name="mix_out", grid=(rows // tm,),
        in_specs=[half, half, g, g, pl.BlockSpec((MLA_W + LRU_W, d), lambda i: (0, 0)), full],
        out_specs=[full, full],
        out_shape=[jax.ShapeDtypeStruct((rows, MLA_W + LRU_W), BF16), jax.ShapeDtypeStruct((rows, d), F32)],
        compiler_params=_params(("parallel",)))(ya, yl, ga, gl, wout, h)


def _mix_out_bwd_call(dhb, wout, ya, yl, ga, gl, tm):
    rows = ya.shape[0]
    d = dhb.shape[1]

    def body(dh_ref, w_ref, ya_ref, yl_ref, ga_ref, gl_ref, dya_ref, dyl_ref, dga_ref, dgl_ref):
        dy = _dot_nt(dh_ref[...], w_ref[...])
        outs = []
        for val, g_ref, lo, out_ref in ((ya_ref[...], ga_ref, 0, dya_ref), (yl_ref[...], gl_ref, MLA_W, dyl_ref)):
            r = _rms(val, MLA_W)
            n = val * r
            dyn = dy[:, lo:lo + MLA_W]
            out_ref[...] = _rms_bwd(dyn * g_ref[...], n, r, MLA_W)
            outs.append(jnp.sum(dyn * n, axis=0, keepdims=True))

        @pl.when(pl.program_id(0) == 0)
        def _():
            dga_ref[...] = outs[0]
            dgl_ref[...] = outs[1]

        @pl.when(pl.program_id(0) != 0)
        def _():
            dga_ref[...] += outs[0]
            dgl_ref[...] += outs[1]

    half = pl.BlockSpec((tm, MLA_W), lambda i: (i, 0))
    g = pl.BlockSpec((1, MLA_W), lambda i: (0, 0))
    return pl.pallas_call(
        body, name="mix_out_bwd", grid=(rows // tm,),
        in_specs=[pl.BlockSpec((tm, d), lambda i: (i, 0)), pl.BlockSpec((MLA_W + LRU_W, d), lambda i: (0, 0)),
                  half, half, g, g],
        out_specs=[half, half, g, g],
        out_shape=[jax.ShapeDtypeStruct((rows, MLA_W), F32), jax.ShapeDtypeStruct((rows, LRU_W), F32),
                   jax.ShapeDtypeStruct((1, MLA_W), F32), jax.ShapeDtypeStruct((1, LRU_W), F32)],
        compiler_params=_params(("arbitrary",)))(dhb, wout, ya, yl, ga, gl)


def _final_call(h, g, target, lp, tm):
    rows, d = h.shape
    tpe = lp // tm

    def body(h_ref, g_ref, t_ref, dh_ref, dhb_ref, dg_ref, loss_ref):
        i = pl.program_id(0)
        x = h_ref[...]
        g = g_ref[...]
        r = _rms(x, d)
        n = x * r
        row = (i % tpe) * tm + lax.broadcasted_iota(jnp.int32, (tm, 1), 0)
        err = jnp.where(row >= CHUNK, n * g - t_ref[...], 0.0)
        dout = err * (1.0 / d)
        dh = _rms_bwd(dout * g, n, r, d)
        dh_ref[...] = dh
        dhb_ref[...] = dh.astype(BF16)
        dg = jnp.sum(dout * n, axis=0, keepdims=True)
        part = jnp.sum(jnp.sum(err * err, axis=1, keepdims=True), axis=0, keepdims=True) * (0.5 / d)
        loss = jnp.broadcast_to(part, (1, 128))

        @pl.when(i == 0)
        def _():
            dg_ref[...] = dg
            loss_ref[...] = loss

        @pl.when(i != 0)
        def _():
            dg_ref[...] += dg
            loss_ref[...] += loss

    full = pl.BlockSpec((tm, d), lambda i: (i, 0))
    return pl.pallas_call(
        body, name="final_loss", grid=(rows // tm,),
        in_specs=[full, pl.BlockSpec((1, d), lambda i: (0, 0)), full],
        out_specs=[full, full, pl.BlockSpec((1, d), lambda i: (0, 0)), pl.BlockSpec((1, 128), lambda i: (0, 0))],
        out_shape=[jax.ShapeDtypeStruct((rows, d), F32), jax.ShapeDtypeStruct((rows, d), BF16),
                   jax.ShapeDtypeStruct((1, d), F32), jax.ShapeDtypeStruct((1, 128), F32)],
        compiler_params=_params(("arbitrary",)))(h, g, target)


def _ffn_dact_call(name, dhb, wd, gate, up, tm, comm=None):
    rows, d = dhb.shape
    ns, fs, _ = wd.shape

    nsub = 2 if tm % 32 == 0 else 1
    sub = tm // nsub

    def body(dh_ref, wd_ref, g_ref, p_ref, dg_ref, dp_ref):
        wd = wd_ref[0]
        for r in range(nsub):
            rs = slice(r * sub, (r + 1) * sub)
            da = (0.5 * _dot_nt(dh_ref[rs, :], wd)).astype(BF16)
            g = g_ref[0, rs, :]
            p = p_ref[0, rs, :]
            sg = jax.nn.sigmoid(g)
            dg_ref[0, rs, :] = (da * p) * (sg * (1.0 + g * (1.0 - sg)))
            dp_ref[0, rs, :] = da * (g * sg)

    aspec = pl.BlockSpec((1, tm, fs), lambda s, i: (s, i, 0))
    oshape = jax.ShapeDtypeStruct((ns, rows, fs), BF16)
    return _hosted_call(
        body, name=name, grid=(ns, rows // tm),
        in_specs=[pl.BlockSpec((tm, d), lambda s, i: (i, 0)), pl.BlockSpec((1, fs, d), lambda s, i: (s, 0, 0)),
                  aspec, aspec],
        out_specs=[aspec, aspec], out_shape=[oshape, oshape],
        dims=("parallel", "parallel"), args=(dhb, wd, gate, up), comm=comm)


def _norm_in_bwd_call(name, pieces, h, g, dres, tm, comm=None):
    rows, d = h.shape
    npc = len(pieces)

    def body(*refs):
        d_refs = refs[0:2 * npc:2]
        w_refs = refs[1:2 * npc:2]
        h_ref, g_ref, dres_ref, dh_ref, dhb_ref, dg_ref = refs[2 * npc:]
        du = jnp.zeros((tm, d), F32)
        for d_ref, w_ref in zip(d_refs, w_refs):
            if len(d_ref.shape) == 3:
                for s in range(d_ref.shape[0]):
                    du = du + _dot(d_ref[s], w_ref[s])
            else:
                du = du + _dot(d_ref[...], w_ref[...])
        x = h_ref[...]
        r = _rms(x, d)
        n = x * r
        dh = dres_ref[...] + _rms_bwd(du * g_ref[...], n, r, d)
        dh_ref[...] = dh
        dhb_ref[...] = dh.astype(BF16)
        dg = jnp.sum(du * n, axis=0, keepdims=True)

        @pl.when(pl.program_id(0) == 0)
        def _():
            dg_ref[...] = dg

        @pl.when(pl.program_id(0) != 0)
        def _():
            dg_ref[...] += dg

    in_specs, args = [], []
    for dd, w in pieces:
        if dd.ndim == 3:
            in_specs.append(pl.BlockSpec((dd.shape[0], tm, dd.shape[2]), lambda i: (0, i, 0)))
            in_specs.append(pl.BlockSpec(w.shape, lambda i: (0, 0, 0)))
        else:
            in_specs.append(pl.BlockSpec((tm, dd.shape[1]), lambda i: (i, 0)))
            in_specs.append(pl.BlockSpec(w.shape, lambda i: (0, 0)))
        args += [dd, w]
    full = pl.BlockSpec((tm, d), lambda i: (i, 0))
    gspec = pl.BlockSpec((1, d), lambda i: (0, 0))
    return _hosted_call(
        body, name=name, grid=(rows // tm,),
        in_specs=in_specs + [full, gspec, full],
        out_specs=[full, full, gspec],
        out_shape=[jax.ShapeDtypeStruct((rows, d), F32), jax.ShapeDtypeStruct((rows, d), BF16),
                   jax.ShapeDtypeStruct((1, d), F32)],
        args=(*args, h, g, dres), comm=comm)


def _wgrad_call(name, a, b, scale=1.0, comm=None):
    a3, b3 = a.ndim == 3, b.ndim == 3
    ns = a.shape[0] if a3 else (b.shape[0] if b3 else 1)
    rows, m = a.shape[-2:]
    n = b.shape[-1]
    tmm = m if a3 else _col_tile(m, 256)

    def body(a_ref, b_ref, o_ref):
        av = a_ref[0] if a3 else a_ref[...]
        bv = b_ref[0] if b3 else b_ref[...]
        res = _dot_tn(av, bv)
        if scale != 1.0:
            res = res * scale
        if a3 or b3:
            o_ref[0] = res
        else:
            o_ref[...] = res

    aspec = (pl.BlockSpec((1, rows, tmm), lambda s, j: (s, 0, j)) if a3
             else pl.BlockSpec((rows, tmm), lambda s, j: (0, j)))
    bspec = (pl.BlockSpec((1, rows, n), lambda s, j: (s, 0, 0)) if b3
             else pl.BlockSpec((rows, n), lambda s, j: (0, 0)))
    if a3 or b3:
        ospec = pl.BlockSpec((1, tmm, n), lambda s, j: (s, j, 0))
        oshape = jax.ShapeDtypeStruct((ns, m, n), F32)
    else:
        ospec = pl.BlockSpec((tmm, n), lambda s, j: (j, 0))
        oshape = jax.ShapeDtypeStruct((m, n), F32)
    return _hosted_call(
        body, name=name, grid=(ns, m // tmm), in_specs=[aspec, bspec], out_specs=[ospec], out_shape=[oshape],
        dims=("parallel", "parallel"), args=(a, b), comm=comm)[0]


def _mla_prep_bwd_call(z, dq, dk, dv, gql, gkvl, gqh, gkh, wuq, wuk, wuv, tabs, lp, tm, comm=None):
    rows = z.shape[0]
    tpe = lp // tm

    def body(z_ref, dq_ref, dk_ref, dv_ref, gql_ref, gkvl_ref, gqh_ref, gkh_ref, wuq_ref, wuk_ref, wuv_ref,
             c_ref, s1_ref, s2_ref, dz_ref, dqp_ref, dkn_ref, dvv_ref, dgql_ref, dgkvl_ref, dgqh_ref, dgkh_ref):
        gql, gkvl = gql_ref[...], gkvl_ref[...]
        gq, gk = gqh_ref[...], gkh_ref[...]
        wuq, wuk, wuv = wuq_ref[...], wuk_ref[...], wuv_ref[...]
        m = _mla_heads(z_ref[...], gql, gkvl, wuq, wuk, wuv)
        c, s1, s2 = c_ref[...], s1_ref[...], s2_ref[...]
        dgq = jnp.zeros((1, D_QKP), F32)
        dgk = jnp.zeros((1, D_QKP), F32)
        dkr = jnp.zeros((tm, D_QKP - D_NOPE), F32)
        for hd in range(HEADS):
            qn, rqh, knn, krn, rkh = m["heads"][hd]
            dqg = jnp.concatenate([dq_ref[hd, :, 0:D_NOPE].astype(F32),
                                   _rope_t(dq_ref[hd, :, D_NOPE:D_QKP].astype(F32), c, s1, s2)], axis=1)
            dgq = dgq + jnp.sum(dqg * qn, axis=0, keepdims=True)
            dqn = dqg * gq
            dqr = rqh * (dqn - qn * (jnp.sum(dqn * qn, axis=-1, keepdims=True) * (1.0 / D_QK)))
            dqp_ref[:, hd * D_QKP:(hd + 1) * D_QKP] = dqr.astype(BF16)
            kn_full = jnp.concatenate([knn, krn], axis=1)
            dkg = jnp.concatenate([dk_ref[hd, :, 0:D_NOPE].astype(F32),
                                   _rope_t(dk_ref[hd, :, D_NOPE:D_QKP].astype(F32), c, s1, s2)], axis=1)
            dgk = dgk + jnp.sum(dkg * kn_full, axis=0, keepdims=True)
            dkn = dkg * gk
            dkraw = rkh * (dkn - kn_full * (jnp.sum(dkn * kn_full, axis=-1, keepdims=True) * (1.0 / D_QK)))
            dkn_ref[:, hd * D_NOPE:(hd + 1) * D_NOPE] = dkraw[:, 0:D_NOPE].astype(BF16)
            dkr = dkr + dkraw[:, D_NOPE:D_QKP]
            dvv_ref[:, hd * D_V:(hd + 1) * D_V] = dv_ref[hd]
        dcqn = _dot(dqp_ref[...], wuq)
        dckvn = _dot_nt(dkn_ref[...], wuk) + _dot_nt(dvv_ref[...], wuv)
        dz_ref[:, 0:Q_RANK] = _rms_bwd(dcqn * gql, m["nq"], m["rq"], Q_RANK).astype(BF16)
        dz_ref[:, Q_RANK:Z_KR] = _rms_bwd(dckvn * gkvl, m["nkv"], m["rkv"], KV_RANK).astype(BF16)
        dz_ref[:, Z_KR:Z_MLA] = dkr.astype(BF16)
        parts = [(dgql_ref, jnp.sum(dcqn * m["nq"], axis=0, keepdims=True)),
                 (dgkvl_ref, jnp.sum(dckvn * m["nkv"], axis=0, keepdims=True)), (dgqh_ref, dgq), (dgkh_ref, dgk)]

        @pl.when(pl.program_id(0) == 0)
        def _():
            for ref, val in parts:
                ref[...] = val

        @pl.when(pl.program_id(0) != 0)
        def _():
            for ref, val in parts:
                ref[...] += val

    def const(shape):
        return pl.BlockSpec(shape, lambda i: tuple(0 for _ in shape))

    tab = pl.BlockSpec((tm, 128), lambda i: (i % tpe, 0))
    hq = pl.BlockSpec((HEADS, tm, D_QKP), lambda i: (0, i, 0))
    hv = pl.BlockSpec((HEADS, tm, D_V), lambda i: (0, i, 0))

    def rowspec(n):
        return pl.BlockSpec((tm, n), lambda i: (i, 0))

    return _hosted_call(
        body, name="mla_prep_bwd", grid=(rows // tm,),
        in_specs=[rowspec(Z_MLA), hq, hq, hv, const((1, Q_RANK)), const((1, KV_RANK)), const((1, D_QKP)),
                  const((1, D_QKP)), const((HEADS * D_QKP, Q_RANK)), const((KV_RANK, HEADS * D_NOPE)),
                  const((KV_RANK, HEADS * D_V)), tab, tab, tab],
        out_specs=[rowspec(Z_MLA), rowspec(HEADS * D_QKP), rowspec(HEADS * D_NOPE), rowspec(HEADS * D_V),
                   const((1, Q_RANK)), const((1, KV_RANK)), const((1, D_QKP)), const((1, D_QKP))],
        out_shape=[jax.ShapeDtypeStruct((rows, Z_MLA), BF16), jax.ShapeDtypeStruct((rows, HEADS * D_QKP), BF16),
                   jax.ShapeDtypeStruct((rows, HEADS * D_NOPE), BF16), jax.ShapeDtypeStruct((rows, HEADS * D_V), BF16),
                   jax.ShapeDtypeStruct((1, Q_RANK), F32), jax.ShapeDtypeStruct((1, KV_RANK), F32),
                   jax.ShapeDtypeStruct((1, D_QKP), F32), jax.ShapeDtypeStruct((1, D_QKP), F32)],
        args=(z, dq, dk, dv, gql, gkvl, gqh, gkh, wuq, wuk, wuv, *tabs), comm=comm)


def _local_step(h0, target, w, nb, lp, sched=None):
    tm = _row_tile(nb * lp, 1408)
    te = _row_tile(lp, 512)
    tabs = _rope_tables(lp)
    g = {}
    if sched is None:
        host = lambda stage: None
    else:
        sched.g = g
        host = sched.host

    def ffn_fwd(tag, h, split):
        u = _rmsnorm_call(tag + "_norm", h, w[tag + "_norm"], te)
        if split:
            gate = _ffn_gate_call(tag + "_gate", u, w[tag + "_w_gate"], tm, host(tag + "_gate"))
            up, act = _ffn_upact_call(tag + "_upact", u, w[tag + "_w_up"], gate, tm, host(tag + "_upact"))
        else:
            gate, up, act = _ffn_up_call(tag + "_up", u, w[tag + "_w_gate"], w[tag + "_w_up"], tm, host(tag + "_up"))
        return _ffn_down_call(tag + "_down", act, w[tag + "_w_down"], h, te, host(tag + "_down")), (u, gate, up, act)

    def ffn_bwd(tag, h, saved, dh, dhb):
        u, gate, up, act = saved
        dgate, dup = _ffn_dact_call(tag + "_dact", dhb, w[tag + "_w_down"], gate, up, tm, host(tag + "_dact"))
        g[tag + "_w_down"] = _wgrad_call(tag + "_dwd", act, dhb, 0.5, host(tag + "_dwd"))
        g[tag + "_w_gate"] = _wgrad_call(tag + "_dwg", dgate, u, 1.0, host(tag + "_dwg"))
        g[tag + "_w_up"] = _wgrad_call(tag + "_dwu", dup, u, 1.0, host(tag + "_dwu"))
        dh_in, dhb_in, g[tag + "_norm"] = _norm_in_bwd_call(
            tag + "_din", [(dgate, w[tag + "_w_gate"]), (dup, w[tag + "_w_up"])], h, w[tag + "_norm"], dh, te,
            host(tag + "_din"))
        return dh_in, dhb_in

    h1, s1 = ffn_fwd("ffn1", h0, True)
    un = _rmsnorm_call("mix_norm", h1, w["mix_norm"], te)
    z = _mm_call("mix_in", un, w["w_in"], tm, F32)
    mla_w = (w["q_latent_norm"], w["kv_latent_norm"], w["q_head_norm"], w["k_head_norm"], w["w_uq"], w["w_uk"],
             w["w_uv"])
    q, k, v, cqn, ckvn = _mla_prep_call(z, *mla_w, tabs, lp, te)
    o, lse = _attn_fwd_call(q, k, v, nb, lp, host("attn_fwd"))
    lru_w = (w["conv_w"], w["conv_b"], w["gate_a_w"], w["gate_a_b"], w["gate_x_w"], w["gate_x_b"], w["lru_lambda"])
    yl, hs = _lru_fwd_call(z, *lru_w, nb, lp, host("lru_fwd"))
    y, h2 = _mix_out_call(o, yl, w["attn_out_norm"], w["lru_out_norm"], w["w_out"], h1, te)
    h3, s2 = ffn_fwd("ffn2", h2, False)
    dh3, dh3b, g["final_norm"], loss = _final_call(h3, w["final_norm"], target, lp, te)
    g["loss"] = loss

    dh2, dh2b = ffn_bwd("ffn2", h2, s2, dh3, dh3b)
    g["w_out"] = _wgrad_call("dw_out", y, dh2b)
    dya, dyl, g["attn_out_norm"], g["lru_out_norm"] = _mix_out_bwd_call(
        dh2b, w["w_out"], o, yl, w["attn_out_norm"], w["lru_out_norm"], te)
    dq, dk, dv = _attn_bwd_call(q, k, v, o, lse, dya, nb, lp, host("attn_bwd"))
    (dz_mla, dqp, dkn, dvv, g["q_latent_norm"], g["kv_latent_norm"], g["q_head_norm"],
     g["k_head_norm"]) = _mla_prep_bwd_call(z, dq, dk, dv, *mla_w, tabs, lp, te, host("mla_prep_bwd"))
    g["w_uq"] = _wgrad_call("dw_uq", dqp, cqn)
    g["w_uk"] = _wgrad_call("dw_uk", ckvn, dkn)
    g["w_uv"] = _wgrad_call("dw_uv", ckvn, dvv)
    (du, dgt, g["conv_w"], g["conv_b"], g["gate_a_w"], g["gate_a_b"], g["gate_x_w"], g["gate_x_b"],
     g["lru_lambda"]) = _lru_bwd_call(z, hs, dyl, *lru_w, nb, lp, host("lru_bwd"))
    win = w["w_in"]
    g["w_in"] = jnp.concatenate(
        [_wgrad_call("dw_in_mla", dz_mla, un), _wgrad_call("dw_in_u", du, un), _wgrad_call("dw_in_g", dgt, un)],
        axis=0)
    dh1, dh1b, g["mix_norm"] = _norm_in_bwd_call(
        "mix_din", [(dz_mla, win[0:Z_MLA]), (du, win[Z_U:Z_G]), (dgt, win[Z_G:Z_W])], h1, w["mix_norm"], dh2, te,
        host("mix_din"))
    dh0, _ = ffn_bwd("ffn1", h0, s1, dh1, dh1b)
    return loss, dh0, g


def _place():
    x, y, c = lax.axis_index("x"), lax.axis_index("y"), lax.axis_index("c")
    return x, y, c, [(1 - x, y), (x, 1 - y), (1 - x, 1 - y)]


def _any_specs(n):
    return [pl.BlockSpec(memory_space=pl.ANY)] * n


def _remote(src, dst, sems, k, dev):
    send_sems, recv_sems, base = sems
    return pltpu.make_async_remote_copy(src_ref=src, dst_ref=dst, send_sem=send_sems.at[base + k],
                                        recv_sem=recv_sems.at[base + k], device_id=dev, device_id_type=MESH)


EW_VMEM_BYTES = 24 * 1024 * 1024


def _fit_rows(rows, cols, blocks):
    return _row_tile(rows, max(16, int(EW_VMEM_BYTES // (8 * blocks)) // cols))


class _Geom:
    def __init__(self, n0, n1, blocks=1.0):
        self.n0, self.n1 = n0, n1
        self.axis = 0 if n0 % 32 == 0 else 1
        self.h0, self.h1 = (n0 // 2, n1) if self.axis == 0 else (n0, n1 // 2)
        self.tr = _fit_rows(self.h0, self.h1, blocks)
        self.nblk = self.h0 // self.tr

    def half_ref(self, ref, lead, idx):
        if self.axis == 0:
            return ref.at[(*lead, pl.ds(idx * self.h0, self.h0))]
        return ref.at[(*lead, slice(None), pl.ds(idx * self.h1, self.h1))]

    def half_block(self, lead, i, idx):
        return (*lead, idx * self.nblk + i, 0) if self.axis == 0 else (*lead, i, idx)


class _Comm:
    def __init__(self, ins, out_shapes, aliases, n_sems, start, finish, deliver):
        self.ins, self.out_shapes, self.aliases, self.n_sems = list(ins), list(out_shapes), dict(aliases), n_sems
        self.start, self.finish, self.deliver = start, finish, deliver

    def scratch(self):
        return [pltpu.SemaphoreType.DMA((self.n_sems,)), pltpu.SemaphoreType.DMA((self.n_sems,))]


def _comm_call(name, comm):
    n_in = len(comm.ins)

    def body(*refs):
        ins, outs, sems = refs[:n_in], refs[n_in:-2], (*refs[-2:], 0)
        comm.start(ins, outs, sems)
        comm.finish(ins, outs, sems)

    res = pl.pallas_call(
        body, name=name, out_shape=comm.out_shapes, in_specs=_any_specs(n_in),
        out_specs=_any_specs(len(comm.out_shapes)), input_output_aliases=comm.aliases,
        scratch_shapes=comm.scratch())(*comm.ins)
    return comm.deliver(list(res))


def _hosted_call(body, *, name, grid, in_specs, out_specs, out_shape, args, scratch_shapes=(), dims=None, comm=None):
    in_specs, out_specs, out_shape = list(in_specs), list(out_specs), list(out_shape)
    if comm is None:
        return pl.pallas_call(
            body, name=name, grid=grid, in_specs=in_specs, out_specs=out_specs, out_shape=out_shape,
            scratch_shapes=list(scratch_shapes),
            compiler_params=_params(dims or ("arbitrary",) * len(grid)))(*args)
    n_in, n_out, n_ci, n_co = len(in_specs), len(out_specs), len(comm.ins), len(comm.out_shapes)

    def wrapped(*refs):
        ins, cins = refs[:n_in], refs[n_in:n_in + n_ci]
        outs = refs[n_in + n_ci:n_in + n_ci + n_out]
        couts = refs[n_in + n_ci + n_out:n_in + n_ci + n_out + n_co]
        scratch, sems = refs[n_in + n_ci + n_out + n_co:-2], (*refs[-2:], 0)
        first = functools.reduce(jnp.logical_and, [pl.program_id(k) == 0 for k in range(len(grid))])
        last = functools.reduce(jnp.logical_and, [pl.program_id(k) == grid[k] - 1 for k in range(len(grid))])

        @pl.when(first)
        def _():
            comm.start(cins, couts, sems)

        body(*ins, *outs, *scratch)

        @pl.when(last)
        def _():
            comm.finish(cins, couts, sems)

    res = pl.pallas_call(
        wrapped, name=name, grid=grid, in_specs=in_specs + _any_specs(n_ci), out_specs=out_specs + _any_specs(n_co),
        out_shape=out_shape + comm.out_shapes,
        input_output_aliases={n_in + i: n_out + o for i, o in comm.aliases.items()},
        scratch_shapes=list(scratch_shapes) + comm.scratch(),
        compiler_params=_params(("arbitrary",) * len(grid)))(*args, *comm.ins)
    comm.deliver(list(res[n_out:]))
    return list(res[:n_out])


def _gather_comm(bufs, deliver):
    n = len(bufs)
    geoms = [_Geom(*b.shape[1:]) for b in bufs]

    def first(outs, sems):
        x, y, c, chips = _place()
        cps = []
        for a in range(n):
            mine = geoms[a].half_ref(outs[a], (2 * x + y,), c)
            cps += [_remote(mine, mine, sems, 6 * a + j, (cx, cy, c)) for j, (cx, cy) in enumerate(chips)]
        return cps

    def start(ins, outs, sems):
        for cp in first(outs, sems):
            cp.start()

    def finish(ins, outs, sems):
        x, y, c, chips = _place()
        sib = (x, y, 1 - c)
        passed = []
        for a in range(n):
            for j, (cx, cy) in enumerate(chips):
                land = geoms[a].half_ref(outs[a], (2 * cx + cy,), c)
                _remote(land, land, sems, 6 * a + j, sib).wait_recv()
                cp = _remote(land, land, sems, 6 * a + 3 + j, sib)
                cp.start()
                passed.append(cp)
        for a in range(n):
            for j, (cx, cy) in enumerate(chips):
                land = geoms[a].half_ref(outs[a], (2 * cx + cy,), 1 - c)
                _remote(land, land, sems, 6 * a + 3 + j, sib).wait_recv()
        for cp in first(outs, sems) + passed:
            cp.wait_send()

    return _Comm(bufs, [jax.ShapeDtypeStruct(b.shape, b.dtype) for b in bufs], {a: a for a in range(n)}, 6 * n,
                 start, finish, deliver)


def _reduce_pair_comm(grads, deliver):
    n = len(grads)
    geoms = [_Geom(*a.shape[1:]) for a in grads]

    def copies(ins, outs, sems):
        x, y, c, _ = _place()
        return [_remote(geoms[a].half_ref(ins[a], (slice(None),), 1 - c), outs[a], sems, a, (x, y, 1 - c))
                for a in range(n)]

    def start(ins, outs, sems):
        for cp in copies(ins, outs, sems):
            cp.start()

    def finish(ins, outs, sems):
        cps = copies(ins, outs, sems)
        for cp in cps:
            cp.wait_recv()
        for cp in cps:
            cp.wait_send()

    shapes = [jax.ShapeDtypeStruct((N_SHARD, g.h0, g.h1), a.dtype) for a, g in zip(grads, geoms)]
    return _Comm(grads, shapes, {}, n, start, finish, deliver)


def _reduce_chips_comm(parts, deliver):
    n = len(parts)

    def copies(ins, outs, sems):
        x, y, c, chips = _place()
        return [_remote(ins[a].at[2 * cx + cy], outs[a].at[j], sems, 3 * a + j, (cx, cy, c))
                for a in range(n) for j, (cx, cy) in enumerate(chips)]

    def start(ins, outs, sems):
        for cp in copies(ins, outs, sems):
            cp.start()

    def finish(ins, outs, sems):
        cps = copies(ins, outs, sems)
        for cp in cps:
            cp.wait_recv()
        for cp in cps:
            cp.wait_send()

    shapes = [jax.ShapeDtypeStruct((3,) + a.shape[1:], a.dtype) for a in parts]
    return _Comm(parts, shapes, {}, 3 * n, start, finish, deliver)


def _share_pair_comm(bufs, deliver):
    n = len(bufs)
    geoms = [_Geom(*b.shape) for b in bufs]

    def copies(outs, sems):
        x, y, c, _ = _place()
        cps = []
        for a in range(n):
            mine = geoms[a].half_ref(outs[a], (), c)
            cps.append(_remote(mine, mine, sems, a, (x, y, 1 - c)))
        return cps

    def start(ins, outs, sems):
        for cp in copies(outs, sems):
            cp.start()

    def finish(ins, outs, sems):
        x, y, c, _ = _place()
        for a in range(n):
            land = geoms[a].half_ref(outs[a], (), 1 - c)
            _remote(land, land, sems, a, (x, y, 1 - c)).wait_recv()
        for cp in copies(outs, sems):
            cp.wait_send()

    return _Comm(bufs, [jax.ShapeDtypeStruct(b.shape, b.dtype) for b in bufs], {a: a for a in range(n)}, n,
                 start, finish, deliver)


def _small_comm(pack, deliver):
    r, d = pack.shape

    def copies(ins, outs, sems):
        x, y, c, _ = _place()
        cps = []
        for k in range(1, 8):
            peer = (x ^ ((k >> 2) & 1), y ^ ((k >> 1) & 1), c ^ (k & 1))
            cps.append(_remote(ins[0], outs[0].at[4 * x + 2 * y + c], sems, k - 1, peer))
        return cps

    def start(ins, outs, sems):
        for cp in copies(ins, outs, sems):
            cp.start()

    def finish(ins, outs, sems):
        cps = copies(ins, outs, sems)
        for cp in cps:
            cp.wait_recv()
        for cp in cps:
            cp.wait_send()

    return _Comm([pack], [jax.ShapeDtypeStruct((8, r, d), pack.dtype)], {}, 7, start, finish, deliver)


def _join_comms(comms):
    comms = [c for c in comms if c is not None]
    if len(comms) <= 1:
        return comms[0] if comms else None
    ins, out_shapes, aliases, spans, n_sems = [], [], {}, [], 0
    for c in comms:
        aliases.update({len(ins) + i: len(out_shapes) + o for i, o in c.aliases.items()})
        spans.append((len(ins), len(ins) + len(c.ins), len(out_shapes), len(out_shapes) + len(c.out_shapes), n_sems))
        ins += c.ins
        out_shapes += c.out_shapes
        n_sems += c.n_sems

    def run(which):
        def go(all_ins, all_outs, sems):
            for c, (i0, i1, o0, o1, base) in zip(comms, spans):
                getattr(c, which)(all_ins[i0:i1], all_outs[o0:o1], (sems[0], sems[1], sems[2] + base))
        return go

    def deliver(outs):
        for c, (_, _, o0, o1, _) in zip(comms, spans):
            c.deliver(outs[o0:o1])
        return outs

    return _Comm(ins, out_shapes, aliases, n_sems, run("start"), run("finish"), deliver)


def _ew_call(name, fn, ins, out_dtypes):
    shape = ins[0].shape
    cols = shape[-1]
    rows = 1
    for s_ in shape[:-1]:
        rows *= s_
    ins2 = [a.reshape(rows, cols) for a in ins]
    tr = rows
    for t in range(16, min(rows, max(16, (1 << 19) // cols)) + 1, 16):
        if rows % t == 0:
            tr = t
    no = len(out_dtypes)

    def body(*refs):
        outs = fn(*[r[...] for r in refs[:len(ins2)]])
        for ref, val in zip(refs[len(ins2):], outs):
            ref[...] = val.astype(ref.dtype)

    spec = pl.BlockSpec((tr, cols), lambda i: (i, 0))
    res = pl.pallas_call(
        body, name=name, grid=(rows // tr,), in_specs=[spec] * len(ins2), out_specs=[spec] * no,
        out_shape=[jax.ShapeDtypeStruct((rows, cols), dt) for dt in out_dtypes],
        compiler_params=_params(("parallel",)))(*ins2)
    return [r.reshape(shape) for r in res]


def _adamw_math(w, g, m, v):
    m = ADAM_B1 * m + (1.0 - ADAM_B1) * g
    v = ADAM_B2 * v + (1.0 - ADAM_B2) * (g * g)
    m_hat = m / (1.0 - ADAM_B1 ** ADAM_STEP)
    v_hat = v / (1.0 - ADAM_B2 ** ADAM_STEP)
    delta = -ADAM_LR * (m_hat / (jnp.sqrt(v_hat) + ADAM_EPS) + ADAM_WD * w)
    return delta, m, v


def _adamw_call(name, w, g, m, v):
    return _ew_call(name, _adamw_math, [w, g, m, v], [F32, F32, F32])


def _tiled_call(name, fn, place, grid, in_items, out_items):
    ni = len(in_items)

    def body(place_ref, *refs):
        vals = fn(*[r[...] for r in refs[:ni]])
        for ref, val in zip(refs[ni:], vals):
            ref[...] = val.astype(ref.dtype)

    spec = pltpu.PrefetchScalarGridSpec(
        num_scalar_prefetch=1, grid=grid,
        in_specs=[pl.BlockSpec(blk, imap) for _, blk, imap in in_items],
        out_specs=[pl.BlockSpec(blk, imap) for _, _, blk, imap in out_items])
    return pl.pallas_call(
        body, name=name, grid_spec=spec,
        out_shape=[jax.ShapeDtypeStruct(shp, dt) for shp, dt, _, _ in out_items],
        compiler_params=_params(("arbitrary",) * len(grid)))(place, *[a for a, _, _ in in_items])


def _cast_call(name, place, shards):
    n0, n1 = shards[0].shape
    tr = _fit_rows(n0, n1, 1.5 * len(shards))
    ins = [(a, (tr, n1), lambda i, p: (i, 0)) for a in shards]
    outs = [((N_SHARD, n0, n1), BF16, (1, tr, n1), lambda i, p: (p[0], i, 0)) for _ in shards]
    return _tiled_call(name, lambda *v: [x[None] for x in v], place, (n0 // tr,), ins, outs)


def _pair_sum_call(name, place, fulls, gots):
    k = len(fulls)
    g = _Geom(*fulls[0].shape[1:], blocks=2.5 * k)
    blk = (1, g.tr, g.h1)
    ins = [(a, blk, lambda s, i, p: g.half_block((s,), i, p[1])) for a in fulls]
    ins += [(a, blk, lambda s, i, p: (s, i, 0)) for a in gots]
    outs = [((N_SHARD, g.h0, g.h1), BF16, blk, lambda s, i, p: (s, i, 0)) for _ in fulls]
    return _tiled_call(name, lambda *v: [v[j] + v[k + j] for j in range(k)], place, (N_SHARD, g.nblk), ins, outs)


def _chip_sum_call(name, place, fulls, gots, recvs):
    k = len(fulls)
    g = _Geom(*fulls[0].shape[1:], blocks=4.5 * k)
    blk = (1, g.tr, g.h1)
    ins = [(a, blk, lambda i, p: g.half_block((p[0],), i, p[1])) for a in fulls]
    ins += [(a, blk, lambda i, p: (p[0], i, 0)) for a in gots]
    ins += [(a, (3, g.tr, g.h1), lambda i, p: (0, i, 0)) for a in recvs]
    outs = [((g.n0, g.n1), F32, (g.tr, g.h1), lambda i, p: g.half_block((), i, p[1])) for _ in fulls]

    def fn(*v):
        res = []
        for j in range(k):
            r = v[2 * k + j].astype(F32)
            res.append(v[j][0] + v[k + j][0] + r[0] + r[1] + r[2])
        return res

    return _tiled_call(name, fn, place, (g.nblk,), ins, outs)


def _adamw_group_call(name, ws, gs, ms, vs, comm=None):
    k = len(ws)
    n0, n1 = ws[0].shape
    tr = _fit_rows(n0, n1, 8 * k)
    spec = pl.BlockSpec((tr, n1), lambda i: (i, 0))

    def body(*refs):
        for j in range(k):
            g = refs[k + j][...]
            delta, m, vv = _adamw_math(refs[j][...], g, refs[2 * k + j][...], refs[3 * k + j][...])
            for ref, val in zip(refs[4 * k + 4 * j:4 * k + 4 * j + 4], (g, delta, m, vv)):
                ref[...] = val

    flat = _hosted_call(
        body, name=name, grid=(n0 // tr,), in_specs=[spec] * (4 * k), out_specs=[spec] * (4 * k),
        out_shape=[jax.ShapeDtypeStruct((n0, n1), F32)] * (4 * k), dims=("parallel",),
        args=(*ws, *gs, *ms, *vs), comm=comm)
    return [flat[4 * j:4 * j + 4] for j in range(k)]


def _small_update_call(me, early, own_early, late, own_late, wp, mp, vp):
    nd, r, d = early.shape

    def body(me_ref, e_ref, oe_ref, l_ref, ol_ref, w_ref, m_ref, v_ref, gs_ref, d_ref, nm_ref, nv_ref):
        mine = me_ref[0]

        def total(g_ref, own_ref):
            acc = None
            for k in range(nd):
                part = jnp.where(mine == k, own_ref[...], g_ref[k])
                acc = part if acc is None else acc + part
            return acc

        gs = total(e_ref, oe_ref)
        ls = total(l_ref, ol_ref)
        gs_ref[...] = gs
        first = gs[0:8] + ls[0:8]
        gs_ref[0:8, :] = first
        gs_ref[ROW_META:ROW_META + N_META, :] = gs[ROW_META:ROW_META + N_META] + ls[8:8 + N_META]
        grads = jnp.concatenate([first, gs[8:SMALL_ADAM_ROWS]], axis=0)
        delta, m, v = _adamw_math(w_ref[...], grads, m_ref[...], v_ref[...])
        d_ref[...] = delta
        nm_ref[...] = m
        nv_ref[...] = v

    vm = pl.BlockSpec(memory_space=pltpu.VMEM)
    ashape = jax.ShapeDtypeStruct((SMALL_ADAM_ROWS, d), F32)
    return pl.pallas_call(
        body, name="small_update", in_specs=[pl.BlockSpec(memory_space=pltpu.SMEM)] + [vm] * 7, out_specs=[vm] * 4,
        out_shape=[jax.ShapeDtypeStruct((r, d), F32), ashape, ashape, ashape],
        compiler_params=pltpu.CompilerParams(vmem_limit_bytes=VMEM_LIMIT_BYTES))(
            me, early, own_early, late, own_late, wp, mp, vp)


SMALL_NAMES = ["ffn1_norm", "mix_norm", "ffn2_norm", "final_norm", "q_latent_norm", "kv_latent_norm",
               "q_head_norm", "k_head_norm", "conv_b", "gate_a_b", "gate_x_b", "lru_lambda", "attn_out_norm",
               "lru_out_norm"]
ROW_CONV_W = 14
ROW_GATE_A = 16
ROW_GATE_X = 48
ROW_META = 80
ROW_LOSS = 96


def _row(a):
    flat = a.reshape(1, -1)
    return jnp.pad(flat, ((0, 0), (0, D_MODEL - flat.shape[1])))


def _pack_small(t, rows):
    parts = [_row(t[nm]) for nm in SMALL_NAMES]
    parts.append(t["conv_w"].reshape(2, D_MODEL))
    parts.append(t["gate_a_w"].reshape(32, D_MODEL))
    parts.append(t["gate_x_w"].reshape(32, D_MODEL))
    p = jnp.concatenate(parts, axis=0)
    return jnp.pad(p, ((0, rows - p.shape[0]), (0, 0)))


def _early_pack(g):
    gs = {nm: g.get(nm, jnp.zeros((1, D_MODEL), F32)) for nm in SMALL_NAMES}
    gs["q_head_norm"] = g["q_head_norm"][:, 0:D_QK]
    gs["k_head_norm"] = g["k_head_norm"][:, 0:D_QK]
    for nm in ("conv_b", "gate_a_b", "gate_x_b", "lru_lambda"):
        gs[nm] = g[nm].reshape(1, LRU_W)
    gs["conv_w"] = g["conv_w"].transpose(1, 0, 2).reshape(CONV_K, LRU_W)
    gs["gate_a_w"] = _gate_blocks(g["gate_a_w"])
    gs["gate_x_w"] = _gate_blocks(g["gate_x_w"])
    return jnp.concatenate([_pack_small(gs, ROW_META), jnp.zeros((N_META, D_MODEL), F32), _row(g["loss"][:, 0:1]),
                            jnp.zeros((SMALL_ROWS - ROW_LOSS - 1, D_MODEL), F32)], axis=0)


def _unpack_small(p, like):
    out = {}
    for k, nm in enumerate(SMALL_NAMES):
        out[nm] = p[k, 0:like[nm].size].reshape(like[nm].shape)
    out["gate_a_w"] = p[ROW_GATE_A:ROW_GATE_A + 32].reshape(like["gate_a_w"].shape)
    out["gate_x_w"] = p[ROW_GATE_X:ROW_GATE_X + 32].reshape(like["gate_x_w"].shape)
    return out


def _gate_dense(wg):
    w4 = wg[0].reshape(N_LRU_TILES, 2, 64, 64)
    zero = jnp.zeros((N_LRU_TILES, 64, 64), wg.dtype)
    top = jnp.concatenate([w4[:, 0], zero], axis=2)
    bot = jnp.concatenate([zero, w4[:, 1]], axis=2)
    return jnp.concatenate([top, bot], axis=1).astype(BF16)


def _gate_blocks(dw):
    return jnp.stack([dw[:, 0:64, 0:64], dw[:, 64:128, 64:128]], axis=1).reshape(8, 64, 64)


BIG_NAMES = ["ffn1_w_gate", "ffn1_w_up", "ffn1_w_down", "w_in", "w_uq", "w_uk", "w_uv", "w_out", "ffn2_w_gate",
             "ffn2_w_up", "ffn2_w_down"]
BIG_GROUPS = [["ffn1_w_gate", "ffn1_w_up", "ffn1_w_down", "ffn2_w_gate", "ffn2_w_up", "ffn2_w_down"], ["w_in"],
              ["w_uq"], ["w_uk", "w_uv"], ["w_out"]]
TRANSPOSED = ("ffn1_w_gate", "ffn1_w_up", "ffn2_w_gate", "ffn2_w_up", "w_in", "w_uq")


def _to2d(nm, a):
    return a[0].T if nm in TRANSPOSED else a[0]


def _from2d(nm, a):
    return (a.T if nm in TRANSPOSED else a)[None]


WEIGHT_NAMES = ["meta_tokens", "ffn1_norm", "ffn1_w_gate", "ffn1_w_up", "ffn1_w_down", "mix_norm", "w_in",
                "q_latent_norm", "w_uq", "kv_latent_norm", "w_uk", "w_uv", "q_head_norm", "k_head_norm", "conv_w",
                "conv_b", "gate_a_w", "gate_a_b", "gate_x_w", "gate_x_b", "lru_lambda", "attn_out_norm",
                "lru_out_norm", "w_out", "ffn2_norm", "ffn2_w_gate", "ffn2_w_up", "ffn2_w_down", "final_norm"]


def _weight_from(nm, slots):
    if nm == "w_in":
        win = slots.reshape(IN_WIDTH, D_MODEL)
        return jnp.concatenate([win[0:Z_KR + D_ROPE], jnp.zeros((128 - D_ROPE, D_MODEL), BF16),
                                win[Z_KR + D_ROPE:]], axis=0)
    if nm == "w_uq":
        return jnp.pad(slots, ((0, 0), (0, D_QKP - D_QK), (0, 0))).reshape(HEADS * D_QKP, Q_RANK)
    if nm in ("w_uk", "w_uv"):
        return slots.transpose(1, 0, 2).reshape(KV_RANK, HEADS * D_NOPE)
    if nm == "w_out":
        return slots.reshape(D_MODEL, D_MODEL)
    return slots


def _small_weights(p, small):
    w = {nm: p[nm] for nm in SMALL_NAMES}
    w["q_head_norm"] = jnp.pad(p["q_head_norm"], ((0, 0), (0, D_QKP - D_QK)))
    w["k_head_norm"] = jnp.pad(p["k_head_norm"], ((0, 0), (0, D_QKP - D_QK)))
    w["conv_w"] = small[:, N_META:N_META + 2, :].reshape(N_SHARD, CONV_K, LRU_TILE)
    w["gate_a_w"] = _gate_dense(p["gate_a_w"])
    w["gate_x_w"] = _gate_dense(p["gate_x_w"])
    meta = small[:, 0:N_META, :].transpose(1, 0, 2).reshape(N_META, D_MODEL)
    return w, meta


def _full_weights(p, gathered, small):
    w, meta = _small_weights(p, small)
    w.update({nm: _weight_from(nm, gathered[nm]) for nm in BIG_NAMES})
    return w, meta


def _shard_grad(nm, g):
    if nm == "w_in":
        return jnp.concatenate([g[0:Z_KR + D_ROPE], g[Z_MLA:]], axis=0).reshape(N_SHARD, IN_WIDTH // N_SHARD, D_MODEL)
    if nm == "w_uq":
        return g.reshape(HEADS, D_QKP, Q_RANK)[:, 0:D_QK, :]
    if nm in ("w_uk", "w_uv"):
        return g.reshape(KV_RANK, HEADS, D_NOPE).transpose(1, 0, 2)
    if nm == "w_out":
        return g.reshape(N_SHARD, D_MODEL // N_SHARD, D_MODEL)
    return g


def _shard_grads(g):
    return {nm: _shard_grad(nm, g[nm]) for nm in BIG_NAMES}


GATHER_FIRST = ["ffn1_w_gate"]
GATHER_AT = {"ffn1_gate": ["ffn1_w_up"], "ffn1_upact": ["ffn1_w_down"],
             "ffn1_down": ["w_in", "w_uq", "w_uk", "w_uv", "w_out"], "attn_fwd": ["ffn2_w_down", "ffn2_w_gate"],
             "lru_fwd": ["ffn2_w_up"]}
PAIR_AT = [("ffn2_din", ["ffn2_w_gate", "ffn2_w_up", "ffn2_w_down"]),
           ("mix_din", ["w_out", "w_uq", "w_uk", "w_uv", "w_in"]),
           ("ffn1_dwg", ["ffn1_w_down"]), ("ffn1_dwu", ["ffn1_w_gate"]), ("ffn1_din", ["ffn1_w_up"])]
CHIPS_AT = [("attn_bwd", ["ffn2_w_down", "ffn2_w_gate"]), ("mla_prep_bwd", ["ffn2_w_up"]),
            ("ffn1_dact", ["w_out", "w_uq", "w_uk", "w_uv", "w_in"]),
            ("ffn1_dwu", ["ffn1_w_down"]), ("ffn1_din", ["ffn1_w_gate"]), (None, ["ffn1_w_up"])]
SMALL_EARLY_AT = "mix_din"


def _same_shape_groups(names):
    return [[nm for nm in grp if nm in names] for grp in BIG_GROUPS if any(nm in names for nm in grp)]


class _Sched:
    def __init__(self, place, w, slots):
        self.place, self.w, self.slots = place, w, slots
        self.g = None
        self.sharded, self.from_pair, self.chip_bf16, self.from_chips = {}, {}, {}, {}
        self.early = self.early_all = None

    def host(self, stage):
        comms = []
        if stage in GATHER_AT:
            comms.append(self.gather(GATHER_AT[stage]))
        comms += [self.chips(names) for at, names in CHIPS_AT if at == stage]
        comms += [self.pair(names) for at, names in PAIR_AT if at == stage]
        if stage == SMALL_EARLY_AT:
            comms.append(self.small_early())
        return _join_comms(comms)

    def small_early(self):
        self.early = _early_pack(self.g)

        def deliver(outs):
            self.early_all = outs[0]
            return outs

        return _small_comm(self.early, deliver)

    def gather(self, names):
        def deliver(outs):
            self.w.update({nm: _weight_from(nm, o) for nm, o in zip(names, outs)})
            return outs

        return _gather_comm([self.slots[nm] for nm in names], deliver)

    def pair(self, names):
        self.sharded.update({nm: _shard_grad(nm, self.g[nm]) for nm in names})

        def deliver(outs):
            self.from_pair.update(zip(names, outs))
            for grp in _same_shape_groups(names):
                sums = _pair_sum_call("pair_sum_" + grp[0], self.place, [self.sharded[nm] for nm in grp],
                                      [self.from_pair[nm] for nm in grp])
                self.chip_bf16.update(zip(grp, sums))
            return outs

        return _reduce_pair_comm([self.sharded[nm] for nm in names], deliver)

    def chips(self, names):
        def deliver(outs):
            self.from_chips.update(zip(names, outs))
            return outs

        return _reduce_chips_comm([self.chip_bf16[nm] for nm in names], deliver)

    def chip_sums(self, names):
        out = {}
        for grp in _same_shape_groups(names):
            sums = _chip_sum_call("chip_sum_" + grp[0], self.place, [self.sharded[nm] for nm in grp],
                                  [self.from_pair[nm] for nm in grp], [self.from_chips[nm] for nm in grp])
            out.update(zip(grp, sums))
        return out


def kernel(x, meta_tokens, ffn1_norm, ffn1_w_gate, ffn1_w_up, ffn1_w_down, mix_norm, w_in, q_latent_norm, w_uq, kv_latent_norm, w_uk, w_uv, q_head_norm, k_head_norm, conv_w, conv_b, gate_a_w, gate_a_b, gate_x_w, gate_x_b, lru_lambda, attn_out_norm, lru_out_norm, w_out, ffn2_norm, ffn2_w_gate, ffn2_w_up, ffn2_w_down, final_norm, loss_target, m_meta_tokens, m_ffn1_norm, m_ffn1_w_gate, m_ffn1_w_up, m_ffn1_w_down, m_mix_norm, m_w_in, m_q_latent_norm, m_w_uq, m_kv_latent_norm, m_w_uk, m_w_uv, m_q_head_norm, m_k_head_norm, m_conv_w, m_conv_b, m_gate_a_w, m_gate_a_b, m_gate_x_w, m_gate_x_b, m_lru_lambda, m_attn_out_norm, m_lru_out_norm, m_w_out, m_ffn2_norm, m_ffn2_w_gate, m_ffn2_w_up, m_ffn2_w_down, m_final_norm, v_meta_tokens, v_ffn1_norm, v_ffn1_w_gate, v_ffn1_w_up, v_ffn1_w_down, v_mix_norm, v_w_in, v_q_latent_norm, v_w_uq, v_kv_latent_norm, v_w_uk, v_w_uv, v_q_head_norm, v_k_head_norm, v_conv_w, v_conv_b, v_gate_a_w, v_gate_a_b, v_gate_x_w, v_gate_x_b, v_lru_lambda, v_attn_out_norm, v_lru_out_norm, v_w_out, v_ffn2_norm, v_ffn2_w_gate, v_ffn2_w_up, v_ffn2_w_down, v_final_norm):
    args = locals()
    p = {nm: args[nm] for nm in WEIGHT_NAMES}
    mom = {nm: args["m_" + nm] for nm in WEIGHT_NAMES}
    var = {nm: args["v_" + nm] for nm in WEIGHT_NAMES}
    nb, seq, d = x.shape
    lp = CHUNK + seq
    xi, yi, ci = lax.axis_index("x"), lax.axis_index("y"), lax.axis_index("c")
    chip = 2 * xi + yi

    place = jnp.stack([chip, ci]).astype(jnp.int32)
    p2 = {nm: _to2d(nm, p[nm]) for nm in BIG_NAMES}
    m2 = {nm: _to2d(nm, mom[nm]) for nm in BIG_NAMES}
    v2 = {nm: _to2d(nm, var[nm]) for nm in BIG_NAMES}

    slots = {}
    for grp in BIG_GROUPS:
        for nm, buf in zip(grp, _cast_call("cast_" + grp[0], place, [p2[nm] for nm in grp])):
            slots[nm] = buf
    small_shard = jnp.concatenate(
        [meta_tokens, conv_w[0].reshape(2, 2 * LRU_TILE), jnp.zeros((14, 2 * LRU_TILE), F32)], axis=0)
    small_slots = lax.dynamic_update_slice(jnp.zeros((N_SHARD,) + small_shard.shape, F32), small_shard[None],
                                           (chip, 0, 0))
    first = _comm_call("gather_first", _gather_comm([slots[nm] for nm in GATHER_FIRST] + [small_slots], lambda o: o))
    w, meta = _small_weights(p, first[-1])
    w.update({nm: _weight_from(nm, o) for nm, o in zip(GATHER_FIRST, first[:-1])})
    sched = _Sched(place, w, slots)

    h0 = jnp.concatenate(
        [jnp.zeros((nb, PAD_ROWS, d), F32), jnp.broadcast_to(meta[None], (nb, N_META, d)), x], axis=1)
    target = jnp.pad(loss_target, ((0, 0), (CHUNK, 0), (0, 0)))
    loss_part, dh0, g = _local_step(h0.reshape(nb * lp, d), target.reshape(nb * lp, d), w, nb, lp, sched)
    dh0 = dh0.reshape(nb, lp, d)
    grad_x = dh0[:, CHUNK:, :]

    late = jnp.concatenate([g["ffn1_norm"], g["mix_norm"], jnp.zeros((6, D_MODEL), F32),
                            jnp.sum(dh0[:, PAD_ROWS:CHUNK, :], axis=0)], axis=0)
    late_all = _comm_call("gather_late", _small_comm(late, lambda o: o))[0]
    small_like = {nm: p[nm] for nm in SMALL_NAMES + ["gate_a_w", "gate_x_w"]}

    def pack_w(t):
        tt = {nm: t[nm] for nm in SMALL_NAMES + ["gate_a_w", "gate_x_w"]}
        tt["conv_w"] = jnp.zeros((CONV_K, LRU_W), F32)
        return _pack_small(tt, SMALL_ADAM_ROWS)

    me = (4 * xi + 2 * yi + ci).astype(jnp.int32).reshape(1)
    gsum, dsm, msm, vsm = _small_update_call(me, sched.early_all, sched.early, late_all, late, pack_w(p),
                                             pack_w(mom), pack_w(var))
    grads = _unpack_small(gsum, small_like)
    delta = _unpack_small(dsm, small_like)
    new_m = _unpack_small(msm, small_like)
    new_v = _unpack_small(vsm, small_like)
    loss = gsum[ROW_LOSS, 0]
    gmeta = gsum[ROW_META:ROW_META + N_META].reshape(N_META, N_SHARD, D_MODEL // N_SHARD)
    grads["meta_tokens"] = lax.dynamic_index_in_dim(gmeta, chip, axis=1, keepdims=False)
    gconv = gsum[ROW_CONV_W:ROW_CONV_W + 2].reshape(CONV_K, N_SHARD, LRU_TILE)
    grads["conv_w"] = lax.dynamic_index_in_dim(gconv, chip, axis=1, keepdims=False)[None]
    for nm in ("meta_tokens", "conv_w"):
        delta[nm], new_m[nm], new_v[nm] = _adamw_call("adamw_" + nm, p[nm], grads[nm], mom[nm], var[nm])

    done = [nm for at, names in CHIPS_AT if at is not None for nm in names]
    last = [nm for at, names in CHIPS_AT if at is None for nm in names]
    mine = sched.chip_sums(done)
    shard_grads = {}
    share = _share_pair_comm([mine[nm] for nm in done], lambda o: shard_grads.update(zip(done, o)))
    _comm_call("share_pair", _join_comms([share, sched.chips(last)]))

    def adamw(names):
        res = _adamw_group_call("adamw_" + names[0], [p2[nm] for nm in names], [shard_grads[nm] for nm in names],
                                [m2[nm] for nm in names], [v2[nm] for nm in names])
        for nm, (gg, dd, mm, vv) in zip(names, res):
            grads[nm], delta[nm], new_m[nm], new_v[nm] = (_from2d(nm, t) for t in (gg, dd, mm, vv))

    for grp in _same_shape_groups(done):
        adamw(grp)
    mine = sched.chip_sums(last)
    shard_grads = dict(zip(last, _comm_call("share_last", _share_pair_comm([mine[nm] for nm in last], lambda o: o))))
    adamw(last)

    return (loss, grad_x, *[grads[nm] for nm in WEIGHT_NAMES], *[delta[nm] for nm in WEIGHT_NAMES],
            *[new_m[nm] for nm in WEIGHT_NAMES], *[new_v[nm] for nm in WEIGHT_NAMES])
```

```python
import functools
import math

import jax
import jax.numpy as jnp
import numpy as np
from jax import lax
from jax.experimental import pallas as pl
from jax.experimental.pallas import tpu as pltpu

F32 = jnp.float32
BF16 = jnp.bfloat16
MESH = pl.DeviceIdType.MESH

D_MODEL = 1024
N_META = 16
CHUNK = 64
PAD_ROWS = CHUNK - N_META
HEADS = 4
D_NOPE = 128
D_ROPE = 64
D_QK = D_NOPE + D_ROPE
D_QKP = 256
D_V = 128
KV_RANK = 256
Q_RANK = 384
MLA_W = HEADS * D_V
LRU_W = 512
LRU_TILE = 128
N_LRU_TILES = LRU_W // LRU_TILE
CONV_K = 4
C_RGLRU = 8.0
ROPE_THETA = 10000.0
D_FF = 2816
N_SHARD = 4
EPS = 1e-6
NEG_INF = -1e30
Z_KR = Q_RANK + KV_RANK
Z_MLA = Z_KR + 128
Z_U = Z_MLA
Z_G = Z_U + LRU_W
Z_W = Z_G + LRU_W
IN_WIDTH = Q_RANK + KV_RANK + D_ROPE + 2 * LRU_W

ADAM_LR = 0.001
ADAM_B1 = 0.9
ADAM_B2 = 0.999
ADAM_EPS = 1e-08
ADAM_WD = 0.01
ADAM_STEP = 10

VMEM_LIMIT_BYTES = 56 * 1024 * 1024
SMALL_ROWS = 104
SMALL_ADAM_ROWS = 80


def _params(sem):
    return pltpu.CompilerParams(dimension_semantics=sem, vmem_limit_bytes=VMEM_LIMIT_BYTES)


def _resident(shape):
    return pl.BlockSpec(tuple(shape), lambda i: (0,) * len(shape), pipeline_mode=pl.Buffered(1))


def _row_tile(rows, target):
    best = 16
    for t in range(16, min(rows, target) + 1, 16):
        if rows % t == 0:
            best = t
    return best


def _col_tile(cols, target):
    best = cols
    for t in range(128, min(cols, target) + 1, 128):
        if cols % t == 0:
            best = t
    return best


def _dot(a, b):
    return jnp.dot(a, b, preferred_element_type=F32)


def _dot_nt(a, b):
    return lax.dot_general(a, b, (((1,), (1,)), ((), ())), preferred_element_type=F32)


def _dot_tn(a, b):
    return lax.dot_general(a, b, (((0,), (0,)), ((), ())), preferred_element_type=F32)


def _rms(x, n):
    return lax.rsqrt(jnp.sum(x * x, axis=-1, keepdims=True) * (1.0 / n) + EPS)


def _rms_bwd(dn, nrm, r, n):
    return r * (dn - nrm * (jnp.sum(dn * nrm, axis=-1, keepdims=True) * (1.0 / n)))


def _gelu(x):
    k = math.sqrt(2.0 / math.pi)
    t = jnp.tanh(k * (x + 0.044715 * x * x * x))
    return 0.5 * x * (1.0 + t), t


def _gelu_grad(x, t):
    k = math.sqrt(2.0 / math.pi)
    return 0.5 * (1.0 + t) + 0.5 * x * (1.0 - t * t) * k * (1.0 + 3.0 * 0.044715 * x * x)


def _sigmoid(x):
    return 0.5 + 0.5 * jnp.tanh(0.5 * x)


def _softplus_neg(lam):
    e = jnp.exp(-jnp.abs(lam))
    log1p = jnp.where(e < 0.01, e * (1.0 - e * (0.5 - e * (1.0 / 3 - e * 0.25))), jnp.log(1.0 + e))
    return jnp.maximum(-lam, 0.0) + log1p


def _rope(t, c, s1, s2):
    return t * c + pltpu.roll(t, 96, 1) * s1 + pltpu.roll(t, 32, 1) * s2


def _rope_t(d, c, s1, s2):
    return d * c + pltpu.roll(d * s1, 32, 1) + pltpu.roll(d * s2, 96, 1)


def _rope_tables(lp):
    pos = (np.arange(lp, dtype=np.int32) - PAD_ROWS).astype(np.float32)
    inv_freq = (ROPE_THETA ** (-np.arange(0, D_ROPE // 2, dtype=np.float32) / (D_ROPE // 2))).astype(np.float32)
    ang = (pos[:, None] * inv_freq[None, :]).astype(np.float32).astype(np.float64)
    cos, sin = np.cos(ang).astype(np.float32), np.sin(ang).astype(np.float32)
    z = np.zeros_like(cos)
    return (jnp.asarray(np.concatenate([cos, cos, z, z], 1)), jnp.asarray(np.concatenate([-sin, z, z, z], 1)),
            jnp.asarray(np.concatenate([z, sin, z, z], 1)))


def _rmsnorm_call(name, h, g, tm):
    rows, d = h.shape

    def body(h_ref, g_ref, o_ref):
        x = h_ref[...]
        o_ref[...] = (x * _rms(x, d) * g_ref[...]).astype(BF16)

    return pl.pallas_call(
        body, name=name, grid=(rows // tm,),
        in_specs=[pl.BlockSpec((tm, d), lambda i: (i, 0)), pl.BlockSpec((1, d), lambda i: (0, 0))],
        out_specs=pl.BlockSpec((tm, d), lambda i: (i, 0)),
        out_shape=jax.ShapeDtypeStruct((rows, d), BF16),
        compiler_params=_params(("parallel",)))(h, g)


def _ffn_up_call(name, u, wg, wu, tm, comm=None):
    rows, d = u.shape
    ns, fs, _ = wg.shape

    def body(u_ref, wg_ref, wu_ref, g_ref, p_ref, a_ref):
        uu = u_ref[...]
        g = _dot_nt(uu, wg_ref[0])
        p = _dot_nt(uu, wu_ref[0])
        g_ref[0] = g.astype(BF16)
        p_ref[0] = p.astype(BF16)
        a_ref[0] = (g * jax.nn.sigmoid(g) * p).astype(BF16)

    wspec = pl.BlockSpec((1, fs, d), lambda s, i: (s, 0, 0))
    ospec = pl.BlockSpec((1, tm, fs), lambda s, i: (s, i, 0))
    oshape = jax.ShapeDtypeStruct((ns, rows, fs), BF16)
    return _hosted_call(
        body, name=name, grid=(ns, rows // tm),
        in_specs=[pl.BlockSpec((tm, d), lambda s, i: (i, 0)), wspec, wspec],
        out_specs=[ospec, ospec, ospec], out_shape=[oshape, oshape, oshape],
        dims=("parallel", "parallel"), args=(u, wg, wu), comm=comm)


def _ffn_gate_call(name, u, wg, tm, comm=None):
    rows, d = u.shape
    ns, fs, _ = wg.shape

    def body(u_ref, wg_ref, g_ref):
        g_ref[0] = _dot_nt(u_ref[...], wg_ref[0]).astype(BF16)

    return _hosted_call(
        body, name=name, grid=(ns, rows // tm),
        in_specs=[pl.BlockSpec((tm, d), lambda s, i: (i, 0)), pl.BlockSpec((1, fs, d), lambda s, i: (s, 0, 0))],
        out_specs=[pl.BlockSpec((1, tm, fs), lambda s, i: (s, i, 0))],
        out_shape=[jax.ShapeDtypeStruct((ns, rows, fs), BF16)],
        dims=("parallel", "parallel"), args=(u, wg), comm=comm)[0]


def _ffn_upact_call(name, u, wu, gate, tm, comm=None):
    rows, d = u.shape
    ns, fs, _ = wu.shape

    def body(u_ref, wu_ref, g_ref, p_ref, a_ref):
        p = _dot_nt(u_ref[...], wu_ref[0])
        g = g_ref[0].astype(F32)
        p_ref[0] = p.astype(BF16)
        a_ref[0] = (g * jax.nn.sigmoid(g) * p).astype(BF16)

    ospec = pl.BlockSpec((1, tm, fs), lambda s, i: (s, i, 0))
    oshape = jax.ShapeDtypeStruct((ns, rows, fs), BF16)
    return _hosted_call(
        body, name=name, grid=(ns, rows // tm),
        in_specs=[pl.BlockSpec((tm, d), lambda s, i: (i, 0)), pl.BlockSpec((1, fs, d), lambda s, i: (s, 0, 0)), ospec],
        out_specs=[ospec, ospec], out_shape=[oshape, oshape],
        dims=("parallel", "parallel"), args=(u, wu, gate), comm=comm)


def _ffn_down_call(name, a, wd, h, tm, comm=None):
    rows, d = h.shape
    ns, _, fs = a.shape

    def body(a_ref, wd_ref, h_ref, o_ref):
        acc = h_ref[...]
        for s in range(ns):
            acc = acc + 0.5 * _dot(a_ref[s], wd_ref[s])
        o_ref[...] = acc

    return _hosted_call(
        body, name=name, grid=(rows // tm,),
        in_specs=[pl.BlockSpec((ns, tm, fs), lambda i: (0, i, 0)),
                  _resident((ns, fs, d)),
                  pl.BlockSpec((tm, d), lambda i: (i, 0))],
        out_specs=[pl.BlockSpec((tm, d), lambda i: (i, 0))],
        out_shape=[jax.ShapeDtypeStruct((rows, d), F32)],
        dims=("parallel",), args=(a, wd, h), comm=comm)[0]


def _mm_call(name, a, bt, tm, out_dtype):
    rows, k = a.shape
    n = bt.shape[0]

    def body(a_ref, b_ref, o_ref):
        o_ref[...] = _dot_nt(a_ref[...], b_ref[...]).astype(out_dtype)

    return pl.pallas_call(
        body, name=name, grid=(rows // tm,),
        in_specs=[pl.BlockSpec((tm, k), lambda i: (i, 0)), pl.BlockSpec((n, k), lambda i: (0, 0))],
        out_specs=pl.BlockSpec((tm, n), lambda i: (i, 0)),
        out_shape=jax.ShapeDtypeStruct((rows, n), out_dtype),
        compiler_params=_params(("parallel",)))(a, bt)


def _mla_heads(z, gql, gkvl, wuq, wuk, wuv):
    cq = z[:, 0:Q_RANK]
    ckv = z[:, Q_RANK:Z_KR]
    kr = z[:, Z_KR:Z_MLA]
    rq = _rms(cq, Q_RANK)
    nq = cq * rq
    cqn = (nq * gql).astype(BF16)
    rkv = _rms(ckv, KV_RANK)
    nkv = ckv * rkv
    ckvn = (nkv * gkvl).astype(BF16)
    qraw = _dot_nt(cqn, wuq)
    knope = _dot(ckvn, wuk)
    v = _dot(ckvn, wuv)
    skr = jnp.sum(kr * kr, axis=-1, keepdims=True)
    heads = []
    for hd in range(HEADS):
        qh = qraw[:, hd * D_QKP:(hd + 1) * D_QKP]
        rqh = lax.rsqrt(jnp.sum(qh * qh, axis=-1, keepdims=True) * (1.0 / D_QK) + EPS)
        kn = knope[:, hd * D_NOPE:(hd + 1) * D_NOPE]
        rkh = lax.rsqrt((jnp.sum(kn * kn, axis=-1, keepdims=True) + skr) * (1.0 / D_QK) + EPS)
        heads.append((qh * rqh, rqh, kn * rkh, kr * rkh, rkh))
    return dict(rq=rq, nq=nq, cqn=cqn, rkv=rkv, nkv=nkv, ckvn=ckvn, v=v, heads=heads)


def _mla_prep_call(z, gql, gkvl, gqh, gkh, wuq, wuk, wuv, tabs, lp, tm):
    rows = z.shape[0]
    tpe = lp // tm

    def body(z_ref, gql_ref, gkvl_ref, gqh_ref, gkh_ref, wuq_ref, wuk_ref, wuv_ref, c_ref, s1_ref, s2_ref,
             q_ref, k_ref, v_ref, cqn_ref, ckvn_ref):
        m = _mla_heads(z_ref[...], gql_ref[...], gkvl_ref[...], wuq_ref[...], wuk_ref[...], wuv_ref[...])
        c, s1, s2 = c_ref[...], s1_ref[...], s2_ref[...]
        gq, gk = gqh_ref[...], gkh_ref[...]
        row = (pl.program_id(0) % tpe) * tm + lax.broadcasted_iota(jnp.int32, (tm, 1), 0)
        spare = (lax.broadcasted_iota(jnp.int32, (1, D_QKP - D_NOPE), 1) == D_ROPE).astype(F32)
        kmask = jnp.where(row < PAD_ROWS, NEG_INF * math.sqrt(D_QK), 0.0) * spare
        for hd in range(HEADS):
            qn, _, knn, krn, _ = m["heads"][hd]
            qg = qn * gq
            q_ref[hd, :, 0:D_NOPE] = qg[:, 0:D_NOPE].astype(BF16)
            q_ref[hd, :, D_NOPE:D_QKP] = (_rope(qg[:, D_NOPE:D_QKP], c, s1, s2) + spare).astype(BF16)
            k_ref[hd, :, 0:D_NOPE] = (knn * gk[:, 0:D_NOPE]).astype(BF16)
            k_ref[hd, :, D_NOPE:D_QKP] = (_rope(krn * gk[:, D_NOPE:D_QKP], c, s1, s2) + kmask).astype(BF16)
            v_ref[hd] = m["v"][:, hd * D_V:(hd + 1) * D_V].astype(BF16)
        cqn_ref[...] = m["cqn"]
        ckvn_ref[...] = m["ckvn"]

    def const(shape):
        return pl.BlockSpec(shape, lambda i: tuple(0 for _ in shape))

    tab = pl.BlockSpec((tm, 128), lambda i: (i % tpe, 0))
    return pl.pallas_call(
        body, name="mla_prep", grid=(rows // tm,),
        in_specs=[pl.BlockSpec((tm, Z_MLA), lambda i: (i, 0)), const((1, Q_RANK)), const((1, KV_RANK)),
                  const((1, D_QKP)), const((1, D_QKP)), const((HEADS * D_QKP, Q_RANK)),
                  const((KV_RANK, HEADS * D_NOPE)), const((KV_RANK, HEADS * D_V)), tab, tab, tab],
        out_specs=[pl.BlockSpec((HEADS, tm, D_QKP), lambda i: (0, i, 0)),
                   pl.BlockSpec((HEADS, tm, D_QKP), lambda i: (0, i, 0)),
                   pl.BlockSpec((HEADS, tm, D_V), lambda i: (0, i, 0)),
                   pl.BlockSpec((tm, Q_RANK), lambda i: (i, 0)),
                   pl.BlockSpec((tm, KV_RANK), lambda i: (i, 0))],
        out_shape=[jax.ShapeDtypeStruct((HEADS, rows, D_QKP), BF16),
                   jax.ShapeDtypeStruct((HEADS, rows, D_QKP), BF16),
                   jax.ShapeDtypeStruct((HEADS, rows, D_V), BF16),
                   jax.ShapeDtypeStruct((rows, Q_RANK), BF16),
                   jax.ShapeDtypeStruct((rows, KV_RANK), BF16)],
        compiler_params=_params(("parallel",)))(z, gql, gkvl, gqh, gkh, wuq, wuk, wuv, *tabs)


Q_BLOCK_ROWS = 528


def _q_block(lp):
    return _row_tile(lp, Q_BLOCK_ROWS)


def _key_end(ext, lp):
    return min(lp, -(-ext // CHUNK) * CHUNK)


def _diag_bias(qb, j0, nk):
    shift = CHUNK.bit_length() - 1
    r = jnp.right_shift(j0 + lax.broadcasted_iota(jnp.int32, (qb, nk), 0), shift)
    c = jnp.right_shift(j0 + lax.broadcasted_iota(jnp.int32, (qb, nk), 1), shift)
    return jnp.where(c <= r, 0.0, NEG_INF)


def _attn_fwd_call(q, k, v, nb, lp, comm=None):
    rows = nb * lp
    qb = _q_block(lp)
    scale = 1.0 / math.sqrt(D_QK)

    def body(q_ref, k_ref, v_ref, o_ref, lse_ref):
        for j in range(lp // qb):
            j0, ext = j * qb, (j + 1) * qb
            kend = _key_end(ext, lp)
            qj = q_ref[0, j0:ext, :]
            sd = _dot_nt(qj, k_ref[0, j0:kend, :]) * scale + _diag_bias(qb, j0, kend - j0)
            mx = jnp.max(sd, axis=-1, keepdims=True)
            if j > 0:
                so = _dot_nt(qj, k_ref[0, 0:j0, :]) * scale
                mx = jnp.maximum(mx, jnp.max(so, axis=-1, keepdims=True))
            pd = jnp.exp(sd - mx)
            l = jnp.sum(pd, axis=-1, keepdims=True)
            o = _dot(pd.astype(BF16), v_ref[0, j0:kend, :])
            if j > 0:
                po = jnp.exp(so - mx)
                l = l + jnp.sum(po, axis=-1, keepdims=True)
                o = o + _dot(po.astype(BF16), v_ref[0, 0:j0, :])
            o_ref[j0:ext, :] = o / l
            lse_ref[0, j0:ext, :] = mx + jnp.log(l)

    return _hosted_call(
        body, name="attn_fwd", grid=(nb, HEADS),
        in_specs=[pl.BlockSpec((1, lp, D_QKP), lambda b, h: (h, b, 0)),
                  pl.BlockSpec((1, lp, D_QKP), lambda b, h: (h, b, 0)),
                  pl.BlockSpec((1, lp, D_V), lambda b, h: (h, b, 0))],
        out_specs=[pl.BlockSpec((lp, D_V), lambda b, h: (b, h)),
                   pl.BlockSpec((1, lp, 1), lambda b, h: (h, b, 0))],
        out_shape=[jax.ShapeDtypeStruct((rows, MLA_W), F32),
                   jax.ShapeDtypeStruct((HEADS, rows, 1), F32)],
        dims=("parallel", "parallel"), args=(q, k, v), comm=comm)


def _attn_bwd_call(q, k, v, o, lse, do, nb, lp, comm=None):
    rows = nb * lp
    qb = _q_block(lp)
    scale = 1.0 / math.sqrt(D_QK)

    def body(q_ref, k_ref, v_ref, o_ref, lse_ref, do_ref, dq_ref, dk_ref, dv_ref, dk_acc, dv_acc):
        dk_acc[...] = jnp.zeros_like(dk_acc)
        dv_acc[...] = jnp.zeros_like(dv_acc)
        for j in range(lp // qb):
            j0, ext = j * qb, (j + 1) * qb
            kend = _key_end(ext, lp)
            dbias = _diag_bias(qb, j0, kend - j0)
            qj = q_ref[0, j0:ext, :]
            doj = do_ref[j0:ext, :]
            delta = jnp.sum(doj * o_ref[j0:ext, :], axis=-1, keepdims=True)
            dob = doj.astype(BF16)
            lse = lse_ref[0, j0:ext, :]
            dq = jnp.zeros((qb, D_QKP), F32)
            for lo, hi, bias in ((j0, kend, dbias), (0, j0, None)):
                if hi == lo:
                    continue
                kk = k_ref[0, lo:hi, :]
                s = _dot_nt(qj, kk) * scale
                p = jnp.exp((s if bias is None else s + bias) - lse)
                dv_acc[lo:hi, :] += _dot_tn(p.astype(BF16), dob)
                dp = _dot_nt(dob, v_ref[0, lo:hi, :])
                ds = (p * (dp - delta) * scale).astype(BF16)
                dq = dq + _dot(ds, kk)
                dk_acc[lo:hi, :] += _dot_tn(ds, qj)
            dq_ref[0, j0:ext, :] = dq.astype(BF16)
        dk_ref[0] = dk_acc[...].astype(BF16)
        dv_ref[0] = dv_acc[...].astype(BF16)

    qspec = pl.BlockSpec((1, lp, D_QKP), lambda b, h: (h, b, 0))
    vspec = pl.BlockSpec((1, lp, D_V), lambda b, h: (h, b, 0))
    ospec = pl.BlockSpec((lp, D_V), lambda b, h: (b, h))
    return _hosted_call(
        body, name="attn_bwd", grid=(nb, HEADS),
        in_specs=[qspec, qspec, vspec, ospec, pl.BlockSpec((1, lp, 1), lambda b, h: (h, b, 0)), ospec],
        out_specs=[qspec, qspec, vspec],
        out_shape=[jax.ShapeDtypeStruct((HEADS, rows, D_QKP), BF16),
                   jax.ShapeDtypeStruct((HEADS, rows, D_QKP), BF16),
                   jax.ShapeDtypeStruct((HEADS, rows, D_V), BF16)],
        scratch_shapes=[pltpu.VMEM((lp, D_QKP), F32), pltpu.VMEM((lp, D_V), F32)],
        dims=("parallel", "parallel"), args=(q, k, v, o, lse, do), comm=comm)


def _lru_gates(u, cw, cb, wa, ba, wx, bx, lam, lp):
    xc = (cw[3:4, :] * u + cw[2:3, :] * pltpu.roll(u, 1, 0) + cw[1:2, :] * pltpu.roll(u, 2, 0)
          + cw[0:1, :] * pltpu.roll(u, 3, 0) + cb)
    xcb = xc.astype(BF16)
    r = _sigmoid(_dot(xcb, wa) + ba)
    i = _sigmoid(_dot(xcb, wx) + bx)
    sp = _softplus_neg(lam)
    la = -C_RGLRU * r * sp
    a = jnp.exp(la)
    x2 = 2.0 * la
    e2 = a * a
    m2 = jnp.maximum(jnp.where(x2 > -0.01, -x2 * (1.0 + 0.5 * x2), 1.0 - e2), 1e-30)
    rs = lax.rsqrt(m2)
    row = lax.broadcasted_iota(jnp.int32, (lp, LRU_TILE), 0)
    first = row == PAD_ROWS
    valid = row >= PAD_ROWS
    mult_eff = jnp.where(first, 1.0, m2 * rs)
    return dict(xc=xc, xcb=xcb, r=r, i=i, sp=sp, a=a, e2=e2, rs=rs, mult_eff=mult_eff, first=first, valid=valid)


def _scan_rows(a, b, a_s, b_s, out_ref, lp, reverse):
    sub = lax.broadcasted_iota(jnp.int32, (lp, LRU_TILE), 0) & 7
    for dist in (1, 2, 4):
        shift = lp - dist if reverse else dist
        keep = (sub + dist <= 7) if reverse else (sub >= dist)
        a_sh = pltpu.roll(a, shift, 0)
        b_sh = pltpu.roll(b, shift, 0)
        b = jnp.where(keep, a * b_sh + b, b)
        a = jnp.where(keep, a * a_sh, a)
    a_s[...] = a
    b_s[...] = b
    n_groups = lp // 8
    edge = 0 if reverse else 7

    def group(gi, carry):
        r0 = pl.multiple_of(((n_groups - 1 - gi) if reverse else gi) * 8, 8)
        a8 = a_s[pl.ds(r0, 8), :]
        b8 = b_s[pl.ds(r0, 8), :]
        out_ref[pl.ds(r0, 8), :] = a8 * carry + b8
        return a8[edge:edge + 1, :] * carry + b8[edge:edge + 1, :]

    lax.fori_loop(0, n_groups, group, jnp.zeros((1, LRU_TILE), F32), unroll=4)


def _lru_specs(lp):
    seq = lambda col0: pl.BlockSpec((lp, LRU_TILE), lambda t, b: (b, col0 + t))
    cw = pl.BlockSpec((1, CONV_K, LRU_TILE), lambda t, b: (t, 0, 0))
    vec = pl.BlockSpec((1, LRU_TILE), lambda t, b: (0, t))
    mat = pl.BlockSpec((1, LRU_TILE, LRU_TILE), lambda t, b: (t, 0, 0))
    return seq, cw, vec, mat


def _lru_fwd_call(z, cw, cb, wa, ba, wx, bx, lam, nb, lp, comm=None):
    rows = nb * lp
    seq, cwspec, vec, mat = _lru_specs(lp)

    def body(u_ref, g_ref, cw_ref, cb_ref, wa_ref, ba_ref, wx_ref, bx_ref, lam_ref, y_ref, hs_ref, a_s, b_s):
        m = _lru_gates(u_ref[...], cw_ref[0], cb_ref[...], wa_ref[0], ba_ref[...], wx_ref[0], bx_ref[...],
                       lam_ref[...], lp)
        a = jnp.where(m["valid"], m["a"], 0.0)
        b = jnp.where(m["valid"], m["mult_eff"] * (m["i"] * m["xc"]), 0.0)
        _scan_rows(a, b, a_s, b_s, hs_ref, lp, reverse=False)
        gl, _ = _gelu(g_ref[...])
        y_ref[...] = hs_ref[...] * gl

    oshape = jax.ShapeDtypeStruct((rows, LRU_W), F32)
    return _hosted_call(
        body, name="lru_fwd", grid=(N_LRU_TILES, nb),
        in_specs=[seq(Z_U // LRU_TILE), seq(Z_G // LRU_TILE), cwspec, vec, mat, vec, mat, vec, vec],
        out_specs=[seq(0), seq(0)], out_shape=[oshape, oshape],
        scratch_shapes=[pltpu.VMEM((lp, LRU_TILE), F32), pltpu.VMEM((lp, LRU_TILE), F32)],
        dims=("parallel", "parallel"), args=(z, z, cw, cb, wa, ba, wx, bx, lam), comm=comm)


def _lru_bwd_call(z, hs, dy, cw, cb, wa, ba, wx, bx, lam, nb, lp, comm=None):
    rows = nb * lp
    seq, cwspec, vec, mat = _lru_specs(lp)

    def body(u_ref, g_ref, hs_ref, dy_ref, cw_ref, cb_ref, wa_ref, ba_ref, wx_ref, bx_ref, lam_ref,
             du_ref, dg_ref, dcw_ref, dcb_ref, dwa_ref, dba_ref, dwx_ref, dbx_ref, dlam_ref, a_s, b_s, d_s):
        b_idx = pl.program_id(1)
        u = u_ref[...]
        cw = cw_ref[0]
        wa, wx = wa_ref[0], wx_ref[0]
        lam = lam_ref[...]
        m = _lru_gates(u, cw, cb_ref[...], wa, ba_ref[...], wx, bx_ref[...], lam, lp)
        gate = g_ref[...]
        gl, th = _gelu(gate)
        dy = dy_ref[...]
        hs = hs_ref[...]
        dg_ref[...] = (dy * hs * _gelu_grad(gate, th)).astype(BF16)
        a_eff = jnp.where(m["valid"], m["a"], 0.0)
        _scan_rows(pltpu.roll(a_eff, lp - 1, 0), dy * gl, a_s, b_s, d_s, lp, reverse=True)
        ds = d_s[...]
        xc, r, i = m["xc"], m["r"], m["i"]
        row = lax.broadcasted_iota(jnp.int32, (lp, LRU_TILE), 0)
        da = ds * jnp.where(row >= 1, pltpu.roll(hs, 1, 0), 0.0)
        db = jnp.where(m["valid"], ds, 0.0)
        di = db * m["mult_eff"] * xc
        dxc = db * m["mult_eff"] * i
        live = m["valid"] & jnp.logical_not(m["first"])
        dm = jnp.where(live, db * i * xc, 0.0)
        dla = da * m["a"] - dm * (m["e2"] * m["rs"])
        dr = dla * (-C_RGLRU * m["sp"])
        dsp = jnp.sum(dla * (-C_RGLRU * r), axis=0, keepdims=True)
        dpr = (dr * r * (1.0 - r))
        dpi = (di * i * (1.0 - i))
        dprb, dpib = dpr.astype(BF16), dpi.astype(BF16)
        dxc = dxc + _dot_nt(dprb, wa) + _dot_nt(dpib, wx)
        du = (cw[3:4, :] * dxc + cw[2:3, :] * pltpu.roll(dxc, lp - 1, 0) + cw[1:2, :] * pltpu.roll(dxc, lp - 2, 0)
              + cw[0:1, :] * pltpu.roll(dxc, lp - 3, 0))
        du_ref[...] = jnp.where(m["valid"], du, 0.0).astype(BF16)
        tap = lax.broadcasted_iota(jnp.int32, (CONV_K, LRU_TILE), 0)
        dcw = jnp.zeros((CONV_K, LRU_TILE), F32)
        for kk in range(CONV_K):
            shifted = u if kk == CONV_K - 1 else pltpu.roll(u, CONV_K - 1 - kk, 0)
            dcw = jnp.where(tap == kk, jnp.sum(dxc * shifted, axis=0, keepdims=True), dcw)
        parts = [(dcw_ref, dcw[None]), (dcb_ref, jnp.sum(dxc, axis=0, keepdims=True)[None]),
                 (dwa_ref, _dot_tn(m["xcb"], dprb)[None]), (dba_ref, jnp.sum(dpr, axis=0, keepdims=True)[None]),
                 (dwx_ref, _dot_tn(m["xcb"], dpib)[None]), (dbx_ref, jnp.sum(dpi, axis=0, keepdims=True)[None]),
                 (dlam_ref, (dsp * (-jax.nn.sigmoid(-lam)))[None])]

        @pl.when(b_idx == 0)
        def _():
            for ref, val in parts:
                ref[...] = val

        @pl.when(b_idx != 0)
        def _():
            for ref, val in parts:
                ref[...] += val

    bshape = jax.ShapeDtypeStruct((rows, LRU_W), BF16)
    vec3 = pl.BlockSpec((1, 1, LRU_TILE), lambda t, b: (t, 0, 0))
    vshape = jax.ShapeDtypeStruct((N_LRU_TILES, 1, LRU_TILE), F32)
    mshape = jax.ShapeDtypeStruct((N_LRU_TILES, LRU_TILE, LRU_TILE), F32)
    return _hosted_call(
        body, name="lru_bwd", grid=(N_LRU_TILES, nb),
        in_specs=[seq(Z_U // LRU_TILE), seq(Z_G // LRU_TILE), seq(0), seq(0), cwspec, vec, mat, vec, mat, vec, vec],
        out_specs=[seq(0), seq(0), cwspec, vec3, mat, vec3, mat, vec3, vec3],
        out_shape=[bshape, bshape, jax.ShapeDtypeStruct((N_LRU_TILES, CONV_K, LRU_TILE), F32), vshape, mshape,
                   vshape, mshape, vshape, vshape],
        scratch_shapes=[pltpu.VMEM((lp, LRU_TILE), F32)] * 3,
        dims=("parallel", "arbitrary"), args=(z, z, hs, dy, cw, cb, wa, ba, wx, bx, lam), comm=comm)


def _mix_out_call(ya, yl, ga, gl, wout, h, tm):
    rows, d = h.shape

    def body(ya_ref, yl_ref, ga_ref, gl_ref, w_ref, h_ref, y_ref, o_ref):
        a = ya_ref[...]
        l = yl_ref[...]
        an = (a * _rms(a, MLA_W) * ga_ref[...]).astype(BF16)
        ln = (l * _rms(l, LRU_W) * gl_ref[...]).astype(BF16)
        y_ref[:, 0:MLA_W] = an
        y_ref[:, MLA_W:MLA_W + LRU_W] = ln
        o_ref[...] = h_ref[...] + _dot(an, w_ref[0:MLA_W, :]) + _dot(ln, w_ref[MLA_W:MLA_W + LRU_W, :])

    half = pl.BlockSpec((tm, MLA_W), lambda i: (i, 0))
    g = pl.BlockSpec((1, MLA_W), lambda i: (0, 0))
    full = pl.BlockSpec((tm, d), lambda i: (i, 0))
    return pl.pallas_call(
        body, name="mix_out", grid=(rows // tm,),
        in_specs=[half, half, g, g, pl.BlockSpec((MLA_W + LRU_W, d), lambda i: (0, 0)), full],
        out_specs=[full, full],
        out_shape=[jax.ShapeDtypeStruct((rows, MLA_W + LRU_W), BF16), jax.ShapeDtypeStruct((rows, d), F32)],
        compiler_params=_params(("parallel",)))(ya, yl, ga, gl, wout, h)


def _mix_out_bwd_call(dhb, wout, ya, yl, ga, gl, tm):
    rows = ya.shape[0]
    d = dhb.shape[1]

    def body(dh_ref, w_ref, ya_ref, yl_ref, ga_ref, gl_ref, dya_ref, dyl_ref, dga_ref, dgl_ref):
        dy = _dot_nt(dh_ref[...], w_ref[...])
        outs = []
        for val, g_ref, lo, out_ref in ((ya_ref[...], ga_ref, 0, dya_ref), (yl_ref[...], gl_ref, MLA_W, dyl_ref)):
            r = _rms(val, MLA_W)
            n = val * r
            dyn = dy[:, lo:lo + MLA_W]
            out_ref[...] = _rms_bwd(dyn * g_ref[...], n, r, MLA_W)
            outs.append(jnp.sum(dyn * n, axis=0, keepdims=True))

        @pl.when(pl.program_id(0) == 0)
        def _():
            dga_ref[...] = outs[0]
            dgl_ref[...] = outs[1]

        @pl.when(pl.program_id(0) != 0)
        def _():
            dga_ref[...] += outs[0]
            dgl_ref[...] += outs[1]

    half = pl.BlockSpec((tm, MLA_W), lambda i: (i, 0))
    g = pl.BlockSpec((1, MLA_W), lambda i: (0, 0))
    return pl.pallas_call(
        body, name="mix_out_bwd", grid=(rows // tm,),
        in_specs=[pl.BlockSpec((tm, d), lambda i: (i, 0)), pl.BlockSpec((MLA_W + LRU_W, d), lambda i: (0, 0)),
                  half, half, g, g],
        out_specs=[half, half, g, g],
        out_shape=[jax.ShapeDtypeStruct((rows, MLA_W), F32), jax.ShapeDtypeStruct((rows, LRU_W), F32),
                   jax.ShapeDtypeStruct((1, MLA_W), F32), jax.ShapeDtypeStruct((1, LRU_W), F32)],
        compiler_params=_params(("arbitrary",)))(dhb, wout, ya, yl, ga, gl)


def _final_call(h, g, target, lp, tm):
    rows, d = h.shape
    tpe = lp // tm

    def body(h_ref, g_ref, t_ref, dh_ref, dhb_ref, dg_ref, loss_ref):
        i = pl.program_id(0)
        x = h_ref[...]
        g = g_ref[...]
        r = _rms(x, d)
        n = x * r
        row = (i % tpe) * tm + lax.broadcasted_iota(jnp.int32, (tm, 1), 0)
        err = jnp.where(row >= CHUNK, n * g - t_ref[...], 0.0)
        dout = err * (1.0 / d)
        dh = _rms_bwd(dout * g, n, r, d)
        dh_ref[...] = dh
        dhb_ref[...] = dh.astype(BF16)
        dg = jnp.sum(dout * n, axis=0, keepdims=True)
        part = jnp.sum(jnp.sum(err * err, axis=1, keepdims=True), axis=0, keepdims=True) * (0.5 / d)
        loss = jnp.broadcast_to(part, (1, 128))

        @pl.when(i == 0)
        def _():
            dg_ref[...] = dg
            loss_ref[...] = loss

        @pl.when(i != 0)
        def _():
            dg_ref[...] += dg
            loss_ref[...] += loss

    full = pl.BlockSpec((tm, d), lambda i: (i, 0))
    return pl.pallas_call(
        body, name="final_loss", grid=(rows // tm,),
        in_specs=[full, pl.BlockSpec((1, d), lambda i: (0, 0)), full],
        out_specs=[full, full, pl.BlockSpec((1, d), lambda i: (0, 0)), pl.BlockSpec((1, 128), lambda i: (0, 0))],
        out_shape=[jax.ShapeDtypeStruct((rows, d), F32), jax.ShapeDtypeStruct((rows, d), BF16),
                   jax.ShapeDtypeStruct((1, d), F32), jax.ShapeDtypeStruct((1, 128), F32)],
        compiler_params=_params(("arbitrary",)))(h, g, target)


def _ffn_dact_call(name, dhb, wd, gate, up, tm, comm=None):
    rows, d = dhb.shape
    ns, fs, _ = wd.shape

    nsub = 2 if tm % 32 == 0 else 1
    sub = tm // nsub

    def body(dh_ref, wd_ref, g_ref, p_ref, dg_ref, dp_ref):
        wd = wd_ref[0]
        for r in range(nsub):
            rs = slice(r * sub, (r + 1) * sub)
            da = (0.5 * _dot_nt(dh_ref[rs, :], wd)).astype(BF16)
            g = g_ref[0, rs, :]
            p = p_ref[0, rs, :]
            sg = jax.nn.sigmoid(g)
            dg_ref[0, rs, :] = (da * p) * (sg * (1.0 + g * (1.0 - sg)))
            dp_ref[0, rs, :] = da * (g * sg)

    aspec = pl.BlockSpec((1, tm, fs), lambda s, i: (s, i, 0))
    oshape = jax.ShapeDtypeStruct((ns, rows, fs), BF16)
    return _hosted_call(
        body, name=name, grid=(ns, rows // tm),
        in_specs=[pl.BlockSpec((tm, d), lambda s, i: (i, 0)), pl.BlockSpec((1, fs, d), lambda s, i: (s, 0, 0)),
                  aspec, aspec],
        out_specs=[aspec, aspec], out_shape=[oshape, oshape],
        dims=("parallel", "parallel"), args=(dhb, wd, gate, up), comm=comm)


def _norm_in_bwd_call(name, pieces, h, g, dres, tm, comm=None):
    rows, d = h.shape
    npc = len(pieces)

    def body(*refs):
        d_refs = refs[0:2 * npc:2]
        w_refs = refs[1:2 * npc:2]
        h_ref, g_ref, dres_ref, dh_ref, dhb_ref, dg_ref = refs[2 * npc:]
        du = jnp.zeros((tm, d), F32)
        for d_ref, w_ref in zip(d_refs, w_refs):
            if len(d_ref.shape) == 3:
                for s in range(d_ref.shape[0]):
                    du = du + _dot(d_ref[s], w_ref[s])
            else:
                du = du + _dot(d_ref[...], w_ref[...])
        x = h_ref[...]
        r = _rms(x, d)
        n = x * r
        dh = dres_ref[...] + _rms_bwd(du * g_ref[...], n, r, d)
        dh_ref[...] = dh
        dhb_ref[...] = dh.astype(BF16)
        dg = jnp.sum(du * n, axis=0, keepdims=True)

        @pl.when(pl.program_id(0) == 0)
        def _():
            dg_ref[...] = dg

        @pl.when(pl.program_id(0) != 0)
        def _():
            dg_ref[...] += dg

    in_specs, args = [], []
    for dd, w in pieces:
        if dd.ndim == 3:
            in_specs.append(pl.BlockSpec((dd.shape[0], tm, dd.shape[2]), lambda i: (0, i, 0)))
            in_specs.append(_resident(w.shape))
        else:
            in_specs.append(pl.BlockSpec((tm, dd.shape[1]), lambda i: (i, 0)))
            in_specs.append(_resident(w.shape))
        args += [dd, w]
    full = pl.BlockSpec((tm, d), lambda i: (i, 0))
    gspec = pl.BlockSpec((1, d), lambda i: (0, 0))
    return _hosted_call(
        body, name=name, grid=(rows // tm,),
        in_specs=in_specs + [full, gspec, full],
        out_specs=[full, full, gspec],
        out_shape=[jax.ShapeDtypeStruct((rows, d), F32), jax.ShapeDtypeStruct((rows, d), BF16),
                   jax.ShapeDtypeStruct((1, d), F32)],
        args=(*args, h, g, dres), comm=comm)


def _wgrad_call(name, a, b, scale=1.0, comm=None):
    a3, b3 = a.ndim == 3, b.ndim == 3
    ns = a.shape[0] if a3 else (b.shape[0] if b3 else 1)
    rows, m = a.shape[-2:]
    n = b.shape[-1]
    tmm = m if a3 else _col_tile(m, 256)

    def body(a_ref, b_ref, o_ref):
        av = a_ref[0] if a3 else a_ref[...]
        bv = b_ref[0] if b3 else b_ref[...]
        res = _dot_tn(av, bv)
        if scale != 1.0:
            res = res * scale
        if a3 or b3:
            o_ref[0] = res
        else:
            o_ref[...] = res

    aspec = (pl.BlockSpec((1, rows, tmm), lambda s, j: (s, 0, j)) if a3
             else pl.BlockSpec((rows, tmm), lambda s, j: (0, j)))
    bspec = (pl.BlockSpec((1, rows, n), lambda s, j: (s, 0, 0)) if b3
             else pl.BlockSpec((rows, n), lambda s, j: (0, 0)))
    if a3 or b3:
        ospec = pl.BlockSpec((1, tmm, n), lambda s, j: (s, j, 0))
        oshape = jax.ShapeDtypeStruct((ns, m, n), F32)
    else:
        ospec = pl.BlockSpec((tmm, n), lambda s, j: (j, 0))
        oshape = jax.ShapeDtypeStruct((m, n), F32)
    return _hosted_call(
        body, name=name, grid=(ns, m // tmm), in_specs=[aspec, bspec], out_specs=[ospec], out_shape=[oshape],
        dims=("parallel", "parallel"), args=(a, b), comm=comm)[0]


def _mla_prep_bwd_call(z, dq, dk, dv, gql, gkvl, gqh, gkh, wuq, wuk, wuv, tabs, lp, tm, comm=None):
    rows = z.shape[0]
    tpe = lp // tm

    def body(z_ref, dq_ref, dk_ref, dv_ref, gql_ref, gkvl_ref, gqh_ref, gkh_ref, wuq_ref, wuk_ref, wuv_ref,
             c_ref, s1_ref, s2_ref, dz_ref, dqp_ref, dkn_ref, dvv_ref, dgql_ref, dgkvl_ref, dgqh_ref, dgkh_ref):
        gql, gkvl = gql_ref[...], gkvl_ref[...]
        gq, gk = gqh_ref[...], gkh_ref[...]
        wuq, wuk, wuv = wuq_ref[...], wuk_ref[...], wuv_ref[...]
        m = _mla_heads(z_ref[...], gql, gkvl, wuq, wuk, wuv)
        c, s1, s2 = c_ref[...], s1_ref[...], s2_ref[...]
        dgq = jnp.zeros((1, D_QKP), F32)
        dgk = jnp.zeros((1, D_QKP), F32)
        dkr = jnp.zeros((tm, D_QKP - D_NOPE), F32)
        for hd in range(HEADS):
            qn, rqh, knn, krn, rkh = m["heads"][hd]
            dqg = jnp.concatenate([dq_ref[hd, :, 0:D_NOPE].astype(F32),
                                   _rope_t(dq_ref[hd, :, D_NOPE:D_QKP].astype(F32), c, s1, s2)], axis=1)
            dgq = dgq + jnp.sum(dqg * qn, axis=0, keepdims=True)
            dqn = dqg * gq
            dqr = rqh * (dqn - qn * (jnp.sum(dqn * qn, axis=-1, keepdims=True) * (1.0 / D_QK)))
            dqp_ref[:, hd * D_QKP:(hd + 1) * D_QKP] = dqr.astype(BF16)
            kn_full = jnp.concatenate([knn, krn], axis=1)
            dkg = jnp.concatenate([dk_ref[hd, :, 0:D_NOPE].astype(F32),
                                   _rope_t(dk_ref[hd, :, D_NOPE:D_QKP].astype(F32), c, s1, s2)], axis=1)
            dgk = dgk + jnp.sum(dkg * kn_full, axis=0, keepdims=True)
            dkn = dkg * gk
            dkraw = rkh * (dkn - kn_full * (jnp.sum(dkn * kn_full, axis=-1, keepdims=True) * (1.0 / D_QK)))
            dkn_ref[:, hd * D_NOPE:(hd + 1) * D_NOPE] = dkraw[:, 0:D_NOPE].astype(BF16)
            dkr = dkr + dkraw[:, D_NOPE:D_QKP]
            dvv_ref[:, hd * D_V:(hd + 1) * D_V] = dv_ref[hd]
        dcqn = _dot(dqp_ref[...], wuq)
        dckvn = _dot_nt(dkn_ref[...], wuk) + _dot_nt(dvv_ref[...], wuv)
        dz_ref[:, 0:Q_RANK] = _rms_bwd(dcqn * gql, m["nq"], m["rq"], Q_RANK).astype(BF16)
        dz_ref[:, Q_RANK:Z_KR] = _rms_bwd(dckvn * gkvl, m["nkv"], m["rkv"], KV_RANK).astype(BF16)
        dz_ref[:, Z_KR:Z_MLA] = dkr.astype(BF16)
        parts = [(dgql_ref, jnp.sum(dcqn * m["nq"], axis=0, keepdims=True)),
                 (dgkvl_ref, jnp.sum(dckvn * m["nkv"], axis=0, keepdims=True)), (dgqh_ref, dgq), (dgkh_ref, dgk)]

        @pl.when(pl.program_id(0) == 0)
        def _():
            for ref, val in parts:
                ref[...] = val

        @pl.when(pl.program_id(0) != 0)
        def _():
            for ref, val in parts:
                ref[...] += val

    def const(shape):
        return pl.BlockSpec(shape, lambda i: tuple(0 for _ in shape))

    tab = pl.BlockSpec((tm, 128), lambda i: (i % tpe, 0))
    hq = pl.BlockSpec((HEADS, tm, D_QKP), lambda i: (0, i, 0))
    hv = pl.BlockSpec((HEADS, tm, D_V), lambda i: (0, i, 0))

    def rowspec(n):
        return pl.BlockSpec((tm, n), lambda i: (i, 0))

    return _hosted_call(
        body, name="mla_prep_bwd", grid=(rows // tm,),
        in_specs=[rowspec(Z_MLA), hq, hq, hv, const((1, Q_RANK)), const((1, KV_RANK)), const((1, D_QKP)),
                  const((1, D_QKP)), const((HEADS * D_QKP, Q_RANK)), const((KV_RANK, HEADS * D_NOPE)),
                  const((KV_RANK, HEADS * D_V)), tab, tab, tab],
        out_specs=[rowspec(Z_MLA), rowspec(HEADS * D_QKP), rowspec(HEADS * D_NOPE), rowspec(HEADS * D_V),
                   const((1, Q_RANK)), const((1, KV_RANK)), const((1, D_QKP)), const((1, D_QKP))],
        out_shape=[jax.ShapeDtypeStruct((rows, Z_MLA), BF16), jax.ShapeDtypeStruct((rows, HEADS * D_QKP), BF16),
                   jax.ShapeDtypeStruct((rows, HEADS * D_NOPE), BF16), jax.ShapeDtypeStruct((rows, HEADS * D_V), BF16),
                   jax.ShapeDtypeStruct((1, Q_RANK), F32), jax.ShapeDtypeStruct((1, KV_RANK), F32),
                   jax.ShapeDtypeStruct((1, D_QKP), F32), jax.ShapeDtypeStruct((1, D_QKP), F32)],
        args=(z, dq, dk, dv, gql, gkvl, gqh, gkh, wuq, wuk, wuv, *tabs), comm=comm)


def _local_step(h0, target, w, nb, lp, sched=None):
    tm = _row_tile(nb * lp, 1408)
    tmid = _row_tile(nb * lp, 704)
    te = _row_tile(lp, 512)
    tabs = _rope_tables(lp)
    g = {}
    if sched is None:
        host = lambda stage: None
    else:
        sched.g = g
        host = sched.host

    def ffn_fwd(tag, h, split):
        u = _rmsnorm_call(tag + "_norm", h, w[tag + "_norm"], te)
        if split:
            gate = _ffn_gate_call(tag + "_gate", u, w[tag + "_w_gate"], tm, host(tag + "_gate"))
            up, act = _ffn_upact_call(tag + "_upact", u, w[tag + "_w_up"], gate, tm, host(tag + "_upact"))
        else:
            gate, up, act = _ffn_up_call(tag + "_up", u, w[tag + "_w_gate"], w[tag + "_w_up"], tm, host(tag + "_up"))
        return _ffn_down_call(tag + "_down", act, w[tag + "_w_down"], h, tmid, host(tag + "_down")), (u, gate, up, act)

    def ffn_bwd(tag, h, saved, dh, dhb):
        u, gate, up, act = saved
        dgate, dup = _ffn_dact_call(tag + "_dact", dhb, w[tag + "_w_down"], gate, up, tm, host(tag + "_dact"))
        g[tag + "_w_down"] = _wgrad_call(tag + "_dwd", act, dhb, 0.5, host(tag + "_dwd"))
        g[tag + "_w_gate"] = _wgrad_call(tag + "_dwg", dgate, u, 1.0, host(tag + "_dwg"))
        g[tag + "_w_up"] = _wgrad_call(tag + "_dwu", dup, u, 1.0, host(tag + "_dwu"))
        dh_in, dhb_in, g[tag + "_norm"] = _norm_in_bwd_call(
            tag + "_din", [(dgate, w[tag + "_w_gate"]), (dup, w[tag + "_w_up"])], h, w[tag + "_norm"], dh, tmid,
            host(tag + "_din"))
        return dh_in, dhb_in

    h1, s1 = ffn_fwd("ffn1", h0, True)
    un = _rmsnorm_call("mix_norm", h1, w["mix_norm"], te)
    z = _mm_call("mix_in", un, w["w_in"], tm, F32)
    mla_w = (w["q_latent_norm"], w["kv_latent_norm"], w["q_head_norm"], w["k_head_norm"], w["w_uq"], w["w_uk"],
             w["w_uv"])
    q, k, v, cqn, ckvn = _mla_prep_call(z, *mla_w, tabs, lp, te)
    o, lse = _attn_fwd_call(q, k, v, nb, lp, host("attn_fwd"))
    lru_w = (w["conv_w"], w["conv_b"], w["gate_a_w"], w["gate_a_b"], w["gate_x_w"], w["gate_x_b"], w["lru_lambda"])
    yl, hs = _lru_fwd_call(z, *lru_w, nb, lp, host("lru_fwd"))
    y, h2 = _mix_out_call(o, yl, w["attn_out_norm"], w["lru_out_norm"], w["w_out"], h1, te)
    h3, s2 = ffn_fwd("ffn2", h2, False)
    dh3, dh3b, g["final_norm"], loss = _final_call(h3, w["final_norm"], target, lp, te)
    g["loss"] = loss

    dh2, dh2b = ffn_bwd("ffn2", h2, s2, dh3, dh3b)
    g["w_out"] = _wgrad_call("dw_out", y, dh2b)
    dya, dyl, g["attn_out_norm"], g["lru_out_norm"] = _mix_out_bwd_call(
        dh2b, w["w_out"], o, yl, w["attn_out_norm"], w["lru_out_norm"], te)
    dq, dk, dv = _attn_bwd_call(q, k, v, o, lse, dya, nb, lp, host("attn_bwd"))
    (dz_mla, dqp, dkn, dvv, g["q_latent_norm"], g["kv_latent_norm"], g["q_head_norm"],
     g["k_head_norm"]) = _mla_prep_bwd_call(z, dq, dk, dv, *mla_w, tabs, lp, te, host("mla_prep_bwd"))
    g["w_uq"] = _wgrad_call("dw_uq", dqp, cqn)
    g["w_uk"] = _wgrad_call("dw_uk", ckvn, dkn)
    g["w_uv"] = _wgrad_call("dw_uv", ckvn, dvv)
    (du, dgt, g["conv_w"], g["conv_b"], g["gate_a_w"], g["gate_a_b"], g["gate_x_w"], g["gate_x_b"],
     g["lru_lambda"]) = _lru_bwd_call(z, hs, dyl, *lru_w, nb, lp, host("lru_bwd"))
    win = w["w_in"]
    g["w_in"] = jnp.concatenate(
        [_wgrad_call("dw_in_mla", dz_mla, un), _wgrad_call("dw_in_u", du, un), _wgrad_call("dw_in_g", dgt, un)],
        axis=0)
    dh1, dh1b, g["mix_norm"] = _norm_in_bwd_call(
        "mix_din", [(dz_mla, win[0:Z_MLA]), (du, win[Z_U:Z_G]), (dgt, win[Z_G:Z_W])], h1, w["mix_norm"], dh2, tmid,
        host("mix_din"))
    dh0, _ = ffn_bwd("ffn1", h0, s1, dh1, dh1b)
    return loss, dh0, g


def _place():
    x, y, c = lax.axis_index("x"), lax.axis_index("y"), lax.axis_index("c")
    return x, y, c, [(1 - x, y), (x, 1 - y), (1 - x, 1 - y)]


def _any_specs(n):
    return [pl.BlockSpec(memory_space=pl.ANY)] * n


def _remote(src, dst, sems, k, dev):
    send_sems, recv_sems, base = sems
    return pltpu.make_async_remote_copy(src_ref=src, dst_ref=dst, send_sem=send_sems.at[base + k],
                                        recv_sem=recv_sems.at[base + k], device_id=dev, device_id_type=MESH)


EW_VMEM_BYTES = 24 * 1024 * 1024


def _fit_rows(rows, cols, blocks):
    return _row_tile(rows, max(16, int(EW_VMEM_BYTES // (8 * blocks)) // cols))


class _Geom:
    def __init__(self, n0, n1, blocks=1.0):
        self.n0, self.n1 = n0, n1
        self.axis = 0 if n0 % 32 == 0 else 1
        self.h0, self.h1 = (n0 // 2, n1) if self.axis == 0 else (n0, n1 // 2)
        self.tr = _fit_rows(self.h0, self.h1, blocks)
        self.nblk = self.h0 // self.tr

    def half_ref(self, ref, lead, idx):
        if self.axis == 0:
            return ref.at[(*lead, pl.ds(idx * self.h0, self.h0))]
        return ref.at[(*lead, slice(None), pl.ds(idx * self.h1, self.h1))]

    def half_block(self, lead, i, idx):
        return (*lead, idx * self.nblk + i, 0) if self.axis == 0 else (*lead, i, idx)


class _Comm:
    def __init__(self, ins, out_shapes, aliases, n_sems, start, finish, deliver):
        self.ins, self.out_shapes, self.aliases, self.n_sems = list(ins), list(out_shapes), dict(aliases), n_sems
        self.start, self.finish, self.deliver = start, finish, deliver

    def scratch(self):
        return [pltpu.SemaphoreType.DMA((self.n_sems,)), pltpu.SemaphoreType.DMA((self.n_sems,))]


def _comm_call(name, comm):
    n_in = len(comm.ins)

    def body(*refs):
        ins, outs, sems = refs[:n_in], refs[n_in:-2], (*refs[-2:], 0)
        comm.start(ins, outs, sems)
        comm.finish(ins, outs, sems)

    res = pl.pallas_call(
        body, name=name, out_shape=comm.out_shapes, in_specs=_any_specs(n_in),
        out_specs=_any_specs(len(comm.out_shapes)), input_output_aliases=comm.aliases,
        scratch_shapes=comm.scratch())(*comm.ins)
    return comm.deliver(list(res))


def _hosted_call(body, *, name, grid, in_specs, out_specs, out_shape, args, scratch_shapes=(), dims=None, comm=None):
    in_specs, out_specs, out_shape = list(in_specs), list(out_specs), list(out_shape)
    if comm is None:
        return pl.pallas_call(
            body, name=name, grid=grid, in_specs=in_specs, out_specs=out_specs, out_shape=out_shape,
            scratch_shapes=list(scratch_shapes),
            compiler_params=_params(dims or ("arbitrary",) * len(grid)))(*args)
    n_in, n_out, n_ci, n_co = len(in_specs), len(out_specs), len(comm.ins), len(comm.out_shapes)

    def wrapped(*refs):
        ins, cins = refs[:n_in], refs[n_in:n_in + n_ci]
        outs = refs[n_in + n_ci:n_in + n_ci + n_out]
        couts = refs[n_in + n_ci + n_out:n_in + n_ci + n_out + n_co]
        scratch, sems = refs[n_in + n_ci + n_out + n_co:-2], (*refs[-2:], 0)
        first = functools.reduce(jnp.logical_and, [pl.program_id(k) == 0 for k in range(len(grid))])
        last = functools.reduce(jnp.logical_and, [pl.program_id(k) == grid[k] - 1 for k in range(len(grid))])

        @pl.when(first)
        def _():
            comm.start(cins, couts, sems)

        body(*ins, *outs, *scratch)

        @pl.when(last)
        def _():
            comm.finish(cins, couts, sems)

    res = pl.pallas_call(
        wrapped, name=name, grid=grid, in_specs=in_specs + _any_specs(n_ci), out_specs=out_specs + _any_specs(n_co),
        out_shape=out_shape + comm.out_shapes,
        input_output_aliases={n_in + i: n_out + o for i, o in comm.aliases.items()},
        scratch_shapes=list(scratch_shapes) + comm.scratch(),
        compiler_params=_params(("arbitrary",) * len(grid)))(*args, *comm.ins)
    comm.deliver(list(res[n_out:]))
    return list(res[:n_out])


def _gather_comm(bufs, deliver):
    n = len(bufs)
    geoms = [_Geom(*b.shape[1:]) for b in bufs]

    def first(outs, sems):
        x, y, c, chips = _place()
        cps = []
        for a in range(n):
            mine = geoms[a].half_ref(outs[a], (2 * x + y,), c)
            cps += [_remote(mine, mine, sems, 6 * a + j, (cx, cy, c)) for j, (cx, cy) in enumerate(chips)]
        return cps

    def start(ins, outs, sems):
        for cp in first(outs, sems):
            cp.start()

    def finish(ins, outs, sems):
        x, y, c, chips = _place()
        sib = (x, y, 1 - c)
        passed = []
        for a in range(n):
            for j, (cx, cy) in enumerate(chips):
                land = geoms[a].half_ref(outs[a], (2 * cx + cy,), c)
                _remote(land, land, sems, 6 * a + j, sib).wait_recv()
                cp = _remote(land, land, sems, 6 * a + 3 + j, sib)
                cp.start()
                passed.append(cp)
        for a in range(n):
            for j, (cx, cy) in enumerate(chips):
                land = geoms[a].half_ref(outs[a], (2 * cx + cy,), 1 - c)
                _remote(land, land, sems, 6 * a + 3 + j, sib).wait_recv()
        for cp in first(outs, sems) + passed:
            cp.wait_send()

    return _Comm(bufs, [jax.ShapeDtypeStruct(b.shape, b.dtype) for b in bufs], {a: a for a in range(n)}, 6 * n,
                 start, finish, deliver)


def _reduce_pair_comm(grads, deliver):
    n = len(grads)
    geoms = [_Geom(*a.shape[1:]) for a in grads]

    def copies(ins, outs, sems):
        x, y, c, _ = _place()
        return [_remote(geoms[a].half_ref(ins[a], (slice(None),), 1 - c), outs[a], sems, a, (x, y, 1 - c))
                for a in range(n)]

    def start(ins, outs, sems):
        for cp in copies(ins, outs, sems):
            cp.start()

    def finish(ins, outs, sems):
        cps = copies(ins, outs, sems)
        for cp in cps:
            cp.wait_recv()
        for cp in cps:
            cp.wait_send()

    shapes = [jax.ShapeDtypeStruct((N_SHARD, g.h0, g.h1), a.dtype) for a, g in zip(grads, geoms)]
    return _Comm(grads, shapes, {}, n, start, finish, deliver)


def _reduce_chips_comm(parts, deliver):
    n = len(parts)

    def copies(ins, outs, sems):
        x, y, c, chips = _place()
        return [_remote(ins[a].at[2 * cx + cy], outs[a].at[j], sems, 3 * a + j, (cx, cy, c))
                for a in range(n) for j, (cx, cy) in enumerate(chips)]

    def start(ins, outs, sems):
        for cp in copies(ins, outs, sems):
            cp.start()

    def finish(ins, outs, sems):
        cps = copies(ins, outs, sems)
        for cp in cps:
            cp.wait_recv()
        for cp in cps:
            cp.wait_send()

    shapes = [jax.ShapeDtypeStruct((3,) + a.shape[1:], a.dtype) for a in parts]
    return _Comm(parts, shapes, {}, 3 * n, start, finish, deliver)


def _share_pair_comm(bufs, deliver):
    n = len(bufs)
    geoms = [_Geom(*b.shape) for b in bufs]

    def copies(outs, sems):
        x, y, c, _ = _place()
        cps = []
        for a in range(n):
            mine = geoms[a].half_ref(outs[a], (), c)
            cps.append(_remote(mine, mine, sems, a, (x, y, 1 - c)))
        return cps

    def start(ins, outs, sems):
        for cp in copies(outs, sems):
            cp.start()

    def finish(ins, outs, sems):
        x, y, c, _ = _place()
        for a in range(n):
            land = geoms[a].half_ref(outs[a], (), 1 - c)
            _remote(land, land, sems, a, (x, y, 1 - c)).wait_recv()
        for cp in copies(outs, sems):
            cp.wait_send()

    return _Comm(bufs, [jax.ShapeDtypeStruct(b.shape, b.dtype) for b in bufs], {a: a for a in range(n)}, n,
                 start, finish, deliver)


def _small_comm(pack, deliver):
    r, d = pack.shape

    def copies(ins, outs, sems):
        x, y, c, _ = _place()
        cps = []
        for k in range(1, 8):
            peer = (x ^ ((k >> 2) & 1), y ^ ((k >> 1) & 1), c ^ (k & 1))
            cps.append(_remote(ins[0], outs[0].at[4 * x + 2 * y + c], sems, k - 1, peer))
        return cps

    def start(ins, outs, sems):
        for cp in copies(ins, outs, sems):
            cp.start()

    def finish(ins, outs, sems):
        cps = copies(ins, outs, sems)
        for cp in cps:
            cp.wait_recv()
        for cp in cps:
            cp.wait_send()

    return _Comm([pack], [jax.ShapeDtypeStruct((8, r, d), pack.dtype)], {}, 7, start, finish, deliver)


def _join_comms(comms):
    comms = [c for c in comms if c is not None]
    if len(comms) <= 1:
        return comms[0] if comms else None
    ins, out_shapes, aliases, spans, n_sems = [], [], {}, [], 0
    for c in comms:
        aliases.update({len(ins) + i: len(out_shapes) + o for i, o in c.aliases.items()})
        spans.append((len(ins), len(ins) + len(c.ins), len(out_shapes), len(out_shapes) + len(c.out_shapes), n_sems))
        ins += c.ins
        out_shapes += c.out_shapes
        n_sems += c.n_sems

    def run(which):
        def go(all_ins, all_outs, sems):
            for c, (i0, i1, o0, o1, base) in zip(comms, spans):
                getattr(c, which)(all_ins[i0:i1], all_outs[o0:o1], (sems[0], sems[1], sems[2] + base))
        return go

    def deliver(outs):
        for c, (_, _, o0, o1, _) in zip(comms, spans):
            c.deliver(outs[o0:o1])
        return outs

    return _Comm(ins, out_shapes, aliases, n_sems, run("start"), run("finish"), deliver)


def _ew_call(name, fn, ins, out_dtypes):
    shape = ins[0].shape
    cols = shape[-1]
    rows = 1
    for s_ in shape[:-1]:
        rows *= s_
    ins2 = [a.reshape(rows, cols) for a in ins]
    tr = rows
    for t in range(16, min(rows, max(16, (1 << 19) // cols)) + 1, 16):
        if rows % t == 0:
            tr = t
    no = len(out_dtypes)

    def body(*refs):
        outs = fn(*[r[...] for r in refs[:len(ins2)]])
        for ref, val in zip(refs[len(ins2):], outs):
            ref[...] = val.astype(ref.dtype)

    spec = pl.BlockSpec((tr, cols), lambda i: (i, 0))
    res = pl.pallas_call(
        body, name=name, grid=(rows // tr,), in_specs=[spec] * len(ins2), out_specs=[spec] * no,
        out_shape=[jax.ShapeDtypeStruct((rows, cols), dt) for dt in out_dtypes],
        compiler_params=_params(("parallel",)))(*ins2)
    return [r.reshape(shape) for r in res]


def _adamw_math(w, g, m, v):
    m = ADAM_B1 * m + (1.0 - ADAM_B1) * g
    v = ADAM_B2 * v + (1.0 - ADAM_B2) * (g * g)
    m_hat = m / (1.0 - ADAM_B1 ** ADAM_STEP)
    v_hat = v / (1.0 - ADAM_B2 ** ADAM_STEP)
    delta = -ADAM_LR * (m_hat / (jnp.sqrt(v_hat) + ADAM_EPS) + ADAM_WD * w)
    return delta, m, v


def _adamw_call(name, w, g, m, v):
    return _ew_call(name, _adamw_math, [w, g, m, v], [F32, F32, F32])


def _tiled_call(name, fn, place, grid, in_items, out_items):
    ni = len(in_items)

    def body(place_ref, *refs):
        vals = fn(*[r[...] for r in refs[:ni]])
        for ref, val in zip(refs[ni:], vals):
            ref[...] = val.astype(ref.dtype)

    spec = pltpu.PrefetchScalarGridSpec(
        num_scalar_prefetch=1, grid=grid,
        in_specs=[pl.BlockSpec(blk, imap) for _, blk, imap in in_items],
        out_specs=[pl.BlockSpec(blk, imap) for _, _, blk, imap in out_items])
    return pl.pallas_call(
        body, name=name, grid_spec=spec,
        out_shape=[jax.ShapeDtypeStruct(shp, dt) for shp, dt, _, _ in out_items],
        compiler_params=_params(("arbitrary",) * len(grid)))(place, *[a for a, _, _ in in_items])


def _cast_call(name, place, shards):
    n0, n1 = shards[0].shape
    tr = _fit_rows(n0, n1, 1.5 * len(shards))
    ins = [(a, (tr, n1), lambda i, p: (i, 0)) for a in shards]
    outs = [((N_SHARD, n0, n1), BF16, (1, tr, n1), lambda i, p: (p[0], i, 0)) for _ in shards]
    return _tiled_call(name, lambda *v: [x[None] for x in v], place, (n0 // tr,), ins, outs)


def _pair_sum_call(name, place, fulls, gots):
    k = len(fulls)
    g = _Geom(*fulls[0].shape[1:], blocks=2.5 * k)
    blk = (1, g.tr, g.h1)
    ins = [(a, blk, lambda s, i, p: g.half_block((s,), i, p[1])) for a in fulls]
    ins += [(a, blk, lambda s, i, p: (s, i, 0)) for a in gots]
    outs = [((N_SHARD, g.h0, g.h1), BF16, blk, lambda s, i, p: (s, i, 0)) for _ in fulls]
    return _tiled_call(name, lambda *v: [v[j] + v[k + j] for j in range(k)], place, (N_SHARD, g.nblk), ins, outs)


def _chip_sum_call(name, place, fulls, gots, recvs):
    k = len(fulls)
    g = _Geom(*fulls[0].shape[1:], blocks=4.5 * k)
    blk = (1, g.tr, g.h1)
    ins = [(a, blk, lambda i, p: g.half_block((p[0],), i, p[1])) for a in fulls]
    ins += [(a, blk, lambda i, p: (p[0], i, 0)) for a in gots]
    ins += [(a, (3, g.tr, g.h1), lambda i, p: (0, i, 0)) for a in recvs]
    outs = [((g.n0, g.n1), F32, (g.tr, g.h1), lambda i, p: g.half_block((), i, p[1])) for _ in fulls]

    def fn(*v):
        res = []
        for j in range(k):
            r = v[2 * k + j].astype(F32)
            res.append(v[j][0] + v[k + j][0] + r[0] + r[1] + r[2])
        return res

    return _tiled_call(name, fn, place, (g.nblk,), ins, outs)


def _adamw_group_call(name, ws, gs, ms, vs, comm=None):
    k = len(ws)
    n0, n1 = ws[0].shape
    tr = _fit_rows(n0, n1, 8 * k)
    spec = pl.BlockSpec((tr, n1), lambda i: (i, 0))

    def body(*refs):
        for j in range(k):
            g = refs[k + j][...]
            delta, m, vv = _adamw_math(refs[j][...], g, refs[2 * k + j][...], refs[3 * k + j][...])
            for ref, val in zip(refs[4 * k + 4 * j:4 * k + 4 * j + 4], (g, delta, m, vv)):
                ref[...] = val

    flat = _hosted_call(
        body, name=name, grid=(n0 // tr,), in_specs=[spec] * (4 * k), out_specs=[spec] * (4 * k),
        out_shape=[jax.ShapeDtypeStruct((n0, n1), F32)] * (4 * k), dims=("parallel",),
        args=(*ws, *gs, *ms, *vs), comm=comm)
    return [flat[4 * j:4 * j + 4] for j in range(k)]


def _small_update_call(me, early, own_early, late, own_late, wp, mp, vp):
    nd, r, d = early.shape

    def body(me_ref, e_ref, oe_ref, l_ref, ol_ref, w_ref, m_ref, v_ref, gs_ref, d_ref, nm_ref, nv_ref):
        mine = me_ref[0]

        def total(g_ref, own_ref):
            acc = None
            for k in range(nd):
                part = jnp.where(mine == k, own_ref[...], g_ref[k])
                acc = part if acc is None else acc + part
            return acc

        gs = total(e_ref, oe_ref)
        ls = total(l_ref, ol_ref)
        gs_ref[...] = gs
        first = gs[0:8] + ls[0:8]
        gs_ref[0:8, :] = first
        gs_ref[ROW_META:ROW_META + N_META, :] = gs[ROW_META:ROW_META + N_META] + ls[8:8 + N_META]
        grads = jnp.concatenate([first, gs[8:SMALL_ADAM_ROWS]], axis=0)
        delta, m, v = _adamw_math(w_ref[...], grads, m_ref[...], v_ref[...])
        d_ref[...] = delta
        nm_ref[...] = m
        nv_ref[...] = v

    vm = pl.BlockSpec(memory_space=pltpu.VMEM)
    ashape = jax.ShapeDtypeStruct((SMALL_ADAM_ROWS, d), F32)
    return pl.pallas_call(
        body, name="small_update", in_specs=[pl.BlockSpec(memory_space=pltpu.SMEM)] + [vm] * 7, out_specs=[vm] * 4,
        out_shape=[jax.ShapeDtypeStruct((r, d), F32), ashape, ashape, ashape],
        compiler_params=pltpu.CompilerParams(vmem_limit_bytes=VMEM_LIMIT_BYTES))(
            me, early, own_early, late, own_late, wp, mp, vp)


SMALL_NAMES = ["ffn1_norm", "mix_norm", "ffn2_norm", "final_norm", "q_latent_norm", "kv_latent_norm",
               "q_head_norm", "k_head_norm", "conv_b", "gate_a_b", "gate_x_b", "lru_lambda", "attn_out_norm",
               "lru_out_norm"]
ROW_CONV_W = 14
ROW_GATE_A = 16
ROW_GATE_X = 48
ROW_META = 80
ROW_LOSS = 96


def _row(a):
    flat = a.reshape(1, -1)
    return jnp.pad(flat, ((0, 0), (0, D_MODEL - flat.shape[1])))


def _pack_small(t, rows):
    parts = [_row(t[nm]) for nm in SMALL_NAMES]
    parts.append(t["conv_w"].reshape(2, D_MODEL))
    parts.append(t["gate_a_w"].reshape(32, D_MODEL))
    parts.append(t["gate_x_w"].reshape(32, D_MODEL))
    p = jnp.concatenate(parts, axis=0)
    return jnp.pad(p, ((0, rows - p.shape[0]), (0, 0)))


def _early_pack(g):
    gs = {nm: g.get(nm, jnp.zeros((1, D_MODEL), F32)) for nm in SMALL_NAMES}
    gs["q_head_norm"] = g["q_head_norm"][:, 0:D_QK]
    gs["k_head_norm"] = g["k_head_norm"][:, 0:D_QK]
    for nm in ("conv_b", "gate_a_b", "gate_x_b", "lru_lambda"):
        gs[nm] = g[nm].reshape(1, LRU_W)
    gs["conv_w"] = g["conv_w"].transpose(1, 0, 2).reshape(CONV_K, LRU_W)
    gs["gate_a_w"] = _gate_blocks(g["gate_a_w"])
    gs["gate_x_w"] = _gate_blocks(g["gate_x_w"])
    return jnp.concatenate([_pack_small(gs, ROW_META), jnp.zeros((N_META, D_MODEL), F32), _row(g["loss"][:, 0:1]),
                            jnp.zeros((SMALL_ROWS - ROW_LOSS - 1, D_MODEL), F32)], axis=0)


def _unpack_small(p, like):
    out = {}
    for k, nm in enumerate(SMALL_NAMES):
        out[nm] = p[k, 0:like[nm].size].reshape(like[nm].shape)
    out["gate_a_w"] = p[ROW_GATE_A:ROW_GATE_A + 32].reshape(like["gate_a_w"].shape)
    out["gate_x_w"] = p[ROW_GATE_X:ROW_GATE_X + 32].reshape(like["gate_x_w"].shape)
    return out


def _gate_dense(wg):
    w4 = wg[0].reshape(N_LRU_TILES, 2, 64, 64)
    zero = jnp.zeros((N_LRU_TILES, 64, 64), wg.dtype)
    top = jnp.concatenate([w4[:, 0], zero], axis=2)
    bot = jnp.concatenate([zero, w4[:, 1]], axis=2)
    return jnp.concatenate([top, bot], axis=1).astype(BF16)


def _gate_blocks(dw):
    return jnp.stack([dw[:, 0:64, 0:64], dw[:, 64:128, 64:128]], axis=1).reshape(8, 64, 64)


BIG_NAMES = ["ffn1_w_gate", "ffn1_w_up", "ffn1_w_down", "w_in", "w_uq", "w_uk", "w_uv", "w_out", "ffn2_w_gate",
             "ffn2_w_up", "ffn2_w_down"]
BIG_GROUPS = [["ffn1_w_gate", "ffn1_w_up", "ffn1_w_down", "ffn2_w_gate", "ffn2_w_up", "ffn2_w_down"], ["w_in"],
              ["w_uq"], ["w_uk", "w_uv"], ["w_out"]]
TRANSPOSED = ("ffn1_w_gate", "ffn1_w_up", "ffn2_w_gate", "ffn2_w_up", "w_in", "w_uq")


def _to2d(nm, a):
    return a[0].T if nm in TRANSPOSED else a[0]


def _from2d(nm, a):
    return (a.T if nm in TRANSPOSED else a)[None]


WEIGHT_NAMES = ["meta_tokens", "ffn1_norm", "ffn1_w_gate", "ffn1_w_up", "ffn1_w_down", "mix_norm", "w_in",
                "q_latent_norm", "w_uq", "kv_latent_norm", "w_uk", "w_uv", "q_head_norm", "k_head_norm", "conv_w",
                "conv_b", "gate_a_w", "gate_a_b", "gate_x_w", "gate_x_b", "lru_lambda", "attn_out_norm",
                "lru_out_norm", "w_out", "ffn2_norm", "ffn2_w_gate", "ffn2_w_up", "ffn2_w_down", "final_norm"]


def _weight_from(nm, slots):
    if nm == "w_in":
        win = slots.reshape(IN_WIDTH, D_MODEL)
        return jnp.concatenate([win[0:Z_KR + D_ROPE], jnp.zeros((128 - D_ROPE, D_MODEL), BF16),
                                win[Z_KR + D_ROPE:]], axis=0)
    if nm == "w_uq":
        return jnp.pad(slots, ((0, 0), (0, D_QKP - D_QK), (0, 0))).reshape(HEADS * D_QKP, Q_RANK)
    if nm in ("w_uk", "w_uv"):
        return slots.transpose(1, 0, 2).reshape(KV_RANK, HEADS * D_NOPE)
    if nm == "w_out":
        return slots.reshape(D_MODEL, D_MODEL)
    return slots


def _small_weights(p, small):
    w = {nm: p[nm] for nm in SMALL_NAMES}
    w["q_head_norm"] = jnp.pad(p["q_head_norm"], ((0, 0), (0, D_QKP - D_QK)))
    w["k_head_norm"] = jnp.pad(p["k_head_norm"], ((0, 0), (0, D_QKP - D_QK)))
    w["conv_w"] = small[:, N_META:N_META + 2, :].reshape(N_SHARD, CONV_K, LRU_TILE)
    w["gate_a_w"] = _gate_dense(p["gate_a_w"])
    w["gate_x_w"] = _gate_dense(p["gate_x_w"])
    meta = small[:, 0:N_META, :].transpose(1, 0, 2).reshape(N_META, D_MODEL)
    return w, meta


def _full_weights(p, gathered, small):
    w, meta = _small_weights(p, small)
    w.update({nm: _weight_from(nm, gathered[nm]) for nm in BIG_NAMES})
    return w, meta


def _shard_grad(nm, g):
    if nm == "w_in":
        return jnp.concatenate([g[0:Z_KR + D_ROPE], g[Z_MLA:]], axis=0).reshape(N_SHARD, IN_WIDTH // N_SHARD, D_MODEL)
    if nm == "w_uq":
        return g.reshape(HEADS, D_QKP, Q_RANK)[:, 0:D_QK, :]
    if nm in ("w_uk", "w_uv"):
        return g.reshape(KV_RANK, HEADS, D_NOPE).transpose(1, 0, 2)
    if nm == "w_out":
        return g.reshape(N_SHARD, D_MODEL // N_SHARD, D_MODEL)
    return g


def _shard_grads(g):
    return {nm: _shard_grad(nm, g[nm]) for nm in BIG_NAMES}


GATHER_FIRST = ["ffn1_w_gate"]
GATHER_AT = {"ffn1_gate": ["ffn1_w_up"], "ffn1_upact": ["ffn1_w_down"],
             "ffn1_down": ["w_in", "w_uq", "w_uk", "w_uv", "w_out"], "attn_fwd": ["ffn2_w_down", "ffn2_w_gate"],
             "lru_fwd": ["ffn2_w_up"]}
PAIR_AT = [("ffn2_din", ["ffn2_w_gate", "ffn2_w_up", "ffn2_w_down"]),
           ("mix_din", ["w_out", "w_uq", "w_uk", "w_uv", "w_in"]),
           ("ffn1_dwg", ["ffn1_w_down"]), ("ffn1_dwu", ["ffn1_w_gate"]), ("ffn1_din", ["ffn1_w_up"])]
CHIPS_AT = [("attn_bwd", ["ffn2_w_down", "ffn2_w_gate"]), ("mla_prep_bwd", ["ffn2_w_up"]),
            ("ffn1_dact", ["w_out", "w_uq", "w_uk", "w_uv", "w_in"]),
            ("ffn1_dwu", ["ffn1_w_down"]), ("ffn1_din", ["ffn1_w_gate"]), (None, ["ffn1_w_up"])]
SMALL_EARLY_AT = "mix_din"


def _same_shape_groups(names):
    return [[nm for nm in grp if nm in names] for grp in BIG_GROUPS if any(nm in names for nm in grp)]


class _Sched:
    def __init__(self, place, w, slots):
        self.place, self.w, self.slots = place, w, slots
        self.g = None
        self.sharded, self.from_pair, self.chip_bf16, self.from_chips = {}, {}, {}, {}
        self.early = self.early_all = None

    def host(self, stage):
        comms = []
        if stage in GATHER_AT:
            comms.append(self.gather(GATHER_AT[stage]))
        comms += [self.chips(names) for at, names in CHIPS_AT if at == stage]
        comms += [self.pair(names) for at, names in PAIR_AT if at == stage]
        if stage == SMALL_EARLY_AT:
            comms.append(self.small_early())
        return _join_comms(comms)

    def small_early(self):
        self.early = _early_pack(self.g)

        def deliver(outs):
            self.early_all = outs[0]
            return outs

        return _small_comm(self.early, deliver)

    def gather(self, names):
        def deliver(outs):
            self.w.update({nm: _weight_from(nm, o) for nm, o in zip(names, outs)})
            return outs

        return _gather_comm([self.slots[nm] for nm in names], deliver)

    def pair(self, names):
        self.sharded.update({nm: _shard_grad(nm, self.g[nm]) for nm in names})

        def deliver(outs):
            self.from_pair.update(zip(names, outs))
            for grp in _same_shape_groups(names):
                sums = _pair_sum_call("pair_sum_" + grp[0], self.place, [self.sharded[nm] for nm in grp],
                                      [self.from_pair[nm] for nm in grp])
                self.chip_bf16.update(zip(grp, sums))
            return outs

        return _reduce_pair_comm([self.sharded[nm] for nm in names], deliver)

    def chips(self, names):
        def deliver(outs):
            self.from_chips.update(zip(names, outs))
            return outs

        return _reduce_chips_comm([self.chip_bf16[nm] for nm in names], deliver)

    def chip_sums(self, names):
        out = {}
        for grp in _same_shape_groups(names):
            sums = _chip_sum_call("chip_sum_" + grp[0], self.place, [self.sharded[nm] for nm in grp],
                                  [self.from_pair[nm] for nm in grp], [self.from_chips[nm] for nm in grp])
            out.update(zip(grp, sums))
        return out


def kernel(x, meta_tokens, ffn1_norm, ffn1_w_gate, ffn1_w_up, ffn1_w_down, mix_norm, w_in, q_latent_norm, w_uq, kv_latent_norm, w_uk, w_uv, q_head_norm, k_head_norm, conv_w, conv_b, gate_a_w, gate_a_b, gate_x_w, gate_x_b, lru_lambda, attn_out_norm, lru_out_norm, w_out, ffn2_norm, ffn2_w_gate, ffn2_w_up, ffn2_w_down, final_norm, loss_target, m_meta_tokens, m_ffn1_norm, m_ffn1_w_gate, m_ffn1_w_up, m_ffn1_w_down, m_mix_norm, m_w_in, m_q_latent_norm, m_w_uq, m_kv_latent_norm, m_w_uk, m_w_uv, m_q_head_norm, m_k_head_norm, m_conv_w, m_conv_b, m_gate_a_w, m_gate_a_b, m_gate_x_w, m_gate_x_b, m_lru_lambda, m_attn_out_norm, m_lru_out_norm, m_w_out, m_ffn2_norm, m_ffn2_w_gate, m_ffn2_w_up, m_ffn2_w_down, m_final_norm, v_meta_tokens, v_ffn1_norm, v_ffn1_w_gate, v_ffn1_w_up, v_ffn1_w_down, v_mix_norm, v_w_in, v_q_latent_norm, v_w_uq, v_kv_latent_norm, v_w_uk, v_w_uv, v_q_head_norm, v_k_head_norm, v_conv_w, v_conv_b, v_gate_a_w, v_gate_a_b, v_gate_x_w, v_gate_x_b, v_lru_lambda, v_attn_out_norm, v_lru_out_norm, v_w_out, v_ffn2_norm, v_ffn2_w_gate, v_ffn2_w_up, v_ffn2_w_down, v_final_norm):
    args = locals()
    p = {nm: args[nm] for nm in WEIGHT_NAMES}
    mom = {nm: args["m_" + nm] for nm in WEIGHT_NAMES}
    var = {nm: args["v_" + nm] for nm in WEIGHT_NAMES}
    nb, seq, d = x.shape
    lp = CHUNK + seq
    xi, yi, ci = lax.axis_index("x"), lax.axis_index("y"), lax.axis_index("c")
    chip = 2 * xi + yi

    place = jnp.stack([chip, ci]).astype(jnp.int32)
    p2 = {nm: _to2d(nm, p[nm]) for nm in BIG_NAMES}
    m2 = {nm: _to2d(nm, mom[nm]) for nm in BIG_NAMES}
    v2 = {nm: _to2d(nm, var[nm]) for nm in BIG_NAMES}

    slots = {}
    for grp in BIG_GROUPS:
        for nm, buf in zip(grp, _cast_call("cast_" + grp[0], place, [p2[nm] for nm in grp])):
            slots[nm] = buf
    small_shard = jnp.concatenate(
        [meta_tokens, conv_w[0].reshape(2, 2 * LRU_TILE), jnp.zeros((14, 2 * LRU_TILE), F32)], axis=0)
    small_slots = lax.dynamic_update_slice(jnp.zeros((N_SHARD,) + small_shard.shape, F32), small_shard[None],
                                           (chip, 0, 0))
    first = _comm_call("gather_first", _gather_comm([slots[nm] for nm in GATHER_FIRST] + [small_slots], lambda o: o))
    w, meta = _small_weights(p, first[-1])
    w.update({nm: _weight_from(nm, o) for nm, o in zip(GATHER_FIRST, first[:-1])})
    sched = _Sched(place, w, slots)

    h0 = jnp.concatenate(
        [jnp.zeros((nb, PAD_ROWS, d), F32), jnp.broadcast_to(meta[None], (nb, N_META, d)), x], axis=1)
    target = jnp.pad(loss_target, ((0, 0), (CHUNK, 0), (0, 0)))
    loss_part, dh0, g = _local_step(h0.reshape(nb * lp, d), target.reshape(nb * lp, d), w, nb, lp, sched)
    dh0 = dh0.reshape(nb, lp, d)
    grad_x = dh0[:, CHUNK:, :]

    late = jnp.concatenate([g["ffn1_norm"], g["mix_norm"], jnp.zeros((6, D_MODEL), F32),
                            jnp.sum(dh0[:, PAD_ROWS:CHUNK, :], axis=0)], axis=0)
    late_all = _comm_call("gather_late", _small_comm(late, lambda o: o))[0]
    small_like = {nm: p[nm] for nm in SMALL_NAMES + ["gate_a_w", "gate_x_w"]}

    def pack_w(t):
        tt = {nm: t[nm] for nm in SMALL_NAMES + ["gate_a_w", "gate_x_w"]}
        tt["conv_w"] = jnp.zeros((CONV_K, LRU_W), F32)
        return _pack_small(tt, SMALL_ADAM_ROWS)

    me = (4 * xi + 2 * yi + ci).astype(jnp.int32).reshape(1)
    gsum, dsm, msm, vsm = _small_update_call(me, sched.early_all, sched.early, late_all, late, pack_w(p),
                                             pack_w(mom), pack_w(var))
    grads = _unpack_small(gsum, small_like)
    delta = _unpack_small(dsm, small_like)
    new_m = _unpack_small(msm, small_like)
    new_v = _unpack_small(vsm, small_like)
    loss = gsum[ROW_LOSS, 0]
    gmeta = gsum[ROW_META:ROW_META + N_META].reshape(N_META, N_SHARD, D_MODEL // N_SHARD)
    grads["meta_tokens"] = lax.dynamic_index_in_dim(gmeta, chip, axis=1, keepdims=False)
    gconv = gsum[ROW_CONV_W:ROW_CONV_W + 2].reshape(CONV_K, N_SHARD, LRU_TILE)
    grads["conv_w"] = lax.dynamic_index_in_dim(gconv, chip, axis=1, keepdims=False)[None]
    for nm in ("meta_tokens", "conv_w"):
        delta[nm], new_m[nm], new_v[nm] = _adamw_call("adamw_" + nm, p[nm], grads[nm], mom[nm], var[nm])

    done = [nm for at, names in CHIPS_AT if at is not None for nm in names]
    last = [nm for at, names in CHIPS_AT if at is None for nm in names]
    mine = sched.chip_sums(done)
    shard_grads = {}
    share = _share_pair_comm([mine[nm] for nm in done], lambda o: shard_grads.update(zip(done, o)))
    _comm_call("share_pair", _join_comms([share, sched.chips(last)]))

    def adamw(names):
        res = _adamw_group_call("adamw_" + names[0], [p2[nm] for nm in names], [shard_grads[nm] for nm in names],
                                [m2[nm] for nm in names], [v2[nm] for nm in names])
        for nm, (gg, dd, mm, vv) in zip(names, res):
            grads[nm], delta[nm], new_m[nm], new_v[nm] = (_from2d(nm, t) for t in (gg, dd, mm, vv))

    for grp in _same_shape_groups(done):
        adamw(grp)
    mine = sched.chip_sums(last)
    shard_grads = dict(zip(last, _comm_call("share_last", _share_pair_comm([mine[nm] for nm in last], lambda o: o))))
    adamw(last)

    return (loss, grad_x, *[grads[nm] for nm in WEIGHT_NAMES], *[delta[nm] for nm in WEIGHT_NAMES],
            *[new_m[nm] for nm in WEIGHT_NAMES], *[new_v[nm] for nm in WEIGHT_NAMES])
```

```python
import functools
import math

import jax
import jax.numpy as jnp
import numpy as np
from jax import lax
from jax.experimental import pallas as pl
from jax.experimental.pallas import tpu as pltpu

F32 = jnp.float32
BF16 = jnp.bfloat16
MESH = pl.DeviceIdType.MESH

D_MODEL = 1024
N_META = 16
CHUNK = 64
PAD_ROWS = CHUNK - N_META
HEADS = 4
D_NOPE = 128
D_ROPE = 64
D_QK = D_NOPE + D_ROPE
D_QKP = 256
D_V = 128
KV_RANK = 256
Q_RANK = 384
MLA_W = HEADS * D_V
LRU_W = 512
LRU_TILE = 128
N_LRU_TILES = LRU_W // LRU_TILE
CONV_K = 4
C_RGLRU = 8.0
ROPE_THETA = 10000.0
D_FF = 2816
N_SHARD = 4
EPS = 1e-6
NEG_INF = -1e30
Z_KR = Q_RANK + KV_RANK
Z_MLA = Z_KR + 128
Z_U = Z_MLA
Z_G = Z_U + LRU_W
Z_W = Z_G + LRU_W
IN_WIDTH = Q_RANK + KV_RANK + D_ROPE + 2 * LRU_W

ADAM_LR = 0.001
ADAM_B1 = 0.9
ADAM_B2 = 0.999
ADAM_EPS = 1e-08
ADAM_WD = 0.01
ADAM_STEP = 10

VMEM_LIMIT_BYTES = 56 * 1024 * 1024
SMALL_ROWS = 104
SMALL_ADAM_ROWS = 80


def _params(sem):
    return pltpu.CompilerParams(dimension_semantics=sem, vmem_limit_bytes=VMEM_LIMIT_BYTES)


def _resident(shape):
    return pl.BlockSpec(tuple(shape), lambda i: (0,) * len(shape), pipeline_mode=pl.Buffered(1))


def _row_tile(rows, target):
    best = 16
    for t in range(16, min(rows, target) + 1, 16):
        if rows % t == 0:
            best = t
    return best


def _col_tile(cols, target):
    best = cols
    for t in range(128, min(cols, target) + 1, 128):
        if cols % t == 0:
            best = t
    return best


def _dot(a, b):
    return jnp.dot(a, b, preferred_element_type=F32)


def _dot_nt(a, b):
    return lax.dot_general(a, b, (((1,), (1,)), ((), ())), preferred_element_type=F32)


def _dot_tn(a, b):
    return lax.dot_general(a, b, (((0,), (0,)), ((), ())), preferred_element_type=F32)


def _rms(x, n):
    return lax.rsqrt(jnp.sum(x * x, axis=-1, keepdims=True) * (1.0 / n) + EPS)


def _rms_bwd(dn, nrm, r, n):
    return r * (dn - nrm * (jnp.sum(dn * nrm, axis=-1, keepdims=True) * (1.0 / n)))


def _gelu(x):
    k = math.sqrt(2.0 / math.pi)
    t = jnp.tanh(k * (x + 0.044715 * x * x * x))
    return 0.5 * x * (1.0 + t), t


def _gelu_grad(x, t):
    k = math.sqrt(2.0 / math.pi)
    return 0.5 * (1.0 + t) + 0.5 * x * (1.0 - t * t) * k * (1.0 + 3.0 * 0.044715 * x * x)


def _sigmoid(x):
    return 0.5 + 0.5 * jnp.tanh(0.5 * x)


def _softplus_neg(lam):
    e = jnp.exp(-jnp.abs(lam))
    log1p = jnp.where(e < 0.01, e * (1.0 - e * (0.5 - e * (1.0 / 3 - e * 0.25))), jnp.log(1.0 + e))
    return jnp.maximum(-lam, 0.0) + log1p


def _rope(t, c, s1, s2):
    return t * c + pltpu.roll(t, 96, 1) * s1 + pltpu.roll(t, 32, 1) * s2


def _rope_t(d, c, s1, s2):
    return d * c + pltpu.roll(d * s1, 32, 1) + pltpu.roll(d * s2, 96, 1)


def _rope_tables(lp):
    pos = (np.arange(lp, dtype=np.int32) - PAD_ROWS).astype(np.float32)
    inv_freq = (ROPE_THETA ** (-np.arange(0, D_ROPE // 2, dtype=np.float32) / (D_ROPE // 2))).astype(np.float32)
    ang = (pos[:, None] * inv_freq[None, :]).astype(np.float32).astype(np.float64)
    cos, sin = np.cos(ang).astype(np.float32), np.sin(ang).astype(np.float32)
    z = np.zeros_like(cos)
    return (jnp.asarray(np.concatenate([cos, cos, z, z], 1)), jnp.asarray(np.concatenate([-sin, z, z, z], 1)),
            jnp.asarray(np.concatenate([z, sin, z, z], 1)))


def _rmsnorm_call(name, h, g, tm):
    rows, d = h.shape

    def body(h_ref, g_ref, o_ref):
        x = h_ref[...]
        o_ref[...] = (x * _rms(x, d) * g_ref[...]).astype(BF16)

    return pl.pallas_call(
        body, name=name, grid=(rows // tm,),
        in_specs=[pl.BlockSpec((tm, d), lambda i: (i, 0)), pl.BlockSpec((1, d), lambda i: (0, 0))],
        out_specs=pl.BlockSpec((tm, d), lambda i: (i, 0)),
        out_shape=jax.ShapeDtypeStruct((rows, d), BF16),
        compiler_params=_params(("parallel",)))(h, g)


def _ffn_up_call(name, u, wg, wu, tm, comm=None):
    rows, d = u.shape
    ns, fs, _ = wg.shape

    def body(u_ref, wg_ref, wu_ref, g_ref, p_ref, a_ref):
        uu = u_ref[...]
        g = _dot_nt(uu, wg_ref[0])
        p = _dot_nt(uu, wu_ref[0])
        g_ref[0] = g.astype(BF16)
        p_ref[0] = p.astype(BF16)
        a_ref[0] = (g * jax.nn.sigmoid(g) * p).astype(BF16)

    wspec = pl.BlockSpec((1, fs, d), lambda s, i: (s, 0, 0))
    ospec = pl.BlockSpec((1, tm, fs), lambda s, i: (s, i, 0))
    oshape = jax.ShapeDtypeStruct((ns, rows, fs), BF16)
    return _hosted_call(
        body, name=name, grid=(ns, rows // tm),
        in_specs=[pl.BlockSpec((tm, d), lambda s, i: (i, 0)), wspec, wspec],
        out_specs=[ospec, ospec, ospec], out_shape=[oshape, oshape, oshape],
        dims=("parallel", "parallel"), args=(u, wg, wu), comm=comm)


def _ffn_gate_call(name, u, wg, tm, comm=None):
    rows, d = u.shape
    ns, fs, _ = wg.shape

    def body(u_ref, wg_ref, g_ref):
        g_ref[0] = _dot_nt(u_ref[...], wg_ref[0]).astype(BF16)

    return _hosted_call(
        body, name=name, grid=(ns, rows // tm),
        in_specs=[pl.BlockSpec((tm, d), lambda s, i: (i, 0)), pl.BlockSpec((1, fs, d), lambda s, i: (s, 0, 0))],
        out_specs=[pl.BlockSpec((1, tm, fs), lambda s, i: (s, i, 0))],
        out_shape=[jax.ShapeDtypeStruct((ns, rows, fs), BF16)],
        dims=("parallel", "parallel"), args=(u, wg), comm=comm)[0]


def _ffn_upact_call(name, u, wu, gate, tm, comm=None):
    rows, d = u.shape
    ns, fs, _ = wu.shape

    def body(u_ref, wu_ref, g_ref, p_ref, a_ref):
        p = _dot_nt(u_ref[...], wu_ref[0])
        g = g_ref[0].astype(F32)
        p_ref[0] = p.astype(BF16)
        a_ref[0] = (g * jax.nn.sigmoid(g) * p).astype(BF16)

    ospec = pl.BlockSpec((1, tm, fs), lambda s, i: (s, i, 0))
    oshape = jax.ShapeDtypeStruct((ns, rows, fs), BF16)
    return _hosted_call(
        body, name=name, grid=(ns, rows // tm),
        in_specs=[pl.BlockSpec((tm, d), lambda s, i: (i, 0)), pl.BlockSpec((1, fs, d), lambda s, i: (s, 0, 0)), ospec],
        out_specs=[ospec, ospec], out_shape=[oshape, oshape],
        dims=("parallel", "parallel"), args=(u, wu, gate), comm=comm)


def _ffn_down_call(name, a, wd, h, tm, comm=None):
    rows, d = h.shape
    ns, _, fs = a.shape

    def body(a_ref, wd_ref, h_ref, o_ref):
        acc = h_ref[...]
        for s in range(ns):
            acc = acc + 0.5 * _dot(a_ref[s], wd_ref[s])
        o_ref[...] = acc

    return _hosted_call(
        body, name=name, grid=(rows // tm,),
        in_specs=[pl.BlockSpec((ns, tm, fs), lambda i: (0, i, 0)),
                  _resident((ns, fs, d)),
                  pl.BlockSpec((tm, d), lambda i: (i, 0))],
        out_specs=[pl.BlockSpec((tm, d), lambda i: (i, 0))],
        out_shape=[jax.ShapeDtypeStruct((rows, d), F32)],
        dims=("parallel",), args=(a, wd, h), comm=comm)[0]


def _mm_call(name, a, bt, tm, out_dtype):
    rows, k = a.shape
    n = bt.shape[0]

    def body(a_ref, b_ref, o_ref):
        o_ref[...] = _dot_nt(a_ref[...], b_ref[...]).astype(out_dtype)

    return pl.pallas_call(
        body, name=name, grid=(rows // tm,),
        in_specs=[pl.BlockSpec((tm, k), lambda i: (i, 0)), pl.BlockSpec((n, k), lambda i: (0, 0))],
        out_specs=pl.BlockSpec((tm, n), lambda i: (i, 0)),
        out_shape=jax.ShapeDtypeStruct((rows, n), out_dtype),
        compiler_params=_params(("parallel",)))(a, bt)


def _mla_heads(z, gql, gkvl, wuq, wuk, wuv):
    cq = z[:, 0:Q_RANK]
    ckv = z[:, Q_RANK:Z_KR]
    kr = z[:, Z_KR:Z_MLA]
    rq = _rms(cq, Q_RANK)
    nq = cq * rq
    cqn = (nq * gql).astype(BF16)
    rkv = _rms(ckv, KV_RANK)
    nkv = ckv * rkv
    ckvn = (nkv * gkvl).astype(BF16)
    qraw = _dot_nt(cqn, wuq)
    knope = _dot(ckvn, wuk)
    v = _dot(ckvn, wuv)
    skr = jnp.sum(kr * kr, axis=-1, keepdims=True)
    heads = []
    for hd in range(HEADS):
        qh = qraw[:, hd * D_QKP:(hd + 1) * D_QKP]
        rqh = lax.rsqrt(jnp.sum(qh * qh, axis=-1, keepdims=True) * (1.0 / D_QK) + EPS)
        kn = knope[:, hd * D_NOPE:(hd + 1) * D_NOPE]
        rkh = lax.rsqrt((jnp.sum(kn * kn, axis=-1, keepdims=True) + skr) * (1.0 / D_QK) + EPS)
        heads.append((qh * rqh, rqh, kn * rkh, kr * rkh, rkh))
    return dict(rq=rq, nq=nq, cqn=cqn, rkv=rkv, nkv=nkv, ckvn=ckvn, v=v, heads=heads)


def _mla_prep_call(z, gql, gkvl, gqh, gkh, wuq, wuk, wuv, tabs, lp, tm):
    rows = z.shape[0]
    tpe = lp // tm

    def body(z_ref, gql_ref, gkvl_ref, gqh_ref, gkh_ref, wuq_ref, wuk_ref, wuv_ref, c_ref, s1_ref, s2_ref,
             q_ref, k_ref, v_ref, cqn_ref, ckvn_ref):
        m = _mla_heads(z_ref[...], gql_ref[...], gkvl_ref[...], wuq_ref[...], wuk_ref[...], wuv_ref[...])
        c, s1, s2 = c_ref[...], s1_ref[...], s2_ref[...]
        gq, gk = gqh_ref[...], gkh_ref[...]
        row = (pl.program_id(0) % tpe) * tm + lax.broadcasted_iota(jnp.int32, (tm, 1), 0)
        spare = (lax.broadcasted_iota(jnp.int32, (1, D_QKP - D_NOPE), 1) == D_ROPE).astype(F32)
        kmask = jnp.where(row < PAD_ROWS, NEG_INF * math.sqrt(D_QK), 0.0) * spare
        for hd in range(HEADS):
            qn, _, knn, krn, _ = m["heads"][hd]
            qg = qn * gq
            q_ref[hd, :, 0:D_NOPE] = qg[:, 0:D_NOPE].astype(BF16)
            q_ref[hd, :, D_NOPE:D_QKP] = (_rope(qg[:, D_NOPE:D_QKP], c, s1, s2) + spare).astype(BF16)
            k_ref[hd, :, 0:D_NOPE] = (knn * gk[:, 0:D_NOPE]).astype(BF16)
            k_ref[hd, :, D_NOPE:D_QKP] = (_rope(krn * gk[:, D_NOPE:D_QKP], c, s1, s2) + kmask).astype(BF16)
            v_ref[hd] = m["v"][:, hd * D_V:(hd + 1) * D_V].astype(BF16)
        cqn_ref[...] = m["cqn"]
        ckvn_ref[...] = m["ckvn"]

    def const(shape):
        return pl.BlockSpec(shape, lambda i: tuple(0 for _ in shape))

    tab = pl.BlockSpec((tm, 128), lambda i: (i % tpe, 0))
    return pl.pallas_call(
        body, name="mla_prep", grid=(rows // tm,),
        in_specs=[pl.BlockSpec((tm, Z_MLA), lambda i: (i, 0)), const((1, Q_RANK)), const((1, KV_RANK)),
                  const((1, D_QKP)), const((1, D_QKP)), const((HEADS * D_QKP, Q_RANK)),
                  const((KV_RANK, HEADS * D_NOPE)), const((KV_RANK, HEADS * D_V)), tab, tab, tab],
        out_specs=[pl.BlockSpec((HEADS, tm, D_QKP), lambda i: (0, i, 0)),
                   pl.BlockSpec((HEADS, tm, D_QKP), lambda i: (0, i, 0)),
                   pl.BlockSpec((HEADS, tm, D_V), lambda i: (0, i, 0)),
                   pl.BlockSpec((tm, Q_RANK), lambda i: (i, 0)),
                   pl.BlockSpec((tm, KV_RANK), lambda i: (i, 0))],
        out_shape=[jax.ShapeDtypeStruct((HEADS, rows, D_QKP), BF16),
                   jax.ShapeDtypeStruct((HEADS, rows, D_QKP), BF16),
                   jax.ShapeDtypeStruct((HEADS, rows, D_V), BF16),
                   jax.ShapeDtypeStruct((rows, Q_RANK), BF16),
                   jax.ShapeDtypeStruct((rows, KV_RANK), BF16)],
        compiler_params=_params(("parallel",)))(z, gql, gkvl, gqh, gkh, wuq, wuk, wuv, *tabs)


Q_BLOCK_ROWS = 528


def _q_block(lp):
    return _row_tile(lp, Q_BLOCK_ROWS)


def _key_end(ext, lp):
    return min(lp, -(-ext // CHUNK) * CHUNK)


def _diag_bias(qb, j0, nk):
    shift = CHUNK.bit_length() - 1
    r = jnp.right_shift(j0 + lax.broadcasted_iota(jnp.int32, (qb, nk), 0), shift)
    c = jnp.right_shift(j0 + lax.broadcasted_iota(jnp.int32, (qb, nk), 1), shift)
    return jnp.where(c <= r, 0.0, NEG_INF)


def _attn_fwd_call(q, k, v, nb, lp, comm=None):
    rows = nb * lp
    qb = _q_block(lp)
    scale = 1.0 / math.sqrt(D_QK)

    def body(q_ref, k_ref, v_ref, o_ref, lse_ref):
        for j in range(lp // qb):
            j0, ext = j * qb, (j + 1) * qb
            kend = _key_end(ext, lp)
            qj = q_ref[0, j0:ext, :]
            sd = _dot_nt(qj, k_ref[0, j0:kend, :]) * scale + _diag_bias(qb, j0, kend - j0)
            mx = jnp.max(sd, axis=-1, keepdims=True)
            if j > 0:
                so = _dot_nt(qj, k_ref[0, 0:j0, :]) * scale
                mx = jnp.maximum(mx, jnp.max(so, axis=-1, keepdims=True))
            pd = jnp.exp(sd - mx)
            l = jnp.sum(pd, axis=-1, keepdims=True)
            o = _dot(pd.astype(BF16), v_ref[0, j0:kend, :])
            if j > 0:
                po = jnp.exp(so - mx)
                l = l + jnp.sum(po, axis=-1, keepdims=True)
                o = o + _dot(po.astype(BF16), v_ref[0, 0:j0, :])
            o_ref[j0:ext, :] = o / l
            lse_ref[0, j0:ext, :] = mx + jnp.log(l)

    return _hosted_call(
        body, name="attn_fwd", grid=(nb, HEADS),
        in_specs=[pl.BlockSpec((1, lp, D_QKP), lambda b, h: (h, b, 0)),
                  pl.BlockSpec((1, lp, D_QKP), lambda b, h: (h, b, 0)),
                  pl.BlockSpec((1, lp, D_V), lambda b, h: (h, b, 0))],
        out_specs=[pl.BlockSpec((lp, D_V), lambda b, h: (b, h)),
                   pl.BlockSpec((1, lp, 1), lambda b, h: (h, b, 0))],
        out_shape=[jax.ShapeDtypeStruct((rows, MLA_W), F32),
                   jax.ShapeDtypeStruct((HEADS, rows, 1), F32)],
        dims=("parallel", "parallel"), args=(q, k, v), comm=comm)


def _attn_bwd_call(q, k, v, o, lse, do, nb, lp, comm=None):
    rows = nb * lp
    qb = _q_block(lp)
    scale = 1.0 / math.sqrt(D_QK)

    def body(q_ref, k_ref, v_ref, o_ref, lse_ref, do_ref, dq_ref, dk_ref, dv_ref, dk_acc, dv_acc):
        dk_acc[...] = jnp.zeros_like(dk_acc)
        dv_acc[...] = jnp.zeros_like(dv_acc)
        for j in range(lp // qb):
            j0, ext = j * qb, (j + 1) * qb
            kend = _key_end(ext, lp)
            dbias = _diag_bias(qb, j0, kend - j0)
            qj = q_ref[0, j0:ext, :]
            doj = do_ref[j0:ext, :]
            delta = jnp.sum(doj * o_ref[j0:ext, :], axis=-1, keepdims=True)
            dob = doj.astype(BF16)
            lse = lse_ref[0, j0:ext, :]
            dq = jnp.zeros((qb, D_QKP), F32)
            for lo, hi, bias in ((j0, kend, dbias), (0, j0, None)):
                if hi == lo:
                    continue
                kk = k_ref[0, lo:hi, :]
                s = _dot_nt(qj, kk) * scale
                p = jnp.exp((s if bias is None else s + bias) - lse)
                dv_acc[lo:hi, :] += _dot_tn(p.astype(BF16), dob)
                dp = _dot_nt(dob, v_ref[0, lo:hi, :])
                ds = (p * (dp - delta) * scale).astype(BF16)
                dq = dq + _dot(ds, kk)
                dk_acc[lo:hi, :] += _dot_tn(ds, qj)
            dq_ref[0, j0:ext, :] = dq.astype(BF16)
        dk_ref[0] = dk_acc[...].astype(BF16)
        dv_ref[0] = dv_acc[...].astype(BF16)

    qspec = pl.BlockSpec((1, lp, D_QKP), lambda b, h: (h, b, 0))
    vspec = pl.BlockSpec((1, lp, D_V), lambda b, h: (h, b, 0))
    ospec = pl.BlockSpec((lp, D_V), lambda b, h: (b, h))
    return _hosted_call(
        body, name="attn_bwd", grid=(nb, HEADS),
        in_specs=[qspec, qspec, vspec, ospec, pl.BlockSpec((1, lp, 1), lambda b, h: (h, b, 0)), ospec],
        out_specs=[qspec, qspec, vspec],
        out_shape=[jax.ShapeDtypeStruct((HEADS, rows, D_QKP), BF16),
                   jax.ShapeDtypeStruct((HEADS, rows, D_QKP), BF16),
                   jax.ShapeDtypeStruct((HEADS, rows, D_V), BF16)],
        scratch_shapes=[pltpu.VMEM((lp, D_QKP), F32), pltpu.VMEM((lp, D_V), F32)],
        dims=("parallel", "parallel"), args=(q, k, v, o, lse, do), comm=comm)


def _lru_gates(u, cw, cb, wa, ba, wx, bx, lam, lp):
    xc = (cw[3:4, :] * u + cw[2:3, :] * pltpu.roll(u, 1, 0) + cw[1:2, :] * pltpu.roll(u, 2, 0)
          + cw[0:1, :] * pltpu.roll(u, 3, 0) + cb)
    xcb = xc.astype(BF16)
    r = _sigmoid(_dot(xcb, wa) + ba)
    i = _sigmoid(_dot(xcb, wx) + bx)
    sp = _softplus_neg(lam)
    la = -C_RGLRU * r * sp
    a = jnp.exp(la)
    x2 = 2.0 * la
    e2 = a * a
    m2 = jnp.maximum(jnp.where(x2 > -0.01, -x2 * (1.0 + 0.5 * x2), 1.0 - e2), 1e-30)
    rs = lax.rsqrt(m2)
    row = lax.broadcasted_iota(jnp.int32, (lp, LRU_TILE), 0)
    first = row == PAD_ROWS
    valid = row >= PAD_ROWS
    mult_eff = jnp.where(first, 1.0, m2 * rs)
    return dict(xc=xc, xcb=xcb, r=r, i=i, sp=sp, a=a, e2=e2, rs=rs, mult_eff=mult_eff, first=first, valid=valid)


def _scan_rows(a, b, a_s, b_s, out_ref, lp, reverse):
    sub = lax.broadcasted_iota(jnp.int32, (lp, LRU_TILE), 0) & 7
    for dist in (1, 2, 4):
        shift = lp - dist if reverse else dist
        keep = (sub + dist <= 7) if reverse else (sub >= dist)
        a_sh = pltpu.roll(a, shift, 0)
        b_sh = pltpu.roll(b, shift, 0)
        b = jnp.where(keep, a * b_sh + b, b)
        a = jnp.where(keep, a * a_sh, a)
    a_s[...] = a
    b_s[...] = b
    n_groups = lp // 8
    edge = 0 if reverse else 7

    def group(gi, carry):
        r0 = pl.multiple_of(((n_groups - 1 - gi) if reverse else gi) * 8, 8)
        a8 = a_s[pl.ds(r0, 8), :]
        b8 = b_s[pl.ds(r0, 8), :]
        out_ref[pl.ds(r0, 8), :] = a8 * carry + b8
        return a8[edge:edge + 1, :] * carry + b8[edge:edge + 1, :]

    lax.fori_loop(0, n_groups, group, jnp.zeros((1, LRU_TILE), F32), unroll=4)


def _lru_specs(lp):
    seq = lambda col0: pl.BlockSpec((lp, LRU_TILE), lambda t, b: (b, col0 + t))
    cw = pl.BlockSpec((1, CONV_K, LRU_TILE), lambda t, b: (t, 0, 0))
    vec = pl.BlockSpec((1, LRU_TILE), lambda t, b: (0, t))
    mat = pl.BlockSpec((1, LRU_TILE, LRU_TILE), lambda t, b: (t, 0, 0))
    return seq, cw, vec, mat


def _lru_fwd_call(z, cw, cb, wa, ba, wx, bx, lam, nb, lp, comm=None):
    rows = nb * lp
    seq, cwspec, vec, mat = _lru_specs(lp)

    def body(u_ref, g_ref, cw_ref, cb_ref, wa_ref, ba_ref, wx_ref, bx_ref, lam_ref, y_ref, hs_ref, a_s, b_s):
        m = _lru_gates(u_ref[...], cw_ref[0], cb_ref[...], wa_ref[0], ba_ref[...], wx_ref[0], bx_ref[...],
                       lam_ref[...], lp)
        a = jnp.where(m["valid"], m["a"], 0.0)
        b = jnp.where(m["valid"], m["mult_eff"] * (m["i"] * m["xc"]), 0.0)
        _scan_rows(a, b, a_s, b_s, hs_ref, lp, reverse=False)
        gl, _ = _gelu(g_ref[...])
        y_ref[...] = hs_ref[...] * gl

    oshape = jax.ShapeDtypeStruct((rows, LRU_W), F32)
    return _hosted_call(
        body, name="lru_fwd", grid=(N_LRU_TILES, nb),
        in_specs=[seq(Z_U // LRU_TILE), seq(Z_G // LRU_TILE), cwspec, vec, mat, vec, mat, vec, vec],
        out_specs=[seq(0), seq(0)], out_shape=[oshape, oshape],
        scratch_shapes=[pltpu.VMEM((lp, LRU_TILE), F32), pltpu.VMEM((lp, LRU_TILE), F32)],
        dims=("parallel", "parallel"), args=(z, z, cw, cb, wa, ba, wx, bx, lam), comm=comm)


def _lru_bwd_call(z, hs, dy, cw, cb, wa, ba, wx, bx, lam, nb, lp, comm=None):
    rows = nb * lp
    seq, cwspec, vec, mat = _lru_specs(lp)

    def body(u_ref, g_ref, hs_ref, dy_ref, cw_ref, cb_ref, wa_ref, ba_ref, wx_ref, bx_ref, lam_ref,
             du_ref, dg_ref, dcw_ref, dcb_ref, dwa_ref, dba_ref, dwx_ref, dbx_ref, dlam_ref, a_s, b_s, d_s):
        b_idx = pl.program_id(1)
        u = u_ref[...]
        cw = cw_ref[0]
        wa, wx = wa_ref[0], wx_ref[0]
        lam = lam_ref[...]
        m = _lru_gates(u, cw, cb_ref[...], wa, ba_ref[...], wx, bx_ref[...], lam, lp)
        gate = g_ref[...]
        gl, th = _gelu(gate)
        dy = dy_ref[...]
        hs = hs_ref[...]
        dg_ref[...] = (dy * hs * _gelu_grad(gate, th)).astype(BF16)
        a_eff = jnp.where(m["valid"], m["a"], 0.0)
        _scan_rows(pltpu.roll(a_eff, lp - 1, 0), dy * gl, a_s, b_s, d_s, lp, reverse=True)
        ds = d_s[...]
        xc, r, i = m["xc"], m["r"], m["i"]
        row = lax.broadcasted_iota(jnp.int32, (lp, LRU_TILE), 0)
        da = ds * jnp.where(row >= 1, pltpu.roll(hs, 1, 0), 0.0)
        db = jnp.where(m["valid"], ds, 0.0)
        di = db * m["mult_eff"] * xc
        dxc = db * m["mult_eff"] * i
        live = m["valid"] & jnp.logical_not(m["first"])
        dm = jnp.where(live, db * i * xc, 0.0)
        dla = da * m["a"] - dm * (m["e2"] * m["rs"])
        dr = dla * (-C_RGLRU * m["sp"])
        dsp = jnp.sum(dla * (-C_RGLRU * r), axis=0, keepdims=True)
        dpr = (dr * r * (1.0 - r))
        dpi = (di * i * (1.0 - i))
        dprb, dpib = dpr.astype(BF16), dpi.astype(BF16)
        dxc = dxc + _dot_nt(dprb, wa) + _dot_nt(dpib, wx)
        du = (cw[3:4, :] * dxc + cw[2:3, :] * pltpu.roll(dxc, lp - 1, 0) + cw[1:2, :] * pltpu.roll(dxc, lp - 2, 0)
              + cw[0:1, :] * pltpu.roll(dxc, lp - 3, 0))
        du_ref[...] = jnp.where(m["valid"], du, 0.0).astype(BF16)
        tap = lax.broadcasted_iota(jnp.int32, (CONV_K, LRU_TILE), 0)
        dcw = jnp.zeros((CONV_K, LRU_TILE), F32)
        for kk in range(CONV_K):
            shifted = u if kk == CONV_K - 1 else pltpu.roll(u, CONV_K - 1 - kk, 0)
            dcw = jnp.where(tap == kk, jnp.sum(dxc * shifted, axis=0, keepdims=True), dcw)
        parts = [(dcw_ref, dcw[None]), (dcb_ref, jnp.sum(dxc, axis=0, keepdims=True)[None]),
                 (dwa_ref, _dot_tn(m["xcb"], dprb)[None]), (dba_ref, jnp.sum(dpr, axis=0, keepdims=True)[None]),
                 (dwx_ref, _dot_tn(m["xcb"], dpib)[None]), (dbx_ref, jnp.sum(dpi, axis=0, keepdims=True)[None]),
                 (dlam_ref, (dsp * (-jax.nn.sigmoid(-lam)))[None])]

        @pl.when(b_idx == 0)
        def _():
            for ref, val in parts:
                ref[...] = val

        @pl.when(b_idx != 0)
        def _():
            for ref, val in parts:
                ref[...] += val

    bshape = jax.ShapeDtypeStruct((rows, LRU_W), BF16)
    vec3 = pl.BlockSpec((1, 1, LRU_TILE), lambda t, b: (t, 0, 0))
    vshape = jax.ShapeDtypeStruct((N_LRU_TILES, 1, LRU_TILE), F32)
    mshape = jax.ShapeDtypeStruct((N_LRU_TILES, LRU_TILE, LRU_TILE), F32)
    return _hosted_call(
        body, name="lru_bwd", grid=(N_LRU_TILES, nb),
        in_specs=[seq(Z_U // LRU_TILE), seq(Z_G // LRU_TILE), seq(0), seq(0), cwspec, vec, mat, vec, mat, vec, vec],
        out_specs=[seq(0), seq(0), cwspec, vec3, mat, vec3, mat, vec3, vec3],
        out_shape=[bshape, bshape, jax.ShapeDtypeStruct((N_LRU_TILES, CONV_K, LRU_TILE), F32), vshape, mshape,
                   vshape, mshape, vshape, vshape],
        scratch_shapes=[pltpu.VMEM((lp, LRU_TILE), F32)] * 3,
        dims=("parallel", "arbitrary"), args=(z, z, hs, dy, cw, cb, wa, ba, wx, bx, lam), comm=comm)


def _mix_out_call(ya, yl, ga, gl, wout, h, tm):
    rows, d = h.shape

    def body(ya_ref, yl_ref, ga_ref, gl_ref, w_ref, h_ref, y_ref, o_ref):
        a = ya_ref[...]
        l = yl_ref[...]
        an = (a * _rms(a, MLA_W) * ga_ref[...]).astype(BF16)
        ln = (l * _rms(l, LRU_W) * gl_ref[...]).astype(BF16)
        y_ref[:, 0:MLA_W] = an
        y_ref[:, MLA_W:MLA_W + LRU_W] = ln
        o_ref[...] = h_ref[...] + _dot(an, w_ref[0:MLA_W, :]) + _dot(ln, w_ref[MLA_W:MLA_W + LRU_W, :])

    half = pl.BlockSpec((tm, MLA_W), lambda i: (i, 0))
    g = pl.BlockSpec((1, MLA_W), lambda i: (0, 0))
    full = pl.BlockSpec((tm, d), lambda i: (i, 0))
    return pl.pallas_call(
        body, name="mix_out", grid=(rows // tm,),
        in_specs=[half, half, g, g, pl.BlockSpec((MLA_W + LRU_W, d), lambda i: (0, 0)), full],
        out_specs=[full, full],
        out_shape=[jax.ShapeDtypeStruct((rows, MLA_W + LRU_W), BF16), jax.ShapeDtypeStruct((rows, d), F32)],
        compiler_params=_params(("parallel",)))(ya, yl, ga, gl, wout, h)


def _mix_out_bwd_call(dhb, wout, ya, yl, ga, gl, tm):
    rows = ya.shape[0]
    d = dhb.shape[1]

    def body(dh_ref, w_ref, ya_ref, yl_ref, ga_ref, gl_ref, dya_ref, dyl_ref, dga_ref, dgl_ref):
        dy = _dot_nt(dh_ref[...], w_ref[...])
        outs = []
        for val, g_ref, lo, out_ref in ((ya_ref[...], ga_ref, 0, dya_ref), (yl_ref[...], gl_ref, MLA_W, dyl_ref)):
            r = _rms(val, MLA_W)
            n = val * r
            dyn = dy[:, lo:lo + MLA_W]
            out_ref[...] = _rms_bwd(dyn * g_ref[...], n, r, MLA_W)
            outs.append(jnp.sum(dyn * n, axis=0, keepdims=True))

        @pl.when(pl.program_id(0) == 0)
        def _():
            dga_ref[...] = outs[0]
            dgl_ref[...] = outs[1]

        @pl.when(pl.program_id(0) != 0)
        def _():
            dga_ref[...] += outs[0]
            dgl_ref[...] += outs[1]

    half = pl.BlockSpec((tm, MLA_W), lambda i: (i, 0))
    g = pl.BlockSpec((1, MLA_W), lambda i: (0, 0))
    return pl.pallas_call(
        body, name="mix_out_bwd", grid=(rows // tm,),
        in_specs=[pl.BlockSpec((tm, d), lambda i: (i, 0)), pl.BlockSpec((MLA_W + LRU_W, d), lambda i: (0, 0)),
                  half, half, g, g],
        out_specs=[half, half, g, g],
        out_shape=[jax.ShapeDtypeStruct((rows, MLA_W), F32), jax.ShapeDtypeStruct((rows, LRU_W), F32),
                   jax.ShapeDtypeStruct((1, MLA_W), F32), jax.ShapeDtypeStruct((1, LRU_W), F32)],
        compiler_params=_params(("arbitrary",)))(dhb, wout, ya, yl, ga, gl)


def _final_call(h, g, target, lp, tm):
    rows, d = h.shape
    tpe = lp // tm

    def body(h_ref, g_ref, t_ref, dh_ref, dhb_ref, dg_ref, loss_ref):
        i = pl.program_id(0)
        x = h_ref[...]
        g = g_ref[...]
        r = _rms(x, d)
        n = x * r
        row = (i % tpe) * tm + lax.broadcasted_iota(jnp.int32, (tm, 1), 0)
        err = jnp.where(row >= CHUNK, n * g - t_ref[...], 0.0)
        dout = err * (1.0 / d)
        dh = _rms_bwd(dout * g, n, r, d)
        dh_ref[...] = dh
        dhb_ref[...] = dh.astype(BF16)
        dg = jnp.sum(dout * n, axis=0, keepdims=True)
        part = jnp.sum(jnp.sum(err * err, axis=1, keepdims=True), axis=0, keepdims=True) * (0.5 / d)
        loss = jnp.broadcast_to(part, (1, 128))

        @pl.when(i == 0)
        def _():
            dg_ref[...] = dg
            loss_ref[...] = loss

        @pl.when(i != 0)
        def _():
            dg_ref[...] += dg
            loss_ref[...] += loss

    full = pl.BlockSpec((tm, d), lambda i: (i, 0))
    return pl.pallas_call(
        body, name="final_loss", grid=(rows // tm,),
        in_specs=[full, pl.BlockSpec((1, d), lambda i: (0, 0)), full],
        out_specs=[full, full, pl.BlockSpec((1, d), lambda i: (0, 0)), pl.BlockSpec((1, 128), lambda i: (0, 0))],
        out_shape=[jax.ShapeDtypeStruct((rows, d), F32), jax.ShapeDtypeStruct((rows, d), BF16),
                   jax.ShapeDtypeStruct((1, d), F32), jax.ShapeDtypeStruct((1, 128), F32)],
        compiler_params=_params(("arbitrary",)))(h, g, target)


def _ffn_dact_call(name, dhb, wd, gate, up, tm, comm=None):
    rows, d = dhb.shape
    ns, fs, _ = wd.shape

    nsub = 2 if tm % 32 == 0 else 1
    sub = tm // nsub

    def body(dh_ref, wd_ref, g_ref, p_ref, dg_ref, dp_ref):
        wd = wd_ref[0]
        for r in range(nsub):
            rs = slice(r * sub, (r + 1) * sub)
            da = (0.5 * _dot_nt(dh_ref[rs, :], wd)).astype(BF16)
            g = g_ref[0, rs, :]
            p = p_ref[0, rs, :]
            sg = jax.nn.sigmoid(g)
            dg_ref[0, rs, :] = (da * p) * (sg * (1.0 + g * (1.0 - sg)))
            dp_ref[0, rs, :] = da * (g * sg)

    aspec = pl.BlockSpec((1, tm, fs), lambda s, i: (s, i, 0))
    oshape = jax.ShapeDtypeStruct((ns, rows, fs), BF16)
    return _hosted_call(
        body, name=name, grid=(ns, rows // tm),
        in_specs=[pl.BlockSpec((tm, d), lambda s, i: (i, 0)), pl.BlockSpec((1, fs, d), lambda s, i: (s, 0, 0)),
                  aspec, aspec],
        out_specs=[aspec, aspec], out_shape=[oshape, oshape],
        dims=("parallel", "parallel"), args=(dhb, wd, gate, up), comm=comm)


def _norm_in_bwd_call(name, pieces, h, g, dres, tm, comm=None):
    rows, d = h.shape
    npc = len(pieces)

    def body(*refs):
        d_refs = refs[0:2 * npc:2]
        w_refs = refs[1:2 * npc:2]
        h_ref, g_ref, dres_ref, dh_ref, dhb_ref, dg_ref = refs[2 * npc:]
        du = jnp.zeros((tm, d), F32)
        for d_ref, w_ref in zip(d_refs, w_refs):
            if len(d_ref.shape) == 3:
                for s in range(d_ref.shape[0]):
                    du = du + _dot(d_ref[s], w_ref[s])
            else:
                du = du + _dot(d_ref[...], w_ref[...])
        x = h_ref[...]
        r = _rms(x, d)
        n = x * r
        dh = dres_ref[...] + _rms_bwd(du * g_ref[...], n, r, d)
        dh_ref[...] = dh
        dhb_ref[...] = dh.astype(BF16)
        dg = jnp.sum(du * n, axis=0, keepdims=True)

        @pl.when(pl.program_id(0) == 0)
        def _():
            dg_ref[...] = dg

        @pl.when(pl.program_id(0) != 0)
        def _():
            dg_ref[...] += dg

    in_specs, args = [], []
    for dd, w in pieces:
        if dd.ndim == 3:
            in_specs.append(pl.BlockSpec((dd.shape[0], tm, dd.shape[2]), lambda i: (0, i, 0)))
            in_specs.append(_resident(w.shape))
        else:
            in_specs.append(pl.BlockSpec((tm, dd.shape[1]), lambda i: (i, 0)))
            in_specs.append(_resident(w.shape))
        args += [dd, w]
    full = pl.BlockSpec((tm, d), lambda i: (i, 0))
    gspec = pl.BlockSpec((1, d), lambda i: (0, 0))
    return _hosted_call(
        body, name=name, grid=(rows // tm,),
        in_specs=in_specs + [full, gspec, full],
        out_specs=[full, full, gspec],
        out_shape=[jax.ShapeDtypeStruct((rows, d), F32), jax.ShapeDtypeStruct((rows, d), BF16),
                   jax.ShapeDtypeStruct((1, d), F32)],
        args=(*args, h, g, dres), comm=comm)


def _wgrad_call(name, a, b, scale=1.0, comm=None):
    a3, b3 = a.ndim == 3, b.ndim == 3
    ns = a.shape[0] if a3 else (b.shape[0] if b3 else 1)
    rows, m = a.shape[-2:]
    n = b.shape[-1]
    tmm = m if a3 else _col_tile(m, 256)

    def body(a_ref, b_ref, o_ref):
        av = a_ref[0] if a3 else a_ref[...]
        bv = b_ref[0] if b3 else b_ref[...]
        res = _dot_tn(av, bv)
        if scale != 1.0:
            res = res * scale
        if a3 or b3:
            o_ref[0] = res
        else:
            o_ref[...] = res

    aspec = (pl.BlockSpec((1, rows, tmm), lambda s, j: (s, 0, j)) if a3
             else pl.BlockSpec((rows, tmm), lambda s, j: (0, j)))
    bspec = (pl.BlockSpec((1, rows, n), lambda s, j: (s, 0, 0)) if b3
             else pl.BlockSpec((rows, n), lambda s, j: (0, 0)))
    if a3 or b3:
        ospec = pl.BlockSpec((1, tmm, n), lambda s, j: (s, j, 0))
        oshape = jax.ShapeDtypeStruct((ns, m, n), F32)
    else:
        ospec = pl.BlockSpec((tmm, n), lambda s, j: (j, 0))
        oshape = jax.ShapeDtypeStruct((m, n), F32)
    return _hosted_call(
        body, name=name, grid=(ns, m // tmm), in_specs=[aspec, bspec], out_specs=[ospec], out_shape=[oshape],
        dims=("parallel", "parallel"), args=(a, b), comm=comm)[0]


def _mla_prep_bwd_call(z, dq, dk, dv, gql, gkvl, gqh, gkh, wuq, wuk, wuv, tabs, lp, tm, comm=None):
    rows = z.shape[0]
    tpe = lp // tm

    def body(z_ref, dq_ref, dk_ref, dv_ref, gql_ref, gkvl_ref, gqh_ref, gkh_ref, wuq_ref, wuk_ref, wuv_ref,
             c_ref, s1_ref, s2_ref, dz_ref, dqp_ref, dkn_ref, dvv_ref, dgql_ref, dgkvl_ref, dgqh_ref, dgkh_ref):
        gql, gkvl = gql_ref[...], gkvl_ref[...]
        gq, gk = gqh_ref[...], gkh_ref[...]
        wuq, wuk, wuv = wuq_ref[...], wuk_ref[...], wuv_ref[...]
        m = _mla_heads(z_ref[...], gql, gkvl, wuq, wuk, wuv)
        c, s1, s2 = c_ref[...], s1_ref[...], s2_ref[...]
        dgq = jnp.zeros((1, D_QKP), F32)
        dgk = jnp.zeros((1, D_QKP), F32)
        dkr = jnp.zeros((tm, D_QKP - D_NOPE), F32)
        for hd in range(HEADS):
            qn, rqh, knn, krn, rkh = m["heads"][hd]
            dqg = jnp.concatenate([dq_ref[hd, :, 0:D_NOPE].astype(F32),
                                   _rope_t(dq_ref[hd, :, D_NOPE:D_QKP].astype(F32), c, s1, s2)], axis=1)
            dgq = dgq + jnp.sum(dqg * qn, axis=0, keepdims=True)
            dqn = dqg * gq
            dqr = rqh * (dqn - qn * (jnp.sum(dqn * qn, axis=-1, keepdims=True) * (1.0 / D_QK)))
            dqp_ref[:, hd * D_QKP:(hd + 1) * D_QKP] = dqr.astype(BF16)
            kn_full = jnp.concatenate([knn, krn], axis=1)
            dkg = jnp.concatenate([dk_ref[hd, :, 0:D_NOPE].astype(F32),
                                   _rope_t(dk_ref[hd, :, D_NOPE:D_QKP].astype(F32), c, s1, s2)], axis=1)
            dgk = dgk + jnp.sum(dkg * kn_full, axis=0, keepdims=True)
            dkn = dkg * gk
            dkraw = rkh * (dkn - kn_full * (jnp.sum(dkn * kn_full, axis=-1, keepdims=True) * (1.0 / D_QK)))
            dkn_ref[:, hd * D_NOPE:(hd + 1) * D_NOPE] = dkraw[:, 0:D_NOPE].astype(BF16)
            dkr = dkr + dkraw[:, D_NOPE:D_QKP]
            dvv_ref[:, hd * D_V:(hd + 1) * D_V] = dv_ref[hd]
        dcqn = _dot(dqp_ref[...], wuq)
        dckvn = _dot_nt(dkn_ref[...], wuk) + _dot_nt(dvv_ref[...], wuv)
        dz_ref[:, 0:Q_RANK] = _rms_bwd(dcqn * gql, m["nq"], m["rq"], Q_RANK).astype(BF16)
        dz_ref[:, Q_RANK:Z_KR] = _rms_bwd(dckvn * gkvl, m["nkv"], m["rkv"], KV_RANK).astype(BF16)
        dz_ref[:, Z_KR:Z_MLA] = dkr.astype(BF16)
        parts = [(dgql_ref, jnp.sum(dcqn * m["nq"], axis=0, keepdims=True)),
                 (dgkvl_ref, jnp.sum(dckvn * m["nkv"], axis=0, keepdims=True)), (dgqh_ref, dgq), (dgkh_ref, dgk)]

        @pl.when(pl.program_id(0) == 0)
        def _():
            for ref, val in parts:
                ref[...] = val

        @pl.when(pl.program_id(0) != 0)
        def _():
            for ref, val in parts:
                ref[...] += val

    def const(shape):
        return pl.BlockSpec(shape, lambda i: tuple(0 for _ in shape))

    tab = pl.BlockSpec((tm, 128), lambda i: (i % tpe, 0))
    hq = pl.BlockSpec((HEADS, tm, D_QKP), lambda i: (0, i, 0))
    hv = pl.BlockSpec((HEADS, tm, D_V), lambda i: (0, i, 0))

    def rowspec(n):
        return pl.BlockSpec((tm, n), lambda i: (i, 0))

    return _hosted_call(
        body, name="mla_prep_bwd", grid=(rows // tm,),
        in_specs=[rowspec(Z_MLA), hq, hq, hv, const((1, Q_RANK)), const((1, KV_RANK)), const((1, D_QKP)),
                  const((1, D_QKP)), const((HEADS * D_QKP, Q_RANK)), const((KV_RANK, HEADS * D_NOPE)),
                  const((KV_RANK, HEADS * D_V)), tab, tab, tab],
        out_specs=[rowspec(Z_MLA), rowspec(HEADS * D_QKP), rowspec(HEADS * D_NOPE), rowspec(HEADS * D_V),
                   const((1, Q_RANK)), const((1, KV_RANK)), const((1, D_QKP)), const((1, D_QKP))],
        out_shape=[jax.ShapeDtypeStruct((rows, Z_MLA), BF16), jax.ShapeDtypeStruct((rows, HEADS * D_QKP), BF16),
                   jax.ShapeDtypeStruct((rows, HEADS * D_NOPE), BF16), jax.ShapeDtypeStruct((rows, HEADS * D_V), BF16),
                   jax.ShapeDtypeStruct((1, Q_RANK), F32), jax.ShapeDtypeStruct((1, KV_RANK), F32),
                   jax.ShapeDtypeStruct((1, D_QKP), F32), jax.ShapeDtypeStruct((1, D_QKP), F32)],
        args=(z, dq, dk, dv, gql, gkvl, gqh, gkh, wuq, wuk, wuv, *tabs), comm=comm)


def _local_step(h0, target, w, nb, lp, sched=None):
    tm = _row_tile(nb * lp, 1408)
    te = _row_tile(lp, 512)
    tabs = _rope_tables(lp)
    g = {}
    if sched is None:
        host = lambda stage: None
    else:
        sched.g = g
        host = sched.host

    def ffn_fwd(tag, h, split):
        u = _rmsnorm_call(tag + "_norm", h, w[tag + "_norm"], te)
        if split:
            gate = _ffn_gate_call(tag + "_gate", u, w[tag + "_w_gate"], tm, host(tag + "_gate"))
            up, act = _ffn_upact_call(tag + "_upact", u, w[tag + "_w_up"], gate, tm, host(tag + "_upact"))
        else:
            gate, up, act = _ffn_up_call(tag + "_up", u, w[tag + "_w_gate"], w[tag + "_w_up"], tm, host(tag + "_up"))
        return _ffn_down_call(tag + "_down", act, w[tag + "_w_down"], h, te, host(tag + "_down")), (u, gate, up, act)

    def ffn_bwd(tag, h, saved, dh, dhb):
        u, gate, up, act = saved
        dgate, dup = _ffn_dact_call(tag + "_dact", dhb, w[tag + "_w_down"], gate, up, tm, host(tag + "_dact"))
        g[tag + "_w_down"] = _wgrad_call(tag + "_dwd", act, dhb, 0.5, host(tag + "_dwd"))
        g[tag + "_w_gate"] = _wgrad_call(tag + "_dwg", dgate, u, 1.0, host(tag + "_dwg"))
        g[tag + "_w_up"] = _wgrad_call(tag + "_dwu", dup, u, 1.0, host(tag + "_dwu"))
        dh_in, dhb_in, g[tag + "_norm"] = _norm_in_bwd_call(
            tag + "_din", [(dgate, w[tag + "_w_gate"]), (dup, w[tag + "_w_up"])], h, w[tag + "_norm"], dh, te,
            host(tag + "_din"))
        return dh_in, dhb_in

    h1, s1 = ffn_fwd("ffn1", h0, True)
    un = _rmsnorm_call("mix_norm", h1, w["mix_norm"], te)
    z = _mm_call("mix_in", un, w["w_in"], tm, F32)
    mla_w = (w["q_latent_norm"], w["kv_latent_norm"], w["q_head_norm"], w["k_head_norm"], w["w_uq"], w["w_uk"],
             w["w_uv"])
    q, k, v, cqn, ckvn = _mla_prep_call(z, *mla_w, tabs, lp, te)
    o, lse = _attn_fwd_call(q, k, v, nb, lp, host("attn_fwd"))
    lru_w = (w["conv_w"], w["conv_b"], w["gate_a_w"], w["gate_a_b"], w["gate_x_w"], w["gate_x_b"], w["lru_lambda"])
    yl, hs = _lru_fwd_call(z, *lru_w, nb, lp, host("lru_fwd"))
    y, h2 = _mix_out_call(o, yl, w["attn_out_norm"], w["lru_out_norm"], w["w_out"], h1, te)
    h3, s2 = ffn_fwd("ffn2", h2, False)
    dh3, dh3b, g["final_norm"], loss = _final_call(h3, w["final_norm"], target, lp, te)
    g["loss"] = loss

    dh2, dh2b = ffn_bwd("ffn2", h2, s2, dh3, dh3b)
    g["w_out"] = _wgrad_call("dw_out", y, dh2b)
    dya, dyl, g["attn_out_norm"], g["lru_out_norm"] = _mix_out_bwd_call(
        dh2b, w["w_out"], o, yl, w["attn_out_norm"], w["lru_out_norm"], te)
    dq, dk, dv = _attn_bwd_call(q, k, v, o, lse, dya, nb, lp, host("attn_bwd"))
    (dz_mla, dqp, dkn, dvv, g["q_latent_norm"], g["kv_latent_norm"], g["q_head_norm"],
     g["k_head_norm"]) = _mla_prep_bwd_call(z, dq, dk, dv, *mla_w, tabs, lp, te, host("mla_prep_bwd"))
    g["w_uq"] = _wgrad_call("dw_uq", dqp, cqn)
    g["w_uk"] = _wgrad_call("dw_uk", ckvn, dkn)
    g["w_uv"] = _wgrad_call("dw_uv", ckvn, dvv)
    (du, dgt, g["conv_w"], g["conv_b"], g["gate_a_w"], g["gate_a_b"], g["gate_x_w"], g["gate_x_b"],
     g["lru_lambda"]) = _lru_bwd_call(z, hs, dyl, *lru_w, nb, lp, host("lru_bwd"))
    win = w["w_in"]
    g["w_in"] = jnp.concatenate(
        [_wgrad_call("dw_in_mla", dz_mla, un), _wgrad_call("dw_in_u", du, un), _wgrad_call("dw_in_g", dgt, un)],
        axis=0)
    dh1, dh1b, g["mix_norm"] = _norm_in_bwd_call(
        "mix_din", [(dz_mla, win[0:Z_MLA]), (du, win[Z_U:Z_G]), (dgt, win[Z_G:Z_W])], h1, w["mix_norm"], dh2, te,
        host("mix_din"))
    dh0, _ = ffn_bwd("ffn1", h0, s1, dh1, dh1b)
    return loss, dh0, g


def _place():
    x, y, c = lax.axis_index("x"), lax.axis_index("y"), lax.axis_index("c")
    return x, y, c, [(1 - x, y), (x, 1 - y), (1 - x, 1 - y)]


def _any_specs(n):
    return [pl.BlockSpec(memory_space=pl.ANY)] * n


def _remote(src, dst, sems, k, dev):
    send_sems, recv_sems, base = sems
    return pltpu.make_async_remote_copy(src_ref=src, dst_ref=dst, send_sem=send_sems.at[base + k],
                                        recv_sem=recv_sems.at[base + k], device_id=dev, device_id_type=MESH)


EW_VMEM_BYTES = 24 * 1024 * 1024


def _fit_rows(rows, cols, blocks):
    return _row_tile(rows, max(16, int(EW_VMEM_BYTES // (8 * blocks)) // cols))


class _Geom:
    def __init__(self, n0, n1, blocks=1.0):
        self.n0, self.n1 = n0, n1
        self.axis = 0 if n0 % 32 == 0 else 1
        self.h0, self.h1 = (n0 // 2, n1) if self.axis == 0 else (n0, n1 // 2)
        self.tr = _fit_rows(self.h0, self.h1, blocks)
        self.nblk = self.h0 // self.tr

    def half_ref(self, ref, lead, idx):
        if self.axis == 0:
            return ref.at[(*lead, pl.ds(idx * self.h0, self.h0))]
        return ref.at[(*lead, slice(None), pl.ds(idx * self.h1, self.h1))]

    def half_block(self, lead, i, idx):
        return (*lead, idx * self.nblk + i, 0) if self.axis == 0 else (*lead, i, idx)


class _Comm:
    def __init__(self, ins, out_shapes, aliases, n_sems, start, finish, deliver):
        self.ins, self.out_shapes, self.aliases, self.n_sems = list(ins), list(out_shapes), dict(aliases), n_sems
        self.start, self.finish, self.deliver = start, finish, deliver

    def scratch(self):
        return [pltpu.SemaphoreType.DMA((self.n_sems,)), pltpu.SemaphoreType.DMA((self.n_sems,))]


def _comm_call(name, comm):
    n_in = len(comm.ins)

    def body(*refs):
        ins, outs, sems = refs[:n_in], refs[n_in:-2], (*refs[-2:], 0)
        comm.start(ins, outs, sems)
        comm.finish(ins, outs, sems)

    res = pl.pallas_call(
        body, name=name, out_shape=comm.out_shapes, in_specs=_any_specs(n_in),
        out_specs=_any_specs(len(comm.out_shapes)), input_output_aliases=comm.aliases,
        scratch_shapes=comm.scratch())(*comm.ins)
    return comm.deliver(list(res))


def _hosted_call(body, *, name, grid, in_specs, out_specs, out_shape, args, scratch_shapes=(), dims=None, comm=None,
                 prefetch=None):
    in_specs, out_specs, out_shape = list(in_specs), list(out_specs), list(out_shape)
    n_pre = 0 if prefetch is None else 1

    def call(fn, in_specs, out_specs, out_shape, scratch, aliases, dims, args):
        if prefetch is None:
            return pl.pallas_call(
                fn, name=name, grid=grid, in_specs=in_specs, out_specs=out_specs, out_shape=out_shape,
                scratch_shapes=scratch, input_output_aliases=aliases, compiler_params=_params(dims))(*args)
        spec = pltpu.PrefetchScalarGridSpec(num_scalar_prefetch=1, grid=grid, in_specs=in_specs, out_specs=out_specs,
                                            scratch_shapes=scratch)
        return pl.pallas_call(
            fn, name=name, grid_spec=spec, out_shape=out_shape,
            input_output_aliases={i + 1: o for i, o in aliases.items()}, compiler_params=_params(dims))(prefetch, *args)

    if comm is None:
        return list(call(body, in_specs, out_specs, out_shape, list(scratch_shapes), {},
                         dims or ("arbitrary",) * len(grid), args))
    n_in, n_out, n_ci, n_co = len(in_specs), len(out_specs), len(comm.ins), len(comm.out_shapes)

    def wrapped(*refs):
        pre, refs = refs[:n_pre], refs[n_pre:]
        ins, cins = refs[:n_in], refs[n_in:n_in + n_ci]
        outs = refs[n_in + n_ci:n_in + n_ci + n_out]
        couts = refs[n_in + n_ci + n_out:n_in + n_ci + n_out + n_co]
        scratch, sems = refs[n_in + n_ci + n_out + n_co:-2], (*refs[-2:], 0)
        first = functools.reduce(jnp.logical_and, [pl.program_id(k) == 0 for k in range(len(grid))])
        last = functools.reduce(jnp.logical_and, [pl.program_id(k) == grid[k] - 1 for k in range(len(grid))])

        @pl.when(first)
        def _():
            comm.start(cins, couts, sems)

        body(*pre, *ins, *outs, *scratch)

        @pl.when(last)
        def _():
            comm.finish(cins, couts, sems)

    res = call(wrapped, in_specs + _any_specs(n_ci), out_specs + _any_specs(n_co), out_shape + comm.out_shapes,
               list(scratch_shapes) + comm.scratch(), {n_in + i: n_out + o for i, o in comm.aliases.items()},
               ("arbitrary",) * len(grid), (*args, *comm.ins))
    comm.deliver(list(res[n_out:]))
    return list(res[:n_out])


def _gather_comm(bufs, deliver):
    n = len(bufs)
    geoms = [_Geom(*b.shape[1:]) for b in bufs]

    def first(outs, sems):
        x, y, c, chips = _place()
        cps = []
        for a in range(n):
            mine = geoms[a].half_ref(outs[a], (2 * x + y,), c)
            cps += [_remote(mine, mine, sems, 6 * a + j, (cx, cy, c)) for j, (cx, cy) in enumerate(chips)]
        return cps

    def start(ins, outs, sems):
        for cp in first(outs, sems):
            cp.start()

    def finish(ins, outs, sems):
        x, y, c, chips = _place()
        sib = (x, y, 1 - c)
        passed = []
        for a in range(n):
            for j, (cx, cy) in enumerate(chips):
                land = geoms[a].half_ref(outs[a], (2 * cx + cy,), c)
                _remote(land, land, sems, 6 * a + j, sib).wait_recv()
                cp = _remote(land, land, sems, 6 * a + 3 + j, sib)
                cp.start()
                passed.append(cp)
        for a in range(n):
            for j, (cx, cy) in enumerate(chips):
                land = geoms[a].half_ref(outs[a], (2 * cx + cy,), 1 - c)
                _remote(land, land, sems, 6 * a + 3 + j, sib).wait_recv()
        for cp in first(outs, sems) + passed:
            cp.wait_send()

    return _Comm(bufs, [jax.ShapeDtypeStruct(b.shape, b.dtype) for b in bufs], {a: a for a in range(n)}, 6 * n,
                 start, finish, deliver)


def _reduce_pair_comm(grads, deliver):
    n = len(grads)
    geoms = [_Geom(*a.shape[1:]) for a in grads]

    def copies(ins, outs, sems):
        x, y, c, _ = _place()
        return [_remote(geoms[a].half_ref(ins[a], (slice(None),), 1 - c), outs[a], sems, a, (x, y, 1 - c))
                for a in range(n)]

    def start(ins, outs, sems):
        for cp in copies(ins, outs, sems):
            cp.start()

    def finish(ins, outs, sems):
        cps = copies(ins, outs, sems)
        for cp in cps:
            cp.wait_recv()
        for cp in cps:
            cp.wait_send()

    shapes = [jax.ShapeDtypeStruct((N_SHARD, g.h0, g.h1), a.dtype) for a, g in zip(grads, geoms)]
    return _Comm(grads, shapes, {}, n, start, finish, deliver)


def _reduce_chips_comm(parts, deliver):
    n = len(parts)

    def copies(ins, outs, sems):
        x, y, c, chips = _place()
        return [_remote(ins[a].at[2 * cx + cy], outs[a].at[j], sems, 3 * a + j, (cx, cy, c))
                for a in range(n) for j, (cx, cy) in enumerate(chips)]

    def start(ins, outs, sems):
        for cp in copies(ins, outs, sems):
            cp.start()

    def finish(ins, outs, sems):
        cps = copies(ins, outs, sems)
        for cp in cps:
            cp.wait_recv()
        for cp in cps:
            cp.wait_send()

    shapes = [jax.ShapeDtypeStruct((3,) + a.shape[1:], a.dtype) for a in parts]
    return _Comm(parts, shapes, {}, 3 * n, start, finish, deliver)


def _share_pair_comm(bufs, deliver):
    n = len(bufs)
    geoms = [_Geom(*b.shape) for b in bufs]

    def copies(outs, sems):
        x, y, c, _ = _place()
        cps = []
        for a in range(n):
            mine = geoms[a].half_ref(outs[a], (), c)
            cps.append(_remote(mine, mine, sems, a, (x, y, 1 - c)))
        return cps

    def start(ins, outs, sems):
        for cp in copies(outs, sems):
            cp.start()

    def finish(ins, outs, sems):
        x, y, c, _ = _place()
        for a in range(n):
            land = geoms[a].half_ref(outs[a], (), 1 - c)
            _remote(land, land, sems, a, (x, y, 1 - c)).wait_recv()
        for cp in copies(outs, sems):
            cp.wait_send()

    return _Comm(bufs, [jax.ShapeDtypeStruct(b.shape, b.dtype) for b in bufs], {a: a for a in range(n)}, n,
                 start, finish, deliver)


def _small_comm(pack, deliver):
    r, d = pack.shape

    def copies(ins, outs, sems):
        x, y, c, _ = _place()
        cps = []
        for k in range(1, 8):
            peer = (x ^ ((k >> 2) & 1), y ^ ((k >> 1) & 1), c ^ (k & 1))
            cps.append(_remote(ins[0], outs[0].at[4 * x + 2 * y + c], sems, k - 1, peer))
        return cps

    def start(ins, outs, sems):
        for cp in copies(ins, outs, sems):
            cp.start()

    def finish(ins, outs, sems):
        cps = copies(ins, outs, sems)
        for cp in cps:
            cp.wait_recv()
        for cp in cps:
            cp.wait_send()

    return _Comm([pack], [jax.ShapeDtypeStruct((8, r, d), pack.dtype)], {}, 7, start, finish, deliver)


def _join_comms(comms):
    comms = [c for c in comms if c is not None]
    if len(comms) <= 1:
        return comms[0] if comms else None
    ins, out_shapes, aliases, spans, n_sems = [], [], {}, [], 0
    for c in comms:
        aliases.update({len(ins) + i: len(out_shapes) + o for i, o in c.aliases.items()})
        spans.append((len(ins), len(ins) + len(c.ins), len(out_shapes), len(out_shapes) + len(c.out_shapes), n_sems))
        ins += c.ins
        out_shapes += c.out_shapes
        n_sems += c.n_sems

    def run(which):
        def go(all_ins, all_outs, sems):
            for c, (i0, i1, o0, o1, base) in zip(comms, spans):
                getattr(c, which)(all_ins[i0:i1], all_outs[o0:o1], (sems[0], sems[1], sems[2] + base))
        return go

    def deliver(outs):
        for c, (_, _, o0, o1, _) in zip(comms, spans):
            c.deliver(outs[o0:o1])
        return outs

    return _Comm(ins, out_shapes, aliases, n_sems, run("start"), run("finish"), deliver)


def _ew_call(name, fn, ins, out_dtypes):
    shape = ins[0].shape
    cols = shape[-1]
    rows = 1
    for s_ in shape[:-1]:
        rows *= s_
    ins2 = [a.reshape(rows, cols) for a in ins]
    tr = rows
    for t in range(16, min(rows, max(16, (1 << 19) // cols)) + 1, 16):
        if rows % t == 0:
            tr = t
    no = len(out_dtypes)

    def body(*refs):
        outs = fn(*[r[...] for r in refs[:len(ins2)]])
        for ref, val in zip(refs[len(ins2):], outs):
            ref[...] = val.astype(ref.dtype)

    spec = pl.BlockSpec((tr, cols), lambda i: (i, 0))
    res = pl.pallas_call(
        body, name=name, grid=(rows // tr,), in_specs=[spec] * len(ins2), out_specs=[spec] * no,
        out_shape=[jax.ShapeDtypeStruct((rows, cols), dt) for dt in out_dtypes],
        compiler_params=_params(("parallel",)))(*ins2)
    return [r.reshape(shape) for r in res]


def _adamw_math(w, g, m, v):
    m = ADAM_B1 * m + (1.0 - ADAM_B1) * g
    v = ADAM_B2 * v + (1.0 - ADAM_B2) * (g * g)
    m_hat = m / (1.0 - ADAM_B1 ** ADAM_STEP)
    v_hat = v / (1.0 - ADAM_B2 ** ADAM_STEP)
    delta = -ADAM_LR * (m_hat / (jnp.sqrt(v_hat) + ADAM_EPS) + ADAM_WD * w)
    return delta, m, v


def _adamw_call(name, w, g, m, v):
    return _ew_call(name, _adamw_math, [w, g, m, v], [F32, F32, F32])


def _tiled_call(name, fn, place, grid, in_items, out_items, comm=None):
    ni = len(in_items)

    def body(place_ref, *refs):
        vals = fn(*[r[...] for r in refs[:ni]])
        for ref, val in zip(refs[ni:], vals):
            ref[...] = val.astype(ref.dtype)

    return _hosted_call(
        body, name=name, grid=grid, in_specs=[pl.BlockSpec(blk, imap) for _, blk, imap in in_items],
        out_specs=[pl.BlockSpec(blk, imap) for _, _, blk, imap in out_items],
        out_shape=[jax.ShapeDtypeStruct(shp, dt) for shp, dt, _, _ in out_items],
        args=[a for a, _, _ in in_items], prefetch=place, comm=comm)


def _cast_call(name, place, shards):
    n0, n1 = shards[0].shape
    tr = _fit_rows(n0, n1, 1.5 * len(shards))
    ins = [(a, (tr, n1), lambda i, p: (i, 0)) for a in shards]
    outs = [((N_SHARD, n0, n1), BF16, (1, tr, n1), lambda i, p: (p[0], i, 0)) for _ in shards]
    return _tiled_call(name, lambda *v: [x[None] for x in v], place, (n0 // tr,), ins, outs)


def _pair_sum_call(name, place, fulls, gots):
    k = len(fulls)
    g = _Geom(*fulls[0].shape[1:], blocks=2.5 * k)
    blk = (1, g.tr, g.h1)
    ins = [(a, blk, lambda s, i, p: g.half_block((s,), i, p[1])) for a in fulls]
    ins += [(a, blk, lambda s, i, p: (s, i, 0)) for a in gots]
    outs = [((N_SHARD, g.h0, g.h1), BF16, blk, lambda s, i, p: (s, i, 0)) for _ in fulls]
    return _tiled_call(name, lambda *v: [v[j] + v[k + j] for j in range(k)], place, (N_SHARD, g.nblk), ins, outs)


def _chip_sum_call(name, place, fulls, gots, recvs, comm=None):
    k = len(fulls)
    g = _Geom(*fulls[0].shape[1:], blocks=4.5 * k)
    blk = (1, g.tr, g.h1)
    ins = [(a, blk, lambda i, p: g.half_block((p[0],), i, p[1])) for a in fulls]
    ins += [(a, blk, lambda i, p: (p[0], i, 0)) for a in gots]
    ins += [(a, (3, g.tr, g.h1), lambda i, p: (0, i, 0)) for a in recvs]
    outs = [((g.n0, g.n1), F32, (g.tr, g.h1), lambda i, p: g.half_block((), i, p[1])) for _ in fulls]

    def fn(*v):
        res = []
        for j in range(k):
            r = v[2 * k + j].astype(F32)
            res.append(v[j][0] + v[k + j][0] + r[0] + r[1] + r[2])
        return res

    return _tiled_call(name, fn, place, (g.nblk,), ins, outs, comm)


def _adamw_group_call(name, ws, gs, ms, vs, comm=None):
    k = len(ws)
    n0, n1 = ws[0].shape
    tr = _fit_rows(n0, n1, 8 * k)
    spec = pl.BlockSpec((tr, n1), lambda i: (i, 0))

    def body(*refs):
        for j in range(k):
            g = refs[k + j][...]
            delta, m, vv = _adamw_math(refs[j][...], g, refs[2 * k + j][...], refs[3 * k + j][...])
            for ref, val in zip(refs[4 * k + 4 * j:4 * k + 4 * j + 4], (g, delta, m, vv)):
                ref[...] = val

    flat = _hosted_call(
        body, name=name, grid=(n0 // tr,), in_specs=[spec] * (4 * k), out_specs=[spec] * (4 * k),
        out_shape=[jax.ShapeDtypeStruct((n0, n1), F32)] * (4 * k), dims=("parallel",),
        args=(*ws, *gs, *ms, *vs), comm=comm)
    return [flat[4 * j:4 * j + 4] for j in range(k)]


def _small_update_call(me, early, own_early, late, own_late, wp, mp, vp):
    nd, r, d = early.shape

    def body(me_ref, e_ref, oe_ref, l_ref, ol_ref, w_ref, m_ref, v_ref, gs_ref, d_ref, nm_ref, nv_ref):
        mine = me_ref[0]

        def total(g_ref, own_ref):
            acc = None
            for k in range(nd):
                part = jnp.where(mine == k, own_ref[...], g_ref[k])
                acc = part if acc is None else acc + part
            return acc

        gs = total(e_ref, oe_ref)
        ls = total(l_ref, ol_ref)
        gs_ref[...] = gs
        first = gs[0:8] + ls[0:8]
        gs_ref[0:8, :] = first
        gs_ref[ROW_META:ROW_META + N_META, :] = gs[ROW_META:ROW_META + N_META] + ls[8:8 + N_META]
        grads = jnp.concatenate([first, gs[8:SMALL_ADAM_ROWS]], axis=0)
        delta, m, v = _adamw_math(w_ref[...], grads, m_ref[...], v_ref[...])
        d_ref[...] = delta
        nm_ref[...] = m
        nv_ref[...] = v

    vm = pl.BlockSpec(memory_space=pltpu.VMEM)
    ashape = jax.ShapeDtypeStruct((SMALL_ADAM_ROWS, d), F32)
    return pl.pallas_call(
        body, name="small_update", in_specs=[pl.BlockSpec(memory_space=pltpu.SMEM)] + [vm] * 7, out_specs=[vm] * 4,
        out_shape=[jax.ShapeDtypeStruct((r, d), F32), ashape, ashape, ashape],
        compiler_params=pltpu.CompilerParams(vmem_limit_bytes=VMEM_LIMIT_BYTES))(
            me, early, own_early, late, own_late, wp, mp, vp)


SMALL_NAMES = ["ffn1_norm", "mix_norm", "ffn2_norm", "final_norm", "q_latent_norm", "kv_latent_norm",
               "q_head_norm", "k_head_norm", "conv_b", "gate_a_b", "gate_x_b", "lru_lambda", "attn_out_norm",
               "lru_out_norm"]
ROW_CONV_W = 14
ROW_GATE_A = 16
ROW_GATE_X = 48
ROW_META = 80
ROW_LOSS = 96


def _row(a):
    flat = a.reshape(1, -1)
    return jnp.pad(flat, ((0, 0), (0, D_MODEL - flat.shape[1])))


def _pack_small(t, rows):
    parts = [_row(t[nm]) for nm in SMALL_NAMES]
    parts.append(t["conv_w"].reshape(2, D_MODEL))
    parts.append(t["gate_a_w"].reshape(32, D_MODEL))
    parts.append(t["gate_x_w"].reshape(32, D_MODEL))
    p = jnp.concatenate(parts, axis=0)
    return jnp.pad(p, ((0, rows - p.shape[0]), (0, 0)))


def _early_pack(g):
    gs = {nm: g.get(nm, jnp.zeros((1, D_MODEL), F32)) for nm in SMALL_NAMES}
    gs["q_head_norm"] = g["q_head_norm"][:, 0:D_QK]
    gs["k_head_norm"] = g["k_head_norm"][:, 0:D_QK]
    for nm in ("conv_b", "gate_a_b", "gate_x_b", "lru_lambda"):
        gs[nm] = g[nm].reshape(1, LRU_W)
    gs["conv_w"] = g["conv_w"].transpose(1, 0, 2).reshape(CONV_K, LRU_W)
    gs["gate_a_w"] = _gate_blocks(g["gate_a_w"])
    gs["gate_x_w"] = _gate_blocks(g["gate_x_w"])
    return jnp.concatenate([_pack_small(gs, ROW_META), jnp.zeros((N_META, D_MODEL), F32), _row(g["loss"][:, 0:1]),
                            jnp.zeros((SMALL_ROWS - ROW_LOSS - 1, D_MODEL), F32)], axis=0)


def _unpack_small(p, like):
    out = {}
    for k, nm in enumerate(SMALL_NAMES):
        out[nm] = p[k, 0:like[nm].size].reshape(like[nm].shape)
    out["gate_a_w"] = p[ROW_GATE_A:ROW_GATE_A + 32].reshape(like["gate_a_w"].shape)
    out["gate_x_w"] = p[ROW_GATE_X:ROW_GATE_X + 32].reshape(like["gate_x_w"].shape)
    return out


def _gate_dense(wg):
    w4 = wg[0].reshape(N_LRU_TILES, 2, 64, 64)
    zero = jnp.zeros((N_LRU_TILES, 64, 64), wg.dtype)
    top = jnp.concatenate([w4[:, 0], zero], axis=2)
    bot = jnp.concatenate([zero, w4[:, 1]], axis=2)
    return jnp.concatenate([top, bot], axis=1).astype(BF16)


def _gate_blocks(dw):
    return jnp.stack([dw[:, 0:64, 0:64], dw[:, 64:128, 64:128]], axis=1).reshape(8, 64, 64)


BIG_NAMES = ["ffn1_w_gate", "ffn1_w_up", "ffn1_w_down", "w_in", "w_uq", "w_uk", "w_uv", "w_out", "ffn2_w_gate",
             "ffn2_w_up", "ffn2_w_down"]
BIG_GROUPS = [["ffn1_w_gate", "ffn1_w_up", "ffn1_w_down", "ffn2_w_gate", "ffn2_w_up", "ffn2_w_down"], ["w_in"],
              ["w_uq"], ["w_uk", "w_uv"], ["w_out"]]
TRANSPOSED = ("ffn1_w_gate", "ffn1_w_up", "ffn2_w_gate", "ffn2_w_up", "w_in", "w_uq")


def _to2d(nm, a):
    return a[0].T if nm in TRANSPOSED else a[0]


def _from2d(nm, a):
    return (a.T if nm in TRANSPOSED else a)[None]


WEIGHT_NAMES = ["meta_tokens", "ffn1_norm", "ffn1_w_gate", "ffn1_w_up", "ffn1_w_down", "mix_norm", "w_in",
                "q_latent_norm", "w_uq", "kv_latent_norm", "w_uk", "w_uv", "q_head_norm", "k_head_norm", "conv_w",
                "conv_b", "gate_a_w", "gate_a_b", "gate_x_w", "gate_x_b", "lru_lambda", "attn_out_norm",
                "lru_out_norm", "w_out", "ffn2_norm", "ffn2_w_gate", "ffn2_w_up", "ffn2_w_down", "final_norm"]


def _weight_from(nm, slots):
    if nm == "w_in":
        win = slots.reshape(IN_WIDTH, D_MODEL)
        return jnp.concatenate([win[0:Z_KR + D_ROPE], jnp.zeros((128 - D_ROPE, D_MODEL), BF16),
                                win[Z_KR + D_ROPE:]], axis=0)
    if nm == "w_uq":
        return jnp.pad(slots, ((0, 0), (0, D_QKP - D_QK), (0, 0))).reshape(HEADS * D_QKP, Q_RANK)
    if nm in ("w_uk", "w_uv"):
        return slots.transpose(1, 0, 2).reshape(KV_RANK, HEADS * D_NOPE)
    if nm == "w_out":
        return slots.reshape(D_MODEL, D_MODEL)
    return slots


def _small_weights(p, small):
    w = {nm: p[nm] for nm in SMALL_NAMES}
    w["q_head_norm"] = jnp.pad(p["q_head_norm"], ((0, 0), (0, D_QKP - D_QK)))
    w["k_head_norm"] = jnp.pad(p["k_head_norm"], ((0, 0), (0, D_QKP - D_QK)))
    w["conv_w"] = small[:, N_META:N_META + 2, :].reshape(N_SHARD, CONV_K, LRU_TILE)
    w["gate_a_w"] = _gate_dense(p["gate_a_w"])
    w["gate_x_w"] = _gate_dense(p["gate_x_w"])
    meta = small[:, 0:N_META, :].transpose(1, 0, 2).reshape(N_META, D_MODEL)
    return w, meta


def _full_weights(p, gathered, small):
    w, meta = _small_weights(p, small)
    w.update({nm: _weight_from(nm, gathered[nm]) for nm in BIG_NAMES})
    return w, meta


def _shard_grad(nm, g):
    if nm == "w_in":
        return jnp.concatenate([g[0:Z_KR + D_ROPE], g[Z_MLA:]], axis=0).reshape(N_SHARD, IN_WIDTH // N_SHARD, D_MODEL)
    if nm == "w_uq":
        return g.reshape(HEADS, D_QKP, Q_RANK)[:, 0:D_QK, :]
    if nm in ("w_uk", "w_uv"):
        return g.reshape(KV_RANK, HEADS, D_NOPE).transpose(1, 0, 2)
    if nm == "w_out":
        return g.reshape(N_SHARD, D_MODEL // N_SHARD, D_MODEL)
    return g


def _shard_grads(g):
    return {nm: _shard_grad(nm, g[nm]) for nm in BIG_NAMES}


GATHER_FIRST = ["ffn1_w_gate"]
GATHER_AT = {"ffn1_gate": ["ffn1_w_up"], "ffn1_upact": ["ffn1_w_down"],
             "ffn1_down": ["w_in", "w_uq", "w_uk", "w_uv", "w_out"], "attn_fwd": ["ffn2_w_down", "ffn2_w_gate"],
             "lru_fwd": ["ffn2_w_up"]}
PAIR_AT = [("ffn2_din", ["ffn2_w_gate", "ffn2_w_up", "ffn2_w_down"]),
           ("mix_din", ["w_out", "w_uq", "w_uk", "w_uv", "w_in"]),
           ("ffn1_dwg", ["ffn1_w_down"]), ("ffn1_dwu", ["ffn1_w_gate"]), ("ffn1_din", ["ffn1_w_up"])]
CHIPS_AT = [("attn_bwd", ["ffn2_w_down", "ffn2_w_gate"]), ("mla_prep_bwd", ["ffn2_w_up"]),
            ("ffn1_dact", ["w_out", "w_uq", "w_uk", "w_uv", "w_in"]),
            ("ffn1_dwu", ["ffn1_w_down"]), ("ffn1_din", ["ffn1_w_gate"]), (None, ["ffn1_w_up"])]
SHARE_EARLY_GROUPS, SHARE_EARLY_AT = 3, "ffn1_dwd"
SMALL_EARLY_AT = "mix_din"


def _same_shape_groups(names):
    return [[nm for nm in grp if nm in names] for grp in BIG_GROUPS if any(nm in names for nm in grp)]


class _Sched:
    def __init__(self, place, w, slots):
        self.place, self.w, self.slots = place, w, slots
        self.g = None
        self.sharded, self.from_pair, self.chip_bf16, self.from_chips = {}, {}, {}, {}
        self.early = self.early_all = None
        self.shared = {}

    def host(self, stage):
        comms = []
        if stage in GATHER_AT:
            comms.append(self.gather(GATHER_AT[stage]))
        comms += [self.chips(names) for at, names in CHIPS_AT if at == stage]
        comms += [self.pair(names) for at, names in PAIR_AT if at == stage]
        if stage == SMALL_EARLY_AT:
            comms.append(self.small_early())
        if stage == SHARE_EARLY_AT:
            comms.append(self.share_early())
        return _join_comms(comms)

    def small_early(self):
        self.early = _early_pack(self.g)

        def deliver(outs):
            self.early_all = outs[0]
            return outs

        return _small_comm(self.early, deliver)

    def gather(self, names):
        def deliver(outs):
            self.w.update({nm: _weight_from(nm, o) for nm, o in zip(names, outs)})
            return outs

        return _gather_comm([self.slots[nm] for nm in names], deliver)

    def pair(self, names):
        self.sharded.update({nm: _shard_grad(nm, self.g[nm]) for nm in names})

        def deliver(outs):
            self.from_pair.update(zip(names, outs))
            for grp in _same_shape_groups(names):
                sums = _pair_sum_call("pair_sum_" + grp[0], self.place, [self.sharded[nm] for nm in grp],
                                      [self.from_pair[nm] for nm in grp])
                self.chip_bf16.update(zip(grp, sums))
            return outs

        return _reduce_pair_comm([self.sharded[nm] for nm in names], deliver)

    def chips(self, names):
        def deliver(outs):
            self.from_chips.update(zip(names, outs))
            return outs

        return _reduce_chips_comm([self.chip_bf16[nm] for nm in names], deliver)

    def chip_sums(self, names, comm=None):
        out = {}
        for grp in _same_shape_groups(names):
            sums = _chip_sum_call("chip_sum_" + grp[0], self.place, [self.sharded[nm] for nm in grp],
                                  [self.from_pair[nm] for nm in grp], [self.from_chips[nm] for nm in grp], comm)
            comm = None
            out.update(zip(grp, sums))
        return out

    def share_early(self):
        names = [nm for at, grp in CHIPS_AT[:SHARE_EARLY_GROUPS] for nm in grp]
        mine = self.chip_sums(names)
        return _share_pair_comm([mine[nm] for nm in names], lambda o: self.shared.update(zip(names, o)))


def kernel(x, meta_tokens, ffn1_norm, ffn1_w_gate, ffn1_w_up, ffn1_w_down, mix_norm, w_in, q_latent_norm, w_uq, kv_latent_norm, w_uk, w_uv, q_head_norm, k_head_norm, conv_w, conv_b, gate_a_w, gate_a_b, gate_x_w, gate_x_b, lru_lambda, attn_out_norm, lru_out_norm, w_out, ffn2_norm, ffn2_w_gate, ffn2_w_up, ffn2_w_down, final_norm, loss_target, m_meta_tokens, m_ffn1_norm, m_ffn1_w_gate, m_ffn1_w_up, m_ffn1_w_down, m_mix_norm, m_w_in, m_q_latent_norm, m_w_uq, m_kv_latent_norm, m_w_uk, m_w_uv, m_q_head_norm, m_k_head_norm, m_conv_w, m_conv_b, m_gate_a_w, m_gate_a_b, m_gate_x_w, m_gate_x_b, m_lru_lambda, m_attn_out_norm, m_lru_out_norm, m_w_out, m_ffn2_norm, m_ffn2_w_gate, m_ffn2_w_up, m_ffn2_w_down, m_final_norm, v_meta_tokens, v_ffn1_norm, v_ffn1_w_gate, v_ffn1_w_up, v_ffn1_w_down, v_mix_norm, v_w_in, v_q_latent_norm, v_w_uq, v_kv_latent_norm, v_w_uk, v_w_uv, v_q_head_norm, v_k_head_norm, v_conv_w, v_conv_b, v_gate_a_w, v_gate_a_b, v_gate_x_w, v_gate_x_b, v_lru_lambda, v_attn_out_norm, v_lru_out_norm, v_w_out, v_ffn2_norm, v_ffn2_w_gate, v_ffn2_w_up, v_ffn2_w_down, v_final_norm):
    args = locals()
    p = {nm: args[nm] for nm in WEIGHT_NAMES}
    mom = {nm: args["m_" + nm] for nm in WEIGHT_NAMES}
    var = {nm: args["v_" + nm] for nm in WEIGHT_NAMES}
    nb, seq, d = x.shape
    lp = CHUNK + seq
    xi, yi, ci = lax.axis_index("x"), lax.axis_index("y"), lax.axis_index("c")
    chip = 2 * xi + yi

    place = jnp.stack([chip, ci]).astype(jnp.int32)
    p2 = {nm: _to2d(nm, p[nm]) for nm in BIG_NAMES}
    m2 = {nm: _to2d(nm, mom[nm]) for nm in BIG_NAMES}
    v2 = {nm: _to2d(nm, var[nm]) for nm in BIG_NAMES}

    slots = {}
    for grp in BIG_GROUPS:
        for nm, buf in zip(grp, _cast_call("cast_" + grp[0], place, [p2[nm] for nm in grp])):
            slots[nm] = buf
    small_shard = jnp.concatenate(
        [meta_tokens, conv_w[0].reshape(2, 2 * LRU_TILE), jnp.zeros((14, 2 * LRU_TILE), F32)], axis=0)
    small_slots = lax.dynamic_update_slice(jnp.zeros((N_SHARD,) + small_shard.shape, F32), small_shard[None],
                                           (chip, 0, 0))
    first = _comm_call("gather_first", _gather_comm([slots[nm] for nm in GATHER_FIRST] + [small_slots], lambda o: o))
    w, meta = _small_weights(p, first[-1])
    w.update({nm: _weight_from(nm, o) for nm, o in zip(GATHER_FIRST, first[:-1])})
    sched = _Sched(place, w, slots)

    h0 = jnp.concatenate(
        [jnp.zeros((nb, PAD_ROWS, d), F32), jnp.broadcast_to(meta[None], (nb, N_META, d)), x], axis=1)
    target = jnp.pad(loss_target, ((0, 0), (CHUNK, 0), (0, 0)))
    loss_part, dh0, g = _local_step(h0.reshape(nb * lp, d), target.reshape(nb * lp, d), w, nb, lp, sched)
    dh0 = dh0.reshape(nb, lp, d)
    grad_x = dh0[:, CHUNK:, :]

    late = jnp.concatenate([g["ffn1_norm"], g["mix_norm"], jnp.zeros((6, D_MODEL), F32),
                            jnp.sum(dh0[:, PAD_ROWS:CHUNK, :], axis=0)], axis=0)
    shared = sched.shared
    rest = [nm for at, names in CHIPS_AT[SHARE_EARLY_GROUPS:] if at is not None for nm in names]
    last = [nm for at, names in CHIPS_AT if at is None for nm in names]
    late_box = {}
    mine = sched.chip_sums(rest, _small_comm(late, lambda o: late_box.update(all=o[0])))
    share = _share_pair_comm([mine[nm] for nm in rest], lambda o: shared.update(zip(rest, o)))
    _comm_call("share_pair", _join_comms([share, sched.chips(last)]))
    mine = sched.chip_sums(last)
    _comm_call("share_last", _share_pair_comm([mine[nm] for nm in last], lambda o: shared.update(zip(last, o))))
    late_all = late_box["all"]
    small_like = {nm: p[nm] for nm in SMALL_NAMES + ["gate_a_w", "gate_x_w"]}

    def pack_w(t):
        tt = {nm: t[nm] for nm in SMALL_NAMES + ["gate_a_w", "gate_x_w"]}
        tt["conv_w"] = jnp.zeros((CONV_K, LRU_W), F32)
        return _pack_small(tt, SMALL_ADAM_ROWS)

    me = (4 * xi + 2 * yi + ci).astype(jnp.int32).reshape(1)
    gsum, dsm, msm, vsm = _small_update_call(me, sched.early_all, sched.early, late_all, late, pack_w(p),
                                             pack_w(mom), pack_w(var))
    grads = _unpack_small(gsum, small_like)
    delta = _unpack_small(dsm, small_like)
    new_m = _unpack_small(msm, small_like)
    new_v = _unpack_small(vsm, small_like)
    loss = gsum[ROW_LOSS, 0]
    gmeta = gsum[ROW_META:ROW_META + N_META].reshape(N_META, N_SHARD, D_MODEL // N_SHARD)
    grads["meta_tokens"] = lax.dynamic_index_in_dim(gmeta, chip, axis=1, keepdims=False)
    gconv = gsum[ROW_CONV_W:ROW_CONV_W + 2].reshape(CONV_K, N_SHARD, LRU_TILE)
    grads["conv_w"] = lax.dynamic_index_in_dim(gconv, chip, axis=1, keepdims=False)[None]
    for nm in ("meta_tokens", "conv_w"):
        delta[nm], new_m[nm], new_v[nm] = _adamw_call("adamw_" + nm, p[nm], grads[nm], mom[nm], var[nm])

    for names in BIG_GROUPS:
        res = _adamw_group_call("adamw_" + names[0], [p2[nm] for nm in names], [shared[nm] for nm in names],
                                [m2[nm] for nm in names], [v2[nm] for nm in names])
        for nm, (gg, dd, mm, vv) in zip(names, res):
            grads[nm], delta[nm], new_m[nm], new_v[nm] = (_from2d(nm, t) for t in (gg, dd, mm, vv))

    return (loss, grad_x, *[grads[nm] for nm in WEIGHT_NAMES], *[delta[nm] for nm in WEIGHT_NAMES],
            *[new_m[nm] for nm in WEIGHT_NAMES], *[new_v[nm] for nm in WEIGHT_NAMES])
```

```python
import functools
import math

import jax
import jax.numpy as jnp
import numpy as np
from jax import lax
from jax.experimental import pallas as pl
from jax.experimental.pallas import tpu as pltpu

F32 = jnp.float32
BF16 = jnp.bfloat16
MESH = pl.DeviceIdType.MESH

D_MODEL = 1024
N_META = 16
CHUNK = 64
PAD_ROWS = CHUNK - N_META
HEADS = 4
D_NOPE = 128
D_ROPE = 64
D_QK = D_NOPE + D_ROPE
D_QKP = 256
D_V = 128
KV_RANK = 256
Q_RANK = 384
MLA_W = HEADS * D_V
LRU_W = 512
LRU_TILE = 128
N_LRU_TILES = LRU_W // LRU_TILE
CONV_K = 4
C_RGLRU = 8.0
ROPE_THETA = 10000.0
D_FF = 2816
N_SHARD = 4
EPS = 1e-6
NEG_INF = -1e30
Z_KR = Q_RANK + KV_RANK
Z_MLA = Z_KR + 128
Z_U = Z_MLA
Z_G = Z_U + LRU_W
Z_W = Z_G + LRU_W
IN_WIDTH = Q_RANK + KV_RANK + D_ROPE + 2 * LRU_W

ADAM_LR = 0.001
ADAM_B1 = 0.9
ADAM_B2 = 0.999
ADAM_EPS = 1e-08
ADAM_WD = 0.01
ADAM_STEP = 10

VMEM_LIMIT_BYTES = 56 * 1024 * 1024
SMALL_ROWS = 104
SMALL_ADAM_ROWS = 80


def _params(sem):
    return pltpu.CompilerParams(dimension_semantics=sem, vmem_limit_bytes=VMEM_LIMIT_BYTES)


def _resident(shape):
    return pl.BlockSpec(tuple(shape), lambda i: (0,) * len(shape), pipeline_mode=pl.Buffered(1))


def _row_tile(rows, target):
    best = 16
    for t in range(16, min(rows, target) + 1, 16):
        if rows % t == 0:
            best = t
    return best


def _col_tile(cols, target):
    best = cols
    for t in range(128, min(cols, target) + 1, 128):
        if cols % t == 0:
            best = t
    return best


def _dot(a, b):
    return jnp.dot(a, b, preferred_element_type=F32)


def _dot_nt(a, b):
    return lax.dot_general(a, b, (((1,), (1,)), ((), ())), preferred_element_type=F32)


def _dot_tn(a, b):
    return lax.dot_general(a, b, (((0,), (0,)), ((), ())), preferred_element_type=F32)


def _rms(x, n):
    return lax.rsqrt(jnp.sum(x * x, axis=-1, keepdims=True) * (1.0 / n) + EPS)


def _rms_bwd(dn, nrm, r, n):
    return r * (dn - nrm * (jnp.sum(dn * nrm, axis=-1, keepdims=True) * (1.0 / n)))


def _gelu(x):
    k = math.sqrt(2.0 / math.pi)
    t = jnp.tanh(k * (x + 0.044715 * x * x * x))
    return 0.5 * x * (1.0 + t), t


def _gelu_grad(x, t):
    k = math.sqrt(2.0 / math.pi)
    return 0.5 * (1.0 + t) + 0.5 * x * (1.0 - t * t) * k * (1.0 + 3.0 * 0.044715 * x * x)


def _sigmoid(x):
    return 0.5 + 0.5 * jnp.tanh(0.5 * x)


def _softplus_neg(lam):
    e = jnp.exp(-jnp.abs(lam))
    log1p = jnp.where(e < 0.01, e * (1.0 - e * (0.5 - e * (1.0 / 3 - e * 0.25))), jnp.log(1.0 + e))
    return jnp.maximum(-lam, 0.0) + log1p


def _rope(t, c, s1, s2):
    return t * c + pltpu.roll(t, 96, 1) * s1 + pltpu.roll(t, 32, 1) * s2


def _rope_t(d, c, s1, s2):
    return d * c + pltpu.roll(d * s1, 32, 1) + pltpu.roll(d * s2, 96, 1)


def _rope_tables(lp):
    pos = (np.arange(lp, dtype=np.int32) - PAD_ROWS).astype(np.float32)
    inv_freq = (ROPE_THETA ** (-np.arange(0, D_ROPE // 2, dtype=np.float32) / (D_ROPE // 2))).astype(np.float32)
    ang = (pos[:, None] * inv_freq[None, :]).astype(np.float32).astype(np.float64)
    cos, sin = np.cos(ang).astype(np.float32), np.sin(ang).astype(np.float32)
    z = np.zeros_like(cos)
    return (jnp.asarray(np.concatenate([cos, cos, z, z], 1)), jnp.asarray(np.concatenate([-sin, z, z, z], 1)),
            jnp.asarray(np.concatenate([z, sin, z, z], 1)))


def _rmsnorm_call(name, h, g, tm):
    rows, d = h.shape

    def body(h_ref, g_ref, o_ref):
        x = h_ref[...]
        o_ref[...] = (x * _rms(x, d) * g_ref[...]).astype(BF16)

    return pl.pallas_call(
        body, name=name, grid=(rows // tm,),
        in_specs=[pl.BlockSpec((tm, d), lambda i: (i, 0)), pl.BlockSpec((1, d), lambda i: (0, 0))],
        out_specs=pl.BlockSpec((tm, d), lambda i: (i, 0)),
        out_shape=jax.ShapeDtypeStruct((rows, d), BF16),
        compiler_params=_params(("parallel",)))(h, g)


def _ffn_up_call(name, u, wg, wu, tm, comm=None):
    rows, d = u.shape
    ns, fs, _ = wg.shape

    def body(u_ref, wg_ref, wu_ref, g_ref, p_ref, a_ref):
        uu = u_ref[...]
        g = _dot_nt(uu, wg_ref[0])
        p = _dot_nt(uu, wu_ref[0])
        g_ref[0] = g.astype(BF16)
        p_ref[0] = p.astype(BF16)
        a_ref[0] = (g * jax.nn.sigmoid(g) * p).astype(BF16)

    wspec = pl.BlockSpec((1, fs, d), lambda s, i: (s, 0, 0))
    ospec = pl.BlockSpec((1, tm, fs), lambda s, i: (s, i, 0))
    oshape = jax.ShapeDtypeStruct((ns, rows, fs), BF16)
    return _hosted_call(
        body, name=name, grid=(ns, rows // tm),
        in_specs=[pl.BlockSpec((tm, d), lambda s, i: (i, 0)), wspec, wspec],
        out_specs=[ospec, ospec, ospec], out_shape=[oshape, oshape, oshape],
        dims=("parallel", "parallel"), args=(u, wg, wu), comm=comm)


def _ffn_gate_call(name, u, wg, tm, comm=None):
    rows, d = u.shape
    ns, fs, _ = wg.shape

    def body(u_ref, wg_ref, g_ref):
        g_ref[0] = _dot_nt(u_ref[...], wg_ref[0]).astype(BF16)

    return _hosted_call(
        body, name=name, grid=(ns, rows // tm),
        in_specs=[pl.BlockSpec((tm, d), lambda s, i: (i, 0)), pl.BlockSpec((1, fs, d), lambda s, i: (s, 0, 0))],
        out_specs=[pl.BlockSpec((1, tm, fs), lambda s, i: (s, i, 0))],
        out_shape=[jax.ShapeDtypeStruct((ns, rows, fs), BF16)],
        dims=("parallel", "parallel"), args=(u, wg), comm=comm)[0]


def _ffn_upact_call(name, u, wu, gate, tm, comm=None):
    rows, d = u.shape
    ns, fs, _ = wu.shape

    def body(u_ref, wu_ref, g_ref, p_ref, a_ref):
        p = _dot_nt(u_ref[...], wu_ref[0])
        g = g_ref[0].astype(F32)
        p_ref[0] = p.astype(BF16)
        a_ref[0] = (g * jax.nn.sigmoid(g) * p).astype(BF16)

    ospec = pl.BlockSpec((1, tm, fs), lambda s, i: (s, i, 0))
    oshape = jax.ShapeDtypeStruct((ns, rows, fs), BF16)
    return _hosted_call(
        body, name=name, grid=(ns, rows // tm),
        in_specs=[pl.BlockSpec((tm, d), lambda s, i: (i, 0)), pl.BlockSpec((1, fs, d), lambda s, i: (s, 0, 0)), ospec],
        out_specs=[ospec, ospec], out_shape=[oshape, oshape],
        dims=("parallel", "parallel"), args=(u, wu, gate), comm=comm)


def _ffn_down_call(name, a, wd, h, tm, comm=None):
    rows, d = h.shape
    ns, _, fs = a.shape

    def body(a_ref, wd_ref, h_ref, o_ref):
        acc = h_ref[...]
        for s in range(ns):
            acc = acc + 0.5 * _dot(a_ref[s], wd_ref[s])
        o_ref[...] = acc

    return _hosted_call(
        body, name=name, grid=(rows // tm,),
        in_specs=[pl.BlockSpec((ns, tm, fs), lambda i: (0, i, 0)),
                  _resident((ns, fs, d)),
                  pl.BlockSpec((tm, d), lambda i: (i, 0))],
        out_specs=[pl.BlockSpec((tm, d), lambda i: (i, 0))],
        out_shape=[jax.ShapeDtypeStruct((rows, d), F32)],
        dims=("parallel",), args=(a, wd, h), comm=comm)[0]


def _mm_call(name, a, bt, tm, out_dtype):
    rows, k = a.shape
    n = bt.shape[0]

    def body(a_ref, b_ref, o_ref):
        o_ref[...] = _dot_nt(a_ref[...], b_ref[...]).astype(out_dtype)

    return pl.pallas_call(
        body, name=name, grid=(rows // tm,),
        in_specs=[pl.BlockSpec((tm, k), lambda i: (i, 0)), pl.BlockSpec((n, k), lambda i: (0, 0))],
        out_specs=pl.BlockSpec((tm, n), lambda i: (i, 0)),
        out_shape=jax.ShapeDtypeStruct((rows, n), out_dtype),
        compiler_params=_params(("parallel",)))(a, bt)


def _mla_heads(z, gql, gkvl, wuq, wuk, wuv):
    cq = z[:, 0:Q_RANK]
    ckv = z[:, Q_RANK:Z_KR]
    kr = z[:, Z_KR:Z_MLA]
    rq = _rms(cq, Q_RANK)
    nq = cq * rq
    cqn = (nq * gql).astype(BF16)
    rkv = _rms(ckv, KV_RANK)
    nkv = ckv * rkv
    ckvn = (nkv * gkvl).astype(BF16)
    qraw = _dot_nt(cqn, wuq)
    knope = _dot(ckvn, wuk)
    v = _dot(ckvn, wuv)
    skr = jnp.sum(kr * kr, axis=-1, keepdims=True)
    heads = []
    for hd in range(HEADS):
        qh = qraw[:, hd * D_QKP:(hd + 1) * D_QKP]
        rqh = lax.rsqrt(jnp.sum(qh * qh, axis=-1, keepdims=True) * (1.0 / D_QK) + EPS)
        kn = knope[:, hd * D_NOPE:(hd + 1) * D_NOPE]
        rkh = lax.rsqrt((jnp.sum(kn * kn, axis=-1, keepdims=True) + skr) * (1.0 / D_QK) + EPS)
        heads.append((qh * rqh, rqh, kn * rkh, kr * rkh, rkh))
    return dict(rq=rq, nq=nq, cqn=cqn, rkv=rkv, nkv=nkv, ckvn=ckvn, v=v, heads=heads)


def _mla_prep_call(z, gql, gkvl, gqh, gkh, wuq, wuk, wuv, tabs, lp, tm):
    rows = z.shape[0]
    tpe = lp // tm

    def body(z_ref, gql_ref, gkvl_ref, gqh_ref, gkh_ref, wuq_ref, wuk_ref, wuv_ref, c_ref, s1_ref, s2_ref,
             q_ref, k_ref, v_ref, cqn_ref, ckvn_ref):
        m = _mla_heads(z_ref[...], gql_ref[...], gkvl_ref[...], wuq_ref[...], wuk_ref[...], wuv_ref[...])
        c, s1, s2 = c_ref[...], s1_ref[...], s2_ref[...]
        gq, gk = gqh_ref[...], gkh_ref[...]
        row = (pl.program_id(0) % tpe) * tm + lax.broadcasted_iota(jnp.int32, (tm, 1), 0)
        spare = (lax.broadcasted_iota(jnp.int32, (1, D_QKP - D_NOPE), 1) == D_ROPE).astype(F32)
        kmask = jnp.where(row < PAD_ROWS, NEG_INF * math.sqrt(D_QK), 0.0) * spare
        for hd in range(HEADS):
            qn, _, knn, krn, _ = m["heads"][hd]
            qg = qn * gq
            q_ref[hd, :, 0:D_NOPE] = qg[:, 0:D_NOPE].astype(BF16)
            q_ref[hd, :, D_NOPE:D_QKP] = (_rope(qg[:, D_NOPE:D_QKP], c, s1, s2) + spare).astype(BF16)
            k_ref[hd, :, 0:D_NOPE] = (knn * gk[:, 0:D_NOPE]).astype(BF16)
            k_ref[hd, :, D_NOPE:D_QKP] = (_rope(krn * gk[:, D_NOPE:D_QKP], c, s1, s2) + kmask).astype(BF16)
            v_ref[hd] = m["v"][:, hd * D_V:(hd + 1) * D_V].astype(BF16)
        cqn_ref[...] = m["cqn"]
        ckvn_ref[...] = m["ckvn"]

    def const(shape):
        return pl.BlockSpec(shape, lambda i: tuple(0 for _ in shape))

    tab = pl.BlockSpec((tm, 128), lambda i: (i % tpe, 0))
    return pl.pallas_call(
        body, name="mla_prep", grid=(rows // tm,),
        in_specs=[pl.BlockSpec((tm, Z_MLA), lambda i: (i, 0)), const((1, Q_RANK)), const((1, KV_RANK)),
                  const((1, D_QKP)), const((1, D_QKP)), const((HEADS * D_QKP, Q_RANK)),
                  const((KV_RANK, HEADS * D_NOPE)), const((KV_RANK, HEADS * D_V)), tab, tab, tab],
        out_specs=[pl.BlockSpec((HEADS, tm, D_QKP), lambda i: (0, i, 0)),
                   pl.BlockSpec((HEADS, tm, D_QKP), lambda i: (0, i, 0)),
                   pl.BlockSpec((HEADS, tm, D_V), lambda i: (0, i, 0)),
                   pl.BlockSpec((tm, Q_RANK), lambda i: (i, 0)),
                   pl.BlockSpec((tm, KV_RANK), lambda i: (i, 0))],
        out_shape=[jax.ShapeDtypeStruct((HEADS, rows, D_QKP), BF16),
                   jax.ShapeDtypeStruct((HEADS, rows, D_QKP), BF16),
                   jax.ShapeDtypeStruct((HEADS, rows, D_V), BF16),
                   jax.ShapeDtypeStruct((rows, Q_RANK), BF16),
                   jax.ShapeDtypeStruct((rows, KV_RANK), BF16)],
        compiler_params=_params(("parallel",)))(z, gql, gkvl, gqh, gkh, wuq, wuk, wuv, *tabs)


Q_BLOCK_ROWS = 528


def _q_block(lp):
    return _row_tile(lp, Q_BLOCK_ROWS)


def _key_end(ext, lp):
    return min(lp, -(-ext // CHUNK) * CHUNK)


def _diag_bias(qb, j0, nk):
    shift = CHUNK.bit_length() - 1
    r = jnp.right_shift(j0 + lax.broadcasted_iota(jnp.int32, (qb, nk), 0), shift)
    c = jnp.right_shift(j0 + lax.broadcasted_iota(jnp.int32, (qb, nk), 1), shift)
    return jnp.where(c <= r, 0.0, NEG_INF)


def _attn_fwd_call(q, k, v, nb, lp, comm=None):
    rows = nb * lp
    qb = _q_block(lp)
    scale = 1.0 / math.sqrt(D_QK)

    def body(q_ref, k_ref, v_ref, o_ref, lse_ref):
        for j in range(lp // qb):
            j0, ext = j * qb, (j + 1) * qb
            kend = _key_end(ext, lp)
            qj = q_ref[0, j0:ext, :]
            sd = _dot_nt(qj, k_ref[0, j0:kend, :]) * scale + _diag_bias(qb, j0, kend - j0)
            mx = jnp.max(sd, axis=-1, keepdims=True)
            if j > 0:
                so = _dot_nt(qj, k_ref[0, 0:j0, :]) * scale
                mx = jnp.maximum(mx, jnp.max(so, axis=-1, keepdims=True))
            pd = jnp.exp(sd - mx)
            l = jnp.sum(pd, axis=-1, keepdims=True)
            o = _dot(pd.astype(BF16), v_ref[0, j0:kend, :])
            if j > 0:
                po = jnp.exp(so - mx)
                l = l + jnp.sum(po, axis=-1, keepdims=True)
                o = o + _dot(po.astype(BF16), v_ref[0, 0:j0, :])
            o_ref[j0:ext, :] = o / l
            lse_ref[0, j0:ext, :] = mx + jnp.log(l)

    return _hosted_call(
        body, name="attn_fwd", grid=(nb, HEADS),
        in_specs=[pl.BlockSpec((1, lp, D_QKP), lambda b, h: (h, b, 0)),
                  pl.BlockSpec((1, lp, D_QKP), lambda b, h: (h, b, 0)),
                  pl.BlockSpec((1, lp, D_V), lambda b, h: (h, b, 0))],
        out_specs=[pl.BlockSpec((lp, D_V), lambda b, h: (b, h)),
                   pl.BlockSpec((1, lp, 1), lambda b, h: (h, b, 0))],
        out_shape=[jax.ShapeDtypeStruct((rows, MLA_W), F32),
                   jax.ShapeDtypeStruct((HEADS, rows, 1), F32)],
        dims=("parallel", "parallel"), args=(q, k, v), comm=comm)


def _attn_bwd_call(q, k, v, o, lse, do, nb, lp, comm=None):
    rows = nb * lp
    qb = _q_block(lp)
    scale = 1.0 / math.sqrt(D_QK)

    def body(q_ref, k_ref, v_ref, o_ref, lse_ref, do_ref, dq_ref, dk_ref, dv_ref, dk_acc, dv_acc):
        dk_acc[...] = jnp.zeros_like(dk_acc)
        dv_acc[...] = jnp.zeros_like(dv_acc)
        for j in range(lp // qb):
            j0, ext = j * qb, (j + 1) * qb
            kend = _key_end(ext, lp)
            dbias = _diag_bias(qb, j0, kend - j0)
            qj = q_ref[0, j0:ext, :]
            doj = do_ref[j0:ext, :]
            delta = jnp.sum(doj * o_ref[j0:ext, :], axis=-1, keepdims=True)
            dob = doj.astype(BF16)
            lse = lse_ref[0, j0:ext, :]
            dq = jnp.zeros((qb, D_QKP), F32)
            for lo, hi, bias in ((j0, kend, dbias), (0, j0, None)):
                if hi == lo:
                    continue
                kk = k_ref[0, lo:hi, :]
                s = _dot_nt(qj, kk) * scale
                p = jnp.exp((s if bias is None else s + bias) - lse)
                dv_acc[lo:hi, :] += _dot_tn(p.astype(BF16), dob)
                dp = _dot_nt(dob, v_ref[0, lo:hi, :])
                ds = (p * (dp - delta) * scale).astype(BF16)
                dq = dq + _dot(ds, kk)
                dk_acc[lo:hi, :] += _dot_tn(ds, qj)
            dq_ref[0, j0:ext, :] = dq.astype(BF16)
        dk_ref[0] = dk_acc[...].astype(BF16)
        dv_ref[0] = dv_acc[...].astype(BF16)

    qspec = pl.BlockSpec((1, lp, D_QKP), lambda b, h: (h, b, 0))
    vspec = pl.BlockSpec((1, lp, D_V), lambda b, h: (h, b, 0))
    ospec = pl.BlockSpec((lp, D_V), lambda b, h: (b, h))
    return _hosted_call(
        body, name="attn_bwd", grid=(nb, HEADS),
        in_specs=[qspec, qspec, vspec, ospec, pl.BlockSpec((1, lp, 1), lambda b, h: (h, b, 0)), ospec],
        out_specs=[qspec, qspec, vspec],
        out_shape=[jax.ShapeDtypeStruct((HEADS, rows, D_QKP), BF16),
                   jax.ShapeDtypeStruct((HEADS, rows, D_QKP), BF16),
                   jax.ShapeDtypeStruct((HEADS, rows, D_V), BF16)],
        scratch_shapes=[pltpu.VMEM((lp, D_QKP), F32), pltpu.VMEM((lp, D_V), F32)],
        dims=("parallel", "parallel"), args=(q, k, v, o, lse, do), comm=comm)


def _lru_gates(u, cw, cb, wa, ba, wx, bx, lam, lp):
    xc = (cw[3:4, :] * u + cw[2:3, :] * pltpu.roll(u, 1, 0) + cw[1:2, :] * pltpu.roll(u, 2, 0)
          + cw[0:1, :] * pltpu.roll(u, 3, 0) + cb)
    xcb = xc.astype(BF16)
    r = _sigmoid(_dot(xcb, wa) + ba)
    i = _sigmoid(_dot(xcb, wx) + bx)
    sp = _softplus_neg(lam)
    la = -C_RGLRU * r * sp
    a = jnp.exp(la)
    x2 = 2.0 * la
    e2 = a * a
    m2 = jnp.maximum(jnp.where(x2 > -0.01, -x2 * (1.0 + 0.5 * x2), 1.0 - e2), 1e-30)
    rs = lax.rsqrt(m2)
    row = lax.broadcasted_iota(jnp.int32, (lp, LRU_TILE), 0)
    first = row == PAD_ROWS
    valid = row >= PAD_ROWS
    mult_eff = jnp.where(first, 1.0, m2 * rs)
    return dict(xc=xc, xcb=xcb, r=r, i=i, sp=sp, a=a, e2=e2, rs=rs, mult_eff=mult_eff, first=first, valid=valid)


def _scan_rows(a, b, a_s, b_s, out_ref, lp, reverse):
    sub = lax.broadcasted_iota(jnp.int32, (lp, LRU_TILE), 0) & 7
    for dist in (1, 2, 4):
        shift = lp - dist if reverse else dist
        keep = (sub + dist <= 7) if reverse else (sub >= dist)
        a_sh = pltpu.roll(a, shift, 0)
        b_sh = pltpu.roll(b, shift, 0)
        b = jnp.where(keep, a * b_sh + b, b)
        a = jnp.where(keep, a * a_sh, a)
    a_s[...] = a
    b_s[...] = b
    n_groups = lp // 8
    edge = 0 if reverse else 7

    def group(gi, carry):
        r0 = pl.multiple_of(((n_groups - 1 - gi) if reverse else gi) * 8, 8)
        a8 = a_s[pl.ds(r0, 8), :]
        b8 = b_s[pl.ds(r0, 8), :]
        out_ref[pl.ds(r0, 8), :] = a8 * carry + b8
        return a8[edge:edge + 1, :] * carry + b8[edge:edge + 1, :]

    lax.fori_loop(0, n_groups, group, jnp.zeros((1, LRU_TILE), F32), unroll=4)


def _lru_specs(lp):
    seq = lambda col0: pl.BlockSpec((lp, LRU_TILE), lambda t, b: (b, col0 + t))
    cw = pl.BlockSpec((1, CONV_K, LRU_TILE), lambda t, b: (t, 0, 0))
    vec = pl.BlockSpec((1, LRU_TILE), lambda t, b: (0, t))
    mat = pl.BlockSpec((1, LRU_TILE, LRU_TILE), lambda t, b: (t, 0, 0))
    return seq, cw, vec, mat


def _lru_fwd_call(z, cw, cb, wa, ba, wx, bx, lam, nb, lp, comm=None):
    rows = nb * lp
    seq, cwspec, vec, mat = _lru_specs(lp)

    def body(u_ref, g_ref, cw_ref, cb_ref, wa_ref, ba_ref, wx_ref, bx_ref, lam_ref, y_ref, hs_ref, a_s, b_s):
        m = _lru_gates(u_ref[...], cw_ref[0], cb_ref[...], wa_ref[0], ba_ref[...], wx_ref[0], bx_ref[...],
                       lam_ref[...], lp)
        a = jnp.where(m["valid"], m["a"], 0.0)
        b = jnp.where(m["valid"], m["mult_eff"] * (m["i"] * m["xc"]), 0.0)
        _scan_rows(a, b, a_s, b_s, hs_ref, lp, reverse=False)
        gl, _ = _gelu(g_ref[...])
        y_ref[...] = hs_ref[...] * gl

    oshape = jax.ShapeDtypeStruct((rows, LRU_W), F32)
    return _hosted_call(
        body, name="lru_fwd", grid=(N_LRU_TILES, nb),
        in_specs=[seq(Z_U // LRU_TILE), seq(Z_G // LRU_TILE), cwspec, vec, mat, vec, mat, vec, vec],
        out_specs=[seq(0), seq(0)], out_shape=[oshape, oshape],
        scratch_shapes=[pltpu.VMEM((lp, LRU_TILE), F32), pltpu.VMEM((lp, LRU_TILE), F32)],
        dims=("parallel", "parallel"), args=(z, z, cw, cb, wa, ba, wx, bx, lam), comm=comm)


def _lru_bwd_call(z, hs, dy, cw, cb, wa, ba, wx, bx, lam, nb, lp, comm=None):
    rows = nb * lp
    seq, cwspec, vec, mat = _lru_specs(lp)

    def body(u_ref, g_ref, hs_ref, dy_ref, cw_ref, cb_ref, wa_ref, ba_ref, wx_ref, bx_ref, lam_ref,
             du_ref, dg_ref, dcw_ref, dcb_ref, dwa_ref, dba_ref, dwx_ref, dbx_ref, dlam_ref, a_s, b_s, d_s):
        b_idx = pl.program_id(1)
        u = u_ref[...]
        cw = cw_ref[0]
        wa, wx = wa_ref[0], wx_ref[0]
        lam = lam_ref[...]
        m = _lru_gates(u, cw, cb_ref[...], wa, ba_ref[...], wx, bx_ref[...], lam, lp)
        gate = g_ref[...]
        gl, th = _gelu(gate)
        dy = dy_ref[...]
        hs = hs_ref[...]
        dg_ref[...] = (dy * hs * _gelu_grad(gate, th)).astype(BF16)
        a_eff = jnp.where(m["valid"], m["a"], 0.0)
        _scan_rows(pltpu.roll(a_eff, lp - 1, 0), dy * gl, a_s, b_s, d_s, lp, reverse=True)
        ds = d_s[...]
        xc, r, i = m["xc"], m["r"], m["i"]
        row = lax.broadcasted_iota(jnp.int32, (lp, LRU_TILE), 0)
        da = ds * jnp.where(row >= 1, pltpu.roll(hs, 1, 0), 0.0)
        db = jnp.where(m["valid"], ds, 0.0)
        di = db * m["mult_eff"] * xc
        dxc = db * m["mult_eff"] * i
        live = m["valid"] & jnp.logical_not(m["first"])
        dm = jnp.where(live, db * i * xc, 0.0)
        dla = da * m["a"] - dm * (m["e2"] * m["rs"])
        dr = dla * (-C_RGLRU * m["sp"])
        dsp = jnp.sum(dla * (-C_RGLRU * r), axis=0, keepdims=True)
        dpr = (dr * r * (1.0 - r))
        dpi = (di * i * (1.0 - i))
        dprb, dpib = dpr.astype(BF16), dpi.astype(BF16)
        dxc = dxc + _dot_nt(dprb, wa) + _dot_nt(dpib, wx)
        du = (cw[3:4, :] * dxc + cw[2:3, :] * pltpu.roll(dxc, lp - 1, 0) + cw[1:2, :] * pltpu.roll(dxc, lp - 2, 0)
              + cw[0:1, :] * pltpu.roll(dxc, lp - 3, 0))
        du_ref[...] = jnp.where(m["valid"], du, 0.0).astype(BF16)
        tap = lax.broadcasted_iota(jnp.int32, (CONV_K, LRU_TILE), 0)
        dcw = jnp.zeros((CONV_K, LRU_TILE), F32)
        for kk in range(CONV_K):
            shifted = u if kk == CONV_K - 1 else pltpu.roll(u, CONV_K - 1 - kk, 0)
            dcw = jnp.where(tap == kk, jnp.sum(dxc * shifted, axis=0, keepdims=True), dcw)
        parts = [(dcw_ref, dcw[None]), (dcb_ref, jnp.sum(dxc, axis=0, keepdims=True)[None]),
                 (dwa_ref, _dot_tn(m["xcb"], dprb)[None]), (dba_ref, jnp.sum(dpr, axis=0, keepdims=True)[None]),
                 (dwx_ref, _dot_tn(m["xcb"], dpib)[None]), (dbx_ref, jnp.sum(dpi, axis=0, keepdims=True)[None]),
                 (dlam_ref, (dsp * (-jax.nn.sigmoid(-lam)))[None])]

        @pl.when(b_idx == 0)
        def _():
            for ref, val in parts:
                ref[...] = val

        @pl.when(b_idx != 0)
        def _():
            for ref, val in parts:
                ref[...] += val

    bshape = jax.ShapeDtypeStruct((rows, LRU_W), BF16)
    vec3 = pl.BlockSpec((1, 1, LRU_TILE), lambda t, b: (t, 0, 0))
    vshape = jax.ShapeDtypeStruct((N_LRU_TILES, 1, LRU_TILE), F32)
    mshape = jax.ShapeDtypeStruct((N_LRU_TILES, LRU_TILE, LRU_TILE), F32)
    return _hosted_call(
        body, name="lru_bwd", grid=(N_LRU_TILES, nb),
        in_specs=[seq(Z_U // LRU_TILE), seq(Z_G // LRU_TILE), seq(0), seq(0), cwspec, vec, mat, vec, mat, vec, vec],
        out_specs=[seq(0), seq(0), cwspec, vec3, mat, vec3, mat, vec3, vec3],
        out_shape=[bshape, bshape, jax.ShapeDtypeStruct((N_LRU_TILES, CONV_K, LRU_TILE), F32), vshape, mshape,
                   vshape, mshape, vshape, vshape],
        scratch_shapes=[pltpu.VMEM((lp, LRU_TILE), F32)] * 3,
        dims=("parallel", "arbitrary"), args=(z, z, hs, dy, cw, cb, wa, ba, wx, bx, lam), comm=comm)


def _mix_out_call(ya, yl, ga, gl, wout, h, tm):
    rows, d = h.shape

    def body(ya_ref, yl_ref, ga_ref, gl_ref, w_ref, h_ref, y_ref, o_ref):
        a = ya_ref[...]
        l = yl_ref[...]
        an = (a * _rms(a, MLA_W) * ga_ref[...]).astype(BF16)
        ln = (l * _rms(l, LRU_W) * gl_ref[...]).astype(BF16)
        y_ref[:, 0:MLA_W] = an
        y_ref[:, MLA_W:MLA_W + LRU_W] = ln
        o_ref[...] = h_ref[...] + _dot(an, w_ref[0:MLA_W, :]) + _dot(ln, w_ref[MLA_W:MLA_W + LRU_W, :])

    half = pl.BlockSpec((tm, MLA_W), lambda i: (i, 0))
    g = pl.BlockSpec((1, MLA_W), lambda i: (0, 0))
    full = pl.BlockSpec((tm, d), lambda i: (i, 0))
    return pl.pallas_call(
        body, name="mix_out", grid=(rows // tm,),
        in_specs=[half, half, g, g, pl.BlockSpec((MLA_W + LRU_W, d), lambda i: (0, 0)), full],
        out_specs=[full, full],
        out_shape=[jax.ShapeDtypeStruct((rows, MLA_W + LRU_W), BF16), jax.ShapeDtypeStruct((rows, d), F32)],
        compiler_params=_params(("parallel",)))(ya, yl, ga, gl, wout, h)


def _mix_out_bwd_call(dhb, wout, ya, yl, ga, gl, tm):
    rows = ya.shape[0]
    d = dhb.shape[1]

    def body(dh_ref, w_ref, ya_ref, yl_ref, ga_ref, gl_ref, dya_ref, dyl_ref, dga_ref, dgl_ref):
        dy = _dot_nt(dh_ref[...], w_ref[...])
        outs = []
        for val, g_ref, lo, out_ref in ((ya_ref[...], ga_ref, 0, dya_ref), (yl_ref[...], gl_ref, MLA_W, dyl_ref)):
            r = _rms(val, MLA_W)
            n = val * r
            dyn = dy[:, lo:lo + MLA_W]
            out_ref[...] = _rms_bwd(dyn * g_ref[...], n, r, MLA_W)
            outs.append(jnp.sum(dyn * n, axis=0, keepdims=True))

        @pl.when(pl.program_id(0) == 0)
        def _():
            dga_ref[...] = outs[0]
            dgl_ref[...] = outs[1]

        @pl.when(pl.program_id(0) != 0)
        def _():
            dga_ref[...] += outs[0]
            dgl_ref[...] += outs[1]

    half = pl.BlockSpec((tm, MLA_W), lambda i: (i, 0))
    g = pl.BlockSpec((1, MLA_W), lambda i: (0, 0))
    return pl.pallas_call(
        body, name="mix_out_bwd", grid=(rows // tm,),
        in_specs=[pl.BlockSpec((tm, d), lambda i: (i, 0)), pl.BlockSpec((MLA_W + LRU_W, d), lambda i: (0, 0)),
                  half, half, g, g],
        out_specs=[half, half, g, g],
        out_shape=[jax.ShapeDtypeStruct((rows, MLA_W), F32), jax.ShapeDtypeStruct((rows, LRU_W), F32),
                   jax.ShapeDtypeStruct((1, MLA_W), F32), jax.ShapeDtypeStruct((1, LRU_W), F32)],
        compiler_params=_params(("arbitrary",)))(dhb, wout, ya, yl, ga, gl)


def _final_call(h, g, target, lp, tm):
    rows, d = h.shape
    tpe = lp // tm

    def body(h_ref, g_ref, t_ref, dh_ref, dhb_ref, dg_ref, loss_ref):
        i = pl.program_id(0)
        x = h_ref[...]
        g = g_ref[...]
        r = _rms(x, d)
        n = x * r
        row = (i % tpe) * tm + lax.broadcasted_iota(jnp.int32, (tm, 1), 0)
        err = jnp.where(row >= CHUNK, n * g - t_ref[...], 0.0)
        dout = err * (1.0 / d)
        dh = _rms_bwd(dout * g, n, r, d)
        dh_ref[...] = dh
        dhb_ref[...] = dh.astype(BF16)
        dg = jnp.sum(dout * n, axis=0, keepdims=True)
        part = jnp.sum(jnp.sum(err * err, axis=1, keepdims=True), axis=0, keepdims=True) * (0.5 / d)
        loss = jnp.broadcast_to(part, (1, 128))

        @pl.when(i == 0)
        def _():
            dg_ref[...] = dg
            loss_ref[...] = loss

        @pl.when(i != 0)
        def _():
            dg_ref[...] += dg
            loss_ref[...] += loss

    full = pl.BlockSpec((tm, d), lambda i: (i, 0))
    return pl.pallas_call(
        body, name="final_loss", grid=(rows // tm,),
        in_specs=[full, pl.BlockSpec((1, d), lambda i: (0, 0)), full],
        out_specs=[full, full, pl.BlockSpec((1, d), lambda i: (0, 0)), pl.BlockSpec((1, 128), lambda i: (0, 0))],
        out_shape=[jax.ShapeDtypeStruct((rows, d), F32), jax.ShapeDtypeStruct((rows, d), BF16),
                   jax.ShapeDtypeStruct((1, d), F32), jax.ShapeDtypeStruct((1, 128), F32)],
        compiler_params=_params(("arbitrary",)))(h, g, target)


def _ffn_dact_call(name, dhb, wd, gate, up, tm, comm=None):
    rows, d = dhb.shape
    ns, fs, _ = wd.shape

    nsub = 2 if tm % 32 == 0 else 1
    sub = tm // nsub

    def body(dh_ref, wd_ref, g_ref, p_ref, dg_ref, dp_ref):
        wd = wd_ref[0]
        for r in range(nsub):
            rs = slice(r * sub, (r + 1) * sub)
            da = (0.5 * _dot_nt(dh_ref[rs, :], wd)).astype(BF16)
            g = g_ref[0, rs, :]
            p = p_ref[0, rs, :]
            sg = jax.nn.sigmoid(g)
            dg_ref[0, rs, :] = (da * p) * (sg * (1.0 + g * (1.0 - sg)))
            dp_ref[0, rs, :] = da * (g * sg)

    aspec = pl.BlockSpec((1, tm, fs), lambda s, i: (s, i, 0))
    oshape = jax.ShapeDtypeStruct((ns, rows, fs), BF16)
    return _hosted_call(
        body, name=name, grid=(ns, rows // tm),
        in_specs=[pl.BlockSpec((tm, d), lambda s, i: (i, 0)), pl.BlockSpec((1, fs, d), lambda s, i: (s, 0, 0)),
                  aspec, aspec],
        out_specs=[aspec, aspec], out_shape=[oshape, oshape],
        dims=("parallel", "parallel"), args=(dhb, wd, gate, up), comm=comm)


def _norm_in_bwd_call(name, pieces, h, g, dres, tm, comm=None, part=(0, 1), prev=None):
    rows, d = h.shape
    npc = len(pieces)
    steps = rows // tm // part[1]
    off = part[0] * steps
    n_prev = 0 if prev is None else 2

    def body(*refs):
        d_refs = refs[0:2 * npc:2]
        w_refs = refs[1:2 * npc:2]
        h_ref, g_ref, dres_ref = refs[2 * npc:2 * npc + 3]
        dh_ref, dhb_ref, dg_ref = refs[2 * npc + 3 + n_prev:]
        du = jnp.zeros((tm, d), F32)
        for d_ref, w_ref in zip(d_refs, w_refs):
            if len(d_ref.shape) == 3:
                for s in range(d_ref.shape[0]):
                    du = du + _dot(d_ref[s], w_ref[s])
            else:
                du = du + _dot(d_ref[...], w_ref[...])
        x = h_ref[...]
        r = _rms(x, d)
        n = x * r
        dh = dres_ref[...] + _rms_bwd(du * g_ref[...], n, r, d)
        dh_ref[...] = dh
        dhb_ref[...] = dh.astype(BF16)
        dg = jnp.sum(du * n, axis=0, keepdims=True)

        @pl.when(pl.program_id(0) == 0)
        def _():
            dg_ref[...] = dg

        @pl.when(pl.program_id(0) != 0)
        def _():
            dg_ref[...] += dg

    in_specs, args = [], []
    for dd, w in pieces:
        if dd.ndim == 3:
            in_specs.append(pl.BlockSpec((dd.shape[0], tm, dd.shape[2]), lambda i: (0, i + off, 0)))
            in_specs.append(_resident(w.shape))
        else:
            in_specs.append(pl.BlockSpec((tm, dd.shape[1]), lambda i: (i + off, 0)))
            in_specs.append(_resident(w.shape))
        args += [dd, w]
    full = pl.BlockSpec((tm, d), lambda i: (i + off, 0))
    gspec = pl.BlockSpec((1, d), lambda i: (0, 0))
    n_in = len(in_specs) + 3
    return _hosted_call(
        body, name=name, grid=(steps,),
        in_specs=in_specs + [full, gspec, full] + _any_specs(n_prev),
        out_specs=[full, full, gspec],
        out_shape=[jax.ShapeDtypeStruct((rows, d), F32), jax.ShapeDtypeStruct((rows, d), BF16),
                   jax.ShapeDtypeStruct((1, d), F32)],
        args=(*args, h, g, dres, *(prev or ())), comm=comm,
        aliases={n_in: 0, n_in + 1: 1} if prev is not None else {})


def _wgrad_call(name, a, b, scale=1.0, comm=None):
    a3, b3 = a.ndim == 3, b.ndim == 3
    ns = a.shape[0] if a3 else (b.shape[0] if b3 else 1)
    rows, m = a.shape[-2:]
    n = b.shape[-1]
    tmm = m if a3 else _col_tile(m, 256)

    def body(a_ref, b_ref, o_ref):
        av = a_ref[0] if a3 else a_ref[...]
        bv = b_ref[0] if b3 else b_ref[...]
        res = _dot_tn(av, bv)
        if scale != 1.0:
            res = res * scale
        if a3 or b3:
            o_ref[0] = res
        else:
            o_ref[...] = res

    aspec = (pl.BlockSpec((1, rows, tmm), lambda s, j: (s, 0, j)) if a3
             else pl.BlockSpec((rows, tmm), lambda s, j: (0, j)))
    bspec = (pl.BlockSpec((1, rows, n), lambda s, j: (s, 0, 0)) if b3
             else pl.BlockSpec((rows, n), lambda s, j: (0, 0)))
    if a3 or b3:
        ospec = pl.BlockSpec((1, tmm, n), lambda s, j: (s, j, 0))
        oshape = jax.ShapeDtypeStruct((ns, m, n), F32)
    else:
        ospec = pl.BlockSpec((tmm, n), lambda s, j: (j, 0))
        oshape = jax.ShapeDtypeStruct((m, n), F32)
    return _hosted_call(
        body, name=name, grid=(ns, m // tmm), in_specs=[aspec, bspec], out_specs=[ospec], out_shape=[oshape],
        dims=("parallel", "parallel"), args=(a, b), comm=comm)[0]


def _mla_prep_bwd_call(z, dq, dk, dv, gql, gkvl, gqh, gkh, wuq, wuk, wuv, tabs, lp, tm, comm=None):
    rows = z.shape[0]
    tpe = lp // tm

    def body(z_ref, dq_ref, dk_ref, dv_ref, gql_ref, gkvl_ref, gqh_ref, gkh_ref, wuq_ref, wuk_ref, wuv_ref,
             c_ref, s1_ref, s2_ref, dz_ref, dqp_ref, dkn_ref, dvv_ref, dgql_ref, dgkvl_ref, dgqh_ref, dgkh_ref):
        gql, gkvl = gql_ref[...], gkvl_ref[...]
        gq, gk = gqh_ref[...], gkh_ref[...]
        wuq, wuk, wuv = wuq_ref[...], wuk_ref[...], wuv_ref[...]
        m = _mla_heads(z_ref[...], gql, gkvl, wuq, wuk, wuv)
        c, s1, s2 = c_ref[...], s1_ref[...], s2_ref[...]
        dgq = jnp.zeros((1, D_QKP), F32)
        dgk = jnp.zeros((1, D_QKP), F32)
        dkr = jnp.zeros((tm, D_QKP - D_NOPE), F32)
        for hd in range(HEADS):
            qn, rqh, knn, krn, rkh = m["heads"][hd]
            dqg = jnp.concatenate([dq_ref[hd, :, 0:D_NOPE].astype(F32),
                                   _rope_t(dq_ref[hd, :, D_NOPE:D_QKP].astype(F32), c, s1, s2)], axis=1)
            dgq = dgq + jnp.sum(dqg * qn, axis=0, keepdims=True)
            dqn = dqg * gq
            dqr = rqh * (dqn - qn * (jnp.sum(dqn * qn, axis=-1, keepdims=True) * (1.0 / D_QK)))
            dqp_ref[:, hd * D_QKP:(hd + 1) * D_QKP] = dqr.astype(BF16)
            kn_full = jnp.concatenate([knn, krn], axis=1)
            dkg = jnp.concatenate([dk_ref[hd, :, 0:D_NOPE].astype(F32),
                                   _rope_t(dk_ref[hd, :, D_NOPE:D_QKP].astype(F32), c, s1, s2)], axis=1)
            dgk = dgk + jnp.sum(dkg * kn_full, axis=0, keepdims=True)
            dkn = dkg * gk
            dkraw = rkh * (dkn - kn_full * (jnp.sum(dkn * kn_full, axis=-1, keepdims=True) * (1.0 / D_QK)))
            dkn_ref[:, hd * D_NOPE:(hd + 1) * D_NOPE] = dkraw[:, 0:D_NOPE].astype(BF16)
            dkr = dkr + dkraw[:, D_NOPE:D_QKP]
            dvv_ref[:, hd * D_V:(hd + 1) * D_V] = dv_ref[hd]
        dcqn = _dot(dqp_ref[...], wuq)
        dckvn = _dot_nt(dkn_ref[...], wuk) + _dot_nt(dvv_ref[...], wuv)
        dz_ref[:, 0:Q_RANK] = _rms_bwd(dcqn * gql, m["nq"], m["rq"], Q_RANK).astype(BF16)
        dz_ref[:, Q_RANK:Z_KR] = _rms_bwd(dckvn * gkvl, m["nkv"], m["rkv"], KV_RANK).astype(BF16)
        dz_ref[:, Z_KR:Z_MLA] = dkr.astype(BF16)
        parts = [(dgql_ref, jnp.sum(dcqn * m["nq"], axis=0, keepdims=True)),
                 (dgkvl_ref, jnp.sum(dckvn * m["nkv"], axis=0, keepdims=True)), (dgqh_ref, dgq), (dgkh_ref, dgk)]

        @pl.when(pl.program_id(0) == 0)
        def _():
            for ref, val in parts:
                ref[...] = val

        @pl.when(pl.program_id(0) != 0)
        def _():
            for ref, val in parts:
                ref[...] += val

    def const(shape):
        return pl.BlockSpec(shape, lambda i: tuple(0 for _ in shape))

    tab = pl.BlockSpec((tm, 128), lambda i: (i % tpe, 0))
    hq = pl.BlockSpec((HEADS, tm, D_QKP), lambda i: (0, i, 0))
    hv = pl.BlockSpec((HEADS, tm, D_V), lambda i: (0, i, 0))

    def rowspec(n):
        return pl.BlockSpec((tm, n), lambda i: (i, 0))

    return _hosted_call(
        body, name="mla_prep_bwd", grid=(rows // tm,),
        in_specs=[rowspec(Z_MLA), hq, hq, hv, const((1, Q_RANK)), const((1, KV_RANK)), const((1, D_QKP)),
                  const((1, D_QKP)), const((HEADS * D_QKP, Q_RANK)), const((KV_RANK, HEADS * D_NOPE)),
                  const((KV_RANK, HEADS * D_V)), tab, tab, tab],
        out_specs=[rowspec(Z_MLA), rowspec(HEADS * D_QKP), rowspec(HEADS * D_NOPE), rowspec(HEADS * D_V),
                   const((1, Q_RANK)), const((1, KV_RANK)), const((1, D_QKP)), const((1, D_QKP))],
        out_shape=[jax.ShapeDtypeStruct((rows, Z_MLA), BF16), jax.ShapeDtypeStruct((rows, HEADS * D_QKP), BF16),
                   jax.ShapeDtypeStruct((rows, HEADS * D_NOPE), BF16), jax.ShapeDtypeStruct((rows, HEADS * D_V), BF16),
                   jax.ShapeDtypeStruct((1, Q_RANK), F32), jax.ShapeDtypeStruct((1, KV_RANK), F32),
                   jax.ShapeDtypeStruct((1, D_QKP), F32), jax.ShapeDtypeStruct((1, D_QKP), F32)],
        args=(z, dq, dk, dv, gql, gkvl, gqh, gkh, wuq, wuk, wuv, *tabs), comm=comm)


def _local_step(h0, target, w, nb, lp, sched=None):
    tm = _row_tile(nb * lp, 1408)
    te = _row_tile(lp, 512)
    tabs = _rope_tables(lp)
    g = {}
    if sched is None:
        host = lambda stage: None
    else:
        sched.g = g
        host = sched.host

    def ffn_fwd(tag, h, split):
        u = _rmsnorm_call(tag + "_norm", h, w[tag + "_norm"], te)
        if split:
            gate = _ffn_gate_call(tag + "_gate", u, w[tag + "_w_gate"], tm, host(tag + "_gate"))
            up, act = _ffn_upact_call(tag + "_upact", u, w[tag + "_w_up"], gate, tm, host(tag + "_upact"))
        else:
            gate, up, act = _ffn_up_call(tag + "_up", u, w[tag + "_w_gate"], w[tag + "_w_up"], tm, host(tag + "_up"))
        return _ffn_down_call(tag + "_down", act, w[tag + "_w_down"], h, te, host(tag + "_down")), (u, gate, up, act)

    def ffn_bwd(tag, h, saved, dh, dhb, split):
        u, gate, up, act = saved
        dgate, dup = _ffn_dact_call(tag + "_dact", dhb, w[tag + "_w_down"], gate, up, tm, host(tag + "_dact"))
        g[tag + "_w_down"] = _wgrad_call(tag + "_dwd", act, dhb, 0.5, host(tag + "_dwd"))
        g[tag + "_w_gate"] = _wgrad_call(tag + "_dwg", dgate, u, 1.0, host(tag + "_dwg"))
        g[tag + "_w_up"] = _wgrad_call(tag + "_dwu", dup, u, 1.0, host(tag + "_dwu"))
        pieces = [(dgate, w[tag + "_w_gate"]), (dup, w[tag + "_w_up"])]
        if not split:
            dh_in, dhb_in, g[tag + "_norm"] = _norm_in_bwd_call(tag + "_din", pieces, h, w[tag + "_norm"], dh, te,
                                                                host(tag + "_din"))
            return dh_in, dhb_in
        dh_a, dhb_a, dg_a = _norm_in_bwd_call(tag + "_din_a", pieces, h, w[tag + "_norm"], dh, te,
                                              host(tag + "_din_a"), part=(0, 2))
        dh_in, dhb_in, dg_b = _norm_in_bwd_call(tag + "_din_b", pieces, h, w[tag + "_norm"], dh, te,
                                                host(tag + "_din_b"), part=(1, 2), prev=(dh_a, dhb_a))
        g[tag + "_norm"] = dg_a + dg_b
        return dh_in, dhb_in

    h1, s1 = ffn_fwd("ffn1", h0, True)
    un = _rmsnorm_call("mix_norm", h1, w["mix_norm"], te)
    z = _mm_call("mix_in", un, w["w_in"], tm, F32)
    mla_w = (w["q_latent_norm"], w["kv_latent_norm"], w["q_head_norm"], w["k_head_norm"], w["w_uq"], w["w_uk"],
             w["w_uv"])
    q, k, v, cqn, ckvn = _mla_prep_call(z, *mla_w, tabs, lp, te)
    o, lse = _attn_fwd_call(q, k, v, nb, lp, host("attn_fwd"))
    lru_w = (w["conv_w"], w["conv_b"], w["gate_a_w"], w["gate_a_b"], w["gate_x_w"], w["gate_x_b"], w["lru_lambda"])
    yl, hs = _lru_fwd_call(z, *lru_w, nb, lp, host("lru_fwd"))
    y, h2 = _mix_out_call(o, yl, w["attn_out_norm"], w["lru_out_norm"], w["w_out"], h1, te)
    h3, s2 = ffn_fwd("ffn2", h2, False)
    dh3, dh3b, g["final_norm"], loss = _final_call(h3, w["final_norm"], target, lp, te)
    g["loss"] = loss

    dh2, dh2b = ffn_bwd("ffn2", h2, s2, dh3, dh3b, False)
    g["w_out"] = _wgrad_call("dw_out", y, dh2b)
    dya, dyl, g["attn_out_norm"], g["lru_out_norm"] = _mix_out_bwd_call(
        dh2b, w["w_out"], o, yl, w["attn_out_norm"], w["lru_out_norm"], te)
    dq, dk, dv = _attn_bwd_call(q, k, v, o, lse, dya, nb, lp, host("attn_bwd"))
    (dz_mla, dqp, dkn, dvv, g["q_latent_norm"], g["kv_latent_norm"], g["q_head_norm"],
     g["k_head_norm"]) = _mla_prep_bwd_call(z, dq, dk, dv, *mla_w, tabs, lp, te, host("mla_prep_bwd"))
    g["w_uq"] = _wgrad_call("dw_uq", dqp, cqn)
    g["w_uk"] = _wgrad_call("dw_uk", ckvn, dkn)
    g["w_uv"] = _wgrad_call("dw_uv", ckvn, dvv)
    (du, dgt, g["conv_w"], g["conv_b"], g["gate_a_w"], g["gate_a_b"], g["gate_x_w"], g["gate_x_b"],
     g["lru_lambda"]) = _lru_bwd_call(z, hs, dyl, *lru_w, nb, lp, host("lru_bwd"))
    win = w["w_in"]
    g["w_in"] = jnp.concatenate(
        [_wgrad_call("dw_in_mla", dz_mla, un), _wgrad_call("dw_in_u", du, un), _wgrad_call("dw_in_g", dgt, un)],
        axis=0)
    dh1, dh1b, g["mix_norm"] = _norm_in_bwd_call(
        "mix_din", [(dz_mla, win[0:Z_MLA]), (du, win[Z_U:Z_G]), (dgt, win[Z_G:Z_W])], h1, w["mix_norm"], dh2, te,
        host("mix_din"))
    dh0, _ = ffn_bwd("ffn1", h0, s1, dh1, dh1b, True)
    return loss, dh0, g


def _place():
    x, y, c = lax.axis_index("x"), lax.axis_index("y"), lax.axis_index("c")
    return x, y, c, [(1 - x, y), (x, 1 - y), (1 - x, 1 - y)]


def _any_specs(n):
    return [pl.BlockSpec(memory_space=pl.ANY)] * n


def _remote(src, dst, sems, k, dev):
    send_sems, recv_sems, base = sems
    return pltpu.make_async_remote_copy(src_ref=src, dst_ref=dst, send_sem=send_sems.at[base + k],
                                        recv_sem=recv_sems.at[base + k], device_id=dev, device_id_type=MESH)


EW_VMEM_BYTES = 24 * 1024 * 1024


def _fit_rows(rows, cols, blocks):
    return _row_tile(rows, max(16, int(EW_VMEM_BYTES // (8 * blocks)) // cols))


class _Geom:
    def __init__(self, n0, n1, blocks=1.0):
        self.n0, self.n1 = n0, n1
        self.axis = 0 if n0 % 32 == 0 else 1
        self.h0, self.h1 = (n0 // 2, n1) if self.axis == 0 else (n0, n1 // 2)
        self.tr = _fit_rows(self.h0, self.h1, blocks)
        self.nblk = self.h0 // self.tr

    def half_ref(self, ref, lead, idx):
        if self.axis == 0:
            return ref.at[(*lead, pl.ds(idx * self.h0, self.h0))]
        return ref.at[(*lead, slice(None), pl.ds(idx * self.h1, self.h1))]

    def half_block(self, lead, i, idx):
        return (*lead, idx * self.nblk + i, 0) if self.axis == 0 else (*lead, i, idx)


class _Comm:
    def __init__(self, ins, out_shapes, aliases, n_sems, start, finish, deliver):
        self.ins, self.out_shapes, self.aliases, self.n_sems = list(ins), list(out_shapes), dict(aliases), n_sems
        self.start, self.finish, self.deliver = start, finish, deliver

    def scratch(self):
        return [pltpu.SemaphoreType.DMA((self.n_sems,)), pltpu.SemaphoreType.DMA((self.n_sems,))]


def _comm_call(name, comm):
    n_in = len(comm.ins)

    def body(*refs):
        ins, outs, sems = refs[:n_in], refs[n_in:-2], (*refs[-2:], 0)
        comm.start(ins, outs, sems)
        comm.finish(ins, outs, sems)

    res = pl.pallas_call(
        body, name=name, out_shape=comm.out_shapes, in_specs=_any_specs(n_in),
        out_specs=_any_specs(len(comm.out_shapes)), input_output_aliases=comm.aliases,
        scratch_shapes=comm.scratch())(*comm.ins)
    return comm.deliver(list(res))


def _hosted_call(body, *, name, grid, in_specs, out_specs, out_shape, args, scratch_shapes=(), dims=None, comm=None,
                 prefetch=None, aliases=None):
    aliases = dict(aliases or {})
    in_specs, out_specs, out_shape = list(in_specs), list(out_specs), list(out_shape)
    n_pre = 0 if prefetch is None else 1

    def call(fn, in_specs, out_specs, out_shape, scratch, aliases, dims, args):
        if prefetch is None:
            return pl.pallas_call(
                fn, name=name, grid=grid, in_specs=in_specs, out_specs=out_specs, out_shape=out_shape,
                scratch_shapes=scratch, input_output_aliases=aliases, compiler_params=_params(dims))(*args)
        spec = pltpu.PrefetchScalarGridSpec(num_scalar_prefetch=1, grid=grid, in_specs=in_specs, out_specs=out_specs,
                                            scratch_shapes=scratch)
        return pl.pallas_call(
            fn, name=name, grid_spec=spec, out_shape=out_shape,
            input_output_aliases={i + 1: o for i, o in aliases.items()}, compiler_params=_params(dims))(prefetch, *args)

    if comm is None:
        return list(call(body, in_specs, out_specs, out_shape, list(scratch_shapes), aliases,
                         dims or ("arbitrary",) * len(grid), args))
    n_in, n_out, n_ci, n_co = len(in_specs), len(out_specs), len(comm.ins), len(comm.out_shapes)

    def wrapped(*refs):
        pre, refs = refs[:n_pre], refs[n_pre:]
        ins, cins = refs[:n_in], refs[n_in:n_in + n_ci]
        outs = refs[n_in + n_ci:n_in + n_ci + n_out]
        couts = refs[n_in + n_ci + n_out:n_in + n_ci + n_out + n_co]
        scratch, sems = refs[n_in + n_ci + n_out + n_co:-2], (*refs[-2:], 0)
        first = functools.reduce(jnp.logical_and, [pl.program_id(k) == 0 for k in range(len(grid))])
        last = functools.reduce(jnp.logical_and, [pl.program_id(k) == grid[k] - 1 for k in range(len(grid))])

        @pl.when(first)
        def _():
            comm.start(cins, couts, sems)

        body(*pre, *ins, *outs, *scratch)

        @pl.when(last)
        def _():
            comm.finish(cins, couts, sems)

    res = call(wrapped, in_specs + _any_specs(n_ci), out_specs + _any_specs(n_co), out_shape + comm.out_shapes,
               list(scratch_shapes) + comm.scratch(),
               {**aliases, **{n_in + i: n_out + o for i, o in comm.aliases.items()}},
               ("arbitrary",) * len(grid), (*args, *comm.ins))
    comm.deliver(list(res[n_out:]))
    return list(res[:n_out])


def _gather_comm(bufs, deliver):
    n = len(bufs)
    geoms = [_Geom(*b.shape[1:]) for b in bufs]

    def first(outs, sems):
        x, y, c, chips = _place()
        cps = []
        for a in range(n):
            mine = geoms[a].half_ref(outs[a], (2 * x + y,), c)
            cps += [_remote(mine, mine, sems, 6 * a + j, (cx, cy, c)) for j, (cx, cy) in enumerate(chips)]
        return cps

    def start(ins, outs, sems):
        for cp in first(outs, sems):
            cp.start()

    def finish(ins, outs, sems):
        x, y, c, chips = _place()
        sib = (x, y, 1 - c)
        passed = []
        for a in range(n):
            for j, (cx, cy) in enumerate(chips):
                land = geoms[a].half_ref(outs[a], (2 * cx + cy,), c)
                _remote(land, land, sems, 6 * a + j, sib).wait_recv()
                cp = _remote(land, land, sems, 6 * a + 3 + j, sib)
                cp.start()
                passed.append(cp)
        for a in range(n):
            for j, (cx, cy) in enumerate(chips):
                land = geoms[a].half_ref(outs[a], (2 * cx + cy,), 1 - c)
                _remote(land, land, sems, 6 * a + 3 + j, sib).wait_recv()
        for cp in first(outs, sems) + passed:
            cp.wait_send()

    return _Comm(bufs, [jax.ShapeDtypeStruct(b.shape, b.dtype) for b in bufs], {a: a for a in range(n)}, 6 * n,
                 start, finish, deliver)


def _reduce_pair_comm(grads, deliver):
    n = len(grads)
    geoms = [_Geom(*a.shape[1:]) for a in grads]

    def copies(ins, outs, sems):
        x, y, c, _ = _place()
        return [_remote(geoms[a].half_ref(ins[a], (slice(None),), 1 - c), outs[a], sems, a, (x, y, 1 - c))
                for a in range(n)]

    def start(ins, outs, sems):
        for cp in copies(ins, outs, sems):
            cp.start()

    def finish(ins, outs, sems):
        cps = copies(ins, outs, sems)
        for cp in cps:
            cp.wait_recv()
        for cp in cps:
            cp.wait_send()

    shapes = [jax.ShapeDtypeStruct((N_SHARD, g.h0, g.h1), a.dtype) for a, g in zip(grads, geoms)]
    return _Comm(grads, shapes, {}, n, start, finish, deliver)


def _reduce_chips_comm(parts, deliver):
    n = len(parts)

    def copies(ins, outs, sems):
        x, y, c, chips = _place()
        return [_remote(ins[a].at[2 * cx + cy], outs[a].at[j], sems, 3 * a + j, (cx, cy, c))
                for a in range(n) for j, (cx, cy) in enumerate(chips)]

    def start(ins, outs, sems):
        for cp in copies(ins, outs, sems):
            cp.start()

    def finish(ins, outs, sems):
        cps = copies(ins, outs, sems)
        for cp in cps:
            cp.wait_recv()
        for cp in cps:
            cp.wait_send()

    shapes = [jax.ShapeDtypeStruct((3,) + a.shape[1:], a.dtype) for a in parts]
    return _Comm(parts, shapes, {}, 3 * n, start, finish, deliver)


def _share_pair_comm(bufs, deliver):
    n = len(bufs)
    geoms = [_Geom(*b.shape) for b in bufs]

    def copies(outs, sems):
        x, y, c, _ = _place()
        cps = []
        for a in range(n):
            mine = geoms[a].half_ref(outs[a], (), c)
            cps.append(_remote(mine, mine, sems, a, (x, y, 1 - c)))
        return cps

    def start(ins, outs, sems):
        for cp in copies(outs, sems):
            cp.start()

    def finish(ins, outs, sems):
        x, y, c, _ = _place()
        for a in range(n):
            land = geoms[a].half_ref(outs[a], (), 1 - c)
            _remote(land, land, sems, a, (x, y, 1 - c)).wait_recv()
        for cp in copies(outs, sems):
            cp.wait_send()

    return _Comm(bufs, [jax.ShapeDtypeStruct(b.shape, b.dtype) for b in bufs], {a: a for a in range(n)}, n,
                 start, finish, deliver)


def _small_comm(pack, deliver):
    r, d = pack.shape

    def copies(ins, outs, sems):
        x, y, c, _ = _place()
        cps = []
        for k in range(1, 8):
            peer = (x ^ ((k >> 2) & 1), y ^ ((k >> 1) & 1), c ^ (k & 1))
            cps.append(_remote(ins[0], outs[0].at[4 * x + 2 * y + c], sems, k - 1, peer))
        return cps

    def start(ins, outs, sems):
        for cp in copies(ins, outs, sems):
            cp.start()

    def finish(ins, outs, sems):
        cps = copies(ins, outs, sems)
        for cp in cps:
            cp.wait_recv()
        for cp in cps:
            cp.wait_send()

    return _Comm([pack], [jax.ShapeDtypeStruct((8, r, d), pack.dtype)], {}, 7, start, finish, deliver)


def _join_comms(comms):
    comms = [c for c in comms if c is not None]
    if len(comms) <= 1:
        return comms[0] if comms else None
    ins, out_shapes, aliases, spans, n_sems = [], [], {}, [], 0
    for c in comms:
        aliases.update({len(ins) + i: len(out_shapes) + o for i, o in c.aliases.items()})
        spans.append((len(ins), len(ins) + len(c.ins), len(out_shapes), len(out_shapes) + len(c.out_shapes), n_sems))
        ins += c.ins
        out_shapes += c.out_shapes
        n_sems += c.n_sems

    def run(which):
        def go(all_ins, all_outs, sems):
            for c, (i0, i1, o0, o1, base) in zip(comms, spans):
                getattr(c, which)(all_ins[i0:i1], all_outs[o0:o1], (sems[0], sems[1], sems[2] + base))
        return go

    def deliver(outs):
        for c, (_, _, o0, o1, _) in zip(comms, spans):
            c.deliver(outs[o0:o1])
        return outs

    return _Comm(ins, out_shapes, aliases, n_sems, run("start"), run("finish"), deliver)


def _ew_call(name, fn, ins, out_dtypes):
    shape = ins[0].shape
    cols = shape[-1]
    rows = 1
    for s_ in shape[:-1]:
        rows *= s_
    ins2 = [a.reshape(rows, cols) for a in ins]
    tr = rows
    for t in range(16, min(rows, max(16, (1 << 19) // cols)) + 1, 16):
        if rows % t == 0:
            tr = t
    no = len(out_dtypes)

    def body(*refs):
        outs = fn(*[r[...] for r in refs[:len(ins2)]])
        for ref, val in zip(refs[len(ins2):], outs):
            ref[...] = val.astype(ref.dtype)

    spec = pl.BlockSpec((tr, cols), lambda i: (i, 0))
    res = pl.pallas_call(
        body, name=name, grid=(rows // tr,), in_specs=[spec] * len(ins2), out_specs=[spec] * no,
        out_shape=[jax.ShapeDtypeStruct((rows, cols), dt) for dt in out_dtypes],
        compiler_params=_params(("parallel",)))(*ins2)
    return [r.reshape(shape) for r in res]


def _adamw_math(w, g, m, v):
    m = ADAM_B1 * m + (1.0 - ADAM_B1) * g
    v = ADAM_B2 * v + (1.0 - ADAM_B2) * (g * g)
    m_hat = m / (1.0 - ADAM_B1 ** ADAM_STEP)
    v_hat = v / (1.0 - ADAM_B2 ** ADAM_STEP)
    delta = -ADAM_LR * (m_hat / (jnp.sqrt(v_hat) + ADAM_EPS) + ADAM_WD * w)
    return delta, m, v


def _adamw_call(name, w, g, m, v):
    return _ew_call(name, _adamw_math, [w, g, m, v], [F32, F32, F32])


def _tiled_call(name, fn, place, grid, in_items, out_items, comm=None):
    ni = len(in_items)

    def body(place_ref, *refs):
        vals = fn(*[r[...] for r in refs[:ni]])
        for ref, val in zip(refs[ni:], vals):
            ref[...] = val.astype(ref.dtype)

    return _hosted_call(
        body, name=name, grid=grid, in_specs=[pl.BlockSpec(blk, imap) for _, blk, imap in in_items],
        out_specs=[pl.BlockSpec(blk, imap) for _, _, blk, imap in out_items],
        out_shape=[jax.ShapeDtypeStruct(shp, dt) for shp, dt, _, _ in out_items],
        args=[a for a, _, _ in in_items], prefetch=place, comm=comm)


def _cast_call(name, place, shards):
    n0, n1 = shards[0].shape
    tr = _fit_rows(n0, n1, 1.5 * len(shards))
    ins = [(a, (tr, n1), lambda i, p: (i, 0)) for a in shards]
    outs = [((N_SHARD, n0, n1), BF16, (1, tr, n1), lambda i, p: (p[0], i, 0)) for _ in shards]
    return _tiled_call(name, lambda *v: [x[None] for x in v], place, (n0 // tr,), ins, outs)


def _pair_sum_call(name, place, fulls, gots):
    k = len(fulls)
    g = _Geom(*fulls[0].shape[1:], blocks=2.5 * k)
    blk = (1, g.tr, g.h1)
    ins = [(a, blk, lambda s, i, p: g.half_block((s,), i, p[1])) for a in fulls]
    ins += [(a, blk, lambda s, i, p: (s, i, 0)) for a in gots]
    outs = [((N_SHARD, g.h0, g.h1), BF16, blk, lambda s, i, p: (s, i, 0)) for _ in fulls]
    return _tiled_call(name, lambda *v: [v[j] + v[k + j] for j in range(k)], place, (N_SHARD, g.nblk), ins, outs)


def _chip_sum_call(name, place, fulls, gots, recvs, comm=None):
    k = len(fulls)
    g = _Geom(*fulls[0].shape[1:], blocks=4.5 * k)
    blk = (1, g.tr, g.h1)
    ins = [(a, blk, lambda i, p: g.half_block((p[0],), i, p[1])) for a in fulls]
    ins += [(a, blk, lambda i, p: (p[0], i, 0)) for a in gots]
    ins += [(a, (3, g.tr, g.h1), lambda i, p: (0, i, 0)) for a in recvs]
    outs = [((g.n0, g.n1), F32, (g.tr, g.h1), lambda i, p: g.half_block((), i, p[1])) for _ in fulls]

    def fn(*v):
        res = []
        for j in range(k):
            r = v[2 * k + j].astype(F32)
            res.append(v[j][0] + v[k + j][0] + r[0] + r[1] + r[2])
        return res

    return _tiled_call(name, fn, place, (g.nblk,), ins, outs, comm)


def _adamw_group_call(name, ws, gs, ms, vs, comm=None):
    k = len(ws)
    n0, n1 = ws[0].shape
    tr = _fit_rows(n0, n1, 8 * k)
    spec = pl.BlockSpec((tr, n1), lambda i: (i, 0))

    def body(*refs):
        for j in range(k):
            g = refs[k + j][...]
            delta, m, vv = _adamw_math(refs[j][...], g, refs[2 * k + j][...], refs[3 * k + j][...])
            for ref, val in zip(refs[4 * k + 4 * j:4 * k + 4 * j + 4], (g, delta, m, vv)):
                ref[...] = val

    flat = _hosted_call(
        body, name=name, grid=(n0 // tr,), in_specs=[spec] * (4 * k), out_specs=[spec] * (4 * k),
        out_shape=[jax.ShapeDtypeStruct((n0, n1), F32)] * (4 * k), dims=("parallel",),
        args=(*ws, *gs, *ms, *vs), comm=comm)
    return [flat[4 * j:4 * j + 4] for j in range(k)]


def _small_update_call(me, early, own_early, late, own_late, wp, mp, vp):
    nd, r, d = early.shape

    def body(me_ref, e_ref, oe_ref, l_ref, ol_ref, w_ref, m_ref, v_ref, gs_ref, d_ref, nm_ref, nv_ref):
        mine = me_ref[0]

        def total(g_ref, own_ref):
            acc = None
            for k in range(nd):
                part = jnp.where(mine == k, own_ref[...], g_ref[k])
                acc = part if acc is None else acc + part
            return acc

        gs = total(e_ref, oe_ref)
        ls = total(l_ref, ol_ref)
        gs_ref[...] = gs
        first = gs[0:8] + ls[0:8]
        gs_ref[0:8, :] = first
        gs_ref[ROW_META:ROW_META + N_META, :] = gs[ROW_META:ROW_META + N_META] + ls[8:8 + N_META]
        grads = jnp.concatenate([first, gs[8:SMALL_ADAM_ROWS]], axis=0)
        delta, m, v = _adamw_math(w_ref[...], grads, m_ref[...], v_ref[...])
        d_ref[...] = delta
        nm_ref[...] = m
        nv_ref[...] = v

    vm = pl.BlockSpec(memory_space=pltpu.VMEM)
    ashape = jax.ShapeDtypeStruct((SMALL_ADAM_ROWS, d), F32)
    return pl.pallas_call(
        body, name="small_update", in_specs=[pl.BlockSpec(memory_space=pltpu.SMEM)] + [vm] * 7, out_specs=[vm] * 4,
        out_shape=[jax.ShapeDtypeStruct((r, d), F32), ashape, ashape, ashape],
        compiler_params=pltpu.CompilerParams(vmem_limit_bytes=VMEM_LIMIT_BYTES))(
            me, early, own_early, late, own_late, wp, mp, vp)


SMALL_NAMES = ["ffn1_norm", "mix_norm", "ffn2_norm", "final_norm", "q_latent_norm", "kv_latent_norm",
               "q_head_norm", "k_head_norm", "conv_b", "gate_a_b", "gate_x_b", "lru_lambda", "attn_out_norm",
               "lru_out_norm"]
ROW_CONV_W = 14
ROW_GATE_A = 16
ROW_GATE_X = 48
ROW_META = 80
ROW_LOSS = 96


def _row(a):
    flat = a.reshape(1, -1)
    return jnp.pad(flat, ((0, 0), (0, D_MODEL - flat.shape[1])))


def _pack_small(t, rows):
    parts = [_row(t[nm]) for nm in SMALL_NAMES]
    parts.append(t["conv_w"].reshape(2, D_MODEL))
    parts.append(t["gate_a_w"].reshape(32, D_MODEL))
    parts.append(t["gate_x_w"].reshape(32, D_MODEL))
    p = jnp.concatenate(parts, axis=0)
    return jnp.pad(p, ((0, rows - p.shape[0]), (0, 0)))


def _early_pack(g):
    gs = {nm: g.get(nm, jnp.zeros((1, D_MODEL), F32)) for nm in SMALL_NAMES}
    gs["q_head_norm"] = g["q_head_norm"][:, 0:D_QK]
    gs["k_head_norm"] = g["k_head_norm"][:, 0:D_QK]
    for nm in ("conv_b", "gate_a_b", "gate_x_b", "lru_lambda"):
        gs[nm] = g[nm].reshape(1, LRU_W)
    gs["conv_w"] = g["conv_w"].transpose(1, 0, 2).reshape(CONV_K, LRU_W)
    gs["gate_a_w"] = _gate_blocks(g["gate_a_w"])
    gs["gate_x_w"] = _gate_blocks(g["gate_x_w"])
    return jnp.concatenate([_pack_small(gs, ROW_META), jnp.zeros((N_META, D_MODEL), F32), _row(g["loss"][:, 0:1]),
                            jnp.zeros((SMALL_ROWS - ROW_LOSS - 1, D_MODEL), F32)], axis=0)


def _unpack_small(p, like):
    out = {}
    for k, nm in enumerate(SMALL_NAMES):
        out[nm] = p[k, 0:like[nm].size].reshape(like[nm].shape)
    out["gate_a_w"] = p[ROW_GATE_A:ROW_GATE_A + 32].reshape(like["gate_a_w"].shape)
    out["gate_x_w"] = p[ROW_GATE_X:ROW_GATE_X + 32].reshape(like["gate_x_w"].shape)
    return out


def _gate_dense(wg):
    w4 = wg[0].reshape(N_LRU_TILES, 2, 64, 64)
    zero = jnp.zeros((N_LRU_TILES, 64, 64), wg.dtype)
    top = jnp.concatenate([w4[:, 0], zero], axis=2)
    bot = jnp.concatenate([zero, w4[:, 1]], axis=2)
    return jnp.concatenate([top, bot], axis=1).astype(BF16)


def _gate_blocks(dw):
    return jnp.stack([dw[:, 0:64, 0:64], dw[:, 64:128, 64:128]], axis=1).reshape(8, 64, 64)


BIG_NAMES = ["ffn1_w_gate", "ffn1_w_up", "ffn1_w_down", "w_in", "w_uq", "w_uk", "w_uv", "w_out", "ffn2_w_gate",
             "ffn2_w_up", "ffn2_w_down"]
BIG_GROUPS = [["ffn1_w_gate", "ffn1_w_up", "ffn1_w_down", "ffn2_w_gate", "ffn2_w_up", "ffn2_w_down"], ["w_in"],
              ["w_uq"], ["w_uk", "w_uv"], ["w_out"]]
TRANSPOSED = ("ffn1_w_gate", "ffn1_w_up", "ffn2_w_gate", "ffn2_w_up", "w_in", "w_uq")


def _to2d(nm, a):
    return a[0].T if nm in TRANSPOSED else a[0]


def _from2d(nm, a):
    return (a.T if nm in TRANSPOSED else a)[None]


WEIGHT_NAMES = ["meta_tokens", "ffn1_norm", "ffn1_w_gate", "ffn1_w_up", "ffn1_w_down", "mix_norm", "w_in",
                "q_latent_norm", "w_uq", "kv_latent_norm", "w_uk", "w_uv", "q_head_norm", "k_head_norm", "conv_w",
                "conv_b", "gate_a_w", "gate_a_b", "gate_x_w", "gate_x_b", "lru_lambda", "attn_out_norm",
                "lru_out_norm", "w_out", "ffn2_norm", "ffn2_w_gate", "ffn2_w_up", "ffn2_w_down", "final_norm"]


def _weight_from(nm, slots):
    if nm == "w_in":
        win = slots.reshape(IN_WIDTH, D_MODEL)
        return jnp.concatenate([win[0:Z_KR + D_ROPE], jnp.zeros((128 - D_ROPE, D_MODEL), BF16),
                                win[Z_KR + D_ROPE:]], axis=0)
    if nm == "w_uq":
        return jnp.pad(slots, ((0, 0), (0, D_QKP - D_QK), (0, 0))).reshape(HEADS * D_QKP, Q_RANK)
    if nm in ("w_uk", "w_uv"):
        return slots.transpose(1, 0, 2).reshape(KV_RANK, HEADS * D_NOPE)
    if nm == "w_out":
        return slots.reshape(D_MODEL, D_MODEL)
    return slots


def _small_weights(p, small):
    w = {nm: p[nm] for nm in SMALL_NAMES}
    w["q_head_norm"] = jnp.pad(p["q_head_norm"], ((0, 0), (0, D_QKP - D_QK)))
    w["k_head_norm"] = jnp.pad(p["k_head_norm"], ((0, 0), (0, D_QKP - D_QK)))
    w["conv_w"] = small[:, N_META:N_META + 2, :].reshape(N_SHARD, CONV_K, LRU_TILE)
    w["gate_a_w"] = _gate_dense(p["gate_a_w"])
    w["gate_x_w"] = _gate_dense(p["gate_x_w"])
    meta = small[:, 0:N_META, :].transpose(1, 0, 2).reshape(N_META, D_MODEL)
    return w, meta


def _full_weights(p, gathered, small):
    w, meta = _small_weights(p, small)
    w.update({nm: _weight_from(nm, gathered[nm]) for nm in BIG_NAMES})
    return w, meta


def _shard_grad(nm, g):
    if nm == "w_in":
        return jnp.concatenate([g[0:Z_KR + D_ROPE], g[Z_MLA:]], axis=0).reshape(N_SHARD, IN_WIDTH // N_SHARD, D_MODEL)
    if nm == "w_uq":
        return g.reshape(HEADS, D_QKP, Q_RANK)[:, 0:D_QK, :]
    if nm in ("w_uk", "w_uv"):
        return g.reshape(KV_RANK, HEADS, D_NOPE).transpose(1, 0, 2)
    if nm == "w_out":
        return g.reshape(N_SHARD, D_MODEL // N_SHARD, D_MODEL)
    return g


def _shard_grads(g):
    return {nm: _shard_grad(nm, g[nm]) for nm in BIG_NAMES}


GATHER_FIRST = ["ffn1_w_gate"]
GATHER_AT = {"ffn1_gate": ["ffn1_w_up"], "ffn1_upact": ["ffn1_w_down"],
             "ffn1_down": ["w_in", "w_uq", "w_uk", "w_uv", "w_out"], "attn_fwd": ["ffn2_w_down", "ffn2_w_gate"],
             "lru_fwd": ["ffn2_w_up"]}
PAIR_AT = [("ffn2_din", ["ffn2_w_gate", "ffn2_w_up", "ffn2_w_down"]),
           ("mix_din", ["w_out", "w_uq", "w_uk", "w_uv", "w_in"]),
           ("ffn1_dwg", ["ffn1_w_down"]), ("ffn1_dwu", ["ffn1_w_gate"]), ("ffn1_din_a", ["ffn1_w_up"])]
CHIPS_AT = [("attn_bwd", ["ffn2_w_down", "ffn2_w_gate"]), ("mla_prep_bwd", ["ffn2_w_up"]),
            ("ffn1_dact", ["w_out", "w_uq", "w_uk", "w_uv", "w_in"]),
            ("ffn1_dwu", ["ffn1_w_down"]), ("ffn1_din_a", ["ffn1_w_gate"]), ("ffn1_din_b", ["ffn1_w_up"])]
SHARE_EARLY_GROUPS, SHARE_EARLY_AT = 3, "ffn1_dwd"
SMALL_EARLY_AT = "mix_din"


def _same_shape_groups(names):
    return [[nm for nm in grp if nm in names] for grp in BIG_GROUPS if any(nm in names for nm in grp)]


class _Sched:
    def __init__(self, place, w, slots):
        self.place, self.w, self.slots = place, w, slots
        self.g = None
        self.sharded, self.from_pair, self.chip_bf16, self.from_chips = {}, {}, {}, {}
        self.early = self.early_all = None
        self.shared = {}

    def host(self, stage):
        comms = []
        if stage in GATHER_AT:
            comms.append(self.gather(GATHER_AT[stage]))
        comms += [self.chips(names) for at, names in CHIPS_AT if at == stage]
        comms += [self.pair(names) for at, names in PAIR_AT if at == stage]
        if stage == SMALL_EARLY_AT:
            comms.append(self.small_early())
        if stage == SHARE_EARLY_AT:
            comms.append(self.share_early())
        return _join_comms(comms)

    def small_early(self):
        self.early = _early_pack(self.g)

        def deliver(outs):
            self.early_all = outs[0]
            return outs

        return _small_comm(self.early, deliver)

    def gather(self, names):
        def deliver(outs):
            self.w.update({nm: _weight_from(nm, o) for nm, o in zip(names, outs)})
            return outs

        return _gather_comm([self.slots[nm] for nm in names], deliver)

    def pair(self, names):
        self.sharded.update({nm: _shard_grad(nm, self.g[nm]) for nm in names})

        def deliver(outs):
            self.from_pair.update(zip(names, outs))
            for grp in _same_shape_groups(names):
                sums = _pair_sum_call("pair_sum_" + grp[0], self.place, [self.sharded[nm] for nm in grp],
                                      [self.from_pair[nm] for nm in grp])
                self.chip_bf16.update(zip(grp, sums))
            return outs

        return _reduce_pair_comm([self.sharded[nm] for nm in names], deliver)

    def chips(self, names):
        def deliver(outs):
            self.from_chips.update(zip(names, outs))
            return outs

        return _reduce_chips_comm([self.chip_bf16[nm] for nm in names], deliver)

    def chip_sums(self, names, comm=None):
        out = {}
        for grp in _same_shape_groups(names):
            sums = _chip_sum_call("chip_sum_" + grp[0], self.place, [self.sharded[nm] for nm in grp],
                                  [self.from_pair[nm] for nm in grp], [self.from_chips[nm] for nm in grp], comm)
            comm = None
            out.update(zip(grp, sums))
        return out

    def share_early(self):
        names = [nm for at, grp in CHIPS_AT[:SHARE_EARLY_GROUPS] for nm in grp]
        mine = self.chip_sums(names)
        return _share_pair_comm([mine[nm] for nm in names], lambda o: self.shared.update(zip(names, o)))


def kernel(x, meta_tokens, ffn1_norm, ffn1_w_gate, ffn1_w_up, ffn1_w_down, mix_norm, w_in, q_latent_norm, w_uq, kv_latent_norm, w_uk, w_uv, q_head_norm, k_head_norm, conv_w, conv_b, gate_a_w, gate_a_b, gate_x_w, gate_x_b, lru_lambda, attn_out_norm, lru_out_norm, w_out, ffn2_norm, ffn2_w_gate, ffn2_w_up, ffn2_w_down, final_norm, loss_target, m_meta_tokens, m_ffn1_norm, m_ffn1_w_gate, m_ffn1_w_up, m_ffn1_w_down, m_mix_norm, m_w_in, m_q_latent_norm, m_w_uq, m_kv_latent_norm, m_w_uk, m_w_uv, m_q_head_norm, m_k_head_norm, m_conv_w, m_conv_b, m_gate_a_w, m_gate_a_b, m_gate_x_w, m_gate_x_b, m_lru_lambda, m_attn_out_norm, m_lru_out_norm, m_w_out, m_ffn2_norm, m_ffn2_w_gate, m_ffn2_w_up, m_ffn2_w_down, m_final_norm, v_meta_tokens, v_ffn1_norm, v_ffn1_w_gate, v_ffn1_w_up, v_ffn1_w_down, v_mix_norm, v_w_in, v_q_latent_norm, v_w_uq, v_kv_latent_norm, v_w_uk, v_w_uv, v_q_head_norm, v_k_head_norm, v_conv_w, v_conv_b, v_gate_a_w, v_gate_a_b, v_gate_x_w, v_gate_x_b, v_lru_lambda, v_attn_out_norm, v_lru_out_norm, v_w_out, v_ffn2_norm, v_ffn2_w_gate, v_ffn2_w_up, v_ffn2_w_down, v_final_norm):
    args = locals()
    p = {nm: args[nm] for nm in WEIGHT_NAMES}
    mom = {nm: args["m_" + nm] for nm in WEIGHT_NAMES}
    var = {nm: args["v_" + nm] for nm in WEIGHT_NAMES}
    nb, seq, d = x.shape
    lp = CHUNK + seq
    xi, yi, ci = lax.axis_index("x"), lax.axis_index("y"), lax.axis_index("c")
    chip = 2 * xi + yi

    place = jnp.stack([chip, ci]).astype(jnp.int32)
    p2 = {nm: _to2d(nm, p[nm]) for nm in BIG_NAMES}
    m2 = {nm: _to2d(nm, mom[nm]) for nm in BIG_NAMES}
    v2 = {nm: _to2d(nm, var[nm]) for nm in BIG_NAMES}

    slots = {}
    for grp in BIG_GROUPS:
        for nm, buf in zip(grp, _cast_call("cast_" + grp[0], place, [p2[nm] for nm in grp])):
            slots[nm] = buf
    small_shard = jnp.concatenate(
        [meta_tokens, conv_w[0].reshape(2, 2 * LRU_TILE), jnp.zeros((14, 2 * LRU_TILE), F32)], axis=0)
    small_slots = lax.dynamic_update_slice(jnp.zeros((N_SHARD,) + small_shard.shape, F32), small_shard[None],
                                           (chip, 0, 0))
    first = _comm_call("gather_first", _gather_comm([slots[nm] for nm in GATHER_FIRST] + [small_slots], lambda o: o))
    w, meta = _small_weights(p, first[-1])
    w.update({nm: _weight_from(nm, o) for nm, o in zip(GATHER_FIRST, first[:-1])})
    sched = _Sched(place, w, slots)

    h0 = jnp.concatenate(
        [jnp.zeros((nb, PAD_ROWS, d), F32), jnp.broadcast_to(meta[None], (nb, N_META, d)), x], axis=1)
    target = jnp.pad(loss_target, ((0, 0), (CHUNK, 0), (0, 0)))
    loss_part, dh0, g = _local_step(h0.reshape(nb * lp, d), target.reshape(nb * lp, d), w, nb, lp, sched)
    dh0 = dh0.reshape(nb, lp, d)
    grad_x = dh0[:, CHUNK:, :]

    late = jnp.concatenate([g["ffn1_norm"], g["mix_norm"], jnp.zeros((6, D_MODEL), F32),
                            jnp.sum(dh0[:, PAD_ROWS:CHUNK, :], axis=0)], axis=0)
    shared = sched.shared
    rest = [nm for at, names in CHIPS_AT[SHARE_EARLY_GROUPS:] if at is not None for nm in names]
    last = [nm for at, names in CHIPS_AT if at is None for nm in names]
    late_box = {}
    mine = sched.chip_sums(rest, _small_comm(late, lambda o: late_box.update(all=o[0])))
    share = _share_pair_comm([mine[nm] for nm in rest], lambda o: shared.update(zip(rest, o)))
    _comm_call("share_pair", _join_comms([share, sched.chips(last) if last else None]))
    if last:
        mine = sched.chip_sums(last)
        _comm_call("share_last", _share_pair_comm([mine[nm] for nm in last], lambda o: shared.update(zip(last, o))))
    late_all = late_box["all"]
    small_like = {nm: p[nm] for nm in SMALL_NAMES + ["gate_a_w", "gate_x_w"]}

    def pack_w(t):
        tt = {nm: t[nm] for nm in SMALL_NAMES + ["gate_a_w", "gate_x_w"]}
        tt["conv_w"] = jnp.zeros((CONV_K, LRU_W), F32)
        return _pack_small(tt, SMALL_ADAM_ROWS)

    me = (4 * xi + 2 * yi + ci).astype(jnp.int32).reshape(1)
    gsum, dsm, msm, vsm = _small_update_call(me, sched.early_all, sched.early, late_all, late, pack_w(p),
                                             pack_w(mom), pack_w(var))
    grads = _unpack_small(gsum, small_like)
    delta = _unpack_small(dsm, small_like)
    new_m = _unpack_small(msm, small_like)
    new_v = _unpack_small(vsm, small_like)
    loss = gsum[ROW_LOSS, 0]
    gmeta = gsum[ROW_META:ROW_META + N_META].reshape(N_META, N_SHARD, D_MODEL // N_SHARD)
    grads["meta_tokens"] = lax.dynamic_index_in_dim(gmeta, chip, axis=1, keepdims=False)
    gconv = gsum[ROW_CONV_W:ROW_CONV_W + 2].reshape(CONV_K, N_SHARD, LRU_TILE)
    grads["conv_w"] = lax.dynamic_index_in_dim(gconv, chip, axis=1, keepdims=False)[None]
    for nm in ("meta_tokens", "conv_w"):
        delta[nm], new_m[nm], new_v[nm] = _adamw_call("adamw_" + nm, p[nm], grads[nm], mom[nm], var[nm])

    for names in BIG_GROUPS:
        res = _adamw_group_call("adamw_" + names[0], [p2[nm] for nm in names], [shared[nm] for nm in names],
                                [m2[nm] for nm in names], [v2[nm] for nm in names])
        for nm, (gg, dd, mm, vv) in zip(names, res):
            grads[nm], delta[nm], new_m[nm], new_v[nm] = (_from2d(nm, t) for t in (gg, dd, mm, vv))

    return (loss, grad_x, *[grads[nm] for nm in WEIGHT_NAMES], *[delta[nm] for nm in WEIGHT_NAMES],
            *[new_m[nm] for nm in WEIGHT_NAMES], *[new_v[nm] for nm in WEIGHT_NAMES])
```

```python
import functools
import math

import jax
import jax.numpy as jnp
import numpy as np
from jax import lax
from jax.experimental import pallas as pl
from jax.experimental.pallas import tpu as pltpu

F32 = jnp.float32
BF16 = jnp.bfloat16
MESH = pl.DeviceIdType.MESH

D_MODEL = 1024
N_META = 16
CHUNK = 64
PAD_ROWS = CHUNK - N_META
HEADS = 4
D_NOPE = 128
D_ROPE = 64
D_QK = D_NOPE + D_ROPE
D_QKP = 256
D_V = 128
KV_RANK = 256
Q_RANK = 384
MLA_W = HEADS * D_V
LRU_W = 512
LRU_TILE = 128
N_LRU_TILES = LRU_W // LRU_TILE
CONV_K = 4
C_RGLRU = 8.0
ROPE_THETA = 10000.0
D_FF = 2816
N_SHARD = 4
EPS = 1e-6
NEG_INF = -1e30
Z_KR = Q_RANK + KV_RANK
Z_MLA = Z_KR + 128
Z_U = Z_MLA
Z_G = Z_U + LRU_W
Z_W = Z_G + LRU_W
IN_WIDTH = Q_RANK + KV_RANK + D_ROPE + 2 * LRU_W

ADAM_LR = 0.001
ADAM_B1 = 0.9
ADAM_B2 = 0.999
ADAM_EPS = 1e-08
ADAM_WD = 0.01
ADAM_STEP = 10

VMEM_LIMIT_BYTES = 56 * 1024 * 1024
SMALL_ROWS = 104
SMALL_ADAM_ROWS = 80


def _params(sem):
    return pltpu.CompilerParams(dimension_semantics=sem, vmem_limit_bytes=VMEM_LIMIT_BYTES)


def _resident(shape):
    return pl.BlockSpec(tuple(shape), lambda i: (0,) * len(shape), pipeline_mode=pl.Buffered(1))


def _row_tile(rows, target):
    best = 16
    for t in range(16, min(rows, target) + 1, 16):
        if rows % t == 0:
            best = t
    return best


def _col_tile(cols, target):
    best = cols
    for t in range(128, min(cols, target) + 1, 128):
        if cols % t == 0:
            best = t
    return best


def _dot(a, b):
    return jnp.dot(a, b, preferred_element_type=F32)


def _dot_nt(a, b):
    return lax.dot_general(a, b, (((1,), (1,)), ((), ())), preferred_element_type=F32)


def _dot_tn(a, b):
    return lax.dot_general(a, b, (((0,), (0,)), ((), ())), preferred_element_type=F32)


def _rms(x, n):
    return lax.rsqrt(jnp.sum(x * x, axis=-1, keepdims=True) * (1.0 / n) + EPS)


def _rms_bwd(dn, nrm, r, n):
    return r * (dn - nrm * (jnp.sum(dn * nrm, axis=-1, keepdims=True) * (1.0 / n)))


def _gelu(x):
    k = math.sqrt(2.0 / math.pi)
    t = jnp.tanh(k * (x + 0.044715 * x * x * x))
    return 0.5 * x * (1.0 + t), t


def _gelu_grad(x, t):
    k = math.sqrt(2.0 / math.pi)
    return 0.5 * (1.0 + t) + 0.5 * x * (1.0 - t * t) * k * (1.0 + 3.0 * 0.044715 * x * x)


def _sigmoid(x):
    return 0.5 + 0.5 * jnp.tanh(0.5 * x)


def _softplus_neg(lam):
    e = jnp.exp(-jnp.abs(lam))
    log1p = jnp.where(e < 0.01, e * (1.0 - e * (0.5 - e * (1.0 / 3 - e * 0.25))), jnp.log(1.0 + e))
    return jnp.maximum(-lam, 0.0) + log1p


def _rope(t, c, s1, s2):
    return t * c + pltpu.roll(t, 96, 1) * s1 + pltpu.roll(t, 32, 1) * s2


def _rope_t(d, c, s1, s2):
    return d * c + pltpu.roll(d * s1, 32, 1) + pltpu.roll(d * s2, 96, 1)


def _rope_tables(lp):
    pos = (np.arange(lp, dtype=np.int32) - PAD_ROWS).astype(np.float32)
    inv_freq = (ROPE_THETA ** (-np.arange(0, D_ROPE // 2, dtype=np.float32) / (D_ROPE // 2))).astype(np.float32)
    ang = (pos[:, None] * inv_freq[None, :]).astype(np.float32).astype(np.float64)
    cos, sin = np.cos(ang).astype(np.float32), np.sin(ang).astype(np.float32)
    z = np.zeros_like(cos)
    return (jnp.asarray(np.concatenate([cos, cos, z, z], 1)), jnp.asarray(np.concatenate([-sin, z, z, z], 1)),
            jnp.asarray(np.concatenate([z, sin, z, z], 1)))


def _rmsnorm_call(name, h, g, tm):
    rows, d = h.shape

    def body(h_ref, g_ref, o_ref):
        x = h_ref[...]
        o_ref[...] = (x * _rms(x, d) * g_ref[...]).astype(BF16)

    return pl.pallas_call(
        body, name=name, grid=(rows // tm,),
        in_specs=[pl.BlockSpec((tm, d), lambda i: (i, 0)), pl.BlockSpec((1, d), lambda i: (0, 0))],
        out_specs=pl.BlockSpec((tm, d), lambda i: (i, 0)),
        out_shape=jax.ShapeDtypeStruct((rows, d), BF16),
        compiler_params=_params(("parallel",)))(h, g)


def _ffn_up_call(name, u, wg, wu, tm, comm=None):
    rows, d = u.shape
    ns, fs, _ = wg.shape

    def body(u_ref, wg_ref, wu_ref, g_ref, p_ref, a_ref):
        uu = u_ref[...]
        g = _dot_nt(uu, wg_ref[0])
        p = _dot_nt(uu, wu_ref[0])
        g_ref[0] = g.astype(BF16)
        p_ref[0] = p.astype(BF16)
        a_ref[0] = (g * jax.nn.sigmoid(g) * p).astype(BF16)

    wspec = pl.BlockSpec((1, fs, d), lambda s, i: (s, 0, 0))
    ospec = pl.BlockSpec((1, tm, fs), lambda s, i: (s, i, 0))
    oshape = jax.ShapeDtypeStruct((ns, rows, fs), BF16)
    return _hosted_call(
        body, name=name, grid=(ns, rows // tm),
        in_specs=[pl.BlockSpec((tm, d), lambda s, i: (i, 0)), wspec, wspec],
        out_specs=[ospec, ospec, ospec], out_shape=[oshape, oshape, oshape],
        dims=("parallel", "parallel"), args=(u, wg, wu), comm=comm)


def _ffn_gate_call(name, u, wg, tm, comm=None):
    rows, d = u.shape
    ns, fs, _ = wg.shape

    def body(u_ref, wg_ref, g_ref):
        g_ref[0] = _dot_nt(u_ref[...], wg_ref[0]).astype(BF16)

    return _hosted_call(
        body, name=name, grid=(ns, rows // tm),
        in_specs=[pl.BlockSpec((tm, d), lambda s, i: (i, 0)), pl.BlockSpec((1, fs, d), lambda s, i: (s, 0, 0))],
        out_specs=[pl.BlockSpec((1, tm, fs), lambda s, i: (s, i, 0))],
        out_shape=[jax.ShapeDtypeStruct((ns, rows, fs), BF16)],
        dims=("parallel", "parallel"), args=(u, wg), comm=comm)[0]


def _ffn_upact_call(name, u, wu, gate, tm, comm=None):
    rows, d = u.shape
    ns, fs, _ = wu.shape

    def body(u_ref, wu_ref, g_ref, p_ref, a_ref):
        p = _dot_nt(u_ref[...], wu_ref[0])
        g = g_ref[0].astype(F32)
        p_ref[0] = p.astype(BF16)
        a_ref[0] = (g * jax.nn.sigmoid(g) * p).astype(BF16)

    ospec = pl.BlockSpec((1, tm, fs), lambda s, i: (s, i, 0))
    oshape = jax.ShapeDtypeStruct((ns, rows, fs), BF16)
    return _hosted_call(
        body, name=name, grid=(ns, rows // tm),
        in_specs=[pl.BlockSpec((tm, d), lambda s, i: (i, 0)), pl.BlockSpec((1, fs, d), lambda s, i: (s, 0, 0)), ospec],
        out_specs=[ospec, ospec], out_shape=[oshape, oshape],
        dims=("parallel", "parallel"), args=(u, wu, gate), comm=comm)


def _ffn_down_call(name, a, wd, h, tm, comm=None):
    rows, d = h.shape
    ns, _, fs = a.shape

    def body(a_ref, wd_ref, h_ref, o_ref):
        acc = h_ref[...]
        for s in range(ns):
            acc = acc + 0.5 * _dot(a_ref[s], wd_ref[s])
        o_ref[...] = acc

    return _hosted_call(
        body, name=name, grid=(rows // tm,),
        in_specs=[pl.BlockSpec((ns, tm, fs), lambda i: (0, i, 0)),
                  _resident((ns, fs, d)),
                  pl.BlockSpec((tm, d), lambda i: (i, 0))],
        out_specs=[pl.BlockSpec((tm, d), lambda i: (i, 0))],
        out_shape=[jax.ShapeDtypeStruct((rows, d), F32)],
        dims=("parallel",), args=(a, wd, h), comm=comm)[0]


def _mm_call(name, a, bt, tm, out_dtype):
    rows, k = a.shape
    n = bt.shape[0]

    def body(a_ref, b_ref, o_ref):
        o_ref[...] = _dot_nt(a_ref[...], b_ref[...]).astype(out_dtype)

    return pl.pallas_call(
        body, name=name, grid=(rows // tm,),
        in_specs=[pl.BlockSpec((tm, k), lambda i: (i, 0)), pl.BlockSpec((n, k), lambda i: (0, 0))],
        out_specs=pl.BlockSpec((tm, n), lambda i: (i, 0)),
        out_shape=jax.ShapeDtypeStruct((rows, n), out_dtype),
        compiler_params=_params(("parallel",)))(a, bt)


def _mla_heads(z, gql, gkvl, wuq, wuk, wuv):
    cq = z[:, 0:Q_RANK]
    ckv = z[:, Q_RANK:Z_KR]
    kr = z[:, Z_KR:Z_MLA]
    rq = _rms(cq, Q_RANK)
    nq = cq * rq
    cqn = (nq * gql).astype(BF16)
    rkv = _rms(ckv, KV_RANK)
    nkv = ckv * rkv
    ckvn = (nkv * gkvl).astype(BF16)
    qraw = _dot_nt(cqn, wuq)
    knope = _dot(ckvn, wuk)
    v = _dot(ckvn, wuv)
    skr = jnp.sum(kr * kr, axis=-1, keepdims=True)
    heads = []
    for hd in range(HEADS):
        qh = qraw[:, hd * D_QKP:(hd + 1) * D_QKP]
        rqh = lax.rsqrt(jnp.sum(qh * qh, axis=-1, keepdims=True) * (1.0 / D_QK) + EPS)
        kn = knope[:, hd * D_NOPE:(hd + 1) * D_NOPE]
        rkh = lax.rsqrt((jnp.sum(kn * kn, axis=-1, keepdims=True) + skr) * (1.0 / D_QK) + EPS)
        heads.append((qh * rqh, rqh, kn * rkh, kr * rkh, rkh))
    return dict(rq=rq, nq=nq, cqn=cqn, rkv=rkv, nkv=nkv, ckvn=ckvn, v=v, heads=heads)


def _mla_prep_call(z, gql, gkvl, gqh, gkh, wuq, wuk, wuv, tabs, lp, tm):
    rows = z.shape[0]
    tpe = lp // tm

    def body(z_ref, gql_ref, gkvl_ref, gqh_ref, gkh_ref, wuq_ref, wuk_ref, wuv_ref, c_ref, s1_ref, s2_ref,
             q_ref, k_ref, v_ref, cqn_ref, ckvn_ref):
        m = _mla_heads(z_ref[...], gql_ref[...], gkvl_ref[...], wuq_ref[...], wuk_ref[...], wuv_ref[...])
        c, s1, s2 = c_ref[...], s1_ref[...], s2_ref[...]
        gq, gk = gqh_ref[...], gkh_ref[...]
        row = (pl.program_id(0) % tpe) * tm + lax.broadcasted_iota(jnp.int32, (tm, 1), 0)
        spare = (lax.broadcasted_iota(jnp.int32, (1, D_QKP - D_NOPE), 1) == D_ROPE).astype(F32)
        kmask = jnp.where(row < PAD_ROWS, NEG_INF * math.sqrt(D_QK), 0.0) * spare
        for hd in range(HEADS):
            qn, _, knn, krn, _ = m["heads"][hd]
            qg = qn * gq
            q_ref[hd, :, 0:D_NOPE] = qg[:, 0:D_NOPE].astype(BF16)
            q_ref[hd, :, D_NOPE:D_QKP] = (_rope(qg[:, D_NOPE:D_QKP], c, s1, s2) + spare).astype(BF16)
            k_ref[hd, :, 0:D_NOPE] = (knn * gk[:, 0:D_NOPE]).astype(BF16)
            k_ref[hd, :, D_NOPE:D_QKP] = (_rope(krn * gk[:, D_NOPE:D_QKP], c, s1, s2) + kmask).astype(BF16)
            v_ref[hd] = m["v"][:, hd * D_V:(hd + 1) * D_V].astype(BF16)
        cqn_ref[...] = m["cqn"]
        ckvn_ref[...] = m["ckvn"]

    def const(shape):
        return pl.BlockSpec(shape, lambda i: tuple(0 for _ in shape))

    tab = pl.BlockSpec((tm, 128), lambda i: (i % tpe, 0))
    return pl.pallas_call(
        body, name="mla_prep", grid=(rows // tm,),
        in_specs=[pl.BlockSpec((tm, Z_MLA), lambda i: (i, 0)), const((1, Q_RANK)), const((1, KV_RANK)),
                  const((1, D_QKP)), const((1, D_QKP)), const((HEADS * D_QKP, Q_RANK)),
                  const((KV_RANK, HEADS * D_NOPE)), const((KV_RANK, HEADS * D_V)), tab, tab, tab],
        out_specs=[pl.BlockSpec((HEADS, tm, D_QKP), lambda i: (0, i, 0)),
                   pl.BlockSpec((HEADS, tm, D_QKP), lambda i: (0, i, 0)),
                   pl.BlockSpec((HEADS, tm, D_V), lambda i: (0, i, 0)),
                   pl.BlockSpec((tm, Q_RANK), lambda i: (i, 0)),
                   pl.BlockSpec((tm, KV_RANK), lambda i: (i, 0))],
        out_shape=[jax.ShapeDtypeStruct((HEADS, rows, D_QKP), BF16),
                   jax.ShapeDtypeStruct((HEADS, rows, D_QKP), BF16),
                   jax.ShapeDtypeStruct((HEADS, rows, D_V), BF16),
                   jax.ShapeDtypeStruct((rows, Q_RANK), BF16),
                   jax.ShapeDtypeStruct((rows, KV_RANK), BF16)],
        compiler_params=_params(("parallel",)))(z, gql, gkvl, gqh, gkh, wuq, wuk, wuv, *tabs)


Q_BLOCK_ROWS = 528


def _q_block(lp):
    return _row_tile(lp, Q_BLOCK_ROWS)


def _key_end(ext, lp):
    return min(lp, -(-ext // CHUNK) * CHUNK)


def _diag_bias(qb, j0, nk):
    shift = CHUNK.bit_length() - 1
    r = jnp.right_shift(j0 + lax.broadcasted_iota(jnp.int32, (qb, nk), 0), shift)
    c = jnp.right_shift(j0 + lax.broadcasted_iota(jnp.int32, (qb, nk), 1), shift)
    return jnp.where(c <= r, 0.0, NEG_INF)


def _attn_fwd_call(q, k, v, nb, lp, comm=None):
    rows = nb * lp
    qb = _q_block(lp)
    scale = 1.0 / math.sqrt(D_QK)

    def body(q_ref, k_ref, v_ref, o_ref, lse_ref):
        for j in range(lp // qb):
            j0, ext = j * qb, (j + 1) * qb
            kend = _key_end(ext, lp)
            qj = q_ref[0, j0:ext, :]
            sd = _dot_nt(qj, k_ref[0, j0:kend, :]) * scale + _diag_bias(qb, j0, kend - j0)
            mx = jnp.max(sd, axis=-1, keepdims=True)
            if j > 0:
                so = _dot_nt(qj, k_ref[0, 0:j0, :]) * scale
                mx = jnp.maximum(mx, jnp.max(so, axis=-1, keepdims=True))
            pd = jnp.exp(sd - mx)
            l = jnp.sum(pd, axis=-1, keepdims=True)
            o = _dot(pd.astype(BF16), v_ref[0, j0:kend, :])
            if j > 0:
                po = jnp.exp(so - mx)
                l = l + jnp.sum(po, axis=-1, keepdims=True)
                o = o + _dot(po.astype(BF16), v_ref[0, 0:j0, :])
            o_ref[j0:ext, :] = o / l
            lse_ref[0, j0:ext, :] = mx + jnp.log(l)

    return _hosted_call(
        body, name="attn_fwd", grid=(nb, HEADS),
        in_specs=[pl.BlockSpec((1, lp, D_QKP), lambda b, h: (h, b, 0)),
                  pl.BlockSpec((1, lp, D_QKP), lambda b, h: (h, b, 0)),
                  pl.BlockSpec((1, lp, D_V), lambda b, h: (h, b, 0))],
        out_specs=[pl.BlockSpec((lp, D_V), lambda b, h: (b, h)),
                   pl.BlockSpec((1, lp, 1), lambda b, h: (h, b, 0))],
        out_shape=[jax.ShapeDtypeStruct((rows, MLA_W), F32),
                   jax.ShapeDtypeStruct((HEADS, rows, 1), F32)],
        dims=("parallel", "parallel"), args=(q, k, v), comm=comm)


def _attn_bwd_call(q, k, v, o, lse, do, nb, lp, comm=None):
    rows = nb * lp
    qb = _q_block(lp)
    scale = 1.0 / math.sqrt(D_QK)

    def body(q_ref, k_ref, v_ref, o_ref, lse_ref, do_ref, dq_ref, dk_ref, dv_ref, dk_acc, dv_acc):
        dk_acc[...] = jnp.zeros_like(dk_acc)
        dv_acc[...] = jnp.zeros_like(dv_acc)
        for j in range(lp // qb):
            j0, ext = j * qb, (j + 1) * qb
            kend = _key_end(ext, lp)
            dbias = _diag_bias(qb, j0, kend - j0)
            qj = q_ref[0, j0:ext, :]
            doj = do_ref[j0:ext, :]
            delta = jnp.sum(doj * o_ref[j0:ext, :], axis=-1, keepdims=True)
            dob = doj.astype(BF16)
            lse = lse_ref[0, j0:ext, :]
            dq = jnp.zeros((qb, D_QKP), F32)
            for lo, hi, bias in ((j0, kend, dbias), (0, j0, None)):
                if hi == lo:
                    continue
                kk = k_ref[0, lo:hi, :]
                s = _dot_nt(qj, kk) * scale
                p = jnp.exp((s if bias is None else s + bias) - lse)
                dv_acc[lo:hi, :] += _dot_tn(p.astype(BF16), dob)
                dp = _dot_nt(dob, v_ref[0, lo:hi, :])
                ds = (p * (dp - delta) * scale).astype(BF16)
                dq = dq + _dot(ds, kk)
                dk_acc[lo:hi, :] += _dot_tn(ds, qj)
            dq_ref[0, j0:ext, :] = dq.astype(BF16)
        dk_ref[0] = dk_acc[...].astype(BF16)
        dv_ref[0] = dv_acc[...].astype(BF16)

    qspec = pl.BlockSpec((1, lp, D_QKP), lambda b, h: (h, b, 0))
    vspec = pl.BlockSpec((1, lp, D_V), lambda b, h: (h, b, 0))
    ospec = pl.BlockSpec((lp, D_V), lambda b, h: (b, h))
    return _hosted_call(
        body, name="attn_bwd", grid=(nb, HEADS),
        in_specs=[qspec, qspec, vspec, ospec, pl.BlockSpec((1, lp, 1), lambda b, h: (h, b, 0)), ospec],
        out_specs=[qspec, qspec, vspec],
        out_shape=[jax.ShapeDtypeStruct((HEADS, rows, D_QKP), BF16),
                   jax.ShapeDtypeStruct((HEADS, rows, D_QKP), BF16),
                   jax.ShapeDtypeStruct((HEADS, rows, D_V), BF16)],
        scratch_shapes=[pltpu.VMEM((lp, D_QKP), F32), pltpu.VMEM((lp, D_V), F32)],
        dims=("parallel", "parallel"), args=(q, k, v, o, lse, do), comm=comm)


def _lru_gates(u, cw, cb, wa, ba, wx, bx, lam, lp):
    xc = (cw[3:4, :] * u + cw[2:3, :] * pltpu.roll(u, 1, 0) + cw[1:2, :] * pltpu.roll(u, 2, 0)
          + cw[0:1, :] * pltpu.roll(u, 3, 0) + cb)
    xcb = xc.astype(BF16)
    r = _sigmoid(_dot(xcb, wa) + ba)
    i = _sigmoid(_dot(xcb, wx) + bx)
    sp = _softplus_neg(lam)
    la = -C_RGLRU * r * sp
    a = jnp.exp(la)
    x2 = 2.0 * la
    e2 = a * a
    m2 = jnp.maximum(jnp.where(x2 > -0.01, -x2 * (1.0 + 0.5 * x2), 1.0 - e2), 1e-30)
    rs = lax.rsqrt(m2)
    row = lax.broadcasted_iota(jnp.int32, (lp, LRU_TILE), 0)
    first = row == PAD_ROWS
    valid = row >= PAD_ROWS
    mult_eff = jnp.where(first, 1.0, m2 * rs)
    return dict(xc=xc, xcb=xcb, r=r, i=i, sp=sp, a=a, e2=e2, rs=rs, mult_eff=mult_eff, first=first, valid=valid)


def _scan_rows(a, b, a_s, b_s, out_ref, lp, reverse):
    sub = lax.broadcasted_iota(jnp.int32, (lp, LRU_TILE), 0) & 7
    for dist in (1, 2, 4):
        shift = lp - dist if reverse else dist
        keep = (sub + dist <= 7) if reverse else (sub >= dist)
        a_sh = pltpu.roll(a, shift, 0)
        b_sh = pltpu.roll(b, shift, 0)
        b = jnp.where(keep, a * b_sh + b, b)
        a = jnp.where(keep, a * a_sh, a)
    a_s[...] = a
    b_s[...] = b
    n_groups = lp // 8
    edge = 0 if reverse else 7

    def group(gi, carry):
        r0 = pl.multiple_of(((n_groups - 1 - gi) if reverse else gi) * 8, 8)
        a8 = a_s[pl.ds(r0, 8), :]
        b8 = b_s[pl.ds(r0, 8), :]
        out_ref[pl.ds(r0, 8), :] = a8 * carry + b8
        return a8[edge:edge + 1, :] * carry + b8[edge:edge + 1, :]

    lax.fori_loop(0, n_groups, group, jnp.zeros((1, LRU_TILE), F32), unroll=4)


def _lru_specs(lp):
    seq = lambda col0: pl.BlockSpec((lp, LRU_TILE), lambda t, b: (b, col0 + t))
    cw = pl.BlockSpec((1, CONV_K, LRU_TILE), lambda t, b: (t, 0, 0))
    vec = pl.BlockSpec((1, LRU_TILE), lambda t, b: (0, t))
    mat = pl.BlockSpec((1, LRU_TILE, LRU_TILE), lambda t, b: (t, 0, 0))
    return seq, cw, vec, mat


def _lru_fwd_call(z, cw, cb, wa, ba, wx, bx, lam, nb, lp, comm=None):
    rows = nb * lp
    seq, cwspec, vec, mat = _lru_specs(lp)

    def body(u_ref, g_ref, cw_ref, cb_ref, wa_ref, ba_ref, wx_ref, bx_ref, lam_ref, y_ref, hs_ref, a_s, b_s):
        m = _lru_gates(u_ref[...], cw_ref[0], cb_ref[...], wa_ref[0], ba_ref[...], wx_ref[0], bx_ref[...],
                       lam_ref[...], lp)
        a = jnp.where(m["valid"], m["a"], 0.0)
        b = jnp.where(m["valid"], m["mult_eff"] * (m["i"] * m["xc"]), 0.0)
        _scan_rows(a, b, a_s, b_s, hs_ref, lp, reverse=False)
        gl, _ = _gelu(g_ref[...])
        y_ref[...] = hs_ref[...] * gl

    oshape = jax.ShapeDtypeStruct((rows, LRU_W), F32)
    return _hosted_call(
        body, name="lru_fwd", grid=(N_LRU_TILES, nb),
        in_specs=[seq(Z_U // LRU_TILE), seq(Z_G // LRU_TILE), cwspec, vec, mat, vec, mat, vec, vec],
        out_specs=[seq(0), seq(0)], out_shape=[oshape, oshape],
        scratch_shapes=[pltpu.VMEM((lp, LRU_TILE), F32), pltpu.VMEM((lp, LRU_TILE), F32)],
        dims=("parallel", "parallel"), args=(z, z, cw, cb, wa, ba, wx, bx, lam), comm=comm)


def _lru_bwd_call(z, hs, dy, cw, cb, wa, ba, wx, bx, lam, nb, lp, comm=None):
    rows = nb * lp
    seq, cwspec, vec, mat = _lru_specs(lp)

    def body(u_ref, g_ref, hs_ref, dy_ref, cw_ref, cb_ref, wa_ref, ba_ref, wx_ref, bx_ref, lam_ref,
             du_ref, dg_ref, dcw_ref, dcb_ref, dwa_ref, dba_ref, dwx_ref, dbx_ref, dlam_ref, a_s, b_s, d_s):
        b_idx = pl.program_id(1)
        u = u_ref[...]
        cw = cw_ref[0]
        wa, wx = wa_ref[0], wx_ref[0]
        lam = lam_ref[...]
        m = _lru_gates(u, cw, cb_ref[...], wa, ba_ref[...], wx, bx_ref[...], lam, lp)
        gate = g_ref[...]
        gl, th = _gelu(gate)
        dy = dy_ref[...]
        hs = hs_ref[...]
        dg_ref[...] = (dy * hs * _gelu_grad(gate, th)).astype(BF16)
        a_eff = jnp.where(m["valid"], m["a"], 0.0)
        _scan_rows(pltpu.roll(a_eff, lp - 1, 0), dy * gl, a_s, b_s, d_s, lp, reverse=True)
        ds = d_s[...]
        xc, r, i = m["xc"], m["r"], m["i"]
        row = lax.broadcasted_iota(jnp.int32, (lp, LRU_TILE), 0)
        da = ds * jnp.where(row >= 1, pltpu.roll(hs, 1, 0), 0.0)
        db = jnp.where(m["valid"], ds, 0.0)
        di = db * m["mult_eff"] * xc
        dxc = db * m["mult_eff"] * i
        live = m["valid"] & jnp.logical_not(m["first"])
        dm = jnp.where(live, db * i * xc, 0.0)
        dla = da * m["a"] - dm * (m["e2"] * m["rs"])
        dr = dla * (-C_RGLRU * m["sp"])
        dsp = jnp.sum(dla * (-C_RGLRU * r), axis=0, keepdims=True)
        dpr = (dr * r * (1.0 - r))
        dpi = (di * i * (1.0 - i))
        dprb, dpib = dpr.astype(BF16), dpi.astype(BF16)
        dxc = dxc + _dot_nt(dprb, wa) + _dot_nt(dpib, wx)
        du = (cw[3:4, :] * dxc + cw[2:3, :] * pltpu.roll(dxc, lp - 1, 0) + cw[1:2, :] * pltpu.roll(dxc, lp - 2, 0)
              + cw[0:1, :] * pltpu.roll(dxc, lp - 3, 0))
        du_ref[...] = jnp.where(m["valid"], du, 0.0).astype(BF16)
        tap = lax.broadcasted_iota(jnp.int32, (CONV_K, LRU_TILE), 0)
        dcw = jnp.zeros((CONV_K, LRU_TILE), F32)
        for kk in range(CONV_K):
            shifted = u if kk == CONV_K - 1 else pltpu.roll(u, CONV_K - 1 - kk, 0)
            dcw = jnp.where(tap == kk, jnp.sum(dxc * shifted, axis=0, keepdims=True), dcw)
        parts = [(dcw_ref, dcw[None]), (dcb_ref, jnp.sum(dxc, axis=0, keepdims=True)[None]),
                 (dwa_ref, _dot_tn(m["xcb"], dprb)[None]), (dba_ref, jnp.sum(dpr, axis=0, keepdims=True)[None]),
                 (dwx_ref, _dot_tn(m["xcb"], dpib)[None]), (dbx_ref, jnp.sum(dpi, axis=0, keepdims=True)[None]),
                 (dlam_ref, (dsp * (-jax.nn.sigmoid(-lam)))[None])]

        @pl.when(b_idx == 0)
        def _():
            for ref, val in parts:
                ref[...] = val

        @pl.when(b_idx != 0)
        def _():
            for ref, val in parts:
                ref[...] += val

    bshape = jax.ShapeDtypeStruct((rows, LRU_W), BF16)
    vec3 = pl.BlockSpec((1, 1, LRU_TILE), lambda t, b: (t, 0, 0))
    vshape = jax.ShapeDtypeStruct((N_LRU_TILES, 1, LRU_TILE), F32)
    mshape = jax.ShapeDtypeStruct((N_LRU_TILES, LRU_TILE, LRU_TILE), F32)
    return _hosted_call(
        body, name="lru_bwd", grid=(N_LRU_TILES, nb),
        in_specs=[seq(Z_U // LRU_TILE), seq(Z_G // LRU_TILE), seq(0), seq(0), cwspec, vec, mat, vec, mat, vec, vec],
        out_specs=[seq(0), seq(0), cwspec, vec3, mat, vec3, mat, vec3, vec3],
        out_shape=[bshape, bshape, jax.ShapeDtypeStruct((N_LRU_TILES, CONV_K, LRU_TILE), F32), vshape, mshape,
                   vshape, mshape, vshape, vshape],
        scratch_shapes=[pltpu.VMEM((lp, LRU_TILE), F32)] * 3,
        dims=("parallel", "arbitrary"), args=(z, z, hs, dy, cw, cb, wa, ba, wx, bx, lam), comm=comm)


def _mix_out_call(ya, yl, ga, gl, wout, h, tm):
    rows, d = h.shape

    def body(ya_ref, yl_ref, ga_ref, gl_ref, w_ref, h_ref, y_ref, o_ref):
        a = ya_ref[...]
        l = yl_ref[...]
        an = (a * _rms(a, MLA_W) * ga_ref[...]).astype(BF16)
        ln = (l * _rms(l, LRU_W) * gl_ref[...]).astype(BF16)
        y_ref[:, 0:MLA_W] = an
        y_ref[:, MLA_W:MLA_W + LRU_W] = ln
        o_ref[...] = h_ref[...] + _dot(an, w_ref[0:MLA_W, :]) + _dot(ln, w_ref[MLA_W:MLA_W + LRU_W, :])

    half = pl.BlockSpec((tm, MLA_W), lambda i: (i, 0))
    g = pl.BlockSpec((1, MLA_W), lambda i: (0, 0))
    full = pl.BlockSpec((tm, d), lambda i: (i, 0))
    return pl.pallas_call(
        body, name="mix_out", grid=(rows // tm,),
        in_specs=[half, half, g, g, pl.BlockSpec((MLA_W + LRU_W, d), lambda i: (0, 0)), full],
        out_specs=[full, full],
        out_shape=[jax.ShapeDtypeStruct((rows, MLA_W + LRU_W), BF16), jax.ShapeDtypeStruct((rows, d), F32)],
        compiler_params=_params(("parallel",)))(ya, yl, ga, gl, wout, h)


def _mix_out_bwd_call(dhb, wout, ya, yl, ga, gl, tm):
    rows = ya.shape[0]
    d = dhb.shape[1]

    def body(dh_ref, w_ref, ya_ref, yl_ref, ga_ref, gl_ref, dya_ref, dyl_ref, dga_ref, dgl_ref):
        dy = _dot_nt(dh_ref[...], w_ref[...])
        outs = []
        for val, g_ref, lo, out_ref in ((ya_ref[...], ga_ref, 0, dya_ref), (yl_ref[...], gl_ref, MLA_W, dyl_ref)):
            r = _rms(val, MLA_W)
            n = val * r
            dyn = dy[:, lo:lo + MLA_W]
            out_ref[...] = _rms_bwd(dyn * g_ref[...], n, r, MLA_W)
            outs.append(jnp.sum(dyn * n, axis=0, keepdims=True))

        @pl.when(pl.program_id(0) == 0)
        def _():
            dga_ref[...] = outs[0]
            dgl_ref[...] = outs[1]

        @pl.when(pl.program_id(0) != 0)
        def _():
            dga_ref[...] += outs[0]
            dgl_ref[...] += outs[1]

    half = pl.BlockSpec((tm, MLA_W), lambda i: (i, 0))
    g = pl.BlockSpec((1, MLA_W), lambda i: (0, 0))
    return pl.pallas_call(
        body, name="mix_out_bwd", grid=(rows // tm,),
        in_specs=[pl.BlockSpec((tm, d), lambda i: (i, 0)), pl.BlockSpec((MLA_W + LRU_W, d), lambda i: (0, 0)),
                  half, half, g, g],
        out_specs=[half, half, g, g],
        out_shape=[jax.ShapeDtypeStruct((rows, MLA_W), F32), jax.ShapeDtypeStruct((rows, LRU_W), F32),
                   jax.ShapeDtypeStruct((1, MLA_W), F32), jax.ShapeDtypeStruct((1, LRU_W), F32)],
        compiler_params=_params(("arbitrary",)))(dhb, wout, ya, yl, ga, gl)


def _final_call(h, g, target, lp, tm):
    rows, d = h.shape
    tpe = lp // tm

    def body(h_ref, g_ref, t_ref, dh_ref, dhb_ref, dg_ref, loss_ref):
        i = pl.program_id(0)
        x = h_ref[...]
        g = g_ref[...]
        r = _rms(x, d)
        n = x * r
        row = (i % tpe) * tm + lax.broadcasted_iota(jnp.int32, (tm, 1), 0)
        err = jnp.where(row >= CHUNK, n * g - t_ref[...], 0.0)
        dout = err * (1.0 / d)
        dh = _rms_bwd(dout * g, n, r, d)
        dh_ref[...] = dh
        dhb_ref[...] = dh.astype(BF16)
        dg = jnp.sum(dout * n, axis=0, keepdims=True)
        part = jnp.sum(jnp.sum(err * err, axis=1, keepdims=True), axis=0, keepdims=True) * (0.5 / d)
        loss = jnp.broadcast_to(part, (1, 128))

        @pl.when(i == 0)
        def _():
            dg_ref[...] = dg
            loss_ref[...] = loss

        @pl.when(i != 0)
        def _():
            dg_ref[...] += dg
            loss_ref[...] += loss

    full = pl.BlockSpec((tm, d), lambda i: (i, 0))
    return pl.pallas_call(
        body, name="final_loss", grid=(rows // tm,),
        in_specs=[full, pl.BlockSpec((1, d), lambda i: (0, 0)), full],
        out_specs=[full, full, pl.BlockSpec((1, d), lambda i: (0, 0)), pl.BlockSpec((1, 128), lambda i: (0, 0))],
        out_shape=[jax.ShapeDtypeStruct((rows, d), F32), jax.ShapeDtypeStruct((rows, d), BF16),
                   jax.ShapeDtypeStruct((1, d), F32), jax.ShapeDtypeStruct((1, 128), F32)],
        compiler_params=_params(("arbitrary",)))(h, g, target)


def _ffn_dact_call(name, dhb, wd, gate, up, tm, comm=None):
    rows, d = dhb.shape
    ns, fs, _ = wd.shape

    nsub = 2 if tm % 32 == 0 else 1
    sub = tm // nsub

    def body(dh_ref, wd_ref, g_ref, p_ref, dg_ref, dp_ref):
        wd = wd_ref[0]
        for r in range(nsub):
            rs = slice(r * sub, (r + 1) * sub)
            da = (0.5 * _dot_nt(dh_ref[rs, :], wd)).astype(BF16)
            g = g_ref[0, rs, :]
            p = p_ref[0, rs, :]
            sg = jax.nn.sigmoid(g)
            dg_ref[0, rs, :] = (da * p) * (sg * (1.0 + g * (1.0 - sg)))
            dp_ref[0, rs, :] = da * (g * sg)

    aspec = pl.BlockSpec((1, tm, fs), lambda s, i: (s, i, 0))
    oshape = jax.ShapeDtypeStruct((ns, rows, fs), BF16)
    return _hosted_call(
        body, name=name, grid=(ns, rows // tm),
        in_specs=[pl.BlockSpec((tm, d), lambda s, i: (i, 0)), pl.BlockSpec((1, fs, d), lambda s, i: (s, 0, 0)),
                  aspec, aspec],
        out_specs=[aspec, aspec], out_shape=[oshape, oshape],
        dims=("parallel", "parallel"), args=(dhb, wd, gate, up), comm=comm)


def _norm_in_bwd_call(name, pieces, h, g, dres, tm, comm=None, part=(0, 1), prev=None):
    rows, d = h.shape
    npc = len(pieces)
    steps = rows // tm // part[1]
    off = part[0] * steps
    n_prev = 0 if prev is None else 2

    def body(*refs):
        d_refs = refs[0:2 * npc:2]
        w_refs = refs[1:2 * npc:2]
        h_ref, g_ref, dres_ref = refs[2 * npc:2 * npc + 3]
        dh_ref, dhb_ref, dg_ref = refs[2 * npc + 3 + n_prev:]
        du = jnp.zeros((tm, d), F32)
        for d_ref, w_ref in zip(d_refs, w_refs):
            if len(d_ref.shape) == 3:
                for s in range(d_ref.shape[0]):
                    du = du + _dot(d_ref[s], w_ref[s])
            else:
                du = du + _dot(d_ref[...], w_ref[...])
        x = h_ref[...]
        r = _rms(x, d)
        n = x * r
        dh = dres_ref[...] + _rms_bwd(du * g_ref[...], n, r, d)
        dh_ref[...] = dh
        dhb_ref[...] = dh.astype(BF16)
        dg = jnp.sum(du * n, axis=0, keepdims=True)

        @pl.when(pl.program_id(0) == 0)
        def _():
            dg_ref[...] = dg

        @pl.when(pl.program_id(0) != 0)
        def _():
            dg_ref[...] += dg

    in_specs, args = [], []
    for dd, w in pieces:
        if dd.ndim == 3:
            in_specs.append(pl.BlockSpec((dd.shape[0], tm, dd.shape[2]), lambda i: (0, i + off, 0)))
            in_specs.append(_resident(w.shape))
        else:
            in_specs.append(pl.BlockSpec((tm, dd.shape[1]), lambda i: (i + off, 0)))
            in_specs.append(_resident(w.shape))
        args += [dd, w]
    full = pl.BlockSpec((tm, d), lambda i: (i + off, 0))
    gspec = pl.BlockSpec((1, d), lambda i: (0, 0))
    n_in = len(in_specs) + 3
    return _hosted_call(
        body, name=name, grid=(steps,),
        in_specs=in_specs + [full, gspec, full] + _any_specs(n_prev),
        out_specs=[full, full, gspec],
        out_shape=[jax.ShapeDtypeStruct((rows, d), F32), jax.ShapeDtypeStruct((rows, d), BF16),
                   jax.ShapeDtypeStruct((1, d), F32)],
        args=(*args, h, g, dres, *(prev or ())), comm=comm,
        aliases={n_in: 0, n_in + 1: 1} if prev is not None else {})


def _wgrad_call(name, a, b, scale=1.0, comm=None):
    a3, b3 = a.ndim == 3, b.ndim == 3
    ns = a.shape[0] if a3 else (b.shape[0] if b3 else 1)
    rows, m = a.shape[-2:]
    n = b.shape[-1]
    tmm = m if a3 else _col_tile(m, 256)

    def body(a_ref, b_ref, o_ref):
        av = a_ref[0] if a3 else a_ref[...]
        bv = b_ref[0] if b3 else b_ref[...]
        res = _dot_tn(av, bv)
        if scale != 1.0:
            res = res * scale
        if a3 or b3:
            o_ref[0] = res
        else:
            o_ref[...] = res

    aspec = (pl.BlockSpec((1, rows, tmm), lambda s, j: (s, 0, j)) if a3
             else pl.BlockSpec((rows, tmm), lambda s, j: (0, j)))
    bspec = (pl.BlockSpec((1, rows, n), lambda s, j: (s, 0, 0)) if b3
             else pl.BlockSpec((rows, n), lambda s, j: (0, 0)))
    if a3 or b3:
        ospec = pl.BlockSpec((1, tmm, n), lambda s, j: (s, j, 0))
        oshape = jax.ShapeDtypeStruct((ns, m, n), F32)
    else:
        ospec = pl.BlockSpec((tmm, n), lambda s, j: (j, 0))
        oshape = jax.ShapeDtypeStruct((m, n), F32)
    return _hosted_call(
        body, name=name, grid=(ns, m // tmm), in_specs=[aspec, bspec], out_specs=[ospec], out_shape=[oshape],
        dims=("parallel", "parallel"), args=(a, b), comm=comm)[0]


def _mla_prep_bwd_call(z, dq, dk, dv, gql, gkvl, gqh, gkh, wuq, wuk, wuv, tabs, lp, tm, comm=None):
    rows = z.shape[0]
    tpe = lp // tm

    def body(z_ref, dq_ref, dk_ref, dv_ref, gql_ref, gkvl_ref, gqh_ref, gkh_ref, wuq_ref, wuk_ref, wuv_ref,
             c_ref, s1_ref, s2_ref, dz_ref, dqp_ref, dkn_ref, dvv_ref, dgql_ref, dgkvl_ref, dgqh_ref, dgkh_ref):
        gql, gkvl = gql_ref[...], gkvl_ref[...]
        gq, gk = gqh_ref[...], gkh_ref[...]
        wuq, wuk, wuv = wuq_ref[...], wuk_ref[...], wuv_ref[...]
        m = _mla_heads(z_ref[...], gql, gkvl, wuq, wuk, wuv)
        c, s1, s2 = c_ref[...], s1_ref[...], s2_ref[...]
        dgq = jnp.zeros((1, D_QKP), F32)
        dgk = jnp.zeros((1, D_QKP), F32)
        dkr = jnp.zeros((tm, D_QKP - D_NOPE), F32)
        for hd in range(HEADS):
            qn, rqh, knn, krn, rkh = m["heads"][hd]
            dqg = jnp.concatenate([dq_ref[hd, :, 0:D_NOPE].astype(F32),
                                   _rope_t(dq_ref[hd, :, D_NOPE:D_QKP].astype(F32), c, s1, s2)], axis=1)
            dgq = dgq + jnp.sum(dqg * qn, axis=0, keepdims=True)
            dqn = dqg * gq
            dqr = rqh * (dqn - qn * (jnp.sum(dqn * qn, axis=-1, keepdims=True) * (1.0 / D_QK)))
            dqp_ref[:, hd * D_QKP:(hd + 1) * D_QKP] = dqr.astype(BF16)
            kn_full = jnp.concatenate([knn, krn], axis=1)
            dkg = jnp.concatenate([dk_ref[hd, :, 0:D_NOPE].astype(F32),
                                   _rope_t(dk_ref[hd, :, D_NOPE:D_QKP].astype(F32), c, s1, s2)], axis=1)
            dgk = dgk + jnp.sum(dkg * kn_full, axis=0, keepdims=True)
            dkn = dkg * gk
            dkraw = rkh * (dkn - kn_full * (jnp.sum(dkn * kn_full, axis=-1, keepdims=True) * (1.0 / D_QK)))
            dkn_ref[:, hd * D_NOPE:(hd + 1) * D_NOPE] = dkraw[:, 0:D_NOPE].astype(BF16)
            dkr = dkr + dkraw[:, D_NOPE:D_QKP]
            dvv_ref[:, hd * D_V:(hd + 1) * D_V] = dv_ref[hd]
        dcqn = _dot(dqp_ref[...], wuq)
        dckvn = _dot_nt(dkn_ref[...], wuk) + _dot_nt(dvv_ref[...], wuv)
        dz_ref[:, 0:Q_RANK] = _rms_bwd(dcqn * gql, m["nq"], m["rq"], Q_RANK).astype(BF16)
        dz_ref[:, Q_RANK:Z_KR] = _rms_bwd(dckvn * gkvl, m["nkv"], m["rkv"], KV_RANK).astype(BF16)
        dz_ref[:, Z_KR:Z_MLA] = dkr.astype(BF16)
        parts = [(dgql_ref, jnp.sum(dcqn * m["nq"], axis=0, keepdims=True)),
                 (dgkvl_ref, jnp.sum(dckvn * m["nkv"], axis=0, keepdims=True)), (dgqh_ref, dgq), (dgkh_ref, dgk)]

        @pl.when(pl.program_id(0) == 0)
        def _():
            for ref, val in parts:
                ref[...] = val

        @pl.when(pl.program_id(0) != 0)
        def _():
            for ref, val in parts:
                ref[...] += val

    def const(shape):
        return pl.BlockSpec(shape, lambda i: tuple(0 for _ in shape))

    tab = pl.BlockSpec((tm, 128), lambda i: (i % tpe, 0))
    hq = pl.BlockSpec((HEADS, tm, D_QKP), lambda i: (0, i, 0))
    hv = pl.BlockSpec((HEADS, tm, D_V), lambda i: (0, i, 0))

    def rowspec(n):
        return pl.BlockSpec((tm, n), lambda i: (i, 0))

    return _hosted_call(
        body, name="mla_prep_bwd", grid=(rows // tm,),
        in_specs=[rowspec(Z_MLA), hq, hq, hv, const((1, Q_RANK)), const((1, KV_RANK)), const((1, D_QKP)),
                  const((1, D_QKP)), const((HEADS * D_QKP, Q_RANK)), const((KV_RANK, HEADS * D_NOPE)),
                  const((KV_RANK, HEADS * D_V)), tab, tab, tab],
        out_specs=[rowspec(Z_MLA), rowspec(HEADS * D_QKP), rowspec(HEADS * D_NOPE), rowspec(HEADS * D_V),
                   const((1, Q_RANK)), const((1, KV_RANK)), const((1, D_QKP)), const((1, D_QKP))],
        out_shape=[jax.ShapeDtypeStruct((rows, Z_MLA), BF16), jax.ShapeDtypeStruct((rows, HEADS * D_QKP), BF16),
                   jax.ShapeDtypeStruct((rows, HEADS * D_NOPE), BF16), jax.ShapeDtypeStruct((rows, HEADS * D_V), BF16),
                   jax.ShapeDtypeStruct((1, Q_RANK), F32), jax.ShapeDtypeStruct((1, KV_RANK), F32),
                   jax.ShapeDtypeStruct((1, D_QKP), F32), jax.ShapeDtypeStruct((1, D_QKP), F32)],
        args=(z, dq, dk, dv, gql, gkvl, gqh, gkh, wuq, wuk, wuv, *tabs), comm=comm)


def _local_step(h0, target, w, nb, lp, sched=None):
    tm = _row_tile(nb * lp, 1408)
    te = _row_tile(lp, 512)
    tabs = _rope_tables(lp)
    g = {}
    if sched is None:
        host = lambda stage: None
    else:
        sched.g = g
        host = sched.host

    def ffn_fwd(tag, h, split):
        u = _rmsnorm_call(tag + "_norm", h, w[tag + "_norm"], te)
        if split:
            gate = _ffn_gate_call(tag + "_gate", u, w[tag + "_w_gate"], tm, host(tag + "_gate"))
            up, act = _ffn_upact_call(tag + "_upact", u, w[tag + "_w_up"], gate, tm, host(tag + "_upact"))
        else:
            gate, up, act = _ffn_up_call(tag + "_up", u, w[tag + "_w_gate"], w[tag + "_w_up"], tm, host(tag + "_up"))
        return _ffn_down_call(tag + "_down", act, w[tag + "_w_down"], h, te, host(tag + "_down")), (u, gate, up, act)

    def ffn_bwd(tag, h, saved, dh, dhb, split):
        u, gate, up, act = saved
        dgate, dup = _ffn_dact_call(tag + "_dact", dhb, w[tag + "_w_down"], gate, up, tm, host(tag + "_dact"))
        g[tag + "_w_down"] = _wgrad_call(tag + "_dwd", act, dhb, 0.5, host(tag + "_dwd"))
        g[tag + "_w_gate"] = _wgrad_call(tag + "_dwg", dgate, u, 1.0, host(tag + "_dwg"))
        g[tag + "_w_up"] = _wgrad_call(tag + "_dwu", dup, u, 1.0, host(tag + "_dwu"))
        pieces = [(dgate, w[tag + "_w_gate"]), (dup, w[tag + "_w_up"])]
        if not split:
            dh_in, dhb_in, g[tag + "_norm"] = _norm_in_bwd_call(tag + "_din", pieces, h, w[tag + "_norm"], dh, te,
                                                                host(tag + "_din"))
            return dh_in, dhb_in
        dh_a, dhb_a, dg_a = _norm_in_bwd_call(tag + "_din_a", pieces, h, w[tag + "_norm"], dh, te,
                                              host(tag + "_din_a"), part=(0, 2))
        dh_in, dhb_in, dg_b = _norm_in_bwd_call(tag + "_din_b", pieces, h, w[tag + "_norm"], dh, te,
                                                host(tag + "_din_b"), part=(1, 2), prev=(dh_a, dhb_a))
        g[tag + "_norm"] = dg_a + dg_b
        return dh_in, dhb_in

    h1, s1 = ffn_fwd("ffn1", h0, True)
    un = _rmsnorm_call("mix_norm", h1, w["mix_norm"], te)
    z = _mm_call("mix_in", un, w["w_in"], tm, F32)
    mla_w = (w["q_latent_norm"], w["kv_latent_norm"], w["q_head_norm"], w["k_head_norm"], w["w_uq"], w["w_uk"],
             w["w_uv"])
    q, k, v, cqn, ckvn = _mla_prep_call(z, *mla_w, tabs, lp, te)
    o, lse = _attn_fwd_call(q, k, v, nb, lp, host("attn_fwd"))
    lru_w = (w["conv_w"], w["conv_b"], w["gate_a_w"], w["gate_a_b"], w["gate_x_w"], w["gate_x_b"], w["lru_lambda"])
    yl, hs = _lru_fwd_call(z, *lru_w, nb, lp, host("lru_fwd"))
    y, h2 = _mix_out_call(o, yl, w["attn_out_norm"], w["lru_out_norm"], w["w_out"], h1, te)
    h3, s2 = ffn_fwd("ffn2", h2, False)
    dh3, dh3b, g["final_norm"], loss = _final_call(h3, w["final_norm"], target, lp, te)
    g["loss"] = loss

    dh2, dh2b = ffn_bwd("ffn2", h2, s2, dh3, dh3b, False)
    g["w_out"] = _wgrad_call("dw_out", y, dh2b)
    dya, dyl, g["attn_out_norm"], g["lru_out_norm"] = _mix_out_bwd_call(
        dh2b, w["w_out"], o, yl, w["attn_out_norm"], w["lru_out_norm"], te)
    dq, dk, dv = _attn_bwd_call(q, k, v, o, lse, dya, nb, lp, host("attn_bwd"))
    (dz_mla, dqp, dkn, dvv, g["q_latent_norm"], g["kv_latent_norm"], g["q_head_norm"],
     g["k_head_norm"]) = _mla_prep_bwd_call(z, dq, dk, dv, *mla_w, tabs, lp, te, host("mla_prep_bwd"))
    g["w_uq"] = _wgrad_call("dw_uq", dqp, cqn)
    g["w_uk"] = _wgrad_call("dw_uk", ckvn, dkn)
    g["w_uv"] = _wgrad_call("dw_uv", ckvn, dvv)
    (du, dgt, g["conv_w"], g["conv_b"], g["gate_a_w"], g["gate_a_b"], g["gate_x_w"], g["gate_x_b"],
     g["lru_lambda"]) = _lru_bwd_call(z, hs, dyl, *lru_w, nb, lp, host("lru_bwd"))
    win = w["w_in"]
    g["w_in"] = jnp.concatenate(
        [_wgrad_call("dw_in_mla", dz_mla, un), _wgrad_call("dw_in_u", du, un), _wgrad_call("dw_in_g", dgt, un)],
        axis=0)
    dh1, dh1b, g["mix_norm"] = _norm_in_bwd_call(
        "mix_din", [(dz_mla, win[0:Z_MLA]), (du, win[Z_U:Z_G]), (dgt, win[Z_G:Z_W])], h1, w["mix_norm"], dh2, te,
        host("mix_din"))
    dh0, _ = ffn_bwd("ffn1", h0, s1, dh1, dh1b, True)
    return loss, dh0, g


def _place():
    x, y, c = lax.axis_index("x"), lax.axis_index("y"), lax.axis_index("c")
    return x, y, c, [(1 - x, y), (x, 1 - y), (1 - x, 1 - y)]


def _any_specs(n):
    return [pl.BlockSpec(memory_space=pl.ANY)] * n


def _remote(src, dst, sems, k, dev):
    send_sems, recv_sems, base = sems
    return pltpu.make_async_remote_copy(src_ref=src, dst_ref=dst, send_sem=send_sems.at[base + k],
                                        recv_sem=recv_sems.at[base + k], device_id=dev, device_id_type=MESH)


EW_VMEM_BYTES = 24 * 1024 * 1024


def _fit_rows(rows, cols, blocks):
    return _row_tile(rows, max(16, int(EW_VMEM_BYTES // (8 * blocks)) // cols))


class _Geom:
    def __init__(self, n0, n1, blocks=1.0):
        self.n0, self.n1 = n0, n1
        self.axis = 0 if n0 % 32 == 0 else 1
        self.h0, self.h1 = (n0 // 2, n1) if self.axis == 0 else (n0, n1 // 2)
        self.tr = _fit_rows(self.h0, self.h1, blocks)
        self.nblk = self.h0 // self.tr

    def half_ref(self, ref, lead, idx):
        if self.axis == 0:
            return ref.at[(*lead, pl.ds(idx * self.h0, self.h0))]
        return ref.at[(*lead, slice(None), pl.ds(idx * self.h1, self.h1))]

    def half_block(self, lead, i, idx):
        return (*lead, idx * self.nblk + i, 0) if self.axis == 0 else (*lead, i, idx)


class _Comm:
    def __init__(self, ins, out_shapes, aliases, n_sems, start, finish, deliver):
        self.ins, self.out_shapes, self.aliases, self.n_sems = list(ins), list(out_shapes), dict(aliases), n_sems
        self.start, self.finish, self.deliver = start, finish, deliver

    def scratch(self):
        return [pltpu.SemaphoreType.DMA((self.n_sems,)), pltpu.SemaphoreType.DMA((self.n_sems,))]


def _comm_call(name, comm):
    n_in = len(comm.ins)

    def body(*refs):
        ins, outs, sems = refs[:n_in], refs[n_in:-2], (*refs[-2:], 0)
        comm.start(ins, outs, sems)
        comm.finish(ins, outs, sems)

    res = pl.pallas_call(
        body, name=name, out_shape=comm.out_shapes, in_specs=_any_specs(n_in),
        out_specs=_any_specs(len(comm.out_shapes)), input_output_aliases=comm.aliases,
        scratch_shapes=comm.scratch())(*comm.ins)
    return comm.deliver(list(res))


def _hosted_call(body, *, name, grid, in_specs, out_specs, out_shape, args, scratch_shapes=(), dims=None, comm=None,
                 prefetch=None, aliases=None):
    aliases = dict(aliases or {})
    in_specs, out_specs, out_shape = list(in_specs), list(out_specs), list(out_shape)
    n_pre = 0 if prefetch is None else 1

    def call(fn, in_specs, out_specs, out_shape, scratch, aliases, dims, args):
        if prefetch is None:
            return pl.pallas_call(
                fn, name=name, grid=grid, in_specs=in_specs, out_specs=out_specs, out_shape=out_shape,
                scratch_shapes=scratch, input_output_aliases=aliases, compiler_params=_params(dims))(*args)
        spec = pltpu.PrefetchScalarGridSpec(num_scalar_prefetch=1, grid=grid, in_specs=in_specs, out_specs=out_specs,
                                            scratch_shapes=scratch)
        return pl.pallas_call(
            fn, name=name, grid_spec=spec, out_shape=out_shape,
            input_output_aliases={i + 1: o for i, o in aliases.items()}, compiler_params=_params(dims))(prefetch, *args)

    if comm is None:
        return list(call(body, in_specs, out_specs, out_shape, list(scratch_shapes), aliases,
                         dims or ("arbitrary",) * len(grid), args))
    n_in, n_out, n_ci, n_co = len(in_specs), len(out_specs), len(comm.ins), len(comm.out_shapes)

    def wrapped(*refs):
        pre, refs = refs[:n_pre], refs[n_pre:]
        ins, cins = refs[:n_in], refs[n_in:n_in + n_ci]
        outs = refs[n_in + n_ci:n_in + n_ci + n_out]
        couts = refs[n_in + n_ci + n_out:n_in + n_ci + n_out + n_co]
        scratch, sems = refs[n_in + n_ci + n_out + n_co:-2], (*refs[-2:], 0)
        first = functools.reduce(jnp.logical_and, [pl.program_id(k) == 0 for k in range(len(grid))])
        last = functools.reduce(jnp.logical_and, [pl.program_id(k) == grid[k] - 1 for k in range(len(grid))])

        @pl.when(first)
        def _():
            comm.start(cins, couts, sems)

        body(*pre, *ins, *outs, *scratch)

        @pl.when(last)
        def _():
            comm.finish(cins, couts, sems)

    res = call(wrapped, in_specs + _any_specs(n_ci), out_specs + _any_specs(n_co), out_shape + comm.out_shapes,
               list(scratch_shapes) + comm.scratch(),
               {**aliases, **{n_in + i: n_out + o for i, o in comm.aliases.items()}},
               ("arbitrary",) * len(grid), (*args, *comm.ins))
    comm.deliver(list(res[n_out:]))
    return list(res[:n_out])


def _gather_comm(bufs, deliver):
    n = len(bufs)
    geoms = [_Geom(*b.shape[1:]) for b in bufs]

    def first(outs, sems):
        x, y, c, chips = _place()
        cps = []
        for a in range(n):
            mine = geoms[a].half_ref(outs[a], (2 * x + y,), c)
            cps += [_remote(mine, mine, sems, 6 * a + j, (cx, cy, c)) for j, (cx, cy) in enumerate(chips)]
        return cps

    def start(ins, outs, sems):
        for cp in first(outs, sems):
            cp.start()

    def finish(ins, outs, sems):
        x, y, c, chips = _place()
        sib = (x, y, 1 - c)
        passed = []
        for a in range(n):
            for j, (cx, cy) in enumerate(chips):
                land = geoms[a].half_ref(outs[a], (2 * cx + cy,), c)
                _remote(land, land, sems, 6 * a + j, sib).wait_recv()
                cp = _remote(land, land, sems, 6 * a + 3 + j, sib)
                cp.start()
                passed.append(cp)
        for a in range(n):
            for j, (cx, cy) in enumerate(chips):
                land = geoms[a].half_ref(outs[a], (2 * cx + cy,), 1 - c)
                _remote(land, land, sems, 6 * a + 3 + j, sib).wait_recv()
        for cp in first(outs, sems) + passed:
            cp.wait_send()

    return _Comm(bufs, [jax.ShapeDtypeStruct(b.shape, b.dtype) for b in bufs], {a: a for a in range(n)}, 6 * n,
                 start, finish, deliver)


def _reduce_pair_comm(grads, deliver):
    n = len(grads)
    geoms = [_Geom(*a.shape[1:]) for a in grads]

    def copies(ins, outs, sems):
        x, y, c, _ = _place()
        return [_remote(geoms[a].half_ref(ins[a], (slice(None),), 1 - c), outs[a], sems, a, (x, y, 1 - c))
                for a in range(n)]

    def start(ins, outs, sems):
        for cp in copies(ins, outs, sems):
            cp.start()

    def finish(ins, outs, sems):
        cps = copies(ins, outs, sems)
        for cp in cps:
            cp.wait_recv()
        for cp in cps:
            cp.wait_send()

    shapes = [jax.ShapeDtypeStruct((N_SHARD, g.h0, g.h1), a.dtype) for a, g in zip(grads, geoms)]
    return _Comm(grads, shapes, {}, n, start, finish, deliver)


def _reduce_chips_comm(parts, deliver):
    n = len(parts)

    def copies(ins, outs, sems):
        x, y, c, chips = _place()
        return [_remote(ins[a].at[2 * cx + cy], outs[a].at[j], sems, 3 * a + j, (cx, cy, c))
                for a in range(n) for j, (cx, cy) in enumerate(chips)]

    def start(ins, outs, sems):
        for cp in copies(ins, outs, sems):
            cp.start()

    def finish(ins, outs, sems):
        cps = copies(ins, outs, sems)
        for cp in cps:
            cp.wait_recv()
        for cp in cps:
            cp.wait_send()

    shapes = [jax.ShapeDtypeStruct((3,) + a.shape[1:], a.dtype) for a in parts]
    return _Comm(parts, shapes, {}, 3 * n, start, finish, deliver)


def _share_pair_comm(bufs, deliver):
    n = len(bufs)
    geoms = [_Geom(*b.shape) for b in bufs]

    def copies(outs, sems):
        x, y, c, _ = _place()
        cps = []
        for a in range(n):
            mine = geoms[a].half_ref(outs[a], (), c)
            cps.append(_remote(mine, mine, sems, a, (x, y, 1 - c)))
        return cps

    def start(ins, outs, sems):
        for cp in copies(outs, sems):
            cp.start()

    def finish(ins, outs, sems):
        x, y, c, _ = _place()
        for a in range(n):
            land = geoms[a].half_ref(outs[a], (), 1 - c)
            _remote(land, land, sems, a, (x, y, 1 - c)).wait_recv()
        for cp in copies(outs, sems):
            cp.wait_send()

    return _Comm(bufs, [jax.ShapeDtypeStruct(b.shape, b.dtype) for b in bufs], {a: a for a in range(n)}, n,
                 start, finish, deliver)


def _small_comm(pack, deliver):
    r, d = pack.shape

    def copies(ins, outs, sems):
        x, y, c, _ = _place()
        cps = []
        for k in range(1, 8):
            peer = (x ^ ((k >> 2) & 1), y ^ ((k >> 1) & 1), c ^ (k & 1))
            cps.append(_remote(ins[0], outs[0].at[4 * x + 2 * y + c], sems, k - 1, peer))
        return cps

    def start(ins, outs, sems):
        for cp in copies(ins, outs, sems):
            cp.start()

    def finish(ins, outs, sems):
        cps = copies(ins, outs, sems)
        for cp in cps:
            cp.wait_recv()
        for cp in cps:
            cp.wait_send()

    return _Comm([pack], [jax.ShapeDtypeStruct((8, r, d), pack.dtype)], {}, 7, start, finish, deliver)


def _join_comms(comms):
    comms = [c for c in comms if c is not None]
    if len(comms) <= 1:
        return comms[0] if comms else None
    ins, out_shapes, aliases, spans, n_sems = [], [], {}, [], 0
    for c in comms:
        aliases.update({len(ins) + i: len(out_shapes) + o for i, o in c.aliases.items()})
        spans.append((len(ins), len(ins) + len(c.ins), len(out_shapes), len(out_shapes) + len(c.out_shapes), n_sems))
        ins += c.ins
        out_shapes += c.out_shapes
        n_sems += c.n_sems

    def run(which):
        def go(all_ins, all_outs, sems):
            for c, (i0, i1, o0, o1, base) in zip(comms, spans):
                getattr(c, which)(all_ins[i0:i1], all_outs[o0:o1], (sems[0], sems[1], sems[2] + base))
        return go

    def deliver(outs):
        for c, (_, _, o0, o1, _) in zip(comms, spans):
            c.deliver(outs[o0:o1])
        return outs

    return _Comm(ins, out_shapes, aliases, n_sems, run("start"), run("finish"), deliver)


def _ew_call(name, fn, ins, out_dtypes):
    shape = ins[0].shape
    cols = shape[-1]
    rows = 1
    for s_ in shape[:-1]:
        rows *= s_
    ins2 = [a.reshape(rows, cols) for a in ins]
    tr = rows
    for t in range(16, min(rows, max(16, (1 << 19) // cols)) + 1, 16):
        if rows % t == 0:
            tr = t
    no = len(out_dtypes)

    def body(*refs):
        outs = fn(*[r[...] for r in refs[:len(ins2)]])
        for ref, val in zip(refs[len(ins2):], outs):
            ref[...] = val.astype(ref.dtype)

    spec = pl.BlockSpec((tr, cols), lambda i: (i, 0))
    res = pl.pallas_call(
        body, name=name, grid=(rows // tr,), in_specs=[spec] * len(ins2), out_specs=[spec] * no,
        out_shape=[jax.ShapeDtypeStruct((rows, cols), dt) for dt in out_dtypes],
        compiler_params=_params(("parallel",)))(*ins2)
    return [r.reshape(shape) for r in res]


def _adamw_math(w, g, m, v):
    m = ADAM_B1 * m + (1.0 - ADAM_B1) * g
    v = ADAM_B2 * v + (1.0 - ADAM_B2) * (g * g)
    m_hat = m / (1.0 - ADAM_B1 ** ADAM_STEP)
    v_hat = v / (1.0 - ADAM_B2 ** ADAM_STEP)
    delta = -ADAM_LR * (m_hat / (jnp.sqrt(v_hat) + ADAM_EPS) + ADAM_WD * w)
    return delta, m, v


def _adamw_call(name, w, g, m, v):
    return _ew_call(name, _adamw_math, [w, g, m, v], [F32, F32, F32])


def _tiled_call(name, fn, place, grid, in_items, out_items, comm=None):
    ni = len(in_items)

    def body(place_ref, *refs):
        vals = fn(*[r[...] for r in refs[:ni]])
        for ref, val in zip(refs[ni:], vals):
            ref[...] = val.astype(ref.dtype)

    return _hosted_call(
        body, name=name, grid=grid, in_specs=[pl.BlockSpec(blk, imap) for _, blk, imap in in_items],
        out_specs=[pl.BlockSpec(blk, imap) for _, _, blk, imap in out_items],
        out_shape=[jax.ShapeDtypeStruct(shp, dt) for shp, dt, _, _ in out_items],
        args=[a for a, _, _ in in_items], prefetch=place, comm=comm)


def _cast_call(name, place, shards, comm=None):
    n0, n1 = shards[0].shape
    tr = _fit_rows(n0, n1, 1.5 * len(shards))
    ins = [(a, (tr, n1), lambda i, p: (i, 0)) for a in shards]
    outs = [((N_SHARD, n0, n1), BF16, (1, tr, n1), lambda i, p: (p[0], i, 0)) for _ in shards]
    return _tiled_call(name, lambda *v: [x[None] for x in v], place, (n0 // tr,), ins, outs, comm)


def _pair_sum_call(name, place, fulls, gots):
    k = len(fulls)
    g = _Geom(*fulls[0].shape[1:], blocks=2.5 * k)
    blk = (1, g.tr, g.h1)
    ins = [(a, blk, lambda s, i, p: g.half_block((s,), i, p[1])) for a in fulls]
    ins += [(a, blk, lambda s, i, p: (s, i, 0)) for a in gots]
    outs = [((N_SHARD, g.h0, g.h1), BF16, blk, lambda s, i, p: (s, i, 0)) for _ in fulls]
    return _tiled_call(name, lambda *v: [v[j] + v[k + j] for j in range(k)], place, (N_SHARD, g.nblk), ins, outs)


def _chip_sum_call(name, place, fulls, gots, recvs, comm=None):
    k = len(fulls)
    g = _Geom(*fulls[0].shape[1:], blocks=4.5 * k)
    blk = (1, g.tr, g.h1)
    ins = [(a, blk, lambda i, p: g.half_block((p[0],), i, p[1])) for a in fulls]
    ins += [(a, blk, lambda i, p: (p[0], i, 0)) for a in gots]
    ins += [(a, (3, g.tr, g.h1), lambda i, p: (0, i, 0)) for a in recvs]
    outs = [((g.n0, g.n1), F32, (g.tr, g.h1), lambda i, p: g.half_block((), i, p[1])) for _ in fulls]

    def fn(*v):
        res = []
        for j in range(k):
            r = v[2 * k + j].astype(F32)
            res.append(v[j][0] + v[k + j][0] + r[0] + r[1] + r[2])
        return res

    return _tiled_call(name, fn, place, (g.nblk,), ins, outs, comm)


def _adamw_group_call(name, ws, gs, ms, vs, comm=None):
    k = len(ws)
    n0, n1 = ws[0].shape
    tr = _fit_rows(n0, n1, 8 * k)
    spec = pl.BlockSpec((tr, n1), lambda i: (i, 0))

    def body(*refs):
        for j in range(k):
            g = refs[k + j][...]
            delta, m, vv = _adamw_math(refs[j][...], g, refs[2 * k + j][...], refs[3 * k + j][...])
            for ref, val in zip(refs[4 * k + 4 * j:4 * k + 4 * j + 4], (g, delta, m, vv)):
                ref[...] = val

    flat = _hosted_call(
        body, name=name, grid=(n0 // tr,), in_specs=[spec] * (4 * k), out_specs=[spec] * (4 * k),
        out_shape=[jax.ShapeDtypeStruct((n0, n1), F32)] * (4 * k), dims=("parallel",),
        args=(*ws, *gs, *ms, *vs), comm=comm)
    return [flat[4 * j:4 * j + 4] for j in range(k)]


def _small_update_call(me, early, own_early, late, own_late, wp, mp, vp):
    nd, r, d = early.shape

    def body(me_ref, e_ref, oe_ref, l_ref, ol_ref, w_ref, m_ref, v_ref, gs_ref, d_ref, nm_ref, nv_ref):
        mine = me_ref[0]

        def total(g_ref, own_ref):
            acc = None
            for k in range(nd):
                part = jnp.where(mine == k, own_ref[...], g_ref[k])
                acc = part if acc is None else acc + part
            return acc

        gs = total(e_ref, oe_ref)
        ls = total(l_ref, ol_ref)
        gs_ref[...] = gs
        first = gs[0:8] + ls[0:8]
        gs_ref[0:8, :] = first
        gs_ref[ROW_META:ROW_META + N_META, :] = gs[ROW_META:ROW_META + N_META] + ls[8:8 + N_META]
        grads = jnp.concatenate([first, gs[8:SMALL_ADAM_ROWS]], axis=0)
        delta, m, v = _adamw_math(w_ref[...], grads, m_ref[...], v_ref[...])
        d_ref[...] = delta
        nm_ref[...] = m
        nv_ref[...] = v

    vm = pl.BlockSpec(memory_space=pltpu.VMEM)
    ashape = jax.ShapeDtypeStruct((SMALL_ADAM_ROWS, d), F32)
    return pl.pallas_call(
        body, name="small_update", in_specs=[pl.BlockSpec(memory_space=pltpu.SMEM)] + [vm] * 7, out_specs=[vm] * 4,
        out_shape=[jax.ShapeDtypeStruct((r, d), F32), ashape, ashape, ashape],
        compiler_params=pltpu.CompilerParams(vmem_limit_bytes=VMEM_LIMIT_BYTES))(
            me, early, own_early, late, own_late, wp, mp, vp)


SMALL_NAMES = ["ffn1_norm", "mix_norm", "ffn2_norm", "final_norm", "q_latent_norm", "kv_latent_norm",
               "q_head_norm", "k_head_norm", "conv_b", "gate_a_b", "gate_x_b", "lru_lambda", "attn_out_norm",
               "lru_out_norm"]
ROW_CONV_W = 14
ROW_GATE_A = 16
ROW_GATE_X = 48
ROW_META = 80
ROW_LOSS = 96


def _row(a):
    flat = a.reshape(1, -1)
    return jnp.pad(flat, ((0, 0), (0, D_MODEL - flat.shape[1])))


def _pack_small(t, rows):
    parts = [_row(t[nm]) for nm in SMALL_NAMES]
    parts.append(t["conv_w"].reshape(2, D_MODEL))
    parts.append(t["gate_a_w"].reshape(32, D_MODEL))
    parts.append(t["gate_x_w"].reshape(32, D_MODEL))
    p = jnp.concatenate(parts, axis=0)
    return jnp.pad(p, ((0, rows - p.shape[0]), (0, 0)))


def _early_pack(g):
    gs = {nm: g.get(nm, jnp.zeros((1, D_MODEL), F32)) for nm in SMALL_NAMES}
    gs["q_head_norm"] = g["q_head_norm"][:, 0:D_QK]
    gs["k_head_norm"] = g["k_head_norm"][:, 0:D_QK]
    for nm in ("conv_b", "gate_a_b", "gate_x_b", "lru_lambda"):
        gs[nm] = g[nm].reshape(1, LRU_W)
    gs["conv_w"] = g["conv_w"].transpose(1, 0, 2).reshape(CONV_K, LRU_W)
    gs["gate_a_w"] = _gate_blocks(g["gate_a_w"])
    gs["gate_x_w"] = _gate_blocks(g["gate_x_w"])
    return jnp.concatenate([_pack_small(gs, ROW_META), jnp.zeros((N_META, D_MODEL), F32), _row(g["loss"][:, 0:1]),
                            jnp.zeros((SMALL_ROWS - ROW_LOSS - 1, D_MODEL), F32)], axis=0)


def _unpack_small(p, like):
    out = {}
    for k, nm in enumerate(SMALL_NAMES):
        out[nm] = p[k, 0:like[nm].size].reshape(like[nm].shape)
    out["gate_a_w"] = p[ROW_GATE_A:ROW_GATE_A + 32].reshape(like["gate_a_w"].shape)
    out["gate_x_w"] = p[ROW_GATE_X:ROW_GATE_X + 32].reshape(like["gate_x_w"].shape)
    return out


def _gate_dense(wg):
    w4 = wg[0].reshape(N_LRU_TILES, 2, 64, 64)
    zero = jnp.zeros((N_LRU_TILES, 64, 64), wg.dtype)
    top = jnp.concatenate([w4[:, 0], zero], axis=2)
    bot = jnp.concatenate([zero, w4[:, 1]], axis=2)
    return jnp.concatenate([top, bot], axis=1).astype(BF16)


def _gate_blocks(dw):
    return jnp.stack([dw[:, 0:64, 0:64], dw[:, 64:128, 64:128]], axis=1).reshape(8, 64, 64)


BIG_NAMES = ["ffn1_w_gate", "ffn1_w_up", "ffn1_w_down", "w_in", "w_uq", "w_uk", "w_uv", "w_out", "ffn2_w_gate",
             "ffn2_w_up", "ffn2_w_down"]
BIG_GROUPS = [["ffn1_w_gate", "ffn1_w_up", "ffn1_w_down", "ffn2_w_gate", "ffn2_w_up", "ffn2_w_down"], ["w_in"],
              ["w_uq"], ["w_uk", "w_uv"], ["w_out"]]
TRANSPOSED = ("ffn1_w_gate", "ffn1_w_up", "ffn2_w_gate", "ffn2_w_up", "w_in", "w_uq")


def _to2d(nm, a):
    return a[0].T if nm in TRANSPOSED else a[0]


def _from2d(nm, a):
    return (a.T if nm in TRANSPOSED else a)[None]


WEIGHT_NAMES = ["meta_tokens", "ffn1_norm", "ffn1_w_gate", "ffn1_w_up", "ffn1_w_down", "mix_norm", "w_in",
                "q_latent_norm", "w_uq", "kv_latent_norm", "w_uk", "w_uv", "q_head_norm", "k_head_norm", "conv_w",
                "conv_b", "gate_a_w", "gate_a_b", "gate_x_w", "gate_x_b", "lru_lambda", "attn_out_norm",
                "lru_out_norm", "w_out", "ffn2_norm", "ffn2_w_gate", "ffn2_w_up", "ffn2_w_down", "final_norm"]


def _weight_from(nm, slots):
    if nm == "w_in":
        win = slots.reshape(IN_WIDTH, D_MODEL)
        return jnp.concatenate([win[0:Z_KR + D_ROPE], jnp.zeros((128 - D_ROPE, D_MODEL), BF16),
                                win[Z_KR + D_ROPE:]], axis=0)
    if nm == "w_uq":
        return jnp.pad(slots, ((0, 0), (0, D_QKP - D_QK), (0, 0))).reshape(HEADS * D_QKP, Q_RANK)
    if nm in ("w_uk", "w_uv"):
        return slots.transpose(1, 0, 2).reshape(KV_RANK, HEADS * D_NOPE)
    if nm == "w_out":
        return slots.reshape(D_MODEL, D_MODEL)
    return slots


def _small_weights(p, small):
    w = {nm: p[nm] for nm in SMALL_NAMES}
    w["q_head_norm"] = jnp.pad(p["q_head_norm"], ((0, 0), (0, D_QKP - D_QK)))
    w["k_head_norm"] = jnp.pad(p["k_head_norm"], ((0, 0), (0, D_QKP - D_QK)))
    w["conv_w"] = small[:, N_META:N_META + 2, :].reshape(N_SHARD, CONV_K, LRU_TILE)
    w["gate_a_w"] = _gate_dense(p["gate_a_w"])
    w["gate_x_w"] = _gate_dense(p["gate_x_w"])
    meta = small[:, 0:N_META, :].transpose(1, 0, 2).reshape(N_META, D_MODEL)
    return w, meta


def _full_weights(p, gathered, small):
    w, meta = _small_weights(p, small)
    w.update({nm: _weight_from(nm, gathered[nm]) for nm in BIG_NAMES})
    return w, meta


def _shard_grad(nm, g):
    if nm == "w_in":
        return jnp.concatenate([g[0:Z_KR + D_ROPE], g[Z_MLA:]], axis=0).reshape(N_SHARD, IN_WIDTH // N_SHARD, D_MODEL)
    if nm == "w_uq":
        return g.reshape(HEADS, D_QKP, Q_RANK)[:, 0:D_QK, :]
    if nm in ("w_uk", "w_uv"):
        return g.reshape(KV_RANK, HEADS, D_NOPE).transpose(1, 0, 2)
    if nm == "w_out":
        return g.reshape(N_SHARD, D_MODEL // N_SHARD, D_MODEL)
    return g


def _shard_grads(g):
    return {nm: _shard_grad(nm, g[nm]) for nm in BIG_NAMES}


GATHER_FIRST = ["ffn1_w_gate"]
GATHER_AT = {"ffn1_gate": ["ffn1_w_up"], "ffn1_upact": ["ffn1_w_down"],
             "ffn1_down": ["w_in", "w_uq", "w_uk", "w_uv", "w_out"], "attn_fwd": ["ffn2_w_down", "ffn2_w_gate"],
             "lru_fwd": ["ffn2_w_up"]}
PAIR_AT = [("ffn2_din", ["ffn2_w_gate", "ffn2_w_up", "ffn2_w_down"]),
           ("mix_din", ["w_out", "w_uq", "w_uk", "w_uv", "w_in"]),
           ("ffn1_dwg", ["ffn1_w_down"]), ("ffn1_dwu", ["ffn1_w_gate"]), ("ffn1_din_a", ["ffn1_w_up"])]
CHIPS_AT = [("attn_bwd", ["ffn2_w_down", "ffn2_w_gate"]), ("mla_prep_bwd", ["ffn2_w_up"]),
            ("ffn1_dact", ["w_out", "w_uq", "w_uk", "w_uv", "w_in"]),
            ("ffn1_dwu", ["ffn1_w_down"]), ("ffn1_din_a", ["ffn1_w_gate"]), ("ffn1_din_b", ["ffn1_w_up"])]
SHARE_EARLY_GROUPS, SHARE_EARLY_AT = 3, "ffn1_dwd"
SMALL_EARLY_AT = "mix_din"


def _same_shape_groups(names):
    return [[nm for nm in grp if nm in names] for grp in BIG_GROUPS if any(nm in names for nm in grp)]


class _Sched:
    def __init__(self, place, w, slots):
        self.place, self.w, self.slots = place, w, slots
        self.g = None
        self.sharded, self.from_pair, self.chip_bf16, self.from_chips = {}, {}, {}, {}
        self.early = self.early_all = None
        self.shared = {}

    def host(self, stage):
        comms = []
        if stage in GATHER_AT:
            comms.append(self.gather(GATHER_AT[stage]))
        comms += [self.chips(names) for at, names in CHIPS_AT if at == stage]
        comms += [self.pair(names) for at, names in PAIR_AT if at == stage]
        if stage == SMALL_EARLY_AT:
            comms.append(self.small_early())
        if stage == SHARE_EARLY_AT:
            comms.append(self.share_early())
        return _join_comms(comms)

    def small_early(self):
        self.early = _early_pack(self.g)

        def deliver(outs):
            self.early_all = outs[0]
            return outs

        return _small_comm(self.early, deliver)

    def gather(self, names):
        def deliver(outs):
            self.w.update({nm: _weight_from(nm, o) for nm, o in zip(names, outs)})
            return outs

        return _gather_comm([self.slots[nm] for nm in names], deliver)

    def pair(self, names):
        self.sharded.update({nm: _shard_grad(nm, self.g[nm]) for nm in names})

        def deliver(outs):
            self.from_pair.update(zip(names, outs))
            for grp in _same_shape_groups(names):
                sums = _pair_sum_call("pair_sum_" + grp[0], self.place, [self.sharded[nm] for nm in grp],
                                      [self.from_pair[nm] for nm in grp])
                self.chip_bf16.update(zip(grp, sums))
            return outs

        return _reduce_pair_comm([self.sharded[nm] for nm in names], deliver)

    def chips(self, names):
        def deliver(outs):
            self.from_chips.update(zip(names, outs))
            return outs

        return _reduce_chips_comm([self.chip_bf16[nm] for nm in names], deliver)

    def chip_sums(self, names, comm=None):
        out = {}
        for grp in _same_shape_groups(names):
            sums = _chip_sum_call("chip_sum_" + grp[0], self.place, [self.sharded[nm] for nm in grp],
                                  [self.from_pair[nm] for nm in grp], [self.from_chips[nm] for nm in grp], comm)
            comm = None
            out.update(zip(grp, sums))
        return out

    def share_early(self):
        names = [nm for at, grp in CHIPS_AT[:SHARE_EARLY_GROUPS] for nm in grp]
        mine = self.chip_sums(names)
        return _share_pair_comm([mine[nm] for nm in names], lambda o: self.shared.update(zip(names, o)))


def kernel(x, meta_tokens, ffn1_norm, ffn1_w_gate, ffn1_w_up, ffn1_w_down, mix_norm, w_in, q_latent_norm, w_uq, kv_latent_norm, w_uk, w_uv, q_head_norm, k_head_norm, conv_w, conv_b, gate_a_w, gate_a_b, gate_x_w, gate_x_b, lru_lambda, attn_out_norm, lru_out_norm, w_out, ffn2_norm, ffn2_w_gate, ffn2_w_up, ffn2_w_down, final_norm, loss_target, m_meta_tokens, m_ffn1_norm, m_ffn1_w_gate, m_ffn1_w_up, m_ffn1_w_down, m_mix_norm, m_w_in, m_q_latent_norm, m_w_uq, m_kv_latent_norm, m_w_uk, m_w_uv, m_q_head_norm, m_k_head_norm, m_conv_w, m_conv_b, m_gate_a_w, m_gate_a_b, m_gate_x_w, m_gate_x_b, m_lru_lambda, m_attn_out_norm, m_lru_out_norm, m_w_out, m_ffn2_norm, m_ffn2_w_gate, m_ffn2_w_up, m_ffn2_w_down, m_final_norm, v_meta_tokens, v_ffn1_norm, v_ffn1_w_gate, v_ffn1_w_up, v_ffn1_w_down, v_mix_norm, v_w_in, v_q_latent_norm, v_w_uq, v_kv_latent_norm, v_w_uk, v_w_uv, v_q_head_norm, v_k_head_norm, v_conv_w, v_conv_b, v_gate_a_w, v_gate_a_b, v_gate_x_w, v_gate_x_b, v_lru_lambda, v_attn_out_norm, v_lru_out_norm, v_w_out, v_ffn2_norm, v_ffn2_w_gate, v_ffn2_w_up, v_ffn2_w_down, v_final_norm):
    args = locals()
    p = {nm: args[nm] for nm in WEIGHT_NAMES}
    mom = {nm: args["m_" + nm] for nm in WEIGHT_NAMES}
    var = {nm: args["v_" + nm] for nm in WEIGHT_NAMES}
    nb, seq, d = x.shape
    lp = CHUNK + seq
    xi, yi, ci = lax.axis_index("x"), lax.axis_index("y"), lax.axis_index("c")
    chip = 2 * xi + yi

    place = jnp.stack([chip, ci]).astype(jnp.int32)
    p2 = {nm: _to2d(nm, p[nm]) for nm in BIG_NAMES}
    m2 = {nm: _to2d(nm, mom[nm]) for nm in BIG_NAMES}
    v2 = {nm: _to2d(nm, var[nm]) for nm in BIG_NAMES}

    slots = {}
    small_shard = jnp.concatenate(
        [meta_tokens, conv_w[0].reshape(2, 2 * LRU_TILE), jnp.zeros((14, 2 * LRU_TILE), F32)], axis=0)
    small_slots = lax.dynamic_update_slice(jnp.zeros((N_SHARD,) + small_shard.shape, F32), small_shard[None],
                                           (chip, 0, 0))
    for nm, buf in zip(GATHER_FIRST, _cast_call("cast_first", place, [p2[nm] for nm in GATHER_FIRST])):
        slots[nm] = buf
    first = []
    comm = _gather_comm([slots[nm] for nm in GATHER_FIRST] + [small_slots], first.extend)
    for grp in BIG_GROUPS:
        names = [nm for nm in grp if nm not in GATHER_FIRST]
        for nm, buf in zip(names, _cast_call("cast_" + names[0], place, [p2[nm] for nm in names], comm)):
            slots[nm] = buf
        comm = None
    w, meta = _small_weights(p, first[-1])
    w.update({nm: _weight_from(nm, o) for nm, o in zip(GATHER_FIRST, first[:-1])})
    sched = _Sched(place, w, slots)

    h0 = jnp.concatenate(
        [jnp.zeros((nb, PAD_ROWS, d), F32), jnp.broadcast_to(meta[None], (nb, N_META, d)), x], axis=1)
    target = jnp.pad(loss_target, ((0, 0), (CHUNK, 0), (0, 0)))
    loss_part, dh0, g = _local_step(h0.reshape(nb * lp, d), target.reshape(nb * lp, d), w, nb, lp, sched)
    dh0 = dh0.reshape(nb, lp, d)
    grad_x = dh0[:, CHUNK:, :]

    late = jnp.concatenate([g["ffn1_norm"], g["mix_norm"], jnp.zeros((6, D_MODEL), F32),
                            jnp.sum(dh0[:, PAD_ROWS:CHUNK, :], axis=0)], axis=0)
    shared = sched.shared
    rest = [nm for at, names in CHIPS_AT[SHARE_EARLY_GROUPS:] if at is not None for nm in names]
    last = [nm for at, names in CHIPS_AT if at is None for nm in names]
    late_box = {}
    mine = sched.chip_sums(rest, _small_comm(late, lambda o: late_box.update(all=o[0])))
    share = _share_pair_comm([mine[nm] for nm in rest], lambda o: shared.update(zip(rest, o)))
    _comm_call("share_pair", _join_comms([share, sched.chips(last) if last else None]))
    if last:
        mine = sched.chip_sums(last)
        _comm_call("share_last", _share_pair_comm([mine[nm] for nm in last], lambda o: shared.update(zip(last, o))))
    late_all = late_box["all"]
    small_like = {nm: p[nm] for nm in SMALL_NAMES + ["gate_a_w", "gate_x_w"]}

    def pack_w(t):
        tt = {nm: t[nm] for nm in SMALL_NAMES + ["gate_a_w", "gate_x_w"]}
        tt["conv_w"] = jnp.zeros((CONV_K, LRU_W), F32)
        return _pack_small(tt, SMALL_ADAM_ROWS)

    me = (4 * xi + 2 * yi + ci).astype(jnp.int32).reshape(1)
    gsum, dsm, msm, vsm = _small_update_call(me, sched.early_all, sched.early, late_all, late, pack_w(p),
                                             pack_w(mom), pack_w(var))
    grads = _unpack_small(gsum, small_like)
    delta = _unpack_small(dsm, small_like)
    new_m = _unpack_small(msm, small_like)
    new_v = _unpack_small(vsm, small_like)
    loss = gsum[ROW_LOSS, 0]
    gmeta = gsum[ROW_META:ROW_META + N_META].reshape(N_META, N_SHARD, D_MODEL // N_SHARD)
    grads["meta_tokens"] = lax.dynamic_index_in_dim(gmeta, chip, axis=1, keepdims=False)
    gconv = gsum[ROW_CONV_W:ROW_CONV_W + 2].reshape(CONV_K, N_SHARD, LRU_TILE)
    grads["conv_w"] = lax.dynamic_index_in_dim(gconv, chip, axis=1, keepdims=False)[None]
    for nm in ("meta_tokens", "conv_w"):
        delta[nm], new_m[nm], new_v[nm] = _adamw_call("adamw_" + nm, p[nm], grads[nm], mom[nm], var[nm])

    for names in BIG_GROUPS:
        res = _adamw_group_call("adamw_" + names[0], [p2[nm] for nm in names], [shared[nm] for nm in names],
                                [m2[nm] for nm in names], [v2[nm] for nm in names])
        for nm, (gg, dd, mm, vv) in zip(names, res):
            grads[nm], delta[nm], new_m[nm], new_v[nm] = (_from2d(nm, t) for t in (gg, dd, mm, vv))

    return (loss, grad_x, *[grads[nm] for nm in WEIGHT_NAMES], *[delta[nm] for nm in WEIGHT_NAMES],
            *[new_m[nm] for nm in WEIGHT_NAMES], *[new_v[nm] for nm in WEIGHT_NAMES])
```

```python
import functools
import math

import jax
import jax.numpy as jnp
import numpy as np
from jax import lax
from jax.experimental import pallas as pl
from jax.experimental.pallas import tpu as pltpu

F32 = jnp.float32
BF16 = jnp.bfloat16
MESH = pl.DeviceIdType.MESH

D_MODEL = 1024
N_META = 16
CHUNK = 64
PAD_ROWS = CHUNK - N_META
HEADS = 4
D_NOPE = 128
D_ROPE = 64
D_QK = D_NOPE + D_ROPE
D_QKP = 256
D_V = 128
KV_RANK = 256
Q_RANK = 384
MLA_W = HEADS * D_V
LRU_W = 512
LRU_TILE = 128
N_LRU_TILES = LRU_W // LRU_TILE
CONV_K = 4
C_RGLRU = 8.0
ROPE_THETA = 10000.0
D_FF = 2816
N_SHARD = 4
EPS = 1e-6
NEG_INF = -1e30
Z_KR = Q_RANK + KV_RANK
Z_MLA = Z_KR + 128
Z_U = Z_MLA
Z_G = Z_U + LRU_W
Z_W = Z_G + LRU_W
IN_WIDTH = Q_RANK + KV_RANK + D_ROPE + 2 * LRU_W

ADAM_LR = 0.001
ADAM_B1 = 0.9
ADAM_B2 = 0.999
ADAM_EPS = 1e-08
ADAM_WD = 0.01
ADAM_STEP = 10

VMEM_LIMIT_BYTES = 56 * 1024 * 1024
SMALL_ROWS = 104
SMALL_ADAM_ROWS = 80


def _params(sem):
    return pltpu.CompilerParams(dimension_semantics=sem, vmem_limit_bytes=VMEM_LIMIT_BYTES)


def _resident(shape):
    return pl.BlockSpec(tuple(shape), lambda i: (0,) * len(shape), pipeline_mode=pl.Buffered(1))


def _row_tile(rows, target):
    best = 16
    for t in range(16, min(rows, target) + 1, 16):
        if rows % t == 0:
            best = t
    return best


def _col_tile(cols, target):
    best = cols
    for t in range(128, min(cols, target) + 1, 128):
        if cols % t == 0:
            best = t
    return best


def _dot(a, b):
    return jnp.dot(a, b, preferred_element_type=F32)


def _dot_nt(a, b):
    return lax.dot_general(a, b, (((1,), (1,)), ((), ())), preferred_element_type=F32)


def _dot_tn(a, b):
    return lax.dot_general(a, b, (((0,), (0,)), ((), ())), preferred_element_type=F32)


def _rms(x, n):
    return lax.rsqrt(jnp.sum(x * x, axis=-1, keepdims=True) * (1.0 / n) + EPS)


def _rms_bwd(dn, nrm, r, n):
    return r * (dn - nrm * (jnp.sum(dn * nrm, axis=-1, keepdims=True) * (1.0 / n)))


def _gelu(x):
    k = math.sqrt(2.0 / math.pi)
    t = jnp.tanh(k * (x + 0.044715 * x * x * x))
    return 0.5 * x * (1.0 + t), t


def _gelu_grad(x, t):
    k = math.sqrt(2.0 / math.pi)
    return 0.5 * (1.0 + t) + 0.5 * x * (1.0 - t * t) * k * (1.0 + 3.0 * 0.044715 * x * x)


def _sigmoid(x):
    return 0.5 + 0.5 * jnp.tanh(0.5 * x)


def _softplus_neg(lam):
    e = jnp.exp(-jnp.abs(lam))
    log1p = jnp.where(e < 0.01, e * (1.0 - e * (0.5 - e * (1.0 / 3 - e * 0.25))), jnp.log(1.0 + e))
    return jnp.maximum(-lam, 0.0) + log1p


def _rope(t, c, s1, s2):
    return t * c + pltpu.roll(t, 96, 1) * s1 + pltpu.roll(t, 32, 1) * s2


def _rope_t(d, c, s1, s2):
    return d * c + pltpu.roll(d * s1, 32, 1) + pltpu.roll(d * s2, 96, 1)


def _rope_tables(lp):
    pos = (np.arange(lp, dtype=np.int32) - PAD_ROWS).astype(np.float32)
    inv_freq = (ROPE_THETA ** (-np.arange(0, D_ROPE // 2, dtype=np.float32) / (D_ROPE // 2))).astype(np.float32)
    ang = (pos[:, None] * inv_freq[None, :]).astype(np.float32).astype(np.float64)
    cos, sin = np.cos(ang).astype(np.float32), np.sin(ang).astype(np.float32)
    z = np.zeros_like(cos)
    return (jnp.asarray(np.concatenate([cos, cos, z, z], 1)), jnp.asarray(np.concatenate([-sin, z, z, z], 1)),
            jnp.asarray(np.concatenate([z, sin, z, z], 1)))


def _rmsnorm_call(name, h, g, tm):
    rows, d = h.shape

    def body(h_ref, g_ref, o_ref):
        x = h_ref[...]
        o_ref[...] = (x * _rms(x, d) * g_ref[...]).astype(BF16)

    return pl.pallas_call(
        body, name=name, grid=(rows // tm,),
        in_specs=[pl.BlockSpec((tm, d), lambda i: (i, 0)), pl.BlockSpec((1, d), lambda i: (0, 0))],
        out_specs=pl.BlockSpec((tm, d), lambda i: (i, 0)),
        out_shape=jax.ShapeDtypeStruct((rows, d), BF16),
        compiler_params=_params(("parallel",)))(h, g)


def _ffn_up_call(name, u, wg, wu, tm, comm=None):
    rows, d = u.shape
    ns, fs, _ = wg.shape

    def body(u_ref, wg_ref, wu_ref, g_ref, p_ref, a_ref):
        uu = u_ref[...]
        g = _dot_nt(uu, wg_ref[0])
        p = _dot_nt(uu, wu_ref[0])
        g_ref[0] = g.astype(BF16)
        p_ref[0] = p.astype(BF16)
        a_ref[0] = (g * jax.nn.sigmoid(g) * p).astype(BF16)

    wspec = pl.BlockSpec((1, fs, d), lambda s, i: (s, 0, 0))
    ospec = pl.BlockSpec((1, tm, fs), lambda s, i: (s, i, 0))
    oshape = jax.ShapeDtypeStruct((ns, rows, fs), BF16)
    return _hosted_call(
        body, name=name, grid=(ns, rows // tm),
        in_specs=[pl.BlockSpec((tm, d), lambda s, i: (i, 0)), wspec, wspec],
        out_specs=[ospec, ospec, ospec], out_shape=[oshape, oshape, oshape],
        dims=("parallel", "parallel"), args=(u, wg, wu), comm=comm)


def _ffn_gate_call(name, u, wg, tm, comm=None):
    rows, d = u.shape
    ns, fs, _ = wg.shape

    def body(u_ref, wg_ref, g_ref):
        g_ref[0] = _dot_nt(u_ref[...], wg_ref[0]).astype(BF16)

    return _hosted_call(
        body, name=name, grid=(ns, rows // tm),
        in_specs=[pl.BlockSpec((tm, d), lambda s, i: (i, 0)), pl.BlockSpec((1, fs, d), lambda s, i: (s, 0, 0))],
        out_specs=[pl.BlockSpec((1, tm, fs), lambda s, i: (s, i, 0))],
        out_shape=[jax.ShapeDtypeStruct((ns, rows, fs), BF16)],
        dims=("parallel", "parallel"), args=(u, wg), comm=comm)[0]


def _ffn_upact_call(name, u, wu, gate, tm, comm=None):
    rows, d = u.shape
    ns, fs, _ = wu.shape

    def body(u_ref, wu_ref, g_ref, p_ref, a_ref):
        p = _dot_nt(u_ref[...], wu_ref[0])
        g = g_ref[0].astype(F32)
        p_ref[0] = p.astype(BF16)
        a_ref[0] = (g * jax.nn.sigmoid(g) * p).astype(BF16)

    ospec = pl.BlockSpec((1, tm, fs), lambda s, i: (s, i, 0))
    oshape = jax.ShapeDtypeStruct((ns, rows, fs), BF16)
    return _hosted_call(
        body, name=name, grid=(ns, rows // tm),
        in_specs=[pl.BlockSpec((tm, d), lambda s, i: (i, 0)), pl.BlockSpec((1, fs, d), lambda s, i: (s, 0, 0)), ospec],
        out_specs=[ospec, ospec], out_shape=[oshape, oshape],
        dims=("parallel", "parallel"), args=(u, wu, gate), comm=comm)


def _ffn_down_call(name, a, wd, h, tm, comm=None):
    rows, d = h.shape
    ns, _, fs = a.shape

    def body(a_ref, wd_ref, h_ref, o_ref):
        acc = h_ref[...]
        for s in range(ns):
            acc = acc + 0.5 * _dot(a_ref[s], wd_ref[s])
        o_ref[...] = acc

    return _hosted_call(
        body, name=name, grid=(rows // tm,),
        in_specs=[pl.BlockSpec((ns, tm, fs), lambda i: (0, i, 0)),
                  _resident((ns, fs, d)),
                  pl.BlockSpec((tm, d), lambda i: (i, 0))],
        out_specs=[pl.BlockSpec((tm, d), lambda i: (i, 0))],
        out_shape=[jax.ShapeDtypeStruct((rows, d), F32)],
        dims=("parallel",), args=(a, wd, h), comm=comm)[0]


def _mm_call(name, a, bt, tm, out_dtype):
    rows, k = a.shape
    n = bt.shape[0]

    def body(a_ref, b_ref, o_ref):
        o_ref[...] = _dot_nt(a_ref[...], b_ref[...]).astype(out_dtype)

    return pl.pallas_call(
        body, name=name, grid=(rows // tm,),
        in_specs=[pl.BlockSpec((tm, k), lambda i: (i, 0)), pl.BlockSpec((n, k), lambda i: (0, 0))],
        out_specs=pl.BlockSpec((tm, n), lambda i: (i, 0)),
        out_shape=jax.ShapeDtypeStruct((rows, n), out_dtype),
        compiler_params=_params(("parallel",)))(a, bt)


def _mla_heads(z, gql, gkvl, wuq, wuk, wuv):
    cq = z[:, 0:Q_RANK]
    ckv = z[:, Q_RANK:Z_KR]
    kr = z[:, Z_KR:Z_MLA]
    rq = _rms(cq, Q_RANK)
    nq = cq * rq
    cqn = (nq * gql).astype(BF16)
    rkv = _rms(ckv, KV_RANK)
    nkv = ckv * rkv
    ckvn = (nkv * gkvl).astype(BF16)
    qraw = _dot_nt(cqn, wuq)
    knope = _dot(ckvn, wuk)
    v = _dot(ckvn, wuv)
    skr = jnp.sum(kr * kr, axis=-1, keepdims=True)
    heads = []
    for hd in range(HEADS):
        qh = qraw[:, hd * D_QKP:(hd + 1) * D_QKP]
        rqh = lax.rsqrt(jnp.sum(qh * qh, axis=-1, keepdims=True) * (1.0 / D_QK) + EPS)
        kn = knope[:, hd * D_NOPE:(hd + 1) * D_NOPE]
        rkh = lax.rsqrt((jnp.sum(kn * kn, axis=-1, keepdims=True) + skr) * (1.0 / D_QK) + EPS)
        heads.append((qh * rqh, rqh, kn * rkh, kr * rkh, rkh))
    return dict(rq=rq, nq=nq, cqn=cqn, rkv=rkv, nkv=nkv, ckvn=ckvn, v=v, heads=heads)


def _mla_prep_call(z, gql, gkvl, gqh, gkh, wuq, wuk, wuv, tabs, lp, tm):
    rows = z.shape[0]
    tpe = lp // tm

    def body(z_ref, gql_ref, gkvl_ref, gqh_ref, gkh_ref, wuq_ref, wuk_ref, wuv_ref, c_ref, s1_ref, s2_ref,
             q_ref, k_ref, v_ref, cqn_ref, ckvn_ref):
        m = _mla_heads(z_ref[...], gql_ref[...], gkvl_ref[...], wuq_ref[...], wuk_ref[...], wuv_ref[...])
        c, s1, s2 = c_ref[...], s1_ref[...], s2_ref[...]
        gq, gk = gqh_ref[...], gkh_ref[...]
        row = (pl.program_id(0) % tpe) * tm + lax.broadcasted_iota(jnp.int32, (tm, 1), 0)
        spare = (lax.broadcasted_iota(jnp.int32, (1, D_QKP - D_NOPE), 1) == D_ROPE).astype(F32)
        kmask = jnp.where(row < PAD_ROWS, NEG_INF * math.sqrt(D_QK), 0.0) * spare
        for hd in range(HEADS):
            qn, _, knn, krn, _ = m["heads"][hd]
            qg = qn * gq
            q_ref[hd, :, 0:D_NOPE] = qg[:, 0:D_NOPE].astype(BF16)
            q_ref[hd, :, D_NOPE:D_QKP] = (_rope(qg[:, D_NOPE:D_QKP], c, s1, s2) + spare).astype(BF16)
            k_ref[hd, :, 0:D_NOPE] = (knn * gk[:, 0:D_NOPE]).astype(BF16)
            k_ref[hd, :, D_NOPE:D_QKP] = (_rope(krn * gk[:, D_NOPE:D_QKP], c, s1, s2) + kmask).astype(BF16)
            v_ref[hd] = m["v"][:, hd * D_V:(hd + 1) * D_V].astype(BF16)
        cqn_ref[...] = m["cqn"]
        ckvn_ref[...] = m["ckvn"]

    def const(shape):
        return pl.BlockSpec(shape, lambda i: tuple(0 for _ in shape))

    tab = pl.BlockSpec((tm, 128), lambda i: (i % tpe, 0))
    return pl.pallas_call(
        body, name="mla_prep", grid=(rows // tm,),
        in_specs=[pl.BlockSpec((tm, Z_MLA), lambda i: (i, 0)), const((1, Q_RANK)), const((1, KV_RANK)),
                  const((1, D_QKP)), const((1, D_QKP)), const((HEADS * D_QKP, Q_RANK)),
                  const((KV_RANK, HEADS * D_NOPE)), const((KV_RANK, HEADS * D_V)), tab, tab, tab],
        out_specs=[pl.BlockSpec((HEADS, tm, D_QKP), lambda i: (0, i, 0)),
                   pl.BlockSpec((HEADS, tm, D_QKP), lambda i: (0, i, 0)),
                   pl.BlockSpec((HEADS, tm, D_V), lambda i: (0, i, 0)),
                   pl.BlockSpec((tm, Q_RANK), lambda i: (i, 0)),
                   pl.BlockSpec((tm, KV_RANK), lambda i: (i, 0))],
        out_shape=[jax.ShapeDtypeStruct((HEADS, rows, D_QKP), BF16),
                   jax.ShapeDtypeStruct((HEADS, rows, D_QKP), BF16),
                   jax.ShapeDtypeStruct((HEADS, rows, D_V), BF16),
                   jax.ShapeDtypeStruct((rows, Q_RANK), BF16),
                   jax.ShapeDtypeStruct((rows, KV_RANK), BF16)],
        compiler_params=_params(("parallel",)))(z, gql, gkvl, gqh, gkh, wuq, wuk, wuv, *tabs)


Q_BLOCK_ROWS = 528
Q_BLOCK_ROWS_BWD = 192


def _q_block(lp):
    return _row_tile(lp, Q_BLOCK_ROWS)


def _key_end(ext, lp):
    return min(lp, -(-ext // CHUNK) * CHUNK)


def _diag_bias(qb, j0, nk):
    shift = CHUNK.bit_length() - 1
    r = jnp.right_shift(j0 + lax.broadcasted_iota(jnp.int32, (qb, nk), 0), shift)
    c = jnp.right_shift(j0 + lax.broadcasted_iota(jnp.int32, (qb, nk), 1), shift)
    return jnp.where(c <= r, 0.0, NEG_INF)


def _attn_fwd_call(q, k, v, nb, lp, comm=None):
    rows = nb * lp
    qb = _q_block(lp)
    scale = 1.0 / math.sqrt(D_QK)

    def body(q_ref, k_ref, v_ref, o_ref, lse_ref):
        for j in range(lp // qb):
            j0, ext = j * qb, (j + 1) * qb
            kend = _key_end(ext, lp)
            qj = q_ref[0, j0:ext, :]
            sd = _dot_nt(qj, k_ref[0, j0:kend, :]) * scale + _diag_bias(qb, j0, kend - j0)
            mx = jnp.max(sd, axis=-1, keepdims=True)
            if j > 0:
                so = _dot_nt(qj, k_ref[0, 0:j0, :]) * scale
                mx = jnp.maximum(mx, jnp.max(so, axis=-1, keepdims=True))
            pd = jnp.exp(sd - mx)
            l = jnp.sum(pd, axis=-1, keepdims=True)
            o = _dot(pd.astype(BF16), v_ref[0, j0:kend, :])
            if j > 0:
                po = jnp.exp(so - mx)
                l = l + jnp.sum(po, axis=-1, keepdims=True)
                o = o + _dot(po.astype(BF16), v_ref[0, 0:j0, :])
            o_ref[j0:ext, :] = o / l
            lse_ref[0, j0:ext, :] = mx + jnp.log(l)

    return _hosted_call(
        body, name="attn_fwd", grid=(nb, HEADS),
        in_specs=[pl.BlockSpec((1, lp, D_QKP), lambda b, h: (h, b, 0)),
                  pl.BlockSpec((1, lp, D_QKP), lambda b, h: (h, b, 0)),
                  pl.BlockSpec((1, lp, D_V), lambda b, h: (h, b, 0))],
        out_specs=[pl.BlockSpec((lp, D_V), lambda b, h: (b, h)),
                   pl.BlockSpec((1, lp, 1), lambda b, h: (h, b, 0))],
        out_shape=[jax.ShapeDtypeStruct((rows, MLA_W), F32),
                   jax.ShapeDtypeStruct((HEADS, rows, 1), F32)],
        dims=("parallel", "parallel"), args=(q, k, v), comm=comm)


def _attn_bwd_call(q, k, v, o, lse, do, nb, lp, comm=None):
    rows = nb * lp
    qb = _row_tile(lp, Q_BLOCK_ROWS_BWD)
    scale = 1.0 / math.sqrt(D_QK)

    def body(q_ref, k_ref, v_ref, o_ref, lse_ref, do_ref, dq_ref, dk_ref, dv_ref, dk_acc, dv_acc):
        dk_acc[...] = jnp.zeros_like(dk_acc)
        dv_acc[...] = jnp.zeros_like(dv_acc)
        shift = CHUNK.bit_length() - 1
        for j in range(lp // qb):
            j0, ext = j * qb, (j + 1) * qb
            kend = _key_end(ext, lp)
            qj = q_ref[0, j0:ext, :]
            doj = do_ref[j0:ext, :]
            delta = jnp.sum(doj * o_ref[j0:ext, :], axis=-1, keepdims=True)
            dob = doj.astype(BF16)
            kk = k_ref[0, 0:kend, :]
            qchunk = jnp.right_shift(j0 + lax.broadcasted_iota(jnp.int32, (qb, kend), 0), shift)
            kchunk = jnp.right_shift(lax.broadcasted_iota(jnp.int32, (qb, kend), 1), shift)
            s = _dot_nt(qj, kk) * scale - lse_ref[0, j0:ext, :]
            p = jnp.where(kchunk <= qchunk, jnp.exp(s), 0.0)
            dv_acc[0:kend, :] += _dot_tn(p.astype(BF16), dob)
            dp = _dot_nt(dob, v_ref[0, 0:kend, :])
            ds = (p * (dp - delta) * scale).astype(BF16)
            dq_ref[0, j0:ext, :] = _dot(ds, kk).astype(BF16)
            dk_acc[0:kend, :] += _dot_tn(ds, qj)
        dk_ref[0] = dk_acc[...].astype(BF16)
        dv_ref[0] = dv_acc[...].astype(BF16)

    qspec = pl.BlockSpec((1, lp, D_QKP), lambda b, h: (h, b, 0))
    vspec = pl.BlockSpec((1, lp, D_V), lambda b, h: (h, b, 0))
    ospec = pl.BlockSpec((lp, D_V), lambda b, h: (b, h))
    return _hosted_call(
        body, name="attn_bwd", grid=(nb, HEADS),
        in_specs=[qspec, qspec, vspec, ospec, pl.BlockSpec((1, lp, 1), lambda b, h: (h, b, 0)), ospec],
        out_specs=[qspec, qspec, vspec],
        out_shape=[jax.ShapeDtypeStruct((HEADS, rows, D_QKP), BF16),
                   jax.ShapeDtypeStruct((HEADS, rows, D_QKP), BF16),
                   jax.ShapeDtypeStruct((HEADS, rows, D_V), BF16)],
        scratch_shapes=[pltpu.VMEM((lp, D_QKP), F32), pltpu.VMEM((lp, D_V), F32)],
        dims=("parallel", "parallel"), args=(q, k, v, o, lse, do), comm=comm)


def _lru_gates(u, cw, cb, wa, ba, wx, bx, lam, lp):
    xc = (cw[3:4, :] * u + cw[2:3, :] * pltpu.roll(u, 1, 0) + cw[1:2, :] * pltpu.roll(u, 2, 0)
          + cw[0:1, :] * pltpu.roll(u, 3, 0) + cb)
    xcb = xc.astype(BF16)
    r = _sigmoid(_dot(xcb, wa) + ba)
    i = _sigmoid(_dot(xcb, wx) + bx)
    sp = _softplus_neg(lam)
    la = -C_RGLRU * r * sp
    a = jnp.exp(la)
    x2 = 2.0 * la
    e2 = a * a
    m2 = jnp.maximum(jnp.where(x2 > -0.01, -x2 * (1.0 + 0.5 * x2), 1.0 - e2), 1e-30)
    rs = lax.rsqrt(m2)
    row = lax.broadcasted_iota(jnp.int32, (lp, LRU_TILE), 0)
    first = row == PAD_ROWS
    valid = row >= PAD_ROWS
    mult_eff = jnp.where(first, 1.0, m2 * rs)
    return dict(xc=xc, xcb=xcb, r=r, i=i, sp=sp, a=a, e2=e2, rs=rs, mult_eff=mult_eff, first=first, valid=valid)


def _scan_rows(a, b, a_s, b_s, out_ref, lp, reverse):
    sub = lax.broadcasted_iota(jnp.int32, (lp, LRU_TILE), 0) & 7
    for dist in (1, 2, 4):
        shift = lp - dist if reverse else dist
        keep = (sub + dist <= 7) if reverse else (sub >= dist)
        a_sh = pltpu.roll(a, shift, 0)
        b_sh = pltpu.roll(b, shift, 0)
        b = jnp.where(keep, a * b_sh + b, b)
        a = jnp.where(keep, a * a_sh, a)
    a_s[...] = a
    b_s[...] = b
    n_groups = lp // 8
    edge = 0 if reverse else 7

    def group(gi, carry):
        r0 = pl.multiple_of(((n_groups - 1 - gi) if reverse else gi) * 8, 8)
        a8 = a_s[pl.ds(r0, 8), :]
        b8 = b_s[pl.ds(r0, 8), :]
        out_ref[pl.ds(r0, 8), :] = a8 * carry + b8
        return a8[edge:edge + 1, :] * carry + b8[edge:edge + 1, :]

    lax.fori_loop(0, n_groups, group, jnp.zeros((1, LRU_TILE), F32), unroll=4)


def _lru_specs(lp):
    seq = lambda col0: pl.BlockSpec((lp, LRU_TILE), lambda t, b: (b, col0 + t))
    cw = pl.BlockSpec((1, CONV_K, LRU_TILE), lambda t, b: (t, 0, 0))
    vec = pl.BlockSpec((1, LRU_TILE), lambda t, b: (0, t))
    mat = pl.BlockSpec((1, LRU_TILE, LRU_TILE), lambda t, b: (t, 0, 0))
    return seq, cw, vec, mat


def _lru_fwd_call(z, cw, cb, wa, ba, wx, bx, lam, nb, lp, comm=None):
    rows = nb * lp
    seq, cwspec, vec, mat = _lru_specs(lp)

    def body(u_ref, g_ref, cw_ref, cb_ref, wa_ref, ba_ref, wx_ref, bx_ref, lam_ref, y_ref, hs_ref, a_s, b_s):
        m = _lru_gates(u_ref[...], cw_ref[0], cb_ref[...], wa_ref[0], ba_ref[...], wx_ref[0], bx_ref[...],
                       lam_ref[...], lp)
        a = jnp.where(m["valid"], m["a"], 0.0)
        b = jnp.where(m["valid"], m["mult_eff"] * (m["i"] * m["xc"]), 0.0)
        _scan_rows(a, b, a_s, b_s, hs_ref, lp, reverse=False)
        gl, _ = _gelu(g_ref[...])
        y_ref[...] = hs_ref[...] * gl

    oshape = jax.ShapeDtypeStruct((rows, LRU_W), F32)
    return _hosted_call(
        body, name="lru_fwd", grid=(N_LRU_TILES, nb),
        in_specs=[seq(Z_U // LRU_TILE), seq(Z_G // LRU_TILE), cwspec, vec, mat, vec, mat, vec, vec],
        out_specs=[seq(0), seq(0)], out_shape=[oshape, oshape],
        scratch_shapes=[pltpu.VMEM((lp, LRU_TILE), F32), pltpu.VMEM((lp, LRU_TILE), F32)],
        dims=("parallel", "parallel"), args=(z, z, cw, cb, wa, ba, wx, bx, lam), comm=comm)


def _lru_bwd_call(z, hs, dy, cw, cb, wa, ba, wx, bx, lam, nb, lp, comm=None):
    rows = nb * lp
    seq, cwspec, vec, mat = _lru_specs(lp)

    def body(u_ref, g_ref, hs_ref, dy_ref, cw_ref, cb_ref, wa_ref, ba_ref, wx_ref, bx_ref, lam_ref,
             du_ref, dg_ref, dcw_ref, dcb_ref, dwa_ref, dba_ref, dwx_ref, dbx_ref, dlam_ref, a_s, b_s, d_s):
        b_idx = pl.program_id(1)
        u = u_ref[...]
        cw = cw_ref[0]
        wa, wx = wa_ref[0], wx_ref[0]
        lam = lam_ref[...]
        m = _lru_gates(u, cw, cb_ref[...], wa, ba_ref[...], wx, bx_ref[...], lam, lp)
        gate = g_ref[...]
        gl, th = _gelu(gate)
        dy = dy_ref[...]
        hs = hs_ref[...]
        dg_ref[...] = (dy * hs * _gelu_grad(gate, th)).astype(BF16)
        a_eff = jnp.where(m["valid"], m["a"], 0.0)
        _scan_rows(pltpu.roll(a_eff, lp - 1, 0), dy * gl, a_s, b_s, d_s, lp, reverse=True)
        ds = d_s[...]
        xc, r, i = m["xc"], m["r"], m["i"]
        row = lax.broadcasted_iota(jnp.int32, (lp, LRU_TILE), 0)
        da = ds * jnp.where(row >= 1, pltpu.roll(hs, 1, 0), 0.0)
        db = jnp.where(m["valid"], ds, 0.0)
        di = db * m["mult_eff"] * xc
        dxc = db * m["mult_eff"] * i
        live = m["valid"] & jnp.logical_not(m["first"])
        dm = jnp.where(live, db * i * xc, 0.0)
        dla = da * m["a"] - dm * (m["e2"] * m["rs"])
        dr = dla * (-C_RGLRU * m["sp"])
        dsp = jnp.sum(dla * (-C_RGLRU * r), axis=0, keepdims=True)
        dpr = (dr * r * (1.0 - r))
        dpi = (di * i * (1.0 - i))
        dprb, dpib = dpr.astype(BF16), dpi.astype(BF16)
        dxc = dxc + _dot_nt(dprb, wa) + _dot_nt(dpib, wx)
        du = (cw[3:4, :] * dxc + cw[2:3, :] * pltpu.roll(dxc, lp - 1, 0) + cw[1:2, :] * pltpu.roll(dxc, lp - 2, 0)
              + cw[0:1, :] * pltpu.roll(dxc, lp - 3, 0))
        du_ref[...] = jnp.where(m["valid"], du, 0.0).astype(BF16)
        tap = lax.broadcasted_iota(jnp.int32, (CONV_K, LRU_TILE), 0)
        dcw = jnp.zeros((CONV_K, LRU_TILE), F32)
        for kk in range(CONV_K):
            shifted = u if kk == CONV_K - 1 else pltpu.roll(u, CONV_K - 1 - kk, 0)
            dcw = jnp.where(tap == kk, jnp.sum(dxc * shifted, axis=0, keepdims=True), dcw)
        parts = [(dcw_ref, dcw[None]), (dcb_ref, jnp.sum(dxc, axis=0, keepdims=True)[None]),
                 (dwa_ref, _dot_tn(m["xcb"], dprb)[None]), (dba_ref, jnp.sum(dpr, axis=0, keepdims=True)[None]),
                 (dwx_ref, _dot_tn(m["xcb"], dpib)[None]), (dbx_ref, jnp.sum(dpi, axis=0, keepdims=True)[None]),
                 (dlam_ref, (dsp * (-jax.nn.sigmoid(-lam)))[None])]

        @pl.when(b_idx == 0)
        def _():
            for ref, val in parts:
                ref[...] = val

        @pl.when(b_idx != 0)
        def _():
            for ref, val in parts:
                ref[...] += val

    bshape = jax.ShapeDtypeStruct((rows, LRU_W), BF16)
    vec3 = pl.BlockSpec((1, 1, LRU_TILE), lambda t, b: (t, 0, 0))
    vshape = jax.ShapeDtypeStruct((N_LRU_TILES, 1, LRU_TILE), F32)
    mshape = jax.ShapeDtypeStruct((N_LRU_TILES, LRU_TILE, LRU_TILE), F32)
    return _hosted_call(
        body, name="lru_bwd", grid=(N_LRU_TILES, nb),
        in_specs=[seq(Z_U // LRU_TILE), seq(Z_G // LRU_TILE), seq(0), seq(0), cwspec, vec, mat, vec, mat, vec, vec],
        out_specs=[seq(0), seq(0), cwspec, vec3, mat, vec3, mat, vec3, vec3],
        out_shape=[bshape, bshape, jax.ShapeDtypeStruct((N_LRU_TILES, CONV_K, LRU_TILE), F32), vshape, mshape,
                   vshape, mshape, vshape, vshape],
        scratch_shapes=[pltpu.VMEM((lp, LRU_TILE), F32)] * 3,
        dims=("parallel", "arbitrary"), args=(z, z, hs, dy, cw, cb, wa, ba, wx, bx, lam), comm=comm)


def _mix_out_call(ya, yl, ga, gl, wout, h, tm):
    rows, d = h.shape

    def body(ya_ref, yl_ref, ga_ref, gl_ref, w_ref, h_ref, y_ref, o_ref):
        a = ya_ref[...]
        l = yl_ref[...]
        an = (a * _rms(a, MLA_W) * ga_ref[...]).astype(BF16)
        ln = (l * _rms(l, LRU_W) * gl_ref[...]).astype(BF16)
        y_ref[:, 0:MLA_W] = an
        y_ref[:, MLA_W:MLA_W + LRU_W] = ln
        o_ref[...] = h_ref[...] + _dot(an, w_ref[0:MLA_W, :]) + _dot(ln, w_ref[MLA_W:MLA_W + LRU_W, :])

    half = pl.BlockSpec((tm, MLA_W), lambda i: (i, 0))
    g = pl.BlockSpec((1, MLA_W), lambda i: (0, 0))
    full = pl.BlockSpec((tm, d), lambda i: (i, 0))
    return pl.pallas_call(
        body, name="mix_out", grid=(rows // tm,),
        in_specs=[half, half, g, g, pl.BlockSpec((MLA_W + LRU_W, d), lambda i: (0, 0)), full],
        out_specs=[full, full],
        out_shape=[jax.ShapeDtypeStruct((rows, MLA_W + LRU_W), BF16), jax.ShapeDtypeStruct((rows, d), F32)],
        compiler_params=_params(("parallel",)))(ya, yl, ga, gl, wout, h)


def _mix_out_bwd_call(dhb, wout, ya, yl, ga, gl, tm):
    rows = ya.shape[0]
    d = dhb.shape[1]

    def body(dh_ref, w_ref, ya_ref, yl_ref, ga_ref, gl_ref, dya_ref, dyl_ref, dga_ref, dgl_ref):
        dy = _dot_nt(dh_ref[...], w_ref[...])
        outs = []
        for val, g_ref, lo, out_ref in ((ya_ref[...], ga_ref, 0, dya_ref), (yl_ref[...], gl_ref, MLA_W, dyl_ref)):
            r = _rms(val, MLA_W)
            n = val * r
            dyn = dy[:, lo:lo + MLA_W]
            out_ref[...] = _rms_bwd(dyn * g_ref[...], n, r, MLA_W)
            outs.append(jnp.sum(dyn * n, axis=0, keepdims=True))

        @pl.when(pl.program_id(0) == 0)
        def _():
            dga_ref[...] = outs[0]
            dgl_ref[...] = outs[1]

        @pl.when(pl.program_id(0) != 0)
        def _():
            dga_ref[...] += outs[0]
            dgl_ref[...] += outs[1]

    half = pl.BlockSpec((tm, MLA_W), lambda i: (i, 0))
    g = pl.BlockSpec((1, MLA_W), lambda i: (0, 0))
    return pl.pallas_call(
        body, name="mix_out_bwd", grid=(rows // tm,),
        in_specs=[pl.BlockSpec((tm, d), lambda i: (i, 0)), pl.BlockSpec((MLA_W + LRU_W, d), lambda i: (0, 0)),
                  half, half, g, g],
        out_specs=[half, half, g, g],
        out_shape=[jax.ShapeDtypeStruct((rows, MLA_W), F32), jax.ShapeDtypeStruct((rows, LRU_W), F32),
                   jax.ShapeDtypeStruct((1, MLA_W), F32), jax.ShapeDtypeStruct((1, LRU_W), F32)],
        compiler_params=_params(("arbitrary",)))(dhb, wout, ya, yl, ga, gl)


def _final_call(h, g, target, lp, tm):
    rows, d = h.shape
    tpe = lp // tm

    def body(h_ref, g_ref, t_ref, dh_ref, dhb_ref, dg_ref, loss_ref):
        i = pl.program_id(0)
        x = h_ref[...]
        g = g_ref[...]
        r = _rms(x, d)
        n = x * r
        row = (i % tpe) * tm + lax.broadcasted_iota(jnp.int32, (tm, 1), 0)
        err = jnp.where(row >= CHUNK, n * g - t_ref[...], 0.0)
        dout = err * (1.0 / d)
        dh = _rms_bwd(dout * g, n, r, d)
        dh_ref[...] = dh
        dhb_ref[...] = dh.astype(BF16)
        dg = jnp.sum(dout * n, axis=0, keepdims=True)
        part = jnp.sum(jnp.sum(err * err, axis=1, keepdims=True), axis=0, keepdims=True) * (0.5 / d)
        loss = jnp.broadcast_to(part, (1, 128))

        @pl.when(i == 0)
        def _():
            dg_ref[...] = dg
            loss_ref[...] = loss

        @pl.when(i != 0)
        def _():
            dg_ref[...] += dg
            loss_ref[...] += loss

    full = pl.BlockSpec((tm, d), lambda i: (i, 0))
    return pl.pallas_call(
        body, name="final_loss", grid=(rows // tm,),
        in_specs=[full, pl.BlockSpec((1, d), lambda i: (0, 0)), full],
        out_specs=[full, full, pl.BlockSpec((1, d), lambda i: (0, 0)), pl.BlockSpec((1, 128), lambda i: (0, 0))],
        out_shape=[jax.ShapeDtypeStruct((rows, d), F32), jax.ShapeDtypeStruct((rows, d), BF16),
                   jax.ShapeDtypeStruct((1, d), F32), jax.ShapeDtypeStruct((1, 128), F32)],
        compiler_params=_params(("arbitrary",)))(h, g, target)


def _ffn_dact_call(name, dhb, wd, gate, up, tm, comm=None):
    rows, d = dhb.shape
    ns, fs, _ = wd.shape

    nsub = 2 if tm % 32 == 0 else 1
    sub = tm // nsub

    def body(dh_ref, wd_ref, g_ref, p_ref, dg_ref, dp_ref):
        wd = wd_ref[0]
        for r in range(nsub):
            rs = slice(r * sub, (r + 1) * sub)
            da = (0.5 * _dot_nt(dh_ref[rs, :], wd)).astype(BF16)
            g = g_ref[0, rs, :]
            p = p_ref[0, rs, :]
            sg = jax.nn.sigmoid(g)
            dg_ref[0, rs, :] = (da * p) * (sg * (1.0 + g * (1.0 - sg)))
            dp_ref[0, rs, :] = da * (g * sg)

    aspec = pl.BlockSpec((1, tm, fs), lambda s, i: (s, i, 0))
    oshape = jax.ShapeDtypeStruct((ns, rows, fs), BF16)
    return _hosted_call(
        body, name=name, grid=(ns, rows // tm),
        in_specs=[pl.BlockSpec((tm, d), lambda s, i: (i, 0)), pl.BlockSpec((1, fs, d), lambda s, i: (s, 0, 0)),
                  aspec, aspec],
        out_specs=[aspec, aspec], out_shape=[oshape, oshape],
        dims=("parallel", "parallel"), args=(dhb, wd, gate, up), comm=comm)


def _norm_in_bwd_call(name, pieces, h, g, dres, tm, comm=None, part=(0, 1), prev=None):
    rows, d = h.shape
    npc = len(pieces)
    steps = rows // tm // part[1]
    off = part[0] * steps
    n_prev = 0 if prev is None else 2

    def body(*refs):
        d_refs = refs[0:2 * npc:2]
        w_refs = refs[1:2 * npc:2]
        h_ref, g_ref, dres_ref = refs[2 * npc:2 * npc + 3]
        dh_ref, dhb_ref, dg_ref = refs[2 * npc + 3 + n_prev:]
        du = jnp.zeros((tm, d), F32)
        for d_ref, w_ref in zip(d_refs, w_refs):
            if len(d_ref.shape) == 3:
                for s in range(d_ref.shape[0]):
                    du = du + _dot(d_ref[s], w_ref[s])
            else:
                du = du + _dot(d_ref[...], w_ref[...])
        x = h_ref[...]
        r = _rms(x, d)
        n = x * r
        dh = dres_ref[...] + _rms_bwd(du * g_ref[...], n, r, d)
        dh_ref[...] = dh
        dhb_ref[...] = dh.astype(BF16)
        dg = jnp.sum(du * n, axis=0, keepdims=True)

        @pl.when(pl.program_id(0) == 0)
        def _():
            dg_ref[...] = dg

        @pl.when(pl.program_id(0) != 0)
        def _():
            dg_ref[...] += dg

    in_specs, args = [], []
    for dd, w in pieces:
        if dd.ndim == 3:
            in_specs.append(pl.BlockSpec((dd.shape[0], tm, dd.shape[2]), lambda i: (0, i + off, 0)))
            in_specs.append(_resident(w.shape))
        else:
            in_specs.append(pl.BlockSpec((tm, dd.shape[1]), lambda i: (i + off, 0)))
            in_specs.append(_resident(w.shape))
        args += [dd, w]
    full = pl.BlockSpec((tm, d), lambda i: (i + off, 0))
    gspec = pl.BlockSpec((1, d), lambda i: (0, 0))
    n_in = len(in_specs) + 3
    return _hosted_call(
        body, name=name, grid=(steps,),
        in_specs=in_specs + [full, gspec, full] + _any_specs(n_prev),
        out_specs=[full, full, gspec],
        out_shape=[jax.ShapeDtypeStruct((rows, d), F32), jax.ShapeDtypeStruct((rows, d), BF16),
                   jax.ShapeDtypeStruct((1, d), F32)],
        args=(*args, h, g, dres, *(prev or ())), comm=comm,
        aliases={n_in: 0, n_in + 1: 1} if prev is not None else {})


def _wgrad_call(name, a, b, scale=1.0, comm=None):
    a3, b3 = a.ndim == 3, b.ndim == 3
    ns = a.shape[0] if a3 else (b.shape[0] if b3 else 1)
    rows, m = a.shape[-2:]
    n = b.shape[-1]
    tmm = m if a3 else _col_tile(m, 256)

    def body(a_ref, b_ref, o_ref):
        av = a_ref[0] if a3 else a_ref[...]
        bv = b_ref[0] if b3 else b_ref[...]
        res = _dot_tn(av, bv)
        if scale != 1.0:
            res = res * scale
        if a3 or b3:
            o_ref[0] = res
        else:
            o_ref[...] = res

    aspec = (pl.BlockSpec((1, rows, tmm), lambda s, j: (s, 0, j)) if a3
             else pl.BlockSpec((rows, tmm), lambda s, j: (0, j)))
    bspec = (pl.BlockSpec((1, rows, n), lambda s, j: (s, 0, 0)) if b3
             else pl.BlockSpec((rows, n), lambda s, j: (0, 0)))
    if a3 or b3:
        ospec = pl.BlockSpec((1, tmm, n), lambda s, j: (s, j, 0))
        oshape = jax.ShapeDtypeStruct((ns, m, n), F32)
    else:
        ospec = pl.BlockSpec((tmm, n), lambda s, j: (j, 0))
        oshape = jax.ShapeDtypeStruct((m, n), F32)
    return _hosted_call(
        body, name=name, grid=(ns, m // tmm), in_specs=[aspec, bspec], out_specs=[ospec], out_shape=[oshape],
        dims=("parallel", "parallel"), args=(a, b), comm=comm)[0]


def _mla_prep_bwd_call(z, dq, dk, dv, gql, gkvl, gqh, gkh, wuq, wuk, wuv, tabs, lp, tm, comm=None):
    rows = z.shape[0]
    tpe = lp // tm

    def body(z_ref, dq_ref, dk_ref, dv_ref, gql_ref, gkvl_ref, gqh_ref, gkh_ref, wuq_ref, wuk_ref, wuv_ref,
             c_ref, s1_ref, s2_ref, dz_ref, dqp_ref, dkn_ref, dvv_ref, dgql_ref, dgkvl_ref, dgqh_ref, dgkh_ref):
        gql, gkvl = gql_ref[...], gkvl_ref[...]
        gq, gk = gqh_ref[...], gkh_ref[...]
        wuq, wuk, wuv = wuq_ref[...], wuk_ref[...], wuv_ref[...]
        m = _mla_heads(z_ref[...], gql, gkvl, wuq, wuk, wuv)
        c, s1, s2 = c_ref[...], s1_ref[...], s2_ref[...]
        dgq = jnp.zeros((1, D_QKP), F32)
        dgk = jnp.zeros((1, D_QKP), F32)
        dkr = jnp.zeros((tm, D_QKP - D_NOPE), F32)
        for hd in range(HEADS):
            qn, rqh, knn, krn, rkh = m["heads"][hd]
            dqg = jnp.concatenate([dq_ref[hd, :, 0:D_NOPE].astype(F32),
                                   _rope_t(dq_ref[hd, :, D_NOPE:D_QKP].astype(F32), c, s1, s2)], axis=1)
            dgq = dgq + jnp.sum(dqg * qn, axis=0, keepdims=True)
            dqn = dqg * gq
            dqr = rqh * (dqn - qn * (jnp.sum(dqn * qn, axis=-1, keepdims=True) * (1.0 / D_QK)))
            dqp_ref[:, hd * D_QKP:(hd + 1) * D_QKP] = dqr.astype(BF16)
            kn_full = jnp.concatenate([knn, krn], axis=1)
            dkg = jnp.concatenate([dk_ref[hd, :, 0:D_NOPE].astype(F32),
                                   _rope_t(dk_ref[hd, :, D_NOPE:D_QKP].astype(F32), c, s1, s2)], axis=1)
            dgk = dgk + jnp.sum(dkg * kn_full, axis=0, keepdims=True)
            dkn = dkg * gk
            dkraw = rkh * (dkn - kn_full * (jnp.sum(dkn * kn_full, axis=-1, keepdims=True) * (1.0 / D_QK)))
            dkn_ref[:, hd * D_NOPE:(hd + 1) * D_NOPE] = dkraw[:, 0:D_NOPE].astype(BF16)
            dkr = dkr + dkraw[:, D_NOPE:D_QKP]
            dvv_ref[:, hd * D_V:(hd + 1) * D_V] = dv_ref[hd]
        dcqn = _dot(dqp_ref[...], wuq)
        dckvn = _dot_nt(dkn_ref[...], wuk) + _dot_nt(dvv_ref[...], wuv)
        dz_ref[:, 0:Q_RANK] = _rms_bwd(dcqn * gql, m["nq"], m["rq"], Q_RANK).astype(BF16)
        dz_ref[:, Q_RANK:Z_KR] = _rms_bwd(dckvn * gkvl, m["nkv"], m["rkv"], KV_RANK).astype(BF16)
        dz_ref[:, Z_KR:Z_MLA] = dkr.astype(BF16)
        parts = [(dgql_ref, jnp.sum(dcqn * m["nq"], axis=0, keepdims=True)),
                 (dgkvl_ref, jnp.sum(dckvn * m["nkv"], axis=0, keepdims=True)), (dgqh_ref, dgq), (dgkh_ref, dgk)]

        @pl.when(pl.program_id(0) == 0)
        def _():
            for ref, val in parts:
                ref[...] = val

        @pl.when(pl.program_id(0) != 0)
        def _():
            for ref, val in parts:
                ref[...] += val

    def const(shape):
        return pl.BlockSpec(shape, lambda i: tuple(0 for _ in shape))

    tab = pl.BlockSpec((tm, 128), lambda i: (i % tpe, 0))
    hq = pl.BlockSpec((HEADS, tm, D_QKP), lambda i: (0, i, 0))
    hv = pl.BlockSpec((HEADS, tm, D_V), lambda i: (0, i, 0))

    def rowspec(n):
        return pl.BlockSpec((tm, n), lambda i: (i, 0))

    return _hosted_call(
        body, name="mla_prep_bwd", grid=(rows // tm,),
        in_specs=[rowspec(Z_MLA), hq, hq, hv, const((1, Q_RANK)), const((1, KV_RANK)), const((1, D_QKP)),
                  const((1, D_QKP)), const((HEADS * D_QKP, Q_RANK)), const((KV_RANK, HEADS * D_NOPE)),
                  const((KV_RANK, HEADS * D_V)), tab, tab, tab],
        out_specs=[rowspec(Z_MLA), rowspec(HEADS * D_QKP), rowspec(HEADS * D_NOPE), rowspec(HEADS * D_V),
                   const((1, Q_RANK)), const((1, KV_RANK)), const((1, D_QKP)), const((1, D_QKP))],
        out_shape=[jax.ShapeDtypeStruct((rows, Z_MLA), BF16), jax.ShapeDtypeStruct((rows, HEADS * D_QKP), BF16),
                   jax.ShapeDtypeStruct((rows, HEADS * D_NOPE), BF16), jax.ShapeDtypeStruct((rows, HEADS * D_V), BF16),
                   jax.ShapeDtypeStruct((1, Q_RANK), F32), jax.ShapeDtypeStruct((1, KV_RANK), F32),
                   jax.ShapeDtypeStruct((1, D_QKP), F32), jax.ShapeDtypeStruct((1, D_QKP), F32)],
        args=(z, dq, dk, dv, gql, gkvl, gqh, gkh, wuq, wuk, wuv, *tabs), comm=comm)


def _local_step(h0, target, w, nb, lp, sched=None):
    tm = _row_tile(nb * lp, 1408)
    te = _row_tile(lp, 512)
    tabs = _rope_tables(lp)
    g = {}
    if sched is None:
        host = lambda stage: None
    else:
        sched.g = g
        host = sched.host

    def ffn_fwd(tag, h, split):
        u = _rmsnorm_call(tag + "_norm", h, w[tag + "_norm"], te)
        if split:
            gate = _ffn_gate_call(tag + "_gate", u, w[tag + "_w_gate"], tm, host(tag + "_gate"))
            up, act = _ffn_upact_call(tag + "_upact", u, w[tag + "_w_up"], gate, tm, host(tag + "_upact"))
        else:
            gate, up, act = _ffn_up_call(tag + "_up", u, w[tag + "_w_gate"], w[tag + "_w_up"], tm, host(tag + "_up"))
        return _ffn_down_call(tag + "_down", act, w[tag + "_w_down"], h, te, host(tag + "_down")), (u, gate, up, act)

    def ffn_bwd(tag, h, saved, dh, dhb, split):
        u, gate, up, act = saved
        dgate, dup = _ffn_dact_call(tag + "_dact", dhb, w[tag + "_w_down"], gate, up, tm, host(tag + "_dact"))
        g[tag + "_w_down"] = _wgrad_call(tag + "_dwd", act, dhb, 0.5, host(tag + "_dwd"))
        g[tag + "_w_gate"] = _wgrad_call(tag + "_dwg", dgate, u, 1.0, host(tag + "_dwg"))
        g[tag + "_w_up"] = _wgrad_call(tag + "_dwu", dup, u, 1.0, host(tag + "_dwu"))
        pieces = [(dgate, w[tag + "_w_gate"]), (dup, w[tag + "_w_up"])]
        if not split:
            dh_in, dhb_in, g[tag + "_norm"] = _norm_in_bwd_call(tag + "_din", pieces, h, w[tag + "_norm"], dh, te,
                                                                host(tag + "_din"))
            return dh_in, dhb_in
        dh_a, dhb_a, dg_a = _norm_in_bwd_call(tag + "_din_a", pieces, h, w[tag + "_norm"], dh, te,
                                              host(tag + "_din_a"), part=(0, 2))
        dh_in, dhb_in, dg_b = _norm_in_bwd_call(tag + "_din_b", pieces, h, w[tag + "_norm"], dh, te,
                                                host(tag + "_din_b"), part=(1, 2), prev=(dh_a, dhb_a))
        g[tag + "_norm"] = dg_a + dg_b
        return dh_in, dhb_in

    h1, s1 = ffn_fwd("ffn1", h0, True)
    un = _rmsnorm_call("mix_norm", h1, w["mix_norm"], te)
    z = _mm_call("mix_in", un, w["w_in"], tm, F32)
    mla_w = (w["q_latent_norm"], w["kv_latent_norm"], w["q_head_norm"], w["k_head_norm"], w["w_uq"], w["w_uk"],
             w["w_uv"])
    q, k, v, cqn, ckvn = _mla_prep_call(z, *mla_w, tabs, lp, te)
    o, lse = _attn_fwd_call(q, k, v, nb, lp, host("attn_fwd"))
    lru_w = (w["conv_w"], w["conv_b"], w["gate_a_w"], w["gate_a_b"], w["gate_x_w"], w["gate_x_b"], w["lru_lambda"])
    yl, hs = _lru_fwd_call(z, *lru_w, nb, lp, host("lru_fwd"))
    y, h2 = _mix_out_call(o, yl, w["attn_out_norm"], w["lru_out_norm"], w["w_out"], h1, te)
    h3, s2 = ffn_fwd("ffn2", h2, False)
    dh3, dh3b, g["final_norm"], loss = _final_call(h3, w["final_norm"], target, lp, te)
    g["loss"] = loss

    dh2, dh2b = ffn_bwd("ffn2", h2, s2, dh3, dh3b, False)
    g["w_out"] = _wgrad_call("dw_out", y, dh2b)
    dya, dyl, g["attn_out_norm"], g["lru_out_norm"] = _mix_out_bwd_call(
        dh2b, w["w_out"], o, yl, w["attn_out_norm"], w["lru_out_norm"], te)
    dq, dk, dv = _attn_bwd_call(q, k, v, o, lse, dya, nb, lp, host("attn_bwd"))
    (dz_mla, dqp, dkn, dvv, g["q_latent_norm"], g["kv_latent_norm"], g["q_head_norm"],
     g["k_head_norm"]) = _mla_prep_bwd_call(z, dq, dk, dv, *mla_w, tabs, lp, te, host("mla_prep_bwd"))
    g["w_uq"] = _wgrad_call("dw_uq", dqp, cqn)
    g["w_uk"] = _wgrad_call("dw_uk", ckvn, dkn)
    g["w_uv"] = _wgrad_call("dw_uv", ckvn, dvv)
    (du, dgt, g["conv_w"], g["conv_b"], g["gate_a_w"], g["gate_a_b"], g["gate_x_w"], g["gate_x_b"],
     g["lru_lambda"]) = _lru_bwd_call(z, hs, dyl, *lru_w, nb, lp, host("lru_bwd"))
    win = w["w_in"]
    g["w_in"] = jnp.concatenate(
        [_wgrad_call("dw_in_mla", dz_mla, un), _wgrad_call("dw_in_u", du, un), _wgrad_call("dw_in_g", dgt, un)],
        axis=0)
    dh1, dh1b, g["mix_norm"] = _norm_in_bwd_call(
        "mix_din", [(dz_mla, win[0:Z_MLA]), (du, win[Z_U:Z_G]), (dgt, win[Z_G:Z_W])], h1, w["mix_norm"], dh2, te,
        host("mix_din"))
    dh0, _ = ffn_bwd("ffn1", h0, s1, dh1, dh1b, True)
    return loss, dh0, g


def _place():
    x, y, c = lax.axis_index("x"), lax.axis_index("y"), lax.axis_index("c")
    return x, y, c, [(1 - x, y), (x, 1 - y), (1 - x, 1 - y)]


def _any_specs(n):
    return [pl.BlockSpec(memory_space=pl.ANY)] * n


def _remote(src, dst, sems, k, dev):
    send_sems, recv_sems, base = sems
    return pltpu.make_async_remote_copy(src_ref=src, dst_ref=dst, send_sem=send_sems.at[base + k],
                                        recv_sem=recv_sems.at[base + k], device_id=dev, device_id_type=MESH)


EW_VMEM_BYTES = 24 * 1024 * 1024


def _fit_rows(rows, cols, blocks):
    return _row_tile(rows, max(16, int(EW_VMEM_BYTES // (8 * blocks)) // cols))


class _Geom:
    def __init__(self, n0, n1, blocks=1.0):
        self.n0, self.n1 = n0, n1
        self.axis = 0 if n0 % 32 == 0 else 1
        self.h0, self.h1 = (n0 // 2, n1) if self.axis == 0 else (n0, n1 // 2)
        self.tr = _fit_rows(self.h0, self.h1, blocks)
        self.nblk = self.h0 // self.tr

    def half_ref(self, ref, lead, idx):
        if self.axis == 0:
            return ref.at[(*lead, pl.ds(idx * self.h0, self.h0))]
        return ref.at[(*lead, slice(None), pl.ds(idx * self.h1, self.h1))]

    def half_block(self, lead, i, idx):
        return (*lead, idx * self.nblk + i, 0) if self.axis == 0 else (*lead, i, idx)


class _Comm:
    def __init__(self, ins, out_shapes, aliases, n_sems, start, finish, deliver):
        self.ins, self.out_shapes, self.aliases, self.n_sems = list(ins), list(out_shapes), dict(aliases), n_sems
        self.start, self.finish, self.deliver = start, finish, deliver

    def scratch(self):
        return [pltpu.SemaphoreType.DMA((self.n_sems,)), pltpu.SemaphoreType.DMA((self.n_sems,))]


def _comm_call(name, comm):
    n_in = len(comm.ins)

    def body(*refs):
        ins, outs, sems = refs[:n_in], refs[n_in:-2], (*refs[-2:], 0)
        comm.start(ins, outs, sems)
        comm.finish(ins, outs, sems)

    res = pl.pallas_call(
        body, name=name, out_shape=comm.out_shapes, in_specs=_any_specs(n_in),
        out_specs=_any_specs(len(comm.out_shapes)), input_output_aliases=comm.aliases,
        scratch_shapes=comm.scratch())(*comm.ins)
    return comm.deliver(list(res))


def _hosted_call(body, *, name, grid, in_specs, out_specs, out_shape, args, scratch_shapes=(), dims=None, comm=None,
                 prefetch=None, aliases=None):
    aliases = dict(aliases or {})
    in_specs, out_specs, out_shape = list(in_specs), list(out_specs), list(out_shape)
    n_pre = 0 if prefetch is None else 1

    def call(fn, in_specs, out_specs, out_shape, scratch, aliases, dims, args):
        if prefetch is None:
            return pl.pallas_call(
                fn, name=name, grid=grid, in_specs=in_specs, out_specs=out_specs, out_shape=out_shape,
                scratch_shapes=scratch, input_output_aliases=aliases, compiler_params=_params(dims))(*args)
        spec = pltpu.PrefetchScalarGridSpec(num_scalar_prefetch=1, grid=grid, in_specs=in_specs, out_specs=out_specs,
                                            scratch_shapes=scratch)
        return pl.pallas_call(
            fn, name=name, grid_spec=spec, out_shape=out_shape,
            input_output_aliases={i + 1: o for i, o in aliases.items()}, compiler_params=_params(dims))(prefetch, *args)

    if comm is None:
        return list(call(body, in_specs, out_specs, out_shape, list(scratch_shapes), aliases,
                         dims or ("arbitrary",) * len(grid), args))
    n_in, n_out, n_ci, n_co = len(in_specs), len(out_specs), len(comm.ins), len(comm.out_shapes)

    def wrapped(*refs):
        pre, refs = refs[:n_pre], refs[n_pre:]
        ins, cins = refs[:n_in], refs[n_in:n_in + n_ci]
        outs = refs[n_in + n_ci:n_in + n_ci + n_out]
        couts = refs[n_in + n_ci + n_out:n_in + n_ci + n_out + n_co]
        scratch, sems = refs[n_in + n_ci + n_out + n_co:-2], (*refs[-2:], 0)
        first = functools.reduce(jnp.logical_and, [pl.program_id(k) == 0 for k in range(len(grid))])
        last = functools.reduce(jnp.logical_and, [pl.program_id(k) == grid[k] - 1 for k in range(len(grid))])

        @pl.when(first)
        def _():
            comm.start(cins, couts, sems)

        body(*pre, *ins, *outs, *scratch)

        @pl.when(last)
        def _():
            comm.finish(cins, couts, sems)

    res = call(wrapped, in_specs + _any_specs(n_ci), out_specs + _any_specs(n_co), out_shape + comm.out_shapes,
               list(scratch_shapes) + comm.scratch(),
               {**aliases, **{n_in + i: n_out + o for i, o in comm.aliases.items()}},
               ("arbitrary",) * len(grid), (*args, *comm.ins))
    comm.deliver(list(res[n_out:]))
    return list(res[:n_out])


def _gather_comm(bufs, deliver):
    n = len(bufs)
    geoms = [_Geom(*b.shape[1:]) for b in bufs]

    def first(outs, sems):
        x, y, c, chips = _place()
        cps = []
        for a in range(n):
            mine = geoms[a].half_ref(outs[a], (2 * x + y,), c)
            cps += [_remote(mine, mine, sems, 6 * a + j, (cx, cy, c)) for j, (cx, cy) in enumerate(chips)]
        return cps

    def start(ins, outs, sems):
        for cp in first(outs, sems):
            cp.start()

    def finish(ins, outs, sems):
        x, y, c, chips = _place()
        sib = (x, y, 1 - c)
        passed = []
        for a in range(n):
            for j, (cx, cy) in enumerate(chips):
                land = geoms[a].half_ref(outs[a], (2 * cx + cy,), c)
                _remote(land, land, sems, 6 * a + j, sib).wait_recv()
                cp = _remote(land, land, sems, 6 * a + 3 + j, sib)
                cp.start()
                passed.append(cp)
        for a in range(n):
            for j, (cx, cy) in enumerate(chips):
                land = geoms[a].half_ref(outs[a], (2 * cx + cy,), 1 - c)
                _remote(land, land, sems, 6 * a + 3 + j, sib).wait_recv()
        for cp in first(outs, sems) + passed:
            cp.wait_send()

    return _Comm(bufs, [jax.ShapeDtypeStruct(b.shape, b.dtype) for b in bufs], {a: a for a in range(n)}, 6 * n,
                 start, finish, deliver)


def _reduce_pair_comm(grads, deliver):
    n = len(grads)
    geoms = [_Geom(*a.shape[1:]) for a in grads]

    def copies(ins, outs, sems):
        x, y, c, _ = _place()
        return [_remote(geoms[a].half_ref(ins[a], (slice(None),), 1 - c), outs[a], sems, a, (x, y, 1 - c))
                for a in range(n)]

    def start(ins, outs, sems):
        for cp in copies(ins, outs, sems):
            cp.start()

    def finish(ins, outs, sems):
        cps = copies(ins, outs, sems)
        for cp in cps:
            cp.wait_recv()
        for cp in cps:
            cp.wait_send()

    shapes = [jax.ShapeDtypeStruct((N_SHARD, g.h0, g.h1), a.dtype) for a, g in zip(grads, geoms)]
    return _Comm(grads, shapes, {}, n, start, finish, deliver)


def _reduce_chips_comm(parts, deliver):
    n = len(parts)

    def copies(ins, outs, sems):
        x, y, c, chips = _place()
        return [_remote(ins[a].at[2 * cx + cy], outs[a].at[j], sems, 3 * a + j, (cx, cy, c))
                for a in range(n) for j, (cx, cy) in enumerate(chips)]

    def start(ins, outs, sems):
        for cp in copies(ins, outs, sems):
            cp.start()

    def finish(ins, outs, sems):
        cps = copies(ins, outs, sems)
        for cp in cps:
            cp.wait_recv()
        for cp in cps:
            cp.wait_send()

    shapes = [jax.ShapeDtypeStruct((3,) + a.shape[1:], a.dtype) for a in parts]
    return _Comm(parts, shapes, {}, 3 * n, start, finish, deliver)


def _share_pair_comm(bufs, deliver):
    n = len(bufs)
    geoms = [_Geom(*b.shape) for b in bufs]

    def copies(outs, sems):
        x, y, c, _ = _place()
        cps = []
        for a in range(n):
            mine = geoms[a].half_ref(outs[a], (), c)
            cps.append(_remote(mine, mine, sems, a, (x, y, 1 - c)))
        return cps

    def start(ins, outs, sems):
        for cp in copies(outs, sems):
            cp.start()

    def finish(ins, outs, sems):
        x, y, c, _ = _place()
        for a in range(n):
            land = geoms[a].half_ref(outs[a], (), 1 - c)
            _remote(land, land, sems, a, (x, y, 1 - c)).wait_recv()
        for cp in copies(outs, sems):
            cp.wait_send()

    return _Comm(bufs, [jax.ShapeDtypeStruct(b.shape, b.dtype) for b in bufs], {a: a for a in range(n)}, n,
                 start, finish, deliver)


def _small_comm(pack, deliver):
    r, d = pack.shape

    def copies(ins, outs, sems):
        x, y, c, _ = _place()
        cps = []
        for k in range(1, 8):
            peer = (x ^ ((k >> 2) & 1), y ^ ((k >> 1) & 1), c ^ (k & 1))
            cps.append(_remote(ins[0], outs[0].at[4 * x + 2 * y + c], sems, k - 1, peer))
        return cps

    def start(ins, outs, sems):
        for cp in copies(ins, outs, sems):
            cp.start()

    def finish(ins, outs, sems):
        cps = copies(ins, outs, sems)
        for cp in cps:
            cp.wait_recv()
        for cp in cps:
            cp.wait_send()

    return _Comm([pack], [jax.ShapeDtypeStruct((8, r, d), pack.dtype)], {}, 7, start, finish, deliver)


def _join_comms(comms):
    comms = [c for c in comms if c is not None]
    if len(comms) <= 1:
        return comms[0] if comms else None
    ins, out_shapes, aliases, spans, n_sems = [], [], {}, [], 0
    for c in comms:
        aliases.update({len(ins) + i: len(out_shapes) + o for i, o in c.aliases.items()})
        spans.append((len(ins), len(ins) + len(c.ins), len(out_shapes), len(out_shapes) + len(c.out_shapes), n_sems))
        ins += c.ins
        out_shapes += c.out_shapes
        n_sems += c.n_sems

    def run(which):
        def go(all_ins, all_outs, sems):
            for c, (i0, i1, o0, o1, base) in zip(comms, spans):
                getattr(c, which)(all_ins[i0:i1], all_outs[o0:o1], (sems[0], sems[1], sems[2] + base))
        return go

    def deliver(outs):
        for c, (_, _, o0, o1, _) in zip(comms, spans):
            c.deliver(outs[o0:o1])
        return outs

    return _Comm(ins, out_shapes, aliases, n_sems, run("start"), run("finish"), deliver)


def _ew_call(name, fn, ins, out_dtypes):
    shape = ins[0].shape
    cols = shape[-1]
    rows = 1
    for s_ in shape[:-1]:
        rows *= s_
    ins2 = [a.reshape(rows, cols) for a in ins]
    tr = rows
    for t in range(16, min(rows, max(16, (1 << 19) // cols)) + 1, 16):
        if rows % t == 0:
            tr = t
    no = len(out_dtypes)

    def body(*refs):
        outs = fn(*[r[...] for r in refs[:len(ins2)]])
        for ref, val in zip(refs[len(ins2):], outs):
            ref[...] = val.astype(ref.dtype)

    spec = pl.BlockSpec((tr, cols), lambda i: (i, 0))
    res = pl.pallas_call(
        body, name=name, grid=(rows // tr,), in_specs=[spec] * len(ins2), out_specs=[spec] * no,
        out_shape=[jax.ShapeDtypeStruct((rows, cols), dt) for dt in out_dtypes],
        compiler_params=_params(("parallel",)))(*ins2)
    return [r.reshape(shape) for r in res]


def _adamw_math(w, g, m, v):
    m = ADAM_B1 * m + (1.0 - ADAM_B1) * g
    v = ADAM_B2 * v + (1.0 - ADAM_B2) * (g * g)
    m_hat = m / (1.0 - ADAM_B1 ** ADAM_STEP)
    v_hat = v / (1.0 - ADAM_B2 ** ADAM_STEP)
    delta = -ADAM_LR * (m_hat / (jnp.sqrt(v_hat) + ADAM_EPS) + ADAM_WD * w)
    return delta, m, v


def _adamw_call(name, w, g, m, v):
    return _ew_call(name, _adamw_math, [w, g, m, v], [F32, F32, F32])


def _tiled_call(name, fn, place, grid, in_items, out_items, comm=None):
    ni = len(in_items)

    def body(place_ref, *refs):
        vals = fn(*[r[...] for r in refs[:ni]])
        for ref, val in zip(refs[ni:], vals):
            ref[...] = val.astype(ref.dtype)

    return _hosted_call(
        body, name=name, grid=grid, in_specs=[pl.BlockSpec(blk, imap) for _, blk, imap in in_items],
        out_specs=[pl.BlockSpec(blk, imap) for _, _, blk, imap in out_items],
        out_shape=[jax.ShapeDtypeStruct(shp, dt) for shp, dt, _, _ in out_items],
        args=[a for a, _, _ in in_items], prefetch=place, comm=comm)


def _cast_call(name, place, shards, comm=None):
    n0, n1 = shards[0].shape
    tr = _fit_rows(n0, n1, 1.5 * len(shards))
    ins = [(a, (tr, n1), lambda i, p: (i, 0)) for a in shards]
    outs = [((N_SHARD, n0, n1), BF16, (1, tr, n1), lambda i, p: (p[0], i, 0)) for _ in shards]
    return _tiled_call(name, lambda *v: [x[None] for x in v], place, (n0 // tr,), ins, outs, comm)


def _pair_sum_call(name, place, fulls, gots):
    k = len(fulls)
    g = _Geom(*fulls[0].shape[1:], blocks=2.5 * k)
    blk = (1, g.tr, g.h1)
    ins = [(a, blk, lambda s, i, p: g.half_block((s,), i, p[1])) for a in fulls]
    ins += [(a, blk, lambda s, i, p: (s, i, 0)) for a in gots]
    outs = [((N_SHARD, g.h0, g.h1), BF16, blk, lambda s, i, p: (s, i, 0)) for _ in fulls]
    return _tiled_call(name, lambda *v: [v[j] + v[k + j] for j in range(k)], place, (N_SHARD, g.nblk), ins, outs)


def _chip_sum_call(name, place, fulls, gots, recvs, comm=None):
    k = len(fulls)
    g = _Geom(*fulls[0].shape[1:], blocks=4.5 * k)
    blk = (1, g.tr, g.h1)
    ins = [(a, blk, lambda i, p: g.half_block((p[0],), i, p[1])) for a in fulls]
    ins += [(a, blk, lambda i, p: (p[0], i, 0)) for a in gots]
    ins += [(a, (3, g.tr, g.h1), lambda i, p: (0, i, 0)) for a in recvs]
    outs = [((g.n0, g.n1), F32, (g.tr, g.h1), lambda i, p: g.half_block((), i, p[1])) for _ in fulls]

    def fn(*v):
        res = []
        for j in range(k):
            r = v[2 * k + j].astype(F32)
            res.append(v[j][0] + v[k + j][0] + r[0] + r[1] + r[2])
        return res

    return _tiled_call(name, fn, place, (g.nblk,), ins, outs, comm)


def _adamw_group_call(name, ws, gs, ms, vs, comm=None):
    k = len(ws)
    n0, n1 = ws[0].shape
    tr = _fit_rows(n0, n1, 8 * k)
    spec = pl.BlockSpec((tr, n1), lambda i: (i, 0))

    def body(*refs):
        for j in range(k):
            g = refs[k + j][...]
            delta, m, vv = _adamw_math(refs[j][...], g, refs[2 * k + j][...], refs[3 * k + j][...])
            for ref, val in zip(refs[4 * k + 4 * j:4 * k + 4 * j + 4], (g, delta, m, vv)):
                ref[...] = val

    flat = _hosted_call(
        body, name=name, grid=(n0 // tr,), in_specs=[spec] * (4 * k), out_specs=[spec] * (4 * k),
        out_shape=[jax.ShapeDtypeStruct((n0, n1), F32)] * (4 * k), dims=("parallel",),
        args=(*ws, *gs, *ms, *vs), comm=comm)
    return [flat[4 * j:4 * j + 4] for j in range(k)]


def _small_update_call(me, early, own_early, late, own_late, wp, mp, vp):
    nd, r, d = early.shape

    def body(me_ref, e_ref, oe_ref, l_ref, ol_ref, w_ref, m_ref, v_ref, gs_ref, d_ref, nm_ref, nv_ref):
        mine = me_ref[0]

        def total(g_ref, own_ref):
            acc = None
            for k in range(nd):
                part = jnp.where(mine == k, own_ref[...], g_ref[k])
                acc = part if acc is None else acc + part
            return acc

        gs = total(e_ref, oe_ref)
        ls = total(l_ref, ol_ref)
        gs_ref[...] = gs
        first = gs[0:8] + ls[0:8]
        gs_ref[0:8, :] = first
        gs_ref[ROW_META:ROW_META + N_META, :] = gs[ROW_META:ROW_META + N_META] + ls[8:8 + N_META]
        grads = jnp.concatenate([first, gs[8:SMALL_ADAM_ROWS]], axis=0)
        delta, m, v = _adamw_math(w_ref[...], grads, m_ref[...], v_ref[...])
        d_ref[...] = delta
        nm_ref[...] = m
        nv_ref[...] = v

    vm = pl.BlockSpec(memory_space=pltpu.VMEM)
    ashape = jax.ShapeDtypeStruct((SMALL_ADAM_ROWS, d), F32)
    return pl.pallas_call(
        body, name="small_update", in_specs=[pl.BlockSpec(memory_space=pltpu.SMEM)] + [vm] * 7, out_specs=[vm] * 4,
        out_shape=[jax.ShapeDtypeStruct((r, d), F32), ashape, ashape, ashape],
        compiler_params=pltpu.CompilerParams(vmem_limit_bytes=VMEM_LIMIT_BYTES))(
            me, early, own_early, late, own_late, wp, mp, vp)


SMALL_NAMES = ["ffn1_norm", "mix_norm", "ffn2_norm", "final_norm", "q_latent_norm", "kv_latent_norm",
               "q_head_norm", "k_head_norm", "conv_b", "gate_a_b", "gate_x_b", "lru_lambda", "attn_out_norm",
               "lru_out_norm"]
ROW_CONV_W = 14
ROW_GATE_A = 16
ROW_GATE_X = 48
ROW_META = 80
ROW_LOSS = 96


def _row(a):
    flat = a.reshape(1, -1)
    return jnp.pad(flat, ((0, 0), (0, D_MODEL - flat.shape[1])))


def _pack_small(t, rows):
    parts = [_row(t[nm]) for nm in SMALL_NAMES]
    parts.append(t["conv_w"].reshape(2, D_MODEL))
    parts.append(t["gate_a_w"].reshape(32, D_MODEL))
    parts.append(t["gate_x_w"].reshape(32, D_MODEL))
    p = jnp.concatenate(parts, axis=0)
    return jnp.pad(p, ((0, rows - p.shape[0]), (0, 0)))


def _early_pack(g):
    gs = {nm: g.get(nm, jnp.zeros((1, D_MODEL), F32)) for nm in SMALL_NAMES}
    gs["q_head_norm"] = g["q_head_norm"][:, 0:D_QK]
    gs["k_head_norm"] = g["k_head_norm"][:, 0:D_QK]
    for nm in ("conv_b", "gate_a_b", "gate_x_b", "lru_lambda"):
        gs[nm] = g[nm].reshape(1, LRU_W)
    gs["conv_w"] = g["conv_w"].transpose(1, 0, 2).reshape(CONV_K, LRU_W)
    gs["gate_a_w"] = _gate_blocks(g["gate_a_w"])
    gs["gate_x_w"] = _gate_blocks(g["gate_x_w"])
    return jnp.concatenate([_pack_small(gs, ROW_META), jnp.zeros((N_META, D_MODEL), F32), _row(g["loss"][:, 0:1]),
                            jnp.zeros((SMALL_ROWS - ROW_LOSS - 1, D_MODEL), F32)], axis=0)


def _unpack_small(p, like):
    out = {}
    for k, nm in enumerate(SMALL_NAMES):
        out[nm] = p[k, 0:like[nm].size].reshape(like[nm].shape)
    out["gate_a_w"] = p[ROW_GATE_A:ROW_GATE_A + 32].reshape(like["gate_a_w"].shape)
    out["gate_x_w"] = p[ROW_GATE_X:ROW_GATE_X + 32].reshape(like["gate_x_w"].shape)
    return out


def _gate_dense(wg):
    w4 = wg[0].reshape(N_LRU_TILES, 2, 64, 64)
    zero = jnp.zeros((N_LRU_TILES, 64, 64), wg.dtype)
    top = jnp.concatenate([w4[:, 0], zero], axis=2)
    bot = jnp.concatenate([zero, w4[:, 1]], axis=2)
    return jnp.concatenate([top, bot], axis=1).astype(BF16)


def _gate_blocks(dw):
    return jnp.stack([dw[:, 0:64, 0:64], dw[:, 64:128, 64:128]], axis=1).reshape(8, 64, 64)


BIG_NAMES = ["ffn1_w_gate", "ffn1_w_up", "ffn1_w_down", "w_in", "w_uq", "w_uk", "w_uv", "w_out", "ffn2_w_gate",
             "ffn2_w_up", "ffn2_w_down"]
BIG_GROUPS = [["ffn1_w_gate", "ffn1_w_up", "ffn1_w_down", "ffn2_w_gate", "ffn2_w_up", "ffn2_w_down"], ["w_in"],
              ["w_uq"], ["w_uk", "w_uv"], ["w_out"]]
TRANSPOSED = ("ffn1_w_gate", "ffn1_w_up", "ffn2_w_gate", "ffn2_w_up", "w_in", "w_uq")


def _to2d(nm, a):
    return a[0].T if nm in TRANSPOSED else a[0]


def _from2d(nm, a):
    return (a.T if nm in TRANSPOSED else a)[None]


WEIGHT_NAMES = ["meta_tokens", "ffn1_norm", "ffn1_w_gate", "ffn1_w_up", "ffn1_w_down", "mix_norm", "w_in",
                "q_latent_norm", "w_uq", "kv_latent_norm", "w_uk", "w_uv", "q_head_norm", "k_head_norm", "conv_w",
                "conv_b", "gate_a_w", "gate_a_b", "gate_x_w", "gate_x_b", "lru_lambda", "attn_out_norm",
                "lru_out_norm", "w_out", "ffn2_norm", "ffn2_w_gate", "ffn2_w_up", "ffn2_w_down", "final_norm"]


def _weight_from(nm, slots):
    if nm == "w_in":
        win = slots.reshape(IN_WIDTH, D_MODEL)
        return jnp.concatenate([win[0:Z_KR + D_ROPE], jnp.zeros((128 - D_ROPE, D_MODEL), BF16),
                                win[Z_KR + D_ROPE:]], axis=0)
    if nm == "w_uq":
        return jnp.pad(slots, ((0, 0), (0, D_QKP - D_QK), (0, 0))).reshape(HEADS * D_QKP, Q_RANK)
    if nm in ("w_uk", "w_uv"):
        return slots.transpose(1, 0, 2).reshape(KV_RANK, HEADS * D_NOPE)
    if nm == "w_out":
        return slots.reshape(D_MODEL, D_MODEL)
    return slots


def _small_weights(p, small):
    w = {nm: p[nm] for nm in SMALL_NAMES}
    w["q_head_norm"] = jnp.pad(p["q_head_norm"], ((0, 0), (0, D_QKP - D_QK)))
    w["k_head_norm"] = jnp.pad(p["k_head_norm"], ((0, 0), (0, D_QKP - D_QK)))
    w["conv_w"] = small[:, N_META:N_META + 2, :].reshape(N_SHARD, CONV_K, LRU_TILE)
    w["gate_a_w"] = _gate_dense(p["gate_a_w"])
    w["gate_x_w"] = _gate_dense(p["gate_x_w"])
    meta = small[:, 0:N_META, :].transpose(1, 0, 2).reshape(N_META, D_MODEL)
    return w, meta


def _full_weights(p, gathered, small):
    w, meta = _small_weights(p, small)
    w.update({nm: _weight_from(nm, gathered[nm]) for nm in BIG_NAMES})
    return w, meta


def _shard_grad(nm, g):
    if nm == "w_in":
        return jnp.concatenate([g[0:Z_KR + D_ROPE], g[Z_MLA:]], axis=0).reshape(N_SHARD, IN_WIDTH // N_SHARD, D_MODEL)
    if nm == "w_uq":
        return g.reshape(HEADS, D_QKP, Q_RANK)[:, 0:D_QK, :]
    if nm in ("w_uk", "w_uv"):
        return g.reshape(KV_RANK, HEADS, D_NOPE).transpose(1, 0, 2)
    if nm == "w_out":
        return g.reshape(N_SHARD, D_MODEL // N_SHARD, D_MODEL)
    return g


def _shard_grads(g):
    return {nm: _shard_grad(nm, g[nm]) for nm in BIG_NAMES}


GATHER_FIRST = ["ffn1_w_gate"]
GATHER_AT = {"ffn1_gate": ["ffn1_w_up"], "ffn1_upact": ["ffn1_w_down"],
             "ffn1_down": ["w_in", "w_uq", "w_uk", "w_uv", "w_out"], "attn_fwd": ["ffn2_w_down", "ffn2_w_gate"],
             "lru_fwd": ["ffn2_w_up"]}
PAIR_AT = [("ffn2_din", ["ffn2_w_gate", "ffn2_w_up", "ffn2_w_down"]),
           ("mix_din", ["w_out", "w_uq", "w_uk", "w_uv", "w_in"]),
           ("ffn1_dwg", ["ffn1_w_down"]), ("ffn1_dwu", ["ffn1_w_gate"]), ("ffn1_din_a", ["ffn1_w_up"])]
CHIPS_AT = [("attn_bwd", ["ffn2_w_down", "ffn2_w_gate"]), ("mla_prep_bwd", ["ffn2_w_up"]),
            ("ffn1_dact", ["w_out", "w_uq", "w_uk", "w_uv", "w_in"]),
            ("ffn1_dwu", ["ffn1_w_down"]), ("ffn1_din_a", ["ffn1_w_gate"]), ("ffn1_din_b", ["ffn1_w_up"])]
SHARE_EARLY_GROUPS, SHARE_EARLY_AT = 3, "ffn1_dwd"
SMALL_EARLY_AT = "mix_din"


def _same_shape_groups(names):
    return [[nm for nm in grp if nm in names] for grp in BIG_GROUPS if any(nm in names for nm in grp)]


class _Sched:
    def __init__(self, place, w, slots):
        self.place, self.w, self.slots = place, w, slots
        self.g = None
        self.sharded, self.from_pair, self.chip_bf16, self.from_chips = {}, {}, {}, {}
        self.early = self.early_all = None
        self.shared = {}

    def host(self, stage):
        comms = []
        if stage in GATHER_AT:
            comms.append(self.gather(GATHER_AT[stage]))
        comms += [self.chips(names) for at, names in CHIPS_AT if at == stage]
        comms += [self.pair(names) for at, names in PAIR_AT if at == stage]
        if stage == SMALL_EARLY_AT:
            comms.append(self.small_early())
        if stage == SHARE_EARLY_AT:
            comms.append(self.share_early())
        return _join_comms(comms)

    def small_early(self):
        self.early = _early_pack(self.g)

        def deliver(outs):
            self.early_all = outs[0]
            return outs

        return _small_comm(self.early, deliver)

    def gather(self, names):
        def deliver(outs):
            self.w.update({nm: _weight_from(nm, o) for nm, o in zip(names, outs)})
            return outs

        return _gather_comm([self.slots[nm] for nm in names], deliver)

    def pair(self, names):
        self.sharded.update({nm: _shard_grad(nm, self.g[nm]) for nm in names})

        def deliver(outs):
            self.from_pair.update(zip(names, outs))
            for grp in _same_shape_groups(names):
                sums = _pair_sum_call("pair_sum_" + grp[0], self.place, [self.sharded[nm] for nm in grp],
                                      [self.from_pair[nm] for nm in grp])
                self.chip_bf16.update(zip(grp, sums))
            return outs

        return _reduce_pair_comm([self.sharded[nm] for nm in names], deliver)

    def chips(self, names):
        def deliver(outs):
            self.from_chips.update(zip(names, outs))
            return outs

        return _reduce_chips_comm([self.chip_bf16[nm] for nm in names], deliver)

    def chip_sums(self, names, comm=None):
        out = {}
        for grp in _same_shape_groups(names):
            sums = _chip_sum_call("chip_sum_" + grp[0], self.place, [self.sharded[nm] for nm in grp],
                                  [self.from_pair[nm] for nm in grp], [self.from_chips[nm] for nm in grp], comm)
            comm = None
            out.update(zip(grp, sums))
        return out

    def share_early(self):
        names = [nm for at, grp in CHIPS_AT[:SHARE_EARLY_GROUPS] for nm in grp]
        mine = self.chip_sums(names)
        return _share_pair_comm([mine[nm] for nm in names], lambda o: self.shared.update(zip(names, o)))


def kernel(x, meta_tokens, ffn1_norm, ffn1_w_gate, ffn1_w_up, ffn1_w_down, mix_norm, w_in, q_latent_norm, w_uq, kv_latent_norm, w_uk, w_uv, q_head_norm, k_head_norm, conv_w, conv_b, gate_a_w, gate_a_b, gate_x_w, gate_x_b, lru_lambda, attn_out_norm, lru_out_norm, w_out, ffn2_norm, ffn2_w_gate, ffn2_w_up, ffn2_w_down, final_norm, loss_target, m_meta_tokens, m_ffn1_norm, m_ffn1_w_gate, m_ffn1_w_up, m_ffn1_w_down, m_mix_norm, m_w_in, m_q_latent_norm, m_w_uq, m_kv_latent_norm, m_w_uk, m_w_uv, m_q_head_norm, m_k_head_norm, m_conv_w, m_conv_b, m_gate_a_w, m_gate_a_b, m_gate_x_w, m_gate_x_b, m_lru_lambda, m_attn_out_norm, m_lru_out_norm, m_w_out, m_ffn2_norm, m_ffn2_w_gate, m_ffn2_w_up, m_ffn2_w_down, m_final_norm, v_meta_tokens, v_ffn1_norm, v_ffn1_w_gate, v_ffn1_w_up, v_ffn1_w_down, v_mix_norm, v_w_in, v_q_latent_norm, v_w_uq, v_kv_latent_norm, v_w_uk, v_w_uv, v_q_head_norm, v_k_head_norm, v_conv_w, v_conv_b, v_gate_a_w, v_gate_a_b, v_gate_x_w, v_gate_x_b, v_lru_lambda, v_attn_out_norm, v_lru_out_norm, v_w_out, v_ffn2_norm, v_ffn2_w_gate, v_ffn2_w_up, v_ffn2_w_down, v_final_norm):
    args = locals()
    p = {nm: args[nm] for nm in WEIGHT_NAMES}
    mom = {nm: args["m_" + nm] for nm in WEIGHT_NAMES}
    var = {nm: args["v_" + nm] for nm in WEIGHT_NAMES}
    nb, seq, d = x.shape
    lp = CHUNK + seq
    xi, yi, ci = lax.axis_index("x"), lax.axis_index("y"), lax.axis_index("c")
    chip = 2 * xi + yi

    place = jnp.stack([chip, ci]).astype(jnp.int32)
    p2 = {nm: _to2d(nm, p[nm]) for nm in BIG_NAMES}
    m2 = {nm: _to2d(nm, mom[nm]) for nm in BIG_NAMES}
    v2 = {nm: _to2d(nm, var[nm]) for nm in BIG_NAMES}

    slots = {}
    small_shard = jnp.concatenate(
        [meta_tokens, conv_w[0].reshape(2, 2 * LRU_TILE), jnp.zeros((14, 2 * LRU_TILE), F32)], axis=0)
    small_slots = lax.dynamic_update_slice(jnp.zeros((N_SHARD,) + small_shard.shape, F32), small_shard[None],
                                           (chip, 0, 0))
    for grp in BIG_GROUPS:
        for nm, buf in zip(grp, _cast_call("cast_" + grp[0], place, [p2[nm] for nm in grp])):
            slots[nm] = buf
    first = _comm_call("gather_first", _gather_comm([slots[nm] for nm in GATHER_FIRST] + [small_slots], lambda o: o))
    w, meta = _small_weights(p, first[-1])
    w.update({nm: _weight_from(nm, o) for nm, o in zip(GATHER_FIRST, first[:-1])})
    sched = _Sched(place, w, slots)

    h0 = jnp.concatenate(
        [jnp.zeros((nb, PAD_ROWS, d), F32), jnp.broadcast_to(meta[None], (nb, N_META, d)), x], axis=1)
    target = jnp.pad(loss_target, ((0, 0), (CHUNK, 0), (0, 0)))
    loss_part, dh0, g = _local_step(h0.reshape(nb * lp, d), target.reshape(nb * lp, d), w, nb, lp, sched)
    dh0 = dh0.reshape(nb, lp, d)
    grad_x = dh0[:, CHUNK:, :]

    late = jnp.concatenate([g["ffn1_norm"], g["mix_norm"], jnp.zeros((6, D_MODEL), F32),
                            jnp.sum(dh0[:, PAD_ROWS:CHUNK, :], axis=0)], axis=0)
    shared = sched.shared
    rest = [nm for at, names in CHIPS_AT[SHARE_EARLY_GROUPS:] if at is not None for nm in names]
    last = [nm for at, names in CHIPS_AT if at is None for nm in names]
    late_box = {}
    mine = sched.chip_sums(rest, _small_comm(late, lambda o: late_box.update(all=o[0])))
    share = _share_pair_comm([mine[nm] for nm in rest], lambda o: shared.update(zip(rest, o)))
    _comm_call("share_pair", _join_comms([share, sched.chips(last) if last else None]))
    if last:
        mine = sched.chip_sums(last)
        _comm_call("share_last", _share_pair_comm([mine[nm] for nm in last], lambda o: shared.update(zip(last, o))))
    late_all = late_box["all"]
    small_like = {nm: p[nm] for nm in SMALL_NAMES + ["gate_a_w", "gate_x_w"]}

    def pack_w(t):
        tt = {nm: t[nm] for nm in SMALL_NAMES + ["gate_a_w", "gate_x_w"]}
        tt["conv_w"] = jnp.zeros((CONV_K, LRU_W), F32)
        return _pack_small(tt, SMALL_ADAM_ROWS)

    me = (4 * xi + 2 * yi + ci).astype(jnp.int32).reshape(1)
    gsum, dsm, msm, vsm = _small_update_call(me, sched.early_all, sched.early, late_all, late, pack_w(p),
                                             pack_w(mom), pack_w(var))
    grads = _unpack_small(gsum, small_like)
    delta = _unpack_small(dsm, small_like)
    new_m = _unpack_small(msm, small_like)
    new_v = _unpack_small(vsm, small_like)
    loss = gsum[ROW_LOSS, 0]
    gmeta = gsum[ROW_META:ROW_META + N_META].reshape(N_META, N_SHARD, D_MODEL // N_SHARD)
    grads["meta_tokens"] = lax.dynamic_index_in_dim(gmeta, chip, axis=1, keepdims=False)
    gconv = gsum[ROW_CONV_W:ROW_CONV_W + 2].reshape(CONV_K, N_SHARD, LRU_TILE)
    grads["conv_w"] = lax.dynamic_index_in_dim(gconv, chip, axis=1, keepdims=False)[None]
    for nm in ("meta_tokens", "conv_w"):
        delta[nm], new_m[nm], new_v[nm] = _adamw_call("adamw_" + nm, p[nm], grads[nm], mom[nm], var[nm])

    for names in BIG_GROUPS:
        res = _adamw_group_call("adamw_" + names[0], [p2[nm] for nm in names], [shared[nm] for nm in names],
                                [m2[nm] for nm in names], [v2[nm] for nm in names])
        for nm, (gg, dd, mm, vv) in zip(names, res):
            grads[nm], delta[nm], new_m[nm], new_v[nm] = (_from2d(nm, t) for t in (gg, dd, mm, vv))

    return (loss, grad_x, *[grads[nm] for nm in WEIGHT_NAMES], *[delta[nm] for nm in WEIGHT_NAMES],
            *[new_m[nm] for nm in WEIGHT_NAMES], *[new_v[nm] for nm in WEIGHT_NAMES])
```

```python
import functools
import math

import jax
import jax.numpy as jnp
import numpy as np
from jax import lax
from jax.experimental import pallas as pl
from jax.experimental.pallas import tpu as pltpu

F32 = jnp.float32
BF16 = jnp.bfloat16
MESH = pl.DeviceIdType.MESH

D_MODEL = 1024
N_META = 16
CHUNK = 64
PAD_ROWS = CHUNK - N_META
HEADS = 4
D_NOPE = 128
D_ROPE = 64
D_QK = D_NOPE + D_ROPE
D_QKP = 256
D_V = 128
KV_RANK = 256
Q_RANK = 384
MLA_W = HEADS * D_V
LRU_W = 512
LRU_TILE = 128
N_LRU_TILES = LRU_W // LRU_TILE
CONV_K = 4
C_RGLRU = 8.0
ROPE_THETA = 10000.0
D_FF = 2816
N_SHARD = 4
EPS = 1e-6
NEG_INF = -1e30
Z_KR = Q_RANK + KV_RANK
Z_MLA = Z_KR + 128
Z_U = Z_MLA
Z_G = Z_U + LRU_W
Z_W = Z_G + LRU_W
IN_WIDTH = Q_RANK + KV_RANK + D_ROPE + 2 * LRU_W

ADAM_LR = 0.001
ADAM_B1 = 0.9
ADAM_B2 = 0.999
ADAM_EPS = 1e-08
ADAM_WD = 0.01
ADAM_STEP = 10

VMEM_LIMIT_BYTES = 56 * 1024 * 1024
SMALL_ROWS = 104
SMALL_ADAM_ROWS = 80


def _params(sem):
    return pltpu.CompilerParams(dimension_semantics=sem, vmem_limit_bytes=VMEM_LIMIT_BYTES)


def _resident(shape):
    return pl.BlockSpec(tuple(shape), lambda i: (0,) * len(shape), pipeline_mode=pl.Buffered(1))


def _row_tile(rows, target):
    best = 16
    for t in range(16, min(rows, target) + 1, 16):
        if rows % t == 0:
            best = t
    return best


def _col_tile(cols, target):
    best = cols
    for t in range(128, min(cols, target) + 1, 128):
        if cols % t == 0:
            best = t
    return best


def _dot(a, b):
    return jnp.dot(a, b, preferred_element_type=F32)


def _dot_nt(a, b):
    return lax.dot_general(a, b, (((1,), (1,)), ((), ())), preferred_element_type=F32)


def _dot_tn(a, b):
    return lax.dot_general(a, b, (((0,), (0,)), ((), ())), preferred_element_type=F32)


def _rms(x, n):
    return lax.rsqrt(jnp.sum(x * x, axis=-1, keepdims=True) * (1.0 / n) + EPS)


def _rms_bwd(dn, nrm, r, n):
    return r * (dn - nrm * (jnp.sum(dn * nrm, axis=-1, keepdims=True) * (1.0 / n)))


def _gelu(x):
    k = math.sqrt(2.0 / math.pi)
    t = jnp.tanh(k * (x + 0.044715 * x * x * x))
    return 0.5 * x * (1.0 + t), t


def _gelu_grad(x, t):
    k = math.sqrt(2.0 / math.pi)
    return 0.5 * (1.0 + t) + 0.5 * x * (1.0 - t * t) * k * (1.0 + 3.0 * 0.044715 * x * x)


def _sigmoid(x):
    return 0.5 + 0.5 * jnp.tanh(0.5 * x)


def _softplus_neg(lam):
    e = jnp.exp(-jnp.abs(lam))
    log1p = jnp.where(e < 0.01, e * (1.0 - e * (0.5 - e * (1.0 / 3 - e * 0.25))), jnp.log(1.0 + e))
    return jnp.maximum(-lam, 0.0) + log1p


def _rope(t, c, s1, s2):
    return t * c + pltpu.roll(t, 96, 1) * s1 + pltpu.roll(t, 32, 1) * s2


def _rope_t(d, c, s1, s2):
    return d * c + pltpu.roll(d * s1, 32, 1) + pltpu.roll(d * s2, 96, 1)


def _rope_tables(lp):
    pos = (np.arange(lp, dtype=np.int32) - PAD_ROWS).astype(np.float32)
    inv_freq = (ROPE_THETA ** (-np.arange(0, D_ROPE // 2, dtype=np.float32) / (D_ROPE // 2))).astype(np.float32)
    ang = (pos[:, None] * inv_freq[None, :]).astype(np.float32).astype(np.float64)
    cos, sin = np.cos(ang).astype(np.float32), np.sin(ang).astype(np.float32)
    z = np.zeros_like(cos)
    return (jnp.asarray(np.concatenate([cos, cos, z, z], 1)), jnp.asarray(np.concatenate([-sin, z, z, z], 1)),
            jnp.asarray(np.concatenate([z, sin, z, z], 1)))


def _rmsnorm_call(name, h, g, tm):
    rows, d = h.shape

    def body(h_ref, g_ref, o_ref):
        x = h_ref[...]
        o_ref[...] = (x * _rms(x, d) * g_ref[...]).astype(BF16)

    return pl.pallas_call(
        body, name=name, grid=(rows // tm,),
        in_specs=[pl.BlockSpec((tm, d), lambda i: (i, 0)), pl.BlockSpec((1, d), lambda i: (0, 0))],
        out_specs=pl.BlockSpec((tm, d), lambda i: (i, 0)),
        out_shape=jax.ShapeDtypeStruct((rows, d), BF16),
        compiler_params=_params(("parallel",)))(h, g)


def _ffn_up_call(name, u, wg, wu, tm, comm=None):
    rows, d = u.shape
    ns, fs, _ = wg.shape

    def body(u_ref, wg_ref, wu_ref, g_ref, p_ref, a_ref):
        uu = u_ref[...]
        g = _dot_nt(uu, wg_ref[0]).astype(BF16)
        p = _dot_nt(uu, wu_ref[0]).astype(BF16)
        g_ref[0] = g
        p_ref[0] = p
        a_ref[0] = (g * jax.nn.sigmoid(g)) * p

    wspec = pl.BlockSpec((1, fs, d), lambda s, i: (s, 0, 0))
    ospec = pl.BlockSpec((1, tm, fs), lambda s, i: (s, i, 0))
    oshape = jax.ShapeDtypeStruct((ns, rows, fs), BF16)
    return _hosted_call(
        body, name=name, grid=(ns, rows // tm),
        in_specs=[pl.BlockSpec((tm, d), lambda s, i: (i, 0)), wspec, wspec],
        out_specs=[ospec, ospec, ospec], out_shape=[oshape, oshape, oshape],
        dims=("parallel", "parallel"), args=(u, wg, wu), comm=comm)


def _ffn_gate_call(name, u, wg, tm, comm=None):
    rows, d = u.shape
    ns, fs, _ = wg.shape

    def body(u_ref, wg_ref, g_ref):
        g_ref[0] = _dot_nt(u_ref[...], wg_ref[0]).astype(BF16)

    return _hosted_call(
        body, name=name, grid=(ns, rows // tm),
        in_specs=[pl.BlockSpec((tm, d), lambda s, i: (i, 0)), pl.BlockSpec((1, fs, d), lambda s, i: (s, 0, 0))],
        out_specs=[pl.BlockSpec((1, tm, fs), lambda s, i: (s, i, 0))],
        out_shape=[jax.ShapeDtypeStruct((ns, rows, fs), BF16)],
        dims=("parallel", "parallel"), args=(u, wg), comm=comm)[0]


def _ffn_upact_call(name, u, wu, gate, tm, comm=None):
    rows, d = u.shape
    ns, fs, _ = wu.shape

    def body(u_ref, wu_ref, g_ref, p_ref, a_ref):
        p = _dot_nt(u_ref[...], wu_ref[0]).astype(BF16)
        g = g_ref[0]
        p_ref[0] = p
        a_ref[0] = (g * jax.nn.sigmoid(g)) * p

    ospec = pl.BlockSpec((1, tm, fs), lambda s, i: (s, i, 0))
    oshape = jax.ShapeDtypeStruct((ns, rows, fs), BF16)
    return _hosted_call(
        body, name=name, grid=(ns, rows // tm),
        in_specs=[pl.BlockSpec((tm, d), lambda s, i: (i, 0)), pl.BlockSpec((1, fs, d), lambda s, i: (s, 0, 0)), ospec],
        out_specs=[ospec, ospec], out_shape=[oshape, oshape],
        dims=("parallel", "parallel"), args=(u, wu, gate), comm=comm)


def _ffn_down_call(name, a, wd, h, tm, comm=None):
    rows, d = h.shape
    ns, _, fs = a.shape

    def body(a_ref, wd_ref, h_ref, o_ref):
        acc = h_ref[...]
        for s in range(ns):
            acc = acc + 0.5 * _dot(a_ref[s], wd_ref[s])
        o_ref[...] = acc

    return _hosted_call(
        body, name=name, grid=(rows // tm,),
        in_specs=[pl.BlockSpec((ns, tm, fs), lambda i: (0, i, 0)),
                  _resident((ns, fs, d)),
                  pl.BlockSpec((tm, d), lambda i: (i, 0))],
        out_specs=[pl.BlockSpec((tm, d), lambda i: (i, 0))],
        out_shape=[jax.ShapeDtypeStruct((rows, d), F32)],
        dims=("parallel",), args=(a, wd, h), comm=comm)[0]


def _mm_call(name, a, bt, tm, out_dtype):
    rows, k = a.shape
    n = bt.shape[0]

    def body(a_ref, b_ref, o_ref):
        o_ref[...] = _dot_nt(a_ref[...], b_ref[...]).astype(out_dtype)

    return pl.pallas_call(
        body, name=name, grid=(rows // tm,),
        in_specs=[pl.BlockSpec((tm, k), lambda i: (i, 0)), pl.BlockSpec((n, k), lambda i: (0, 0))],
        out_specs=pl.BlockSpec((tm, n), lambda i: (i, 0)),
        out_shape=jax.ShapeDtypeStruct((rows, n), out_dtype),
        compiler_params=_params(("parallel",)))(a, bt)


def _mla_heads(z, gql, gkvl, wuq, wuk, wuv):
    cq = z[:, 0:Q_RANK]
    ckv = z[:, Q_RANK:Z_KR]
    kr = z[:, Z_KR:Z_MLA]
    rq = _rms(cq, Q_RANK)
    nq = cq * rq
    cqn = (nq * gql).astype(BF16)
    rkv = _rms(ckv, KV_RANK)
    nkv = ckv * rkv
    ckvn = (nkv * gkvl).astype(BF16)
    qraw = _dot_nt(cqn, wuq)
    knope = _dot(ckvn, wuk)
    v = _dot(ckvn, wuv)
    skr = jnp.sum(kr * kr, axis=-1, keepdims=True)
    heads = []
    for hd in range(HEADS):
        qh = qraw[:, hd * D_QKP:(hd + 1) * D_QKP]
        rqh = lax.rsqrt(jnp.sum(qh * qh, axis=-1, keepdims=True) * (1.0 / D_QK) + EPS)
        kn = knope[:, hd * D_NOPE:(hd + 1) * D_NOPE]
        rkh = lax.rsqrt((jnp.sum(kn * kn, axis=-1, keepdims=True) + skr) * (1.0 / D_QK) + EPS)
        heads.append((qh * rqh, rqh, kn * rkh, kr * rkh, rkh))
    return dict(rq=rq, nq=nq, cqn=cqn, rkv=rkv, nkv=nkv, ckvn=ckvn, v=v, heads=heads)


def _mla_prep_call(z, gql, gkvl, gqh, gkh, wuq, wuk, wuv, tabs, lp, tm):
    rows = z.shape[0]
    tpe = lp // tm

    def body(z_ref, gql_ref, gkvl_ref, gqh_ref, gkh_ref, wuq_ref, wuk_ref, wuv_ref, c_ref, s1_ref, s2_ref,
             q_ref, k_ref, v_ref, cqn_ref, ckvn_ref):
        m = _mla_heads(z_ref[...], gql_ref[...], gkvl_ref[...], wuq_ref[...], wuk_ref[...], wuv_ref[...])
        c, s1, s2 = c_ref[...], s1_ref[...], s2_ref[...]
        gq, gk = gqh_ref[...], gkh_ref[...]
        row = (pl.program_id(0) % tpe) * tm + lax.broadcasted_iota(jnp.int32, (tm, 1), 0)
        spare = (lax.broadcasted_iota(jnp.int32, (1, D_QKP - D_NOPE), 1) == D_ROPE).astype(F32)
        kmask = jnp.where(row < PAD_ROWS, NEG_INF * math.sqrt(D_QK), 0.0) * spare
        for hd in range(HEADS):
            qn, _, knn, krn, _ = m["heads"][hd]
            qg = qn * gq
            q_ref[hd, :, 0:D_NOPE] = qg[:, 0:D_NOPE].astype(BF16)
            q_ref[hd, :, D_NOPE:D_QKP] = (_rope(qg[:, D_NOPE:D_QKP], c, s1, s2) + spare).astype(BF16)
            k_ref[hd, :, 0:D_NOPE] = (knn * gk[:, 0:D_NOPE]).astype(BF16)
            k_ref[hd, :, D_NOPE:D_QKP] = (_rope(krn * gk[:, D_NOPE:D_QKP], c, s1, s2) + kmask).astype(BF16)
            v_ref[hd] = m["v"][:, hd * D_V:(hd + 1) * D_V].astype(BF16)
        cqn_ref[...] = m["cqn"]
        ckvn_ref[...] = m["ckvn"]

    def const(shape):
        return pl.BlockSpec(shape, lambda i: tuple(0 for _ in shape))

    tab = pl.BlockSpec((tm, 128), lambda i: (i % tpe, 0))
    return pl.pallas_call(
        body, name="mla_prep", grid=(rows // tm,),
        in_specs=[pl.BlockSpec((tm, Z_MLA), lambda i: (i, 0)), const((1, Q_RANK)), const((1, KV_RANK)),
                  const((1, D_QKP)), const((1, D_QKP)), const((HEADS * D_QKP, Q_RANK)),
                  const((KV_RANK, HEADS * D_NOPE)), const((KV_RANK, HEADS * D_V)), tab, tab, tab],
        out_specs=[pl.BlockSpec((HEADS, tm, D_QKP), lambda i: (0, i, 0)),
                   pl.BlockSpec((HEADS, tm, D_QKP), lambda i: (0, i, 0)),
                   pl.BlockSpec((HEADS, tm, D_V), lambda i: (0, i, 0)),
                   pl.BlockSpec((tm, Q_RANK), lambda i: (i, 0)),
                   pl.BlockSpec((tm, KV_RANK), lambda i: (i, 0))],
        out_shape=[jax.ShapeDtypeStruct((HEADS, rows, D_QKP), BF16),
                   jax.ShapeDtypeStruct((HEADS, rows, D_QKP), BF16),
                   jax.ShapeDtypeStruct((HEADS, rows, D_V), BF16),
                   jax.ShapeDtypeStruct((rows, Q_RANK), BF16),
                   jax.ShapeDtypeStruct((rows, KV_RANK), BF16)],
        compiler_params=_params(("parallel",)))(z, gql, gkvl, gqh, gkh, wuq, wuk, wuv, *tabs)


Q_BLOCK_ROWS = 528
Q_BLOCK_ROWS_BWD = 192


def _q_block(lp):
    return _row_tile(lp, Q_BLOCK_ROWS)


def _key_end(ext, lp):
    return min(lp, -(-ext // CHUNK) * CHUNK)


def _diag_bias(qb, j0, nk):
    shift = CHUNK.bit_length() - 1
    r = jnp.right_shift(j0 + lax.broadcasted_iota(jnp.int32, (qb, nk), 0), shift)
    c = jnp.right_shift(j0 + lax.broadcasted_iota(jnp.int32, (qb, nk), 1), shift)
    return jnp.where(c <= r, 0.0, NEG_INF)


def _attn_fwd_call(q, k, v, nb, lp, comm=None):
    rows = nb * lp
    qb = _q_block(lp)
    scale = 1.0 / math.sqrt(D_QK)

    def body(q_ref, k_ref, v_ref, o_ref, lse_ref):
        for j in range(lp // qb):
            j0, ext = j * qb, (j + 1) * qb
            kend = _key_end(ext, lp)
            qj = q_ref[0, j0:ext, :]
            sd = _dot_nt(qj, k_ref[0, j0:kend, :]) * scale + _diag_bias(qb, j0, kend - j0)
            mx = jnp.max(sd, axis=-1, keepdims=True)
            if j > 0:
                so = _dot_nt(qj, k_ref[0, 0:j0, :]) * scale
                mx = jnp.maximum(mx, jnp.max(so, axis=-1, keepdims=True))
            pd = jnp.exp(sd - mx)
            l = jnp.sum(pd, axis=-1, keepdims=True)
            o = _dot(pd.astype(BF16), v_ref[0, j0:kend, :])
            if j > 0:
                po = jnp.exp(so - mx)
                l = l + jnp.sum(po, axis=-1, keepdims=True)
                o = o + _dot(po.astype(BF16), v_ref[0, 0:j0, :])
            o_ref[j0:ext, :] = o / l
            lse_ref[0, j0:ext, :] = mx + jnp.log(l)

    return _hosted_call(
        body, name="attn_fwd", grid=(nb, HEADS),
        in_specs=[pl.BlockSpec((1, lp, D_QKP), lambda b, h: (h, b, 0)),
                  pl.BlockSpec((1, lp, D_QKP), lambda b, h: (h, b, 0)),
                  pl.BlockSpec((1, lp, D_V), lambda b, h: (h, b, 0))],
        out_specs=[pl.BlockSpec((lp, D_V), lambda b, h: (b, h)),
                   pl.BlockSpec((1, lp, 1), lambda b, h: (h, b, 0))],
        out_shape=[jax.ShapeDtypeStruct((rows, MLA_W), F32),
                   jax.ShapeDtypeStruct((HEADS, rows, 1), F32)],
        dims=("parallel", "parallel"), args=(q, k, v), comm=comm)


def _attn_bwd_call(q, k, v, o, lse, do, nb, lp, comm=None):
    rows = nb * lp
    qb = _row_tile(lp, Q_BLOCK_ROWS_BWD)
    scale = 1.0 / math.sqrt(D_QK)

    def body(q_ref, k_ref, v_ref, o_ref, lse_ref, do_ref, dq_ref, dk_ref, dv_ref, dk_acc, dv_acc):
        dk_acc[...] = jnp.zeros_like(dk_acc)
        dv_acc[...] = jnp.zeros_like(dv_acc)
        shift = CHUNK.bit_length() - 1
        for j in range(lp // qb):
            j0, ext = j * qb, (j + 1) * qb
            kend = _key_end(ext, lp)
            qj = q_ref[0, j0:ext, :]
            doj = do_ref[j0:ext, :]
            delta = jnp.sum(doj * o_ref[j0:ext, :], axis=-1, keepdims=True)
            dob = doj.astype(BF16)
            kk = k_ref[0, 0:kend, :]
            qchunk = jnp.right_shift(j0 + lax.broadcasted_iota(jnp.int32, (qb, kend), 0), shift)
            kchunk = jnp.right_shift(lax.broadcasted_iota(jnp.int32, (qb, kend), 1), shift)
            s = _dot_nt(qj, kk) * scale - lse_ref[0, j0:ext, :]
            p = jnp.where(kchunk <= qchunk, jnp.exp(s), 0.0)
            dv_acc[0:kend, :] += _dot_tn(p.astype(BF16), dob)
            dp = _dot_nt(dob, v_ref[0, 0:kend, :])
            ds = (p * (dp - delta) * scale).astype(BF16)
            dq_ref[0, j0:ext, :] = _dot(ds, kk).astype(BF16)
            dk_acc[0:kend, :] += _dot_tn(ds, qj)
        dk_ref[0] = dk_acc[...].astype(BF16)
        dv_ref[0] = dv_acc[...].astype(BF16)

    qspec = pl.BlockSpec((1, lp, D_QKP), lambda b, h: (h, b, 0))
    vspec = pl.BlockSpec((1, lp, D_V), lambda b, h: (h, b, 0))
    ospec = pl.BlockSpec((lp, D_V), lambda b, h: (b, h))
    return _hosted_call(
        body, name="attn_bwd", grid=(nb, HEADS),
        in_specs=[qspec, qspec, vspec, ospec, pl.BlockSpec((1, lp, 1), lambda b, h: (h, b, 0)), ospec],
        out_specs=[qspec, qspec, vspec],
        out_shape=[jax.ShapeDtypeStruct((HEADS, rows, D_QKP), BF16),
                   jax.ShapeDtypeStruct((HEADS, rows, D_QKP), BF16),
                   jax.ShapeDtypeStruct((HEADS, rows, D_V), BF16)],
        scratch_shapes=[pltpu.VMEM((lp, D_QKP), F32), pltpu.VMEM((lp, D_V), F32)],
        dims=("parallel", "parallel"), args=(q, k, v, o, lse, do), comm=comm)


def _lru_gates(u, cw, cb, wa, ba, wx, bx, lam, lp):
    xc = (cw[3:4, :] * u + cw[2:3, :] * pltpu.roll(u, 1, 0) + cw[1:2, :] * pltpu.roll(u, 2, 0)
          + cw[0:1, :] * pltpu.roll(u, 3, 0) + cb)
    xcb = xc.astype(BF16)
    r = _sigmoid(_dot(xcb, wa) + ba)
    i = _sigmoid(_dot(xcb, wx) + bx)
    sp = _softplus_neg(lam)
    la = -C_RGLRU * r * sp
    a = jnp.exp(la)
    x2 = 2.0 * la
    e2 = a * a
    m2 = jnp.maximum(jnp.where(x2 > -0.01, -x2 * (1.0 + 0.5 * x2), 1.0 - e2), 1e-30)
    rs = lax.rsqrt(m2)
    row = lax.broadcasted_iota(jnp.int32, (lp, LRU_TILE), 0)
    first = row == PAD_ROWS
    valid = row >= PAD_ROWS
    mult_eff = jnp.where(first, 1.0, m2 * rs)
    return dict(xc=xc, xcb=xcb, r=r, i=i, sp=sp, a=a, e2=e2, rs=rs, mult_eff=mult_eff, first=first, valid=valid)


def _scan_rows(a, b, a_s, b_s, out_ref, lp, reverse):
    sub = lax.broadcasted_iota(jnp.int32, (lp, LRU_TILE), 0) & 7
    for dist in (1, 2, 4):
        shift = lp - dist if reverse else dist
        keep = (sub + dist <= 7) if reverse else (sub >= dist)
        a_sh = pltpu.roll(a, shift, 0)
        b_sh = pltpu.roll(b, shift, 0)
        b = jnp.where(keep, a * b_sh + b, b)
        a = jnp.where(keep, a * a_sh, a)
    a_s[...] = a
    b_s[...] = b
    n_groups = lp // 8
    edge = 0 if reverse else 7

    def group(gi, carry):
        r0 = pl.multiple_of(((n_groups - 1 - gi) if reverse else gi) * 8, 8)
        a8 = a_s[pl.ds(r0, 8), :]
        b8 = b_s[pl.ds(r0, 8), :]
        out_ref[pl.ds(r0, 8), :] = a8 * carry + b8
        return a8[edge:edge + 1, :] * carry + b8[edge:edge + 1, :]

    lax.fori_loop(0, n_groups, group, jnp.zeros((1, LRU_TILE), F32), unroll=4)


def _lru_specs(lp):
    seq = lambda col0: pl.BlockSpec((lp, LRU_TILE), lambda t, b: (b, col0 + t))
    cw = pl.BlockSpec((1, CONV_K, LRU_TILE), lambda t, b: (t, 0, 0))
    vec = pl.BlockSpec((1, LRU_TILE), lambda t, b: (0, t))
    mat = pl.BlockSpec((1, LRU_TILE, LRU_TILE), lambda t, b: (t, 0, 0))
    return seq, cw, vec, mat


def _lru_fwd_call(z, cw, cb, wa, ba, wx, bx, lam, nb, lp, comm=None):
    rows = nb * lp
    seq, cwspec, vec, mat = _lru_specs(lp)

    def body(u_ref, g_ref, cw_ref, cb_ref, wa_ref, ba_ref, wx_ref, bx_ref, lam_ref, y_ref, hs_ref, a_s, b_s):
        m = _lru_gates(u_ref[...], cw_ref[0], cb_ref[...], wa_ref[0], ba_ref[...], wx_ref[0], bx_ref[...],
                       lam_ref[...], lp)
        a = jnp.where(m["valid"], m["a"], 0.0)
        b = jnp.where(m["valid"], m["mult_eff"] * (m["i"] * m["xc"]), 0.0)
        _scan_rows(a, b, a_s, b_s, hs_ref, lp, reverse=False)
        gl, _ = _gelu(g_ref[...])
        y_ref[...] = hs_ref[...] * gl

    oshape = jax.ShapeDtypeStruct((rows, LRU_W), F32)
    return _hosted_call(
        body, name="lru_fwd", grid=(N_LRU_TILES, nb),
        in_specs=[seq(Z_U // LRU_TILE), seq(Z_G // LRU_TILE), cwspec, vec, mat, vec, mat, vec, vec],
        out_specs=[seq(0), seq(0)], out_shape=[oshape, oshape],
        scratch_shapes=[pltpu.VMEM((lp, LRU_TILE), F32), pltpu.VMEM((lp, LRU_TILE), F32)],
        dims=("parallel", "parallel"), args=(z, z, cw, cb, wa, ba, wx, bx, lam), comm=comm)


def _lru_bwd_call(z, hs, dy, cw, cb, wa, ba, wx, bx, lam, nb, lp, comm=None):
    rows = nb * lp
    seq, cwspec, vec, mat = _lru_specs(lp)

    def body(u_ref, g_ref, hs_ref, dy_ref, cw_ref, cb_ref, wa_ref, ba_ref, wx_ref, bx_ref, lam_ref,
             du_ref, dg_ref, dcw_ref, dcb_ref, dwa_ref, dba_ref, dwx_ref, dbx_ref, dlam_ref, a_s, b_s, d_s):
        b_idx = pl.program_id(1)
        u = u_ref[...]
        cw = cw_ref[0]
        wa, wx = wa_ref[0], wx_ref[0]
        lam = lam_ref[...]
        m = _lru_gates(u, cw, cb_ref[...], wa, ba_ref[...], wx, bx_ref[...], lam, lp)
        gate = g_ref[...]
        gl, th = _gelu(gate)
        dy = dy_ref[...]
        hs = hs_ref[...]
        dg_ref[...] = (dy * hs * _gelu_grad(gate, th)).astype(BF16)
        a_eff = jnp.where(m["valid"], m["a"], 0.0)
        _scan_rows(pltpu.roll(a_eff, lp - 1, 0), dy * gl, a_s, b_s, d_s, lp, reverse=True)
        ds = d_s[...]
        xc, r, i = m["xc"], m["r"], m["i"]
        row = lax.broadcasted_iota(jnp.int32, (lp, LRU_TILE), 0)
        da = ds * jnp.where(row >= 1, pltpu.roll(hs, 1, 0), 0.0)
        db = jnp.where(m["valid"], ds, 0.0)
        di = db * m["mult_eff"] * xc
        dxc = db * m["mult_eff"] * i
        live = m["valid"] & jnp.logical_not(m["first"])
        dm = jnp.where(live, db * i * xc, 0.0)
        dla = da * m["a"] - dm * (m["e2"] * m["rs"])
        dr = dla * (-C_RGLRU * m["sp"])
        dsp = jnp.sum(dla * (-C_RGLRU * r), axis=0, keepdims=True)
        dpr = (dr * r * (1.0 - r))
        dpi = (di * i * (1.0 - i))
        dprb, dpib = dpr.astype(BF16), dpi.astype(BF16)
        dxc = dxc + _dot_nt(dprb, wa) + _dot_nt(dpib, wx)
        du = (cw[3:4, :] * dxc + cw[2:3, :] * pltpu.roll(dxc, lp - 1, 0) + cw[1:2, :] * pltpu.roll(dxc, lp - 2, 0)
              + cw[0:1, :] * pltpu.roll(dxc, lp - 3, 0))
        du_ref[...] = jnp.where(m["valid"], du, 0.0).astype(BF16)
        tap = lax.broadcasted_iota(jnp.int32, (CONV_K, LRU_TILE), 0)
        dcw = jnp.zeros((CONV_K, LRU_TILE), F32)
        for kk in range(CONV_K):
            shifted = u if kk == CONV_K - 1 else pltpu.roll(u, CONV_K - 1 - kk, 0)
            dcw = jnp.where(tap == kk, jnp.sum(dxc * shifted, axis=0, keepdims=True), dcw)
        parts = [(dcw_ref, dcw[None]), (dcb_ref, jnp.sum(dxc, axis=0, keepdims=True)[None]),
                 (dwa_ref, _dot_tn(m["xcb"], dprb)[None]), (dba_ref, jnp.sum(dpr, axis=0, keepdims=True)[None]),
                 (dwx_ref, _dot_tn(m["xcb"], dpib)[None]), (dbx_ref, jnp.sum(dpi, axis=0, keepdims=True)[None]),
                 (dlam_ref, (dsp * (-jax.nn.sigmoid(-lam)))[None])]

        @pl.when(b_idx == 0)
        def _():
            for ref, val in parts:
                ref[...] = val

        @pl.when(b_idx != 0)
        def _():
            for ref, val in parts:
                ref[...] += val

    bshape = jax.ShapeDtypeStruct((rows, LRU_W), BF16)
    vec3 = pl.BlockSpec((1, 1, LRU_TILE), lambda t, b: (t, 0, 0))
    vshape = jax.ShapeDtypeStruct((N_LRU_TILES, 1, LRU_TILE), F32)
    mshape = jax.ShapeDtypeStruct((N_LRU_TILES, LRU_TILE, LRU_TILE), F32)
    return _hosted_call(
        body, name="lru_bwd", grid=(N_LRU_TILES, nb),
        in_specs=[seq(Z_U // LRU_TILE), seq(Z_G // LRU_TILE), seq(0), seq(0), cwspec, vec, mat, vec, mat, vec, vec],
        out_specs=[seq(0), seq(0), cwspec, vec3, mat, vec3, mat, vec3, vec3],
        out_shape=[bshape, bshape, jax.ShapeDtypeStruct((N_LRU_TILES, CONV_K, LRU_TILE), F32), vshape, mshape,
                   vshape, mshape, vshape, vshape],
        scratch_shapes=[pltpu.VMEM((lp, LRU_TILE), F32)] * 3,
        dims=("parallel", "arbitrary"), args=(z, z, hs, dy, cw, cb, wa, ba, wx, bx, lam), comm=comm)


def _mix_out_call(ya, yl, ga, gl, wout, h, tm):
    rows, d = h.shape

    def body(ya_ref, yl_ref, ga_ref, gl_ref, w_ref, h_ref, y_ref, o_ref):
        a = ya_ref[...]
        l = yl_ref[...]
        an = (a * _rms(a, MLA_W) * ga_ref[...]).astype(BF16)
        ln = (l * _rms(l, LRU_W) * gl_ref[...]).astype(BF16)
        y_ref[:, 0:MLA_W] = an
        y_ref[:, MLA_W:MLA_W + LRU_W] = ln
        o_ref[...] = h_ref[...] + _dot(an, w_ref[0:MLA_W, :]) + _dot(ln, w_ref[MLA_W:MLA_W + LRU_W, :])

    half = pl.BlockSpec((tm, MLA_W), lambda i: (i, 0))
    g = pl.BlockSpec((1, MLA_W), lambda i: (0, 0))
    full = pl.BlockSpec((tm, d), lambda i: (i, 0))
    return pl.pallas_call(
        body, name="mix_out", grid=(rows // tm,),
        in_specs=[half, half, g, g, pl.BlockSpec((MLA_W + LRU_W, d), lambda i: (0, 0)), full],
        out_specs=[full, full],
        out_shape=[jax.ShapeDtypeStruct((rows, MLA_W + LRU_W), BF16), jax.ShapeDtypeStruct((rows, d), F32)],
        compiler_params=_params(("parallel",)))(ya, yl, ga, gl, wout, h)


def _mix_out_bwd_call(dhb, wout, ya, yl, ga, gl, tm):
    rows = ya.shape[0]
    d = dhb.shape[1]

    def body(dh_ref, w_ref, ya_ref, yl_ref, ga_ref, gl_ref, dya_ref, dyl_ref, dga_ref, dgl_ref):
        dy = _dot_nt(dh_ref[...], w_ref[...])
        outs = []
        for val, g_ref, lo, out_ref in ((ya_ref[...], ga_ref, 0, dya_ref), (yl_ref[...], gl_ref, MLA_W, dyl_ref)):
            r = _rms(val, MLA_W)
            n = val * r
            dyn = dy[:, lo:lo + MLA_W]
            out_ref[...] = _rms_bwd(dyn * g_ref[...], n, r, MLA_W)
            outs.append(jnp.sum(dyn * n, axis=0, keepdims=True))

        @pl.when(pl.program_id(0) == 0)
        def _():
            dga_ref[...] = outs[0]
            dgl_ref[...] = outs[1]

        @pl.when(pl.program_id(0) != 0)
        def _():
            dga_ref[...] += outs[0]
            dgl_ref[...] += outs[1]

    half = pl.BlockSpec((tm, MLA_W), lambda i: (i, 0))
    g = pl.BlockSpec((1, MLA_W), lambda i: (0, 0))
    return pl.pallas_call(
        body, name="mix_out_bwd", grid=(rows // tm,),
        in_specs=[pl.BlockSpec((tm, d), lambda i: (i, 0)), pl.BlockSpec((MLA_W + LRU_W, d), lambda i: (0, 0)),
                  half, half, g, g],
        out_specs=[half, half, g, g],
        out_shape=[jax.ShapeDtypeStruct((rows, MLA_W), F32), jax.ShapeDtypeStruct((rows, LRU_W), F32),
                   jax.ShapeDtypeStruct((1, MLA_W), F32), jax.ShapeDtypeStruct((1, LRU_W), F32)],
        compiler_params=_params(("arbitrary",)))(dhb, wout, ya, yl, ga, gl)


def _final_call(h, g, target, lp, tm):
    rows, d = h.shape
    tpe = lp // tm

    def body(h_ref, g_ref, t_ref, dh_ref, dhb_ref, dg_ref, loss_ref):
        i = pl.program_id(0)
        x = h_ref[...]
        g = g_ref[...]
        r = _rms(x, d)
        n = x * r
        row = (i % tpe) * tm + lax.broadcasted_iota(jnp.int32, (tm, 1), 0)
        err = jnp.where(row >= CHUNK, n * g - t_ref[...], 0.0)
        dout = err * (1.0 / d)
        dh = _rms_bwd(dout * g, n, r, d)
        dh_ref[...] = dh
        dhb_ref[...] = dh.astype(BF16)
        dg = jnp.sum(dout * n, axis=0, keepdims=True)
        part = jnp.sum(jnp.sum(err * err, axis=1, keepdims=True), axis=0, keepdims=True) * (0.5 / d)
        loss = jnp.broadcast_to(part, (1, 128))

        @pl.when(i == 0)
        def _():
            dg_ref[...] = dg
            loss_ref[...] = loss

        @pl.when(i != 0)
        def _():
            dg_ref[...] += dg
            loss_ref[...] += loss

    full = pl.BlockSpec((tm, d), lambda i: (i, 0))
    return pl.pallas_call(
        body, name="final_loss", grid=(rows // tm,),
        in_specs=[full, pl.BlockSpec((1, d), lambda i: (0, 0)), full],
        out_specs=[full, full, pl.BlockSpec((1, d), lambda i: (0, 0)), pl.BlockSpec((1, 128), lambda i: (0, 0))],
        out_shape=[jax.ShapeDtypeStruct((rows, d), F32), jax.ShapeDtypeStruct((rows, d), BF16),
                   jax.ShapeDtypeStruct((1, d), F32), jax.ShapeDtypeStruct((1, 128), F32)],
        compiler_params=_params(("arbitrary",)))(h, g, target)


def _ffn_dact_call(name, dhb, wd, gate, up, tm, comm=None):
    rows, d = dhb.shape
    ns, fs, _ = wd.shape

    nsub = 2 if tm % 32 == 0 else 1
    sub = tm // nsub

    def body(dh_ref, wd_ref, g_ref, p_ref, dg_ref, dp_ref):
        wd = wd_ref[0]
        for r in range(nsub):
            rs = slice(r * sub, (r + 1) * sub)
            da = (0.5 * _dot_nt(dh_ref[rs, :], wd)).astype(BF16)
            g = g_ref[0, rs, :]
            p = p_ref[0, rs, :]
            sg = jax.nn.sigmoid(g)
            dg_ref[0, rs, :] = (da * p) * (sg * (1.0 + g * (1.0 - sg)))
            dp_ref[0, rs, :] = da * (g * sg)

    aspec = pl.BlockSpec((1, tm, fs), lambda s, i: (s, i, 0))
    oshape = jax.ShapeDtypeStruct((ns, rows, fs), BF16)
    return _hosted_call(
        body, name=name, grid=(ns, rows // tm),
        in_specs=[pl.BlockSpec((tm, d), lambda s, i: (i, 0)), pl.BlockSpec((1, fs, d), lambda s, i: (s, 0, 0)),
                  aspec, aspec],
        out_specs=[aspec, aspec], out_shape=[oshape, oshape],
        dims=("parallel", "parallel"), args=(dhb, wd, gate, up), comm=comm)


def _norm_in_bwd_call(name, pieces, h, g, dres, tm, comm=None, part=(0, 1), prev=None):
    rows, d = h.shape
    npc = len(pieces)
    steps = rows // tm // part[1]
    off = part[0] * steps
    n_prev = 0 if prev is None else 2

    def body(*refs):
        d_refs = refs[0:2 * npc:2]
        w_refs = refs[1:2 * npc:2]
        h_ref, g_ref, dres_ref = refs[2 * npc:2 * npc + 3]
        dh_ref, dhb_ref, dg_ref = refs[2 * npc + 3 + n_prev:]
        du = jnp.zeros((tm, d), F32)
        for d_ref, w_ref in zip(d_refs, w_refs):
            if len(d_ref.shape) == 3:
                for s in range(d_ref.shape[0]):
                    du = du + _dot(d_ref[s], w_ref[s])
            else:
                du = du + _dot(d_ref[...], w_ref[...])
        x = h_ref[...]
        r = _rms(x, d)
        n = x * r
        dh = dres_ref[...] + _rms_bwd(du * g_ref[...], n, r, d)
        dh_ref[...] = dh
        dhb_ref[...] = dh.astype(BF16)
        dg = jnp.sum(du * n, axis=0, keepdims=True)

        @pl.when(pl.program_id(0) == 0)
        def _():
            dg_ref[...] = dg

        @pl.when(pl.program_id(0) != 0)
        def _():
            dg_ref[...] += dg

    in_specs, args = [], []
    for dd, w in pieces:
        if dd.ndim == 3:
            in_specs.append(pl.BlockSpec((dd.shape[0], tm, dd.shape[2]), lambda i: (0, i + off, 0)))
            in_specs.append(_resident(w.shape))
        else:
            in_specs.append(pl.BlockSpec((tm, dd.shape[1]), lambda i: (i + off, 0)))
            in_specs.append(_resident(w.shape))
        args += [dd, w]
    full = pl.BlockSpec((tm, d), lambda i: (i + off, 0))
    gspec = pl.BlockSpec((1, d), lambda i: (0, 0))
    n_in = len(in_specs) + 3
    return _hosted_call(
        body, name=name, grid=(steps,),
        in_specs=in_specs + [full, gspec, full] + _any_specs(n_prev),
        out_specs=[full, full, gspec],
        out_shape=[jax.ShapeDtypeStruct((rows, d), F32), jax.ShapeDtypeStruct((rows, d), BF16),
                   jax.ShapeDtypeStruct((1, d), F32)],
        args=(*args, h, g, dres, *(prev or ())), comm=comm,
        aliases={n_in: 0, n_in + 1: 1} if prev is not None else {})


def _wgrad_call(name, a, b, scale=1.0, comm=None):
    a3, b3 = a.ndim == 3, b.ndim == 3
    ns = a.shape[0] if a3 else (b.shape[0] if b3 else 1)
    rows, m = a.shape[-2:]
    n = b.shape[-1]
    tmm = m if a3 else _col_tile(m, 256)

    def body(a_ref, b_ref, o_ref):
        av = a_ref[0] if a3 else a_ref[...]
        bv = b_ref[0] if b3 else b_ref[...]
        res = _dot_tn(av, bv)
        if scale != 1.0:
            res = res * scale
        if a3 or b3:
            o_ref[0] = res
        else:
            o_ref[...] = res

    aspec = (pl.BlockSpec((1, rows, tmm), lambda s, j: (s, 0, j)) if a3
             else pl.BlockSpec((rows, tmm), lambda s, j: (0, j)))
    bspec = (pl.BlockSpec((1, rows, n), lambda s, j: (s, 0, 0)) if b3
             else pl.BlockSpec((rows, n), lambda s, j: (0, 0)))
    if a3 or b3:
        ospec = pl.BlockSpec((1, tmm, n), lambda s, j: (s, j, 0))
        oshape = jax.ShapeDtypeStruct((ns, m, n), F32)
    else:
        ospec = pl.BlockSpec((tmm, n), lambda s, j: (j, 0))
        oshape = jax.ShapeDtypeStruct((m, n), F32)
    return _hosted_call(
        body, name=name, grid=(ns, m // tmm), in_specs=[aspec, bspec], out_specs=[ospec], out_shape=[oshape],
        dims=("parallel", "parallel"), args=(a, b), comm=comm)[0]


def _mla_prep_bwd_call(z, dq, dk, dv, gql, gkvl, gqh, gkh, wuq, wuk, wuv, tabs, lp, tm, comm=None):
    rows = z.shape[0]
    tpe = lp // tm

    def body(z_ref, dq_ref, dk_ref, dv_ref, gql_ref, gkvl_ref, gqh_ref, gkh_ref, wuq_ref, wuk_ref, wuv_ref,
             c_ref, s1_ref, s2_ref, dz_ref, dqp_ref, dkn_ref, dvv_ref, dgql_ref, dgkvl_ref, dgqh_ref, dgkh_ref):
        gql, gkvl = gql_ref[...], gkvl_ref[...]
        gq, gk = gqh_ref[...], gkh_ref[...]
        wuq, wuk, wuv = wuq_ref[...], wuk_ref[...], wuv_ref[...]
        m = _mla_heads(z_ref[...], gql, gkvl, wuq, wuk, wuv)
        c, s1, s2 = c_ref[...], s1_ref[...], s2_ref[...]
        dgq = jnp.zeros((1, D_QKP), F32)
        dgk = jnp.zeros((1, D_QKP), F32)
        dkr = jnp.zeros((tm, D_QKP - D_NOPE), F32)
        for hd in range(HEADS):
            qn, rqh, knn, krn, rkh = m["heads"][hd]
            dqg = jnp.concatenate([dq_ref[hd, :, 0:D_NOPE].astype(F32),
                                   _rope_t(dq_ref[hd, :, D_NOPE:D_QKP].astype(F32), c, s1, s2)], axis=1)
            dgq = dgq + jnp.sum(dqg * qn, axis=0, keepdims=True)
            dqn = dqg * gq
            dqr = rqh * (dqn - qn * (jnp.sum(dqn * qn, axis=-1, keepdims=True) * (1.0 / D_QK)))
            dqp_ref[:, hd * D_QKP:(hd + 1) * D_QKP] = dqr.astype(BF16)
            kn_full = jnp.concatenate([knn, krn], axis=1)
            dkg = jnp.concatenate([dk_ref[hd, :, 0:D_NOPE].astype(F32),
                                   _rope_t(dk_ref[hd, :, D_NOPE:D_QKP].astype(F32), c, s1, s2)], axis=1)
            dgk = dgk + jnp.sum(dkg * kn_full, axis=0, keepdims=True)
            dkn = dkg * gk
            dkraw = rkh * (dkn - kn_full * (jnp.sum(dkn * kn_full, axis=-1, keepdims=True) * (1.0 / D_QK)))
            dkn_ref[:, hd * D_NOPE:(hd + 1) * D_NOPE] = dkraw[:, 0:D_NOPE].astype(BF16)
            dkr = dkr + dkraw[:, D_NOPE:D_QKP]
            dvv_ref[:, hd * D_V:(hd + 1) * D_V] = dv_ref[hd]
        dcqn = _dot(dqp_ref[...], wuq)
        dckvn = _dot_nt(dkn_ref[...], wuk) + _dot_nt(dvv_ref[...], wuv)
        dz_ref[:, 0:Q_RANK] = _rms_bwd(dcqn * gql, m["nq"], m["rq"], Q_RANK).astype(BF16)
        dz_ref[:, Q_RANK:Z_KR] = _rms_bwd(dckvn * gkvl, m["nkv"], m["rkv"], KV_RANK).astype(BF16)
        dz_ref[:, Z_KR:Z_MLA] = dkr.astype(BF16)
        parts = [(dgql_ref, jnp.sum(dcqn * m["nq"], axis=0, keepdims=True)),
                 (dgkvl_ref, jnp.sum(dckvn * m["nkv"], axis=0, keepdims=True)), (dgqh_ref, dgq), (dgkh_ref, dgk)]

        @pl.when(pl.program_id(0) == 0)
        def _():
            for ref, val in parts:
                ref[...] = val

        @pl.when(pl.program_id(0) != 0)
        def _():
            for ref, val in parts:
                ref[...] += val

    def const(shape):
        return pl.BlockSpec(shape, lambda i: tuple(0 for _ in shape))

    tab = pl.BlockSpec((tm, 128), lambda i: (i % tpe, 0))
    hq = pl.BlockSpec((HEADS, tm, D_QKP), lambda i: (0, i, 0))
    hv = pl.BlockSpec((HEADS, tm, D_V), lambda i: (0, i, 0))

    def rowspec(n):
        return pl.BlockSpec((tm, n), lambda i: (i, 0))

    return _hosted_call(
        body, name="mla_prep_bwd", grid=(rows // tm,),
        in_specs=[rowspec(Z_MLA), hq, hq, hv, const((1, Q_RANK)), const((1, KV_RANK)), const((1, D_QKP)),
                  const((1, D_QKP)), const((HEADS * D_QKP, Q_RANK)), const((KV_RANK, HEADS * D_NOPE)),
                  const((KV_RANK, HEADS * D_V)), tab, tab, tab],
        out_specs=[rowspec(Z_MLA), rowspec(HEADS * D_QKP), rowspec(HEADS * D_NOPE), rowspec(HEADS * D_V),
                   const((1, Q_RANK)), const((1, KV_RANK)), const((1, D_QKP)), const((1, D_QKP))],
        out_shape=[jax.ShapeDtypeStruct((rows, Z_MLA), BF16), jax.ShapeDtypeStruct((rows, HEADS * D_QKP), BF16),
                   jax.ShapeDtypeStruct((rows, HEADS * D_NOPE), BF16), jax.ShapeDtypeStruct((rows, HEADS * D_V), BF16),
                   jax.ShapeDtypeStruct((1, Q_RANK), F32), jax.ShapeDtypeStruct((1, KV_RANK), F32),
                   jax.ShapeDtypeStruct((1, D_QKP), F32), jax.ShapeDtypeStruct((1, D_QKP), F32)],
        args=(z, dq, dk, dv, gql, gkvl, gqh, gkh, wuq, wuk, wuv, *tabs), comm=comm)


def _local_step(h0, target, w, nb, lp, sched=None):
    tm = _row_tile(nb * lp, 1408)
    te = _row_tile(lp, 512)
    tabs = _rope_tables(lp)
    g = {}
    if sched is None:
        host = lambda stage: None
    else:
        sched.g = g
        host = sched.host

    def ffn_fwd(tag, h, split):
        u = _rmsnorm_call(tag + "_norm", h, w[tag + "_norm"], te)
        if split:
            gate = _ffn_gate_call(tag + "_gate", u, w[tag + "_w_gate"], tm, host(tag + "_gate"))
            up, act = _ffn_upact_call(tag + "_upact", u, w[tag + "_w_up"], gate, tm, host(tag + "_upact"))
        else:
            gate, up, act = _ffn_up_call(tag + "_up", u, w[tag + "_w_gate"], w[tag + "_w_up"], tm, host(tag + "_up"))
        return _ffn_down_call(tag + "_down", act, w[tag + "_w_down"], h, te, host(tag + "_down")), (u, gate, up, act)

    def ffn_bwd(tag, h, saved, dh, dhb, split):
        u, gate, up, act = saved
        dgate, dup = _ffn_dact_call(tag + "_dact", dhb, w[tag + "_w_down"], gate, up, tm, host(tag + "_dact"))
        g[tag + "_w_down"] = _wgrad_call(tag + "_dwd", act, dhb, 0.5, host(tag + "_dwd"))
        g[tag + "_w_gate"] = _wgrad_call(tag + "_dwg", dgate, u, 1.0, host(tag + "_dwg"))
        g[tag + "_w_up"] = _wgrad_call(tag + "_dwu", dup, u, 1.0, host(tag + "_dwu"))
        pieces = [(dgate, w[tag + "_w_gate"]), (dup, w[tag + "_w_up"])]
        if not split:
            dh_in, dhb_in, g[tag + "_norm"] = _norm_in_bwd_call(tag + "_din", pieces, h, w[tag + "_norm"], dh, te,
                                                                host(tag + "_din"))
            return dh_in, dhb_in
        dh_a, dhb_a, dg_a = _norm_in_bwd_call(tag + "_din_a", pieces, h, w[tag + "_norm"], dh, te,
                                              host(tag + "_din_a"), part=(0, 2))
        dh_in, dhb_in, dg_b = _norm_in_bwd_call(tag + "_din_b", pieces, h, w[tag + "_norm"], dh, te,
                                                host(tag + "_din_b"), part=(1, 2), prev=(dh_a, dhb_a))
        g[tag + "_norm"] = dg_a + dg_b
        return dh_in, dhb_in

    h1, s1 = ffn_fwd("ffn1", h0, True)
    un = _rmsnorm_call("mix_norm", h1, w["mix_norm"], te)
    z = _mm_call("mix_in", un, w["w_in"], tm, F32)
    mla_w = (w["q_latent_norm"], w["kv_latent_norm"], w["q_head_norm"], w["k_head_norm"], w["w_uq"], w["w_uk"],
             w["w_uv"])
    q, k, v, cqn, ckvn = _mla_prep_call(z, *mla_w, tabs, lp, te)
    o, lse = _attn_fwd_call(q, k, v, nb, lp, host("attn_fwd"))
    lru_w = (w["conv_w"], w["conv_b"], w["gate_a_w"], w["gate_a_b"], w["gate_x_w"], w["gate_x_b"], w["lru_lambda"])
    yl, hs = _lru_fwd_call(z, *lru_w, nb, lp, host("lru_fwd"))
    y, h2 = _mix_out_call(o, yl, w["attn_out_norm"], w["lru_out_norm"], w["w_out"], h1, te)
    h3, s2 = ffn_fwd("ffn2", h2, False)
    dh3, dh3b, g["final_norm"], loss = _final_call(h3, w["final_norm"], target, lp, te)
    g["loss"] = loss

    dh2, dh2b = ffn_bwd("ffn2", h2, s2, dh3, dh3b, False)
    g["w_out"] = _wgrad_call("dw_out", y, dh2b)
    dya, dyl, g["attn_out_norm"], g["lru_out_norm"] = _mix_out_bwd_call(
        dh2b, w["w_out"], o, yl, w["attn_out_norm"], w["lru_out_norm"], te)
    dq, dk, dv = _attn_bwd_call(q, k, v, o, lse, dya, nb, lp, host("attn_bwd"))
    (dz_mla, dqp, dkn, dvv, g["q_latent_norm"], g["kv_latent_norm"], g["q_head_norm"],
     g["k_head_norm"]) = _mla_prep_bwd_call(z, dq, dk, dv, *mla_w, tabs, lp, te, host("mla_prep_bwd"))
    g["w_uq"] = _wgrad_call("dw_uq", dqp, cqn)
    g["w_uk"] = _wgrad_call("dw_uk", ckvn, dkn)
    g["w_uv"] = _wgrad_call("dw_uv", ckvn, dvv)
    (du, dgt, g["conv_w"], g["conv_b"], g["gate_a_w"], g["gate_a_b"], g["gate_x_w"], g["gate_x_b"],
     g["lru_lambda"]) = _lru_bwd_call(z, hs, dyl, *lru_w, nb, lp, host("lru_bwd"))
    win = w["w_in"]
    g["w_in"] = jnp.concatenate(
        [_wgrad_call("dw_in_mla", dz_mla, un), _wgrad_call("dw_in_u", du, un), _wgrad_call("dw_in_g", dgt, un)],
        axis=0)
    dh1, dh1b, g["mix_norm"] = _norm_in_bwd_call(
        "mix_din", [(dz_mla, win[0:Z_MLA]), (du, win[Z_U:Z_G]), (dgt, win[Z_G:Z_W])], h1, w["mix_norm"], dh2, te,
        host("mix_din"))
    dh0, _ = ffn_bwd("ffn1", h0, s1, dh1, dh1b, True)
    return loss, dh0, g


def _place():
    x, y, c = lax.axis_index("x"), lax.axis_index("y"), lax.axis_index("c")
    return x, y, c, [(1 - x, y), (x, 1 - y), (1 - x, 1 - y)]


def _any_specs(n):
    return [pl.BlockSpec(memory_space=pl.ANY)] * n


def _remote(src, dst, sems, k, dev):
    send_sems, recv_sems, base = sems
    return pltpu.make_async_remote_copy(src_ref=src, dst_ref=dst, send_sem=send_sems.at[base + k],
                                        recv_sem=recv_sems.at[base + k], device_id=dev, device_id_type=MESH)


EW_VMEM_BYTES = 24 * 1024 * 1024


def _fit_rows(rows, cols, blocks):
    return _row_tile(rows, max(16, int(EW_VMEM_BYTES // (8 * blocks)) // cols))


class _Geom:
    def __init__(self, n0, n1, blocks=1.0):
        self.n0, self.n1 = n0, n1
        self.axis = 0 if n0 % 32 == 0 else 1
        self.h0, self.h1 = (n0 // 2, n1) if self.axis == 0 else (n0, n1 // 2)
        self.tr = _fit_rows(self.h0, self.h1, blocks)
        self.nblk = self.h0 // self.tr

    def half_ref(self, ref, lead, idx):
        if self.axis == 0:
            return ref.at[(*lead, pl.ds(idx * self.h0, self.h0))]
        return ref.at[(*lead, slice(None), pl.ds(idx * self.h1, self.h1))]

    def half_block(self, lead, i, idx):
        return (*lead, idx * self.nblk + i, 0) if self.axis == 0 else (*lead, i, idx)


class _Comm:
    def __init__(self, ins, out_shapes, aliases, n_sems, start, finish, deliver):
        self.ins, self.out_shapes, self.aliases, self.n_sems = list(ins), list(out_shapes), dict(aliases), n_sems
        self.start, self.finish, self.deliver = start, finish, deliver

    def scratch(self):
        return [pltpu.SemaphoreType.DMA((self.n_sems,)), pltpu.SemaphoreType.DMA((self.n_sems,))]


def _comm_call(name, comm):
    n_in = len(comm.ins)

    def body(*refs):
        ins, outs, sems = refs[:n_in], refs[n_in:-2], (*refs[-2:], 0)
        comm.start(ins, outs, sems)
        comm.finish(ins, outs, sems)

    res = pl.pallas_call(
        body, name=name, out_shape=comm.out_shapes, in_specs=_any_specs(n_in),
        out_specs=_any_specs(len(comm.out_shapes)), input_output_aliases=comm.aliases,
        scratch_shapes=comm.scratch())(*comm.ins)
    return comm.deliver(list(res))


def _hosted_call(body, *, name, grid, in_specs, out_specs, out_shape, args, scratch_shapes=(), dims=None, comm=None,
                 prefetch=None, aliases=None):
    aliases = dict(aliases or {})
    in_specs, out_specs, out_shape = list(in_specs), list(out_specs), list(out_shape)
    n_pre = 0 if prefetch is None else 1

    def call(fn, in_specs, out_specs, out_shape, scratch, aliases, dims, args):
        if prefetch is None:
            return pl.pallas_call(
                fn, name=name, grid=grid, in_specs=in_specs, out_specs=out_specs, out_shape=out_shape,
                scratch_shapes=scratch, input_output_aliases=aliases, compiler_params=_params(dims))(*args)
        spec = pltpu.PrefetchScalarGridSpec(num_scalar_prefetch=1, grid=grid, in_specs=in_specs, out_specs=out_specs,
                                            scratch_shapes=scratch)
        return pl.pallas_call(
            fn, name=name, grid_spec=spec, out_shape=out_shape,
            input_output_aliases={i + 1: o for i, o in aliases.items()}, compiler_params=_params(dims))(prefetch, *args)

    if comm is None:
        return list(call(body, in_specs, out_specs, out_shape, list(scratch_shapes), aliases,
                         dims or ("arbitrary",) * len(grid), args))
    n_in, n_out, n_ci, n_co = len(in_specs), len(out_specs), len(comm.ins), len(comm.out_shapes)

    def wrapped(*refs):
        pre, refs = refs[:n_pre], refs[n_pre:]
        ins, cins = refs[:n_in], refs[n_in:n_in + n_ci]
        outs = refs[n_in + n_ci:n_in + n_ci + n_out]
        couts = refs[n_in + n_ci + n_out:n_in + n_ci + n_out + n_co]
        scratch, sems = refs[n_in + n_ci + n_out + n_co:-2], (*refs[-2:], 0)
        first = functools.reduce(jnp.logical_and, [pl.program_id(k) == 0 for k in range(len(grid))])
        last = functools.reduce(jnp.logical_and, [pl.program_id(k) == grid[k] - 1 for k in range(len(grid))])

        @pl.when(first)
        def _():
            comm.start(cins, couts, sems)

        body(*pre, *ins, *outs, *scratch)

        @pl.when(last)
        def _():
            comm.finish(cins, couts, sems)

    res = call(wrapped, in_specs + _any_specs(n_ci), out_specs + _any_specs(n_co), out_shape + comm.out_shapes,
               list(scratch_shapes) + comm.scratch(),
               {**aliases, **{n_in + i: n_out + o for i, o in comm.aliases.items()}},
               ("arbitrary",) * len(grid), (*args, *comm.ins))
    comm.deliver(list(res[n_out:]))
    return list(res[:n_out])


def _gather_comm(bufs, deliver):
    n = len(bufs)
    geoms = [_Geom(*b.shape[1:]) for b in bufs]

    def first(outs, sems):
        x, y, c, chips = _place()
        cps = []
        for a in range(n):
            mine = geoms[a].half_ref(outs[a], (2 * x + y,), c)
            cps += [_remote(mine, mine, sems, 6 * a + j, (cx, cy, c)) for j, (cx, cy) in enumerate(chips)]
        return cps

    def start(ins, outs, sems):
        for cp in first(outs, sems):
            cp.start()

    def finish(ins, outs, sems):
        x, y, c, chips = _place()
        sib = (x, y, 1 - c)
        passed = []
        for a in range(n):
            for j, (cx, cy) in enumerate(chips):
                land = geoms[a].half_ref(outs[a], (2 * cx + cy,), c)
                _remote(land, land, sems, 6 * a + j, sib).wait_recv()
                cp = _remote(land, land, sems, 6 * a + 3 + j, sib)
                cp.start()
                passed.append(cp)
        for a in range(n):
            for j, (cx, cy) in enumerate(chips):
                land = geoms[a].half_ref(outs[a], (2 * cx + cy,), 1 - c)
                _remote(land, land, sems, 6 * a + 3 + j, sib).wait_recv()
        for cp in first(outs, sems) + passed:
            cp.wait_send()

    return _Comm(bufs, [jax.ShapeDtypeStruct(b.shape, b.dtype) for b in bufs], {a: a for a in range(n)}, 6 * n,
                 start, finish, deliver)


def _reduce_pair_comm(grads, deliver):
    n = len(grads)
    geoms = [_Geom(*a.shape[1:]) for a in grads]

    def copies(ins, outs, sems):
        x, y, c, _ = _place()
        return [_remote(geoms[a].half_ref(ins[a], (slice(None),), 1 - c), outs[a], sems, a, (x, y, 1 - c))
                for a in range(n)]

    def start(ins, outs, sems):
        for cp in copies(ins, outs, sems):
            cp.start()

    def finish(ins, outs, sems):
        cps = copies(ins, outs, sems)
        for cp in cps:
            cp.wait_recv()
        for cp in cps:
            cp.wait_send()

    shapes = [jax.ShapeDtypeStruct((N_SHARD, g.h0, g.h1), a.dtype) for a, g in zip(grads, geoms)]
    return _Comm(grads, shapes, {}, n, start, finish, deliver)


def _reduce_chips_comm(parts, deliver):
    n = len(parts)

    def copies(ins, outs, sems):
        x, y, c, chips = _place()
        return [_remote(ins[a].at[2 * cx + cy], outs[a].at[j], sems, 3 * a + j, (cx, cy, c))
                for a in range(n) for j, (cx, cy) in enumerate(chips)]

    def start(ins, outs, sems):
        for cp in copies(ins, outs, sems):
            cp.start()

    def finish(ins, outs, sems):
        cps = copies(ins, outs, sems)
        for cp in cps:
            cp.wait_recv()
        for cp in cps:
            cp.wait_send()

    shapes = [jax.ShapeDtypeStruct((3,) + a.shape[1:], a.dtype) for a in parts]
    return _Comm(parts, shapes, {}, 3 * n, start, finish, deliver)


def _share_pair_comm(bufs, deliver):
    n = len(bufs)
    geoms = [_Geom(*b.shape) for b in bufs]

    def copies(outs, sems):
        x, y, c, _ = _place()
        cps = []
        for a in range(n):
            mine = geoms[a].half_ref(outs[a], (), c)
            cps.append(_remote(mine, mine, sems, a, (x, y, 1 - c)))
        return cps

    def start(ins, outs, sems):
        for cp in copies(outs, sems):
            cp.start()

    def finish(ins, outs, sems):
        x, y, c, _ = _place()
        for a in range(n):
            land = geoms[a].half_ref(outs[a], (), 1 - c)
            _remote(land, land, sems, a, (x, y, 1 - c)).wait_recv()
        for cp in copies(outs, sems):
            cp.wait_send()

    return _Comm(bufs, [jax.ShapeDtypeStruct(b.shape, b.dtype) for b in bufs], {a: a for a in range(n)}, n,
                 start, finish, deliver)


def _small_comm(pack, deliver):
    r, d = pack.shape

    def copies(ins, outs, sems):
        x, y, c, _ = _place()
        cps = []
        for k in range(1, 8):
            peer = (x ^ ((k >> 2) & 1), y ^ ((k >> 1) & 1), c ^ (k & 1))
            cps.append(_remote(ins[0], outs[0].at[4 * x + 2 * y + c], sems, k - 1, peer))
        return cps

    def start(ins, outs, sems):
        for cp in copies(ins, outs, sems):
            cp.start()

    def finish(ins, outs, sems):
        cps = copies(ins, outs, sems)
        for cp in cps:
            cp.wait_recv()
        for cp in cps:
            cp.wait_send()

    return _Comm([pack], [jax.ShapeDtypeStruct((8, r, d), pack.dtype)], {}, 7, start, finish, deliver)


def _join_comms(comms):
    comms = [c for c in comms if c is not None]
    if len(comms) <= 1:
        return comms[0] if comms else None
    ins, out_shapes, aliases, spans, n_sems = [], [], {}, [], 0
    for c in comms:
        aliases.update({len(ins) + i: len(out_shapes) + o for i, o in c.aliases.items()})
        spans.append((len(ins), len(ins) + len(c.ins), len(out_shapes), len(out_shapes) + len(c.out_shapes), n_sems))
        ins += c.ins
        out_shapes += c.out_shapes
        n_sems += c.n_sems

    def run(which):
        def go(all_ins, all_outs, sems):
            for c, (i0, i1, o0, o1, base) in zip(comms, spans):
                getattr(c, which)(all_ins[i0:i1], all_outs[o0:o1], (sems[0], sems[1], sems[2] + base))
        return go

    def deliver(outs):
        for c, (_, _, o0, o1, _) in zip(comms, spans):
            c.deliver(outs[o0:o1])
        return outs

    return _Comm(ins, out_shapes, aliases, n_sems, run("start"), run("finish"), deliver)


def _ew_call(name, fn, ins, out_dtypes):
    shape = ins[0].shape
    cols = shape[-1]
    rows = 1
    for s_ in shape[:-1]:
        rows *= s_
    ins2 = [a.reshape(rows, cols) for a in ins]
    tr = rows
    for t in range(16, min(rows, max(16, (1 << 19) // cols)) + 1, 16):
        if rows % t == 0:
            tr = t
    no = len(out_dtypes)

    def body(*refs):
        outs = fn(*[r[...] for r in refs[:len(ins2)]])
        for ref, val in zip(refs[len(ins2):], outs):
            ref[...] = val.astype(ref.dtype)

    spec = pl.BlockSpec((tr, cols), lambda i: (i, 0))
    res = pl.pallas_call(
        body, name=name, grid=(rows // tr,), in_specs=[spec] * len(ins2), out_specs=[spec] * no,
        out_shape=[jax.ShapeDtypeStruct((rows, cols), dt) for dt in out_dtypes],
        compiler_params=_params(("parallel",)))(*ins2)
    return [r.reshape(shape) for r in res]


def _adamw_math(w, g, m, v):
    m = ADAM_B1 * m + (1.0 - ADAM_B1) * g
    v = ADAM_B2 * v + (1.0 - ADAM_B2) * (g * g)
    m_hat = m / (1.0 - ADAM_B1 ** ADAM_STEP)
    v_hat = v / (1.0 - ADAM_B2 ** ADAM_STEP)
    delta = -ADAM_LR * (m_hat / (jnp.sqrt(v_hat) + ADAM_EPS) + ADAM_WD * w)
    return delta, m, v


def _adamw_call(name, w, g, m, v):
    return _ew_call(name, _adamw_math, [w, g, m, v], [F32, F32, F32])


def _tiled_call(name, fn, place, grid, in_items, out_items, comm=None):
    ni = len(in_items)

    def body(place_ref, *refs):
        vals = fn(*[r[...] for r in refs[:ni]])
        for ref, val in zip(refs[ni:], vals):
            ref[...] = val.astype(ref.dtype)

    return _hosted_call(
        body, name=name, grid=grid, in_specs=[pl.BlockSpec(blk, imap) for _, blk, imap in in_items],
        out_specs=[pl.BlockSpec(blk, imap) for _, _, blk, imap in out_items],
        out_shape=[jax.ShapeDtypeStruct(shp, dt) for shp, dt, _, _ in out_items],
        args=[a for a, _, _ in in_items], prefetch=place, comm=comm)


def _cast_call(name, place, shards, comm=None):
    n0, n1 = shards[0].shape
    tr = _fit_rows(n0, n1, 1.5 * len(shards))
    ins = [(a, (tr, n1), lambda i, p: (i, 0)) for a in shards]
    outs = [((N_SHARD, n0, n1), BF16, (1, tr, n1), lambda i, p: (p[0], i, 0)) for _ in shards]
    return _tiled_call(name, lambda *v: [x[None] for x in v], place, (n0 // tr,), ins, outs, comm)


def _pair_sum_call(name, place, fulls, gots):
    k = len(fulls)
    g = _Geom(*fulls[0].shape[1:], blocks=2.5 * k)
    blk = (1, g.tr, g.h1)
    ins = [(a, blk, lambda s, i, p: g.half_block((s,), i, p[1])) for a in fulls]
    ins += [(a, blk, lambda s, i, p: (s, i, 0)) for a in gots]
    outs = [((N_SHARD, g.h0, g.h1), BF16, blk, lambda s, i, p: (s, i, 0)) for _ in fulls]
    return _tiled_call(name, lambda *v: [v[j] + v[k + j] for j in range(k)], place, (N_SHARD, g.nblk), ins, outs)


def _chip_sum_call(name, place, fulls, gots, recvs, comm=None):
    k = len(fulls)
    g = _Geom(*fulls[0].shape[1:], blocks=4.5 * k)
    blk = (1, g.tr, g.h1)
    ins = [(a, blk, lambda i, p: g.half_block((p[0],), i, p[1])) for a in fulls]
    ins += [(a, blk, lambda i, p: (p[0], i, 0)) for a in gots]
    ins += [(a, (3, g.tr, g.h1), lambda i, p: (0, i, 0)) for a in recvs]
    outs = [((g.n0, g.n1), F32, (g.tr, g.h1), lambda i, p: g.half_block((), i, p[1])) for _ in fulls]

    def fn(*v):
        res = []
        for j in range(k):
            r = v[2 * k + j].astype(F32)
            res.append(v[j][0] + v[k + j][0] + r[0] + r[1] + r[2])
        return res

    return _tiled_call(name, fn, place, (g.nblk,), ins, outs, comm)


def _adamw_group_call(name, ws, gs, ms, vs, comm=None):
    k = len(ws)
    n0, n1 = ws[0].shape
    tr = _fit_rows(n0, n1, 8 * k)
    spec = pl.BlockSpec((tr, n1), lambda i: (i, 0))

    def body(*refs):
        for j in range(k):
            g = refs[k + j][...]
            delta, m, vv = _adamw_math(refs[j][...], g, refs[2 * k + j][...], refs[3 * k + j][...])
            for ref, val in zip(refs[4 * k + 4 * j:4 * k + 4 * j + 4], (g, delta, m, vv)):
                ref[...] = val

    flat = _hosted_call(
        body, name=name, grid=(n0 // tr,), in_specs=[spec] * (4 * k), out_specs=[spec] * (4 * k),
        out_shape=[jax.ShapeDtypeStruct((n0, n1), F32)] * (4 * k), dims=("parallel",),
        args=(*ws, *gs, *ms, *vs), comm=comm)
    return [flat[4 * j:4 * j + 4] for j in range(k)]


def _small_update_call(me, early, own_early, late, own_late, wp, mp, vp):
    nd, r, d = early.shape

    def body(me_ref, e_ref, oe_ref, l_ref, ol_ref, w_ref, m_ref, v_ref, gs_ref, d_ref, nm_ref, nv_ref):
        mine = me_ref[0]

        def total(g_ref, own_ref):
            acc = None
            for k in range(nd):
                part = jnp.where(mine == k, own_ref[...], g_ref[k])
                acc = part if acc is None else acc + part
            return acc

        gs = total(e_ref, oe_ref)
        ls = total(l_ref, ol_ref)
        gs_ref[...] = gs
        first = gs[0:8] + ls[0:8]
        gs_ref[0:8, :] = first
        gs_ref[ROW_META:ROW_META + N_META, :] = gs[ROW_META:ROW_META + N_META] + ls[8:8 + N_META]
        grads = jnp.concatenate([first, gs[8:SMALL_ADAM_ROWS]], axis=0)
        delta, m, v = _adamw_math(w_ref[...], grads, m_ref[...], v_ref[...])
        d_ref[...] = delta
        nm_ref[...] = m
        nv_ref[...] = v

    vm = pl.BlockSpec(memory_space=pltpu.VMEM)
    ashape = jax.ShapeDtypeStruct((SMALL_ADAM_ROWS, d), F32)
    return pl.pallas_call(
        body, name="small_update", in_specs=[pl.BlockSpec(memory_space=pltpu.SMEM)] + [vm] * 7, out_specs=[vm] * 4,
        out_shape=[jax.ShapeDtypeStruct((r, d), F32), ashape, ashape, ashape],
        compiler_params=pltpu.CompilerParams(vmem_limit_bytes=VMEM_LIMIT_BYTES))(
            me, early, own_early, late, own_late, wp, mp, vp)


SMALL_NAMES = ["ffn1_norm", "mix_norm", "ffn2_norm", "final_norm", "q_latent_norm", "kv_latent_norm",
               "q_head_norm", "k_head_norm", "conv_b", "gate_a_b", "gate_x_b", "lru_lambda", "attn_out_norm",
               "lru_out_norm"]
ROW_CONV_W = 14
ROW_GATE_A = 16
ROW_GATE_X = 48
ROW_META = 80
ROW_LOSS = 96


def _row(a):
    flat = a.reshape(1, -1)
    return jnp.pad(flat, ((0, 0), (0, D_MODEL - flat.shape[1])))


def _pack_small(t, rows):
    parts = [_row(t[nm]) for nm in SMALL_NAMES]
    parts.append(t["conv_w"].reshape(2, D_MODEL))
    parts.append(t["gate_a_w"].reshape(32, D_MODEL))
    parts.append(t["gate_x_w"].reshape(32, D_MODEL))
    p = jnp.concatenate(parts, axis=0)
    return jnp.pad(p, ((0, rows - p.shape[0]), (0, 0)))


def _early_pack(g):
    gs = {nm: g.get(nm, jnp.zeros((1, D_MODEL), F32)) for nm in SMALL_NAMES}
    gs["q_head_norm"] = g["q_head_norm"][:, 0:D_QK]
    gs["k_head_norm"] = g["k_head_norm"][:, 0:D_QK]
    for nm in ("conv_b", "gate_a_b", "gate_x_b", "lru_lambda"):
        gs[nm] = g[nm].reshape(1, LRU_W)
    gs["conv_w"] = g["conv_w"].transpose(1, 0, 2).reshape(CONV_K, LRU_W)
    gs["gate_a_w"] = _gate_blocks(g["gate_a_w"])
    gs["gate_x_w"] = _gate_blocks(g["gate_x_w"])
    return jnp.concatenate([_pack_small(gs, ROW_META), jnp.zeros((N_META, D_MODEL), F32), _row(g["loss"][:, 0:1]),
                            jnp.zeros((SMALL_ROWS - ROW_LOSS - 1, D_MODEL), F32)], axis=0)


def _unpack_small(p, like):
    out = {}
    for k, nm in enumerate(SMALL_NAMES):
        out[nm] = p[k, 0:like[nm].size].reshape(like[nm].shape)
    out["gate_a_w"] = p[ROW_GATE_A:ROW_GATE_A + 32].reshape(like["gate_a_w"].shape)
    out["gate_x_w"] = p[ROW_GATE_X:ROW_GATE_X + 32].reshape(like["gate_x_w"].shape)
    return out


def _gate_dense(wg):
    w4 = wg[0].reshape(N_LRU_TILES, 2, 64, 64)
    zero = jnp.zeros((N_LRU_TILES, 64, 64), wg.dtype)
    top = jnp.concatenate([w4[:, 0], zero], axis=2)
    bot = jnp.concatenate([zero, w4[:, 1]], axis=2)
    return jnp.concatenate([top, bot], axis=1).astype(BF16)


def _gate_blocks(dw):
    return jnp.stack([dw[:, 0:64, 0:64], dw[:, 64:128, 64:128]], axis=1).reshape(8, 64, 64)


BIG_NAMES = ["ffn1_w_gate", "ffn1_w_up", "ffn1_w_down", "w_in", "w_uq", "w_uk", "w_uv", "w_out", "ffn2_w_gate",
             "ffn2_w_up", "ffn2_w_down"]
BIG_GROUPS = [["ffn1_w_gate", "ffn1_w_up", "ffn1_w_down", "ffn2_w_gate", "ffn2_w_up", "ffn2_w_down"], ["w_in"],
              ["w_uq"], ["w_uk", "w_uv"], ["w_out"]]
TRANSPOSED = ("ffn1_w_gate", "ffn1_w_up", "ffn2_w_gate", "ffn2_w_up", "w_in", "w_uq")


def _to2d(nm, a):
    return a[0].T if nm in TRANSPOSED else a[0]


def _from2d(nm, a):
    return (a.T if nm in TRANSPOSED else a)[None]


WEIGHT_NAMES = ["meta_tokens", "ffn1_norm", "ffn1_w_gate", "ffn1_w_up", "ffn1_w_down", "mix_norm", "w_in",
                "q_latent_norm", "w_uq", "kv_latent_norm", "w_uk", "w_uv", "q_head_norm", "k_head_norm", "conv_w",
                "conv_b", "gate_a_w", "gate_a_b", "gate_x_w", "gate_x_b", "lru_lambda", "attn_out_norm",
                "lru_out_norm", "w_out", "ffn2_norm", "ffn2_w_gate", "ffn2_w_up", "ffn2_w_down", "final_norm"]


def _weight_from(nm, slots):
    if nm == "w_in":
        win = slots.reshape(IN_WIDTH, D_MODEL)
        return jnp.concatenate([win[0:Z_KR + D_ROPE], jnp.zeros((128 - D_ROPE, D_MODEL), BF16),
                                win[Z_KR + D_ROPE:]], axis=0)
    if nm == "w_uq":
        return jnp.pad(slots, ((0, 0), (0, D_QKP - D_QK), (0, 0))).reshape(HEADS * D_QKP, Q_RANK)
    if nm in ("w_uk", "w_uv"):
        return slots.transpose(1, 0, 2).reshape(KV_RANK, HEADS * D_NOPE)
    if nm == "w_out":
        return slots.reshape(D_MODEL, D_MODEL)
    return slots


def _small_weights(p, small):
    w = {nm: p[nm] for nm in SMALL_NAMES}
    w["q_head_norm"] = jnp.pad(p["q_head_norm"], ((0, 0), (0, D_QKP - D_QK)))
    w["k_head_norm"] = jnp.pad(p["k_head_norm"], ((0, 0), (0, D_QKP - D_QK)))
    w["conv_w"] = small[:, N_META:N_META + 2, :].reshape(N_SHARD, CONV_K, LRU_TILE)
    w["gate_a_w"] = _gate_dense(p["gate_a_w"])
    w["gate_x_w"] = _gate_dense(p["gate_x_w"])
    meta = small[:, 0:N_META, :].transpose(1, 0, 2).reshape(N_META, D_MODEL)
    return w, meta


def _full_weights(p, gathered, small):
    w, meta = _small_weights(p, small)
    w.update({nm: _weight_from(nm, gathered[nm]) for nm in BIG_NAMES})
    return w, meta


def _shard_grad(nm, g):
    if nm == "w_in":
        return jnp.concatenate([g[0:Z_KR + D_ROPE], g[Z_MLA:]], axis=0).reshape(N_SHARD, IN_WIDTH // N_SHARD, D_MODEL)
    if nm == "w_uq":
        return g.reshape(HEADS, D_QKP, Q_RANK)[:, 0:D_QK, :]
    if nm in ("w_uk", "w_uv"):
        return g.reshape(KV_RANK, HEADS, D_NOPE).transpose(1, 0, 2)
    if nm == "w_out":
        return g.reshape(N_SHARD, D_MODEL // N_SHARD, D_MODEL)
    return g


def _shard_grads(g):
    return {nm: _shard_grad(nm, g[nm]) for nm in BIG_NAMES}


GATHER_FIRST = ["ffn1_w_gate"]
GATHER_AT = {"ffn1_gate": ["ffn1_w_up"], "ffn1_upact": ["ffn1_w_down"],
             "ffn1_down": ["w_in", "w_uq", "w_uk", "w_uv", "w_out"], "attn_fwd": ["ffn2_w_down", "ffn2_w_gate"],
             "lru_fwd": ["ffn2_w_up"]}
PAIR_AT = [("ffn2_din", ["ffn2_w_gate", "ffn2_w_up", "ffn2_w_down"]),
           ("mix_din", ["w_out", "w_uq", "w_uk", "w_uv", "w_in"]),
           ("ffn1_dwg", ["ffn1_w_down"]), ("ffn1_dwu", ["ffn1_w_gate"]), ("ffn1_din_a", ["ffn1_w_up"])]
CHIPS_AT = [("attn_bwd", ["ffn2_w_down", "ffn2_w_gate"]), ("mla_prep_bwd", ["ffn2_w_up"]),
            ("ffn1_dact", ["w_out", "w_uq", "w_uk", "w_uv", "w_in"]),
            ("ffn1_dwu", ["ffn1_w_down"]), ("ffn1_din_a", ["ffn1_w_gate"]), ("ffn1_din_b", ["ffn1_w_up"])]
SHARE_EARLY_GROUPS, SHARE_EARLY_AT = 3, "ffn1_dwd"
SMALL_EARLY_AT = "mix_din"


def _same_shape_groups(names):
    return [[nm for nm in grp if nm in names] for grp in BIG_GROUPS if any(nm in names for nm in grp)]


class _Sched:
    def __init__(self, place, w, slots):
        self.place, self.w, self.slots = place, w, slots
        self.g = None
        self.sharded, self.from_pair, self.chip_bf16, self.from_chips = {}, {}, {}, {}
        self.early = self.early_all = None
        self.shared = {}

    def host(self, stage):
        comms = []
        if stage in GATHER_AT:
            comms.append(self.gather(GATHER_AT[stage]))
        comms += [self.chips(names) for at, names in CHIPS_AT if at == stage]
        comms += [self.pair(names) for at, names in PAIR_AT if at == stage]
        if stage == SMALL_EARLY_AT:
            comms.append(self.small_early())
        if stage == SHARE_EARLY_AT:
            comms.append(self.share_early())
        return _join_comms(comms)

    def small_early(self):
        self.early = _early_pack(self.g)

        def deliver(outs):
            self.early_all = outs[0]
            return outs

        return _small_comm(self.early, deliver)

    def gather(self, names):
        def deliver(outs):
            self.w.update({nm: _weight_from(nm, o) for nm, o in zip(names, outs)})
            return outs

        return _gather_comm([self.slots[nm] for nm in names], deliver)

    def pair(self, names):
        self.sharded.update({nm: _shard_grad(nm, self.g[nm]) for nm in names})

        def deliver(outs):
            self.from_pair.update(zip(names, outs))
            for grp in _same_shape_groups(names):
                sums = _pair_sum_call("pair_sum_" + grp[0], self.place, [self.sharded[nm] for nm in grp],
                                      [self.from_pair[nm] for nm in grp])
                self.chip_bf16.update(zip(grp, sums))
            return outs

        return _reduce_pair_comm([self.sharded[nm] for nm in names], deliver)

    def chips(self, names):
        def deliver(outs):
            self.from_chips.update(zip(names, outs))
            return outs

        return _reduce_chips_comm([self.chip_bf16[nm] for nm in names], deliver)

    def chip_sums(self, names, comm=None):
        out = {}
        for grp in _same_shape_groups(names):
            sums = _chip_sum_call("chip_sum_" + grp[0], self.place, [self.sharded[nm] for nm in grp],
                                  [self.from_pair[nm] for nm in grp], [self.from_chips[nm] for nm in grp], comm)
            comm = None
            out.update(zip(grp, sums))
        return out

    def share_early(self):
        names = [nm for at, grp in CHIPS_AT[:SHARE_EARLY_GROUPS] for nm in grp]
        mine = self.chip_sums(names)
        return _share_pair_comm([mine[nm] for nm in names], lambda o: self.shared.update(zip(names, o)))


def kernel(x, meta_tokens, ffn1_norm, ffn1_w_gate, ffn1_w_up, ffn1_w_down, mix_norm, w_in, q_latent_norm, w_uq, kv_latent_norm, w_uk, w_uv, q_head_norm, k_head_norm, conv_w, conv_b, gate_a_w, gate_a_b, gate_x_w, gate_x_b, lru_lambda, attn_out_norm, lru_out_norm, w_out, ffn2_norm, ffn2_w_gate, ffn2_w_up, ffn2_w_down, final_norm, loss_target, m_meta_tokens, m_ffn1_norm, m_ffn1_w_gate, m_ffn1_w_up, m_ffn1_w_down, m_mix_norm, m_w_in, m_q_latent_norm, m_w_uq, m_kv_latent_norm, m_w_uk, m_w_uv, m_q_head_norm, m_k_head_norm, m_conv_w, m_conv_b, m_gate_a_w, m_gate_a_b, m_gate_x_w, m_gate_x_b, m_lru_lambda, m_attn_out_norm, m_lru_out_norm, m_w_out, m_ffn2_norm, m_ffn2_w_gate, m_ffn2_w_up, m_ffn2_w_down, m_final_norm, v_meta_tokens, v_ffn1_norm, v_ffn1_w_gate, v_ffn1_w_up, v_ffn1_w_down, v_mix_norm, v_w_in, v_q_latent_norm, v_w_uq, v_kv_latent_norm, v_w_uk, v_w_uv, v_q_head_norm, v_k_head_norm, v_conv_w, v_conv_b, v_gate_a_w, v_gate_a_b, v_gate_x_w, v_gate_x_b, v_lru_lambda, v_attn_out_norm, v_lru_out_norm, v_w_out, v_ffn2_norm, v_ffn2_w_gate, v_ffn2_w_up, v_ffn2_w_down, v_final_norm):
    args = locals()
    p = {nm: args[nm] for nm in WEIGHT_NAMES}
    mom = {nm: args["m_" + nm] for nm in WEIGHT_NAMES}
    var = {nm: args["v_" + nm] for nm in WEIGHT_NAMES}
    nb, seq, d = x.shape
    lp = CHUNK + seq
    xi, yi, ci = lax.axis_index("x"), lax.axis_index("y"), lax.axis_index("c")
    chip = 2 * xi + yi

    place = jnp.stack([chip, ci]).astype(jnp.int32)
    p2 = {nm: _to2d(nm, p[nm]) for nm in BIG_NAMES}
    m2 = {nm: _to2d(nm, mom[nm]) for nm in BIG_NAMES}
    v2 = {nm: _to2d(nm, var[nm]) for nm in BIG_NAMES}

    slots = {}
    small_shard = jnp.concatenate(
        [meta_tokens, conv_w[0].reshape(2, 2 * LRU_TILE), jnp.zeros((14, 2 * LRU_TILE), F32)], axis=0)
    small_slots = lax.dynamic_update_slice(jnp.zeros((N_SHARD,) + small_shard.shape, F32), small_shard[None],
                                           (chip, 0, 0))
    for grp in BIG_GROUPS:
        for nm, buf in zip(grp, _cast_call("cast_" + grp[0], place, [p2[nm] for nm in grp])):
            slots[nm] = buf
    first = _comm_call("gather_first", _gather_comm([slots[nm] for nm in GATHER_FIRST] + [small_slots], lambda o: o))
    w, meta = _small_weights(p, first[-1])
    w.update({nm: _weight_from(nm, o) for nm, o in zip(GATHER_FIRST, first[:-1])})
    sched = _Sched(place, w, slots)

    h0 = jnp.concatenate(
        [jnp.zeros((nb, PAD_ROWS, d), F32), jnp.broadcast_to(meta[None], (nb, N_META, d)), x], axis=1)
    target = jnp.pad(loss_target, ((0, 0), (CHUNK, 0), (0, 0)))
    loss_part, dh0, g = _local_step(h0.reshape(nb * lp, d), target.reshape(nb * lp, d), w, nb, lp, sched)
    dh0 = dh0.reshape(nb, lp, d)
    grad_x = dh0[:, CHUNK:, :]

    late = jnp.concatenate([g["ffn1_norm"], g["mix_norm"], jnp.zeros((6, D_MODEL), F32),
                            jnp.sum(dh0[:, PAD_ROWS:CHUNK, :], axis=0)], axis=0)
    shared = sched.shared
    rest = [nm for at, names in CHIPS_AT[SHARE_EARLY_GROUPS:] if at is not None for nm in names]
    last = [nm for at, names in CHIPS_AT if at is None for nm in names]
    late_box = {}
    mine = sched.chip_sums(rest, _small_comm(late, lambda o: late_box.update(all=o[0])))
    share = _share_pair_comm([mine[nm] for nm in rest], lambda o: shared.update(zip(rest, o)))
    _comm_call("share_pair", _join_comms([share, sched.chips(last) if last else None]))
    if last:
        mine = sched.chip_sums(last)
        _comm_call("share_last", _share_pair_comm([mine[nm] for nm in last], lambda o: shared.update(zip(last, o))))
    late_all = late_box["all"]
    small_like = {nm: p[nm] for nm in SMALL_NAMES + ["gate_a_w", "gate_x_w"]}

    def pack_w(t):
        tt = {nm: t[nm] for nm in SMALL_NAMES + ["gate_a_w", "gate_x_w"]}
        tt["conv_w"] = jnp.zeros((CONV_K, LRU_W), F32)
        return _pack_small(tt, SMALL_ADAM_ROWS)

    me = (4 * xi + 2 * yi + ci).astype(jnp.int32).reshape(1)
    gsum, dsm, msm, vsm = _small_update_call(me, sched.early_all, sched.early, late_all, late, pack_w(p),
                                             pack_w(mom), pack_w(var))
    grads = _unpack_small(gsum, small_like)
    delta = _unpack_small(dsm, small_like)
    new_m = _unpack_small(msm, small_like)
    new_v = _unpack_small(vsm, small_like)
    loss = gsum[ROW_LOSS, 0]
    gmeta = gsum[ROW_META:ROW_META + N_META].reshape(N_META, N_SHARD, D_MODEL // N_SHARD)
    grads["meta_tokens"] = lax.dynamic_index_in_dim(gmeta, chip, axis=1, keepdims=False)
    gconv = gsum[ROW_CONV_W:ROW_CONV_W + 2].reshape(CONV_K, N_SHARD, LRU_TILE)
    grads["conv_w"] = lax.dynamic_index_in_dim(gconv, chip, axis=1, keepdims=False)[None]
    for nm in ("meta_tokens", "conv_w"):
        delta[nm], new_m[nm], new_v[nm] = _adamw_call("adamw_" + nm, p[nm], grads[nm], mom[nm], var[nm])

    for names in BIG_GROUPS:
        res = _adamw_group_call("adamw_" + names[0], [p2[nm] for nm in names], [shared[nm] for nm in names],
                                [m2[nm] for nm in names], [v2[nm] for nm in names])
        for nm, (gg, dd, mm, vv) in zip(names, res):
            grads[nm], delta[nm], new_m[nm], new_v[nm] = (_from2d(nm, t) for t in (gg, dd, mm, vv))

    return (loss, grad_x, *[grads[nm] for nm in WEIGHT_NAMES], *[delta[nm] for nm in WEIGHT_NAMES],
            *[new_m[nm] for nm in WEIGHT_NAMES], *[new_v[nm] for nm in WEIGHT_NAMES])
```

```python
import functools
import math

import jax
import jax.numpy as jnp
import numpy as np
from jax import lax
from jax.experimental import pallas as pl
from jax.experimental.pallas import tpu as pltpu

F32 = jnp.float32
BF16 = jnp.bfloat16
MESH = pl.DeviceIdType.MESH

D_MODEL = 1024
N_META = 16
CHUNK = 64
PAD_ROWS = CHUNK - N_META
HEADS = 4
D_NOPE = 128
D_ROPE = 64
D_QK = D_NOPE + D_ROPE
D_QKP = 256
D_V = 128
KV_RANK = 256
Q_RANK = 384
MLA_W = HEADS * D_V
LRU_W = 512
LRU_TILE = 128
N_LRU_TILES = LRU_W // LRU_TILE
CONV_K = 4
C_RGLRU = 8.0
ROPE_THETA = 10000.0
D_FF = 2816
N_SHARD = 4
EPS = 1e-6
NEG_INF = -1e30
Z_KR = Q_RANK + KV_RANK
Z_MLA = Z_KR + 128
Z_U = Z_MLA
Z_G = Z_U + LRU_W
Z_W = Z_G + LRU_W
IN_WIDTH = Q_RANK + KV_RANK + D_ROPE + 2 * LRU_W

ADAM_LR = 0.001
ADAM_B1 = 0.9
ADAM_B2 = 0.999
ADAM_EPS = 1e-08
ADAM_WD = 0.01
ADAM_STEP = 10

VMEM_LIMIT_BYTES = 56 * 1024 * 1024
SMALL_ROWS = 104
SMALL_ADAM_ROWS = 80


def _params(sem):
    return pltpu.CompilerParams(dimension_semantics=sem, vmem_limit_bytes=VMEM_LIMIT_BYTES)


def _resident(shape):
    return pl.BlockSpec(tuple(shape), lambda i: (0,) * len(shape), pipeline_mode=pl.Buffered(1))


def _row_tile(rows, target):
    best = 16
    for t in range(16, min(rows, target) + 1, 16):
        if rows % t == 0:
            best = t
    return best


def _col_tile(cols, target):
    best = cols
    for t in range(128, min(cols, target) + 1, 128):
        if cols % t == 0:
            best = t
    return best


def _dot(a, b):
    return jnp.dot(a, b, preferred_element_type=F32)


def _dot_nt(a, b):
    return lax.dot_general(a, b, (((1,), (1,)), ((), ())), preferred_element_type=F32)


def _dot_tn(a, b):
    return lax.dot_general(a, b, (((0,), (0,)), ((), ())), preferred_element_type=F32)


def _rms(x, n):
    return lax.rsqrt(jnp.sum(x * x, axis=-1, keepdims=True) * (1.0 / n) + EPS)


def _rms_bwd(dn, nrm, r, n):
    return r * (dn - nrm * (jnp.sum(dn * nrm, axis=-1, keepdims=True) * (1.0 / n)))


def _gelu(x):
    k = math.sqrt(2.0 / math.pi)
    t = jnp.tanh(k * (x + 0.044715 * x * x * x))
    return 0.5 * x * (1.0 + t), t


def _gelu_grad(x, t):
    k = math.sqrt(2.0 / math.pi)
    return 0.5 * (1.0 + t) + 0.5 * x * (1.0 - t * t) * k * (1.0 + 3.0 * 0.044715 * x * x)


def _sigmoid(x):
    return 0.5 + 0.5 * jnp.tanh(0.5 * x)


def _softplus_neg(lam):
    e = jnp.exp(-jnp.abs(lam))
    log1p = jnp.where(e < 0.01, e * (1.0 - e * (0.5 - e * (1.0 / 3 - e * 0.25))), jnp.log(1.0 + e))
    return jnp.maximum(-lam, 0.0) + log1p


def _rope(t, c, s1, s2):
    return t * c + pltpu.roll(t, 96, 1) * s1 + pltpu.roll(t, 32, 1) * s2


def _rope_t(d, c, s1, s2):
    return d * c + pltpu.roll(d * s1, 32, 1) + pltpu.roll(d * s2, 96, 1)


def _rope_tables(lp):
    pos = (np.arange(lp, dtype=np.int32) - PAD_ROWS).astype(np.float32)
    inv_freq = (ROPE_THETA ** (-np.arange(0, D_ROPE // 2, dtype=np.float32) / (D_ROPE // 2))).astype(np.float32)
    ang = (pos[:, None] * inv_freq[None, :]).astype(np.float32).astype(np.float64)
    cos, sin = np.cos(ang).astype(np.float32), np.sin(ang).astype(np.float32)
    z = np.zeros_like(cos)
    return (jnp.asarray(np.concatenate([cos, cos, z, z], 1)), jnp.asarray(np.concatenate([-sin, z, z, z], 1)),
            jnp.asarray(np.concatenate([z, sin, z, z], 1)))


def _rmsnorm_call(name, h, g, tm):
    rows, d = h.shape

    def body(h_ref, g_ref, o_ref):
        x = h_ref[...]
        o_ref[...] = (x * _rms(x, d) * g_ref[...]).astype(BF16)

    return pl.pallas_call(
        body, name=name, grid=(rows // tm,),
        in_specs=[pl.BlockSpec((tm, d), lambda i: (i, 0)), pl.BlockSpec((1, d), lambda i: (0, 0))],
        out_specs=pl.BlockSpec((tm, d), lambda i: (i, 0)),
        out_shape=jax.ShapeDtypeStruct((rows, d), BF16),
        compiler_params=_params(("parallel",)))(h, g)


def _ffn_up_call(name, u, wg, wu, tm, comm=None):
    rows, d = u.shape
    ns, fs, _ = wg.shape

    def body(u_ref, wg_ref, wu_ref, g_ref, p_ref, a_ref):
        uu = u_ref[...]
        g = _dot_nt(uu, wg_ref[0])
        p = _dot_nt(uu, wu_ref[0])
        g_ref[0] = g.astype(BF16)
        p_ref[0] = p.astype(BF16)
        a_ref[0] = (g * jax.nn.sigmoid(g) * p).astype(BF16)

    wspec = pl.BlockSpec((1, fs, d), lambda s, i: (s, 0, 0))
    ospec = pl.BlockSpec((1, tm, fs), lambda s, i: (s, i, 0))
    oshape = jax.ShapeDtypeStruct((ns, rows, fs), BF16)
    return _hosted_call(
        body, name=name, grid=(ns, rows // tm),
        in_specs=[pl.BlockSpec((tm, d), lambda s, i: (i, 0)), wspec, wspec],
        out_specs=[ospec, ospec, ospec], out_shape=[oshape, oshape, oshape],
        dims=("parallel", "parallel"), args=(u, wg, wu), comm=comm)


def _ffn_gate_call(name, u, wg, tm, comm=None):
    rows, d = u.shape
    ns, fs, _ = wg.shape

    def body(u_ref, wg_ref, g_ref):
        g_ref[0] = _dot_nt(u_ref[...], wg_ref[0]).astype(BF16)

    return _hosted_call(
        body, name=name, grid=(ns, rows // tm),
        in_specs=[pl.BlockSpec((tm, d), lambda s, i: (i, 0)), pl.BlockSpec((1, fs, d), lambda s, i: (s, 0, 0))],
        out_specs=[pl.BlockSpec((1, tm, fs), lambda s, i: (s, i, 0))],
        out_shape=[jax.ShapeDtypeStruct((ns, rows, fs), BF16)],
        dims=("parallel", "parallel"), args=(u, wg), comm=comm)[0]


def _ffn_upact_call(name, u, wu, gate, tm, comm=None):
    rows, d = u.shape
    ns, fs, _ = wu.shape

    def body(u_ref, wu_ref, g_ref, p_ref, a_ref):
        p = _dot_nt(u_ref[...], wu_ref[0])
        g = g_ref[0].astype(F32)
        p_ref[0] = p.astype(BF16)
        a_ref[0] = (g * jax.nn.sigmoid(g) * p).astype(BF16)

    ospec = pl.BlockSpec((1, tm, fs), lambda s, i: (s, i, 0))
    oshape = jax.ShapeDtypeStruct((ns, rows, fs), BF16)
    return _hosted_call(
        body, name=name, grid=(ns, rows // tm),
        in_specs=[pl.BlockSpec((tm, d), lambda s, i: (i, 0)), pl.BlockSpec((1, fs, d), lambda s, i: (s, 0, 0)), ospec],
        out_specs=[ospec, ospec], out_shape=[oshape, oshape],
        dims=("parallel", "parallel"), args=(u, wu, gate), comm=comm)


def _loss_tail(x, g, t, row, d):
    r = _rms(x, d)
    n = x * r
    err = jnp.where(row >= CHUNK, n * g - t, 0.0)
    dout = err * (1.0 / d)
    dh = _rms_bwd(dout * g, n, r, d)
    dg = jnp.sum(dout * n, axis=0, keepdims=True)
    part = jnp.sum(jnp.sum(err * err, axis=1, keepdims=True), axis=0, keepdims=True) * (0.5 / d)
    return dh, dg, jnp.broadcast_to(part, (1, 128))


def _ffn_down_call(name, a, wd, h, tm, comm=None, next_gain=None):
    rows, d = h.shape
    ns, _, fs = a.shape
    more = next_gain is not None

    def body(a_ref, wd_ref, h_ref, *rest):
        acc = h_ref[...]
        for s in range(ns):
            acc = acc + 0.5 * _dot(a_ref[s], wd_ref[s])
        rest[-2 if more else -1][...] = acc
        if more:
            rest[-1][...] = (acc * _rms(acc, d) * rest[0][...]).astype(BF16)

    full = pl.BlockSpec((tm, d), lambda i: (i, 0))
    res = _hosted_call(
        body, name=name, grid=(rows // tm,),
        in_specs=[pl.BlockSpec((ns, tm, fs), lambda i: (0, i, 0)), _resident((ns, fs, d)), full]
        + ([pl.BlockSpec((1, d), lambda i: (0, 0))] if more else []),
        out_specs=[full] * (2 if more else 1),
        out_shape=[jax.ShapeDtypeStruct((rows, d), F32)] + ([jax.ShapeDtypeStruct((rows, d), BF16)] if more else []),
        dims=("parallel",), args=(a, wd, h) + ((next_gain,) if more else ()), comm=comm)
    return res if more else res[0]


def _ffn_down_loss_call(name, a, wd, h, g, target, lp, tm):
    rows, d = h.shape
    ns, _, fs = a.shape
    tpe = lp // tm

    def body(a_ref, wd_ref, h_ref, g_ref, t_ref, dh_ref, dhb_ref, dg_ref, loss_ref):
        i = pl.program_id(0)
        acc = h_ref[...]
        for s in range(ns):
            acc = acc + 0.5 * _dot(a_ref[s], wd_ref[s])
        row = (i % tpe) * tm + lax.broadcasted_iota(jnp.int32, (tm, 1), 0)
        dh, dg, loss = _loss_tail(acc, g_ref[...], t_ref[...], row, d)
        dh_ref[...] = dh
        dhb_ref[...] = dh.astype(BF16)

        @pl.when(i == 0)
        def _():
            dg_ref[...] = dg
            loss_ref[...] = loss

        @pl.when(i != 0)
        def _():
            dg_ref[...] += dg
            loss_ref[...] += loss

    full = pl.BlockSpec((tm, d), lambda i: (i, 0))
    gspec = pl.BlockSpec((1, d), lambda i: (0, 0))
    return _hosted_call(
        body, name=name, grid=(rows // tm,),
        in_specs=[pl.BlockSpec((ns, tm, fs), lambda i: (0, i, 0)), _resident((ns, fs, d)), full, gspec, full],
        out_specs=[full, full, gspec, pl.BlockSpec((1, 128), lambda i: (0, 0))],
        out_shape=[jax.ShapeDtypeStruct((rows, d), F32), jax.ShapeDtypeStruct((rows, d), BF16),
                   jax.ShapeDtypeStruct((1, d), F32), jax.ShapeDtypeStruct((1, 128), F32)],
        args=(a, wd, h, g, target))


def _mm_call(name, a, bt, tm, out_dtype):
    rows, k = a.shape
    n = bt.shape[0]

    def body(a_ref, b_ref, o_ref):
        o_ref[...] = _dot_nt(a_ref[...], b_ref[...]).astype(out_dtype)

    return pl.pallas_call(
        body, name=name, grid=(rows // tm,),
        in_specs=[pl.BlockSpec((tm, k), lambda i: (i, 0)), pl.BlockSpec((n, k), lambda i: (0, 0))],
        out_specs=pl.BlockSpec((tm, n), lambda i: (i, 0)),
        out_shape=jax.ShapeDtypeStruct((rows, n), out_dtype),
        compiler_params=_params(("parallel",)))(a, bt)


def _mla_heads(z, gql, gkvl, wuq, wuk, wuv):
    cq = z[:, 0:Q_RANK]
    ckv = z[:, Q_RANK:Z_KR]
    kr = z[:, Z_KR:Z_MLA]
    rq = _rms(cq, Q_RANK)
    nq = cq * rq
    cqn = (nq * gql).astype(BF16)
    rkv = _rms(ckv, KV_RANK)
    nkv = ckv * rkv
    ckvn = (nkv * gkvl).astype(BF16)
    qraw = _dot_nt(cqn, wuq)
    knope = _dot(ckvn, wuk)
    v = _dot(ckvn, wuv)
    skr = jnp.sum(kr * kr, axis=-1, keepdims=True)
    heads = []
    for hd in range(HEADS):
        qh = qraw[:, hd * D_QKP:(hd + 1) * D_QKP]
        rqh = lax.rsqrt(jnp.sum(qh * qh, axis=-1, keepdims=True) * (1.0 / D_QK) + EPS)
        kn = knope[:, hd * D_NOPE:(hd + 1) * D_NOPE]
        rkh = lax.rsqrt((jnp.sum(kn * kn, axis=-1, keepdims=True) + skr) * (1.0 / D_QK) + EPS)
        heads.append((qh * rqh, rqh, kn * rkh, kr * rkh, rkh))
    return dict(rq=rq, nq=nq, cqn=cqn, rkv=rkv, nkv=nkv, ckvn=ckvn, v=v, heads=heads)


def _mla_prep_call(z, gql, gkvl, gqh, gkh, wuq, wuk, wuv, tabs, lp, tm):
    rows = z.shape[0]
    tpe = lp // tm

    def body(z_ref, gql_ref, gkvl_ref, gqh_ref, gkh_ref, wuq_ref, wuk_ref, wuv_ref, c_ref, s1_ref, s2_ref,
             q_ref, k_ref, v_ref, cqn_ref, ckvn_ref):
        m = _mla_heads(z_ref[...], gql_ref[...], gkvl_ref[...], wuq_ref[...], wuk_ref[...], wuv_ref[...])
        c, s1, s2 = c_ref[...], s1_ref[...], s2_ref[...]
        gq, gk = gqh_ref[...], gkh_ref[...]
        row = (pl.program_id(0) % tpe) * tm + lax.broadcasted_iota(jnp.int32, (tm, 1), 0)
        spare = (lax.broadcasted_iota(jnp.int32, (1, D_QKP - D_NOPE), 1) == D_ROPE).astype(F32)
        kmask = jnp.where(row < PAD_ROWS, NEG_INF * math.sqrt(D_QK), 0.0) * spare
        for hd in range(HEADS):
            qn, _, knn, krn, _ = m["heads"][hd]
            qg = qn * gq
            q_ref[hd, :, 0:D_NOPE] = qg[:, 0:D_NOPE].astype(BF16)
            q_ref[hd, :, D_NOPE:D_QKP] = (_rope(qg[:, D_NOPE:D_QKP], c, s1, s2) + spare).astype(BF16)
            k_ref[hd, :, 0:D_NOPE] = (knn * gk[:, 0:D_NOPE]).astype(BF16)
            k_ref[hd, :, D_NOPE:D_QKP] = (_rope(krn * gk[:, D_NOPE:D_QKP], c, s1, s2) + kmask).astype(BF16)
            v_ref[hd] = m["v"][:, hd * D_V:(hd + 1) * D_V].astype(BF16)
        cqn_ref[...] = m["cqn"]
        ckvn_ref[...] = m["ckvn"]

    def const(shape):
        return pl.BlockSpec(shape, lambda i: tuple(0 for _ in shape))

    tab = pl.BlockSpec((tm, 128), lambda i: (i % tpe, 0))
    return pl.pallas_call(
        body, name="mla_prep", grid=(rows // tm,),
        in_specs=[pl.BlockSpec((tm, Z_MLA), lambda i: (i, 0)), const((1, Q_RANK)), const((1, KV_RANK)),
                  const((1, D_QKP)), const((1, D_QKP)), const((HEADS * D_QKP, Q_RANK)),
                  const((KV_RANK, HEADS * D_NOPE)), const((KV_RANK, HEADS * D_V)), tab, tab, tab],
        out_specs=[pl.BlockSpec((HEADS, tm, D_QKP), lambda i: (0, i, 0)),
                   pl.BlockSpec((HEADS, tm, D_QKP), lambda i: (0, i, 0)),
                   pl.BlockSpec((HEADS, tm, D_V), lambda i: (0, i, 0)),
                   pl.BlockSpec((tm, Q_RANK), lambda i: (i, 0)),
                   pl.BlockSpec((tm, KV_RANK), lambda i: (i, 0))],
        out_shape=[jax.ShapeDtypeStruct((HEADS, rows, D_QKP), BF16),
                   jax.ShapeDtypeStruct((HEADS, rows, D_QKP), BF16),
                   jax.ShapeDtypeStruct((HEADS, rows, D_V), BF16),
                   jax.ShapeDtypeStruct((rows, Q_RANK), BF16),
                   jax.ShapeDtypeStruct((rows, KV_RANK), BF16)],
        compiler_params=_params(("parallel",)))(z, gql, gkvl, gqh, gkh, wuq, wuk, wuv, *tabs)


Q_BLOCK_ROWS = 528
Q_BLOCK_ROWS_BWD = 192


def _q_block(lp):
    return _row_tile(lp, Q_BLOCK_ROWS)


def _key_end(ext, lp):
    return min(lp, -(-ext // CHUNK) * CHUNK)


def _diag_bias(qb, j0, nk):
    shift = CHUNK.bit_length() - 1
    r = jnp.right_shift(j0 + lax.broadcasted_iota(jnp.int32, (qb, nk), 0), shift)
    c = jnp.right_shift(j0 + lax.broadcasted_iota(jnp.int32, (qb, nk), 1), shift)
    return jnp.where(c <= r, 0.0, NEG_INF)


def _attn_fwd_call(q, k, v, nb, lp, comm=None):
    rows = nb * lp
    qb = _q_block(lp)
    scale = 1.0 / math.sqrt(D_QK)

    def body(q_ref, k_ref, v_ref, o_ref, lse_ref):
        for j in range(lp // qb):
            j0, ext = j * qb, (j + 1) * qb
            kend = _key_end(ext, lp)
            qj = q_ref[0, j0:ext, :]
            sd = _dot_nt(qj, k_ref[0, j0:kend, :]) * scale + _diag_bias(qb, j0, kend - j0)
            mx = jnp.max(sd, axis=-1, keepdims=True)
            if j > 0:
                so = _dot_nt(qj, k_ref[0, 0:j0, :]) * scale
                mx = jnp.maximum(mx, jnp.max(so, axis=-1, keepdims=True))
            pd = jnp.exp(sd - mx)
            l = jnp.sum(pd, axis=-1, keepdims=True)
            o = _dot(pd.astype(BF16), v_ref[0, j0:kend, :])
            if j > 0:
                po = jnp.exp(so - mx)
                l = l + jnp.sum(po, axis=-1, keepdims=True)
                o = o + _dot(po.astype(BF16), v_ref[0, 0:j0, :])
            o_ref[j0:ext, :] = o / l
            lse_ref[0, j0:ext, :] = mx + jnp.log(l)

    return _hosted_call(
        body, name="attn_fwd", grid=(nb, HEADS),
        in_specs=[pl.BlockSpec((1, lp, D_QKP), lambda b, h: (h, b, 0)),
                  pl.BlockSpec((1, lp, D_QKP), lambda b, h: (h, b, 0)),
                  pl.BlockSpec((1, lp, D_V), lambda b, h: (h, b, 0))],
        out_specs=[pl.BlockSpec((lp, D_V), lambda b, h: (b, h)),
                   pl.BlockSpec((1, lp, 1), lambda b, h: (h, b, 0))],
        out_shape=[jax.ShapeDtypeStruct((rows, MLA_W), F32),
                   jax.ShapeDtypeStruct((HEADS, rows, 1), F32)],
        dims=("parallel", "parallel"), args=(q, k, v), comm=comm)


def _attn_bwd_call(q, k, v, o, lse, do, nb, lp, comm=None):
    rows = nb * lp
    qb = _row_tile(lp, Q_BLOCK_ROWS_BWD)
    scale = 1.0 / math.sqrt(D_QK)

    def body(q_ref, k_ref, v_ref, o_ref, lse_ref, do_ref, dq_ref, dk_ref, dv_ref, dk_acc, dv_acc):
        dk_acc[...] = jnp.zeros_like(dk_acc)
        dv_acc[...] = jnp.zeros_like(dv_acc)
        shift = CHUNK.bit_length() - 1
        for j in range(lp // qb):
            j0, ext = j * qb, (j + 1) * qb
            kend = _key_end(ext, lp)
            qj = q_ref[0, j0:ext, :]
            doj = do_ref[j0:ext, :]
            delta = jnp.sum(doj * o_ref[j0:ext, :], axis=-1, keepdims=True)
            dob = doj.astype(BF16)
            kk = k_ref[0, 0:kend, :]
            qchunk = jnp.right_shift(j0 + lax.broadcasted_iota(jnp.int32, (qb, kend), 0), shift)
            kchunk = jnp.right_shift(lax.broadcasted_iota(jnp.int32, (qb, kend), 1), shift)
            s = _dot_nt(qj, kk) * scale - lse_ref[0, j0:ext, :]
            p = jnp.where(kchunk <= qchunk, jnp.exp(s), 0.0)
            dv_acc[0:kend, :] += _dot_tn(p.astype(BF16), dob)
            dp = _dot_nt(dob, v_ref[0, 0:kend, :])
            ds = (p * (dp - delta) * scale).astype(BF16)
            dq_ref[0, j0:ext, :] = _dot(ds, kk).astype(BF16)
            dk_acc[0:kend, :] += _dot_tn(ds, qj)
        dk_ref[0] = dk_acc[...].astype(BF16)
        dv_ref[0] = dv_acc[...].astype(BF16)

    qspec = pl.BlockSpec((1, lp, D_QKP), lambda b, h: (h, b, 0))
    vspec = pl.BlockSpec((1, lp, D_V), lambda b, h: (h, b, 0))
    ospec = pl.BlockSpec((lp, D_V), lambda b, h: (b, h))
    return _hosted_call(
        body, name="attn_bwd", grid=(nb, HEADS),
        in_specs=[qspec, qspec, vspec, ospec, pl.BlockSpec((1, lp, 1), lambda b, h: (h, b, 0)), ospec],
        out_specs=[qspec, qspec, vspec],
        out_shape=[jax.ShapeDtypeStruct((HEADS, rows, D_QKP), BF16),
                   jax.ShapeDtypeStruct((HEADS, rows, D_QKP), BF16),
                   jax.ShapeDtypeStruct((HEADS, rows, D_V), BF16)],
        scratch_shapes=[pltpu.VMEM((lp, D_QKP), F32), pltpu.VMEM((lp, D_V), F32)],
        dims=("parallel", "parallel"), args=(q, k, v, o, lse, do), comm=comm)


def _lru_gates(u, cw, cb, wa, ba, wx, bx, lam, lp):
    xc = (cw[3:4, :] * u + cw[2:3, :] * pltpu.roll(u, 1, 0) + cw[1:2, :] * pltpu.roll(u, 2, 0)
          + cw[0:1, :] * pltpu.roll(u, 3, 0) + cb)
    xcb = xc.astype(BF16)
    r = _sigmoid(_dot(xcb, wa) + ba)
    i = _sigmoid(_dot(xcb, wx) + bx)
    sp = _softplus_neg(lam)
    la = -C_RGLRU * r * sp
    a = jnp.exp(la)
    x2 = 2.0 * la
    e2 = a * a
    m2 = jnp.maximum(jnp.where(x2 > -0.01, -x2 * (1.0 + 0.5 * x2), 1.0 - e2), 1e-30)
    rs = lax.rsqrt(m2)
    row = lax.broadcasted_iota(jnp.int32, (lp, LRU_TILE), 0)
    first = row == PAD_ROWS
    valid = row >= PAD_ROWS
    mult_eff = jnp.where(first, 1.0, m2 * rs)
    return dict(xc=xc, xcb=xcb, r=r, i=i, sp=sp, a=a, e2=e2, rs=rs, mult_eff=mult_eff, first=first, valid=valid)


def _scan_rows(a, b, a_s, b_s, out_ref, lp, reverse):
    sub = lax.broadcasted_iota(jnp.int32, (lp, LRU_TILE), 0) & 7
    for dist in (1, 2, 4):
        shift = lp - dist if reverse else dist
        keep = (sub + dist <= 7) if reverse else (sub >= dist)
        a_sh = pltpu.roll(a, shift, 0)
        b_sh = pltpu.roll(b, shift, 0)
        b = jnp.where(keep, a * b_sh + b, b)
        a = jnp.where(keep, a * a_sh, a)
    a_s[...] = a
    b_s[...] = b
    n_groups = lp // 8
    edge = 0 if reverse else 7

    def group(gi, carry):
        r0 = pl.multiple_of(((n_groups - 1 - gi) if reverse else gi) * 8, 8)
        a8 = a_s[pl.ds(r0, 8), :]
        b8 = b_s[pl.ds(r0, 8), :]
        out_ref[pl.ds(r0, 8), :] = a8 * carry + b8
        return a8[edge:edge + 1, :] * carry + b8[edge:edge + 1, :]

    lax.fori_loop(0, n_groups, group, jnp.zeros((1, LRU_TILE), F32), unroll=4)


def _lru_specs(lp):
    seq = lambda col0: pl.BlockSpec((lp, LRU_TILE), lambda t, b: (b, col0 + t))
    cw = pl.BlockSpec((1, CONV_K, LRU_TILE), lambda t, b: (t, 0, 0))
    vec = pl.BlockSpec((1, LRU_TILE), lambda t, b: (0, t))
    mat = pl.BlockSpec((1, LRU_TILE, LRU_TILE), lambda t, b: (t, 0, 0))
    return seq, cw, vec, mat


def _lru_fwd_call(z, cw, cb, wa, ba, wx, bx, lam, nb, lp, comm=None):
    rows = nb * lp
    seq, cwspec, vec, mat = _lru_specs(lp)

    def body(u_ref, g_ref, cw_ref, cb_ref, wa_ref, ba_ref, wx_ref, bx_ref, lam_ref, y_ref, hs_ref, a_s, b_s):
        m = _lru_gates(u_ref[...], cw_ref[0], cb_ref[...], wa_ref[0], ba_ref[...], wx_ref[0], bx_ref[...],
                       lam_ref[...], lp)
        a = jnp.where(m["valid"], m["a"], 0.0)
        b = jnp.where(m["valid"], m["mult_eff"] * (m["i"] * m["xc"]), 0.0)
        _scan_rows(a, b, a_s, b_s, hs_ref, lp, reverse=False)
        gl, _ = _gelu(g_ref[...])
        y_ref[...] = hs_ref[...] * gl

    oshape = jax.ShapeDtypeStruct((rows, LRU_W), F32)
    return _hosted_call(
        body, name="lru_fwd", grid=(N_LRU_TILES, nb),
        in_specs=[seq(Z_U // LRU_TILE), seq(Z_G // LRU_TILE), cwspec, vec, mat, vec, mat, vec, vec],
        out_specs=[seq(0), seq(0)], out_shape=[oshape, oshape],
        scratch_shapes=[pltpu.VMEM((lp, LRU_TILE), F32), pltpu.VMEM((lp, LRU_TILE), F32)],
        dims=("parallel", "parallel"), args=(z, z, cw, cb, wa, ba, wx, bx, lam), comm=comm)


def _lru_bwd_call(z, hs, dy, cw, cb, wa, ba, wx, bx, lam, nb, lp, comm=None):
    rows = nb * lp
    seq, cwspec, vec, mat = _lru_specs(lp)

    def body(u_ref, g_ref, hs_ref, dy_ref, cw_ref, cb_ref, wa_ref, ba_ref, wx_ref, bx_ref, lam_ref,
             du_ref, dg_ref, dcw_ref, dcb_ref, dwa_ref, dba_ref, dwx_ref, dbx_ref, dlam_ref, a_s, b_s, d_s):
        b_idx = pl.program_id(1)
        u = u_ref[...]
        cw = cw_ref[0]
        wa, wx = wa_ref[0], wx_ref[0]
        lam = lam_ref[...]
        m = _lru_gates(u, cw, cb_ref[...], wa, ba_ref[...], wx, bx_ref[...], lam, lp)
        gate = g_ref[...]
        gl, th = _gelu(gate)
        dy = dy_ref[...]
        hs = hs_ref[...]
        dg_ref[...] = (dy * hs * _gelu_grad(gate, th)).astype(BF16)
        a_eff = jnp.where(m["valid"], m["a"], 0.0)
        _scan_rows(pltpu.roll(a_eff, lp - 1, 0), dy * gl, a_s, b_s, d_s, lp, reverse=True)
        ds = d_s[...]
        xc, r, i = m["xc"], m["r"], m["i"]
        row = lax.broadcasted_iota(jnp.int32, (lp, LRU_TILE), 0)
        da = ds * jnp.where(row >= 1, pltpu.roll(hs, 1, 0), 0.0)
        db = jnp.where(m["valid"], ds, 0.0)
        di = db * m["mult_eff"] * xc
        dxc = db * m["mult_eff"] * i
        live = m["valid"] & jnp.logical_not(m["first"])
        dm = jnp.where(live, db * i * xc, 0.0)
        dla = da * m["a"] - dm * (m["e2"] * m["rs"])
        dr = dla * (-C_RGLRU * m["sp"])
        dsp = jnp.sum(dla * (-C_RGLRU * r), axis=0, keepdims=True)
        dpr = (dr * r * (1.0 - r))
        dpi = (di * i * (1.0 - i))
        dprb, dpib = dpr.astype(BF16), dpi.astype(BF16)
        dxc = dxc + _dot_nt(dprb, wa) + _dot_nt(dpib, wx)
        du = (cw[3:4, :] * dxc + cw[2:3, :] * pltpu.roll(dxc, lp - 1, 0) + cw[1:2, :] * pltpu.roll(dxc, lp - 2, 0)
              + cw[0:1, :] * pltpu.roll(dxc, lp - 3, 0))
        du_ref[...] = jnp.where(m["valid"], du, 0.0).astype(BF16)
        tap = lax.broadcasted_iota(jnp.int32, (CONV_K, LRU_TILE), 0)
        dcw = jnp.zeros((CONV_K, LRU_TILE), F32)
        for kk in range(CONV_K):
            shifted = u if kk == CONV_K - 1 else pltpu.roll(u, CONV_K - 1 - kk, 0)
            dcw = jnp.where(tap == kk, jnp.sum(dxc * shifted, axis=0, keepdims=True), dcw)
        parts = [(dcw_ref, dcw[None]), (dcb_ref, jnp.sum(dxc, axis=0, keepdims=True)[None]),
                 (dwa_ref, _dot_tn(m["xcb"], dprb)[None]), (dba_ref, jnp.sum(dpr, axis=0, keepdims=True)[None]),
                 (dwx_ref, _dot_tn(m["xcb"], dpib)[None]), (dbx_ref, jnp.sum(dpi, axis=0, keepdims=True)[None]),
                 (dlam_ref, (dsp * (-jax.nn.sigmoid(-lam)))[None])]

        @pl.when(b_idx == 0)
        def _():
            for ref, val in parts:
                ref[...] = val

        @pl.when(b_idx != 0)
        def _():
            for ref, val in parts:
                ref[...] += val

    bshape = jax.ShapeDtypeStruct((rows, LRU_W), BF16)
    vec3 = pl.BlockSpec((1, 1, LRU_TILE), lambda t, b: (t, 0, 0))
    vshape = jax.ShapeDtypeStruct((N_LRU_TILES, 1, LRU_TILE), F32)
    mshape = jax.ShapeDtypeStruct((N_LRU_TILES, LRU_TILE, LRU_TILE), F32)
    return _hosted_call(
        body, name="lru_bwd", grid=(N_LRU_TILES, nb),
        in_specs=[seq(Z_U // LRU_TILE), seq(Z_G // LRU_TILE), seq(0), seq(0), cwspec, vec, mat, vec, mat, vec, vec],
        out_specs=[seq(0), seq(0), cwspec, vec3, mat, vec3, mat, vec3, vec3],
        out_shape=[bshape, bshape, jax.ShapeDtypeStruct((N_LRU_TILES, CONV_K, LRU_TILE), F32), vshape, mshape,
                   vshape, mshape, vshape, vshape],
        scratch_shapes=[pltpu.VMEM((lp, LRU_TILE), F32)] * 3,
        dims=("parallel", "arbitrary"), args=(z, z, hs, dy, cw, cb, wa, ba, wx, bx, lam), comm=comm)


def _mix_out_call(ya, yl, ga, gl, wout, h, next_gain, tm):
    rows, d = h.shape

    def body(ya_ref, yl_ref, ga_ref, gl_ref, w_ref, h_ref, ng_ref, y_ref, o_ref, u_ref):
        a = ya_ref[...]
        l = yl_ref[...]
        an = (a * _rms(a, MLA_W) * ga_ref[...]).astype(BF16)
        ln = (l * _rms(l, LRU_W) * gl_ref[...]).astype(BF16)
        y_ref[:, 0:MLA_W] = an
        y_ref[:, MLA_W:MLA_W + LRU_W] = ln
        out = h_ref[...] + _dot(an, w_ref[0:MLA_W, :]) + _dot(ln, w_ref[MLA_W:MLA_W + LRU_W, :])
        o_ref[...] = out
        u_ref[...] = (out * _rms(out, d) * ng_ref[...]).astype(BF16)

    half = pl.BlockSpec((tm, MLA_W), lambda i: (i, 0))
    g = pl.BlockSpec((1, MLA_W), lambda i: (0, 0))
    full = pl.BlockSpec((tm, d), lambda i: (i, 0))
    return pl.pallas_call(
        body, name="mix_out", grid=(rows // tm,),
        in_specs=[half, half, g, g, pl.BlockSpec((MLA_W + LRU_W, d), lambda i: (0, 0)), full,
                  pl.BlockSpec((1, d), lambda i: (0, 0))],
        out_specs=[full, full, full],
        out_shape=[jax.ShapeDtypeStruct((rows, MLA_W + LRU_W), BF16), jax.ShapeDtypeStruct((rows, d), F32),
                   jax.ShapeDtypeStruct((rows, d), BF16)],
        compiler_params=_params(("parallel",)))(ya, yl, ga, gl, wout, h, next_gain)


def _mix_out_bwd_call(dhb, wout, ya, yl, ga, gl, tm):
    rows = ya.shape[0]
    d = dhb.shape[1]

    def body(dh_ref, w_ref, ya_ref, yl_ref, ga_ref, gl_ref, dya_ref, dyl_ref, dga_ref, dgl_ref):
        dy = _dot_nt(dh_ref[...], w_ref[...])
        outs = []
        for val, g_ref, lo, out_ref in ((ya_ref[...], ga_ref, 0, dya_ref), (yl_ref[...], gl_ref, MLA_W, dyl_ref)):
            r = _rms(val, MLA_W)
            n = val * r
            dyn = dy[:, lo:lo + MLA_W]
            out_ref[...] = _rms_bwd(dyn * g_ref[...], n, r, MLA_W)
            outs.append(jnp.sum(dyn * n, axis=0, keepdims=True))

        @pl.when(pl.program_id(0) == 0)
        def _():
            dga_ref[...] = outs[0]
            dgl_ref[...] = outs[1]

        @pl.when(pl.program_id(0) != 0)
        def _():
            dga_ref[...] += outs[0]
            dgl_ref[...] += outs[1]

    half = pl.BlockSpec((tm, MLA_W), lambda i: (i, 0))
    g = pl.BlockSpec((1, MLA_W), lambda i: (0, 0))
    return pl.pallas_call(
        body, name="mix_out_bwd", grid=(rows // tm,),
        in_specs=[pl.BlockSpec((tm, d), lambda i: (i, 0)), pl.BlockSpec((MLA_W + LRU_W, d), lambda i: (0, 0)),
                  half, half, g, g],
        out_specs=[half, half, g, g],
        out_shape=[jax.ShapeDtypeStruct((rows, MLA_W), F32), jax.ShapeDtypeStruct((rows, LRU_W), F32),
                   jax.ShapeDtypeStruct((1, MLA_W), F32), jax.ShapeDtypeStruct((1, LRU_W), F32)],
        compiler_params=_params(("arbitrary",)))(dhb, wout, ya, yl, ga, gl)


def _ffn_dact_call(name, dhb, wd, gate, up, tm, comm=None):
    rows, d = dhb.shape
    ns, fs, _ = wd.shape

    nsub = 2 if tm % 32 == 0 else 1
    sub = tm // nsub

    def body(dh_ref, wd_ref, g_ref, p_ref, dg_ref, dp_ref):
        wd = wd_ref[0]
        for r in range(nsub):
            rs = slice(r * sub, (r + 1) * sub)
            da = (0.5 * _dot_nt(dh_ref[rs, :], wd)).astype(BF16)
            g = g_ref[0, rs, :]
            p = p_ref[0, rs, :]
            sg = jax.nn.sigmoid(g)
            dg_ref[0, rs, :] = (da * p) * (sg * (1.0 + g * (1.0 - sg)))
            dp_ref[0, rs, :] = da * (g * sg)

    aspec = pl.BlockSpec((1, tm, fs), lambda s, i: (s, i, 0))
    oshape = jax.ShapeDtypeStruct((ns, rows, fs), BF16)
    return _hosted_call(
        body, name=name, grid=(ns, rows // tm),
        in_specs=[pl.BlockSpec((tm, d), lambda s, i: (i, 0)), pl.BlockSpec((1, fs, d), lambda s, i: (s, 0, 0)),
                  aspec, aspec],
        out_specs=[aspec, aspec], out_shape=[oshape, oshape],
        dims=("parallel", "parallel"), args=(dhb, wd, gate, up), comm=comm)


def _norm_in_bwd_call(name, pieces, h, g, dres, tm, comm=None, part=(0, 1), prev=None):
    rows, d = h.shape
    npc = len(pieces)
    steps = rows // tm // part[1]
    off = part[0] * steps
    n_prev = 0 if prev is None else 2

    def body(*refs):
        d_refs = refs[0:2 * npc:2]
        w_refs = refs[1:2 * npc:2]
        h_ref, g_ref, dres_ref = refs[2 * npc:2 * npc + 3]
        dh_ref, dhb_ref, dg_ref = refs[2 * npc + 3 + n_prev:]
        du = jnp.zeros((tm, d), F32)
        for d_ref, w_ref in zip(d_refs, w_refs):
            if len(d_ref.shape) == 3:
                for s in range(d_ref.shape[0]):
                    du = du + _dot(d_ref[s], w_ref[s])
            else:
                du = du + _dot(d_ref[...], w_ref[...])
        x = h_ref[...]
        r = _rms(x, d)
        n = x * r
        dh = dres_ref[...] + _rms_bwd(du * g_ref[...], n, r, d)
        dh_ref[...] = dh
        dhb_ref[...] = dh.astype(BF16)
        dg = jnp.sum(du * n, axis=0, keepdims=True)

        @pl.when(pl.program_id(0) == 0)
        def _():
            dg_ref[...] = dg

        @pl.when(pl.program_id(0) != 0)
        def _():
            dg_ref[...] += dg

    in_specs, args = [], []
    for dd, w in pieces:
        if dd.ndim == 3:
            in_specs.append(pl.BlockSpec((dd.shape[0], tm, dd.shape[2]), lambda i: (0, i + off, 0)))
            in_specs.append(_resident(w.shape))
        else:
            in_specs.append(pl.BlockSpec((tm, dd.shape[1]), lambda i: (i + off, 0)))
            in_specs.append(_resident(w.shape))
        args += [dd, w]
    full = pl.BlockSpec((tm, d), lambda i: (i + off, 0))
    gspec = pl.BlockSpec((1, d), lambda i: (0, 0))
    n_in = len(in_specs) + 3
    return _hosted_call(
        body, name=name, grid=(steps,),
        in_specs=in_specs + [full, gspec, full] + _any_specs(n_prev),
        out_specs=[full, full, gspec],
        out_shape=[jax.ShapeDtypeStruct((rows, d), F32), jax.ShapeDtypeStruct((rows, d), BF16),
                   jax.ShapeDtypeStruct((1, d), F32)],
        args=(*args, h, g, dres, *(prev or ())), comm=comm,
        aliases={n_in: 0, n_in + 1: 1} if prev is not None else {})


def _wgrad_call(name, a, b, scale=1.0, comm=None):
    a3, b3 = a.ndim == 3, b.ndim == 3
    ns = a.shape[0] if a3 else (b.shape[0] if b3 else 1)
    rows, m = a.shape[-2:]
    n = b.shape[-1]
    tmm = m if a3 else _col_tile(m, 256)

    def body(a_ref, b_ref, o_ref):
        av = a_ref[0] if a3 else a_ref[...]
        bv = b_ref[0] if b3 else b_ref[...]
        res = _dot_tn(av, bv)
        if scale != 1.0:
            res = res * scale
        if a3 or b3:
            o_ref[0] = res
        else:
            o_ref[...] = res

    aspec = (pl.BlockSpec((1, rows, tmm), lambda s, j: (s, 0, j)) if a3
             else pl.BlockSpec((rows, tmm), lambda s, j: (0, j)))
    bspec = (pl.BlockSpec((1, rows, n), lambda s, j: (s, 0, 0)) if b3
             else pl.BlockSpec((rows, n), lambda s, j: (0, 0)))
    if a3 or b3:
        ospec = pl.BlockSpec((1, tmm, n), lambda s, j: (s, j, 0))
        oshape = jax.ShapeDtypeStruct((ns, m, n), F32)
    else:
        ospec = pl.BlockSpec((tmm, n), lambda s, j: (j, 0))
        oshape = jax.ShapeDtypeStruct((m, n), F32)
    return _hosted_call(
        body, name=name, grid=(ns, m // tmm), in_specs=[aspec, bspec], out_specs=[ospec], out_shape=[oshape],
        dims=("parallel", "parallel"), args=(a, b), comm=comm)[0]


def _mla_prep_bwd_call(z, dq, dk, dv, gql, gkvl, gqh, gkh, wuq, wuk, wuv, tabs, lp, tm, comm=None):
    rows = z.shape[0]
    tpe = lp // tm

    def body(z_ref, dq_ref, dk_ref, dv_ref, gql_ref, gkvl_ref, gqh_ref, gkh_ref, wuq_ref, wuk_ref, wuv_ref,
             c_ref, s1_ref, s2_ref, dz_ref, dqp_ref, dkn_ref, dvv_ref, dgql_ref, dgkvl_ref, dgqh_ref, dgkh_ref):
        gql, gkvl = gql_ref[...], gkvl_ref[...]
        gq, gk = gqh_ref[...], gkh_ref[...]
        wuq, wuk, wuv = wuq_ref[...], wuk_ref[...], wuv_ref[...]
        m = _mla_heads(z_ref[...], gql, gkvl, wuq, wuk, wuv)
        c, s1, s2 = c_ref[...], s1_ref[...], s2_ref[...]
        dgq = jnp.zeros((1, D_QKP), F32)
        dgk = jnp.zeros((1, D_QKP), F32)
        dkr = jnp.zeros((tm, D_QKP - D_NOPE), F32)
        for hd in range(HEADS):
            qn, rqh, knn, krn, rkh = m["heads"][hd]
            dqg = jnp.concatenate([dq_ref[hd, :, 0:D_NOPE].astype(F32),
                                   _rope_t(dq_ref[hd, :, D_NOPE:D_QKP].astype(F32), c, s1, s2)], axis=1)
            dgq = dgq + jnp.sum(dqg * qn, axis=0, keepdims=True)
            dqn = dqg * gq
            dqr = rqh * (dqn - qn * (jnp.sum(dqn * qn, axis=-1, keepdims=True) * (1.0 / D_QK)))
            dqp_ref[:, hd * D_QKP:(hd + 1) * D_QKP] = dqr.astype(BF16)
            kn_full = jnp.concatenate([knn, krn], axis=1)
            dkg = jnp.concatenate([dk_ref[hd, :, 0:D_NOPE].astype(F32),
                                   _rope_t(dk_ref[hd, :, D_NOPE:D_QKP].astype(F32), c, s1, s2)], axis=1)
            dgk = dgk + jnp.sum(dkg * kn_full, axis=0, keepdims=True)
            dkn = dkg * gk
            dkraw = rkh * (dkn - kn_full * (jnp.sum(dkn * kn_full, axis=-1, keepdims=True) * (1.0 / D_QK)))
            dkn_ref[:, hd * D_NOPE:(hd + 1) * D_NOPE] = dkraw[:, 0:D_NOPE].astype(BF16)
            dkr = dkr + dkraw[:, D_NOPE:D_QKP]
            dvv_ref[:, hd * D_V:(hd + 1) * D_V] = dv_ref[hd]
        dcqn = _dot(dqp_ref[...], wuq)
        dckvn = _dot_nt(dkn_ref[...], wuk) + _dot_nt(dvv_ref[...], wuv)
        dz_ref[:, 0:Q_RANK] = _rms_bwd(dcqn * gql, m["nq"], m["rq"], Q_RANK).astype(BF16)
        dz_ref[:, Q_RANK:Z_KR] = _rms_bwd(dckvn * gkvl, m["nkv"], m["rkv"], KV_RANK).astype(BF16)
        dz_ref[:, Z_KR:Z_MLA] = dkr.astype(BF16)
        parts = [(dgql_ref, jnp.sum(dcqn * m["nq"], axis=0, keepdims=True)),
                 (dgkvl_ref, jnp.sum(dckvn * m["nkv"], axis=0, keepdims=True)), (dgqh_ref, dgq), (dgkh_ref, dgk)]

        @pl.when(pl.program_id(0) == 0)
        def _():
            for ref, val in parts:
                ref[...] = val

        @pl.when(pl.program_id(0) != 0)
        def _():
            for ref, val in parts:
                ref[...] += val

    def const(shape):
        return pl.BlockSpec(shape, lambda i: tuple(0 for _ in shape))

    tab = pl.BlockSpec((tm, 128), lambda i: (i % tpe, 0))
    hq = pl.BlockSpec((HEADS, tm, D_QKP), lambda i: (0, i, 0))
    hv = pl.BlockSpec((HEADS, tm, D_V), lambda i: (0, i, 0))

    def rowspec(n):
        return pl.BlockSpec((tm, n), lambda i: (i, 0))

    return _hosted_call(
        body, name="mla_prep_bwd", grid=(rows // tm,),
        in_specs=[rowspec(Z_MLA), hq, hq, hv, const((1, Q_RANK)), const((1, KV_RANK)), const((1, D_QKP)),
                  const((1, D_QKP)), const((HEADS * D_QKP, Q_RANK)), const((KV_RANK, HEADS * D_NOPE)),
                  const((KV_RANK, HEADS * D_V)), tab, tab, tab],
        out_specs=[rowspec(Z_MLA), rowspec(HEADS * D_QKP), rowspec(HEADS * D_NOPE), rowspec(HEADS * D_V),
                   const((1, Q_RANK)), const((1, KV_RANK)), const((1, D_QKP)), const((1, D_QKP))],
        out_shape=[jax.ShapeDtypeStruct((rows, Z_MLA), BF16), jax.ShapeDtypeStruct((rows, HEADS * D_QKP), BF16),
                   jax.ShapeDtypeStruct((rows, HEADS * D_NOPE), BF16), jax.ShapeDtypeStruct((rows, HEADS * D_V), BF16),
                   jax.ShapeDtypeStruct((1, Q_RANK), F32), jax.ShapeDtypeStruct((1, KV_RANK), F32),
                   jax.ShapeDtypeStruct((1, D_QKP), F32), jax.ShapeDtypeStruct((1, D_QKP), F32)],
        args=(z, dq, dk, dv, gql, gkvl, gqh, gkh, wuq, wuk, wuv, *tabs), comm=comm)


def _local_step(h0, target, w, nb, lp, sched=None):
    tm = _row_tile(nb * lp, 1408)
    te = _row_tile(lp, 512)
    tabs = _rope_tables(lp)
    g = {}
    if sched is None:
        host = lambda stage: None
    else:
        sched.g = g
        host = sched.host

    def ffn_act(tag, u, split):
        if split:
            gate = _ffn_gate_call(tag + "_gate", u, w[tag + "_w_gate"], tm, host(tag + "_gate"))
            up, act = _ffn_upact_call(tag + "_upact", u, w[tag + "_w_up"], gate, tm, host(tag + "_upact"))
        else:
            gate, up, act = _ffn_up_call(tag + "_up", u, w[tag + "_w_gate"], w[tag + "_w_up"], tm, host(tag + "_up"))
        return u, gate, up, act

    def ffn_bwd(tag, h, saved, dh, dhb, split):
        u, gate, up, act = saved
        dgate, dup = _ffn_dact_call(tag + "_dact", dhb, w[tag + "_w_down"], gate, up, tm, host(tag + "_dact"))
        g[tag + "_w_down"] = _wgrad_call(tag + "_dwd", act, dhb, 0.5, host(tag + "_dwd"))
        g[tag + "_w_gate"] = _wgrad_call(tag + "_dwg", dgate, u, 1.0, host(tag + "_dwg"))
        g[tag + "_w_up"] = _wgrad_call(tag + "_dwu", dup, u, 1.0, host(tag + "_dwu"))
        pieces = [(dgate, w[tag + "_w_gate"]), (dup, w[tag + "_w_up"])]
        if not split:
            dh_in, dhb_in, g[tag + "_norm"] = _norm_in_bwd_call(tag + "_din", pieces, h, w[tag + "_norm"], dh, te,
                                                                host(tag + "_din"))
            return dh_in, dhb_in
        dh_a, dhb_a, dg_a = _norm_in_bwd_call(tag + "_din_a", pieces, h, w[tag + "_norm"], dh, te,
                                              host(tag + "_din_a"), part=(0, 2))
        dh_in, dhb_in, dg_b = _norm_in_bwd_call(tag + "_din_b", pieces, h, w[tag + "_norm"], dh, te,
                                                host(tag + "_din_b"), part=(1, 2), prev=(dh_a, dhb_a))
        g[tag + "_norm"] = dg_a + dg_b
        return dh_in, dhb_in

    s1 = ffn_act("ffn1", _rmsnorm_call("ffn1_norm", h0, w["ffn1_norm"], te), True)
    h1, un = _ffn_down_call("ffn1_down", s1[3], w["ffn1_w_down"], h0, te, host("ffn1_down"), w["mix_norm"])
    z = _mm_call("mix_in", un, w["w_in"], tm, F32)
    mla_w = (w["q_latent_norm"], w["kv_latent_norm"], w["q_head_norm"], w["k_head_norm"], w["w_uq"], w["w_uk"],
             w["w_uv"])
    q, k, v, cqn, ckvn = _mla_prep_call(z, *mla_w, tabs, lp, te)
    o, lse = _attn_fwd_call(q, k, v, nb, lp, host("attn_fwd"))
    lru_w = (w["conv_w"], w["conv_b"], w["gate_a_w"], w["gate_a_b"], w["gate_x_w"], w["gate_x_b"], w["lru_lambda"])
    yl, hs = _lru_fwd_call(z, *lru_w, nb, lp, host("lru_fwd"))
    y, h2, u2 = _mix_out_call(o, yl, w["attn_out_norm"], w["lru_out_norm"], w["w_out"], h1, w["ffn2_norm"], te)
    s2 = ffn_act("ffn2", u2, False)
    dh3, dh3b, g["final_norm"], loss = _ffn_down_loss_call("ffn2_down", s2[3], w["ffn2_w_down"], h2, w["final_norm"],
                                                           target, lp, te)
    g["loss"] = loss

    dh2, dh2b = ffn_bwd("ffn2", h2, s2, dh3, dh3b, False)
    g["w_out"] = _wgrad_call("dw_out", y, dh2b)
    dya, dyl, g["attn_out_norm"], g["lru_out_norm"] = _mix_out_bwd_call(
        dh2b, w["w_out"], o, yl, w["attn_out_norm"], w["lru_out_norm"], te)
    dq, dk, dv = _attn_bwd_call(q, k, v, o, lse, dya, nb, lp, host("attn_bwd"))
    (dz_mla, dqp, dkn, dvv, g["q_latent_norm"], g["kv_latent_norm"], g["q_head_norm"],
     g["k_head_norm"]) = _mla_prep_bwd_call(z, dq, dk, dv, *mla_w, tabs, lp, te, host("mla_prep_bwd"))
    g["w_uq"] = _wgrad_call("dw_uq", dqp, cqn)
    g["w_uk"] = _wgrad_call("dw_uk", ckvn, dkn)
    g["w_uv"] = _wgrad_call("dw_uv", ckvn, dvv)
    (du, dgt, g["conv_w"], g["conv_b"], g["gate_a_w"], g["gate_a_b"], g["gate_x_w"], g["gate_x_b"],
     g["lru_lambda"]) = _lru_bwd_call(z, hs, dyl, *lru_w, nb, lp, host("lru_bwd"))
    win = w["w_in"]
    g["w_in"] = jnp.concatenate(
        [_wgrad_call("dw_in_mla", dz_mla, un), _wgrad_call("dw_in_u", du, un), _wgrad_call("dw_in_g", dgt, un)],
        axis=0)
    dh1, dh1b, g["mix_norm"] = _norm_in_bwd_call(
        "mix_din", [(dz_mla, win[0:Z_MLA]), (du, win[Z_U:Z_G]), (dgt, win[Z_G:Z_W])], h1, w["mix_norm"], dh2, te,
        host("mix_din"))
    dh0, _ = ffn_bwd("ffn1", h0, s1, dh1, dh1b, True)
    return loss, dh0, g


def _place():
    x, y, c = lax.axis_index("x"), lax.axis_index("y"), lax.axis_index("c")
    return x, y, c, [(1 - x, y), (x, 1 - y), (1 - x, 1 - y)]


def _any_specs(n):
    return [pl.BlockSpec(memory_space=pl.ANY)] * n


def _remote(src, dst, sems, k, dev):
    send_sems, recv_sems, base = sems
    return pltpu.make_async_remote_copy(src_ref=src, dst_ref=dst, send_sem=send_sems.at[base + k],
                                        recv_sem=recv_sems.at[base + k], device_id=dev, device_id_type=MESH)


EW_VMEM_BYTES = 24 * 1024 * 1024


def _fit_rows(rows, cols, blocks):
    return _row_tile(rows, max(16, int(EW_VMEM_BYTES // (8 * blocks)) // cols))


class _Geom:
    def __init__(self, n0, n1, blocks=1.0):
        self.n0, self.n1 = n0, n1
        self.axis = 0 if n0 % 32 == 0 else 1
        self.h0, self.h1 = (n0 // 2, n1) if self.axis == 0 else (n0, n1 // 2)
        self.tr = _fit_rows(self.h0, self.h1, blocks)
        self.nblk = self.h0 // self.tr

    def half_ref(self, ref, lead, idx):
        if self.axis == 0:
            return ref.at[(*lead, pl.ds(idx * self.h0, self.h0))]
        return ref.at[(*lead, slice(None), pl.ds(idx * self.h1, self.h1))]

    def half_block(self, lead, i, idx):
        return (*lead, idx * self.nblk + i, 0) if self.axis == 0 else (*lead, i, idx)


class _Comm:
    def __init__(self, ins, out_shapes, aliases, n_sems, start, finish, deliver):
        self.ins, self.out_shapes, self.aliases, self.n_sems = list(ins), list(out_shapes), dict(aliases), n_sems
        self.start, self.finish, self.deliver = start, finish, deliver

    def scratch(self):
        return [pltpu.SemaphoreType.DMA((self.n_sems,)), pltpu.SemaphoreType.DMA((self.n_sems,))]


def _comm_call(name, comm):
    n_in = len(comm.ins)

    def body(*refs):
        ins, outs, sems = refs[:n_in], refs[n_in:-2], (*refs[-2:], 0)
        comm.start(ins, outs, sems)
        comm.finish(ins, outs, sems)

    res = pl.pallas_call(
        body, name=name, out_shape=comm.out_shapes, in_specs=_any_specs(n_in),
        out_specs=_any_specs(len(comm.out_shapes)), input_output_aliases=comm.aliases,
        scratch_shapes=comm.scratch())(*comm.ins)
    return comm.deliver(list(res))


def _hosted_call(body, *, name, grid, in_specs, out_specs, out_shape, args, scratch_shapes=(), dims=None, comm=None,
                 prefetch=None, aliases=None):
    aliases = dict(aliases or {})
    in_specs, out_specs, out_shape = list(in_specs), list(out_specs), list(out_shape)
    n_pre = 0 if prefetch is None else 1

    def call(fn, in_specs, out_specs, out_shape, scratch, aliases, dims, args):
        if prefetch is None:
            return pl.pallas_call(
                fn, name=name, grid=grid, in_specs=in_specs, out_specs=out_specs, out_shape=out_shape,
                scratch_shapes=scratch, input_output_aliases=aliases, compiler_params=_params(dims))(*args)
        spec = pltpu.PrefetchScalarGridSpec(num_scalar_prefetch=1, grid=grid, in_specs=in_specs, out_specs=out_specs,
                                            scratch_shapes=scratch)
        return pl.pallas_call(
            fn, name=name, grid_spec=spec, out_shape=out_shape,
            input_output_aliases={i + 1: o for i, o in aliases.items()}, compiler_params=_params(dims))(prefetch, *args)

    if comm is None:
        return list(call(body, in_specs, out_specs, out_shape, list(scratch_shapes), aliases,
                         dims or ("arbitrary",) * len(grid), args))
    n_in, n_out, n_ci, n_co = len(in_specs), len(out_specs), len(comm.ins), len(comm.out_shapes)

    def wrapped(*refs):
        pre, refs = refs[:n_pre], refs[n_pre:]
        ins, cins = refs[:n_in], refs[n_in:n_in + n_ci]
        outs = refs[n_in + n_ci:n_in + n_ci + n_out]
        couts = refs[n_in + n_ci + n_out:n_in + n_ci + n_out + n_co]
        scratch, sems = refs[n_in + n_ci + n_out + n_co:-2], (*refs[-2:], 0)
        first = functools.reduce(jnp.logical_and, [pl.program_id(k) == 0 for k in range(len(grid))])
        last = functools.reduce(jnp.logical_and, [pl.program_id(k) == grid[k] - 1 for k in range(len(grid))])

        @pl.when(first)
        def _():
            comm.start(cins, couts, sems)

        body(*pre, *ins, *outs, *scratch)

        @pl.when(last)
        def _():
            comm.finish(cins, couts, sems)

    res = call(wrapped, in_specs + _any_specs(n_ci), out_specs + _any_specs(n_co), out_shape + comm.out_shapes,
               list(scratch_shapes) + comm.scratch(),
               {**aliases, **{n_in + i: n_out + o for i, o in comm.aliases.items()}},
               ("arbitrary",) * len(grid), (*args, *comm.ins))
    comm.deliver(list(res[n_out:]))
    return list(res[:n_out])


def _gather_comm(bufs, deliver):
    n = len(bufs)
    geoms = [_Geom(*b.shape[1:]) for b in bufs]

    def first(outs, sems):
        x, y, c, chips = _place()
        cps = []
        for a in range(n):
            mine = geoms[a].half_ref(outs[a], (2 * x + y,), c)
            cps += [_remote(mine, mine, sems, 6 * a + j, (cx, cy, c)) for j, (cx, cy) in enumerate(chips)]
        return cps

    def start(ins, outs, sems):
        for cp in first(outs, sems):
            cp.start()

    def finish(ins, outs, sems):
        x, y, c, chips = _place()
        sib = (x, y, 1 - c)
        passed = []
        for a in range(n):
            for j, (cx, cy) in enumerate(chips):
                land = geoms[a].half_ref(outs[a], (2 * cx + cy,), c)
                _remote(land, land, sems, 6 * a + j, sib).wait_recv()
                cp = _remote(land, land, sems, 6 * a + 3 + j, sib)
                cp.start()
                passed.append(cp)
        for a in range(n):
            for j, (cx, cy) in enumerate(chips):
                land = geoms[a].half_ref(outs[a], (2 * cx + cy,), 1 - c)
                _remote(land, land, sems, 6 * a + 3 + j, sib).wait_recv()
        for cp in first(outs, sems) + passed:
            cp.wait_send()

    return _Comm(bufs, [jax.ShapeDtypeStruct(b.shape, b.dtype) for b in bufs], {a: a for a in range(n)}, 6 * n,
                 start, finish, deliver)


def _reduce_pair_comm(grads, deliver):
    n = len(grads)
    geoms = [_Geom(*a.shape[1:]) for a in grads]

    def copies(ins, outs, sems):
        x, y, c, _ = _place()
        return [_remote(geoms[a].half_ref(ins[a], (slice(None),), 1 - c), outs[a], sems, a, (x, y, 1 - c))
                for a in range(n)]

    def start(ins, outs, sems):
        for cp in copies(ins, outs, sems):
            cp.start()

    def finish(ins, outs, sems):
        cps = copies(ins, outs, sems)
        for cp in cps:
            cp.wait_recv()
        for cp in cps:
            cp.wait_send()

    shapes = [jax.ShapeDtypeStruct((N_SHARD, g.h0, g.h1), a.dtype) for a, g in zip(grads, geoms)]
    return _Comm(grads, shapes, {}, n, start, finish, deliver)


def _reduce_chips_comm(parts, deliver):
    n = len(parts)

    def copies(ins, outs, sems):
        x, y, c, chips = _place()
        return [_remote(ins[a].at[2 * cx + cy], outs[a].at[j], sems, 3 * a + j, (cx, cy, c))
                for a in range(n) for j, (cx, cy) in enumerate(chips)]

    def start(ins, outs, sems):
        for cp in copies(ins, outs, sems):
            cp.start()

    def finish(ins, outs, sems):
        cps = copies(ins, outs, sems)
        for cp in cps:
            cp.wait_recv()
        for cp in cps:
            cp.wait_send()

    shapes = [jax.ShapeDtypeStruct((3,) + a.shape[1:], a.dtype) for a in parts]
    return _Comm(parts, shapes, {}, 3 * n, start, finish, deliver)


def _share_pair_comm(bufs, deliver):
    n = len(bufs)
    geoms = [_Geom(*b.shape) for b in bufs]

    def copies(outs, sems):
        x, y, c, _ = _place()
        cps = []
        for a in range(n):
            mine = geoms[a].half_ref(outs[a], (), c)
            cps.append(_remote(mine, mine, sems, a, (x, y, 1 - c)))
        return cps

    def start(ins, outs, sems):
        for cp in copies(outs, sems):
            cp.start()

    def finish(ins, outs, sems):
        x, y, c, _ = _place()
        for a in range(n):
            land = geoms[a].half_ref(outs[a], (), 1 - c)
            _remote(land, land, sems, a, (x, y, 1 - c)).wait_recv()
        for cp in copies(outs, sems):
            cp.wait_send()

    return _Comm(bufs, [jax.ShapeDtypeStruct(b.shape, b.dtype) for b in bufs], {a: a for a in range(n)}, n,
                 start, finish, deliver)


def _small_comm(pack, deliver):
    r, d = pack.shape

    def copies(ins, outs, sems):
        x, y, c, _ = _place()
        cps = []
        for k in range(1, 8):
            peer = (x ^ ((k >> 2) & 1), y ^ ((k >> 1) & 1), c ^ (k & 1))
            cps.append(_remote(ins[0], outs[0].at[4 * x + 2 * y + c], sems, k - 1, peer))
        return cps

    def start(ins, outs, sems):
        for cp in copies(ins, outs, sems):
            cp.start()

    def finish(ins, outs, sems):
        cps = copies(ins, outs, sems)
        for cp in cps:
            cp.wait_recv()
        for cp in cps:
            cp.wait_send()

    return _Comm([pack], [jax.ShapeDtypeStruct((8, r, d), pack.dtype)], {}, 7, start, finish, deliver)


def _join_comms(comms):
    comms = [c for c in comms if c is not None]
    if len(comms) <= 1:
        return comms[0] if comms else None
    ins, out_shapes, aliases, spans, n_sems = [], [], {}, [], 0
    for c in comms:
        aliases.update({len(ins) + i: len(out_shapes) + o for i, o in c.aliases.items()})
        spans.append((len(ins), len(ins) + len(c.ins), len(out_shapes), len(out_shapes) + len(c.out_shapes), n_sems))
        ins += c.ins
        out_shapes += c.out_shapes
        n_sems += c.n_sems

    def run(which):
        def go(all_ins, all_outs, sems):
            for c, (i0, i1, o0, o1, base) in zip(comms, spans):
                getattr(c, which)(all_ins[i0:i1], all_outs[o0:o1], (sems[0], sems[1], sems[2] + base))
        return go

    def deliver(outs):
        for c, (_, _, o0, o1, _) in zip(comms, spans):
            c.deliver(outs[o0:o1])
        return outs

    return _Comm(ins, out_shapes, aliases, n_sems, run("start"), run("finish"), deliver)


def _ew_call(name, fn, ins, out_dtypes):
    shape = ins[0].shape
    cols = shape[-1]
    rows = 1
    for s_ in shape[:-1]:
        rows *= s_
    ins2 = [a.reshape(rows, cols) for a in ins]
    tr = rows
    for t in range(16, min(rows, max(16, (1 << 19) // cols)) + 1, 16):
        if rows % t == 0:
            tr = t
    no = len(out_dtypes)

    def body(*refs):
        outs = fn(*[r[...] for r in refs[:len(ins2)]])
        for ref, val in zip(refs[len(ins2):], outs):
            ref[...] = val.astype(ref.dtype)

    spec = pl.BlockSpec((tr, cols), lambda i: (i, 0))
    res = pl.pallas_call(
        body, name=name, grid=(rows // tr,), in_specs=[spec] * len(ins2), out_specs=[spec] * no,
        out_shape=[jax.ShapeDtypeStruct((rows, cols), dt) for dt in out_dtypes],
        compiler_params=_params(("parallel",)))(*ins2)
    return [r.reshape(shape) for r in res]


def _adamw_math(w, g, m, v):
    m = ADAM_B1 * m + (1.0 - ADAM_B1) * g
    v = ADAM_B2 * v + (1.0 - ADAM_B2) * (g * g)
    m_hat = m / (1.0 - ADAM_B1 ** ADAM_STEP)
    v_hat = v / (1.0 - ADAM_B2 ** ADAM_STEP)
    delta = -ADAM_LR * (m_hat / (jnp.sqrt(v_hat) + ADAM_EPS) + ADAM_WD * w)
    return delta, m, v


def _adamw_call(name, w, g, m, v):
    return _ew_call(name, _adamw_math, [w, g, m, v], [F32, F32, F32])


def _tiled_call(name, fn, place, grid, in_items, out_items, comm=None):
    ni = len(in_items)

    def body(place_ref, *refs):
        vals = fn(*[r[...] for r in refs[:ni]])
        for ref, val in zip(refs[ni:], vals):
            ref[...] = val.astype(ref.dtype)

    return _hosted_call(
        body, name=name, grid=grid, in_specs=[pl.BlockSpec(blk, imap) for _, blk, imap in in_items],
        out_specs=[pl.BlockSpec(blk, imap) for _, _, blk, imap in out_items],
        out_shape=[jax.ShapeDtypeStruct(shp, dt) for shp, dt, _, _ in out_items],
        args=[a for a, _, _ in in_items], prefetch=place, comm=comm)


def _cast_call(name, place, shards, comm=None):
    n0, n1 = shards[0].shape
    tr = _fit_rows(n0, n1, 1.5 * len(shards))
    ins = [(a, (tr, n1), lambda i, p: (i, 0)) for a in shards]
    outs = [((N_SHARD, n0, n1), BF16, (1, tr, n1), lambda i, p: (p[0], i, 0)) for _ in shards]
    return _tiled_call(name, lambda *v: [x[None] for x in v], place, (n0 // tr,), ins, outs, comm)


def _pair_sum_call(name, place, fulls, gots):
    k = len(fulls)
    g = _Geom(*fulls[0].shape[1:], blocks=2.5 * k)
    blk = (1, g.tr, g.h1)
    ins = [(a, blk, lambda s, i, p: g.half_block((s,), i, p[1])) for a in fulls]
    ins += [(a, blk, lambda s, i, p: (s, i, 0)) for a in gots]
    outs = [((N_SHARD, g.h0, g.h1), BF16, blk, lambda s, i, p: (s, i, 0)) for _ in fulls]
    return _tiled_call(name, lambda *v: [v[j] + v[k + j] for j in range(k)], place, (N_SHARD, g.nblk), ins, outs)


def _chip_sum_call(name, place, fulls, gots, recvs, comm=None):
    k = len(fulls)
    g = _Geom(*fulls[0].shape[1:], blocks=4.5 * k)
    blk = (1, g.tr, g.h1)
    ins = [(a, blk, lambda i, p: g.half_block((p[0],), i, p[1])) for a in fulls]
    ins += [(a, blk, lambda i, p: (p[0], i, 0)) for a in gots]
    ins += [(a, (3, g.tr, g.h1), lambda i, p: (0, i, 0)) for a in recvs]
    outs = [((g.n0, g.n1), F32, (g.tr, g.h1), lambda i, p: g.half_block((), i, p[1])) for _ in fulls]

    def fn(*v):
        res = []
        for j in range(k):
            r = v[2 * k + j].astype(F32)
            res.append(v[j][0] + v[k + j][0] + r[0] + r[1] + r[2])
        return res

    return _tiled_call(name, fn, place, (g.nblk,), ins, outs, comm)


def _adamw_group_call(name, ws, gs, ms, vs, comm=None):
    k = len(ws)
    n0, n1 = ws[0].shape
    tr = _fit_rows(n0, n1, 8 * k)
    spec = pl.BlockSpec((tr, n1), lambda i: (i, 0))

    def body(*refs):
        for j in range(k):
            g = refs[k + j][...]
            delta, m, vv = _adamw_math(refs[j][...], g, refs[2 * k + j][...], refs[3 * k + j][...])
            for ref, val in zip(refs[4 * k + 4 * j:4 * k + 4 * j + 4], (g, delta, m, vv)):
                ref[...] = val

    flat = _hosted_call(
        body, name=name, grid=(n0 // tr,), in_specs=[spec] * (4 * k), out_specs=[spec] * (4 * k),
        out_shape=[jax.ShapeDtypeStruct((n0, n1), F32)] * (4 * k), dims=("parallel",),
        args=(*ws, *gs, *ms, *vs), comm=comm)
    return [flat[4 * j:4 * j + 4] for j in range(k)]


def _small_update_call(me, early, own_early, late, own_late, wp, mp, vp):
    nd, r, d = early.shape

    def body(me_ref, e_ref, oe_ref, l_ref, ol_ref, w_ref, m_ref, v_ref, gs_ref, d_ref, nm_ref, nv_ref):
        mine = me_ref[0]

        def total(g_ref, own_ref):
            acc = None
            for k in range(nd):
                part = jnp.where(mine == k, own_ref[...], g_ref[k])
                acc = part if acc is None else acc + part
            return acc

        gs = total(e_ref, oe_ref)
        ls = total(l_ref, ol_ref)
        gs_ref[...] = gs
        first = gs[0:8] + ls[0:8]
        gs_ref[0:8, :] = first
        gs_ref[ROW_META:ROW_META + N_META, :] = gs[ROW_META:ROW_META + N_META] + ls[8:8 + N_META]
        grads = jnp.concatenate([first, gs[8:SMALL_ADAM_ROWS]], axis=0)
        delta, m, v = _adamw_math(w_ref[...], grads, m_ref[...], v_ref[...])
        d_ref[...] = delta
        nm_ref[...] = m
        nv_ref[...] = v

    vm = pl.BlockSpec(memory_space=pltpu.VMEM)
    ashape = jax.ShapeDtypeStruct((SMALL_ADAM_ROWS, d), F32)
    return pl.pallas_call(
        body, name="small_update", in_specs=[pl.BlockSpec(memory_space=pltpu.SMEM)] + [vm] * 7, out_specs=[vm] * 4,
        out_shape=[jax.ShapeDtypeStruct((r, d), F32), ashape, ashape, ashape],
        compiler_params=pltpu.CompilerParams(vmem_limit_bytes=VMEM_LIMIT_BYTES))(
            me, early, own_early, late, own_late, wp, mp, vp)


SMALL_NAMES = ["ffn1_norm", "mix_norm", "ffn2_norm", "final_norm", "q_latent_norm", "kv_latent_norm",
               "q_head_norm", "k_head_norm", "conv_b", "gate_a_b", "gate_x_b", "lru_lambda", "attn_out_norm",
               "lru_out_norm"]
ROW_CONV_W = 14
ROW_GATE_A = 16
ROW_GATE_X = 48
ROW_META = 80
ROW_LOSS = 96


def _row(a):
    flat = a.reshape(1, -1)
    return jnp.pad(flat, ((0, 0), (0, D_MODEL - flat.shape[1])))


def _pack_small(t, rows):
    parts = [_row(t[nm]) for nm in SMALL_NAMES]
    parts.append(t["conv_w"].reshape(2, D_MODEL))
    parts.append(t["gate_a_w"].reshape(32, D_MODEL))
    parts.append(t["gate_x_w"].reshape(32, D_MODEL))
    p = jnp.concatenate(parts, axis=0)
    return jnp.pad(p, ((0, rows - p.shape[0]), (0, 0)))


def _early_pack(g):
    gs = {nm: g.get(nm, jnp.zeros((1, D_MODEL), F32)) for nm in SMALL_NAMES}
    gs["q_head_norm"] = g["q_head_norm"][:, 0:D_QK]
    gs["k_head_norm"] = g["k_head_norm"][:, 0:D_QK]
    for nm in ("conv_b", "gate_a_b", "gate_x_b", "lru_lambda"):
        gs[nm] = g[nm].reshape(1, LRU_W)
    gs["conv_w"] = g["conv_w"].transpose(1, 0, 2).reshape(CONV_K, LRU_W)
    gs["gate_a_w"] = _gate_blocks(g["gate_a_w"])
    gs["gate_x_w"] = _gate_blocks(g["gate_x_w"])
    return jnp.concatenate([_pack_small(gs, ROW_META), jnp.zeros((N_META, D_MODEL), F32), _row(g["loss"][:, 0:1]),
                            jnp.zeros((SMALL_ROWS - ROW_LOSS - 1, D_MODEL), F32)], axis=0)


def _unpack_small(p, like):
    out = {}
    for k, nm in enumerate(SMALL_NAMES):
        out[nm] = p[k, 0:like[nm].size].reshape(like[nm].shape)
    out["gate_a_w"] = p[ROW_GATE_A:ROW_GATE_A + 32].reshape(like["gate_a_w"].shape)
    out["gate_x_w"] = p[ROW_GATE_X:ROW_GATE_X + 32].reshape(like["gate_x_w"].shape)
    return out


def _gate_dense(wg):
    w4 = wg[0].reshape(N_LRU_TILES, 2, 64, 64)
    zero = jnp.zeros((N_LRU_TILES, 64, 64), wg.dtype)
    top = jnp.concatenate([w4[:, 0], zero], axis=2)
    bot = jnp.concatenate([zero, w4[:, 1]], axis=2)
    return jnp.concatenate([top, bot], axis=1).astype(BF16)


def _gate_blocks(dw):
    return jnp.stack([dw[:, 0:64, 0:64], dw[:, 64:128, 64:128]], axis=1).reshape(8, 64, 64)


BIG_NAMES = ["ffn1_w_gate", "ffn1_w_up", "ffn1_w_down", "w_in", "w_uq", "w_uk", "w_uv", "w_out", "ffn2_w_gate",
             "ffn2_w_up", "ffn2_w_down"]
BIG_GROUPS = [["ffn1_w_gate", "ffn1_w_up", "ffn1_w_down", "ffn2_w_gate", "ffn2_w_up", "ffn2_w_down"], ["w_in"],
              ["w_uq"], ["w_uk", "w_uv"], ["w_out"]]
TRANSPOSED = ("ffn1_w_gate", "ffn1_w_up", "ffn2_w_gate", "ffn2_w_up", "w_in", "w_uq")


def _to2d(nm, a):
    return a[0].T if nm in TRANSPOSED else a[0]


def _from2d(nm, a):
    return (a.T if nm in TRANSPOSED else a)[None]


WEIGHT_NAMES = ["meta_tokens", "ffn1_norm", "ffn1_w_gate", "ffn1_w_up", "ffn1_w_down", "mix_norm", "w_in",
                "q_latent_norm", "w_uq", "kv_latent_norm", "w_uk", "w_uv", "q_head_norm", "k_head_norm", "conv_w",
                "conv_b", "gate_a_w", "gate_a_b", "gate_x_w", "gate_x_b", "lru_lambda", "attn_out_norm",
                "lru_out_norm", "w_out", "ffn2_norm", "ffn2_w_gate", "ffn2_w_up", "ffn2_w_down", "final_norm"]


def _weight_from(nm, slots):
    if nm == "w_in":
        win = slots.reshape(IN_WIDTH, D_MODEL)
        return jnp.concatenate([win[0:Z_KR + D_ROPE], jnp.zeros((128 - D_ROPE, D_MODEL), BF16),
                                win[Z_KR + D_ROPE:]], axis=0)
    if nm == "w_uq":
        return jnp.pad(slots, ((0, 0), (0, D_QKP - D_QK), (0, 0))).reshape(HEADS * D_QKP, Q_RANK)
    if nm in ("w_uk", "w_uv"):
        return slots.transpose(1, 0, 2).reshape(KV_RANK, HEADS * D_NOPE)
    if nm == "w_out":
        return slots.reshape(D_MODEL, D_MODEL)
    return slots


def _small_weights(p, small):
    w = {nm: p[nm] for nm in SMALL_NAMES}
    w["q_head_norm"] = jnp.pad(p["q_head_norm"], ((0, 0), (0, D_QKP - D_QK)))
    w["k_head_norm"] = jnp.pad(p["k_head_norm"], ((0, 0), (0, D_QKP - D_QK)))
    w["conv_w"] = small[:, N_META:N_META + 2, :].reshape(N_SHARD, CONV_K, LRU_TILE)
    w["gate_a_w"] = _gate_dense(p["gate_a_w"])
    w["gate_x_w"] = _gate_dense(p["gate_x_w"])
    meta = small[:, 0:N_META, :].transpose(1, 0, 2).reshape(N_META, D_MODEL)
    return w, meta


def _full_weights(p, gathered, small):
    w, meta = _small_weights(p, small)
    w.update({nm: _weight_from(nm, gathered[nm]) for nm in BIG_NAMES})
    return w, meta


def _shard_grad(nm, g):
    if nm == "w_in":
        return jnp.concatenate([g[0:Z_KR + D_ROPE], g[Z_MLA:]], axis=0).reshape(N_SHARD, IN_WIDTH // N_SHARD, D_MODEL)
    if nm == "w_uq":
        return g.reshape(HEADS, D_QKP, Q_RANK)[:, 0:D_QK, :]
    if nm in ("w_uk", "w_uv"):
        return g.reshape(KV_RANK, HEADS, D_NOPE).transpose(1, 0, 2)
    if nm == "w_out":
        return g.reshape(N_SHARD, D_MODEL // N_SHARD, D_MODEL)
    return g


def _shard_grads(g):
    return {nm: _shard_grad(nm, g[nm]) for nm in BIG_NAMES}


GATHER_FIRST = ["ffn1_w_gate"]
GATHER_AT = {"ffn1_gate": ["ffn1_w_up"], "ffn1_upact": ["ffn1_w_down"],
             "ffn1_down": ["w_in", "w_uq", "w_uk", "w_uv", "w_out"], "attn_fwd": ["ffn2_w_down", "ffn2_w_gate"],
             "lru_fwd": ["ffn2_w_up"]}
PAIR_AT = [("ffn2_din", ["ffn2_w_gate", "ffn2_w_up", "ffn2_w_down"]),
           ("mix_din", ["w_out", "w_uq", "w_uk", "w_uv", "w_in"]),
           ("ffn1_dwg", ["ffn1_w_down"]), ("ffn1_dwu", ["ffn1_w_gate"]), ("ffn1_din_a", ["ffn1_w_up"])]
CHIPS_AT = [("attn_bwd", ["ffn2_w_down", "ffn2_w_gate"]), ("mla_prep_bwd", ["ffn2_w_up"]),
            ("ffn1_dact", ["w_out", "w_uq", "w_uk", "w_uv", "w_in"]),
            ("ffn1_dwu", ["ffn1_w_down"]), ("ffn1_din_a", ["ffn1_w_gate"]), ("ffn1_din_b", ["ffn1_w_up"])]
SHARE_EARLY_GROUPS, SHARE_EARLY_AT = 3, "ffn1_dwd"
SMALL_EARLY_AT = "mix_din"


def _same_shape_groups(names):
    return [[nm for nm in grp if nm in names] for grp in BIG_GROUPS if any(nm in names for nm in grp)]


class _Sched:
    def __init__(self, place, w, slots):
        self.place, self.w, self.slots = place, w, slots
        self.g = None
        self.sharded, self.from_pair, self.chip_bf16, self.from_chips = {}, {}, {}, {}
        self.early = self.early_all = None
        self.shared = {}

    def host(self, stage):
        comms = []
        if stage in GATHER_AT:
            comms.append(self.gather(GATHER_AT[stage]))
        comms += [self.chips(names) for at, names in CHIPS_AT if at == stage]
        comms += [self.pair(names) for at, names in PAIR_AT if at == stage]
        if stage == SMALL_EARLY_AT:
            comms.append(self.small_early())
        if stage == SHARE_EARLY_AT:
            comms.append(self.share_early())
        return _join_comms(comms)

    def small_early(self):
        self.early = _early_pack(self.g)

        def deliver(outs):
            self.early_all = outs[0]
            return outs

        return _small_comm(self.early, deliver)

    def gather(self, names):
        def deliver(outs):
            self.w.update({nm: _weight_from(nm, o) for nm, o in zip(names, outs)})
            return outs

        return _gather_comm([self.slots[nm] for nm in names], deliver)

    def pair(self, names):
        self.sharded.update({nm: _shard_grad(nm, self.g[nm]) for nm in names})

        def deliver(outs):
            self.from_pair.update(zip(names, outs))
            for grp in _same_shape_groups(names):
                sums = _pair_sum_call("pair_sum_" + grp[0], self.place, [self.sharded[nm] for nm in grp],
                                      [self.from_pair[nm] for nm in grp])
                self.chip_bf16.update(zip(grp, sums))
            return outs

        return _reduce_pair_comm([self.sharded[nm] for nm in names], deliver)

    def chips(self, names):
        def deliver(outs):
            self.from_chips.update(zip(names, outs))
            return outs

        return _reduce_chips_comm([self.chip_bf16[nm] for nm in names], deliver)

    def chip_sums(self, names, comm=None):
        out = {}
        for grp in _same_shape_groups(names):
            sums = _chip_sum_call("chip_sum_" + grp[0], self.place, [self.sharded[nm] for nm in grp],
                                  [self.from_pair[nm] for nm in grp], [self.from_chips[nm] for nm in grp], comm)
            comm = None
            out.update(zip(grp, sums))
        return out

    def share_early(self):
        names = [nm for at, grp in CHIPS_AT[:SHARE_EARLY_GROUPS] for nm in grp]
        mine = self.chip_sums(names)
        return _share_pair_comm([mine[nm] for nm in names], lambda o: self.shared.update(zip(names, o)))


def kernel(x, meta_tokens, ffn1_norm, ffn1_w_gate, ffn1_w_up, ffn1_w_down, mix_norm, w_in, q_latent_norm, w_uq, kv_latent_norm, w_uk, w_uv, q_head_norm, k_head_norm, conv_w, conv_b, gate_a_w, gate_a_b, gate_x_w, gate_x_b, lru_lambda, attn_out_norm, lru_out_norm, w_out, ffn2_norm, ffn2_w_gate, ffn2_w_up, ffn2_w_down, final_norm, loss_target, m_meta_tokens, m_ffn1_norm, m_ffn1_w_gate, m_ffn1_w_up, m_ffn1_w_down, m_mix_norm, m_w_in, m_q_latent_norm, m_w_uq, m_kv_latent_norm, m_w_uk, m_w_uv, m_q_head_norm, m_k_head_norm, m_conv_w, m_conv_b, m_gate_a_w, m_gate_a_b, m_gate_x_w, m_gate_x_b, m_lru_lambda, m_attn_out_norm, m_lru_out_norm, m_w_out, m_ffn2_norm, m_ffn2_w_gate, m_ffn2_w_up, m_ffn2_w_down, m_final_norm, v_meta_tokens, v_ffn1_norm, v_ffn1_w_gate, v_ffn1_w_up, v_ffn1_w_down, v_mix_norm, v_w_in, v_q_latent_norm, v_w_uq, v_kv_latent_norm, v_w_uk, v_w_uv, v_q_head_norm, v_k_head_norm, v_conv_w, v_conv_b, v_gate_a_w, v_gate_a_b, v_gate_x_w, v_gate_x_b, v_lru_lambda, v_attn_out_norm, v_lru_out_norm, v_w_out, v_ffn2_norm, v_ffn2_w_gate, v_ffn2_w_up, v_ffn2_w_down, v_final_norm):
    args = locals()
    p = {nm: args[nm] for nm in WEIGHT_NAMES}
    mom = {nm: args["m_" + nm] for nm in WEIGHT_NAMES}
    var = {nm: args["v_" + nm] for nm in WEIGHT_NAMES}
    nb, seq, d = x.shape
    lp = CHUNK + seq
    xi, yi, ci = lax.axis_index("x"), lax.axis_index("y"), lax.axis_index("c")
    chip = 2 * xi + yi

    place = jnp.stack([chip, ci]).astype(jnp.int32)
    p2 = {nm: _to2d(nm, p[nm]) for nm in BIG_NAMES}
    m2 = {nm: _to2d(nm, mom[nm]) for nm in BIG_NAMES}
    v2 = {nm: _to2d(nm, var[nm]) for nm in BIG_NAMES}

    slots = {}
    small_shard = jnp.concatenate(
        [meta_tokens, conv_w[0].reshape(2, 2 * LRU_TILE), jnp.zeros((14, 2 * LRU_TILE), F32)], axis=0)
    small_slots = lax.dynamic_update_slice(jnp.zeros((N_SHARD,) + small_shard.shape, F32), small_shard[None],
                                           (chip, 0, 0))
    for grp in BIG_GROUPS:
        for nm, buf in zip(grp, _cast_call("cast_" + grp[0], place, [p2[nm] for nm in grp])):
            slots[nm] = buf
    first = _comm_call("gather_first", _gather_comm([slots[nm] for nm in GATHER_FIRST] + [small_slots], lambda o: o))
    w, meta = _small_weights(p, first[-1])
    w.update({nm: _weight_from(nm, o) for nm, o in zip(GATHER_FIRST, first[:-1])})
    sched = _Sched(place, w, slots)

    h0 = jnp.concatenate(
        [jnp.zeros((nb, PAD_ROWS, d), F32), jnp.broadcast_to(meta[None], (nb, N_META, d)), x], axis=1)
    target = jnp.pad(loss_target, ((0, 0), (CHUNK, 0), (0, 0)))
    loss_part, dh0, g = _local_step(h0.reshape(nb * lp, d), target.reshape(nb * lp, d), w, nb, lp, sched)
    dh0 = dh0.reshape(nb, lp, d)
    grad_x = dh0[:, CHUNK:, :]

    late = jnp.concatenate([g["ffn1_norm"], g["mix_norm"], jnp.zeros((6, D_MODEL), F32),
                            jnp.sum(dh0[:, PAD_ROWS:CHUNK, :], axis=0)], axis=0)
    shared = sched.shared
    rest = [nm for at, names in CHIPS_AT[SHARE_EARLY_GROUPS:] if at is not None for nm in names]
    last = [nm for at, names in CHIPS_AT if at is None for nm in names]
    late_box = {}
    mine = sched.chip_sums(rest, _small_comm(late, lambda o: late_box.update(all=o[0])))
    share = _share_pair_comm([mine[nm] for nm in rest], lambda o: shared.update(zip(rest, o)))
    _comm_call("share_pair", _join_comms([share, sched.chips(last) if last else None]))
    if last:
        mine = sched.chip_sums(last)
        _comm_call("share_last", _share_pair_comm([mine[nm] for nm in last], lambda o: shared.update(zip(last, o))))
    late_all = late_box["all"]
    small_like = {nm: p[nm] for nm in SMALL_NAMES + ["gate_a_w", "gate_x_w"]}

    def pack_w(t):
        tt = {nm: t[nm] for nm in SMALL_NAMES + ["gate_a_w", "gate_x_w"]}
        tt["conv_w"] = jnp.zeros((CONV_K, LRU_W), F32)
        return _pack_small(tt, SMALL_ADAM_ROWS)

    me = (4 * xi + 2 * yi + ci).astype(jnp.int32).reshape(1)
    gsum, dsm, msm, vsm = _small_update_call(me, sched.early_all, sched.early, late_all, late, pack_w(p),
                                             pack_w(mom), pack_w(var))
    grads = _unpack_small(gsum, small_like)
    delta = _unpack_small(dsm, small_like)
    new_m = _unpack_small(msm, small_like)
    new_v = _unpack_small(vsm, small_like)
    loss = gsum[ROW_LOSS, 0]
    gmeta = gsum[ROW_META:ROW_META + N_META].reshape(N_META, N_SHARD, D_MODEL // N_SHARD)
    grads["meta_tokens"] = lax.dynamic_index_in_dim(gmeta, chip, axis=1, keepdims=False)
    gconv = gsum[ROW_CONV_W:ROW_CONV_W + 2].reshape(CONV_K, N_SHARD, LRU_TILE)
    grads["conv_w"] = lax.dynamic_index_in_dim(gconv, chip, axis=1, keepdims=False)[None]
    for nm in ("meta_tokens", "conv_w"):
        delta[nm], new_m[nm], new_v[nm] = _adamw_call("adamw_" + nm, p[nm], grads[nm], mom[nm], var[nm])

    for names in BIG_GROUPS:
        res = _adamw_group_call("adamw_" + names[0], [p2[nm] for nm in names], [shared[nm] for nm in names],
                                [m2[nm] for nm in names], [v2[nm] for nm in names])
        for nm, (gg, dd, mm, vv) in zip(names, res):
            grads[nm], delta[nm], new_m[nm], new_v[nm] = (_from2d(nm, t) for t in (gg, dd, mm, vv))

    return (loss, grad_x, *[grads[nm] for nm in WEIGHT_NAMES], *[delta[nm] for nm in WEIGHT_NAMES],
            *[new_m[nm] for nm in WEIGHT_NAMES], *[new_v[nm] for nm in WEIGHT_NAMES])
```

```python
import functools
import math

import jax
import jax.numpy as jnp
import numpy as np
from jax import lax
from jax.experimental import pallas as pl
from jax.experimental.pallas import tpu as pltpu

F32 = jnp.float32
BF16 = jnp.bfloat16
MESH = pl.DeviceIdType.MESH

D_MODEL = 1024
N_META = 16
CHUNK = 64
PAD_ROWS = CHUNK - N_META
HEADS = 4
D_NOPE = 128
D_ROPE = 64
D_QK = D_NOPE + D_ROPE
D_QKP = 256
D_V = 128
KV_RANK = 256
Q_RANK = 384
MLA_W = HEADS * D_V
LRU_W = 512
LRU_TILE = 128
N_LRU_TILES = LRU_W // LRU_TILE
CONV_K = 4
C_RGLRU = 8.0
ROPE_THETA = 10000.0
D_FF = 2816
N_SHARD = 4
EPS = 1e-6
NEG_INF = -1e30
Z_KR = Q_RANK + KV_RANK
Z_MLA = Z_KR + 128
Z_U = Z_MLA
Z_G = Z_U + LRU_W
Z_W = Z_G + LRU_W
IN_WIDTH = Q_RANK + KV_RANK + D_ROPE + 2 * LRU_W

ADAM_LR = 0.001
ADAM_B1 = 0.9
ADAM_B2 = 0.999
ADAM_EPS = 1e-08
ADAM_WD = 0.01
ADAM_STEP = 10

VMEM_LIMIT_BYTES = 56 * 1024 * 1024
SMALL_ROWS = 104
SMALL_ADAM_ROWS = 80


def _params(sem):
    return pltpu.CompilerParams(dimension_semantics=sem, vmem_limit_bytes=VMEM_LIMIT_BYTES)


def _resident(shape):
    return pl.BlockSpec(tuple(shape), lambda i: (0,) * len(shape), pipeline_mode=pl.Buffered(1))


def _row_tile(rows, target):
    best = 16
    for t in range(16, min(rows, target) + 1, 16):
        if rows % t == 0:
            best = t
    return best


def _col_tile(cols, target):
    best = cols
    for t in range(128, min(cols, target) + 1, 128):
        if cols % t == 0:
            best = t
    return best


def _dot(a, b):
    return jnp.dot(a, b, preferred_element_type=F32)


def _dot_nt(a, b):
    return lax.dot_general(a, b, (((1,), (1,)), ((), ())), preferred_element_type=F32)


def _dot_tn(a, b):
    return lax.dot_general(a, b, (((0,), (0,)), ((), ())), preferred_element_type=F32)


def _rms(x, n):
    return lax.rsqrt(jnp.sum(x * x, axis=-1, keepdims=True) * (1.0 / n) + EPS)


def _rms_bwd(dn, nrm, r, n):
    return r * (dn - nrm * (jnp.sum(dn * nrm, axis=-1, keepdims=True) * (1.0 / n)))


def _gelu(x):
    k = math.sqrt(2.0 / math.pi)
    t = jnp.tanh(k * (x + 0.044715 * x * x * x))
    return 0.5 * x * (1.0 + t), t


def _gelu_grad(x, t):
    k = math.sqrt(2.0 / math.pi)
    return 0.5 * (1.0 + t) + 0.5 * x * (1.0 - t * t) * k * (1.0 + 3.0 * 0.044715 * x * x)


def _sigmoid(x):
    return 0.5 + 0.5 * jnp.tanh(0.5 * x)


def _softplus_neg(lam):
    e = jnp.exp(-jnp.abs(lam))
    log1p = jnp.where(e < 0.01, e * (1.0 - e * (0.5 - e * (1.0 / 3 - e * 0.25))), jnp.log(1.0 + e))
    return jnp.maximum(-lam, 0.0) + log1p


def _rope(t, c, s1, s2):
    return t * c + pltpu.roll(t, 96, 1) * s1 + pltpu.roll(t, 32, 1) * s2


def _rope_t(d, c, s1, s2):
    return d * c + pltpu.roll(d * s1, 32, 1) + pltpu.roll(d * s2, 96, 1)


def _rope_tables(lp):
    pos = (np.arange(lp, dtype=np.int32) - PAD_ROWS).astype(np.float32)
    inv_freq = (ROPE_THETA ** (-np.arange(0, D_ROPE // 2, dtype=np.float32) / (D_ROPE // 2))).astype(np.float32)
    ang = (pos[:, None] * inv_freq[None, :]).astype(np.float32).astype(np.float64)
    cos, sin = np.cos(ang).astype(np.float32), np.sin(ang).astype(np.float32)
    z = np.zeros_like(cos)
    return (jnp.asarray(np.concatenate([cos, cos, z, z], 1)), jnp.asarray(np.concatenate([-sin, z, z, z], 1)),
            jnp.asarray(np.concatenate([z, sin, z, z], 1)))


def _rmsnorm_call(name, h, g, tm):
    rows, d = h.shape

    def body(h_ref, g_ref, o_ref):
        x = h_ref[...]
        o_ref[...] = (x * _rms(x, d) * g_ref[...]).astype(BF16)

    return pl.pallas_call(
        body, name=name, grid=(rows // tm,),
        in_specs=[pl.BlockSpec((tm, d), lambda i: (i, 0)), pl.BlockSpec((1, d), lambda i: (0, 0))],
        out_specs=pl.BlockSpec((tm, d), lambda i: (i, 0)),
        out_shape=jax.ShapeDtypeStruct((rows, d), BF16),
        compiler_params=_params(("parallel",)))(h, g)


def _ffn_up_call(name, u, wg, wu, tm, comm=None):
    rows, d = u.shape
    ns, fs, _ = wg.shape

    def body(u_ref, wg_ref, wu_ref, g_ref, p_ref, a_ref):
        uu = u_ref[...]
        g = _dot_nt(uu, wg_ref[0])
        p = _dot_nt(uu, wu_ref[0])
        g_ref[0] = g.astype(BF16)
        p_ref[0] = p.astype(BF16)
        a_ref[0] = (g * jax.nn.sigmoid(g) * p).astype(BF16)

    wspec = pl.BlockSpec((1, fs, d), lambda s, i: (s, 0, 0))
    ospec = pl.BlockSpec((1, tm, fs), lambda s, i: (s, i, 0))
    oshape = jax.ShapeDtypeStruct((ns, rows, fs), BF16)
    return _hosted_call(
        body, name=name, grid=(ns, rows // tm),
        in_specs=[pl.BlockSpec((tm, d), lambda s, i: (i, 0)), wspec, wspec],
        out_specs=[ospec, ospec, ospec], out_shape=[oshape, oshape, oshape],
        dims=("parallel", "parallel"), args=(u, wg, wu), comm=comm)


def _ffn_gate_call(name, u, wg, tm, comm=None):
    rows, d = u.shape
    ns, fs, _ = wg.shape

    def body(u_ref, wg_ref, g_ref):
        g_ref[0] = _dot_nt(u_ref[...], wg_ref[0]).astype(BF16)

    return _hosted_call(
        body, name=name, grid=(ns, rows // tm),
        in_specs=[pl.BlockSpec((tm, d), lambda s, i: (i, 0)), pl.BlockSpec((1, fs, d), lambda s, i: (s, 0, 0))],
        out_specs=[pl.BlockSpec((1, tm, fs), lambda s, i: (s, i, 0))],
        out_shape=[jax.ShapeDtypeStruct((ns, rows, fs), BF16)],
        dims=("parallel", "parallel"), args=(u, wg), comm=comm)[0]


def _ffn_upact_call(name, u, wu, gate, tm, comm=None):
    rows, d = u.shape
    ns, fs, _ = wu.shape

    def body(u_ref, wu_ref, g_ref, p_ref, a_ref):
        p = _dot_nt(u_ref[...], wu_ref[0])
        g = g_ref[0].astype(F32)
        p_ref[0] = p.astype(BF16)
        a_ref[0] = (g * jax.nn.sigmoid(g) * p).astype(BF16)

    ospec = pl.BlockSpec((1, tm, fs), lambda s, i: (s, i, 0))
    oshape = jax.ShapeDtypeStruct((ns, rows, fs), BF16)
    return _hosted_call(
        body, name=name, grid=(ns, rows // tm),
        in_specs=[pl.BlockSpec((tm, d), lambda s, i: (i, 0)), pl.BlockSpec((1, fs, d), lambda s, i: (s, 0, 0)), ospec],
        out_specs=[ospec, ospec], out_shape=[oshape, oshape],
        dims=("parallel", "parallel"), args=(u, wu, gate), comm=comm)


def _loss_tail(x, g, t, row, d):
    r = _rms(x, d)
    n = x * r
    err = jnp.where(row >= CHUNK, n * g - t, 0.0)
    dout = err * (1.0 / d)
    dh = _rms_bwd(dout * g, n, r, d)
    dg = jnp.sum(dout * n, axis=0, keepdims=True)
    part = jnp.sum(jnp.sum(err * err, axis=1, keepdims=True), axis=0, keepdims=True) * (0.5 / d)
    return dh, dg, jnp.broadcast_to(part, (1, 128))


def _ffn_down_call(name, a, wd, h, tm, comm=None, next_gain=None):
    rows, d = h.shape
    ns, _, fs = a.shape
    more = next_gain is not None

    def body(a_ref, wd_ref, h_ref, *rest):
        acc = h_ref[...]
        for s in range(ns):
            acc = acc + 0.5 * _dot(a_ref[s], wd_ref[s])
        rest[-2 if more else -1][...] = acc
        if more:
            rest[-1][...] = (acc * _rms(acc, d) * rest[0][...]).astype(BF16)

    full = pl.BlockSpec((tm, d), lambda i: (i, 0))
    res = _hosted_call(
        body, name=name, grid=(rows // tm,),
        in_specs=[pl.BlockSpec((ns, tm, fs), lambda i: (0, i, 0)), _resident((ns, fs, d)), full]
        + ([pl.BlockSpec((1, d), lambda i: (0, 0))] if more else []),
        out_specs=[full] * (2 if more else 1),
        out_shape=[jax.ShapeDtypeStruct((rows, d), F32)] + ([jax.ShapeDtypeStruct((rows, d), BF16)] if more else []),
        dims=("parallel",), args=(a, wd, h) + ((next_gain,) if more else ()), comm=comm)
    return res if more else res[0]


def _ffn_down_loss_call(name, a, wd, h, g, target, lp, tm):
    rows, d = h.shape
    ns, _, fs = a.shape
    tpe = lp // tm

    def body(a_ref, wd_ref, h_ref, g_ref, t_ref, dh_ref, dhb_ref, dg_ref, loss_ref):
        i = pl.program_id(0)
        acc = h_ref[...]
        for s in range(ns):
            acc = acc + 0.5 * _dot(a_ref[s], wd_ref[s])
        row = (i % tpe) * tm + lax.broadcasted_iota(jnp.int32, (tm, 1), 0)
        dh, dg, loss = _loss_tail(acc, g_ref[...], t_ref[...], row, d)
        dh_ref[...] = dh
        dhb_ref[...] = dh.astype(BF16)

        @pl.when(i == 0)
        def _():
            dg_ref[...] = dg
            loss_ref[...] = loss

        @pl.when(i != 0)
        def _():
            dg_ref[...] += dg
            loss_ref[...] += loss

    full = pl.BlockSpec((tm, d), lambda i: (i, 0))
    gspec = pl.BlockSpec((1, d), lambda i: (0, 0))
    return _hosted_call(
        body, name=name, grid=(rows // tm,),
        in_specs=[pl.BlockSpec((ns, tm, fs), lambda i: (0, i, 0)), _resident((ns, fs, d)), full, gspec, full],
        out_specs=[full, full, gspec, pl.BlockSpec((1, 128), lambda i: (0, 0))],
        out_shape=[jax.ShapeDtypeStruct((rows, d), F32), jax.ShapeDtypeStruct((rows, d), BF16),
                   jax.ShapeDtypeStruct((1, d), F32), jax.ShapeDtypeStruct((1, 128), F32)],
        args=(a, wd, h, g, target))


def _mm_call(name, a, bt, tm, out_dtype):
    rows, k = a.shape
    n = bt.shape[0]

    def body(a_ref, b_ref, o_ref):
        o_ref[...] = _dot_nt(a_ref[...], b_ref[...]).astype(out_dtype)

    return pl.pallas_call(
        body, name=name, grid=(rows // tm,),
        in_specs=[pl.BlockSpec((tm, k), lambda i: (i, 0)), pl.BlockSpec((n, k), lambda i: (0, 0))],
        out_specs=pl.BlockSpec((tm, n), lambda i: (i, 0)),
        out_shape=jax.ShapeDtypeStruct((rows, n), out_dtype),
        compiler_params=_params(("parallel",)))(a, bt)


def _mla_heads(z, gql, gkvl, wuq, wuk, wuv):
    cq = z[:, 0:Q_RANK]
    ckv = z[:, Q_RANK:Z_KR]
    kr = z[:, Z_KR:Z_MLA]
    rq = _rms(cq, Q_RANK)
    nq = cq * rq
    cqn = (nq * gql).astype(BF16)
    rkv = _rms(ckv, KV_RANK)
    nkv = ckv * rkv
    ckvn = (nkv * gkvl).astype(BF16)
    qraw = _dot_nt(cqn, wuq)
    knope = _dot(ckvn, wuk)
    v = _dot(ckvn, wuv)
    skr = jnp.sum(kr * kr, axis=-1, keepdims=True)
    heads = []
    for hd in range(HEADS):
        qh = qraw[:, hd * D_QKP:(hd + 1) * D_QKP]
        rqh = lax.rsqrt(jnp.sum(qh * qh, axis=-1, keepdims=True) * (1.0 / D_QK) + EPS)
        kn = knope[:, hd * D_NOPE:(hd + 1) * D_NOPE]
        rkh = lax.rsqrt((jnp.sum(kn * kn, axis=-1, keepdims=True) + skr) * (1.0 / D_QK) + EPS)
        heads.append((qh * rqh, rqh, kn * rkh, kr * rkh, rkh))
    return dict(rq=rq, nq=nq, cqn=cqn, rkv=rkv, nkv=nkv, ckvn=ckvn, v=v, heads=heads)


def _mla_prep_call(z, gql, gkvl, gqh, gkh, wuq, wuk, wuv, tabs, lp, tm):
    rows = z.shape[0]
    tpe = lp // tm

    def body(z_ref, gql_ref, gkvl_ref, gqh_ref, gkh_ref, wuq_ref, wuk_ref, wuv_ref, c_ref, s1_ref, s2_ref,
             q_ref, k_ref, v_ref, cqn_ref, ckvn_ref):
        m = _mla_heads(z_ref[...], gql_ref[...], gkvl_ref[...], wuq_ref[...], wuk_ref[...], wuv_ref[...])
        c, s1, s2 = c_ref[...], s1_ref[...], s2_ref[...]
        gq, gk = gqh_ref[...], gkh_ref[...]
        row = (pl.program_id(0) % tpe) * tm + lax.broadcasted_iota(jnp.int32, (tm, 1), 0)
        spare = (lax.broadcasted_iota(jnp.int32, (1, D_QKP - D_NOPE), 1) == D_ROPE).astype(F32)
        kmask = jnp.where(row < PAD_ROWS, NEG_INF * math.sqrt(D_QK), 0.0) * spare
        for hd in range(HEADS):
            qn, _, knn, krn, _ = m["heads"][hd]
            qg = qn * gq
            q_ref[hd, :, 0:D_NOPE] = qg[:, 0:D_NOPE].astype(BF16)
            q_ref[hd, :, D_NOPE:D_QKP] = (_rope(qg[:, D_NOPE:D_QKP], c, s1, s2) + spare).astype(BF16)
            k_ref[hd, :, 0:D_NOPE] = (knn * gk[:, 0:D_NOPE]).astype(BF16)
            k_ref[hd, :, D_NOPE:D_QKP] = (_rope(krn * gk[:, D_NOPE:D_QKP], c, s1, s2) + kmask).astype(BF16)
            v_ref[hd] = m["v"][:, hd * D_V:(hd + 1) * D_V].astype(BF16)
        cqn_ref[...] = m["cqn"]
        ckvn_ref[...] = m["ckvn"]

    def const(shape):
        return pl.BlockSpec(shape, lambda i: tuple(0 for _ in shape))

    tab = pl.BlockSpec((tm, 128), lambda i: (i % tpe, 0))
    return pl.pallas_call(
        body, name="mla_prep", grid=(rows // tm,),
        in_specs=[pl.BlockSpec((tm, Z_MLA), lambda i: (i, 0)), const((1, Q_RANK)), const((1, KV_RANK)),
                  const((1, D_QKP)), const((1, D_QKP)), const((HEADS * D_QKP, Q_RANK)),
                  const((KV_RANK, HEADS * D_NOPE)), const((KV_RANK, HEADS * D_V)), tab, tab, tab],
        out_specs=[pl.BlockSpec((HEADS, tm, D_QKP), lambda i: (0, i, 0)),
                   pl.BlockSpec((HEADS, tm, D_QKP), lambda i: (0, i, 0)),
                   pl.BlockSpec((HEADS, tm, D_V), lambda i: (0, i, 0)),
                   pl.BlockSpec((tm, Q_RANK), lambda i: (i, 0)),
                   pl.BlockSpec((tm, KV_RANK), lambda i: (i, 0))],
        out_shape=[jax.ShapeDtypeStruct((HEADS, rows, D_QKP), BF16),
                   jax.ShapeDtypeStruct((HEADS, rows, D_QKP), BF16),
                   jax.ShapeDtypeStruct((HEADS, rows, D_V), BF16),
                   jax.ShapeDtypeStruct((rows, Q_RANK), BF16),
                   jax.ShapeDtypeStruct((rows, KV_RANK), BF16)],
        compiler_params=_params(("parallel",)))(z, gql, gkvl, gqh, gkh, wuq, wuk, wuv, *tabs)


Q_BLOCK_ROWS = 528
Q_BLOCK_ROWS_BWD = 192


def _q_block(lp):
    return _row_tile(lp, Q_BLOCK_ROWS)


def _key_end(ext, lp):
    return min(lp, -(-ext // CHUNK) * CHUNK)


def _diag_bias(qb, j0, nk):
    shift = CHUNK.bit_length() - 1
    r = jnp.right_shift(j0 + lax.broadcasted_iota(jnp.int32, (qb, nk), 0), shift)
    c = jnp.right_shift(j0 + lax.broadcasted_iota(jnp.int32, (qb, nk), 1), shift)
    return jnp.where(c <= r, 0.0, NEG_INF)


def _attn_fwd_call(q, k, v, nb, lp, comm=None):
    rows = nb * lp
    qb = _q_block(lp)
    scale = 1.0 / math.sqrt(D_QK)

    def body(q_ref, k_ref, v_ref, o_ref, lse_ref):
        for j in range(lp // qb):
            j0, ext = j * qb, (j + 1) * qb
            kend = _key_end(ext, lp)
            qj = q_ref[0, j0:ext, :]
            sd = _dot_nt(qj, k_ref[0, j0:kend, :]) * scale + _diag_bias(qb, j0, kend - j0)
            mx = jnp.max(sd, axis=-1, keepdims=True)
            if j > 0:
                so = _dot_nt(qj, k_ref[0, 0:j0, :]) * scale
                mx = jnp.maximum(mx, jnp.max(so, axis=-1, keepdims=True))
            pd = jnp.exp(sd - mx)
            l = jnp.sum(pd, axis=-1, keepdims=True)
            o = _dot(pd.astype(BF16), v_ref[0, j0:kend, :])
            if j > 0:
                po = jnp.exp(so - mx)
                l = l + jnp.sum(po, axis=-1, keepdims=True)
                o = o + _dot(po.astype(BF16), v_ref[0, 0:j0, :])
            o_ref[j0:ext, :] = o / l
            lse_ref[0, j0:ext, :] = mx + jnp.log(l)

    return _hosted_call(
        body, name="attn_fwd", grid=(nb, HEADS),
        in_specs=[pl.BlockSpec((1, lp, D_QKP), lambda b, h: (h, b, 0)),
                  pl.BlockSpec((1, lp, D_QKP), lambda b, h: (h, b, 0)),
                  pl.BlockSpec((1, lp, D_V), lambda b, h: (h, b, 0))],
        out_specs=[pl.BlockSpec((lp, D_V), lambda b, h: (b, h)),
                   pl.BlockSpec((1, lp, 1), lambda b, h: (h, b, 0))],
        out_shape=[jax.ShapeDtypeStruct((rows, MLA_W), F32),
                   jax.ShapeDtypeStruct((HEADS, rows, 1), F32)],
        dims=("parallel", "parallel"), args=(q, k, v), comm=comm)


def _attn_bwd_call(q, k, v, o, lse, do, nb, lp, comm=None):
    rows = nb * lp
    qb = _row_tile(lp, Q_BLOCK_ROWS_BWD)
    scale = 1.0 / math.sqrt(D_QK)

    def body(q_ref, k_ref, v_ref, o_ref, lse_ref, do_ref, dq_ref, dk_ref, dv_ref, dk_acc, dv_acc):
        dk_acc[...] = jnp.zeros_like(dk_acc)
        dv_acc[...] = jnp.zeros_like(dv_acc)
        shift = CHUNK.bit_length() - 1
        for j in range(lp // qb):
            j0, ext = j * qb, (j + 1) * qb
            kend = _key_end(ext, lp)
            qj = q_ref[0, j0:ext, :]
            doj = do_ref[j0:ext, :]
            delta = jnp.sum(doj * o_ref[j0:ext, :], axis=-1, keepdims=True)
            dob = doj.astype(BF16)
            kk = k_ref[0, 0:kend, :]
            qchunk = jnp.right_shift(j0 + lax.broadcasted_iota(jnp.int32, (qb, kend), 0), shift)
            kchunk = jnp.right_shift(lax.broadcasted_iota(jnp.int32, (qb, kend), 1), shift)
            s = _dot_nt(qj, kk) * scale - lse_ref[0, j0:ext, :]
            p = jnp.where(kchunk <= qchunk, jnp.exp(s), 0.0)
            dv_acc[0:kend, :] += _dot_tn(p.astype(BF16), dob)
            dp = _dot_nt(dob, v_ref[0, 0:kend, :])
            ds = (p * (dp - delta) * scale).astype(BF16)
            dq_ref[0, j0:ext, :] = _dot(ds, kk).astype(BF16)
            dk_acc[0:kend, :] += _dot_tn(ds, qj)
        dk_ref[0] = dk_acc[...].astype(BF16)
        dv_ref[0] = dv_acc[...].astype(BF16)

    qspec = pl.BlockSpec((1, lp, D_QKP), lambda b, h: (h, b, 0))
    vspec = pl.BlockSpec((1, lp, D_V), lambda b, h: (h, b, 0))
    ospec = pl.BlockSpec((lp, D_V), lambda b, h: (b, h))
    return _hosted_call(
        body, name="attn_bwd", grid=(nb, HEADS),
        in_specs=[qspec, qspec, vspec, ospec, pl.BlockSpec((1, lp, 1), lambda b, h: (h, b, 0)), ospec],
        out_specs=[qspec, qspec, vspec],
        out_shape=[jax.ShapeDtypeStruct((HEADS, rows, D_QKP), BF16),
                   jax.ShapeDtypeStruct((HEADS, rows, D_QKP), BF16),
                   jax.ShapeDtypeStruct((HEADS, rows, D_V), BF16)],
        scratch_shapes=[pltpu.VMEM((lp, D_QKP), F32), pltpu.VMEM((lp, D_V), F32)],
        dims=("parallel", "parallel"), args=(q, k, v, o, lse, do), comm=comm)


def _lru_gates(u, cw, cb, wa, ba, wx, bx, lam, lp):
    xc = (cw[3:4, :] * u + cw[2:3, :] * pltpu.roll(u, 1, 0) + cw[1:2, :] * pltpu.roll(u, 2, 0)
          + cw[0:1, :] * pltpu.roll(u, 3, 0) + cb)
    xcb = xc.astype(BF16)
    r = _sigmoid(_dot(xcb, wa) + ba)
    i = _sigmoid(_dot(xcb, wx) + bx)
    sp = _softplus_neg(lam)
    la = -C_RGLRU * r * sp
    a = jnp.exp(la)
    x2 = 2.0 * la
    e2 = a * a
    m2 = jnp.maximum(jnp.where(x2 > -0.01, -x2 * (1.0 + 0.5 * x2), 1.0 - e2), 1e-30)
    rs = lax.rsqrt(m2)
    row = lax.broadcasted_iota(jnp.int32, (lp, LRU_TILE), 0)
    first = row == PAD_ROWS
    valid = row >= PAD_ROWS
    mult_eff = jnp.where(first, 1.0, m2 * rs)
    return dict(xc=xc, xcb=xcb, r=r, i=i, sp=sp, a=a, e2=e2, rs=rs, mult_eff=mult_eff, first=first, valid=valid)


def _scan_rows(a, b, a_s, b_s, out_ref, lp, reverse):
    sub = lax.broadcasted_iota(jnp.int32, (lp, LRU_TILE), 0) & 7
    for dist in (1, 2, 4):
        shift = lp - dist if reverse else dist
        keep = (sub + dist <= 7) if reverse else (sub >= dist)
        a_sh = pltpu.roll(a, shift, 0)
        b_sh = pltpu.roll(b, shift, 0)
        b = jnp.where(keep, a * b_sh + b, b)
        a = jnp.where(keep, a * a_sh, a)
    a_s[...] = a
    b_s[...] = b
    n_groups = lp // 8
    edge = 0 if reverse else 7

    def group(gi, carry):
        r0 = pl.multiple_of(((n_groups - 1 - gi) if reverse else gi) * 8, 8)
        a8 = a_s[pl.ds(r0, 8), :]
        b8 = b_s[pl.ds(r0, 8), :]
        out_ref[pl.ds(r0, 8), :] = a8 * carry + b8
        return a8[edge:edge + 1, :] * carry + b8[edge:edge + 1, :]

    lax.fori_loop(0, n_groups, group, jnp.zeros((1, LRU_TILE), F32), unroll=4)


def _lru_specs(lp):
    seq = lambda col0: pl.BlockSpec((lp, LRU_TILE), lambda t, b: (b, col0 + t))
    cw = pl.BlockSpec((1, CONV_K, LRU_TILE), lambda t, b: (t, 0, 0))
    vec = pl.BlockSpec((1, LRU_TILE), lambda t, b: (0, t))
    mat = pl.BlockSpec((1, LRU_TILE, LRU_TILE), lambda t, b: (t, 0, 0))
    return seq, cw, vec, mat


def _lru_fwd_call(z, cw, cb, wa, ba, wx, bx, lam, nb, lp, comm=None):
    rows = nb * lp
    seq, cwspec, vec, mat = _lru_specs(lp)

    def body(u_ref, g_ref, cw_ref, cb_ref, wa_ref, ba_ref, wx_ref, bx_ref, lam_ref, y_ref, hs_ref, a_s, b_s):
        m = _lru_gates(u_ref[...], cw_ref[0], cb_ref[...], wa_ref[0], ba_ref[...], wx_ref[0], bx_ref[...],
                       lam_ref[...], lp)
        a = jnp.where(m["valid"], m["a"], 0.0)
        b = jnp.where(m["valid"], m["mult_eff"] * (m["i"] * m["xc"]), 0.0)
        _scan_rows(a, b, a_s, b_s, hs_ref, lp, reverse=False)
        gl, _ = _gelu(g_ref[...])
        y_ref[...] = hs_ref[...] * gl

    oshape = jax.ShapeDtypeStruct((rows, LRU_W), F32)
    return _hosted_call(
        body, name="lru_fwd", grid=(N_LRU_TILES, nb),
        in_specs=[seq(Z_U // LRU_TILE), seq(Z_G // LRU_TILE), cwspec, vec, mat, vec, mat, vec, vec],
        out_specs=[seq(0), seq(0)], out_shape=[oshape, oshape],
        scratch_shapes=[pltpu.VMEM((lp, LRU_TILE), F32), pltpu.VMEM((lp, LRU_TILE), F32)],
        dims=("parallel", "parallel"), args=(z, z, cw, cb, wa, ba, wx, bx, lam), comm=comm)


def _lru_bwd_call(z, hs, dy, cw, cb, wa, ba, wx, bx, lam, nb, lp, comm=None):
    rows = nb * lp
    seq, cwspec, vec, mat = _lru_specs(lp)

    def body(u_ref, g_ref, hs_ref, dy_ref, cw_ref, cb_ref, wa_ref, ba_ref, wx_ref, bx_ref, lam_ref,
             du_ref, dg_ref, dcw_ref, dcb_ref, dwa_ref, dba_ref, dwx_ref, dbx_ref, dlam_ref, a_s, b_s, d_s):
        b_idx = pl.program_id(1)
        u = u_ref[...]
        cw = cw_ref[0]
        wa, wx = wa_ref[0], wx_ref[0]
        lam = lam_ref[...]
        m = _lru_gates(u, cw, cb_ref[...], wa, ba_ref[...], wx, bx_ref[...], lam, lp)
        gate = g_ref[...]
        gl, th = _gelu(gate)
        dy = dy_ref[...]
        hs = hs_ref[...]
        dg_ref[...] = (dy * hs * _gelu_grad(gate, th)).astype(BF16)
        a_eff = jnp.where(m["valid"], m["a"], 0.0)
        _scan_rows(pltpu.roll(a_eff, lp - 1, 0), dy * gl, a_s, b_s, d_s, lp, reverse=True)
        ds = d_s[...]
        xc, r, i = m["xc"], m["r"], m["i"]
        row = lax.broadcasted_iota(jnp.int32, (lp, LRU_TILE), 0)
        da = ds * jnp.where(row >= 1, pltpu.roll(hs, 1, 0), 0.0)
        db = jnp.where(m["valid"], ds, 0.0)
        di = db * m["mult_eff"] * xc
        dxc = db * m["mult_eff"] * i
        live = m["valid"] & jnp.logical_not(m["first"])
        dm = jnp.where(live, db * i * xc, 0.0)
        dla = da * m["a"] - dm * (m["e2"] * m["rs"])
        dr = dla * (-C_RGLRU * m["sp"])
        dsp = jnp.sum(dla * (-C_RGLRU * r), axis=0, keepdims=True)
        dpr = (dr * r * (1.0 - r))
        dpi = (di * i * (1.0 - i))
        dprb, dpib = dpr.astype(BF16), dpi.astype(BF16)
        dxc = dxc + _dot_nt(dprb, wa) + _dot_nt(dpib, wx)
        du = (cw[3:4, :] * dxc + cw[2:3, :] * pltpu.roll(dxc, lp - 1, 0) + cw[1:2, :] * pltpu.roll(dxc, lp - 2, 0)
              + cw[0:1, :] * pltpu.roll(dxc, lp - 3, 0))
        du_ref[...] = jnp.where(m["valid"], du, 0.0).astype(BF16)
        tap = lax.broadcasted_iota(jnp.int32, (CONV_K, LRU_TILE), 0)
        dcw = jnp.zeros((CONV_K, LRU_TILE), F32)
        for kk in range(CONV_K):
            shifted = u if kk == CONV_K - 1 else pltpu.roll(u, CONV_K - 1 - kk, 0)
            dcw = jnp.where(tap == kk, jnp.sum(dxc * shifted, axis=0, keepdims=True), dcw)
        parts = [(dcw_ref, dcw[None]), (dcb_ref, jnp.sum(dxc, axis=0, keepdims=True)[None]),
                 (dwa_ref, _dot_tn(m["xcb"], dprb)[None]), (dba_ref, jnp.sum(dpr, axis=0, keepdims=True)[None]),
                 (dwx_ref, _dot_tn(m["xcb"], dpib)[None]), (dbx_ref, jnp.sum(dpi, axis=0, keepdims=True)[None]),
                 (dlam_ref, (dsp * (-jax.nn.sigmoid(-lam)))[None])]

        @pl.when(b_idx == 0)
        def _():
            for ref, val in parts:
                ref[...] = val

        @pl.when(b_idx != 0)
        def _():
            for ref, val in parts:
                ref[...] += val

    bshape = jax.ShapeDtypeStruct((rows, LRU_W), BF16)
    vec3 = pl.BlockSpec((1, 1, LRU_TILE), lambda t, b: (t, 0, 0))
    vshape = jax.ShapeDtypeStruct((N_LRU_TILES, 1, LRU_TILE), F32)
    mshape = jax.ShapeDtypeStruct((N_LRU_TILES, LRU_TILE, LRU_TILE), F32)
    return _hosted_call(
        body, name="lru_bwd", grid=(N_LRU_TILES, nb),
        in_specs=[seq(Z_U // LRU_TILE), seq(Z_G // LRU_TILE), seq(0), seq(0), cwspec, vec, mat, vec, mat, vec, vec],
        out_specs=[seq(0), seq(0), cwspec, vec3, mat, vec3, mat, vec3, vec3],
        out_shape=[bshape, bshape, jax.ShapeDtypeStruct((N_LRU_TILES, CONV_K, LRU_TILE), F32), vshape, mshape,
                   vshape, mshape, vshape, vshape],
        scratch_shapes=[pltpu.VMEM((lp, LRU_TILE), F32)] * 3,
        dims=("parallel", "arbitrary"), args=(z, z, hs, dy, cw, cb, wa, ba, wx, bx, lam), comm=comm)


def _mix_out_call(ya, yl, ga, gl, wout, h, next_gain, tm):
    rows, d = h.shape

    def body(ya_ref, yl_ref, ga_ref, gl_ref, w_ref, h_ref, ng_ref, y_ref, o_ref, u_ref):
        a = ya_ref[...]
        l = yl_ref[...]
        an = (a * _rms(a, MLA_W) * ga_ref[...]).astype(BF16)
        ln = (l * _rms(l, LRU_W) * gl_ref[...]).astype(BF16)
        y_ref[:, 0:MLA_W] = an
        y_ref[:, MLA_W:MLA_W + LRU_W] = ln
        out = h_ref[...] + _dot(an, w_ref[0:MLA_W, :]) + _dot(ln, w_ref[MLA_W:MLA_W + LRU_W, :])
        o_ref[...] = out
        u_ref[...] = (out * _rms(out, d) * ng_ref[...]).astype(BF16)

    half = pl.BlockSpec((tm, MLA_W), lambda i: (i, 0))
    g = pl.BlockSpec((1, MLA_W), lambda i: (0, 0))
    full = pl.BlockSpec((tm, d), lambda i: (i, 0))
    return pl.pallas_call(
        body, name="mix_out", grid=(rows // tm,),
        in_specs=[half, half, g, g, pl.BlockSpec((MLA_W + LRU_W, d), lambda i: (0, 0)), full,
                  pl.BlockSpec((1, d), lambda i: (0, 0))],
        out_specs=[full, full, full],
        out_shape=[jax.ShapeDtypeStruct((rows, MLA_W + LRU_W), BF16), jax.ShapeDtypeStruct((rows, d), F32),
                   jax.ShapeDtypeStruct((rows, d), BF16)],
        compiler_params=_params(("parallel",)))(ya, yl, ga, gl, wout, h, next_gain)


def _ffn_dact_call(name, dhb, wd, gate, up, tm, comm=None):
    rows, d = dhb.shape
    ns, fs, _ = wd.shape

    nsub = 2 if tm % 32 == 0 else 1
    sub = tm // nsub

    def body(dh_ref, wd_ref, g_ref, p_ref, dg_ref, dp_ref):
        wd = wd_ref[0]
        for r in range(nsub):
            rs = slice(r * sub, (r + 1) * sub)
            da = (0.5 * _dot_nt(dh_ref[rs, :], wd)).astype(BF16)
            g = g_ref[0, rs, :]
            p = p_ref[0, rs, :]
            sg = jax.nn.sigmoid(g)
            dg_ref[0, rs, :] = (da * p) * (sg * (1.0 + g * (1.0 - sg)))
            dp_ref[0, rs, :] = da * (g * sg)

    aspec = pl.BlockSpec((1, tm, fs), lambda s, i: (s, i, 0))
    oshape = jax.ShapeDtypeStruct((ns, rows, fs), BF16)
    return _hosted_call(
        body, name=name, grid=(ns, rows // tm),
        in_specs=[pl.BlockSpec((tm, d), lambda s, i: (i, 0)), pl.BlockSpec((1, fs, d), lambda s, i: (s, 0, 0)),
                  aspec, aspec],
        out_specs=[aspec, aspec], out_shape=[oshape, oshape],
        dims=("parallel", "parallel"), args=(dhb, wd, gate, up), comm=comm)


def _norm_in_bwd_call(name, pieces, h, g, dres, tm, comm=None, part=(0, 1), prev=None, mix=None):
    rows, d = h.shape
    npc = len(pieces)
    steps = rows // tm // part[1]
    off = part[0] * steps
    n_prev = 0 if prev is None else 2
    n_mix = 0 if mix is None else 5

    def body(*refs):
        d_refs = refs[0:2 * npc:2]
        w_refs = refs[1:2 * npc:2]
        h_ref, g_ref, dres_ref = refs[2 * npc:2 * npc + 3]
        mix_in_refs = refs[2 * npc + 3:2 * npc + 3 + n_mix]
        dh_ref, dhb_ref, dg_ref = refs[2 * npc + 3 + n_mix + n_prev:2 * npc + 6 + n_mix + n_prev]
        mix_out_refs = refs[2 * npc + 6 + n_mix + n_prev:]
        du = jnp.zeros((tm, d), F32)
        for d_ref, w_ref in zip(d_refs, w_refs):
            if len(d_ref.shape) == 3:
                for s in range(d_ref.shape[0]):
                    du = du + _dot(d_ref[s], w_ref[s])
            else:
                du = du + _dot(d_ref[...], w_ref[...])
        x = h_ref[...]
        r = _rms(x, d)
        n = x * r
        dh = dres_ref[...] + _rms_bwd(du * g_ref[...], n, r, d)
        dhb = dh.astype(BF16)
        dh_ref[...] = dh
        dhb_ref[...] = dhb
        sums = [(dg_ref, jnp.sum(du * n, axis=0, keepdims=True))]
        if mix is not None:
            wo_ref, ya_ref, yl_ref, ga_ref, gl_ref = mix_in_refs
            dya_ref, dyl_ref, dga_ref, dgl_ref = mix_out_refs
            dy = _dot_nt(dhb, wo_ref[...])
            for val, gain_ref, lo, out_ref, acc_ref in ((ya_ref[...], ga_ref, 0, dya_ref, dga_ref),
                                                        (yl_ref[...], gl_ref, MLA_W, dyl_ref, dgl_ref)):
                rb = _rms(val, MLA_W)
                nb_ = val * rb
                dyn = dy[:, lo:lo + MLA_W]
                out_ref[...] = _rms_bwd(dyn * gain_ref[...], nb_, rb, MLA_W)
                sums.append((acc_ref, jnp.sum(dyn * nb_, axis=0, keepdims=True)))

        @pl.when(pl.program_id(0) == 0)
        def _():
            for ref, val in sums:
                ref[...] = val

        @pl.when(pl.program_id(0) != 0)
        def _():
            for ref, val in sums:
                ref[...] += val

    in_specs, args = [], []
    for dd, w in pieces:
        if dd.ndim == 3:
            in_specs.append(pl.BlockSpec((dd.shape[0], tm, dd.shape[2]), lambda i: (0, i + off, 0)))
            in_specs.append(_resident(w.shape))
        else:
            in_specs.append(pl.BlockSpec((tm, dd.shape[1]), lambda i: (i + off, 0)))
            in_specs.append(_resident(w.shape))
        args += [dd, w]
    full = pl.BlockSpec((tm, d), lambda i: (i + off, 0))
    gspec = pl.BlockSpec((1, d), lambda i: (0, 0))
    half = pl.BlockSpec((tm, MLA_W), lambda i: (i + off, 0))
    hgain = pl.BlockSpec((1, MLA_W), lambda i: (0, 0))
    mix_in_specs = [] if mix is None else [_resident(mix[0].shape), half, half, hgain, hgain]
    mix_out_specs = [] if mix is None else [half, half, hgain, hgain]
    mix_out_shapes = [] if mix is None else [
        jax.ShapeDtypeStruct((rows, MLA_W), F32), jax.ShapeDtypeStruct((rows, LRU_W), F32),
        jax.ShapeDtypeStruct((1, MLA_W), F32), jax.ShapeDtypeStruct((1, LRU_W), F32)]
    n_in = len(in_specs) + 3 + n_mix
    return _hosted_call(
        body, name=name, grid=(steps,),
        in_specs=in_specs + [full, gspec, full] + mix_in_specs + _any_specs(n_prev),
        out_specs=[full, full, gspec] + mix_out_specs,
        out_shape=[jax.ShapeDtypeStruct((rows, d), F32), jax.ShapeDtypeStruct((rows, d), BF16),
                   jax.ShapeDtypeStruct((1, d), F32)] + mix_out_shapes,
        args=(*args, h, g, dres, *(mix or ()), *(prev or ())), comm=comm,
        aliases={n_in: 0, n_in + 1: 1} if prev is not None else {})


def _wgrad_call(name, a, b, scale=1.0, comm=None):
    a3, b3 = a.ndim == 3, b.ndim == 3
    ns = a.shape[0] if a3 else (b.shape[0] if b3 else 1)
    rows, m = a.shape[-2:]
    n = b.shape[-1]
    tmm = m if a3 else _col_tile(m, 256)

    def body(a_ref, b_ref, o_ref):
        av = a_ref[0] if a3 else a_ref[...]
        bv = b_ref[0] if b3 else b_ref[...]
        res = _dot_tn(av, bv)
        if scale != 1.0:
            res = res * scale
        if a3 or b3:
            o_ref[0] = res
        else:
            o_ref[...] = res

    aspec = (pl.BlockSpec((1, rows, tmm), lambda s, j: (s, 0, j)) if a3
             else pl.BlockSpec((rows, tmm), lambda s, j: (0, j)))
    bspec = (pl.BlockSpec((1, rows, n), lambda s, j: (s, 0, 0)) if b3
             else pl.BlockSpec((rows, n), lambda s, j: (0, 0)))
    if a3 or b3:
        ospec = pl.BlockSpec((1, tmm, n), lambda s, j: (s, j, 0))
        oshape = jax.ShapeDtypeStruct((ns, m, n), F32)
    else:
        ospec = pl.BlockSpec((tmm, n), lambda s, j: (j, 0))
        oshape = jax.ShapeDtypeStruct((m, n), F32)
    return _hosted_call(
        body, name=name, grid=(ns, m // tmm), in_specs=[aspec, bspec], out_specs=[ospec], out_shape=[oshape],
        dims=("parallel", "parallel"), args=(a, b), comm=comm)[0]


def _mla_prep_bwd_call(z, dq, dk, dv, gql, gkvl, gqh, gkh, wuq, wuk, wuv, tabs, lp, tm, comm=None):
    rows = z.shape[0]
    tpe = lp // tm

    def body(z_ref, dq_ref, dk_ref, dv_ref, gql_ref, gkvl_ref, gqh_ref, gkh_ref, wuq_ref, wuk_ref, wuv_ref,
             c_ref, s1_ref, s2_ref, dz_ref, dqp_ref, dkn_ref, dvv_ref, dgql_ref, dgkvl_ref, dgqh_ref, dgkh_ref):
        gql, gkvl = gql_ref[...], gkvl_ref[...]
        gq, gk = gqh_ref[...], gkh_ref[...]
        wuq, wuk, wuv = wuq_ref[...], wuk_ref[...], wuv_ref[...]
        m = _mla_heads(z_ref[...], gql, gkvl, wuq, wuk, wuv)
        c, s1, s2 = c_ref[...], s1_ref[...], s2_ref[...]
        dgq = jnp.zeros((1, D_QKP), F32)
        dgk = jnp.zeros((1, D_QKP), F32)
        dkr = jnp.zeros((tm, D_QKP - D_NOPE), F32)
        for hd in range(HEADS):
            qn, rqh, knn, krn, rkh = m["heads"][hd]
            dqg = jnp.concatenate([dq_ref[hd, :, 0:D_NOPE].astype(F32),
                                   _rope_t(dq_ref[hd, :, D_NOPE:D_QKP].astype(F32), c, s1, s2)], axis=1)
            dgq = dgq + jnp.sum(dqg * qn, axis=0, keepdims=True)
            dqn = dqg * gq
            dqr = rqh * (dqn - qn * (jnp.sum(dqn * qn, axis=-1, keepdims=True) * (1.0 / D_QK)))
            dqp_ref[:, hd * D_QKP:(hd + 1) * D_QKP] = dqr.astype(BF16)
            kn_full = jnp.concatenate([knn, krn], axis=1)
            dkg = jnp.concatenate([dk_ref[hd, :, 0:D_NOPE].astype(F32),
                                   _rope_t(dk_ref[hd, :, D_NOPE:D_QKP].astype(F32), c, s1, s2)], axis=1)
            dgk = dgk + jnp.sum(dkg * kn_full, axis=0, keepdims=True)
            dkn = dkg * gk
            dkraw = rkh * (dkn - kn_full * (jnp.sum(dkn * kn_full, axis=-1, keepdims=True) * (1.0 / D_QK)))
            dkn_ref[:, hd * D_NOPE:(hd + 1) * D_NOPE] = dkraw[:, 0:D_NOPE].astype(BF16)
            dkr = dkr + dkraw[:, D_NOPE:D_QKP]
            dvv_ref[:, hd * D_V:(hd + 1) * D_V] = dv_ref[hd]
        dcqn = _dot(dqp_ref[...], wuq)
        dckvn = _dot_nt(dkn_ref[...], wuk) + _dot_nt(dvv_ref[...], wuv)
        dz_ref[:, 0:Q_RANK] = _rms_bwd(dcqn * gql, m["nq"], m["rq"], Q_RANK).astype(BF16)
        dz_ref[:, Q_RANK:Z_KR] = _rms_bwd(dckvn * gkvl, m["nkv"], m["rkv"], KV_RANK).astype(BF16)
        dz_ref[:, Z_KR:Z_MLA] = dkr.astype(BF16)
        parts = [(dgql_ref, jnp.sum(dcqn * m["nq"], axis=0, keepdims=True)),
                 (dgkvl_ref, jnp.sum(dckvn * m["nkv"], axis=0, keepdims=True)), (dgqh_ref, dgq), (dgkh_ref, dgk)]

        @pl.when(pl.program_id(0) == 0)
        def _():
            for ref, val in parts:
                ref[...] = val

        @pl.when(pl.program_id(0) != 0)
        def _():
            for ref, val in parts:
                ref[...] += val

    def const(shape):
        return pl.BlockSpec(shape, lambda i: tuple(0 for _ in shape))

    tab = pl.BlockSpec((tm, 128), lambda i: (i % tpe, 0))
    hq = pl.BlockSpec((HEADS, tm, D_QKP), lambda i: (0, i, 0))
    hv = pl.BlockSpec((HEADS, tm, D_V), lambda i: (0, i, 0))

    def rowspec(n):
        return pl.BlockSpec((tm, n), lambda i: (i, 0))

    return _hosted_call(
        body, name="mla_prep_bwd", grid=(rows // tm,),
        in_specs=[rowspec(Z_MLA), hq, hq, hv, const((1, Q_RANK)), const((1, KV_RANK)), const((1, D_QKP)),
                  const((1, D_QKP)), const((HEADS * D_QKP, Q_RANK)), const((KV_RANK, HEADS * D_NOPE)),
                  const((KV_RANK, HEADS * D_V)), tab, tab, tab],
        out_specs=[rowspec(Z_MLA), rowspec(HEADS * D_QKP), rowspec(HEADS * D_NOPE), rowspec(HEADS * D_V),
                   const((1, Q_RANK)), const((1, KV_RANK)), const((1, D_QKP)), const((1, D_QKP))],
        out_shape=[jax.ShapeDtypeStruct((rows, Z_MLA), BF16), jax.ShapeDtypeStruct((rows, HEADS * D_QKP), BF16),
                   jax.ShapeDtypeStruct((rows, HEADS * D_NOPE), BF16), jax.ShapeDtypeStruct((rows, HEADS * D_V), BF16),
                   jax.ShapeDtypeStruct((1, Q_RANK), F32), jax.ShapeDtypeStruct((1, KV_RANK), F32),
                   jax.ShapeDtypeStruct((1, D_QKP), F32), jax.ShapeDtypeStruct((1, D_QKP), F32)],
        args=(z, dq, dk, dv, gql, gkvl, gqh, gkh, wuq, wuk, wuv, *tabs), comm=comm)


def _local_step(h0, target, w, nb, lp, sched=None):
    tm = _row_tile(nb * lp, 1408)
    te = _row_tile(lp, 512)
    tabs = _rope_tables(lp)
    g = {}
    if sched is None:
        host = lambda stage: None
    else:
        sched.g = g
        host = sched.host

    def ffn_act(tag, u, split):
        if split:
            gate = _ffn_gate_call(tag + "_gate", u, w[tag + "_w_gate"], tm, host(tag + "_gate"))
            up, act = _ffn_upact_call(tag + "_upact", u, w[tag + "_w_up"], gate, tm, host(tag + "_upact"))
        else:
            gate, up, act = _ffn_up_call(tag + "_up", u, w[tag + "_w_gate"], w[tag + "_w_up"], tm, host(tag + "_up"))
        return u, gate, up, act

    def ffn_bwd(tag, h, saved, dh, dhb, split, mix=None):
        u, gate, up, act = saved
        dgate, dup = _ffn_dact_call(tag + "_dact", dhb, w[tag + "_w_down"], gate, up, tm, host(tag + "_dact"))
        g[tag + "_w_down"] = _wgrad_call(tag + "_dwd", act, dhb, 0.5, host(tag + "_dwd"))
        g[tag + "_w_gate"] = _wgrad_call(tag + "_dwg", dgate, u, 1.0, host(tag + "_dwg"))
        g[tag + "_w_up"] = _wgrad_call(tag + "_dwu", dup, u, 1.0, host(tag + "_dwu"))
        pieces = [(dgate, w[tag + "_w_gate"]), (dup, w[tag + "_w_up"])]
        if not split:
            res = _norm_in_bwd_call(tag + "_din", pieces, h, w[tag + "_norm"], dh, te, host(tag + "_din"), mix=mix)
            g[tag + "_norm"] = res[2]
            return (res[0], res[1], *res[3:])
        dh_a, dhb_a, dg_a = _norm_in_bwd_call(tag + "_din_a", pieces, h, w[tag + "_norm"], dh, te,
                                              host(tag + "_din_a"), part=(0, 2))
        dh_in, dhb_in, dg_b = _norm_in_bwd_call(tag + "_din_b", pieces, h, w[tag + "_norm"], dh, te,
                                                host(tag + "_din_b"), part=(1, 2), prev=(dh_a, dhb_a))
        g[tag + "_norm"] = dg_a + dg_b
        return dh_in, dhb_in

    s1 = ffn_act("ffn1", _rmsnorm_call("ffn1_norm", h0, w["ffn1_norm"], te), True)
    h1, un = _ffn_down_call("ffn1_down", s1[3], w["ffn1_w_down"], h0, te, host("ffn1_down"), w["mix_norm"])
    z = _mm_call("mix_in", un, w["w_in"], tm, F32)
    mla_w = (w["q_latent_norm"], w["kv_latent_norm"], w["q_head_norm"], w["k_head_norm"], w["w_uq"], w["w_uk"],
             w["w_uv"])
    q, k, v, cqn, ckvn = _mla_prep_call(z, *mla_w, tabs, lp, te)
    o, lse = _attn_fwd_call(q, k, v, nb, lp, host("attn_fwd"))
    lru_w = (w["conv_w"], w["conv_b"], w["gate_a_w"], w["gate_a_b"], w["gate_x_w"], w["gate_x_b"], w["lru_lambda"])
    yl, hs = _lru_fwd_call(z, *lru_w, nb, lp, host("lru_fwd"))
    y, h2, u2 = _mix_out_call(o, yl, w["attn_out_norm"], w["lru_out_norm"], w["w_out"], h1, w["ffn2_norm"], te)
    s2 = ffn_act("ffn2", u2, False)
    dh3, dh3b, g["final_norm"], loss = _ffn_down_loss_call("ffn2_down", s2[3], w["ffn2_w_down"], h2, w["final_norm"],
                                                           target, lp, te)
    g["loss"] = loss

    dh2, dh2b, dya, dyl, g["attn_out_norm"], g["lru_out_norm"] = ffn_bwd(
        "ffn2", h2, s2, dh3, dh3b, False, (w["w_out"], o, yl, w["attn_out_norm"], w["lru_out_norm"]))
    g["w_out"] = _wgrad_call("dw_out", y, dh2b)
    dq, dk, dv = _attn_bwd_call(q, k, v, o, lse, dya, nb, lp, host("attn_bwd"))
    (dz_mla, dqp, dkn, dvv, g["q_latent_norm"], g["kv_latent_norm"], g["q_head_norm"],
     g["k_head_norm"]) = _mla_prep_bwd_call(z, dq, dk, dv, *mla_w, tabs, lp, te, host("mla_prep_bwd"))
    g["w_uq"] = _wgrad_call("dw_uq", dqp, cqn)
    g["w_uk"] = _wgrad_call("dw_uk", ckvn, dkn)
    g["w_uv"] = _wgrad_call("dw_uv", ckvn, dvv)
    (du, dgt, g["conv_w"], g["conv_b"], g["gate_a_w"], g["gate_a_b"], g["gate_x_w"], g["gate_x_b"],
     g["lru_lambda"]) = _lru_bwd_call(z, hs, dyl, *lru_w, nb, lp, host("lru_bwd"))
    win = w["w_in"]
    g["w_in"] = jnp.concatenate(
        [_wgrad_call("dw_in_mla", dz_mla, un), _wgrad_call("dw_in_u", du, un), _wgrad_call("dw_in_g", dgt, un)],
        axis=0)
    dh1, dh1b, g["mix_norm"] = _norm_in_bwd_call(
        "mix_din", [(dz_mla, win[0:Z_MLA]), (du, win[Z_U:Z_G]), (dgt, win[Z_G:Z_W])], h1, w["mix_norm"], dh2, te,
        host("mix_din"))
    dh0 = ffn_bwd("ffn1", h0, s1, dh1, dh1b, True)[0]
    return loss, dh0, g


def _place():
    x, y, c = lax.axis_index("x"), lax.axis_index("y"), lax.axis_index("c")
    return x, y, c, [(1 - x, y), (x, 1 - y), (1 - x, 1 - y)]


def _any_specs(n):
    return [pl.BlockSpec(memory_space=pl.ANY)] * n


def _remote(src, dst, sems, k, dev):
    send_sems, recv_sems, base = sems
    return pltpu.make_async_remote_copy(src_ref=src, dst_ref=dst, send_sem=send_sems.at[base + k],
                                        recv_sem=recv_sems.at[base + k], device_id=dev, device_id_type=MESH)


EW_VMEM_BYTES = 24 * 1024 * 1024


def _fit_rows(rows, cols, blocks):
    return _row_tile(rows, max(16, int(EW_VMEM_BYTES // (8 * blocks)) // cols))


class _Geom:
    def __init__(self, n0, n1, blocks=1.0):
        self.n0, self.n1 = n0, n1
        self.axis = 0 if n0 % 32 == 0 else 1
        self.h0, self.h1 = (n0 // 2, n1) if self.axis == 0 else (n0, n1 // 2)
        self.tr = _fit_rows(self.h0, self.h1, blocks)
        self.nblk = self.h0 // self.tr

    def half_ref(self, ref, lead, idx):
        if self.axis == 0:
            return ref.at[(*lead, pl.ds(idx * self.h0, self.h0))]
        return ref.at[(*lead, slice(None), pl.ds(idx * self.h1, self.h1))]

    def half_block(self, lead, i, idx):
        return (*lead, idx * self.nblk + i, 0) if self.axis == 0 else (*lead, i, idx)


class _Comm:
    def __init__(self, ins, out_shapes, aliases, n_sems, start, finish, deliver):
        self.ins, self.out_shapes, self.aliases, self.n_sems = list(ins), list(out_shapes), dict(aliases), n_sems
        self.start, self.finish, self.deliver = start, finish, deliver

    def scratch(self):
        return [pltpu.SemaphoreType.DMA((self.n_sems,)), pltpu.SemaphoreType.DMA((self.n_sems,))]


def _comm_call(name, comm):
    n_in = len(comm.ins)

    def body(*refs):
        ins, outs, sems = refs[:n_in], refs[n_in:-2], (*refs[-2:], 0)
        comm.start(ins, outs, sems)
        comm.finish(ins, outs, sems)

    res = pl.pallas_call(
        body, name=name, out_shape=comm.out_shapes, in_specs=_any_specs(n_in),
        out_specs=_any_specs(len(comm.out_shapes)), input_output_aliases=comm.aliases,
        scratch_shapes=comm.scratch())(*comm.ins)
    return comm.deliver(list(res))


def _hosted_call(body, *, name, grid, in_specs, out_specs, out_shape, args, scratch_shapes=(), dims=None, comm=None,
                 prefetch=None, aliases=None):
    aliases = dict(aliases or {})
    in_specs, out_specs, out_shape = list(in_specs), list(out_specs), list(out_shape)
    n_pre = 0 if prefetch is None else 1

    def call(fn, in_specs, out_specs, out_shape, scratch, aliases, dims, args):
        if prefetch is None:
            return pl.pallas_call(
                fn, name=name, grid=grid, in_specs=in_specs, out_specs=out_specs, out_shape=out_shape,
                scratch_shapes=scratch, input_output_aliases=aliases, compiler_params=_params(dims))(*args)
        spec = pltpu.PrefetchScalarGridSpec(num_scalar_prefetch=1, grid=grid, in_specs=in_specs, out_specs=out_specs,
                                            scratch_shapes=scratch)
        return pl.pallas_call(
            fn, name=name, grid_spec=spec, out_shape=out_shape,
            input_output_aliases={i + 1: o for i, o in aliases.items()}, compiler_params=_params(dims))(prefetch, *args)

    if comm is None:
        return list(call(body, in_specs, out_specs, out_shape, list(scratch_shapes), aliases,
                         dims or ("arbitrary",) * len(grid), args))
    n_in, n_out, n_ci, n_co = len(in_specs), len(out_specs), len(comm.ins), len(comm.out_shapes)

    def wrapped(*refs):
        pre, refs = refs[:n_pre], refs[n_pre:]
        ins, cins = refs[:n_in], refs[n_in:n_in + n_ci]
        outs = refs[n_in + n_ci:n_in + n_ci + n_out]
        couts = refs[n_in + n_ci + n_out:n_in + n_ci + n_out + n_co]
        scratch, sems = refs[n_in + n_ci + n_out + n_co:-2], (*refs[-2:], 0)
        first = functools.reduce(jnp.logical_and, [pl.program_id(k) == 0 for k in range(len(grid))])
        last = functools.reduce(jnp.logical_and, [pl.program_id(k) == grid[k] - 1 for k in range(len(grid))])

        @pl.when(first)
        def _():
            comm.start(cins, couts, sems)

        body(*pre, *ins, *outs, *scratch)

        @pl.when(last)
        def _():
            comm.finish(cins, couts, sems)

    res = call(wrapped, in_specs + _any_specs(n_ci), out_specs + _any_specs(n_co), out_shape + comm.out_shapes,
               list(scratch_shapes) + comm.scratch(),
               {**aliases, **{n_in + i: n_out + o for i, o in comm.aliases.items()}},
               ("arbitrary",) * len(grid), (*args, *comm.ins))
    comm.deliver(list(res[n_out:]))
    return list(res[:n_out])


def _gather_comm(bufs, deliver):
    n = len(bufs)
    geoms = [_Geom(*b.shape[1:]) for b in bufs]

    def first(outs, sems):
        x, y, c, chips = _place()
        cps = []
        for a in range(n):
            mine = geoms[a].half_ref(outs[a], (2 * x + y,), c)
            cps += [_remote(mine, mine, sems, 6 * a + j, (cx, cy, c)) for j, (cx, cy) in enumerate(chips)]
        return cps

    def start(ins, outs, sems):
        for cp in first(outs, sems):
            cp.start()

    def finish(ins, outs, sems):
        x, y, c, chips = _place()
        sib = (x, y, 1 - c)
        passed = []
        for a in range(n):
            for j, (cx, cy) in enumerate(chips):
                land = geoms[a].half_ref(outs[a], (2 * cx + cy,), c)
                _remote(land, land, sems, 6 * a + j, sib).wait_recv()
                cp = _remote(land, land, sems, 6 * a + 3 + j, sib)
                cp.start()
                passed.append(cp)
        for a in range(n):
            for j, (cx, cy) in enumerate(chips):
                land = geoms[a].half_ref(outs[a], (2 * cx + cy,), 1 - c)
                _remote(land, land, sems, 6 * a + 3 + j, sib).wait_recv()
        for cp in first(outs, sems) + passed:
            cp.wait_send()

    return _Comm(bufs, [jax.ShapeDtypeStruct(b.shape, b.dtype) for b in bufs], {a: a for a in range(n)}, 6 * n,
                 start, finish, deliver)


def _reduce_pair_comm(grads, deliver):
    n = len(grads)
    geoms = [_Geom(*a.shape[1:]) for a in grads]

    def copies(ins, outs, sems):
        x, y, c, _ = _place()
        return [_remote(geoms[a].half_ref(ins[a], (slice(None),), 1 - c), outs[a], sems, a, (x, y, 1 - c))
                for a in range(n)]

    def start(ins, outs, sems):
        for cp in copies(ins, outs, sems):
            cp.start()

    def finish(ins, outs, sems):
        cps = copies(ins, outs, sems)
        for cp in cps:
            cp.wait_recv()
        for cp in cps:
            cp.wait_send()

    shapes = [jax.ShapeDtypeStruct((N_SHARD, g.h0, g.h1), a.dtype) for a, g in zip(grads, geoms)]
    return _Comm(grads, shapes, {}, n, start, finish, deliver)


def _reduce_chips_comm(parts, deliver):
    n = len(parts)

    def copies(ins, outs, sems):
        x, y, c, chips = _place()
        return [_remote(ins[a].at[2 * cx + cy], outs[a].at[j], sems, 3 * a + j, (cx, cy, c))
                for a in range(n) for j, (cx, cy) in enumerate(chips)]

    def start(ins, outs, sems):
        for cp in copies(ins, outs, sems):
            cp.start()

    def finish(ins, outs, sems):
        cps = copies(ins, outs, sems)
        for cp in cps:
            cp.wait_recv()
        for cp in cps:
            cp.wait_send()

    shapes = [jax.ShapeDtypeStruct((3,) + a.shape[1:], a.dtype) for a in parts]
    return _Comm(parts, shapes, {}, 3 * n, start, finish, deliver)


def _share_pair_comm(bufs, deliver):
    n = len(bufs)
    geoms = [_Geom(*b.shape) for b in bufs]

    def copies(outs, sems):
        x, y, c, _ = _place()
        cps = []
        for a in range(n):
            mine = geoms[a].half_ref(outs[a], (), c)
            cps.append(_remote(mine, mine, sems, a, (x, y, 1 - c)))
        return cps

    def start(ins, outs, sems):
        for cp in copies(outs, sems):
            cp.start()

    def finish(ins, outs, sems):
        x, y, c, _ = _place()
        for a in range(n):
            land = geoms[a].half_ref(outs[a], (), 1 - c)
            _remote(land, land, sems, a, (x, y, 1 - c)).wait_recv()
        for cp in copies(outs, sems):
            cp.wait_send()

    return _Comm(bufs, [jax.ShapeDtypeStruct(b.shape, b.dtype) for b in bufs], {a: a for a in range(n)}, n,
                 start, finish, deliver)


def _small_comm(pack, deliver):
    r, d = pack.shape

    def copies(ins, outs, sems):
        x, y, c, _ = _place()
        cps = []
        for k in range(1, 8):
            peer = (x ^ ((k >> 2) & 1), y ^ ((k >> 1) & 1), c ^ (k & 1))
            cps.append(_remote(ins[0], outs[0].at[4 * x + 2 * y + c], sems, k - 1, peer))
        return cps

    def start(ins, outs, sems):
        for cp in copies(ins, outs, sems):
            cp.start()

    def finish(ins, outs, sems):
        cps = copies(ins, outs, sems)
        for cp in cps:
            cp.wait_recv()
        for cp in cps:
            cp.wait_send()

    return _Comm([pack], [jax.ShapeDtypeStruct((8, r, d), pack.dtype)], {}, 7, start, finish, deliver)


def _join_comms(comms):
    comms = [c for c in comms if c is not None]
    if len(comms) <= 1:
        return comms[0] if comms else None
    ins, out_shapes, aliases, spans, n_sems = [], [], {}, [], 0
    for c in comms:
        aliases.update({len(ins) + i: len(out_shapes) + o for i, o in c.aliases.items()})
        spans.append((len(ins), len(ins) + len(c.ins), len(out_shapes), len(out_shapes) + len(c.out_shapes), n_sems))
        ins += c.ins
        out_shapes += c.out_shapes
        n_sems += c.n_sems

    def run(which):
        def go(all_ins, all_outs, sems):
            for c, (i0, i1, o0, o1, base) in zip(comms, spans):
                getattr(c, which)(all_ins[i0:i1], all_outs[o0:o1], (sems[0], sems[1], sems[2] + base))
        return go

    def deliver(outs):
        for c, (_, _, o0, o1, _) in zip(comms, spans):
            c.deliver(outs[o0:o1])
        return outs

    return _Comm(ins, out_shapes, aliases, n_sems, run("start"), run("finish"), deliver)


def _ew_call(name, fn, ins, out_dtypes):
    shape = ins[0].shape
    cols = shape[-1]
    rows = 1
    for s_ in shape[:-1]:
        rows *= s_
    ins2 = [a.reshape(rows, cols) for a in ins]
    tr = rows
    for t in range(16, min(rows, max(16, (1 << 19) // cols)) + 1, 16):
        if rows % t == 0:
            tr = t
    no = len(out_dtypes)

    def body(*refs):
        outs = fn(*[r[...] for r in refs[:len(ins2)]])
        for ref, val in zip(refs[len(ins2):], outs):
            ref[...] = val.astype(ref.dtype)

    spec = pl.BlockSpec((tr, cols), lambda i: (i, 0))
    res = pl.pallas_call(
        body, name=name, grid=(rows // tr,), in_specs=[spec] * len(ins2), out_specs=[spec] * no,
        out_shape=[jax.ShapeDtypeStruct((rows, cols), dt) for dt in out_dtypes],
        compiler_params=_params(("parallel",)))(*ins2)
    return [r.reshape(shape) for r in res]


def _adamw_math(w, g, m, v):
    m = ADAM_B1 * m + (1.0 - ADAM_B1) * g
    v = ADAM_B2 * v + (1.0 - ADAM_B2) * (g * g)
    m_hat = m / (1.0 - ADAM_B1 ** ADAM_STEP)
    v_hat = v / (1.0 - ADAM_B2 ** ADAM_STEP)
    delta = -ADAM_LR * (m_hat / (jnp.sqrt(v_hat) + ADAM_EPS) + ADAM_WD * w)
    return delta, m, v


def _adamw_call(name, w, g, m, v):
    return _ew_call(name, _adamw_math, [w, g, m, v], [F32, F32, F32])


def _tiled_call(name, fn, place, grid, in_items, out_items, comm=None):
    ni = len(in_items)

    def body(place_ref, *refs):
        vals = fn(*[r[...] for r in refs[:ni]])
        for ref, val in zip(refs[ni:], vals):
            ref[...] = val.astype(ref.dtype)

    return _hosted_call(
        body, name=name, grid=grid, in_specs=[pl.BlockSpec(blk, imap) for _, blk, imap in in_items],
        out_specs=[pl.BlockSpec(blk, imap) for _, _, blk, imap in out_items],
        out_shape=[jax.ShapeDtypeStruct(shp, dt) for shp, dt, _, _ in out_items],
        args=[a for a, _, _ in in_items], prefetch=place, comm=comm)


def _cast_call(name, place, shards, comm=None):
    n0, n1 = shards[0].shape
    tr = _fit_rows(n0, n1, 1.5 * len(shards))
    ins = [(a, (tr, n1), lambda i, p: (i, 0)) for a in shards]
    outs = [((N_SHARD, n0, n1), BF16, (1, tr, n1), lambda i, p: (p[0], i, 0)) for _ in shards]
    return _tiled_call(name, lambda *v: [x[None] for x in v], place, (n0 // tr,), ins, outs, comm)


def _pair_sum_call(name, place, fulls, gots):
    k = len(fulls)
    g = _Geom(*fulls[0].shape[1:], blocks=2.5 * k)
    blk = (1, g.tr, g.h1)
    ins = [(a, blk, lambda s, i, p: g.half_block((s,), i, p[1])) for a in fulls]
    ins += [(a, blk, lambda s, i, p: (s, i, 0)) for a in gots]
    outs = [((N_SHARD, g.h0, g.h1), BF16, blk, lambda s, i, p: (s, i, 0)) for _ in fulls]
    return _tiled_call(name, lambda *v: [v[j] + v[k + j] for j in range(k)], place, (N_SHARD, g.nblk), ins, outs)


def _chip_sum_call(name, place, fulls, gots, recvs, comm=None):
    k = len(fulls)
    g = _Geom(*fulls[0].shape[1:], blocks=4.5 * k)
    blk = (1, g.tr, g.h1)
    ins = [(a, blk, lambda i, p: g.half_block((p[0],), i, p[1])) for a in fulls]
    ins += [(a, blk, lambda i, p: (p[0], i, 0)) for a in gots]
    ins += [(a, (3, g.tr, g.h1), lambda i, p: (0, i, 0)) for a in recvs]
    outs = [((g.n0, g.n1), F32, (g.tr, g.h1), lambda i, p: g.half_block((), i, p[1])) for _ in fulls]

    def fn(*v):
        res = []
        for j in range(k):
            r = v[2 * k + j].astype(F32)
            res.append(v[j][0] + v[k + j][0] + r[0] + r[1] + r[2])
        return res

    return _tiled_call(name, fn, place, (g.nblk,), ins, outs, comm)


def _adamw_group_call(name, ws, gs, ms, vs, comm=None):
    k = len(ws)
    n0, n1 = ws[0].shape
    tr = _fit_rows(n0, n1, 8 * k)
    spec = pl.BlockSpec((tr, n1), lambda i: (i, 0))

    def body(*refs):
        for j in range(k):
            g = refs[k + j][...]
            delta, m, vv = _adamw_math(refs[j][...], g, refs[2 * k + j][...], refs[3 * k + j][...])
            for ref, val in zip(refs[4 * k + 4 * j:4 * k + 4 * j + 4], (g, delta, m, vv)):
                ref[...] = val

    flat = _hosted_call(
        body, name=name, grid=(n0 // tr,), in_specs=[spec] * (4 * k), out_specs=[spec] * (4 * k),
        out_shape=[jax.ShapeDtypeStruct((n0, n1), F32)] * (4 * k), dims=("parallel",),
        args=(*ws, *gs, *ms, *vs), comm=comm)
    return [flat[4 * j:4 * j + 4] for j in range(k)]


def _small_update_call(me, early, own_early, late, own_late, wp, mp, vp):
    nd, r, d = early.shape

    def body(me_ref, e_ref, oe_ref, l_ref, ol_ref, w_ref, m_ref, v_ref, gs_ref, d_ref, nm_ref, nv_ref):
        mine = me_ref[0]

        def total(g_ref, own_ref):
            acc = None
            for k in range(nd):
                part = jnp.where(mine == k, own_ref[...], g_ref[k])
                acc = part if acc is None else acc + part
            return acc

        gs = total(e_ref, oe_ref)
        ls = total(l_ref, ol_ref)
        gs_ref[...] = gs
        first = gs[0:8] + ls[0:8]
        gs_ref[0:8, :] = first
        gs_ref[ROW_META:ROW_META + N_META, :] = gs[ROW_META:ROW_META + N_META] + ls[8:8 + N_META]
        grads = jnp.concatenate([first, gs[8:SMALL_ADAM_ROWS]], axis=0)
        delta, m, v = _adamw_math(w_ref[...], grads, m_ref[...], v_ref[...])
        d_ref[...] = delta
        nm_ref[...] = m
        nv_ref[...] = v

    vm = pl.BlockSpec(memory_space=pltpu.VMEM)
    ashape = jax.ShapeDtypeStruct((SMALL_ADAM_ROWS, d), F32)
    return pl.pallas_call(
        body, name="small_update", in_specs=[pl.BlockSpec(memory_space=pltpu.SMEM)] + [vm] * 7, out_specs=[vm] * 4,
        out_shape=[jax.ShapeDtypeStruct((r, d), F32), ashape, ashape, ashape],
        compiler_params=pltpu.CompilerParams(vmem_limit_bytes=VMEM_LIMIT_BYTES))(
            me, early, own_early, late, own_late, wp, mp, vp)


SMALL_NAMES = ["ffn1_norm", "mix_norm", "ffn2_norm", "final_norm", "q_latent_norm", "kv_latent_norm",
               "q_head_norm", "k_head_norm", "conv_b", "gate_a_b", "gate_x_b", "lru_lambda", "attn_out_norm",
               "lru_out_norm"]
ROW_CONV_W = 14
ROW_GATE_A = 16
ROW_GATE_X = 48
ROW_META = 80
ROW_LOSS = 96


def _row(a):
    flat = a.reshape(1, -1)
    return jnp.pad(flat, ((0, 0), (0, D_MODEL - flat.shape[1])))


def _pack_small(t, rows):
    parts = [_row(t[nm]) for nm in SMALL_NAMES]
    parts.append(t["conv_w"].reshape(2, D_MODEL))
    parts.append(t["gate_a_w"].reshape(32, D_MODEL))
    parts.append(t["gate_x_w"].reshape(32, D_MODEL))
    p = jnp.concatenate(parts, axis=0)
    return jnp.pad(p, ((0, rows - p.shape[0]), (0, 0)))


def _early_pack(g):
    gs = {nm: g.get(nm, jnp.zeros((1, D_MODEL), F32)) for nm in SMALL_NAMES}
    gs["q_head_norm"] = g["q_head_norm"][:, 0:D_QK]
    gs["k_head_norm"] = g["k_head_norm"][:, 0:D_QK]
    for nm in ("conv_b", "gate_a_b", "gate_x_b", "lru_lambda"):
        gs[nm] = g[nm].reshape(1, LRU_W)
    gs["conv_w"] = g["conv_w"].transpose(1, 0, 2).reshape(CONV_K, LRU_W)
    gs["gate_a_w"] = _gate_blocks(g["gate_a_w"])
    gs["gate_x_w"] = _gate_blocks(g["gate_x_w"])
    return jnp.concatenate([_pack_small(gs, ROW_META), jnp.zeros((N_META, D_MODEL), F32), _row(g["loss"][:, 0:1]),
                            jnp.zeros((SMALL_ROWS - ROW_LOSS - 1, D_MODEL), F32)], axis=0)


def _unpack_small(p, like):
    out = {}
    for k, nm in enumerate(SMALL_NAMES):
        out[nm] = p[k, 0:like[nm].size].reshape(like[nm].shape)
    out["gate_a_w"] = p[ROW_GATE_A:ROW_GATE_A + 32].reshape(like["gate_a_w"].shape)
    out["gate_x_w"] = p[ROW_GATE_X:ROW_GATE_X + 32].reshape(like["gate_x_w"].shape)
    return out


def _gate_dense(wg):
    w4 = wg[0].reshape(N_LRU_TILES, 2, 64, 64)
    zero = jnp.zeros((N_LRU_TILES, 64, 64), wg.dtype)
    top = jnp.concatenate([w4[:, 0], zero], axis=2)
    bot = jnp.concatenate([zero, w4[:, 1]], axis=2)
    return jnp.concatenate([top, bot], axis=1).astype(BF16)


def _gate_blocks(dw):
    return jnp.stack([dw[:, 0:64, 0:64], dw[:, 64:128, 64:128]], axis=1).reshape(8, 64, 64)


BIG_NAMES = ["ffn1_w_gate", "ffn1_w_up", "ffn1_w_down", "w_in", "w_uq", "w_uk", "w_uv", "w_out", "ffn2_w_gate",
             "ffn2_w_up", "ffn2_w_down"]
BIG_GROUPS = [["ffn1_w_gate", "ffn1_w_up", "ffn1_w_down", "ffn2_w_gate", "ffn2_w_up", "ffn2_w_down"], ["w_in"],
              ["w_uq"], ["w_uk", "w_uv"], ["w_out"]]
TRANSPOSED = ("ffn1_w_gate", "ffn1_w_up", "ffn2_w_gate", "ffn2_w_up", "w_in", "w_uq")


def _to2d(nm, a):
    return a[0].T if nm in TRANSPOSED else a[0]


def _from2d(nm, a):
    return (a.T if nm in TRANSPOSED else a)[None]


WEIGHT_NAMES = ["meta_tokens", "ffn1_norm", "ffn1_w_gate", "ffn1_w_up", "ffn1_w_down", "mix_norm", "w_in",
                "q_latent_norm", "w_uq", "kv_latent_norm", "w_uk", "w_uv", "q_head_norm", "k_head_norm", "conv_w",
                "conv_b", "gate_a_w", "gate_a_b", "gate_x_w", "gate_x_b", "lru_lambda", "attn_out_norm",
                "lru_out_norm", "w_out", "ffn2_norm", "ffn2_w_gate", "ffn2_w_up", "ffn2_w_down", "final_norm"]


def _weight_from(nm, slots):
    if nm == "w_in":
        win = slots.reshape(IN_WIDTH, D_MODEL)
        return jnp.concatenate([win[0:Z_KR + D_ROPE], jnp.zeros((128 - D_ROPE, D_MODEL), BF16),
                                win[Z_KR + D_ROPE:]], axis=0)
    if nm == "w_uq":
        return jnp.pad(slots, ((0, 0), (0, D_QKP - D_QK), (0, 0))).reshape(HEADS * D_QKP, Q_RANK)
    if nm in ("w_uk", "w_uv"):
        return slots.transpose(1, 0, 2).reshape(KV_RANK, HEADS * D_NOPE)
    if nm == "w_out":
        return slots.reshape(D_MODEL, D_MODEL)
    return slots


def _small_weights(p, small):
    w = {nm: p[nm] for nm in SMALL_NAMES}
    w["q_head_norm"] = jnp.pad(p["q_head_norm"], ((0, 0), (0, D_QKP - D_QK)))
    w["k_head_norm"] = jnp.pad(p["k_head_norm"], ((0, 0), (0, D_QKP - D_QK)))
    w["conv_w"] = small[:, N_META:N_META + 2, :].reshape(N_SHARD, CONV_K, LRU_TILE)
    w["gate_a_w"] = _gate_dense(p["gate_a_w"])
    w["gate_x_w"] = _gate_dense(p["gate_x_w"])
    meta = small[:, 0:N_META, :].transpose(1, 0, 2).reshape(N_META, D_MODEL)
    return w, meta


def _full_weights(p, gathered, small):
    w, meta = _small_weights(p, small)
    w.update({nm: _weight_from(nm, gathered[nm]) for nm in BIG_NAMES})
    return w, meta


def _shard_grad(nm, g):
    if nm == "w_in":
        return jnp.concatenate([g[0:Z_KR + D_ROPE], g[Z_MLA:]], axis=0).reshape(N_SHARD, IN_WIDTH // N_SHARD, D_MODEL)
    if nm == "w_uq":
        return g.reshape(HEADS, D_QKP, Q_RANK)[:, 0:D_QK, :]
    if nm in ("w_uk", "w_uv"):
        return g.reshape(KV_RANK, HEADS, D_NOPE).transpose(1, 0, 2)
    if nm == "w_out":
        return g.reshape(N_SHARD, D_MODEL // N_SHARD, D_MODEL)
    return g


def _shard_grads(g):
    return {nm: _shard_grad(nm, g[nm]) for nm in BIG_NAMES}


GATHER_FIRST = ["ffn1_w_gate"]
GATHER_AT = {"ffn1_gate": ["ffn1_w_up"], "ffn1_upact": ["ffn1_w_down"],
             "ffn1_down": ["w_in", "w_uq", "w_uk", "w_uv", "w_out"], "attn_fwd": ["ffn2_w_down", "ffn2_w_gate"],
             "lru_fwd": ["ffn2_w_up"]}
PAIR_AT = [("ffn2_din", ["ffn2_w_gate", "ffn2_w_up", "ffn2_w_down"]),
           ("mix_din", ["w_out", "w_uq", "w_uk", "w_uv", "w_in"]),
           ("ffn1_dwg", ["ffn1_w_down"]), ("ffn1_dwu", ["ffn1_w_gate"]), ("ffn1_din_a", ["ffn1_w_up"])]
CHIPS_AT = [("attn_bwd", ["ffn2_w_down", "ffn2_w_gate"]), ("mla_prep_bwd", ["ffn2_w_up"]),
            ("ffn1_dact", ["w_out", "w_uq", "w_uk", "w_uv", "w_in"]),
            ("ffn1_dwu", ["ffn1_w_down"]), ("ffn1_din_a", ["ffn1_w_gate"]), ("ffn1_din_b", ["ffn1_w_up"])]
SHARE_EARLY_GROUPS, SHARE_EARLY_AT = 3, "ffn1_dwd"
SMALL_EARLY_AT = "mix_din"


def _same_shape_groups(names):
    return [[nm for nm in grp if nm in names] for grp in BIG_GROUPS if any(nm in names for nm in grp)]


class _Sched:
    def __init__(self, place, w, slots):
        self.place, self.w, self.slots = place, w, slots
        self.g = None
        self.sharded, self.from_pair, self.chip_bf16, self.from_chips = {}, {}, {}, {}
        self.early = self.early_all = None
        self.shared = {}

    def host(self, stage):
        comms = []
        if stage in GATHER_AT:
            comms.append(self.gather(GATHER_AT[stage]))
        comms += [self.chips(names) for at, names in CHIPS_AT if at == stage]
        comms += [self.pair(names) for at, names in PAIR_AT if at == stage]
        if stage == SMALL_EARLY_AT:
            comms.append(self.small_early())
        if stage == SHARE_EARLY_AT:
            comms.append(self.share_early())
        return _join_comms(comms)

    def small_early(self):
        self.early = _early_pack(self.g)

        def deliver(outs):
            self.early_all = outs[0]
            return outs

        return _small_comm(self.early, deliver)

    def gather(self, names):
        def deliver(outs):
            self.w.update({nm: _weight_from(nm, o) for nm, o in zip(names, outs)})
            return outs

        return _gather_comm([self.slots[nm] for nm in names], deliver)

    def pair(self, names):
        self.sharded.update({nm: _shard_grad(nm, self.g[nm]) for nm in names})

        def deliver(outs):
            self.from_pair.update(zip(names, outs))
            for grp in _same_shape_groups(names):
                sums = _pair_sum_call("pair_sum_" + grp[0], self.place, [self.sharded[nm] for nm in grp],
                                      [self.from_pair[nm] for nm in grp])
                self.chip_bf16.update(zip(grp, sums))
            return outs

        return _reduce_pair_comm([self.sharded[nm] for nm in names], deliver)

    def chips(self, names):
        def deliver(outs):
            self.from_chips.update(zip(names, outs))
            return outs

        return _reduce_chips_comm([self.chip_bf16[nm] for nm in names], deliver)

    def chip_sums(self, names, comm=None):
        out = {}
        for grp in _same_shape_groups(names):
            sums = _chip_sum_call("chip_sum_" + grp[0], self.place, [self.sharded[nm] for nm in grp],
                                  [self.from_pair[nm] for nm in grp], [self.from_chips[nm] for nm in grp], comm)
            comm = None
            out.update(zip(grp, sums))
        return out

    def share_early(self):
        names = [nm for at, grp in CHIPS_AT[:SHARE_EARLY_GROUPS] for nm in grp]
        mine = self.chip_sums(names)
        return _share_pair_comm([mine[nm] for nm in names], lambda o: self.shared.update(zip(names, o)))


def kernel(x, meta_tokens, ffn1_norm, ffn1_w_gate, ffn1_w_up, ffn1_w_down, mix_norm, w_in, q_latent_norm, w_uq, kv_latent_norm, w_uk, w_uv, q_head_norm, k_head_norm, conv_w, conv_b, gate_a_w, gate_a_b, gate_x_w, gate_x_b, lru_lambda, attn_out_norm, lru_out_norm, w_out, ffn2_norm, ffn2_w_gate, ffn2_w_up, ffn2_w_down, final_norm, loss_target, m_meta_tokens, m_ffn1_norm, m_ffn1_w_gate, m_ffn1_w_up, m_ffn1_w_down, m_mix_norm, m_w_in, m_q_latent_norm, m_w_uq, m_kv_latent_norm, m_w_uk, m_w_uv, m_q_head_norm, m_k_head_norm, m_conv_w, m_conv_b, m_gate_a_w, m_gate_a_b, m_gate_x_w, m_gate_x_b, m_lru_lambda, m_attn_out_norm, m_lru_out_norm, m_w_out, m_ffn2_norm, m_ffn2_w_gate, m_ffn2_w_up, m_ffn2_w_down, m_final_norm, v_meta_tokens, v_ffn1_norm, v_ffn1_w_gate, v_ffn1_w_up, v_ffn1_w_down, v_mix_norm, v_w_in, v_q_latent_norm, v_w_uq, v_kv_latent_norm, v_w_uk, v_w_uv, v_q_head_norm, v_k_head_norm, v_conv_w, v_conv_b, v_gate_a_w, v_gate_a_b, v_gate_x_w, v_gate_x_b, v_lru_lambda, v_attn_out_norm, v_lru_out_norm, v_w_out, v_ffn2_norm, v_ffn2_w_gate, v_ffn2_w_up, v_ffn2_w_down, v_final_norm):
    args = locals()
    p = {nm: args[nm] for nm in WEIGHT_NAMES}
    mom = {nm: args["m_" + nm] for nm in WEIGHT_NAMES}
    var = {nm: args["v_" + nm] for nm in WEIGHT_NAMES}
    nb, seq, d = x.shape
    lp = CHUNK + seq
    xi, yi, ci = lax.axis_index("x"), lax.axis_index("y"), lax.axis_index("c")
    chip = 2 * xi + yi

    place = jnp.stack([chip, ci]).astype(jnp.int32)
    p2 = {nm: _to2d(nm, p[nm]) for nm in BIG_NAMES}
    m2 = {nm: _to2d(nm, mom[nm]) for nm in BIG_NAMES}
    v2 = {nm: _to2d(nm, var[nm]) for nm in BIG_NAMES}

    slots = {}
    small_shard = jnp.concatenate(
        [meta_tokens, conv_w[0].reshape(2, 2 * LRU_TILE), jnp.zeros((14, 2 * LRU_TILE), F32)], axis=0)
    small_slots = lax.dynamic_update_slice(jnp.zeros((N_SHARD,) + small_shard.shape, F32), small_shard[None],
                                           (chip, 0, 0))
    for grp in BIG_GROUPS:
        for nm, buf in zip(grp, _cast_call("cast_" + grp[0], place, [p2[nm] for nm in grp])):
            slots[nm] = buf
    first = _comm_call("gather_first", _gather_comm([slots[nm] for nm in GATHER_FIRST] + [small_slots], lambda o: o))
    w, meta = _small_weights(p, first[-1])
    w.update({nm: _weight_from(nm, o) for nm, o in zip(GATHER_FIRST, first[:-1])})
    sched = _Sched(place, w, slots)

    h0 = jnp.concatenate(
        [jnp.zeros((nb, PAD_ROWS, d), F32), jnp.broadcast_to(meta[None], (nb, N_META, d)), x], axis=1)
    target = jnp.pad(loss_target, ((0, 0), (CHUNK, 0), (0, 0)))
    loss_part, dh0, g = _local_step(h0.reshape(nb * lp, d), target.reshape(nb * lp, d), w, nb, lp, sched)
    dh0 = dh0.reshape(nb, lp, d)
    grad_x = dh0[:, CHUNK:, :]

    late = jnp.concatenate([g["ffn1_norm"], g["mix_norm"], jnp.zeros((6, D_MODEL), F32),
                            jnp.sum(dh0[:, PAD_ROWS:CHUNK, :], axis=0)], axis=0)
    shared = sched.shared
    rest = [nm for at, names in CHIPS_AT[SHARE_EARLY_GROUPS:] if at is not None for nm in names]
    last = [nm for at, names in CHIPS_AT if at is None for nm in names]
    late_box = {}
    mine = sched.chip_sums(rest, _small_comm(late, lambda o: late_box.update(all=o[0])))
    share = _share_pair_comm([mine[nm] for nm in rest], lambda o: shared.update(zip(rest, o)))
    _comm_call("share_pair", _join_comms([share, sched.chips(last) if last else None]))
    if last:
        mine = sched.chip_sums(last)
        _comm_call("share_last", _share_pair_comm([mine[nm] for nm in last], lambda o: shared.update(zip(last, o))))
    late_all = late_box["all"]
    small_like = {nm: p[nm] for nm in SMALL_NAMES + ["gate_a_w", "gate_x_w"]}

    def pack_w(t):
        tt = {nm: t[nm] for nm in SMALL_NAMES + ["gate_a_w", "gate_x_w"]}
        tt["conv_w"] = jnp.zeros((CONV_K, LRU_W), F32)
        return _pack_small(tt, SMALL_ADAM_ROWS)

    me = (4 * xi + 2 * yi + ci).astype(jnp.int32).reshape(1)
    gsum, dsm, msm, vsm = _small_update_call(me, sched.early_all, sched.early, late_all, late, pack_w(p),
                                             pack_w(mom), pack_w(var))
    grads = _unpack_small(gsum, small_like)
    delta = _unpack_small(dsm, small_like)
    new_m = _unpack_small(msm, small_like)
    new_v = _unpack_small(vsm, small_like)
    loss = gsum[ROW_LOSS, 0]
    gmeta = gsum[ROW_META:ROW_META + N_META].reshape(N_META, N_SHARD, D_MODEL // N_SHARD)
    grads["meta_tokens"] = lax.dynamic_index_in_dim(gmeta, chip, axis=1, keepdims=False)
    gconv = gsum[ROW_CONV_W:ROW_CONV_W + 2].reshape(CONV_K, N_SHARD, LRU_TILE)
    grads["conv_w"] = lax.dynamic_index_in_dim(gconv, chip, axis=1, keepdims=False)[None]
    for nm in ("meta_tokens", "conv_w"):
        delta[nm], new_m[nm], new_v[nm] = _adamw_call("adamw_" + nm, p[nm], grads[nm], mom[nm], var[nm])

    for names in BIG_GROUPS:
        res = _adamw_group_call("adamw_" + names[0], [p2[nm] for nm in names], [shared[nm] for nm in names],
                                [m2[nm] for nm in names], [v2[nm] for nm in names])
        for nm, (gg, dd, mm, vv) in zip(names, res):
            grads[nm], delta[nm], new_m[nm], new_v[nm] = (_from2d(nm, t) for t in (gg, dd, mm, vv))

    return (loss, grad_x, *[grads[nm] for nm in WEIGHT_NAMES], *[delta[nm] for nm in WEIGHT_NAMES],
            *[new_m[nm] for nm in WEIGHT_NAMES], *[new_v[nm] for nm in WEIGHT_NAMES])
```

```python
import functools
import math

import jax
import jax.numpy as jnp
import numpy as np
from jax import lax
from jax.experimental import pallas as pl
from jax.experimental.pallas import tpu as pltpu

F32 = jnp.float32
BF16 = jnp.bfloat16
MESH = pl.DeviceIdType.MESH

D_MODEL = 1024
N_META = 16
CHUNK = 64
PAD_ROWS = CHUNK - N_META
HEADS = 4
D_NOPE = 128
D_ROPE = 64
D_QK = D_NOPE + D_ROPE
D_QKP = 256
D_V = 128
KV_RANK = 256
Q_RANK = 384
MLA_W = HEADS * D_V
LRU_W = 512
LRU_TILE = 128
N_LRU_TILES = LRU_W // LRU_TILE
CONV_K = 4
C_RGLRU = 8.0
ROPE_THETA = 10000.0
D_FF = 2816
N_SHARD = 4
EPS = 1e-6
NEG_INF = -1e30
Z_KR = Q_RANK + KV_RANK
Z_MLA = Z_KR + 128
Z_U = Z_MLA
Z_G = Z_U + LRU_W
Z_W = Z_G + LRU_W
IN_WIDTH = Q_RANK + KV_RANK + D_ROPE + 2 * LRU_W

ADAM_LR = 0.001
ADAM_B1 = 0.9
ADAM_B2 = 0.999
ADAM_EPS = 1e-08
ADAM_WD = 0.01
ADAM_STEP = 10

VMEM_LIMIT_BYTES = 56 * 1024 * 1024
SMALL_ROWS = 104
SMALL_ADAM_ROWS = 80


def _params(sem):
    return pltpu.CompilerParams(dimension_semantics=sem, vmem_limit_bytes=VMEM_LIMIT_BYTES)


def _resident(shape):
    return pl.BlockSpec(tuple(shape), lambda i: (0,) * len(shape), pipeline_mode=pl.Buffered(1))


def _row_tile(rows, target):
    best = 16
    for t in range(16, min(rows, target) + 1, 16):
        if rows % t == 0:
            best = t
    return best


def _col_tile(cols, target):
    best = cols
    for t in range(128, min(cols, target) + 1, 128):
        if cols % t == 0:
            best = t
    return best


def _dot(a, b):
    return jnp.dot(a, b, preferred_element_type=F32)


def _dot_nt(a, b):
    return lax.dot_general(a, b, (((1,), (1,)), ((), ())), preferred_element_type=F32)


def _dot_tn(a, b):
    return lax.dot_general(a, b, (((0,), (0,)), ((), ())), preferred_element_type=F32)


def _rms(x, n):
    return lax.rsqrt(jnp.sum(x * x, axis=-1, keepdims=True) * (1.0 / n) + EPS)


def _rms_bwd(dn, nrm, r, n):
    return r * (dn - nrm * (jnp.sum(dn * nrm, axis=-1, keepdims=True) * (1.0 / n)))


def _gelu(x):
    k = math.sqrt(2.0 / math.pi)
    t = jnp.tanh(k * (x + 0.044715 * x * x * x))
    return 0.5 * x * (1.0 + t), t


def _gelu_grad(x, t):
    k = math.sqrt(2.0 / math.pi)
    return 0.5 * (1.0 + t) + 0.5 * x * (1.0 - t * t) * k * (1.0 + 3.0 * 0.044715 * x * x)


def _sigmoid(x):
    return 0.5 + 0.5 * jnp.tanh(0.5 * x)


def _softplus_neg(lam):
    e = jnp.exp(-jnp.abs(lam))
    log1p = jnp.where(e < 0.01, e * (1.0 - e * (0.5 - e * (1.0 / 3 - e * 0.25))), jnp.log(1.0 + e))
    return jnp.maximum(-lam, 0.0) + log1p


def _rope(t, c, s1, s2):
    return t * c + pltpu.roll(t, 96, 1) * s1 + pltpu.roll(t, 32, 1) * s2


def _rope_t(d, c, s1, s2):
    return d * c + pltpu.roll(d * s1, 32, 1) + pltpu.roll(d * s2, 96, 1)


def _rope_tables(lp):
    pos = (np.arange(lp, dtype=np.int32) - PAD_ROWS).astype(np.float32)
    inv_freq = (ROPE_THETA ** (-np.arange(0, D_ROPE // 2, dtype=np.float32) / (D_ROPE // 2))).astype(np.float32)
    ang = (pos[:, None] * inv_freq[None, :]).astype(np.float32).astype(np.float64)
    cos, sin = np.cos(ang).astype(np.float32), np.sin(ang).astype(np.float32)
    z = np.zeros_like(cos)
    return (jnp.asarray(np.concatenate([cos, cos, z, z], 1)), jnp.asarray(np.concatenate([-sin, z, z, z], 1)),
            jnp.asarray(np.concatenate([z, sin, z, z], 1)))


def _rmsnorm_call(name, h, g, tm):
    rows, d = h.shape

    def body(h_ref, g_ref, o_ref):
        x = h_ref[...]
        o_ref[...] = (x * _rms(x, d) * g_ref[...]).astype(BF16)

    return pl.pallas_call(
        body, name=name, grid=(rows // tm,),
        in_specs=[pl.BlockSpec((tm, d), lambda i: (i, 0)), pl.BlockSpec((1, d), lambda i: (0, 0))],
        out_specs=pl.BlockSpec((tm, d), lambda i: (i, 0)),
        out_shape=jax.ShapeDtypeStruct((rows, d), BF16),
        compiler_params=_params(("parallel",)))(h, g)


def _ffn_up_call(name, u, wg, wu, tm, comm=None):
    rows, d = u.shape
    ns, fs, _ = wg.shape

    def body(u_ref, wg_ref, wu_ref, g_ref, p_ref, a_ref):
        uu = u_ref[...]
        g = _dot_nt(uu, wg_ref[0])
        p = _dot_nt(uu, wu_ref[0])
        g_ref[0] = g.astype(BF16)
        p_ref[0] = p.astype(BF16)
        a_ref[0] = (g * jax.nn.sigmoid(g) * p).astype(BF16)

    wspec = pl.BlockSpec((1, fs, d), lambda s, i: (s, 0, 0))
    ospec = pl.BlockSpec((1, tm, fs), lambda s, i: (s, i, 0))
    oshape = jax.ShapeDtypeStruct((ns, rows, fs), BF16)
    return _hosted_call(
        body, name=name, grid=(ns, rows // tm),
        in_specs=[pl.BlockSpec((tm, d), lambda s, i: (i, 0)), wspec, wspec],
        out_specs=[ospec, ospec, ospec], out_shape=[oshape, oshape, oshape],
        dims=("parallel", "parallel"), args=(u, wg, wu), comm=comm)


def _ffn_gate_call(name, u, wg, tm, comm=None):
    rows, d = u.shape
    ns, fs, _ = wg.shape

    def body(u_ref, wg_ref, g_ref):
        g_ref[0] = _dot_nt(u_ref[...], wg_ref[0]).astype(BF16)

    return _hosted_call(
        body, name=name, grid=(ns, rows // tm),
        in_specs=[pl.BlockSpec((tm, d), lambda s, i: (i, 0)), pl.BlockSpec((1, fs, d), lambda s, i: (s, 0, 0))],
        out_specs=[pl.BlockSpec((1, tm, fs), lambda s, i: (s, i, 0))],
        out_shape=[jax.ShapeDtypeStruct((ns, rows, fs), BF16)],
        dims=("parallel", "parallel"), args=(u, wg), comm=comm)[0]


def _ffn_upact_call(name, u, wu, gate, tm, comm=None):
    rows, d = u.shape
    ns, fs, _ = wu.shape

    def body(u_ref, wu_ref, g_ref, p_ref, a_ref):
        p = _dot_nt(u_ref[...], wu_ref[0])
        g = g_ref[0].astype(F32)
        p_ref[0] = p.astype(BF16)
        a_ref[0] = (g * jax.nn.sigmoid(g) * p).astype(BF16)

    ospec = pl.BlockSpec((1, tm, fs), lambda s, i: (s, i, 0))
    oshape = jax.ShapeDtypeStruct((ns, rows, fs), BF16)
    return _hosted_call(
        body, name=name, grid=(ns, rows // tm),
        in_specs=[pl.BlockSpec((tm, d), lambda s, i: (i, 0)), pl.BlockSpec((1, fs, d), lambda s, i: (s, 0, 0)), ospec],
        out_specs=[ospec, ospec], out_shape=[oshape, oshape],
        dims=("parallel", "parallel"), args=(u, wu, gate), comm=comm)


def _loss_tail(x, g, t, row, d):
    r = _rms(x, d)
    n = x * r
    err = jnp.where(row >= CHUNK, n * g - t, 0.0)
    dout = err * (1.0 / d)
    dh = _rms_bwd(dout * g, n, r, d)
    dg = jnp.sum(dout * n, axis=0, keepdims=True)
    part = jnp.sum(jnp.sum(err * err, axis=1, keepdims=True), axis=0, keepdims=True) * (0.5 / d)
    return dh, dg, jnp.broadcast_to(part, (1, 128))


def _ffn_down_call(name, a, wd, h, tm, comm=None, next_gain=None):
    rows, d = h.shape
    ns, _, fs = a.shape
    more = next_gain is not None

    def body(a_ref, wd_ref, h_ref, *rest):
        acc = h_ref[...]
        for s in range(ns):
            acc = acc + 0.5 * _dot(a_ref[s], wd_ref[s])
        rest[-2 if more else -1][...] = acc
        if more:
            rest[-1][...] = (acc * _rms(acc, d) * rest[0][...]).astype(BF16)

    full = pl.BlockSpec((tm, d), lambda i: (i, 0))
    res = _hosted_call(
        body, name=name, grid=(rows // tm,),
        in_specs=[pl.BlockSpec((ns, tm, fs), lambda i: (0, i, 0)), _resident((ns, fs, d)), full]
        + ([pl.BlockSpec((1, d), lambda i: (0, 0))] if more else []),
        out_specs=[full] * (2 if more else 1),
        out_shape=[jax.ShapeDtypeStruct((rows, d), F32)] + ([jax.ShapeDtypeStruct((rows, d), BF16)] if more else []),
        dims=("parallel",), args=(a, wd, h) + ((next_gain,) if more else ()), comm=comm)
    return res if more else res[0]


def _ffn_down_loss_call(name, a, wd, h, g, target, lp, tm):
    rows, d = h.shape
    ns, _, fs = a.shape
    tpe = lp // tm

    def body(a_ref, wd_ref, h_ref, g_ref, t_ref, dh_ref, dhb_ref, dg_ref, loss_ref):
        i = pl.program_id(0)
        acc = h_ref[...]
        for s in range(ns):
            acc = acc + 0.5 * _dot(a_ref[s], wd_ref[s])
        row = (i % tpe) * tm + lax.broadcasted_iota(jnp.int32, (tm, 1), 0)
        dh, dg, loss = _loss_tail(acc, g_ref[...], t_ref[...], row, d)
        dh_ref[...] = dh
        dhb_ref[...] = dh.astype(BF16)

        @pl.when(i == 0)
        def _():
            dg_ref[...] = dg
            loss_ref[...] = loss

        @pl.when(i != 0)
        def _():
            dg_ref[...] += dg
            loss_ref[...] += loss

    full = pl.BlockSpec((tm, d), lambda i: (i, 0))
    gspec = pl.BlockSpec((1, d), lambda i: (0, 0))
    return _hosted_call(
        body, name=name, grid=(rows // tm,),
        in_specs=[pl.BlockSpec((ns, tm, fs), lambda i: (0, i, 0)), _resident((ns, fs, d)), full, gspec, full],
        out_specs=[full, full, gspec, pl.BlockSpec((1, 128), lambda i: (0, 0))],
        out_shape=[jax.ShapeDtypeStruct((rows, d), F32), jax.ShapeDtypeStruct((rows, d), BF16),
                   jax.ShapeDtypeStruct((1, d), F32), jax.ShapeDtypeStruct((1, 128), F32)],
        args=(a, wd, h, g, target))


def _mm_call(name, a, bt, tm, out_dtype):
    rows, k = a.shape
    n = bt.shape[0]

    def body(a_ref, b_ref, o_ref):
        o_ref[...] = _dot_nt(a_ref[...], b_ref[...]).astype(out_dtype)

    return pl.pallas_call(
        body, name=name, grid=(rows // tm,),
        in_specs=[pl.BlockSpec((tm, k), lambda i: (i, 0)), pl.BlockSpec((n, k), lambda i: (0, 0))],
        out_specs=pl.BlockSpec((tm, n), lambda i: (i, 0)),
        out_shape=jax.ShapeDtypeStruct((rows, n), out_dtype),
        compiler_params=_params(("parallel",)))(a, bt)


def _mla_heads(z, gql, gkvl, wuq, wuk, wuv):
    cq = z[:, 0:Q_RANK]
    ckv = z[:, Q_RANK:Z_KR]
    kr = z[:, Z_KR:Z_MLA]
    rq = _rms(cq, Q_RANK)
    nq = cq * rq
    cqn = (nq * gql).astype(BF16)
    rkv = _rms(ckv, KV_RANK)
    nkv = ckv * rkv
    ckvn = (nkv * gkvl).astype(BF16)
    qraw = _dot_nt(cqn, wuq)
    knope = _dot(ckvn, wuk)
    v = _dot(ckvn, wuv)
    skr = jnp.sum(kr * kr, axis=-1, keepdims=True)
    heads = []
    for hd in range(HEADS):
        qh = qraw[:, hd * D_QKP:(hd + 1) * D_QKP]
        rqh = lax.rsqrt(jnp.sum(qh * qh, axis=-1, keepdims=True) * (1.0 / D_QK) + EPS)
        kn = knope[:, hd * D_NOPE:(hd + 1) * D_NOPE]
        rkh = lax.rsqrt((jnp.sum(kn * kn, axis=-1, keepdims=True) + skr) * (1.0 / D_QK) + EPS)
        heads.append((qh * rqh, rqh, kn * rkh, kr * rkh, rkh))
    return dict(rq=rq, nq=nq, cqn=cqn, rkv=rkv, nkv=nkv, ckvn=ckvn, v=v, heads=heads)


def _mla_prep_call(z, gql, gkvl, gqh, gkh, wuq, wuk, wuv, tabs, lp, tm):
    rows = z.shape[0]
    tpe = lp // tm

    def body(z_ref, gql_ref, gkvl_ref, gqh_ref, gkh_ref, wuq_ref, wuk_ref, wuv_ref, c_ref, s1_ref, s2_ref,
             q_ref, k_ref, v_ref, cqn_ref, ckvn_ref):
        m = _mla_heads(z_ref[...], gql_ref[...], gkvl_ref[...], wuq_ref[...], wuk_ref[...], wuv_ref[...])
        c, s1, s2 = c_ref[...], s1_ref[...], s2_ref[...]
        gq, gk = gqh_ref[...], gkh_ref[...]
        row = (pl.program_id(0) % tpe) * tm + lax.broadcasted_iota(jnp.int32, (tm, 1), 0)
        spare = (lax.broadcasted_iota(jnp.int32, (1, D_QKP - D_NOPE), 1) == D_ROPE).astype(F32)
        kmask = jnp.where(row < PAD_ROWS, NEG_INF * math.sqrt(D_QK), 0.0) * spare
        for hd in range(HEADS):
            qn, _, knn, krn, _ = m["heads"][hd]
            qg = qn * gq
            q_ref[hd, :, 0:D_NOPE] = qg[:, 0:D_NOPE].astype(BF16)
            q_ref[hd, :, D_NOPE:D_QKP] = (_rope(qg[:, D_NOPE:D_QKP], c, s1, s2) + spare).astype(BF16)
            k_ref[hd, :, 0:D_NOPE] = (knn * gk[:, 0:D_NOPE]).astype(BF16)
            k_ref[hd, :, D_NOPE:D_QKP] = (_rope(krn * gk[:, D_NOPE:D_QKP], c, s1, s2) + kmask).astype(BF16)
            v_ref[hd] = m["v"][:, hd * D_V:(hd + 1) * D_V].astype(BF16)
        cqn_ref[...] = m["cqn"]
        ckvn_ref[...] = m["ckvn"]

    def const(shape):
        return pl.BlockSpec(shape, lambda i: tuple(0 for _ in shape))

    tab = pl.BlockSpec((tm, 128), lambda i: (i % tpe, 0))
    return pl.pallas_call(
        body, name="mla_prep", grid=(rows // tm,),
        in_specs=[pl.BlockSpec((tm, Z_MLA), lambda i: (i, 0)), const((1, Q_RANK)), const((1, KV_RANK)),
                  const((1, D_QKP)), const((1, D_QKP)), const((HEADS * D_QKP, Q_RANK)),
                  const((KV_RANK, HEADS * D_NOPE)), const((KV_RANK, HEADS * D_V)), tab, tab, tab],
        out_specs=[pl.BlockSpec((HEADS, tm, D_QKP), lambda i: (0, i, 0)),
                   pl.BlockSpec((HEADS, tm, D_QKP), lambda i: (0, i, 0)),
                   pl.BlockSpec((HEADS, tm, D_V), lambda i: (0, i, 0)),
                   pl.BlockSpec((tm, Q_RANK), lambda i: (i, 0)),
                   pl.BlockSpec((tm, KV_RANK), lambda i: (i, 0))],
        out_shape=[jax.ShapeDtypeStruct((HEADS, rows, D_QKP), BF16),
                   jax.ShapeDtypeStruct((HEADS, rows, D_QKP), BF16),
                   jax.ShapeDtypeStruct((HEADS, rows, D_V), BF16),
                   jax.ShapeDtypeStruct((rows, Q_RANK), BF16),
                   jax.ShapeDtypeStruct((rows, KV_RANK), BF16)],
        compiler_params=_params(("parallel",)))(z, gql, gkvl, gqh, gkh, wuq, wuk, wuv, *tabs)


Q_BLOCK_ROWS = 528
Q_BLOCK_ROWS_BWD = 192


def _q_block(lp):
    return _row_tile(lp, Q_BLOCK_ROWS)


def _key_end(ext, lp):
    return min(lp, -(-ext // CHUNK) * CHUNK)


def _diag_bias(qb, j0, nk):
    shift = CHUNK.bit_length() - 1
    r = jnp.right_shift(j0 + lax.broadcasted_iota(jnp.int32, (qb, nk), 0), shift)
    c = jnp.right_shift(j0 + lax.broadcasted_iota(jnp.int32, (qb, nk), 1), shift)
    return jnp.where(c <= r, 0.0, NEG_INF)


def _attn_fwd_call(q, k, v, nb, lp, comm=None):
    rows = nb * lp
    qb = _q_block(lp)
    scale = 1.0 / math.sqrt(D_QK)

    def body(q_ref, k_ref, v_ref, o_ref, lse_ref):
        for j in range(lp // qb):
            j0, ext = j * qb, (j + 1) * qb
            kend = _key_end(ext, lp)
            qj = q_ref[0, j0:ext, :]
            sd = _dot_nt(qj, k_ref[0, j0:kend, :]) * scale + _diag_bias(qb, j0, kend - j0)
            mx = jnp.max(sd, axis=-1, keepdims=True)
            if j > 0:
                so = _dot_nt(qj, k_ref[0, 0:j0, :]) * scale
                mx = jnp.maximum(mx, jnp.max(so, axis=-1, keepdims=True))
            pd = jnp.exp(sd - mx)
            l = jnp.sum(pd, axis=-1, keepdims=True)
            o = _dot(pd.astype(BF16), v_ref[0, j0:kend, :])
            if j > 0:
                po = jnp.exp(so - mx)
                l = l + jnp.sum(po, axis=-1, keepdims=True)
                o = o + _dot(po.astype(BF16), v_ref[0, 0:j0, :])
            o_ref[j0:ext, :] = o / l
            lse_ref[0, j0:ext, :] = mx + jnp.log(l)

    return _hosted_call(
        body, name="attn_fwd", grid=(nb, HEADS),
        in_specs=[pl.BlockSpec((1, lp, D_QKP), lambda b, h: (h, b, 0)),
                  pl.BlockSpec((1, lp, D_QKP), lambda b, h: (h, b, 0)),
                  pl.BlockSpec((1, lp, D_V), lambda b, h: (h, b, 0))],
        out_specs=[pl.BlockSpec((lp, D_V), lambda b, h: (b, h)),
                   pl.BlockSpec((1, lp, 1), lambda b, h: (h, b, 0))],
        out_shape=[jax.ShapeDtypeStruct((rows, MLA_W), F32),
                   jax.ShapeDtypeStruct((HEADS, rows, 1), F32)],
        dims=("parallel", "parallel"), args=(q, k, v), comm=comm)


def _attn_bwd_call(q, k, v, o, lse, do, nb, lp, comm=None):
    rows = nb * lp
    qb = _row_tile(lp, Q_BLOCK_ROWS_BWD)
    scale = 1.0 / math.sqrt(D_QK)

    def body(q_ref, k_ref, v_ref, o_ref, lse_ref, do_ref, dq_ref, dk_ref, dv_ref, dk_acc, dv_acc):
        dk_acc[...] = jnp.zeros_like(dk_acc)
        dv_acc[...] = jnp.zeros_like(dv_acc)
        shift = CHUNK.bit_length() - 1
        for j in range(lp // qb):
            j0, ext = j * qb, (j + 1) * qb
            kend = _key_end(ext, lp)
            qj = q_ref[0, j0:ext, :]
            doj = do_ref[j0:ext, :]
            delta = jnp.sum(doj * o_ref[j0:ext, :], axis=-1, keepdims=True)
            dob = doj.astype(BF16)
            kk = k_ref[0, 0:kend, :]
            qchunk = jnp.right_shift(j0 + lax.broadcasted_iota(jnp.int32, (qb, kend), 0), shift)
            kchunk = jnp.right_shift(lax.broadcasted_iota(jnp.int32, (qb, kend), 1), shift)
            s = _dot_nt(qj, kk) * scale - lse_ref[0, j0:ext, :]
            p = jnp.where(kchunk <= qchunk, jnp.exp(s), 0.0)
            dv_acc[0:kend, :] += _dot_tn(p.astype(BF16), dob)
            dp = _dot_nt(dob, v_ref[0, 0:kend, :])
            ds = (p * (dp - delta) * scale).astype(BF16)
            dq_ref[0, j0:ext, :] = _dot(ds, kk).astype(BF16)
            dk_acc[0:kend, :] += _dot_tn(ds, qj)
        dk_ref[0] = dk_acc[...].astype(BF16)
        dv_ref[0] = dv_acc[...].astype(BF16)

    qspec = pl.BlockSpec((1, lp, D_QKP), lambda b, h: (h, b, 0))
    vspec = pl.BlockSpec((1, lp, D_V), lambda b, h: (h, b, 0))
    ospec = pl.BlockSpec((lp, D_V), lambda b, h: (b, h))
    return _hosted_call(
        body, name="attn_bwd", grid=(nb, HEADS),
        in_specs=[qspec, qspec, vspec, ospec, pl.BlockSpec((1, lp, 1), lambda b, h: (h, b, 0)), ospec],
        out_specs=[qspec, qspec, vspec],
        out_shape=[jax.ShapeDtypeStruct((HEADS, rows, D_QKP), BF16),
                   jax.ShapeDtypeStruct((HEADS, rows, D_QKP), BF16),
                   jax.ShapeDtypeStruct((HEADS, rows, D_V), BF16)],
        scratch_shapes=[pltpu.VMEM((lp, D_QKP), F32), pltpu.VMEM((lp, D_V), F32)],
        dims=("parallel", "parallel"), args=(q, k, v, o, lse, do), comm=comm)


def _lru_gates(u, cw, cb, wa, ba, wx, bx, lam, lp):
    xc = (cw[3:4, :] * u + cw[2:3, :] * pltpu.roll(u, 1, 0) + cw[1:2, :] * pltpu.roll(u, 2, 0)
          + cw[0:1, :] * pltpu.roll(u, 3, 0) + cb)
    xcb = xc.astype(BF16)
    r = _sigmoid(_dot(xcb, wa) + ba)
    i = _sigmoid(_dot(xcb, wx) + bx)
    sp = _softplus_neg(lam)
    la = -C_RGLRU * r * sp
    a = jnp.exp(la)
    x2 = 2.0 * la
    e2 = a * a
    m2 = jnp.maximum(jnp.where(x2 > -0.01, -x2 * (1.0 + 0.5 * x2), 1.0 - e2), 1e-30)
    rs = lax.rsqrt(m2)
    row = lax.broadcasted_iota(jnp.int32, (lp, LRU_TILE), 0)
    first = row == PAD_ROWS
    valid = row >= PAD_ROWS
    mult_eff = jnp.where(first, 1.0, m2 * rs)
    return dict(xc=xc, xcb=xcb, r=r, i=i, sp=sp, a=a, e2=e2, rs=rs, mult_eff=mult_eff, first=first, valid=valid)


def _scan_rows(a, b, a_s, b_s, out_ref, lp, reverse):
    sub = lax.broadcasted_iota(jnp.int32, (lp, LRU_TILE), 0) & 7
    for dist in (1, 2, 4):
        shift = lp - dist if reverse else dist
        keep = (sub + dist <= 7) if reverse else (sub >= dist)
        a_sh = pltpu.roll(a, shift, 0)
        b_sh = pltpu.roll(b, shift, 0)
        b = jnp.where(keep, a * b_sh + b, b)
        a = jnp.where(keep, a * a_sh, a)
    a_s[...] = a
    b_s[...] = b
    n_groups = lp // 8
    edge = 0 if reverse else 7

    def group(gi, carry):
        r0 = pl.multiple_of(((n_groups - 1 - gi) if reverse else gi) * 8, 8)
        a8 = a_s[pl.ds(r0, 8), :]
        b8 = b_s[pl.ds(r0, 8), :]
        out_ref[pl.ds(r0, 8), :] = a8 * carry + b8
        return a8[edge:edge + 1, :] * carry + b8[edge:edge + 1, :]

    lax.fori_loop(0, n_groups, group, jnp.zeros((1, LRU_TILE), F32), unroll=4)


def _lru_specs(lp):
    seq = lambda col0, stride=1: pl.BlockSpec((lp, LRU_TILE), lambda t, b: (b, col0 + stride * t))
    cw = pl.BlockSpec((1, CONV_K, LRU_TILE), lambda t, b: (t, 0, 0))
    vec = pl.BlockSpec((1, LRU_TILE), lambda t, b: (0, t))
    mat = pl.BlockSpec((1, LRU_TILE, LRU_TILE), lambda t, b: (t, 0, 0))
    return seq, cw, vec, mat


def _lru_fwd_call(z, cw, cb, wa, ba, wx, bx, lam, nb, lp, comm=None):
    rows = nb * lp
    seq, cwspec, vec, mat = _lru_specs(lp)

    def body(u_ref, g_ref, cw_ref, cb_ref, wa_ref, ba_ref, wx_ref, bx_ref, lam_ref, y_ref, hs_ref, a_s, b_s):
        m = _lru_gates(u_ref[...], cw_ref[0], cb_ref[...], wa_ref[0], ba_ref[...], wx_ref[0], bx_ref[...],
                       lam_ref[...], lp)
        a = jnp.where(m["valid"], m["a"], 0.0)
        b = jnp.where(m["valid"], m["mult_eff"] * (m["i"] * m["xc"]), 0.0)
        _scan_rows(a, b, a_s, b_s, hs_ref, lp, reverse=False)
        gl, _ = _gelu(g_ref[...])
        y_ref[...] = hs_ref[...] * gl

    oshape = jax.ShapeDtypeStruct((rows, LRU_W), F32)
    return _hosted_call(
        body, name="lru_fwd", grid=(N_LRU_TILES, nb),
        in_specs=[seq(Z_U // LRU_TILE, 2), seq(Z_U // LRU_TILE + 1, 2), cwspec, vec, mat, vec, mat, vec, vec],
        out_specs=[seq(0), seq(0)], out_shape=[oshape, oshape],
        scratch_shapes=[pltpu.VMEM((lp, LRU_TILE), F32), pltpu.VMEM((lp, LRU_TILE), F32)],
        dims=("parallel", "parallel"), args=(z, z, cw, cb, wa, ba, wx, bx, lam), comm=comm)


def _lru_bwd_call(z, dz, hs, dy, cw, cb, wa, ba, wx, bx, lam, nb, lp, comm=None):
    rows = nb * lp
    seq, cwspec, vec, mat = _lru_specs(lp)

    def body(u_ref, g_ref, hs_ref, dy_ref, cw_ref, cb_ref, wa_ref, ba_ref, wx_ref, bx_ref, lam_ref, dz_in,
             dz_ref, dcw_ref, dcb_ref, dwa_ref, dba_ref, dwx_ref, dbx_ref, dlam_ref, a_s, b_s, d_s):
        du_ref = dz_ref.at[:, 0:LRU_TILE]
        dg_ref = dz_ref.at[:, LRU_TILE:2 * LRU_TILE]
        b_idx = pl.program_id(1)
        u = u_ref[...]
        cw = cw_ref[0]
        wa, wx = wa_ref[0], wx_ref[0]
        lam = lam_ref[...]
        m = _lru_gates(u, cw, cb_ref[...], wa, ba_ref[...], wx, bx_ref[...], lam, lp)
        gate = g_ref[...]
        gl, th = _gelu(gate)
        dy = dy_ref[...]
        hs = hs_ref[...]
        dg_ref[...] = (dy * hs * _gelu_grad(gate, th)).astype(BF16)
        a_eff = jnp.where(m["valid"], m["a"], 0.0)
        _scan_rows(pltpu.roll(a_eff, lp - 1, 0), dy * gl, a_s, b_s, d_s, lp, reverse=True)
        ds = d_s[...]
        xc, r, i = m["xc"], m["r"], m["i"]
        row = lax.broadcasted_iota(jnp.int32, (lp, LRU_TILE), 0)
        da = ds * jnp.where(row >= 1, pltpu.roll(hs, 1, 0), 0.0)
        db = jnp.where(m["valid"], ds, 0.0)
        di = db * m["mult_eff"] * xc
        dxc = db * m["mult_eff"] * i
        live = m["valid"] & jnp.logical_not(m["first"])
        dm = jnp.where(live, db * i * xc, 0.0)
        dla = da * m["a"] - dm * (m["e2"] * m["rs"])
        dr = dla * (-C_RGLRU * m["sp"])
        dsp = jnp.sum(dla * (-C_RGLRU * r), axis=0, keepdims=True)
        dpr = (dr * r * (1.0 - r))
        dpi = (di * i * (1.0 - i))
        dprb, dpib = dpr.astype(BF16), dpi.astype(BF16)
        dxc = dxc + _dot_nt(dprb, wa) + _dot_nt(dpib, wx)
        du = (cw[3:4, :] * dxc + cw[2:3, :] * pltpu.roll(dxc, lp - 1, 0) + cw[1:2, :] * pltpu.roll(dxc, lp - 2, 0)
              + cw[0:1, :] * pltpu.roll(dxc, lp - 3, 0))
        du_ref[...] = jnp.where(m["valid"], du, 0.0).astype(BF16)
        tap = lax.broadcasted_iota(jnp.int32, (CONV_K, LRU_TILE), 0)
        dcw = jnp.zeros((CONV_K, LRU_TILE), F32)
        for kk in range(CONV_K):
            shifted = u if kk == CONV_K - 1 else pltpu.roll(u, CONV_K - 1 - kk, 0)
            dcw = jnp.where(tap == kk, jnp.sum(dxc * shifted, axis=0, keepdims=True), dcw)
        parts = [(dcw_ref, dcw[None]), (dcb_ref, jnp.sum(dxc, axis=0, keepdims=True)[None]),
                 (dwa_ref, _dot_tn(m["xcb"], dprb)[None]), (dba_ref, jnp.sum(dpr, axis=0, keepdims=True)[None]),
                 (dwx_ref, _dot_tn(m["xcb"], dpib)[None]), (dbx_ref, jnp.sum(dpi, axis=0, keepdims=True)[None]),
                 (dlam_ref, (dsp * (-jax.nn.sigmoid(-lam)))[None])]

        @pl.when(b_idx == 0)
        def _():
            for ref, val in parts:
                ref[...] = val

        @pl.when(b_idx != 0)
        def _():
            for ref, val in parts:
                ref[...] += val

    vec3 = pl.BlockSpec((1, 1, LRU_TILE), lambda t, b: (t, 0, 0))
    vshape = jax.ShapeDtypeStruct((N_LRU_TILES, 1, LRU_TILE), F32)
    mshape = jax.ShapeDtypeStruct((N_LRU_TILES, LRU_TILE, LRU_TILE), F32)
    pair = pl.BlockSpec((lp, 2 * LRU_TILE), lambda t, b: (b, Z_U // (2 * LRU_TILE) + t))
    return _hosted_call(
        body, name="lru_bwd", grid=(N_LRU_TILES, nb),
        in_specs=[seq(Z_U // LRU_TILE, 2), seq(Z_U // LRU_TILE + 1, 2), seq(0), seq(0), cwspec, vec, mat, vec, mat,
                  vec, vec, pl.BlockSpec(memory_space=pl.ANY)],
        out_specs=[pair, cwspec, vec3, mat, vec3, mat, vec3, vec3],
        out_shape=[jax.ShapeDtypeStruct(dz.shape, dz.dtype), jax.ShapeDtypeStruct((N_LRU_TILES, CONV_K, LRU_TILE), F32),
                   vshape, mshape, vshape, mshape, vshape, vshape],
        scratch_shapes=[pltpu.VMEM((lp, LRU_TILE), F32)] * 3,
        dims=("parallel", "arbitrary"), args=(z, z, hs, dy, cw, cb, wa, ba, wx, bx, lam, dz), comm=comm,
        aliases={11: 0})


def _mix_out_call(ya, yl, ga, gl, wout, h, next_gain, tm):
    rows, d = h.shape

    def body(ya_ref, yl_ref, ga_ref, gl_ref, w_ref, h_ref, ng_ref, y_ref, o_ref, u_ref):
        a = ya_ref[...]
        l = yl_ref[...]
        an = (a * _rms(a, MLA_W) * ga_ref[...]).astype(BF16)
        ln = (l * _rms(l, LRU_W) * gl_ref[...]).astype(BF16)
        y_ref[:, 0:MLA_W] = an
        y_ref[:, MLA_W:MLA_W + LRU_W] = ln
        out = h_ref[...] + _dot(an, w_ref[0:MLA_W, :]) + _dot(ln, w_ref[MLA_W:MLA_W + LRU_W, :])
        o_ref[...] = out
        u_ref[...] = (out * _rms(out, d) * ng_ref[...]).astype(BF16)

    half = pl.BlockSpec((tm, MLA_W), lambda i: (i, 0))
    g = pl.BlockSpec((1, MLA_W), lambda i: (0, 0))
    full = pl.BlockSpec((tm, d), lambda i: (i, 0))
    return pl.pallas_call(
        body, name="mix_out", grid=(rows // tm,),
        in_specs=[half, half, g, g, pl.BlockSpec((MLA_W + LRU_W, d), lambda i: (0, 0)), full,
                  pl.BlockSpec((1, d), lambda i: (0, 0))],
        out_specs=[full, full, full],
        out_shape=[jax.ShapeDtypeStruct((rows, MLA_W + LRU_W), BF16), jax.ShapeDtypeStruct((rows, d), F32),
                   jax.ShapeDtypeStruct((rows, d), BF16)],
        compiler_params=_params(("parallel",)))(ya, yl, ga, gl, wout, h, next_gain)


def _ffn_dact_call(name, dhb, wd, gate, up, tm, comm=None):
    rows, d = dhb.shape
    ns, fs, _ = wd.shape

    nsub = 2 if tm % 32 == 0 else 1
    sub = tm // nsub

    def body(dh_ref, wd_ref, g_ref, p_ref, dg_ref, dp_ref):
        wd = wd_ref[0]
        for r in range(nsub):
            rs = slice(r * sub, (r + 1) * sub)
            da = (0.5 * _dot_nt(dh_ref[rs, :], wd)).astype(BF16)
            g = g_ref[0, rs, :]
            p = p_ref[0, rs, :]
            sg = jax.nn.sigmoid(g)
            dg_ref[0, rs, :] = (da * p) * (sg * (1.0 + g * (1.0 - sg)))
            dp_ref[0, rs, :] = da * (g * sg)

    aspec = pl.BlockSpec((1, tm, fs), lambda s, i: (s, i, 0))
    oshape = jax.ShapeDtypeStruct((ns, rows, fs), BF16)
    return _hosted_call(
        body, name=name, grid=(ns, rows // tm),
        in_specs=[pl.BlockSpec((tm, d), lambda s, i: (i, 0)), pl.BlockSpec((1, fs, d), lambda s, i: (s, 0, 0)),
                  aspec, aspec],
        out_specs=[aspec, aspec], out_shape=[oshape, oshape],
        dims=("parallel", "parallel"), args=(dhb, wd, gate, up), comm=comm)


def _norm_in_bwd_call(name, pieces, h, g, dres, tm, comm=None, part=(0, 1), prev=None, mix=None):
    rows, d = h.shape
    npc = len(pieces)
    steps = rows // tm // part[1]
    off = part[0] * steps
    n_prev = 0 if prev is None else 2
    n_mix = 0 if mix is None else 5

    def body(*refs):
        d_refs = refs[0:2 * npc:2]
        w_refs = refs[1:2 * npc:2]
        h_ref, g_ref, dres_ref = refs[2 * npc:2 * npc + 3]
        mix_in_refs = refs[2 * npc + 3:2 * npc + 3 + n_mix]
        dh_ref, dhb_ref, dg_ref = refs[2 * npc + 3 + n_mix + n_prev:2 * npc + 6 + n_mix + n_prev]
        mix_out_refs = refs[2 * npc + 6 + n_mix + n_prev:]
        du = jnp.zeros((tm, d), F32)
        for d_ref, w_ref in zip(d_refs, w_refs):
            if len(d_ref.shape) == 3:
                for s in range(d_ref.shape[0]):
                    du = du + _dot(d_ref[s], w_ref[s])
            else:
                du = du + _dot(d_ref[...], w_ref[...])
        x = h_ref[...]
        r = _rms(x, d)
        n = x * r
        dh = dres_ref[...] + _rms_bwd(du * g_ref[...], n, r, d)
        dhb = dh.astype(BF16)
        dh_ref[...] = dh
        dhb_ref[...] = dhb
        sums = [(dg_ref, jnp.sum(du * n, axis=0, keepdims=True))]
        if mix is not None:
            wo_ref, ya_ref, yl_ref, ga_ref, gl_ref = mix_in_refs
            dya_ref, dyl_ref, dga_ref, dgl_ref = mix_out_refs
            dy = _dot_nt(dhb, wo_ref[...])
            for val, gain_ref, lo, out_ref, acc_ref in ((ya_ref[...], ga_ref, 0, dya_ref, dga_ref),
                                                        (yl_ref[...], gl_ref, MLA_W, dyl_ref, dgl_ref)):
                rb = _rms(val, MLA_W)
                nb_ = val * rb
                dyn = dy[:, lo:lo + MLA_W]
                out_ref[...] = _rms_bwd(dyn * gain_ref[...], nb_, rb, MLA_W)
                sums.append((acc_ref, jnp.sum(dyn * nb_, axis=0, keepdims=True)))

        @pl.when(pl.program_id(0) == 0)
        def _():
            for ref, val in sums:
                ref[...] = val

        @pl.when(pl.program_id(0) != 0)
        def _():
            for ref, val in sums:
                ref[...] += val

    in_specs, args = [], []
    for dd, w in pieces:
        if dd.ndim == 3:
            in_specs.append(pl.BlockSpec((dd.shape[0], tm, dd.shape[2]), lambda i: (0, i + off, 0)))
            in_specs.append(_resident(w.shape))
        else:
            in_specs.append(pl.BlockSpec((tm, dd.shape[1]), lambda i: (i + off, 0)))
            in_specs.append(_resident(w.shape))
        args += [dd, w]
    full = pl.BlockSpec((tm, d), lambda i: (i + off, 0))
    gspec = pl.BlockSpec((1, d), lambda i: (0, 0))
    half = pl.BlockSpec((tm, MLA_W), lambda i: (i + off, 0))
    hgain = pl.BlockSpec((1, MLA_W), lambda i: (0, 0))
    mix_in_specs = [] if mix is None else [_resident(mix[0].shape), half, half, hgain, hgain]
    mix_out_specs = [] if mix is None else [half, half, hgain, hgain]
    mix_out_shapes = [] if mix is None else [
        jax.ShapeDtypeStruct((rows, MLA_W), F32), jax.ShapeDtypeStruct((rows, LRU_W), F32),
        jax.ShapeDtypeStruct((1, MLA_W), F32), jax.ShapeDtypeStruct((1, LRU_W), F32)]
    n_in = len(in_specs) + 3 + n_mix
    return _hosted_call(
        body, name=name, grid=(steps,),
        in_specs=in_specs + [full, gspec, full] + mix_in_specs + _any_specs(n_prev),
        out_specs=[full, full, gspec] + mix_out_specs,
        out_shape=[jax.ShapeDtypeStruct((rows, d), F32), jax.ShapeDtypeStruct((rows, d), BF16),
                   jax.ShapeDtypeStruct((1, d), F32)] + mix_out_shapes,
        args=(*args, h, g, dres, *(mix or ()), *(prev or ())), comm=comm,
        aliases={n_in: 0, n_in + 1: 1} if prev is not None else {})


def _wgrad_call(name, a, b, scale=1.0, comm=None):
    a3, b3 = a.ndim == 3, b.ndim == 3
    ns = a.shape[0] if a3 else (b.shape[0] if b3 else 1)
    rows, m = a.shape[-2:]
    n = b.shape[-1]
    tmm = m if a3 else _col_tile(m, 256)

    def body(a_ref, b_ref, o_ref):
        av = a_ref[0] if a3 else a_ref[...]
        bv = b_ref[0] if b3 else b_ref[...]
        res = _dot_tn(av, bv)
        if scale != 1.0:
            res = res * scale
        if a3 or b3:
            o_ref[0] = res
        else:
            o_ref[...] = res

    aspec = (pl.BlockSpec((1, rows, tmm), lambda s, j: (s, 0, j)) if a3
             else pl.BlockSpec((rows, tmm), lambda s, j: (0, j)))
    bspec = (pl.BlockSpec((1, rows, n), lambda s, j: (s, 0, 0)) if b3
             else pl.BlockSpec((rows, n), lambda s, j: (0, 0)))
    if a3 or b3:
        ospec = pl.BlockSpec((1, tmm, n), lambda s, j: (s, j, 0))
        oshape = jax.ShapeDtypeStruct((ns, m, n), F32)
    else:
        ospec = pl.BlockSpec((tmm, n), lambda s, j: (j, 0))
        oshape = jax.ShapeDtypeStruct((m, n), F32)
    return _hosted_call(
        body, name=name, grid=(ns, m // tmm), in_specs=[aspec, bspec], out_specs=[ospec], out_shape=[oshape],
        dims=("parallel", "parallel"), args=(a, b), comm=comm)[0]


def _mla_prep_bwd_call(z, dq, dk, dv, gql, gkvl, gqh, gkh, wuq, wuk, wuv, tabs, lp, tm, comm=None):
    rows = z.shape[0]
    tpe = lp // tm

    def body(z_ref, dq_ref, dk_ref, dv_ref, gql_ref, gkvl_ref, gqh_ref, gkh_ref, wuq_ref, wuk_ref, wuv_ref,
             c_ref, s1_ref, s2_ref, dz_ref, dqp_ref, dkn_ref, dvv_ref, dgql_ref, dgkvl_ref, dgqh_ref, dgkh_ref):
        gql, gkvl = gql_ref[...], gkvl_ref[...]
        gq, gk = gqh_ref[...], gkh_ref[...]
        wuq, wuk, wuv = wuq_ref[...], wuk_ref[...], wuv_ref[...]
        m = _mla_heads(z_ref[...], gql, gkvl, wuq, wuk, wuv)
        c, s1, s2 = c_ref[...], s1_ref[...], s2_ref[...]
        dgq = jnp.zeros((1, D_QKP), F32)
        dgk = jnp.zeros((1, D_QKP), F32)
        dkr = jnp.zeros((tm, D_QKP - D_NOPE), F32)
        for hd in range(HEADS):
            qn, rqh, knn, krn, rkh = m["heads"][hd]
            dqg = jnp.concatenate([dq_ref[hd, :, 0:D_NOPE].astype(F32),
                                   _rope_t(dq_ref[hd, :, D_NOPE:D_QKP].astype(F32), c, s1, s2)], axis=1)
            dgq = dgq + jnp.sum(dqg * qn, axis=0, keepdims=True)
            dqn = dqg * gq
            dqr = rqh * (dqn - qn * (jnp.sum(dqn * qn, axis=-1, keepdims=True) * (1.0 / D_QK)))
            dqp_ref[:, hd * D_QKP:(hd + 1) * D_QKP] = dqr.astype(BF16)
            kn_full = jnp.concatenate([knn, krn], axis=1)
            dkg = jnp.concatenate([dk_ref[hd, :, 0:D_NOPE].astype(F32),
                                   _rope_t(dk_ref[hd, :, D_NOPE:D_QKP].astype(F32), c, s1, s2)], axis=1)
            dgk = dgk + jnp.sum(dkg * kn_full, axis=0, keepdims=True)
            dkn = dkg * gk
            dkraw = rkh * (dkn - kn_full * (jnp.sum(dkn * kn_full, axis=-1, keepdims=True) * (1.0 / D_QK)))
            dkn_ref[:, hd * D_NOPE:(hd + 1) * D_NOPE] = dkraw[:, 0:D_NOPE].astype(BF16)
            dkr = dkr + dkraw[:, D_NOPE:D_QKP]
            dvv_ref[:, hd * D_V:(hd + 1) * D_V] = dv_ref[hd]
        dcqn = _dot(dqp_ref[...], wuq)
        dckvn = _dot_nt(dkn_ref[...], wuk) + _dot_nt(dvv_ref[...], wuv)
        dz_ref[:, 0:Q_RANK] = _rms_bwd(dcqn * gql, m["nq"], m["rq"], Q_RANK).astype(BF16)
        dz_ref[:, Q_RANK:Z_KR] = _rms_bwd(dckvn * gkvl, m["nkv"], m["rkv"], KV_RANK).astype(BF16)
        dz_ref[:, Z_KR:Z_MLA] = dkr.astype(BF16)
        parts = [(dgql_ref, jnp.sum(dcqn * m["nq"], axis=0, keepdims=True)),
                 (dgkvl_ref, jnp.sum(dckvn * m["nkv"], axis=0, keepdims=True)), (dgqh_ref, dgq), (dgkh_ref, dgk)]

        @pl.when(pl.program_id(0) == 0)
        def _():
            for ref, val in parts:
                ref[...] = val

        @pl.when(pl.program_id(0) != 0)
        def _():
            for ref, val in parts:
                ref[...] += val

    def const(shape):
        return pl.BlockSpec(shape, lambda i: tuple(0 for _ in shape))

    tab = pl.BlockSpec((tm, 128), lambda i: (i % tpe, 0))
    hq = pl.BlockSpec((HEADS, tm, D_QKP), lambda i: (0, i, 0))
    hv = pl.BlockSpec((HEADS, tm, D_V), lambda i: (0, i, 0))

    def rowspec(n):
        return pl.BlockSpec((tm, n), lambda i: (i, 0))

    return _hosted_call(
        body, name="mla_prep_bwd", grid=(rows // tm,),
        in_specs=[rowspec(Z_MLA), hq, hq, hv, const((1, Q_RANK)), const((1, KV_RANK)), const((1, D_QKP)),
                  const((1, D_QKP)), const((HEADS * D_QKP, Q_RANK)), const((KV_RANK, HEADS * D_NOPE)),
                  const((KV_RANK, HEADS * D_V)), tab, tab, tab],
        out_specs=[rowspec(Z_MLA), rowspec(HEADS * D_QKP), rowspec(HEADS * D_NOPE), rowspec(HEADS * D_V),
                   const((1, Q_RANK)), const((1, KV_RANK)), const((1, D_QKP)), const((1, D_QKP))],
        out_shape=[jax.ShapeDtypeStruct((rows, Z_W), BF16), jax.ShapeDtypeStruct((rows, HEADS * D_QKP), BF16),
                   jax.ShapeDtypeStruct((rows, HEADS * D_NOPE), BF16), jax.ShapeDtypeStruct((rows, HEADS * D_V), BF16),
                   jax.ShapeDtypeStruct((1, Q_RANK), F32), jax.ShapeDtypeStruct((1, KV_RANK), F32),
                   jax.ShapeDtypeStruct((1, D_QKP), F32), jax.ShapeDtypeStruct((1, D_QKP), F32)],
        args=(z, dq, dk, dv, gql, gkvl, gqh, gkh, wuq, wuk, wuv, *tabs), comm=comm)


def _local_step(h0, target, w, nb, lp, sched=None):
    tm = _row_tile(nb * lp, 1408)
    te = _row_tile(lp, 512)
    tabs = _rope_tables(lp)
    g = {}
    if sched is None:
        host = lambda stage: None
    else:
        sched.g = g
        host = sched.host

    def ffn_act(tag, u, split):
        if split:
            gate = _ffn_gate_call(tag + "_gate", u, w[tag + "_w_gate"], tm, host(tag + "_gate"))
            up, act = _ffn_upact_call(tag + "_upact", u, w[tag + "_w_up"], gate, tm, host(tag + "_upact"))
        else:
            gate, up, act = _ffn_up_call(tag + "_up", u, w[tag + "_w_gate"], w[tag + "_w_up"], tm, host(tag + "_up"))
        return u, gate, up, act

    def ffn_bwd(tag, h, saved, dh, dhb, split, mix=None):
        u, gate, up, act = saved
        dgate, dup = _ffn_dact_call(tag + "_dact", dhb, w[tag + "_w_down"], gate, up, tm, host(tag + "_dact"))
        g[tag + "_w_down"] = _wgrad_call(tag + "_dwd", act, dhb, 0.5, host(tag + "_dwd"))
        g[tag + "_w_gate"] = _wgrad_call(tag + "_dwg", dgate, u, 1.0, host(tag + "_dwg"))
        g[tag + "_w_up"] = _wgrad_call(tag + "_dwu", dup, u, 1.0, host(tag + "_dwu"))
        pieces = [(dgate, w[tag + "_w_gate"]), (dup, w[tag + "_w_up"])]
        if not split:
            res = _norm_in_bwd_call(tag + "_din", pieces, h, w[tag + "_norm"], dh, te, host(tag + "_din"), mix=mix)
            g[tag + "_norm"] = res[2]
            return (res[0], res[1], *res[3:])
        dh_a, dhb_a, dg_a = _norm_in_bwd_call(tag + "_din_a", pieces, h, w[tag + "_norm"], dh, te,
                                              host(tag + "_din_a"), part=(0, 2))
        dh_in, dhb_in, dg_b = _norm_in_bwd_call(tag + "_din_b", pieces, h, w[tag + "_norm"], dh, te,
                                                host(tag + "_din_b"), part=(1, 2), prev=(dh_a, dhb_a))
        g[tag + "_norm"] = dg_a + dg_b
        return dh_in, dhb_in

    s1 = ffn_act("ffn1", _rmsnorm_call("ffn1_norm", h0, w["ffn1_norm"], te), True)
    h1, un = _ffn_down_call("ffn1_down", s1[3], w["ffn1_w_down"], h0, te, host("ffn1_down"), w["mix_norm"])
    z = _mm_call("mix_in", un, w["w_in"], tm, F32)
    mla_w = (w["q_latent_norm"], w["kv_latent_norm"], w["q_head_norm"], w["k_head_norm"], w["w_uq"], w["w_uk"],
             w["w_uv"])
    q, k, v, cqn, ckvn = _mla_prep_call(z, *mla_w, tabs, lp, te)
    o, lse = _attn_fwd_call(q, k, v, nb, lp, host("attn_fwd"))
    lru_w = (w["conv_w"], w["conv_b"], w["gate_a_w"], w["gate_a_b"], w["gate_x_w"], w["gate_x_b"], w["lru_lambda"])
    yl, hs = _lru_fwd_call(z, *lru_w, nb, lp, host("lru_fwd"))
    y, h2, u2 = _mix_out_call(o, yl, w["attn_out_norm"], w["lru_out_norm"], w["w_out"], h1, w["ffn2_norm"], te)
    s2 = ffn_act("ffn2", u2, False)
    dh3, dh3b, g["final_norm"], loss = _ffn_down_loss_call("ffn2_down", s2[3], w["ffn2_w_down"], h2, w["final_norm"],
                                                           target, lp, te)
    g["loss"] = loss

    dh2, dh2b, dya, dyl, g["attn_out_norm"], g["lru_out_norm"] = ffn_bwd(
        "ffn2", h2, s2, dh3, dh3b, False, (w["w_out"], o, yl, w["attn_out_norm"], w["lru_out_norm"]))
    g["w_out"] = _wgrad_call("dw_out", y, dh2b)
    dq, dk, dv = _attn_bwd_call(q, k, v, o, lse, dya, nb, lp, host("attn_bwd"))
    (dz_mla, dqp, dkn, dvv, g["q_latent_norm"], g["kv_latent_norm"], g["q_head_norm"],
     g["k_head_norm"]) = _mla_prep_bwd_call(z, dq, dk, dv, *mla_w, tabs, lp, te, host("mla_prep_bwd"))
    g["w_uq"] = _wgrad_call("dw_uq", dqp, cqn)
    g["w_uk"] = _wgrad_call("dw_uk", ckvn, dkn)
    g["w_uv"] = _wgrad_call("dw_uv", ckvn, dvv)
    (dz, g["conv_w"], g["conv_b"], g["gate_a_w"], g["gate_a_b"], g["gate_x_w"], g["gate_x_b"],
     g["lru_lambda"]) = _lru_bwd_call(z, dz_mla, hs, dyl, *lru_w, nb, lp, host("lru_bwd"))
    g["w_in"] = _wgrad_call("dw_in", dz, un)
    dh1, dh1b, g["mix_norm"] = _norm_in_bwd_call("mix_din", [(dz, w["w_in"])], h1, w["mix_norm"], dh2, te,
                                                 host("mix_din"))
    dh0 = ffn_bwd("ffn1", h0, s1, dh1, dh1b, True)[0]
    return loss, dh0, g


def _place():
    x, y, c = lax.axis_index("x"), lax.axis_index("y"), lax.axis_index("c")
    return x, y, c, [(1 - x, y), (x, 1 - y), (1 - x, 1 - y)]


def _any_specs(n):
    return [pl.BlockSpec(memory_space=pl.ANY)] * n


def _remote(src, dst, sems, k, dev):
    send_sems, recv_sems, base = sems
    return pltpu.make_async_remote_copy(src_ref=src, dst_ref=dst, send_sem=send_sems.at[base + k],
                                        recv_sem=recv_sems.at[base + k], device_id=dev, device_id_type=MESH)


EW_VMEM_BYTES = 24 * 1024 * 1024


def _fit_rows(rows, cols, blocks):
    return _row_tile(rows, max(16, int(EW_VMEM_BYTES // (8 * blocks)) // cols))


class _Geom:
    def __init__(self, n0, n1, blocks=1.0):
        self.n0, self.n1 = n0, n1
        self.axis = 0 if n0 % 32 == 0 else 1
        self.h0, self.h1 = (n0 // 2, n1) if self.axis == 0 else (n0, n1 // 2)
        self.tr = _fit_rows(self.h0, self.h1, blocks)
        self.nblk = self.h0 // self.tr

    def half_ref(self, ref, lead, idx):
        if self.axis == 0:
            return ref.at[(*lead, pl.ds(idx * self.h0, self.h0))]
        return ref.at[(*lead, slice(None), pl.ds(idx * self.h1, self.h1))]

    def half_block(self, lead, i, idx):
        return (*lead, idx * self.nblk + i, 0) if self.axis == 0 else (*lead, i, idx)


class _Comm:
    def __init__(self, ins, out_shapes, aliases, n_sems, start, finish, deliver):
        self.ins, self.out_shapes, self.aliases, self.n_sems = list(ins), list(out_shapes), dict(aliases), n_sems
        self.start, self.finish, self.deliver = start, finish, deliver

    def scratch(self):
        return [pltpu.SemaphoreType.DMA((self.n_sems,)), pltpu.SemaphoreType.DMA((self.n_sems,))]


def _comm_call(name, comm):
    n_in = len(comm.ins)

    def body(*refs):
        ins, outs, sems = refs[:n_in], refs[n_in:-2], (*refs[-2:], 0)
        comm.start(ins, outs, sems)
        comm.finish(ins, outs, sems)

    res = pl.pallas_call(
        body, name=name, out_shape=comm.out_shapes, in_specs=_any_specs(n_in),
        out_specs=_any_specs(len(comm.out_shapes)), input_output_aliases=comm.aliases,
        scratch_shapes=comm.scratch())(*comm.ins)
    return comm.deliver(list(res))


def _hosted_call(body, *, name, grid, in_specs, out_specs, out_shape, args, scratch_shapes=(), dims=None, comm=None,
                 prefetch=None, aliases=None):
    aliases = dict(aliases or {})
    in_specs, out_specs, out_shape = list(in_specs), list(out_specs), list(out_shape)
    n_pre = 0 if prefetch is None else 1

    def call(fn, in_specs, out_specs, out_shape, scratch, aliases, dims, args):
        if prefetch is None:
            return pl.pallas_call(
                fn, name=name, grid=grid, in_specs=in_specs, out_specs=out_specs, out_shape=out_shape,
                scratch_shapes=scratch, input_output_aliases=aliases, compiler_params=_params(dims))(*args)
        spec = pltpu.PrefetchScalarGridSpec(num_scalar_prefetch=1, grid=grid, in_specs=in_specs, out_specs=out_specs,
                                            scratch_shapes=scratch)
        return pl.pallas_call(
            fn, name=name, grid_spec=spec, out_shape=out_shape,
            input_output_aliases={i + 1: o for i, o in aliases.items()}, compiler_params=_params(dims))(prefetch, *args)

    if comm is None:
        return list(call(body, in_specs, out_specs, out_shape, list(scratch_shapes), aliases,
                         dims or ("arbitrary",) * len(grid), args))
    n_in, n_out, n_ci, n_co = len(in_specs), len(out_specs), len(comm.ins), len(comm.out_shapes)

    def wrapped(*refs):
        pre, refs = refs[:n_pre], refs[n_pre:]
        ins, cins = refs[:n_in], refs[n_in:n_in + n_ci]
        outs = refs[n_in + n_ci:n_in + n_ci + n_out]
        couts = refs[n_in + n_ci + n_out:n_in + n_ci + n_out + n_co]
        scratch, sems = refs[n_in + n_ci + n_out + n_co:-2], (*refs[-2:], 0)
        first = functools.reduce(jnp.logical_and, [pl.program_id(k) == 0 for k in range(len(grid))])
        last = functools.reduce(jnp.logical_and, [pl.program_id(k) == grid[k] - 1 for k in range(len(grid))])

        @pl.when(first)
        def _():
            comm.start(cins, couts, sems)

        body(*pre, *ins, *outs, *scratch)

        @pl.when(last)
        def _():
            comm.finish(cins, couts, sems)

    res = call(wrapped, in_specs + _any_specs(n_ci), out_specs + _any_specs(n_co), out_shape + comm.out_shapes,
               list(scratch_shapes) + comm.scratch(),
               {**aliases, **{n_in + i: n_out + o for i, o in comm.aliases.items()}},
               ("arbitrary",) * len(grid), (*args, *comm.ins))
    comm.deliver(list(res[n_out:]))
    return list(res[:n_out])


def _gather_comm(bufs, deliver):
    n = len(bufs)
    geoms = [_Geom(*b.shape[1:]) for b in bufs]

    def first(outs, sems):
        x, y, c, chips = _place()
        cps = []
        for a in range(n):
            mine = geoms[a].half_ref(outs[a], (2 * x + y,), c)
            cps += [_remote(mine, mine, sems, 6 * a + j, (cx, cy, c)) for j, (cx, cy) in enumerate(chips)]
        return cps

    def start(ins, outs, sems):
        for cp in first(outs, sems):
            cp.start()

    def finish(ins, outs, sems):
        x, y, c, chips = _place()
        sib = (x, y, 1 - c)
        passed = []
        for a in range(n):
            for j, (cx, cy) in enumerate(chips):
                land = geoms[a].half_ref(outs[a], (2 * cx + cy,), c)
                _remote(land, land, sems, 6 * a + j, sib).wait_recv()
                cp = _remote(land, land, sems, 6 * a + 3 + j, sib)
                cp.start()
                passed.append(cp)
        for a in range(n):
            for j, (cx, cy) in enumerate(chips):
                land = geoms[a].half_ref(outs[a], (2 * cx + cy,), 1 - c)
                _remote(land, land, sems, 6 * a + 3 + j, sib).wait_recv()
        for cp in first(outs, sems) + passed:
            cp.wait_send()

    return _Comm(bufs, [jax.ShapeDtypeStruct(b.shape, b.dtype) for b in bufs], {a: a for a in range(n)}, 6 * n,
                 start, finish, deliver)


def _reduce_pair_comm(grads, deliver):
    n = len(grads)
    geoms = [_Geom(*a.shape[1:]) for a in grads]

    def copies(ins, outs, sems):
        x, y, c, _ = _place()
        return [_remote(geoms[a].half_ref(ins[a], (slice(None),), 1 - c), outs[a], sems, a, (x, y, 1 - c))
                for a in range(n)]

    def start(ins, outs, sems):
        for cp in copies(ins, outs, sems):
            cp.start()

    def finish(ins, outs, sems):
        cps = copies(ins, outs, sems)
        for cp in cps:
            cp.wait_recv()
        for cp in cps:
            cp.wait_send()

    shapes = [jax.ShapeDtypeStruct((N_SHARD, g.h0, g.h1), a.dtype) for a, g in zip(grads, geoms)]
    return _Comm(grads, shapes, {}, n, start, finish, deliver)


def _reduce_chips_comm(parts, deliver):
    n = len(parts)

    def copies(ins, outs, sems):
        x, y, c, chips = _place()
        return [_remote(ins[a].at[2 * cx + cy], outs[a].at[j], sems, 3 * a + j, (cx, cy, c))
                for a in range(n) for j, (cx, cy) in enumerate(chips)]

    def start(ins, outs, sems):
        for cp in copies(ins, outs, sems):
            cp.start()

    def finish(ins, outs, sems):
        cps = copies(ins, outs, sems)
        for cp in cps:
            cp.wait_recv()
        for cp in cps:
            cp.wait_send()

    shapes = [jax.ShapeDtypeStruct((3,) + a.shape[1:], a.dtype) for a in parts]
    return _Comm(parts, shapes, {}, 3 * n, start, finish, deliver)


def _share_pair_comm(bufs, deliver):
    n = len(bufs)
    geoms = [_Geom(*b.shape) for b in bufs]

    def copies(outs, sems):
        x, y, c, _ = _place()
        cps = []
        for a in range(n):
            mine = geoms[a].half_ref(outs[a], (), c)
            cps.append(_remote(mine, mine, sems, a, (x, y, 1 - c)))
        return cps

    def start(ins, outs, sems):
        for cp in copies(outs, sems):
            cp.start()

    def finish(ins, outs, sems):
        x, y, c, _ = _place()
        for a in range(n):
            land = geoms[a].half_ref(outs[a], (), 1 - c)
            _remote(land, land, sems, a, (x, y, 1 - c)).wait_recv()
        for cp in copies(outs, sems):
            cp.wait_send()

    return _Comm(bufs, [jax.ShapeDtypeStruct(b.shape, b.dtype) for b in bufs], {a: a for a in range(n)}, n,
                 start, finish, deliver)


def _small_comm(pack, deliver):
    r, d = pack.shape

    def copies(ins, outs, sems):
        x, y, c, _ = _place()
        cps = []
        for k in range(1, 8):
            peer = (x ^ ((k >> 2) & 1), y ^ ((k >> 1) & 1), c ^ (k & 1))
            cps.append(_remote(ins[0], outs[0].at[4 * x + 2 * y + c], sems, k - 1, peer))
        return cps

    def start(ins, outs, sems):
        for cp in copies(ins, outs, sems):
            cp.start()

    def finish(ins, outs, sems):
        cps = copies(ins, outs, sems)
        for cp in cps:
            cp.wait_recv()
        for cp in cps:
            cp.wait_send()

    return _Comm([pack], [jax.ShapeDtypeStruct((8, r, d), pack.dtype)], {}, 7, start, finish, deliver)


def _join_comms(comms):
    comms = [c for c in comms if c is not None]
    if len(comms) <= 1:
        return comms[0] if comms else None
    ins, out_shapes, aliases, spans, n_sems = [], [], {}, [], 0
    for c in comms:
        aliases.update({len(ins) + i: len(out_shapes) + o for i, o in c.aliases.items()})
        spans.append((len(ins), len(ins) + len(c.ins), len(out_shapes), len(out_shapes) + len(c.out_shapes), n_sems))
        ins += c.ins
        out_shapes += c.out_shapes
        n_sems += c.n_sems

    def run(which):
        def go(all_ins, all_outs, sems):
            for c, (i0, i1, o0, o1, base) in zip(comms, spans):
                getattr(c, which)(all_ins[i0:i1], all_outs[o0:o1], (sems[0], sems[1], sems[2] + base))
        return go

    def deliver(outs):
        for c, (_, _, o0, o1, _) in zip(comms, spans):
            c.deliver(outs[o0:o1])
        return outs

    return _Comm(ins, out_shapes, aliases, n_sems, run("start"), run("finish"), deliver)


def _ew_call(name, fn, ins, out_dtypes):
    shape = ins[0].shape
    cols = shape[-1]
    rows = 1
    for s_ in shape[:-1]:
        rows *= s_
    ins2 = [a.reshape(rows, cols) for a in ins]
    tr = rows
    for t in range(16, min(rows, max(16, (1 << 19) // cols)) + 1, 16):
        if rows % t == 0:
            tr = t
    no = len(out_dtypes)

    def body(*refs):
        outs = fn(*[r[...] for r in refs[:len(ins2)]])
        for ref, val in zip(refs[len(ins2):], outs):
            ref[...] = val.astype(ref.dtype)

    spec = pl.BlockSpec((tr, cols), lambda i: (i, 0))
    res = pl.pallas_call(
        body, name=name, grid=(rows // tr,), in_specs=[spec] * len(ins2), out_specs=[spec] * no,
        out_shape=[jax.ShapeDtypeStruct((rows, cols), dt) for dt in out_dtypes],
        compiler_params=_params(("parallel",)))(*ins2)
    return [r.reshape(shape) for r in res]


def _adamw_math(w, g, m, v):
    m = ADAM_B1 * m + (1.0 - ADAM_B1) * g
    v = ADAM_B2 * v + (1.0 - ADAM_B2) * (g * g)
    m_hat = m / (1.0 - ADAM_B1 ** ADAM_STEP)
    v_hat = v / (1.0 - ADAM_B2 ** ADAM_STEP)
    delta = -ADAM_LR * (m_hat / (jnp.sqrt(v_hat) + ADAM_EPS) + ADAM_WD * w)
    return delta, m, v


def _adamw_call(name, w, g, m, v):
    return _ew_call(name, _adamw_math, [w, g, m, v], [F32, F32, F32])


def _tiled_call(name, fn, place, grid, in_items, out_items, comm=None):
    ni = len(in_items)

    def body(place_ref, *refs):
        vals = fn(*[r[...] for r in refs[:ni]])
        for ref, val in zip(refs[ni:], vals):
            ref[...] = val.astype(ref.dtype)

    return _hosted_call(
        body, name=name, grid=grid, in_specs=[pl.BlockSpec(blk, imap) for _, blk, imap in in_items],
        out_specs=[pl.BlockSpec(blk, imap) for _, _, blk, imap in out_items],
        out_shape=[jax.ShapeDtypeStruct(shp, dt) for shp, dt, _, _ in out_items],
        args=[a for a, _, _ in in_items], prefetch=place, comm=comm)


def _cast_call(name, place, shards, comm=None):
    n0, n1 = shards[0].shape
    tr = _fit_rows(n0, n1, 1.5 * len(shards))
    ins = [(a, (tr, n1), lambda i, p: (i, 0)) for a in shards]
    outs = [((N_SHARD, n0, n1), BF16, (1, tr, n1), lambda i, p: (p[0], i, 0)) for _ in shards]
    return _tiled_call(name, lambda *v: [x[None] for x in v], place, (n0 // tr,), ins, outs, comm)


def _pair_sum_call(name, place, fulls, gots):
    k = len(fulls)
    g = _Geom(*fulls[0].shape[1:], blocks=2.5 * k)
    blk = (1, g.tr, g.h1)
    ins = [(a, blk, lambda s, i, p: g.half_block((s,), i, p[1])) for a in fulls]
    ins += [(a, blk, lambda s, i, p: (s, i, 0)) for a in gots]
    outs = [((N_SHARD, g.h0, g.h1), BF16, blk, lambda s, i, p: (s, i, 0)) for _ in fulls]
    return _tiled_call(name, lambda *v: [v[j] + v[k + j] for j in range(k)], place, (N_SHARD, g.nblk), ins, outs)


def _chip_sum_call(name, place, fulls, gots, recvs, comm=None):
    k = len(fulls)
    g = _Geom(*fulls[0].shape[1:], blocks=4.5 * k)
    blk = (1, g.tr, g.h1)
    ins = [(a, blk, lambda i, p: g.half_block((p[0],), i, p[1])) for a in fulls]
    ins += [(a, blk, lambda i, p: (p[0], i, 0)) for a in gots]
    ins += [(a, (3, g.tr, g.h1), lambda i, p: (0, i, 0)) for a in recvs]
    outs = [((g.n0, g.n1), F32, (g.tr, g.h1), lambda i, p: g.half_block((), i, p[1])) for _ in fulls]

    def fn(*v):
        res = []
        for j in range(k):
            r = v[2 * k + j].astype(F32)
            res.append(v[j][0] + v[k + j][0] + r[0] + r[1] + r[2])
        return res

    return _tiled_call(name, fn, place, (g.nblk,), ins, outs, comm)


def _adamw_group_call(name, ws, gs, ms, vs, comm=None):
    k = len(ws)
    n0, n1 = ws[0].shape
    tr = _fit_rows(n0, n1, 8 * k)
    spec = pl.BlockSpec((tr, n1), lambda i: (i, 0))

    def body(*refs):
        for j in range(k):
            g = refs[k + j][...]
            delta, m, vv = _adamw_math(refs[j][...], g, refs[2 * k + j][...], refs[3 * k + j][...])
            for ref, val in zip(refs[4 * k + 4 * j:4 * k + 4 * j + 4], (g, delta, m, vv)):
                ref[...] = val

    flat = _hosted_call(
        body, name=name, grid=(n0 // tr,), in_specs=[spec] * (4 * k), out_specs=[spec] * (4 * k),
        out_shape=[jax.ShapeDtypeStruct((n0, n1), F32)] * (4 * k), dims=("parallel",),
        args=(*ws, *gs, *ms, *vs), comm=comm)
    return [flat[4 * j:4 * j + 4] for j in range(k)]


def _small_update_call(me, early, own_early, late, own_late, wp, mp, vp):
    nd, r, d = early.shape

    def body(me_ref, e_ref, oe_ref, l_ref, ol_ref, w_ref, m_ref, v_ref, gs_ref, d_ref, nm_ref, nv_ref):
        mine = me_ref[0]

        def total(g_ref, own_ref):
            acc = None
            for k in range(nd):
                part = jnp.where(mine == k, own_ref[...], g_ref[k])
                acc = part if acc is None else acc + part
            return acc

        gs = total(e_ref, oe_ref)
        ls = total(l_ref, ol_ref)
        gs_ref[...] = gs
        first = gs[0:8] + ls[0:8]
        gs_ref[0:8, :] = first
        gs_ref[ROW_META:ROW_META + N_META, :] = gs[ROW_META:ROW_META + N_META] + ls[8:8 + N_META]
        grads = jnp.concatenate([first, gs[8:SMALL_ADAM_ROWS]], axis=0)
        delta, m, v = _adamw_math(w_ref[...], grads, m_ref[...], v_ref[...])
        d_ref[...] = delta
        nm_ref[...] = m
        nv_ref[...] = v

    vm = pl.BlockSpec(memory_space=pltpu.VMEM)
    ashape = jax.ShapeDtypeStruct((SMALL_ADAM_ROWS, d), F32)
    return pl.pallas_call(
        body, name="small_update", in_specs=[pl.BlockSpec(memory_space=pltpu.SMEM)] + [vm] * 7, out_specs=[vm] * 4,
        out_shape=[jax.ShapeDtypeStruct((r, d), F32), ashape, ashape, ashape],
        compiler_params=pltpu.CompilerParams(vmem_limit_bytes=VMEM_LIMIT_BYTES))(
            me, early, own_early, late, own_late, wp, mp, vp)


SMALL_NAMES = ["ffn1_norm", "mix_norm", "ffn2_norm", "final_norm", "q_latent_norm", "kv_latent_norm",
               "q_head_norm", "k_head_norm", "conv_b", "gate_a_b", "gate_x_b", "lru_lambda", "attn_out_norm",
               "lru_out_norm"]
ROW_CONV_W = 14
ROW_GATE_A = 16
ROW_GATE_X = 48
ROW_META = 80
ROW_LOSS = 96


def _row(a):
    flat = a.reshape(1, -1)
    return jnp.pad(flat, ((0, 0), (0, D_MODEL - flat.shape[1])))


def _pack_small(t, rows):
    parts = [_row(t[nm]) for nm in SMALL_NAMES]
    parts.append(t["conv_w"].reshape(2, D_MODEL))
    parts.append(t["gate_a_w"].reshape(32, D_MODEL))
    parts.append(t["gate_x_w"].reshape(32, D_MODEL))
    p = jnp.concatenate(parts, axis=0)
    return jnp.pad(p, ((0, rows - p.shape[0]), (0, 0)))


def _early_pack(g):
    gs = {nm: g.get(nm, jnp.zeros((1, D_MODEL), F32)) for nm in SMALL_NAMES}
    gs["q_head_norm"] = g["q_head_norm"][:, 0:D_QK]
    gs["k_head_norm"] = g["k_head_norm"][:, 0:D_QK]
    for nm in ("conv_b", "gate_a_b", "gate_x_b", "lru_lambda"):
        gs[nm] = g[nm].reshape(1, LRU_W)
    gs["conv_w"] = g["conv_w"].transpose(1, 0, 2).reshape(CONV_K, LRU_W)
    gs["gate_a_w"] = _gate_blocks(g["gate_a_w"])
    gs["gate_x_w"] = _gate_blocks(g["gate_x_w"])
    return jnp.concatenate([_pack_small(gs, ROW_META), jnp.zeros((N_META, D_MODEL), F32), _row(g["loss"][:, 0:1]),
                            jnp.zeros((SMALL_ROWS - ROW_LOSS - 1, D_MODEL), F32)], axis=0)


def _unpack_small(p, like):
    out = {}
    for k, nm in enumerate(SMALL_NAMES):
        out[nm] = p[k, 0:like[nm].size].reshape(like[nm].shape)
    out["gate_a_w"] = p[ROW_GATE_A:ROW_GATE_A + 32].reshape(like["gate_a_w"].shape)
    out["gate_x_w"] = p[ROW_GATE_X:ROW_GATE_X + 32].reshape(like["gate_x_w"].shape)
    return out


def _gate_dense(wg):
    w4 = wg[0].reshape(N_LRU_TILES, 2, 64, 64)
    zero = jnp.zeros((N_LRU_TILES, 64, 64), wg.dtype)
    top = jnp.concatenate([w4[:, 0], zero], axis=2)
    bot = jnp.concatenate([zero, w4[:, 1]], axis=2)
    return jnp.concatenate([top, bot], axis=1).astype(BF16)


def _gate_blocks(dw):
    return jnp.stack([dw[:, 0:64, 0:64], dw[:, 64:128, 64:128]], axis=1).reshape(8, 64, 64)


BIG_NAMES = ["ffn1_w_gate", "ffn1_w_up", "ffn1_w_down", "w_in", "w_uq", "w_uk", "w_uv", "w_out", "ffn2_w_gate",
             "ffn2_w_up", "ffn2_w_down"]
BIG_GROUPS = [["ffn1_w_gate", "ffn1_w_up", "ffn1_w_down", "ffn2_w_gate", "ffn2_w_up", "ffn2_w_down"], ["w_in"],
              ["w_uq"], ["w_uk", "w_uv"], ["w_out"]]
TRANSPOSED = ("ffn1_w_gate", "ffn1_w_up", "ffn2_w_gate", "ffn2_w_up", "w_in", "w_uq")


def _to2d(nm, a):
    return a[0].T if nm in TRANSPOSED else a[0]


def _from2d(nm, a):
    return (a.T if nm in TRANSPOSED else a)[None]


WEIGHT_NAMES = ["meta_tokens", "ffn1_norm", "ffn1_w_gate", "ffn1_w_up", "ffn1_w_down", "mix_norm", "w_in",
                "q_latent_norm", "w_uq", "kv_latent_norm", "w_uk", "w_uv", "q_head_norm", "k_head_norm", "conv_w",
                "conv_b", "gate_a_w", "gate_a_b", "gate_x_w", "gate_x_b", "lru_lambda", "attn_out_norm",
                "lru_out_norm", "w_out", "ffn2_norm", "ffn2_w_gate", "ffn2_w_up", "ffn2_w_down", "final_norm"]


def _weight_from(nm, slots):
    if nm == "w_in":
        win = slots.reshape(IN_WIDTH, D_MODEL)
        lru = win[Z_KR + D_ROPE:].reshape(2, N_LRU_TILES, LRU_TILE, D_MODEL).transpose(1, 0, 2, 3)
        return jnp.concatenate([win[0:Z_KR + D_ROPE], jnp.zeros((128 - D_ROPE, D_MODEL), BF16),
                                lru.reshape(2 * LRU_W, D_MODEL)], axis=0)
    if nm == "w_uq":
        return jnp.pad(slots, ((0, 0), (0, D_QKP - D_QK), (0, 0))).reshape(HEADS * D_QKP, Q_RANK)
    if nm in ("w_uk", "w_uv"):
        return slots.transpose(1, 0, 2).reshape(KV_RANK, HEADS * D_NOPE)
    if nm == "w_out":
        return slots.reshape(D_MODEL, D_MODEL)
    return slots


def _small_weights(p, small):
    w = {nm: p[nm] for nm in SMALL_NAMES}
    w["q_head_norm"] = jnp.pad(p["q_head_norm"], ((0, 0), (0, D_QKP - D_QK)))
    w["k_head_norm"] = jnp.pad(p["k_head_norm"], ((0, 0), (0, D_QKP - D_QK)))
    w["conv_w"] = small[:, N_META:N_META + 2, :].reshape(N_SHARD, CONV_K, LRU_TILE)
    w["gate_a_w"] = _gate_dense(p["gate_a_w"])
    w["gate_x_w"] = _gate_dense(p["gate_x_w"])
    meta = small[:, 0:N_META, :].transpose(1, 0, 2).reshape(N_META, D_MODEL)
    return w, meta


def _full_weights(p, gathered, small):
    w, meta = _small_weights(p, small)
    w.update({nm: _weight_from(nm, gathered[nm]) for nm in BIG_NAMES})
    return w, meta


def _shard_grad(nm, g):
    if nm == "w_in":
        lru = g[Z_MLA:].reshape(N_LRU_TILES, 2, LRU_TILE, D_MODEL).transpose(1, 0, 2, 3).reshape(2 * LRU_W, D_MODEL)
        return jnp.concatenate([g[0:Z_KR + D_ROPE], lru], axis=0).reshape(N_SHARD, IN_WIDTH // N_SHARD, D_MODEL)
    if nm == "w_uq":
        return g.reshape(HEADS, D_QKP, Q_RANK)[:, 0:D_QK, :]
    if nm in ("w_uk", "w_uv"):
        return g.reshape(KV_RANK, HEADS, D_NOPE).transpose(1, 0, 2)
    if nm == "w_out":
        return g.reshape(N_SHARD, D_MODEL // N_SHARD, D_MODEL)
    return g


def _shard_grads(g):
    return {nm: _shard_grad(nm, g[nm]) for nm in BIG_NAMES}


GATHER_FIRST = ["ffn1_w_gate"]
GATHER_AT = {"ffn1_gate": ["ffn1_w_up"], "ffn1_upact": ["ffn1_w_down"],
             "ffn1_down": ["w_in", "w_uq", "w_uk", "w_uv", "w_out"], "attn_fwd": ["ffn2_w_down", "ffn2_w_gate"],
             "lru_fwd": ["ffn2_w_up"]}
PAIR_AT = [("ffn2_din", ["ffn2_w_gate", "ffn2_w_up", "ffn2_w_down"]),
           ("mix_din", ["w_out", "w_uq", "w_uk", "w_uv", "w_in"]),
           ("ffn1_dwg", ["ffn1_w_down"]), ("ffn1_dwu", ["ffn1_w_gate"]), ("ffn1_din_a", ["ffn1_w_up"])]
CHIPS_AT = [("attn_bwd", ["ffn2_w_down", "ffn2_w_gate"]), ("mla_prep_bwd", ["ffn2_w_up"]),
            ("ffn1_dact", ["w_out", "w_uq", "w_uk", "w_uv", "w_in"]),
            ("ffn1_dwu", ["ffn1_w_down"]), ("ffn1_din_a", ["ffn1_w_gate"]), ("ffn1_din_b", ["ffn1_w_up"])]
SHARE_EARLY_GROUPS, SHARE_EARLY_AT = 3, "ffn1_dwd"
SMALL_EARLY_AT = "mix_din"


def _same_shape_groups(names):
    return [[nm for nm in grp if nm in names] for grp in BIG_GROUPS if any(nm in names for nm in grp)]


class _Sched:
    def __init__(self, place, w, slots):
        self.place, self.w, self.slots = place, w, slots
        self.g = None
        self.sharded, self.from_pair, self.chip_bf16, self.from_chips = {}, {}, {}, {}
        self.early = self.early_all = None
        self.shared = {}

    def host(self, stage):
        comms = []
        if stage in GATHER_AT:
            comms.append(self.gather(GATHER_AT[stage]))
        comms += [self.chips(names) for at, names in CHIPS_AT if at == stage]
        comms += [self.pair(names) for at, names in PAIR_AT if at == stage]
        if stage == SMALL_EARLY_AT:
            comms.append(self.small_early())
        if stage == SHARE_EARLY_AT:
            comms.append(self.share_early())
        return _join_comms(comms)

    def small_early(self):
        self.early = _early_pack(self.g)

        def deliver(outs):
            self.early_all = outs[0]
            return outs

        return _small_comm(self.early, deliver)

    def gather(self, names):
        def deliver(outs):
            self.w.update({nm: _weight_from(nm, o) for nm, o in zip(names, outs)})
            return outs

        return _gather_comm([self.slots[nm] for nm in names], deliver)

    def pair(self, names):
        self.sharded.update({nm: _shard_grad(nm, self.g[nm]) for nm in names})

        def deliver(outs):
            self.from_pair.update(zip(names, outs))
            for grp in _same_shape_groups(names):
                sums = _pair_sum_call("pair_sum_" + grp[0], self.place, [self.sharded[nm] for nm in grp],
                                      [self.from_pair[nm] for nm in grp])
                self.chip_bf16.update(zip(grp, sums))
            return outs

        return _reduce_pair_comm([self.sharded[nm] for nm in names], deliver)

    def chips(self, names):
        def deliver(outs):
            self.from_chips.update(zip(names, outs))
            return outs

        return _reduce_chips_comm([self.chip_bf16[nm] for nm in names], deliver)

    def chip_sums(self, names, comm=None):
        out = {}
        for grp in _same_shape_groups(names):
            sums = _chip_sum_call("chip_sum_" + grp[0], self.place, [self.sharded[nm] for nm in grp],
                                  [self.from_pair[nm] for nm in grp], [self.from_chips[nm] for nm in grp], comm)
            comm = None
            out.update(zip(grp, sums))
        return out

    def share_early(self):
        names = [nm for at, grp in CHIPS_AT[:SHARE_EARLY_GROUPS] for nm in grp]
        mine = self.chip_sums(names)
        return _share_pair_comm([mine[nm] for nm in names], lambda o: self.shared.update(zip(names, o)))


def kernel(x, meta_tokens, ffn1_norm, ffn1_w_gate, ffn1_w_up, ffn1_w_down, mix_norm, w_in, q_latent_norm, w_uq, kv_latent_norm, w_uk, w_uv, q_head_norm, k_head_norm, conv_w, conv_b, gate_a_w, gate_a_b, gate_x_w, gate_x_b, lru_lambda, attn_out_norm, lru_out_norm, w_out, ffn2_norm, ffn2_w_gate, ffn2_w_up, ffn2_w_down, final_norm, loss_target, m_meta_tokens, m_ffn1_norm, m_ffn1_w_gate, m_ffn1_w_up, m_ffn1_w_down, m_mix_norm, m_w_in, m_q_latent_norm, m_w_uq, m_kv_latent_norm, m_w_uk, m_w_uv, m_q_head_norm, m_k_head_norm, m_conv_w, m_conv_b, m_gate_a_w, m_gate_a_b, m_gate_x_w, m_gate_x_b, m_lru_lambda, m_attn_out_norm, m_lru_out_norm, m_w_out, m_ffn2_norm, m_ffn2_w_gate, m_ffn2_w_up, m_ffn2_w_down, m_final_norm, v_meta_tokens, v_ffn1_norm, v_ffn1_w_gate, v_ffn1_w_up, v_ffn1_w_down, v_mix_norm, v_w_in, v_q_latent_norm, v_w_uq, v_kv_latent_norm, v_w_uk, v_w_uv, v_q_head_norm, v_k_head_norm, v_conv_w, v_conv_b, v_gate_a_w, v_gate_a_b, v_gate_x_w, v_gate_x_b, v_lru_lambda, v_attn_out_norm, v_lru_out_norm, v_w_out, v_ffn2_norm, v_ffn2_w_gate, v_ffn2_w_up, v_ffn2_w_down, v_final_norm):
    args = locals()
    p = {nm: args[nm] for nm in WEIGHT_NAMES}
    mom = {nm: args["m_" + nm] for nm in WEIGHT_NAMES}
    var = {nm: args["v_" + nm] for nm in WEIGHT_NAMES}
    nb, seq, d = x.shape
    lp = CHUNK + seq
    xi, yi, ci = lax.axis_index("x"), lax.axis_index("y"), lax.axis_index("c")
    chip = 2 * xi + yi

    place = jnp.stack([chip, ci]).astype(jnp.int32)
    p2 = {nm: _to2d(nm, p[nm]) for nm in BIG_NAMES}
    m2 = {nm: _to2d(nm, mom[nm]) for nm in BIG_NAMES}
    v2 = {nm: _to2d(nm, var[nm]) for nm in BIG_NAMES}

    slots = {}
    small_shard = jnp.concatenate(
        [meta_tokens, conv_w[0].reshape(2, 2 * LRU_TILE), jnp.zeros((14, 2 * LRU_TILE), F32)], axis=0)
    small_slots = lax.dynamic_update_slice(jnp.zeros((N_SHARD,) + small_shard.shape, F32), small_shard[None],
                                           (chip, 0, 0))
    for grp in BIG_GROUPS:
        for nm, buf in zip(grp, _cast_call("cast_" + grp[0], place, [p2[nm] for nm in grp])):
            slots[nm] = buf
    first = _comm_call("gather_first", _gather_comm([slots[nm] for nm in GATHER_FIRST] + [small_slots], lambda o: o))
    w, meta = _small_weights(p, first[-1])
    w.update({nm: _weight_from(nm, o) for nm, o in zip(GATHER_FIRST, first[:-1])})
    sched = _Sched(place, w, slots)

    h0 = jnp.concatenate(
        [jnp.zeros((nb, PAD_ROWS, d), F32), jnp.broadcast_to(meta[None], (nb, N_META, d)), x], axis=1)
    target = jnp.pad(loss_target, ((0, 0), (CHUNK, 0), (0, 0)))
    loss_part, dh0, g = _local_step(h0.reshape(nb * lp, d), target.reshape(nb * lp, d), w, nb, lp, sched)
    dh0 = dh0.reshape(nb, lp, d)
    grad_x = dh0[:, CHUNK:, :]

    late = jnp.concatenate([g["ffn1_norm"], g["mix_norm"], jnp.zeros((6, D_MODEL), F32),
                            jnp.sum(dh0[:, PAD_ROWS:CHUNK, :], axis=0)], axis=0)
    shared = sched.shared
    rest = [nm for at, names in CHIPS_AT[SHARE_EARLY_GROUPS:] if at is not None for nm in names]
    last = [nm for at, names in CHIPS_AT if at is None for nm in names]
    late_box = {}
    mine = sched.chip_sums(rest, _small_comm(late, lambda o: late_box.update(all=o[0])))
    share = _share_pair_comm([mine[nm] for nm in rest], lambda o: shared.update(zip(rest, o)))
    _comm_call("share_pair", _join_comms([share, sched.chips(last) if last else None]))
    if last:
        mine = sched.chip_sums(last)
        _comm_call("share_last", _share_pair_comm([mine[nm] for nm in last], lambda o: shared.update(zip(last, o))))
    late_all = late_box["all"]
    small_like = {nm: p[nm] for nm in SMALL_NAMES + ["gate_a_w", "gate_x_w"]}

    def pack_w(t):
        tt = {nm: t[nm] for nm in SMALL_NAMES + ["gate_a_w", "gate_x_w"]}
        tt["conv_w"] = jnp.zeros((CONV_K, LRU_W), F32)
        return _pack_small(tt, SMALL_ADAM_ROWS)

    me = (4 * xi + 2 * yi + ci).astype(jnp.int32).reshape(1)
    gsum, dsm, msm, vsm = _small_update_call(me, sched.early_all, sched.early, late_all, late, pack_w(p),
                                             pack_w(mom), pack_w(var))
    grads = _unpack_small(gsum, small_like)
    delta = _unpack_small(dsm, small_like)
    new_m = _unpack_small(msm, small_like)
    new_v = _unpack_small(vsm, small_like)
    loss = gsum[ROW_LOSS, 0]
    gmeta = gsum[ROW_META:ROW_META + N_META].reshape(N_META, N_SHARD, D_MODEL // N_SHARD)
    grads["meta_tokens"] = lax.dynamic_index_in_dim(gmeta, chip, axis=1, keepdims=False)
    gconv = gsum[ROW_CONV_W:ROW_CONV_W + 2].reshape(CONV_K, N_SHARD, LRU_TILE)
    grads["conv_w"] = lax.dynamic_index_in_dim(gconv, chip, axis=1, keepdims=False)[None]
    for nm in ("meta_tokens", "conv_w"):
        delta[nm], new_m[nm], new_v[nm] = _adamw_call("adamw_" + nm, p[nm], grads[nm], mom[nm], var[nm])

    for names in BIG_GROUPS:
        res = _adamw_group_call("adamw_" + names[0], [p2[nm] for nm in names], [shared[nm] for nm in names],
                                [m2[nm] for nm in names], [v2[nm] for nm in names])
        for nm, (gg, dd, mm, vv) in zip(names, res):
            grads[nm], delta[nm], new_m[nm], new_v[nm] = (_from2d(nm, t) for t in (gg, dd, mm, vv))

    return (loss, grad_x, *[grads[nm] for nm in WEIGHT_NAMES], *[delta[nm] for nm in WEIGHT_NAMES],
            *[new_m[nm] for nm in WEIGHT_NAMES], *[new_v[nm] for nm in WEIGHT_NAMES])
```

```python
import functools
import math

import jax
import jax.numpy as jnp
import numpy as np
from jax import lax
from jax.experimental import pallas as pl
from jax.experimental.pallas import tpu as pltpu

F32 = jnp.float32
BF16 = jnp.bfloat16
MESH = pl.DeviceIdType.MESH

D_MODEL = 1024
N_META = 16
CHUNK = 64
PAD_ROWS = CHUNK - N_META
HEADS = 4
D_NOPE = 128
D_ROPE = 64
D_QK = D_NOPE + D_ROPE
D_QKP = 256
D_V = 128
KV_RANK = 256
Q_RANK = 384
MLA_W = HEADS * D_V
LRU_W = 512
LRU_TILE = 128
N_LRU_TILES = LRU_W // LRU_TILE
CONV_K = 4
C_RGLRU = 8.0
ROPE_THETA = 10000.0
D_FF = 2816
N_SHARD = 4
EPS = 1e-6
NEG_INF = -1e30
Z_KR = Q_RANK + KV_RANK
Z_MLA = Z_KR + 128
Z_U = Z_MLA
Z_G = Z_U + LRU_W
Z_W = Z_G + LRU_W
IN_WIDTH = Q_RANK + KV_RANK + D_ROPE + 2 * LRU_W

ADAM_LR = 0.001
ADAM_B1 = 0.9
ADAM_B2 = 0.999
ADAM_EPS = 1e-08
ADAM_WD = 0.01
ADAM_STEP = 10

VMEM_LIMIT_BYTES = 56 * 1024 * 1024
SMALL_ROWS = 104
SMALL_ADAM_ROWS = 80


def _params(sem):
    return pltpu.CompilerParams(dimension_semantics=sem, vmem_limit_bytes=VMEM_LIMIT_BYTES)


def _resident(shape):
    return pl.BlockSpec(tuple(shape), lambda i: (0,) * len(shape), pipeline_mode=pl.Buffered(1))


def _row_tile(rows, target):
    best = 16
    for t in range(16, min(rows, target) + 1, 16):
        if rows % t == 0:
            best = t
    return best


def _col_tile(cols, target):
    best = cols
    for t in range(128, min(cols, target) + 1, 128):
        if cols % t == 0:
            best = t
    return best


def _dot(a, b):
    return jnp.dot(a, b, preferred_element_type=F32)


def _dot_nt(a, b):
    return lax.dot_general(a, b, (((1,), (1,)), ((), ())), preferred_element_type=F32)


def _dot_tn(a, b):
    return lax.dot_general(a, b, (((0,), (0,)), ((), ())), preferred_element_type=F32)


def _rms(x, n):
    return lax.rsqrt(jnp.sum(x * x, axis=-1, keepdims=True) * (1.0 / n) + EPS)


def _rms_bwd(dn, nrm, r, n):
    return r * (dn - nrm * (jnp.sum(dn * nrm, axis=-1, keepdims=True) * (1.0 / n)))


def _gelu(x):
    k = math.sqrt(2.0 / math.pi)
    t = jnp.tanh(k * (x + 0.044715 * x * x * x))
    return 0.5 * x * (1.0 + t), t


def _gelu_grad(x, t):
    k = math.sqrt(2.0 / math.pi)
    return 0.5 * (1.0 + t) + 0.5 * x * (1.0 - t * t) * k * (1.0 + 3.0 * 0.044715 * x * x)


def _sigmoid(x):
    return 0.5 + 0.5 * jnp.tanh(0.5 * x)


def _softplus_neg(lam):
    e = jnp.exp(-jnp.abs(lam))
    log1p = jnp.where(e < 0.01, e * (1.0 - e * (0.5 - e * (1.0 / 3 - e * 0.25))), jnp.log(1.0 + e))
    return jnp.maximum(-lam, 0.0) + log1p


def _rope(t, c, s1, s2):
    return t * c + pltpu.roll(t, 96, 1) * s1 + pltpu.roll(t, 32, 1) * s2


def _rope_t(d, c, s1, s2):
    return d * c + pltpu.roll(d * s1, 32, 1) + pltpu.roll(d * s2, 96, 1)


def _rope_tables(lp):
    pos = (np.arange(lp, dtype=np.int32) - PAD_ROWS).astype(np.float32)
    inv_freq = (ROPE_THETA ** (-np.arange(0, D_ROPE // 2, dtype=np.float32) / (D_ROPE // 2))).astype(np.float32)
    ang = (pos[:, None] * inv_freq[None, :]).astype(np.float32).astype(np.float64)
    cos, sin = np.cos(ang).astype(np.float32), np.sin(ang).astype(np.float32)
    z = np.zeros_like(cos)
    return (jnp.asarray(np.concatenate([cos, cos, z, z], 1)), jnp.asarray(np.concatenate([-sin, z, z, z], 1)),
            jnp.asarray(np.concatenate([z, sin, z, z], 1)))


def _rmsnorm_call(name, h, g, tm, comm=None):
    rows, d = h.shape

    def body(h_ref, g_ref, o_ref):
        x = h_ref[...]
        o_ref[...] = (x * _rms(x, d) * g_ref[...]).astype(BF16)

    return _hosted_call(
        body, name=name, grid=(rows // tm,),
        in_specs=[pl.BlockSpec((tm, d), lambda i: (i, 0)), pl.BlockSpec((1, d), lambda i: (0, 0))],
        out_specs=[pl.BlockSpec((tm, d), lambda i: (i, 0))],
        out_shape=[jax.ShapeDtypeStruct((rows, d), BF16)],
        dims=("parallel",), args=(h, g), comm=comm)[0]


def _ffn_up_call(name, u, wg, wu, tm, comm=None):
    rows, d = u.shape
    ns, fs, _ = wg.shape

    def body(u_ref, wg_ref, wu_ref, g_ref, p_ref, a_ref):
        uu = u_ref[...]
        g = _dot_nt(uu, wg_ref[0])
        p = _dot_nt(uu, wu_ref[0])
        g_ref[0] = g.astype(BF16)
        p_ref[0] = p.astype(BF16)
        a_ref[0] = (g * jax.nn.sigmoid(g) * p).astype(BF16)

    wspec = pl.BlockSpec((1, fs, d), lambda s, i: (s, 0, 0))
    ospec = pl.BlockSpec((1, tm, fs), lambda s, i: (s, i, 0))
    oshape = jax.ShapeDtypeStruct((ns, rows, fs), BF16)
    return _hosted_call(
        body, name=name, grid=(ns, rows // tm),
        in_specs=[pl.BlockSpec((tm, d), lambda s, i: (i, 0)), wspec, wspec],
        out_specs=[ospec, ospec, ospec], out_shape=[oshape, oshape, oshape],
        dims=("parallel", "parallel"), args=(u, wg, wu), comm=comm)


def _ffn_gate_call(name, u, wg, tm, comm=None):
    rows, d = u.shape
    ns, fs, _ = wg.shape

    def body(u_ref, wg_ref, g_ref):
        g_ref[0] = _dot_nt(u_ref[...], wg_ref[0]).astype(BF16)

    return _hosted_call(
        body, name=name, grid=(ns, rows // tm),
        in_specs=[pl.BlockSpec((tm, d), lambda s, i: (i, 0)), pl.BlockSpec((1, fs, d), lambda s, i: (s, 0, 0))],
        out_specs=[pl.BlockSpec((1, tm, fs), lambda s, i: (s, i, 0))],
        out_shape=[jax.ShapeDtypeStruct((ns, rows, fs), BF16)],
        dims=("parallel", "parallel"), args=(u, wg), comm=comm)[0]


def _ffn_upact_call(name, u, wu, gate, tm, comm=None):
    rows, d = u.shape
    ns, fs, _ = wu.shape

    def body(u_ref, wu_ref, g_ref, p_ref, a_ref):
        p = _dot_nt(u_ref[...], wu_ref[0])
        g = g_ref[0].astype(F32)
        p_ref[0] = p.astype(BF16)
        a_ref[0] = (g * jax.nn.sigmoid(g) * p).astype(BF16)

    ospec = pl.BlockSpec((1, tm, fs), lambda s, i: (s, i, 0))
    oshape = jax.ShapeDtypeStruct((ns, rows, fs), BF16)
    return _hosted_call(
        body, name=name, grid=(ns, rows // tm),
        in_specs=[pl.BlockSpec((tm, d), lambda s, i: (i, 0)), pl.BlockSpec((1, fs, d), lambda s, i: (s, 0, 0)), ospec],
        out_specs=[ospec, ospec], out_shape=[oshape, oshape],
        dims=("parallel", "parallel"), args=(u, wu, gate), comm=comm)


def _loss_tail(x, g, t, row, d):
    r = _rms(x, d)
    n = x * r
    err = jnp.where(row >= CHUNK, n * g - t, 0.0)
    dout = err * (1.0 / d)
    dh = _rms_bwd(dout * g, n, r, d)
    dg = jnp.sum(dout * n, axis=0, keepdims=True)
    part = jnp.sum(jnp.sum(err * err, axis=1, keepdims=True), axis=0, keepdims=True) * (0.5 / d)
    return dh, dg, jnp.broadcast_to(part, (1, 128))


def _ffn_down_call(name, a, wd, h, tm, comm=None, next_gain=None):
    rows, d = h.shape
    ns, _, fs = a.shape
    more = next_gain is not None

    def body(a_ref, wd_ref, h_ref, *rest):
        acc = h_ref[...]
        for s in range(ns):
            acc = acc + 0.5 * _dot(a_ref[s], wd_ref[s])
        rest[-2 if more else -1][...] = acc
        if more:
            rest[-1][...] = (acc * _rms(acc, d) * rest[0][...]).astype(BF16)

    full = pl.BlockSpec((tm, d), lambda i: (i, 0))
    res = _hosted_call(
        body, name=name, grid=(rows // tm,),
        in_specs=[pl.BlockSpec((ns, tm, fs), lambda i: (0, i, 0)), _resident((ns, fs, d)), full]
        + ([pl.BlockSpec((1, d), lambda i: (0, 0))] if more else []),
        out_specs=[full] * (2 if more else 1),
        out_shape=[jax.ShapeDtypeStruct((rows, d), F32)] + ([jax.ShapeDtypeStruct((rows, d), BF16)] if more else []),
        dims=("parallel",), args=(a, wd, h) + ((next_gain,) if more else ()), comm=comm)
    return res if more else res[0]


def _ffn_down_loss_call(name, a, wd, h, g, target, lp, tm):
    rows, d = h.shape
    ns, _, fs = a.shape
    tpe = lp // tm

    def body(a_ref, wd_ref, h_ref, g_ref, t_ref, dh_ref, dhb_ref, dg_ref, loss_ref):
        i = pl.program_id(0)
        acc = h_ref[...]
        for s in range(ns):
            acc = acc + 0.5 * _dot(a_ref[s], wd_ref[s])
        row = (i % tpe) * tm + lax.broadcasted_iota(jnp.int32, (tm, 1), 0)
        dh, dg, loss = _loss_tail(acc, g_ref[...], t_ref[...], row, d)
        dh_ref[...] = dh
        dhb_ref[...] = dh.astype(BF16)

        @pl.when(i == 0)
        def _():
            dg_ref[...] = dg
            loss_ref[...] = loss

        @pl.when(i != 0)
        def _():
            dg_ref[...] += dg
            loss_ref[...] += loss

    full = pl.BlockSpec((tm, d), lambda i: (i, 0))
    gspec = pl.BlockSpec((1, d), lambda i: (0, 0))
    return _hosted_call(
        body, name=name, grid=(rows // tm,),
        in_specs=[pl.BlockSpec((ns, tm, fs), lambda i: (0, i, 0)), _resident((ns, fs, d)), full, gspec, full],
        out_specs=[full, full, gspec, pl.BlockSpec((1, 128), lambda i: (0, 0))],
        out_shape=[jax.ShapeDtypeStruct((rows, d), F32), jax.ShapeDtypeStruct((rows, d), BF16),
                   jax.ShapeDtypeStruct((1, d), F32), jax.ShapeDtypeStruct((1, 128), F32)],
        args=(a, wd, h, g, target))


def _mm_call(name, a, bt, tm, out_dtype):
    rows, k = a.shape
    n = bt.shape[0]

    def body(a_ref, b_ref, o_ref):
        o_ref[...] = _dot_nt(a_ref[...], b_ref[...]).astype(out_dtype)

    return pl.pallas_call(
        body, name=name, grid=(rows // tm,),
        in_specs=[pl.BlockSpec((tm, k), lambda i: (i, 0)), pl.BlockSpec((n, k), lambda i: (0, 0))],
        out_specs=pl.BlockSpec((tm, n), lambda i: (i, 0)),
        out_shape=jax.ShapeDtypeStruct((rows, n), out_dtype),
        compiler_params=_params(("parallel",)))(a, bt)


def _mla_heads(z, gql, gkvl, wuq, wuk, wuv):
    cq = z[:, 0:Q_RANK]
    ckv = z[:, Q_RANK:Z_KR]
    kr = z[:, Z_KR:Z_MLA]
    rq = _rms(cq, Q_RANK)
    nq = cq * rq
    cqn = (nq * gql).astype(BF16)
    rkv = _rms(ckv, KV_RANK)
    nkv = ckv * rkv
    ckvn = (nkv * gkvl).astype(BF16)
    qraw = _dot_nt(cqn, wuq)
    knope = _dot(ckvn, wuk)
    v = _dot(ckvn, wuv)
    skr = jnp.sum(kr * kr, axis=-1, keepdims=True)
    heads = []
    for hd in range(HEADS):
        qh = qraw[:, hd * D_QKP:(hd + 1) * D_QKP]
        rqh = lax.rsqrt(jnp.sum(qh * qh, axis=-1, keepdims=True) * (1.0 / D_QK) + EPS)
        kn = knope[:, hd * D_NOPE:(hd + 1) * D_NOPE]
        rkh = lax.rsqrt((jnp.sum(kn * kn, axis=-1, keepdims=True) + skr) * (1.0 / D_QK) + EPS)
        heads.append((qh * rqh, rqh, kn * rkh, kr * rkh, rkh))
    return dict(rq=rq, nq=nq, cqn=cqn, rkv=rkv, nkv=nkv, ckvn=ckvn, v=v, heads=heads)


def _mla_prep_call(z, gql, gkvl, gqh, gkh, wuq, wuk, wuv, tabs, lp, tm):
    rows = z.shape[0]
    tpe = lp // tm

    def body(z_ref, gql_ref, gkvl_ref, gqh_ref, gkh_ref, wuq_ref, wuk_ref, wuv_ref, c_ref, s1_ref, s2_ref,
             q_ref, k_ref, v_ref, cqn_ref, ckvn_ref):
        m = _mla_heads(z_ref[...], gql_ref[...], gkvl_ref[...], wuq_ref[...], wuk_ref[...], wuv_ref[...])
        c, s1, s2 = c_ref[...], s1_ref[...], s2_ref[...]
        gq, gk = gqh_ref[...], gkh_ref[...]
        row = (pl.program_id(0) % tpe) * tm + lax.broadcasted_iota(jnp.int32, (tm, 1), 0)
        spare = (lax.broadcasted_iota(jnp.int32, (1, D_QKP - D_NOPE), 1) == D_ROPE).astype(F32)
        kmask = jnp.where(row < PAD_ROWS, NEG_INF * math.sqrt(D_QK), 0.0) * spare
        for hd in range(HEADS):
            qn, _, knn, krn, _ = m["heads"][hd]
            qg = qn * gq
            q_ref[hd, :, 0:D_NOPE] = qg[:, 0:D_NOPE].astype(BF16)
            q_ref[hd, :, D_NOPE:D_QKP] = (_rope(qg[:, D_NOPE:D_QKP], c, s1, s2) + spare).astype(BF16)
            k_ref[hd, :, 0:D_NOPE] = (knn * gk[:, 0:D_NOPE]).astype(BF16)
            k_ref[hd, :, D_NOPE:D_QKP] = (_rope(krn * gk[:, D_NOPE:D_QKP], c, s1, s2) + kmask).astype(BF16)
            v_ref[hd] = m["v"][:, hd * D_V:(hd + 1) * D_V].astype(BF16)
        cqn_ref[...] = m["cqn"]
        ckvn_ref[...] = m["ckvn"]

    def const(shape):
        return pl.BlockSpec(shape, lambda i: tuple(0 for _ in shape))

    tab = pl.BlockSpec((tm, 128), lambda i: (i % tpe, 0))
    return pl.pallas_call(
        body, name="mla_prep", grid=(rows // tm,),
        in_specs=[pl.BlockSpec((tm, Z_MLA), lambda i: (i, 0)), const((1, Q_RANK)), const((1, KV_RANK)),
                  const((1, D_QKP)), const((1, D_QKP)), const((HEADS * D_QKP, Q_RANK)),
                  const((KV_RANK, HEADS * D_NOPE)), const((KV_RANK, HEADS * D_V)), tab, tab, tab],
        out_specs=[pl.BlockSpec((HEADS, tm, D_QKP), lambda i: (0, i, 0)),
                   pl.BlockSpec((HEADS, tm, D_QKP), lambda i: (0, i, 0)),
                   pl.BlockSpec((HEADS, tm, D_V), lambda i: (0, i, 0)),
                   pl.BlockSpec((tm, Q_RANK), lambda i: (i, 0)),
                   pl.BlockSpec((tm, KV_RANK), lambda i: (i, 0))],
        out_shape=[jax.ShapeDtypeStruct((HEADS, rows, D_QKP), BF16),
                   jax.ShapeDtypeStruct((HEADS, rows, D_QKP), BF16),
                   jax.ShapeDtypeStruct((HEADS, rows, D_V), BF16),
                   jax.ShapeDtypeStruct((rows, Q_RANK), BF16),
                   jax.ShapeDtypeStruct((rows, KV_RANK), BF16)],
        compiler_params=_params(("parallel",)))(z, gql, gkvl, gqh, gkh, wuq, wuk, wuv, *tabs)


Q_BLOCK_ROWS = 528
Q_BLOCK_ROWS_BWD = 192


def _q_block(lp):
    return _row_tile(lp, Q_BLOCK_ROWS)


def _key_end(ext, lp):
    return min(lp, -(-ext // CHUNK) * CHUNK)


def _diag_bias(qb, j0, nk):
    shift = CHUNK.bit_length() - 1
    r = jnp.right_shift(j0 + lax.broadcasted_iota(jnp.int32, (qb, nk), 0), shift)
    c = jnp.right_shift(j0 + lax.broadcasted_iota(jnp.int32, (qb, nk), 1), shift)
    return jnp.where(c <= r, 0.0, NEG_INF)


def _attn_fwd_call(q, k, v, nb, lp, comm=None):
    rows = nb * lp
    qb = _q_block(lp)
    scale = 1.0 / math.sqrt(D_QK)

    def body(q_ref, k_ref, v_ref, o_ref, lse_ref):
        for j in range(lp // qb):
            j0, ext = j * qb, (j + 1) * qb
            kend = _key_end(ext, lp)
            qj = q_ref[0, j0:ext, :]
            sd = _dot_nt(qj, k_ref[0, j0:kend, :]) * scale + _diag_bias(qb, j0, kend - j0)
            mx = jnp.max(sd, axis=-1, keepdims=True)
            if j > 0:
                so = _dot_nt(qj, k_ref[0, 0:j0, :]) * scale
                mx = jnp.maximum(mx, jnp.max(so, axis=-1, keepdims=True))
            pd = jnp.exp(sd - mx)
            l = jnp.sum(pd, axis=-1, keepdims=True)
            o = _dot(pd.astype(BF16), v_ref[0, j0:kend, :])
            if j > 0:
                po = jnp.exp(so - mx)
                l = l + jnp.sum(po, axis=-1, keepdims=True)
                o = o + _dot(po.astype(BF16), v_ref[0, 0:j0, :])
            o_ref[j0:ext, :] = o / l
            lse_ref[0, j0:ext, :] = mx + jnp.log(l)

    return _hosted_call(
        body, name="attn_fwd", grid=(nb, HEADS),
        in_specs=[pl.BlockSpec((1, lp, D_QKP), lambda b, h: (h, b, 0)),
                  pl.BlockSpec((1, lp, D_QKP), lambda b, h: (h, b, 0)),
                  pl.BlockSpec((1, lp, D_V), lambda b, h: (h, b, 0))],
        out_specs=[pl.BlockSpec((lp, D_V), lambda b, h: (b, h)),
                   pl.BlockSpec((1, lp, 1), lambda b, h: (h, b, 0))],
        out_shape=[jax.ShapeDtypeStruct((rows, MLA_W), F32),
                   jax.ShapeDtypeStruct((HEADS, rows, 1), F32)],
        dims=("parallel", "parallel"), args=(q, k, v), comm=comm)


def _attn_bwd_call(q, k, v, o, lse, do, nb, lp, comm=None):
    rows = nb * lp
    qb = _row_tile(lp, Q_BLOCK_ROWS_BWD)
    scale = 1.0 / math.sqrt(D_QK)

    def body(q_ref, k_ref, v_ref, o_ref, lse_ref, do_ref, dq_ref, dk_ref, dv_ref, dk_acc, dv_acc):
        dk_acc[...] = jnp.zeros_like(dk_acc)
        dv_acc[...] = jnp.zeros_like(dv_acc)
        shift = CHUNK.bit_length() - 1
        for j in range(lp // qb):
            j0, ext = j * qb, (j + 1) * qb
            kend = _key_end(ext, lp)
            qj = q_ref[0, j0:ext, :]
            doj = do_ref[j0:ext, :]
            delta = jnp.sum(doj * o_ref[j0:ext, :], axis=-1, keepdims=True)
            dob = doj.astype(BF16)
            kk = k_ref[0, 0:kend, :]
            qchunk = jnp.right_shift(j0 + lax.broadcasted_iota(jnp.int32, (qb, kend), 0), shift)
            kchunk = jnp.right_shift(lax.broadcasted_iota(jnp.int32, (qb, kend), 1), shift)
            s = _dot_nt(qj, kk) * scale - lse_ref[0, j0:ext, :]
            p = jnp.where(kchunk <= qchunk, jnp.exp(s), 0.0)
            dv_acc[0:kend, :] += _dot_tn(p.astype(BF16), dob)
            dp = _dot_nt(dob, v_ref[0, 0:kend, :])
            ds = (p * (dp - delta) * scale).astype(BF16)
            dq_ref[0, j0:ext, :] = _dot(ds, kk).astype(BF16)
            dk_acc[0:kend, :] += _dot_tn(ds, qj)
        dk_ref[0] = dk_acc[...].astype(BF16)
        dv_ref[0] = dv_acc[...].astype(BF16)

    qspec = pl.BlockSpec((1, lp, D_QKP), lambda b, h: (h, b, 0))
    vspec = pl.BlockSpec((1, lp, D_V), lambda b, h: (h, b, 0))
    ospec = pl.BlockSpec((lp, D_V), lambda b, h: (b, h))
    return _hosted_call(
        body, name="attn_bwd", grid=(nb, HEADS),
        in_specs=[qspec, qspec, vspec, ospec, pl.BlockSpec((1, lp, 1), lambda b, h: (h, b, 0)), ospec],
        out_specs=[qspec, qspec, vspec],
        out_shape=[jax.ShapeDtypeStruct((HEADS, rows, D_QKP), BF16),
                   jax.ShapeDtypeStruct((HEADS, rows, D_QKP), BF16),
                   jax.ShapeDtypeStruct((HEADS, rows, D_V), BF16)],
        scratch_shapes=[pltpu.VMEM((lp, D_QKP), F32), pltpu.VMEM((lp, D_V), F32)],
        dims=("parallel", "parallel"), args=(q, k, v, o, lse, do), comm=comm)


def _lru_gates(u, cw, cb, wa, ba, wx, bx, lam, lp):
    xc = (cw[3:4, :] * u + cw[2:3, :] * pltpu.roll(u, 1, 0) + cw[1:2, :] * pltpu.roll(u, 2, 0)
          + cw[0:1, :] * pltpu.roll(u, 3, 0) + cb)
    xcb = xc.astype(BF16)
    r = _sigmoid(_dot(xcb, wa) + ba)
    i = _sigmoid(_dot(xcb, wx) + bx)
    sp = _softplus_neg(lam)
    la = -C_RGLRU * r * sp
    a = jnp.exp(la)
    x2 = 2.0 * la
    e2 = a * a
    m2 = jnp.maximum(jnp.where(x2 > -0.01, -x2 * (1.0 + 0.5 * x2), 1.0 - e2), 1e-30)
    rs = lax.rsqrt(m2)
    row = lax.broadcasted_iota(jnp.int32, (lp, LRU_TILE), 0)
    first = row == PAD_ROWS
    valid = row >= PAD_ROWS
    mult_eff = jnp.where(first, 1.0, m2 * rs)
    return dict(xc=xc, xcb=xcb, r=r, i=i, sp=sp, a=a, e2=e2, rs=rs, mult_eff=mult_eff, first=first, valid=valid)


def _scan_rows(a, b, a_s, b_s, out_ref, lp, reverse):
    sub = lax.broadcasted_iota(jnp.int32, (lp, LRU_TILE), 0) & 7
    for dist in (1, 2, 4):
        shift = lp - dist if reverse else dist
        keep = (sub + dist <= 7) if reverse else (sub >= dist)
        a_sh = pltpu.roll(a, shift, 0)
        b_sh = pltpu.roll(b, shift, 0)
        b = jnp.where(keep, a * b_sh + b, b)
        a = jnp.where(keep, a * a_sh, a)
    a_s[...] = a
    b_s[...] = b
    n_groups = lp // 8
    edge = 0 if reverse else 7

    def group(gi, carry):
        r0 = pl.multiple_of(((n_groups - 1 - gi) if reverse else gi) * 8, 8)
        a8 = a_s[pl.ds(r0, 8), :]
        b8 = b_s[pl.ds(r0, 8), :]
        out_ref[pl.ds(r0, 8), :] = a8 * carry + b8
        return a8[edge:edge + 1, :] * carry + b8[edge:edge + 1, :]

    lax.fori_loop(0, n_groups, group, jnp.zeros((1, LRU_TILE), F32), unroll=4)


def _lru_specs(lp):
    seq = lambda col0, stride=1: pl.BlockSpec((lp, LRU_TILE), lambda t, b: (b, col0 + stride * t))
    cw = pl.BlockSpec((1, CONV_K, LRU_TILE), lambda t, b: (t, 0, 0))
    vec = pl.BlockSpec((1, LRU_TILE), lambda t, b: (0, t))
    mat = pl.BlockSpec((1, LRU_TILE, LRU_TILE), lambda t, b: (t, 0, 0))
    return seq, cw, vec, mat


def _lru_fwd_call(z, cw, cb, wa, ba, wx, bx, lam, nb, lp, comm=None):
    rows = nb * lp
    seq, cwspec, vec, mat = _lru_specs(lp)

    def body(u_ref, g_ref, cw_ref, cb_ref, wa_ref, ba_ref, wx_ref, bx_ref, lam_ref, y_ref, hs_ref, a_s, b_s):
        m = _lru_gates(u_ref[...], cw_ref[0], cb_ref[...], wa_ref[0], ba_ref[...], wx_ref[0], bx_ref[...],
                       lam_ref[...], lp)
        a = jnp.where(m["valid"], m["a"], 0.0)
        b = jnp.where(m["valid"], m["mult_eff"] * (m["i"] * m["xc"]), 0.0)
        _scan_rows(a, b, a_s, b_s, hs_ref, lp, reverse=False)
        gl, _ = _gelu(g_ref[...])
        y_ref[...] = hs_ref[...] * gl

    oshape = jax.ShapeDtypeStruct((rows, LRU_W), F32)
    return _hosted_call(
        body, name="lru_fwd", grid=(N_LRU_TILES, nb),
        in_specs=[seq(Z_U // LRU_TILE, 2), seq(Z_U // LRU_TILE + 1, 2), cwspec, vec, mat, vec, mat, vec, vec],
        out_specs=[seq(0), seq(0)], out_shape=[oshape, oshape],
        scratch_shapes=[pltpu.VMEM((lp, LRU_TILE), F32), pltpu.VMEM((lp, LRU_TILE), F32)],
        dims=("parallel", "parallel"), args=(z, z, cw, cb, wa, ba, wx, bx, lam), comm=comm)


def _lru_bwd_call(z, dz, hs, dy, cw, cb, wa, ba, wx, bx, lam, nb, lp, comm=None):
    rows = nb * lp
    seq, cwspec, vec, mat = _lru_specs(lp)

    def body(u_ref, g_ref, hs_ref, dy_ref, cw_ref, cb_ref, wa_ref, ba_ref, wx_ref, bx_ref, lam_ref, dz_in,
             dz_ref, dcw_ref, dcb_ref, dwa_ref, dba_ref, dwx_ref, dbx_ref, dlam_ref, a_s, b_s, d_s):
        du_ref = dz_ref.at[:, 0:LRU_TILE]
        dg_ref = dz_ref.at[:, LRU_TILE:2 * LRU_TILE]
        b_idx = pl.program_id(1)
        u = u_ref[...]
        cw = cw_ref[0]
        wa, wx = wa_ref[0], wx_ref[0]
        lam = lam_ref[...]
        m = _lru_gates(u, cw, cb_ref[...], wa, ba_ref[...], wx, bx_ref[...], lam, lp)
        gate = g_ref[...]
        gl, th = _gelu(gate)
        dy = dy_ref[...]
        hs = hs_ref[...]
        dg_ref[...] = (dy * hs * _gelu_grad(gate, th)).astype(BF16)
        a_eff = jnp.where(m["valid"], m["a"], 0.0)
        _scan_rows(pltpu.roll(a_eff, lp - 1, 0), dy * gl, a_s, b_s, d_s, lp, reverse=True)
        ds = d_s[...]
        xc, r, i = m["xc"], m["r"], m["i"]
        row = lax.broadcasted_iota(jnp.int32, (lp, LRU_TILE), 0)
        da = ds * jnp.where(row >= 1, pltpu.roll(hs, 1, 0), 0.0)
        db = jnp.where(m["valid"], ds, 0.0)
        di = db * m["mult_eff"] * xc
        dxc = db * m["mult_eff"] * i
        live = m["valid"] & jnp.logical_not(m["first"])
        dm = jnp.where(live, db * i * xc, 0.0)
        dla = da * m["a"] - dm * (m["e2"] * m["rs"])
        dr = dla * (-C_RGLRU * m["sp"])
        dsp = jnp.sum(dla * (-C_RGLRU * r), axis=0, keepdims=True)
        dpr = (dr * r * (1.0 - r))
        dpi = (di * i * (1.0 - i))
        dprb, dpib = dpr.astype(BF16), dpi.astype(BF16)
        dxc = dxc + _dot_nt(dprb, wa) + _dot_nt(dpib, wx)
        du = (cw[3:4, :] * dxc + cw[2:3, :] * pltpu.roll(dxc, lp - 1, 0) + cw[1:2, :] * pltpu.roll(dxc, lp - 2, 0)
              + cw[0:1, :] * pltpu.roll(dxc, lp - 3, 0))
        du_ref[...] = jnp.where(m["valid"], du, 0.0).astype(BF16)
        tap = lax.broadcasted_iota(jnp.int32, (CONV_K, LRU_TILE), 0)
        dcw = jnp.zeros((CONV_K, LRU_TILE), F32)
        for kk in range(CONV_K):
            shifted = u if kk == CONV_K - 1 else pltpu.roll(u, CONV_K - 1 - kk, 0)
            dcw = jnp.where(tap == kk, jnp.sum(dxc * shifted, axis=0, keepdims=True), dcw)
        parts = [(dcw_ref, dcw[None]), (dcb_ref, jnp.sum(dxc, axis=0, keepdims=True)[None]),
                 (dwa_ref, _dot_tn(m["xcb"], dprb)[None]), (dba_ref, jnp.sum(dpr, axis=0, keepdims=True)[None]),
                 (dwx_ref, _dot_tn(m["xcb"], dpib)[None]), (dbx_ref, jnp.sum(dpi, axis=0, keepdims=True)[None]),
                 (dlam_ref, (dsp * (-jax.nn.sigmoid(-lam)))[None])]

        @pl.when(b_idx == 0)
        def _():
            for ref, val in parts:
                ref[...] = val

        @pl.when(b_idx != 0)
        def _():
            for ref, val in parts:
                ref[...] += val

    vec3 = pl.BlockSpec((1, 1, LRU_TILE), lambda t, b: (t, 0, 0))
    vshape = jax.ShapeDtypeStruct((N_LRU_TILES, 1, LRU_TILE), F32)
    mshape = jax.ShapeDtypeStruct((N_LRU_TILES, LRU_TILE, LRU_TILE), F32)
    pair = pl.BlockSpec((lp, 2 * LRU_TILE), lambda t, b: (b, Z_U // (2 * LRU_TILE) + t))
    return _hosted_call(
        body, name="lru_bwd", grid=(N_LRU_TILES, nb),
        in_specs=[seq(Z_U // LRU_TILE, 2), seq(Z_U // LRU_TILE + 1, 2), seq(0), seq(0), cwspec, vec, mat, vec, mat,
                  vec, vec, pl.BlockSpec(memory_space=pl.ANY)],
        out_specs=[pair, cwspec, vec3, mat, vec3, mat, vec3, vec3],
        out_shape=[jax.ShapeDtypeStruct(dz.shape, dz.dtype), jax.ShapeDtypeStruct((N_LRU_TILES, CONV_K, LRU_TILE), F32),
                   vshape, mshape, vshape, mshape, vshape, vshape],
        scratch_shapes=[pltpu.VMEM((lp, LRU_TILE), F32)] * 3,
        dims=("parallel", "arbitrary"), args=(z, z, hs, dy, cw, cb, wa, ba, wx, bx, lam, dz), comm=comm,
        aliases={11: 0})


def _mix_out_call(ya, yl, ga, gl, wout, h, next_gain, tm):
    rows, d = h.shape

    def body(ya_ref, yl_ref, ga_ref, gl_ref, w_ref, h_ref, ng_ref, y_ref, o_ref, u_ref):
        a = ya_ref[...]
        l = yl_ref[...]
        an = (a * _rms(a, MLA_W) * ga_ref[...]).astype(BF16)
        ln = (l * _rms(l, LRU_W) * gl_ref[...]).astype(BF16)
        y_ref[:, 0:MLA_W] = an
        y_ref[:, MLA_W:MLA_W + LRU_W] = ln
        out = h_ref[...] + _dot(an, w_ref[0:MLA_W, :]) + _dot(ln, w_ref[MLA_W:MLA_W + LRU_W, :])
        o_ref[...] = out
        u_ref[...] = (out * _rms(out, d) * ng_ref[...]).astype(BF16)

    half = pl.BlockSpec((tm, MLA_W), lambda i: (i, 0))
    g = pl.BlockSpec((1, MLA_W), lambda i: (0, 0))
    full = pl.BlockSpec((tm, d), lambda i: (i, 0))
    return pl.pallas_call(
        body, name="mix_out", grid=(rows // tm,),
        in_specs=[half, half, g, g, pl.BlockSpec((MLA_W + LRU_W, d), lambda i: (0, 0)), full,
                  pl.BlockSpec((1, d), lambda i: (0, 0))],
        out_specs=[full, full, full],
        out_shape=[jax.ShapeDtypeStruct((rows, MLA_W + LRU_W), BF16), jax.ShapeDtypeStruct((rows, d), F32),
                   jax.ShapeDtypeStruct((rows, d), BF16)],
        compiler_params=_params(("parallel",)))(ya, yl, ga, gl, wout, h, next_gain)


def _ffn_dact_call(name, dhb, wd, gate, up, tm, comm=None):
    rows, d = dhb.shape
    ns, fs, _ = wd.shape

    nsub = 2 if tm % 32 == 0 else 1
    sub = tm // nsub

    def body(dh_ref, wd_ref, g_ref, p_ref, dg_ref, dp_ref):
        wd = wd_ref[0]
        for r in range(nsub):
            rs = slice(r * sub, (r + 1) * sub)
            da = (0.5 * _dot_nt(dh_ref[rs, :], wd)).astype(BF16)
            g = g_ref[0, rs, :]
            p = p_ref[0, rs, :]
            sg = jax.nn.sigmoid(g)
            dg_ref[0, rs, :] = (da * p) * (sg * (1.0 + g * (1.0 - sg)))
            dp_ref[0, rs, :] = da * (g * sg)

    aspec = pl.BlockSpec((1, tm, fs), lambda s, i: (s, i, 0))
    oshape = jax.ShapeDtypeStruct((ns, rows, fs), BF16)
    return _hosted_call(
        body, name=name, grid=(ns, rows // tm),
        in_specs=[pl.BlockSpec((tm, d), lambda s, i: (i, 0)), pl.BlockSpec((1, fs, d), lambda s, i: (s, 0, 0)),
                  aspec, aspec],
        out_specs=[aspec, aspec], out_shape=[oshape, oshape],
        dims=("parallel", "parallel"), args=(dhb, wd, gate, up), comm=comm)


def _norm_in_bwd_call(name, pieces, h, g, dres, tm, comm=None, part=(0, 1), prev=None, mix=None):
    rows, d = h.shape
    npc = len(pieces)
    steps = rows // tm // part[1]
    off = part[0] * steps
    n_prev = 0 if prev is None else 2
    n_mix = 0 if mix is None else 5

    def body(*refs):
        d_refs = refs[0:2 * npc:2]
        w_refs = refs[1:2 * npc:2]
        h_ref, g_ref, dres_ref = refs[2 * npc:2 * npc + 3]
        mix_in_refs = refs[2 * npc + 3:2 * npc + 3 + n_mix]
        dh_ref, dhb_ref, dg_ref = refs[2 * npc + 3 + n_mix + n_prev:2 * npc + 6 + n_mix + n_prev]
        mix_out_refs = refs[2 * npc + 6 + n_mix + n_prev:]
        du = jnp.zeros((tm, d), F32)
        for d_ref, w_ref in zip(d_refs, w_refs):
            if len(d_ref.shape) == 3:
                for s in range(d_ref.shape[0]):
                    du = du + _dot(d_ref[s], w_ref[s])
            else:
                du = du + _dot(d_ref[...], w_ref[...])
        x = h_ref[...]
        r = _rms(x, d)
        n = x * r
        dh = dres_ref[...] + _rms_bwd(du * g_ref[...], n, r, d)
        dhb = dh.astype(BF16)
        dh_ref[...] = dh
        dhb_ref[...] = dhb
        sums = [(dg_ref, jnp.sum(du * n, axis=0, keepdims=True))]
        if mix is not None:
            wo_ref, ya_ref, yl_ref, ga_ref, gl_ref = mix_in_refs
            dya_ref, dyl_ref, dga_ref, dgl_ref = mix_out_refs
            dy = _dot_nt(dhb, wo_ref[...])
            for val, gain_ref, lo, out_ref, acc_ref in ((ya_ref[...], ga_ref, 0, dya_ref, dga_ref),
                                                        (yl_ref[...], gl_ref, MLA_W, dyl_ref, dgl_ref)):
                rb = _rms(val, MLA_W)
                nb_ = val * rb
                dyn = dy[:, lo:lo + MLA_W]
                out_ref[...] = _rms_bwd(dyn * gain_ref[...], nb_, rb, MLA_W)
                sums.append((acc_ref, jnp.sum(dyn * nb_, axis=0, keepdims=True)))

        @pl.when(pl.program_id(0) == 0)
        def _():
            for ref, val in sums:
                ref[...] = val

        @pl.when(pl.program_id(0) != 0)
        def _():
            for ref, val in sums:
                ref[...] += val

    in_specs, args = [], []
    for dd, w in pieces:
        if dd.ndim == 3:
            in_specs.append(pl.BlockSpec((dd.shape[0], tm, dd.shape[2]), lambda i: (0, i + off, 0)))
            in_specs.append(_resident(w.shape))
        else:
            in_specs.append(pl.BlockSpec((tm, dd.shape[1]), lambda i: (i + off, 0)))
            in_specs.append(_resident(w.shape))
        args += [dd, w]
    full = pl.BlockSpec((tm, d), lambda i: (i + off, 0))
    gspec = pl.BlockSpec((1, d), lambda i: (0, 0))
    half = pl.BlockSpec((tm, MLA_W), lambda i: (i + off, 0))
    hgain = pl.BlockSpec((1, MLA_W), lambda i: (0, 0))
    mix_in_specs = [] if mix is None else [_resident(mix[0].shape), half, half, hgain, hgain]
    mix_out_specs = [] if mix is None else [half, half, hgain, hgain]
    mix_out_shapes = [] if mix is None else [
        jax.ShapeDtypeStruct((rows, MLA_W), F32), jax.ShapeDtypeStruct((rows, LRU_W), F32),
        jax.ShapeDtypeStruct((1, MLA_W), F32), jax.ShapeDtypeStruct((1, LRU_W), F32)]
    n_in = len(in_specs) + 3 + n_mix
    return _hosted_call(
        body, name=name, grid=(steps,),
        in_specs=in_specs + [full, gspec, full] + mix_in_specs + _any_specs(n_prev),
        out_specs=[full, full, gspec] + mix_out_specs,
        out_shape=[jax.ShapeDtypeStruct((rows, d), F32), jax.ShapeDtypeStruct((rows, d), BF16),
                   jax.ShapeDtypeStruct((1, d), F32)] + mix_out_shapes,
        args=(*args, h, g, dres, *(mix or ()), *(prev or ())), comm=comm,
        aliases={n_in: 0, n_in + 1: 1} if prev is not None else {})


def _wgrad_call(name, a, b, scale=1.0, comm=None):
    a3, b3 = a.ndim == 3, b.ndim == 3
    ns = a.shape[0] if a3 else (b.shape[0] if b3 else 1)
    rows, m = a.shape[-2:]
    n = b.shape[-1]
    tmm = m if a3 else _col_tile(m, 256)

    def body(a_ref, b_ref, o_ref):
        av = a_ref[0] if a3 else a_ref[...]
        bv = b_ref[0] if b3 else b_ref[...]
        res = _dot_tn(av, bv)
        if scale != 1.0:
            res = res * scale
        if a3 or b3:
            o_ref[0] = res
        else:
            o_ref[...] = res

    aspec = (pl.BlockSpec((1, rows, tmm), lambda s, j: (s, 0, j)) if a3
             else pl.BlockSpec((rows, tmm), lambda s, j: (0, j)))
    bspec = (pl.BlockSpec((1, rows, n), lambda s, j: (s, 0, 0)) if b3
             else pl.BlockSpec((rows, n), lambda s, j: (0, 0)))
    if a3 or b3:
        ospec = pl.BlockSpec((1, tmm, n), lambda s, j: (s, j, 0))
        oshape = jax.ShapeDtypeStruct((ns, m, n), F32)
    else:
        ospec = pl.BlockSpec((tmm, n), lambda s, j: (j, 0))
        oshape = jax.ShapeDtypeStruct((m, n), F32)
    return _hosted_call(
        body, name=name, grid=(ns, m // tmm), in_specs=[aspec, bspec], out_specs=[ospec], out_shape=[oshape],
        dims=("parallel", "parallel"), args=(a, b), comm=comm)[0]


def _mla_prep_bwd_call(z, dq, dk, dv, gql, gkvl, gqh, gkh, wuq, wuk, wuv, tabs, lp, tm, comm=None):
    rows = z.shape[0]
    tpe = lp // tm

    def body(z_ref, dq_ref, dk_ref, dv_ref, gql_ref, gkvl_ref, gqh_ref, gkh_ref, wuq_ref, wuk_ref, wuv_ref,
             c_ref, s1_ref, s2_ref, dz_ref, dqp_ref, dkn_ref, dvv_ref, dgql_ref, dgkvl_ref, dgqh_ref, dgkh_ref):
        gql, gkvl = gql_ref[...], gkvl_ref[...]
        gq, gk = gqh_ref[...], gkh_ref[...]
        wuq, wuk, wuv = wuq_ref[...], wuk_ref[...], wuv_ref[...]
        m = _mla_heads(z_ref[...], gql, gkvl, wuq, wuk, wuv)
        c, s1, s2 = c_ref[...], s1_ref[...], s2_ref[...]
        dgq = jnp.zeros((1, D_QKP), F32)
        dgk = jnp.zeros((1, D_QKP), F32)
        dkr = jnp.zeros((tm, D_QKP - D_NOPE), F32)
        for hd in range(HEADS):
            qn, rqh, knn, krn, rkh = m["heads"][hd]
            dqg = jnp.concatenate([dq_ref[hd, :, 0:D_NOPE].astype(F32),
                                   _rope_t(dq_ref[hd, :, D_NOPE:D_QKP].astype(F32), c, s1, s2)], axis=1)
            dgq = dgq + jnp.sum(dqg * qn, axis=0, keepdims=True)
            dqn = dqg * gq
            dqr = rqh * (dqn - qn * (jnp.sum(dqn * qn, axis=-1, keepdims=True) * (1.0 / D_QK)))
            dqp_ref[:, hd * D_QKP:(hd + 1) * D_QKP] = dqr.astype(BF16)
            kn_full = jnp.concatenate([knn, krn], axis=1)
            dkg = jnp.concatenate([dk_ref[hd, :, 0:D_NOPE].astype(F32),
                                   _rope_t(dk_ref[hd, :, D_NOPE:D_QKP].astype(F32), c, s1, s2)], axis=1)
            dgk = dgk + jnp.sum(dkg * kn_full, axis=0, keepdims=True)
            dkn = dkg * gk
            dkraw = rkh * (dkn - kn_full * (jnp.sum(dkn * kn_full, axis=-1, keepdims=True) * (1.0 / D_QK)))
            dkn_ref[:, hd * D_NOPE:(hd + 1) * D_NOPE] = dkraw[:, 0:D_NOPE].astype(BF16)
            dkr = dkr + dkraw[:, D_NOPE:D_QKP]
            dvv_ref[:, hd * D_V:(hd + 1) * D_V] = dv_ref[hd]
        dcqn = _dot(dqp_ref[...], wuq)
        dckvn = _dot_nt(dkn_ref[...], wuk) + _dot_nt(dvv_ref[...], wuv)
        dz_ref[:, 0:Q_RANK] = _rms_bwd(dcqn * gql, m["nq"], m["rq"], Q_RANK).astype(BF16)
        dz_ref[:, Q_RANK:Z_KR] = _rms_bwd(dckvn * gkvl, m["nkv"], m["rkv"], KV_RANK).astype(BF16)
        dz_ref[:, Z_KR:Z_MLA] = dkr.astype(BF16)
        parts = [(dgql_ref, jnp.sum(dcqn * m["nq"], axis=0, keepdims=True)),
                 (dgkvl_ref, jnp.sum(dckvn * m["nkv"], axis=0, keepdims=True)), (dgqh_ref, dgq), (dgkh_ref, dgk)]

        @pl.when(pl.program_id(0) == 0)
        def _():
            for ref, val in parts:
                ref[...] = val

        @pl.when(pl.program_id(0) != 0)
        def _():
            for ref, val in parts:
                ref[...] += val

    def const(shape):
        return pl.BlockSpec(shape, lambda i: tuple(0 for _ in shape))

    tab = pl.BlockSpec((tm, 128), lambda i: (i % tpe, 0))
    hq = pl.BlockSpec((HEADS, tm, D_QKP), lambda i: (0, i, 0))
    hv = pl.BlockSpec((HEADS, tm, D_V), lambda i: (0, i, 0))

    def rowspec(n):
        return pl.BlockSpec((tm, n), lambda i: (i, 0))

    return _hosted_call(
        body, name="mla_prep_bwd", grid=(rows // tm,),
        in_specs=[rowspec(Z_MLA), hq, hq, hv, const((1, Q_RANK)), const((1, KV_RANK)), const((1, D_QKP)),
                  const((1, D_QKP)), const((HEADS * D_QKP, Q_RANK)), const((KV_RANK, HEADS * D_NOPE)),
                  const((KV_RANK, HEADS * D_V)), tab, tab, tab],
        out_specs=[rowspec(Z_MLA), rowspec(HEADS * D_QKP), rowspec(HEADS * D_NOPE), rowspec(HEADS * D_V),
                   const((1, Q_RANK)), const((1, KV_RANK)), const((1, D_QKP)), const((1, D_QKP))],
        out_shape=[jax.ShapeDtypeStruct((rows, Z_W), BF16), jax.ShapeDtypeStruct((rows, HEADS * D_QKP), BF16),
                   jax.ShapeDtypeStruct((rows, HEADS * D_NOPE), BF16), jax.ShapeDtypeStruct((rows, HEADS * D_V), BF16),
                   jax.ShapeDtypeStruct((1, Q_RANK), F32), jax.ShapeDtypeStruct((1, KV_RANK), F32),
                   jax.ShapeDtypeStruct((1, D_QKP), F32), jax.ShapeDtypeStruct((1, D_QKP), F32)],
        args=(z, dq, dk, dv, gql, gkvl, gqh, gkh, wuq, wuk, wuv, *tabs), comm=comm)


def _local_step(h0, target, w, nb, lp, sched=None):
    tm = _row_tile(nb * lp, 1408)
    te = _row_tile(lp, 512)
    tabs = _rope_tables(lp)
    g = {}
    if sched is None:
        host = lambda stage: None
    else:
        sched.g = g
        host = sched.host

    def ffn_act(tag, u, split):
        if split:
            gate = _ffn_gate_call(tag + "_gate", u, w[tag + "_w_gate"], tm, host(tag + "_gate"))
            up, act = _ffn_upact_call(tag + "_upact", u, w[tag + "_w_up"], gate, tm, host(tag + "_upact"))
        else:
            gate, up, act = _ffn_up_call(tag + "_up", u, w[tag + "_w_gate"], w[tag + "_w_up"], tm, host(tag + "_up"))
        return u, gate, up, act

    def ffn_bwd(tag, h, saved, dh, dhb, split, mix=None):
        u, gate, up, act = saved
        dgate, dup = _ffn_dact_call(tag + "_dact", dhb, w[tag + "_w_down"], gate, up, tm, host(tag + "_dact"))
        g[tag + "_w_down"] = _wgrad_call(tag + "_dwd", act, dhb, 0.5, host(tag + "_dwd"))
        g[tag + "_w_gate"] = _wgrad_call(tag + "_dwg", dgate, u, 1.0, host(tag + "_dwg"))
        g[tag + "_w_up"] = _wgrad_call(tag + "_dwu", dup, u, 1.0, host(tag + "_dwu"))
        pieces = [(dgate, w[tag + "_w_gate"]), (dup, w[tag + "_w_up"])]
        if not split:
            res = _norm_in_bwd_call(tag + "_din", pieces, h, w[tag + "_norm"], dh, te, host(tag + "_din"), mix=mix)
            g[tag + "_norm"] = res[2]
            return (res[0], res[1], *res[3:])
        dh_a, dhb_a, dg_a = _norm_in_bwd_call(tag + "_din_a", pieces, h, w[tag + "_norm"], dh, te,
                                              host(tag + "_din_a"), part=(0, 2))
        dh_in, dhb_in, dg_b = _norm_in_bwd_call(tag + "_din_b", pieces, h, w[tag + "_norm"], dh, te,
                                                host(tag + "_din_b"), part=(1, 2), prev=(dh_a, dhb_a))
        g[tag + "_norm"] = dg_a + dg_b
        return dh_in, dhb_in

    s1 = ffn_act("ffn1", _rmsnorm_call("ffn1_norm", h0, w["ffn1_norm"], te, host("ffn1_norm")), True)
    h1, un = _ffn_down_call("ffn1_down", s1[3], w["ffn1_w_down"], h0, te, host("ffn1_down"), w["mix_norm"])
    z = _mm_call("mix_in", un, w["w_in"], tm, F32)
    mla_w = (w["q_latent_norm"], w["kv_latent_norm"], w["q_head_norm"], w["k_head_norm"], w["w_uq"], w["w_uk"],
             w["w_uv"])
    q, k, v, cqn, ckvn = _mla_prep_call(z, *mla_w, tabs, lp, te)
    o, lse = _attn_fwd_call(q, k, v, nb, lp, host("attn_fwd"))
    lru_w = (w["conv_w"], w["conv_b"], w["gate_a_w"], w["gate_a_b"], w["gate_x_w"], w["gate_x_b"], w["lru_lambda"])
    yl, hs = _lru_fwd_call(z, *lru_w, nb, lp, host("lru_fwd"))
    y, h2, u2 = _mix_out_call(o, yl, w["attn_out_norm"], w["lru_out_norm"], w["w_out"], h1, w["ffn2_norm"], te)
    s2 = ffn_act("ffn2", u2, False)
    dh3, dh3b, g["final_norm"], loss = _ffn_down_loss_call("ffn2_down", s2[3], w["ffn2_w_down"], h2, w["final_norm"],
                                                           target, lp, te)
    g["loss"] = loss

    dh2, dh2b, dya, dyl, g["attn_out_norm"], g["lru_out_norm"] = ffn_bwd(
        "ffn2", h2, s2, dh3, dh3b, False, (w["w_out"], o, yl, w["attn_out_norm"], w["lru_out_norm"]))
    g["w_out"] = _wgrad_call("dw_out", y, dh2b)
    dq, dk, dv = _attn_bwd_call(q, k, v, o, lse, dya, nb, lp, host("attn_bwd"))
    (dz_mla, dqp, dkn, dvv, g["q_latent_norm"], g["kv_latent_norm"], g["q_head_norm"],
     g["k_head_norm"]) = _mla_prep_bwd_call(z, dq, dk, dv, *mla_w, tabs, lp, te, host("mla_prep_bwd"))
    g["w_uq"] = _wgrad_call("dw_uq", dqp, cqn)
    g["w_uk"] = _wgrad_call("dw_uk", ckvn, dkn)
    g["w_uv"] = _wgrad_call("dw_uv", ckvn, dvv)
    (dz, g["conv_w"], g["conv_b"], g["gate_a_w"], g["gate_a_b"], g["gate_x_w"], g["gate_x_b"],
     g["lru_lambda"]) = _lru_bwd_call(z, dz_mla, hs, dyl, *lru_w, nb, lp, host("lru_bwd"))
    g["w_in"] = _wgrad_call("dw_in", dz, un)
    dh1, dh1b, g["mix_norm"] = _norm_in_bwd_call("mix_din", [(dz, w["w_in"])], h1, w["mix_norm"], dh2, te,
                                                 host("mix_din"))
    dh0 = ffn_bwd("ffn1", h0, s1, dh1, dh1b, True)[0]
    return loss, dh0, g


def _place():
    x, y, c = lax.axis_index("x"), lax.axis_index("y"), lax.axis_index("c")
    return x, y, c, [(1 - x, y), (x, 1 - y), (1 - x, 1 - y)]


def _any_specs(n):
    return [pl.BlockSpec(memory_space=pl.ANY)] * n


def _remote(src, dst, sems, k, dev):
    send_sems, recv_sems, base = sems
    return pltpu.make_async_remote_copy(src_ref=src, dst_ref=dst, send_sem=send_sems.at[base + k],
                                        recv_sem=recv_sems.at[base + k], device_id=dev, device_id_type=MESH)


EW_VMEM_BYTES = 24 * 1024 * 1024


def _fit_rows(rows, cols, blocks):
    return _row_tile(rows, max(16, int(EW_VMEM_BYTES // (8 * blocks)) // cols))


class _Geom:
    def __init__(self, n0, n1, blocks=1.0):
        self.n0, self.n1 = n0, n1
        self.axis = 0 if n0 % 32 == 0 else 1
        self.h0, self.h1 = (n0 // 2, n1) if self.axis == 0 else (n0, n1 // 2)
        self.tr = _fit_rows(self.h0, self.h1, blocks)
        self.nblk = self.h0 // self.tr

    def half_ref(self, ref, lead, idx):
        if self.axis == 0:
            return ref.at[(*lead, pl.ds(idx * self.h0, self.h0))]
        return ref.at[(*lead, slice(None), pl.ds(idx * self.h1, self.h1))]

    def half_block(self, lead, i, idx):
        return (*lead, idx * self.nblk + i, 0) if self.axis == 0 else (*lead, i, idx)


class _Comm:
    def __init__(self, ins, out_shapes, aliases, n_sems, start, finish, deliver):
        self.ins, self.out_shapes, self.aliases, self.n_sems = list(ins), list(out_shapes), dict(aliases), n_sems
        self.start, self.finish, self.deliver = start, finish, deliver

    def scratch(self):
        return [pltpu.SemaphoreType.DMA((self.n_sems,)), pltpu.SemaphoreType.DMA((self.n_sems,))]


def _comm_call(name, comm):
    n_in = len(comm.ins)

    def body(*refs):
        ins, outs, sems = refs[:n_in], refs[n_in:-2], (*refs[-2:], 0)
        comm.start(ins, outs, sems)
        comm.finish(ins, outs, sems)

    res = pl.pallas_call(
        body, name=name, out_shape=comm.out_shapes, in_specs=_any_specs(n_in),
        out_specs=_any_specs(len(comm.out_shapes)), input_output_aliases=comm.aliases,
        scratch_shapes=comm.scratch())(*comm.ins)
    return comm.deliver(list(res))


def _hosted_call(body, *, name, grid, in_specs, out_specs, out_shape, args, scratch_shapes=(), dims=None, comm=None,
                 prefetch=None, aliases=None):
    aliases = dict(aliases or {})
    in_specs, out_specs, out_shape = list(in_specs), list(out_specs), list(out_shape)
    n_pre = 0 if prefetch is None else 1

    def call(fn, in_specs, out_specs, out_shape, scratch, aliases, dims, args):
        if prefetch is None:
            return pl.pallas_call(
                fn, name=name, grid=grid, in_specs=in_specs, out_specs=out_specs, out_shape=out_shape,
                scratch_shapes=scratch, input_output_aliases=aliases, compiler_params=_params(dims))(*args)
        spec = pltpu.PrefetchScalarGridSpec(num_scalar_prefetch=1, grid=grid, in_specs=in_specs, out_specs=out_specs,
                                            scratch_shapes=scratch)
        return pl.pallas_call(
            fn, name=name, grid_spec=spec, out_shape=out_shape,
            input_output_aliases={i + 1: o for i, o in aliases.items()}, compiler_params=_params(dims))(prefetch, *args)

    if comm is None:
        return list(call(body, in_specs, out_specs, out_shape, list(scratch_shapes), aliases,
                         dims or ("arbitrary",) * len(grid), args))
    n_in, n_out, n_ci, n_co = len(in_specs), len(out_specs), len(comm.ins), len(comm.out_shapes)

    def wrapped(*refs):
        pre, refs = refs[:n_pre], refs[n_pre:]
        ins, cins = refs[:n_in], refs[n_in:n_in + n_ci]
        outs = refs[n_in + n_ci:n_in + n_ci + n_out]
        couts = refs[n_in + n_ci + n_out:n_in + n_ci + n_out + n_co]
        scratch, sems = refs[n_in + n_ci + n_out + n_co:-2], (*refs[-2:], 0)
        first = functools.reduce(jnp.logical_and, [pl.program_id(k) == 0 for k in range(len(grid))])
        last = functools.reduce(jnp.logical_and, [pl.program_id(k) == grid[k] - 1 for k in range(len(grid))])

        @pl.when(first)
        def _():
            comm.start(cins, couts, sems)

        body(*pre, *ins, *outs, *scratch)

        @pl.when(last)
        def _():
            comm.finish(cins, couts, sems)

    res = call(wrapped, in_specs + _any_specs(n_ci), out_specs + _any_specs(n_co), out_shape + comm.out_shapes,
               list(scratch_shapes) + comm.scratch(),
               {**aliases, **{n_in + i: n_out + o for i, o in comm.aliases.items()}},
               ("arbitrary",) * len(grid), (*args, *comm.ins))
    comm.deliver(list(res[n_out:]))
    return list(res[:n_out])


def _gather_comm(bufs, deliver):
    n = len(bufs)
    geoms = [_Geom(*b.shape[1:]) for b in bufs]

    def first(outs, sems):
        x, y, c, chips = _place()
        cps = []
        for a in range(n):
            mine = geoms[a].half_ref(outs[a], (2 * x + y,), c)
            cps += [_remote(mine, mine, sems, 6 * a + j, (cx, cy, c)) for j, (cx, cy) in enumerate(chips)]
        return cps

    def start(ins, outs, sems):
        for cp in first(outs, sems):
            cp.start()

    def finish(ins, outs, sems):
        x, y, c, chips = _place()
        sib = (x, y, 1 - c)
        passed = []
        for a in range(n):
            for j, (cx, cy) in enumerate(chips):
                land = geoms[a].half_ref(outs[a], (2 * cx + cy,), c)
                _remote(land, land, sems, 6 * a + j, sib).wait_recv()
                cp = _remote(land, land, sems, 6 * a + 3 + j, sib)
                cp.start()
                passed.append(cp)
        for a in range(n):
            for j, (cx, cy) in enumerate(chips):
                land = geoms[a].half_ref(outs[a], (2 * cx + cy,), 1 - c)
                _remote(land, land, sems, 6 * a + 3 + j, sib).wait_recv()
        for cp in first(outs, sems) + passed:
            cp.wait_send()

    return _Comm(bufs, [jax.ShapeDtypeStruct(b.shape, b.dtype) for b in bufs], {a: a for a in range(n)}, 6 * n,
                 start, finish, deliver)


def _reduce_pair_comm(grads, deliver):
    n = len(grads)
    geoms = [_Geom(*a.shape[1:]) for a in grads]

    def copies(ins, outs, sems):
        x, y, c, _ = _place()
        return [_remote(geoms[a].half_ref(ins[a], (slice(None),), 1 - c), outs[a], sems, a, (x, y, 1 - c))
                for a in range(n)]

    def start(ins, outs, sems):
        for cp in copies(ins, outs, sems):
            cp.start()

    def finish(ins, outs, sems):
        cps = copies(ins, outs, sems)
        for cp in cps:
            cp.wait_recv()
        for cp in cps:
            cp.wait_send()

    shapes = [jax.ShapeDtypeStruct((N_SHARD, g.h0, g.h1), a.dtype) for a, g in zip(grads, geoms)]
    return _Comm(grads, shapes, {}, n, start, finish, deliver)


def _reduce_chips_comm(parts, deliver):
    n = len(parts)

    def copies(ins, outs, sems):
        x, y, c, chips = _place()
        return [_remote(ins[a].at[2 * cx + cy], outs[a].at[j], sems, 3 * a + j, (cx, cy, c))
                for a in range(n) for j, (cx, cy) in enumerate(chips)]

    def start(ins, outs, sems):
        for cp in copies(ins, outs, sems):
            cp.start()

    def finish(ins, outs, sems):
        cps = copies(ins, outs, sems)
        for cp in cps:
            cp.wait_recv()
        for cp in cps:
            cp.wait_send()

    shapes = [jax.ShapeDtypeStruct((3,) + a.shape[1:], a.dtype) for a in parts]
    return _Comm(parts, shapes, {}, 3 * n, start, finish, deliver)


def _share_pair_comm(bufs, deliver):
    n = len(bufs)
    geoms = [_Geom(*b.shape) for b in bufs]

    def copies(outs, sems):
        x, y, c, _ = _place()
        cps = []
        for a in range(n):
            mine = geoms[a].half_ref(outs[a], (), c)
            cps.append(_remote(mine, mine, sems, a, (x, y, 1 - c)))
        return cps

    def start(ins, outs, sems):
        for cp in copies(outs, sems):
            cp.start()

    def finish(ins, outs, sems):
        x, y, c, _ = _place()
        for a in range(n):
            land = geoms[a].half_ref(outs[a], (), 1 - c)
            _remote(land, land, sems, a, (x, y, 1 - c)).wait_recv()
        for cp in copies(outs, sems):
            cp.wait_send()

    return _Comm(bufs, [jax.ShapeDtypeStruct(b.shape, b.dtype) for b in bufs], {a: a for a in range(n)}, n,
                 start, finish, deliver)


def _small_comm(pack, deliver):
    r, d = pack.shape

    def copies(ins, outs, sems):
        x, y, c, _ = _place()
        cps = []
        for k in range(1, 8):
            peer = (x ^ ((k >> 2) & 1), y ^ ((k >> 1) & 1), c ^ (k & 1))
            cps.append(_remote(ins[0], outs[0].at[4 * x + 2 * y + c], sems, k - 1, peer))
        return cps

    def start(ins, outs, sems):
        for cp in copies(ins, outs, sems):
            cp.start()

    def finish(ins, outs, sems):
        cps = copies(ins, outs, sems)
        for cp in cps:
            cp.wait_recv()
        for cp in cps:
            cp.wait_send()

    return _Comm([pack], [jax.ShapeDtypeStruct((8, r, d), pack.dtype)], {}, 7, start, finish, deliver)


def _join_comms(comms):
    comms = [c for c in comms if c is not None]
    if len(comms) <= 1:
        return comms[0] if comms else None
    ins, out_shapes, aliases, spans, n_sems = [], [], {}, [], 0
    for c in comms:
        aliases.update({len(ins) + i: len(out_shapes) + o for i, o in c.aliases.items()})
        spans.append((len(ins), len(ins) + len(c.ins), len(out_shapes), len(out_shapes) + len(c.out_shapes), n_sems))
        ins += c.ins
        out_shapes += c.out_shapes
        n_sems += c.n_sems

    def run(which):
        def go(all_ins, all_outs, sems):
            for c, (i0, i1, o0, o1, base) in zip(comms, spans):
                getattr(c, which)(all_ins[i0:i1], all_outs[o0:o1], (sems[0], sems[1], sems[2] + base))
        return go

    def deliver(outs):
        for c, (_, _, o0, o1, _) in zip(comms, spans):
            c.deliver(outs[o0:o1])
        return outs

    return _Comm(ins, out_shapes, aliases, n_sems, run("start"), run("finish"), deliver)


def _ew_call(name, fn, ins, out_dtypes):
    shape = ins[0].shape
    cols = shape[-1]
    rows = 1
    for s_ in shape[:-1]:
        rows *= s_
    ins2 = [a.reshape(rows, cols) for a in ins]
    tr = rows
    for t in range(16, min(rows, max(16, (1 << 19) // cols)) + 1, 16):
        if rows % t == 0:
            tr = t
    no = len(out_dtypes)

    def body(*refs):
        outs = fn(*[r[...] for r in refs[:len(ins2)]])
        for ref, val in zip(refs[len(ins2):], outs):
            ref[...] = val.astype(ref.dtype)

    spec = pl.BlockSpec((tr, cols), lambda i: (i, 0))
    res = pl.pallas_call(
        body, name=name, grid=(rows // tr,), in_specs=[spec] * len(ins2), out_specs=[spec] * no,
        out_shape=[jax.ShapeDtypeStruct((rows, cols), dt) for dt in out_dtypes],
        compiler_params=_params(("parallel",)))(*ins2)
    return [r.reshape(shape) for r in res]


def _adamw_math(w, g, m, v):
    m = ADAM_B1 * m + (1.0 - ADAM_B1) * g
    v = ADAM_B2 * v + (1.0 - ADAM_B2) * (g * g)
    m_hat = m / (1.0 - ADAM_B1 ** ADAM_STEP)
    v_hat = v / (1.0 - ADAM_B2 ** ADAM_STEP)
    delta = -ADAM_LR * (m_hat / (jnp.sqrt(v_hat) + ADAM_EPS) + ADAM_WD * w)
    return delta, m, v


def _adamw_call(name, w, g, m, v):
    return _ew_call(name, _adamw_math, [w, g, m, v], [F32, F32, F32])


def _tiled_call(name, fn, place, grid, in_items, out_items, comm=None):
    ni = len(in_items)

    def body(place_ref, *refs):
        vals = fn(*[r[...] for r in refs[:ni]])
        for ref, val in zip(refs[ni:], vals):
            ref[...] = val.astype(ref.dtype)

    return _hosted_call(
        body, name=name, grid=grid, in_specs=[pl.BlockSpec(blk, imap) for _, blk, imap in in_items],
        out_specs=[pl.BlockSpec(blk, imap) for _, _, blk, imap in out_items],
        out_shape=[jax.ShapeDtypeStruct(shp, dt) for shp, dt, _, _ in out_items],
        args=[a for a, _, _ in in_items], prefetch=place, comm=comm)


def _cast_call(name, place, shards, comm=None):
    n0, n1 = shards[0].shape
    tr = _fit_rows(n0, n1, 1.5 * len(shards))
    ins = [(a, (tr, n1), lambda i, p: (i, 0)) for a in shards]
    outs = [((N_SHARD, n0, n1), BF16, (1, tr, n1), lambda i, p: (p[0], i, 0)) for _ in shards]
    return _tiled_call(name, lambda *v: [x[None] for x in v], place, (n0 // tr,), ins, outs, comm)


def _pair_sum_call(name, place, fulls, gots):
    k = len(fulls)
    g = _Geom(*fulls[0].shape[1:], blocks=2.5 * k)
    blk = (1, g.tr, g.h1)
    ins = [(a, blk, lambda s, i, p: g.half_block((s,), i, p[1])) for a in fulls]
    ins += [(a, blk, lambda s, i, p: (s, i, 0)) for a in gots]
    outs = [((N_SHARD, g.h0, g.h1), BF16, blk, lambda s, i, p: (s, i, 0)) for _ in fulls]
    return _tiled_call(name, lambda *v: [v[j] + v[k + j] for j in range(k)], place, (N_SHARD, g.nblk), ins, outs)


def _chip_sum_call(name, place, fulls, gots, recvs, comm=None):
    k = len(fulls)
    g = _Geom(*fulls[0].shape[1:], blocks=4.5 * k)
    blk = (1, g.tr, g.h1)
    ins = [(a, blk, lambda i, p: g.half_block((p[0],), i, p[1])) for a in fulls]
    ins += [(a, blk, lambda i, p: (p[0], i, 0)) for a in gots]
    ins += [(a, (3, g.tr, g.h1), lambda i, p: (0, i, 0)) for a in recvs]
    outs = [((g.n0, g.n1), F32, (g.tr, g.h1), lambda i, p: g.half_block((), i, p[1])) for _ in fulls]

    def fn(*v):
        res = []
        for j in range(k):
            r = v[2 * k + j].astype(F32)
            res.append(v[j][0] + v[k + j][0] + r[0] + r[1] + r[2])
        return res

    return _tiled_call(name, fn, place, (g.nblk,), ins, outs, comm)


def _adamw_group_call(name, ws, gs, ms, vs, comm=None):
    k = len(ws)
    n0, n1 = ws[0].shape
    tr = _fit_rows(n0, n1, 8 * k)
    spec = pl.BlockSpec((tr, n1), lambda i: (i, 0))

    def body(*refs):
        for j in range(k):
            g = refs[k + j][...]
            delta, m, vv = _adamw_math(refs[j][...], g, refs[2 * k + j][...], refs[3 * k + j][...])
            for ref, val in zip(refs[4 * k + 4 * j:4 * k + 4 * j + 4], (g, delta, m, vv)):
                ref[...] = val

    flat = _hosted_call(
        body, name=name, grid=(n0 // tr,), in_specs=[spec] * (4 * k), out_specs=[spec] * (4 * k),
        out_shape=[jax.ShapeDtypeStruct((n0, n1), F32)] * (4 * k), dims=("parallel",),
        args=(*ws, *gs, *ms, *vs), comm=comm)
    return [flat[4 * j:4 * j + 4] for j in range(k)]


def _small_update_call(me, early, own_early, late, own_late, wp, mp, vp):
    nd, r, d = early.shape

    def body(me_ref, e_ref, oe_ref, l_ref, ol_ref, w_ref, m_ref, v_ref, gs_ref, d_ref, nm_ref, nv_ref):
        mine = me_ref[0]

        def total(g_ref, own_ref):
            acc = None
            for k in range(nd):
                part = jnp.where(mine == k, own_ref[...], g_ref[k])
                acc = part if acc is None else acc + part
            return acc

        gs = total(e_ref, oe_ref)
        ls = total(l_ref, ol_ref)
        gs_ref[...] = gs
        first = gs[0:8] + ls[0:8]
        gs_ref[0:8, :] = first
        gs_ref[ROW_META:ROW_META + N_META, :] = gs[ROW_META:ROW_META + N_META] + ls[8:8 + N_META]
        grads = jnp.concatenate([first, gs[8:SMALL_ADAM_ROWS]], axis=0)
        delta, m, v = _adamw_math(w_ref[...], grads, m_ref[...], v_ref[...])
        d_ref[...] = delta
        nm_ref[...] = m
        nv_ref[...] = v

    vm = pl.BlockSpec(memory_space=pltpu.VMEM)
    ashape = jax.ShapeDtypeStruct((SMALL_ADAM_ROWS, d), F32)
    return pl.pallas_call(
        body, name="small_update", in_specs=[pl.BlockSpec(memory_space=pltpu.SMEM)] + [vm] * 7, out_specs=[vm] * 4,
        out_shape=[jax.ShapeDtypeStruct((r, d), F32), ashape, ashape, ashape],
        compiler_params=pltpu.CompilerParams(vmem_limit_bytes=VMEM_LIMIT_BYTES))(
            me, early, own_early, late, own_late, wp, mp, vp)


SMALL_NAMES = ["ffn1_norm", "mix_norm", "ffn2_norm", "final_norm", "q_latent_norm", "kv_latent_norm",
               "q_head_norm", "k_head_norm", "conv_b", "gate_a_b", "gate_x_b", "lru_lambda", "attn_out_norm",
               "lru_out_norm"]
ROW_CONV_W = 14
ROW_GATE_A = 16
ROW_GATE_X = 48
ROW_META = 80
ROW_LOSS = 96


def _row(a):
    flat = a.reshape(1, -1)
    return jnp.pad(flat, ((0, 0), (0, D_MODEL - flat.shape[1])))


def _pack_small(t, rows):
    parts = [_row(t[nm]) for nm in SMALL_NAMES]
    parts.append(t["conv_w"].reshape(2, D_MODEL))
    parts.append(t["gate_a_w"].reshape(32, D_MODEL))
    parts.append(t["gate_x_w"].reshape(32, D_MODEL))
    p = jnp.concatenate(parts, axis=0)
    return jnp.pad(p, ((0, rows - p.shape[0]), (0, 0)))


def _early_pack(g):
    gs = {nm: g.get(nm, jnp.zeros((1, D_MODEL), F32)) for nm in SMALL_NAMES}
    gs["q_head_norm"] = g["q_head_norm"][:, 0:D_QK]
    gs["k_head_norm"] = g["k_head_norm"][:, 0:D_QK]
    for nm in ("conv_b", "gate_a_b", "gate_x_b", "lru_lambda"):
        gs[nm] = g[nm].reshape(1, LRU_W)
    gs["conv_w"] = g["conv_w"].transpose(1, 0, 2).reshape(CONV_K, LRU_W)
    gs["gate_a_w"] = _gate_blocks(g["gate_a_w"])
    gs["gate_x_w"] = _gate_blocks(g["gate_x_w"])
    return jnp.concatenate([_pack_small(gs, ROW_META), jnp.zeros((N_META, D_MODEL), F32), _row(g["loss"][:, 0:1]),
                            jnp.zeros((SMALL_ROWS - ROW_LOSS - 1, D_MODEL), F32)], axis=0)


def _unpack_small(p, like):
    out = {}
    for k, nm in enumerate(SMALL_NAMES):
        out[nm] = p[k, 0:like[nm].size].reshape(like[nm].shape)
    out["gate_a_w"] = p[ROW_GATE_A:ROW_GATE_A + 32].reshape(like["gate_a_w"].shape)
    out["gate_x_w"] = p[ROW_GATE_X:ROW_GATE_X + 32].reshape(like["gate_x_w"].shape)
    return out


def _gate_dense(wg):
    w4 = wg[0].reshape(N_LRU_TILES, 2, 64, 64)
    zero = jnp.zeros((N_LRU_TILES, 64, 64), wg.dtype)
    top = jnp.concatenate([w4[:, 0], zero], axis=2)
    bot = jnp.concatenate([zero, w4[:, 1]], axis=2)
    return jnp.concatenate([top, bot], axis=1).astype(BF16)


def _gate_blocks(dw):
    return jnp.stack([dw[:, 0:64, 0:64], dw[:, 64:128, 64:128]], axis=1).reshape(8, 64, 64)


BIG_NAMES = ["ffn1_w_gate", "ffn1_w_up", "ffn1_w_down", "w_in", "w_uq", "w_uk", "w_uv", "w_out", "ffn2_w_gate",
             "ffn2_w_up", "ffn2_w_down"]
BIG_GROUPS = [["ffn1_w_gate", "ffn1_w_up", "ffn1_w_down", "ffn2_w_gate", "ffn2_w_up", "ffn2_w_down"], ["w_in"],
              ["w_uq"], ["w_uk", "w_uv"], ["w_out"]]
TRANSPOSED = ("ffn1_w_gate", "ffn1_w_up", "ffn2_w_gate", "ffn2_w_up", "w_in", "w_uq")


def _to2d(nm, a):
    return a[0].T if nm in TRANSPOSED else a[0]


def _from2d(nm, a):
    return (a.T if nm in TRANSPOSED else a)[None]


WEIGHT_NAMES = ["meta_tokens", "ffn1_norm", "ffn1_w_gate", "ffn1_w_up", "ffn1_w_down", "mix_norm", "w_in",
                "q_latent_norm", "w_uq", "kv_latent_norm", "w_uk", "w_uv", "q_head_norm", "k_head_norm", "conv_w",
                "conv_b", "gate_a_w", "gate_a_b", "gate_x_w", "gate_x_b", "lru_lambda", "attn_out_norm",
                "lru_out_norm", "w_out", "ffn2_norm", "ffn2_w_gate", "ffn2_w_up", "ffn2_w_down", "final_norm"]


def _weight_from(nm, slots):
    if nm == "w_in":
        win = slots.reshape(IN_WIDTH, D_MODEL)
        lru = win[Z_KR + D_ROPE:].reshape(2, N_LRU_TILES, LRU_TILE, D_MODEL).transpose(1, 0, 2, 3)
        return jnp.concatenate([win[0:Z_KR + D_ROPE], jnp.zeros((128 - D_ROPE, D_MODEL), BF16),
                                lru.reshape(2 * LRU_W, D_MODEL)], axis=0)
    if nm == "w_uq":
        return jnp.pad(slots, ((0, 0), (0, D_QKP - D_QK), (0, 0))).reshape(HEADS * D_QKP, Q_RANK)
    if nm in ("w_uk", "w_uv"):
        return slots.transpose(1, 0, 2).reshape(KV_RANK, HEADS * D_NOPE)
    if nm == "w_out":
        return slots.reshape(D_MODEL, D_MODEL)
    return slots


def _small_weights(p, small):
    w = {nm: p[nm] for nm in SMALL_NAMES}
    w["q_head_norm"] = jnp.pad(p["q_head_norm"], ((0, 0), (0, D_QKP - D_QK)))
    w["k_head_norm"] = jnp.pad(p["k_head_norm"], ((0, 0), (0, D_QKP - D_QK)))
    w["conv_w"] = small[:, N_META:N_META + 2, :].reshape(N_SHARD, CONV_K, LRU_TILE)
    w["gate_a_w"] = _gate_dense(p["gate_a_w"])
    w["gate_x_w"] = _gate_dense(p["gate_x_w"])
    meta = small[:, 0:N_META, :].transpose(1, 0, 2).reshape(N_META, D_MODEL)
    return w, meta


def _full_weights(p, gathered, small):
    w, meta = _small_weights(p, small)
    w.update({nm: _weight_from(nm, gathered[nm]) for nm in BIG_NAMES})
    return w, meta


def _shard_grad(nm, g):
    if nm == "w_in":
        lru = g[Z_MLA:].reshape(N_LRU_TILES, 2, LRU_TILE, D_MODEL).transpose(1, 0, 2, 3).reshape(2 * LRU_W, D_MODEL)
        return jnp.concatenate([g[0:Z_KR + D_ROPE], lru], axis=0).reshape(N_SHARD, IN_WIDTH // N_SHARD, D_MODEL)
    if nm == "w_uq":
        return g.reshape(HEADS, D_QKP, Q_RANK)[:, 0:D_QK, :]
    if nm in ("w_uk", "w_uv"):
        return g.reshape(KV_RANK, HEADS, D_NOPE).transpose(1, 0, 2)
    if nm == "w_out":
        return g.reshape(N_SHARD, D_MODEL // N_SHARD, D_MODEL)
    return g


def _shard_grads(g):
    return {nm: _shard_grad(nm, g[nm]) for nm in BIG_NAMES}


GATHER_AT = {"ffn1_norm": ["ffn1_w_gate"], "ffn1_gate": ["ffn1_w_up"], "ffn1_upact": ["ffn1_w_down"],
             "ffn1_down": ["w_in", "w_uq", "w_uk", "w_uv", "w_out"], "attn_fwd": ["ffn2_w_down", "ffn2_w_gate"],
             "lru_fwd": ["ffn2_w_up"]}
PAIR_AT = [("ffn2_din", ["ffn2_w_gate", "ffn2_w_up", "ffn2_w_down"]),
           ("mix_din", ["w_out", "w_uq", "w_uk", "w_uv", "w_in"]),
           ("ffn1_dwg", ["ffn1_w_down"]), ("ffn1_dwu", ["ffn1_w_gate"]), ("ffn1_din_a", ["ffn1_w_up"])]
CHIPS_AT = [("attn_bwd", ["ffn2_w_down", "ffn2_w_gate"]), ("mla_prep_bwd", ["ffn2_w_up"]),
            ("ffn1_dact", ["w_out", "w_uq", "w_uk", "w_uv", "w_in"]),
            ("ffn1_dwu", ["ffn1_w_down"]), ("ffn1_din_a", ["ffn1_w_gate"]), ("ffn1_din_b", ["ffn1_w_up"])]
SHARE_EARLY_GROUPS, SHARE_EARLY_AT = 3, "ffn1_dwd"
SMALL_EARLY_AT = "mix_din"


def _same_shape_groups(names):
    return [[nm for nm in grp if nm in names] for grp in BIG_GROUPS if any(nm in names for nm in grp)]


class _Sched:
    def __init__(self, place, w, slots):
        self.place, self.w, self.slots = place, w, slots
        self.g = None
        self.sharded, self.from_pair, self.chip_bf16, self.from_chips = {}, {}, {}, {}
        self.early = self.early_all = None
        self.shared = {}

    def host(self, stage):
        comms = []
        if stage in GATHER_AT:
            comms.append(self.gather(GATHER_AT[stage]))
        comms += [self.chips(names) for at, names in CHIPS_AT if at == stage]
        comms += [self.pair(names) for at, names in PAIR_AT if at == stage]
        if stage == SMALL_EARLY_AT:
            comms.append(self.small_early())
        if stage == SHARE_EARLY_AT:
            comms.append(self.share_early())
        return _join_comms(comms)

    def small_early(self):
        self.early = _early_pack(self.g)

        def deliver(outs):
            self.early_all = outs[0]
            return outs

        return _small_comm(self.early, deliver)

    def gather(self, names):
        def deliver(outs):
            self.w.update({nm: _weight_from(nm, o) for nm, o in zip(names, outs)})
            return outs

        return _gather_comm([self.slots[nm] for nm in names], deliver)

    def pair(self, names):
        self.sharded.update({nm: _shard_grad(nm, self.g[nm]) for nm in names})

        def deliver(outs):
            self.from_pair.update(zip(names, outs))
            for grp in _same_shape_groups(names):
                sums = _pair_sum_call("pair_sum_" + grp[0], self.place, [self.sharded[nm] for nm in grp],
                                      [self.from_pair[nm] for nm in grp])
                self.chip_bf16.update(zip(grp, sums))
            return outs

        return _reduce_pair_comm([self.sharded[nm] for nm in names], deliver)

    def chips(self, names):
        def deliver(outs):
            self.from_chips.update(zip(names, outs))
            return outs

        return _reduce_chips_comm([self.chip_bf16[nm] for nm in names], deliver)

    def chip_sums(self, names, comm=None):
        out = {}
        for grp in _same_shape_groups(names):
            sums = _chip_sum_call("chip_sum_" + grp[0], self.place, [self.sharded[nm] for nm in grp],
                                  [self.from_pair[nm] for nm in grp], [self.from_chips[nm] for nm in grp], comm)
            comm = None
            out.update(zip(grp, sums))
        return out

    def share_early(self):
        names = [nm for at, grp in CHIPS_AT[:SHARE_EARLY_GROUPS] for nm in grp]
        mine = self.chip_sums(names)
        return _share_pair_comm([mine[nm] for nm in names], lambda o: self.shared.update(zip(names, o)))


def kernel(x, meta_tokens, ffn1_norm, ffn1_w_gate, ffn1_w_up, ffn1_w_down, mix_norm, w_in, q_latent_norm, w_uq, kv_latent_norm, w_uk, w_uv, q_head_norm, k_head_norm, conv_w, conv_b, gate_a_w, gate_a_b, gate_x_w, gate_x_b, lru_lambda, attn_out_norm, lru_out_norm, w_out, ffn2_norm, ffn2_w_gate, ffn2_w_up, ffn2_w_down, final_norm, loss_target, m_meta_tokens, m_ffn1_norm, m_ffn1_w_gate, m_ffn1_w_up, m_ffn1_w_down, m_mix_norm, m_w_in, m_q_latent_norm, m_w_uq, m_kv_latent_norm, m_w_uk, m_w_uv, m_q_head_norm, m_k_head_norm, m_conv_w, m_conv_b, m_gate_a_w, m_gate_a_b, m_gate_x_w, m_gate_x_b, m_lru_lambda, m_attn_out_norm, m_lru_out_norm, m_w_out, m_ffn2_norm, m_ffn2_w_gate, m_ffn2_w_up, m_ffn2_w_down, m_final_norm, v_meta_tokens, v_ffn1_norm, v_ffn1_w_gate, v_ffn1_w_up, v_ffn1_w_down, v_mix_norm, v_w_in, v_q_latent_norm, v_w_uq, v_kv_latent_norm, v_w_uk, v_w_uv, v_q_head_norm, v_k_head_norm, v_conv_w, v_conv_b, v_gate_a_w, v_gate_a_b, v_gate_x_w, v_gate_x_b, v_lru_lambda, v_attn_out_norm, v_lru_out_norm, v_w_out, v_ffn2_norm, v_ffn2_w_gate, v_ffn2_w_up, v_ffn2_w_down, v_final_norm):
    args = locals()
    p = {nm: args[nm] for nm in WEIGHT_NAMES}
    mom = {nm: args["m_" + nm] for nm in WEIGHT_NAMES}
    var = {nm: args["v_" + nm] for nm in WEIGHT_NAMES}
    nb, seq, d = x.shape
    lp = CHUNK + seq
    xi, yi, ci = lax.axis_index("x"), lax.axis_index("y"), lax.axis_index("c")
    chip = 2 * xi + yi

    place = jnp.stack([chip, ci]).astype(jnp.int32)
    p2 = {nm: _to2d(nm, p[nm]) for nm in BIG_NAMES}
    m2 = {nm: _to2d(nm, mom[nm]) for nm in BIG_NAMES}
    v2 = {nm: _to2d(nm, var[nm]) for nm in BIG_NAMES}

    slots = {}
    small_shard = jnp.concatenate(
        [meta_tokens, conv_w[0].reshape(2, 2 * LRU_TILE), jnp.zeros((14, 2 * LRU_TILE), F32)], axis=0)
    small_slots = lax.dynamic_update_slice(jnp.zeros((N_SHARD,) + small_shard.shape, F32), small_shard[None],
                                           (chip, 0, 0))
    first = []
    comm = _gather_comm([small_slots], first.extend)
    for grp in BIG_GROUPS:
        for nm, buf in zip(grp, _cast_call("cast_" + grp[0], place, [p2[nm] for nm in grp], comm)):
            slots[nm] = buf
        comm = None
    w, meta = _small_weights(p, first[0])
    sched = _Sched(place, w, slots)

    h0 = jnp.concatenate(
        [jnp.zeros((nb, PAD_ROWS, d), F32), jnp.broadcast_to(meta[None], (nb, N_META, d)), x], axis=1)
    target = jnp.pad(loss_target, ((0, 0), (CHUNK, 0), (0, 0)))
    loss_part, dh0, g = _local_step(h0.reshape(nb * lp, d), target.reshape(nb * lp, d), w, nb, lp, sched)
    dh0 = dh0.reshape(nb, lp, d)
    grad_x = dh0[:, CHUNK:, :]

    late = jnp.concatenate([g["ffn1_norm"], g["mix_norm"], jnp.zeros((6, D_MODEL), F32),
                            jnp.sum(dh0[:, PAD_ROWS:CHUNK, :], axis=0)], axis=0)
    shared = sched.shared
    rest = [nm for at, names in CHIPS_AT[SHARE_EARLY_GROUPS:] if at is not None for nm in names]
    last = [nm for at, names in CHIPS_AT if at is None for nm in names]
    late_box = {}
    mine = sched.chip_sums(rest, _small_comm(late, lambda o: late_box.update(all=o[0])))
    share = _share_pair_comm([mine[nm] for nm in rest], lambda o: shared.update(zip(rest, o)))
    _comm_call("share_pair", _join_comms([share, sched.chips(last) if last else None]))
    if last:
        mine = sched.chip_sums(last)
        _comm_call("share_last", _share_pair_comm([mine[nm] for nm in last], lambda o: shared.update(zip(last, o))))
    late_all = late_box["all"]
    small_like = {nm: p[nm] for nm in SMALL_NAMES + ["gate_a_w", "gate_x_w"]}

    def pack_w(t):
        tt = {nm: t[nm] for nm in SMALL_NAMES + ["gate_a_w", "gate_x_w"]}
        tt["conv_w"] = jnp.zeros((CONV_K, LRU_W), F32)
        return _pack_small(tt, SMALL_ADAM_ROWS)

    me = (4 * xi + 2 * yi + ci).astype(jnp.int32).reshape(1)
    gsum, dsm, msm, vsm = _small_update_call(me, sched.early_all, sched.early, late_all, late, pack_w(p),
                                             pack_w(mom), pack_w(var))
    grads = _unpack_small(gsum, small_like)
    delta = _unpack_small(dsm, small_like)
    new_m = _unpack_small(msm, small_like)
    new_v = _unpack_small(vsm, small_like)
    loss = gsum[ROW_LOSS, 0]
    gmeta = gsum[ROW_META:ROW_META + N_META].reshape(N_META, N_SHARD, D_MODEL // N_SHARD)
    grads["meta_tokens"] = lax.dynamic_index_in_dim(gmeta, chip, axis=1, keepdims=False)
    gconv = gsum[ROW_CONV_W:ROW_CONV_W + 2].reshape(CONV_K, N_SHARD, LRU_TILE)
    grads["conv_w"] = lax.dynamic_index_in_dim(gconv, chip, axis=1, keepdims=False)[None]
    for nm in ("meta_tokens", "conv_w"):
        delta[nm], new_m[nm], new_v[nm] = _adamw_call("adamw_" + nm, p[nm], grads[nm], mom[nm], var[nm])

    for names in BIG_GROUPS:
        res = _adamw_group_call("adamw_" + names[0], [p2[nm] for nm in names], [shared[nm] for nm in names],
                                [m2[nm] for nm in names], [v2[nm] for nm in names])
        for nm, (gg, dd, mm, vv) in zip(names, res):
            grads[nm], delta[nm], new_m[nm], new_v[nm] = (_from2d(nm, t) for t in (gg, dd, mm, vv))

    return (loss, grad_x, *[grads[nm] for nm in WEIGHT_NAMES], *[delta[nm] for nm in WEIGHT_NAMES],
            *[new_m[nm] for nm in WEIGHT_NAMES], *[new_v[nm] for nm in WEIGHT_NAMES])
```

```python
import functools
import math

import jax
import jax.numpy as jnp
import numpy as np
from jax import lax
from jax.experimental import pallas as pl
from jax.experimental.pallas import tpu as pltpu

F32 = jnp.float32
BF16 = jnp.bfloat16
MESH = pl.DeviceIdType.MESH

D_MODEL = 1024
N_META = 16
CHUNK = 64
PAD_ROWS = CHUNK - N_META
HEADS = 4
D_NOPE = 128
D_ROPE = 64
D_QK = D_NOPE + D_ROPE
D_QKP = 256
D_V = 128
KV_RANK = 256
Q_RANK = 384
MLA_W = HEADS * D_V
LRU_W = 512
LRU_TILE = 128
N_LRU_TILES = LRU_W // LRU_TILE
CONV_K = 4
C_RGLRU = 8.0
ROPE_THETA = 10000.0
D_FF = 2816
N_SHARD = 4
EPS = 1e-6
NEG_INF = -1e30
Z_KR = Q_RANK + KV_RANK
Z_MLA = Z_KR + 128
Z_U = Z_MLA
Z_G = Z_U + LRU_W
Z_W = Z_G + LRU_W
IN_WIDTH = Q_RANK + KV_RANK + D_ROPE + 2 * LRU_W

ADAM_LR = 0.001
ADAM_B1 = 0.9
ADAM_B2 = 0.999
ADAM_EPS = 1e-08
ADAM_WD = 0.01
ADAM_STEP = 10

VMEM_LIMIT_BYTES = 56 * 1024 * 1024
SMALL_ROWS = 104
SMALL_ADAM_ROWS = 80


def _params(sem):
    return pltpu.CompilerParams(dimension_semantics=sem, vmem_limit_bytes=VMEM_LIMIT_BYTES)


def _resident(shape):
    return pl.BlockSpec(tuple(shape), lambda i: (0,) * len(shape), pipeline_mode=pl.Buffered(1))


def _row_tile(rows, target):
    best = 16
    for t in range(16, min(rows, target) + 1, 16):
        if rows % t == 0:
            best = t
    return best


def _col_tile(cols, target):
    best = cols
    for t in range(128, min(cols, target) + 1, 128):
        if cols % t == 0:
            best = t
    return best


def _dot(a, b):
    return jnp.dot(a, b, preferred_element_type=F32)


def _dot_nt(a, b):
    return lax.dot_general(a, b, (((1,), (1,)), ((), ())), preferred_element_type=F32)


def _dot_tn(a, b):
    return lax.dot_general(a, b, (((0,), (0,)), ((), ())), preferred_element_type=F32)


def _rms(x, n):
    return lax.rsqrt(jnp.sum(x * x, axis=-1, keepdims=True) * (1.0 / n) + EPS)


def _rms_bwd(dn, nrm, r, n):
    return r * (dn - nrm * (jnp.sum(dn * nrm, axis=-1, keepdims=True) * (1.0 / n)))


def _gelu(x):
    k = math.sqrt(2.0 / math.pi)
    t = jnp.tanh(k * (x + 0.044715 * x * x * x))
    return 0.5 * x * (1.0 + t), t


def _gelu_grad(x, t):
    k = math.sqrt(2.0 / math.pi)
    return 0.5 * (1.0 + t) + 0.5 * x * (1.0 - t * t) * k * (1.0 + 3.0 * 0.044715 * x * x)


def _sigmoid(x):
    return 0.5 + 0.5 * jnp.tanh(0.5 * x)


def _softplus_neg(lam):
    e = jnp.exp(-jnp.abs(lam))
    log1p = jnp.where(e < 0.01, e * (1.0 - e * (0.5 - e * (1.0 / 3 - e * 0.25))), jnp.log(1.0 + e))
    return jnp.maximum(-lam, 0.0) + log1p


def _rope(t, c, s1, s2):
    return t * c + pltpu.roll(t, 96, 1) * s1 + pltpu.roll(t, 32, 1) * s2


def _rope_t(d, c, s1, s2):
    return d * c + pltpu.roll(d * s1, 32, 1) + pltpu.roll(d * s2, 96, 1)


def _rope_tables(lp):
    pos = (np.arange(lp, dtype=np.int32) - PAD_ROWS).astype(np.float32)
    inv_freq = (ROPE_THETA ** (-np.arange(0, D_ROPE // 2, dtype=np.float32) / (D_ROPE // 2))).astype(np.float32)
    ang = (pos[:, None] * inv_freq[None, :]).astype(np.float32).astype(np.float64)
    cos, sin = np.cos(ang).astype(np.float32), np.sin(ang).astype(np.float32)
    z = np.zeros_like(cos)
    return (jnp.asarray(np.concatenate([cos, cos, z, z], 1)), jnp.asarray(np.concatenate([-sin, z, z, z], 1)),
            jnp.asarray(np.concatenate([z, sin, z, z], 1)))


def _rmsnorm_call(name, h, g, tm, comm=None):
    rows, d = h.shape

    def body(h_ref, g_ref, o_ref):
        x = h_ref[...]
        o_ref[...] = (x * _rms(x, d) * g_ref[...]).astype(BF16)

    return _hosted_call(
        body, name=name, grid=(rows // tm,),
        in_specs=[pl.BlockSpec((tm, d), lambda i: (i, 0)), pl.BlockSpec((1, d), lambda i: (0, 0))],
        out_specs=[pl.BlockSpec((tm, d), lambda i: (i, 0))],
        out_shape=[jax.ShapeDtypeStruct((rows, d), BF16)],
        dims=("parallel",), args=(h, g), comm=comm)[0]


def _ffn_up_call(name, u, wg, wu, tm, comm=None):
    rows, d = u.shape
    ns, fs, _ = wg.shape

    def body(u_ref, wg_ref, wu_ref, g_ref, p_ref, a_ref):
        uu = u_ref[...]
        g = _dot_nt(uu, wg_ref[0])
        p = _dot_nt(uu, wu_ref[0])
        g_ref[0] = g.astype(BF16)
        p_ref[0] = p.astype(BF16)
        a_ref[0] = (g * jax.nn.sigmoid(g) * p).astype(BF16)

    wspec = pl.BlockSpec((1, fs, d), lambda s, i: (s, 0, 0))
    ospec = pl.BlockSpec((1, tm, fs), lambda s, i: (s, i, 0))
    oshape = jax.ShapeDtypeStruct((ns, rows, fs), BF16)
    return _hosted_call(
        body, name=name, grid=(ns, rows // tm),
        in_specs=[pl.BlockSpec((tm, d), lambda s, i: (i, 0)), wspec, wspec],
        out_specs=[ospec, ospec, ospec], out_shape=[oshape, oshape, oshape],
        dims=("parallel", "parallel"), args=(u, wg, wu), comm=comm)


def _ffn_gate_call(name, u, wg, tm, comm=None):
    rows, d = u.shape
    ns, fs, _ = wg.shape

    def body(u_ref, wg_ref, g_ref):
        g_ref[0] = _dot_nt(u_ref[...], wg_ref[0]).astype(BF16)

    return _hosted_call(
        body, name=name, grid=(ns, rows // tm),
        in_specs=[pl.BlockSpec((tm, d), lambda s, i: (i, 0)), pl.BlockSpec((1, fs, d), lambda s, i: (s, 0, 0))],
        out_specs=[pl.BlockSpec((1, tm, fs), lambda s, i: (s, i, 0))],
        out_shape=[jax.ShapeDtypeStruct((ns, rows, fs), BF16)],
        dims=("parallel", "parallel"), args=(u, wg), comm=comm)[0]


def _ffn_upact_call(name, u, wu, gate, tm, comm=None):
    rows, d = u.shape
    ns, fs, _ = wu.shape

    def body(u_ref, wu_ref, g_ref, p_ref, a_ref):
        p = _dot_nt(u_ref[...], wu_ref[0])
        g = g_ref[0].astype(F32)
        p_ref[0] = p.astype(BF16)
        a_ref[0] = (g * jax.nn.sigmoid(g) * p).astype(BF16)

    ospec = pl.BlockSpec((1, tm, fs), lambda s, i: (s, i, 0))
    oshape = jax.ShapeDtypeStruct((ns, rows, fs), BF16)
    return _hosted_call(
        body, name=name, grid=(ns, rows // tm),
        in_specs=[pl.BlockSpec((tm, d), lambda s, i: (i, 0)), pl.BlockSpec((1, fs, d), lambda s, i: (s, 0, 0)), ospec],
        out_specs=[ospec, ospec], out_shape=[oshape, oshape],
        dims=("parallel", "parallel"), args=(u, wu, gate), comm=comm)


def _loss_tail(x, g, t, row, d):
    r = _rms(x, d)
    n = x * r
    err = jnp.where(row >= CHUNK, n * g - t, 0.0)
    dout = err * (1.0 / d)
    dh = _rms_bwd(dout * g, n, r, d)
    dg = jnp.sum(dout * n, axis=0, keepdims=True)
    part = jnp.sum(jnp.sum(err * err, axis=1, keepdims=True), axis=0, keepdims=True) * (0.5 / d)
    return dh, dg, jnp.broadcast_to(part, (1, 128))


def _ffn_down_call(name, a, wd, h, tm, comm=None, next_gain=None):
    rows, d = h.shape
    ns, _, fs = a.shape
    more = next_gain is not None

    def body(a_ref, wd_ref, h_ref, *rest):
        acc = h_ref[...]
        for s in range(ns):
            acc = acc + 0.5 * _dot(a_ref[s], wd_ref[s])
        rest[-2 if more else -1][...] = acc
        if more:
            rest[-1][...] = (acc * _rms(acc, d) * rest[0][...]).astype(BF16)

    full = pl.BlockSpec((tm, d), lambda i: (i, 0))
    res = _hosted_call(
        body, name=name, grid=(rows // tm,),
        in_specs=[pl.BlockSpec((ns, tm, fs), lambda i: (0, i, 0)), _resident((ns, fs, d)), full]
        + ([pl.BlockSpec((1, d), lambda i: (0, 0))] if more else []),
        out_specs=[full] * (2 if more else 1),
        out_shape=[jax.ShapeDtypeStruct((rows, d), F32)] + ([jax.ShapeDtypeStruct((rows, d), BF16)] if more else []),
        dims=("parallel",), args=(a, wd, h) + ((next_gain,) if more else ()), comm=comm)
    return res if more else res[0]


def _ffn_down_loss_call(name, a, wd, h, g, target, lp, tm):
    rows, d = h.shape
    ns, _, fs = a.shape
    tpe = lp // tm

    def body(a_ref, wd_ref, h_ref, g_ref, t_ref, dh_ref, dhb_ref, dg_ref, loss_ref):
        i = pl.program_id(0)
        acc = h_ref[...]
        for s in range(ns):
            acc = acc + 0.5 * _dot(a_ref[s], wd_ref[s])
        row = (i % tpe) * tm + lax.broadcasted_iota(jnp.int32, (tm, 1), 0)
        dh, dg, loss = _loss_tail(acc, g_ref[...], t_ref[...], row, d)
        dh_ref[...] = dh
        dhb_ref[...] = dh.astype(BF16)

        @pl.when(i == 0)
        def _():
            dg_ref[...] = dg
            loss_ref[...] = loss

        @pl.when(i != 0)
        def _():
            dg_ref[...] += dg
            loss_ref[...] += loss

    full = pl.BlockSpec((tm, d), lambda i: (i, 0))
    gspec = pl.BlockSpec((1, d), lambda i: (0, 0))
    return _hosted_call(
        body, name=name, grid=(rows // tm,),
        in_specs=[pl.BlockSpec((ns, tm, fs), lambda i: (0, i, 0)), _resident((ns, fs, d)), full, gspec, full],
        out_specs=[full, full, gspec, pl.BlockSpec((1, 128), lambda i: (0, 0))],
        out_shape=[jax.ShapeDtypeStruct((rows, d), F32), jax.ShapeDtypeStruct((rows, d), BF16),
                   jax.ShapeDtypeStruct((1, d), F32), jax.ShapeDtypeStruct((1, 128), F32)],
        args=(a, wd, h, g, target))


def _mm_call(name, a, bt, tm, out_dtype):
    rows, k = a.shape
    n = bt.shape[0]

    def body(a_ref, b_ref, o_ref):
        o_ref[...] = _dot_nt(a_ref[...], b_ref[...]).astype(out_dtype)

    return pl.pallas_call(
        body, name=name, grid=(rows // tm,),
        in_specs=[pl.BlockSpec((tm, k), lambda i: (i, 0)), pl.BlockSpec((n, k), lambda i: (0, 0))],
        out_specs=pl.BlockSpec((tm, n), lambda i: (i, 0)),
        out_shape=jax.ShapeDtypeStruct((rows, n), out_dtype),
        compiler_params=_params(("parallel",)))(a, bt)


def _mla_heads(z, gql, gkvl, wuq, wuk, wuv):
    cq = z[:, 0:Q_RANK]
    ckv = z[:, Q_RANK:Z_KR]
    kr = z[:, Z_KR:Z_MLA]
    rq = _rms(cq, Q_RANK)
    nq = cq * rq
    cqn = (nq * gql).astype(BF16)
    rkv = _rms(ckv, KV_RANK)
    nkv = ckv * rkv
    ckvn = (nkv * gkvl).astype(BF16)
    qraw = _dot_nt(cqn, wuq)
    knope = _dot(ckvn, wuk)
    v = _dot(ckvn, wuv)
    skr = jnp.sum(kr * kr, axis=-1, keepdims=True)
    heads = []
    for hd in range(HEADS):
        qh = qraw[:, hd * D_QKP:(hd + 1) * D_QKP]
        rqh = lax.rsqrt(jnp.sum(qh * qh, axis=-1, keepdims=True) * (1.0 / D_QK) + EPS)
        kn = knope[:, hd * D_NOPE:(hd + 1) * D_NOPE]
        rkh = lax.rsqrt((jnp.sum(kn * kn, axis=-1, keepdims=True) + skr) * (1.0 / D_QK) + EPS)
        heads.append((qh * rqh, rqh, kn * rkh, kr * rkh, rkh))
    return dict(rq=rq, nq=nq, cqn=cqn, rkv=rkv, nkv=nkv, ckvn=ckvn, v=v, heads=heads)


def _mla_prep_call(z, gql, gkvl, gqh, gkh, wuq, wuk, wuv, tabs, lp, tm):
    rows = z.shape[0]
    tpe = lp // tm

    def body(z_ref, gql_ref, gkvl_ref, gqh_ref, gkh_ref, wuq_ref, wuk_ref, wuv_ref, c_ref, s1_ref, s2_ref,
             q_ref, k_ref, v_ref):
        m = _mla_heads(z_ref[...], gql_ref[...], gkvl_ref[...], wuq_ref[...], wuk_ref[...], wuv_ref[...])
        c, s1, s2 = c_ref[...], s1_ref[...], s2_ref[...]
        gq, gk = gqh_ref[...], gkh_ref[...]
        row = (pl.program_id(0) % tpe) * tm + lax.broadcasted_iota(jnp.int32, (tm, 1), 0)
        spare = (lax.broadcasted_iota(jnp.int32, (1, D_QKP - D_NOPE), 1) == D_ROPE).astype(F32)
        kmask = jnp.where(row < PAD_ROWS, NEG_INF * math.sqrt(D_QK), 0.0) * spare
        for hd in range(HEADS):
            qn, _, knn, krn, _ = m["heads"][hd]
            qg = qn * gq
            q_ref[hd, :, 0:D_NOPE] = qg[:, 0:D_NOPE].astype(BF16)
            q_ref[hd, :, D_NOPE:D_QKP] = (_rope(qg[:, D_NOPE:D_QKP], c, s1, s2) + spare).astype(BF16)
            k_ref[hd, :, 0:D_NOPE] = (knn * gk[:, 0:D_NOPE]).astype(BF16)
            k_ref[hd, :, D_NOPE:D_QKP] = (_rope(krn * gk[:, D_NOPE:D_QKP], c, s1, s2) + kmask).astype(BF16)
            v_ref[hd] = m["v"][:, hd * D_V:(hd + 1) * D_V].astype(BF16)

    def const(shape):
        return pl.BlockSpec(shape, lambda i: tuple(0 for _ in shape))

    tab = pl.BlockSpec((tm, 128), lambda i: (i % tpe, 0))
    return pl.pallas_call(
        body, name="mla_prep", grid=(rows // tm,),
        in_specs=[pl.BlockSpec((tm, Z_MLA), lambda i: (i, 0)), const((1, Q_RANK)), const((1, KV_RANK)),
                  const((1, D_QKP)), const((1, D_QKP)), const((HEADS * D_QKP, Q_RANK)),
                  const((KV_RANK, HEADS * D_NOPE)), const((KV_RANK, HEADS * D_V)), tab, tab, tab],
        out_specs=[pl.BlockSpec((HEADS, tm, D_QKP), lambda i: (0, i, 0)),
                   pl.BlockSpec((HEADS, tm, D_QKP), lambda i: (0, i, 0)),
                   pl.BlockSpec((HEADS, tm, D_V), lambda i: (0, i, 0))],
        out_shape=[jax.ShapeDtypeStruct((HEADS, rows, D_QKP), BF16),
                   jax.ShapeDtypeStruct((HEADS, rows, D_QKP), BF16),
                   jax.ShapeDtypeStruct((HEADS, rows, D_V), BF16)],
        compiler_params=_params(("parallel",)))(z, gql, gkvl, gqh, gkh, wuq, wuk, wuv, *tabs)


Q_BLOCK_ROWS = 528
Q_BLOCK_ROWS_BWD = 192


def _q_block(lp):
    return _row_tile(lp, Q_BLOCK_ROWS)


def _key_end(ext, lp):
    return min(lp, -(-ext // CHUNK) * CHUNK)


def _diag_bias(qb, j0, nk):
    shift = CHUNK.bit_length() - 1
    r = jnp.right_shift(j0 + lax.broadcasted_iota(jnp.int32, (qb, nk), 0), shift)
    c = jnp.right_shift(j0 + lax.broadcasted_iota(jnp.int32, (qb, nk), 1), shift)
    return jnp.where(c <= r, 0.0, NEG_INF)


def _attn_fwd_call(q, k, v, nb, lp, comm=None):
    rows = nb * lp
    qb = _q_block(lp)
    scale = 1.0 / math.sqrt(D_QK)

    def body(q_ref, k_ref, v_ref, o_ref, lse_ref):
        for j in range(lp // qb):
            j0, ext = j * qb, (j + 1) * qb
            kend = _key_end(ext, lp)
            qj = q_ref[0, j0:ext, :]
            sd = _dot_nt(qj, k_ref[0, j0:kend, :]) * scale + _diag_bias(qb, j0, kend - j0)
            mx = jnp.max(sd, axis=-1, keepdims=True)
            if j > 0:
                so = _dot_nt(qj, k_ref[0, 0:j0, :]) * scale
                mx = jnp.maximum(mx, jnp.max(so, axis=-1, keepdims=True))
            pd = jnp.exp(sd - mx)
            l = jnp.sum(pd, axis=-1, keepdims=True)
            o = _dot(pd.astype(BF16), v_ref[0, j0:kend, :])
            if j > 0:
                po = jnp.exp(so - mx)
                l = l + jnp.sum(po, axis=-1, keepdims=True)
                o = o + _dot(po.astype(BF16), v_ref[0, 0:j0, :])
            o_ref[j0:ext, :] = o / l
            lse_ref[0, j0:ext, :] = mx + jnp.log(l)

    return _hosted_call(
        body, name="attn_fwd", grid=(nb, HEADS),
        in_specs=[pl.BlockSpec((1, lp, D_QKP), lambda b, h: (h, b, 0)),
                  pl.BlockSpec((1, lp, D_QKP), lambda b, h: (h, b, 0)),
                  pl.BlockSpec((1, lp, D_V), lambda b, h: (h, b, 0))],
        out_specs=[pl.BlockSpec((lp, D_V), lambda b, h: (b, h)),
                   pl.BlockSpec((1, lp, 1), lambda b, h: (h, b, 0))],
        out_shape=[jax.ShapeDtypeStruct((rows, MLA_W), F32),
                   jax.ShapeDtypeStruct((HEADS, rows, 1), F32)],
        dims=("parallel", "parallel"), args=(q, k, v), comm=comm)


def _attn_bwd_call(q, k, v, o, lse, do, nb, lp, comm=None):
    rows = nb * lp
    qb = _row_tile(lp, Q_BLOCK_ROWS_BWD)
    scale = 1.0 / math.sqrt(D_QK)

    def body(q_ref, k_ref, v_ref, o_ref, lse_ref, do_ref, dq_ref, dk_ref, dv_ref, dk_acc, dv_acc):
        dk_acc[...] = jnp.zeros_like(dk_acc)
        dv_acc[...] = jnp.zeros_like(dv_acc)
        shift = CHUNK.bit_length() - 1
        for j in range(lp // qb):
            j0, ext = j * qb, (j + 1) * qb
            kend = _key_end(ext, lp)
            qj = q_ref[0, j0:ext, :]
            doj = do_ref[j0:ext, :]
            delta = jnp.sum(doj * o_ref[j0:ext, :], axis=-1, keepdims=True)
            dob = doj.astype(BF16)
            kk = k_ref[0, 0:kend, :]
            qchunk = jnp.right_shift(j0 + lax.broadcasted_iota(jnp.int32, (qb, kend), 0), shift)
            kchunk = jnp.right_shift(lax.broadcasted_iota(jnp.int32, (qb, kend), 1), shift)
            s = _dot_nt(qj, kk) * scale - lse_ref[0, j0:ext, :]
            p = jnp.where(kchunk <= qchunk, jnp.exp(s), 0.0)
            dv_acc[0:kend, :] += _dot_tn(p.astype(BF16), dob)
            dp = _dot_nt(dob, v_ref[0, 0:kend, :])
            ds = (p * (dp - delta) * scale).astype(BF16)
            dq_ref[0, j0:ext, :] = _dot(ds, kk).astype(BF16)
            dk_acc[0:kend, :] += _dot_tn(ds, qj)
        dk_ref[0] = dk_acc[...].astype(BF16)
        dv_ref[0] = dv_acc[...].astype(BF16)

    qspec = pl.BlockSpec((1, lp, D_QKP), lambda b, h: (h, b, 0))
    vspec = pl.BlockSpec((1, lp, D_V), lambda b, h: (h, b, 0))
    ospec = pl.BlockSpec((lp, D_V), lambda b, h: (b, h))
    return _hosted_call(
        body, name="attn_bwd", grid=(nb, HEADS),
        in_specs=[qspec, qspec, vspec, ospec, pl.BlockSpec((1, lp, 1), lambda b, h: (h, b, 0)), ospec],
        out_specs=[qspec, qspec, vspec],
        out_shape=[jax.ShapeDtypeStruct((HEADS, rows, D_QKP), BF16),
                   jax.ShapeDtypeStruct((HEADS, rows, D_QKP), BF16),
                   jax.ShapeDtypeStruct((HEADS, rows, D_V), BF16)],
        scratch_shapes=[pltpu.VMEM((lp, D_QKP), F32), pltpu.VMEM((lp, D_V), F32)],
        dims=("parallel", "parallel"), args=(q, k, v, o, lse, do), comm=comm)


def _lru_gates(u, cw, cb, wa, ba, wx, bx, lam, lp):
    xc = (cw[3:4, :] * u + cw[2:3, :] * pltpu.roll(u, 1, 0) + cw[1:2, :] * pltpu.roll(u, 2, 0)
          + cw[0:1, :] * pltpu.roll(u, 3, 0) + cb)
    xcb = xc.astype(BF16)
    r = _sigmoid(_dot(xcb, wa) + ba)
    i = _sigmoid(_dot(xcb, wx) + bx)
    sp = _softplus_neg(lam)
    la = -C_RGLRU * r * sp
    a = jnp.exp(la)
    x2 = 2.0 * la
    e2 = a * a
    m2 = jnp.maximum(jnp.where(x2 > -0.01, -x2 * (1.0 + 0.5 * x2), 1.0 - e2), 1e-30)
    rs = lax.rsqrt(m2)
    row = lax.broadcasted_iota(jnp.int32, (lp, LRU_TILE), 0)
    first = row == PAD_ROWS
    valid = row >= PAD_ROWS
    mult_eff = jnp.where(first, 1.0, m2 * rs)
    return dict(xc=xc, xcb=xcb, r=r, i=i, sp=sp, a=a, e2=e2, rs=rs, mult_eff=mult_eff, first=first, valid=valid)


def _scan_rows(a, b, a_s, b_s, out_ref, lp, reverse):
    sub = lax.broadcasted_iota(jnp.int32, (lp, LRU_TILE), 0) & 7
    for dist in (1, 2, 4):
        shift = lp - dist if reverse else dist
        keep = (sub + dist <= 7) if reverse else (sub >= dist)
        a_sh = pltpu.roll(a, shift, 0)
        b_sh = pltpu.roll(b, shift, 0)
        b = jnp.where(keep, a * b_sh + b, b)
        a = jnp.where(keep, a * a_sh, a)
    a_s[...] = a
    b_s[...] = b
    n_groups = lp // 8
    edge = 0 if reverse else 7

    def group(gi, carry):
        r0 = pl.multiple_of(((n_groups - 1 - gi) if reverse else gi) * 8, 8)
        a8 = a_s[pl.ds(r0, 8), :]
        b8 = b_s[pl.ds(r0, 8), :]
        out_ref[pl.ds(r0, 8), :] = a8 * carry + b8
        return a8[edge:edge + 1, :] * carry + b8[edge:edge + 1, :]

    lax.fori_loop(0, n_groups, group, jnp.zeros((1, LRU_TILE), F32), unroll=4)


def _lru_specs(lp):
    seq = lambda col0, stride=1: pl.BlockSpec((lp, LRU_TILE), lambda t, b: (b, col0 + stride * t))
    cw = pl.BlockSpec((1, CONV_K, LRU_TILE), lambda t, b: (t, 0, 0))
    vec = pl.BlockSpec((1, LRU_TILE), lambda t, b: (0, t))
    mat = pl.BlockSpec((1, LRU_TILE, LRU_TILE), lambda t, b: (t, 0, 0))
    return seq, cw, vec, mat


def _lru_fwd_call(z, cw, cb, wa, ba, wx, bx, lam, nb, lp, comm=None):
    rows = nb * lp
    seq, cwspec, vec, mat = _lru_specs(lp)

    def body(u_ref, g_ref, cw_ref, cb_ref, wa_ref, ba_ref, wx_ref, bx_ref, lam_ref, y_ref, hs_ref, a_s, b_s):
        m = _lru_gates(u_ref[...], cw_ref[0], cb_ref[...], wa_ref[0], ba_ref[...], wx_ref[0], bx_ref[...],
                       lam_ref[...], lp)
        a = jnp.where(m["valid"], m["a"], 0.0)
        b = jnp.where(m["valid"], m["mult_eff"] * (m["i"] * m["xc"]), 0.0)
        _scan_rows(a, b, a_s, b_s, hs_ref, lp, reverse=False)
        gl, _ = _gelu(g_ref[...])
        y_ref[...] = hs_ref[...] * gl

    oshape = jax.ShapeDtypeStruct((rows, LRU_W), F32)
    return _hosted_call(
        body, name="lru_fwd", grid=(N_LRU_TILES, nb),
        in_specs=[seq(Z_U // LRU_TILE, 2), seq(Z_U // LRU_TILE + 1, 2), cwspec, vec, mat, vec, mat, vec, vec],
        out_specs=[seq(0), seq(0)], out_shape=[oshape, oshape],
        scratch_shapes=[pltpu.VMEM((lp, LRU_TILE), F32), pltpu.VMEM((lp, LRU_TILE), F32)],
        dims=("parallel", "parallel"), args=(z, z, cw, cb, wa, ba, wx, bx, lam), comm=comm)


def _lru_bwd_call(z, dz, hs, dy, cw, cb, wa, ba, wx, bx, lam, nb, lp, comm=None):
    rows = nb * lp
    seq, cwspec, vec, mat = _lru_specs(lp)

    def body(u_ref, g_ref, hs_ref, dy_ref, cw_ref, cb_ref, wa_ref, ba_ref, wx_ref, bx_ref, lam_ref, dz_in,
             dz_ref, dcw_ref, dcb_ref, dwa_ref, dba_ref, dwx_ref, dbx_ref, dlam_ref, a_s, b_s, d_s):
        du_ref = dz_ref.at[:, 0:LRU_TILE]
        dg_ref = dz_ref.at[:, LRU_TILE:2 * LRU_TILE]
        b_idx = pl.program_id(1)
        u = u_ref[...]
        cw = cw_ref[0]
        wa, wx = wa_ref[0], wx_ref[0]
        lam = lam_ref[...]
        m = _lru_gates(u, cw, cb_ref[...], wa, ba_ref[...], wx, bx_ref[...], lam, lp)
        gate = g_ref[...]
        gl, th = _gelu(gate)
        dy = dy_ref[...]
        hs = hs_ref[...]
        dg_ref[...] = (dy * hs * _gelu_grad(gate, th)).astype(BF16)
        a_eff = jnp.where(m["valid"], m["a"], 0.0)
        _scan_rows(pltpu.roll(a_eff, lp - 1, 0), dy * gl, a_s, b_s, d_s, lp, reverse=True)
        ds = d_s[...]
        xc, r, i = m["xc"], m["r"], m["i"]
        row = lax.broadcasted_iota(jnp.int32, (lp, LRU_TILE), 0)
        da = ds * jnp.where(row >= 1, pltpu.roll(hs, 1, 0), 0.0)
        db = jnp.where(m["valid"], ds, 0.0)
        di = db * m["mult_eff"] * xc
        dxc = db * m["mult_eff"] * i
        live = m["valid"] & jnp.logical_not(m["first"])
        dm = jnp.where(live, db * i * xc, 0.0)
        dla = da * m["a"] - dm * (m["e2"] * m["rs"])
        dr = dla * (-C_RGLRU * m["sp"])
        dsp = jnp.sum(dla * (-C_RGLRU * r), axis=0, keepdims=True)
        dpr = (dr * r * (1.0 - r))
        dpi = (di * i * (1.0 - i))
        dprb, dpib = dpr.astype(BF16), dpi.astype(BF16)
        dxc = dxc + _dot_nt(dprb, wa) + _dot_nt(dpib, wx)
        du = (cw[3:4, :] * dxc + cw[2:3, :] * pltpu.roll(dxc, lp - 1, 0) + cw[1:2, :] * pltpu.roll(dxc, lp - 2, 0)
              + cw[0:1, :] * pltpu.roll(dxc, lp - 3, 0))
        du_ref[...] = jnp.where(m["valid"], du, 0.0).astype(BF16)
        tap = lax.broadcasted_iota(jnp.int32, (CONV_K, LRU_TILE), 0)
        dcw = jnp.zeros((CONV_K, LRU_TILE), F32)
        for kk in range(CONV_K):
            shifted = u if kk == CONV_K - 1 else pltpu.roll(u, CONV_K - 1 - kk, 0)
            dcw = jnp.where(tap == kk, jnp.sum(dxc * shifted, axis=0, keepdims=True), dcw)
        parts = [(dcw_ref, dcw[None]), (dcb_ref, jnp.sum(dxc, axis=0, keepdims=True)[None]),
                 (dwa_ref, _dot_tn(m["xcb"], dprb)[None]), (dba_ref, jnp.sum(dpr, axis=0, keepdims=True)[None]),
                 (dwx_ref, _dot_tn(m["xcb"], dpib)[None]), (dbx_ref, jnp.sum(dpi, axis=0, keepdims=True)[None]),
                 (dlam_ref, (dsp * (-jax.nn.sigmoid(-lam)))[None])]

        @pl.when(b_idx == 0)
        def _():
            for ref, val in parts:
                ref[...] = val

        @pl.when(b_idx != 0)
        def _():
            for ref, val in parts:
                ref[...] += val

    vec3 = pl.BlockSpec((1, 1, LRU_TILE), lambda t, b: (t, 0, 0))
    vshape = jax.ShapeDtypeStruct((N_LRU_TILES, 1, LRU_TILE), F32)
    mshape = jax.ShapeDtypeStruct((N_LRU_TILES, LRU_TILE, LRU_TILE), F32)
    pair = pl.BlockSpec((lp, 2 * LRU_TILE), lambda t, b: (b, Z_U // (2 * LRU_TILE) + t))
    return _hosted_call(
        body, name="lru_bwd", grid=(N_LRU_TILES, nb),
        in_specs=[seq(Z_U // LRU_TILE, 2), seq(Z_U // LRU_TILE + 1, 2), seq(0), seq(0), cwspec, vec, mat, vec, mat,
                  vec, vec, pl.BlockSpec(memory_space=pl.ANY)],
        out_specs=[pair, cwspec, vec3, mat, vec3, mat, vec3, vec3],
        out_shape=[jax.ShapeDtypeStruct(dz.shape, dz.dtype), jax.ShapeDtypeStruct((N_LRU_TILES, CONV_K, LRU_TILE), F32),
                   vshape, mshape, vshape, mshape, vshape, vshape],
        scratch_shapes=[pltpu.VMEM((lp, LRU_TILE), F32)] * 3,
        dims=("parallel", "arbitrary"), args=(z, z, hs, dy, cw, cb, wa, ba, wx, bx, lam, dz), comm=comm,
        aliases={11: 0})


def _mix_out_call(ya, yl, ga, gl, wout, h, next_gain, tm):
    rows, d = h.shape

    def body(ya_ref, yl_ref, ga_ref, gl_ref, w_ref, h_ref, ng_ref, y_ref, o_ref, u_ref):
        a = ya_ref[...]
        l = yl_ref[...]
        an = (a * _rms(a, MLA_W) * ga_ref[...]).astype(BF16)
        ln = (l * _rms(l, LRU_W) * gl_ref[...]).astype(BF16)
        y_ref[:, 0:MLA_W] = an
        y_ref[:, MLA_W:MLA_W + LRU_W] = ln
        out = h_ref[...] + _dot(an, w_ref[0:MLA_W, :]) + _dot(ln, w_ref[MLA_W:MLA_W + LRU_W, :])
        o_ref[...] = out
        u_ref[...] = (out * _rms(out, d) * ng_ref[...]).astype(BF16)

    half = pl.BlockSpec((tm, MLA_W), lambda i: (i, 0))
    g = pl.BlockSpec((1, MLA_W), lambda i: (0, 0))
    full = pl.BlockSpec((tm, d), lambda i: (i, 0))
    return pl.pallas_call(
        body, name="mix_out", grid=(rows // tm,),
        in_specs=[half, half, g, g, pl.BlockSpec((MLA_W + LRU_W, d), lambda i: (0, 0)), full,
                  pl.BlockSpec((1, d), lambda i: (0, 0))],
        out_specs=[full, full, full],
        out_shape=[jax.ShapeDtypeStruct((rows, MLA_W + LRU_W), BF16), jax.ShapeDtypeStruct((rows, d), F32),
                   jax.ShapeDtypeStruct((rows, d), BF16)],
        compiler_params=_params(("parallel",)))(ya, yl, ga, gl, wout, h, next_gain)


def _ffn_dact_call(name, dhb, wd, gate, up, tm, comm=None):
    rows, d = dhb.shape
    ns, fs, _ = wd.shape

    nsub = 2 if tm % 32 == 0 else 1
    sub = tm // nsub

    def body(dh_ref, wd_ref, g_ref, p_ref, dg_ref, dp_ref):
        wd = wd_ref[0]
        for r in range(nsub):
            rs = slice(r * sub, (r + 1) * sub)
            da = (0.5 * _dot_nt(dh_ref[rs, :], wd)).astype(BF16)
            g = g_ref[0, rs, :]
            p = p_ref[0, rs, :]
            sg = jax.nn.sigmoid(g)
            dg_ref[0, rs, :] = (da * p) * (sg * (1.0 + g * (1.0 - sg)))
            dp_ref[0, rs, :] = da * (g * sg)

    aspec = pl.BlockSpec((1, tm, fs), lambda s, i: (s, i, 0))
    oshape = jax.ShapeDtypeStruct((ns, rows, fs), BF16)
    return _hosted_call(
        body, name=name, grid=(ns, rows // tm),
        in_specs=[pl.BlockSpec((tm, d), lambda s, i: (i, 0)), pl.BlockSpec((1, fs, d), lambda s, i: (s, 0, 0)),
                  aspec, aspec],
        out_specs=[aspec, aspec], out_shape=[oshape, oshape],
        dims=("parallel", "parallel"), args=(dhb, wd, gate, up), comm=comm)


def _norm_in_bwd_call(name, pieces, h, g, dres, tm, comm=None, part=(0, 1), prev=None, mix=None):
    rows, d = h.shape
    npc = len(pieces)
    steps = rows // tm // part[1]
    off = part[0] * steps
    n_prev = 0 if prev is None else 2
    n_mix = 0 if mix is None else 5

    def body(*refs):
        d_refs = refs[0:2 * npc:2]
        w_refs = refs[1:2 * npc:2]
        h_ref, g_ref, dres_ref = refs[2 * npc:2 * npc + 3]
        mix_in_refs = refs[2 * npc + 3:2 * npc + 3 + n_mix]
        dh_ref, dhb_ref, dg_ref = refs[2 * npc + 3 + n_mix + n_prev:2 * npc + 6 + n_mix + n_prev]
        mix_out_refs = refs[2 * npc + 6 + n_mix + n_prev:]
        du = jnp.zeros((tm, d), F32)
        for d_ref, w_ref in zip(d_refs, w_refs):
            if len(d_ref.shape) == 3:
                for s in range(d_ref.shape[0]):
                    du = du + _dot(d_ref[s], w_ref[s])
            else:
                du = du + _dot(d_ref[...], w_ref[...])
        x = h_ref[...]
        r = _rms(x, d)
        n = x * r
        dh = dres_ref[...] + _rms_bwd(du * g_ref[...], n, r, d)
        dhb = dh.astype(BF16)
        dh_ref[...] = dh
        dhb_ref[...] = dhb
        sums = [(dg_ref, jnp.sum(du * n, axis=0, keepdims=True))]
        if mix is not None:
            wo_ref, ya_ref, yl_ref, ga_ref, gl_ref = mix_in_refs
            dya_ref, dyl_ref, dga_ref, dgl_ref = mix_out_refs
            dy = _dot_nt(dhb, wo_ref[...])
            for val, gain_ref, lo, out_ref, acc_ref in ((ya_ref[...], ga_ref, 0, dya_ref, dga_ref),
                                                        (yl_ref[...], gl_ref, MLA_W, dyl_ref, dgl_ref)):
                rb = _rms(val, MLA_W)
                nb_ = val * rb
                dyn = dy[:, lo:lo + MLA_W]
                out_ref[...] = _rms_bwd(dyn * gain_ref[...], nb_, rb, MLA_W)
                sums.append((acc_ref, jnp.sum(dyn * nb_, axis=0, keepdims=True)))

        @pl.when(pl.program_id(0) == 0)
        def _():
            for ref, val in sums:
                ref[...] = val

        @pl.when(pl.program_id(0) != 0)
        def _():
            for ref, val in sums:
                ref[...] += val

    in_specs, args = [], []
    for dd, w in pieces:
        if dd.ndim == 3:
            in_specs.append(pl.BlockSpec((dd.shape[0], tm, dd.shape[2]), lambda i: (0, i + off, 0)))
            in_specs.append(_resident(w.shape))
        else:
            in_specs.append(pl.BlockSpec((tm, dd.shape[1]), lambda i: (i + off, 0)))
            in_specs.append(_resident(w.shape))
        args += [dd, w]
    full = pl.BlockSpec((tm, d), lambda i: (i + off, 0))
    gspec = pl.BlockSpec((1, d), lambda i: (0, 0))
    half = pl.BlockSpec((tm, MLA_W), lambda i: (i + off, 0))
    hgain = pl.BlockSpec((1, MLA_W), lambda i: (0, 0))
    mix_in_specs = [] if mix is None else [_resident(mix[0].shape), half, half, hgain, hgain]
    mix_out_specs = [] if mix is None else [half, half, hgain, hgain]
    mix_out_shapes = [] if mix is None else [
        jax.ShapeDtypeStruct((rows, MLA_W), F32), jax.ShapeDtypeStruct((rows, LRU_W), F32),
        jax.ShapeDtypeStruct((1, MLA_W), F32), jax.ShapeDtypeStruct((1, LRU_W), F32)]
    n_in = len(in_specs) + 3 + n_mix
    return _hosted_call(
        body, name=name, grid=(steps,),
        in_specs=in_specs + [full, gspec, full] + mix_in_specs + _any_specs(n_prev),
        out_specs=[full, full, gspec] + mix_out_specs,
        out_shape=[jax.ShapeDtypeStruct((rows, d), F32), jax.ShapeDtypeStruct((rows, d), BF16),
                   jax.ShapeDtypeStruct((1, d), F32)] + mix_out_shapes,
        args=(*args, h, g, dres, *(mix or ()), *(prev or ())), comm=comm,
        aliases={n_in: 0, n_in + 1: 1} if prev is not None else {})


def _wgrad_call(name, a, b, scale=1.0, comm=None):
    a3, b3 = a.ndim == 3, b.ndim == 3
    ns = a.shape[0] if a3 else (b.shape[0] if b3 else 1)
    rows, m = a.shape[-2:]
    n = b.shape[-1]
    tmm = m if a3 else _col_tile(m, 256)

    def body(a_ref, b_ref, o_ref):
        av = a_ref[0] if a3 else a_ref[...]
        bv = b_ref[0] if b3 else b_ref[...]
        res = _dot_tn(av, bv)
        if scale != 1.0:
            res = res * scale
        if a3 or b3:
            o_ref[0] = res
        else:
            o_ref[...] = res

    aspec = (pl.BlockSpec((1, rows, tmm), lambda s, j: (s, 0, j)) if a3
             else pl.BlockSpec((rows, tmm), lambda s, j: (0, j)))
    bspec = (pl.BlockSpec((1, rows, n), lambda s, j: (s, 0, 0)) if b3
             else pl.BlockSpec((rows, n), lambda s, j: (0, 0)))
    if a3 or b3:
        ospec = pl.BlockSpec((1, tmm, n), lambda s, j: (s, j, 0))
        oshape = jax.ShapeDtypeStruct((ns, m, n), F32)
    else:
        ospec = pl.BlockSpec((tmm, n), lambda s, j: (j, 0))
        oshape = jax.ShapeDtypeStruct((m, n), F32)
    return _hosted_call(
        body, name=name, grid=(ns, m // tmm), in_specs=[aspec, bspec], out_specs=[ospec], out_shape=[oshape],
        dims=("parallel", "parallel"), args=(a, b), comm=comm)[0]


def _mla_prep_bwd_call(z, dq, dk, dv, gql, gkvl, gqh, gkh, wuq, wuk, wuv, tabs, lp, tm, comm=None):
    rows = z.shape[0]
    tpe = lp // tm

    def body(z_ref, dq_ref, dk_ref, dv_ref, gql_ref, gkvl_ref, gqh_ref, gkh_ref, wuq_ref, wuk_ref, wuv_ref,
             c_ref, s1_ref, s2_ref, dz_ref, dgql_ref, dgkvl_ref, dgqh_ref, dgkh_ref, dwuq_ref, dwuk_ref, dwuv_ref,
             dqp_ref, dkn_ref, dvv_ref):
        gql, gkvl = gql_ref[...], gkvl_ref[...]
        gq, gk = gqh_ref[...], gkh_ref[...]
        wuq, wuk, wuv = wuq_ref[...], wuk_ref[...], wuv_ref[...]
        m = _mla_heads(z_ref[...], gql, gkvl, wuq, wuk, wuv)
        c, s1, s2 = c_ref[...], s1_ref[...], s2_ref[...]
        dgq = jnp.zeros((1, D_QKP), F32)
        dgk = jnp.zeros((1, D_QKP), F32)
        dkr = jnp.zeros((tm, D_QKP - D_NOPE), F32)
        for hd in range(HEADS):
            qn, rqh, knn, krn, rkh = m["heads"][hd]
            dqg = jnp.concatenate([dq_ref[hd, :, 0:D_NOPE].astype(F32),
                                   _rope_t(dq_ref[hd, :, D_NOPE:D_QKP].astype(F32), c, s1, s2)], axis=1)
            dgq = dgq + jnp.sum(dqg * qn, axis=0, keepdims=True)
            dqn = dqg * gq
            dqr = rqh * (dqn - qn * (jnp.sum(dqn * qn, axis=-1, keepdims=True) * (1.0 / D_QK)))
            dqp_ref[:, hd * D_QKP:(hd + 1) * D_QKP] = dqr.astype(BF16)
            kn_full = jnp.concatenate([knn, krn], axis=1)
            dkg = jnp.concatenate([dk_ref[hd, :, 0:D_NOPE].astype(F32),
                                   _rope_t(dk_ref[hd, :, D_NOPE:D_QKP].astype(F32), c, s1, s2)], axis=1)
            dgk = dgk + jnp.sum(dkg * kn_full, axis=0, keepdims=True)
            dkn = dkg * gk
            dkraw = rkh * (dkn - kn_full * (jnp.sum(dkn * kn_full, axis=-1, keepdims=True) * (1.0 / D_QK)))
            dkn_ref[:, hd * D_NOPE:(hd + 1) * D_NOPE] = dkraw[:, 0:D_NOPE].astype(BF16)
            dkr = dkr + dkraw[:, D_NOPE:D_QKP]
            dvv_ref[:, hd * D_V:(hd + 1) * D_V] = dv_ref[hd]
        dcqn = _dot(dqp_ref[...], wuq)
        dckvn = _dot_nt(dkn_ref[...], wuk) + _dot_nt(dvv_ref[...], wuv)
        dz_ref[:, 0:Q_RANK] = _rms_bwd(dcqn * gql, m["nq"], m["rq"], Q_RANK).astype(BF16)
        dz_ref[:, Q_RANK:Z_KR] = _rms_bwd(dckvn * gkvl, m["nkv"], m["rkv"], KV_RANK).astype(BF16)
        dz_ref[:, Z_KR:Z_MLA] = dkr.astype(BF16)
        parts = [(dgql_ref, jnp.sum(dcqn * m["nq"], axis=0, keepdims=True)),
                 (dgkvl_ref, jnp.sum(dckvn * m["nkv"], axis=0, keepdims=True)), (dgqh_ref, dgq), (dgkh_ref, dgk),
                 (dwuq_ref, _dot_tn(dqp_ref[...], m["cqn"])), (dwuk_ref, _dot_tn(m["ckvn"], dkn_ref[...])),
                 (dwuv_ref, _dot_tn(m["ckvn"], dvv_ref[...]))]

        @pl.when(pl.program_id(0) == 0)
        def _():
            for ref, val in parts:
                ref[...] = val

        @pl.when(pl.program_id(0) != 0)
        def _():
            for ref, val in parts:
                ref[...] += val

    def const(shape):
        return pl.BlockSpec(shape, lambda i: tuple(0 for _ in shape))

    tab = pl.BlockSpec((tm, 128), lambda i: (i % tpe, 0))
    hq = pl.BlockSpec((HEADS, tm, D_QKP), lambda i: (0, i, 0))
    hv = pl.BlockSpec((HEADS, tm, D_V), lambda i: (0, i, 0))

    def rowspec(n):
        return pl.BlockSpec((tm, n), lambda i: (i, 0))

    return _hosted_call(
        body, name="mla_prep_bwd", grid=(rows // tm,),
        in_specs=[rowspec(Z_MLA), hq, hq, hv, const((1, Q_RANK)), const((1, KV_RANK)), const((1, D_QKP)),
                  const((1, D_QKP)), const((HEADS * D_QKP, Q_RANK)), const((KV_RANK, HEADS * D_NOPE)),
                  const((KV_RANK, HEADS * D_V)), tab, tab, tab],
        out_specs=[rowspec(Z_MLA), const((1, Q_RANK)), const((1, KV_RANK)), const((1, D_QKP)), const((1, D_QKP)),
                   const((HEADS * D_QKP, Q_RANK)), const((KV_RANK, HEADS * D_NOPE)), const((KV_RANK, HEADS * D_V))],
        out_shape=[jax.ShapeDtypeStruct((rows, Z_W), BF16),
                   jax.ShapeDtypeStruct((1, Q_RANK), F32), jax.ShapeDtypeStruct((1, KV_RANK), F32),
                   jax.ShapeDtypeStruct((1, D_QKP), F32), jax.ShapeDtypeStruct((1, D_QKP), F32),
                   jax.ShapeDtypeStruct((HEADS * D_QKP, Q_RANK), F32),
                   jax.ShapeDtypeStruct((KV_RANK, HEADS * D_NOPE), F32), jax.ShapeDtypeStruct((KV_RANK, HEADS * D_V), F32)],
        scratch_shapes=[pltpu.VMEM((tm, HEADS * D_QKP), BF16), pltpu.VMEM((tm, HEADS * D_NOPE), BF16),
                        pltpu.VMEM((tm, HEADS * D_V), BF16)],
        args=(z, dq, dk, dv, gql, gkvl, gqh, gkh, wuq, wuk, wuv, *tabs), comm=comm)


def _local_step(h0, target, w, nb, lp, sched=None):
    tm = _row_tile(nb * lp, 1408)
    te = _row_tile(lp, 512)
    tabs = _rope_tables(lp)
    g = {}
    if sched is None:
        host = lambda stage: None
    else:
        sched.g = g
        host = sched.host

    def ffn_act(tag, u, split):
        if split:
            gate = _ffn_gate_call(tag + "_gate", u, w[tag + "_w_gate"], tm, host(tag + "_gate"))
            up, act = _ffn_upact_call(tag + "_upact", u, w[tag + "_w_up"], gate, tm, host(tag + "_upact"))
        else:
            gate, up, act = _ffn_up_call(tag + "_up", u, w[tag + "_w_gate"], w[tag + "_w_up"], tm, host(tag + "_up"))
        return u, gate, up, act

    def ffn_bwd(tag, h, saved, dh, dhb, split, mix=None):
        u, gate, up, act = saved
        dgate, dup = _ffn_dact_call(tag + "_dact", dhb, w[tag + "_w_down"], gate, up, tm, host(tag + "_dact"))
        g[tag + "_w_down"] = _wgrad_call(tag + "_dwd", act, dhb, 0.5, host(tag + "_dwd"))
        g[tag + "_w_gate"] = _wgrad_call(tag + "_dwg", dgate, u, 1.0, host(tag + "_dwg"))
        g[tag + "_w_up"] = _wgrad_call(tag + "_dwu", dup, u, 1.0, host(tag + "_dwu"))
        pieces = [(dgate, w[tag + "_w_gate"]), (dup, w[tag + "_w_up"])]
        if not split:
            res = _norm_in_bwd_call(tag + "_din", pieces, h, w[tag + "_norm"], dh, te, host(tag + "_din"), mix=mix)
            g[tag + "_norm"] = res[2]
            return (res[0], res[1], *res[3:])
        dh_a, dhb_a, dg_a = _norm_in_bwd_call(tag + "_din_a", pieces, h, w[tag + "_norm"], dh, te,
                                              host(tag + "_din_a"), part=(0, 2))
        dh_in, dhb_in, dg_b = _norm_in_bwd_call(tag + "_din_b", pieces, h, w[tag + "_norm"], dh, te,
                                                host(tag + "_din_b"), part=(1, 2), prev=(dh_a, dhb_a))
        g[tag + "_norm"] = dg_a + dg_b
        return dh_in, dhb_in

    s1 = ffn_act("ffn1", _rmsnorm_call("ffn1_norm", h0, w["ffn1_norm"], te, host("ffn1_norm")), True)
    h1, un = _ffn_down_call("ffn1_down", s1[3], w["ffn1_w_down"], h0, te, host("ffn1_down"), w["mix_norm"])
    z = _mm_call("mix_in", un, w["w_in"], tm, F32)
    mla_w = (w["q_latent_norm"], w["kv_latent_norm"], w["q_head_norm"], w["k_head_norm"], w["w_uq"], w["w_uk"],
             w["w_uv"])
    q, k, v = _mla_prep_call(z, *mla_w, tabs, lp, te)
    o, lse = _attn_fwd_call(q, k, v, nb, lp, host("attn_fwd"))
    lru_w = (w["conv_w"], w["conv_b"], w["gate_a_w"], w["gate_a_b"], w["gate_x_w"], w["gate_x_b"], w["lru_lambda"])
    yl, hs = _lru_fwd_call(z, *lru_w, nb, lp, host("lru_fwd"))
    y, h2, u2 = _mix_out_call(o, yl, w["attn_out_norm"], w["lru_out_norm"], w["w_out"], h1, w["ffn2_norm"], te)
    s2 = ffn_act("ffn2", u2, False)
    dh3, dh3b, g["final_norm"], loss = _ffn_down_loss_call("ffn2_down", s2[3], w["ffn2_w_down"], h2, w["final_norm"],
                                                           target, lp, te)
    g["loss"] = loss

    dh2, dh2b, dya, dyl, g["attn_out_norm"], g["lru_out_norm"] = ffn_bwd(
        "ffn2", h2, s2, dh3, dh3b, False, (w["w_out"], o, yl, w["attn_out_norm"], w["lru_out_norm"]))
    g["w_out"] = _wgrad_call("dw_out", y, dh2b)
    dq, dk, dv = _attn_bwd_call(q, k, v, o, lse, dya, nb, lp, host("attn_bwd"))
    (dz_mla, g["q_latent_norm"], g["kv_latent_norm"], g["q_head_norm"], g["k_head_norm"], g["w_uq"], g["w_uk"],
     g["w_uv"]) = _mla_prep_bwd_call(z, dq, dk, dv, *mla_w, tabs, lp, te, host("mla_prep_bwd"))
    (dz, g["conv_w"], g["conv_b"], g["gate_a_w"], g["gate_a_b"], g["gate_x_w"], g["gate_x_b"],
     g["lru_lambda"]) = _lru_bwd_call(z, dz_mla, hs, dyl, *lru_w, nb, lp, host("lru_bwd"))
    g["w_in"] = _wgrad_call("dw_in", dz, un)
    dh1, dh1b, g["mix_norm"] = _norm_in_bwd_call("mix_din", [(dz, w["w_in"])], h1, w["mix_norm"], dh2, te,
                                                 host("mix_din"))
    dh0 = ffn_bwd("ffn1", h0, s1, dh1, dh1b, True)[0]
    return loss, dh0, g


def _place():
    x, y, c = lax.axis_index("x"), lax.axis_index("y"), lax.axis_index("c")
    return x, y, c, [(1 - x, y), (x, 1 - y), (1 - x, 1 - y)]


def _any_specs(n):
    return [pl.BlockSpec(memory_space=pl.ANY)] * n


def _remote(src, dst, sems, k, dev):
    send_sems, recv_sems, base = sems
    return pltpu.make_async_remote_copy(src_ref=src, dst_ref=dst, send_sem=send_sems.at[base + k],
                                        recv_sem=recv_sems.at[base + k], device_id=dev, device_id_type=MESH)


EW_VMEM_BYTES = 24 * 1024 * 1024


def _fit_rows(rows, cols, blocks):
    return _row_tile(rows, max(16, int(EW_VMEM_BYTES // (8 * blocks)) // cols))


class _Geom:
    def __init__(self, n0, n1, blocks=1.0):
        self.n0, self.n1 = n0, n1
        self.axis = 0 if n0 % 32 == 0 else 1
        self.h0, self.h1 = (n0 // 2, n1) if self.axis == 0 else (n0, n1 // 2)
        self.tr = _fit_rows(self.h0, self.h1, blocks)
        self.nblk = self.h0 // self.tr

    def half_ref(self, ref, lead, idx):
        if self.axis == 0:
            return ref.at[(*lead, pl.ds(idx * self.h0, self.h0))]
        return ref.at[(*lead, slice(None), pl.ds(idx * self.h1, self.h1))]

    def half_block(self, lead, i, idx):
        return (*lead, idx * self.nblk + i, 0) if self.axis == 0 else (*lead, i, idx)


class _Comm:
    def __init__(self, ins, out_shapes, aliases, n_sems, start, finish, deliver):
        self.ins, self.out_shapes, self.aliases, self.n_sems = list(ins), list(out_shapes), dict(aliases), n_sems
        self.start, self.finish, self.deliver = start, finish, deliver

    def scratch(self):
        return [pltpu.SemaphoreType.DMA((self.n_sems,)), pltpu.SemaphoreType.DMA((self.n_sems,))]


def _comm_call(name, comm):
    n_in = len(comm.ins)

    def body(*refs):
        ins, outs, sems = refs[:n_in], refs[n_in:-2], (*refs[-2:], 0)
        comm.start(ins, outs, sems)
        comm.finish(ins, outs, sems)

    res = pl.pallas_call(
        body, name=name, out_shape=comm.out_shapes, in_specs=_any_specs(n_in),
        out_specs=_any_specs(len(comm.out_shapes)), input_output_aliases=comm.aliases,
        scratch_shapes=comm.scratch())(*comm.ins)
    return comm.deliver(list(res))


def _hosted_call(body, *, name, grid, in_specs, out_specs, out_shape, args, scratch_shapes=(), dims=None, comm=None,
                 prefetch=None, aliases=None):
    aliases = dict(aliases or {})
    in_specs, out_specs, out_shape = list(in_specs), list(out_specs), list(out_shape)
    n_pre = 0 if prefetch is None else 1

    def call(fn, in_specs, out_specs, out_shape, scratch, aliases, dims, args):
        if prefetch is None:
            return pl.pallas_call(
                fn, name=name, grid=grid, in_specs=in_specs, out_specs=out_specs, out_shape=out_shape,
                scratch_shapes=scratch, input_output_aliases=aliases, compiler_params=_params(dims))(*args)
        spec = pltpu.PrefetchScalarGridSpec(num_scalar_prefetch=1, grid=grid, in_specs=in_specs, out_specs=out_specs,
                                            scratch_shapes=scratch)
        return pl.pallas_call(
            fn, name=name, grid_spec=spec, out_shape=out_shape,
            input_output_aliases={i + 1: o for i, o in aliases.items()}, compiler_params=_params(dims))(prefetch, *args)

    if comm is None:
        return list(call(body, in_specs, out_specs, out_shape, list(scratch_shapes), aliases,
                         dims or ("arbitrary",) * len(grid), args))
    n_in, n_out, n_ci, n_co = len(in_specs), len(out_specs), len(comm.ins), len(comm.out_shapes)

    def wrapped(*refs):
        pre, refs = refs[:n_pre], refs[n_pre:]
        ins, cins = refs[:n_in], refs[n_in:n_in + n_ci]
        outs = refs[n_in + n_ci:n_in + n_ci + n_out]
        couts = refs[n_in + n_ci + n_out:n_in + n_ci + n_out + n_co]
        scratch, sems = refs[n_in + n_ci + n_out + n_co:-2], (*refs[-2:], 0)
        first = functools.reduce(jnp.logical_and, [pl.program_id(k) == 0 for k in range(len(grid))])
        last = functools.reduce(jnp.logical_and, [pl.program_id(k) == grid[k] - 1 for k in range(len(grid))])

        @pl.when(first)
        def _():
            comm.start(cins, couts, sems)

        body(*pre, *ins, *outs, *scratch)

        @pl.when(last)
        def _():
            comm.finish(cins, couts, sems)

    res = call(wrapped, in_specs + _any_specs(n_ci), out_specs + _any_specs(n_co), out_shape + comm.out_shapes,
               list(scratch_shapes) + comm.scratch(),
               {**aliases, **{n_in + i: n_out + o for i, o in comm.aliases.items()}},
               ("arbitrary",) * len(grid), (*args, *comm.ins))
    comm.deliver(list(res[n_out:]))
    return list(res[:n_out])


def _gather_comm(bufs, deliver):
    n = len(bufs)
    geoms = [_Geom(*b.shape[1:]) for b in bufs]

    def first(outs, sems):
        x, y, c, chips = _place()
        cps = []
        for a in range(n):
            mine = geoms[a].half_ref(outs[a], (2 * x + y,), c)
            cps += [_remote(mine, mine, sems, 6 * a + j, (cx, cy, c)) for j, (cx, cy) in enumerate(chips)]
        return cps

    def start(ins, outs, sems):
        for cp in first(outs, sems):
            cp.start()

    def finish(ins, outs, sems):
        x, y, c, chips = _place()
        sib = (x, y, 1 - c)
        passed = []
        for a in range(n):
            for j, (cx, cy) in enumerate(chips):
                land = geoms[a].half_ref(outs[a], (2 * cx + cy,), c)
                _remote(land, land, sems, 6 * a + j, sib).wait_recv()
                cp = _remote(land, land, sems, 6 * a + 3 + j, sib)
                cp.start()
                passed.append(cp)
        for a in range(n):
            for j, (cx, cy) in enumerate(chips):
                land = geoms[a].half_ref(outs[a], (2 * cx + cy,), 1 - c)
                _remote(land, land, sems, 6 * a + 3 + j, sib).wait_recv()
        for cp in first(outs, sems) + passed:
            cp.wait_send()

    return _Comm(bufs, [jax.ShapeDtypeStruct(b.shape, b.dtype) for b in bufs], {a: a for a in range(n)}, 6 * n,
                 start, finish, deliver)


def _reduce_pair_comm(grads, deliver):
    n = len(grads)
    geoms = [_Geom(*a.shape[1:]) for a in grads]

    def copies(ins, outs, sems):
        x, y, c, _ = _place()
        return [_remote(geoms[a].half_ref(ins[a], (slice(None),), 1 - c), outs[a], sems, a, (x, y, 1 - c))
                for a in range(n)]

    def start(ins, outs, sems):
        for cp in copies(ins, outs, sems):
            cp.start()

    def finish(ins, outs, sems):
        cps = copies(ins, outs, sems)
        for cp in cps:
            cp.wait_recv()
        for cp in cps:
            cp.wait_send()

    shapes = [jax.ShapeDtypeStruct((N_SHARD, g.h0, g.h1), a.dtype) for a, g in zip(grads, geoms)]
    return _Comm(grads, shapes, {}, n, start, finish, deliver)


def _reduce_chips_comm(parts, deliver):
    n = len(parts)

    def copies(ins, outs, sems):
        x, y, c, chips = _place()
        return [_remote(ins[a].at[2 * cx + cy], outs[a].at[j], sems, 3 * a + j, (cx, cy, c))
                for a in range(n) for j, (cx, cy) in enumerate(chips)]

    def start(ins, outs, sems):
        for cp in copies(ins, outs, sems):
            cp.start()

    def finish(ins, outs, sems):
        cps = copies(ins, outs, sems)
        for cp in cps:
            cp.wait_recv()
        for cp in cps:
            cp.wait_send()

    shapes = [jax.ShapeDtypeStruct((3,) + a.shape[1:], a.dtype) for a in parts]
    return _Comm(parts, shapes, {}, 3 * n, start, finish, deliver)


def _share_pair_comm(bufs, deliver):
    n = len(bufs)
    geoms = [_Geom(*b.shape) for b in bufs]

    def copies(outs, sems):
        x, y, c, _ = _place()
        cps = []
        for a in range(n):
            mine = geoms[a].half_ref(outs[a], (), c)
            cps.append(_remote(mine, mine, sems, a, (x, y, 1 - c)))
        return cps

    def start(ins, outs, sems):
        for cp in copies(outs, sems):
            cp.start()

    def finish(ins, outs, sems):
        x, y, c, _ = _place()
        for a in range(n):
            land = geoms[a].half_ref(outs[a], (), 1 - c)
            _remote(land, land, sems, a, (x, y, 1 - c)).wait_recv()
        for cp in copies(outs, sems):
            cp.wait_send()

    return _Comm(bufs, [jax.ShapeDtypeStruct(b.shape, b.dtype) for b in bufs], {a: a for a in range(n)}, n,
                 start, finish, deliver)


def _small_comm(pack, deliver):
    r, d = pack.shape

    def copies(ins, outs, sems):
        x, y, c, _ = _place()
        cps = []
        for k in range(1, 8):
            peer = (x ^ ((k >> 2) & 1), y ^ ((k >> 1) & 1), c ^ (k & 1))
            cps.append(_remote(ins[0], outs[0].at[4 * x + 2 * y + c], sems, k - 1, peer))
        return cps

    def start(ins, outs, sems):
        for cp in copies(ins, outs, sems):
            cp.start()

    def finish(ins, outs, sems):
        cps = copies(ins, outs, sems)
        for cp in cps:
            cp.wait_recv()
        for cp in cps:
            cp.wait_send()

    return _Comm([pack], [jax.ShapeDtypeStruct((8, r, d), pack.dtype)], {}, 7, start, finish, deliver)


def _join_comms(comms):
    comms = [c for c in comms if c is not None]
    if len(comms) <= 1:
        return comms[0] if comms else None
    ins, out_shapes, aliases, spans, n_sems = [], [], {}, [], 0
    for c in comms:
        aliases.update({len(ins) + i: len(out_shapes) + o for i, o in c.aliases.items()})
        spans.append((len(ins), len(ins) + len(c.ins), len(out_shapes), len(out_shapes) + len(c.out_shapes), n_sems))
        ins += c.ins
        out_shapes += c.out_shapes
        n_sems += c.n_sems

    def run(which):
        def go(all_ins, all_outs, sems):
            for c, (i0, i1, o0, o1, base) in zip(comms, spans):
                getattr(c, which)(all_ins[i0:i1], all_outs[o0:o1], (sems[0], sems[1], sems[2] + base))
        return go

    def deliver(outs):
        for c, (_, _, o0, o1, _) in zip(comms, spans):
            c.deliver(outs[o0:o1])
        return outs

    return _Comm(ins, out_shapes, aliases, n_sems, run("start"), run("finish"), deliver)


def _ew_call(name, fn, ins, out_dtypes):
    shape = ins[0].shape
    cols = shape[-1]
    rows = 1
    for s_ in shape[:-1]:
        rows *= s_
    ins2 = [a.reshape(rows, cols) for a in ins]
    tr = rows
    for t in range(16, min(rows, max(16, (1 << 19) // cols)) + 1, 16):
        if rows % t == 0:
            tr = t
    no = len(out_dtypes)

    def body(*refs):
        outs = fn(*[r[...] for r in refs[:len(ins2)]])
        for ref, val in zip(refs[len(ins2):], outs):
            ref[...] = val.astype(ref.dtype)

    spec = pl.BlockSpec((tr, cols), lambda i: (i, 0))
    res = pl.pallas_call(
        body, name=name, grid=(rows // tr,), in_specs=[spec] * len(ins2), out_specs=[spec] * no,
        out_shape=[jax.ShapeDtypeStruct((rows, cols), dt) for dt in out_dtypes],
        compiler_params=_params(("parallel",)))(*ins2)
    return [r.reshape(shape) for r in res]


def _adamw_math(w, g, m, v):
    m = ADAM_B1 * m + (1.0 - ADAM_B1) * g
    v = ADAM_B2 * v + (1.0 - ADAM_B2) * (g * g)
    m_hat = m / (1.0 - ADAM_B1 ** ADAM_STEP)
    v_hat = v / (1.0 - ADAM_B2 ** ADAM_STEP)
    delta = -ADAM_LR * (m_hat / (jnp.sqrt(v_hat) + ADAM_EPS) + ADAM_WD * w)
    return delta, m, v


def _adamw_call(name, w, g, m, v):
    return _ew_call(name, _adamw_math, [w, g, m, v], [F32, F32, F32])


def _tiled_call(name, fn, place, grid, in_items, out_items, comm=None):
    ni = len(in_items)

    def body(place_ref, *refs):
        vals = fn(*[r[...] for r in refs[:ni]])
        for ref, val in zip(refs[ni:], vals):
            ref[...] = val.astype(ref.dtype)

    return _hosted_call(
        body, name=name, grid=grid, in_specs=[pl.BlockSpec(blk, imap) for _, blk, imap in in_items],
        out_specs=[pl.BlockSpec(blk, imap) for _, _, blk, imap in out_items],
        out_shape=[jax.ShapeDtypeStruct(shp, dt) for shp, dt, _, _ in out_items],
        args=[a for a, _, _ in in_items], prefetch=place, comm=comm)


def _cast_call(name, place, shards, comm=None):
    n0, n1 = shards[0].shape
    tr = _fit_rows(n0, n1, 1.5 * len(shards))
    ins = [(a, (tr, n1), lambda i, p: (i, 0)) for a in shards]
    outs = [((N_SHARD, n0, n1), BF16, (1, tr, n1), lambda i, p: (p[0], i, 0)) for _ in shards]
    return _tiled_call(name, lambda *v: [x[None] for x in v], place, (n0 // tr,), ins, outs, comm)


def _pair_sum_call(name, place, fulls, gots):
    k = len(fulls)
    g = _Geom(*fulls[0].shape[1:], blocks=2.5 * k)
    blk = (1, g.tr, g.h1)
    ins = [(a, blk, lambda s, i, p: g.half_block((s,), i, p[1])) for a in fulls]
    ins += [(a, blk, lambda s, i, p: (s, i, 0)) for a in gots]
    outs = [((N_SHARD, g.h0, g.h1), BF16, blk, lambda s, i, p: (s, i, 0)) for _ in fulls]
    return _tiled_call(name, lambda *v: [v[j] + v[k + j] for j in range(k)], place, (N_SHARD, g.nblk), ins, outs)


def _chip_sum_call(name, place, fulls, gots, recvs, comm=None):
    k = len(fulls)
    g = _Geom(*fulls[0].shape[1:], blocks=4.5 * k)
    blk = (1, g.tr, g.h1)
    ins = [(a, blk, lambda i, p: g.half_block((p[0],), i, p[1])) for a in fulls]
    ins += [(a, blk, lambda i, p: (p[0], i, 0)) for a in gots]
    ins += [(a, (3, g.tr, g.h1), lambda i, p: (0, i, 0)) for a in recvs]
    outs = [((g.n0, g.n1), F32, (g.tr, g.h1), lambda i, p: g.half_block((), i, p[1])) for _ in fulls]

    def fn(*v):
        res = []
        for j in range(k):
            r = v[2 * k + j].astype(F32)
            res.append(v[j][0] + v[k + j][0] + r[0] + r[1] + r[2])
        return res

    return _tiled_call(name, fn, place, (g.nblk,), ins, outs, comm)


def _adamw_group_call(name, ws, gs, ms, vs, comm=None):
    k = len(ws)
    n0, n1 = ws[0].shape
    tr = _fit_rows(n0, n1, 8 * k)
    spec = pl.BlockSpec((tr, n1), lambda i: (i, 0))

    def body(*refs):
        for j in range(k):
            g = refs[k + j][...]
            delta, m, vv = _adamw_math(refs[j][...], g, refs[2 * k + j][...], refs[3 * k + j][...])
            for ref, val in zip(refs[4 * k + 4 * j:4 * k + 4 * j + 4], (g, delta, m, vv)):
                ref[...] = val

    flat = _hosted_call(
        body, name=name, grid=(n0 // tr,), in_specs=[spec] * (4 * k), out_specs=[spec] * (4 * k),
        out_shape=[jax.ShapeDtypeStruct((n0, n1), F32)] * (4 * k), dims=("parallel",),
        args=(*ws, *gs, *ms, *vs), comm=comm)
    return [flat[4 * j:4 * j + 4] for j in range(k)]


def _small_update_call(me, early, own_early, late, own_late, wp, mp, vp):
    nd, r, d = early.shape

    def body(me_ref, e_ref, oe_ref, l_ref, ol_ref, w_ref, m_ref, v_ref, gs_ref, d_ref, nm_ref, nv_ref):
        mine = me_ref[0]

        def total(g_ref, own_ref):
            acc = None
            for k in range(nd):
                part = jnp.where(mine == k, own_ref[...], g_ref[k])
                acc = part if acc is None else acc + part
            return acc

        gs = total(e_ref, oe_ref)
        ls = total(l_ref, ol_ref)
        gs_ref[...] = gs
        first = gs[0:8] + ls[0:8]
        gs_ref[0:8, :] = first
        gs_ref[ROW_META:ROW_META + N_META, :] = gs[ROW_META:ROW_META + N_META] + ls[8:8 + N_META]
        grads = jnp.concatenate([first, gs[8:SMALL_ADAM_ROWS]], axis=0)
        delta, m, v = _adamw_math(w_ref[...], grads, m_ref[...], v_ref[...])
        d_ref[...] = delta
        nm_ref[...] = m
        nv_ref[...] = v

    vm = pl.BlockSpec(memory_space=pltpu.VMEM)
    ashape = jax.ShapeDtypeStruct((SMALL_ADAM_ROWS, d), F32)
    return pl.pallas_call(
        body, name="small_update", in_specs=[pl.BlockSpec(memory_space=pltpu.SMEM)] + [vm] * 7, out_specs=[vm] * 4,
        out_shape=[jax.ShapeDtypeStruct((r, d), F32), ashape, ashape, ashape],
        compiler_params=pltpu.CompilerParams(vmem_limit_bytes=VMEM_LIMIT_BYTES))(
            me, early, own_early, late, own_late, wp, mp, vp)


SMALL_NAMES = ["ffn1_norm", "mix_norm", "ffn2_norm", "final_norm", "q_latent_norm", "kv_latent_norm",
               "q_head_norm", "k_head_norm", "conv_b", "gate_a_b", "gate_x_b", "lru_lambda", "attn_out_norm",
               "lru_out_norm"]
ROW_CONV_W = 14
ROW_GATE_A = 16
ROW_GATE_X = 48
ROW_META = 80
ROW_LOSS = 96


def _row(a):
    flat = a.reshape(1, -1)
    return jnp.pad(flat, ((0, 0), (0, D_MODEL - flat.shape[1])))


def _pack_small(t, rows):
    parts = [_row(t[nm]) for nm in SMALL_NAMES]
    parts.append(t["conv_w"].reshape(2, D_MODEL))
    parts.append(t["gate_a_w"].reshape(32, D_MODEL))
    parts.append(t["gate_x_w"].reshape(32, D_MODEL))
    p = jnp.concatenate(parts, axis=0)
    return jnp.pad(p, ((0, rows - p.shape[0]), (0, 0)))


def _early_pack(g):
    gs = {nm: g.get(nm, jnp.zeros((1, D_MODEL), F32)) for nm in SMALL_NAMES}
    gs["q_head_norm"] = g["q_head_norm"][:, 0:D_QK]
    gs["k_head_norm"] = g["k_head_norm"][:, 0:D_QK]
    for nm in ("conv_b", "gate_a_b", "gate_x_b", "lru_lambda"):
        gs[nm] = g[nm].reshape(1, LRU_W)
    gs["conv_w"] = g["conv_w"].transpose(1, 0, 2).reshape(CONV_K, LRU_W)
    gs["gate_a_w"] = _gate_blocks(g["gate_a_w"])
    gs["gate_x_w"] = _gate_blocks(g["gate_x_w"])
    return jnp.concatenate([_pack_small(gs, ROW_META), jnp.zeros((N_META, D_MODEL), F32), _row(g["loss"][:, 0:1]),
                            jnp.zeros((SMALL_ROWS - ROW_LOSS - 1, D_MODEL), F32)], axis=0)


def _unpack_small(p, like):
    out = {}
    for k, nm in enumerate(SMALL_NAMES):
        out[nm] = p[k, 0:like[nm].size].reshape(like[nm].shape)
    out["gate_a_w"] = p[ROW_GATE_A:ROW_GATE_A + 32].reshape(like["gate_a_w"].shape)
    out["gate_x_w"] = p[ROW_GATE_X:ROW_GATE_X + 32].reshape(like["gate_x_w"].shape)
    return out


def _gate_dense(wg):
    w4 = wg[0].reshape(N_LRU_TILES, 2, 64, 64)
    zero = jnp.zeros((N_LRU_TILES, 64, 64), wg.dtype)
    top = jnp.concatenate([w4[:, 0], zero], axis=2)
    bot = jnp.concatenate([zero, w4[:, 1]], axis=2)
    return jnp.concatenate([top, bot], axis=1).astype(BF16)


def _gate_blocks(dw):
    return jnp.stack([dw[:, 0:64, 0:64], dw[:, 64:128, 64:128]], axis=1).reshape(8, 64, 64)


BIG_NAMES = ["ffn1_w_gate", "ffn1_w_up", "ffn1_w_down", "w_in", "w_uq", "w_uk", "w_uv", "w_out", "ffn2_w_gate",
             "ffn2_w_up", "ffn2_w_down"]
BIG_GROUPS = [["ffn1_w_gate", "ffn1_w_up", "ffn1_w_down", "ffn2_w_gate", "ffn2_w_up", "ffn2_w_down"], ["w_in"],
              ["w_uq"], ["w_uk", "w_uv"], ["w_out"]]
TRANSPOSED = ("ffn1_w_gate", "ffn1_w_up", "ffn2_w_gate", "ffn2_w_up", "w_in", "w_uq")


def _to2d(nm, a):
    return a[0].T if nm in TRANSPOSED else a[0]


def _from2d(nm, a):
    return (a.T if nm in TRANSPOSED else a)[None]


WEIGHT_NAMES = ["meta_tokens", "ffn1_norm", "ffn1_w_gate", "ffn1_w_up", "ffn1_w_down", "mix_norm", "w_in",
                "q_latent_norm", "w_uq", "kv_latent_norm", "w_uk", "w_uv", "q_head_norm", "k_head_norm", "conv_w",
                "conv_b", "gate_a_w", "gate_a_b", "gate_x_w", "gate_x_b", "lru_lambda", "attn_out_norm",
                "lru_out_norm", "w_out", "ffn2_norm", "ffn2_w_gate", "ffn2_w_up", "ffn2_w_down", "final_norm"]


def _weight_from(nm, slots):
    if nm == "w_in":
        win = slots.reshape(IN_WIDTH, D_MODEL)
        lru = win[Z_KR + D_ROPE:].reshape(2, N_LRU_TILES, LRU_TILE, D_MODEL).transpose(1, 0, 2, 3)
        return jnp.concatenate([win[0:Z_KR + D_ROPE], jnp.zeros((128 - D_ROPE, D_MODEL), BF16),
                                lru.reshape(2 * LRU_W, D_MODEL)], axis=0)
    if nm == "w_uq":
        return jnp.pad(slots, ((0, 0), (0, D_QKP - D_QK), (0, 0))).reshape(HEADS * D_QKP, Q_RANK)
    if nm in ("w_uk", "w_uv"):
        return slots.transpose(1, 0, 2).reshape(KV_RANK, HEADS * D_NOPE)
    if nm == "w_out":
        return slots.reshape(D_MODEL, D_MODEL)
    return slots


def _small_weights(p, small):
    w = {nm: p[nm] for nm in SMALL_NAMES}
    w["q_head_norm"] = jnp.pad(p["q_head_norm"], ((0, 0), (0, D_QKP - D_QK)))
    w["k_head_norm"] = jnp.pad(p["k_head_norm"], ((0, 0), (0, D_QKP - D_QK)))
    w["conv_w"] = small[:, N_META:N_META + 2, :].reshape(N_SHARD, CONV_K, LRU_TILE)
    w["gate_a_w"] = _gate_dense(p["gate_a_w"])
    w["gate_x_w"] = _gate_dense(p["gate_x_w"])
    meta = small[:, 0:N_META, :].transpose(1, 0, 2).reshape(N_META, D_MODEL)
    return w, meta


def _full_weights(p, gathered, small):
    w, meta = _small_weights(p, small)
    w.update({nm: _weight_from(nm, gathered[nm]) for nm in BIG_NAMES})
    return w, meta


def _shard_grad(nm, g):
    if nm == "w_in":
        lru = g[Z_MLA:].reshape(N_LRU_TILES, 2, LRU_TILE, D_MODEL).transpose(1, 0, 2, 3).reshape(2 * LRU_W, D_MODEL)
        return jnp.concatenate([g[0:Z_KR + D_ROPE], lru], axis=0).reshape(N_SHARD, IN_WIDTH // N_SHARD, D_MODEL)
    if nm == "w_uq":
        return g.reshape(HEADS, D_QKP, Q_RANK)[:, 0:D_QK, :]
    if nm in ("w_uk", "w_uv"):
        return g.reshape(KV_RANK, HEADS, D_NOPE).transpose(1, 0, 2)
    if nm == "w_out":
        return g.reshape(N_SHARD, D_MODEL // N_SHARD, D_MODEL)
    return g


def _shard_grads(g):
    return {nm: _shard_grad(nm, g[nm]) for nm in BIG_NAMES}


GATHER_AT = {"ffn1_norm": ["ffn1_w_gate"], "ffn1_gate": ["ffn1_w_up"], "ffn1_upact": ["ffn1_w_down"],
             "ffn1_down": ["w_in", "w_uq", "w_uk", "w_uv", "w_out"], "attn_fwd": ["ffn2_w_down", "ffn2_w_gate"],
             "lru_fwd": ["ffn2_w_up"]}
PAIR_AT = [("ffn2_din", ["ffn2_w_gate", "ffn2_w_up", "ffn2_w_down"]),
           ("mix_din", ["w_out", "w_uq", "w_uk", "w_uv", "w_in"]),
           ("ffn1_dwg", ["ffn1_w_down"]), ("ffn1_dwu", ["ffn1_w_gate"]), ("ffn1_din_a", ["ffn1_w_up"])]
CHIPS_AT = [("attn_bwd", ["ffn2_w_down", "ffn2_w_gate"]), ("mla_prep_bwd", ["ffn2_w_up"]),
            ("ffn1_dact", ["w_out", "w_uq", "w_uk", "w_uv", "w_in"]),
            ("ffn1_dwu", ["ffn1_w_down"]), ("ffn1_din_a", ["ffn1_w_gate"]), ("ffn1_din_b", ["ffn1_w_up"])]
SHARE_EARLY_GROUPS, SHARE_EARLY_AT = 3, "ffn1_dwd"
SMALL_EARLY_AT = "mix_din"


def _same_shape_groups(names):
    return [[nm for nm in grp if nm in names] for grp in BIG_GROUPS if any(nm in names for nm in grp)]


class _Sched:
    def __init__(self, place, w, slots):
        self.place, self.w, self.slots = place, w, slots
        self.g = None
        self.sharded, self.from_pair, self.chip_bf16, self.from_chips = {}, {}, {}, {}
        self.early = self.early_all = None
        self.shared = {}

    def host(self, stage):
        comms = []
        if stage in GATHER_AT:
            comms.append(self.gather(GATHER_AT[stage]))
        comms += [self.chips(names) for at, names in CHIPS_AT if at == stage]
        comms += [self.pair(names) for at, names in PAIR_AT if at == stage]
        if stage == SMALL_EARLY_AT:
            comms.append(self.small_early())
        if stage == SHARE_EARLY_AT:
            comms.append(self.share_early())
        return _join_comms(comms)

    def small_early(self):
        self.early = _early_pack(self.g)

        def deliver(outs):
            self.early_all = outs[0]
            return outs

        return _small_comm(self.early, deliver)

    def gather(self, names):
        def deliver(outs):
            self.w.update({nm: _weight_from(nm, o) for nm, o in zip(names, outs)})
            return outs

        return _gather_comm([self.slots[nm] for nm in names], deliver)

    def pair(self, names):
        self.sharded.update({nm: _shard_grad(nm, self.g[nm]) for nm in names})

        def deliver(outs):
            self.from_pair.update(zip(names, outs))
            for grp in _same_shape_groups(names):
                sums = _pair_sum_call("pair_sum_" + grp[0], self.place, [self.sharded[nm] for nm in grp],
                                      [self.from_pair[nm] for nm in grp])
                self.chip_bf16.update(zip(grp, sums))
            return outs

        return _reduce_pair_comm([self.sharded[nm] for nm in names], deliver)

    def chips(self, names):
        def deliver(outs):
            self.from_chips.update(zip(names, outs))
            return outs

        return _reduce_chips_comm([self.chip_bf16[nm] for nm in names], deliver)

    def chip_sums(self, names, comm=None):
        out = {}
        for grp in _same_shape_groups(names):
            sums = _chip_sum_call("chip_sum_" + grp[0], self.place, [self.sharded[nm] for nm in grp],
                                  [self.from_pair[nm] for nm in grp], [self.from_chips[nm] for nm in grp], comm)
            comm = None
            out.update(zip(grp, sums))
        return out

    def share_early(self):
        names = [nm for at, grp in CHIPS_AT[:SHARE_EARLY_GROUPS] for nm in grp]
        mine = self.chip_sums(names)
        return _share_pair_comm([mine[nm] for nm in names], lambda o: self.shared.update(zip(names, o)))


def kernel(x, meta_tokens, ffn1_norm, ffn1_w_gate, ffn1_w_up, ffn1_w_down, mix_norm, w_in, q_latent_norm, w_uq, kv_latent_norm, w_uk, w_uv, q_head_norm, k_head_norm, conv_w, conv_b, gate_a_w, gate_a_b, gate_x_w, gate_x_b, lru_lambda, attn_out_norm, lru_out_norm, w_out, ffn2_norm, ffn2_w_gate, ffn2_w_up, ffn2_w_down, final_norm, loss_target, m_meta_tokens, m_ffn1_norm, m_ffn1_w_gate, m_ffn1_w_up, m_ffn1_w_down, m_mix_norm, m_w_in, m_q_latent_norm, m_w_uq, m_kv_latent_norm, m_w_uk, m_w_uv, m_q_head_norm, m_k_head_norm, m_conv_w, m_conv_b, m_gate_a_w, m_gate_a_b, m_gate_x_w, m_gate_x_b, m_lru_lambda, m_attn_out_norm, m_lru_out_norm, m_w_out, m_ffn2_norm, m_ffn2_w_gate, m_ffn2_w_up, m_ffn2_w_down, m_final_norm, v_meta_tokens, v_ffn1_norm, v_ffn1_w_gate, v_ffn1_w_up, v_ffn1_w_down, v_mix_norm, v_w_in, v_q_latent_norm, v_w_uq, v_kv_latent_norm, v_w_uk, v_w_uv, v_q_head_norm, v_k_head_norm, v_conv_w, v_conv_b, v_gate_a_w, v_gate_a_b, v_gate_x_w, v_gate_x_b, v_lru_lambda, v_attn_out_norm, v_lru_out_norm, v_w_out, v_ffn2_norm, v_ffn2_w_gate, v_ffn2_w_up, v_ffn2_w_down, v_final_norm):
    args = locals()
    p = {nm: args[nm] for nm in WEIGHT_NAMES}
    mom = {nm: args["m_" + nm] for nm in WEIGHT_NAMES}
    var = {nm: args["v_" + nm] for nm in WEIGHT_NAMES}
    nb, seq, d = x.shape
    lp = CHUNK + seq
    xi, yi, ci = lax.axis_index("x"), lax.axis_index("y"), lax.axis_index("c")
    chip = 2 * xi + yi

    place = jnp.stack([chip, ci]).astype(jnp.int32)
    p2 = {nm: _to2d(nm, p[nm]) for nm in BIG_NAMES}
    m2 = {nm: _to2d(nm, mom[nm]) for nm in BIG_NAMES}
    v2 = {nm: _to2d(nm, var[nm]) for nm in BIG_NAMES}

    slots = {}
    small_shard = jnp.concatenate(
        [meta_tokens, conv_w[0].reshape(2, 2 * LRU_TILE), jnp.zeros((14, 2 * LRU_TILE), F32)], axis=0)
    small_slots = lax.dynamic_update_slice(jnp.zeros((N_SHARD,) + small_shard.shape, F32), small_shard[None],
                                           (chip, 0, 0))
    first = []
    comm = _gather_comm([small_slots], first.extend)
    for grp in BIG_GROUPS:
        for nm, buf in zip(grp, _cast_call("cast_" + grp[0], place, [p2[nm] for nm in grp], comm)):
            slots[nm] = buf
        comm = None
    w, meta = _small_weights(p, first[0])
    sched = _Sched(place, w, slots)

    h0 = jnp.concatenate(
        [jnp.zeros((nb, PAD_ROWS, d), F32), jnp.broadcast_to(meta[None], (nb, N_META, d)), x], axis=1)
    target = jnp.pad(loss_target, ((0, 0), (CHUNK, 0), (0, 0)))
    loss_part, dh0, g = _local_step(h0.reshape(nb * lp, d), target.reshape(nb * lp, d), w, nb, lp, sched)
    dh0 = dh0.reshape(nb, lp, d)
    grad_x = dh0[:, CHUNK:, :]

    late = jnp.concatenate([g["ffn1_norm"], g["mix_norm"], jnp.zeros((6, D_MODEL), F32),
                            jnp.sum(dh0[:, PAD_ROWS:CHUNK, :], axis=0)], axis=0)
    shared = sched.shared
    rest = [nm for at, names in CHIPS_AT[SHARE_EARLY_GROUPS:] if at is not None for nm in names]
    last = [nm for at, names in CHIPS_AT if at is None for nm in names]
    late_box = {}
    mine = sched.chip_sums(rest, _small_comm(late, lambda o: late_box.update(all=o[0])))
    share = _share_pair_comm([mine[nm] for nm in rest], lambda o: shared.update(zip(rest, o)))
    _comm_call("share_pair", _join_comms([share, sched.chips(last) if last else None]))
    if last:
        mine = sched.chip_sums(last)
        _comm_call("share_last", _share_pair_comm([mine[nm] for nm in last], lambda o: shared.update(zip(last, o))))
    late_all = late_box["all"]
    small_like = {nm: p[nm] for nm in SMALL_NAMES + ["gate_a_w", "gate_x_w"]}

    def pack_w(t):
        tt = {nm: t[nm] for nm in SMALL_NAMES + ["gate_a_w", "gate_x_w"]}
        tt["conv_w"] = jnp.zeros((CONV_K, LRU_W), F32)
        return _pack_small(tt, SMALL_ADAM_ROWS)

    me = (4 * xi + 2 * yi + ci).astype(jnp.int32).reshape(1)
    gsum, dsm, msm, vsm = _small_update_call(me, sched.early_all, sched.early, late_all, late, pack_w(p),
                                             pack_w(mom), pack_w(var))
    grads = _unpack_small(gsum, small_like)
    delta = _unpack_small(dsm, small_like)
    new_m = _unpack_small(msm, small_like)
    new_v = _unpack_small(vsm, small_like)
    loss = gsum[ROW_LOSS, 0]
    gmeta = gsum[ROW_META:ROW_META + N_META].reshape(N_META, N_SHARD, D_MODEL // N_SHARD)
    grads["meta_tokens"] = lax.dynamic_index_in_dim(gmeta, chip, axis=1, keepdims=False)
    gconv = gsum[ROW_CONV_W:ROW_CONV_W + 2].reshape(CONV_K, N_SHARD, LRU_TILE)
    grads["conv_w"] = lax.dynamic_index_in_dim(gconv, chip, axis=1, keepdims=False)[None]
    for nm in ("meta_tokens", "conv_w"):
        delta[nm], new_m[nm], new_v[nm] = _adamw_call("adamw_" + nm, p[nm], grads[nm], mom[nm], var[nm])

    for names in BIG_GROUPS:
        res = _adamw_group_call("adamw_" + names[0], [p2[nm] for nm in names], [shared[nm] for nm in names],
                                [m2[nm] for nm in names], [v2[nm] for nm in names])
        for nm, (gg, dd, mm, vv) in zip(names, res):
            grads[nm], delta[nm], new_m[nm], new_v[nm] = (_from2d(nm, t) for t in (gg, dd, mm, vv))

    return (loss, grad_x, *[grads[nm] for nm in WEIGHT_NAMES], *[delta[nm] for nm in WEIGHT_NAMES],
            *[new_m[nm] for nm in WEIGHT_NAMES], *[new_v[nm] for nm in WEIGHT_NAMES])
```

```python
import functools
import math

import jax
import jax.numpy as jnp
import numpy as np
from jax import lax
from jax.experimental import pallas as pl
from jax.experimental.pallas import tpu as pltpu

F32 = jnp.float32
BF16 = jnp.bfloat16
MESH = pl.DeviceIdType.MESH

D_MODEL = 1024
N_META = 16
CHUNK = 64
PAD_ROWS = CHUNK - N_META
HEADS = 4
D_NOPE = 128
D_ROPE = 64
D_QK = D_NOPE + D_ROPE
D_QKP = 256
D_V = 128
KV_RANK = 256
Q_RANK = 384
MLA_W = HEADS * D_V
LRU_W = 512
LRU_TILE = 128
N_LRU_TILES = LRU_W // LRU_TILE
CONV_K = 4
C_RGLRU = 8.0
ROPE_THETA = 10000.0
D_FF = 2816
N_SHARD = 4
EPS = 1e-6
NEG_INF = -1e30
Z_KR = Q_RANK + KV_RANK
Z_MLA = Z_KR + 128
Z_U = Z_MLA
Z_G = Z_U + LRU_W
Z_W = Z_G + LRU_W
IN_WIDTH = Q_RANK + KV_RANK + D_ROPE + 2 * LRU_W

ADAM_LR = 0.001
ADAM_B1 = 0.9
ADAM_B2 = 0.999
ADAM_EPS = 1e-08
ADAM_WD = 0.01
ADAM_STEP = 10

VMEM_LIMIT_BYTES = 56 * 1024 * 1024
SMALL_ROWS = 104
SMALL_ADAM_ROWS = 80


def _params(sem):
    return pltpu.CompilerParams(dimension_semantics=sem, vmem_limit_bytes=VMEM_LIMIT_BYTES)


def _resident(shape):
    return pl.BlockSpec(tuple(shape), lambda i: (0,) * len(shape), pipeline_mode=pl.Buffered(1))


def _row_tile(rows, target):
    best = 16
    for t in range(16, min(rows, target) + 1, 16):
        if rows % t == 0:
            best = t
    return best


def _col_tile(cols, target):
    best = cols
    for t in range(128, min(cols, target) + 1, 128):
        if cols % t == 0:
            best = t
    return best


def _dot(a, b):
    return jnp.dot(a, b, preferred_element_type=F32)


def _dot_nt(a, b):
    return lax.dot_general(a, b, (((1,), (1,)), ((), ())), preferred_element_type=F32)


def _dot_tn(a, b):
    return lax.dot_general(a, b, (((0,), (0,)), ((), ())), preferred_element_type=F32)


def _rms(x, n):
    return lax.rsqrt(jnp.sum(x * x, axis=-1, keepdims=True) * (1.0 / n) + EPS)


def _rms_bwd(dn, nrm, r, n):
    return r * (dn - nrm * (jnp.sum(dn * nrm, axis=-1, keepdims=True) * (1.0 / n)))


def _gelu(x):
    k = math.sqrt(2.0 / math.pi)
    t = jnp.tanh(k * (x + 0.044715 * x * x * x))
    return 0.5 * x * (1.0 + t), t


def _gelu_grad(x, t):
    k = math.sqrt(2.0 / math.pi)
    return 0.5 * (1.0 + t) + 0.5 * x * (1.0 - t * t) * k * (1.0 + 3.0 * 0.044715 * x * x)


def _sigmoid(x):
    return 0.5 + 0.5 * jnp.tanh(0.5 * x)


def _softplus_neg(lam):
    e = jnp.exp(-jnp.abs(lam))
    log1p = jnp.where(e < 0.01, e * (1.0 - e * (0.5 - e * (1.0 / 3 - e * 0.25))), jnp.log(1.0 + e))
    return jnp.maximum(-lam, 0.0) + log1p


def _rope(t, c, s1, s2):
    return t * c + pltpu.roll(t, 96, 1) * s1 + pltpu.roll(t, 32, 1) * s2


def _rope_t(d, c, s1, s2):
    return d * c + pltpu.roll(d * s1, 32, 1) + pltpu.roll(d * s2, 96, 1)


def _rope_tables(lp):
    pos = (np.arange(lp, dtype=np.int32) - PAD_ROWS).astype(np.float32)
    inv_freq = (ROPE_THETA ** (-np.arange(0, D_ROPE // 2, dtype=np.float32) / (D_ROPE // 2))).astype(np.float32)
    ang = (pos[:, None] * inv_freq[None, :]).astype(np.float32).astype(np.float64)
    cos, sin = np.cos(ang).astype(np.float32), np.sin(ang).astype(np.float32)
    z = np.zeros_like(cos)
    return (jnp.asarray(np.concatenate([cos, cos, z, z], 1)), jnp.asarray(np.concatenate([-sin, z, z, z], 1)),
            jnp.asarray(np.concatenate([z, sin, z, z], 1)))


def _rmsnorm_call(name, h, g, tm, comm=None):
    rows, d = h.shape

    def body(h_ref, g_ref, o_ref):
        x = h_ref[...]
        o_ref[...] = (x * _rms(x, d) * g_ref[...]).astype(BF16)

    return _hosted_call(
        body, name=name, grid=(rows // tm,),
        in_specs=[pl.BlockSpec((tm, d), lambda i: (i, 0)), pl.BlockSpec((1, d), lambda i: (0, 0))],
        out_specs=[pl.BlockSpec((tm, d), lambda i: (i, 0))],
        out_shape=[jax.ShapeDtypeStruct((rows, d), BF16)],
        dims=("parallel",), args=(h, g), comm=comm)[0]


def _ffn_up_call(name, u, wg, wu, tm, comm=None):
    rows, d = u.shape
    ns, fs, _ = wg.shape

    def body(u_ref, wg_ref, wu_ref, g_ref, p_ref, a_ref):
        uu = u_ref[...]
        g = _dot_nt(uu, wg_ref[0])
        p = _dot_nt(uu, wu_ref[0])
        g_ref[0] = g.astype(BF16)
        p_ref[0] = p.astype(BF16)
        a_ref[0] = (g * jax.nn.sigmoid(g) * p).astype(BF16)

    wspec = pl.BlockSpec((1, fs, d), lambda s, i: (s, 0, 0))
    ospec = pl.BlockSpec((1, tm, fs), lambda s, i: (s, i, 0))
    oshape = jax.ShapeDtypeStruct((ns, rows, fs), BF16)
    return _hosted_call(
        body, name=name, grid=(ns, rows // tm),
        in_specs=[pl.BlockSpec((tm, d), lambda s, i: (i, 0)), wspec, wspec],
        out_specs=[ospec, ospec, ospec], out_shape=[oshape, oshape, oshape],
        dims=("parallel", "parallel"), args=(u, wg, wu), comm=comm)


def _ffn_gate_call(name, u, wg, tm, comm=None):
    rows, d = u.shape
    ns, fs, _ = wg.shape

    def body(u_ref, wg_ref, g_ref):
        g_ref[0] = _dot_nt(u_ref[...], wg_ref[0]).astype(BF16)

    return _hosted_call(
        body, name=name, grid=(ns, rows // tm),
        in_specs=[pl.BlockSpec((tm, d), lambda s, i: (i, 0)), pl.BlockSpec((1, fs, d), lambda s, i: (s, 0, 0))],
        out_specs=[pl.BlockSpec((1, tm, fs), lambda s, i: (s, i, 0))],
        out_shape=[jax.ShapeDtypeStruct((ns, rows, fs), BF16)],
        dims=("parallel", "parallel"), args=(u, wg), comm=comm)[0]


def _ffn_upact_call(name, u, wu, gate, tm, comm=None):
    rows, d = u.shape
    ns, fs, _ = wu.shape

    def body(u_ref, wu_ref, g_ref, p_ref, a_ref):
        p = _dot_nt(u_ref[...], wu_ref[0])
        g = g_ref[0].astype(F32)
        p_ref[0] = p.astype(BF16)
        a_ref[0] = (g * jax.nn.sigmoid(g) * p).astype(BF16)

    ospec = pl.BlockSpec((1, tm, fs), lambda s, i: (s, i, 0))
    oshape = jax.ShapeDtypeStruct((ns, rows, fs), BF16)
    return _hosted_call(
        body, name=name, grid=(ns, rows // tm),
        in_specs=[pl.BlockSpec((tm, d), lambda s, i: (i, 0)), pl.BlockSpec((1, fs, d), lambda s, i: (s, 0, 0)), ospec],
        out_specs=[ospec, ospec], out_shape=[oshape, oshape],
        dims=("parallel", "parallel"), args=(u, wu, gate), comm=comm)


def _loss_tail(x, g, t, row, d):
    r = _rms(x, d)
    n = x * r
    err = jnp.where(row >= CHUNK, n * g - t, 0.0)
    dout = err * (1.0 / d)
    dh = _rms_bwd(dout * g, n, r, d)
    dg = jnp.sum(dout * n, axis=0, keepdims=True)
    part = jnp.sum(jnp.sum(err * err, axis=1, keepdims=True), axis=0, keepdims=True) * (0.5 / d)
    return dh, dg, jnp.broadcast_to(part, (1, 128))


def _ffn_down_call(name, a, wd, h, tm, comm=None, next_gain=None):
    rows, d = h.shape
    ns, _, fs = a.shape
    more = next_gain is not None

    def body(a_ref, wd_ref, h_ref, *rest):
        acc = h_ref[...]
        for s in range(ns):
            acc = acc + 0.5 * _dot(a_ref[s], wd_ref[s])
        rest[-2 if more else -1][...] = acc
        if more:
            rest[-1][...] = (acc * _rms(acc, d) * rest[0][...]).astype(BF16)

    full = pl.BlockSpec((tm, d), lambda i: (i, 0))
    res = _hosted_call(
        body, name=name, grid=(rows // tm,),
        in_specs=[pl.BlockSpec((ns, tm, fs), lambda i: (0, i, 0)), _resident((ns, fs, d)), full]
        + ([pl.BlockSpec((1, d), lambda i: (0, 0))] if more else []),
        out_specs=[full] * (2 if more else 1),
        out_shape=[jax.ShapeDtypeStruct((rows, d), F32)] + ([jax.ShapeDtypeStruct((rows, d), BF16)] if more else []),
        dims=("parallel",), args=(a, wd, h) + ((next_gain,) if more else ()), comm=comm)
    return res if more else res[0]


def _ffn_down_loss_call(name, a, wd, h, g, target, lp, tm):
    rows, d = h.shape
    ns, _, fs = a.shape
    tpe = lp // tm

    def body(a_ref, wd_ref, h_ref, g_ref, t_ref, dh_ref, dhb_ref, dg_ref, loss_ref):
        i = pl.program_id(0)
        acc = h_ref[...]
        for s in range(ns):
            acc = acc + 0.5 * _dot(a_ref[s], wd_ref[s])
        row = (i % tpe) * tm + lax.broadcasted_iota(jnp.int32, (tm, 1), 0)
        dh, dg, loss = _loss_tail(acc, g_ref[...], t_ref[...], row, d)
        dh_ref[...] = dh
        dhb_ref[...] = dh.astype(BF16)

        @pl.when(i == 0)
        def _():
            dg_ref[...] = dg
            loss_ref[...] = loss

        @pl.when(i != 0)
        def _():
            dg_ref[...] += dg
            loss_ref[...] += loss

    full = pl.BlockSpec((tm, d), lambda i: (i, 0))
    gspec = pl.BlockSpec((1, d), lambda i: (0, 0))
    return _hosted_call(
        body, name=name, grid=(rows // tm,),
        in_specs=[pl.BlockSpec((ns, tm, fs), lambda i: (0, i, 0)), _resident((ns, fs, d)), full, gspec, full],
        out_specs=[full, full, gspec, pl.BlockSpec((1, 128), lambda i: (0, 0))],
        out_shape=[jax.ShapeDtypeStruct((rows, d), F32), jax.ShapeDtypeStruct((rows, d), BF16),
                   jax.ShapeDtypeStruct((1, d), F32), jax.ShapeDtypeStruct((1, 128), F32)],
        args=(a, wd, h, g, target))


def _mm_call(name, a, bt, tm, out_dtype, comm=None):
    rows, k = a.shape
    n = bt.shape[0]

    def body(a_ref, b_ref, o_ref):
        o_ref[...] = _dot_nt(a_ref[...], b_ref[...]).astype(out_dtype)

    return _hosted_call(
        body, name=name, grid=(rows // tm,),
        in_specs=[pl.BlockSpec((tm, k), lambda i: (i, 0)), pl.BlockSpec((n, k), lambda i: (0, 0))],
        out_specs=[pl.BlockSpec((tm, n), lambda i: (i, 0))],
        out_shape=[jax.ShapeDtypeStruct((rows, n), out_dtype)],
        dims=("parallel",), args=(a, bt), comm=comm)[0]


def _mla_heads(z, gql, gkvl, wuq, wuk, wuv):
    cq = z[:, 0:Q_RANK]
    ckv = z[:, Q_RANK:Z_KR]
    kr = z[:, Z_KR:Z_MLA]
    rq = _rms(cq, Q_RANK)
    nq = cq * rq
    cqn = (nq * gql).astype(BF16)
    rkv = _rms(ckv, KV_RANK)
    nkv = ckv * rkv
    ckvn = (nkv * gkvl).astype(BF16)
    qraw = _dot_nt(cqn, wuq)
    knope = _dot(ckvn, wuk)
    v = _dot(ckvn, wuv)
    skr = jnp.sum(kr * kr, axis=-1, keepdims=True)
    heads = []
    for hd in range(HEADS):
        qh = qraw[:, hd * D_QKP:(hd + 1) * D_QKP]
        rqh = lax.rsqrt(jnp.sum(qh * qh, axis=-1, keepdims=True) * (1.0 / D_QK) + EPS)
        kn = knope[:, hd * D_NOPE:(hd + 1) * D_NOPE]
        rkh = lax.rsqrt((jnp.sum(kn * kn, axis=-1, keepdims=True) + skr) * (1.0 / D_QK) + EPS)
        heads.append((qh * rqh, rqh, kn * rkh, kr * rkh, rkh))
    return dict(rq=rq, nq=nq, cqn=cqn, rkv=rkv, nkv=nkv, ckvn=ckvn, v=v, heads=heads)


def _mla_prep_call(z, gql, gkvl, gqh, gkh, wuq, wuk, wuv, tabs, lp, tm):
    rows = z.shape[0]
    tpe = lp // tm

    def body(z_ref, gql_ref, gkvl_ref, gqh_ref, gkh_ref, wuq_ref, wuk_ref, wuv_ref, c_ref, s1_ref, s2_ref,
             q_ref, k_ref, v_ref):
        m = _mla_heads(z_ref[...], gql_ref[...], gkvl_ref[...], wuq_ref[...], wuk_ref[...], wuv_ref[...])
        c, s1, s2 = c_ref[...], s1_ref[...], s2_ref[...]
        gq, gk = gqh_ref[...], gkh_ref[...]
        row = (pl.program_id(0) % tpe) * tm + lax.broadcasted_iota(jnp.int32, (tm, 1), 0)
        spare = (lax.broadcasted_iota(jnp.int32, (1, D_QKP - D_NOPE), 1) == D_ROPE).astype(F32)
        kmask = jnp.where(row < PAD_ROWS, NEG_INF * math.sqrt(D_QK), 0.0) * spare
        for hd in range(HEADS):
            qn, _, knn, krn, _ = m["heads"][hd]
            qg = qn * gq
            q_ref[hd, :, 0:D_NOPE] = qg[:, 0:D_NOPE].astype(BF16)
            q_ref[hd, :, D_NOPE:D_QKP] = (_rope(qg[:, D_NOPE:D_QKP], c, s1, s2) + spare).astype(BF16)
            k_ref[hd, :, 0:D_NOPE] = (knn * gk[:, 0:D_NOPE]).astype(BF16)
            k_ref[hd, :, D_NOPE:D_QKP] = (_rope(krn * gk[:, D_NOPE:D_QKP], c, s1, s2) + kmask).astype(BF16)
            v_ref[hd] = m["v"][:, hd * D_V:(hd + 1) * D_V].astype(BF16)

    def const(shape):
        return pl.BlockSpec(shape, lambda i: tuple(0 for _ in shape))

    tab = pl.BlockSpec((tm, 128), lambda i: (i % tpe, 0))
    return pl.pallas_call(
        body, name="mla_prep", grid=(rows // tm,),
        in_specs=[pl.BlockSpec((tm, Z_MLA), lambda i: (i, 0)), const((1, Q_RANK)), const((1, KV_RANK)),
                  const((1, D_QKP)), const((1, D_QKP)), const((HEADS * D_QKP, Q_RANK)),
                  const((KV_RANK, HEADS * D_NOPE)), const((KV_RANK, HEADS * D_V)), tab, tab, tab],
        out_specs=[pl.BlockSpec((HEADS, tm, D_QKP), lambda i: (0, i, 0)),
                   pl.BlockSpec((HEADS, tm, D_QKP), lambda i: (0, i, 0)),
                   pl.BlockSpec((HEADS, tm, D_V), lambda i: (0, i, 0))],
        out_shape=[jax.ShapeDtypeStruct((HEADS, rows, D_QKP), BF16),
                   jax.ShapeDtypeStruct((HEADS, rows, D_QKP), BF16),
                   jax.ShapeDtypeStruct((HEADS, rows, D_V), BF16)],
        compiler_params=_params(("parallel",)))(z, gql, gkvl, gqh, gkh, wuq, wuk, wuv, *tabs)


Q_BLOCK_ROWS = 528
Q_BLOCK_ROWS_BWD = 192


def _q_block(lp):
    return _row_tile(lp, Q_BLOCK_ROWS)


def _key_end(ext, lp):
    return min(lp, -(-ext // CHUNK) * CHUNK)


def _diag_bias(qb, j0, nk):
    shift = CHUNK.bit_length() - 1
    r = jnp.right_shift(j0 + lax.broadcasted_iota(jnp.int32, (qb, nk), 0), shift)
    c = jnp.right_shift(j0 + lax.broadcasted_iota(jnp.int32, (qb, nk), 1), shift)
    return jnp.where(c <= r, 0.0, NEG_INF)


def _attn_fwd_call(q, k, v, nb, lp, comm=None):
    rows = nb * lp
    qb = _q_block(lp)
    scale = 1.0 / math.sqrt(D_QK)

    def body(q_ref, k_ref, v_ref, o_ref, lse_ref):
        for j in range(lp // qb):
            j0, ext = j * qb, (j + 1) * qb
            kend = _key_end(ext, lp)
            qj = q_ref[0, j0:ext, :]
            sd = _dot_nt(qj, k_ref[0, j0:kend, :]) * scale + _diag_bias(qb, j0, kend - j0)
            mx = jnp.max(sd, axis=-1, keepdims=True)
            if j > 0:
                so = _dot_nt(qj, k_ref[0, 0:j0, :]) * scale
                mx = jnp.maximum(mx, jnp.max(so, axis=-1, keepdims=True))
            pd = jnp.exp(sd - mx)
            l = jnp.sum(pd, axis=-1, keepdims=True)
            o = _dot(pd.astype(BF16), v_ref[0, j0:kend, :])
            if j > 0:
                po = jnp.exp(so - mx)
                l = l + jnp.sum(po, axis=-1, keepdims=True)
                o = o + _dot(po.astype(BF16), v_ref[0, 0:j0, :])
            o_ref[j0:ext, :] = o / l
            lse_ref[0, j0:ext, :] = mx + jnp.log(l)

    return _hosted_call(
        body, name="attn_fwd", grid=(nb, HEADS),
        in_specs=[pl.BlockSpec((1, lp, D_QKP), lambda b, h: (h, b, 0)),
                  pl.BlockSpec((1, lp, D_QKP), lambda b, h: (h, b, 0)),
                  pl.BlockSpec((1, lp, D_V), lambda b, h: (h, b, 0))],
        out_specs=[pl.BlockSpec((lp, D_V), lambda b, h: (b, h)),
                   pl.BlockSpec((1, lp, 1), lambda b, h: (h, b, 0))],
        out_shape=[jax.ShapeDtypeStruct((rows, MLA_W), F32),
                   jax.ShapeDtypeStruct((HEADS, rows, 1), F32)],
        dims=("parallel", "parallel"), args=(q, k, v), comm=comm)


def _attn_bwd_call(q, k, v, o, lse, do, nb, lp, comm=None):
    rows = nb * lp
    qb = _row_tile(lp, Q_BLOCK_ROWS_BWD)
    scale = 1.0 / math.sqrt(D_QK)

    def body(q_ref, k_ref, v_ref, o_ref, lse_ref, do_ref, dq_ref, dk_ref, dv_ref, dk_acc, dv_acc):
        dk_acc[...] = jnp.zeros_like(dk_acc)
        dv_acc[...] = jnp.zeros_like(dv_acc)
        shift = CHUNK.bit_length() - 1
        for j in range(lp // qb):
            j0, ext = j * qb, (j + 1) * qb
            kend = _key_end(ext, lp)
            qj = q_ref[0, j0:ext, :]
            doj = do_ref[j0:ext, :]
            delta = jnp.sum(doj * o_ref[j0:ext, :], axis=-1, keepdims=True)
            dob = doj.astype(BF16)
            kk = k_ref[0, 0:kend, :]
            qchunk = jnp.right_shift(j0 + lax.broadcasted_iota(jnp.int32, (qb, kend), 0), shift)
            kchunk = jnp.right_shift(lax.broadcasted_iota(jnp.int32, (qb, kend), 1), shift)
            s = _dot_nt(qj, kk) * scale - lse_ref[0, j0:ext, :]
            p = jnp.where(kchunk <= qchunk, jnp.exp(s), 0.0)
            dv_acc[0:kend, :] += _dot_tn(p.astype(BF16), dob)
            dp = _dot_nt(dob, v_ref[0, 0:kend, :])
            ds = (p * (dp - delta) * scale).astype(BF16)
            dq_ref[0, j0:ext, :] = _dot(ds, kk).astype(BF16)
            dk_acc[0:kend, :] += _dot_tn(ds, qj)
        dk_ref[0] = dk_acc[...].astype(BF16)
        dv_ref[0] = dv_acc[...].astype(BF16)

    qspec = pl.BlockSpec((1, lp, D_QKP), lambda b, h: (h, b, 0))
    vspec = pl.BlockSpec((1, lp, D_V), lambda b, h: (h, b, 0))
    ospec = pl.BlockSpec((lp, D_V), lambda b, h: (b, h))
    return _hosted_call(
        body, name="attn_bwd", grid=(nb, HEADS),
        in_specs=[qspec, qspec, vspec, ospec, pl.BlockSpec((1, lp, 1), lambda b, h: (h, b, 0)), ospec],
        out_specs=[qspec, qspec, vspec],
        out_shape=[jax.ShapeDtypeStruct((HEADS, rows, D_QKP), BF16),
                   jax.ShapeDtypeStruct((HEADS, rows, D_QKP), BF16),
                   jax.ShapeDtypeStruct((HEADS, rows, D_V), BF16)],
        scratch_shapes=[pltpu.VMEM((lp, D_QKP), F32), pltpu.VMEM((lp, D_V), F32)],
        dims=("parallel", "parallel"), args=(q, k, v, o, lse, do), comm=comm)


def _lru_gates(u, cw, cb, wa, ba, wx, bx, lam, lp):
    xc = (cw[3:4, :] * u + cw[2:3, :] * pltpu.roll(u, 1, 0) + cw[1:2, :] * pltpu.roll(u, 2, 0)
          + cw[0:1, :] * pltpu.roll(u, 3, 0) + cb)
    xcb = xc.astype(BF16)
    r = _sigmoid(_dot(xcb, wa) + ba)
    i = _sigmoid(_dot(xcb, wx) + bx)
    sp = _softplus_neg(lam)
    la = -C_RGLRU * r * sp
    a = jnp.exp(la)
    x2 = 2.0 * la
    e2 = a * a
    m2 = jnp.maximum(jnp.where(x2 > -0.01, -x2 * (1.0 + 0.5 * x2), 1.0 - e2), 1e-30)
    rs = lax.rsqrt(m2)
    row = lax.broadcasted_iota(jnp.int32, (lp, LRU_TILE), 0)
    first = row == PAD_ROWS
    valid = row >= PAD_ROWS
    mult_eff = jnp.where(first, 1.0, m2 * rs)
    return dict(xc=xc, xcb=xcb, r=r, i=i, sp=sp, a=a, e2=e2, rs=rs, mult_eff=mult_eff, first=first, valid=valid)


def _scan_rows(a, b, a_s, b_s, out_ref, lp, reverse):
    sub = lax.broadcasted_iota(jnp.int32, (lp, LRU_TILE), 0) & 7
    for dist in (1, 2, 4):
        shift = lp - dist if reverse else dist
        keep = (sub + dist <= 7) if reverse else (sub >= dist)
        a_sh = pltpu.roll(a, shift, 0)
        b_sh = pltpu.roll(b, shift, 0)
        b = jnp.where(keep, a * b_sh + b, b)
        a = jnp.where(keep, a * a_sh, a)
    a_s[...] = a
    b_s[...] = b
    n_groups = lp // 8
    edge = 0 if reverse else 7

    def group(gi, carry):
        r0 = pl.multiple_of(((n_groups - 1 - gi) if reverse else gi) * 8, 8)
        a8 = a_s[pl.ds(r0, 8), :]
        b8 = b_s[pl.ds(r0, 8), :]
        out_ref[pl.ds(r0, 8), :] = a8 * carry + b8
        return a8[edge:edge + 1, :] * carry + b8[edge:edge + 1, :]

    lax.fori_loop(0, n_groups, group, jnp.zeros((1, LRU_TILE), F32), unroll=4)


def _lru_specs(lp):
    seq = lambda col0, stride=1: pl.BlockSpec((lp, LRU_TILE), lambda t, b: (b, col0 + stride * t))
    cw = pl.BlockSpec((1, CONV_K, LRU_TILE), lambda t, b: (t, 0, 0))
    vec = pl.BlockSpec((1, LRU_TILE), lambda t, b: (0, t))
    mat = pl.BlockSpec((1, LRU_TILE, LRU_TILE), lambda t, b: (t, 0, 0))
    return seq, cw, vec, mat


def _lru_fwd_call(z, cw, cb, wa, ba, wx, bx, lam, nb, lp, comm=None):
    rows = nb * lp
    seq, cwspec, vec, mat = _lru_specs(lp)

    def body(u_ref, g_ref, cw_ref, cb_ref, wa_ref, ba_ref, wx_ref, bx_ref, lam_ref, y_ref, hs_ref, a_s, b_s):
        m = _lru_gates(u_ref[...], cw_ref[0], cb_ref[...], wa_ref[0], ba_ref[...], wx_ref[0], bx_ref[...],
                       lam_ref[...], lp)
        a = jnp.where(m["valid"], m["a"], 0.0)
        b = jnp.where(m["valid"], m["mult_eff"] * (m["i"] * m["xc"]), 0.0)
        _scan_rows(a, b, a_s, b_s, hs_ref, lp, reverse=False)
        gl, _ = _gelu(g_ref[...])
        y_ref[...] = hs_ref[...] * gl

    oshape = jax.ShapeDtypeStruct((rows, LRU_W), F32)
    return _hosted_call(
        body, name="lru_fwd", grid=(N_LRU_TILES, nb),
        in_specs=[seq(Z_U // LRU_TILE, 2), seq(Z_U // LRU_TILE + 1, 2), cwspec, vec, mat, vec, mat, vec, vec],
        out_specs=[seq(0), seq(0)], out_shape=[oshape, oshape],
        scratch_shapes=[pltpu.VMEM((lp, LRU_TILE), F32), pltpu.VMEM((lp, LRU_TILE), F32)],
        dims=("parallel", "parallel"), args=(z, z, cw, cb, wa, ba, wx, bx, lam), comm=comm)


def _lru_bwd_call(z, dz, hs, dy, cw, cb, wa, ba, wx, bx, lam, nb, lp, comm=None):
    rows = nb * lp
    seq, cwspec, vec, mat = _lru_specs(lp)

    def body(u_ref, g_ref, hs_ref, dy_ref, cw_ref, cb_ref, wa_ref, ba_ref, wx_ref, bx_ref, lam_ref, dz_in,
             dz_ref, dcw_ref, dcb_ref, dwa_ref, dba_ref, dwx_ref, dbx_ref, dlam_ref, a_s, b_s, d_s):
        du_ref = dz_ref.at[:, 0:LRU_TILE]
        dg_ref = dz_ref.at[:, LRU_TILE:2 * LRU_TILE]
        b_idx = pl.program_id(1)
        u = u_ref[...]
        cw = cw_ref[0]
        wa, wx = wa_ref[0], wx_ref[0]
        lam = lam_ref[...]
        m = _lru_gates(u, cw, cb_ref[...], wa, ba_ref[...], wx, bx_ref[...], lam, lp)
        gate = g_ref[...]
        gl, th = _gelu(gate)
        dy = dy_ref[...]
        hs = hs_ref[...]
        dg_ref[...] = (dy * hs * _gelu_grad(gate, th)).astype(BF16)
        a_eff = jnp.where(m["valid"], m["a"], 0.0)
        _scan_rows(pltpu.roll(a_eff, lp - 1, 0), dy * gl, a_s, b_s, d_s, lp, reverse=True)
        ds = d_s[...]
        xc, r, i = m["xc"], m["r"], m["i"]
        row = lax.broadcasted_iota(jnp.int32, (lp, LRU_TILE), 0)
        da = ds * jnp.where(row >= 1, pltpu.roll(hs, 1, 0), 0.0)
        db = jnp.where(m["valid"], ds, 0.0)
        di = db * m["mult_eff"] * xc
        dxc = db * m["mult_eff"] * i
        live = m["valid"] & jnp.logical_not(m["first"])
        dm = jnp.where(live, db * i * xc, 0.0)
        dla = da * m["a"] - dm * (m["e2"] * m["rs"])
        dr = dla * (-C_RGLRU * m["sp"])
        dsp = jnp.sum(dla * (-C_RGLRU * r), axis=0, keepdims=True)
        dpr = (dr * r * (1.0 - r))
        dpi = (di * i * (1.0 - i))
        dprb, dpib = dpr.astype(BF16), dpi.astype(BF16)
        dxc = dxc + _dot_nt(dprb, wa) + _dot_nt(dpib, wx)
        du = (cw[3:4, :] * dxc + cw[2:3, :] * pltpu.roll(dxc, lp - 1, 0) + cw[1:2, :] * pltpu.roll(dxc, lp - 2, 0)
              + cw[0:1, :] * pltpu.roll(dxc, lp - 3, 0))
        du_ref[...] = jnp.where(m["valid"], du, 0.0).astype(BF16)
        tap = lax.broadcasted_iota(jnp.int32, (CONV_K, LRU_TILE), 0)
        dcw = jnp.zeros((CONV_K, LRU_TILE), F32)
        for kk in range(CONV_K):
            shifted = u if kk == CONV_K - 1 else pltpu.roll(u, CONV_K - 1 - kk, 0)
            dcw = jnp.where(tap == kk, jnp.sum(dxc * shifted, axis=0, keepdims=True), dcw)
        parts = [(dcw_ref, dcw[None]), (dcb_ref, jnp.sum(dxc, axis=0, keepdims=True)[None]),
                 (dwa_ref, _dot_tn(m["xcb"], dprb)[None]), (dba_ref, jnp.sum(dpr, axis=0, keepdims=True)[None]),
                 (dwx_ref, _dot_tn(m["xcb"], dpib)[None]), (dbx_ref, jnp.sum(dpi, axis=0, keepdims=True)[None]),
                 (dlam_ref, (dsp * (-jax.nn.sigmoid(-lam)))[None])]

        @pl.when(b_idx == 0)
        def _():
            for ref, val in parts:
                ref[...] = val

        @pl.when(b_idx != 0)
        def _():
            for ref, val in parts:
                ref[...] += val

    vec3 = pl.BlockSpec((1, 1, LRU_TILE), lambda t, b: (t, 0, 0))
    vshape = jax.ShapeDtypeStruct((N_LRU_TILES, 1, LRU_TILE), F32)
    mshape = jax.ShapeDtypeStruct((N_LRU_TILES, LRU_TILE, LRU_TILE), F32)
    pair = pl.BlockSpec((lp, 2 * LRU_TILE), lambda t, b: (b, Z_U // (2 * LRU_TILE) + t))
    return _hosted_call(
        body, name="lru_bwd", grid=(N_LRU_TILES, nb),
        in_specs=[seq(Z_U // LRU_TILE, 2), seq(Z_U // LRU_TILE + 1, 2), seq(0), seq(0), cwspec, vec, mat, vec, mat,
                  vec, vec, pl.BlockSpec(memory_space=pl.ANY)],
        out_specs=[pair, cwspec, vec3, mat, vec3, mat, vec3, vec3],
        out_shape=[jax.ShapeDtypeStruct(dz.shape, dz.dtype), jax.ShapeDtypeStruct((N_LRU_TILES, CONV_K, LRU_TILE), F32),
                   vshape, mshape, vshape, mshape, vshape, vshape],
        scratch_shapes=[pltpu.VMEM((lp, LRU_TILE), F32)] * 3,
        dims=("parallel", "arbitrary"), args=(z, z, hs, dy, cw, cb, wa, ba, wx, bx, lam, dz), comm=comm,
        aliases={11: 0})


def _mix_out_call(ya, yl, ga, gl, wout, h, next_gain, tm):
    rows, d = h.shape

    def body(ya_ref, yl_ref, ga_ref, gl_ref, w_ref, h_ref, ng_ref, y_ref, o_ref, u_ref):
        a = ya_ref[...]
        l = yl_ref[...]
        an = (a * _rms(a, MLA_W) * ga_ref[...]).astype(BF16)
        ln = (l * _rms(l, LRU_W) * gl_ref[...]).astype(BF16)
        y_ref[:, 0:MLA_W] = an
        y_ref[:, MLA_W:MLA_W + LRU_W] = ln
        out = h_ref[...] + _dot(an, w_ref[0:MLA_W, :]) + _dot(ln, w_ref[MLA_W:MLA_W + LRU_W, :])
        o_ref[...] = out
        u_ref[...] = (out * _rms(out, d) * ng_ref[...]).astype(BF16)

    half = pl.BlockSpec((tm, MLA_W), lambda i: (i, 0))
    g = pl.BlockSpec((1, MLA_W), lambda i: (0, 0))
    full = pl.BlockSpec((tm, d), lambda i: (i, 0))
    return pl.pallas_call(
        body, name="mix_out", grid=(rows // tm,),
        in_specs=[half, half, g, g, pl.BlockSpec((MLA_W + LRU_W, d), lambda i: (0, 0)), full,
                  pl.BlockSpec((1, d), lambda i: (0, 0))],
        out_specs=[full, full, full],
        out_shape=[jax.ShapeDtypeStruct((rows, MLA_W + LRU_W), BF16), jax.ShapeDtypeStruct((rows, d), F32),
                   jax.ShapeDtypeStruct((rows, d), BF16)],
        compiler_params=_params(("parallel",)))(ya, yl, ga, gl, wout, h, next_gain)


def _ffn_dact_call(name, dhb, wd, gate, up, tm, comm=None):
    rows, d = dhb.shape
    ns, fs, _ = wd.shape

    nsub = 2 if tm % 32 == 0 else 1
    sub = tm // nsub

    def body(dh_ref, wd_ref, g_ref, p_ref, dg_ref, dp_ref):
        wd = wd_ref[0]
        for r in range(nsub):
            rs = slice(r * sub, (r + 1) * sub)
            da = (0.5 * _dot_nt(dh_ref[rs, :], wd)).astype(BF16)
            g = g_ref[0, rs, :]
            p = p_ref[0, rs, :]
            sg = jax.nn.sigmoid(g)
            dg_ref[0, rs, :] = (da * p) * (sg * (1.0 + g * (1.0 - sg)))
            dp_ref[0, rs, :] = da * (g * sg)

    aspec = pl.BlockSpec((1, tm, fs), lambda s, i: (s, i, 0))
    oshape = jax.ShapeDtypeStruct((ns, rows, fs), BF16)
    return _hosted_call(
        body, name=name, grid=(ns, rows // tm),
        in_specs=[pl.BlockSpec((tm, d), lambda s, i: (i, 0)), pl.BlockSpec((1, fs, d), lambda s, i: (s, 0, 0)),
                  aspec, aspec],
        out_specs=[aspec, aspec], out_shape=[oshape, oshape],
        dims=("parallel", "parallel"), args=(dhb, wd, gate, up), comm=comm)


def _norm_in_bwd_call(name, pieces, h, g, dres, tm, comm=None, part=(0, 1), prev=None, mix=None):
    rows, d = h.shape
    npc = len(pieces)
    steps = rows // tm // part[1]
    off = part[0] * steps
    n_prev = 0 if prev is None else 2
    n_mix = 0 if mix is None else 5

    def body(*refs):
        d_refs = refs[0:2 * npc:2]
        w_refs = refs[1:2 * npc:2]
        h_ref, g_ref, dres_ref = refs[2 * npc:2 * npc + 3]
        mix_in_refs = refs[2 * npc + 3:2 * npc + 3 + n_mix]
        dh_ref, dhb_ref, dg_ref = refs[2 * npc + 3 + n_mix + n_prev:2 * npc + 6 + n_mix + n_prev]
        mix_out_refs = refs[2 * npc + 6 + n_mix + n_prev:]
        du = jnp.zeros((tm, d), F32)
        for d_ref, w_ref in zip(d_refs, w_refs):
            if len(d_ref.shape) == 3:
                for s in range(d_ref.shape[0]):
                    du = du + _dot(d_ref[s], w_ref[s])
            else:
                du = du + _dot(d_ref[...], w_ref[...])
        x = h_ref[...]
        r = _rms(x, d)
        n = x * r
        dh = dres_ref[...] + _rms_bwd(du * g_ref[...], n, r, d)
        dhb = dh.astype(BF16)
        dh_ref[...] = dh
        dhb_ref[...] = dhb
        sums = [(dg_ref, jnp.sum(du * n, axis=0, keepdims=True))]
        if mix is not None:
            wo_ref, ya_ref, yl_ref, ga_ref, gl_ref = mix_in_refs
            dya_ref, dyl_ref, dga_ref, dgl_ref = mix_out_refs
            dy = _dot_nt(dhb, wo_ref[...])
            for val, gain_ref, lo, out_ref, acc_ref in ((ya_ref[...], ga_ref, 0, dya_ref, dga_ref),
                                                        (yl_ref[...], gl_ref, MLA_W, dyl_ref, dgl_ref)):
                rb = _rms(val, MLA_W)
                nb_ = val * rb
                dyn = dy[:, lo:lo + MLA_W]
                out_ref[...] = _rms_bwd(dyn * gain_ref[...], nb_, rb, MLA_W)
                sums.append((acc_ref, jnp.sum(dyn * nb_, axis=0, keepdims=True)))

        @pl.when(pl.program_id(0) == 0)
        def _():
            for ref, val in sums:
                ref[...] = val

        @pl.when(pl.program_id(0) != 0)
        def _():
            for ref, val in sums:
                ref[...] += val

    in_specs, args = [], []
    for dd, w in pieces:
        if dd.ndim == 3:
            in_specs.append(pl.BlockSpec((dd.shape[0], tm, dd.shape[2]), lambda i: (0, i + off, 0)))
            in_specs.append(_resident(w.shape))
        else:
            in_specs.append(pl.BlockSpec((tm, dd.shape[1]), lambda i: (i + off, 0)))
            in_specs.append(_resident(w.shape))
        args += [dd, w]
    full = pl.BlockSpec((tm, d), lambda i: (i + off, 0))
    gspec = pl.BlockSpec((1, d), lambda i: (0, 0))
    half = pl.BlockSpec((tm, MLA_W), lambda i: (i + off, 0))
    hgain = pl.BlockSpec((1, MLA_W), lambda i: (0, 0))
    mix_in_specs = [] if mix is None else [_resident(mix[0].shape), half, half, hgain, hgain]
    mix_out_specs = [] if mix is None else [half, half, hgain, hgain]
    mix_out_shapes = [] if mix is None else [
        jax.ShapeDtypeStruct((rows, MLA_W), F32), jax.ShapeDtypeStruct((rows, LRU_W), F32),
        jax.ShapeDtypeStruct((1, MLA_W), F32), jax.ShapeDtypeStruct((1, LRU_W), F32)]
    n_in = len(in_specs) + 3 + n_mix
    return _hosted_call(
        body, name=name, grid=(steps,),
        in_specs=in_specs + [full, gspec, full] + mix_in_specs + _any_specs(n_prev),
        out_specs=[full, full, gspec] + mix_out_specs,
        out_shape=[jax.ShapeDtypeStruct((rows, d), F32), jax.ShapeDtypeStruct((rows, d), BF16),
                   jax.ShapeDtypeStruct((1, d), F32)] + mix_out_shapes,
        args=(*args, h, g, dres, *(mix or ()), *(prev or ())), comm=comm,
        aliases={n_in: 0, n_in + 1: 1} if prev is not None else {})


def _wgrad_call(name, a, b, scale=1.0, comm=None):
    a3, b3 = a.ndim == 3, b.ndim == 3
    ns = a.shape[0] if a3 else (b.shape[0] if b3 else 1)
    rows, m = a.shape[-2:]
    n = b.shape[-1]
    tmm = m if a3 else _col_tile(m, 256)

    def body(a_ref, b_ref, o_ref):
        av = a_ref[0] if a3 else a_ref[...]
        bv = b_ref[0] if b3 else b_ref[...]
        res = _dot_tn(av, bv)
        if scale != 1.0:
            res = res * scale
        if a3 or b3:
            o_ref[0] = res
        else:
            o_ref[...] = res

    aspec = (pl.BlockSpec((1, rows, tmm), lambda s, j: (s, 0, j)) if a3
             else pl.BlockSpec((rows, tmm), lambda s, j: (0, j)))
    bspec = (pl.BlockSpec((1, rows, n), lambda s, j: (s, 0, 0)) if b3
             else pl.BlockSpec((rows, n), lambda s, j: (0, 0)))
    if a3 or b3:
        ospec = pl.BlockSpec((1, tmm, n), lambda s, j: (s, j, 0))
        oshape = jax.ShapeDtypeStruct((ns, m, n), F32)
    else:
        ospec = pl.BlockSpec((tmm, n), lambda s, j: (j, 0))
        oshape = jax.ShapeDtypeStruct((m, n), F32)
    return _hosted_call(
        body, name=name, grid=(ns, m // tmm), in_specs=[aspec, bspec], out_specs=[ospec], out_shape=[oshape],
        dims=("parallel", "parallel"), args=(a, b), comm=comm)[0]


def _mla_prep_bwd_call(z, dq, dk, dv, gql, gkvl, gqh, gkh, wuq, wuk, wuv, tabs, lp, tm, comm=None):
    rows = z.shape[0]
    tpe = lp // tm

    def body(z_ref, dq_ref, dk_ref, dv_ref, gql_ref, gkvl_ref, gqh_ref, gkh_ref, wuq_ref, wuk_ref, wuv_ref,
             c_ref, s1_ref, s2_ref, dz_ref, dgql_ref, dgkvl_ref, dgqh_ref, dgkh_ref, dwuq_ref, dwuk_ref, dwuv_ref,
             dqp_ref, dkn_ref, dvv_ref):
        gql, gkvl = gql_ref[...], gkvl_ref[...]
        gq, gk = gqh_ref[...], gkh_ref[...]
        wuq, wuk, wuv = wuq_ref[...], wuk_ref[...], wuv_ref[...]
        m = _mla_heads(z_ref[...], gql, gkvl, wuq, wuk, wuv)
        c, s1, s2 = c_ref[...], s1_ref[...], s2_ref[...]
        dgq = jnp.zeros((1, D_QKP), F32)
        dgk = jnp.zeros((1, D_QKP), F32)
        dkr = jnp.zeros((tm, D_QKP - D_NOPE), F32)
        for hd in range(HEADS):
            qn, rqh, knn, krn, rkh = m["heads"][hd]
            dqg = jnp.concatenate([dq_ref[hd, :, 0:D_NOPE].astype(F32),
                                   _rope_t(dq_ref[hd, :, D_NOPE:D_QKP].astype(F32), c, s1, s2)], axis=1)
            dgq = dgq + jnp.sum(dqg * qn, axis=0, keepdims=True)
            dqn = dqg * gq
            dqr = rqh * (dqn - qn * (jnp.sum(dqn * qn, axis=-1, keepdims=True) * (1.0 / D_QK)))
            dqp_ref[:, hd * D_QKP:(hd + 1) * D_QKP] = dqr.astype(BF16)
            kn_full = jnp.concatenate([knn, krn], axis=1)
            dkg = jnp.concatenate([dk_ref[hd, :, 0:D_NOPE].astype(F32),
                                   _rope_t(dk_ref[hd, :, D_NOPE:D_QKP].astype(F32), c, s1, s2)], axis=1)
            dgk = dgk + jnp.sum(dkg * kn_full, axis=0, keepdims=True)
            dkn = dkg * gk
            dkraw = rkh * (dkn - kn_full * (jnp.sum(dkn * kn_full, axis=-1, keepdims=True) * (1.0 / D_QK)))
            dkn_ref[:, hd * D_NOPE:(hd + 1) * D_NOPE] = dkraw[:, 0:D_NOPE].astype(BF16)
            dkr = dkr + dkraw[:, D_NOPE:D_QKP]
            dvv_ref[:, hd * D_V:(hd + 1) * D_V] = dv_ref[hd]
        dcqn = _dot(dqp_ref[...], wuq)
        dckvn = _dot_nt(dkn_ref[...], wuk) + _dot_nt(dvv_ref[...], wuv)
        dz_ref[:, 0:Q_RANK] = _rms_bwd(dcqn * gql, m["nq"], m["rq"], Q_RANK).astype(BF16)
        dz_ref[:, Q_RANK:Z_KR] = _rms_bwd(dckvn * gkvl, m["nkv"], m["rkv"], KV_RANK).astype(BF16)
        dz_ref[:, Z_KR:Z_MLA] = dkr.astype(BF16)
        parts = [(dgql_ref, jnp.sum(dcqn * m["nq"], axis=0, keepdims=True)),
                 (dgkvl_ref, jnp.sum(dckvn * m["nkv"], axis=0, keepdims=True)), (dgqh_ref, dgq), (dgkh_ref, dgk),
                 (dwuq_ref, _dot_tn(dqp_ref[...], m["cqn"])), (dwuk_ref, _dot_tn(m["ckvn"], dkn_ref[...])),
                 (dwuv_ref, _dot_tn(m["ckvn"], dvv_ref[...]))]

        @pl.when(pl.program_id(0) == 0)
        def _():
            for ref, val in parts:
                ref[...] = val

        @pl.when(pl.program_id(0) != 0)
        def _():
            for ref, val in parts:
                ref[...] += val

    def const(shape):
        return pl.BlockSpec(shape, lambda i: tuple(0 for _ in shape))

    tab = pl.BlockSpec((tm, 128), lambda i: (i % tpe, 0))
    hq = pl.BlockSpec((HEADS, tm, D_QKP), lambda i: (0, i, 0))
    hv = pl.BlockSpec((HEADS, tm, D_V), lambda i: (0, i, 0))

    def rowspec(n):
        return pl.BlockSpec((tm, n), lambda i: (i, 0))

    return _hosted_call(
        body, name="mla_prep_bwd", grid=(rows // tm,),
        in_specs=[rowspec(Z_MLA), hq, hq, hv, const((1, Q_RANK)), const((1, KV_RANK)), const((1, D_QKP)),
                  const((1, D_QKP)), const((HEADS * D_QKP, Q_RANK)), const((KV_RANK, HEADS * D_NOPE)),
                  const((KV_RANK, HEADS * D_V)), tab, tab, tab],
        out_specs=[rowspec(Z_MLA), const((1, Q_RANK)), const((1, KV_RANK)), const((1, D_QKP)), const((1, D_QKP)),
                   const((HEADS * D_QKP, Q_RANK)), const((KV_RANK, HEADS * D_NOPE)), const((KV_RANK, HEADS * D_V))],
        out_shape=[jax.ShapeDtypeStruct((rows, Z_W), BF16),
                   jax.ShapeDtypeStruct((1, Q_RANK), F32), jax.ShapeDtypeStruct((1, KV_RANK), F32),
                   jax.ShapeDtypeStruct((1, D_QKP), F32), jax.ShapeDtypeStruct((1, D_QKP), F32),
                   jax.ShapeDtypeStruct((HEADS * D_QKP, Q_RANK), F32),
                   jax.ShapeDtypeStruct((KV_RANK, HEADS * D_NOPE), F32), jax.ShapeDtypeStruct((KV_RANK, HEADS * D_V), F32)],
        scratch_shapes=[pltpu.VMEM((tm, HEADS * D_QKP), BF16), pltpu.VMEM((tm, HEADS * D_NOPE), BF16),
                        pltpu.VMEM((tm, HEADS * D_V), BF16)],
        args=(z, dq, dk, dv, gql, gkvl, gqh, gkh, wuq, wuk, wuv, *tabs), comm=comm)


def _local_step(h0, target, w, nb, lp, sched=None):
    tm = _row_tile(nb * lp, 1408)
    te = _row_tile(lp, 512)
    tabs = _rope_tables(lp)
    g = {}
    if sched is None:
        host = lambda stage: None
    else:
        sched.g = g
        host = sched.host

    def ffn_act(tag, u, split):
        if split:
            gate = _ffn_gate_call(tag + "_gate", u, w[tag + "_w_gate"], tm, host(tag + "_gate"))
            up, act = _ffn_upact_call(tag + "_upact", u, w[tag + "_w_up"], gate, tm, host(tag + "_upact"))
        else:
            gate, up, act = _ffn_up_call(tag + "_up", u, w[tag + "_w_gate"], w[tag + "_w_up"], tm, host(tag + "_up"))
        return u, gate, up, act

    def ffn_bwd(tag, h, saved, dh, dhb, split, mix=None):
        u, gate, up, act = saved
        dgate, dup = _ffn_dact_call(tag + "_dact", dhb, w[tag + "_w_down"], gate, up, tm, host(tag + "_dact"))
        g[tag + "_w_down"] = _wgrad_call(tag + "_dwd", act, dhb, 0.5, host(tag + "_dwd"))
        g[tag + "_w_gate"] = _wgrad_call(tag + "_dwg", dgate, u, 1.0, host(tag + "_dwg"))
        g[tag + "_w_up"] = _wgrad_call(tag + "_dwu", dup, u, 1.0, host(tag + "_dwu"))
        pieces = [(dgate, w[tag + "_w_gate"]), (dup, w[tag + "_w_up"])]
        if not split:
            res = _norm_in_bwd_call(tag + "_din", pieces, h, w[tag + "_norm"], dh, te, host(tag + "_din"), mix=mix)
            g[tag + "_norm"] = res[2]
            return (res[0], res[1], *res[3:])
        dh_a, dhb_a, dg_a = _norm_in_bwd_call(tag + "_din_a", pieces, h, w[tag + "_norm"], dh, te,
                                              host(tag + "_din_a"), part=(0, 2))
        dh_in, dhb_in, dg_b = _norm_in_bwd_call(tag + "_din_b", pieces, h, w[tag + "_norm"], dh, te,
                                                host(tag + "_din_b"), part=(1, 2), prev=(dh_a, dhb_a))
        g[tag + "_norm"] = dg_a + dg_b
        return dh_in, dhb_in

    s1 = ffn_act("ffn1", _rmsnorm_call("ffn1_norm", h0, w["ffn1_norm"], te, host("ffn1_norm")), True)
    h1, un = _ffn_down_call("ffn1_down", s1[3], w["ffn1_w_down"], h0, te, host("ffn1_down"), w["mix_norm"])
    z = _mm_call("mix_in", un, w["w_in"], tm, F32, host("mix_in"))
    mla_w = (w["q_latent_norm"], w["kv_latent_norm"], w["q_head_norm"], w["k_head_norm"], w["w_uq"], w["w_uk"],
             w["w_uv"])
    q, k, v = _mla_prep_call(z, *mla_w, tabs, lp, te)
    o, lse = _attn_fwd_call(q, k, v, nb, lp, host("attn_fwd"))
    lru_w = (w["conv_w"], w["conv_b"], w["gate_a_w"], w["gate_a_b"], w["gate_x_w"], w["gate_x_b"], w["lru_lambda"])
    yl, hs = _lru_fwd_call(z, *lru_w, nb, lp, host("lru_fwd"))
    y, h2, u2 = _mix_out_call(o, yl, w["attn_out_norm"], w["lru_out_norm"], w["w_out"], h1, w["ffn2_norm"], te)
    s2 = ffn_act("ffn2", u2, False)
    dh3, dh3b, g["final_norm"], loss = _ffn_down_loss_call("ffn2_down", s2[3], w["ffn2_w_down"], h2, w["final_norm"],
                                                           target, lp, te)
    g["loss"] = loss

    dh2, dh2b, dya, dyl, g["attn_out_norm"], g["lru_out_norm"] = ffn_bwd(
        "ffn2", h2, s2, dh3, dh3b, False, (w["w_out"], o, yl, w["attn_out_norm"], w["lru_out_norm"]))
    g["w_out"] = _wgrad_call("dw_out", y, dh2b)
    dq, dk, dv = _attn_bwd_call(q, k, v, o, lse, dya, nb, lp, host("attn_bwd"))
    (dz_mla, g["q_latent_norm"], g["kv_latent_norm"], g["q_head_norm"], g["k_head_norm"], g["w_uq"], g["w_uk"],
     g["w_uv"]) = _mla_prep_bwd_call(z, dq, dk, dv, *mla_w, tabs, lp, te, host("mla_prep_bwd"))
    (dz, g["conv_w"], g["conv_b"], g["gate_a_w"], g["gate_a_b"], g["gate_x_w"], g["gate_x_b"],
     g["lru_lambda"]) = _lru_bwd_call(z, dz_mla, hs, dyl, *lru_w, nb, lp, host("lru_bwd"))
    g["w_in"] = _wgrad_call("dw_in", dz, un)
    dh1, dh1b, g["mix_norm"] = _norm_in_bwd_call("mix_din", [(dz, w["w_in"])], h1, w["mix_norm"], dh2, te,
                                                 host("mix_din"))
    dh0 = ffn_bwd("ffn1", h0, s1, dh1, dh1b, True)[0]
    return loss, dh0, g


def _place():
    x, y, c = lax.axis_index("x"), lax.axis_index("y"), lax.axis_index("c")
    return x, y, c, [(1 - x, y), (x, 1 - y), (1 - x, 1 - y)]


def _any_specs(n):
    return [pl.BlockSpec(memory_space=pl.ANY)] * n


def _remote(src, dst, sems, k, dev):
    send_sems, recv_sems, base = sems
    return pltpu.make_async_remote_copy(src_ref=src, dst_ref=dst, send_sem=send_sems.at[base + k],
                                        recv_sem=recv_sems.at[base + k], device_id=dev, device_id_type=MESH)


EW_VMEM_BYTES = 24 * 1024 * 1024


def _fit_rows(rows, cols, blocks):
    return _row_tile(rows, max(16, int(EW_VMEM_BYTES // (8 * blocks)) // cols))


class _Geom:
    def __init__(self, n0, n1, blocks=1.0):
        self.n0, self.n1 = n0, n1
        self.axis = 0 if n0 % 32 == 0 else 1
        self.h0, self.h1 = (n0 // 2, n1) if self.axis == 0 else (n0, n1 // 2)
        self.tr = _fit_rows(self.h0, self.h1, blocks)
        self.nblk = self.h0 // self.tr

    def half_ref(self, ref, lead, idx):
        if self.axis == 0:
            return ref.at[(*lead, pl.ds(idx * self.h0, self.h0))]
        return ref.at[(*lead, slice(None), pl.ds(idx * self.h1, self.h1))]

    def half_block(self, lead, i, idx):
        return (*lead, idx * self.nblk + i, 0) if self.axis == 0 else (*lead, i, idx)


class _Comm:
    def __init__(self, ins, out_shapes, aliases, n_sems, start, finish, deliver):
        self.ins, self.out_shapes, self.aliases, self.n_sems = list(ins), list(out_shapes), dict(aliases), n_sems
        self.start, self.finish, self.deliver = start, finish, deliver

    def scratch(self):
        return [pltpu.SemaphoreType.DMA((self.n_sems,)), pltpu.SemaphoreType.DMA((self.n_sems,))]


def _comm_call(name, comm):
    n_in = len(comm.ins)

    def body(*refs):
        ins, outs, sems = refs[:n_in], refs[n_in:-2], (*refs[-2:], 0)
        comm.start(ins, outs, sems)
        comm.finish(ins, outs, sems)

    res = pl.pallas_call(
        body, name=name, out_shape=comm.out_shapes, in_specs=_any_specs(n_in),
        out_specs=_any_specs(len(comm.out_shapes)), input_output_aliases=comm.aliases,
        scratch_shapes=comm.scratch())(*comm.ins)
    return comm.deliver(list(res))


def _hosted_call(body, *, name, grid, in_specs, out_specs, out_shape, args, scratch_shapes=(), dims=None, comm=None,
                 prefetch=None, aliases=None):
    aliases = dict(aliases or {})
    in_specs, out_specs, out_shape = list(in_specs), list(out_specs), list(out_shape)
    n_pre = 0 if prefetch is None else 1

    def call(fn, in_specs, out_specs, out_shape, scratch, aliases, dims, args):
        if prefetch is None:
            return pl.pallas_call(
                fn, name=name, grid=grid, in_specs=in_specs, out_specs=out_specs, out_shape=out_shape,
                scratch_shapes=scratch, input_output_aliases=aliases, compiler_params=_params(dims))(*args)
        spec = pltpu.PrefetchScalarGridSpec(num_scalar_prefetch=1, grid=grid, in_specs=in_specs, out_specs=out_specs,
                                            scratch_shapes=scratch)
        return pl.pallas_call(
            fn, name=name, grid_spec=spec, out_shape=out_shape,
            input_output_aliases={i + 1: o for i, o in aliases.items()}, compiler_params=_params(dims))(prefetch, *args)

    if comm is None:
        return list(call(body, in_specs, out_specs, out_shape, list(scratch_shapes), aliases,
                         dims or ("arbitrary",) * len(grid), args))
    n_in, n_out, n_ci, n_co = len(in_specs), len(out_specs), len(comm.ins), len(comm.out_shapes)

    def wrapped(*refs):
        pre, refs = refs[:n_pre], refs[n_pre:]
        ins, cins = refs[:n_in], refs[n_in:n_in + n_ci]
        outs = refs[n_in + n_ci:n_in + n_ci + n_out]
        couts = refs[n_in + n_ci + n_out:n_in + n_ci + n_out + n_co]
        scratch, sems = refs[n_in + n_ci + n_out + n_co:-2], (*refs[-2:], 0)
        first = functools.reduce(jnp.logical_and, [pl.program_id(k) == 0 for k in range(len(grid))])
        last = functools.reduce(jnp.logical_and, [pl.program_id(k) == grid[k] - 1 for k in range(len(grid))])

        @pl.when(first)
        def _():
            comm.start(cins, couts, sems)

        body(*pre, *ins, *outs, *scratch)

        @pl.when(last)
        def _():
            comm.finish(cins, couts, sems)

    res = call(wrapped, in_specs + _any_specs(n_ci), out_specs + _any_specs(n_co), out_shape + comm.out_shapes,
               list(scratch_shapes) + comm.scratch(),
               {**aliases, **{n_in + i: n_out + o for i, o in comm.aliases.items()}},
               ("arbitrary",) * len(grid), (*args, *comm.ins))
    comm.deliver(list(res[n_out:]))
    return list(res[:n_out])


def _gather_comm(bufs, deliver):
    n = len(bufs)
    geoms = [_Geom(*b.shape[1:]) for b in bufs]

    def first(outs, sems):
        x, y, c, chips = _place()
        cps = []
        for a in range(n):
            mine = geoms[a].half_ref(outs[a], (2 * x + y,), c)
            cps += [_remote(mine, mine, sems, 6 * a + j, (cx, cy, c)) for j, (cx, cy) in enumerate(chips)]
        return cps

    def start(ins, outs, sems):
        for cp in first(outs, sems):
            cp.start()

    def finish(ins, outs, sems):
        x, y, c, chips = _place()
        sib = (x, y, 1 - c)
        passed = []
        for a in range(n):
            for j, (cx, cy) in enumerate(chips):
                land = geoms[a].half_ref(outs[a], (2 * cx + cy,), c)
                _remote(land, land, sems, 6 * a + j, sib).wait_recv()
                cp = _remote(land, land, sems, 6 * a + 3 + j, sib)
                cp.start()
                passed.append(cp)
        for a in range(n):
            for j, (cx, cy) in enumerate(chips):
                land = geoms[a].half_ref(outs[a], (2 * cx + cy,), 1 - c)
                _remote(land, land, sems, 6 * a + 3 + j, sib).wait_recv()
        for cp in first(outs, sems) + passed:
            cp.wait_send()

    return _Comm(bufs, [jax.ShapeDtypeStruct(b.shape, b.dtype) for b in bufs], {a: a for a in range(n)}, 6 * n,
                 start, finish, deliver)


def _reduce_pair_comm(grads, deliver):
    n = len(grads)
    geoms = [_Geom(*a.shape[1:]) for a in grads]

    def copies(ins, outs, sems):
        x, y, c, _ = _place()
        return [_remote(geoms[a].half_ref(ins[a], (slice(None),), 1 - c), outs[a], sems, a, (x, y, 1 - c))
                for a in range(n)]

    def start(ins, outs, sems):
        for cp in copies(ins, outs, sems):
            cp.start()

    def finish(ins, outs, sems):
        cps = copies(ins, outs, sems)
        for cp in cps:
            cp.wait_recv()
        for cp in cps:
            cp.wait_send()

    shapes = [jax.ShapeDtypeStruct((N_SHARD, g.h0, g.h1), a.dtype) for a, g in zip(grads, geoms)]
    return _Comm(grads, shapes, {}, n, start, finish, deliver)


def _reduce_chips_comm(parts, deliver):
    n = len(parts)

    def copies(ins, outs, sems):
        x, y, c, chips = _place()
        return [_remote(ins[a].at[2 * cx + cy], outs[a].at[j], sems, 3 * a + j, (cx, cy, c))
                for a in range(n) for j, (cx, cy) in enumerate(chips)]

    def start(ins, outs, sems):
        for cp in copies(ins, outs, sems):
            cp.start()

    def finish(ins, outs, sems):
        cps = copies(ins, outs, sems)
        for cp in cps:
            cp.wait_recv()
        for cp in cps:
            cp.wait_send()

    shapes = [jax.ShapeDtypeStruct((3,) + a.shape[1:], a.dtype) for a in parts]
    return _Comm(parts, shapes, {}, 3 * n, start, finish, deliver)


def _share_pair_comm(bufs, deliver):
    n = len(bufs)
    geoms = [_Geom(*b.shape) for b in bufs]

    def copies(outs, sems):
        x, y, c, _ = _place()
        cps = []
        for a in range(n):
            mine = geoms[a].half_ref(outs[a], (), c)
            cps.append(_remote(mine, mine, sems, a, (x, y, 1 - c)))
        return cps

    def start(ins, outs, sems):
        for cp in copies(outs, sems):
            cp.start()

    def finish(ins, outs, sems):
        x, y, c, _ = _place()
        for a in range(n):
            land = geoms[a].half_ref(outs[a], (), 1 - c)
            _remote(land, land, sems, a, (x, y, 1 - c)).wait_recv()
        for cp in copies(outs, sems):
            cp.wait_send()

    return _Comm(bufs, [jax.ShapeDtypeStruct(b.shape, b.dtype) for b in bufs], {a: a for a in range(n)}, n,
                 start, finish, deliver)


def _small_comm(pack, deliver):
    r, d = pack.shape

    def copies(ins, outs, sems):
        x, y, c, _ = _place()
        cps = []
        for k in range(1, 8):
            peer = (x ^ ((k >> 2) & 1), y ^ ((k >> 1) & 1), c ^ (k & 1))
            cps.append(_remote(ins[0], outs[0].at[4 * x + 2 * y + c], sems, k - 1, peer))
        return cps

    def start(ins, outs, sems):
        for cp in copies(ins, outs, sems):
            cp.start()

    def finish(ins, outs, sems):
        cps = copies(ins, outs, sems)
        for cp in cps:
            cp.wait_recv()
        for cp in cps:
            cp.wait_send()

    return _Comm([pack], [jax.ShapeDtypeStruct((8, r, d), pack.dtype)], {}, 7, start, finish, deliver)


def _join_comms(comms):
    comms = [c for c in comms if c is not None]
    if len(comms) <= 1:
        return comms[0] if comms else None
    ins, out_shapes, aliases, spans, n_sems = [], [], {}, [], 0
    for c in comms:
        aliases.update({len(ins) + i: len(out_shapes) + o for i, o in c.aliases.items()})
        spans.append((len(ins), len(ins) + len(c.ins), len(out_shapes), len(out_shapes) + len(c.out_shapes), n_sems))
        ins += c.ins
        out_shapes += c.out_shapes
        n_sems += c.n_sems

    def run(which):
        def go(all_ins, all_outs, sems):
            for c, (i0, i1, o0, o1, base) in zip(comms, spans):
                getattr(c, which)(all_ins[i0:i1], all_outs[o0:o1], (sems[0], sems[1], sems[2] + base))
        return go

    def deliver(outs):
        for c, (_, _, o0, o1, _) in zip(comms, spans):
            c.deliver(outs[o0:o1])
        return outs

    return _Comm(ins, out_shapes, aliases, n_sems, run("start"), run("finish"), deliver)


def _ew_call(name, fn, ins, out_dtypes):
    shape = ins[0].shape
    cols = shape[-1]
    rows = 1
    for s_ in shape[:-1]:
        rows *= s_
    ins2 = [a.reshape(rows, cols) for a in ins]
    tr = rows
    for t in range(16, min(rows, max(16, (1 << 19) // cols)) + 1, 16):
        if rows % t == 0:
            tr = t
    no = len(out_dtypes)

    def body(*refs):
        outs = fn(*[r[...] for r in refs[:len(ins2)]])
        for ref, val in zip(refs[len(ins2):], outs):
            ref[...] = val.astype(ref.dtype)

    spec = pl.BlockSpec((tr, cols), lambda i: (i, 0))
    res = pl.pallas_call(
        body, name=name, grid=(rows // tr,), in_specs=[spec] * len(ins2), out_specs=[spec] * no,
        out_shape=[jax.ShapeDtypeStruct((rows, cols), dt) for dt in out_dtypes],
        compiler_params=_params(("parallel",)))(*ins2)
    return [r.reshape(shape) for r in res]


def _adamw_math(w, g, m, v):
    m = ADAM_B1 * m + (1.0 - ADAM_B1) * g
    v = ADAM_B2 * v + (1.0 - ADAM_B2) * (g * g)
    m_hat = m / (1.0 - ADAM_B1 ** ADAM_STEP)
    v_hat = v / (1.0 - ADAM_B2 ** ADAM_STEP)
    delta = -ADAM_LR * (m_hat / (jnp.sqrt(v_hat) + ADAM_EPS) + ADAM_WD * w)
    return delta, m, v


def _adamw_call(name, w, g, m, v):
    return _ew_call(name, _adamw_math, [w, g, m, v], [F32, F32, F32])


def _tiled_call(name, fn, place, grid, in_items, out_items, comm=None):
    ni = len(in_items)

    def body(place_ref, *refs):
        vals = fn(*[r[...] for r in refs[:ni]])
        for ref, val in zip(refs[ni:], vals):
            ref[...] = val.astype(ref.dtype)

    return _hosted_call(
        body, name=name, grid=grid, in_specs=[pl.BlockSpec(blk, imap) for _, blk, imap in in_items],
        out_specs=[pl.BlockSpec(blk, imap) for _, _, blk, imap in out_items],
        out_shape=[jax.ShapeDtypeStruct(shp, dt) for shp, dt, _, _ in out_items],
        args=[a for a, _, _ in in_items], prefetch=place, comm=comm)


def _cast_call(name, place, shards, comm=None):
    n0, n1 = shards[0].shape
    tr = _fit_rows(n0, n1, 1.5 * len(shards))
    ins = [(a, (tr, n1), lambda i, p: (i, 0)) for a in shards]
    outs = [((N_SHARD, n0, n1), BF16, (1, tr, n1), lambda i, p: (p[0], i, 0)) for _ in shards]
    return _tiled_call(name, lambda *v: [x[None] for x in v], place, (n0 // tr,), ins, outs, comm)


def _pair_sum_call(name, place, fulls, gots):
    k = len(fulls)
    g = _Geom(*fulls[0].shape[1:], blocks=2.5 * k)
    blk = (1, g.tr, g.h1)
    ins = [(a, blk, lambda s, i, p: g.half_block((s,), i, p[1])) for a in fulls]
    ins += [(a, blk, lambda s, i, p: (s, i, 0)) for a in gots]
    outs = [((N_SHARD, g.h0, g.h1), BF16, blk, lambda s, i, p: (s, i, 0)) for _ in fulls]
    return _tiled_call(name, lambda *v: [v[j] + v[k + j] for j in range(k)], place, (N_SHARD, g.nblk), ins, outs)


def _chip_sum_call(name, place, fulls, gots, recvs, comm=None):
    k = len(fulls)
    g = _Geom(*fulls[0].shape[1:], blocks=4.5 * k)
    blk = (1, g.tr, g.h1)
    ins = [(a, blk, lambda i, p: g.half_block((p[0],), i, p[1])) for a in fulls]
    ins += [(a, blk, lambda i, p: (p[0], i, 0)) for a in gots]
    ins += [(a, (3, g.tr, g.h1), lambda i, p: (0, i, 0)) for a in recvs]
    outs = [((g.n0, g.n1), F32, (g.tr, g.h1), lambda i, p: g.half_block((), i, p[1])) for _ in fulls]

    def fn(*v):
        res = []
        for j in range(k):
            r = v[2 * k + j].astype(F32)
            res.append(v[j][0] + v[k + j][0] + r[0] + r[1] + r[2])
        return res

    return _tiled_call(name, fn, place, (g.nblk,), ins, outs, comm)


def _adamw_group_call(name, ws, gs, ms, vs, comm=None):
    k = len(ws)
    n0, n1 = ws[0].shape
    tr = _fit_rows(n0, n1, 8 * k)
    spec = pl.BlockSpec((tr, n1), lambda i: (i, 0))

    def body(*refs):
        for j in range(k):
            g = refs[k + j][...]
            delta, m, vv = _adamw_math(refs[j][...], g, refs[2 * k + j][...], refs[3 * k + j][...])
            for ref, val in zip(refs[4 * k + 4 * j:4 * k + 4 * j + 4], (g, delta, m, vv)):
                ref[...] = val

    flat = _hosted_call(
        body, name=name, grid=(n0 // tr,), in_specs=[spec] * (4 * k), out_specs=[spec] * (4 * k),
        out_shape=[jax.ShapeDtypeStruct((n0, n1), F32)] * (4 * k), dims=("parallel",),
        args=(*ws, *gs, *ms, *vs), comm=comm)
    return [flat[4 * j:4 * j + 4] for j in range(k)]


def _small_update_call(me, early, own_early, late, own_late, wp, mp, vp):
    nd, r, d = early.shape

    def body(me_ref, e_ref, oe_ref, l_ref, ol_ref, w_ref, m_ref, v_ref, gs_ref, d_ref, nm_ref, nv_ref):
        mine = me_ref[0]

        def total(g_ref, own_ref):
            acc = None
            for k in range(nd):
                part = jnp.where(mine == k, own_ref[...], g_ref[k])
                acc = part if acc is None else acc + part
            return acc

        gs = total(e_ref, oe_ref)
        ls = total(l_ref, ol_ref)
        gs_ref[...] = gs
        first = gs[0:8] + ls[0:8]
        gs_ref[0:8, :] = first
        gs_ref[ROW_META:ROW_META + N_META, :] = gs[ROW_META:ROW_META + N_META] + ls[8:8 + N_META]
        grads = jnp.concatenate([first, gs[8:SMALL_ADAM_ROWS]], axis=0)
        delta, m, v = _adamw_math(w_ref[...], grads, m_ref[...], v_ref[...])
        d_ref[...] = delta
        nm_ref[...] = m
        nv_ref[...] = v

    vm = pl.BlockSpec(memory_space=pltpu.VMEM)
    ashape = jax.ShapeDtypeStruct((SMALL_ADAM_ROWS, d), F32)
    return pl.pallas_call(
        body, name="small_update", in_specs=[pl.BlockSpec(memory_space=pltpu.SMEM)] + [vm] * 7, out_specs=[vm] * 4,
        out_shape=[jax.ShapeDtypeStruct((r, d), F32), ashape, ashape, ashape],
        compiler_params=pltpu.CompilerParams(vmem_limit_bytes=VMEM_LIMIT_BYTES))(
            me, early, own_early, late, own_late, wp, mp, vp)


SMALL_NAMES = ["ffn1_norm", "mix_norm", "ffn2_norm", "final_norm", "q_latent_norm", "kv_latent_norm",
               "q_head_norm", "k_head_norm", "conv_b", "gate_a_b", "gate_x_b", "lru_lambda", "attn_out_norm",
               "lru_out_norm"]
ROW_CONV_W = 14
ROW_GATE_A = 16
ROW_GATE_X = 48
ROW_META = 80
ROW_LOSS = 96


def _row(a):
    flat = a.reshape(1, -1)
    return jnp.pad(flat, ((0, 0), (0, D_MODEL - flat.shape[1])))


def _pack_small(t, rows):
    parts = [_row(t[nm]) for nm in SMALL_NAMES]
    parts.append(t["conv_w"].reshape(2, D_MODEL))
    parts.append(t["gate_a_w"].reshape(32, D_MODEL))
    parts.append(t["gate_x_w"].reshape(32, D_MODEL))
    p = jnp.concatenate(parts, axis=0)
    return jnp.pad(p, ((0, rows - p.shape[0]), (0, 0)))


def _early_pack(g):
    gs = {nm: g.get(nm, jnp.zeros((1, D_MODEL), F32)) for nm in SMALL_NAMES}
    gs["q_head_norm"] = g["q_head_norm"][:, 0:D_QK]
    gs["k_head_norm"] = g["k_head_norm"][:, 0:D_QK]
    for nm in ("conv_b", "gate_a_b", "gate_x_b", "lru_lambda"):
        gs[nm] = g[nm].reshape(1, LRU_W)
    gs["conv_w"] = g["conv_w"].transpose(1, 0, 2).reshape(CONV_K, LRU_W)
    gs["gate_a_w"] = _gate_blocks(g["gate_a_w"])
    gs["gate_x_w"] = _gate_blocks(g["gate_x_w"])
    return jnp.concatenate([_pack_small(gs, ROW_META), jnp.zeros((N_META, D_MODEL), F32), _row(g["loss"][:, 0:1]),
                            jnp.zeros((SMALL_ROWS - ROW_LOSS - 1, D_MODEL), F32)], axis=0)


def _unpack_small(p, like):
    out = {}
    for k, nm in enumerate(SMALL_NAMES):
        out[nm] = p[k, 0:like[nm].size].reshape(like[nm].shape)
    out["gate_a_w"] = p[ROW_GATE_A:ROW_GATE_A + 32].reshape(like["gate_a_w"].shape)
    out["gate_x_w"] = p[ROW_GATE_X:ROW_GATE_X + 32].reshape(like["gate_x_w"].shape)
    return out


def _gate_dense(wg):
    w4 = wg[0].reshape(N_LRU_TILES, 2, 64, 64)
    zero = jnp.zeros((N_LRU_TILES, 64, 64), wg.dtype)
    top = jnp.concatenate([w4[:, 0], zero], axis=2)
    bot = jnp.concatenate([zero, w4[:, 1]], axis=2)
    return jnp.concatenate([top, bot], axis=1).astype(BF16)


def _gate_blocks(dw):
    return jnp.stack([dw[:, 0:64, 0:64], dw[:, 64:128, 64:128]], axis=1).reshape(8, 64, 64)


BIG_NAMES = ["ffn1_w_gate", "ffn1_w_up", "ffn1_w_down", "w_in", "w_uq", "w_uk", "w_uv", "w_out", "ffn2_w_gate",
             "ffn2_w_up", "ffn2_w_down"]
BIG_GROUPS = [["ffn1_w_gate", "ffn1_w_up", "ffn1_w_down", "ffn2_w_gate", "ffn2_w_up", "ffn2_w_down"], ["w_in"],
              ["w_uq"], ["w_uk", "w_uv"], ["w_out"]]
TRANSPOSED = ("ffn1_w_gate", "ffn1_w_up", "ffn2_w_gate", "ffn2_w_up", "w_in", "w_uq")


def _to2d(nm, a):
    return a[0].T if nm in TRANSPOSED else a[0]


def _from2d(nm, a):
    return (a.T if nm in TRANSPOSED else a)[None]


WEIGHT_NAMES = ["meta_tokens", "ffn1_norm", "ffn1_w_gate", "ffn1_w_up", "ffn1_w_down", "mix_norm", "w_in",
                "q_latent_norm", "w_uq", "kv_latent_norm", "w_uk", "w_uv", "q_head_norm", "k_head_norm", "conv_w",
                "conv_b", "gate_a_w", "gate_a_b", "gate_x_w", "gate_x_b", "lru_lambda", "attn_out_norm",
                "lru_out_norm", "w_out", "ffn2_norm", "ffn2_w_gate", "ffn2_w_up", "ffn2_w_down", "final_norm"]


def _weight_from(nm, slots):
    if nm == "w_in":
        win = slots.reshape(IN_WIDTH, D_MODEL)
        lru = win[Z_KR + D_ROPE:].reshape(2, N_LRU_TILES, LRU_TILE, D_MODEL).transpose(1, 0, 2, 3)
        return jnp.concatenate([win[0:Z_KR + D_ROPE], jnp.zeros((128 - D_ROPE, D_MODEL), BF16),
                                lru.reshape(2 * LRU_W, D_MODEL)], axis=0)
    if nm == "w_uq":
        return jnp.pad(slots, ((0, 0), (0, D_QKP - D_QK), (0, 0))).reshape(HEADS * D_QKP, Q_RANK)
    if nm in ("w_uk", "w_uv"):
        return slots.transpose(1, 0, 2).reshape(KV_RANK, HEADS * D_NOPE)
    if nm == "w_out":
        return slots.reshape(D_MODEL, D_MODEL)
    return slots


def _small_weights(p, small):
    w = {nm: p[nm] for nm in SMALL_NAMES}
    w["q_head_norm"] = jnp.pad(p["q_head_norm"], ((0, 0), (0, D_QKP - D_QK)))
    w["k_head_norm"] = jnp.pad(p["k_head_norm"], ((0, 0), (0, D_QKP - D_QK)))
    w["conv_w"] = small[:, N_META:N_META + 2, :].reshape(N_SHARD, CONV_K, LRU_TILE)
    w["gate_a_w"] = _gate_dense(p["gate_a_w"])
    w["gate_x_w"] = _gate_dense(p["gate_x_w"])
    meta = small[:, 0:N_META, :].transpose(1, 0, 2).reshape(N_META, D_MODEL)
    return w, meta


def _full_weights(p, gathered, small):
    w, meta = _small_weights(p, small)
    w.update({nm: _weight_from(nm, gathered[nm]) for nm in BIG_NAMES})
    return w, meta


def _shard_grad(nm, g):
    if nm == "w_in":
        lru = g[Z_MLA:].reshape(N_LRU_TILES, 2, LRU_TILE, D_MODEL).transpose(1, 0, 2, 3).reshape(2 * LRU_W, D_MODEL)
        return jnp.concatenate([g[0:Z_KR + D_ROPE], lru], axis=0).reshape(N_SHARD, IN_WIDTH // N_SHARD, D_MODEL)
    if nm == "w_uq":
        return g.reshape(HEADS, D_QKP, Q_RANK)[:, 0:D_QK, :]
    if nm in ("w_uk", "w_uv"):
        return g.reshape(KV_RANK, HEADS, D_NOPE).transpose(1, 0, 2)
    if nm == "w_out":
        return g.reshape(N_SHARD, D_MODEL // N_SHARD, D_MODEL)
    return g


def _shard_grads(g):
    return {nm: _shard_grad(nm, g[nm]) for nm in BIG_NAMES}


GATHER_AT = {"ffn1_norm": ["ffn1_w_gate"], "ffn1_gate": ["ffn1_w_up"], "ffn1_upact": ["ffn1_w_down"],
             "ffn1_down": ["w_in", "w_uq", "w_uk", "w_uv"], "mix_in": ["w_out"],
             "attn_fwd": ["ffn2_w_down", "ffn2_w_gate"],
             "lru_fwd": ["ffn2_w_up"]}
PAIR_AT = [("ffn2_din", ["ffn2_w_gate", "ffn2_w_up", "ffn2_w_down"]),
           ("mix_din", ["w_out", "w_uq", "w_uk", "w_uv", "w_in"]),
           ("ffn1_dwg", ["ffn1_w_down"]), ("ffn1_dwu", ["ffn1_w_gate"]), ("ffn1_din_a", ["ffn1_w_up"])]
CHIPS_AT = [("attn_bwd", ["ffn2_w_down", "ffn2_w_gate"]), ("mla_prep_bwd", ["ffn2_w_up"]),
            ("ffn1_dact", ["w_out", "w_uq", "w_uk", "w_uv", "w_in"]),
            ("ffn1_dwu", ["ffn1_w_down"]), ("ffn1_din_a", ["ffn1_w_gate"]), ("ffn1_din_b", ["ffn1_w_up"])]
SHARE_EARLY_GROUPS, SHARE_EARLY_AT = 3, "ffn1_dwd"
SMALL_EARLY_AT = "mix_din"


def _same_shape_groups(names):
    return [[nm for nm in grp if nm in names] for grp in BIG_GROUPS if any(nm in names for nm in grp)]


class _Sched:
    def __init__(self, place, w, slots):
        self.place, self.w, self.slots = place, w, slots
        self.g = None
        self.sharded, self.from_pair, self.chip_bf16, self.from_chips = {}, {}, {}, {}
        self.early = self.early_all = None
        self.shared = {}

    def host(self, stage):
        comms = []
        if stage in GATHER_AT:
            comms.append(self.gather(GATHER_AT[stage]))
        comms += [self.chips(names) for at, names in CHIPS_AT if at == stage]
        comms += [self.pair(names) for at, names in PAIR_AT if at == stage]
        if stage == SMALL_EARLY_AT:
            comms.append(self.small_early())
        if stage == SHARE_EARLY_AT:
            comms.append(self.share_early())
        return _join_comms(comms)

    def small_early(self):
        self.early = _early_pack(self.g)

        def deliver(outs):
            self.early_all = outs[0]
            return outs

        return _small_comm(self.early, deliver)

    def gather(self, names):
        def deliver(outs):
            self.w.update({nm: _weight_from(nm, o) for nm, o in zip(names, outs)})
            return outs

        return _gather_comm([self.slots[nm] for nm in names], deliver)

    def pair(self, names):
        self.sharded.update({nm: _shard_grad(nm, self.g[nm]) for nm in names})

        def deliver(outs):
            self.from_pair.update(zip(names, outs))
            for grp in _same_shape_groups(names):
                sums = _pair_sum_call("pair_sum_" + grp[0], self.place, [self.sharded[nm] for nm in grp],
                                      [self.from_pair[nm] for nm in grp])
                self.chip_bf16.update(zip(grp, sums))
            return outs

        return _reduce_pair_comm([self.sharded[nm] for nm in names], deliver)

    def chips(self, names):
        def deliver(outs):
            self.from_chips.update(zip(names, outs))
            return outs

        return _reduce_chips_comm([self.chip_bf16[nm] for nm in names], deliver)

    def chip_sums(self, names, comm=None):
        out = {}
        for grp in _same_shape_groups(names):
            sums = _chip_sum_call("chip_sum_" + grp[0], self.place, [self.sharded[nm] for nm in grp],
                                  [self.from_pair[nm] for nm in grp], [self.from_chips[nm] for nm in grp], comm)
            comm = None
            out.update(zip(grp, sums))
        return out

    def share_early(self):
        names = [nm for at, grp in CHIPS_AT[:SHARE_EARLY_GROUPS] for nm in grp]
        mine = self.chip_sums(names)
        return _share_pair_comm([mine[nm] for nm in names], lambda o: self.shared.update(zip(names, o)))


def kernel(x, meta_tokens, ffn1_norm, ffn1_w_gate, ffn1_w_up, ffn1_w_down, mix_norm, w_in, q_latent_norm, w_uq, kv_latent_norm, w_uk, w_uv, q_head_norm, k_head_norm, conv_w, conv_b, gate_a_w, gate_a_b, gate_x_w, gate_x_b, lru_lambda, attn_out_norm, lru_out_norm, w_out, ffn2_norm, ffn2_w_gate, ffn2_w_up, ffn2_w_down, final_norm, loss_target, m_meta_tokens, m_ffn1_norm, m_ffn1_w_gate, m_ffn1_w_up, m_ffn1_w_down, m_mix_norm, m_w_in, m_q_latent_norm, m_w_uq, m_kv_latent_norm, m_w_uk, m_w_uv, m_q_head_norm, m_k_head_norm, m_conv_w, m_conv_b, m_gate_a_w, m_gate_a_b, m_gate_x_w, m_gate_x_b, m_lru_lambda, m_attn_out_norm, m_lru_out_norm, m_w_out, m_ffn2_norm, m_ffn2_w_gate, m_ffn2_w_up, m_ffn2_w_down, m_final_norm, v_meta_tokens, v_ffn1_norm, v_ffn1_w_gate, v_ffn1_w_up, v_ffn1_w_down, v_mix_norm, v_w_in, v_q_latent_norm, v_w_uq, v_kv_latent_norm, v_w_uk, v_w_uv, v_q_head_norm, v_k_head_norm, v_conv_w, v_conv_b, v_gate_a_w, v_gate_a_b, v_gate_x_w, v_gate_x_b, v_lru_lambda, v_attn_out_norm, v_lru_out_norm, v_w_out, v_ffn2_norm, v_ffn2_w_gate, v_ffn2_w_up, v_ffn2_w_down, v_final_norm):
    args = locals()
    p = {nm: args[nm] for nm in WEIGHT_NAMES}
    mom = {nm: args["m_" + nm] for nm in WEIGHT_NAMES}
    var = {nm: args["v_" + nm] for nm in WEIGHT_NAMES}
    nb, seq, d = x.shape
    lp = CHUNK + seq
    xi, yi, ci = lax.axis_index("x"), lax.axis_index("y"), lax.axis_index("c")
    chip = 2 * xi + yi

    place = jnp.stack([chip, ci]).astype(jnp.int32)
    p2 = {nm: _to2d(nm, p[nm]) for nm in BIG_NAMES}
    m2 = {nm: _to2d(nm, mom[nm]) for nm in BIG_NAMES}
    v2 = {nm: _to2d(nm, var[nm]) for nm in BIG_NAMES}

    slots = {}
    small_shard = jnp.concatenate(
        [meta_tokens, conv_w[0].reshape(2, 2 * LRU_TILE), jnp.zeros((14, 2 * LRU_TILE), F32)], axis=0)
    small_slots = lax.dynamic_update_slice(jnp.zeros((N_SHARD,) + small_shard.shape, F32), small_shard[None],
                                           (chip, 0, 0))
    first = []
    comm = _gather_comm([small_slots], first.extend)
    for grp in BIG_GROUPS:
        for nm, buf in zip(grp, _cast_call("cast_" + grp[0], place, [p2[nm] for nm in grp], comm)):
            slots[nm] = buf
        comm = None
    w, meta = _small_weights(p, first[0])
    sched = _Sched(place, w, slots)

    h0 = jnp.concatenate(
        [jnp.zeros((nb, PAD_ROWS, d), F32), jnp.broadcast_to(meta[None], (nb, N_META, d)), x], axis=1)
    target = jnp.pad(loss_target, ((0, 0), (CHUNK, 0), (0, 0)))
    loss_part, dh0, g = _local_step(h0.reshape(nb * lp, d), target.reshape(nb * lp, d), w, nb, lp, sched)
    dh0 = dh0.reshape(nb, lp, d)
    grad_x = dh0[:, CHUNK:, :]

    late = jnp.concatenate([g["ffn1_norm"], g["mix_norm"], jnp.zeros((6, D_MODEL), F32),
                            jnp.sum(dh0[:, PAD_ROWS:CHUNK, :], axis=0)], axis=0)
    shared = sched.shared
    rest = [nm for at, names in CHIPS_AT[SHARE_EARLY_GROUPS:] if at is not None for nm in names]
    last = [nm for at, names in CHIPS_AT if at is None for nm in names]
    late_box = {}
    mine = sched.chip_sums(rest, _small_comm(late, lambda o: late_box.update(all=o[0])))
    share = _share_pair_comm([mine[nm] for nm in rest], lambda o: shared.update(zip(rest, o)))
    _comm_call("share_pair", _join_comms([share, sched.chips(last) if last else None]))
    if last:
        mine = sched.chip_sums(last)
        _comm_call("share_last", _share_pair_comm([mine[nm] for nm in last], lambda o: shared.update(zip(last, o))))
    late_all = late_box["all"]
    small_like = {nm: p[nm] for nm in SMALL_NAMES + ["gate_a_w", "gate_x_w"]}

    def pack_w(t):
        tt = {nm: t[nm] for nm in SMALL_NAMES + ["gate_a_w", "gate_x_w"]}
        tt["conv_w"] = jnp.zeros((CONV_K, LRU_W), F32)
        return _pack_small(tt, SMALL_ADAM_ROWS)

    me = (4 * xi + 2 * yi + ci).astype(jnp.int32).reshape(1)
    gsum, dsm, msm, vsm = _small_update_call(me, sched.early_all, sched.early, late_all, late, pack_w(p),
                                             pack_w(mom), pack_w(var))
    grads = _unpack_small(gsum, small_like)
    delta = _unpack_small(dsm, small_like)
    new_m = _unpack_small(msm, small_like)
    new_v = _unpack_small(vsm, small_like)
    loss = gsum[ROW_LOSS, 0]
    gmeta = gsum[ROW_META:ROW_META + N_META].reshape(N_META, N_SHARD, D_MODEL // N_SHARD)
    grads["meta_tokens"] = lax.dynamic_index_in_dim(gmeta, chip, axis=1, keepdims=False)
    gconv = gsum[ROW_CONV_W:ROW_CONV_W + 2].reshape(CONV_K, N_SHARD, LRU_TILE)
    grads["conv_w"] = lax.dynamic_index_in_dim(gconv, chip, axis=1, keepdims=False)[None]
    for nm in ("meta_tokens", "conv_w"):
        delta[nm], new_m[nm], new_v[nm] = _adamw_call("adamw_" + nm, p[nm], grads[nm], mom[nm], var[nm])

    for names in BIG_GROUPS:
        res = _adamw_group_call("adamw_" + names[0], [p2[nm] for nm in names], [shared[nm] for nm in names],
                                [m2[nm] for nm in names], [v2[nm] for nm in names])
        for nm, (gg, dd, mm, vv) in zip(names, res):
            grads[nm], delta[nm], new_m[nm], new_v[nm] = (_from2d(nm, t) for t in (gg, dd, mm, vv))

    return (loss, grad_x, *[grads[nm] for nm in WEIGHT_NAMES], *[delta[nm] for nm in WEIGHT_NAMES],
            *[new_m[nm] for nm in WEIGHT_NAMES], *[new_v[nm] for nm in WEIGHT_NAMES])
```

```python
import functools
import math

import jax
import jax.numpy as jnp
import numpy as np
from jax import lax
from jax.experimental import pallas as pl
from jax.experimental.pallas import tpu as pltpu

F32 = jnp.float32
BF16 = jnp.bfloat16
MESH = pl.DeviceIdType.MESH

D_MODEL = 1024
N_META = 16
CHUNK = 64
PAD_ROWS = CHUNK - N_META
HEADS = 4
D_NOPE = 128
D_ROPE = 64
D_QK = D_NOPE + D_ROPE
D_QKP = 256
D_V = 128
KV_RANK = 256
Q_RANK = 384
MLA_W = HEADS * D_V
LRU_W = 512
LRU_TILE = 128
N_LRU_TILES = LRU_W // LRU_TILE
CONV_K = 4
C_RGLRU = 8.0
ROPE_THETA = 10000.0
D_FF = 2816
N_SHARD = 4
EPS = 1e-6
NEG_INF = -1e30
Z_KR = Q_RANK + KV_RANK
Z_MLA = Z_KR + 128
Z_U = Z_MLA
Z_G = Z_U + LRU_W
Z_W = Z_G + LRU_W
IN_WIDTH = Q_RANK + KV_RANK + D_ROPE + 2 * LRU_W

ADAM_LR = 0.001
ADAM_B1 = 0.9
ADAM_B2 = 0.999
ADAM_EPS = 1e-08
ADAM_WD = 0.01
ADAM_STEP = 10

VMEM_LIMIT_BYTES = 56 * 1024 * 1024
SMALL_ROWS = 104
SMALL_ADAM_ROWS = 80


def _params(sem):
    return pltpu.CompilerParams(dimension_semantics=sem, vmem_limit_bytes=VMEM_LIMIT_BYTES)


def _resident(shape):
    return pl.BlockSpec(tuple(shape), lambda i: (0,) * len(shape), pipeline_mode=pl.Buffered(1))


def _row_tile(rows, target):
    best = 16
    for t in range(16, min(rows, target) + 1, 16):
        if rows % t == 0:
            best = t
    return best


def _col_tile(cols, target):
    best = cols
    for t in range(128, min(cols, target) + 1, 128):
        if cols % t == 0:
            best = t
    return best


def _dot(a, b):
    return jnp.dot(a, b, preferred_element_type=F32)


def _dot_nt(a, b):
    return lax.dot_general(a, b, (((1,), (1,)), ((), ())), preferred_element_type=F32)


def _dot_tn(a, b):
    return lax.dot_general(a, b, (((0,), (0,)), ((), ())), preferred_element_type=F32)


def _rms(x, n):
    return lax.rsqrt(jnp.sum(x * x, axis=-1, keepdims=True) * (1.0 / n) + EPS)


def _rms_bwd(dn, nrm, r, n):
    return r * (dn - nrm * (jnp.sum(dn * nrm, axis=-1, keepdims=True) * (1.0 / n)))


def _gelu(x):
    k = math.sqrt(2.0 / math.pi)
    t = jnp.tanh(k * (x + 0.044715 * x * x * x))
    return 0.5 * x * (1.0 + t), t


def _gelu_grad(x, t):
    k = math.sqrt(2.0 / math.pi)
    return 0.5 * (1.0 + t) + 0.5 * x * (1.0 - t * t) * k * (1.0 + 3.0 * 0.044715 * x * x)


def _sigmoid(x):
    return 0.5 + 0.5 * jnp.tanh(0.5 * x)


def _softplus_neg(lam):
    e = jnp.exp(-jnp.abs(lam))
    log1p = jnp.where(e < 0.01, e * (1.0 - e * (0.5 - e * (1.0 / 3 - e * 0.25))), jnp.log(1.0 + e))
    return jnp.maximum(-lam, 0.0) + log1p


def _rope(t, c, s1, s2):
    return t * c + pltpu.roll(t, 96, 1) * s1 + pltpu.roll(t, 32, 1) * s2


def _rope_t(d, c, s1, s2):
    return d * c + pltpu.roll(d * s1, 32, 1) + pltpu.roll(d * s2, 96, 1)


def _rope_tables(lp):
    pos = (np.arange(lp, dtype=np.int32) - PAD_ROWS).astype(np.float32)
    inv_freq = (ROPE_THETA ** (-np.arange(0, D_ROPE // 2, dtype=np.float32) / (D_ROPE // 2))).astype(np.float32)
    ang = (pos[:, None] * inv_freq[None, :]).astype(np.float32).astype(np.float64)
    cos, sin = np.cos(ang).astype(np.float32), np.sin(ang).astype(np.float32)
    z = np.zeros_like(cos)
    return (jnp.asarray(np.concatenate([cos, cos, z, z], 1)), jnp.asarray(np.concatenate([-sin, z, z, z], 1)),
            jnp.asarray(np.concatenate([z, sin, z, z], 1)))


def _rmsnorm_call(name, h, g, tm, comm=None):
    rows, d = h.shape

    def body(h_ref, g_ref, o_ref):
        x = h_ref[...]
        o_ref[...] = (x * _rms(x, d) * g_ref[...]).astype(BF16)

    return _hosted_call(
        body, name=name, grid=(rows // tm,),
        in_specs=[pl.BlockSpec((tm, d), lambda i: (i, 0)), pl.BlockSpec((1, d), lambda i: (0, 0))],
        out_specs=[pl.BlockSpec((tm, d), lambda i: (i, 0))],
        out_shape=[jax.ShapeDtypeStruct((rows, d), BF16)],
        dims=("parallel",), args=(h, g), comm=comm)[0]


def _ffn_up_call(name, u, wg, wu, tm, comm=None):
    rows, d = u.shape
    ns, fs, _ = wg.shape

    def body(u_ref, wg_ref, wu_ref, g_ref, p_ref, a_ref):
        uu = u_ref[...]
        g = _dot_nt(uu, wg_ref[0])
        p = _dot_nt(uu, wu_ref[0])
        g_ref[0] = g.astype(BF16)
        p_ref[0] = p.astype(BF16)
        a_ref[0] = (g * jax.nn.sigmoid(g) * p).astype(BF16)

    wspec = pl.BlockSpec((1, fs, d), lambda s, i: (s, 0, 0))
    ospec = pl.BlockSpec((1, tm, fs), lambda s, i: (s, i, 0))
    oshape = jax.ShapeDtypeStruct((ns, rows, fs), BF16)
    return _hosted_call(
        body, name=name, grid=(ns, rows // tm),
        in_specs=[pl.BlockSpec((tm, d), lambda s, i: (i, 0)), wspec, wspec],
        out_specs=[ospec, ospec, ospec], out_shape=[oshape, oshape, oshape],
        dims=("parallel", "parallel"), args=(u, wg, wu), comm=comm)


def _ffn_gate_call(name, u, wg, tm, comm=None):
    rows, d = u.shape
    ns, fs, _ = wg.shape

    def body(u_ref, wg_ref, g_ref):
        g_ref[0] = _dot_nt(u_ref[...], wg_ref[0]).astype(BF16)

    return _hosted_call(
        body, name=name, grid=(ns, rows // tm),
        in_specs=[pl.BlockSpec((tm, d), lambda s, i: (i, 0)), pl.BlockSpec((1, fs, d), lambda s, i: (s, 0, 0))],
        out_specs=[pl.BlockSpec((1, tm, fs), lambda s, i: (s, i, 0))],
        out_shape=[jax.ShapeDtypeStruct((ns, rows, fs), BF16)],
        dims=("parallel", "parallel"), args=(u, wg), comm=comm)[0]


def _ffn_upact_call(name, u, wu, gate, tm, comm=None):
    rows, d = u.shape
    ns, fs, _ = wu.shape

    def body(u_ref, wu_ref, g_ref, p_ref, a_ref):
        p = _dot_nt(u_ref[...], wu_ref[0])
        g = g_ref[0].astype(F32)
        p_ref[0] = p.astype(BF16)
        a_ref[0] = (g * jax.nn.sigmoid(g) * p).astype(BF16)

    ospec = pl.BlockSpec((1, tm, fs), lambda s, i: (s, i, 0))
    oshape = jax.ShapeDtypeStruct((ns, rows, fs), BF16)
    return _hosted_call(
        body, name=name, grid=(ns, rows // tm),
        in_specs=[pl.BlockSpec((tm, d), lambda s, i: (i, 0)), pl.BlockSpec((1, fs, d), lambda s, i: (s, 0, 0)), ospec],
        out_specs=[ospec, ospec], out_shape=[oshape, oshape],
        dims=("parallel", "parallel"), args=(u, wu, gate), comm=comm)


def _loss_tail(x, g, t, row, d):
    r = _rms(x, d)
    n = x * r
    err = jnp.where(row >= CHUNK, n * g - t, 0.0)
    dout = err * (1.0 / d)
    dh = _rms_bwd(dout * g, n, r, d)
    dg = jnp.sum(dout * n, axis=0, keepdims=True)
    part = jnp.sum(jnp.sum(err * err, axis=1, keepdims=True), axis=0, keepdims=True) * (0.5 / d)
    return dh, dg, jnp.broadcast_to(part, (1, 128))


def _ffn_down_call(name, a, wd, h, tm, comm=None, next_gain=None):
    rows, d = h.shape
    ns, _, fs = a.shape
    more = next_gain is not None

    def body(a_ref, wd_ref, h_ref, *rest):
        acc = h_ref[...]
        for s in range(ns):
            acc = acc + 0.5 * _dot(a_ref[s], wd_ref[s])
        rest[-2 if more else -1][...] = acc
        if more:
            rest[-1][...] = (acc * _rms(acc, d) * rest[0][...]).astype(BF16)

    full = pl.BlockSpec((tm, d), lambda i: (i, 0))
    res = _hosted_call(
        body, name=name, grid=(rows // tm,),
        in_specs=[pl.BlockSpec((ns, tm, fs), lambda i: (0, i, 0)), _resident((ns, fs, d)), full]
        + ([pl.BlockSpec((1, d), lambda i: (0, 0))] if more else []),
        out_specs=[full] * (2 if more else 1),
        out_shape=[jax.ShapeDtypeStruct((rows, d), F32)] + ([jax.ShapeDtypeStruct((rows, d), BF16)] if more else []),
        dims=("parallel",), args=(a, wd, h) + ((next_gain,) if more else ()), comm=comm)
    return res if more else res[0]


def _ffn_down_loss_call(name, a, wd, h, g, target, lp, tm):
    rows, d = h.shape
    ns, _, fs = a.shape
    tpe = lp // tm

    def body(a_ref, wd_ref, h_ref, g_ref, t_ref, dh_ref, dhb_ref, dg_ref, loss_ref):
        i = pl.program_id(0)
        acc = h_ref[...]
        for s in range(ns):
            acc = acc + 0.5 * _dot(a_ref[s], wd_ref[s])
        row = (i % tpe) * tm + lax.broadcasted_iota(jnp.int32, (tm, 1), 0)
        dh, dg, loss = _loss_tail(acc, g_ref[...], t_ref[...], row, d)
        dh_ref[...] = dh
        dhb_ref[...] = dh.astype(BF16)

        @pl.when(i == 0)
        def _():
            dg_ref[...] = dg
            loss_ref[...] = loss

        @pl.when(i != 0)
        def _():
            dg_ref[...] += dg
            loss_ref[...] += loss

    full = pl.BlockSpec((tm, d), lambda i: (i, 0))
    gspec = pl.BlockSpec((1, d), lambda i: (0, 0))
    return _hosted_call(
        body, name=name, grid=(rows // tm,),
        in_specs=[pl.BlockSpec((ns, tm, fs), lambda i: (0, i, 0)), _resident((ns, fs, d)), full, gspec, full],
        out_specs=[full, full, gspec, pl.BlockSpec((1, 128), lambda i: (0, 0))],
        out_shape=[jax.ShapeDtypeStruct((rows, d), F32), jax.ShapeDtypeStruct((rows, d), BF16),
                   jax.ShapeDtypeStruct((1, d), F32), jax.ShapeDtypeStruct((1, 128), F32)],
        args=(a, wd, h, g, target))


def _mm_call(name, a, bt, tm, out_dtype):
    rows, k = a.shape
    n = bt.shape[0]

    def body(a_ref, b_ref, o_ref):
        o_ref[...] = _dot_nt(a_ref[...], b_ref[...]).astype(out_dtype)

    return pl.pallas_call(
        body, name=name, grid=(rows // tm,),
        in_specs=[pl.BlockSpec((tm, k), lambda i: (i, 0)), pl.BlockSpec((n, k), lambda i: (0, 0))],
        out_specs=pl.BlockSpec((tm, n), lambda i: (i, 0)),
        out_shape=jax.ShapeDtypeStruct((rows, n), out_dtype),
        compiler_params=_params(("parallel",)))(a, bt)


def _mla_heads(z, gql, gkvl, wuq, wuk, wuv):
    cq = z[:, 0:Q_RANK]
    ckv = z[:, Q_RANK:Z_KR]
    kr = z[:, Z_KR:Z_MLA]
    rq = _rms(cq, Q_RANK)
    nq = cq * rq
    cqn = (nq * gql).astype(BF16)
    rkv = _rms(ckv, KV_RANK)
    nkv = ckv * rkv
    ckvn = (nkv * gkvl).astype(BF16)
    qraw = _dot_nt(cqn, wuq)
    knope = _dot(ckvn, wuk)
    v = _dot(ckvn, wuv)
    skr = jnp.sum(kr * kr, axis=-1, keepdims=True)
    heads = []
    for hd in range(HEADS):
        qh = qraw[:, hd * D_QKP:(hd + 1) * D_QKP]
        rqh = lax.rsqrt(jnp.sum(qh * qh, axis=-1, keepdims=True) * (1.0 / D_QK) + EPS)
        kn = knope[:, hd * D_NOPE:(hd + 1) * D_NOPE]
        rkh = lax.rsqrt((jnp.sum(kn * kn, axis=-1, keepdims=True) + skr) * (1.0 / D_QK) + EPS)
        heads.append((qh * rqh, rqh, kn * rkh, kr * rkh, rkh))
    return dict(rq=rq, nq=nq, cqn=cqn, rkv=rkv, nkv=nkv, ckvn=ckvn, v=v, heads=heads)


def _mla_prep_call(z, gql, gkvl, gqh, gkh, wuq, wuk, wuv, tabs, lp, tm):
    rows = z.shape[0]
    tpe = lp // tm

    def body(z_ref, gql_ref, gkvl_ref, gqh_ref, gkh_ref, wuq_ref, wuk_ref, wuv_ref, c_ref, s1_ref, s2_ref,
             q_ref, k_ref, v_ref):
        m = _mla_heads(z_ref[...], gql_ref[...], gkvl_ref[...], wuq_ref[...], wuk_ref[...], wuv_ref[...])
        c, s1, s2 = c_ref[...], s1_ref[...], s2_ref[...]
        gq, gk = gqh_ref[...], gkh_ref[...]
        row = (pl.program_id(0) % tpe) * tm + lax.broadcasted_iota(jnp.int32, (tm, 1), 0)
        spare = (lax.broadcasted_iota(jnp.int32, (1, D_QKP - D_NOPE), 1) == D_ROPE).astype(F32)
        kmask = jnp.where(row < PAD_ROWS, NEG_INF * math.sqrt(D_QK), 0.0) * spare
        for hd in range(HEADS):
            qn, _, knn, krn, _ = m["heads"][hd]
            qg = qn * gq
            q_ref[hd, :, 0:D_NOPE] = qg[:, 0:D_NOPE].astype(BF16)
            q_ref[hd, :, D_NOPE:D_QKP] = (_rope(qg[:, D_NOPE:D_QKP], c, s1, s2) + spare).astype(BF16)
            k_ref[hd, :, 0:D_NOPE] = (knn * gk[:, 0:D_NOPE]).astype(BF16)
            k_ref[hd, :, D_NOPE:D_QKP] = (_rope(krn * gk[:, D_NOPE:D_QKP], c, s1, s2) + kmask).astype(BF16)
            v_ref[hd] = m["v"][:, hd * D_V:(hd + 1) * D_V].astype(BF16)

    def const(shape):
        return pl.BlockSpec(shape, lambda i: tuple(0 for _ in shape))

    tab = pl.BlockSpec((tm, 128), lambda i: (i % tpe, 0))
    return pl.pallas_call(
        body, name="mla_prep", grid=(rows // tm,),
        in_specs=[pl.BlockSpec((tm, Z_MLA), lambda i: (i, 0)), const((1, Q_RANK)), const((1, KV_RANK)),
                  const((1, D_QKP)), const((1, D_QKP)), const((HEADS * D_QKP, Q_RANK)),
                  const((KV_RANK, HEADS * D_NOPE)), const((KV_RANK, HEADS * D_V)), tab, tab, tab],
        out_specs=[pl.BlockSpec((HEADS, tm, D_QKP), lambda i: (0, i, 0)),
                   pl.BlockSpec((HEADS, tm, D_QKP), lambda i: (0, i, 0)),
                   pl.BlockSpec((HEADS, tm, D_V), lambda i: (0, i, 0))],
        out_shape=[jax.ShapeDtypeStruct((HEADS, rows, D_QKP), BF16),
                   jax.ShapeDtypeStruct((HEADS, rows, D_QKP), BF16),
                   jax.ShapeDtypeStruct((HEADS, rows, D_V), BF16)],
        compiler_params=_params(("parallel",)))(z, gql, gkvl, gqh, gkh, wuq, wuk, wuv, *tabs)


Q_BLOCK_ROWS = 528
Q_BLOCK_ROWS_BWD = 192


def _q_block(lp):
    return _row_tile(lp, Q_BLOCK_ROWS)


def _key_end(ext, lp):
    return min(lp, -(-ext // CHUNK) * CHUNK)


def _diag_bias(qb, j0, nk):
    shift = CHUNK.bit_length() - 1
    r = jnp.right_shift(j0 + lax.broadcasted_iota(jnp.int32, (qb, nk), 0), shift)
    c = jnp.right_shift(j0 + lax.broadcasted_iota(jnp.int32, (qb, nk), 1), shift)
    return jnp.where(c <= r, 0.0, NEG_INF)


def _attn_fwd_call(q, k, v, nb, lp, comm=None):
    rows = nb * lp
    qb = _q_block(lp)
    scale = 1.0 / math.sqrt(D_QK)

    def body(q_ref, k_ref, v_ref, o_ref, lse_ref):
        for j in range(lp // qb):
            j0, ext = j * qb, (j + 1) * qb
            kend = _key_end(ext, lp)
            qj = q_ref[0, j0:ext, :]
            sd = _dot_nt(qj, k_ref[0, j0:kend, :]) * scale + _diag_bias(qb, j0, kend - j0)
            mx = jnp.max(sd, axis=-1, keepdims=True)
            if j > 0:
                so = _dot_nt(qj, k_ref[0, 0:j0, :]) * scale
                mx = jnp.maximum(mx, jnp.max(so, axis=-1, keepdims=True))
            pd = jnp.exp(sd - mx)
            l = jnp.sum(pd, axis=-1, keepdims=True)
            o = _dot(pd.astype(BF16), v_ref[0, j0:kend, :])
            if j > 0:
                po = jnp.exp(so - mx)
                l = l + jnp.sum(po, axis=-1, keepdims=True)
                o = o + _dot(po.astype(BF16), v_ref[0, 0:j0, :])
            o_ref[j0:ext, :] = o / l
            lse_ref[0, j0:ext, :] = mx + jnp.log(l)

    return _hosted_call(
        body, name="attn_fwd", grid=(nb, HEADS),
        in_specs=[pl.BlockSpec((1, lp, D_QKP), lambda b, h: (h, b, 0)),
                  pl.BlockSpec((1, lp, D_QKP), lambda b, h: (h, b, 0)),
                  pl.BlockSpec((1, lp, D_V), lambda b, h: (h, b, 0))],
        out_specs=[pl.BlockSpec((lp, D_V), lambda b, h: (b, h)),
                   pl.BlockSpec((1, lp, 1), lambda b, h: (h, b, 0))],
        out_shape=[jax.ShapeDtypeStruct((rows, MLA_W), F32),
                   jax.ShapeDtypeStruct((HEADS, rows, 1), F32)],
        dims=("parallel", "parallel"), args=(q, k, v), comm=comm)


def _attn_bwd_call(q, k, v, o, lse, do, nb, lp, comm=None):
    rows = nb * lp
    qb = _row_tile(lp, Q_BLOCK_ROWS_BWD)
    scale = 1.0 / math.sqrt(D_QK)

    def body(q_ref, k_ref, v_ref, o_ref, lse_ref, do_ref, dq_ref, dk_ref, dv_ref, dk_acc, dv_acc):
        dk_acc[...] = jnp.zeros_like(dk_acc)
        dv_acc[...] = jnp.zeros_like(dv_acc)
        shift = CHUNK.bit_length() - 1
        for j in range(lp // qb):
            j0, ext = j * qb, (j + 1) * qb
            kend = _key_end(ext, lp)
            qj = q_ref[0, j0:ext, :]
            doj = do_ref[j0:ext, :]
            delta = jnp.sum(doj * o_ref[j0:ext, :], axis=-1, keepdims=True)
            dob = doj.astype(BF16)
            kk = k_ref[0, 0:kend, :]
            qchunk = jnp.right_shift(j0 + lax.broadcasted_iota(jnp.int32, (qb, kend), 0), shift)
            kchunk = jnp.right_shift(lax.broadcasted_iota(jnp.int32, (qb, kend), 1), shift)
            s = _dot_nt(qj, kk) * scale - lse_ref[0, j0:ext, :]
            p = jnp.where(kchunk <= qchunk, jnp.exp(s), 0.0)
            dv_acc[0:kend, :] += _dot_tn(p.astype(BF16), dob)
            dp = _dot_nt(dob, v_ref[0, 0:kend, :])
            ds = (p * (dp - delta) * scale).astype(BF16)
            dq_ref[0, j0:ext, :] = _dot(ds, kk).astype(BF16)
            dk_acc[0:kend, :] += _dot_tn(ds, qj)
        dk_ref[0] = dk_acc[...].astype(BF16)
        dv_ref[0] = dv_acc[...].astype(BF16)

    qspec = pl.BlockSpec((1, lp, D_QKP), lambda b, h: (h, b, 0))
    vspec = pl.BlockSpec((1, lp, D_V), lambda b, h: (h, b, 0))
    ospec = pl.BlockSpec((lp, D_V), lambda b, h: (b, h))
    return _hosted_call(
        body, name="attn_bwd", grid=(nb, HEADS),
        in_specs=[qspec, qspec, vspec, ospec, pl.BlockSpec((1, lp, 1), lambda b, h: (h, b, 0)), ospec],
        out_specs=[qspec, qspec, vspec],
        out_shape=[jax.ShapeDtypeStruct((HEADS, rows, D_QKP), BF16),
                   jax.ShapeDtypeStruct((HEADS, rows, D_QKP), BF16),
                   jax.ShapeDtypeStruct((HEADS, rows, D_V), BF16)],
        scratch_shapes=[pltpu.VMEM((lp, D_QKP), F32), pltpu.VMEM((lp, D_V), F32)],
        dims=("parallel", "parallel"), args=(q, k, v, o, lse, do), comm=comm)


def _lru_gates(u, cw, cb, wa, ba, wx, bx, lam, lp):
    xc = (cw[3:4, :] * u + cw[2:3, :] * pltpu.roll(u, 1, 0) + cw[1:2, :] * pltpu.roll(u, 2, 0)
          + cw[0:1, :] * pltpu.roll(u, 3, 0) + cb)
    xcb = xc.astype(BF16)
    r = _sigmoid(_dot(xcb, wa) + ba)
    i = _sigmoid(_dot(xcb, wx) + bx)
    sp = _softplus_neg(lam)
    la = -C_RGLRU * r * sp
    a = jnp.exp(la)
    x2 = 2.0 * la
    e2 = a * a
    m2 = jnp.maximum(jnp.where(x2 > -0.01, -x2 * (1.0 + 0.5 * x2), 1.0 - e2), 1e-30)
    rs = lax.rsqrt(m2)
    row = lax.broadcasted_iota(jnp.int32, (lp, LRU_TILE), 0)
    first = row == PAD_ROWS
    valid = row >= PAD_ROWS
    mult_eff = jnp.where(first, 1.0, m2 * rs)
    return dict(xc=xc, xcb=xcb, r=r, i=i, sp=sp, a=a, e2=e2, rs=rs, mult_eff=mult_eff, first=first, valid=valid)


def _scan_rows(a, b, a_s, b_s, out_ref, lp, reverse):
    sub = lax.broadcasted_iota(jnp.int32, (lp, LRU_TILE), 0) & 7
    for dist in (1, 2, 4):
        shift = lp - dist if reverse else dist
        keep = (sub + dist <= 7) if reverse else (sub >= dist)
        a_sh = pltpu.roll(a, shift, 0)
        b_sh = pltpu.roll(b, shift, 0)
        b = jnp.where(keep, a * b_sh + b, b)
        a = jnp.where(keep, a * a_sh, a)
    a_s[...] = a
    b_s[...] = b
    n_groups = lp // 8
    edge = 0 if reverse else 7

    def group(gi, carry):
        r0 = pl.multiple_of(((n_groups - 1 - gi) if reverse else gi) * 8, 8)
        a8 = a_s[pl.ds(r0, 8), :]
        b8 = b_s[pl.ds(r0, 8), :]
        out_ref[pl.ds(r0, 8), :] = a8 * carry + b8
        return a8[edge:edge + 1, :] * carry + b8[edge:edge + 1, :]

    lax.fori_loop(0, n_groups, group, jnp.zeros((1, LRU_TILE), F32), unroll=4)


def _lru_specs(lp):
    seq = lambda col0, stride=1: pl.BlockSpec((lp, LRU_TILE), lambda t, b: (b, col0 + stride * t))
    cw = pl.BlockSpec((1, CONV_K, LRU_TILE), lambda t, b: (t, 0, 0))
    vec = pl.BlockSpec((1, LRU_TILE), lambda t, b: (0, t))
    mat = pl.BlockSpec((1, LRU_TILE, LRU_TILE), lambda t, b: (t, 0, 0))
    return seq, cw, vec, mat


def _lru_fwd_call(z, cw, cb, wa, ba, wx, bx, lam, nb, lp, comm=None):
    rows = nb * lp
    seq, cwspec, vec, mat = _lru_specs(lp)

    def body(u_ref, g_ref, cw_ref, cb_ref, wa_ref, ba_ref, wx_ref, bx_ref, lam_ref, y_ref, hs_ref, a_s, b_s):
        m = _lru_gates(u_ref[...], cw_ref[0], cb_ref[...], wa_ref[0], ba_ref[...], wx_ref[0], bx_ref[...],
                       lam_ref[...], lp)
        a = jnp.where(m["valid"], m["a"], 0.0)
        b = jnp.where(m["valid"], m["mult_eff"] * (m["i"] * m["xc"]), 0.0)
        _scan_rows(a, b, a_s, b_s, hs_ref, lp, reverse=False)
        gl, _ = _gelu(g_ref[...])
        y_ref[...] = hs_ref[...] * gl

    oshape = jax.ShapeDtypeStruct((rows, LRU_W), F32)
    return _hosted_call(
        body, name="lru_fwd", grid=(N_LRU_TILES, nb),
        in_specs=[seq(Z_U // LRU_TILE, 2), seq(Z_U // LRU_TILE + 1, 2), cwspec, vec, mat, vec, mat, vec, vec],
        out_specs=[seq(0), seq(0)], out_shape=[oshape, oshape],
        scratch_shapes=[pltpu.VMEM((lp, LRU_TILE), F32), pltpu.VMEM((lp, LRU_TILE), F32)],
        dims=("parallel", "parallel"), args=(z, z, cw, cb, wa, ba, wx, bx, lam), comm=comm)


def _lru_bwd_call(z, dz, hs, dy, cw, cb, wa, ba, wx, bx, lam, nb, lp, comm=None):
    rows = nb * lp
    seq, cwspec, vec, mat = _lru_specs(lp)

    def body(u_ref, g_ref, hs_ref, dy_ref, cw_ref, cb_ref, wa_ref, ba_ref, wx_ref, bx_ref, lam_ref, dz_in,
             dz_ref, dcw_ref, dcb_ref, dwa_ref, dba_ref, dwx_ref, dbx_ref, dlam_ref, a_s, b_s, d_s):
        du_ref = dz_ref.at[:, 0:LRU_TILE]
        dg_ref = dz_ref.at[:, LRU_TILE:2 * LRU_TILE]
        b_idx = pl.program_id(1)
        u = u_ref[...]
        cw = cw_ref[0]
        wa, wx = wa_ref[0], wx_ref[0]
        lam = lam_ref[...]
        m = _lru_gates(u, cw, cb_ref[...], wa, ba_ref[...], wx, bx_ref[...], lam, lp)
        gate = g_ref[...]
        gl, th = _gelu(gate)
        dy = dy_ref[...]
        hs = hs_ref[...]
        dg_ref[...] = (dy * hs * _gelu_grad(gate, th)).astype(BF16)
        a_eff = jnp.where(m["valid"], m["a"], 0.0)
        _scan_rows(pltpu.roll(a_eff, lp - 1, 0), dy * gl, a_s, b_s, d_s, lp, reverse=True)
        ds = d_s[...]
        xc, r, i = m["xc"], m["r"], m["i"]
        row = lax.broadcasted_iota(jnp.int32, (lp, LRU_TILE), 0)
        da = ds * jnp.where(row >= 1, pltpu.roll(hs, 1, 0), 0.0)
        db = jnp.where(m["valid"], ds, 0.0)
        di = db * m["mult_eff"] * xc
        dxc = db * m["mult_eff"] * i
        live = m["valid"] & jnp.logical_not(m["first"])
        dm = jnp.where(live, db * i * xc, 0.0)
        dla = da * m["a"] - dm * (m["e2"] * m["rs"])
        dr = dla * (-C_RGLRU * m["sp"])
        dsp = jnp.sum(dla * (-C_RGLRU * r), axis=0, keepdims=True)
        dpr = (dr * r * (1.0 - r))
        dpi = (di * i * (1.0 - i))
        dprb, dpib = dpr.astype(BF16), dpi.astype(BF16)
        dxc = dxc + _dot_nt(dprb, wa) + _dot_nt(dpib, wx)
        du = (cw[3:4, :] * dxc + cw[2:3, :] * pltpu.roll(dxc, lp - 1, 0) + cw[1:2, :] * pltpu.roll(dxc, lp - 2, 0)
              + cw[0:1, :] * pltpu.roll(dxc, lp - 3, 0))
        du_ref[...] = jnp.where(m["valid"], du, 0.0).astype(BF16)
        tap = lax.broadcasted_iota(jnp.int32, (CONV_K, LRU_TILE), 0)
        dcw = jnp.zeros((CONV_K, LRU_TILE), F32)
        for kk in range(CONV_K):
            shifted = u if kk == CONV_K - 1 else pltpu.roll(u, CONV_K - 1 - kk, 0)
            dcw = jnp.where(tap == kk, jnp.sum(dxc * shifted, axis=0, keepdims=True), dcw)
        parts = [(dcw_ref, dcw[None]), (dcb_ref, jnp.sum(dxc, axis=0, keepdims=True)[None]),
                 (dwa_ref, _dot_tn(m["xcb"], dprb)[None]), (dba_ref, jnp.sum(dpr, axis=0, keepdims=True)[None]),
                 (dwx_ref, _dot_tn(m["xcb"], dpib)[None]), (dbx_ref, jnp.sum(dpi, axis=0, keepdims=True)[None]),
                 (dlam_ref, (dsp * (-jax.nn.sigmoid(-lam)))[None])]

        @pl.when(b_idx == 0)
        def _():
            for ref, val in parts:
                ref[...] = val

        @pl.when(b_idx != 0)
        def _():
            for ref, val in parts:
                ref[...] += val

    vec3 = pl.BlockSpec((1, 1, LRU_TILE), lambda t, b: (t, 0, 0))
    vshape = jax.ShapeDtypeStruct((N_LRU_TILES, 1, LRU_TILE), F32)
    mshape = jax.ShapeDtypeStruct((N_LRU_TILES, LRU_TILE, LRU_TILE), F32)
    pair = pl.BlockSpec((lp, 2 * LRU_TILE), lambda t, b: (b, Z_U // (2 * LRU_TILE) + t))
    return _hosted_call(
        body, name="lru_bwd", grid=(N_LRU_TILES, nb),
        in_specs=[seq(Z_U // LRU_TILE, 2), seq(Z_U // LRU_TILE + 1, 2), seq(0), seq(0), cwspec, vec, mat, vec, mat,
                  vec, vec, pl.BlockSpec(memory_space=pl.ANY)],
        out_specs=[pair, cwspec, vec3, mat, vec3, mat, vec3, vec3],
        out_shape=[jax.ShapeDtypeStruct(dz.shape, dz.dtype), jax.ShapeDtypeStruct((N_LRU_TILES, CONV_K, LRU_TILE), F32),
                   vshape, mshape, vshape, mshape, vshape, vshape],
        scratch_shapes=[pltpu.VMEM((lp, LRU_TILE), F32)] * 3,
        dims=("parallel", "arbitrary"), args=(z, z, hs, dy, cw, cb, wa, ba, wx, bx, lam, dz), comm=comm,
        aliases={11: 0})


def _mix_out_call(ya, yl, ga, gl, wout, h, next_gain, tm):
    rows, d = h.shape

    def body(ya_ref, yl_ref, ga_ref, gl_ref, w_ref, h_ref, ng_ref, y_ref, o_ref, u_ref):
        a = ya_ref[...]
        l = yl_ref[...]
        an = (a * _rms(a, MLA_W) * ga_ref[...]).astype(BF16)
        ln = (l * _rms(l, LRU_W) * gl_ref[...]).astype(BF16)
        y_ref[:, 0:MLA_W] = an
        y_ref[:, MLA_W:MLA_W + LRU_W] = ln
        out = h_ref[...] + _dot(an, w_ref[0:MLA_W, :]) + _dot(ln, w_ref[MLA_W:MLA_W + LRU_W, :])
        o_ref[...] = out
        u_ref[...] = (out * _rms(out, d) * ng_ref[...]).astype(BF16)

    half = pl.BlockSpec((tm, MLA_W), lambda i: (i, 0))
    g = pl.BlockSpec((1, MLA_W), lambda i: (0, 0))
    full = pl.BlockSpec((tm, d), lambda i: (i, 0))
    return pl.pallas_call(
        body, name="mix_out", grid=(rows // tm,),
        in_specs=[half, half, g, g, pl.BlockSpec((MLA_W + LRU_W, d), lambda i: (0, 0)), full,
                  pl.BlockSpec((1, d), lambda i: (0, 0))],
        out_specs=[full, full, full],
        out_shape=[jax.ShapeDtypeStruct((rows, MLA_W + LRU_W), BF16), jax.ShapeDtypeStruct((rows, d), F32),
                   jax.ShapeDtypeStruct((rows, d), BF16)],
        compiler_params=_params(("parallel",)))(ya, yl, ga, gl, wout, h, next_gain)


def _ffn_dact_call(name, dhb, wd, gate, up, tm, comm=None):
    rows, d = dhb.shape
    ns, fs, _ = wd.shape

    nsub = 2 if tm % 32 == 0 else 1
    sub = tm // nsub

    def body(dh_ref, wd_ref, g_ref, p_ref, dg_ref, dp_ref):
        wd = wd_ref[0]
        for r in range(nsub):
            rs = slice(r * sub, (r + 1) * sub)
            da = (0.5 * _dot_nt(dh_ref[rs, :], wd)).astype(BF16)
            g = g_ref[0, rs, :]
            p = p_ref[0, rs, :]
            sg = jax.nn.sigmoid(g)
            dg_ref[0, rs, :] = (da * p) * (sg * (1.0 + g * (1.0 - sg)))
            dp_ref[0, rs, :] = da * (g * sg)

    aspec = pl.BlockSpec((1, tm, fs), lambda s, i: (s, i, 0))
    oshape = jax.ShapeDtypeStruct((ns, rows, fs), BF16)
    return _hosted_call(
        body, name=name, grid=(ns, rows // tm),
        in_specs=[pl.BlockSpec((tm, d), lambda s, i: (i, 0)), pl.BlockSpec((1, fs, d), lambda s, i: (s, 0, 0)),
                  aspec, aspec],
        out_specs=[aspec, aspec], out_shape=[oshape, oshape],
        dims=("parallel", "parallel"), args=(dhb, wd, gate, up), comm=comm)


def _norm_in_bwd_call(name, pieces, h, g, dres, tm, comm=None, part=(0, 1), prev=None, mix=None):
    rows, d = h.shape
    npc = len(pieces)
    steps = rows // tm // part[1]
    off = part[0] * steps
    n_prev = 0 if prev is None else 2
    n_mix = 0 if mix is None else 5

    def body(*refs):
        d_refs = refs[0:2 * npc:2]
        w_refs = refs[1:2 * npc:2]
        h_ref, g_ref, dres_ref = refs[2 * npc:2 * npc + 3]
        mix_in_refs = refs[2 * npc + 3:2 * npc + 3 + n_mix]
        dh_ref, dhb_ref, dg_ref = refs[2 * npc + 3 + n_mix + n_prev:2 * npc + 6 + n_mix + n_prev]
        mix_out_refs = refs[2 * npc + 6 + n_mix + n_prev:]
        du = jnp.zeros((tm, d), F32)
        for d_ref, w_ref in zip(d_refs, w_refs):
            if len(d_ref.shape) == 3:
                for s in range(d_ref.shape[0]):
                    du = du + _dot(d_ref[s], w_ref[s])
            else:
                du = du + _dot(d_ref[...], w_ref[...])
        x = h_ref[...]
        r = _rms(x, d)
        n = x * r
        dh = dres_ref[...] + _rms_bwd(du * g_ref[...], n, r, d)
        dhb = dh.astype(BF16)
        dh_ref[...] = dh
        dhb_ref[...] = dhb
        sums = [(dg_ref, jnp.sum(du * n, axis=0, keepdims=True))]
        if mix is not None:
            wo_ref, ya_ref, yl_ref, ga_ref, gl_ref = mix_in_refs
            dya_ref, dyl_ref, dga_ref, dgl_ref = mix_out_refs
            dy = _dot_nt(dhb, wo_ref[...])
            for val, gain_ref, lo, out_ref, acc_ref in ((ya_ref[...], ga_ref, 0, dya_ref, dga_ref),
                                                        (yl_ref[...], gl_ref, MLA_W, dyl_ref, dgl_ref)):
                rb = _rms(val, MLA_W)
                nb_ = val * rb
                dyn = dy[:, lo:lo + MLA_W]
                out_ref[...] = _rms_bwd(dyn * gain_ref[...], nb_, rb, MLA_W)
                sums.append((acc_ref, jnp.sum(dyn * nb_, axis=0, keepdims=True)))

        @pl.when(pl.program_id(0) == 0)
        def _():
            for ref, val in sums:
                ref[...] = val

        @pl.when(pl.program_id(0) != 0)
        def _():
            for ref, val in sums:
                ref[...] += val

    in_specs, args = [], []
    for dd, w in pieces:
        if dd.ndim == 3:
            in_specs.append(pl.BlockSpec((dd.shape[0], tm, dd.shape[2]), lambda i: (0, i + off, 0)))
            in_specs.append(_resident(w.shape))
        else:
            in_specs.append(pl.BlockSpec((tm, dd.shape[1]), lambda i: (i + off, 0)))
            in_specs.append(_resident(w.shape))
        args += [dd, w]
    full = pl.BlockSpec((tm, d), lambda i: (i + off, 0))
    gspec = pl.BlockSpec((1, d), lambda i: (0, 0))
    half = pl.BlockSpec((tm, MLA_W), lambda i: (i + off, 0))
    hgain = pl.BlockSpec((1, MLA_W), lambda i: (0, 0))
    mix_in_specs = [] if mix is None else [_resident(mix[0].shape), half, half, hgain, hgain]
    mix_out_specs = [] if mix is None else [half, half, hgain, hgain]
    mix_out_shapes = [] if mix is None else [
        jax.ShapeDtypeStruct((rows, MLA_W), F32), jax.ShapeDtypeStruct((rows, LRU_W), F32),
        jax.ShapeDtypeStruct((1, MLA_W), F32), jax.ShapeDtypeStruct((1, LRU_W), F32)]
    n_in = len(in_specs) + 3 + n_mix
    return _hosted_call(
        body, name=name, grid=(steps,),
        in_specs=in_specs + [full, gspec, full] + mix_in_specs + _any_specs(n_prev),
        out_specs=[full, full, gspec] + mix_out_specs,
        out_shape=[jax.ShapeDtypeStruct((rows, d), F32), jax.ShapeDtypeStruct((rows, d), BF16),
                   jax.ShapeDtypeStruct((1, d), F32)] + mix_out_shapes,
        args=(*args, h, g, dres, *(mix or ()), *(prev or ())), comm=comm,
        aliases={n_in: 0, n_in + 1: 1} if prev is not None else {})


def _wgrad_call(name, a, b, scale=1.0, comm=None):
    a3, b3 = a.ndim == 3, b.ndim == 3
    ns = a.shape[0] if a3 else (b.shape[0] if b3 else 1)
    rows, m = a.shape[-2:]
    n = b.shape[-1]
    tmm = m if a3 else _col_tile(m, 256)

    def body(a_ref, b_ref, o_ref):
        av = a_ref[0] if a3 else a_ref[...]
        bv = b_ref[0] if b3 else b_ref[...]
        res = _dot_tn(av, bv)
        if scale != 1.0:
            res = res * scale
        if a3 or b3:
            o_ref[0] = res
        else:
            o_ref[...] = res

    aspec = (pl.BlockSpec((1, rows, tmm), lambda s, j: (s, 0, j)) if a3
             else pl.BlockSpec((rows, tmm), lambda s, j: (0, j)))
    bspec = (pl.BlockSpec((1, rows, n), lambda s, j: (s, 0, 0)) if b3
             else pl.BlockSpec((rows, n), lambda s, j: (0, 0)))
    if a3 or b3:
        ospec = pl.BlockSpec((1, tmm, n), lambda s, j: (s, j, 0))
        oshape = jax.ShapeDtypeStruct((ns, m, n), F32)
    else:
        ospec = pl.BlockSpec((tmm, n), lambda s, j: (j, 0))
        oshape = jax.ShapeDtypeStruct((m, n), F32)
    return _hosted_call(
        body, name=name, grid=(ns, m // tmm), in_specs=[aspec, bspec], out_specs=[ospec], out_shape=[oshape],
        dims=("parallel", "parallel"), args=(a, b), comm=comm)[0]


def _mla_prep_bwd_call(z, dq, dk, dv, gql, gkvl, gqh, gkh, wuq, wuk, wuv, tabs, lp, tm, comm=None):
    rows = z.shape[0]
    tpe = lp // tm

    def body(z_ref, dq_ref, dk_ref, dv_ref, gql_ref, gkvl_ref, gqh_ref, gkh_ref, wuq_ref, wuk_ref, wuv_ref,
             c_ref, s1_ref, s2_ref, dz_ref, dgql_ref, dgkvl_ref, dgqh_ref, dgkh_ref, dwuq_ref, dwuk_ref, dwuv_ref,
             dqp_ref, dkn_ref, dvv_ref):
        gql, gkvl = gql_ref[...], gkvl_ref[...]
        gq, gk = gqh_ref[...], gkh_ref[...]
        wuq, wuk, wuv = wuq_ref[...], wuk_ref[...], wuv_ref[...]
        m = _mla_heads(z_ref[...], gql, gkvl, wuq, wuk, wuv)
        c, s1, s2 = c_ref[...], s1_ref[...], s2_ref[...]
        dgq = jnp.zeros((1, D_QKP), F32)
        dgk = jnp.zeros((1, D_QKP), F32)
        dkr = jnp.zeros((tm, D_QKP - D_NOPE), F32)
        for hd in range(HEADS):
            qn, rqh, knn, krn, rkh = m["heads"][hd]
            dqg = jnp.concatenate([dq_ref[hd, :, 0:D_NOPE].astype(F32),
                                   _rope_t(dq_ref[hd, :, D_NOPE:D_QKP].astype(F32), c, s1, s2)], axis=1)
            dgq = dgq + jnp.sum(dqg * qn, axis=0, keepdims=True)
            dqn = dqg * gq
            dqr = rqh * (dqn - qn * (jnp.sum(dqn * qn, axis=-1, keepdims=True) * (1.0 / D_QK)))
            dqp_ref[:, hd * D_QKP:(hd + 1) * D_QKP] = dqr.astype(BF16)
            kn_full = jnp.concatenate([knn, krn], axis=1)
            dkg = jnp.concatenate([dk_ref[hd, :, 0:D_NOPE].astype(F32),
                                   _rope_t(dk_ref[hd, :, D_NOPE:D_QKP].astype(F32), c, s1, s2)], axis=1)
            dgk = dgk + jnp.sum(dkg * kn_full, axis=0, keepdims=True)
            dkn = dkg * gk
            dkraw = rkh * (dkn - kn_full * (jnp.sum(dkn * kn_full, axis=-1, keepdims=True) * (1.0 / D_QK)))
            dkn_ref[:, hd * D_NOPE:(hd + 1) * D_NOPE] = dkraw[:, 0:D_NOPE].astype(BF16)
            dkr = dkr + dkraw[:, D_NOPE:D_QKP]
            dvv_ref[:, hd * D_V:(hd + 1) * D_V] = dv_ref[hd]
        dcqn = _dot(dqp_ref[...], wuq)
        dckvn = _dot_nt(dkn_ref[...], wuk) + _dot_nt(dvv_ref[...], wuv)
        dz_ref[:, 0:Q_RANK] = _rms_bwd(dcqn * gql, m["nq"], m["rq"], Q_RANK).astype(BF16)
        dz_ref[:, Q_RANK:Z_KR] = _rms_bwd(dckvn * gkvl, m["nkv"], m["rkv"], KV_RANK).astype(BF16)
        dz_ref[:, Z_KR:Z_MLA] = dkr.astype(BF16)
        parts = [(dgql_ref, jnp.sum(dcqn * m["nq"], axis=0, keepdims=True)),
                 (dgkvl_ref, jnp.sum(dckvn * m["nkv"], axis=0, keepdims=True)), (dgqh_ref, dgq), (dgkh_ref, dgk),
                 (dwuq_ref, _dot_tn(dqp_ref[...], m["cqn"])), (dwuk_ref, _dot_tn(m["ckvn"], dkn_ref[...])),
                 (dwuv_ref, _dot_tn(m["ckvn"], dvv_ref[...]))]

        @pl.when(pl.program_id(0) == 0)
        def _():
            for ref, val in parts:
                ref[...] = val

        @pl.when(pl.program_id(0) != 0)
        def _():
            for ref, val in parts:
                ref[...] += val

    def const(shape):
        return pl.BlockSpec(shape, lambda i: tuple(0 for _ in shape))

    tab = pl.BlockSpec((tm, 128), lambda i: (i % tpe, 0))
    hq = pl.BlockSpec((HEADS, tm, D_QKP), lambda i: (0, i, 0))
    hv = pl.BlockSpec((HEADS, tm, D_V), lambda i: (0, i, 0))

    def rowspec(n):
        return pl.BlockSpec((tm, n), lambda i: (i, 0))

    return _hosted_call(
        body, name="mla_prep_bwd", grid=(rows // tm,),
        in_specs=[rowspec(Z_MLA), hq, hq, hv, const((1, Q_RANK)), const((1, KV_RANK)), const((1, D_QKP)),
                  const((1, D_QKP)), const((HEADS * D_QKP, Q_RANK)), const((KV_RANK, HEADS * D_NOPE)),
                  const((KV_RANK, HEADS * D_V)), tab, tab, tab],
        out_specs=[rowspec(Z_MLA), const((1, Q_RANK)), const((1, KV_RANK)), const((1, D_QKP)), const((1, D_QKP)),
                   const((HEADS * D_QKP, Q_RANK)), const((KV_RANK, HEADS * D_NOPE)), const((KV_RANK, HEADS * D_V))],
        out_shape=[jax.ShapeDtypeStruct((rows, Z_W), BF16),
                   jax.ShapeDtypeStruct((1, Q_RANK), F32), jax.ShapeDtypeStruct((1, KV_RANK), F32),
                   jax.ShapeDtypeStruct((1, D_QKP), F32), jax.ShapeDtypeStruct((1, D_QKP), F32),
                   jax.ShapeDtypeStruct((HEADS * D_QKP, Q_RANK), F32),
                   jax.ShapeDtypeStruct((KV_RANK, HEADS * D_NOPE), F32), jax.ShapeDtypeStruct((KV_RANK, HEADS * D_V), F32)],
        scratch_shapes=[pltpu.VMEM((tm, HEADS * D_QKP), BF16), pltpu.VMEM((tm, HEADS * D_NOPE), BF16),
                        pltpu.VMEM((tm, HEADS * D_V), BF16)],
        args=(z, dq, dk, dv, gql, gkvl, gqh, gkh, wuq, wuk, wuv, *tabs), comm=comm)


def _local_step(h0, target, w, nb, lp, sched=None):
    tm = _row_tile(nb * lp, 1408)
    te = _row_tile(lp, 512)
    tabs = _rope_tables(lp)
    g = {}
    if sched is None:
        host = lambda stage: None
    else:
        sched.g = g
        host = sched.host

    def ffn_act(tag, u, split):
        if split:
            gate = _ffn_gate_call(tag + "_gate", u, w[tag + "_w_gate"], tm, host(tag + "_gate"))
            up, act = _ffn_upact_call(tag + "_upact", u, w[tag + "_w_up"], gate, tm, host(tag + "_upact"))
        else:
            gate, up, act = _ffn_up_call(tag + "_up", u, w[tag + "_w_gate"], w[tag + "_w_up"], tm, host(tag + "_up"))
        return u, gate, up, act

    def ffn_bwd(tag, h, saved, dh, dhb, split, mix=None):
        u, gate, up, act = saved
        dgate, dup = _ffn_dact_call(tag + "_dact", dhb, w[tag + "_w_down"], gate, up, tm, host(tag + "_dact"))
        g[tag + "_w_down"] = _wgrad_call(tag + "_dwd", act, dhb, 0.5, host(tag + "_dwd"))
        g[tag + "_w_gate"] = _wgrad_call(tag + "_dwg", dgate, u, 1.0, host(tag + "_dwg"))
        g[tag + "_w_up"] = _wgrad_call(tag + "_dwu", dup, u, 1.0, host(tag + "_dwu"))
        pieces = [(dgate, w[tag + "_w_gate"]), (dup, w[tag + "_w_up"])]
        if not split:
            res = _norm_in_bwd_call(tag + "_din", pieces, h, w[tag + "_norm"], dh, te, host(tag + "_din"), mix=mix)
            g[tag + "_norm"] = res[2]
            return (res[0], res[1], *res[3:])
        dh_a, dhb_a, dg_a = _norm_in_bwd_call(tag + "_din_a", pieces, h, w[tag + "_norm"], dh, te,
                                              host(tag + "_din_a"), part=(0, 2))
        dh_in, dhb_in, dg_b = _norm_in_bwd_call(tag + "_din_b", pieces, h, w[tag + "_norm"], dh, te,
                                                host(tag + "_din_b"), part=(1, 2), prev=(dh_a, dhb_a))
        g[tag + "_norm"] = dg_a + dg_b
        return dh_in, dhb_in

    s1 = ffn_act("ffn1", _rmsnorm_call("ffn1_norm", h0, w["ffn1_norm"], te, host("ffn1_norm")), True)
    h1, un = _ffn_down_call("ffn1_down", s1[3], w["ffn1_w_down"], h0, te, host("ffn1_down"), w["mix_norm"])
    z = _mm_call("mix_in", un, w["w_in"], tm, F32)
    mla_w = (w["q_latent_norm"], w["kv_latent_norm"], w["q_head_norm"], w["k_head_norm"], w["w_uq"], w["w_uk"],
             w["w_uv"])
    q, k, v = _mla_prep_call(z, *mla_w, tabs, lp, te)
    o, lse = _attn_fwd_call(q, k, v, nb, lp, host("attn_fwd"))
    lru_w = (w["conv_w"], w["conv_b"], w["gate_a_w"], w["gate_a_b"], w["gate_x_w"], w["gate_x_b"], w["lru_lambda"])
    yl, hs = _lru_fwd_call(z, *lru_w, nb, lp, host("lru_fwd"))
    y, h2, u2 = _mix_out_call(o, yl, w["attn_out_norm"], w["lru_out_norm"], w["w_out"], h1, w["ffn2_norm"], te)
    s2 = ffn_act("ffn2", u2, False)
    dh3, dh3b, g["final_norm"], loss = _ffn_down_loss_call("ffn2_down", s2[3], w["ffn2_w_down"], h2, w["final_norm"],
                                                           target, lp, te)
    g["loss"] = loss

    dh2, dh2b, dya, dyl, g["attn_out_norm"], g["lru_out_norm"] = ffn_bwd(
        "ffn2", h2, s2, dh3, dh3b, False, (w["w_out"], o, yl, w["attn_out_norm"], w["lru_out_norm"]))
    g["w_out"] = _wgrad_call("dw_out", y, dh2b)
    dq, dk, dv = _attn_bwd_call(q, k, v, o, lse, dya, nb, lp, host("attn_bwd"))
    (dz_mla, g["q_latent_norm"], g["kv_latent_norm"], g["q_head_norm"], g["k_head_norm"], g["w_uq"], g["w_uk"],
     g["w_uv"]) = _mla_prep_bwd_call(z, dq, dk, dv, *mla_w, tabs, lp, te, host("mla_prep_bwd"))
    (dz, g["conv_w"], g["conv_b"], g["gate_a_w"], g["gate_a_b"], g["gate_x_w"], g["gate_x_b"],
     g["lru_lambda"]) = _lru_bwd_call(z, dz_mla, hs, dyl, *lru_w, nb, lp, host("lru_bwd"))
    g["w_in"] = _wgrad_call("dw_in", dz, un)
    dh1, dh1b, g["mix_norm"] = _norm_in_bwd_call("mix_din", [(dz, w["w_in"])], h1, w["mix_norm"], dh2, te,
                                                 host("mix_din"))
    dh0 = ffn_bwd("ffn1", h0, s1, dh1, dh1b, True)[0]
    return loss, dh0, g


def _place():
    x, y, c = lax.axis_index("x"), lax.axis_index("y"), lax.axis_index("c")
    return x, y, c, [(1 - x, y), (x, 1 - y), (1 - x, 1 - y)]


def _any_specs(n):
    return [pl.BlockSpec(memory_space=pl.ANY)] * n


def _remote(src, dst, sems, k, dev):
    send_sems, recv_sems, base = sems
    return pltpu.make_async_remote_copy(src_ref=src, dst_ref=dst, send_sem=send_sems.at[base + k],
                                        recv_sem=recv_sems.at[base + k], device_id=dev, device_id_type=MESH)


EW_VMEM_BYTES = 24 * 1024 * 1024


def _fit_rows(rows, cols, blocks):
    return _row_tile(rows, max(16, int(EW_VMEM_BYTES // (8 * blocks)) // cols))


class _Geom:
    def __init__(self, n0, n1, blocks=1.0):
        self.n0, self.n1 = n0, n1
        self.axis = 0 if n0 % 32 == 0 else 1
        self.h0, self.h1 = (n0 // 2, n1) if self.axis == 0 else (n0, n1 // 2)
        self.tr = _fit_rows(self.h0, self.h1, blocks)
        self.nblk = self.h0 // self.tr

    def half_ref(self, ref, lead, idx):
        if self.axis == 0:
            return ref.at[(*lead, pl.ds(idx * self.h0, self.h0))]
        return ref.at[(*lead, slice(None), pl.ds(idx * self.h1, self.h1))]

    def half_block(self, lead, i, idx):
        return (*lead, idx * self.nblk + i, 0) if self.axis == 0 else (*lead, i, idx)


class _Comm:
    def __init__(self, ins, out_shapes, aliases, n_sems, start, finish, deliver):
        self.ins, self.out_shapes, self.aliases, self.n_sems = list(ins), list(out_shapes), dict(aliases), n_sems
        self.start, self.finish, self.deliver = start, finish, deliver

    def scratch(self):
        return [pltpu.SemaphoreType.DMA((self.n_sems,)), pltpu.SemaphoreType.DMA((self.n_sems,))]


def _comm_call(name, comm):
    n_in = len(comm.ins)

    def body(*refs):
        ins, outs, sems = refs[:n_in], refs[n_in:-2], (*refs[-2:], 0)
        comm.start(ins, outs, sems)
        comm.finish(ins, outs, sems)

    res = pl.pallas_call(
        body, name=name, out_shape=comm.out_shapes, in_specs=_any_specs(n_in),
        out_specs=_any_specs(len(comm.out_shapes)), input_output_aliases=comm.aliases,
        scratch_shapes=comm.scratch())(*comm.ins)
    return comm.deliver(list(res))


def _hosted_call(body, *, name, grid, in_specs, out_specs, out_shape, args, scratch_shapes=(), dims=None, comm=None,
                 prefetch=None, aliases=None):
    aliases = dict(aliases or {})
    in_specs, out_specs, out_shape = list(in_specs), list(out_specs), list(out_shape)
    n_pre = 0 if prefetch is None else 1

    def call(fn, in_specs, out_specs, out_shape, scratch, aliases, dims, args):
        if prefetch is None:
            return pl.pallas_call(
                fn, name=name, grid=grid, in_specs=in_specs, out_specs=out_specs, out_shape=out_shape,
                scratch_shapes=scratch, input_output_aliases=aliases, compiler_params=_params(dims))(*args)
        spec = pltpu.PrefetchScalarGridSpec(num_scalar_prefetch=1, grid=grid, in_specs=in_specs, out_specs=out_specs,
                                            scratch_shapes=scratch)
        return pl.pallas_call(
            fn, name=name, grid_spec=spec, out_shape=out_shape,
            input_output_aliases={i + 1: o for i, o in aliases.items()}, compiler_params=_params(dims))(prefetch, *args)

    if comm is None:
        return list(call(body, in_specs, out_specs, out_shape, list(scratch_shapes), aliases,
                         dims or ("arbitrary",) * len(grid), args))
    n_in, n_out, n_ci, n_co = len(in_specs), len(out_specs), len(comm.ins), len(comm.out_shapes)

    def wrapped(*refs):
        pre, refs = refs[:n_pre], refs[n_pre:]
        ins, cins = refs[:n_in], refs[n_in:n_in + n_ci]
        outs = refs[n_in + n_ci:n_in + n_ci + n_out]
        couts = refs[n_in + n_ci + n_out:n_in + n_ci + n_out + n_co]
        scratch, sems = refs[n_in + n_ci + n_out + n_co:-2], (*refs[-2:], 0)
        first = functools.reduce(jnp.logical_and, [pl.program_id(k) == 0 for k in range(len(grid))])
        last = functools.reduce(jnp.logical_and, [pl.program_id(k) == grid[k] - 1 for k in range(len(grid))])

        @pl.when(first)
        def _():
            comm.start(cins, couts, sems)

        body(*pre, *ins, *outs, *scratch)

        @pl.when(last)
        def _():
            comm.finish(cins, couts, sems)

    res = call(wrapped, in_specs + _any_specs(n_ci), out_specs + _any_specs(n_co), out_shape + comm.out_shapes,
               list(scratch_shapes) + comm.scratch(),
               {**aliases, **{n_in + i: n_out + o for i, o in comm.aliases.items()}},
               ("arbitrary",) * len(grid), (*args, *comm.ins))
    comm.deliver(list(res[n_out:]))
    return list(res[:n_out])


def _gather_comm(bufs, deliver):
    n = len(bufs)
    geoms = [_Geom(*b.shape[1:]) for b in bufs]

    def first(outs, sems):
        x, y, c, chips = _place()
        cps = []
        for a in range(n):
            mine = geoms[a].half_ref(outs[a], (2 * x + y,), c)
            cps += [_remote(mine, mine, sems, 6 * a + j, (cx, cy, c)) for j, (cx, cy) in enumerate(chips)]
        return cps

    def start(ins, outs, sems):
        for cp in first(outs, sems):
            cp.start()

    def finish(ins, outs, sems):
        x, y, c, chips = _place()
        sib = (x, y, 1 - c)
        passed = []
        for a in range(n):
            for j, (cx, cy) in enumerate(chips):
                land = geoms[a].half_ref(outs[a], (2 * cx + cy,), c)
                _remote(land, land, sems, 6 * a + j, sib).wait_recv()
                cp = _remote(land, land, sems, 6 * a + 3 + j, sib)
                cp.start()
                passed.append(cp)
        for a in range(n):
            for j, (cx, cy) in enumerate(chips):
                land = geoms[a].half_ref(outs[a], (2 * cx + cy,), 1 - c)
                _remote(land, land, sems, 6 * a + 3 + j, sib).wait_recv()
        for cp in first(outs, sems) + passed:
            cp.wait_send()

    return _Comm(bufs, [jax.ShapeDtypeStruct(b.shape, b.dtype) for b in bufs], {a: a for a in range(n)}, 6 * n,
                 start, finish, deliver)


def _reduce_pair_comm(grads, deliver):
    n = len(grads)
    geoms = [_Geom(*a.shape[1:]) for a in grads]

    def copies(ins, outs, sems):
        x, y, c, _ = _place()
        return [_remote(geoms[a].half_ref(ins[a], (slice(None),), 1 - c), outs[a], sems, a, (x, y, 1 - c))
                for a in range(n)]

    def start(ins, outs, sems):
        for cp in copies(ins, outs, sems):
            cp.start()

    def finish(ins, outs, sems):
        cps = copies(ins, outs, sems)
        for cp in cps:
            cp.wait_recv()
        for cp in cps:
            cp.wait_send()

    shapes = [jax.ShapeDtypeStruct((N_SHARD, g.h0, g.h1), a.dtype) for a, g in zip(grads, geoms)]
    return _Comm(grads, shapes, {}, n, start, finish, deliver)


def _reduce_chips_comm(parts, deliver):
    n = len(parts)

    def copies(ins, outs, sems):
        x, y, c, chips = _place()
        return [_remote(ins[a].at[2 * cx + cy], outs[a].at[j], sems, 3 * a + j, (cx, cy, c))
                for a in range(n) for j, (cx, cy) in enumerate(chips)]

    def start(ins, outs, sems):
        for cp in copies(ins, outs, sems):
            cp.start()

    def finish(ins, outs, sems):
        cps = copies(ins, outs, sems)
        for cp in cps:
            cp.wait_recv()
        for cp in cps:
            cp.wait_send()

    shapes = [jax.ShapeDtypeStruct((3,) + a.shape[1:], a.dtype) for a in parts]
    return _Comm(parts, shapes, {}, 3 * n, start, finish, deliver)


def _share_pair_comm(bufs, deliver):
    n = len(bufs)
    geoms = [_Geom(*b.shape) for b in bufs]

    def copies(outs, sems):
        x, y, c, _ = _place()
        cps = []
        for a in range(n):
            mine = geoms[a].half_ref(outs[a], (), c)
            cps.append(_remote(mine, mine, sems, a, (x, y, 1 - c)))
        return cps

    def start(ins, outs, sems):
        for cp in copies(outs, sems):
            cp.start()

    def finish(ins, outs, sems):
        x, y, c, _ = _place()
        for a in range(n):
            land = geoms[a].half_ref(outs[a], (), 1 - c)
            _remote(land, land, sems, a, (x, y, 1 - c)).wait_recv()
        for cp in copies(outs, sems):
            cp.wait_send()

    return _Comm(bufs, [jax.ShapeDtypeStruct(b.shape, b.dtype) for b in bufs], {a: a for a in range(n)}, n,
                 start, finish, deliver)


def _small_comm(pack, deliver):
    r, d = pack.shape

    def copies(ins, outs, sems):
        x, y, c, _ = _place()
        cps = []
        for k in range(1, 8):
            peer = (x ^ ((k >> 2) & 1), y ^ ((k >> 1) & 1), c ^ (k & 1))
            cps.append(_remote(ins[0], outs[0].at[4 * x + 2 * y + c], sems, k - 1, peer))
        return cps

    def start(ins, outs, sems):
        for cp in copies(ins, outs, sems):
            cp.start()

    def finish(ins, outs, sems):
        cps = copies(ins, outs, sems)
        for cp in cps:
            cp.wait_recv()
        for cp in cps:
            cp.wait_send()

    return _Comm([pack], [jax.ShapeDtypeStruct((8, r, d), pack.dtype)], {}, 7, start, finish, deliver)


def _join_comms(comms):
    comms = [c for c in comms if c is not None]
    if len(comms) <= 1:
        return comms[0] if comms else None
    ins, out_shapes, aliases, spans, n_sems = [], [], {}, [], 0
    for c in comms:
        aliases.update({len(ins) + i: len(out_shapes) + o for i, o in c.aliases.items()})
        spans.append((len(ins), len(ins) + len(c.ins), len(out_shapes), len(out_shapes) + len(c.out_shapes), n_sems))
        ins += c.ins
        out_shapes += c.out_shapes
        n_sems += c.n_sems

    def run(which):
        def go(all_ins, all_outs, sems):
            for c, (i0, i1, o0, o1, base) in zip(comms, spans):
                getattr(c, which)(all_ins[i0:i1], all_outs[o0:o1], (sems[0], sems[1], sems[2] + base))
        return go

    def deliver(outs):
        for c, (_, _, o0, o1, _) in zip(comms, spans):
            c.deliver(outs[o0:o1])
        return outs

    return _Comm(ins, out_shapes, aliases, n_sems, run("start"), run("finish"), deliver)


def _ew_call(name, fn, ins, out_dtypes):
    shape = ins[0].shape
    cols = shape[-1]
    rows = 1
    for s_ in shape[:-1]:
        rows *= s_
    ins2 = [a.reshape(rows, cols) for a in ins]
    tr = rows
    for t in range(16, min(rows, max(16, (1 << 19) // cols)) + 1, 16):
        if rows % t == 0:
            tr = t
    no = len(out_dtypes)

    def body(*refs):
        outs = fn(*[r[...] for r in refs[:len(ins2)]])
        for ref, val in zip(refs[len(ins2):], outs):
            ref[...] = val.astype(ref.dtype)

    spec = pl.BlockSpec((tr, cols), lambda i: (i, 0))
    res = pl.pallas_call(
        body, name=name, grid=(rows // tr,), in_specs=[spec] * len(ins2), out_specs=[spec] * no,
        out_shape=[jax.ShapeDtypeStruct((rows, cols), dt) for dt in out_dtypes],
        compiler_params=_params(("parallel",)))(*ins2)
    return [r.reshape(shape) for r in res]


def _adamw_math(w, g, m, v):
    m = ADAM_B1 * m + (1.0 - ADAM_B1) * g
    v = ADAM_B2 * v + (1.0 - ADAM_B2) * (g * g)
    m_hat = m / (1.0 - ADAM_B1 ** ADAM_STEP)
    v_hat = v / (1.0 - ADAM_B2 ** ADAM_STEP)
    delta = -ADAM_LR * (m_hat / (jnp.sqrt(v_hat) + ADAM_EPS) + ADAM_WD * w)
    return delta, m, v


def _adamw_call(name, w, g, m, v):
    return _ew_call(name, _adamw_math, [w, g, m, v], [F32, F32, F32])


def _tiled_call(name, fn, place, grid, in_items, out_items, comm=None):
    ni = len(in_items)

    def body(place_ref, *refs):
        vals = fn(*[r[...] for r in refs[:ni]])
        for ref, val in zip(refs[ni:], vals):
            ref[...] = val.astype(ref.dtype)

    return _hosted_call(
        body, name=name, grid=grid, in_specs=[pl.BlockSpec(blk, imap) for _, blk, imap in in_items],
        out_specs=[pl.BlockSpec(blk, imap) for _, _, blk, imap in out_items],
        out_shape=[jax.ShapeDtypeStruct(shp, dt) for shp, dt, _, _ in out_items],
        args=[a for a, _, _ in in_items], prefetch=place, comm=comm)


def _cast_call(name, place, shards, comm=None):
    n0, n1 = shards[0].shape
    tr = _fit_rows(n0, n1, 1.5 * len(shards))
    ins = [(a, (tr, n1), lambda i, p: (i, 0)) for a in shards]
    outs = [((N_SHARD, n0, n1), BF16, (1, tr, n1), lambda i, p: (p[0], i, 0)) for _ in shards]
    return _tiled_call(name, lambda *v: [x[None] for x in v], place, (n0 // tr,), ins, outs, comm)


def _pair_sum_call(name, place, fulls, gots):
    k = len(fulls)
    g = _Geom(*fulls[0].shape[1:], blocks=2.5 * k)
    blk = (1, g.tr, g.h1)
    ins = [(a, blk, lambda s, i, p: g.half_block((s,), i, p[1])) for a in fulls]
    ins += [(a, blk, lambda s, i, p: (s, i, 0)) for a in gots]
    outs = [((N_SHARD, g.h0, g.h1), BF16, blk, lambda s, i, p: (s, i, 0)) for _ in fulls]
    return _tiled_call(name, lambda *v: [v[j] + v[k + j] for j in range(k)], place, (N_SHARD, g.nblk), ins, outs)


def _chip_sum_call(name, place, fulls, gots, recvs, comm=None):
    k = len(fulls)
    g = _Geom(*fulls[0].shape[1:], blocks=4.5 * k)
    blk = (1, g.tr, g.h1)
    ins = [(a, blk, lambda i, p: g.half_block((p[0],), i, p[1])) for a in fulls]
    ins += [(a, blk, lambda i, p: (p[0], i, 0)) for a in gots]
    ins += [(a, (3, g.tr, g.h1), lambda i, p: (0, i, 0)) for a in recvs]
    outs = [((g.n0, g.n1), F32, (g.tr, g.h1), lambda i, p: g.half_block((), i, p[1])) for _ in fulls]

    def fn(*v):
        res = []
        for j in range(k):
            r = v[2 * k + j].astype(F32)
            res.append(v[j][0] + v[k + j][0] + r[0] + r[1] + r[2])
        return res

    return _tiled_call(name, fn, place, (g.nblk,), ins, outs, comm)


def _adamw_group_call(name, ws, gs, ms, vs, comm=None):
    k = len(ws)
    n0, n1 = ws[0].shape
    tr = _fit_rows(n0, n1, 8 * k)
    spec = pl.BlockSpec((tr, n1), lambda i: (i, 0))

    def body(*refs):
        for j in range(k):
            g = refs[k + j][...]
            delta, m, vv = _adamw_math(refs[j][...], g, refs[2 * k + j][...], refs[3 * k + j][...])
            for ref, val in zip(refs[4 * k + 4 * j:4 * k + 4 * j + 4], (g, delta, m, vv)):
                ref[...] = val

    flat = _hosted_call(
        body, name=name, grid=(n0 // tr,), in_specs=[spec] * (4 * k), out_specs=[spec] * (4 * k),
        out_shape=[jax.ShapeDtypeStruct((n0, n1), F32)] * (4 * k), dims=("parallel",),
        args=(*ws, *gs, *ms, *vs), comm=comm)
    return [flat[4 * j:4 * j + 4] for j in range(k)]


def _small_update_call(me, early, own_early, late, own_late, states):
    nd, r, d = early.shape
    widths = [a.shape[1] for a in states[0][0]]
    nw = len(widths)
    n_state = nw + 2

    def body(me_ref, e_ref, oe_ref, l_ref, ol_ref, *refs):
        state_refs, refs = refs[:3 * n_state], refs[3 * n_state:]
        (gs_ref, d_ref, nm_ref, nv_ref), rows, packs = refs[:4], refs[4:4 + 4 * nw], refs[4 + 4 * nw:]
        for kind, pack in enumerate(packs):
            srefs = state_refs[kind * n_state:(kind + 1) * n_state]
            pack[...] = jnp.zeros_like(pack)
            for k, width in enumerate(widths):
                pack[k:k + 1, 0:width] = srefs[k][...]
            pack[ROW_GATE_A:ROW_GATE_A + 32, :] = srefs[nw][...]
            pack[ROW_GATE_X:ROW_GATE_X + 32, :] = srefs[nw + 1][...]
        w_ref, m_ref, v_ref = packs
        mine = me_ref[0]

        def total(g_ref, own_ref):
            acc = None
            for k in range(nd):
                part = jnp.where(mine == k, own_ref[...], g_ref[k])
                acc = part if acc is None else acc + part
            return acc

        gs = total(e_ref, oe_ref)
        ls = total(l_ref, ol_ref)
        gs_ref[...] = gs
        first = gs[0:8] + ls[0:8]
        gs_ref[0:8, :] = first
        gs_ref[ROW_META:ROW_META + N_META, :] = gs[ROW_META:ROW_META + N_META] + ls[8:8 + N_META]
        grads = jnp.concatenate([first, gs[8:SMALL_ADAM_ROWS]], axis=0)
        delta, m, v = _adamw_math(w_ref[...], grads, m_ref[...], v_ref[...])
        d_ref[...] = delta
        nm_ref[...] = m
        nv_ref[...] = v
        for kind, pack_ref in enumerate((gs_ref, d_ref, nm_ref, nv_ref)):
            for k, width in enumerate(widths):
                rows[kind * nw + k][...] = pack_ref[k:k + 1, 0:width]

    vm = pl.BlockSpec(memory_space=pltpu.VMEM)
    ashape = jax.ShapeDtypeStruct((SMALL_ADAM_ROWS, d), F32)
    row_shapes = [jax.ShapeDtypeStruct((1, width), F32) for _ in range(4) for width in widths]
    flat_states = [a for rows_, ga, gx in states for a in (*rows_, ga, gx)]
    res = pl.pallas_call(
        body, name="small_update", in_specs=[pl.BlockSpec(memory_space=pltpu.SMEM)] + [vm] * (4 + 3 * n_state),
        out_specs=[vm] * (4 + 4 * nw),
        out_shape=[jax.ShapeDtypeStruct((r, d), F32), ashape, ashape, ashape] + row_shapes,
        scratch_shapes=[pltpu.VMEM((SMALL_ADAM_ROWS, d), F32)] * 3,
        compiler_params=pltpu.CompilerParams(vmem_limit_bytes=VMEM_LIMIT_BYTES))(
            me, early, own_early, late, own_late, *flat_states)
    return res[:4], [res[4 + kind * nw:4 + (kind + 1) * nw] for kind in range(4)]


SMALL_NAMES = ["ffn1_norm", "mix_norm", "ffn2_norm", "final_norm", "q_latent_norm", "kv_latent_norm",
               "q_head_norm", "k_head_norm", "conv_b", "gate_a_b", "gate_x_b", "lru_lambda", "attn_out_norm",
               "lru_out_norm"]
ROW_CONV_W = 14
ROW_GATE_A = 16
ROW_GATE_X = 48
ROW_META = 80
ROW_LOSS = 96


def _row(a):
    flat = a.reshape(1, -1)
    return jnp.pad(flat, ((0, 0), (0, D_MODEL - flat.shape[1])))


def _pack_small(t, rows):
    parts = [_row(t[nm]) for nm in SMALL_NAMES]
    parts.append(t["conv_w"].reshape(2, D_MODEL))
    parts.append(t["gate_a_w"].reshape(32, D_MODEL))
    parts.append(t["gate_x_w"].reshape(32, D_MODEL))
    p = jnp.concatenate(parts, axis=0)
    return jnp.pad(p, ((0, rows - p.shape[0]), (0, 0)))


def _early_pack(g):
    gs = {nm: g.get(nm, jnp.zeros((1, D_MODEL), F32)) for nm in SMALL_NAMES}
    gs["q_head_norm"] = g["q_head_norm"][:, 0:D_QK]
    gs["k_head_norm"] = g["k_head_norm"][:, 0:D_QK]
    for nm in ("conv_b", "gate_a_b", "gate_x_b", "lru_lambda"):
        gs[nm] = g[nm].reshape(1, LRU_W)
    gs["conv_w"] = g["conv_w"].transpose(1, 0, 2).reshape(CONV_K, LRU_W)
    gs["gate_a_w"] = _gate_blocks(g["gate_a_w"])
    gs["gate_x_w"] = _gate_blocks(g["gate_x_w"])
    return jnp.concatenate([_pack_small(gs, ROW_META), jnp.zeros((N_META, D_MODEL), F32), _row(g["loss"][:, 0:1]),
                            jnp.zeros((SMALL_ROWS - ROW_LOSS - 1, D_MODEL), F32)], axis=0)


def _unpack_small(p, rows, like):
    out = dict(zip(SMALL_NAMES, rows))
    out["gate_a_w"] = p[ROW_GATE_A:ROW_GATE_A + 32].reshape(like["gate_a_w"].shape)
    out["gate_x_w"] = p[ROW_GATE_X:ROW_GATE_X + 32].reshape(like["gate_x_w"].shape)
    return out


def _gate_dense(wg):
    w4 = wg[0].reshape(N_LRU_TILES, 2, 64, 64)
    zero = jnp.zeros((N_LRU_TILES, 64, 64), wg.dtype)
    top = jnp.concatenate([w4[:, 0], zero], axis=2)
    bot = jnp.concatenate([zero, w4[:, 1]], axis=2)
    return jnp.concatenate([top, bot], axis=1).astype(BF16)


def _gate_blocks(dw):
    return jnp.stack([dw[:, 0:64, 0:64], dw[:, 64:128, 64:128]], axis=1).reshape(8, 64, 64)


BIG_NAMES = ["ffn1_w_gate", "ffn1_w_up", "ffn1_w_down", "w_in", "w_uq", "w_uk", "w_uv", "w_out", "ffn2_w_gate",
             "ffn2_w_up", "ffn2_w_down"]
BIG_GROUPS = [["ffn1_w_gate", "ffn1_w_up", "ffn1_w_down", "ffn2_w_gate", "ffn2_w_up", "ffn2_w_down"], ["w_in"],
              ["w_uq"], ["w_uk", "w_uv"], ["w_out"]]
TRANSPOSED = ("ffn1_w_gate", "ffn1_w_up", "ffn2_w_gate", "ffn2_w_up", "w_in", "w_uq")


def _to2d(nm, a):
    return a[0].T if nm in TRANSPOSED else a[0]


def _from2d(nm, a):
    return (a.T if nm in TRANSPOSED else a)[None]


WEIGHT_NAMES = ["meta_tokens", "ffn1_norm", "ffn1_w_gate", "ffn1_w_up", "ffn1_w_down", "mix_norm", "w_in",
                "q_latent_norm", "w_uq", "kv_latent_norm", "w_uk", "w_uv", "q_head_norm", "k_head_norm", "conv_w",
                "conv_b", "gate_a_w", "gate_a_b", "gate_x_w", "gate_x_b", "lru_lambda", "attn_out_norm",
                "lru_out_norm", "w_out", "ffn2_norm", "ffn2_w_gate", "ffn2_w_up", "ffn2_w_down", "final_norm"]


def _weight_from(nm, slots):
    if nm == "w_in":
        win = slots.reshape(IN_WIDTH, D_MODEL)
        lru = win[Z_KR + D_ROPE:].reshape(2, N_LRU_TILES, LRU_TILE, D_MODEL).transpose(1, 0, 2, 3)
        return jnp.concatenate([win[0:Z_KR + D_ROPE], jnp.zeros((128 - D_ROPE, D_MODEL), BF16),
                                lru.reshape(2 * LRU_W, D_MODEL)], axis=0)
    if nm == "w_uq":
        return jnp.pad(slots, ((0, 0), (0, D_QKP - D_QK), (0, 0))).reshape(HEADS * D_QKP, Q_RANK)
    if nm in ("w_uk", "w_uv"):
        return slots.transpose(1, 0, 2).reshape(KV_RANK, HEADS * D_NOPE)
    if nm == "w_out":
        return slots.reshape(D_MODEL, D_MODEL)
    return slots


def _small_weights(p, small):
    w = {nm: p[nm] for nm in SMALL_NAMES}
    w["q_head_norm"] = jnp.pad(p["q_head_norm"], ((0, 0), (0, D_QKP - D_QK)))
    w["k_head_norm"] = jnp.pad(p["k_head_norm"], ((0, 0), (0, D_QKP - D_QK)))
    w["conv_w"] = small[:, N_META:N_META + 2, :].reshape(N_SHARD, CONV_K, LRU_TILE)
    w["gate_a_w"] = _gate_dense(p["gate_a_w"])
    w["gate_x_w"] = _gate_dense(p["gate_x_w"])
    meta = small[:, 0:N_META, :].transpose(1, 0, 2).reshape(N_META, D_MODEL)
    return w, meta


def _full_weights(p, gathered, small):
    w, meta = _small_weights(p, small)
    w.update({nm: _weight_from(nm, gathered[nm]) for nm in BIG_NAMES})
    return w, meta


def _shard_grad(nm, g):
    if nm == "w_in":
        lru = g[Z_MLA:].reshape(N_LRU_TILES, 2, LRU_TILE, D_MODEL).transpose(1, 0, 2, 3).reshape(2 * LRU_W, D_MODEL)
        return jnp.concatenate([g[0:Z_KR + D_ROPE], lru], axis=0).reshape(N_SHARD, IN_WIDTH // N_SHARD, D_MODEL)
    if nm == "w_uq":
        return g.reshape(HEADS, D_QKP, Q_RANK)[:, 0:D_QK, :]
    if nm in ("w_uk", "w_uv"):
        return g.reshape(KV_RANK, HEADS, D_NOPE).transpose(1, 0, 2)
    if nm == "w_out":
        return g.reshape(N_SHARD, D_MODEL // N_SHARD, D_MODEL)
    return g


def _shard_grads(g):
    return {nm: _shard_grad(nm, g[nm]) for nm in BIG_NAMES}


GATHER_AT = {"ffn1_norm": ["ffn1_w_gate"], "ffn1_gate": ["ffn1_w_up"], "ffn1_upact": ["ffn1_w_down"],
             "ffn1_down": ["w_in", "w_uq", "w_uk", "w_uv", "w_out"], "attn_fwd": ["ffn2_w_down", "ffn2_w_gate"],
             "lru_fwd": ["ffn2_w_up"]}
PAIR_AT = [("ffn2_din", ["ffn2_w_gate", "ffn2_w_up", "ffn2_w_down"]),
           ("mix_din", ["w_out", "w_uq", "w_uk", "w_uv", "w_in"]),
           ("ffn1_dwg", ["ffn1_w_down"]), ("ffn1_dwu", ["ffn1_w_gate"]), ("ffn1_din_a", ["ffn1_w_up"])]
CHIPS_AT = [("attn_bwd", ["ffn2_w_down", "ffn2_w_gate"]), ("mla_prep_bwd", ["ffn2_w_up"]),
            ("ffn1_dact", ["w_out", "w_uq", "w_uk", "w_uv", "w_in"]),
            ("ffn1_dwu", ["ffn1_w_down"]), ("ffn1_din_a", ["ffn1_w_gate"]), ("ffn1_din_b", ["ffn1_w_up"])]
SHARE_EARLY_GROUPS, SHARE_EARLY_AT = 3, "ffn1_dwd"
SMALL_EARLY_AT = "mix_din"


def _same_shape_groups(names):
    return [[nm for nm in grp if nm in names] for grp in BIG_GROUPS if any(nm in names for nm in grp)]


class _Sched:
    def __init__(self, place, w, slots):
        self.place, self.w, self.slots = place, w, slots
        self.g = None
        self.sharded, self.from_pair, self.chip_bf16, self.from_chips = {}, {}, {}, {}
        self.early = self.early_all = None
        self.shared = {}

    def host(self, stage):
        comms = []
        if stage in GATHER_AT:
            comms.append(self.gather(GATHER_AT[stage]))
        comms += [self.chips(names) for at, names in CHIPS_AT if at == stage]
        comms += [self.pair(names) for at, names in PAIR_AT if at == stage]
        if stage == SMALL_EARLY_AT:
            comms.append(self.small_early())
        if stage == SHARE_EARLY_AT:
            comms.append(self.share_early())
        return _join_comms(comms)

    def small_early(self):
        self.early = _early_pack(self.g)

        def deliver(outs):
            self.early_all = outs[0]
            return outs

        return _small_comm(self.early, deliver)

    def gather(self, names):
        def deliver(outs):
            self.w.update({nm: _weight_from(nm, o) for nm, o in zip(names, outs)})
            return outs

        return _gather_comm([self.slots[nm] for nm in names], deliver)

    def pair(self, names):
        self.sharded.update({nm: _shard_grad(nm, self.g[nm]) for nm in names})

        def deliver(outs):
            self.from_pair.update(zip(names, outs))
            for grp in _same_shape_groups(names):
                sums = _pair_sum_call("pair_sum_" + grp[0], self.place, [self.sharded[nm] for nm in grp],
                                      [self.from_pair[nm] for nm in grp])
                self.chip_bf16.update(zip(grp, sums))
            return outs

        return _reduce_pair_comm([self.sharded[nm] for nm in names], deliver)

    def chips(self, names):
        def deliver(outs):
            self.from_chips.update(zip(names, outs))
            return outs

        return _reduce_chips_comm([self.chip_bf16[nm] for nm in names], deliver)

    def chip_sums(self, names, comm=None):
        out = {}
        for grp in _same_shape_groups(names):
            sums = _chip_sum_call("chip_sum_" + grp[0], self.place, [self.sharded[nm] for nm in grp],
                                  [self.from_pair[nm] for nm in grp], [self.from_chips[nm] for nm in grp], comm)
            comm = None
            out.update(zip(grp, sums))
        return out

    def share_early(self):
        names = [nm for at, grp in CHIPS_AT[:SHARE_EARLY_GROUPS] for nm in grp]
        mine = self.chip_sums(names)
        return _share_pair_comm([mine[nm] for nm in names], lambda o: self.shared.update(zip(names, o)))


def kernel(x, meta_tokens, ffn1_norm, ffn1_w_gate, ffn1_w_up, ffn1_w_down, mix_norm, w_in, q_latent_norm, w_uq, kv_latent_norm, w_uk, w_uv, q_head_norm, k_head_norm, conv_w, conv_b, gate_a_w, gate_a_b, gate_x_w, gate_x_b, lru_lambda, attn_out_norm, lru_out_norm, w_out, ffn2_norm, ffn2_w_gate, ffn2_w_up, ffn2_w_down, final_norm, loss_target, m_meta_tokens, m_ffn1_norm, m_ffn1_w_gate, m_ffn1_w_up, m_ffn1_w_down, m_mix_norm, m_w_in, m_q_latent_norm, m_w_uq, m_kv_latent_norm, m_w_uk, m_w_uv, m_q_head_norm, m_k_head_norm, m_conv_w, m_conv_b, m_gate_a_w, m_gate_a_b, m_gate_x_w, m_gate_x_b, m_lru_lambda, m_attn_out_norm, m_lru_out_norm, m_w_out, m_ffn2_norm, m_ffn2_w_gate, m_ffn2_w_up, m_ffn2_w_down, m_final_norm, v_meta_tokens, v_ffn1_norm, v_ffn1_w_gate, v_ffn1_w_up, v_ffn1_w_down, v_mix_norm, v_w_in, v_q_latent_norm, v_w_uq, v_kv_latent_norm, v_w_uk, v_w_uv, v_q_head_norm, v_k_head_norm, v_conv_w, v_conv_b, v_gate_a_w, v_gate_a_b, v_gate_x_w, v_gate_x_b, v_lru_lambda, v_attn_out_norm, v_lru_out_norm, v_w_out, v_ffn2_norm, v_ffn2_w_gate, v_ffn2_w_up, v_ffn2_w_down, v_final_norm):
    args = locals()
    p = {nm: args[nm] for nm in WEIGHT_NAMES}
    mom = {nm: args["m_" + nm] for nm in WEIGHT_NAMES}
    var = {nm: args["v_" + nm] for nm in WEIGHT_NAMES}
    nb, seq, d = x.shape
    lp = CHUNK + seq
    xi, yi, ci = lax.axis_index("x"), lax.axis_index("y"), lax.axis_index("c")
    chip = 2 * xi + yi

    place = jnp.stack([chip, ci]).astype(jnp.int32)
    p2 = {nm: _to2d(nm, p[nm]) for nm in BIG_NAMES}
    m2 = {nm: _to2d(nm, mom[nm]) for nm in BIG_NAMES}
    v2 = {nm: _to2d(nm, var[nm]) for nm in BIG_NAMES}

    slots = {}
    small_shard = jnp.concatenate(
        [meta_tokens, conv_w[0].reshape(2, 2 * LRU_TILE), jnp.zeros((14, 2 * LRU_TILE), F32)], axis=0)
    small_slots = lax.dynamic_update_slice(jnp.zeros((N_SHARD,) + small_shard.shape, F32), small_shard[None],
                                           (chip, 0, 0))
    first = []
    comm = _gather_comm([small_slots], first.extend)
    for grp in BIG_GROUPS:
        for nm, buf in zip(grp, _cast_call("cast_" + grp[0], place, [p2[nm] for nm in grp], comm)):
            slots[nm] = buf
        comm = None
    w, meta = _small_weights(p, first[0])
    sched = _Sched(place, w, slots)

    h0 = jnp.concatenate(
        [jnp.zeros((nb, PAD_ROWS, d), F32), jnp.broadcast_to(meta[None], (nb, N_META, d)), x], axis=1)
    target = jnp.pad(loss_target, ((0, 0), (CHUNK, 0), (0, 0)))
    loss_part, dh0, g = _local_step(h0.reshape(nb * lp, d), target.reshape(nb * lp, d), w, nb, lp, sched)
    dh0 = dh0.reshape(nb, lp, d)
    grad_x = dh0[:, CHUNK:, :]

    late = jnp.concatenate([g["ffn1_norm"], g["mix_norm"], jnp.zeros((6, D_MODEL), F32),
                            jnp.sum(dh0[:, PAD_ROWS:CHUNK, :], axis=0)], axis=0)
    shared = sched.shared
    rest = [nm for at, names in CHIPS_AT[SHARE_EARLY_GROUPS:] if at is not None for nm in names]
    last = [nm for at, names in CHIPS_AT if at is None for nm in names]
    late_box = {}
    mine = sched.chip_sums(rest, _small_comm(late, lambda o: late_box.update(all=o[0])))
    share = _share_pair_comm([mine[nm] for nm in rest], lambda o: shared.update(zip(rest, o)))
    _comm_call("share_pair", _join_comms([share, sched.chips(last) if last else None]))
    if last:
        mine = sched.chip_sums(last)
        _comm_call("share_last", _share_pair_comm([mine[nm] for nm in last], lambda o: shared.update(zip(last, o))))
    late_all = late_box["all"]
    small_like = {nm: p[nm] for nm in SMALL_NAMES + ["gate_a_w", "gate_x_w"]}

    def state(t):
        return ([t[nm] for nm in SMALL_NAMES], t["gate_a_w"].reshape(32, D_MODEL), t["gate_x_w"].reshape(32, D_MODEL))

    me = (4 * xi + 2 * yi + ci).astype(jnp.int32).reshape(1)
    (gsum, dsm, msm, vsm), rows = _small_update_call(me, sched.early_all, sched.early, late_all, late,
                                                     [state(p), state(mom), state(var)])
    grads = _unpack_small(gsum, rows[0], small_like)
    delta = _unpack_small(dsm, rows[1], small_like)
    new_m = _unpack_small(msm, rows[2], small_like)
    new_v = _unpack_small(vsm, rows[3], small_like)
    loss = gsum[ROW_LOSS, 0]
    gmeta = gsum[ROW_META:ROW_META + N_META].reshape(N_META, N_SHARD, D_MODEL // N_SHARD)
    grads["meta_tokens"] = lax.dynamic_index_in_dim(gmeta, chip, axis=1, keepdims=False)
    gconv = gsum[ROW_CONV_W:ROW_CONV_W + 2].reshape(CONV_K, N_SHARD, LRU_TILE)
    grads["conv_w"] = lax.dynamic_index_in_dim(gconv, chip, axis=1, keepdims=False)[None]
    for nm in ("meta_tokens", "conv_w"):
        delta[nm], new_m[nm], new_v[nm] = _adamw_call("adamw_" + nm, p[nm], grads[nm], mom[nm], var[nm])

    for names in BIG_GROUPS:
        res = _adamw_group_call("adamw_" + names[0], [p2[nm] for nm in names], [shared[nm] for nm in names],
                                [m2[nm] for nm in names], [v2[nm] for nm in names])
        for nm, (gg, dd, mm, vv) in zip(names, res):
            grads[nm], delta[nm], new_m[nm], new_v[nm] = (_from2d(nm, t) for t in (gg, dd, mm, vv))

    return (loss, grad_x, *[grads[nm] for nm in WEIGHT_NAMES], *[delta[nm] for nm in WEIGHT_NAMES],
            *[new_m[nm] for nm in WEIGHT_NAMES], *[new_v[nm] for nm in WEIGHT_NAMES])
```

```python
import functools
import math

import jax
import jax.numpy as jnp
import numpy as np
from jax import lax
from jax.experimental import pallas as pl
from jax.experimental.pallas import tpu as pltpu

F32 = jnp.float32
BF16 = jnp.bfloat16
MESH = pl.DeviceIdType.MESH

D_MODEL = 1024
N_META = 16
CHUNK = 64
PAD_ROWS = CHUNK - N_META
HEADS = 4
D_NOPE = 128
D_ROPE = 64
D_QK = D_NOPE + D_ROPE
D_QKP = 256
D_V = 128
KV_RANK = 256
Q_RANK = 384
MLA_W = HEADS * D_V
LRU_W = 512
LRU_TILE = 128
N_LRU_TILES = LRU_W // LRU_TILE
CONV_K = 4
C_RGLRU = 8.0
ROPE_THETA = 10000.0
D_FF = 2816
N_SHARD = 4
EPS = 1e-6
NEG_INF = -1e30
Z_KR = Q_RANK + KV_RANK
Z_MLA = Z_KR + 128
Z_U = Z_MLA
Z_G = Z_U + LRU_W
Z_W = Z_G + LRU_W
IN_WIDTH = Q_RANK + KV_RANK + D_ROPE + 2 * LRU_W

ADAM_LR = 0.001
ADAM_B1 = 0.9
ADAM_B2 = 0.999
ADAM_EPS = 1e-08
ADAM_WD = 0.01
ADAM_STEP = 10

VMEM_LIMIT_BYTES = 56 * 1024 * 1024
MATMUL_ROWS = 1408
ELEMENTWISE_ROWS = 512
WGRAD_ROW_PARTS = 2
SMALL_ROWS = 104
SMALL_ADAM_ROWS = 80


def _params(sem):
    return pltpu.CompilerParams(dimension_semantics=sem, vmem_limit_bytes=VMEM_LIMIT_BYTES)


def _resident(shape):
    return pl.BlockSpec(tuple(shape), lambda i: (0,) * len(shape), pipeline_mode=pl.Buffered(1))


def _row_tile(rows, target):
    best = 16
    for t in range(16, min(rows, target) + 1, 16):
        if rows % t == 0:
            best = t
    return best


def _col_tile(cols, target):
    best = cols
    for t in range(128, min(cols, target) + 1, 128):
        if cols % t == 0:
            best = t
    return best


def _dot(a, b):
    return jnp.dot(a, b, preferred_element_type=F32)


def _dot_nt(a, b):
    return lax.dot_general(a, b, (((1,), (1,)), ((), ())), preferred_element_type=F32)


def _dot_tn(a, b):
    return lax.dot_general(a, b, (((0,), (0,)), ((), ())), preferred_element_type=F32)


def _rms(x, n):
    return lax.rsqrt(jnp.sum(x * x, axis=-1, keepdims=True) * (1.0 / n) + EPS)


def _rms_bwd(dn, nrm, r, n):
    return r * (dn - nrm * (jnp.sum(dn * nrm, axis=-1, keepdims=True) * (1.0 / n)))


def _gelu(x):
    k = math.sqrt(2.0 / math.pi)
    t = jnp.tanh(k * (x + 0.044715 * x * x * x))
    return 0.5 * x * (1.0 + t), t


def _gelu_grad(x, t):
    k = math.sqrt(2.0 / math.pi)
    return 0.5 * (1.0 + t) + 0.5 * x * (1.0 - t * t) * k * (1.0 + 3.0 * 0.044715 * x * x)


def _sigmoid(x):
    return 0.5 + 0.5 * jnp.tanh(0.5 * x)


def _softplus_neg(lam):
    e = jnp.exp(-jnp.abs(lam))
    log1p = jnp.where(e < 0.01, e * (1.0 - e * (0.5 - e * (1.0 / 3 - e * 0.25))), jnp.log(1.0 + e))
    return jnp.maximum(-lam, 0.0) + log1p


def _rope(t, c, s1, s2):
    return t * c + pltpu.roll(t, 96, 1) * s1 + pltpu.roll(t, 32, 1) * s2


def _rope_t(d, c, s1, s2):
    return d * c + pltpu.roll(d * s1, 32, 1) + pltpu.roll(d * s2, 96, 1)


def _rope_tables(lp):
    pos = (np.arange(lp, dtype=np.int32) - PAD_ROWS).astype(np.float32)
    inv_freq = (ROPE_THETA ** (-np.arange(0, D_ROPE // 2, dtype=np.float32) / (D_ROPE // 2))).astype(np.float32)
    ang = (pos[:, None] * inv_freq[None, :]).astype(np.float32).astype(np.float64)
    cos, sin = np.cos(ang).astype(np.float32), np.sin(ang).astype(np.float32)
    z = np.zeros_like(cos)
    return (jnp.asarray(np.concatenate([cos, cos, z, z], 1)), jnp.asarray(np.concatenate([-sin, z, z, z], 1)),
            jnp.asarray(np.concatenate([z, sin, z, z], 1)))


def _rmsnorm_call(name, h, g, tm, comm=None):
    rows, d = h.shape

    def body(h_ref, g_ref, o_ref):
        x = h_ref[...]
        o_ref[...] = (x * _rms(x, d) * g_ref[...]).astype(BF16)

    return _hosted_call(
        body, name=name, grid=(rows // tm,),
        in_specs=[pl.BlockSpec((tm, d), lambda i: (i, 0)), pl.BlockSpec((1, d), lambda i: (0, 0))],
        out_specs=[pl.BlockSpec((tm, d), lambda i: (i, 0))],
        out_shape=[jax.ShapeDtypeStruct((rows, d), BF16)],
        dims=("parallel",), args=(h, g), comm=comm)[0]


def _ffn_up_call(name, u, wg, wu, tm, comm=None):
    rows, d = u.shape
    ns, fs, _ = wg.shape

    def body(u_ref, wg_ref, wu_ref, g_ref, p_ref, a_ref):
        uu = u_ref[...]
        g = _dot_nt(uu, wg_ref[0])
        p = _dot_nt(uu, wu_ref[0])
        g_ref[0] = g.astype(BF16)
        p_ref[0] = p.astype(BF16)
        a_ref[0] = (g * jax.nn.sigmoid(g) * p).astype(BF16)

    wspec = pl.BlockSpec((1, fs, d), lambda s, i: (s, 0, 0))
    ospec = pl.BlockSpec((1, tm, fs), lambda s, i: (s, i, 0))
    oshape = jax.ShapeDtypeStruct((ns, rows, fs), BF16)
    return _hosted_call(
        body, name=name, grid=(ns, rows // tm),
        in_specs=[pl.BlockSpec((tm, d), lambda s, i: (i, 0)), wspec, wspec],
        out_specs=[ospec, ospec, ospec], out_shape=[oshape, oshape, oshape],
        dims=("parallel", "parallel"), args=(u, wg, wu), comm=comm)


def _ffn_gate_call(name, u, wg, tm, comm=None):
    rows, d = u.shape
    ns, fs, _ = wg.shape

    def body(u_ref, wg_ref, g_ref):
        g_ref[0] = _dot_nt(u_ref[...], wg_ref[0]).astype(BF16)

    return _hosted_call(
        body, name=name, grid=(ns, rows // tm),
        in_specs=[pl.BlockSpec((tm, d), lambda s, i: (i, 0)), pl.BlockSpec((1, fs, d), lambda s, i: (s, 0, 0))],
        out_specs=[pl.BlockSpec((1, tm, fs), lambda s, i: (s, i, 0))],
        out_shape=[jax.ShapeDtypeStruct((ns, rows, fs), BF16)],
        dims=("parallel", "parallel"), args=(u, wg), comm=comm)[0]


def _ffn_upact_call(name, u, wu, gate, tm, comm=None):
    rows, d = u.shape
    ns, fs, _ = wu.shape

    def body(u_ref, wu_ref, g_ref, p_ref, a_ref):
        p = _dot_nt(u_ref[...], wu_ref[0])
        g = g_ref[0].astype(F32)
        p_ref[0] = p.astype(BF16)
        a_ref[0] = (g * jax.nn.sigmoid(g) * p).astype(BF16)

    ospec = pl.BlockSpec((1, tm, fs), lambda s, i: (s, i, 0))
    oshape = jax.ShapeDtypeStruct((ns, rows, fs), BF16)
    return _hosted_call(
        body, name=name, grid=(ns, rows // tm),
        in_specs=[pl.BlockSpec((tm, d), lambda s, i: (i, 0)), pl.BlockSpec((1, fs, d), lambda s, i: (s, 0, 0)), ospec],
        out_specs=[ospec, ospec], out_shape=[oshape, oshape],
        dims=("parallel", "parallel"), args=(u, wu, gate), comm=comm)


def _loss_tail(x, g, t, row, d):
    r = _rms(x, d)
    n = x * r
    err = jnp.where(row >= CHUNK, n * g - t, 0.0)
    dout = err * (1.0 / d)
    dh = _rms_bwd(dout * g, n, r, d)
    dg = jnp.sum(dout * n, axis=0, keepdims=True)
    part = jnp.sum(jnp.sum(err * err, axis=1, keepdims=True), axis=0, keepdims=True) * (0.5 / d)
    return dh, dg, jnp.broadcast_to(part, (1, 128))


def _ffn_down_call(name, a, wd, h, tm, comm=None, next_gain=None):
    rows, d = h.shape
    ns, _, fs = a.shape
    more = next_gain is not None

    def body(a_ref, wd_ref, h_ref, *rest):
        acc = h_ref[...]
        for s in range(ns):
            acc = acc + 0.5 * _dot(a_ref[s], wd_ref[s])
        rest[-2 if more else -1][...] = acc
        if more:
            rest[-1][...] = (acc * _rms(acc, d) * rest[0][...]).astype(BF16)

    full = pl.BlockSpec((tm, d), lambda i: (i, 0))
    res = _hosted_call(
        body, name=name, grid=(rows // tm,),
        in_specs=[pl.BlockSpec((ns, tm, fs), lambda i: (0, i, 0)), _resident((ns, fs, d)), full]
        + ([pl.BlockSpec((1, d), lambda i: (0, 0))] if more else []),
        out_specs=[full] * (2 if more else 1),
        out_shape=[jax.ShapeDtypeStruct((rows, d), F32)] + ([jax.ShapeDtypeStruct((rows, d), BF16)] if more else []),
        dims=("parallel",), args=(a, wd, h) + ((next_gain,) if more else ()), comm=comm)
    return res if more else res[0]


def _ffn_down_loss_call(name, a, wd, h, g, target, lp, tm):
    rows, d = h.shape
    ns, _, fs = a.shape
    tpe = lp // tm

    def body(a_ref, wd_ref, h_ref, g_ref, t_ref, dh_ref, dhb_ref, dg_ref, loss_ref):
        i = pl.program_id(0)
        acc = h_ref[...]
        for s in range(ns):
            acc = acc + 0.5 * _dot(a_ref[s], wd_ref[s])
        row = (i % tpe) * tm + lax.broadcasted_iota(jnp.int32, (tm, 1), 0)
        dh, dg, loss = _loss_tail(acc, g_ref[...], t_ref[...], row, d)
        dh_ref[...] = dh
        dhb_ref[...] = dh.astype(BF16)

        @pl.when(i == 0)
        def _():
            dg_ref[...] = dg
            loss_ref[...] = loss

        @pl.when(i != 0)
        def _():
            dg_ref[...] += dg
            loss_ref[...] += loss

    full = pl.BlockSpec((tm, d), lambda i: (i, 0))
    gspec = pl.BlockSpec((1, d), lambda i: (0, 0))
    return _hosted_call(
        body, name=name, grid=(rows // tm,),
        in_specs=[pl.BlockSpec((ns, tm, fs), lambda i: (0, i, 0)), _resident((ns, fs, d)), full, gspec, full],
        out_specs=[full, full, gspec, pl.BlockSpec((1, 128), lambda i: (0, 0))],
        out_shape=[jax.ShapeDtypeStruct((rows, d), F32), jax.ShapeDtypeStruct((rows, d), BF16),
                   jax.ShapeDtypeStruct((1, d), F32), jax.ShapeDtypeStruct((1, 128), F32)],
        args=(a, wd, h, g, target))


def _mm_call(name, a, bt, tm, out_dtype):
    rows, k = a.shape
    n = bt.shape[0]

    def body(a_ref, b_ref, o_ref):
        o_ref[...] = _dot_nt(a_ref[...], b_ref[...]).astype(out_dtype)

    return pl.pallas_call(
        body, name=name, grid=(rows // tm,),
        in_specs=[pl.BlockSpec((tm, k), lambda i: (i, 0)), pl.BlockSpec((n, k), lambda i: (0, 0))],
        out_specs=pl.BlockSpec((tm, n), lambda i: (i, 0)),
        out_shape=jax.ShapeDtypeStruct((rows, n), out_dtype),
        compiler_params=_params(("parallel",)))(a, bt)


def _mla_heads(z, gql, gkvl, wuq, wuk, wuv):
    cq = z[:, 0:Q_RANK]
    ckv = z[:, Q_RANK:Z_KR]
    kr = z[:, Z_KR:Z_MLA]
    rq = _rms(cq, Q_RANK)
    nq = cq * rq
    cqn = (nq * gql).astype(BF16)
    rkv = _rms(ckv, KV_RANK)
    nkv = ckv * rkv
    ckvn = (nkv * gkvl).astype(BF16)
    qraw = _dot_nt(cqn, wuq)
    knope = _dot(ckvn, wuk)
    v = _dot(ckvn, wuv)
    skr = jnp.sum(kr * kr, axis=-1, keepdims=True)
    heads = []
    for hd in range(HEADS):
        qh = qraw[:, hd * D_QKP:(hd + 1) * D_QKP]
        rqh = lax.rsqrt(jnp.sum(qh * qh, axis=-1, keepdims=True) * (1.0 / D_QK) + EPS)
        kn = knope[:, hd * D_NOPE:(hd + 1) * D_NOPE]
        rkh = lax.rsqrt((jnp.sum(kn * kn, axis=-1, keepdims=True) + skr) * (1.0 / D_QK) + EPS)
        heads.append((qh * rqh, rqh, kn * rkh, kr * rkh, rkh))
    return dict(rq=rq, nq=nq, cqn=cqn, rkv=rkv, nkv=nkv, ckvn=ckvn, v=v, heads=heads)


def _mla_prep_call(z, gql, gkvl, gqh, gkh, wuq, wuk, wuv, tabs, lp, tm):
    rows = z.shape[0]
    tpe = lp // tm

    def body(z_ref, gql_ref, gkvl_ref, gqh_ref, gkh_ref, wuq_ref, wuk_ref, wuv_ref, c_ref, s1_ref, s2_ref,
             q_ref, k_ref, v_ref):
        m = _mla_heads(z_ref[...], gql_ref[...], gkvl_ref[...], wuq_ref[...], wuk_ref[...], wuv_ref[...])
        c, s1, s2 = c_ref[...], s1_ref[...], s2_ref[...]
        gq, gk = gqh_ref[...], gkh_ref[...]
        row = (pl.program_id(0) % tpe) * tm + lax.broadcasted_iota(jnp.int32, (tm, 1), 0)
        spare = (lax.broadcasted_iota(jnp.int32, (1, D_QKP - D_NOPE), 1) == D_ROPE).astype(F32)
        kmask = jnp.where(row < PAD_ROWS, NEG_INF * math.sqrt(D_QK), 0.0) * spare
        for hd in range(HEADS):
            qn, _, knn, krn, _ = m["heads"][hd]
            qg = qn * gq
            q_ref[hd, :, 0:D_NOPE] = qg[:, 0:D_NOPE].astype(BF16)
            q_ref[hd, :, D_NOPE:D_QKP] = (_rope(qg[:, D_NOPE:D_QKP], c, s1, s2) + spare).astype(BF16)
            k_ref[hd, :, 0:D_NOPE] = (knn * gk[:, 0:D_NOPE]).astype(BF16)
            k_ref[hd, :, D_NOPE:D_QKP] = (_rope(krn * gk[:, D_NOPE:D_QKP], c, s1, s2) + kmask).astype(BF16)
            v_ref[hd] = m["v"][:, hd * D_V:(hd + 1) * D_V].astype(BF16)

    def const(shape):
        return pl.BlockSpec(shape, lambda i: tuple(0 for _ in shape))

    tab = pl.BlockSpec((tm, 128), lambda i: (i % tpe, 0))
    return pl.pallas_call(
        body, name="mla_prep", grid=(rows // tm,),
        in_specs=[pl.BlockSpec((tm, Z_MLA), lambda i: (i, 0)), const((1, Q_RANK)), const((1, KV_RANK)),
                  const((1, D_QKP)), const((1, D_QKP)), const((HEADS * D_QKP, Q_RANK)),
                  const((KV_RANK, HEADS * D_NOPE)), const((KV_RANK, HEADS * D_V)), tab, tab, tab],
        out_specs=[pl.BlockSpec((HEADS, tm, D_QKP), lambda i: (0, i, 0)),
                   pl.BlockSpec((HEADS, tm, D_QKP), lambda i: (0, i, 0)),
                   pl.BlockSpec((HEADS, tm, D_V), lambda i: (0, i, 0))],
        out_shape=[jax.ShapeDtypeStruct((HEADS, rows, D_QKP), BF16),
                   jax.ShapeDtypeStruct((HEADS, rows, D_QKP), BF16),
                   jax.ShapeDtypeStruct((HEADS, rows, D_V), BF16)],
        compiler_params=_params(("parallel",)))(z, gql, gkvl, gqh, gkh, wuq, wuk, wuv, *tabs)


Q_BLOCK_ROWS = 528
Q_BLOCK_ROWS_BWD = 192


def _q_block(lp):
    return _row_tile(lp, Q_BLOCK_ROWS)


def _key_end(ext, lp):
    return min(lp, -(-ext // CHUNK) * CHUNK)


def _diag_bias(qb, j0, nk):
    shift = CHUNK.bit_length() - 1
    r = jnp.right_shift(j0 + lax.broadcasted_iota(jnp.int32, (qb, nk), 0), shift)
    c = jnp.right_shift(j0 + lax.broadcasted_iota(jnp.int32, (qb, nk), 1), shift)
    return jnp.where(c <= r, 0.0, NEG_INF)


def _attn_fwd_call(q, k, v, nb, lp, comm=None):
    rows = nb * lp
    qb = _q_block(lp)
    scale = 1.0 / math.sqrt(D_QK)

    def body(q_ref, k_ref, v_ref, o_ref, lse_ref):
        for j in range(lp // qb):
            j0, ext = j * qb, (j + 1) * qb
            kend = _key_end(ext, lp)
            qj = q_ref[0, j0:ext, :]
            sd = _dot_nt(qj, k_ref[0, j0:kend, :]) * scale + _diag_bias(qb, j0, kend - j0)
            mx = jnp.max(sd, axis=-1, keepdims=True)
            if j > 0:
                so = _dot_nt(qj, k_ref[0, 0:j0, :]) * scale
                mx = jnp.maximum(mx, jnp.max(so, axis=-1, keepdims=True))
            pd = jnp.exp(sd - mx)
            l = jnp.sum(pd, axis=-1, keepdims=True)
            o = _dot(pd.astype(BF16), v_ref[0, j0:kend, :])
            if j > 0:
                po = jnp.exp(so - mx)
                l = l + jnp.sum(po, axis=-1, keepdims=True)
                o = o + _dot(po.astype(BF16), v_ref[0, 0:j0, :])
            o_ref[j0:ext, :] = o / l
            lse_ref[0, j0:ext, :] = mx + jnp.log(l)

    return _hosted_call(
        body, name="attn_fwd", grid=(nb, HEADS),
        in_specs=[pl.BlockSpec((1, lp, D_QKP), lambda b, h: (h, b, 0)),
                  pl.BlockSpec((1, lp, D_QKP), lambda b, h: (h, b, 0)),
                  pl.BlockSpec((1, lp, D_V), lambda b, h: (h, b, 0))],
        out_specs=[pl.BlockSpec((lp, D_V), lambda b, h: (b, h)),
                   pl.BlockSpec((1, lp, 1), lambda b, h: (h, b, 0))],
        out_shape=[jax.ShapeDtypeStruct((rows, MLA_W), F32),
                   jax.ShapeDtypeStruct((HEADS, rows, 1), F32)],
        dims=("parallel", "parallel"), args=(q, k, v), comm=comm)


def _attn_bwd_call(q, k, v, o, lse, do, nb, lp, comm=None):
    rows = nb * lp
    qb = _row_tile(lp, Q_BLOCK_ROWS_BWD)
    scale = 1.0 / math.sqrt(D_QK)

    def body(q_ref, k_ref, v_ref, o_ref, lse_ref, do_ref, dq_ref, dk_ref, dv_ref, dk_acc, dv_acc):
        dk_acc[...] = jnp.zeros_like(dk_acc)
        dv_acc[...] = jnp.zeros_like(dv_acc)
        shift = CHUNK.bit_length() - 1
        for j in range(lp // qb):
            j0, ext = j * qb, (j + 1) * qb
            kend = _key_end(ext, lp)
            qj = q_ref[0, j0:ext, :]
            doj = do_ref[j0:ext, :]
            delta = jnp.sum(doj * o_ref[j0:ext, :], axis=-1, keepdims=True)
            dob = doj.astype(BF16)
            kk = k_ref[0, 0:kend, :]
            qchunk = jnp.right_shift(j0 + lax.broadcasted_iota(jnp.int32, (qb, kend), 0), shift)
            kchunk = jnp.right_shift(lax.broadcasted_iota(jnp.int32, (qb, kend), 1), shift)
            s = _dot_nt(qj, kk) * scale - lse_ref[0, j0:ext, :]
            p = jnp.where(kchunk <= qchunk, jnp.exp(s), 0.0)
            dv_acc[0:kend, :] += _dot_tn(p.astype(BF16), dob)
            dp = _dot_nt(dob, v_ref[0, 0:kend, :])
            ds = (p * (dp - delta) * scale).astype(BF16)
            dq_ref[0, j0:ext, :] = _dot(ds, kk).astype(BF16)
            dk_acc[0:kend, :] += _dot_tn(ds, qj)
        dk_ref[0] = dk_acc[...].astype(BF16)
        dv_ref[0] = dv_acc[...].astype(BF16)

    qspec = pl.BlockSpec((1, lp, D_QKP), lambda b, h: (h, b, 0))
    vspec = pl.BlockSpec((1, lp, D_V), lambda b, h: (h, b, 0))
    ospec = pl.BlockSpec((lp, D_V), lambda b, h: (b, h))
    return _hosted_call(
        body, name="attn_bwd", grid=(nb, HEADS),
        in_specs=[qspec, qspec, vspec, ospec, pl.BlockSpec((1, lp, 1), lambda b, h: (h, b, 0)), ospec],
        out_specs=[qspec, qspec, vspec],
        out_shape=[jax.ShapeDtypeStruct((HEADS, rows, D_QKP), BF16),
                   jax.ShapeDtypeStruct((HEADS, rows, D_QKP), BF16),
                   jax.ShapeDtypeStruct((HEADS, rows, D_V), BF16)],
        scratch_shapes=[pltpu.VMEM((lp, D_QKP), F32), pltpu.VMEM((lp, D_V), F32)],
        dims=("parallel", "parallel"), args=(q, k, v, o, lse, do), comm=comm)


def _lru_gates(u, cw, cb, wa, ba, wx, bx, lam, lp):
    xc = (cw[3:4, :] * u + cw[2:3, :] * pltpu.roll(u, 1, 0) + cw[1:2, :] * pltpu.roll(u, 2, 0)
          + cw[0:1, :] * pltpu.roll(u, 3, 0) + cb)
    xcb = xc.astype(BF16)
    r = _sigmoid(_dot(xcb, wa) + ba)
    i = _sigmoid(_dot(xcb, wx) + bx)
    sp = _softplus_neg(lam)
    la = -C_RGLRU * r * sp
    a = jnp.exp(la)
    x2 = 2.0 * la
    e2 = a * a
    m2 = jnp.maximum(jnp.where(x2 > -0.01, -x2 * (1.0 + 0.5 * x2), 1.0 - e2), 1e-30)
    rs = lax.rsqrt(m2)
    row = lax.broadcasted_iota(jnp.int32, (lp, LRU_TILE), 0)
    first = row == PAD_ROWS
    valid = row >= PAD_ROWS
    mult_eff = jnp.where(first, 1.0, m2 * rs)
    return dict(xc=xc, xcb=xcb, r=r, i=i, sp=sp, a=a, e2=e2, rs=rs, mult_eff=mult_eff, first=first, valid=valid)


def _scan_rows(a, b, a_s, b_s, out_ref, lp, reverse):
    sub = lax.broadcasted_iota(jnp.int32, (lp, LRU_TILE), 0) & 7
    for dist in (1, 2, 4):
        shift = lp - dist if reverse else dist
        keep = (sub + dist <= 7) if reverse else (sub >= dist)
        a_sh = pltpu.roll(a, shift, 0)
        b_sh = pltpu.roll(b, shift, 0)
        b = jnp.where(keep, a * b_sh + b, b)
        a = jnp.where(keep, a * a_sh, a)
    a_s[...] = a
    b_s[...] = b
    n_groups = lp // 8
    edge = 0 if reverse else 7

    def group(gi, carry):
        r0 = pl.multiple_of(((n_groups - 1 - gi) if reverse else gi) * 8, 8)
        a8 = a_s[pl.ds(r0, 8), :]
        b8 = b_s[pl.ds(r0, 8), :]
        out_ref[pl.ds(r0, 8), :] = a8 * carry + b8
        return a8[edge:edge + 1, :] * carry + b8[edge:edge + 1, :]

    lax.fori_loop(0, n_groups, group, jnp.zeros((1, LRU_TILE), F32), unroll=4)


def _lru_specs(lp):
    seq = lambda col0, stride=1: pl.BlockSpec((lp, LRU_TILE), lambda t, b: (b, col0 + stride * t))
    cw = pl.BlockSpec((1, CONV_K, LRU_TILE), lambda t, b: (t, 0, 0))
    vec = pl.BlockSpec((1, LRU_TILE), lambda t, b: (0, t))
    mat = pl.BlockSpec((1, LRU_TILE, LRU_TILE), lambda t, b: (t, 0, 0))
    return seq, cw, vec, mat


def _lru_fwd_call(z, cw, cb, wa, ba, wx, bx, lam, nb, lp, comm=None):
    rows = nb * lp
    seq, cwspec, vec, mat = _lru_specs(lp)

    def body(u_ref, g_ref, cw_ref, cb_ref, wa_ref, ba_ref, wx_ref, bx_ref, lam_ref, y_ref, hs_ref, a_s, b_s):
        m = _lru_gates(u_ref[...], cw_ref[0], cb_ref[...], wa_ref[0], ba_ref[...], wx_ref[0], bx_ref[...],
                       lam_ref[...], lp)
        a = jnp.where(m["valid"], m["a"], 0.0)
        b = jnp.where(m["valid"], m["mult_eff"] * (m["i"] * m["xc"]), 0.0)
        _scan_rows(a, b, a_s, b_s, hs_ref, lp, reverse=False)
        gl, _ = _gelu(g_ref[...])
        y_ref[...] = hs_ref[...] * gl

    oshape = jax.ShapeDtypeStruct((rows, LRU_W), F32)
    return _hosted_call(
        body, name="lru_fwd", grid=(N_LRU_TILES, nb),
        in_specs=[seq(Z_U // LRU_TILE, 2), seq(Z_U // LRU_TILE + 1, 2), cwspec, vec, mat, vec, mat, vec, vec],
        out_specs=[seq(0), seq(0)], out_shape=[oshape, oshape],
        scratch_shapes=[pltpu.VMEM((lp, LRU_TILE), F32), pltpu.VMEM((lp, LRU_TILE), F32)],
        dims=("parallel", "parallel"), args=(z, z, cw, cb, wa, ba, wx, bx, lam), comm=comm)


def _lru_bwd_call(z, dz, hs, dy, cw, cb, wa, ba, wx, bx, lam, nb, lp, comm=None):
    rows = nb * lp
    seq, cwspec, vec, mat = _lru_specs(lp)

    def body(u_ref, g_ref, hs_ref, dy_ref, cw_ref, cb_ref, wa_ref, ba_ref, wx_ref, bx_ref, lam_ref, dz_in,
             dz_ref, dcw_ref, dcb_ref, dwa_ref, dba_ref, dwx_ref, dbx_ref, dlam_ref, a_s, b_s, d_s):
        du_ref = dz_ref.at[:, 0:LRU_TILE]
        dg_ref = dz_ref.at[:, LRU_TILE:2 * LRU_TILE]
        b_idx = pl.program_id(1)
        u = u_ref[...]
        cw = cw_ref[0]
        wa, wx = wa_ref[0], wx_ref[0]
        lam = lam_ref[...]
        m = _lru_gates(u, cw, cb_ref[...], wa, ba_ref[...], wx, bx_ref[...], lam, lp)
        gate = g_ref[...]
        gl, th = _gelu(gate)
        dy = dy_ref[...]
        hs = hs_ref[...]
        dg_ref[...] = (dy * hs * _gelu_grad(gate, th)).astype(BF16)
        a_eff = jnp.where(m["valid"], m["a"], 0.0)
        _scan_rows(pltpu.roll(a_eff, lp - 1, 0), dy * gl, a_s, b_s, d_s, lp, reverse=True)
        ds = d_s[...]
        xc, r, i = m["xc"], m["r"], m["i"]
        row = lax.broadcasted_iota(jnp.int32, (lp, LRU_TILE), 0)
        da = ds * jnp.where(row >= 1, pltpu.roll(hs, 1, 0), 0.0)
        db = jnp.where(m["valid"], ds, 0.0)
        di = db * m["mult_eff"] * xc
        dxc = db * m["mult_eff"] * i
        live = m["valid"] & jnp.logical_not(m["first"])
        dm = jnp.where(live, db * i * xc, 0.0)
        dla = da * m["a"] - dm * (m["e2"] * m["rs"])
        dr = dla * (-C_RGLRU * m["sp"])
        dsp = jnp.sum(dla * (-C_RGLRU * r), axis=0, keepdims=True)
        dpr = (dr * r * (1.0 - r))
        dpi = (di * i * (1.0 - i))
        dprb, dpib = dpr.astype(BF16), dpi.astype(BF16)
        dxc = dxc + _dot_nt(dprb, wa) + _dot_nt(dpib, wx)
        du = (cw[3:4, :] * dxc + cw[2:3, :] * pltpu.roll(dxc, lp - 1, 0) + cw[1:2, :] * pltpu.roll(dxc, lp - 2, 0)
              + cw[0:1, :] * pltpu.roll(dxc, lp - 3, 0))
        du_ref[...] = jnp.where(m["valid"], du, 0.0).astype(BF16)
        tap = lax.broadcasted_iota(jnp.int32, (CONV_K, LRU_TILE), 0)
        dcw = jnp.zeros((CONV_K, LRU_TILE), F32)
        for kk in range(CONV_K):
            shifted = u if kk == CONV_K - 1 else pltpu.roll(u, CONV_K - 1 - kk, 0)
            dcw = jnp.where(tap == kk, jnp.sum(dxc * shifted, axis=0, keepdims=True), dcw)
        parts = [(dcw_ref, dcw[None]), (dcb_ref, jnp.sum(dxc, axis=0, keepdims=True)[None]),
                 (dwa_ref, _dot_tn(m["xcb"], dprb)[None]), (dba_ref, jnp.sum(dpr, axis=0, keepdims=True)[None]),
                 (dwx_ref, _dot_tn(m["xcb"], dpib)[None]), (dbx_ref, jnp.sum(dpi, axis=0, keepdims=True)[None]),
                 (dlam_ref, (dsp * (-jax.nn.sigmoid(-lam)))[None])]

        @pl.when(b_idx == 0)
        def _():
            for ref, val in parts:
                ref[...] = val

        @pl.when(b_idx != 0)
        def _():
            for ref, val in parts:
                ref[...] += val

    vec3 = pl.BlockSpec((1, 1, LRU_TILE), lambda t, b: (t, 0, 0))
    vshape = jax.ShapeDtypeStruct((N_LRU_TILES, 1, LRU_TILE), F32)
    mshape = jax.ShapeDtypeStruct((N_LRU_TILES, LRU_TILE, LRU_TILE), F32)
    pair = pl.BlockSpec((lp, 2 * LRU_TILE), lambda t, b: (b, Z_U // (2 * LRU_TILE) + t))
    return _hosted_call(
        body, name="lru_bwd", grid=(N_LRU_TILES, nb),
        in_specs=[seq(Z_U // LRU_TILE, 2), seq(Z_U // LRU_TILE + 1, 2), seq(0), seq(0), cwspec, vec, mat, vec, mat,
                  vec, vec, pl.BlockSpec(memory_space=pl.ANY)],
        out_specs=[pair, cwspec, vec3, mat, vec3, mat, vec3, vec3],
        out_shape=[jax.ShapeDtypeStruct(dz.shape, dz.dtype), jax.ShapeDtypeStruct((N_LRU_TILES, CONV_K, LRU_TILE), F32),
                   vshape, mshape, vshape, mshape, vshape, vshape],
        scratch_shapes=[pltpu.VMEM((lp, LRU_TILE), F32)] * 3,
        dims=("parallel", "arbitrary"), args=(z, z, hs, dy, cw, cb, wa, ba, wx, bx, lam, dz), comm=comm,
        aliases={11: 0})


def _mix_out_call(ya, yl, ga, gl, wout, h, next_gain, tm):
    rows, d = h.shape

    def body(ya_ref, yl_ref, ga_ref, gl_ref, w_ref, h_ref, ng_ref, y_ref, o_ref, u_ref):
        a = ya_ref[...]
        l = yl_ref[...]
        an = (a * _rms(a, MLA_W) * ga_ref[...]).astype(BF16)
        ln = (l * _rms(l, LRU_W) * gl_ref[...]).astype(BF16)
        y_ref[:, 0:MLA_W] = an
        y_ref[:, MLA_W:MLA_W + LRU_W] = ln
        out = h_ref[...] + _dot(an, w_ref[0:MLA_W, :]) + _dot(ln, w_ref[MLA_W:MLA_W + LRU_W, :])
        o_ref[...] = out
        u_ref[...] = (out * _rms(out, d) * ng_ref[...]).astype(BF16)

    half = pl.BlockSpec((tm, MLA_W), lambda i: (i, 0))
    g = pl.BlockSpec((1, MLA_W), lambda i: (0, 0))
    full = pl.BlockSpec((tm, d), lambda i: (i, 0))
    return pl.pallas_call(
        body, name="mix_out", grid=(rows // tm,),
        in_specs=[half, half, g, g, pl.BlockSpec((MLA_W + LRU_W, d), lambda i: (0, 0)), full,
                  pl.BlockSpec((1, d), lambda i: (0, 0))],
        out_specs=[full, full, full],
        out_shape=[jax.ShapeDtypeStruct((rows, MLA_W + LRU_W), BF16), jax.ShapeDtypeStruct((rows, d), F32),
                   jax.ShapeDtypeStruct((rows, d), BF16)],
        compiler_params=_params(("parallel",)))(ya, yl, ga, gl, wout, h, next_gain)


def _ffn_dact_call(name, dhb, wd, gate, up, tm, comm=None):
    rows, d = dhb.shape
    ns, fs, _ = wd.shape

    nsub = 2 if tm % 32 == 0 else 1
    sub = tm // nsub

    def body(dh_ref, wd_ref, g_ref, p_ref, dg_ref, dp_ref):
        wd = wd_ref[0]
        for r in range(nsub):
            rs = slice(r * sub, (r + 1) * sub)
            da = (0.5 * _dot_nt(dh_ref[rs, :], wd)).astype(BF16)
            g = g_ref[0, rs, :]
            p = p_ref[0, rs, :]
            sg = jax.nn.sigmoid(g)
            dg_ref[0, rs, :] = (da * p) * (sg * (1.0 + g * (1.0 - sg)))
            dp_ref[0, rs, :] = da * (g * sg)

    aspec = pl.BlockSpec((1, tm, fs), lambda s, i: (s, i, 0))
    oshape = jax.ShapeDtypeStruct((ns, rows, fs), BF16)
    return _hosted_call(
        body, name=name, grid=(ns, rows // tm),
        in_specs=[pl.BlockSpec((tm, d), lambda s, i: (i, 0)), pl.BlockSpec((1, fs, d), lambda s, i: (s, 0, 0)),
                  aspec, aspec],
        out_specs=[aspec, aspec], out_shape=[oshape, oshape],
        dims=("parallel", "parallel"), args=(dhb, wd, gate, up), comm=comm)


def _norm_in_bwd_call(name, pieces, h, g, dres, tm, comm=None, part=(0, 1), prev=None, mix=None):
    rows, d = h.shape
    npc = len(pieces)
    steps = rows // tm // part[1]
    off = part[0] * steps
    n_prev = 0 if prev is None else 2
    n_mix = 0 if mix is None else 5

    def body(*refs):
        d_refs = refs[0:2 * npc:2]
        w_refs = refs[1:2 * npc:2]
        h_ref, g_ref, dres_ref = refs[2 * npc:2 * npc + 3]
        mix_in_refs = refs[2 * npc + 3:2 * npc + 3 + n_mix]
        dh_ref, dhb_ref, dg_ref = refs[2 * npc + 3 + n_mix + n_prev:2 * npc + 6 + n_mix + n_prev]
        mix_out_refs = refs[2 * npc + 6 + n_mix + n_prev:]
        du = jnp.zeros((tm, d), F32)
        for d_ref, w_ref in zip(d_refs, w_refs):
            if len(d_ref.shape) == 3:
                for s in range(d_ref.shape[0]):
                    du = du + _dot(d_ref[s], w_ref[s])
            else:
                du = du + _dot(d_ref[...], w_ref[...])
        x = h_ref[...]
        r = _rms(x, d)
        n = x * r
        dh = dres_ref[...] + _rms_bwd(du * g_ref[...], n, r, d)
        dhb = dh.astype(BF16)
        dh_ref[...] = dh
        dhb_ref[...] = dhb
        sums = [(dg_ref, jnp.sum(du * n, axis=0, keepdims=True))]
        if mix is not None:
            wo_ref, ya_ref, yl_ref, ga_ref, gl_ref = mix_in_refs
            dya_ref, dyl_ref, dga_ref, dgl_ref = mix_out_refs
            dy = _dot_nt(dhb, wo_ref[...])
            for val, gain_ref, lo, out_ref, acc_ref in ((ya_ref[...], ga_ref, 0, dya_ref, dga_ref),
                                                        (yl_ref[...], gl_ref, MLA_W, dyl_ref, dgl_ref)):
                rb = _rms(val, MLA_W)
                nb_ = val * rb
                dyn = dy[:, lo:lo + MLA_W]
                out_ref[...] = _rms_bwd(dyn * gain_ref[...], nb_, rb, MLA_W)
                sums.append((acc_ref, jnp.sum(dyn * nb_, axis=0, keepdims=True)))

        @pl.when(pl.program_id(0) == 0)
        def _():
            for ref, val in sums:
                ref[...] = val

        @pl.when(pl.program_id(0) != 0)
        def _():
            for ref, val in sums:
                ref[...] += val

    in_specs, args = [], []
    for dd, w in pieces:
        if dd.ndim == 3:
            in_specs.append(pl.BlockSpec((dd.shape[0], tm, dd.shape[2]), lambda i: (0, i + off, 0)))
            in_specs.append(_resident(w.shape))
        else:
            in_specs.append(pl.BlockSpec((tm, dd.shape[1]), lambda i: (i + off, 0)))
            in_specs.append(_resident(w.shape))
        args += [dd, w]
    full = pl.BlockSpec((tm, d), lambda i: (i + off, 0))
    gspec = pl.BlockSpec((1, d), lambda i: (0, 0))
    half = pl.BlockSpec((tm, MLA_W), lambda i: (i + off, 0))
    hgain = pl.BlockSpec((1, MLA_W), lambda i: (0, 0))
    mix_in_specs = [] if mix is None else [_resident(mix[0].shape), half, half, hgain, hgain]
    mix_out_specs = [] if mix is None else [half, half, hgain, hgain]
    mix_out_shapes = [] if mix is None else [
        jax.ShapeDtypeStruct((rows, MLA_W), F32), jax.ShapeDtypeStruct((rows, LRU_W), F32),
        jax.ShapeDtypeStruct((1, MLA_W), F32), jax.ShapeDtypeStruct((1, LRU_W), F32)]
    n_in = len(in_specs) + 3 + n_mix
    return _hosted_call(
        body, name=name, grid=(steps,),
        in_specs=in_specs + [full, gspec, full] + mix_in_specs + _any_specs(n_prev),
        out_specs=[full, full, gspec] + mix_out_specs,
        out_shape=[jax.ShapeDtypeStruct((rows, d), F32), jax.ShapeDtypeStruct((rows, d), BF16),
                   jax.ShapeDtypeStruct((1, d), F32)] + mix_out_shapes,
        args=(*args, h, g, dres, *(mix or ()), *(prev or ())), comm=comm,
        aliases={n_in: 0, n_in + 1: 1} if prev is not None else {})


def _wgrad_call(name, a, b, scale=1.0, comm=None):
    a3, b3 = a.ndim == 3, b.ndim == 3
    ns = a.shape[0] if a3 else (b.shape[0] if b3 else 1)
    rows, m = a.shape[-2:]
    n = b.shape[-1]
    tmm = m if a3 else _col_tile(m, 256)
    nk = WGRAD_ROW_PARTS if rows % (16 * WGRAD_ROW_PARTS) == 0 else 1
    rk = rows // nk

    def body(a_ref, b_ref, o_ref):
        av = a_ref[0] if a3 else a_ref[...]
        bv = b_ref[0] if b3 else b_ref[...]
        res = _dot_tn(av, bv)
        if scale != 1.0:
            res = res * scale
        out = o_ref.at[0] if (a3 or b3) else o_ref

        @pl.when(pl.program_id(2) == 0)
        def _():
            out[...] = res

        @pl.when(pl.program_id(2) != 0)
        def _():
            out[...] += res

    aspec = (pl.BlockSpec((1, rk, tmm), lambda s, j, k: (s, k, j)) if a3
             else pl.BlockSpec((rk, tmm), lambda s, j, k: (k, j)))
    bspec = (pl.BlockSpec((1, rk, n), lambda s, j, k: (s, k, 0)) if b3
             else pl.BlockSpec((rk, n), lambda s, j, k: (k, 0)))
    if a3 or b3:
        ospec = pl.BlockSpec((1, tmm, n), lambda s, j, k: (s, j, 0))
        oshape = jax.ShapeDtypeStruct((ns, m, n), F32)
    else:
        ospec = pl.BlockSpec((tmm, n), lambda s, j, k: (j, 0))
        oshape = jax.ShapeDtypeStruct((m, n), F32)
    return _hosted_call(
        body, name=name, grid=(ns, m // tmm, nk), in_specs=[aspec, bspec], out_specs=[ospec], out_shape=[oshape],
        dims=("parallel", "parallel", "arbitrary"), args=(a, b), comm=comm)[0]


def _mla_prep_bwd_call(z, dq, dk, dv, gql, gkvl, gqh, gkh, wuq, wuk, wuv, tabs, lp, tm, comm=None):
    rows = z.shape[0]
    tpe = lp // tm

    def body(z_ref, dq_ref, dk_ref, dv_ref, gql_ref, gkvl_ref, gqh_ref, gkh_ref, wuq_ref, wuk_ref, wuv_ref,
             c_ref, s1_ref, s2_ref, dz_ref, dgql_ref, dgkvl_ref, dgqh_ref, dgkh_ref, dwuq_ref, dwuk_ref, dwuv_ref,
             dqp_ref, dkn_ref, dvv_ref):
        gql, gkvl = gql_ref[...], gkvl_ref[...]
        gq, gk = gqh_ref[...], gkh_ref[...]
        wuq, wuk, wuv = wuq_ref[...], wuk_ref[...], wuv_ref[...]
        m = _mla_heads(z_ref[...], gql, gkvl, wuq, wuk, wuv)
        c, s1, s2 = c_ref[...], s1_ref[...], s2_ref[...]
        dgq = jnp.zeros((1, D_QKP), F32)
        dgk = jnp.zeros((1, D_QKP), F32)
        dkr = jnp.zeros((tm, D_QKP - D_NOPE), F32)
        for hd in range(HEADS):
            qn, rqh, knn, krn, rkh = m["heads"][hd]
            dqg = jnp.concatenate([dq_ref[hd, :, 0:D_NOPE].astype(F32),
                                   _rope_t(dq_ref[hd, :, D_NOPE:D_QKP].astype(F32), c, s1, s2)], axis=1)
            dgq = dgq + jnp.sum(dqg * qn, axis=0, keepdims=True)
            dqn = dqg * gq
            dqr = rqh * (dqn - qn * (jnp.sum(dqn * qn, axis=-1, keepdims=True) * (1.0 / D_QK)))
            dqp_ref[:, hd * D_QKP:(hd + 1) * D_QKP] = dqr.astype(BF16)
            kn_full = jnp.concatenate([knn, krn], axis=1)
            dkg = jnp.concatenate([dk_ref[hd, :, 0:D_NOPE].astype(F32),
                                   _rope_t(dk_ref[hd, :, D_NOPE:D_QKP].astype(F32), c, s1, s2)], axis=1)
            dgk = dgk + jnp.sum(dkg * kn_full, axis=0, keepdims=True)
            dkn = dkg * gk
            dkraw = rkh * (dkn - kn_full * (jnp.sum(dkn * kn_full, axis=-1, keepdims=True) * (1.0 / D_QK)))
            dkn_ref[:, hd * D_NOPE:(hd + 1) * D_NOPE] = dkraw[:, 0:D_NOPE].astype(BF16)
            dkr = dkr + dkraw[:, D_NOPE:D_QKP]
            dvv_ref[:, hd * D_V:(hd + 1) * D_V] = dv_ref[hd]
        dcqn = _dot(dqp_ref[...], wuq)
        dckvn = _dot_nt(dkn_ref[...], wuk) + _dot_nt(dvv_ref[...], wuv)
        dz_ref[:, 0:Q_RANK] = _rms_bwd(dcqn * gql, m["nq"], m["rq"], Q_RANK).astype(BF16)
        dz_ref[:, Q_RANK:Z_KR] = _rms_bwd(dckvn * gkvl, m["nkv"], m["rkv"], KV_RANK).astype(BF16)
        dz_ref[:, Z_KR:Z_MLA] = dkr.astype(BF16)
        parts = [(dgql_ref, jnp.sum(dcqn * m["nq"], axis=0, keepdims=True)),
                 (dgkvl_ref, jnp.sum(dckvn * m["nkv"], axis=0, keepdims=True)), (dgqh_ref, dgq), (dgkh_ref, dgk),
                 (dwuq_ref, _dot_tn(dqp_ref[...], m["cqn"])), (dwuk_ref, _dot_tn(m["ckvn"], dkn_ref[...])),
                 (dwuv_ref, _dot_tn(m["ckvn"], dvv_ref[...]))]

        @pl.when(pl.program_id(0) == 0)
        def _():
            for ref, val in parts:
                ref[...] = val

        @pl.when(pl.program_id(0) != 0)
        def _():
            for ref, val in parts:
                ref[...] += val

    def const(shape):
        return pl.BlockSpec(shape, lambda i: tuple(0 for _ in shape))

    tab = pl.BlockSpec((tm, 128), lambda i: (i % tpe, 0))
    hq = pl.BlockSpec((HEADS, tm, D_QKP), lambda i: (0, i, 0))
    hv = pl.BlockSpec((HEADS, tm, D_V), lambda i: (0, i, 0))

    def rowspec(n):
        return pl.BlockSpec((tm, n), lambda i: (i, 0))

    return _hosted_call(
        body, name="mla_prep_bwd", grid=(rows // tm,),
        in_specs=[rowspec(Z_MLA), hq, hq, hv, const((1, Q_RANK)), const((1, KV_RANK)), const((1, D_QKP)),
                  const((1, D_QKP)), const((HEADS * D_QKP, Q_RANK)), const((KV_RANK, HEADS * D_NOPE)),
                  const((KV_RANK, HEADS * D_V)), tab, tab, tab],
        out_specs=[rowspec(Z_MLA), const((1, Q_RANK)), const((1, KV_RANK)), const((1, D_QKP)), const((1, D_QKP)),
                   const((HEADS * D_QKP, Q_RANK)), const((KV_RANK, HEADS * D_NOPE)), const((KV_RANK, HEADS * D_V))],
        out_shape=[jax.ShapeDtypeStruct((rows, Z_W), BF16),
                   jax.ShapeDtypeStruct((1, Q_RANK), F32), jax.ShapeDtypeStruct((1, KV_RANK), F32),
                   jax.ShapeDtypeStruct((1, D_QKP), F32), jax.ShapeDtypeStruct((1, D_QKP), F32),
                   jax.ShapeDtypeStruct((HEADS * D_QKP, Q_RANK), F32),
                   jax.ShapeDtypeStruct((KV_RANK, HEADS * D_NOPE), F32), jax.ShapeDtypeStruct((KV_RANK, HEADS * D_V), F32)],
        scratch_shapes=[pltpu.VMEM((tm, HEADS * D_QKP), BF16), pltpu.VMEM((tm, HEADS * D_NOPE), BF16),
                        pltpu.VMEM((tm, HEADS * D_V), BF16)],
        args=(z, dq, dk, dv, gql, gkvl, gqh, gkh, wuq, wuk, wuv, *tabs), comm=comm)


def _local_step(h0, target, w, nb, lp, sched=None):
    tm = _row_tile(nb * lp, MATMUL_ROWS)
    te = _row_tile(lp, ELEMENTWISE_ROWS)
    tabs = _rope_tables(lp)
    g = {}
    if sched is None:
        host = lambda stage: None
    else:
        sched.g = g
        host = sched.host

    def ffn_act(tag, u, split):
        if split:
            gate = _ffn_gate_call(tag + "_gate", u, w[tag + "_w_gate"], tm, host(tag + "_gate"))
            up, act = _ffn_upact_call(tag + "_upact", u, w[tag + "_w_up"], gate, tm, host(tag + "_upact"))
        else:
            gate, up, act = _ffn_up_call(tag + "_up", u, w[tag + "_w_gate"], w[tag + "_w_up"], tm, host(tag + "_up"))
        return u, gate, up, act

    def ffn_bwd(tag, h, saved, dh, dhb, split, mix=None):
        u, gate, up, act = saved
        dgate, dup = _ffn_dact_call(tag + "_dact", dhb, w[tag + "_w_down"], gate, up, tm, host(tag + "_dact"))
        g[tag + "_w_down"] = _wgrad_call(tag + "_dwd", act, dhb, 0.5, host(tag + "_dwd"))
        g[tag + "_w_gate"] = _wgrad_call(tag + "_dwg", dgate, u, 1.0, host(tag + "_dwg"))
        g[tag + "_w_up"] = _wgrad_call(tag + "_dwu", dup, u, 1.0, host(tag + "_dwu"))
        pieces = [(dgate, w[tag + "_w_gate"]), (dup, w[tag + "_w_up"])]
        if not split:
            res = _norm_in_bwd_call(tag + "_din", pieces, h, w[tag + "_norm"], dh, te, host(tag + "_din"), mix=mix)
            g[tag + "_norm"] = res[2]
            return (res[0], res[1], *res[3:])
        dh_a, dhb_a, dg_a = _norm_in_bwd_call(tag + "_din_a", pieces, h, w[tag + "_norm"], dh, te,
                                              host(tag + "_din_a"), part=(0, 2))
        dh_in, dhb_in, dg_b = _norm_in_bwd_call(tag + "_din_b", pieces, h, w[tag + "_norm"], dh, te,
                                                host(tag + "_din_b"), part=(1, 2), prev=(dh_a, dhb_a))
        g[tag + "_norm"] = dg_a + dg_b
        return dh_in, dhb_in

    s1 = ffn_act("ffn1", _rmsnorm_call("ffn1_norm", h0, w["ffn1_norm"], te, host("ffn1_norm")), True)
    h1, un = _ffn_down_call("ffn1_down", s1[3], w["ffn1_w_down"], h0, te, host("ffn1_down"), w["mix_norm"])
    z = _mm_call("mix_in", un, w["w_in"], tm, F32)
    mla_w = (w["q_latent_norm"], w["kv_latent_norm"], w["q_head_norm"], w["k_head_norm"], w["w_uq"], w["w_uk"],
             w["w_uv"])
    q, k, v = _mla_prep_call(z, *mla_w, tabs, lp, te)
    o, lse = _attn_fwd_call(q, k, v, nb, lp, host("attn_fwd"))
    lru_w = (w["conv_w"], w["conv_b"], w["gate_a_w"], w["gate_a_b"], w["gate_x_w"], w["gate_x_b"], w["lru_lambda"])
    yl, hs = _lru_fwd_call(z, *lru_w, nb, lp, host("lru_fwd"))
    y, h2, u2 = _mix_out_call(o, yl, w["attn_out_norm"], w["lru_out_norm"], w["w_out"], h1, w["ffn2_norm"], te)
    s2 = ffn_act("ffn2", u2, False)
    dh3, dh3b, g["final_norm"], loss = _ffn_down_loss_call("ffn2_down", s2[3], w["ffn2_w_down"], h2, w["final_norm"],
                                                           target, lp, te)
    g["loss"] = loss

    dh2, dh2b, dya, dyl, g["attn_out_norm"], g["lru_out_norm"] = ffn_bwd(
        "ffn2", h2, s2, dh3, dh3b, False, (w["w_out"], o, yl, w["attn_out_norm"], w["lru_out_norm"]))
    g["w_out"] = _wgrad_call("dw_out", y, dh2b)
    dq, dk, dv = _attn_bwd_call(q, k, v, o, lse, dya, nb, lp, host("attn_bwd"))
    (dz_mla, g["q_latent_norm"], g["kv_latent_norm"], g["q_head_norm"], g["k_head_norm"], g["w_uq"], g["w_uk"],
     g["w_uv"]) = _mla_prep_bwd_call(z, dq, dk, dv, *mla_w, tabs, lp, te, host("mla_prep_bwd"))
    (dz, g["conv_w"], g["conv_b"], g["gate_a_w"], g["gate_a_b"], g["gate_x_w"], g["gate_x_b"],
     g["lru_lambda"]) = _lru_bwd_call(z, dz_mla, hs, dyl, *lru_w, nb, lp, host("lru_bwd"))
    g["w_in"] = _wgrad_call("dw_in", dz, un)
    dh1, dh1b, g["mix_norm"] = _norm_in_bwd_call("mix_din", [(dz, w["w_in"])], h1, w["mix_norm"], dh2, te,
                                                 host("mix_din"))
    dh0 = ffn_bwd("ffn1", h0, s1, dh1, dh1b, True)[0]
    return loss, dh0, g


def _place():
    x, y, c = lax.axis_index("x"), lax.axis_index("y"), lax.axis_index("c")
    return x, y, c, [(1 - x, y), (x, 1 - y), (1 - x, 1 - y)]


def _any_specs(n):
    return [pl.BlockSpec(memory_space=pl.ANY)] * n


def _remote(src, dst, sems, k, dev):
    send_sems, recv_sems, base = sems
    return pltpu.make_async_remote_copy(src_ref=src, dst_ref=dst, send_sem=send_sems.at[base + k],
                                        recv_sem=recv_sems.at[base + k], device_id=dev, device_id_type=MESH)


EW_VMEM_BYTES = 24 * 1024 * 1024


def _fit_rows(rows, cols, blocks):
    return _row_tile(rows, max(16, int(EW_VMEM_BYTES // (8 * blocks)) // cols))


class _Geom:
    def __init__(self, n0, n1, blocks=1.0):
        self.n0, self.n1 = n0, n1
        self.axis = 0 if n0 % 32 == 0 else 1
        self.h0, self.h1 = (n0 // 2, n1) if self.axis == 0 else (n0, n1 // 2)
        self.tr = _fit_rows(self.h0, self.h1, blocks)
        self.nblk = self.h0 // self.tr

    def half_ref(self, ref, lead, idx):
        if self.axis == 0:
            return ref.at[(*lead, pl.ds(idx * self.h0, self.h0))]
        return ref.at[(*lead, slice(None), pl.ds(idx * self.h1, self.h1))]

    def half_block(self, lead, i, idx):
        return (*lead, idx * self.nblk + i, 0) if self.axis == 0 else (*lead, i, idx)


class _Comm:
    def __init__(self, ins, out_shapes, aliases, n_sems, start, finish, deliver):
        self.ins, self.out_shapes, self.aliases, self.n_sems = list(ins), list(out_shapes), dict(aliases), n_sems
        self.start, self.finish, self.deliver = start, finish, deliver

    def scratch(self):
        return [pltpu.SemaphoreType.DMA((self.n_sems,)), pltpu.SemaphoreType.DMA((self.n_sems,))]


def _comm_call(name, comm):
    n_in = len(comm.ins)

    def body(*refs):
        ins, outs, sems = refs[:n_in], refs[n_in:-2], (*refs[-2:], 0)
        comm.start(ins, outs, sems)
        comm.finish(ins, outs, sems)

    res = pl.pallas_call(
        body, name=name, out_shape=comm.out_shapes, in_specs=_any_specs(n_in),
        out_specs=_any_specs(len(comm.out_shapes)), input_output_aliases=comm.aliases,
        scratch_shapes=comm.scratch())(*comm.ins)
    return comm.deliver(list(res))


def _hosted_call(body, *, name, grid, in_specs, out_specs, out_shape, args, scratch_shapes=(), dims=None, comm=None,
                 prefetch=None, aliases=None):
    aliases = dict(aliases or {})
    in_specs, out_specs, out_shape = list(in_specs), list(out_specs), list(out_shape)
    n_pre = 0 if prefetch is None else 1

    def call(fn, in_specs, out_specs, out_shape, scratch, aliases, dims, args):
        if prefetch is None:
            return pl.pallas_call(
                fn, name=name, grid=grid, in_specs=in_specs, out_specs=out_specs, out_shape=out_shape,
                scratch_shapes=scratch, input_output_aliases=aliases, compiler_params=_params(dims))(*args)
        spec = pltpu.PrefetchScalarGridSpec(num_scalar_prefetch=1, grid=grid, in_specs=in_specs, out_specs=out_specs,
                                            scratch_shapes=scratch)
        return pl.pallas_call(
            fn, name=name, grid_spec=spec, out_shape=out_shape,
            input_output_aliases={i + 1: o for i, o in aliases.items()}, compiler_params=_params(dims))(prefetch, *args)

    if comm is None:
        return list(call(body, in_specs, out_specs, out_shape, list(scratch_shapes), aliases,
                         dims or ("arbitrary",) * len(grid), args))
    n_in, n_out, n_ci, n_co = len(in_specs), len(out_specs), len(comm.ins), len(comm.out_shapes)

    def wrapped(*refs):
        pre, refs = refs[:n_pre], refs[n_pre:]
        ins, cins = refs[:n_in], refs[n_in:n_in + n_ci]
        outs = refs[n_in + n_ci:n_in + n_ci + n_out]
        couts = refs[n_in + n_ci + n_out:n_in + n_ci + n_out + n_co]
        scratch, sems = refs[n_in + n_ci + n_out + n_co:-2], (*refs[-2:], 0)
        first = functools.reduce(jnp.logical_and, [pl.program_id(k) == 0 for k in range(len(grid))])
        last = functools.reduce(jnp.logical_and, [pl.program_id(k) == grid[k] - 1 for k in range(len(grid))])

        @pl.when(first)
        def _():
            comm.start(cins, couts, sems)

        body(*pre, *ins, *outs, *scratch)

        @pl.when(last)
        def _():
            comm.finish(cins, couts, sems)

    res = call(wrapped, in_specs + _any_specs(n_ci), out_specs + _any_specs(n_co), out_shape + comm.out_shapes,
               list(scratch_shapes) + comm.scratch(),
               {**aliases, **{n_in + i: n_out + o for i, o in comm.aliases.items()}},
               ("arbitrary",) * len(grid), (*args, *comm.ins))
    comm.deliver(list(res[n_out:]))
    return list(res[:n_out])


def _gather_comm(bufs, deliver):
    n = len(bufs)
    geoms = [_Geom(*b.shape[1:]) for b in bufs]

    def first(outs, sems):
        x, y, c, chips = _place()
        cps = []
        for a in range(n):
            mine = geoms[a].half_ref(outs[a], (2 * x + y,), c)
            cps += [_remote(mine, mine, sems, 6 * a + j, (cx, cy, c)) for j, (cx, cy) in enumerate(chips)]
        return cps

    def start(ins, outs, sems):
        for cp in first(outs, sems):
            cp.start()

    def finish(ins, outs, sems):
        x, y, c, chips = _place()
        sib = (x, y, 1 - c)
        passed = []
        for a in range(n):
            for j, (cx, cy) in enumerate(chips):
                land = geoms[a].half_ref(outs[a], (2 * cx + cy,), c)
                _remote(land, land, sems, 6 * a + j, sib).wait_recv()
                cp = _remote(land, land, sems, 6 * a + 3 + j, sib)
                cp.start()
                passed.append(cp)
        for a in range(n):
            for j, (cx, cy) in enumerate(chips):
                land = geoms[a].half_ref(outs[a], (2 * cx + cy,), 1 - c)
                _remote(land, land, sems, 6 * a + 3 + j, sib).wait_recv()
        for cp in first(outs, sems) + passed:
            cp.wait_send()

    return _Comm(bufs, [jax.ShapeDtypeStruct(b.shape, b.dtype) for b in bufs], {a: a for a in range(n)}, 6 * n,
                 start, finish, deliver)


def _reduce_pair_comm(grads, deliver):
    n = len(grads)
    geoms = [_Geom(*a.shape[1:]) for a in grads]

    def copies(ins, outs, sems):
        x, y, c, _ = _place()
        return [_remote(geoms[a].half_ref(ins[a], (slice(None),), 1 - c), outs[a], sems, a, (x, y, 1 - c))
                for a in range(n)]

    def start(ins, outs, sems):
        for cp in copies(ins, outs, sems):
            cp.start()

    def finish(ins, outs, sems):
        cps = copies(ins, outs, sems)
        for cp in cps:
            cp.wait_recv()
        for cp in cps:
            cp.wait_send()

    shapes = [jax.ShapeDtypeStruct((N_SHARD, g.h0, g.h1), a.dtype) for a, g in zip(grads, geoms)]
    return _Comm(grads, shapes, {}, n, start, finish, deliver)


def _reduce_chips_comm(parts, deliver):
    n = len(parts)

    def copies(ins, outs, sems):
        x, y, c, chips = _place()
        return [_remote(ins[a].at[2 * cx + cy], outs[a].at[j], sems, 3 * a + j, (cx, cy, c))
                for a in range(n) for j, (cx, cy) in enumerate(chips)]

    def start(ins, outs, sems):
        for cp in copies(ins, outs, sems):
            cp.start()

    def finish(ins, outs, sems):
        cps = copies(ins, outs, sems)
        for cp in cps:
            cp.wait_recv()
        for cp in cps:
            cp.wait_send()

    shapes = [jax.ShapeDtypeStruct((3,) + a.shape[1:], a.dtype) for a in parts]
    return _Comm(parts, shapes, {}, 3 * n, start, finish, deliver)


def _share_pair_comm(bufs, deliver):
    n = len(bufs)
    geoms = [_Geom(*b.shape) for b in bufs]

    def copies(outs, sems):
        x, y, c, _ = _place()
        cps = []
        for a in range(n):
            mine = geoms[a].half_ref(outs[a], (), c)
            cps.append(_remote(mine, mine, sems, a, (x, y, 1 - c)))
        return cps

    def start(ins, outs, sems):
        for cp in copies(outs, sems):
            cp.start()

    def finish(ins, outs, sems):
        x, y, c, _ = _place()
        for a in range(n):
            land = geoms[a].half_ref(outs[a], (), 1 - c)
            _remote(land, land, sems, a, (x, y, 1 - c)).wait_recv()
        for cp in copies(outs, sems):
            cp.wait_send()

    return _Comm(bufs, [jax.ShapeDtypeStruct(b.shape, b.dtype) for b in bufs], {a: a for a in range(n)}, n,
                 start, finish, deliver)


def _small_comm(pack, deliver):
    r, d = pack.shape

    def copies(ins, outs, sems):
        x, y, c, _ = _place()
        cps = []
        for k in range(1, 8):
            peer = (x ^ ((k >> 2) & 1), y ^ ((k >> 1) & 1), c ^ (k & 1))
            cps.append(_remote(ins[0], outs[0].at[4 * x + 2 * y + c], sems, k - 1, peer))
        return cps

    def start(ins, outs, sems):
        for cp in copies(ins, outs, sems):
            cp.start()

    def finish(ins, outs, sems):
        cps = copies(ins, outs, sems)
        for cp in cps:
            cp.wait_recv()
        for cp in cps:
            cp.wait_send()

    return _Comm([pack], [jax.ShapeDtypeStruct((8, r, d), pack.dtype)], {}, 7, start, finish, deliver)


def _join_comms(comms):
    comms = [c for c in comms if c is not None]
    if len(comms) <= 1:
        return comms[0] if comms else None
    ins, out_shapes, aliases, spans, n_sems = [], [], {}, [], 0
    for c in comms:
        aliases.update({len(ins) + i: len(out_shapes) + o for i, o in c.aliases.items()})
        spans.append((len(ins), len(ins) + len(c.ins), len(out_shapes), len(out_shapes) + len(c.out_shapes), n_sems))
        ins += c.ins
        out_shapes += c.out_shapes
        n_sems += c.n_sems

    def run(which):
        def go(all_ins, all_outs, sems):
            for c, (i0, i1, o0, o1, base) in zip(comms, spans):
                getattr(c, which)(all_ins[i0:i1], all_outs[o0:o1], (sems[0], sems[1], sems[2] + base))
        return go

    def deliver(outs):
        for c, (_, _, o0, o1, _) in zip(comms, spans):
            c.deliver(outs[o0:o1])
        return outs

    return _Comm(ins, out_shapes, aliases, n_sems, run("start"), run("finish"), deliver)


def _ew_call(name, fn, ins, out_dtypes):
    shape = ins[0].shape
    cols = shape[-1]
    rows = 1
    for s_ in shape[:-1]:
        rows *= s_
    ins2 = [a.reshape(rows, cols) for a in ins]
    tr = rows
    for t in range(16, min(rows, max(16, (1 << 19) // cols)) + 1, 16):
        if rows % t == 0:
            tr = t
    no = len(out_dtypes)

    def body(*refs):
        outs = fn(*[r[...] for r in refs[:len(ins2)]])
        for ref, val in zip(refs[len(ins2):], outs):
            ref[...] = val.astype(ref.dtype)

    spec = pl.BlockSpec((tr, cols), lambda i: (i, 0))
    res = pl.pallas_call(
        body, name=name, grid=(rows // tr,), in_specs=[spec] * len(ins2), out_specs=[spec] * no,
        out_shape=[jax.ShapeDtypeStruct((rows, cols), dt) for dt in out_dtypes],
        compiler_params=_params(("parallel",)))(*ins2)
    return [r.reshape(shape) for r in res]


def _adamw_math(w, g, m, v):
    m = ADAM_B1 * m + (1.0 - ADAM_B1) * g
    v = ADAM_B2 * v + (1.0 - ADAM_B2) * (g * g)
    m_hat = m / (1.0 - ADAM_B1 ** ADAM_STEP)
    v_hat = v / (1.0 - ADAM_B2 ** ADAM_STEP)
    delta = -ADAM_LR * (m_hat / (jnp.sqrt(v_hat) + ADAM_EPS) + ADAM_WD * w)
    return delta, m, v


def _adamw_call(name, w, g, m, v):
    return _ew_call(name, _adamw_math, [w, g, m, v], [F32, F32, F32])


def _tiled_call(name, fn, place, grid, in_items, out_items, comm=None):
    ni = len(in_items)

    def body(place_ref, *refs):
        vals = fn(*[r[...] for r in refs[:ni]])
        for ref, val in zip(refs[ni:], vals):
            ref[...] = val.astype(ref.dtype)

    return _hosted_call(
        body, name=name, grid=grid, in_specs=[pl.BlockSpec(blk, imap) for _, blk, imap in in_items],
        out_specs=[pl.BlockSpec(blk, imap) for _, _, blk, imap in out_items],
        out_shape=[jax.ShapeDtypeStruct(shp, dt) for shp, dt, _, _ in out_items],
        args=[a for a, _, _ in in_items], prefetch=place, comm=comm)


def _cast_call(name, place, shards, comm=None):
    n0, n1 = shards[0].shape
    tr = _fit_rows(n0, n1, 1.5 * len(shards))
    ins = [(a, (tr, n1), lambda i, p: (i, 0)) for a in shards]
    outs = [((N_SHARD, n0, n1), BF16, (1, tr, n1), lambda i, p: (p[0], i, 0)) for _ in shards]
    return _tiled_call(name, lambda *v: [x[None] for x in v], place, (n0 // tr,), ins, outs, comm)


def _pair_sum_call(name, place, fulls, gots):
    k = len(fulls)
    g = _Geom(*fulls[0].shape[1:], blocks=2.5 * k)
    blk = (1, g.tr, g.h1)
    ins = [(a, blk, lambda s, i, p: g.half_block((s,), i, p[1])) for a in fulls]
    ins += [(a, blk, lambda s, i, p: (s, i, 0)) for a in gots]
    outs = [((N_SHARD, g.h0, g.h1), BF16, blk, lambda s, i, p: (s, i, 0)) for _ in fulls]
    return _tiled_call(name, lambda *v: [v[j] + v[k + j] for j in range(k)], place, (N_SHARD, g.nblk), ins, outs)


def _chip_sum_call(name, place, fulls, gots, recvs, comm=None):
    k = len(fulls)
    g = _Geom(*fulls[0].shape[1:], blocks=4.5 * k)
    blk = (1, g.tr, g.h1)
    ins = [(a, blk, lambda i, p: g.half_block((p[0],), i, p[1])) for a in fulls]
    ins += [(a, blk, lambda i, p: (p[0], i, 0)) for a in gots]
    ins += [(a, (3, g.tr, g.h1), lambda i, p: (0, i, 0)) for a in recvs]
    outs = [((g.n0, g.n1), F32, (g.tr, g.h1), lambda i, p: g.half_block((), i, p[1])) for _ in fulls]

    def fn(*v):
        res = []
        for j in range(k):
            r = v[2 * k + j].astype(F32)
            res.append(v[j][0] + v[k + j][0] + r[0] + r[1] + r[2])
        return res

    return _tiled_call(name, fn, place, (g.nblk,), ins, outs, comm)


def _adamw_group_call(name, ws, gs, ms, vs, comm=None):
    k = len(ws)
    n0, n1 = ws[0].shape
    tr = _fit_rows(n0, n1, 8 * k)
    spec = pl.BlockSpec((tr, n1), lambda i: (i, 0))

    def body(*refs):
        for j in range(k):
            g = refs[k + j][...]
            delta, m, vv = _adamw_math(refs[j][...], g, refs[2 * k + j][...], refs[3 * k + j][...])
            for ref, val in zip(refs[4 * k + 4 * j:4 * k + 4 * j + 4], (g, delta, m, vv)):
                ref[...] = val

    flat = _hosted_call(
        body, name=name, grid=(n0 // tr,), in_specs=[spec] * (4 * k), out_specs=[spec] * (4 * k),
        out_shape=[jax.ShapeDtypeStruct((n0, n1), F32)] * (4 * k), dims=("parallel",),
        args=(*ws, *gs, *ms, *vs), comm=comm)
    return [flat[4 * j:4 * j + 4] for j in range(k)]


def _small_update_call(me, early, own_early, late, own_late, states):
    nd, r, d = early.shape
    widths = [a.shape[1] for a in states[0][0]]
    nw = len(widths)
    n_state = nw + 2

    def body(me_ref, e_ref, oe_ref, l_ref, ol_ref, *refs):
        state_refs, refs = refs[:3 * n_state], refs[3 * n_state:]
        (gs_ref, d_ref, nm_ref, nv_ref), rows, packs = refs[:4], refs[4:4 + 4 * nw], refs[4 + 4 * nw:]
        for kind, pack in enumerate(packs):
            srefs = state_refs[kind * n_state:(kind + 1) * n_state]
            pack[...] = jnp.zeros_like(pack)
            for k, width in enumerate(widths):
                pack[k:k + 1, 0:width] = srefs[k][...]
            pack[ROW_GATE_A:ROW_GATE_A + 32, :] = srefs[nw][...]
            pack[ROW_GATE_X:ROW_GATE_X + 32, :] = srefs[nw + 1][...]
        w_ref, m_ref, v_ref = packs
        mine = me_ref[0]

        def total(g_ref, own_ref):
            acc = None
            for k in range(nd):
                part = jnp.where(mine == k, own_ref[...], g_ref[k])
                acc = part if acc is None else acc + part
            return acc

        gs = total(e_ref, oe_ref)
        ls = total(l_ref, ol_ref)
        gs_ref[...] = gs
        first = gs[0:8] + ls[0:8]
        gs_ref[0:8, :] = first
        gs_ref[ROW_META:ROW_META + N_META, :] = gs[ROW_META:ROW_META + N_META] + ls[8:8 + N_META]
        grads = jnp.concatenate([first, gs[8:SMALL_ADAM_ROWS]], axis=0)
        delta, m, v = _adamw_math(w_ref[...], grads, m_ref[...], v_ref[...])
        d_ref[...] = delta
        nm_ref[...] = m
        nv_ref[...] = v
        for kind, pack_ref in enumerate((gs_ref, d_ref, nm_ref, nv_ref)):
            for k, width in enumerate(widths):
                rows[kind * nw + k][...] = pack_ref[k:k + 1, 0:width]

    vm = pl.BlockSpec(memory_space=pltpu.VMEM)
    ashape = jax.ShapeDtypeStruct((SMALL_ADAM_ROWS, d), F32)
    row_shapes = [jax.ShapeDtypeStruct((1, width), F32) for _ in range(4) for width in widths]
    flat_states = [a for rows_, ga, gx in states for a in (*rows_, ga, gx)]
    res = pl.pallas_call(
        body, name="small_update", in_specs=[pl.BlockSpec(memory_space=pltpu.SMEM)] + [vm] * (4 + 3 * n_state),
        out_specs=[vm] * (4 + 4 * nw),
        out_shape=[jax.ShapeDtypeStruct((r, d), F32), ashape, ashape, ashape] + row_shapes,
        scratch_shapes=[pltpu.VMEM((SMALL_ADAM_ROWS, d), F32)] * 3,
        compiler_params=pltpu.CompilerParams(vmem_limit_bytes=VMEM_LIMIT_BYTES))(
            me, early, own_early, late, own_late, *flat_states)
    return res[:4], [res[4 + kind * nw:4 + (kind + 1) * nw] for kind in range(4)]


SMALL_NAMES = ["ffn1_norm", "mix_norm", "ffn2_norm", "final_norm", "q_latent_norm", "kv_latent_norm",
               "q_head_norm", "k_head_norm", "conv_b", "gate_a_b", "gate_x_b", "lru_lambda", "attn_out_norm",
               "lru_out_norm"]
ROW_CONV_W = 14
ROW_GATE_A = 16
ROW_GATE_X = 48
ROW_META = 80
ROW_LOSS = 96


def _row(a):
    flat = a.reshape(1, -1)
    return jnp.pad(flat, ((0, 0), (0, D_MODEL - flat.shape[1])))


def _pack_small(t, rows):
    parts = [_row(t[nm]) for nm in SMALL_NAMES]
    parts.append(t["conv_w"].reshape(2, D_MODEL))
    parts.append(t["gate_a_w"].reshape(32, D_MODEL))
    parts.append(t["gate_x_w"].reshape(32, D_MODEL))
    p = jnp.concatenate(parts, axis=0)
    return jnp.pad(p, ((0, rows - p.shape[0]), (0, 0)))


def _early_pack(g):
    gs = {nm: g.get(nm, jnp.zeros((1, D_MODEL), F32)) for nm in SMALL_NAMES}
    gs["q_head_norm"] = g["q_head_norm"][:, 0:D_QK]
    gs["k_head_norm"] = g["k_head_norm"][:, 0:D_QK]
    for nm in ("conv_b", "gate_a_b", "gate_x_b", "lru_lambda"):
        gs[nm] = g[nm].reshape(1, LRU_W)
    gs["conv_w"] = g["conv_w"].transpose(1, 0, 2).reshape(CONV_K, LRU_W)
    gs["gate_a_w"] = _gate_blocks(g["gate_a_w"])
    gs["gate_x_w"] = _gate_blocks(g["gate_x_w"])
    return jnp.concatenate([_pack_small(gs, ROW_META), jnp.zeros((N_META, D_MODEL), F32), _row(g["loss"][:, 0:1]),
                            jnp.zeros((SMALL_ROWS - ROW_LOSS - 1, D_MODEL), F32)], axis=0)


def _unpack_small(p, rows, like):
    out = dict(zip(SMALL_NAMES, rows))
    out["gate_a_w"] = p[ROW_GATE_A:ROW_GATE_A + 32].reshape(like["gate_a_w"].shape)
    out["gate_x_w"] = p[ROW_GATE_X:ROW_GATE_X + 32].reshape(like["gate_x_w"].shape)
    return out


def _gate_dense(wg):
    w4 = wg[0].reshape(N_LRU_TILES, 2, 64, 64)
    zero = jnp.zeros((N_LRU_TILES, 64, 64), wg.dtype)
    top = jnp.concatenate([w4[:, 0], zero], axis=2)
    bot = jnp.concatenate([zero, w4[:, 1]], axis=2)
    return jnp.concatenate([top, bot], axis=1).astype(BF16)


def _gate_blocks(dw):
    return jnp.stack([dw[:, 0:64, 0:64], dw[:, 64:128, 64:128]], axis=1).reshape(8, 64, 64)


BIG_NAMES = ["ffn1_w_gate", "ffn1_w_up", "ffn1_w_down", "w_in", "w_uq", "w_uk", "w_uv", "w_out", "ffn2_w_gate",
             "ffn2_w_up", "ffn2_w_down"]
BIG_GROUPS = [["ffn1_w_gate", "ffn1_w_up", "ffn1_w_down", "ffn2_w_gate", "ffn2_w_up", "ffn2_w_down"], ["w_in"],
              ["w_uq"], ["w_uk", "w_uv"], ["w_out"]]
TRANSPOSED = ("ffn1_w_gate", "ffn1_w_up", "ffn2_w_gate", "ffn2_w_up", "w_in", "w_uq")


def _to2d(nm, a):
    return a[0].T if nm in TRANSPOSED else a[0]


def _from2d(nm, a):
    return (a.T if nm in TRANSPOSED else a)[None]


WEIGHT_NAMES = ["meta_tokens", "ffn1_norm", "ffn1_w_gate", "ffn1_w_up", "ffn1_w_down", "mix_norm", "w_in",
                "q_latent_norm", "w_uq", "kv_latent_norm", "w_uk", "w_uv", "q_head_norm", "k_head_norm", "conv_w",
                "conv_b", "gate_a_w", "gate_a_b", "gate_x_w", "gate_x_b", "lru_lambda", "attn_out_norm",
                "lru_out_norm", "w_out", "ffn2_norm", "ffn2_w_gate", "ffn2_w_up", "ffn2_w_down", "final_norm"]


def _weight_from(nm, slots):
    if nm == "w_in":
        win = slots.reshape(IN_WIDTH, D_MODEL)
        lru = win[Z_KR + D_ROPE:].reshape(2, N_LRU_TILES, LRU_TILE, D_MODEL).transpose(1, 0, 2, 3)
        return jnp.concatenate([win[0:Z_KR + D_ROPE], jnp.zeros((128 - D_ROPE, D_MODEL), BF16),
                                lru.reshape(2 * LRU_W, D_MODEL)], axis=0)
    if nm == "w_uq":
        return jnp.pad(slots, ((0, 0), (0, D_QKP - D_QK), (0, 0))).reshape(HEADS * D_QKP, Q_RANK)
    if nm in ("w_uk", "w_uv"):
        return slots.transpose(1, 0, 2).reshape(KV_RANK, HEADS * D_NOPE)
    if nm == "w_out":
        return slots.reshape(D_MODEL, D_MODEL)
    return slots


def _small_weights(p, small):
    w = {nm: p[nm] for nm in SMALL_NAMES}
    w["q_head_norm"] = jnp.pad(p["q_head_norm"], ((0, 0), (0, D_QKP - D_QK)))
    w["k_head_norm"] = jnp.pad(p["k_head_norm"], ((0, 0), (0, D_QKP - D_QK)))
    w["conv_w"] = small[:, N_META:N_META + 2, :].reshape(N_SHARD, CONV_K, LRU_TILE)
    w["gate_a_w"] = _gate_dense(p["gate_a_w"])
    w["gate_x_w"] = _gate_dense(p["gate_x_w"])
    meta = small[:, 0:N_META, :].transpose(1, 0, 2).reshape(N_META, D_MODEL)
    return w, meta


def _full_weights(p, gathered, small):
    w, meta = _small_weights(p, small)
    w.update({nm: _weight_from(nm, gathered[nm]) for nm in BIG_NAMES})
    return w, meta


def _shard_grad(nm, g):
    if nm == "w_in":
        lru = g[Z_MLA:].reshape(N_LRU_TILES, 2, LRU_TILE, D_MODEL).transpose(1, 0, 2, 3).reshape(2 * LRU_W, D_MODEL)
        return jnp.concatenate([g[0:Z_KR + D_ROPE], lru], axis=0).reshape(N_SHARD, IN_WIDTH // N_SHARD, D_MODEL)
    if nm == "w_uq":
        return g.reshape(HEADS, D_QKP, Q_RANK)[:, 0:D_QK, :]
    if nm in ("w_uk", "w_uv"):
        return g.reshape(KV_RANK, HEADS, D_NOPE).transpose(1, 0, 2)
    if nm == "w_out":
        return g.reshape(N_SHARD, D_MODEL // N_SHARD, D_MODEL)
    return g


def _shard_grads(g):
    return {nm: _shard_grad(nm, g[nm]) for nm in BIG_NAMES}


GATHER_AT = {"ffn1_norm": ["ffn1_w_gate"], "ffn1_gate": ["ffn1_w_up"], "ffn1_upact": ["ffn1_w_down"],
             "ffn1_down": ["w_in", "w_uq", "w_uk", "w_uv", "w_out"], "attn_fwd": ["ffn2_w_down", "ffn2_w_gate"],
             "lru_fwd": ["ffn2_w_up"]}
PAIR_AT = [("ffn2_din", ["ffn2_w_gate", "ffn2_w_up", "ffn2_w_down"]),
           ("mix_din", ["w_out", "w_uq", "w_uk", "w_uv", "w_in"]),
           ("ffn1_dwg", ["ffn1_w_down"]), ("ffn1_dwu", ["ffn1_w_gate"]), ("ffn1_din_a", ["ffn1_w_up"])]
CHIPS_AT = [("attn_bwd", ["ffn2_w_down", "ffn2_w_gate"]), ("mla_prep_bwd", ["ffn2_w_up"]),
            ("ffn1_dact", ["w_out", "w_uq", "w_uk", "w_uv", "w_in"]),
            ("ffn1_dwu", ["ffn1_w_down"]), ("ffn1_din_a", ["ffn1_w_gate"]), ("ffn1_din_b", ["ffn1_w_up"])]
SHARE_EARLY_GROUPS, SHARE_EARLY_AT = 3, "ffn1_dwd"
SMALL_EARLY_AT = "mix_din"


def _same_shape_groups(names):
    return [[nm for nm in grp if nm in names] for grp in BIG_GROUPS if any(nm in names for nm in grp)]


class _Sched:
    def __init__(self, place, w, slots):
        self.place, self.w, self.slots = place, w, slots
        self.g = None
        self.sharded, self.from_pair, self.chip_bf16, self.from_chips = {}, {}, {}, {}
        self.early = self.early_all = None
        self.shared = {}

    def host(self, stage):
        comms = []
        if stage in GATHER_AT:
            comms.append(self.gather(GATHER_AT[stage]))
        comms += [self.chips(names) for at, names in CHIPS_AT if at == stage]
        comms += [self.pair(names) for at, names in PAIR_AT if at == stage]
        if stage == SMALL_EARLY_AT:
            comms.append(self.small_early())
        if stage == SHARE_EARLY_AT:
            comms.append(self.share_early())
        return _join_comms(comms)

    def small_early(self):
        self.early = _early_pack(self.g)

        def deliver(outs):
            self.early_all = outs[0]
            return outs

        return _small_comm(self.early, deliver)

    def gather(self, names):
        def deliver(outs):
            self.w.update({nm: _weight_from(nm, o) for nm, o in zip(names, outs)})
            return outs

        return _gather_comm([self.slots[nm] for nm in names], deliver)

    def pair(self, names):
        self.sharded.update({nm: _shard_grad(nm, self.g[nm]) for nm in names})

        def deliver(outs):
            self.from_pair.update(zip(names, outs))
            for grp in _same_shape_groups(names):
                sums = _pair_sum_call("pair_sum_" + grp[0], self.place, [self.sharded[nm] for nm in grp],
                                      [self.from_pair[nm] for nm in grp])
                self.chip_bf16.update(zip(grp, sums))
            return outs

        return _reduce_pair_comm([self.sharded[nm] for nm in names], deliver)

    def chips(self, names):
        def deliver(outs):
            self.from_chips.update(zip(names, outs))
            return outs

        return _reduce_chips_comm([self.chip_bf16[nm] for nm in names], deliver)

    def chip_sums(self, names, comm=None):
        out = {}
        for grp in _same_shape_groups(names):
            sums = _chip_sum_call("chip_sum_" + grp[0], self.place, [self.sharded[nm] for nm in grp],
                                  [self.from_pair[nm] for nm in grp], [self.from_chips[nm] for nm in grp], comm)
            comm = None
            out.update(zip(grp, sums))
        return out

    def share_early(self):
        names = [nm for at, grp in CHIPS_AT[:SHARE_EARLY_GROUPS] for nm in grp]
        mine = self.chip_sums(names)
        return _share_pair_comm([mine[nm] for nm in names], lambda o: self.shared.update(zip(names, o)))


def kernel(x, meta_tokens, ffn1_norm, ffn1_w_gate, ffn1_w_up, ffn1_w_down, mix_norm, w_in, q_latent_norm, w_uq, kv_latent_norm, w_uk, w_uv, q_head_norm, k_head_norm, conv_w, conv_b, gate_a_w, gate_a_b, gate_x_w, gate_x_b, lru_lambda, attn_out_norm, lru_out_norm, w_out, ffn2_norm, ffn2_w_gate, ffn2_w_up, ffn2_w_down, final_norm, loss_target, m_meta_tokens, m_ffn1_norm, m_ffn1_w_gate, m_ffn1_w_up, m_ffn1_w_down, m_mix_norm, m_w_in, m_q_latent_norm, m_w_uq, m_kv_latent_norm, m_w_uk, m_w_uv, m_q_head_norm, m_k_head_norm, m_conv_w, m_conv_b, m_gate_a_w, m_gate_a_b, m_gate_x_w, m_gate_x_b, m_lru_lambda, m_attn_out_norm, m_lru_out_norm, m_w_out, m_ffn2_norm, m_ffn2_w_gate, m_ffn2_w_up, m_ffn2_w_down, m_final_norm, v_meta_tokens, v_ffn1_norm, v_ffn1_w_gate, v_ffn1_w_up, v_ffn1_w_down, v_mix_norm, v_w_in, v_q_latent_norm, v_w_uq, v_kv_latent_norm, v_w_uk, v_w_uv, v_q_head_norm, v_k_head_norm, v_conv_w, v_conv_b, v_gate_a_w, v_gate_a_b, v_gate_x_w, v_gate_x_b, v_lru_lambda, v_attn_out_norm, v_lru_out_norm, v_w_out, v_ffn2_norm, v_ffn2_w_gate, v_ffn2_w_up, v_ffn2_w_down, v_final_norm):
    args = locals()
    p = {nm: args[nm] for nm in WEIGHT_NAMES}
    mom = {nm: args["m_" + nm] for nm in WEIGHT_NAMES}
    var = {nm: args["v_" + nm] for nm in WEIGHT_NAMES}
    nb, seq, d = x.shape
    lp = CHUNK + seq
    xi, yi, ci = lax.axis_index("x"), lax.axis_index("y"), lax.axis_index("c")
    chip = 2 * xi + yi

    place = jnp.stack([chip, ci]).astype(jnp.int32)
    p2 = {nm: _to2d(nm, p[nm]) for nm in BIG_NAMES}
    m2 = {nm: _to2d(nm, mom[nm]) for nm in BIG_NAMES}
    v2 = {nm: _to2d(nm, var[nm]) for nm in BIG_NAMES}

    slots = {}
    small_shard = jnp.concatenate(
        [meta_tokens, conv_w[0].reshape(2, 2 * LRU_TILE), jnp.zeros((14, 2 * LRU_TILE), F32)], axis=0)
    small_slots = lax.dynamic_update_slice(jnp.zeros((N_SHARD,) + small_shard.shape, F32), small_shard[None],
                                           (chip, 0, 0))
    first = []
    comm = _gather_comm([small_slots], first.extend)
    for grp in BIG_GROUPS:
        for nm, buf in zip(grp, _cast_call("cast_" + grp[0], place, [p2[nm] for nm in grp], comm)):
            slots[nm] = buf
        comm = None
    w, meta = _small_weights(p, first[0])
    sched = _Sched(place, w, slots)

    h0 = jnp.concatenate(
        [jnp.zeros((nb, PAD_ROWS, d), F32), jnp.broadcast_to(meta[None], (nb, N_META, d)), x], axis=1)
    target = jnp.pad(loss_target, ((0, 0), (CHUNK, 0), (0, 0)))
    loss_part, dh0, g = _local_step(h0.reshape(nb * lp, d), target.reshape(nb * lp, d), w, nb, lp, sched)
    dh0 = dh0.reshape(nb, lp, d)
    grad_x = dh0[:, CHUNK:, :]

    late = jnp.concatenate([g["ffn1_norm"], g["mix_norm"], jnp.zeros((6, D_MODEL), F32),
                            jnp.sum(dh0[:, PAD_ROWS:CHUNK, :], axis=0)], axis=0)
    shared = sched.shared
    rest = [nm for at, names in CHIPS_AT[SHARE_EARLY_GROUPS:] if at is not None for nm in names]
    last = [nm for at, names in CHIPS_AT if at is None for nm in names]
    late_box = {}
    mine = sched.chip_sums(rest, _small_comm(late, lambda o: late_box.update(all=o[0])))
    share = _share_pair_comm([mine[nm] for nm in rest], lambda o: shared.update(zip(rest, o)))
    _comm_call("share_pair", _join_comms([share, sched.chips(last) if last else None]))
    if last:
        mine = sched.chip_sums(last)
        _comm_call("share_last", _share_pair_comm([mine[nm] for nm in last], lambda o: shared.update(zip(last, o))))
    late_all = late_box["all"]
    small_like = {nm: p[nm] for nm in SMALL_NAMES + ["gate_a_w", "gate_x_w"]}

    def state(t):
        return ([t[nm] for nm in SMALL_NAMES], t["gate_a_w"].reshape(32, D_MODEL), t["gate_x_w"].reshape(32, D_MODEL))

    me = (4 * xi + 2 * yi + ci).astype(jnp.int32).reshape(1)
    (gsum, dsm, msm, vsm), rows = _small_update_call(me, sched.early_all, sched.early, late_all, late,
                                                     [state(p), state(mom), state(var)])
    grads = _unpack_small(gsum, rows[0], small_like)
    delta = _unpack_small(dsm, rows[1], small_like)
    new_m = _unpack_small(msm, rows[2], small_like)
    new_v = _unpack_small(vsm, rows[3], small_like)
    loss = gsum[ROW_LOSS, 0]
    gmeta = gsum[ROW_META:ROW_META + N_META].reshape(N_META, N_SHARD, D_MODEL // N_SHARD)
    grads["meta_tokens"] = lax.dynamic_index_in_dim(gmeta, chip, axis=1, keepdims=False)
    gconv = gsum[ROW_CONV_W:ROW_CONV_W + 2].reshape(CONV_K, N_SHARD, LRU_TILE)
    grads["conv_w"] = lax.dynamic_index_in_dim(gconv, chip, axis=1, keepdims=False)[None]
    for nm in ("meta_tokens", "conv_w"):
        delta[nm], new_m[nm], new_v[nm] = _adamw_call("adamw_" + nm, p[nm], grads[nm], mom[nm], var[nm])

    for names in BIG_GROUPS:
        res = _adamw_group_call("adamw_" + names[0], [p2[nm] for nm in names], [shared[nm] for nm in names],
                                [m2[nm] for nm in names], [v2[nm] for nm in names])
        for nm, (gg, dd, mm, vv) in zip(names, res):
            grads[nm], delta[nm], new_m[nm], new_v[nm] = (_from2d(nm, t) for t in (gg, dd, mm, vv))

    return (loss, grad_x, *[grads[nm] for nm in WEIGHT_NAMES], *[delta[nm] for nm in WEIGHT_NAMES],
            *[new_m[nm] for nm in WEIGHT_NAMES], *[new_v[nm] for nm in WEIGHT_NAMES])
```

```python
import functools
import math

import jax
import jax.numpy as jnp
import numpy as np
from jax import lax
from jax.experimental import pallas as pl
from jax.experimental.pallas import tpu as pltpu

F32 = jnp.float32
BF16 = jnp.bfloat16
MESH = pl.DeviceIdType.MESH

D_MODEL = 1024
N_META = 16
CHUNK = 64
PAD_ROWS = CHUNK - N_META
HEADS = 4
D_NOPE = 128
D_ROPE = 64
D_QK = D_NOPE + D_ROPE
D_QKP = 256
D_V = 128
KV_RANK = 256
Q_RANK = 384
MLA_W = HEADS * D_V
LRU_W = 512
LRU_TILE = 128
N_LRU_TILES = LRU_W // LRU_TILE
CONV_K = 4
C_RGLRU = 8.0
ROPE_THETA = 10000.0
D_FF = 2816
N_SHARD = 4
EPS = 1e-6
NEG_INF = -1e30
Z_KR = Q_RANK + KV_RANK
Z_MLA = Z_KR + 128
Z_U = Z_MLA
Z_G = Z_U + LRU_W
Z_W = Z_G + LRU_W
IN_WIDTH = Q_RANK + KV_RANK + D_ROPE + 2 * LRU_W

ADAM_LR = 0.001
ADAM_B1 = 0.9
ADAM_B2 = 0.999
ADAM_EPS = 1e-08
ADAM_WD = 0.01
ADAM_STEP = 10

VMEM_LIMIT_BYTES = 56 * 1024 * 1024
MATMUL_ROWS = 1408
ELEMENTWISE_ROWS = 512
SMALL_ROWS = 104
SMALL_ADAM_ROWS = 80


def _params(sem):
    return pltpu.CompilerParams(dimension_semantics=sem, vmem_limit_bytes=VMEM_LIMIT_BYTES)


def _resident(shape):
    return pl.BlockSpec(tuple(shape), lambda i: (0,) * len(shape), pipeline_mode=pl.Buffered(1))


def _row_tile(rows, target):
    best = 16
    for t in range(16, min(rows, target) + 1, 16):
        if rows % t == 0:
            best = t
    return best


def _col_tile(cols, target):
    best = cols
    for t in range(128, min(cols, target) + 1, 128):
        if cols % t == 0:
            best = t
    return best


def _dot(a, b):
    return jnp.dot(a, b, preferred_element_type=F32)


def _dot_nt(a, b):
    return lax.dot_general(a, b, (((1,), (1,)), ((), ())), preferred_element_type=F32)


def _dot_tn(a, b):
    return lax.dot_general(a, b, (((0,), (0,)), ((), ())), preferred_element_type=F32)


def _rms(x, n):
    return lax.rsqrt(jnp.sum(x * x, axis=-1, keepdims=True) * (1.0 / n) + EPS)


def _rms_bwd(dn, nrm, r, n):
    return r * (dn - nrm * (jnp.sum(dn * nrm, axis=-1, keepdims=True) * (1.0 / n)))


def _gelu(x):
    k = math.sqrt(2.0 / math.pi)
    t = jnp.tanh(k * (x + 0.044715 * x * x * x))
    return 0.5 * x * (1.0 + t), t


def _gelu_grad(x, t):
    k = math.sqrt(2.0 / math.pi)
    return 0.5 * (1.0 + t) + 0.5 * x * (1.0 - t * t) * k * (1.0 + 3.0 * 0.044715 * x * x)


def _sigmoid(x):
    return 0.5 + 0.5 * jnp.tanh(0.5 * x)


def _softplus_neg(lam):
    e = jnp.exp(-jnp.abs(lam))
    log1p = jnp.where(e < 0.01, e * (1.0 - e * (0.5 - e * (1.0 / 3 - e * 0.25))), jnp.log(1.0 + e))
    return jnp.maximum(-lam, 0.0) + log1p


def _rope(t, c, s1, s2):
    return t * c + pltpu.roll(t, 96, 1) * s1 + pltpu.roll(t, 32, 1) * s2


def _rope_t(d, c, s1, s2):
    return d * c + pltpu.roll(d * s1, 32, 1) + pltpu.roll(d * s2, 96, 1)


def _rope_tables(lp):
    pos = (np.arange(lp, dtype=np.int32) - PAD_ROWS).astype(np.float32)
    inv_freq = (ROPE_THETA ** (-np.arange(0, D_ROPE // 2, dtype=np.float32) / (D_ROPE // 2))).astype(np.float32)
    ang = (pos[:, None] * inv_freq[None, :]).astype(np.float32).astype(np.float64)
    cos, sin = np.cos(ang).astype(np.float32), np.sin(ang).astype(np.float32)
    z = np.zeros_like(cos)
    return (jnp.asarray(np.concatenate([cos, cos, z, z], 1)), jnp.asarray(np.concatenate([-sin, z, z, z], 1)),
            jnp.asarray(np.concatenate([z, sin, z, z], 1)))


def _rmsnorm_call(name, h, g, tm, comm=None):
    rows, d = h.shape

    def body(h_ref, g_ref, o_ref):
        x = h_ref[...]
        o_ref[...] = (x * _rms(x, d) * g_ref[...]).astype(BF16)

    return _hosted_call(
        body, name=name, grid=(rows // tm,),
        in_specs=[pl.BlockSpec((tm, d), lambda i: (i, 0)), pl.BlockSpec((1, d), lambda i: (0, 0))],
        out_specs=[pl.BlockSpec((tm, d), lambda i: (i, 0))],
        out_shape=[jax.ShapeDtypeStruct((rows, d), BF16)],
        dims=("parallel",), args=(h, g), comm=comm)[0]


def _ffn_up_call(name, u, wg, wu, tm, comm=None):
    rows, d = u.shape
    ns, fs, _ = wg.shape

    def body(u_ref, wg_ref, wu_ref, g_ref, p_ref, a_ref):
        uu = u_ref[...]
        g = _dot_nt(uu, wg_ref[0])
        p = _dot_nt(uu, wu_ref[0])
        g_ref[0] = g.astype(BF16)
        p_ref[0] = p.astype(BF16)
        a_ref[0] = (g * jax.nn.sigmoid(g) * p).astype(BF16)

    wspec = pl.BlockSpec((1, fs, d), lambda s, i: (s, 0, 0))
    ospec = pl.BlockSpec((1, tm, fs), lambda s, i: (s, i, 0))
    oshape = jax.ShapeDtypeStruct((ns, rows, fs), BF16)
    return _hosted_call(
        body, name=name, grid=(ns, rows // tm),
        in_specs=[pl.BlockSpec((tm, d), lambda s, i: (i, 0)), wspec, wspec],
        out_specs=[ospec, ospec, ospec], out_shape=[oshape, oshape, oshape],
        dims=("parallel", "parallel"), args=(u, wg, wu), comm=comm)


def _ffn_gate_call(name, u, wg, tm, comm=None):
    rows, d = u.shape
    ns, fs, _ = wg.shape

    def body(u_ref, wg_ref, g_ref):
        g_ref[0] = _dot_nt(u_ref[...], wg_ref[0]).astype(BF16)

    return _hosted_call(
        body, name=name, grid=(ns, rows // tm),
        in_specs=[pl.BlockSpec((tm, d), lambda s, i: (i, 0)), pl.BlockSpec((1, fs, d), lambda s, i: (s, 0, 0))],
        out_specs=[pl.BlockSpec((1, tm, fs), lambda s, i: (s, i, 0))],
        out_shape=[jax.ShapeDtypeStruct((ns, rows, fs), BF16)],
        dims=("parallel", "parallel"), args=(u, wg), comm=comm)[0]


def _ffn_upact_call(name, u, wu, gate, tm, comm=None):
    rows, d = u.shape
    ns, fs, _ = wu.shape

    def body(u_ref, wu_ref, g_ref, p_ref, a_ref):
        p = _dot_nt(u_ref[...], wu_ref[0])
        g = g_ref[0].astype(F32)
        p_ref[0] = p.astype(BF16)
        a_ref[0] = (g * jax.nn.sigmoid(g) * p).astype(BF16)

    ospec = pl.BlockSpec((1, tm, fs), lambda s, i: (s, i, 0))
    oshape = jax.ShapeDtypeStruct((ns, rows, fs), BF16)
    return _hosted_call(
        body, name=name, grid=(ns, rows // tm),
        in_specs=[pl.BlockSpec((tm, d), lambda s, i: (i, 0)), pl.BlockSpec((1, fs, d), lambda s, i: (s, 0, 0)), ospec],
        out_specs=[ospec, ospec], out_shape=[oshape, oshape],
        dims=("parallel", "parallel"), args=(u, wu, gate), comm=comm)


def _loss_tail(x, g, t, row, d):
    r = _rms(x, d)
    n = x * r
    err = jnp.where(row >= CHUNK, n * g - t, 0.0)
    dout = err * (1.0 / d)
    dh = _rms_bwd(dout * g, n, r, d)
    dg = jnp.sum(dout * n, axis=0, keepdims=True)
    part = jnp.sum(jnp.sum(err * err, axis=1, keepdims=True), axis=0, keepdims=True) * (0.5 / d)
    return dh, dg, jnp.broadcast_to(part, (1, 128))


def _ffn_down_call(name, a, wd, h, tm, comm=None, next_gain=None):
    rows, d = h.shape
    ns, _, fs = a.shape
    more = next_gain is not None

    def body(a_ref, wd_ref, h_ref, *rest):
        acc = h_ref[...]
        for s in range(ns):
            acc = acc + 0.5 * _dot(a_ref[s], wd_ref[s])
        rest[-2 if more else -1][...] = acc
        if more:
            rest[-1][...] = (acc * _rms(acc, d) * rest[0][...]).astype(BF16)

    full = pl.BlockSpec((tm, d), lambda i: (i, 0))
    res = _hosted_call(
        body, name=name, grid=(rows // tm,),
        in_specs=[pl.BlockSpec((ns, tm, fs), lambda i: (0, i, 0)), _resident((ns, fs, d)), full]
        + ([pl.BlockSpec((1, d), lambda i: (0, 0))] if more else []),
        out_specs=[full] * (2 if more else 1),
        out_shape=[jax.ShapeDtypeStruct((rows, d), F32)] + ([jax.ShapeDtypeStruct((rows, d), BF16)] if more else []),
        dims=("parallel",), args=(a, wd, h) + ((next_gain,) if more else ()), comm=comm)
    return res if more else res[0]


def _ffn_down_loss_call(name, a, wd, h, g, target, lp, tm):
    rows, d = h.shape
    ns, _, fs = a.shape
    tpe = lp // tm

    def body(a_ref, wd_ref, h_ref, g_ref, t_ref, dh_ref, dhb_ref, dg_ref, loss_ref):
        i = pl.program_id(0)
        acc = h_ref[...]
        for s in range(ns):
            acc = acc + 0.5 * _dot(a_ref[s], wd_ref[s])
        row = (i % tpe) * tm + lax.broadcasted_iota(jnp.int32, (tm, 1), 0)
        dh, dg, loss = _loss_tail(acc, g_ref[...], t_ref[...], row, d)
        dh_ref[...] = dh
        dhb_ref[...] = dh.astype(BF16)

        @pl.when(i == 0)
        def _():
            dg_ref[...] = dg
            loss_ref[...] = loss

        @pl.when(i != 0)
        def _():
            dg_ref[...] += dg
            loss_ref[...] += loss

    full = pl.BlockSpec((tm, d), lambda i: (i, 0))
    gspec = pl.BlockSpec((1, d), lambda i: (0, 0))
    return _hosted_call(
        body, name=name, grid=(rows // tm,),
        in_specs=[pl.BlockSpec((ns, tm, fs), lambda i: (0, i, 0)), _resident((ns, fs, d)), full, gspec, full],
        out_specs=[full, full, gspec, pl.BlockSpec((1, 128), lambda i: (0, 0))],
        out_shape=[jax.ShapeDtypeStruct((rows, d), F32), jax.ShapeDtypeStruct((rows, d), BF16),
                   jax.ShapeDtypeStruct((1, d), F32), jax.ShapeDtypeStruct((1, 128), F32)],
        args=(a, wd, h, g, target))


def _mm_call(name, a, bt, tm, out_dtype):
    rows, k = a.shape
    n = bt.shape[0]

    def body(a_ref, b_ref, o_ref):
        o_ref[...] = _dot_nt(a_ref[...], b_ref[...]).astype(out_dtype)

    return pl.pallas_call(
        body, name=name, grid=(rows // tm,),
        in_specs=[pl.BlockSpec((tm, k), lambda i: (i, 0)), pl.BlockSpec((n, k), lambda i: (0, 0))],
        out_specs=pl.BlockSpec((tm, n), lambda i: (i, 0)),
        out_shape=jax.ShapeDtypeStruct((rows, n), out_dtype),
        compiler_params=_params(("parallel",)))(a, bt)


def _mla_heads(z, gql, gkvl, wuq, wuk, wuv):
    cq = z[:, 0:Q_RANK]
    ckv = z[:, Q_RANK:Z_KR]
    kr = z[:, Z_KR:Z_MLA]
    rq = _rms(cq, Q_RANK)
    nq = cq * rq
    cqn = (nq * gql).astype(BF16)
    rkv = _rms(ckv, KV_RANK)
    nkv = ckv * rkv
    ckvn = (nkv * gkvl).astype(BF16)
    qraw = _dot_nt(cqn, wuq)
    knope = _dot(ckvn, wuk)
    v = _dot(ckvn, wuv)
    skr = jnp.sum(kr * kr, axis=-1, keepdims=True)
    heads = []
    for hd in range(HEADS):
        qh = qraw[:, hd * D_QKP:(hd + 1) * D_QKP]
        rqh = lax.rsqrt(jnp.sum(qh * qh, axis=-1, keepdims=True) * (1.0 / D_QK) + EPS)
        kn = knope[:, hd * D_NOPE:(hd + 1) * D_NOPE]
        rkh = lax.rsqrt((jnp.sum(kn * kn, axis=-1, keepdims=True) + skr) * (1.0 / D_QK) + EPS)
        heads.append((qh * rqh, rqh, kn * rkh, kr * rkh, rkh))
    return dict(rq=rq, nq=nq, cqn=cqn, rkv=rkv, nkv=nkv, ckvn=ckvn, v=v, heads=heads)


def _mla_prep_call(z, gql, gkvl, gqh, gkh, wuq, wuk, wuv, tabs, lp, tm):
    rows = z.shape[0]
    tpe = lp // tm

    def body(z_ref, gql_ref, gkvl_ref, gqh_ref, gkh_ref, wuq_ref, wuk_ref, wuv_ref, c_ref, s1_ref, s2_ref,
             q_ref, k_ref, v_ref):
        m = _mla_heads(z_ref[...], gql_ref[...], gkvl_ref[...], wuq_ref[...], wuk_ref[...], wuv_ref[...])
        c, s1, s2 = c_ref[...], s1_ref[...], s2_ref[...]
        gq, gk = gqh_ref[...], gkh_ref[...]
        row = (pl.program_id(0) % tpe) * tm + lax.broadcasted_iota(jnp.int32, (tm, 1), 0)
        spare = (lax.broadcasted_iota(jnp.int32, (1, D_QKP - D_NOPE), 1) == D_ROPE).astype(F32)
        kmask = jnp.where(row < PAD_ROWS, NEG_INF * math.sqrt(D_QK), 0.0) * spare
        for hd in range(HEADS):
            qn, _, knn, krn, _ = m["heads"][hd]
            qg = qn * gq
            q_ref[hd, :, 0:D_NOPE] = qg[:, 0:D_NOPE].astype(BF16)
            q_ref[hd, :, D_NOPE:D_QKP] = (_rope(qg[:, D_NOPE:D_QKP], c, s1, s2) + spare).astype(BF16)
            k_ref[hd, :, 0:D_NOPE] = (knn * gk[:, 0:D_NOPE]).astype(BF16)
            k_ref[hd, :, D_NOPE:D_QKP] = (_rope(krn * gk[:, D_NOPE:D_QKP], c, s1, s2) + kmask).astype(BF16)
            v_ref[hd] = m["v"][:, hd * D_V:(hd + 1) * D_V].astype(BF16)

    def const(shape):
        return pl.BlockSpec(shape, lambda i: tuple(0 for _ in shape))

    tab = pl.BlockSpec((tm, 128), lambda i: (i % tpe, 0))
    return pl.pallas_call(
        body, name="mla_prep", grid=(rows // tm,),
        in_specs=[pl.BlockSpec((tm, Z_MLA), lambda i: (i, 0)), const((1, Q_RANK)), const((1, KV_RANK)),
                  const((1, D_QKP)), const((1, D_QKP)), const((HEADS * D_QKP, Q_RANK)),
                  const((KV_RANK, HEADS * D_NOPE)), const((KV_RANK, HEADS * D_V)), tab, tab, tab],
        out_specs=[pl.BlockSpec((HEADS, tm, D_QKP), lambda i: (0, i, 0)),
                   pl.BlockSpec((HEADS, tm, D_QKP), lambda i: (0, i, 0)),
                   pl.BlockSpec((HEADS, tm, D_V), lambda i: (0, i, 0))],
        out_shape=[jax.ShapeDtypeStruct((HEADS, rows, D_QKP), BF16),
                   jax.ShapeDtypeStruct((HEADS, rows, D_QKP), BF16),
                   jax.ShapeDtypeStruct((HEADS, rows, D_V), BF16)],
        compiler_params=_params(("parallel",)))(z, gql, gkvl, gqh, gkh, wuq, wuk, wuv, *tabs)


Q_BLOCK_ROWS = 528
Q_BLOCK_ROWS_BWD = 192


def _q_block(lp):
    return _row_tile(lp, Q_BLOCK_ROWS)


def _key_end(ext, lp):
    return min(lp, -(-ext // CHUNK) * CHUNK)


def _diag_bias(qb, j0, nk):
    shift = CHUNK.bit_length() - 1
    r = jnp.right_shift(j0 + lax.broadcasted_iota(jnp.int32, (qb, nk), 0), shift)
    c = jnp.right_shift(j0 + lax.broadcasted_iota(jnp.int32, (qb, nk), 1), shift)
    return jnp.where(c <= r, 0.0, NEG_INF)


def _attn_fwd_call(q, k, v, nb, lp, comm=None):
    rows = nb * lp
    qb = _q_block(lp)
    scale = 1.0 / math.sqrt(D_QK)

    def body(q_ref, k_ref, v_ref, o_ref, lse_ref):
        for j in range(lp // qb):
            j0, ext = j * qb, (j + 1) * qb
            kend = _key_end(ext, lp)
            qj = q_ref[0, j0:ext, :]
            sd = _dot_nt(qj, k_ref[0, j0:kend, :]) * scale + _diag_bias(qb, j0, kend - j0)
            mx = jnp.max(sd, axis=-1, keepdims=True)
            if j > 0:
                so = _dot_nt(qj, k_ref[0, 0:j0, :]) * scale
                mx = jnp.maximum(mx, jnp.max(so, axis=-1, keepdims=True))
            pd = jnp.exp(sd - mx)
            l = jnp.sum(pd, axis=-1, keepdims=True)
            o = _dot(pd.astype(BF16), v_ref[0, j0:kend, :])
            if j > 0:
                po = jnp.exp(so - mx)
                l = l + jnp.sum(po, axis=-1, keepdims=True)
                o = o + _dot(po.astype(BF16), v_ref[0, 0:j0, :])
            o_ref[j0:ext, :] = o / l
            lse_ref[0, j0:ext, :] = mx + jnp.log(l)

    return _hosted_call(
        body, name="attn_fwd", grid=(nb, HEADS),
        in_specs=[pl.BlockSpec((1, lp, D_QKP), lambda b, h: (h, b, 0)),
                  pl.BlockSpec((1, lp, D_QKP), lambda b, h: (h, b, 0)),
                  pl.BlockSpec((1, lp, D_V), lambda b, h: (h, b, 0))],
        out_specs=[pl.BlockSpec((lp, D_V), lambda b, h: (b, h)),
                   pl.BlockSpec((1, lp, 1), lambda b, h: (h, b, 0))],
        out_shape=[jax.ShapeDtypeStruct((rows, MLA_W), F32),
                   jax.ShapeDtypeStruct((HEADS, rows, 1), F32)],
        dims=("parallel", "parallel"), args=(q, k, v), comm=comm)


def _attn_bwd_call(q, k, v, o, lse, do, nb, lp, comm=None):
    rows = nb * lp
    qb = _row_tile(lp, Q_BLOCK_ROWS_BWD)
    scale = 1.0 / math.sqrt(D_QK)

    def body(q_ref, k_ref, v_ref, o_ref, lse_ref, do_ref, dq_ref, dk_ref, dv_ref, dk_acc, dv_acc):
        dk_acc[...] = jnp.zeros_like(dk_acc)
        dv_acc[...] = jnp.zeros_like(dv_acc)
        shift = CHUNK.bit_length() - 1
        for j in range(lp // qb):
            j0, ext = j * qb, (j + 1) * qb
            kend = _key_end(ext, lp)
            qj = q_ref[0, j0:ext, :]
            doj = do_ref[j0:ext, :]
            delta = jnp.sum(doj * o_ref[j0:ext, :], axis=-1, keepdims=True)
            dob = doj.astype(BF16)
            kk = k_ref[0, 0:kend, :]
            qchunk = jnp.right_shift(j0 + lax.broadcasted_iota(jnp.int32, (qb, kend), 0), shift)
            kchunk = jnp.right_shift(lax.broadcasted_iota(jnp.int32, (qb, kend), 1), shift)
            s = _dot_nt(qj, kk) * scale - lse_ref[0, j0:ext, :]
            p = jnp.where(kchunk <= qchunk, jnp.exp(s), 0.0)
            dv_acc[0:kend, :] += _dot_tn(p.astype(BF16), dob)
            dp = _dot_nt(dob, v_ref[0, 0:kend, :])
            ds = (p * (dp - delta) * scale).astype(BF16)
            dq_ref[0, j0:ext, :] = _dot(ds, kk).astype(BF16)
            dk_acc[0:kend, :] += _dot_tn(ds, qj)
        dk_ref[0] = dk_acc[...].astype(BF16)
        dv_ref[0] = dv_acc[...].astype(BF16)

    qspec = pl.BlockSpec((1, lp, D_QKP), lambda b, h: (h, b, 0))
    vspec = pl.BlockSpec((1, lp, D_V), lambda b, h: (h, b, 0))
    ospec = pl.BlockSpec((lp, D_V), lambda b, h: (b, h))
    return _hosted_call(
        body, name="attn_bwd", grid=(nb, HEADS),
        in_specs=[qspec, qspec, vspec, ospec, pl.BlockSpec((1, lp, 1), lambda b, h: (h, b, 0)), ospec],
        out_specs=[qspec, qspec, vspec],
        out_shape=[jax.ShapeDtypeStruct((HEADS, rows, D_QKP), BF16),
                   jax.ShapeDtypeStruct((HEADS, rows, D_QKP), BF16),
                   jax.ShapeDtypeStruct((HEADS, rows, D_V), BF16)],
        scratch_shapes=[pltpu.VMEM((lp, D_QKP), F32), pltpu.VMEM((lp, D_V), F32)],
        dims=("parallel", "parallel"), args=(q, k, v, o, lse, do), comm=comm)


def _lru_gates(u, cw, cb, wa, ba, wx, bx, lam, lp):
    xc = (cw[3:4, :] * u + cw[2:3, :] * pltpu.roll(u, 1, 0) + cw[1:2, :] * pltpu.roll(u, 2, 0)
          + cw[0:1, :] * pltpu.roll(u, 3, 0) + cb)
    xcb = xc.astype(BF16)
    r = _sigmoid(_dot(xcb, wa) + ba)
    i = _sigmoid(_dot(xcb, wx) + bx)
    sp = _softplus_neg(lam)
    la = -C_RGLRU * r * sp
    a = jnp.exp(la)
    x2 = 2.0 * la
    e2 = a * a
    m2 = jnp.maximum(jnp.where(x2 > -0.01, -x2 * (1.0 + 0.5 * x2), 1.0 - e2), 1e-30)
    rs = lax.rsqrt(m2)
    row = lax.broadcasted_iota(jnp.int32, (lp, LRU_TILE), 0)
    first = row == PAD_ROWS
    valid = row >= PAD_ROWS
    mult_eff = jnp.where(first, 1.0, m2 * rs)
    return dict(xc=xc, xcb=xcb, r=r, i=i, sp=sp, a=a, e2=e2, rs=rs, mult_eff=mult_eff, first=first, valid=valid)


def _scan_rows(a, b, a_s, b_s, out_ref, lp, reverse):
    sub = lax.broadcasted_iota(jnp.int32, (lp, LRU_TILE), 0) & 7
    for dist in (1, 2, 4):
        shift = lp - dist if reverse else dist
        keep = (sub + dist <= 7) if reverse else (sub >= dist)
        a_sh = pltpu.roll(a, shift, 0)
        b_sh = pltpu.roll(b, shift, 0)
        b = jnp.where(keep, a * b_sh + b, b)
        a = jnp.where(keep, a * a_sh, a)
    a_s[...] = a
    b_s[...] = b
    n_groups = lp // 8
    edge = 0 if reverse else 7

    def group(gi, carry):
        r0 = pl.multiple_of(((n_groups - 1 - gi) if reverse else gi) * 8, 8)
        a8 = a_s[pl.ds(r0, 8), :]
        b8 = b_s[pl.ds(r0, 8), :]
        out_ref[pl.ds(r0, 8), :] = a8 * carry + b8
        return a8[edge:edge + 1, :] * carry + b8[edge:edge + 1, :]

    lax.fori_loop(0, n_groups, group, jnp.zeros((1, LRU_TILE), F32), unroll=4)


def _lru_specs(lp):
    seq = lambda col0, stride=1: pl.BlockSpec((lp, LRU_TILE), lambda t, b: (b, col0 + stride * t))
    cw = pl.BlockSpec((1, CONV_K, LRU_TILE), lambda t, b: (t, 0, 0))
    vec = pl.BlockSpec((1, LRU_TILE), lambda t, b: (0, t))
    mat = pl.BlockSpec((1, LRU_TILE, LRU_TILE), lambda t, b: (t, 0, 0))
    return seq, cw, vec, mat


def _lru_fwd_call(z, cw, cb, wa, ba, wx, bx, lam, nb, lp, comm=None):
    rows = nb * lp
    seq, cwspec, vec, mat = _lru_specs(lp)

    def body(u_ref, g_ref, cw_ref, cb_ref, wa_ref, ba_ref, wx_ref, bx_ref, lam_ref, y_ref, hs_ref, a_s, b_s):
        m = _lru_gates(u_ref[...], cw_ref[0], cb_ref[...], wa_ref[0], ba_ref[...], wx_ref[0], bx_ref[...],
                       lam_ref[...], lp)
        a = jnp.where(m["valid"], m["a"], 0.0)
        b = jnp.where(m["valid"], m["mult_eff"] * (m["i"] * m["xc"]), 0.0)
        _scan_rows(a, b, a_s, b_s, hs_ref, lp, reverse=False)
        gl, _ = _gelu(g_ref[...])
        y_ref[...] = hs_ref[...] * gl

    oshape = jax.ShapeDtypeStruct((rows, LRU_W), F32)
    return _hosted_call(
        body, name="lru_fwd", grid=(N_LRU_TILES, nb),
        in_specs=[seq(Z_U // LRU_TILE, 2), seq(Z_U // LRU_TILE + 1, 2), cwspec, vec, mat, vec, mat, vec, vec],
        out_specs=[seq(0), seq(0)], out_shape=[oshape, oshape],
        scratch_shapes=[pltpu.VMEM((lp, LRU_TILE), F32), pltpu.VMEM((lp, LRU_TILE), F32)],
        dims=("parallel", "parallel"), args=(z, z, cw, cb, wa, ba, wx, bx, lam), comm=comm)


def _lru_bwd_call(z, dz, hs, dy, cw, cb, wa, ba, wx, bx, lam, nb, lp, comm=None):
    rows = nb * lp
    seq, cwspec, vec, mat = _lru_specs(lp)

    def body(u_ref, g_ref, hs_ref, dy_ref, cw_ref, cb_ref, wa_ref, ba_ref, wx_ref, bx_ref, lam_ref, dz_in,
             dz_ref, dcw_ref, dcb_ref, dwa_ref, dba_ref, dwx_ref, dbx_ref, dlam_ref, a_s, b_s, d_s):
        du_ref = dz_ref.at[:, 0:LRU_TILE]
        dg_ref = dz_ref.at[:, LRU_TILE:2 * LRU_TILE]
        b_idx = pl.program_id(1)
        u = u_ref[...]
        cw = cw_ref[0]
        wa, wx = wa_ref[0], wx_ref[0]
        lam = lam_ref[...]
        m = _lru_gates(u, cw, cb_ref[...], wa, ba_ref[...], wx, bx_ref[...], lam, lp)
        gate = g_ref[...]
        gl, th = _gelu(gate)
        dy = dy_ref[...]
        hs = hs_ref[...]
        dg_ref[...] = (dy * hs * _gelu_grad(gate, th)).astype(BF16)
        a_eff = jnp.where(m["valid"], m["a"], 0.0)
        _scan_rows(pltpu.roll(a_eff, lp - 1, 0), dy * gl, a_s, b_s, d_s, lp, reverse=True)
        ds = d_s[...]
        xc, r, i = m["xc"], m["r"], m["i"]
        row = lax.broadcasted_iota(jnp.int32, (lp, LRU_TILE), 0)
        da = ds * jnp.where(row >= 1, pltpu.roll(hs, 1, 0), 0.0)
        db = jnp.where(m["valid"], ds, 0.0)
        di = db * m["mult_eff"] * xc
        dxc = db * m["mult_eff"] * i
        live = m["valid"] & jnp.logical_not(m["first"])
        dm = jnp.where(live, db * i * xc, 0.0)
        dla = da * m["a"] - dm * (m["e2"] * m["rs"])
        dr = dla * (-C_RGLRU * m["sp"])
        dsp = jnp.sum(dla * (-C_RGLRU * r), axis=0, keepdims=True)
        dpr = (dr * r * (1.0 - r))
        dpi = (di * i * (1.0 - i))
        dprb, dpib = dpr.astype(BF16), dpi.astype(BF16)
        dxc = dxc + _dot_nt(dprb, wa) + _dot_nt(dpib, wx)
        du = (cw[3:4, :] * dxc + cw[2:3, :] * pltpu.roll(dxc, lp - 1, 0) + cw[1:2, :] * pltpu.roll(dxc, lp - 2, 0)
              + cw[0:1, :] * pltpu.roll(dxc, lp - 3, 0))
        du_ref[...] = jnp.where(m["valid"], du, 0.0).astype(BF16)
        tap = lax.broadcasted_iota(jnp.int32, (CONV_K, LRU_TILE), 0)
        dcw = jnp.zeros((CONV_K, LRU_TILE), F32)
        for kk in range(CONV_K):
            shifted = u if kk == CONV_K - 1 else pltpu.roll(u, CONV_K - 1 - kk, 0)
            dcw = jnp.where(tap == kk, jnp.sum(dxc * shifted, axis=0, keepdims=True), dcw)
        parts = [(dcw_ref, dcw[None]), (dcb_ref, jnp.sum(dxc, axis=0, keepdims=True)[None]),
                 (dwa_ref, _dot_tn(m["xcb"], dprb)[None]), (dba_ref, jnp.sum(dpr, axis=0, keepdims=True)[None]),
                 (dwx_ref, _dot_tn(m["xcb"], dpib)[None]), (dbx_ref, jnp.sum(dpi, axis=0, keepdims=True)[None]),
                 (dlam_ref, (dsp * (-jax.nn.sigmoid(-lam)))[None])]

        @pl.when(b_idx == 0)
        def _():
            for ref, val in parts:
                ref[...] = val

        @pl.when(b_idx != 0)
        def _():
            for ref, val in parts:
                ref[...] += val

    vec3 = pl.BlockSpec((1, 1, LRU_TILE), lambda t, b: (t, 0, 0))
    vshape = jax.ShapeDtypeStruct((N_LRU_TILES, 1, LRU_TILE), F32)
    mshape = jax.ShapeDtypeStruct((N_LRU_TILES, LRU_TILE, LRU_TILE), F32)
    pair = pl.BlockSpec((lp, 2 * LRU_TILE), lambda t, b: (b, Z_U // (2 * LRU_TILE) + t))
    return _hosted_call(
        body, name="lru_bwd", grid=(N_LRU_TILES, nb),
        in_specs=[seq(Z_U // LRU_TILE, 2), seq(Z_U // LRU_TILE + 1, 2), seq(0), seq(0), cwspec, vec, mat, vec, mat,
                  vec, vec, pl.BlockSpec(memory_space=pl.ANY)],
        out_specs=[pair, cwspec, vec3, mat, vec3, mat, vec3, vec3],
        out_shape=[jax.ShapeDtypeStruct(dz.shape, dz.dtype), jax.ShapeDtypeStruct((N_LRU_TILES, CONV_K, LRU_TILE), F32),
                   vshape, mshape, vshape, mshape, vshape, vshape],
        scratch_shapes=[pltpu.VMEM((lp, LRU_TILE), F32)] * 3,
        dims=("parallel", "arbitrary"), args=(z, z, hs, dy, cw, cb, wa, ba, wx, bx, lam, dz), comm=comm,
        aliases={11: 0})


def _mix_out_call(ya, yl, ga, gl, wout, h, next_gain, tm):
    rows, d = h.shape

    def body(ya_ref, yl_ref, ga_ref, gl_ref, w_ref, h_ref, ng_ref, y_ref, o_ref, u_ref):
        a = ya_ref[...]
        l = yl_ref[...]
        an = (a * _rms(a, MLA_W) * ga_ref[...]).astype(BF16)
        ln = (l * _rms(l, LRU_W) * gl_ref[...]).astype(BF16)
        y_ref[:, 0:MLA_W] = an
        y_ref[:, MLA_W:MLA_W + LRU_W] = ln
        out = h_ref[...] + _dot(an, w_ref[0:MLA_W, :]) + _dot(ln, w_ref[MLA_W:MLA_W + LRU_W, :])
        o_ref[...] = out
        u_ref[...] = (out * _rms(out, d) * ng_ref[...]).astype(BF16)

    half = pl.BlockSpec((tm, MLA_W), lambda i: (i, 0))
    g = pl.BlockSpec((1, MLA_W), lambda i: (0, 0))
    full = pl.BlockSpec((tm, d), lambda i: (i, 0))
    return pl.pallas_call(
        body, name="mix_out", grid=(rows // tm,),
        in_specs=[half, half, g, g, pl.BlockSpec((MLA_W + LRU_W, d), lambda i: (0, 0)), full,
                  pl.BlockSpec((1, d), lambda i: (0, 0))],
        out_specs=[full, full, full],
        out_shape=[jax.ShapeDtypeStruct((rows, MLA_W + LRU_W), BF16), jax.ShapeDtypeStruct((rows, d), F32),
                   jax.ShapeDtypeStruct((rows, d), BF16)],
        compiler_params=_params(("parallel",)))(ya, yl, ga, gl, wout, h, next_gain)


def _ffn_dact_call(name, dhb, wd, gate, up, tm, comm=None):
    rows, d = dhb.shape
    ns, fs, _ = wd.shape

    nsub = 2 if tm % 32 == 0 else 1
    sub = tm // nsub

    def body(dh_ref, wd_ref, g_ref, p_ref, dg_ref, dp_ref):
        wd = wd_ref[0]
        for r in range(nsub):
            rs = slice(r * sub, (r + 1) * sub)
            da = (0.5 * _dot_nt(dh_ref[rs, :], wd)).astype(BF16)
            g = g_ref[0, rs, :]
            p = p_ref[0, rs, :]
            sg = jax.nn.sigmoid(g)
            dg_ref[0, rs, :] = (da * p) * (sg * (1.0 + g * (1.0 - sg)))
            dp_ref[0, rs, :] = da * (g * sg)

    aspec = pl.BlockSpec((1, tm, fs), lambda s, i: (s, i, 0))
    oshape = jax.ShapeDtypeStruct((ns, rows, fs), BF16)
    return _hosted_call(
        body, name=name, grid=(ns, rows // tm),
        in_specs=[pl.BlockSpec((tm, d), lambda s, i: (i, 0)), pl.BlockSpec((1, fs, d), lambda s, i: (s, 0, 0)),
                  aspec, aspec],
        out_specs=[aspec, aspec], out_shape=[oshape, oshape],
        dims=("parallel", "parallel"), args=(dhb, wd, gate, up), comm=comm)


def _norm_in_bwd_call(name, pieces, h, g, dres, tm, comm=None, part=(0, 1), prev=None, mix=None):
    rows, d = h.shape
    npc = len(pieces)
    steps = rows // tm // part[1]
    off = part[0] * steps
    n_prev = 0 if prev is None else 2
    n_mix = 0 if mix is None else 5

    def body(*refs):
        d_refs = refs[0:2 * npc:2]
        w_refs = refs[1:2 * npc:2]
        h_ref, g_ref, dres_ref = refs[2 * npc:2 * npc + 3]
        mix_in_refs = refs[2 * npc + 3:2 * npc + 3 + n_mix]
        dh_ref, dhb_ref, dg_ref = refs[2 * npc + 3 + n_mix + n_prev:2 * npc + 6 + n_mix + n_prev]
        mix_out_refs = refs[2 * npc + 6 + n_mix + n_prev:]
        du = jnp.zeros((tm, d), F32)
        for d_ref, w_ref in zip(d_refs, w_refs):
            if len(d_ref.shape) == 3:
                for s in range(d_ref.shape[0]):
                    du = du + _dot(d_ref[s], w_ref[s])
            else:
                du = du + _dot(d_ref[...], w_ref[...])
        x = h_ref[...]
        r = _rms(x, d)
        n = x * r
        dh = dres_ref[...] + _rms_bwd(du * g_ref[...], n, r, d)
        dhb = dh.astype(BF16)
        dh_ref[...] = dh
        dhb_ref[...] = dhb
        sums = [(dg_ref, jnp.sum(du * n, axis=0, keepdims=True))]
        if mix is not None:
            wo_ref, ya_ref, yl_ref, ga_ref, gl_ref = mix_in_refs
            dya_ref, dyl_ref, dga_ref, dgl_ref = mix_out_refs
            dy = _dot_nt(dhb, wo_ref[...])
            for val, gain_ref, lo, out_ref, acc_ref in ((ya_ref[...], ga_ref, 0, dya_ref, dga_ref),
                                                        (yl_ref[...], gl_ref, MLA_W, dyl_ref, dgl_ref)):
                rb = _rms(val, MLA_W)
                nb_ = val * rb
                dyn = dy[:, lo:lo + MLA_W]
                out_ref[...] = _rms_bwd(dyn * gain_ref[...], nb_, rb, MLA_W)
                sums.append((acc_ref, jnp.sum(dyn * nb_, axis=0, keepdims=True)))

        @pl.when(pl.program_id(0) == 0)
        def _():
            for ref, val in sums:
                ref[...] = val

        @pl.when(pl.program_id(0) != 0)
        def _():
            for ref, val in sums:
                ref[...] += val

    in_specs, args = [], []
    for dd, w in pieces:
        if dd.ndim == 3:
            in_specs.append(pl.BlockSpec((dd.shape[0], tm, dd.shape[2]), lambda i: (0, i + off, 0)))
            in_specs.append(_resident(w.shape))
        else:
            in_specs.append(pl.BlockSpec((tm, dd.shape[1]), lambda i: (i + off, 0)))
            in_specs.append(_resident(w.shape))
        args += [dd, w]
    full = pl.BlockSpec((tm, d), lambda i: (i + off, 0))
    gspec = pl.BlockSpec((1, d), lambda i: (0, 0))
    half = pl.BlockSpec((tm, MLA_W), lambda i: (i + off, 0))
    hgain = pl.BlockSpec((1, MLA_W), lambda i: (0, 0))
    mix_in_specs = [] if mix is None else [_resident(mix[0].shape), half, half, hgain, hgain]
    mix_out_specs = [] if mix is None else [half, half, hgain, hgain]
    mix_out_shapes = [] if mix is None else [
        jax.ShapeDtypeStruct((rows, MLA_W), F32), jax.ShapeDtypeStruct((rows, LRU_W), F32),
        jax.ShapeDtypeStruct((1, MLA_W), F32), jax.ShapeDtypeStruct((1, LRU_W), F32)]
    n_in = len(in_specs) + 3 + n_mix
    return _hosted_call(
        body, name=name, grid=(steps,),
        in_specs=in_specs + [full, gspec, full] + mix_in_specs + _any_specs(n_prev),
        out_specs=[full, full, gspec] + mix_out_specs,
        out_shape=[jax.ShapeDtypeStruct((rows, d), F32), jax.ShapeDtypeStruct((rows, d), BF16),
                   jax.ShapeDtypeStruct((1, d), F32)] + mix_out_shapes,
        args=(*args, h, g, dres, *(mix or ()), *(prev or ())), comm=comm,
        aliases={n_in: 0, n_in + 1: 1} if prev is not None else {})


def _wgrad_call(name, a, b, scale=1.0, comm=None):
    a3, b3 = a.ndim == 3, b.ndim == 3
    ns = a.shape[0] if a3 else (b.shape[0] if b3 else 1)
    rows, m = a.shape[-2:]
    n = b.shape[-1]
    tmm = m if a3 else _col_tile(m, 256)

    def body(a_ref, b_ref, o_ref):
        av = a_ref[0] if a3 else a_ref[...]
        bv = b_ref[0] if b3 else b_ref[...]
        res = _dot_tn(av, bv)
        if scale != 1.0:
            res = res * scale
        if a3 or b3:
            o_ref[0] = res
        else:
            o_ref[...] = res

    aspec = (pl.BlockSpec((1, rows, tmm), lambda s, j: (s, 0, j)) if a3
             else pl.BlockSpec((rows, tmm), lambda s, j: (0, j)))
    bspec = (pl.BlockSpec((1, rows, n), lambda s, j: (s, 0, 0)) if b3
             else pl.BlockSpec((rows, n), lambda s, j: (0, 0)))
    if a3 or b3:
        ospec = pl.BlockSpec((1, tmm, n), lambda s, j: (s, j, 0))
        oshape = jax.ShapeDtypeStruct((ns, m, n), F32)
    else:
        ospec = pl.BlockSpec((tmm, n), lambda s, j: (j, 0))
        oshape = jax.ShapeDtypeStruct((m, n), F32)
    return _hosted_call(
        body, name=name, grid=(ns, m // tmm), in_specs=[aspec, bspec], out_specs=[ospec], out_shape=[oshape],
        dims=("parallel", "parallel"), args=(a, b), comm=comm)[0]


def _mla_prep_bwd_call(z, dq, dk, dv, gql, gkvl, gqh, gkh, wuq, wuk, wuv, tabs, lp, tm, comm=None):
    rows = z.shape[0]
    tpe = lp // tm

    def body(z_ref, dq_ref, dk_ref, dv_ref, gql_ref, gkvl_ref, gqh_ref, gkh_ref, wuq_ref, wuk_ref, wuv_ref,
             c_ref, s1_ref, s2_ref, dz_ref, dgql_ref, dgkvl_ref, dgqh_ref, dgkh_ref, dwuq_ref, dwuk_ref, dwuv_ref,
             dqp_ref, dkn_ref, dvv_ref):
        gql, gkvl = gql_ref[...], gkvl_ref[...]
        gq, gk = gqh_ref[...], gkh_ref[...]
        wuq, wuk, wuv = wuq_ref[...], wuk_ref[...], wuv_ref[...]
        m = _mla_heads(z_ref[...], gql, gkvl, wuq, wuk, wuv)
        c, s1, s2 = c_ref[...], s1_ref[...], s2_ref[...]
        dgq = jnp.zeros((1, D_QKP), F32)
        dgk = jnp.zeros((1, D_QKP), F32)
        dkr = jnp.zeros((tm, D_QKP - D_NOPE), F32)
        for hd in range(HEADS):
            qn, rqh, knn, krn, rkh = m["heads"][hd]
            dqg = jnp.concatenate([dq_ref[hd, :, 0:D_NOPE].astype(F32),
                                   _rope_t(dq_ref[hd, :, D_NOPE:D_QKP].astype(F32), c, s1, s2)], axis=1)
            dgq = dgq + jnp.sum(dqg * qn, axis=0, keepdims=True)
            dqn = dqg * gq
            dqr = rqh * (dqn - qn * (jnp.sum(dqn * qn, axis=-1, keepdims=True) * (1.0 / D_QK)))
            dqp_ref[:, hd * D_QKP:(hd + 1) * D_QKP] = dqr.astype(BF16)
            kn_full = jnp.concatenate([knn, krn], axis=1)
            dkg = jnp.concatenate([dk_ref[hd, :, 0:D_NOPE].astype(F32),
                                   _rope_t(dk_ref[hd, :, D_NOPE:D_QKP].astype(F32), c, s1, s2)], axis=1)
            dgk = dgk + jnp.sum(dkg * kn_full, axis=0, keepdims=True)
            dkn = dkg * gk
            dkraw = rkh * (dkn - kn_full * (jnp.sum(dkn * kn_full, axis=-1, keepdims=True) * (1.0 / D_QK)))
            dkn_ref[:, hd * D_NOPE:(hd + 1) * D_NOPE] = dkraw[:, 0:D_NOPE].astype(BF16)
            dkr = dkr + dkraw[:, D_NOPE:D_QKP]
            dvv_ref[:, hd * D_V:(hd + 1) * D_V] = dv_ref[hd]
        dcqn = _dot(dqp_ref[...], wuq)
        dckvn = _dot_nt(dkn_ref[...], wuk) + _dot_nt(dvv_ref[...], wuv)
        dz_ref[:, 0:Q_RANK] = _rms_bwd(dcqn * gql, m["nq"], m["rq"], Q_RANK).astype(BF16)
        dz_ref[:, Q_RANK:Z_KR] = _rms_bwd(dckvn * gkvl, m["nkv"], m["rkv"], KV_RANK).astype(BF16)
        dz_ref[:, Z_KR:Z_MLA] = dkr.astype(BF16)
        parts = [(dgql_ref, jnp.sum(dcqn * m["nq"], axis=0, keepdims=True)),
                 (dgkvl_ref, jnp.sum(dckvn * m["nkv"], axis=0, keepdims=True)), (dgqh_ref, dgq), (dgkh_ref, dgk),
                 (dwuq_ref, _dot_tn(dqp_ref[...], m["cqn"])), (dwuk_ref, _dot_tn(m["ckvn"], dkn_ref[...])),
                 (dwuv_ref, _dot_tn(m["ckvn"], dvv_ref[...]))]

        @pl.when(pl.program_id(0) == 0)
        def _():
            for ref, val in parts:
                ref[...] = val

        @pl.when(pl.program_id(0) != 0)
        def _():
            for ref, val in parts:
                ref[...] += val

    def const(shape):
        return pl.BlockSpec(shape, lambda i: tuple(0 for _ in shape))

    tab = pl.BlockSpec((tm, 128), lambda i: (i % tpe, 0))
    hq = pl.BlockSpec((HEADS, tm, D_QKP), lambda i: (0, i, 0))
    hv = pl.BlockSpec((HEADS, tm, D_V), lambda i: (0, i, 0))

    def rowspec(n):
        return pl.BlockSpec((tm, n), lambda i: (i, 0))

    return _hosted_call(
        body, name="mla_prep_bwd", grid=(rows // tm,),
        in_specs=[rowspec(Z_MLA), hq, hq, hv, const((1, Q_RANK)), const((1, KV_RANK)), const((1, D_QKP)),
                  const((1, D_QKP)), const((HEADS * D_QKP, Q_RANK)), const((KV_RANK, HEADS * D_NOPE)),
                  const((KV_RANK, HEADS * D_V)), tab, tab, tab],
        out_specs=[rowspec(Z_MLA), const((1, Q_RANK)), const((1, KV_RANK)), const((1, D_QKP)), const((1, D_QKP)),
                   const((HEADS * D_QKP, Q_RANK)), const((KV_RANK, HEADS * D_NOPE)), const((KV_RANK, HEADS * D_V))],
        out_shape=[jax.ShapeDtypeStruct((rows, Z_W), BF16),
                   jax.ShapeDtypeStruct((1, Q_RANK), F32), jax.ShapeDtypeStruct((1, KV_RANK), F32),
                   jax.ShapeDtypeStruct((1, D_QKP), F32), jax.ShapeDtypeStruct((1, D_QKP), F32),
                   jax.ShapeDtypeStruct((HEADS * D_QKP, Q_RANK), F32),
                   jax.ShapeDtypeStruct((KV_RANK, HEADS * D_NOPE), F32), jax.ShapeDtypeStruct((KV_RANK, HEADS * D_V), F32)],
        scratch_shapes=[pltpu.VMEM((tm, HEADS * D_QKP), BF16), pltpu.VMEM((tm, HEADS * D_NOPE), BF16),
                        pltpu.VMEM((tm, HEADS * D_V), BF16)],
        args=(z, dq, dk, dv, gql, gkvl, gqh, gkh, wuq, wuk, wuv, *tabs), comm=comm)


def _local_step(h0, target, w, nb, lp, sched=None):
    tm = _row_tile(nb * lp, MATMUL_ROWS)
    te = _row_tile(lp, ELEMENTWISE_ROWS)
    tabs = _rope_tables(lp)
    g = {}
    if sched is None:
        host = lambda stage: None
    else:
        sched.g = g
        host = sched.host

    def ffn_act(tag, u, split):
        if split:
            gate = _ffn_gate_call(tag + "_gate", u, w[tag + "_w_gate"], tm, host(tag + "_gate"))
            up, act = _ffn_upact_call(tag + "_upact", u, w[tag + "_w_up"], gate, tm, host(tag + "_upact"))
        else:
            gate, up, act = _ffn_up_call(tag + "_up", u, w[tag + "_w_gate"], w[tag + "_w_up"], tm, host(tag + "_up"))
        return u, gate, up, act

    def ffn_bwd(tag, h, saved, dh, dhb, split, mix=None):
        u, gate, up, act = saved
        dgate, dup = _ffn_dact_call(tag + "_dact", dhb, w[tag + "_w_down"], gate, up, tm, host(tag + "_dact"))
        g[tag + "_w_down"] = _wgrad_call(tag + "_dwd", act, dhb, 0.5, host(tag + "_dwd"))
        g[tag + "_w_gate"] = _wgrad_call(tag + "_dwg", dgate, u, 1.0, host(tag + "_dwg"))
        g[tag + "_w_up"] = _wgrad_call(tag + "_dwu", dup, u, 1.0, host(tag + "_dwu"))
        pieces = [(dgate, w[tag + "_w_gate"]), (dup, w[tag + "_w_up"])]
        if not split:
            res = _norm_in_bwd_call(tag + "_din", pieces, h, w[tag + "_norm"], dh, te, host(tag + "_din"), mix=mix)
            g[tag + "_norm"] = res[2]
            return (res[0], res[1], *res[3:])
        dh_a, dhb_a, dg_a = _norm_in_bwd_call(tag + "_din_a", pieces, h, w[tag + "_norm"], dh, te,
                                              host(tag + "_din_a"), part=(0, 2))
        dh_in, dhb_in, dg_b = _norm_in_bwd_call(tag + "_din_b", pieces, h, w[tag + "_norm"], dh, te,
                                                host(tag + "_din_b"), part=(1, 2), prev=(dh_a, dhb_a))
        g[tag + "_norm"] = dg_a + dg_b
        return dh_in, dhb_in

    s1 = ffn_act("ffn1", _rmsnorm_call("ffn1_norm", h0, w["ffn1_norm"], te, host("ffn1_norm")), False)
    h1, un = _ffn_down_call("ffn1_down", s1[3], w["ffn1_w_down"], h0, te, host("ffn1_down"), w["mix_norm"])
    z = _mm_call("mix_in", un, w["w_in"], tm, F32)
    mla_w = (w["q_latent_norm"], w["kv_latent_norm"], w["q_head_norm"], w["k_head_norm"], w["w_uq"], w["w_uk"],
             w["w_uv"])
    q, k, v = _mla_prep_call(z, *mla_w, tabs, lp, te)
    o, lse = _attn_fwd_call(q, k, v, nb, lp, host("attn_fwd"))
    lru_w = (w["conv_w"], w["conv_b"], w["gate_a_w"], w["gate_a_b"], w["gate_x_w"], w["gate_x_b"], w["lru_lambda"])
    yl, hs = _lru_fwd_call(z, *lru_w, nb, lp, host("lru_fwd"))
    y, h2, u2 = _mix_out_call(o, yl, w["attn_out_norm"], w["lru_out_norm"], w["w_out"], h1, w["ffn2_norm"], te)
    s2 = ffn_act("ffn2", u2, False)
    dh3, dh3b, g["final_norm"], loss = _ffn_down_loss_call("ffn2_down", s2[3], w["ffn2_w_down"], h2, w["final_norm"],
                                                           target, lp, te)
    g["loss"] = loss

    dh2, dh2b, dya, dyl, g["attn_out_norm"], g["lru_out_norm"] = ffn_bwd(
        "ffn2", h2, s2, dh3, dh3b, False, (w["w_out"], o, yl, w["attn_out_norm"], w["lru_out_norm"]))
    g["w_out"] = _wgrad_call("dw_out", y, dh2b)
    dq, dk, dv = _attn_bwd_call(q, k, v, o, lse, dya, nb, lp, host("attn_bwd"))
    (dz_mla, g["q_latent_norm"], g["kv_latent_norm"], g["q_head_norm"], g["k_head_norm"], g["w_uq"], g["w_uk"],
     g["w_uv"]) = _mla_prep_bwd_call(z, dq, dk, dv, *mla_w, tabs, lp, te, host("mla_prep_bwd"))
    (dz, g["conv_w"], g["conv_b"], g["gate_a_w"], g["gate_a_b"], g["gate_x_w"], g["gate_x_b"],
     g["lru_lambda"]) = _lru_bwd_call(z, dz_mla, hs, dyl, *lru_w, nb, lp, host("lru_bwd"))
    g["w_in"] = _wgrad_call("dw_in", dz, un)
    dh1, dh1b, g["mix_norm"] = _norm_in_bwd_call("mix_din", [(dz, w["w_in"])], h1, w["mix_norm"], dh2, te,
                                                 host("mix_din"))
    dh0 = ffn_bwd("ffn1", h0, s1, dh1, dh1b, True)[0]
    return loss, dh0, g


def _place():
    x, y, c = lax.axis_index("x"), lax.axis_index("y"), lax.axis_index("c")
    return x, y, c, [(1 - x, y), (x, 1 - y), (1 - x, 1 - y)]


def _any_specs(n):
    return [pl.BlockSpec(memory_space=pl.ANY)] * n


def _remote(src, dst, sems, k, dev):
    send_sems, recv_sems, base = sems
    return pltpu.make_async_remote_copy(src_ref=src, dst_ref=dst, send_sem=send_sems.at[base + k],
                                        recv_sem=recv_sems.at[base + k], device_id=dev, device_id_type=MESH)


EW_VMEM_BYTES = 24 * 1024 * 1024


def _fit_rows(rows, cols, blocks):
    return _row_tile(rows, max(16, int(EW_VMEM_BYTES // (8 * blocks)) // cols))


class _Geom:
    def __init__(self, n0, n1, blocks=1.0):
        self.n0, self.n1 = n0, n1
        self.axis = 0 if n0 % 32 == 0 else 1
        self.h0, self.h1 = (n0 // 2, n1) if self.axis == 0 else (n0, n1 // 2)
        self.tr = _fit_rows(self.h0, self.h1, blocks)
        self.nblk = self.h0 // self.tr

    def half_ref(self, ref, lead, idx):
        if self.axis == 0:
            return ref.at[(*lead, pl.ds(idx * self.h0, self.h0))]
        return ref.at[(*lead, slice(None), pl.ds(idx * self.h1, self.h1))]

    def half_block(self, lead, i, idx):
        return (*lead, idx * self.nblk + i, 0) if self.axis == 0 else (*lead, i, idx)


class _Comm:
    def __init__(self, ins, out_shapes, aliases, n_sems, start, finish, deliver):
        self.ins, self.out_shapes, self.aliases, self.n_sems = list(ins), list(out_shapes), dict(aliases), n_sems
        self.start, self.finish, self.deliver = start, finish, deliver

    def scratch(self):
        return [pltpu.SemaphoreType.DMA((self.n_sems,)), pltpu.SemaphoreType.DMA((self.n_sems,))]


def _comm_call(name, comm):
    n_in = len(comm.ins)

    def body(*refs):
        ins, outs, sems = refs[:n_in], refs[n_in:-2], (*refs[-2:], 0)
        comm.start(ins, outs, sems)
        comm.finish(ins, outs, sems)

    res = pl.pallas_call(
        body, name=name, out_shape=comm.out_shapes, in_specs=_any_specs(n_in),
        out_specs=_any_specs(len(comm.out_shapes)), input_output_aliases=comm.aliases,
        scratch_shapes=comm.scratch())(*comm.ins)
    return comm.deliver(list(res))


def _hosted_call(body, *, name, grid, in_specs, out_specs, out_shape, args, scratch_shapes=(), dims=None, comm=None,
                 prefetch=None, aliases=None):
    aliases = dict(aliases or {})
    in_specs, out_specs, out_shape = list(in_specs), list(out_specs), list(out_shape)
    n_pre = 0 if prefetch is None else 1

    def call(fn, in_specs, out_specs, out_shape, scratch, aliases, dims, args):
        if prefetch is None:
            return pl.pallas_call(
                fn, name=name, grid=grid, in_specs=in_specs, out_specs=out_specs, out_shape=out_shape,
                scratch_shapes=scratch, input_output_aliases=aliases, compiler_params=_params(dims))(*args)
        spec = pltpu.PrefetchScalarGridSpec(num_scalar_prefetch=1, grid=grid, in_specs=in_specs, out_specs=out_specs,
                                            scratch_shapes=scratch)
        return pl.pallas_call(
            fn, name=name, grid_spec=spec, out_shape=out_shape,
            input_output_aliases={i + 1: o for i, o in aliases.items()}, compiler_params=_params(dims))(prefetch, *args)

    if comm is None:
        return list(call(body, in_specs, out_specs, out_shape, list(scratch_shapes), aliases,
                         dims or ("arbitrary",) * len(grid), args))
    n_in, n_out, n_ci, n_co = len(in_specs), len(out_specs), len(comm.ins), len(comm.out_shapes)

    def wrapped(*refs):
        pre, refs = refs[:n_pre], refs[n_pre:]
        ins, cins = refs[:n_in], refs[n_in:n_in + n_ci]
        outs = refs[n_in + n_ci:n_in + n_ci + n_out]
        couts = refs[n_in + n_ci + n_out:n_in + n_ci + n_out + n_co]
        scratch, sems = refs[n_in + n_ci + n_out + n_co:-2], (*refs[-2:], 0)
        first = functools.reduce(jnp.logical_and, [pl.program_id(k) == 0 for k in range(len(grid))])
        last = functools.reduce(jnp.logical_and, [pl.program_id(k) == grid[k] - 1 for k in range(len(grid))])

        @pl.when(first)
        def _():
            comm.start(cins, couts, sems)

        body(*pre, *ins, *outs, *scratch)

        @pl.when(last)
        def _():
            comm.finish(cins, couts, sems)

    res = call(wrapped, in_specs + _any_specs(n_ci), out_specs + _any_specs(n_co), out_shape + comm.out_shapes,
               list(scratch_shapes) + comm.scratch(),
               {**aliases, **{n_in + i: n_out + o for i, o in comm.aliases.items()}},
               ("arbitrary",) * len(grid), (*args, *comm.ins))
    comm.deliver(list(res[n_out:]))
    return list(res[:n_out])


def _gather_comm(bufs, deliver):
    n = len(bufs)
    geoms = [_Geom(*b.shape[1:]) for b in bufs]

    def first(outs, sems):
        x, y, c, chips = _place()
        cps = []
        for a in range(n):
            mine = geoms[a].half_ref(outs[a], (2 * x + y,), c)
            cps += [_remote(mine, mine, sems, 6 * a + j, (cx, cy, c)) for j, (cx, cy) in enumerate(chips)]
        return cps

    def start(ins, outs, sems):
        for cp in first(outs, sems):
            cp.start()

    def finish(ins, outs, sems):
        x, y, c, chips = _place()
        sib = (x, y, 1 - c)
        passed = []
        for a in range(n):
            for j, (cx, cy) in enumerate(chips):
                land = geoms[a].half_ref(outs[a], (2 * cx + cy,), c)
                _remote(land, land, sems, 6 * a + j, sib).wait_recv()
                cp = _remote(land, land, sems, 6 * a + 3 + j, sib)
                cp.start()
                passed.append(cp)
        for a in range(n):
            for j, (cx, cy) in enumerate(chips):
                land = geoms[a].half_ref(outs[a], (2 * cx + cy,), 1 - c)
                _remote(land, land, sems, 6 * a + 3 + j, sib).wait_recv()
        for cp in first(outs, sems) + passed:
            cp.wait_send()

    return _Comm(bufs, [jax.ShapeDtypeStruct(b.shape, b.dtype) for b in bufs], {a: a for a in range(n)}, 6 * n,
                 start, finish, deliver)


def _reduce_pair_comm(grads, deliver):
    n = len(grads)
    geoms = [_Geom(*a.shape[1:]) for a in grads]

    def copies(ins, outs, sems):
        x, y, c, _ = _place()
        return [_remote(geoms[a].half_ref(ins[a], (slice(None),), 1 - c), outs[a], sems, a, (x, y, 1 - c))
                for a in range(n)]

    def start(ins, outs, sems):
        for cp in copies(ins, outs, sems):
            cp.start()

    def finish(ins, outs, sems):
        cps = copies(ins, outs, sems)
        for cp in cps:
            cp.wait_recv()
        for cp in cps:
            cp.wait_send()

    shapes = [jax.ShapeDtypeStruct((N_SHARD, g.h0, g.h1), a.dtype) for a, g in zip(grads, geoms)]
    return _Comm(grads, shapes, {}, n, start, finish, deliver)


def _reduce_chips_comm(parts, deliver):
    n = len(parts)

    def copies(ins, outs, sems):
        x, y, c, chips = _place()
        return [_remote(ins[a].at[2 * cx + cy], outs[a].at[j], sems, 3 * a + j, (cx, cy, c))
                for a in range(n) for j, (cx, cy) in enumerate(chips)]

    def start(ins, outs, sems):
        for cp in copies(ins, outs, sems):
            cp.start()

    def finish(ins, outs, sems):
        cps = copies(ins, outs, sems)
        for cp in cps:
            cp.wait_recv()
        for cp in cps:
            cp.wait_send()

    shapes = [jax.ShapeDtypeStruct((3,) + a.shape[1:], a.dtype) for a in parts]
    return _Comm(parts, shapes, {}, 3 * n, start, finish, deliver)


def _share_pair_comm(bufs, deliver):
    n = len(bufs)
    geoms = [_Geom(*b.shape) for b in bufs]

    def copies(outs, sems):
        x, y, c, _ = _place()
        cps = []
        for a in range(n):
            mine = geoms[a].half_ref(outs[a], (), c)
            cps.append(_remote(mine, mine, sems, a, (x, y, 1 - c)))
        return cps

    def start(ins, outs, sems):
        for cp in copies(outs, sems):
            cp.start()

    def finish(ins, outs, sems):
        x, y, c, _ = _place()
        for a in range(n):
            land = geoms[a].half_ref(outs[a], (), 1 - c)
            _remote(land, land, sems, a, (x, y, 1 - c)).wait_recv()
        for cp in copies(outs, sems):
            cp.wait_send()

    return _Comm(bufs, [jax.ShapeDtypeStruct(b.shape, b.dtype) for b in bufs], {a: a for a in range(n)}, n,
                 start, finish, deliver)


def _small_comm(pack, deliver):
    r, d = pack.shape

    def copies(ins, outs, sems):
        x, y, c, _ = _place()
        cps = []
        for k in range(1, 8):
            peer = (x ^ ((k >> 2) & 1), y ^ ((k >> 1) & 1), c ^ (k & 1))
            cps.append(_remote(ins[0], outs[0].at[4 * x + 2 * y + c], sems, k - 1, peer))
        return cps

    def start(ins, outs, sems):
        for cp in copies(ins, outs, sems):
            cp.start()

    def finish(ins, outs, sems):
        cps = copies(ins, outs, sems)
        for cp in cps:
            cp.wait_recv()
        for cp in cps:
            cp.wait_send()

    return _Comm([pack], [jax.ShapeDtypeStruct((8, r, d), pack.dtype)], {}, 7, start, finish, deliver)


def _join_comms(comms):
    comms = [c for c in comms if c is not None]
    if len(comms) <= 1:
        return comms[0] if comms else None
    ins, out_shapes, aliases, spans, n_sems = [], [], {}, [], 0
    for c in comms:
        aliases.update({len(ins) + i: len(out_shapes) + o for i, o in c.aliases.items()})
        spans.append((len(ins), len(ins) + len(c.ins), len(out_shapes), len(out_shapes) + len(c.out_shapes), n_sems))
        ins += c.ins
        out_shapes += c.out_shapes
        n_sems += c.n_sems

    def run(which):
        def go(all_ins, all_outs, sems):
            for c, (i0, i1, o0, o1, base) in zip(comms, spans):
                getattr(c, which)(all_ins[i0:i1], all_outs[o0:o1], (sems[0], sems[1], sems[2] + base))
        return go

    def deliver(outs):
        for c, (_, _, o0, o1, _) in zip(comms, spans):
            c.deliver(outs[o0:o1])
        return outs

    return _Comm(ins, out_shapes, aliases, n_sems, run("start"), run("finish"), deliver)


def _ew_call(name, fn, ins, out_dtypes):
    shape = ins[0].shape
    cols = shape[-1]
    rows = 1
    for s_ in shape[:-1]:
        rows *= s_
    ins2 = [a.reshape(rows, cols) for a in ins]
    tr = rows
    for t in range(16, min(rows, max(16, (1 << 19) // cols)) + 1, 16):
        if rows % t == 0:
            tr = t
    no = len(out_dtypes)

    def body(*refs):
        outs = fn(*[r[...] for r in refs[:len(ins2)]])
        for ref, val in zip(refs[len(ins2):], outs):
            ref[...] = val.astype(ref.dtype)

    spec = pl.BlockSpec((tr, cols), lambda i: (i, 0))
    res = pl.pallas_call(
        body, name=name, grid=(rows // tr,), in_specs=[spec] * len(ins2), out_specs=[spec] * no,
        out_shape=[jax.ShapeDtypeStruct((rows, cols), dt) for dt in out_dtypes],
        compiler_params=_params(("parallel",)))(*ins2)
    return [r.reshape(shape) for r in res]


def _adamw_math(w, g, m, v):
    m = ADAM_B1 * m + (1.0 - ADAM_B1) * g
    v = ADAM_B2 * v + (1.0 - ADAM_B2) * (g * g)
    m_hat = m / (1.0 - ADAM_B1 ** ADAM_STEP)
    v_hat = v / (1.0 - ADAM_B2 ** ADAM_STEP)
    delta = -ADAM_LR * (m_hat / (jnp.sqrt(v_hat) + ADAM_EPS) + ADAM_WD * w)
    return delta, m, v


def _adamw_call(name, w, g, m, v):
    return _ew_call(name, _adamw_math, [w, g, m, v], [F32, F32, F32])


def _tiled_call(name, fn, place, grid, in_items, out_items, comm=None):
    ni = len(in_items)

    def body(place_ref, *refs):
        vals = fn(*[r[...] for r in refs[:ni]])
        for ref, val in zip(refs[ni:], vals):
            ref[...] = val.astype(ref.dtype)

    return _hosted_call(
        body, name=name, grid=grid, in_specs=[pl.BlockSpec(blk, imap) for _, blk, imap in in_items],
        out_specs=[pl.BlockSpec(blk, imap) for _, _, blk, imap in out_items],
        out_shape=[jax.ShapeDtypeStruct(shp, dt) for shp, dt, _, _ in out_items],
        args=[a for a, _, _ in in_items], prefetch=place, comm=comm)


def _cast_call(name, place, shards, comm=None):
    n0, n1 = shards[0].shape
    tr = _fit_rows(n0, n1, 1.5 * len(shards))
    ins = [(a, (tr, n1), lambda i, p: (i, 0)) for a in shards]
    outs = [((N_SHARD, n0, n1), BF16, (1, tr, n1), lambda i, p: (p[0], i, 0)) for _ in shards]
    return _tiled_call(name, lambda *v: [x[None] for x in v], place, (n0 // tr,), ins, outs, comm)


def _pair_sum_call(name, place, fulls, gots):
    k = len(fulls)
    g = _Geom(*fulls[0].shape[1:], blocks=2.5 * k)
    blk = (1, g.tr, g.h1)
    ins = [(a, blk, lambda s, i, p: g.half_block((s,), i, p[1])) for a in fulls]
    ins += [(a, blk, lambda s, i, p: (s, i, 0)) for a in gots]
    outs = [((N_SHARD, g.h0, g.h1), BF16, blk, lambda s, i, p: (s, i, 0)) for _ in fulls]
    return _tiled_call(name, lambda *v: [v[j] + v[k + j] for j in range(k)], place, (N_SHARD, g.nblk), ins, outs)


def _chip_sum_call(name, place, fulls, gots, recvs, comm=None):
    k = len(fulls)
    g = _Geom(*fulls[0].shape[1:], blocks=4.5 * k)
    blk = (1, g.tr, g.h1)
    ins = [(a, blk, lambda i, p: g.half_block((p[0],), i, p[1])) for a in fulls]
    ins += [(a, blk, lambda i, p: (p[0], i, 0)) for a in gots]
    ins += [(a, (3, g.tr, g.h1), lambda i, p: (0, i, 0)) for a in recvs]
    outs = [((g.n0, g.n1), F32, (g.tr, g.h1), lambda i, p: g.half_block((), i, p[1])) for _ in fulls]

    def fn(*v):
        res = []
        for j in range(k):
            r = v[2 * k + j].astype(F32)
            res.append(v[j][0] + v[k + j][0] + r[0] + r[1] + r[2])
        return res

    return _tiled_call(name, fn, place, (g.nblk,), ins, outs, comm)


def _adamw_group_call(name, ws, gs, ms, vs, comm=None):
    k = len(ws)
    n0, n1 = ws[0].shape
    tr = _fit_rows(n0, n1, 8 * k)
    spec = pl.BlockSpec((tr, n1), lambda i: (i, 0))

    def body(*refs):
        for j in range(k):
            g = refs[k + j][...]
            delta, m, vv = _adamw_math(refs[j][...], g, refs[2 * k + j][...], refs[3 * k + j][...])
            for ref, val in zip(refs[4 * k + 4 * j:4 * k + 4 * j + 4], (g, delta, m, vv)):
                ref[...] = val

    flat = _hosted_call(
        body, name=name, grid=(n0 // tr,), in_specs=[spec] * (4 * k), out_specs=[spec] * (4 * k),
        out_shape=[jax.ShapeDtypeStruct((n0, n1), F32)] * (4 * k), dims=("parallel",),
        args=(*ws, *gs, *ms, *vs), comm=comm)
    return [flat[4 * j:4 * j + 4] for j in range(k)]


def _small_update_call(me, early, own_early, late, own_late, states):
    nd, r, d = early.shape
    widths = [a.shape[1] for a in states[0][0]]
    nw = len(widths)
    n_state = nw + 2

    def body(me_ref, e_ref, oe_ref, l_ref, ol_ref, *refs):
        state_refs, refs = refs[:3 * n_state], refs[3 * n_state:]
        (gs_ref, d_ref, nm_ref, nv_ref), rows, packs = refs[:4], refs[4:4 + 4 * nw], refs[4 + 4 * nw:]
        for kind, pack in enumerate(packs):
            srefs = state_refs[kind * n_state:(kind + 1) * n_state]
            pack[...] = jnp.zeros_like(pack)
            for k, width in enumerate(widths):
                pack[k:k + 1, 0:width] = srefs[k][...]
            pack[ROW_GATE_A:ROW_GATE_A + 32, :] = srefs[nw][...]
            pack[ROW_GATE_X:ROW_GATE_X + 32, :] = srefs[nw + 1][...]
        w_ref, m_ref, v_ref = packs
        mine = me_ref[0]

        def total(g_ref, own_ref):
            acc = None
            for k in range(nd):
                part = jnp.where(mine == k, own_ref[...], g_ref[k])
                acc = part if acc is None else acc + part
            return acc

        gs = total(e_ref, oe_ref)
        ls = total(l_ref, ol_ref)
        gs_ref[...] = gs
        first = gs[0:8] + ls[0:8]
        gs_ref[0:8, :] = first
        gs_ref[ROW_META:ROW_META + N_META, :] = gs[ROW_META:ROW_META + N_META] + ls[8:8 + N_META]
        grads = jnp.concatenate([first, gs[8:SMALL_ADAM_ROWS]], axis=0)
        delta, m, v = _adamw_math(w_ref[...], grads, m_ref[...], v_ref[...])
        d_ref[...] = delta
        nm_ref[...] = m
        nv_ref[...] = v
        for kind, pack_ref in enumerate((gs_ref, d_ref, nm_ref, nv_ref)):
            for k, width in enumerate(widths):
                rows[kind * nw + k][...] = pack_ref[k:k + 1, 0:width]

    vm = pl.BlockSpec(memory_space=pltpu.VMEM)
    ashape = jax.ShapeDtypeStruct((SMALL_ADAM_ROWS, d), F32)
    row_shapes = [jax.ShapeDtypeStruct((1, width), F32) for _ in range(4) for width in widths]
    flat_states = [a for rows_, ga, gx in states for a in (*rows_, ga, gx)]
    res = pl.pallas_call(
        body, name="small_update", in_specs=[pl.BlockSpec(memory_space=pltpu.SMEM)] + [vm] * (4 + 3 * n_state),
        out_specs=[vm] * (4 + 4 * nw),
        out_shape=[jax.ShapeDtypeStruct((r, d), F32), ashape, ashape, ashape] + row_shapes,
        scratch_shapes=[pltpu.VMEM((SMALL_ADAM_ROWS, d), F32)] * 3,
        compiler_params=pltpu.CompilerParams(vmem_limit_bytes=VMEM_LIMIT_BYTES))(
            me, early, own_early, late, own_late, *flat_states)
    return res[:4], [res[4 + kind * nw:4 + (kind + 1) * nw] for kind in range(4)]


SMALL_NAMES = ["ffn1_norm", "mix_norm", "ffn2_norm", "final_norm", "q_latent_norm", "kv_latent_norm",
               "q_head_norm", "k_head_norm", "conv_b", "gate_a_b", "gate_x_b", "lru_lambda", "attn_out_norm",
               "lru_out_norm"]
ROW_CONV_W = 14
ROW_GATE_A = 16
ROW_GATE_X = 48
ROW_META = 80
ROW_LOSS = 96


def _row(a):
    flat = a.reshape(1, -1)
    return jnp.pad(flat, ((0, 0), (0, D_MODEL - flat.shape[1])))


def _pack_small(t, rows):
    parts = [_row(t[nm]) for nm in SMALL_NAMES]
    parts.append(t["conv_w"].reshape(2, D_MODEL))
    parts.append(t["gate_a_w"].reshape(32, D_MODEL))
    parts.append(t["gate_x_w"].reshape(32, D_MODEL))
    p = jnp.concatenate(parts, axis=0)
    return jnp.pad(p, ((0, rows - p.shape[0]), (0, 0)))


def _early_pack(g):
    gs = {nm: g.get(nm, jnp.zeros((1, D_MODEL), F32)) for nm in SMALL_NAMES}
    gs["q_head_norm"] = g["q_head_norm"][:, 0:D_QK]
    gs["k_head_norm"] = g["k_head_norm"][:, 0:D_QK]
    for nm in ("conv_b", "gate_a_b", "gate_x_b", "lru_lambda"):
        gs[nm] = g[nm].reshape(1, LRU_W)
    gs["conv_w"] = g["conv_w"].transpose(1, 0, 2).reshape(CONV_K, LRU_W)
    gs["gate_a_w"] = _gate_blocks(g["gate_a_w"])
    gs["gate_x_w"] = _gate_blocks(g["gate_x_w"])
    return jnp.concatenate([_pack_small(gs, ROW_META), jnp.zeros((N_META, D_MODEL), F32), _row(g["loss"][:, 0:1]),
                            jnp.zeros((SMALL_ROWS - ROW_LOSS - 1, D_MODEL), F32)], axis=0)


def _unpack_small(p, rows, like):
    out = dict(zip(SMALL_NAMES, rows))
    out["gate_a_w"] = p[ROW_GATE_A:ROW_GATE_A + 32].reshape(like["gate_a_w"].shape)
    out["gate_x_w"] = p[ROW_GATE_X:ROW_GATE_X + 32].reshape(like["gate_x_w"].shape)
    return out


def _gate_dense(wg):
    w4 = wg[0].reshape(N_LRU_TILES, 2, 64, 64)
    zero = jnp.zeros((N_LRU_TILES, 64, 64), wg.dtype)
    top = jnp.concatenate([w4[:, 0], zero], axis=2)
    bot = jnp.concatenate([zero, w4[:, 1]], axis=2)
    return jnp.concatenate([top, bot], axis=1).astype(BF16)


def _gate_blocks(dw):
    return jnp.stack([dw[:, 0:64, 0:64], dw[:, 64:128, 64:128]], axis=1).reshape(8, 64, 64)


BIG_NAMES = ["ffn1_w_gate", "ffn1_w_up", "ffn1_w_down", "w_in", "w_uq", "w_uk", "w_uv", "w_out", "ffn2_w_gate",
             "ffn2_w_up", "ffn2_w_down"]
BIG_GROUPS = [["ffn1_w_gate", "ffn1_w_up", "ffn1_w_down", "ffn2_w_gate", "ffn2_w_up", "ffn2_w_down"], ["w_in"],
              ["w_uq"], ["w_uk", "w_uv"], ["w_out"]]
TRANSPOSED = ("ffn1_w_gate", "ffn1_w_up", "ffn2_w_gate", "ffn2_w_up", "w_in", "w_uq")


def _to2d(nm, a):
    return a[0].T if nm in TRANSPOSED else a[0]


def _from2d(nm, a):
    return (a.T if nm in TRANSPOSED else a)[None]


WEIGHT_NAMES = ["meta_tokens", "ffn1_norm", "ffn1_w_gate", "ffn1_w_up", "ffn1_w_down", "mix_norm", "w_in",
                "q_latent_norm", "w_uq", "kv_latent_norm", "w_uk", "w_uv", "q_head_norm", "k_head_norm", "conv_w",
                "conv_b", "gate_a_w", "gate_a_b", "gate_x_w", "gate_x_b", "lru_lambda", "attn_out_norm",
                "lru_out_norm", "w_out", "ffn2_norm", "ffn2_w_gate", "ffn2_w_up", "ffn2_w_down", "final_norm"]


def _weight_from(nm, slots):
    if nm == "w_in":
        win = slots.reshape(IN_WIDTH, D_MODEL)
        lru = win[Z_KR + D_ROPE:].reshape(2, N_LRU_TILES, LRU_TILE, D_MODEL).transpose(1, 0, 2, 3)
        return jnp.concatenate([win[0:Z_KR + D_ROPE], jnp.zeros((128 - D_ROPE, D_MODEL), BF16),
                                lru.reshape(2 * LRU_W, D_MODEL)], axis=0)
    if nm == "w_uq":
        return jnp.pad(slots, ((0, 0), (0, D_QKP - D_QK), (0, 0))).reshape(HEADS * D_QKP, Q_RANK)
    if nm in ("w_uk", "w_uv"):
        return slots.transpose(1, 0, 2).reshape(KV_RANK, HEADS * D_NOPE)
    if nm == "w_out":
        return slots.reshape(D_MODEL, D_MODEL)
    return slots


def _small_weights(p, small):
    w = {nm: p[nm] for nm in SMALL_NAMES}
    w["q_head_norm"] = jnp.pad(p["q_head_norm"], ((0, 0), (0, D_QKP - D_QK)))
    w["k_head_norm"] = jnp.pad(p["k_head_norm"], ((0, 0), (0, D_QKP - D_QK)))
    w["conv_w"] = small[:, N_META:N_META + 2, :].reshape(N_SHARD, CONV_K, LRU_TILE)
    w["gate_a_w"] = _gate_dense(p["gate_a_w"])
    w["gate_x_w"] = _gate_dense(p["gate_x_w"])
    meta = small[:, 0:N_META, :].transpose(1, 0, 2).reshape(N_META, D_MODEL)
    return w, meta


def _full_weights(p, gathered, small):
    w, meta = _small_weights(p, small)
    w.update({nm: _weight_from(nm, gathered[nm]) for nm in BIG_NAMES})
    return w, meta


def _shard_grad(nm, g):
    if nm == "w_in":
        lru = g[Z_MLA:].reshape(N_LRU_TILES, 2, LRU_TILE, D_MODEL).transpose(1, 0, 2, 3).reshape(2 * LRU_W, D_MODEL)
        return jnp.concatenate([g[0:Z_KR + D_ROPE], lru], axis=0).reshape(N_SHARD, IN_WIDTH // N_SHARD, D_MODEL)
    if nm == "w_uq":
        return g.reshape(HEADS, D_QKP, Q_RANK)[:, 0:D_QK, :]
    if nm in ("w_uk", "w_uv"):
        return g.reshape(KV_RANK, HEADS, D_NOPE).transpose(1, 0, 2)
    if nm == "w_out":
        return g.reshape(N_SHARD, D_MODEL // N_SHARD, D_MODEL)
    return g


def _shard_grads(g):
    return {nm: _shard_grad(nm, g[nm]) for nm in BIG_NAMES}


GATHER_AT = {"ffn1_norm": ["ffn1_w_gate", "ffn1_w_up"],
             "ffn1_up": ["ffn1_w_down", "w_in", "w_uq", "w_uk", "w_uv", "w_out"],
             "attn_fwd": ["ffn2_w_down", "ffn2_w_gate"], "lru_fwd": ["ffn2_w_up"]}
PAIR_AT = [("ffn2_din", ["ffn2_w_gate", "ffn2_w_up", "ffn2_w_down"]),
           ("mix_din", ["w_out", "w_uq", "w_uk", "w_uv", "w_in"]),
           ("ffn1_dwg", ["ffn1_w_down"]), ("ffn1_dwu", ["ffn1_w_gate"]), ("ffn1_din_a", ["ffn1_w_up"])]
CHIPS_AT = [("attn_bwd", ["ffn2_w_down", "ffn2_w_gate"]), ("mla_prep_bwd", ["ffn2_w_up"]),
            ("ffn1_dact", ["w_out", "w_uq", "w_uk", "w_uv", "w_in"]),
            ("ffn1_dwu", ["ffn1_w_down"]), ("ffn1_din_a", ["ffn1_w_gate"]), ("ffn1_din_b", ["ffn1_w_up"])]
SHARE_EARLY_GROUPS, SHARE_EARLY_AT = 3, "ffn1_dwd"
SMALL_EARLY_AT = "mix_din"


def _same_shape_groups(names):
    return [[nm for nm in grp if nm in names] for grp in BIG_GROUPS if any(nm in names for nm in grp)]


class _Sched:
    def __init__(self, place, w, slots):
        self.place, self.w, self.slots = place, w, slots
        self.g = None
        self.sharded, self.from_pair, self.chip_bf16, self.from_chips = {}, {}, {}, {}
        self.early = self.early_all = None
        self.shared = {}

    def host(self, stage):
        comms = []
        if stage in GATHER_AT:
            comms.append(self.gather(GATHER_AT[stage]))
        comms += [self.chips(names) for at, names in CHIPS_AT if at == stage]
        comms += [self.pair(names) for at, names in PAIR_AT if at == stage]
        if stage == SMALL_EARLY_AT:
            comms.append(self.small_early())
        if stage == SHARE_EARLY_AT:
            comms.append(self.share_early())
        return _join_comms(comms)

    def small_early(self):
        self.early = _early_pack(self.g)

        def deliver(outs):
            self.early_all = outs[0]
            return outs

        return _small_comm(self.early, deliver)

    def gather(self, names):
        def deliver(outs):
            self.w.update({nm: _weight_from(nm, o) for nm, o in zip(names, outs)})
            return outs

        return _gather_comm([self.slots[nm] for nm in names], deliver)

    def pair(self, names):
        self.sharded.update({nm: _shard_grad(nm, self.g[nm]) for nm in names})

        def deliver(outs):
            self.from_pair.update(zip(names, outs))
            for grp in _same_shape_groups(names):
                sums = _pair_sum_call("pair_sum_" + grp[0], self.place, [self.sharded[nm] for nm in grp],
                                      [self.from_pair[nm] for nm in grp])
                self.chip_bf16.update(zip(grp, sums))
            return outs

        return _reduce_pair_comm([self.sharded[nm] for nm in names], deliver)

    def chips(self, names):
        def deliver(outs):
            self.from_chips.update(zip(names, outs))
            return outs

        return _reduce_chips_comm([self.chip_bf16[nm] for nm in names], deliver)

    def chip_sums(self, names, comm=None):
        out = {}
        for grp in _same_shape_groups(names):
            sums = _chip_sum_call("chip_sum_" + grp[0], self.place, [self.sharded[nm] for nm in grp],
                                  [self.from_pair[nm] for nm in grp], [self.from_chips[nm] for nm in grp], comm)
            comm = None
            out.update(zip(grp, sums))
        return out

    def share_early(self):
        names = [nm for at, grp in CHIPS_AT[:SHARE_EARLY_GROUPS] for nm in grp]
        mine = self.chip_sums(names)
        return _share_pair_comm([mine[nm] for nm in names], lambda o: self.shared.update(zip(names, o)))


def kernel(x, meta_tokens, ffn1_norm, ffn1_w_gate, ffn1_w_up, ffn1_w_down, mix_norm, w_in, q_latent_norm, w_uq, kv_latent_norm, w_uk, w_uv, q_head_norm, k_head_norm, conv_w, conv_b, gate_a_w, gate_a_b, gate_x_w, gate_x_b, lru_lambda, attn_out_norm, lru_out_norm, w_out, ffn2_norm, ffn2_w_gate, ffn2_w_up, ffn2_w_down, final_norm, loss_target, m_meta_tokens, m_ffn1_norm, m_ffn1_w_gate, m_ffn1_w_up, m_ffn1_w_down, m_mix_norm, m_w_in, m_q_latent_norm, m_w_uq, m_kv_latent_norm, m_w_uk, m_w_uv, m_q_head_norm, m_k_head_norm, m_conv_w, m_conv_b, m_gate_a_w, m_gate_a_b, m_gate_x_w, m_gate_x_b, m_lru_lambda, m_attn_out_norm, m_lru_out_norm, m_w_out, m_ffn2_norm, m_ffn2_w_gate, m_ffn2_w_up, m_ffn2_w_down, m_final_norm, v_meta_tokens, v_ffn1_norm, v_ffn1_w_gate, v_ffn1_w_up, v_ffn1_w_down, v_mix_norm, v_w_in, v_q_latent_norm, v_w_uq, v_kv_latent_norm, v_w_uk, v_w_uv, v_q_head_norm, v_k_head_norm, v_conv_w, v_conv_b, v_gate_a_w, v_gate_a_b, v_gate_x_w, v_gate_x_b, v_lru_lambda, v_attn_out_norm, v_lru_out_norm, v_w_out, v_ffn2_norm, v_ffn2_w_gate, v_ffn2_w_up, v_ffn2_w_down, v_final_norm):
    args = locals()
    p = {nm: args[nm] for nm in WEIGHT_NAMES}
    mom = {nm: args["m_" + nm] for nm in WEIGHT_NAMES}
    var = {nm: args["v_" + nm] for nm in WEIGHT_NAMES}
    nb, seq, d = x.shape
    lp = CHUNK + seq
    xi, yi, ci = lax.axis_index("x"), lax.axis_index("y"), lax.axis_index("c")
    chip = 2 * xi + yi

    place = jnp.stack([chip, ci]).astype(jnp.int32)
    p2 = {nm: _to2d(nm, p[nm]) for nm in BIG_NAMES}
    m2 = {nm: _to2d(nm, mom[nm]) for nm in BIG_NAMES}
    v2 = {nm: _to2d(nm, var[nm]) for nm in BIG_NAMES}

    slots = {}
    small_shard = jnp.concatenate(
        [meta_tokens, conv_w[0].reshape(2, 2 * LRU_TILE), jnp.zeros((14, 2 * LRU_TILE), F32)], axis=0)
    small_slots = lax.dynamic_update_slice(jnp.zeros((N_SHARD,) + small_shard.shape, F32), small_shard[None],
                                           (chip, 0, 0))
    first = []
    comm = _gather_comm([small_slots], first.extend)
    for grp in BIG_GROUPS:
        for nm, buf in zip(grp, _cast_call("cast_" + grp[0], place, [p2[nm] for nm in grp], comm)):
            slots[nm] = buf
        comm = None
    w, meta = _small_weights(p, first[0])
    sched = _Sched(place, w, slots)

    h0 = jnp.concatenate(
        [jnp.zeros((nb, PAD_ROWS, d), F32), jnp.broadcast_to(meta[None], (nb, N_META, d)), x], axis=1)
    target = jnp.pad(loss_target, ((0, 0), (CHUNK, 0), (0, 0)))
    loss_part, dh0, g = _local_step(h0.reshape(nb * lp, d), target.reshape(nb * lp, d), w, nb, lp, sched)
    dh0 = dh0.reshape(nb, lp, d)
    grad_x = dh0[:, CHUNK:, :]

    late = jnp.concatenate([g["ffn1_norm"], g["mix_norm"], jnp.zeros((6, D_MODEL), F32),
                            jnp.sum(dh0[:, PAD_ROWS:CHUNK, :], axis=0)], axis=0)
    shared = sched.shared
    rest = [nm for at, names in CHIPS_AT[SHARE_EARLY_GROUPS:] if at is not None for nm in names]
    last = [nm for at, names in CHIPS_AT if at is None for nm in names]
    late_box = {}
    mine = sched.chip_sums(rest, _small_comm(late, lambda o: late_box.update(all=o[0])))
    share = _share_pair_comm([mine[nm] for nm in rest], lambda o: shared.update(zip(rest, o)))
    _comm_call("share_pair", _join_comms([share, sched.chips(last) if last else None]))
    if last:
        mine = sched.chip_sums(last)
        _comm_call("share_last", _share_pair_comm([mine[nm] for nm in last], lambda o: shared.update(zip(last, o))))
    late_all = late_box["all"]
    small_like = {nm: p[nm] for nm in SMALL_NAMES + ["gate_a_w", "gate_x_w"]}

    def state(t):
        return ([t[nm] for nm in SMALL_NAMES], t["gate_a_w"].reshape(32, D_MODEL), t["gate_x_w"].reshape(32, D_MODEL))

    me = (4 * xi + 2 * yi + ci).astype(jnp.int32).reshape(1)
    (gsum, dsm, msm, vsm), rows = _small_update_call(me, sched.early_all, sched.early, late_all, late,
                                                     [state(p), state(mom), state(var)])
    grads = _unpack_small(gsum, rows[0], small_like)
    delta = _unpack_small(dsm, rows[1], small_like)
    new_m = _unpack_small(msm, rows[2], small_like)
    new_v = _unpack_small(vsm, rows[3], small_like)
    loss = gsum[ROW_LOSS, 0]
    gmeta = gsum[ROW_META:ROW_META + N_META].reshape(N_META, N_SHARD, D_MODEL // N_SHARD)
    grads["meta_tokens"] = lax.dynamic_index_in_dim(gmeta, chip, axis=1, keepdims=False)
    gconv = gsum[ROW_CONV_W:ROW_CONV_W + 2].reshape(CONV_K, N_SHARD, LRU_TILE)
    grads["conv_w"] = lax.dynamic_index_in_dim(gconv, chip, axis=1, keepdims=False)[None]
    for nm in ("meta_tokens", "conv_w"):
        delta[nm], new_m[nm], new_v[nm] = _adamw_call("adamw_" + nm, p[nm], grads[nm], mom[nm], var[nm])

    for names in BIG_GROUPS:
        res = _adamw_group_call("adamw_" + names[0], [p2[nm] for nm in names], [shared[nm] for nm in names],
                                [m2[nm] for nm in names], [v2[nm] for nm in names])
        for nm, (gg, dd, mm, vv) in zip(names, res):
            grads[nm], delta[nm], new_m[nm], new_v[nm] = (_from2d(nm, t) for t in (gg, dd, mm, vv))

    return (loss, grad_x, *[grads[nm] for nm in WEIGHT_NAMES], *[delta[nm] for nm in WEIGHT_NAMES],
            *[new_m[nm] for nm in WEIGHT_NAMES], *[new_v[nm] for nm in WEIGHT_NAMES])
```

```python
import functools
import math

import jax
import jax.numpy as jnp
import numpy as np
from jax import lax
from jax.experimental import pallas as pl
from jax.experimental.pallas import tpu as pltpu

F32 = jnp.float32
BF16 = jnp.bfloat16
MESH = pl.DeviceIdType.MESH

D_MODEL = 1024
N_META = 16
CHUNK = 64
PAD_ROWS = CHUNK - N_META
HEADS = 4
D_NOPE = 128
D_ROPE = 64
D_QK = D_NOPE + D_ROPE
D_QKP = 256
D_V = 128
KV_RANK = 256
Q_RANK = 384
MLA_W = HEADS * D_V
LRU_W = 512
LRU_TILE = 128
N_LRU_TILES = LRU_W // LRU_TILE
CONV_K = 4
C_RGLRU = 8.0
ROPE_THETA = 10000.0
D_FF = 2816
N_SHARD = 4
EPS = 1e-6
NEG_INF = -1e30
Z_KR = Q_RANK + KV_RANK
Z_MLA = Z_KR + 128
Z_U = Z_MLA
Z_G = Z_U + LRU_W
Z_W = Z_G + LRU_W
IN_WIDTH = Q_RANK + KV_RANK + D_ROPE + 2 * LRU_W

ADAM_LR = 0.001
ADAM_B1 = 0.9
ADAM_B2 = 0.999
ADAM_EPS = 1e-08
ADAM_WD = 0.01
ADAM_STEP = 10

VMEM_LIMIT_BYTES = 56 * 1024 * 1024
MATMUL_ROWS = 1408
ELEMENTWISE_ROWS = 512
PREP_BWD_ROWS = 192
SMALL_ROWS = 104
SMALL_ADAM_ROWS = 80


def _params(sem):
    return pltpu.CompilerParams(dimension_semantics=sem, vmem_limit_bytes=VMEM_LIMIT_BYTES)


def _resident(shape):
    return pl.BlockSpec(tuple(shape), lambda i: (0,) * len(shape), pipeline_mode=pl.Buffered(1))


def _row_tile(rows, target):
    best = 16
    for t in range(16, min(rows, target) + 1, 16):
        if rows % t == 0:
            best = t
    return best


def _col_tile(cols, target):
    best = cols
    for t in range(128, min(cols, target) + 1, 128):
        if cols % t == 0:
            best = t
    return best


def _dot(a, b):
    return jnp.dot(a, b, preferred_element_type=F32)


def _dot_nt(a, b):
    return lax.dot_general(a, b, (((1,), (1,)), ((), ())), preferred_element_type=F32)


def _dot_tn(a, b):
    return lax.dot_general(a, b, (((0,), (0,)), ((), ())), preferred_element_type=F32)


def _rms(x, n):
    return lax.rsqrt(jnp.sum(x * x, axis=-1, keepdims=True) * (1.0 / n) + EPS)


def _rms_bwd(dn, nrm, r, n):
    return r * (dn - nrm * (jnp.sum(dn * nrm, axis=-1, keepdims=True) * (1.0 / n)))


def _gelu(x):
    k = math.sqrt(2.0 / math.pi)
    t = jnp.tanh(k * (x + 0.044715 * x * x * x))
    return 0.5 * x * (1.0 + t), t


def _gelu_grad(x, t):
    k = math.sqrt(2.0 / math.pi)
    return 0.5 * (1.0 + t) + 0.5 * x * (1.0 - t * t) * k * (1.0 + 3.0 * 0.044715 * x * x)


def _sigmoid(x):
    return 0.5 + 0.5 * jnp.tanh(0.5 * x)


def _softplus_neg(lam):
    e = jnp.exp(-jnp.abs(lam))
    log1p = jnp.where(e < 0.01, e * (1.0 - e * (0.5 - e * (1.0 / 3 - e * 0.25))), jnp.log(1.0 + e))
    return jnp.maximum(-lam, 0.0) + log1p


def _rope(t, c, s1, s2):
    return t * c + pltpu.roll(t, 96, 1) * s1 + pltpu.roll(t, 32, 1) * s2


def _rope_t(d, c, s1, s2):
    return d * c + pltpu.roll(d * s1, 32, 1) + pltpu.roll(d * s2, 96, 1)


def _rope_tables(lp):
    pos = (np.arange(lp, dtype=np.int32) - PAD_ROWS).astype(np.float32)
    inv_freq = (ROPE_THETA ** (-np.arange(0, D_ROPE // 2, dtype=np.float32) / (D_ROPE // 2))).astype(np.float32)
    ang = (pos[:, None] * inv_freq[None, :]).astype(np.float32).astype(np.float64)
    cos, sin = np.cos(ang).astype(np.float32), np.sin(ang).astype(np.float32)
    z = np.zeros_like(cos)
    return (jnp.asarray(np.concatenate([cos, cos, z, z], 1)), jnp.asarray(np.concatenate([-sin, z, z, z], 1)),
            jnp.asarray(np.concatenate([z, sin, z, z], 1)))


def _rmsnorm_call(name, h, g, tm, comm=None):
    rows, d = h.shape

    def body(h_ref, g_ref, o_ref):
        x = h_ref[...]
        o_ref[...] = (x * _rms(x, d) * g_ref[...]).astype(BF16)

    return _hosted_call(
        body, name=name, grid=(rows // tm,),
        in_specs=[pl.BlockSpec((tm, d), lambda i: (i, 0)), pl.BlockSpec((1, d), lambda i: (0, 0))],
        out_specs=[pl.BlockSpec((tm, d), lambda i: (i, 0))],
        out_shape=[jax.ShapeDtypeStruct((rows, d), BF16)],
        dims=("parallel",), args=(h, g), comm=comm)[0]


def _ffn_up_call(name, u, wg, wu, tm, comm=None):
    rows, d = u.shape
    ns, fs, _ = wg.shape

    def body(u_ref, wg_ref, wu_ref, g_ref, p_ref, a_ref):
        uu = u_ref[...]
        g = _dot_nt(uu, wg_ref[0])
        p = _dot_nt(uu, wu_ref[0])
        g_ref[0] = g.astype(BF16)
        p_ref[0] = p.astype(BF16)
        a_ref[0] = (g * jax.nn.sigmoid(g) * p).astype(BF16)

    wspec = pl.BlockSpec((1, fs, d), lambda s, i: (s, 0, 0))
    ospec = pl.BlockSpec((1, tm, fs), lambda s, i: (s, i, 0))
    oshape = jax.ShapeDtypeStruct((ns, rows, fs), BF16)
    return _hosted_call(
        body, name=name, grid=(ns, rows // tm),
        in_specs=[pl.BlockSpec((tm, d), lambda s, i: (i, 0)), wspec, wspec],
        out_specs=[ospec, ospec, ospec], out_shape=[oshape, oshape, oshape],
        dims=("parallel", "parallel"), args=(u, wg, wu), comm=comm)


def _loss_tail(x, g, t, row, d):
    r = _rms(x, d)
    n = x * r
    err = jnp.where(row >= CHUNK, n * g - t, 0.0)
    dout = err * (1.0 / d)
    dh = _rms_bwd(dout * g, n, r, d)
    dg = jnp.sum(dout * n, axis=0, keepdims=True)
    part = jnp.sum(jnp.sum(err * err, axis=1, keepdims=True), axis=0, keepdims=True) * (0.5 / d)
    return dh, dg, jnp.broadcast_to(part, (1, 128))


def _ffn_down_call(name, a, wd, h, tm, comm=None, next_gain=None):
    rows, d = h.shape
    ns, _, fs = a.shape
    more = next_gain is not None

    def body(a_ref, wd_ref, h_ref, *rest):
        acc = h_ref[...]
        for s in range(ns):
            acc = acc + 0.5 * _dot(a_ref[s], wd_ref[s])
        rest[-2 if more else -1][...] = acc
        if more:
            rest[-1][...] = (acc * _rms(acc, d) * rest[0][...]).astype(BF16)

    full = pl.BlockSpec((tm, d), lambda i: (i, 0))
    res = _hosted_call(
        body, name=name, grid=(rows // tm,),
        in_specs=[pl.BlockSpec((ns, tm, fs), lambda i: (0, i, 0)), _resident((ns, fs, d)), full]
        + ([pl.BlockSpec((1, d), lambda i: (0, 0))] if more else []),
        out_specs=[full] * (2 if more else 1),
        out_shape=[jax.ShapeDtypeStruct((rows, d), F32)] + ([jax.ShapeDtypeStruct((rows, d), BF16)] if more else []),
        dims=("parallel",), args=(a, wd, h) + ((next_gain,) if more else ()), comm=comm)
    return res if more else res[0]


def _ffn_down_loss_call(name, a, wd, h, g, target, lp, tm):
    rows, d = h.shape
    ns, _, fs = a.shape
    tpe = lp // tm

    def body(a_ref, wd_ref, h_ref, g_ref, t_ref, dh_ref, dhb_ref, dg_ref, loss_ref):
        i = pl.program_id(0)
        acc = h_ref[...]
        for s in range(ns):
            acc = acc + 0.5 * _dot(a_ref[s], wd_ref[s])
        row = (i % tpe) * tm + lax.broadcasted_iota(jnp.int32, (tm, 1), 0)
        dh, dg, loss = _loss_tail(acc, g_ref[...], t_ref[...], row, d)
        dh_ref[...] = dh
        dhb_ref[...] = dh.astype(BF16)

        @pl.when(i == 0)
        def _():
            dg_ref[...] = dg
            loss_ref[...] = loss

        @pl.when(i != 0)
        def _():
            dg_ref[...] += dg
            loss_ref[...] += loss

    full = pl.BlockSpec((tm, d), lambda i: (i, 0))
    gspec = pl.BlockSpec((1, d), lambda i: (0, 0))
    return _hosted_call(
        body, name=name, grid=(rows // tm,),
        in_specs=[pl.BlockSpec((ns, tm, fs), lambda i: (0, i, 0)), _resident((ns, fs, d)), full, gspec, full],
        out_specs=[full, full, gspec, pl.BlockSpec((1, 128), lambda i: (0, 0))],
        out_shape=[jax.ShapeDtypeStruct((rows, d), F32), jax.ShapeDtypeStruct((rows, d), BF16),
                   jax.ShapeDtypeStruct((1, d), F32), jax.ShapeDtypeStruct((1, 128), F32)],
        args=(a, wd, h, g, target))


def _mm_call(name, a, bt, tm, out_dtype):
    rows, k = a.shape
    n = bt.shape[0]

    def body(a_ref, b_ref, o_ref):
        o_ref[...] = _dot_nt(a_ref[...], b_ref[...]).astype(out_dtype)

    return pl.pallas_call(
        body, name=name, grid=(rows // tm,),
        in_specs=[pl.BlockSpec((tm, k), lambda i: (i, 0)), pl.BlockSpec((n, k), lambda i: (0, 0))],
        out_specs=pl.BlockSpec((tm, n), lambda i: (i, 0)),
        out_shape=jax.ShapeDtypeStruct((rows, n), out_dtype),
        compiler_params=_params(("parallel",)))(a, bt)


def _mla_heads(z, gql, gkvl, wuq, wuk, wuv):
    cq = z[:, 0:Q_RANK]
    ckv = z[:, Q_RANK:Z_KR]
    kr = z[:, Z_KR:Z_MLA]
    rq = _rms(cq, Q_RANK)
    nq = cq * rq
    cqn = (nq * gql).astype(BF16)
    rkv = _rms(ckv, KV_RANK)
    nkv = ckv * rkv
    ckvn = (nkv * gkvl).astype(BF16)
    qraw = _dot_nt(cqn, wuq)
    knope = _dot(ckvn, wuk)
    v = _dot(ckvn, wuv)
    skr = jnp.sum(kr * kr, axis=-1, keepdims=True)
    heads = []
    for hd in range(HEADS):
        qh = qraw[:, hd * D_QKP:(hd + 1) * D_QKP]
        rqh = lax.rsqrt(jnp.sum(qh * qh, axis=-1, keepdims=True) * (1.0 / D_QK) + EPS)
        kn = knope[:, hd * D_NOPE:(hd + 1) * D_NOPE]
        rkh = lax.rsqrt((jnp.sum(kn * kn, axis=-1, keepdims=True) + skr) * (1.0 / D_QK) + EPS)
        heads.append((qh * rqh, rqh, kn * rkh, kr * rkh, rkh))
    return dict(rq=rq, nq=nq, cqn=cqn, rkv=rkv, nkv=nkv, ckvn=ckvn, v=v, heads=heads)


def _mla_prep_call(z, gql, gkvl, gqh, gkh, wuq, wuk, wuv, tabs, lp, tm):
    rows = z.shape[0]
    tpe = lp // tm

    def body(z_ref, gql_ref, gkvl_ref, gqh_ref, gkh_ref, wuq_ref, wuk_ref, wuv_ref, c_ref, s1_ref, s2_ref,
             q_ref, k_ref, v_ref):
        m = _mla_heads(z_ref[...], gql_ref[...], gkvl_ref[...], wuq_ref[...], wuk_ref[...], wuv_ref[...])
        c, s1, s2 = c_ref[...], s1_ref[...], s2_ref[...]
        gq, gk = gqh_ref[...], gkh_ref[...]
        row = (pl.program_id(0) % tpe) * tm + lax.broadcasted_iota(jnp.int32, (tm, 1), 0)
        spare = (lax.broadcasted_iota(jnp.int32, (1, D_QKP - D_NOPE), 1) == D_ROPE).astype(F32)
        kmask = jnp.where(row < PAD_ROWS, NEG_INF * math.sqrt(D_QK), 0.0) * spare
        for hd in range(HEADS):
            qn, _, knn, krn, _ = m["heads"][hd]
            qg = qn * gq
            q_ref[hd, :, 0:D_NOPE] = qg[:, 0:D_NOPE].astype(BF16)
            q_ref[hd, :, D_NOPE:D_QKP] = (_rope(qg[:, D_NOPE:D_QKP], c, s1, s2) + spare).astype(BF16)
            k_ref[hd, :, 0:D_NOPE] = (knn * gk[:, 0:D_NOPE]).astype(BF16)
            k_ref[hd, :, D_NOPE:D_QKP] = (_rope(krn * gk[:, D_NOPE:D_QKP], c, s1, s2) + kmask).astype(BF16)
            v_ref[hd] = m["v"][:, hd * D_V:(hd + 1) * D_V].astype(BF16)

    def const(shape):
        return pl.BlockSpec(shape, lambda i: tuple(0 for _ in shape))

    tab = pl.BlockSpec((tm, 128), lambda i: (i % tpe, 0))
    return pl.pallas_call(
        body, name="mla_prep", grid=(rows // tm,),
        in_specs=[pl.BlockSpec((tm, Z_MLA), lambda i: (i, 0)), const((1, Q_RANK)), const((1, KV_RANK)),
                  const((1, D_QKP)), const((1, D_QKP)), const((HEADS * D_QKP, Q_RANK)),
                  const((KV_RANK, HEADS * D_NOPE)), const((KV_RANK, HEADS * D_V)), tab, tab, tab],
        out_specs=[pl.BlockSpec((HEADS, tm, D_QKP), lambda i: (0, i, 0)),
                   pl.BlockSpec((HEADS, tm, D_QKP), lambda i: (0, i, 0)),
                   pl.BlockSpec((HEADS, tm, D_V), lambda i: (0, i, 0))],
        out_shape=[jax.ShapeDtypeStruct((HEADS, rows, D_QKP), BF16),
                   jax.ShapeDtypeStruct((HEADS, rows, D_QKP), BF16),
                   jax.ShapeDtypeStruct((HEADS, rows, D_V), BF16)],
        compiler_params=_params(("parallel",)))(z, gql, gkvl, gqh, gkh, wuq, wuk, wuv, *tabs)


Q_BLOCK_ROWS = 528
Q_BLOCK_ROWS_BWD = 192


def _q_block(lp):
    return _row_tile(lp, Q_BLOCK_ROWS)


def _key_end(ext, lp):
    return min(lp, -(-ext // CHUNK) * CHUNK)


def _diag_bias(qb, j0, nk):
    shift = CHUNK.bit_length() - 1
    r = jnp.right_shift(j0 + lax.broadcasted_iota(jnp.int32, (qb, nk), 0), shift)
    c = jnp.right_shift(j0 + lax.broadcasted_iota(jnp.int32, (qb, nk), 1), shift)
    return jnp.where(c <= r, 0.0, NEG_INF)


def _attn_fwd_call(q, k, v, nb, lp, comm=None):
    rows = nb * lp
    qb = _q_block(lp)
    scale = 1.0 / math.sqrt(D_QK)

    def body(q_ref, k_ref, v_ref, o_ref, lse_ref):
        for j in range(lp // qb):
            j0, ext = j * qb, (j + 1) * qb
            kend = _key_end(ext, lp)
            qj = q_ref[0, j0:ext, :]
            sd = _dot_nt(qj, k_ref[0, j0:kend, :]) * scale + _diag_bias(qb, j0, kend - j0)
            mx = jnp.max(sd, axis=-1, keepdims=True)
            if j > 0:
                so = _dot_nt(qj, k_ref[0, 0:j0, :]) * scale
                mx = jnp.maximum(mx, jnp.max(so, axis=-1, keepdims=True))
            pd = jnp.exp(sd - mx)
            l = jnp.sum(pd, axis=-1, keepdims=True)
            o = _dot(pd.astype(BF16), v_ref[0, j0:kend, :])
            if j > 0:
                po = jnp.exp(so - mx)
                l = l + jnp.sum(po, axis=-1, keepdims=True)
                o = o + _dot(po.astype(BF16), v_ref[0, 0:j0, :])
            o_ref[j0:ext, :] = o / l
            lse_ref[0, j0:ext, :] = mx + jnp.log(l)

    return _hosted_call(
        body, name="attn_fwd", grid=(nb, HEADS),
        in_specs=[pl.BlockSpec((1, lp, D_QKP), lambda b, h: (h, b, 0)),
                  pl.BlockSpec((1, lp, D_QKP), lambda b, h: (h, b, 0)),
                  pl.BlockSpec((1, lp, D_V), lambda b, h: (h, b, 0))],
        out_specs=[pl.BlockSpec((lp, D_V), lambda b, h: (b, h)),
                   pl.BlockSpec((1, lp, 1), lambda b, h: (h, b, 0))],
        out_shape=[jax.ShapeDtypeStruct((rows, MLA_W), F32),
                   jax.ShapeDtypeStruct((HEADS, rows, 1), F32)],
        dims=("parallel", "parallel"), args=(q, k, v), comm=comm)


def _attn_bwd_call(q, k, v, o, lse, do, nb, lp, comm=None):
    rows = nb * lp
    qb = _row_tile(lp, Q_BLOCK_ROWS_BWD)
    scale = 1.0 / math.sqrt(D_QK)

    def body(q_ref, k_ref, v_ref, o_ref, lse_ref, do_ref, dq_ref, dk_ref, dv_ref, dk_acc, dv_acc):
        dk_acc[...] = jnp.zeros_like(dk_acc)
        dv_acc[...] = jnp.zeros_like(dv_acc)
        shift = CHUNK.bit_length() - 1
        for j in range(lp // qb):
            j0, ext = j * qb, (j + 1) * qb
            kend = _key_end(ext, lp)
            qj = q_ref[0, j0:ext, :]
            doj = do_ref[j0:ext, :]
            delta = jnp.sum(doj * o_ref[j0:ext, :], axis=-1, keepdims=True)
            dob = doj.astype(BF16)
            kk = k_ref[0, 0:kend, :]
            qchunk = jnp.right_shift(j0 + lax.broadcasted_iota(jnp.int32, (qb, kend), 0), shift)
            kchunk = jnp.right_shift(lax.broadcasted_iota(jnp.int32, (qb, kend), 1), shift)
            s = _dot_nt(qj, kk) * scale - lse_ref[0, j0:ext, :]
            p = jnp.where(kchunk <= qchunk, jnp.exp(s), 0.0)
            dv_acc[0:kend, :] += _dot_tn(p.astype(BF16), dob)
            dp = _dot_nt(dob, v_ref[0, 0:kend, :])
            ds = (p * (dp - delta) * scale).astype(BF16)
            dq_ref[0, j0:ext, :] = _dot(ds, kk).astype(BF16)
            dk_acc[0:kend, :] += _dot_tn(ds, qj)
        dk_ref[0] = dk_acc[...].astype(BF16)
        dv_ref[0] = dv_acc[...].astype(BF16)

    qspec = pl.BlockSpec((1, lp, D_QKP), lambda b, h: (h, b, 0))
    vspec = pl.BlockSpec((1, lp, D_V), lambda b, h: (h, b, 0))
    ospec = pl.BlockSpec((lp, D_V), lambda b, h: (b, h))
    return _hosted_call(
        body, name="attn_bwd", grid=(nb, HEADS),
        in_specs=[qspec, qspec, vspec, ospec, pl.BlockSpec((1, lp, 1), lambda b, h: (h, b, 0)), ospec],
        out_specs=[qspec, qspec, vspec],
        out_shape=[jax.ShapeDtypeStruct((HEADS, rows, D_QKP), BF16),
                   jax.ShapeDtypeStruct((HEADS, rows, D_QKP), BF16),
                   jax.ShapeDtypeStruct((HEADS, rows, D_V), BF16)],
        scratch_shapes=[pltpu.VMEM((lp, D_QKP), F32), pltpu.VMEM((lp, D_V), F32)],
        dims=("parallel", "parallel"), args=(q, k, v, o, lse, do), comm=comm)


def _lru_gates(u, cw, cb, wa, ba, wx, bx, lam, lp):
    xc = (cw[3:4, :] * u + cw[2:3, :] * pltpu.roll(u, 1, 0) + cw[1:2, :] * pltpu.roll(u, 2, 0)
          + cw[0:1, :] * pltpu.roll(u, 3, 0) + cb)
    xcb = xc.astype(BF16)
    r = _sigmoid(_dot(xcb, wa) + ba)
    i = _sigmoid(_dot(xcb, wx) + bx)
    sp = _softplus_neg(lam)
    la = -C_RGLRU * r * sp
    a = jnp.exp(la)
    x2 = 2.0 * la
    e2 = a * a
    m2 = jnp.maximum(jnp.where(x2 > -0.01, -x2 * (1.0 + 0.5 * x2), 1.0 - e2), 1e-30)
    rs = lax.rsqrt(m2)
    row = lax.broadcasted_iota(jnp.int32, (lp, LRU_TILE), 0)
    first = row == PAD_ROWS
    valid = row >= PAD_ROWS
    mult_eff = jnp.where(first, 1.0, m2 * rs)
    return dict(xc=xc, xcb=xcb, r=r, i=i, sp=sp, a=a, e2=e2, rs=rs, mult_eff=mult_eff, first=first, valid=valid)


def _scan_rows(a, b, a_s, b_s, out_ref, lp, reverse):
    sub = lax.broadcasted_iota(jnp.int32, (lp, LRU_TILE), 0) & 7
    for dist in (1, 2, 4):
        shift = lp - dist if reverse else dist
        keep = (sub + dist <= 7) if reverse else (sub >= dist)
        a_sh = pltpu.roll(a, shift, 0)
        b_sh = pltpu.roll(b, shift, 0)
        b = jnp.where(keep, a * b_sh + b, b)
        a = jnp.where(keep, a * a_sh, a)
    a_s[...] = a
    b_s[...] = b
    n_groups = lp // 8
    edge = 0 if reverse else 7

    def group(gi, carry):
        r0 = pl.multiple_of(((n_groups - 1 - gi) if reverse else gi) * 8, 8)
        a8 = a_s[pl.ds(r0, 8), :]
        b8 = b_s[pl.ds(r0, 8), :]
        out_ref[pl.ds(r0, 8), :] = a8 * carry + b8
        return a8[edge:edge + 1, :] * carry + b8[edge:edge + 1, :]

    lax.fori_loop(0, n_groups, group, jnp.zeros((1, LRU_TILE), F32), unroll=4)


def _lru_specs(lp):
    seq = lambda col0, stride=1: pl.BlockSpec((lp, LRU_TILE), lambda t, b: (b, col0 + stride * t))
    cw = pl.BlockSpec((1, CONV_K, LRU_TILE), lambda t, b: (t, 0, 0))
    vec = pl.BlockSpec((1, LRU_TILE), lambda t, b: (0, t))
    mat = pl.BlockSpec((1, LRU_TILE, LRU_TILE), lambda t, b: (t, 0, 0))
    return seq, cw, vec, mat


def _lru_fwd_call(z, cw, cb, wa, ba, wx, bx, lam, nb, lp, comm=None):
    rows = nb * lp
    seq, cwspec, vec, mat = _lru_specs(lp)

    def body(u_ref, g_ref, cw_ref, cb_ref, wa_ref, ba_ref, wx_ref, bx_ref, lam_ref, y_ref, hs_ref, a_s, b_s):
        m = _lru_gates(u_ref[...], cw_ref[0], cb_ref[...], wa_ref[0], ba_ref[...], wx_ref[0], bx_ref[...],
                       lam_ref[...], lp)
        a = jnp.where(m["valid"], m["a"], 0.0)
        b = jnp.where(m["valid"], m["mult_eff"] * (m["i"] * m["xc"]), 0.0)
        _scan_rows(a, b, a_s, b_s, hs_ref, lp, reverse=False)
        gl, _ = _gelu(g_ref[...])
        y_ref[...] = hs_ref[...] * gl

    oshape = jax.ShapeDtypeStruct((rows, LRU_W), F32)
    return _hosted_call(
        body, name="lru_fwd", grid=(N_LRU_TILES, nb),
        in_specs=[seq(Z_U // LRU_TILE, 2), seq(Z_U // LRU_TILE + 1, 2), cwspec, vec, mat, vec, mat, vec, vec],
        out_specs=[seq(0), seq(0)], out_shape=[oshape, oshape],
        scratch_shapes=[pltpu.VMEM((lp, LRU_TILE), F32), pltpu.VMEM((lp, LRU_TILE), F32)],
        dims=("parallel", "parallel"), args=(z, z, cw, cb, wa, ba, wx, bx, lam), comm=comm)


def _lru_bwd_call(z, dz, hs, dy, cw, cb, wa, ba, wx, bx, lam, nb, lp, comm=None):
    rows = nb * lp
    seq, cwspec, vec, mat = _lru_specs(lp)

    def body(u_ref, g_ref, hs_ref, dy_ref, cw_ref, cb_ref, wa_ref, ba_ref, wx_ref, bx_ref, lam_ref, dz_in,
             dz_ref, dcw_ref, dcb_ref, dwa_ref, dba_ref, dwx_ref, dbx_ref, dlam_ref, a_s, b_s, d_s):
        du_ref = dz_ref.at[:, 0:LRU_TILE]
        dg_ref = dz_ref.at[:, LRU_TILE:2 * LRU_TILE]
        b_idx = pl.program_id(1)
        u = u_ref[...]
        cw = cw_ref[0]
        wa, wx = wa_ref[0], wx_ref[0]
        lam = lam_ref[...]
        m = _lru_gates(u, cw, cb_ref[...], wa, ba_ref[...], wx, bx_ref[...], lam, lp)
        gate = g_ref[...]
        gl, th = _gelu(gate)
        dy = dy_ref[...]
        hs = hs_ref[...]
        dg_ref[...] = (dy * hs * _gelu_grad(gate, th)).astype(BF16)
        a_eff = jnp.where(m["valid"], m["a"], 0.0)
        _scan_rows(pltpu.roll(a_eff, lp - 1, 0), dy * gl, a_s, b_s, d_s, lp, reverse=True)
        ds = d_s[...]
        xc, r, i = m["xc"], m["r"], m["i"]
        row = lax.broadcasted_iota(jnp.int32, (lp, LRU_TILE), 0)
        da = ds * jnp.where(row >= 1, pltpu.roll(hs, 1, 0), 0.0)
        db = jnp.where(m["valid"], ds, 0.0)
        di = db * m["mult_eff"] * xc
        dxc = db * m["mult_eff"] * i
        live = m["valid"] & jnp.logical_not(m["first"])
        dm = jnp.where(live, db * i * xc, 0.0)
        dla = da * m["a"] - dm * (m["e2"] * m["rs"])
        dr = dla * (-C_RGLRU * m["sp"])
        dsp = jnp.sum(dla * (-C_RGLRU * r), axis=0, keepdims=True)
        dpr = (dr * r * (1.0 - r))
        dpi = (di * i * (1.0 - i))
        dprb, dpib = dpr.astype(BF16), dpi.astype(BF16)
        dxc = dxc + _dot_nt(dprb, wa) + _dot_nt(dpib, wx)
        du = (cw[3:4, :] * dxc + cw[2:3, :] * pltpu.roll(dxc, lp - 1, 0) + cw[1:2, :] * pltpu.roll(dxc, lp - 2, 0)
              + cw[0:1, :] * pltpu.roll(dxc, lp - 3, 0))
        du_ref[...] = jnp.where(m["valid"], du, 0.0).astype(BF16)
        tap = lax.broadcasted_iota(jnp.int32, (CONV_K, LRU_TILE), 0)
        dcw = jnp.zeros((CONV_K, LRU_TILE), F32)
        for kk in range(CONV_K):
            shifted = u if kk == CONV_K - 1 else pltpu.roll(u, CONV_K - 1 - kk, 0)
            dcw = jnp.where(tap == kk, jnp.sum(dxc * shifted, axis=0, keepdims=True), dcw)
        parts = [(dcw_ref, dcw[None]), (dcb_ref, jnp.sum(dxc, axis=0, keepdims=True)[None]),
                 (dwa_ref, _dot_tn(m["xcb"], dprb)[None]), (dba_ref, jnp.sum(dpr, axis=0, keepdims=True)[None]),
                 (dwx_ref, _dot_tn(m["xcb"], dpib)[None]), (dbx_ref, jnp.sum(dpi, axis=0, keepdims=True)[None]),
                 (dlam_ref, (dsp * (-jax.nn.sigmoid(-lam)))[None])]

        @pl.when(b_idx == 0)
        def _():
            for ref, val in parts:
                ref[...] = val

        @pl.when(b_idx != 0)
        def _():
            for ref, val in parts:
                ref[...] += val

    vec3 = pl.BlockSpec((1, 1, LRU_TILE), lambda t, b: (t, 0, 0))
    vshape = jax.ShapeDtypeStruct((N_LRU_TILES, 1, LRU_TILE), F32)
    mshape = jax.ShapeDtypeStruct((N_LRU_TILES, LRU_TILE, LRU_TILE), F32)
    pair = pl.BlockSpec((lp, 2 * LRU_TILE), lambda t, b: (b, Z_U // (2 * LRU_TILE) + t))
    return _hosted_call(
        body, name="lru_bwd", grid=(N_LRU_TILES, nb),
        in_specs=[seq(Z_U // LRU_TILE, 2), seq(Z_U // LRU_TILE + 1, 2), seq(0), seq(0), cwspec, vec, mat, vec, mat,
                  vec, vec, pl.BlockSpec(memory_space=pl.ANY)],
        out_specs=[pair, cwspec, vec3, mat, vec3, mat, vec3, vec3],
        out_shape=[jax.ShapeDtypeStruct(dz.shape, dz.dtype), jax.ShapeDtypeStruct((N_LRU_TILES, CONV_K, LRU_TILE), F32),
                   vshape, mshape, vshape, mshape, vshape, vshape],
        scratch_shapes=[pltpu.VMEM((lp, LRU_TILE), F32)] * 3,
        dims=("parallel", "arbitrary"), args=(z, z, hs, dy, cw, cb, wa, ba, wx, bx, lam, dz), comm=comm,
        aliases={11: 0})


def _mix_out_call(ya, yl, ga, gl, wout, h, next_gain, tm):
    rows, d = h.shape

    def body(ya_ref, yl_ref, ga_ref, gl_ref, w_ref, h_ref, ng_ref, y_ref, o_ref, u_ref):
        a = ya_ref[...]
        l = yl_ref[...]
        an = (a * _rms(a, MLA_W) * ga_ref[...]).astype(BF16)
        ln = (l * _rms(l, LRU_W) * gl_ref[...]).astype(BF16)
        y_ref[:, 0:MLA_W] = an
        y_ref[:, MLA_W:MLA_W + LRU_W] = ln
        out = h_ref[...] + _dot(an, w_ref[0:MLA_W, :]) + _dot(ln, w_ref[MLA_W:MLA_W + LRU_W, :])
        o_ref[...] = out
        u_ref[...] = (out * _rms(out, d) * ng_ref[...]).astype(BF16)

    half = pl.BlockSpec((tm, MLA_W), lambda i: (i, 0))
    g = pl.BlockSpec((1, MLA_W), lambda i: (0, 0))
    full = pl.BlockSpec((tm, d), lambda i: (i, 0))
    return pl.pallas_call(
        body, name="mix_out", grid=(rows // tm,),
        in_specs=[half, half, g, g, pl.BlockSpec((MLA_W + LRU_W, d), lambda i: (0, 0)), full,
                  pl.BlockSpec((1, d), lambda i: (0, 0))],
        out_specs=[full, full, full],
        out_shape=[jax.ShapeDtypeStruct((rows, MLA_W + LRU_W), BF16), jax.ShapeDtypeStruct((rows, d), F32),
                   jax.ShapeDtypeStruct((rows, d), BF16)],
        compiler_params=_params(("parallel",)))(ya, yl, ga, gl, wout, h, next_gain)


def _ffn_dact_call(name, dhb, wd, gate, up, tm, comm=None):
    rows, d = dhb.shape
    ns, fs, _ = wd.shape

    nsub = 2 if tm % 32 == 0 else 1
    sub = tm // nsub

    def body(dh_ref, wd_ref, g_ref, p_ref, dg_ref, dp_ref):
        wd = wd_ref[0]
        for r in range(nsub):
            rs = slice(r * sub, (r + 1) * sub)
            da = (0.5 * _dot_nt(dh_ref[rs, :], wd)).astype(BF16)
            g = g_ref[0, rs, :]
            p = p_ref[0, rs, :]
            sg = jax.nn.sigmoid(g)
            dg_ref[0, rs, :] = (da * p) * (sg * (1.0 + g * (1.0 - sg)))
            dp_ref[0, rs, :] = da * (g * sg)

    aspec = pl.BlockSpec((1, tm, fs), lambda s, i: (s, i, 0))
    oshape = jax.ShapeDtypeStruct((ns, rows, fs), BF16)
    return _hosted_call(
        body, name=name, grid=(ns, rows // tm),
        in_specs=[pl.BlockSpec((tm, d), lambda s, i: (i, 0)), pl.BlockSpec((1, fs, d), lambda s, i: (s, 0, 0)),
                  aspec, aspec],
        out_specs=[aspec, aspec], out_shape=[oshape, oshape],
        dims=("parallel", "parallel"), args=(dhb, wd, gate, up), comm=comm)


def _norm_in_bwd_call(name, pieces, h, g, dres, tm, comm=None, part=(0, 1), prev=None, mix=None):
    rows, d = h.shape
    npc = len(pieces)
    steps = rows // tm // part[1]
    off = part[0] * steps
    n_prev = 0 if prev is None else 2
    n_mix = 0 if mix is None else 5

    def body(*refs):
        d_refs = refs[0:2 * npc:2]
        w_refs = refs[1:2 * npc:2]
        h_ref, g_ref, dres_ref = refs[2 * npc:2 * npc + 3]
        mix_in_refs = refs[2 * npc + 3:2 * npc + 3 + n_mix]
        dh_ref, dhb_ref, dg_ref = refs[2 * npc + 3 + n_mix + n_prev:2 * npc + 6 + n_mix + n_prev]
        mix_out_refs = refs[2 * npc + 6 + n_mix + n_prev:]
        du = jnp.zeros((tm, d), F32)
        for d_ref, w_ref in zip(d_refs, w_refs):
            if len(d_ref.shape) == 3:
                for s in range(d_ref.shape[0]):
                    du = du + _dot(d_ref[s], w_ref[s])
            else:
                du = du + _dot(d_ref[...], w_ref[...])
        x = h_ref[...]
        r = _rms(x, d)
        n = x * r
        dh = dres_ref[...] + _rms_bwd(du * g_ref[...], n, r, d)
        dhb = dh.astype(BF16)
        dh_ref[...] = dh
        dhb_ref[...] = dhb
        sums = [(dg_ref, jnp.sum(du * n, axis=0, keepdims=True))]
        if mix is not None:
            wo_ref, ya_ref, yl_ref, ga_ref, gl_ref = mix_in_refs
            dya_ref, dyl_ref, dga_ref, dgl_ref = mix_out_refs
            dy = _dot_nt(dhb, wo_ref[...])
            for val, gain_ref, lo, out_ref, acc_ref in ((ya_ref[...], ga_ref, 0, dya_ref, dga_ref),
                                                        (yl_ref[...], gl_ref, MLA_W, dyl_ref, dgl_ref)):
                rb = _rms(val, MLA_W)
                nb_ = val * rb
                dyn = dy[:, lo:lo + MLA_W]
                out_ref[...] = _rms_bwd(dyn * gain_ref[...], nb_, rb, MLA_W)
                sums.append((acc_ref, jnp.sum(dyn * nb_, axis=0, keepdims=True)))

        @pl.when(pl.program_id(0) == 0)
        def _():
            for ref, val in sums:
                ref[...] = val

        @pl.when(pl.program_id(0) != 0)
        def _():
            for ref, val in sums:
                ref[...] += val

    in_specs, args = [], []
    for dd, w in pieces:
        if dd.ndim == 3:
            in_specs.append(pl.BlockSpec((dd.shape[0], tm, dd.shape[2]), lambda i: (0, i + off, 0)))
            in_specs.append(_resident(w.shape))
        else:
            in_specs.append(pl.BlockSpec((tm, dd.shape[1]), lambda i: (i + off, 0)))
            in_specs.append(_resident(w.shape))
        args += [dd, w]
    full = pl.BlockSpec((tm, d), lambda i: (i + off, 0))
    gspec = pl.BlockSpec((1, d), lambda i: (0, 0))
    half = pl.BlockSpec((tm, MLA_W), lambda i: (i + off, 0))
    hgain = pl.BlockSpec((1, MLA_W), lambda i: (0, 0))
    mix_in_specs = [] if mix is None else [_resident(mix[0].shape), half, half, hgain, hgain]
    mix_out_specs = [] if mix is None else [half, half, hgain, hgain]
    mix_out_shapes = [] if mix is None else [
        jax.ShapeDtypeStruct((rows, MLA_W), F32), jax.ShapeDtypeStruct((rows, LRU_W), F32),
        jax.ShapeDtypeStruct((1, MLA_W), F32), jax.ShapeDtypeStruct((1, LRU_W), F32)]
    n_in = len(in_specs) + 3 + n_mix
    return _hosted_call(
        body, name=name, grid=(steps,),
        in_specs=in_specs + [full, gspec, full] + mix_in_specs + _any_specs(n_prev),
        out_specs=[full, full, gspec] + mix_out_specs,
        out_shape=[jax.ShapeDtypeStruct((rows, d), F32), jax.ShapeDtypeStruct((rows, d), BF16),
                   jax.ShapeDtypeStruct((1, d), F32)] + mix_out_shapes,
        args=(*args, h, g, dres, *(mix or ()), *(prev or ())), comm=comm,
        aliases={n_in: 0, n_in + 1: 1} if prev is not None else {})


def _wgrad_call(name, a, b, scale=1.0, comm=None):
    a3, b3 = a.ndim == 3, b.ndim == 3
    ns = a.shape[0] if a3 else (b.shape[0] if b3 else 1)
    rows, m = a.shape[-2:]
    n = b.shape[-1]
    tmm = m if a3 else _col_tile(m, 256)

    def body(a_ref, b_ref, o_ref):
        av = a_ref[0] if a3 else a_ref[...]
        bv = b_ref[0] if b3 else b_ref[...]
        res = _dot_tn(av, bv)
        if scale != 1.0:
            res = res * scale
        if a3 or b3:
            o_ref[0] = res
        else:
            o_ref[...] = res

    aspec = (pl.BlockSpec((1, rows, tmm), lambda s, j: (s, 0, j)) if a3
             else pl.BlockSpec((rows, tmm), lambda s, j: (0, j)))
    bspec = (pl.BlockSpec((1, rows, n), lambda s, j: (s, 0, 0)) if b3
             else pl.BlockSpec((rows, n), lambda s, j: (0, 0)))
    if a3 or b3:
        ospec = pl.BlockSpec((1, tmm, n), lambda s, j: (s, j, 0))
        oshape = jax.ShapeDtypeStruct((ns, m, n), F32)
    else:
        ospec = pl.BlockSpec((tmm, n), lambda s, j: (j, 0))
        oshape = jax.ShapeDtypeStruct((m, n), F32)
    return _hosted_call(
        body, name=name, grid=(ns, m // tmm), in_specs=[aspec, bspec], out_specs=[ospec], out_shape=[oshape],
        dims=("parallel", "parallel"), args=(a, b), comm=comm)[0]


def _mla_prep_bwd_call(z, dq, dk, dv, gql, gkvl, gqh, gkh, wuq, wuk, wuv, tabs, lp, tm, comm=None):
    rows = z.shape[0]
    tpe = lp // tm

    def body(z_ref, dq_ref, dk_ref, dv_ref, gql_ref, gkvl_ref, gqh_ref, gkh_ref, wuq_ref, wuk_ref, wuv_ref,
             c_ref, s1_ref, s2_ref, dz_ref, dgql_ref, dgkvl_ref, dgqh_ref, dgkh_ref, dwuq_ref, dwuk_ref, dwuv_ref,
             dqp_ref, dkn_ref, dvv_ref):
        gql, gkvl = gql_ref[...], gkvl_ref[...]
        gq, gk = gqh_ref[...], gkh_ref[...]
        wuq, wuk, wuv = wuq_ref[...], wuk_ref[...], wuv_ref[...]
        m = _mla_heads(z_ref[...], gql, gkvl, wuq, wuk, wuv)
        c, s1, s2 = c_ref[...], s1_ref[...], s2_ref[...]
        dgq = jnp.zeros((1, D_QKP), F32)
        dgk = jnp.zeros((1, D_QKP), F32)
        dkr = jnp.zeros((tm, D_QKP - D_NOPE), F32)
        for hd in range(HEADS):
            qn, rqh, knn, krn, rkh = m["heads"][hd]
            dqg = jnp.concatenate([dq_ref[hd, :, 0:D_NOPE].astype(F32),
                                   _rope_t(dq_ref[hd, :, D_NOPE:D_QKP].astype(F32), c, s1, s2)], axis=1)
            dgq = dgq + jnp.sum(dqg * qn, axis=0, keepdims=True)
            dqn = dqg * gq
            dqr = rqh * (dqn - qn * (jnp.sum(dqn * qn, axis=-1, keepdims=True) * (1.0 / D_QK)))
            dqp_ref[:, hd * D_QKP:(hd + 1) * D_QKP] = dqr.astype(BF16)
            kn_full = jnp.concatenate([knn, krn], axis=1)
            dkg = jnp.concatenate([dk_ref[hd, :, 0:D_NOPE].astype(F32),
                                   _rope_t(dk_ref[hd, :, D_NOPE:D_QKP].astype(F32), c, s1, s2)], axis=1)
            dgk = dgk + jnp.sum(dkg * kn_full, axis=0, keepdims=True)
            dkn = dkg * gk
            dkraw = rkh * (dkn - kn_full * (jnp.sum(dkn * kn_full, axis=-1, keepdims=True) * (1.0 / D_QK)))
            dkn_ref[:, hd * D_NOPE:(hd + 1) * D_NOPE] = dkraw[:, 0:D_NOPE].astype(BF16)
            dkr = dkr + dkraw[:, D_NOPE:D_QKP]
            dvv_ref[:, hd * D_V:(hd + 1) * D_V] = dv_ref[hd]
        dcqn = _dot(dqp_ref[...], wuq)
        dckvn = _dot_nt(dkn_ref[...], wuk) + _dot_nt(dvv_ref[...], wuv)
        dz_ref[:, 0:Q_RANK] = _rms_bwd(dcqn * gql, m["nq"], m["rq"], Q_RANK).astype(BF16)
        dz_ref[:, Q_RANK:Z_KR] = _rms_bwd(dckvn * gkvl, m["nkv"], m["rkv"], KV_RANK).astype(BF16)
        dz_ref[:, Z_KR:Z_MLA] = dkr.astype(BF16)
        parts = [(dgql_ref, jnp.sum(dcqn * m["nq"], axis=0, keepdims=True)),
                 (dgkvl_ref, jnp.sum(dckvn * m["nkv"], axis=0, keepdims=True)), (dgqh_ref, dgq), (dgkh_ref, dgk),
                 (dwuq_ref, _dot_tn(dqp_ref[...], m["cqn"])), (dwuk_ref, _dot_tn(m["ckvn"], dkn_ref[...])),
                 (dwuv_ref, _dot_tn(m["ckvn"], dvv_ref[...]))]

        @pl.when(pl.program_id(0) == 0)
        def _():
            for ref, val in parts:
                ref[...] = val

        @pl.when(pl.program_id(0) != 0)
        def _():
            for ref, val in parts:
                ref[...] += val

    def const(shape):
        return pl.BlockSpec(shape, lambda i: tuple(0 for _ in shape))

    tab = pl.BlockSpec((tm, 128), lambda i: (i % tpe, 0))
    hq = pl.BlockSpec((HEADS, tm, D_QKP), lambda i: (0, i, 0))
    hv = pl.BlockSpec((HEADS, tm, D_V), lambda i: (0, i, 0))

    def rowspec(n):
        return pl.BlockSpec((tm, n), lambda i: (i, 0))

    return _hosted_call(
        body, name="mla_prep_bwd", grid=(rows // tm,),
        in_specs=[rowspec(Z_MLA), hq, hq, hv, const((1, Q_RANK)), const((1, KV_RANK)), const((1, D_QKP)),
                  const((1, D_QKP)), const((HEADS * D_QKP, Q_RANK)), const((KV_RANK, HEADS * D_NOPE)),
                  const((KV_RANK, HEADS * D_V)), tab, tab, tab],
        out_specs=[rowspec(Z_MLA), const((1, Q_RANK)), const((1, KV_RANK)), const((1, D_QKP)), const((1, D_QKP)),
                   const((HEADS * D_QKP, Q_RANK)), const((KV_RANK, HEADS * D_NOPE)), const((KV_RANK, HEADS * D_V))],
        out_shape=[jax.ShapeDtypeStruct((rows, Z_W), BF16),
                   jax.ShapeDtypeStruct((1, Q_RANK), F32), jax.ShapeDtypeStruct((1, KV_RANK), F32),
                   jax.ShapeDtypeStruct((1, D_QKP), F32), jax.ShapeDtypeStruct((1, D_QKP), F32),
                   jax.ShapeDtypeStruct((HEADS * D_QKP, Q_RANK), F32),
                   jax.ShapeDtypeStruct((KV_RANK, HEADS * D_NOPE), F32), jax.ShapeDtypeStruct((KV_RANK, HEADS * D_V), F32)],
        scratch_shapes=[pltpu.VMEM((tm, HEADS * D_QKP), BF16), pltpu.VMEM((tm, HEADS * D_NOPE), BF16),
                        pltpu.VMEM((tm, HEADS * D_V), BF16)],
        args=(z, dq, dk, dv, gql, gkvl, gqh, gkh, wuq, wuk, wuv, *tabs), comm=comm)


def _local_step(h0, target, w, nb, lp, sched=None):
    tm = _row_tile(nb * lp, MATMUL_ROWS)
    te = _row_tile(lp, ELEMENTWISE_ROWS)
    tabs = _rope_tables(lp)
    g = {}
    if sched is None:
        host = lambda stage: None
    else:
        sched.g = g
        host = sched.host

    def ffn_act(tag, u):
        gate, up, act = _ffn_up_call(tag + "_up", u, w[tag + "_w_gate"], w[tag + "_w_up"], tm, host(tag + "_up"))
        return u, gate, up, act

    def ffn_bwd(tag, h, saved, dh, dhb, split, mix=None):
        u, gate, up, act = saved
        dgate, dup = _ffn_dact_call(tag + "_dact", dhb, w[tag + "_w_down"], gate, up, tm, host(tag + "_dact"))
        g[tag + "_w_down"] = _wgrad_call(tag + "_dwd", act, dhb, 0.5, host(tag + "_dwd"))
        g[tag + "_w_gate"] = _wgrad_call(tag + "_dwg", dgate, u, 1.0, host(tag + "_dwg"))
        g[tag + "_w_up"] = _wgrad_call(tag + "_dwu", dup, u, 1.0, host(tag + "_dwu"))
        pieces = [(dgate, w[tag + "_w_gate"]), (dup, w[tag + "_w_up"])]
        if not split:
            res = _norm_in_bwd_call(tag + "_din", pieces, h, w[tag + "_norm"], dh, te, host(tag + "_din"), mix=mix)
            g[tag + "_norm"] = res[2]
            return (res[0], res[1], *res[3:])
        dh_a, dhb_a, dg_a = _norm_in_bwd_call(tag + "_din_a", pieces, h, w[tag + "_norm"], dh, te,
                                              host(tag + "_din_a"), part=(0, 2))
        dh_in, dhb_in, dg_b = _norm_in_bwd_call(tag + "_din_b", pieces, h, w[tag + "_norm"], dh, te,
                                                host(tag + "_din_b"), part=(1, 2), prev=(dh_a, dhb_a))
        g[tag + "_norm"] = dg_a + dg_b
        return dh_in, dhb_in

    s1 = ffn_act("ffn1", _rmsnorm_call("ffn1_norm", h0, w["ffn1_norm"], te, host("ffn1_norm")))
    h1, un = _ffn_down_call("ffn1_down", s1[3], w["ffn1_w_down"], h0, te, host("ffn1_down"), w["mix_norm"])
    z = _mm_call("mix_in", un, w["w_in"], tm, F32)
    mla_w = (w["q_latent_norm"], w["kv_latent_norm"], w["q_head_norm"], w["k_head_norm"], w["w_uq"], w["w_uk"],
             w["w_uv"])
    q, k, v = _mla_prep_call(z, *mla_w, tabs, lp, te)
    o, lse = _attn_fwd_call(q, k, v, nb, lp, host("attn_fwd"))
    lru_w = (w["conv_w"], w["conv_b"], w["gate_a_w"], w["gate_a_b"], w["gate_x_w"], w["gate_x_b"], w["lru_lambda"])
    yl, hs = _lru_fwd_call(z, *lru_w, nb, lp, host("lru_fwd"))
    y, h2, u2 = _mix_out_call(o, yl, w["attn_out_norm"], w["lru_out_norm"], w["w_out"], h1, w["ffn2_norm"], te)
    s2 = ffn_act("ffn2", u2)
    dh3, dh3b, g["final_norm"], loss = _ffn_down_loss_call("ffn2_down", s2[3], w["ffn2_w_down"], h2, w["final_norm"],
                                                           target, lp, te)
    g["loss"] = loss

    dh2, dh2b, dya, dyl, g["attn_out_norm"], g["lru_out_norm"] = ffn_bwd(
        "ffn2", h2, s2, dh3, dh3b, False, (w["w_out"], o, yl, w["attn_out_norm"], w["lru_out_norm"]))
    g["w_out"] = _wgrad_call("dw_out", y, dh2b)
    dq, dk, dv = _attn_bwd_call(q, k, v, o, lse, dya, nb, lp, host("attn_bwd"))
    (dz_mla, g["q_latent_norm"], g["kv_latent_norm"], g["q_head_norm"], g["k_head_norm"], g["w_uq"], g["w_uk"],
     g["w_uv"]) = _mla_prep_bwd_call(z, dq, dk, dv, *mla_w, tabs, lp, _row_tile(lp, PREP_BWD_ROWS),
                                     host("mla_prep_bwd"))
    (dz, g["conv_w"], g["conv_b"], g["gate_a_w"], g["gate_a_b"], g["gate_x_w"], g["gate_x_b"],
     g["lru_lambda"]) = _lru_bwd_call(z, dz_mla, hs, dyl, *lru_w, nb, lp, host("lru_bwd"))
    g["w_in"] = _wgrad_call("dw_in", dz, un)
    dh1, dh1b, g["mix_norm"] = _norm_in_bwd_call("mix_din", [(dz, w["w_in"])], h1, w["mix_norm"], dh2, te,
                                                 host("mix_din"))
    dh0 = ffn_bwd("ffn1", h0, s1, dh1, dh1b, True)[0]
    return loss, dh0, g


def _place():
    x, y, c = lax.axis_index("x"), lax.axis_index("y"), lax.axis_index("c")
    return x, y, c, [(1 - x, y), (x, 1 - y), (1 - x, 1 - y)]


def _any_specs(n):
    return [pl.BlockSpec(memory_space=pl.ANY)] * n


def _remote(src, dst, sems, k, dev):
    send_sems, recv_sems, base = sems
    return pltpu.make_async_remote_copy(src_ref=src, dst_ref=dst, send_sem=send_sems.at[base + k],
                                        recv_sem=recv_sems.at[base + k], device_id=dev, device_id_type=MESH)


EW_VMEM_BYTES = 24 * 1024 * 1024


def _fit_rows(rows, cols, blocks):
    return _row_tile(rows, max(16, int(EW_VMEM_BYTES // (8 * blocks)) // cols))


class _Geom:
    def __init__(self, n0, n1, blocks=1.0):
        self.n0, self.n1 = n0, n1
        self.axis = 0 if n0 % 32 == 0 else 1
        self.h0, self.h1 = (n0 // 2, n1) if self.axis == 0 else (n0, n1 // 2)
        self.tr = _fit_rows(self.h0, self.h1, blocks)
        self.nblk = self.h0 // self.tr

    def half_ref(self, ref, lead, idx):
        if self.axis == 0:
            return ref.at[(*lead, pl.ds(idx * self.h0, self.h0))]
        return ref.at[(*lead, slice(None), pl.ds(idx * self.h1, self.h1))]

    def half_block(self, lead, i, idx):
        return (*lead, idx * self.nblk + i, 0) if self.axis == 0 else (*lead, i, idx)


class _Comm:
    def __init__(self, ins, out_shapes, aliases, n_sems, start, finish, deliver):
        self.ins, self.out_shapes, self.aliases, self.n_sems = list(ins), list(out_shapes), dict(aliases), n_sems
        self.start, self.finish, self.deliver = start, finish, deliver

    def scratch(self):
        return [pltpu.SemaphoreType.DMA((self.n_sems,)), pltpu.SemaphoreType.DMA((self.n_sems,))]


def _comm_call(name, comm):
    n_in = len(comm.ins)

    def body(*refs):
        ins, outs, sems = refs[:n_in], refs[n_in:-2], (*refs[-2:], 0)
        comm.start(ins, outs, sems)
        comm.finish(ins, outs, sems)

    res = pl.pallas_call(
        body, name=name, out_shape=comm.out_shapes, in_specs=_any_specs(n_in),
        out_specs=_any_specs(len(comm.out_shapes)), input_output_aliases=comm.aliases,
        scratch_shapes=comm.scratch())(*comm.ins)
    return comm.deliver(list(res))


def _hosted_call(body, *, name, grid, in_specs, out_specs, out_shape, args, scratch_shapes=(), dims=None, comm=None,
                 prefetch=None, aliases=None):
    aliases = dict(aliases or {})
    in_specs, out_specs, out_shape = list(in_specs), list(out_specs), list(out_shape)
    n_pre = 0 if prefetch is None else 1

    def call(fn, in_specs, out_specs, out_shape, scratch, aliases, dims, args):
        if prefetch is None:
            return pl.pallas_call(
                fn, name=name, grid=grid, in_specs=in_specs, out_specs=out_specs, out_shape=out_shape,
                scratch_shapes=scratch, input_output_aliases=aliases, compiler_params=_params(dims))(*args)
        spec = pltpu.PrefetchScalarGridSpec(num_scalar_prefetch=1, grid=grid, in_specs=in_specs, out_specs=out_specs,
                                            scratch_shapes=scratch)
        return pl.pallas_call(
            fn, name=name, grid_spec=spec, out_shape=out_shape,
            input_output_aliases={i + 1: o for i, o in aliases.items()}, compiler_params=_params(dims))(prefetch, *args)

    if comm is None:
        return list(call(body, in_specs, out_specs, out_shape, list(scratch_shapes), aliases,
                         dims or ("arbitrary",) * len(grid), args))
    n_in, n_out, n_ci, n_co = len(in_specs), len(out_specs), len(comm.ins), len(comm.out_shapes)

    def wrapped(*refs):
        pre, refs = refs[:n_pre], refs[n_pre:]
        ins, cins = refs[:n_in], refs[n_in:n_in + n_ci]
        outs = refs[n_in + n_ci:n_in + n_ci + n_out]
        couts = refs[n_in + n_ci + n_out:n_in + n_ci + n_out + n_co]
        scratch, sems = refs[n_in + n_ci + n_out + n_co:-2], (*refs[-2:], 0)
        first = functools.reduce(jnp.logical_and, [pl.program_id(k) == 0 for k in range(len(grid))])
        last = functools.reduce(jnp.logical_and, [pl.program_id(k) == grid[k] - 1 for k in range(len(grid))])

        @pl.when(first)
        def _():
            comm.start(cins, couts, sems)

        body(*pre, *ins, *outs, *scratch)

        @pl.when(last)
        def _():
            comm.finish(cins, couts, sems)

    res = call(wrapped, in_specs + _any_specs(n_ci), out_specs + _any_specs(n_co), out_shape + comm.out_shapes,
               list(scratch_shapes) + comm.scratch(),
               {**aliases, **{n_in + i: n_out + o for i, o in comm.aliases.items()}},
               ("arbitrary",) * len(grid), (*args, *comm.ins))
    comm.deliver(list(res[n_out:]))
    return list(res[:n_out])


def _gather_comm(bufs, deliver):
    n = len(bufs)
    geoms = [_Geom(*b.shape[1:]) for b in bufs]

    def first(outs, sems):
        x, y, c, chips = _place()
        cps = []
        for a in range(n):
            mine = geoms[a].half_ref(outs[a], (2 * x + y,), c)
            cps += [_remote(mine, mine, sems, 6 * a + j, (cx, cy, c)) for j, (cx, cy) in enumerate(chips)]
        return cps

    def start(ins, outs, sems):
        for cp in first(outs, sems):
            cp.start()

    def finish(ins, outs, sems):
        x, y, c, chips = _place()
        sib = (x, y, 1 - c)
        passed = []
        for a in range(n):
            for j, (cx, cy) in enumerate(chips):
                land = geoms[a].half_ref(outs[a], (2 * cx + cy,), c)
                _remote(land, land, sems, 6 * a + j, sib).wait_recv()
                cp = _remote(land, land, sems, 6 * a + 3 + j, sib)
                cp.start()
                passed.append(cp)
        for a in range(n):
            for j, (cx, cy) in enumerate(chips):
                land = geoms[a].half_ref(outs[a], (2 * cx + cy,), 1 - c)
                _remote(land, land, sems, 6 * a + 3 + j, sib).wait_recv()
        for cp in first(outs, sems) + passed:
            cp.wait_send()

    return _Comm(bufs, [jax.ShapeDtypeStruct(b.shape, b.dtype) for b in bufs], {a: a for a in range(n)}, 6 * n,
                 start, finish, deliver)


def _reduce_pair_comm(grads, deliver):
    n = len(grads)
    geoms = [_Geom(*a.shape[1:]) for a in grads]

    def copies(ins, outs, sems):
        x, y, c, _ = _place()
        return [_remote(geoms[a].half_ref(ins[a], (slice(None),), 1 - c), outs[a], sems, a, (x, y, 1 - c))
                for a in range(n)]

    def start(ins, outs, sems):
        for cp in copies(ins, outs, sems):
            cp.start()

    def finish(ins, outs, sems):
        cps = copies(ins, outs, sems)
        for cp in cps:
            cp.wait_recv()
        for cp in cps:
            cp.wait_send()

    shapes = [jax.ShapeDtypeStruct((N_SHARD, g.h0, g.h1), a.dtype) for a, g in zip(grads, geoms)]
    return _Comm(grads, shapes, {}, n, start, finish, deliver)


def _reduce_chips_comm(parts, deliver):
    n = len(parts)

    def copies(ins, outs, sems):
        x, y, c, chips = _place()
        return [_remote(ins[a].at[2 * cx + cy], outs[a].at[j], sems, 3 * a + j, (cx, cy, c))
                for a in range(n) for j, (cx, cy) in enumerate(chips)]

    def start(ins, outs, sems):
        for cp in copies(ins, outs, sems):
            cp.start()

    def finish(ins, outs, sems):
        cps = copies(ins, outs, sems)
        for cp in cps:
            cp.wait_recv()
        for cp in cps:
            cp.wait_send()

    shapes = [jax.ShapeDtypeStruct((3,) + a.shape[1:], a.dtype) for a in parts]
    return _Comm(parts, shapes, {}, 3 * n, start, finish, deliver)


def _share_pair_comm(bufs, deliver):
    n = len(bufs)
    geoms = [_Geom(*b.shape) for b in bufs]

    def copies(outs, sems):
        x, y, c, _ = _place()
        cps = []
        for a in range(n):
            mine = geoms[a].half_ref(outs[a], (), c)
            cps.append(_remote(mine, mine, sems, a, (x, y, 1 - c)))
        return cps

    def start(ins, outs, sems):
        for cp in copies(outs, sems):
            cp.start()

    def finish(ins, outs, sems):
        x, y, c, _ = _place()
        for a in range(n):
            land = geoms[a].half_ref(outs[a], (), 1 - c)
            _remote(land, land, sems, a, (x, y, 1 - c)).wait_recv()
        for cp in copies(outs, sems):
            cp.wait_send()

    return _Comm(bufs, [jax.ShapeDtypeStruct(b.shape, b.dtype) for b in bufs], {a: a for a in range(n)}, n,
                 start, finish, deliver)


def _small_comm(pack, deliver):
    r, d = pack.shape

    def copies(ins, outs, sems):
        x, y, c, _ = _place()
        cps = []
        for k in range(1, 8):
            peer = (x ^ ((k >> 2) & 1), y ^ ((k >> 1) & 1), c ^ (k & 1))
            cps.append(_remote(ins[0], outs[0].at[4 * x + 2 * y + c], sems, k - 1, peer))
        return cps

    def start(ins, outs, sems):
        for cp in copies(ins, outs, sems):
            cp.start()

    def finish(ins, outs, sems):
        cps = copies(ins, outs, sems)
        for cp in cps:
            cp.wait_recv()
        for cp in cps:
            cp.wait_send()

    return _Comm([pack], [jax.ShapeDtypeStruct((8, r, d), pack.dtype)], {}, 7, start, finish, deliver)


def _join_comms(comms):
    comms = [c for c in comms if c is not None]
    if len(comms) <= 1:
        return comms[0] if comms else None
    ins, out_shapes, aliases, spans, n_sems = [], [], {}, [], 0
    for c in comms:
        aliases.update({len(ins) + i: len(out_shapes) + o for i, o in c.aliases.items()})
        spans.append((len(ins), len(ins) + len(c.ins), len(out_shapes), len(out_shapes) + len(c.out_shapes), n_sems))
        ins += c.ins
        out_shapes += c.out_shapes
        n_sems += c.n_sems

    def run(which):
        def go(all_ins, all_outs, sems):
            for c, (i0, i1, o0, o1, base) in zip(comms, spans):
                getattr(c, which)(all_ins[i0:i1], all_outs[o0:o1], (sems[0], sems[1], sems[2] + base))
        return go

    def deliver(outs):
        for c, (_, _, o0, o1, _) in zip(comms, spans):
            c.deliver(outs[o0:o1])
        return outs

    return _Comm(ins, out_shapes, aliases, n_sems, run("start"), run("finish"), deliver)


def _ew_call(name, fn, ins, out_dtypes):
    shape = ins[0].shape
    cols = shape[-1]
    rows = 1
    for s_ in shape[:-1]:
        rows *= s_
    ins2 = [a.reshape(rows, cols) for a in ins]
    tr = rows
    for t in range(16, min(rows, max(16, (1 << 19) // cols)) + 1, 16):
        if rows % t == 0:
            tr = t
    no = len(out_dtypes)

    def body(*refs):
        outs = fn(*[r[...] for r in refs[:len(ins2)]])
        for ref, val in zip(refs[len(ins2):], outs):
            ref[...] = val.astype(ref.dtype)

    spec = pl.BlockSpec((tr, cols), lambda i: (i, 0))
    res = pl.pallas_call(
        body, name=name, grid=(rows // tr,), in_specs=[spec] * len(ins2), out_specs=[spec] * no,
        out_shape=[jax.ShapeDtypeStruct((rows, cols), dt) for dt in out_dtypes],
        compiler_params=_params(("parallel",)))(*ins2)
    return [r.reshape(shape) for r in res]


def _adamw_math(w, g, m, v):
    m = ADAM_B1 * m + (1.0 - ADAM_B1) * g
    v = ADAM_B2 * v + (1.0 - ADAM_B2) * (g * g)
    m_hat = m / (1.0 - ADAM_B1 ** ADAM_STEP)
    v_hat = v / (1.0 - ADAM_B2 ** ADAM_STEP)
    delta = -ADAM_LR * (m_hat / (jnp.sqrt(v_hat) + ADAM_EPS) + ADAM_WD * w)
    return delta, m, v


def _adamw_call(name, w, g, m, v):
    return _ew_call(name, _adamw_math, [w, g, m, v], [F32, F32, F32])


def _tiled_call(name, fn, place, grid, in_items, out_items, comm=None):
    ni = len(in_items)

    def body(place_ref, *refs):
        vals = fn(*[r[...] for r in refs[:ni]])
        for ref, val in zip(refs[ni:], vals):
            ref[...] = val.astype(ref.dtype)

    return _hosted_call(
        body, name=name, grid=grid, in_specs=[pl.BlockSpec(blk, imap) for _, blk, imap in in_items],
        out_specs=[pl.BlockSpec(blk, imap) for _, _, blk, imap in out_items],
        out_shape=[jax.ShapeDtypeStruct(shp, dt) for shp, dt, _, _ in out_items],
        args=[a for a, _, _ in in_items], prefetch=place, comm=comm)


def _cast_call(name, place, shards, comm=None):
    n0, n1 = shards[0].shape
    tr = _fit_rows(n0, n1, 1.5 * len(shards))
    ins = [(a, (tr, n1), lambda i, p: (i, 0)) for a in shards]
    outs = [((N_SHARD, n0, n1), BF16, (1, tr, n1), lambda i, p: (p[0], i, 0)) for _ in shards]
    return _tiled_call(name, lambda *v: [x[None] for x in v], place, (n0 // tr,), ins, outs, comm)


def _pair_sum_call(name, place, fulls, gots):
    k = len(fulls)
    g = _Geom(*fulls[0].shape[1:], blocks=2.5 * k)
    blk = (1, g.tr, g.h1)
    ins = [(a, blk, lambda s, i, p: g.half_block((s,), i, p[1])) for a in fulls]
    ins += [(a, blk, lambda s, i, p: (s, i, 0)) for a in gots]
    outs = [((N_SHARD, g.h0, g.h1), BF16, blk, lambda s, i, p: (s, i, 0)) for _ in fulls]
    return _tiled_call(name, lambda *v: [v[j] + v[k + j] for j in range(k)], place, (N_SHARD, g.nblk), ins, outs)


def _chip_sum_call(name, place, fulls, gots, recvs, comm=None):
    k = len(fulls)
    g = _Geom(*fulls[0].shape[1:], blocks=4.5 * k)
    blk = (1, g.tr, g.h1)
    ins = [(a, blk, lambda i, p: g.half_block((p[0],), i, p[1])) for a in fulls]
    ins += [(a, blk, lambda i, p: (p[0], i, 0)) for a in gots]
    ins += [(a, (3, g.tr, g.h1), lambda i, p: (0, i, 0)) for a in recvs]
    outs = [((g.n0, g.n1), F32, (g.tr, g.h1), lambda i, p: g.half_block((), i, p[1])) for _ in fulls]

    def fn(*v):
        res = []
        for j in range(k):
            r = v[2 * k + j].astype(F32)
            res.append(v[j][0] + v[k + j][0] + r[0] + r[1] + r[2])
        return res

    return _tiled_call(name, fn, place, (g.nblk,), ins, outs, comm)


def _adamw_group_call(name, ws, gs, ms, vs, comm=None):
    k = len(ws)
    n0, n1 = ws[0].shape
    tr = _fit_rows(n0, n1, 8 * k)
    spec = pl.BlockSpec((tr, n1), lambda i: (i, 0))

    def body(*refs):
        for j in range(k):
            g = refs[k + j][...]
            delta, m, vv = _adamw_math(refs[j][...], g, refs[2 * k + j][...], refs[3 * k + j][...])
            for ref, val in zip(refs[4 * k + 4 * j:4 * k + 4 * j + 4], (g, delta, m, vv)):
                ref[...] = val

    flat = _hosted_call(
        body, name=name, grid=(n0 // tr,), in_specs=[spec] * (4 * k), out_specs=[spec] * (4 * k),
        out_shape=[jax.ShapeDtypeStruct((n0, n1), F32)] * (4 * k), dims=("parallel",),
        args=(*ws, *gs, *ms, *vs), comm=comm)
    return [flat[4 * j:4 * j + 4] for j in range(k)]


def _small_update_call(me, early, own_early, late, own_late, states):
    nd, r, d = early.shape
    widths = [a.shape[1] for a in states[0][0]]
    nw = len(widths)
    n_state = nw + 2

    def body(me_ref, e_ref, oe_ref, l_ref, ol_ref, *refs):
        state_refs, refs = refs[:3 * n_state], refs[3 * n_state:]
        (gs_ref, d_ref, nm_ref, nv_ref), rows, packs = refs[:4], refs[4:4 + 4 * nw], refs[4 + 4 * nw:]
        for kind, pack in enumerate(packs):
            srefs = state_refs[kind * n_state:(kind + 1) * n_state]
            pack[...] = jnp.zeros_like(pack)
            for k, width in enumerate(widths):
                pack[k:k + 1, 0:width] = srefs[k][...]
            pack[ROW_GATE_A:ROW_GATE_A + 32, :] = srefs[nw][...]
            pack[ROW_GATE_X:ROW_GATE_X + 32, :] = srefs[nw + 1][...]
        w_ref, m_ref, v_ref = packs
        mine = me_ref[0]

        def total(g_ref, own_ref):
            acc = None
            for k in range(nd):
                part = jnp.where(mine == k, own_ref[...], g_ref[k])
                acc = part if acc is None else acc + part
            return acc

        gs = total(e_ref, oe_ref)
        ls = total(l_ref, ol_ref)
        gs_ref[...] = gs
        first = gs[0:8] + ls[0:8]
        gs_ref[0:8, :] = first
        gs_ref[ROW_META:ROW_META + N_META, :] = gs[ROW_META:ROW_META + N_META] + ls[8:8 + N_META]
        grads = jnp.concatenate([first, gs[8:SMALL_ADAM_ROWS]], axis=0)
        delta, m, v = _adamw_math(w_ref[...], grads, m_ref[...], v_ref[...])
        d_ref[...] = delta
        nm_ref[...] = m
        nv_ref[...] = v
        for kind, pack_ref in enumerate((gs_ref, d_ref, nm_ref, nv_ref)):
            for k, width in enumerate(widths):
                rows[kind * nw + k][...] = pack_ref[k:k + 1, 0:width]

    vm = pl.BlockSpec(memory_space=pltpu.VMEM)
    ashape = jax.ShapeDtypeStruct((SMALL_ADAM_ROWS, d), F32)
    row_shapes = [jax.ShapeDtypeStruct((1, width), F32) for _ in range(4) for width in widths]
    flat_states = [a for rows_, ga, gx in states for a in (*rows_, ga, gx)]
    res = pl.pallas_call(
        body, name="small_update", in_specs=[pl.BlockSpec(memory_space=pltpu.SMEM)] + [vm] * (4 + 3 * n_state),
        out_specs=[vm] * (4 + 4 * nw),
        out_shape=[jax.ShapeDtypeStruct((r, d), F32), ashape, ashape, ashape] + row_shapes,
        scratch_shapes=[pltpu.VMEM((SMALL_ADAM_ROWS, d), F32)] * 3,
        compiler_params=pltpu.CompilerParams(vmem_limit_bytes=VMEM_LIMIT_BYTES))(
            me, early, own_early, late, own_late, *flat_states)
    return res[:4], [res[4 + kind * nw:4 + (kind + 1) * nw] for kind in range(4)]


SMALL_NAMES = ["ffn1_norm", "mix_norm", "ffn2_norm", "final_norm", "q_latent_norm", "kv_latent_norm",
               "q_head_norm", "k_head_norm", "conv_b", "gate_a_b", "gate_x_b", "lru_lambda", "attn_out_norm",
               "lru_out_norm"]
ROW_CONV_W = 14
ROW_GATE_A = 16
ROW_GATE_X = 48
ROW_META = 80
ROW_LOSS = 96


def _row(a):
    flat = a.reshape(1, -1)
    return jnp.pad(flat, ((0, 0), (0, D_MODEL - flat.shape[1])))


def _pack_small(t, rows):
    parts = [_row(t[nm]) for nm in SMALL_NAMES]
    parts.append(t["conv_w"].reshape(2, D_MODEL))
    parts.append(t["gate_a_w"].reshape(32, D_MODEL))
    parts.append(t["gate_x_w"].reshape(32, D_MODEL))
    p = jnp.concatenate(parts, axis=0)
    return jnp.pad(p, ((0, rows - p.shape[0]), (0, 0)))


def _early_pack(g):
    gs = {nm: g.get(nm, jnp.zeros((1, D_MODEL), F32)) for nm in SMALL_NAMES}
    gs["q_head_norm"] = g["q_head_norm"][:, 0:D_QK]
    gs["k_head_norm"] = g["k_head_norm"][:, 0:D_QK]
    for nm in ("conv_b", "gate_a_b", "gate_x_b", "lru_lambda"):
        gs[nm] = g[nm].reshape(1, LRU_W)
    gs["conv_w"] = g["conv_w"].transpose(1, 0, 2).reshape(CONV_K, LRU_W)
    gs["gate_a_w"] = _gate_blocks(g["gate_a_w"])
    gs["gate_x_w"] = _gate_blocks(g["gate_x_w"])
    return jnp.concatenate([_pack_small(gs, ROW_META), jnp.zeros((N_META, D_MODEL), F32), _row(g["loss"][:, 0:1]),
                            jnp.zeros((SMALL_ROWS - ROW_LOSS - 1, D_MODEL), F32)], axis=0)


def _unpack_small(p, rows, like):
    out = dict(zip(SMALL_NAMES, rows))
    out["gate_a_w"] = p[ROW_GATE_A:ROW_GATE_A + 32].reshape(like["gate_a_w"].shape)
    out["gate_x_w"] = p[ROW_GATE_X:ROW_GATE_X + 32].reshape(like["gate_x_w"].shape)
    return out


def _gate_dense(wg):
    w4 = wg[0].reshape(N_LRU_TILES, 2, 64, 64)
    zero = jnp.zeros((N_LRU_TILES, 64, 64), wg.dtype)
    top = jnp.concatenate([w4[:, 0], zero], axis=2)
    bot = jnp.concatenate([zero, w4[:, 1]], axis=2)
    return jnp.concatenate([top, bot], axis=1).astype(BF16)


def _gate_blocks(dw):
    return jnp.stack([dw[:, 0:64, 0:64], dw[:, 64:128, 64:128]], axis=1).reshape(8, 64, 64)


BIG_NAMES = ["ffn1_w_gate", "ffn1_w_up", "ffn1_w_down", "w_in", "w_uq", "w_uk", "w_uv", "w_out", "ffn2_w_gate",
             "ffn2_w_up", "ffn2_w_down"]
BIG_GROUPS = [["ffn1_w_gate", "ffn1_w_up", "ffn1_w_down", "ffn2_w_gate", "ffn2_w_up", "ffn2_w_down"], ["w_in"],
              ["w_uq"], ["w_uk", "w_uv"], ["w_out"]]
TRANSPOSED = ("ffn1_w_gate", "ffn1_w_up", "ffn2_w_gate", "ffn2_w_up", "w_in", "w_uq")


def _to2d(nm, a):
    return a[0].T if nm in TRANSPOSED else a[0]


def _from2d(nm, a):
    return (a.T if nm in TRANSPOSED else a)[None]


WEIGHT_NAMES = ["meta_tokens", "ffn1_norm", "ffn1_w_gate", "ffn1_w_up", "ffn1_w_down", "mix_norm", "w_in",
                "q_latent_norm", "w_uq", "kv_latent_norm", "w_uk", "w_uv", "q_head_norm", "k_head_norm", "conv_w",
                "conv_b", "gate_a_w", "gate_a_b", "gate_x_w", "gate_x_b", "lru_lambda", "attn_out_norm",
                "lru_out_norm", "w_out", "ffn2_norm", "ffn2_w_gate", "ffn2_w_up", "ffn2_w_down", "final_norm"]


def _weight_from(nm, slots):
    if nm == "w_in":
        win = slots.reshape(IN_WIDTH, D_MODEL)
        lru = win[Z_KR + D_ROPE:].reshape(2, N_LRU_TILES, LRU_TILE, D_MODEL).transpose(1, 0, 2, 3)
        return jnp.concatenate([win[0:Z_KR + D_ROPE], jnp.zeros((128 - D_ROPE, D_MODEL), BF16),
                                lru.reshape(2 * LRU_W, D_MODEL)], axis=0)
    if nm == "w_uq":
        return jnp.pad(slots, ((0, 0), (0, D_QKP - D_QK), (0, 0))).reshape(HEADS * D_QKP, Q_RANK)
    if nm in ("w_uk", "w_uv"):
        return slots.transpose(1, 0, 2).reshape(KV_RANK, HEADS * D_NOPE)
    if nm == "w_out":
        return slots.reshape(D_MODEL, D_MODEL)
    return slots


def _small_weights(p, small):
    w = {nm: p[nm] for nm in SMALL_NAMES}
    w["q_head_norm"] = jnp.pad(p["q_head_norm"], ((0, 0), (0, D_QKP - D_QK)))
    w["k_head_norm"] = jnp.pad(p["k_head_norm"], ((0, 0), (0, D_QKP - D_QK)))
    w["conv_w"] = small[:, N_META:N_META + 2, :].reshape(N_SHARD, CONV_K, LRU_TILE)
    w["gate_a_w"] = _gate_dense(p["gate_a_w"])
    w["gate_x_w"] = _gate_dense(p["gate_x_w"])
    meta = small[:, 0:N_META, :].transpose(1, 0, 2).reshape(N_META, D_MODEL)
    return w, meta


def _full_weights(p, gathered, small):
    w, meta = _small_weights(p, small)
    w.update({nm: _weight_from(nm, gathered[nm]) for nm in BIG_NAMES})
    return w, meta


def _shard_grad(nm, g):
    if nm == "w_in":
        lru = g[Z_MLA:].reshape(N_LRU_TILES, 2, LRU_TILE, D_MODEL).transpose(1, 0, 2, 3).reshape(2 * LRU_W, D_MODEL)
        return jnp.concatenate([g[0:Z_KR + D_ROPE], lru], axis=0).reshape(N_SHARD, IN_WIDTH // N_SHARD, D_MODEL)
    if nm == "w_uq":
        return g.reshape(HEADS, D_QKP, Q_RANK)[:, 0:D_QK, :]
    if nm in ("w_uk", "w_uv"):
        return g.reshape(KV_RANK, HEADS, D_NOPE).transpose(1, 0, 2)
    if nm == "w_out":
        return g.reshape(N_SHARD, D_MODEL // N_SHARD, D_MODEL)
    return g


def _shard_grads(g):
    return {nm: _shard_grad(nm, g[nm]) for nm in BIG_NAMES}


GATHER_AT = {"ffn1_norm": ["ffn1_w_gate", "ffn1_w_up"],
             "ffn1_up": ["ffn1_w_down", "w_in", "w_uq", "w_uk", "w_uv", "w_out"],
             "attn_fwd": ["ffn2_w_down", "ffn2_w_gate"], "lru_fwd": ["ffn2_w_up"]}
PAIR_AT = [("ffn2_din", ["ffn2_w_gate", "ffn2_w_up", "ffn2_w_down"]),
           ("mix_din", ["w_out", "w_uq", "w_uk", "w_uv", "w_in"]),
           ("ffn1_dwg", ["ffn1_w_down"]), ("ffn1_dwu", ["ffn1_w_gate"]), ("ffn1_din_a", ["ffn1_w_up"])]
CHIPS_AT = [("attn_bwd", ["ffn2_w_down", "ffn2_w_gate"]), ("mla_prep_bwd", ["ffn2_w_up"]),
            ("ffn1_dact", ["w_out", "w_uq", "w_uk", "w_uv", "w_in"]),
            ("ffn1_dwu", ["ffn1_w_down"]), ("ffn1_din_a", ["ffn1_w_gate"]), ("ffn1_din_b", ["ffn1_w_up"])]
SHARE_EARLY_GROUPS, SHARE_EARLY_AT = 3, "ffn1_dwd"
SMALL_EARLY_AT = "mix_din"


def _same_shape_groups(names):
    return [[nm for nm in grp if nm in names] for grp in BIG_GROUPS if any(nm in names for nm in grp)]


class _Sched:
    def __init__(self, place, w, slots):
        self.place, self.w, self.slots = place, w, slots
        self.g = None
        self.sharded, self.from_pair, self.chip_bf16, self.from_chips = {}, {}, {}, {}
        self.early = self.early_all = None
        self.shared = {}

    def host(self, stage):
        comms = []
        if stage in GATHER_AT:
            comms.append(self.gather(GATHER_AT[stage]))
        comms += [self.chips(names) for at, names in CHIPS_AT if at == stage]
        comms += [self.pair(names) for at, names in PAIR_AT if at == stage]
        if stage == SMALL_EARLY_AT:
            comms.append(self.small_early())
        if stage == SHARE_EARLY_AT:
            comms.append(self.share_early())
        return _join_comms(comms)

    def small_early(self):
        self.early = _early_pack(self.g)

        def deliver(outs):
            self.early_all = outs[0]
            return outs

        return _small_comm(self.early, deliver)

    def gather(self, names):
        def deliver(outs):
            self.w.update({nm: _weight_from(nm, o) for nm, o in zip(names, outs)})
            return outs

        return _gather_comm([self.slots[nm] for nm in names], deliver)

    def pair(self, names):
        self.sharded.update({nm: _shard_grad(nm, self.g[nm]) for nm in names})

        def deliver(outs):
            self.from_pair.update(zip(names, outs))
            for grp in _same_shape_groups(names):
                sums = _pair_sum_call("pair_sum_" + grp[0], self.place, [self.sharded[nm] for nm in grp],
                                      [self.from_pair[nm] for nm in grp])
                self.chip_bf16.update(zip(grp, sums))
            return outs

        return _reduce_pair_comm([self.sharded[nm] for nm in names], deliver)

    def chips(self, names):
        def deliver(outs):
            self.from_chips.update(zip(names, outs))
            return outs

        return _reduce_chips_comm([self.chip_bf16[nm] for nm in names], deliver)

    def chip_sums(self, names, comm=None):
        out = {}
        for grp in _same_shape_groups(names):
            sums = _chip_sum_call("chip_sum_" + grp[0], self.place, [self.sharded[nm] for nm in grp],
                                  [self.from_pair[nm] for nm in grp], [self.from_chips[nm] for nm in grp], comm)
            comm = None
            out.update(zip(grp, sums))
        return out

    def share_early(self):
        names = [nm for at, grp in CHIPS_AT[:SHARE_EARLY_GROUPS] for nm in grp]
        mine = self.chip_sums(names)
        return _share_pair_comm([mine[nm] for nm in names], lambda o: self.shared.update(zip(names, o)))


def kernel(x, meta_tokens, ffn1_norm, ffn1_w_gate, ffn1_w_up, ffn1_w_down, mix_norm, w_in, q_latent_norm, w_uq, kv_latent_norm, w_uk, w_uv, q_head_norm, k_head_norm, conv_w, conv_b, gate_a_w, gate_a_b, gate_x_w, gate_x_b, lru_lambda, attn_out_norm, lru_out_norm, w_out, ffn2_norm, ffn2_w_gate, ffn2_w_up, ffn2_w_down, final_norm, loss_target, m_meta_tokens, m_ffn1_norm, m_ffn1_w_gate, m_ffn1_w_up, m_ffn1_w_down, m_mix_norm, m_w_in, m_q_latent_norm, m_w_uq, m_kv_latent_norm, m_w_uk, m_w_uv, m_q_head_norm, m_k_head_norm, m_conv_w, m_conv_b, m_gate_a_w, m_gate_a_b, m_gate_x_w, m_gate_x_b, m_lru_lambda, m_attn_out_norm, m_lru_out_norm, m_w_out, m_ffn2_norm, m_ffn2_w_gate, m_ffn2_w_up, m_ffn2_w_down, m_final_norm, v_meta_tokens, v_ffn1_norm, v_ffn1_w_gate, v_ffn1_w_up, v_ffn1_w_down, v_mix_norm, v_w_in, v_q_latent_norm, v_w_uq, v_kv_latent_norm, v_w_uk, v_w_uv, v_q_head_norm, v_k_head_norm, v_conv_w, v_conv_b, v_gate_a_w, v_gate_a_b, v_gate_x_w, v_gate_x_b, v_lru_lambda, v_attn_out_norm, v_lru_out_norm, v_w_out, v_ffn2_norm, v_ffn2_w_gate, v_ffn2_w_up, v_ffn2_w_down, v_final_norm):
    args = locals()
    p = {nm: args[nm] for nm in WEIGHT_NAMES}
    mom = {nm: args["m_" + nm] for nm in WEIGHT_NAMES}
    var = {nm: args["v_" + nm] for nm in WEIGHT_NAMES}
    nb, seq, d = x.shape
    lp = CHUNK + seq
    xi, yi, ci = lax.axis_index("x"), lax.axis_index("y"), lax.axis_index("c")
    chip = 2 * xi + yi

    place = jnp.stack([chip, ci]).astype(jnp.int32)
    p2 = {nm: _to2d(nm, p[nm]) for nm in BIG_NAMES}
    m2 = {nm: _to2d(nm, mom[nm]) for nm in BIG_NAMES}
    v2 = {nm: _to2d(nm, var[nm]) for nm in BIG_NAMES}

    slots = {}
    small_shard = jnp.concatenate(
        [meta_tokens, conv_w[0].reshape(2, 2 * LRU_TILE), jnp.zeros((14, 2 * LRU_TILE), F32)], axis=0)
    small_slots = lax.dynamic_update_slice(jnp.zeros((N_SHARD,) + small_shard.shape, F32), small_shard[None],
                                           (chip, 0, 0))
    first = []
    comm = _gather_comm([small_slots], first.extend)
    for grp in BIG_GROUPS:
        for nm, buf in zip(grp, _cast_call("cast_" + grp[0], place, [p2[nm] for nm in grp], comm)):
            slots[nm] = buf
        comm = None
    w, meta = _small_weights(p, first[0])
    sched = _Sched(place, w, slots)

    h0 = jnp.concatenate(
        [jnp.zeros((nb, PAD_ROWS, d), F32), jnp.broadcast_to(meta[None], (nb, N_META, d)), x], axis=1)
    target = jnp.pad(loss_target, ((0, 0), (CHUNK, 0), (0, 0)))
    loss_part, dh0, g = _local_step(h0.reshape(nb * lp, d), target.reshape(nb * lp, d), w, nb, lp, sched)
    dh0 = dh0.reshape(nb, lp, d)
    grad_x = dh0[:, CHUNK:, :]

    late = jnp.concatenate([g["ffn1_norm"], g["mix_norm"], jnp.zeros((6, D_MODEL), F32),
                            jnp.sum(dh0[:, PAD_ROWS:CHUNK, :], axis=0)], axis=0)
    shared = sched.shared
    rest = [nm for at, names in CHIPS_AT[SHARE_EARLY_GROUPS:] if at is not None for nm in names]
    last = [nm for at, names in CHIPS_AT if at is None for nm in names]
    late_box = {}
    mine = sched.chip_sums(rest, _small_comm(late, lambda o: late_box.update(all=o[0])))
    share = _share_pair_comm([mine[nm] for nm in rest], lambda o: shared.update(zip(rest, o)))
    _comm_call("share_pair", _join_comms([share, sched.chips(last) if last else None]))
    if last:
        mine = sched.chip_sums(last)
        _comm_call("share_last", _share_pair_comm([mine[nm] for nm in last], lambda o: shared.update(zip(last, o))))
    late_all = late_box["all"]
    small_like = {nm: p[nm] for nm in SMALL_NAMES + ["gate_a_w", "gate_x_w"]}

    def state(t):
        return ([t[nm] for nm in SMALL_NAMES], t["gate_a_w"].reshape(32, D_MODEL), t["gate_x_w"].reshape(32, D_MODEL))

    me = (4 * xi + 2 * yi + ci).astype(jnp.int32).reshape(1)
    (gsum, dsm, msm, vsm), rows = _small_update_call(me, sched.early_all, sched.early, late_all, late,
                                                     [state(p), state(mom), state(var)])
    grads = _unpack_small(gsum, rows[0], small_like)
    delta = _unpack_small(dsm, rows[1], small_like)
    new_m = _unpack_small(msm, rows[2], small_like)
    new_v = _unpack_small(vsm, rows[3], small_like)
    loss = gsum[ROW_LOSS, 0]
    gmeta = gsum[ROW_META:ROW_META + N_META].reshape(N_META, N_SHARD, D_MODEL // N_SHARD)
    grads["meta_tokens"] = lax.dynamic_index_in_dim(gmeta, chip, axis=1, keepdims=False)
    gconv = gsum[ROW_CONV_W:ROW_CONV_W + 2].reshape(CONV_K, N_SHARD, LRU_TILE)
    grads["conv_w"] = lax.dynamic_index_in_dim(gconv, chip, axis=1, keepdims=False)[None]
    for nm in ("meta_tokens", "conv_w"):
        delta[nm], new_m[nm], new_v[nm] = _adamw_call("adamw_" + nm, p[nm], grads[nm], mom[nm], var[nm])

    for names in BIG_GROUPS:
        res = _adamw_group_call("adamw_" + names[0], [p2[nm] for nm in names], [shared[nm] for nm in names],
                                [m2[nm] for nm in names], [v2[nm] for nm in names])
        for nm, (gg, dd, mm, vv) in zip(names, res):
            grads[nm], delta[nm], new_m[nm], new_v[nm] = (_from2d(nm, t) for t in (gg, dd, mm, vv))

    return (loss, grad_x, *[grads[nm] for nm in WEIGHT_NAMES], *[delta[nm] for nm in WEIGHT_NAMES],
            *[new_m[nm] for nm in WEIGHT_NAMES], *[new_v[nm] for nm in WEIGHT_NAMES])
```

```python
import functools
import math

import jax
import jax.numpy as jnp
import numpy as np
from jax import lax
from jax.experimental import pallas as pl
from jax.experimental.pallas import tpu as pltpu

F32 = jnp.float32
BF16 = jnp.bfloat16
MESH = pl.DeviceIdType.MESH

D_MODEL = 1024
N_META = 16
CHUNK = 64
PAD_ROWS = CHUNK - N_META
HEADS = 4
D_NOPE = 128
D_ROPE = 64
D_QK = D_NOPE + D_ROPE
D_QKP = 256
D_V = 128
KV_RANK = 256
Q_RANK = 384
MLA_W = HEADS * D_V
LRU_W = 512
LRU_TILE = 128
N_LRU_TILES = LRU_W // LRU_TILE
CONV_K = 4
C_RGLRU = 8.0
ROPE_THETA = 10000.0
D_FF = 2816
N_SHARD = 4
EPS = 1e-6
NEG_INF = -1e30
Z_KR = Q_RANK + KV_RANK
Z_MLA = Z_KR + 128
Z_U = Z_MLA
Z_G = Z_U + LRU_W
Z_W = Z_G + LRU_W
IN_WIDTH = Q_RANK + KV_RANK + D_ROPE + 2 * LRU_W

ADAM_LR = 0.001
ADAM_B1 = 0.9
ADAM_B2 = 0.999
ADAM_EPS = 1e-08
ADAM_WD = 0.01
ADAM_STEP = 10

VMEM_LIMIT_BYTES = 56 * 1024 * 1024
MATMUL_ROWS = 1408
ELEMENTWISE_ROWS = 512
SMALL_ROWS = 104
SMALL_ADAM_ROWS = 80


def _params(sem):
    return pltpu.CompilerParams(dimension_semantics=sem, vmem_limit_bytes=VMEM_LIMIT_BYTES)


def _resident(shape):
    return pl.BlockSpec(tuple(shape), lambda i: (0,) * len(shape), pipeline_mode=pl.Buffered(1))


def _row_tile(rows, target):
    best = 16
    for t in range(16, min(rows, target) + 1, 16):
        if rows % t == 0:
            best = t
    return best


def _col_tile(cols, target):
    best = cols
    for t in range(128, min(cols, target) + 1, 128):
        if cols % t == 0:
            best = t
    return best


def _dot(a, b):
    return jnp.dot(a, b, preferred_element_type=F32)


def _dot_nt(a, b):
    return lax.dot_general(a, b, (((1,), (1,)), ((), ())), preferred_element_type=F32)


def _dot_tn(a, b):
    return lax.dot_general(a, b, (((0,), (0,)), ((), ())), preferred_element_type=F32)


def _rms(x, n):
    return lax.rsqrt(jnp.sum(x * x, axis=-1, keepdims=True) * (1.0 / n) + EPS)


def _rms_bwd(dn, nrm, r, n):
    return r * (dn - nrm * (jnp.sum(dn * nrm, axis=-1, keepdims=True) * (1.0 / n)))


def _gelu(x):
    k = math.sqrt(2.0 / math.pi)
    t = jnp.tanh(k * (x + 0.044715 * x * x * x))
    return 0.5 * x * (1.0 + t), t


def _gelu_grad(x, t):
    k = math.sqrt(2.0 / math.pi)
    return 0.5 * (1.0 + t) + 0.5 * x * (1.0 - t * t) * k * (1.0 + 3.0 * 0.044715 * x * x)


def _sigmoid(x):
    return 0.5 + 0.5 * jnp.tanh(0.5 * x)


def _softplus_neg(lam):
    e = jnp.exp(-jnp.abs(lam))
    log1p = jnp.where(e < 0.01, e * (1.0 - e * (0.5 - e * (1.0 / 3 - e * 0.25))), jnp.log(1.0 + e))
    return jnp.maximum(-lam, 0.0) + log1p


def _rope(t, c, s1, s2):
    return t * c + pltpu.roll(t, 96, 1) * s1 + pltpu.roll(t, 32, 1) * s2


def _rope_t(d, c, s1, s2):
    return d * c + pltpu.roll(d * s1, 32, 1) + pltpu.roll(d * s2, 96, 1)


def _rope_tables(lp):
    pos = (np.arange(lp, dtype=np.int32) - PAD_ROWS).astype(np.float32)
    inv_freq = (ROPE_THETA ** (-np.arange(0, D_ROPE // 2, dtype=np.float32) / (D_ROPE // 2))).astype(np.float32)
    ang = (pos[:, None] * inv_freq[None, :]).astype(np.float32).astype(np.float64)
    cos, sin = np.cos(ang).astype(np.float32), np.sin(ang).astype(np.float32)
    z = np.zeros_like(cos)
    return (jnp.asarray(np.concatenate([cos, cos, z, z], 1)), jnp.asarray(np.concatenate([-sin, z, z, z], 1)),
            jnp.asarray(np.concatenate([z, sin, z, z], 1)))


def _rmsnorm_call(name, h, g, tm, comm=None):
    rows, d = h.shape

    def body(h_ref, g_ref, o_ref):
        x = h_ref[...]
        o_ref[...] = (x * _rms(x, d) * g_ref[...]).astype(BF16)

    return _hosted_call(
        body, name=name, grid=(rows // tm,),
        in_specs=[pl.BlockSpec((tm, d), lambda i: (i, 0)), pl.BlockSpec((1, d), lambda i: (0, 0))],
        out_specs=[pl.BlockSpec((tm, d), lambda i: (i, 0))],
        out_shape=[jax.ShapeDtypeStruct((rows, d), BF16)],
        dims=("parallel",), args=(h, g), comm=comm)[0]


def _ffn_up_call(name, u, wg, wu, tm, comm=None):
    rows, d = u.shape
    ns, fs, _ = wg.shape

    def body(u_ref, wg_ref, wu_ref, g_ref, p_ref, a_ref):
        uu = u_ref[...]
        g = _dot_nt(uu, wg_ref[0])
        p = _dot_nt(uu, wu_ref[0])
        g_ref[0] = g.astype(BF16)
        p_ref[0] = p.astype(BF16)
        a_ref[0] = (g * jax.nn.sigmoid(g) * p).astype(BF16)

    wspec = pl.BlockSpec((1, fs, d), lambda s, i: (s, 0, 0))
    ospec = pl.BlockSpec((1, tm, fs), lambda s, i: (s, i, 0))
    oshape = jax.ShapeDtypeStruct((ns, rows, fs), BF16)
    return _hosted_call(
        body, name=name, grid=(ns, rows // tm),
        in_specs=[pl.BlockSpec((tm, d), lambda s, i: (i, 0)), wspec, wspec],
        out_specs=[ospec, ospec, ospec], out_shape=[oshape, oshape, oshape],
        dims=("parallel", "parallel"), args=(u, wg, wu), comm=comm)


def _loss_tail(x, g, t, row, d):
    r = _rms(x, d)
    n = x * r
    err = jnp.where(row >= CHUNK, n * g - t, 0.0)
    dout = err * (1.0 / d)
    dh = _rms_bwd(dout * g, n, r, d)
    dg = jnp.sum(dout * n, axis=0, keepdims=True)
    part = jnp.sum(jnp.sum(err * err, axis=1, keepdims=True), axis=0, keepdims=True) * (0.5 / d)
    return dh, dg, jnp.broadcast_to(part, (1, 128))


def _ffn_down_call(name, a, wd, h, tm, comm=None, next_gain=None):
    rows, d = h.shape
    ns, _, fs = a.shape
    more = next_gain is not None

    def body(a_ref, wd_ref, h_ref, *rest):
        acc = h_ref[...]
        for s in range(ns):
            acc = acc + 0.5 * _dot(a_ref[s], wd_ref[s])
        rest[-2 if more else -1][...] = acc
        if more:
            rest[-1][...] = (acc * _rms(acc, d) * rest[0][...]).astype(BF16)

    full = pl.BlockSpec((tm, d), lambda i: (i, 0))
    res = _hosted_call(
        body, name=name, grid=(rows // tm,),
        in_specs=[pl.BlockSpec((ns, tm, fs), lambda i: (0, i, 0)), _resident((ns, fs, d)), full]
        + ([pl.BlockSpec((1, d), lambda i: (0, 0))] if more else []),
        out_specs=[full] * (2 if more else 1),
        out_shape=[jax.ShapeDtypeStruct((rows, d), F32)] + ([jax.ShapeDtypeStruct((rows, d), BF16)] if more else []),
        dims=("parallel",), args=(a, wd, h) + ((next_gain,) if more else ()), comm=comm)
    return res if more else res[0]


def _ffn_down_loss_call(name, a, wd, h, g, target, lp, tm):
    rows, d = h.shape
    ns, _, fs = a.shape
    tpe = lp // tm

    def body(a_ref, wd_ref, h_ref, g_ref, t_ref, dh_ref, dhb_ref, dg_ref, loss_ref):
        i = pl.program_id(0)
        acc = h_ref[...]
        for s in range(ns):
            acc = acc + 0.5 * _dot(a_ref[s], wd_ref[s])
        row = (i % tpe) * tm + lax.broadcasted_iota(jnp.int32, (tm, 1), 0)
        dh, dg, loss = _loss_tail(acc, g_ref[...], t_ref[...], row, d)
        dh_ref[...] = dh
        dhb_ref[...] = dh.astype(BF16)

        @pl.when(i == 0)
        def _():
            dg_ref[...] = dg
            loss_ref[...] = loss

        @pl.when(i != 0)
        def _():
            dg_ref[...] += dg
            loss_ref[...] += loss

    full = pl.BlockSpec((tm, d), lambda i: (i, 0))
    gspec = pl.BlockSpec((1, d), lambda i: (0, 0))
    return _hosted_call(
        body, name=name, grid=(rows // tm,),
        in_specs=[pl.BlockSpec((ns, tm, fs), lambda i: (0, i, 0)), _resident((ns, fs, d)), full, gspec, full],
        out_specs=[full, full, gspec, pl.BlockSpec((1, 128), lambda i: (0, 0))],
        out_shape=[jax.ShapeDtypeStruct((rows, d), F32), jax.ShapeDtypeStruct((rows, d), BF16),
                   jax.ShapeDtypeStruct((1, d), F32), jax.ShapeDtypeStruct((1, 128), F32)],
        args=(a, wd, h, g, target))


def _mm_call(name, a, bt, tm, out_dtype):
    rows, k = a.shape
    n = bt.shape[0]

    def body(a_ref, b_ref, o_ref):
        o_ref[...] = _dot_nt(a_ref[...], b_ref[...]).astype(out_dtype)

    return pl.pallas_call(
        body, name=name, grid=(rows // tm,),
        in_specs=[pl.BlockSpec((tm, k), lambda i: (i, 0)), pl.BlockSpec((n, k), lambda i: (0, 0))],
        out_specs=pl.BlockSpec((tm, n), lambda i: (i, 0)),
        out_shape=jax.ShapeDtypeStruct((rows, n), out_dtype),
        compiler_params=_params(("parallel",)))(a, bt)


def _mla_heads(z, gql, gkvl, wuq, wuk, wuv):
    cq = z[:, 0:Q_RANK]
    ckv = z[:, Q_RANK:Z_KR]
    kr = z[:, Z_KR:Z_MLA]
    rq = _rms(cq, Q_RANK)
    nq = cq * rq
    cqn = (nq * gql).astype(BF16)
    rkv = _rms(ckv, KV_RANK)
    nkv = ckv * rkv
    ckvn = (nkv * gkvl).astype(BF16)
    qraw = _dot_nt(cqn, wuq)
    knope = _dot(ckvn, wuk)
    v = _dot(ckvn, wuv)
    skr = jnp.sum(kr * kr, axis=-1, keepdims=True)
    heads = []
    for hd in range(HEADS):
        qh = qraw[:, hd * D_QKP:(hd + 1) * D_QKP]
        rqh = lax.rsqrt(jnp.sum(qh * qh, axis=-1, keepdims=True) * (1.0 / D_QK) + EPS)
        kn = knope[:, hd * D_NOPE:(hd + 1) * D_NOPE]
        rkh = lax.rsqrt((jnp.sum(kn * kn, axis=-1, keepdims=True) + skr) * (1.0 / D_QK) + EPS)
        heads.append((qh * rqh, rqh, kn * rkh, kr * rkh, rkh))
    return dict(rq=rq, nq=nq, cqn=cqn, rkv=rkv, nkv=nkv, ckvn=ckvn, v=v, heads=heads)


def _mla_prep_call(z, gql, gkvl, gqh, gkh, wuq, wuk, wuv, tabs, lp, tm):
    rows = z.shape[0]
    tpe = lp // tm

    def body(z_ref, gql_ref, gkvl_ref, gqh_ref, gkh_ref, wuq_ref, wuk_ref, wuv_ref, c_ref, s1_ref, s2_ref,
             q_ref, k_ref, v_ref):
        m = _mla_heads(z_ref[...], gql_ref[...], gkvl_ref[...], wuq_ref[...], wuk_ref[...], wuv_ref[...])
        c, s1, s2 = c_ref[...], s1_ref[...], s2_ref[...]
        gq, gk = gqh_ref[...], gkh_ref[...]
        row = (pl.program_id(0) % tpe) * tm + lax.broadcasted_iota(jnp.int32, (tm, 1), 0)
        spare = (lax.broadcasted_iota(jnp.int32, (1, D_QKP - D_NOPE), 1) == D_ROPE).astype(F32)
        kmask = jnp.where(row < PAD_ROWS, NEG_INF * math.sqrt(D_QK), 0.0) * spare
        for hd in range(HEADS):
            qn, _, knn, krn, _ = m["heads"][hd]
            qg = qn * gq
            q_ref[hd, :, 0:D_NOPE] = qg[:, 0:D_NOPE].astype(BF16)
            q_ref[hd, :, D_NOPE:D_QKP] = (_rope(qg[:, D_NOPE:D_QKP], c, s1, s2) + spare).astype(BF16)
            k_ref[hd, :, 0:D_NOPE] = (knn * gk[:, 0:D_NOPE]).astype(BF16)
            k_ref[hd, :, D_NOPE:D_QKP] = (_rope(krn * gk[:, D_NOPE:D_QKP], c, s1, s2) + kmask).astype(BF16)
            v_ref[hd] = m["v"][:, hd * D_V:(hd + 1) * D_V].astype(BF16)

    def const(shape):
        return pl.BlockSpec(shape, lambda i: tuple(0 for _ in shape))

    tab = pl.BlockSpec((tm, 128), lambda i: (i % tpe, 0))
    return pl.pallas_call(
        body, name="mla_prep", grid=(rows // tm,),
        in_specs=[pl.BlockSpec((tm, Z_MLA), lambda i: (i, 0)), const((1, Q_RANK)), const((1, KV_RANK)),
                  const((1, D_QKP)), const((1, D_QKP)), const((HEADS * D_QKP, Q_RANK)),
                  const((KV_RANK, HEADS * D_NOPE)), const((KV_RANK, HEADS * D_V)), tab, tab, tab],
        out_specs=[pl.BlockSpec((HEADS, tm, D_QKP), lambda i: (0, i, 0)),
                   pl.BlockSpec((HEADS, tm, D_QKP), lambda i: (0, i, 0)),
                   pl.BlockSpec((HEADS, tm, D_V), lambda i: (0, i, 0))],
        out_shape=[jax.ShapeDtypeStruct((HEADS, rows, D_QKP), BF16),
                   jax.ShapeDtypeStruct((HEADS, rows, D_QKP), BF16),
                   jax.ShapeDtypeStruct((HEADS, rows, D_V), BF16)],
        compiler_params=_params(("parallel",)))(z, gql, gkvl, gqh, gkh, wuq, wuk, wuv, *tabs)


Q_BLOCK_ROWS = 528
Q_BLOCK_ROWS_BWD = 192


def _q_block(lp):
    return _row_tile(lp, Q_BLOCK_ROWS)


def _key_end(ext, lp):
    return min(lp, -(-ext // CHUNK) * CHUNK)


def _diag_bias(qb, j0, nk):
    shift = CHUNK.bit_length() - 1
    r = jnp.right_shift(j0 + lax.broadcasted_iota(jnp.int32, (qb, nk), 0), shift)
    c = jnp.right_shift(j0 + lax.broadcasted_iota(jnp.int32, (qb, nk), 1), shift)
    return jnp.where(c <= r, 0.0, NEG_INF)


def _attn_fwd_call(q, k, v, nb, lp, comm=None):
    rows = nb * lp
    qb = _q_block(lp)
    scale = 1.0 / math.sqrt(D_QK)

    def body(q_ref, k_ref, v_ref, o_ref, lse_ref):
        for j in range(lp // qb):
            j0, ext = j * qb, (j + 1) * qb
            kend = _key_end(ext, lp)
            qj = q_ref[0, j0:ext, :]
            sd = _dot_nt(qj, k_ref[0, j0:kend, :]) * scale + _diag_bias(qb, j0, kend - j0)
            mx = jnp.max(sd, axis=-1, keepdims=True)
            if j > 0:
                so = _dot_nt(qj, k_ref[0, 0:j0, :]) * scale
                mx = jnp.maximum(mx, jnp.max(so, axis=-1, keepdims=True))
            pd = jnp.exp(sd - mx)
            l = jnp.sum(pd, axis=-1, keepdims=True)
            o = _dot(pd.astype(BF16), v_ref[0, j0:kend, :])
            if j > 0:
                po = jnp.exp(so - mx)
                l = l + jnp.sum(po, axis=-1, keepdims=True)
                o = o + _dot(po.astype(BF16), v_ref[0, 0:j0, :])
            o_ref[j0:ext, :] = o / l
            lse_ref[0, j0:ext, :] = mx + jnp.log(l)

    return _hosted_call(
        body, name="attn_fwd", grid=(nb, HEADS),
        in_specs=[pl.BlockSpec((1, lp, D_QKP), lambda b, h: (h, b, 0)),
                  pl.BlockSpec((1, lp, D_QKP), lambda b, h: (h, b, 0)),
                  pl.BlockSpec((1, lp, D_V), lambda b, h: (h, b, 0))],
        out_specs=[pl.BlockSpec((lp, D_V), lambda b, h: (b, h)),
                   pl.BlockSpec((1, lp, 1), lambda b, h: (h, b, 0))],
        out_shape=[jax.ShapeDtypeStruct((rows, MLA_W), F32),
                   jax.ShapeDtypeStruct((HEADS, rows, 1), F32)],
        dims=("parallel", "parallel"), args=(q, k, v), comm=comm)


def _attn_bwd_call(q, k, v, o, lse, do, nb, lp, comm=None):
    rows = nb * lp
    qb = _row_tile(lp, Q_BLOCK_ROWS_BWD)
    scale = 1.0 / math.sqrt(D_QK)

    def body(q_ref, k_ref, v_ref, o_ref, lse_ref, do_ref, dq_ref, dk_ref, dv_ref, dk_acc, dv_acc):
        dk_acc[...] = jnp.zeros_like(dk_acc)
        dv_acc[...] = jnp.zeros_like(dv_acc)
        shift = CHUNK.bit_length() - 1
        for j in range(lp // qb):
            j0, ext = j * qb, (j + 1) * qb
            kend = _key_end(ext, lp)
            qj = q_ref[0, j0:ext, :]
            doj = do_ref[j0:ext, :]
            delta = jnp.sum(doj * o_ref[j0:ext, :], axis=-1, keepdims=True)
            dob = doj.astype(BF16)
            kk = k_ref[0, 0:kend, :]
            qchunk = jnp.right_shift(j0 + lax.broadcasted_iota(jnp.int32, (qb, kend), 0), shift)
            kchunk = jnp.right_shift(lax.broadcasted_iota(jnp.int32, (qb, kend), 1), shift)
            s = _dot_nt(qj, kk) * scale - lse_ref[0, j0:ext, :]
            p = jnp.where(kchunk <= qchunk, jnp.exp(s), 0.0)
            dv_acc[0:kend, :] += _dot_tn(p.astype(BF16), dob)
            dp = _dot_nt(dob, v_ref[0, 0:kend, :])
            ds = (p * (dp - delta) * scale).astype(BF16)
            dq_ref[0, j0:ext, :] = _dot(ds, kk).astype(BF16)
            dk_acc[0:kend, :] += _dot_tn(ds, qj)
        dk_ref[0] = dk_acc[...].astype(BF16)
        dv_ref[0] = dv_acc[...].astype(BF16)

    qspec = pl.BlockSpec((1, lp, D_QKP), lambda b, h: (h, b, 0))
    vspec = pl.BlockSpec((1, lp, D_V), lambda b, h: (h, b, 0))
    ospec = pl.BlockSpec((lp, D_V), lambda b, h: (b, h))
    return _hosted_call(
        body, name="attn_bwd", grid=(nb, HEADS),
        in_specs=[qspec, qspec, vspec, ospec, pl.BlockSpec((1, lp, 1), lambda b, h: (h, b, 0)), ospec],
        out_specs=[qspec, qspec, vspec],
        out_shape=[jax.ShapeDtypeStruct((HEADS, rows, D_QKP), BF16),
                   jax.ShapeDtypeStruct((HEADS, rows, D_QKP), BF16),
                   jax.ShapeDtypeStruct((HEADS, rows, D_V), BF16)],
        scratch_shapes=[pltpu.VMEM((lp, D_QKP), F32), pltpu.VMEM((lp, D_V), F32)],
        dims=("parallel", "parallel"), args=(q, k, v, o, lse, do), comm=comm)


def _lru_gates(u, cw, cb, wa, ba, wx, bx, lam, lp):
    xc = (cw[3:4, :] * u + cw[2:3, :] * pltpu.roll(u, 1, 0) + cw[1:2, :] * pltpu.roll(u, 2, 0)
          + cw[0:1, :] * pltpu.roll(u, 3, 0) + cb)
    xcb = xc.astype(BF16)
    r = _sigmoid(_dot(xcb, wa) + ba)
    i = _sigmoid(_dot(xcb, wx) + bx)
    sp = _softplus_neg(lam)
    la = -C_RGLRU * r * sp
    a = jnp.exp(la)
    x2 = 2.0 * la
    e2 = a * a
    m2 = jnp.maximum(jnp.where(x2 > -0.01, -x2 * (1.0 + 0.5 * x2), 1.0 - e2), 1e-30)
    rs = lax.rsqrt(m2)
    row = lax.broadcasted_iota(jnp.int32, (lp, LRU_TILE), 0)
    first = row == PAD_ROWS
    valid = row >= PAD_ROWS
    mult_eff = jnp.where(first, 1.0, m2 * rs)
    return dict(xc=xc, xcb=xcb, r=r, i=i, sp=sp, a=a, e2=e2, rs=rs, mult_eff=mult_eff, first=first, valid=valid)


def _scan_rows(a, b, a_s, b_s, out_ref, lp, reverse):
    sub = lax.broadcasted_iota(jnp.int32, (lp, LRU_TILE), 0) & 7
    for dist in (1, 2, 4):
        shift = lp - dist if reverse else dist
        keep = (sub + dist <= 7) if reverse else (sub >= dist)
        a_sh = pltpu.roll(a, shift, 0)
        b_sh = pltpu.roll(b, shift, 0)
        b = jnp.where(keep, a * b_sh + b, b)
        a = jnp.where(keep, a * a_sh, a)
    a_s[...] = a
    b_s[...] = b
    n_groups = lp // 8
    edge = 0 if reverse else 7

    def group(gi, carry):
        r0 = pl.multiple_of(((n_groups - 1 - gi) if reverse else gi) * 8, 8)
        a8 = a_s[pl.ds(r0, 8), :]
        b8 = b_s[pl.ds(r0, 8), :]
        out_ref[pl.ds(r0, 8), :] = a8 * carry + b8
        return a8[edge:edge + 1, :] * carry + b8[edge:edge + 1, :]

    lax.fori_loop(0, n_groups, group, jnp.zeros((1, LRU_TILE), F32), unroll=4)


def _lru_specs(lp):
    seq = lambda col0, stride=1: pl.BlockSpec((lp, LRU_TILE), lambda t, b: (b, col0 + stride * t))
    cw = pl.BlockSpec((1, CONV_K, LRU_TILE), lambda t, b: (t, 0, 0))
    vec = pl.BlockSpec((1, LRU_TILE), lambda t, b: (0, t))
    mat = pl.BlockSpec((1, LRU_TILE, LRU_TILE), lambda t, b: (t, 0, 0))
    return seq, cw, vec, mat


def _lru_fwd_call(z, cw, cb, wa, ba, wx, bx, lam, nb, lp, comm=None):
    rows = nb * lp
    seq, cwspec, vec, mat = _lru_specs(lp)

    def body(u_ref, g_ref, cw_ref, cb_ref, wa_ref, ba_ref, wx_ref, bx_ref, lam_ref, y_ref, hs_ref, a_s, b_s):
        m = _lru_gates(u_ref[...], cw_ref[0], cb_ref[...], wa_ref[0], ba_ref[...], wx_ref[0], bx_ref[...],
                       lam_ref[...], lp)
        a = jnp.where(m["valid"], m["a"], 0.0)
        b = jnp.where(m["valid"], m["mult_eff"] * (m["i"] * m["xc"]), 0.0)
        _scan_rows(a, b, a_s, b_s, hs_ref, lp, reverse=False)
        gl, _ = _gelu(g_ref[...])
        y_ref[...] = hs_ref[...] * gl

    oshape = jax.ShapeDtypeStruct((rows, LRU_W), F32)
    return _hosted_call(
        body, name="lru_fwd", grid=(N_LRU_TILES, nb),
        in_specs=[seq(Z_U // LRU_TILE, 2), seq(Z_U // LRU_TILE + 1, 2), cwspec, vec, mat, vec, mat, vec, vec],
        out_specs=[seq(0), seq(0)], out_shape=[oshape, oshape],
        scratch_shapes=[pltpu.VMEM((lp, LRU_TILE), F32), pltpu.VMEM((lp, LRU_TILE), F32)],
        dims=("parallel", "parallel"), args=(z, z, cw, cb, wa, ba, wx, bx, lam), comm=comm)


def _lru_bwd_call(z, dz, hs, dy, cw, cb, wa, ba, wx, bx, lam, nb, lp, comm=None):
    rows = nb * lp
    seq, cwspec, vec, mat = _lru_specs(lp)

    def body(u_ref, g_ref, hs_ref, dy_ref, cw_ref, cb_ref, wa_ref, ba_ref, wx_ref, bx_ref, lam_ref, dz_in,
             dz_ref, dcw_ref, dcb_ref, dwa_ref, dba_ref, dwx_ref, dbx_ref, dlam_ref, a_s, b_s, d_s):
        du_ref = dz_ref.at[:, 0:LRU_TILE]
        dg_ref = dz_ref.at[:, LRU_TILE:2 * LRU_TILE]
        b_idx = pl.program_id(1)
        u = u_ref[...]
        cw = cw_ref[0]
        wa, wx = wa_ref[0], wx_ref[0]
        lam = lam_ref[...]
        m = _lru_gates(u, cw, cb_ref[...], wa, ba_ref[...], wx, bx_ref[...], lam, lp)
        gate = g_ref[...]
        gl, th = _gelu(gate)
        dy = dy_ref[...]
        hs = hs_ref[...]
        dg_ref[...] = (dy * hs * _gelu_grad(gate, th)).astype(BF16)
        a_eff = jnp.where(m["valid"], m["a"], 0.0)
        _scan_rows(pltpu.roll(a_eff, lp - 1, 0), dy * gl, a_s, b_s, d_s, lp, reverse=True)
        ds = d_s[...]
        xc, r, i = m["xc"], m["r"], m["i"]
        row = lax.broadcasted_iota(jnp.int32, (lp, LRU_TILE), 0)
        da = ds * jnp.where(row >= 1, pltpu.roll(hs, 1, 0), 0.0)
        db = jnp.where(m["valid"], ds, 0.0)
        di = db * m["mult_eff"] * xc
        dxc = db * m["mult_eff"] * i
        live = m["valid"] & jnp.logical_not(m["first"])
        dm = jnp.where(live, db * i * xc, 0.0)
        dla = da * m["a"] - dm * (m["e2"] * m["rs"])
        dr = dla * (-C_RGLRU * m["sp"])
        dsp = jnp.sum(dla * (-C_RGLRU * r), axis=0, keepdims=True)
        dpr = (dr * r * (1.0 - r))
        dpi = (di * i * (1.0 - i))
        dprb, dpib = dpr.astype(BF16), dpi.astype(BF16)
        dxc = dxc + _dot_nt(dprb, wa) + _dot_nt(dpib, wx)
        du = (cw[3:4, :] * dxc + cw[2:3, :] * pltpu.roll(dxc, lp - 1, 0) + cw[1:2, :] * pltpu.roll(dxc, lp - 2, 0)
              + cw[0:1, :] * pltpu.roll(dxc, lp - 3, 0))
        du_ref[...] = jnp.where(m["valid"], du, 0.0).astype(BF16)
        tap = lax.broadcasted_iota(jnp.int32, (CONV_K, LRU_TILE), 0)
        dcw = jnp.zeros((CONV_K, LRU_TILE), F32)
        for kk in range(CONV_K):
            shifted = u if kk == CONV_K - 1 else pltpu.roll(u, CONV_K - 1 - kk, 0)
            dcw = jnp.where(tap == kk, jnp.sum(dxc * shifted, axis=0, keepdims=True), dcw)
        parts = [(dcw_ref, dcw[None]), (dcb_ref, jnp.sum(dxc, axis=0, keepdims=True)[None]),
                 (dwa_ref, _dot_tn(m["xcb"], dprb)[None]), (dba_ref, jnp.sum(dpr, axis=0, keepdims=True)[None]),
                 (dwx_ref, _dot_tn(m["xcb"], dpib)[None]), (dbx_ref, jnp.sum(dpi, axis=0, keepdims=True)[None]),
                 (dlam_ref, (dsp * (-jax.nn.sigmoid(-lam)))[None])]

        @pl.when(b_idx == 0)
        def _():
            for ref, val in parts:
                ref[...] = val

        @pl.when(b_idx != 0)
        def _():
            for ref, val in parts:
                ref[...] += val

    vec3 = pl.BlockSpec((1, 1, LRU_TILE), lambda t, b: (t, 0, 0))
    vshape = jax.ShapeDtypeStruct((N_LRU_TILES, 1, LRU_TILE), F32)
    mshape = jax.ShapeDtypeStruct((N_LRU_TILES, LRU_TILE, LRU_TILE), F32)
    pair = pl.BlockSpec((lp, 2 * LRU_TILE), lambda t, b: (b, Z_U // (2 * LRU_TILE) + t))
    return _hosted_call(
        body, name="lru_bwd", grid=(N_LRU_TILES, nb),
        in_specs=[seq(Z_U // LRU_TILE, 2), seq(Z_U // LRU_TILE + 1, 2), seq(0), seq(0), cwspec, vec, mat, vec, mat,
                  vec, vec, pl.BlockSpec(memory_space=pl.ANY)],
        out_specs=[pair, cwspec, vec3, mat, vec3, mat, vec3, vec3],
        out_shape=[jax.ShapeDtypeStruct(dz.shape, dz.dtype), jax.ShapeDtypeStruct((N_LRU_TILES, CONV_K, LRU_TILE), F32),
                   vshape, mshape, vshape, mshape, vshape, vshape],
        scratch_shapes=[pltpu.VMEM((lp, LRU_TILE), F32)] * 3,
        dims=("parallel", "arbitrary"), args=(z, z, hs, dy, cw, cb, wa, ba, wx, bx, lam, dz), comm=comm,
        aliases={11: 0})


def _mix_out_call(ya, yl, ga, gl, wout, h, next_gain, tm):
    rows, d = h.shape

    def body(ya_ref, yl_ref, ga_ref, gl_ref, w_ref, h_ref, ng_ref, y_ref, o_ref, u_ref):
        a = ya_ref[...]
        l = yl_ref[...]
        an = (a * _rms(a, MLA_W) * ga_ref[...]).astype(BF16)
        ln = (l * _rms(l, LRU_W) * gl_ref[...]).astype(BF16)
        y_ref[:, 0:MLA_W] = an
        y_ref[:, MLA_W:MLA_W + LRU_W] = ln
        out = h_ref[...] + _dot(an, w_ref[0:MLA_W, :]) + _dot(ln, w_ref[MLA_W:MLA_W + LRU_W, :])
        o_ref[...] = out
        u_ref[...] = (out * _rms(out, d) * ng_ref[...]).astype(BF16)

    half = pl.BlockSpec((tm, MLA_W), lambda i: (i, 0))
    g = pl.BlockSpec((1, MLA_W), lambda i: (0, 0))
    full = pl.BlockSpec((tm, d), lambda i: (i, 0))
    return pl.pallas_call(
        body, name="mix_out", grid=(rows // tm,),
        in_specs=[half, half, g, g, pl.BlockSpec((MLA_W + LRU_W, d), lambda i: (0, 0)), full,
                  pl.BlockSpec((1, d), lambda i: (0, 0))],
        out_specs=[full, full, full],
        out_shape=[jax.ShapeDtypeStruct((rows, MLA_W + LRU_W), BF16), jax.ShapeDtypeStruct((rows, d), F32),
                   jax.ShapeDtypeStruct((rows, d), BF16)],
        compiler_params=_params(("parallel",)))(ya, yl, ga, gl, wout, h, next_gain)


def _ffn_dact_call(name, dhb, wd, gate, up, tm, comm=None):
    rows, d = dhb.shape
    ns, fs, _ = wd.shape

    nsub = 2 if tm % 32 == 0 else 1
    sub = tm // nsub

    def body(dh_ref, wd_ref, g_ref, p_ref, dg_ref, dp_ref):
        wd = wd_ref[0]
        for r in range(nsub):
            rs = slice(r * sub, (r + 1) * sub)
            da = (0.5 * _dot_nt(dh_ref[rs, :], wd)).astype(BF16)
            g = g_ref[0, rs, :]
            p = p_ref[0, rs, :]
            sg = jax.nn.sigmoid(g)
            dg_ref[0, rs, :] = (da * p) * (sg * (1.0 + g * (1.0 - sg)))
            dp_ref[0, rs, :] = da * (g * sg)

    aspec = pl.BlockSpec((1, tm, fs), lambda s, i: (s, i, 0))
    oshape = jax.ShapeDtypeStruct((ns, rows, fs), BF16)
    return _hosted_call(
        body, name=name, grid=(ns, rows // tm),
        in_specs=[pl.BlockSpec((tm, d), lambda s, i: (i, 0)), pl.BlockSpec((1, fs, d), lambda s, i: (s, 0, 0)),
                  aspec, aspec],
        out_specs=[aspec, aspec], out_shape=[oshape, oshape],
        dims=("parallel", "parallel"), args=(dhb, wd, gate, up), comm=comm)


def _norm_in_bwd_call(name, pieces, h, g, dres, tm, comm=None, part=(0, 1), prev=None, mix=None):
    rows, d = h.shape
    npc = len(pieces)
    steps = rows // tm // part[1]
    off = part[0] * steps
    n_prev = 0 if prev is None else 2
    n_mix = 0 if mix is None else 5

    def body(*refs):
        d_refs = refs[0:2 * npc:2]
        w_refs = refs[1:2 * npc:2]
        h_ref, g_ref, dres_ref = refs[2 * npc:2 * npc + 3]
        mix_in_refs = refs[2 * npc + 3:2 * npc + 3 + n_mix]
        dh_ref, dhb_ref, dg_ref = refs[2 * npc + 3 + n_mix + n_prev:2 * npc + 6 + n_mix + n_prev]
        mix_out_refs = refs[2 * npc + 6 + n_mix + n_prev:]
        du = jnp.zeros((tm, d), F32)
        for d_ref, w_ref in zip(d_refs, w_refs):
            if len(d_ref.shape) == 3:
                for s in range(d_ref.shape[0]):
                    du = du + _dot(d_ref[s], w_ref[s])
            else:
                du = du + _dot(d_ref[...], w_ref[...])
        x = h_ref[...]
        r = _rms(x, d)
        n = x * r
        dh = dres_ref[...] + _rms_bwd(du * g_ref[...], n, r, d)
        dhb = dh.astype(BF16)
        dh_ref[...] = dh
        dhb_ref[...] = dhb
        sums = [(dg_ref, jnp.sum(du * n, axis=0, keepdims=True))]
        if mix is not None:
            wo_ref, ya_ref, yl_ref, ga_ref, gl_ref = mix_in_refs
            dya_ref, dyl_ref, dga_ref, dgl_ref = mix_out_refs
            dy = _dot_nt(dhb, wo_ref[...])
            for val, gain_ref, lo, out_ref, acc_ref in ((ya_ref[...], ga_ref, 0, dya_ref, dga_ref),
                                                        (yl_ref[...], gl_ref, MLA_W, dyl_ref, dgl_ref)):
                rb = _rms(val, MLA_W)
                nb_ = val * rb
                dyn = dy[:, lo:lo + MLA_W]
                out_ref[...] = _rms_bwd(dyn * gain_ref[...], nb_, rb, MLA_W)
                sums.append((acc_ref, jnp.sum(dyn * nb_, axis=0, keepdims=True)))

        @pl.when(pl.program_id(0) == 0)
        def _():
            for ref, val in sums:
                ref[...] = val

        @pl.when(pl.program_id(0) != 0)
        def _():
            for ref, val in sums:
                ref[...] += val

    in_specs, args = [], []
    for dd, w in pieces:
        if dd.ndim == 3:
            in_specs.append(pl.BlockSpec((dd.shape[0], tm, dd.shape[2]), lambda i: (0, i + off, 0)))
            in_specs.append(_resident(w.shape))
        else:
            in_specs.append(pl.BlockSpec((tm, dd.shape[1]), lambda i: (i + off, 0)))
            in_specs.append(_resident(w.shape))
        args += [dd, w]
    full = pl.BlockSpec((tm, d), lambda i: (i + off, 0))
    gspec = pl.BlockSpec((1, d), lambda i: (0, 0))
    half = pl.BlockSpec((tm, MLA_W), lambda i: (i + off, 0))
    hgain = pl.BlockSpec((1, MLA_W), lambda i: (0, 0))
    mix_in_specs = [] if mix is None else [_resident(mix[0].shape), half, half, hgain, hgain]
    mix_out_specs = [] if mix is None else [half, half, hgain, hgain]
    mix_out_shapes = [] if mix is None else [
        jax.ShapeDtypeStruct((rows, MLA_W), F32), jax.ShapeDtypeStruct((rows, LRU_W), F32),
        jax.ShapeDtypeStruct((1, MLA_W), F32), jax.ShapeDtypeStruct((1, LRU_W), F32)]
    n_in = len(in_specs) + 3 + n_mix
    return _hosted_call(
        body, name=name, grid=(steps,),
        in_specs=in_specs + [full, gspec, full] + mix_in_specs + _any_specs(n_prev),
        out_specs=[full, full, gspec] + mix_out_specs,
        out_shape=[jax.ShapeDtypeStruct((rows, d), F32), jax.ShapeDtypeStruct((rows, d), BF16),
                   jax.ShapeDtypeStruct((1, d), F32)] + mix_out_shapes,
        args=(*args, h, g, dres, *(mix or ()), *(prev or ())), comm=comm,
        aliases={n_in: 0, n_in + 1: 1} if prev is not None else {})


def _wgrad_call(name, a, b, scale=1.0, comm=None, bf16_copy=False):
    a3, b3 = a.ndim == 3, b.ndim == 3
    ns = a.shape[0] if a3 else (b.shape[0] if b3 else 1)
    rows, m = a.shape[-2:]
    n = b.shape[-1]
    tmm = m if a3 else _col_tile(m, 256)

    def body(a_ref, b_ref, *o_refs):
        av = a_ref[0] if a3 else a_ref[...]
        bv = b_ref[0] if b3 else b_ref[...]
        res = _dot_tn(av, bv)
        if scale != 1.0:
            res = res * scale
        for o_ref in o_refs:
            if a3 or b3:
                o_ref[0] = res.astype(o_ref.dtype)
            else:
                o_ref[...] = res.astype(o_ref.dtype)

    aspec = (pl.BlockSpec((1, rows, tmm), lambda s, j: (s, 0, j)) if a3
             else pl.BlockSpec((rows, tmm), lambda s, j: (0, j)))
    bspec = (pl.BlockSpec((1, rows, n), lambda s, j: (s, 0, 0)) if b3
             else pl.BlockSpec((rows, n), lambda s, j: (0, 0)))
    if a3 or b3:
        ospec = pl.BlockSpec((1, tmm, n), lambda s, j: (s, j, 0))
        oshape = jax.ShapeDtypeStruct((ns, m, n), F32)
    else:
        ospec = pl.BlockSpec((tmm, n), lambda s, j: (j, 0))
        oshape = jax.ShapeDtypeStruct((m, n), F32)
    shapes = [oshape] + ([jax.ShapeDtypeStruct(oshape.shape, BF16)] if bf16_copy else [])
    res = _hosted_call(
        body, name=name, grid=(ns, m // tmm), in_specs=[aspec, bspec], out_specs=[ospec] * len(shapes),
        out_shape=shapes, dims=("parallel", "parallel"), args=(a, b), comm=comm)
    return res if bf16_copy else res[0]


def _mla_prep_bwd_call(z, dq, dk, dv, gql, gkvl, gqh, gkh, wuq, wuk, wuv, tabs, lp, tm, comm=None):
    rows = z.shape[0]
    tpe = lp // tm

    def body(z_ref, dq_ref, dk_ref, dv_ref, gql_ref, gkvl_ref, gqh_ref, gkh_ref, wuq_ref, wuk_ref, wuv_ref,
             c_ref, s1_ref, s2_ref, dz_ref, dgql_ref, dgkvl_ref, dgqh_ref, dgkh_ref, dwuq_ref, dwuk_ref, dwuv_ref,
             dqp_ref, dkn_ref, dvv_ref):
        gql, gkvl = gql_ref[...], gkvl_ref[...]
        gq, gk = gqh_ref[...], gkh_ref[...]
        wuq, wuk, wuv = wuq_ref[...], wuk_ref[...], wuv_ref[...]
        m = _mla_heads(z_ref[...], gql, gkvl, wuq, wuk, wuv)
        c, s1, s2 = c_ref[...], s1_ref[...], s2_ref[...]
        dgq = jnp.zeros((1, D_QKP), F32)
        dgk = jnp.zeros((1, D_QKP), F32)
        dkr = jnp.zeros((tm, D_QKP - D_NOPE), F32)
        for hd in range(HEADS):
            qn, rqh, knn, krn, rkh = m["heads"][hd]
            dqg = jnp.concatenate([dq_ref[hd, :, 0:D_NOPE].astype(F32),
                                   _rope_t(dq_ref[hd, :, D_NOPE:D_QKP].astype(F32), c, s1, s2)], axis=1)
            dgq = dgq + jnp.sum(dqg * qn, axis=0, keepdims=True)
            dqn = dqg * gq
            dqr = rqh * (dqn - qn * (jnp.sum(dqn * qn, axis=-1, keepdims=True) * (1.0 / D_QK)))
            dqp_ref[:, hd * D_QKP:(hd + 1) * D_QKP] = dqr.astype(BF16)
            kn_full = jnp.concatenate([knn, krn], axis=1)
            dkg = jnp.concatenate([dk_ref[hd, :, 0:D_NOPE].astype(F32),
                                   _rope_t(dk_ref[hd, :, D_NOPE:D_QKP].astype(F32), c, s1, s2)], axis=1)
            dgk = dgk + jnp.sum(dkg * kn_full, axis=0, keepdims=True)
            dkn = dkg * gk
            dkraw = rkh * (dkn - kn_full * (jnp.sum(dkn * kn_full, axis=-1, keepdims=True) * (1.0 / D_QK)))
            dkn_ref[:, hd * D_NOPE:(hd + 1) * D_NOPE] = dkraw[:, 0:D_NOPE].astype(BF16)
            dkr = dkr + dkraw[:, D_NOPE:D_QKP]
            dvv_ref[:, hd * D_V:(hd + 1) * D_V] = dv_ref[hd]
        dcqn = _dot(dqp_ref[...], wuq)
        dckvn = _dot_nt(dkn_ref[...], wuk) + _dot_nt(dvv_ref[...], wuv)
        dz_ref[:, 0:Q_RANK] = _rms_bwd(dcqn * gql, m["nq"], m["rq"], Q_RANK).astype(BF16)
        dz_ref[:, Q_RANK:Z_KR] = _rms_bwd(dckvn * gkvl, m["nkv"], m["rkv"], KV_RANK).astype(BF16)
        dz_ref[:, Z_KR:Z_MLA] = dkr.astype(BF16)
        parts = [(dgql_ref, jnp.sum(dcqn * m["nq"], axis=0, keepdims=True)),
                 (dgkvl_ref, jnp.sum(dckvn * m["nkv"], axis=0, keepdims=True)), (dgqh_ref, dgq), (dgkh_ref, dgk),
                 (dwuq_ref, _dot_tn(dqp_ref[...], m["cqn"])), (dwuk_ref, _dot_tn(m["ckvn"], dkn_ref[...])),
                 (dwuv_ref, _dot_tn(m["ckvn"], dvv_ref[...]))]

        @pl.when(pl.program_id(0) == 0)
        def _():
            for ref, val in parts:
                ref[...] = val

        @pl.when(pl.program_id(0) != 0)
        def _():
            for ref, val in parts:
                ref[...] += val

    def const(shape):
        return pl.BlockSpec(shape, lambda i: tuple(0 for _ in shape))

    tab = pl.BlockSpec((tm, 128), lambda i: (i % tpe, 0))
    hq = pl.BlockSpec((HEADS, tm, D_QKP), lambda i: (0, i, 0))
    hv = pl.BlockSpec((HEADS, tm, D_V), lambda i: (0, i, 0))

    def rowspec(n):
        return pl.BlockSpec((tm, n), lambda i: (i, 0))

    return _hosted_call(
        body, name="mla_prep_bwd", grid=(rows // tm,),
        in_specs=[rowspec(Z_MLA), hq, hq, hv, const((1, Q_RANK)), const((1, KV_RANK)), const((1, D_QKP)),
                  const((1, D_QKP)), const((HEADS * D_QKP, Q_RANK)), const((KV_RANK, HEADS * D_NOPE)),
                  const((KV_RANK, HEADS * D_V)), tab, tab, tab],
        out_specs=[rowspec(Z_MLA), const((1, Q_RANK)), const((1, KV_RANK)), const((1, D_QKP)), const((1, D_QKP)),
                   const((HEADS * D_QKP, Q_RANK)), const((KV_RANK, HEADS * D_NOPE)), const((KV_RANK, HEADS * D_V))],
        out_shape=[jax.ShapeDtypeStruct((rows, Z_W), BF16),
                   jax.ShapeDtypeStruct((1, Q_RANK), F32), jax.ShapeDtypeStruct((1, KV_RANK), F32),
                   jax.ShapeDtypeStruct((1, D_QKP), F32), jax.ShapeDtypeStruct((1, D_QKP), F32),
                   jax.ShapeDtypeStruct((HEADS * D_QKP, Q_RANK), F32),
                   jax.ShapeDtypeStruct((KV_RANK, HEADS * D_NOPE), F32), jax.ShapeDtypeStruct((KV_RANK, HEADS * D_V), F32)],
        scratch_shapes=[pltpu.VMEM((tm, HEADS * D_QKP), BF16), pltpu.VMEM((tm, HEADS * D_NOPE), BF16),
                        pltpu.VMEM((tm, HEADS * D_V), BF16)],
        args=(z, dq, dk, dv, gql, gkvl, gqh, gkh, wuq, wuk, wuv, *tabs), comm=comm)


def _local_step(h0, target, w, nb, lp, sched=None):
    tm = _row_tile(nb * lp, MATMUL_ROWS)
    te = _row_tile(lp, ELEMENTWISE_ROWS)
    tabs = _rope_tables(lp)
    g = {}
    if sched is None:
        host = lambda stage: None
    else:
        sched.g = g
        host = sched.host

    def ffn_act(tag, u):
        gate, up, act = _ffn_up_call(tag + "_up", u, w[tag + "_w_gate"], w[tag + "_w_up"], tm, host(tag + "_up"))
        return u, gate, up, act

    def ffn_bwd(tag, h, saved, dh, dhb, split, mix=None):
        u, gate, up, act = saved
        dgate, dup = _ffn_dact_call(tag + "_dact", dhb, w[tag + "_w_down"], gate, up, tm, host(tag + "_dact"))
        for nm, call, a, b, scale in (("_w_down", "_dwd", act, dhb, 0.5), ("_w_gate", "_dwg", dgate, u, 1.0),
                                      ("_w_up", "_dwu", dup, u, 1.0)):
            g[tag + nm], g[tag + nm + BF16_COPY] = _wgrad_call(tag + call, a, b, scale, host(tag + call),
                                                               bf16_copy=True)
        pieces = [(dgate, w[tag + "_w_gate"]), (dup, w[tag + "_w_up"])]
        if not split:
            res = _norm_in_bwd_call(tag + "_din", pieces, h, w[tag + "_norm"], dh, te, host(tag + "_din"), mix=mix)
            g[tag + "_norm"] = res[2]
            return (res[0], res[1], *res[3:])
        dh_a, dhb_a, dg_a = _norm_in_bwd_call(tag + "_din_a", pieces, h, w[tag + "_norm"], dh, te,
                                              host(tag + "_din_a"), part=(0, 2))
        dh_in, dhb_in, dg_b = _norm_in_bwd_call(tag + "_din_b", pieces, h, w[tag + "_norm"], dh, te,
                                                host(tag + "_din_b"), part=(1, 2), prev=(dh_a, dhb_a))
        g[tag + "_norm"] = dg_a + dg_b
        return dh_in, dhb_in

    s1 = ffn_act("ffn1", _rmsnorm_call("ffn1_norm", h0, w["ffn1_norm"], te, host("ffn1_norm")))
    h1, un = _ffn_down_call("ffn1_down", s1[3], w["ffn1_w_down"], h0, te, host("ffn1_down"), w["mix_norm"])
    z = _mm_call("mix_in", un, w["w_in"], tm, F32)
    mla_w = (w["q_latent_norm"], w["kv_latent_norm"], w["q_head_norm"], w["k_head_norm"], w["w_uq"], w["w_uk"],
             w["w_uv"])
    q, k, v = _mla_prep_call(z, *mla_w, tabs, lp, te)
    o, lse = _attn_fwd_call(q, k, v, nb, lp, host("attn_fwd"))
    lru_w = (w["conv_w"], w["conv_b"], w["gate_a_w"], w["gate_a_b"], w["gate_x_w"], w["gate_x_b"], w["lru_lambda"])
    yl, hs = _lru_fwd_call(z, *lru_w, nb, lp, host("lru_fwd"))
    y, h2, u2 = _mix_out_call(o, yl, w["attn_out_norm"], w["lru_out_norm"], w["w_out"], h1, w["ffn2_norm"], te)
    s2 = ffn_act("ffn2", u2)
    dh3, dh3b, g["final_norm"], loss = _ffn_down_loss_call("ffn2_down", s2[3], w["ffn2_w_down"], h2, w["final_norm"],
                                                           target, lp, te)
    g["loss"] = loss

    dh2, dh2b, dya, dyl, g["attn_out_norm"], g["lru_out_norm"] = ffn_bwd(
        "ffn2", h2, s2, dh3, dh3b, False, (w["w_out"], o, yl, w["attn_out_norm"], w["lru_out_norm"]))
    g["w_out"] = _wgrad_call("dw_out", y, dh2b)
    dq, dk, dv = _attn_bwd_call(q, k, v, o, lse, dya, nb, lp, host("attn_bwd"))
    (dz_mla, g["q_latent_norm"], g["kv_latent_norm"], g["q_head_norm"], g["k_head_norm"], g["w_uq"], g["w_uk"],
     g["w_uv"]) = _mla_prep_bwd_call(z, dq, dk, dv, *mla_w, tabs, lp, te, host("mla_prep_bwd"))
    (dz, g["conv_w"], g["conv_b"], g["gate_a_w"], g["gate_a_b"], g["gate_x_w"], g["gate_x_b"],
     g["lru_lambda"]) = _lru_bwd_call(z, dz_mla, hs, dyl, *lru_w, nb, lp, host("lru_bwd"))
    g["w_in"] = _wgrad_call("dw_in", dz, un)
    dh1, dh1b, g["mix_norm"] = _norm_in_bwd_call("mix_din", [(dz, w["w_in"])], h1, w["mix_norm"], dh2, te,
                                                 host("mix_din"))
    dh0 = ffn_bwd("ffn1", h0, s1, dh1, dh1b, True)[0]
    return loss, dh0, g


def _place():
    x, y, c = lax.axis_index("x"), lax.axis_index("y"), lax.axis_index("c")
    return x, y, c, [(1 - x, y), (x, 1 - y), (1 - x, 1 - y)]


def _any_specs(n):
    return [pl.BlockSpec(memory_space=pl.ANY)] * n


def _remote(src, dst, sems, k, dev):
    send_sems, recv_sems, base = sems
    return pltpu.make_async_remote_copy(src_ref=src, dst_ref=dst, send_sem=send_sems.at[base + k],
                                        recv_sem=recv_sems.at[base + k], device_id=dev, device_id_type=MESH)


EW_VMEM_BYTES = 24 * 1024 * 1024


def _fit_rows(rows, cols, blocks):
    return _row_tile(rows, max(16, int(EW_VMEM_BYTES // (8 * blocks)) // cols))


class _Geom:
    def __init__(self, n0, n1, blocks=1.0):
        self.n0, self.n1 = n0, n1
        self.axis = 0 if n0 % 32 == 0 else 1
        self.h0, self.h1 = (n0 // 2, n1) if self.axis == 0 else (n0, n1 // 2)
        self.tr = _fit_rows(self.h0, self.h1, blocks)
        self.nblk = self.h0 // self.tr

    def half_ref(self, ref, lead, idx):
        if self.axis == 0:
            return ref.at[(*lead, pl.ds(idx * self.h0, self.h0))]
        return ref.at[(*lead, slice(None), pl.ds(idx * self.h1, self.h1))]

    def half_block(self, lead, i, idx):
        return (*lead, idx * self.nblk + i, 0) if self.axis == 0 else (*lead, i, idx)


class _Comm:
    def __init__(self, ins, out_shapes, aliases, n_sems, start, finish, deliver):
        self.ins, self.out_shapes, self.aliases, self.n_sems = list(ins), list(out_shapes), dict(aliases), n_sems
        self.start, self.finish, self.deliver = start, finish, deliver

    def scratch(self):
        return [pltpu.SemaphoreType.DMA((self.n_sems,)), pltpu.SemaphoreType.DMA((self.n_sems,))]


def _comm_call(name, comm):
    n_in = len(comm.ins)

    def body(*refs):
        ins, outs, sems = refs[:n_in], refs[n_in:-2], (*refs[-2:], 0)
        comm.start(ins, outs, sems)
        comm.finish(ins, outs, sems)

    res = pl.pallas_call(
        body, name=name, out_shape=comm.out_shapes, in_specs=_any_specs(n_in),
        out_specs=_any_specs(len(comm.out_shapes)), input_output_aliases=comm.aliases,
        scratch_shapes=comm.scratch())(*comm.ins)
    return comm.deliver(list(res))


def _hosted_call(body, *, name, grid, in_specs, out_specs, out_shape, args, scratch_shapes=(), dims=None, comm=None,
                 prefetch=None, aliases=None):
    aliases = dict(aliases or {})
    in_specs, out_specs, out_shape = list(in_specs), list(out_specs), list(out_shape)
    n_pre = 0 if prefetch is None else 1

    def call(fn, in_specs, out_specs, out_shape, scratch, aliases, dims, args):
        if prefetch is None:
            return pl.pallas_call(
                fn, name=name, grid=grid, in_specs=in_specs, out_specs=out_specs, out_shape=out_shape,
                scratch_shapes=scratch, input_output_aliases=aliases, compiler_params=_params(dims))(*args)
        spec = pltpu.PrefetchScalarGridSpec(num_scalar_prefetch=1, grid=grid, in_specs=in_specs, out_specs=out_specs,
                                            scratch_shapes=scratch)
        return pl.pallas_call(
            fn, name=name, grid_spec=spec, out_shape=out_shape,
            input_output_aliases={i + 1: o for i, o in aliases.items()}, compiler_params=_params(dims))(prefetch, *args)

    if comm is None:
        return list(call(body, in_specs, out_specs, out_shape, list(scratch_shapes), aliases,
                         dims or ("arbitrary",) * len(grid), args))
    n_in, n_out, n_ci, n_co = len(in_specs), len(out_specs), len(comm.ins), len(comm.out_shapes)

    def wrapped(*refs):
        pre, refs = refs[:n_pre], refs[n_pre:]
        ins, cins = refs[:n_in], refs[n_in:n_in + n_ci]
        outs = refs[n_in + n_ci:n_in + n_ci + n_out]
        couts = refs[n_in + n_ci + n_out:n_in + n_ci + n_out + n_co]
        scratch, sems = refs[n_in + n_ci + n_out + n_co:-2], (*refs[-2:], 0)
        first = functools.reduce(jnp.logical_and, [pl.program_id(k) == 0 for k in range(len(grid))])
        last = functools.reduce(jnp.logical_and, [pl.program_id(k) == grid[k] - 1 for k in range(len(grid))])

        @pl.when(first)
        def _():
            comm.start(cins, couts, sems)

        body(*pre, *ins, *outs, *scratch)

        @pl.when(last)
        def _():
            comm.finish(cins, couts, sems)

    res = call(wrapped, in_specs + _any_specs(n_ci), out_specs + _any_specs(n_co), out_shape + comm.out_shapes,
               list(scratch_shapes) + comm.scratch(),
               {**aliases, **{n_in + i: n_out + o for i, o in comm.aliases.items()}},
               ("arbitrary",) * len(grid), (*args, *comm.ins))
    comm.deliver(list(res[n_out:]))
    return list(res[:n_out])


def _gather_comm(bufs, deliver):
    n = len(bufs)
    geoms = [_Geom(*b.shape[1:]) for b in bufs]

    def first(outs, sems):
        x, y, c, chips = _place()
        cps = []
        for a in range(n):
            mine = geoms[a].half_ref(outs[a], (2 * x + y,), c)
            cps += [_remote(mine, mine, sems, 6 * a + j, (cx, cy, c)) for j, (cx, cy) in enumerate(chips)]
        return cps

    def start(ins, outs, sems):
        for cp in first(outs, sems):
            cp.start()

    def finish(ins, outs, sems):
        x, y, c, chips = _place()
        sib = (x, y, 1 - c)
        passed = []
        for a in range(n):
            for j, (cx, cy) in enumerate(chips):
                land = geoms[a].half_ref(outs[a], (2 * cx + cy,), c)
                _remote(land, land, sems, 6 * a + j, sib).wait_recv()
                cp = _remote(land, land, sems, 6 * a + 3 + j, sib)
                cp.start()
                passed.append(cp)
        for a in range(n):
            for j, (cx, cy) in enumerate(chips):
                land = geoms[a].half_ref(outs[a], (2 * cx + cy,), 1 - c)
                _remote(land, land, sems, 6 * a + 3 + j, sib).wait_recv()
        for cp in first(outs, sems) + passed:
            cp.wait_send()

    return _Comm(bufs, [jax.ShapeDtypeStruct(b.shape, b.dtype) for b in bufs], {a: a for a in range(n)}, 6 * n,
                 start, finish, deliver)


def _reduce_pair_comm(grads, deliver):
    n = len(grads)
    geoms = [_Geom(*a.shape[1:]) for a in grads]

    def copies(ins, outs, sems):
        x, y, c, _ = _place()
        return [_remote(geoms[a].half_ref(ins[a], (slice(None),), 1 - c), outs[a], sems, a, (x, y, 1 - c))
                for a in range(n)]

    def start(ins, outs, sems):
        for cp in copies(ins, outs, sems):
            cp.start()

    def finish(ins, outs, sems):
        cps = copies(ins, outs, sems)
        for cp in cps:
            cp.wait_recv()
        for cp in cps:
            cp.wait_send()

    shapes = [jax.ShapeDtypeStruct((N_SHARD, g.h0, g.h1), a.dtype) for a, g in zip(grads, geoms)]
    return _Comm(grads, shapes, {}, n, start, finish, deliver)


def _reduce_chips_comm(parts, deliver):
    n = len(parts)

    def copies(ins, outs, sems):
        x, y, c, chips = _place()
        return [_remote(ins[a].at[2 * cx + cy], outs[a].at[j], sems, 3 * a + j, (cx, cy, c))
                for a in range(n) for j, (cx, cy) in enumerate(chips)]

    def start(ins, outs, sems):
        for cp in copies(ins, outs, sems):
            cp.start()

    def finish(ins, outs, sems):
        cps = copies(ins, outs, sems)
        for cp in cps:
            cp.wait_recv()
        for cp in cps:
            cp.wait_send()

    shapes = [jax.ShapeDtypeStruct((3,) + a.shape[1:], a.dtype) for a in parts]
    return _Comm(parts, shapes, {}, 3 * n, start, finish, deliver)


def _share_pair_comm(bufs, deliver):
    n = len(bufs)
    geoms = [_Geom(*b.shape) for b in bufs]

    def copies(outs, sems):
        x, y, c, _ = _place()
        cps = []
        for a in range(n):
            mine = geoms[a].half_ref(outs[a], (), c)
            cps.append(_remote(mine, mine, sems, a, (x, y, 1 - c)))
        return cps

    def start(ins, outs, sems):
        for cp in copies(outs, sems):
            cp.start()

    def finish(ins, outs, sems):
        x, y, c, _ = _place()
        for a in range(n):
            land = geoms[a].half_ref(outs[a], (), 1 - c)
            _remote(land, land, sems, a, (x, y, 1 - c)).wait_recv()
        for cp in copies(outs, sems):
            cp.wait_send()

    return _Comm(bufs, [jax.ShapeDtypeStruct(b.shape, b.dtype) for b in bufs], {a: a for a in range(n)}, n,
                 start, finish, deliver)


def _small_comm(pack, deliver):
    r, d = pack.shape

    def copies(ins, outs, sems):
        x, y, c, _ = _place()
        cps = []
        for k in range(1, 8):
            peer = (x ^ ((k >> 2) & 1), y ^ ((k >> 1) & 1), c ^ (k & 1))
            cps.append(_remote(ins[0], outs[0].at[4 * x + 2 * y + c], sems, k - 1, peer))
        return cps

    def start(ins, outs, sems):
        for cp in copies(ins, outs, sems):
            cp.start()

    def finish(ins, outs, sems):
        cps = copies(ins, outs, sems)
        for cp in cps:
            cp.wait_recv()
        for cp in cps:
            cp.wait_send()

    return _Comm([pack], [jax.ShapeDtypeStruct((8, r, d), pack.dtype)], {}, 7, start, finish, deliver)


def _join_comms(comms):
    comms = [c for c in comms if c is not None]
    if len(comms) <= 1:
        return comms[0] if comms else None
    ins, out_shapes, aliases, spans, n_sems = [], [], {}, [], 0
    for c in comms:
        aliases.update({len(ins) + i: len(out_shapes) + o for i, o in c.aliases.items()})
        spans.append((len(ins), len(ins) + len(c.ins), len(out_shapes), len(out_shapes) + len(c.out_shapes), n_sems))
        ins += c.ins
        out_shapes += c.out_shapes
        n_sems += c.n_sems

    def run(which):
        def go(all_ins, all_outs, sems):
            for c, (i0, i1, o0, o1, base) in zip(comms, spans):
                getattr(c, which)(all_ins[i0:i1], all_outs[o0:o1], (sems[0], sems[1], sems[2] + base))
        return go

    def deliver(outs):
        for c, (_, _, o0, o1, _) in zip(comms, spans):
            c.deliver(outs[o0:o1])
        return outs

    return _Comm(ins, out_shapes, aliases, n_sems, run("start"), run("finish"), deliver)


def _ew_call(name, fn, ins, out_dtypes):
    shape = ins[0].shape
    cols = shape[-1]
    rows = 1
    for s_ in shape[:-1]:
        rows *= s_
    ins2 = [a.reshape(rows, cols) for a in ins]
    tr = rows
    for t in range(16, min(rows, max(16, (1 << 19) // cols)) + 1, 16):
        if rows % t == 0:
            tr = t
    no = len(out_dtypes)

    def body(*refs):
        outs = fn(*[r[...] for r in refs[:len(ins2)]])
        for ref, val in zip(refs[len(ins2):], outs):
            ref[...] = val.astype(ref.dtype)

    spec = pl.BlockSpec((tr, cols), lambda i: (i, 0))
    res = pl.pallas_call(
        body, name=name, grid=(rows // tr,), in_specs=[spec] * len(ins2), out_specs=[spec] * no,
        out_shape=[jax.ShapeDtypeStruct((rows, cols), dt) for dt in out_dtypes],
        compiler_params=_params(("parallel",)))(*ins2)
    return [r.reshape(shape) for r in res]


def _adamw_math(w, g, m, v):
    m = ADAM_B1 * m + (1.0 - ADAM_B1) * g
    v = ADAM_B2 * v + (1.0 - ADAM_B2) * (g * g)
    m_hat = m / (1.0 - ADAM_B1 ** ADAM_STEP)
    v_hat = v / (1.0 - ADAM_B2 ** ADAM_STEP)
    delta = -ADAM_LR * (m_hat / (jnp.sqrt(v_hat) + ADAM_EPS) + ADAM_WD * w)
    return delta, m, v


def _adamw_call(name, w, g, m, v):
    return _ew_call(name, _adamw_math, [w, g, m, v], [F32, F32, F32])


def _tiled_call(name, fn, place, grid, in_items, out_items, comm=None):
    ni = len(in_items)

    def body(place_ref, *refs):
        vals = fn(*[r[...] for r in refs[:ni]])
        for ref, val in zip(refs[ni:], vals):
            ref[...] = val.astype(ref.dtype)

    return _hosted_call(
        body, name=name, grid=grid, in_specs=[pl.BlockSpec(blk, imap) for _, blk, imap in in_items],
        out_specs=[pl.BlockSpec(blk, imap) for _, _, blk, imap in out_items],
        out_shape=[jax.ShapeDtypeStruct(shp, dt) for shp, dt, _, _ in out_items],
        args=[a for a, _, _ in in_items], prefetch=place, comm=comm)


def _cast_call(name, place, shards, comm=None):
    n0, n1 = shards[0].shape
    tr = _fit_rows(n0, n1, 1.5 * len(shards))
    ins = [(a, (tr, n1), lambda i, p: (i, 0)) for a in shards]
    outs = [((N_SHARD, n0, n1), BF16, (1, tr, n1), lambda i, p: (p[0], i, 0)) for _ in shards]
    return _tiled_call(name, lambda *v: [x[None] for x in v], place, (n0 // tr,), ins, outs, comm)


def _pair_sum_call(name, place, fulls, gots):
    k = len(fulls)
    g = _Geom(*fulls[0].shape[1:], blocks=2.5 * k)
    blk = (1, g.tr, g.h1)
    ins = [(a, blk, lambda s, i, p: g.half_block((s,), i, p[1])) for a in fulls]
    ins += [(a, blk, lambda s, i, p: (s, i, 0)) for a in gots]
    outs = [((N_SHARD, g.h0, g.h1), BF16, blk, lambda s, i, p: (s, i, 0)) for _ in fulls]
    return _tiled_call(name, lambda *v: [v[j] + v[k + j] for j in range(k)], place, (N_SHARD, g.nblk), ins, outs)


def _chip_sum_call(name, place, fulls, gots, recvs, comm=None):
    k = len(fulls)
    g = _Geom(*fulls[0].shape[1:], blocks=4.5 * k)
    blk = (1, g.tr, g.h1)
    ins = [(a, blk, lambda i, p: g.half_block((p[0],), i, p[1])) for a in fulls]
    ins += [(a, blk, lambda i, p: (p[0], i, 0)) for a in gots]
    ins += [(a, (3, g.tr, g.h1), lambda i, p: (0, i, 0)) for a in recvs]
    outs = [((g.n0, g.n1), F32, (g.tr, g.h1), lambda i, p: g.half_block((), i, p[1])) for _ in fulls]

    def fn(*v):
        res = []
        for j in range(k):
            r = v[2 * k + j].astype(F32)
            res.append(v[j][0] + v[k + j][0] + r[0] + r[1] + r[2])
        return res

    return _tiled_call(name, fn, place, (g.nblk,), ins, outs, comm)


def _adamw_group_call(name, ws, gs, ms, vs, comm=None):
    k = len(ws)
    n0, n1 = ws[0].shape
    tr = _fit_rows(n0, n1, 8 * k)
    spec = pl.BlockSpec((tr, n1), lambda i: (i, 0))

    def body(*refs):
        for j in range(k):
            g = refs[k + j][...]
            delta, m, vv = _adamw_math(refs[j][...], g, refs[2 * k + j][...], refs[3 * k + j][...])
            for ref, val in zip(refs[4 * k + 4 * j:4 * k + 4 * j + 4], (g, delta, m, vv)):
                ref[...] = val

    flat = _hosted_call(
        body, name=name, grid=(n0 // tr,), in_specs=[spec] * (4 * k), out_specs=[spec] * (4 * k),
        out_shape=[jax.ShapeDtypeStruct((n0, n1), F32)] * (4 * k), dims=("parallel",),
        args=(*ws, *gs, *ms, *vs), comm=comm)
    return [flat[4 * j:4 * j + 4] for j in range(k)]


def _small_update_call(me, early, own_early, late, own_late, states):
    nd, r, d = early.shape
    widths = [a.shape[1] for a in states[0][0]]
    nw = len(widths)
    n_state = nw + 2

    def body(me_ref, e_ref, oe_ref, l_ref, ol_ref, *refs):
        state_refs, refs = refs[:3 * n_state], refs[3 * n_state:]
        (gs_ref, d_ref, nm_ref, nv_ref), rows, packs = refs[:4], refs[4:4 + 4 * nw], refs[4 + 4 * nw:]
        for kind, pack in enumerate(packs):
            srefs = state_refs[kind * n_state:(kind + 1) * n_state]
            pack[...] = jnp.zeros_like(pack)
            for k, width in enumerate(widths):
                pack[k:k + 1, 0:width] = srefs[k][...]
            pack[ROW_GATE_A:ROW_GATE_A + 32, :] = srefs[nw][...]
            pack[ROW_GATE_X:ROW_GATE_X + 32, :] = srefs[nw + 1][...]
        w_ref, m_ref, v_ref = packs
        mine = me_ref[0]

        def total(g_ref, own_ref):
            acc = None
            for k in range(nd):
                part = jnp.where(mine == k, own_ref[...], g_ref[k])
                acc = part if acc is None else acc + part
            return acc

        gs = total(e_ref, oe_ref)
        ls = total(l_ref, ol_ref)
        gs_ref[...] = gs
        first = gs[0:8] + ls[0:8]
        gs_ref[0:8, :] = first
        gs_ref[ROW_META:ROW_META + N_META, :] = gs[ROW_META:ROW_META + N_META] + ls[8:8 + N_META]
        grads = jnp.concatenate([first, gs[8:SMALL_ADAM_ROWS]], axis=0)
        delta, m, v = _adamw_math(w_ref[...], grads, m_ref[...], v_ref[...])
        d_ref[...] = delta
        nm_ref[...] = m
        nv_ref[...] = v
        for kind, pack_ref in enumerate((gs_ref, d_ref, nm_ref, nv_ref)):
            for k, width in enumerate(widths):
                rows[kind * nw + k][...] = pack_ref[k:k + 1, 0:width]

    vm = pl.BlockSpec(memory_space=pltpu.VMEM)
    ashape = jax.ShapeDtypeStruct((SMALL_ADAM_ROWS, d), F32)
    row_shapes = [jax.ShapeDtypeStruct((1, width), F32) for _ in range(4) for width in widths]
    flat_states = [a for rows_, ga, gx in states for a in (*rows_, ga, gx)]
    res = pl.pallas_call(
        body, name="small_update", in_specs=[pl.BlockSpec(memory_space=pltpu.SMEM)] + [vm] * (4 + 3 * n_state),
        out_specs=[vm] * (4 + 4 * nw),
        out_shape=[jax.ShapeDtypeStruct((r, d), F32), ashape, ashape, ashape] + row_shapes,
        scratch_shapes=[pltpu.VMEM((SMALL_ADAM_ROWS, d), F32)] * 3,
        compiler_params=pltpu.CompilerParams(vmem_limit_bytes=VMEM_LIMIT_BYTES))(
            me, early, own_early, late, own_late, *flat_states)
    return res[:4], [res[4 + kind * nw:4 + (kind + 1) * nw] for kind in range(4)]


SMALL_NAMES = ["ffn1_norm", "mix_norm", "ffn2_norm", "final_norm", "q_latent_norm", "kv_latent_norm",
               "q_head_norm", "k_head_norm", "conv_b", "gate_a_b", "gate_x_b", "lru_lambda", "attn_out_norm",
               "lru_out_norm"]
ROW_CONV_W = 14
ROW_GATE_A = 16
ROW_GATE_X = 48
ROW_META = 80
ROW_LOSS = 96


def _row(a):
    flat = a.reshape(1, -1)
    return jnp.pad(flat, ((0, 0), (0, D_MODEL - flat.shape[1])))


def _pack_small(t, rows):
    parts = [_row(t[nm]) for nm in SMALL_NAMES]
    parts.append(t["conv_w"].reshape(2, D_MODEL))
    parts.append(t["gate_a_w"].reshape(32, D_MODEL))
    parts.append(t["gate_x_w"].reshape(32, D_MODEL))
    p = jnp.concatenate(parts, axis=0)
    return jnp.pad(p, ((0, rows - p.shape[0]), (0, 0)))


def _early_pack(g):
    gs = {nm: g.get(nm, jnp.zeros((1, D_MODEL), F32)) for nm in SMALL_NAMES}
    gs["q_head_norm"] = g["q_head_norm"][:, 0:D_QK]
    gs["k_head_norm"] = g["k_head_norm"][:, 0:D_QK]
    for nm in ("conv_b", "gate_a_b", "gate_x_b", "lru_lambda"):
        gs[nm] = g[nm].reshape(1, LRU_W)
    gs["conv_w"] = g["conv_w"].transpose(1, 0, 2).reshape(CONV_K, LRU_W)
    gs["gate_a_w"] = _gate_blocks(g["gate_a_w"])
    gs["gate_x_w"] = _gate_blocks(g["gate_x_w"])
    return jnp.concatenate([_pack_small(gs, ROW_META), jnp.zeros((N_META, D_MODEL), F32), _row(g["loss"][:, 0:1]),
                            jnp.zeros((SMALL_ROWS - ROW_LOSS - 1, D_MODEL), F32)], axis=0)


def _unpack_small(p, rows, like):
    out = dict(zip(SMALL_NAMES, rows))
    out["gate_a_w"] = p[ROW_GATE_A:ROW_GATE_A + 32].reshape(like["gate_a_w"].shape)
    out["gate_x_w"] = p[ROW_GATE_X:ROW_GATE_X + 32].reshape(like["gate_x_w"].shape)
    return out


def _gate_dense(wg):
    w4 = wg[0].reshape(N_LRU_TILES, 2, 64, 64)
    zero = jnp.zeros((N_LRU_TILES, 64, 64), wg.dtype)
    top = jnp.concatenate([w4[:, 0], zero], axis=2)
    bot = jnp.concatenate([zero, w4[:, 1]], axis=2)
    return jnp.concatenate([top, bot], axis=1).astype(BF16)


def _gate_blocks(dw):
    return jnp.stack([dw[:, 0:64, 0:64], dw[:, 64:128, 64:128]], axis=1).reshape(8, 64, 64)


BIG_NAMES = ["ffn1_w_gate", "ffn1_w_up", "ffn1_w_down", "w_in", "w_uq", "w_uk", "w_uv", "w_out", "ffn2_w_gate",
             "ffn2_w_up", "ffn2_w_down"]
BIG_GROUPS = [["ffn1_w_gate", "ffn1_w_up", "ffn1_w_down", "ffn2_w_gate", "ffn2_w_up", "ffn2_w_down"], ["w_in"],
              ["w_uq"], ["w_uk", "w_uv"], ["w_out"]]
TRANSPOSED = ("ffn1_w_gate", "ffn1_w_up", "ffn2_w_gate", "ffn2_w_up", "w_in", "w_uq")


def _to2d(nm, a):
    return a[0].T if nm in TRANSPOSED else a[0]


def _from2d(nm, a):
    return (a.T if nm in TRANSPOSED else a)[None]


WEIGHT_NAMES = ["meta_tokens", "ffn1_norm", "ffn1_w_gate", "ffn1_w_up", "ffn1_w_down", "mix_norm", "w_in",
                "q_latent_norm", "w_uq", "kv_latent_norm", "w_uk", "w_uv", "q_head_norm", "k_head_norm", "conv_w",
                "conv_b", "gate_a_w", "gate_a_b", "gate_x_w", "gate_x_b", "lru_lambda", "attn_out_norm",
                "lru_out_norm", "w_out", "ffn2_norm", "ffn2_w_gate", "ffn2_w_up", "ffn2_w_down", "final_norm"]


def _weight_from(nm, slots):
    if nm == "w_in":
        win = slots.reshape(IN_WIDTH, D_MODEL)
        lru = win[Z_KR + D_ROPE:].reshape(2, N_LRU_TILES, LRU_TILE, D_MODEL).transpose(1, 0, 2, 3)
        return jnp.concatenate([win[0:Z_KR + D_ROPE], jnp.zeros((128 - D_ROPE, D_MODEL), BF16),
                                lru.reshape(2 * LRU_W, D_MODEL)], axis=0)
    if nm == "w_uq":
        return jnp.pad(slots, ((0, 0), (0, D_QKP - D_QK), (0, 0))).reshape(HEADS * D_QKP, Q_RANK)
    if nm in ("w_uk", "w_uv"):
        return slots.transpose(1, 0, 2).reshape(KV_RANK, HEADS * D_NOPE)
    if nm == "w_out":
        return slots.reshape(D_MODEL, D_MODEL)
    return slots


def _small_weights(p, small):
    w = {nm: p[nm] for nm in SMALL_NAMES}
    w["q_head_norm"] = jnp.pad(p["q_head_norm"], ((0, 0), (0, D_QKP - D_QK)))
    w["k_head_norm"] = jnp.pad(p["k_head_norm"], ((0, 0), (0, D_QKP - D_QK)))
    w["conv_w"] = small[:, N_META:N_META + 2, :].reshape(N_SHARD, CONV_K, LRU_TILE)
    w["gate_a_w"] = _gate_dense(p["gate_a_w"])
    w["gate_x_w"] = _gate_dense(p["gate_x_w"])
    meta = small[:, 0:N_META, :].transpose(1, 0, 2).reshape(N_META, D_MODEL)
    return w, meta


def _full_weights(p, gathered, small):
    w, meta = _small_weights(p, small)
    w.update({nm: _weight_from(nm, gathered[nm]) for nm in BIG_NAMES})
    return w, meta


def _shard_grad(nm, g):
    if nm == "w_in":
        lru = g[Z_MLA:].reshape(N_LRU_TILES, 2, LRU_TILE, D_MODEL).transpose(1, 0, 2, 3).reshape(2 * LRU_W, D_MODEL)
        return jnp.concatenate([g[0:Z_KR + D_ROPE], lru], axis=0).reshape(N_SHARD, IN_WIDTH // N_SHARD, D_MODEL)
    if nm == "w_uq":
        return g.reshape(HEADS, D_QKP, Q_RANK)[:, 0:D_QK, :]
    if nm in ("w_uk", "w_uv"):
        return g.reshape(KV_RANK, HEADS, D_NOPE).transpose(1, 0, 2)
    if nm == "w_out":
        return g.reshape(N_SHARD, D_MODEL // N_SHARD, D_MODEL)
    return g


def _shard_grads(g):
    return {nm: _shard_grad(nm, g[nm]) for nm in BIG_NAMES}


GATHER_AT = {"ffn1_norm": ["ffn1_w_gate", "ffn1_w_up"],
             "ffn1_up": ["ffn1_w_down", "w_in", "w_uq", "w_uk", "w_uv", "w_out"],
             "attn_fwd": ["ffn2_w_down", "ffn2_w_gate"], "lru_fwd": ["ffn2_w_up"]}
PAIR_AT = [("ffn2_din", ["ffn2_w_gate", "ffn2_w_up", "ffn2_w_down"]),
           ("mix_din", ["w_out", "w_uq", "w_uk", "w_uv", "w_in"]),
           ("ffn1_dwg", ["ffn1_w_down"]), ("ffn1_dwu", ["ffn1_w_gate"]), ("ffn1_din_a", ["ffn1_w_up"])]
CHIPS_AT = [("attn_bwd", ["ffn2_w_down", "ffn2_w_gate"]), ("mla_prep_bwd", ["ffn2_w_up"]),
            ("ffn1_dact", ["w_out", "w_uq", "w_uk", "w_uv", "w_in"]),
            ("ffn1_dwu", ["ffn1_w_down"]), ("ffn1_din_a", ["ffn1_w_gate"]), ("ffn1_din_b", ["ffn1_w_up"])]
BF16_COPY = "@bf16"
SHARE_EARLY_GROUPS, SHARE_EARLY_AT = 3, "ffn1_dwd"
SMALL_EARLY_AT = "mix_din"


def _same_shape_groups(names):
    return [[nm for nm in grp if nm in names] for grp in BIG_GROUPS if any(nm in names for nm in grp)]


class _Sched:
    def __init__(self, place, w, slots):
        self.place, self.w, self.slots = place, w, slots
        self.g = None
        self.sharded, self.from_pair, self.chip_bf16, self.from_chips = {}, {}, {}, {}
        self.early = self.early_all = None
        self.shared = {}

    def host(self, stage):
        comms = []
        if stage in GATHER_AT:
            comms.append(self.gather(GATHER_AT[stage]))
        comms += [self.chips(names) for at, names in CHIPS_AT if at == stage]
        comms += [self.pair(names) for at, names in PAIR_AT if at == stage]
        if stage == SMALL_EARLY_AT:
            comms.append(self.small_early())
        if stage == SHARE_EARLY_AT:
            comms.append(self.share_early())
        return _join_comms(comms)

    def small_early(self):
        self.early = _early_pack(self.g)

        def deliver(outs):
            self.early_all = outs[0]
            return outs

        return _small_comm(self.early, deliver)

    def gather(self, names):
        def deliver(outs):
            self.w.update({nm: _weight_from(nm, o) for nm, o in zip(names, outs)})
            return outs

        return _gather_comm([self.slots[nm] for nm in names], deliver)

    def pair(self, names):
        self.sharded.update({nm: _shard_grad(nm, self.g[nm]) for nm in names})
        send = [_shard_grad(nm, self.g.get(nm + BF16_COPY, self.g[nm])) for nm in names]

        def deliver(outs):
            self.from_pair.update(zip(names, outs))
            for grp in _same_shape_groups(names):
                sums = _pair_sum_call("pair_sum_" + grp[0], self.place, [self.sharded[nm] for nm in grp],
                                      [self.from_pair[nm] for nm in grp])
                self.chip_bf16.update(zip(grp, sums))
            return outs

        return _reduce_pair_comm(send, deliver)

    def chips(self, names):
        def deliver(outs):
            self.from_chips.update(zip(names, outs))
            return outs

        return _reduce_chips_comm([self.chip_bf16[nm] for nm in names], deliver)

    def chip_sums(self, names, comm=None):
        out = {}
        for grp in _same_shape_groups(names):
            sums = _chip_sum_call("chip_sum_" + grp[0], self.place, [self.sharded[nm] for nm in grp],
                                  [self.from_pair[nm] for nm in grp], [self.from_chips[nm] for nm in grp], comm)
            comm = None
            out.update(zip(grp, sums))
        return out

    def share_early(self):
        names = [nm for at, grp in CHIPS_AT[:SHARE_EARLY_GROUPS] for nm in grp]
        mine = self.chip_sums(names)
        return _share_pair_comm([mine[nm] for nm in names], lambda o: self.shared.update(zip(names, o)))


def kernel(x, meta_tokens, ffn1_norm, ffn1_w_gate, ffn1_w_up, ffn1_w_down, mix_norm, w_in, q_latent_norm, w_uq, kv_latent_norm, w_uk, w_uv, q_head_norm, k_head_norm, conv_w, conv_b, gate_a_w, gate_a_b, gate_x_w, gate_x_b, lru_lambda, attn_out_norm, lru_out_norm, w_out, ffn2_norm, ffn2_w_gate, ffn2_w_up, ffn2_w_down, final_norm, loss_target, m_meta_tokens, m_ffn1_norm, m_ffn1_w_gate, m_ffn1_w_up, m_ffn1_w_down, m_mix_norm, m_w_in, m_q_latent_norm, m_w_uq, m_kv_latent_norm, m_w_uk, m_w_uv, m_q_head_norm, m_k_head_norm, m_conv_w, m_conv_b, m_gate_a_w, m_gate_a_b, m_gate_x_w, m_gate_x_b, m_lru_lambda, m_attn_out_norm, m_lru_out_norm, m_w_out, m_ffn2_norm, m_ffn2_w_gate, m_ffn2_w_up, m_ffn2_w_down, m_final_norm, v_meta_tokens, v_ffn1_norm, v_ffn1_w_gate, v_ffn1_w_up, v_ffn1_w_down, v_mix_norm, v_w_in, v_q_latent_norm, v_w_uq, v_kv_latent_norm, v_w_uk, v_w_uv, v_q_head_norm, v_k_head_norm, v_conv_w, v_conv_b, v_gate_a_w, v_gate_a_b, v_gate_x_w, v_gate_x_b, v_lru_lambda, v_attn_out_norm, v_lru_out_norm, v_w_out, v_ffn2_norm, v_ffn2_w_gate, v_ffn2_w_up, v_ffn2_w_down, v_final_norm):
    args = locals()
    p = {nm: args[nm] for nm in WEIGHT_NAMES}
    mom = {nm: args["m_" + nm] for nm in WEIGHT_NAMES}
    var = {nm: args["v_" + nm] for nm in WEIGHT_NAMES}
    nb, seq, d = x.shape
    lp = CHUNK + seq
    xi, yi, ci = lax.axis_index("x"), lax.axis_index("y"), lax.axis_index("c")
    chip = 2 * xi + yi

    place = jnp.stack([chip, ci]).astype(jnp.int32)
    p2 = {nm: _to2d(nm, p[nm]) for nm in BIG_NAMES}
    m2 = {nm: _to2d(nm, mom[nm]) for nm in BIG_NAMES}
    v2 = {nm: _to2d(nm, var[nm]) for nm in BIG_NAMES}

    slots = {}
    small_shard = jnp.concatenate(
        [meta_tokens, conv_w[0].reshape(2, 2 * LRU_TILE), jnp.zeros((14, 2 * LRU_TILE), F32)], axis=0)
    small_slots = lax.dynamic_update_slice(jnp.zeros((N_SHARD,) + small_shard.shape, F32), small_shard[None],
                                           (chip, 0, 0))
    first = []
    comm = _gather_comm([small_slots], first.extend)
    for grp in BIG_GROUPS:
        for nm, buf in zip(grp, _cast_call("cast_" + grp[0], place, [p2[nm] for nm in grp], comm)):
            slots[nm] = buf
        comm = None
    w, meta = _small_weights(p, first[0])
    sched = _Sched(place, w, slots)

    h0 = jnp.concatenate(
        [jnp.zeros((nb, PAD_ROWS, d), F32), jnp.broadcast_to(meta[None], (nb, N_META, d)), x], axis=1)
    target = jnp.pad(loss_target, ((0, 0), (CHUNK, 0), (0, 0)))
    loss_part, dh0, g = _local_step(h0.reshape(nb * lp, d), target.reshape(nb * lp, d), w, nb, lp, sched)
    dh0 = dh0.reshape(nb, lp, d)
    grad_x = dh0[:, CHUNK:, :]

    late = jnp.concatenate([g["ffn1_norm"], g["mix_norm"], jnp.zeros((6, D_MODEL), F32),
                            jnp.sum(dh0[:, PAD_ROWS:CHUNK, :], axis=0)], axis=0)
    shared = sched.shared
    rest = [nm for at, names in CHIPS_AT[SHARE_EARLY_GROUPS:] if at is not None for nm in names]
    last = [nm for at, names in CHIPS_AT if at is None for nm in names]
    late_box = {}
    mine = sched.chip_sums(rest, _small_comm(late, lambda o: late_box.update(all=o[0])))
    share = _share_pair_comm([mine[nm] for nm in rest], lambda o: shared.update(zip(rest, o)))
    _comm_call("share_pair", _join_comms([share, sched.chips(last) if last else None]))
    if last:
        mine = sched.chip_sums(last)
        _comm_call("share_last", _share_pair_comm([mine[nm] for nm in last], lambda o: shared.update(zip(last, o))))
    late_all = late_box["all"]
    small_like = {nm: p[nm] for nm in SMALL_NAMES + ["gate_a_w", "gate_x_w"]}

    def state(t):
        return ([t[nm] for nm in SMALL_NAMES], t["gate_a_w"].reshape(32, D_MODEL), t["gate_x_w"].reshape(32, D_MODEL))

    me = (4 * xi + 2 * yi + ci).astype(jnp.int32).reshape(1)
    (gsum, dsm, msm, vsm), rows = _small_update_call(me, sched.early_all, sched.early, late_all, late,
                                                     [state(p), state(mom), state(var)])
    grads = _unpack_small(gsum, rows[0], small_like)
    delta = _unpack_small(dsm, rows[1], small_like)
    new_m = _unpack_small(msm, rows[2], small_like)
    new_v = _unpack_small(vsm, rows[3], small_like)
    loss = gsum[ROW_LOSS, 0]
    gmeta = gsum[ROW_META:ROW_META + N_META].reshape(N_META, N_SHARD, D_MODEL // N_SHARD)
    grads["meta_tokens"] = lax.dynamic_index_in_dim(gmeta, chip, axis=1, keepdims=False)
    gconv = gsum[ROW_CONV_W:ROW_CONV_W + 2].reshape(CONV_K, N_SHARD, LRU_TILE)
    grads["conv_w"] = lax.dynamic_index_in_dim(gconv, chip, axis=1, keepdims=False)[None]
    for nm in ("meta_tokens", "conv_w"):
        delta[nm], new_m[nm], new_v[nm] = _adamw_call("adamw_" + nm, p[nm], grads[nm], mom[nm], var[nm])

    for names in BIG_GROUPS:
        res = _adamw_group_call("adamw_" + names[0], [p2[nm] for nm in names], [shared[nm] for nm in names],
                                [m2[nm] for nm in names], [v2[nm] for nm in names])
        for nm, (gg, dd, mm, vv) in zip(names, res):
            grads[nm], delta[nm], new_m[nm], new_v[nm] = (_from2d(nm, t) for t in (gg, dd, mm, vv))

    return (loss, grad_x, *[grads[nm] for nm in WEIGHT_NAMES], *[delta[nm] for nm in WEIGHT_NAMES],
            *[new_m[nm] for nm in WEIGHT_NAMES], *[new_v[nm] for nm in WEIGHT_NAMES])
```

```python
import functools
import math

import jax
import jax.numpy as jnp
import numpy as np
from jax import lax
from jax.experimental import pallas as pl
from jax.experimental.pallas import tpu as pltpu

F32 = jnp.float32
BF16 = jnp.bfloat16
MESH = pl.DeviceIdType.MESH

D_MODEL = 1024
N_META = 16
CHUNK = 64
PAD_ROWS = CHUNK - N_META
HEADS = 4
D_NOPE = 128
D_ROPE = 64
D_QK = D_NOPE + D_ROPE
D_QKP = 256
D_V = 128
KV_RANK = 256
Q_RANK = 384
MLA_W = HEADS * D_V
LRU_W = 512
LRU_TILE = 128
N_LRU_TILES = LRU_W // LRU_TILE
CONV_K = 4
C_RGLRU = 8.0
ROPE_THETA = 10000.0
D_FF = 2816
N_SHARD = 4
EPS = 1e-6
NEG_INF = -1e30
Z_KR = Q_RANK + KV_RANK
Z_MLA = Z_KR + 128
Z_U = Z_MLA
Z_G = Z_U + LRU_W
Z_W = Z_G + LRU_W
IN_WIDTH = Q_RANK + KV_RANK + D_ROPE + 2 * LRU_W

ADAM_LR = 0.001
ADAM_B1 = 0.9
ADAM_B2 = 0.999
ADAM_EPS = 1e-08
ADAM_WD = 0.01
ADAM_STEP = 10

VMEM_LIMIT_BYTES = 56 * 1024 * 1024
MATMUL_ROWS = 1408
ELEMENTWISE_ROWS = 512
SMALL_ROWS = 104
SMALL_ADAM_ROWS = 80


def _params(sem):
    return pltpu.CompilerParams(dimension_semantics=sem, vmem_limit_bytes=VMEM_LIMIT_BYTES)


def _resident(shape):
    return pl.BlockSpec(tuple(shape), lambda i: (0,) * len(shape), pipeline_mode=pl.Buffered(1))


def _row_tile(rows, target):
    best = 16
    for t in range(16, min(rows, target) + 1, 16):
        if rows % t == 0:
            best = t
    return best


def _col_tile(cols, target):
    best = cols
    for t in range(128, min(cols, target) + 1, 128):
        if cols % t == 0:
            best = t
    return best


def _dot(a, b):
    return jnp.dot(a, b, preferred_element_type=F32)


def _dot_nt(a, b):
    return lax.dot_general(a, b, (((1,), (1,)), ((), ())), preferred_element_type=F32)


def _dot_tn(a, b):
    return lax.dot_general(a, b, (((0,), (0,)), ((), ())), preferred_element_type=F32)


def _rms(x, n):
    return lax.rsqrt(jnp.sum(x * x, axis=-1, keepdims=True) * (1.0 / n) + EPS)


def _rms_bwd(dn, nrm, r, n):
    return r * (dn - nrm * (jnp.sum(dn * nrm, axis=-1, keepdims=True) * (1.0 / n)))


def _gelu(x):
    k = math.sqrt(2.0 / math.pi)
    t = jnp.tanh(k * (x + 0.044715 * x * x * x))
    return 0.5 * x * (1.0 + t), t


def _gelu_grad(x, t):
    k = math.sqrt(2.0 / math.pi)
    return 0.5 * (1.0 + t) + 0.5 * x * (1.0 - t * t) * k * (1.0 + 3.0 * 0.044715 * x * x)


def _sigmoid(x):
    return 0.5 + 0.5 * jnp.tanh(0.5 * x)


def _softplus_neg(lam):
    e = jnp.exp(-jnp.abs(lam))
    log1p = jnp.where(e < 0.01, e * (1.0 - e * (0.5 - e * (1.0 / 3 - e * 0.25))), jnp.log(1.0 + e))
    return jnp.maximum(-lam, 0.0) + log1p


def _rope(t, c, s1, s2):
    return t * c + pltpu.roll(t, 96, 1) * s1 + pltpu.roll(t, 32, 1) * s2


def _rope_t(d, c, s1, s2):
    return d * c + pltpu.roll(d * s1, 32, 1) + pltpu.roll(d * s2, 96, 1)


def _rope_tables(lp):
    pos = (np.arange(lp, dtype=np.int32) - PAD_ROWS).astype(np.float32)
    inv_freq = (ROPE_THETA ** (-np.arange(0, D_ROPE // 2, dtype=np.float32) / (D_ROPE // 2))).astype(np.float32)
    ang = (pos[:, None] * inv_freq[None, :]).astype(np.float32).astype(np.float64)
    cos, sin = np.cos(ang).astype(np.float32), np.sin(ang).astype(np.float32)
    z = np.zeros_like(cos)
    return (jnp.asarray(np.concatenate([cos, cos, z, z], 1)), jnp.asarray(np.concatenate([-sin, z, z, z], 1)),
            jnp.asarray(np.concatenate([z, sin, z, z], 1)))


def _rmsnorm_call(name, h, g, tm, comm=None):
    rows, d = h.shape

    def body(h_ref, g_ref, o_ref):
        x = h_ref[...]
        o_ref[...] = (x * _rms(x, d) * g_ref[...]).astype(BF16)

    return _hosted_call(
        body, name=name, grid=(rows // tm,),
        in_specs=[pl.BlockSpec((tm, d), lambda i: (i, 0)), pl.BlockSpec((1, d), lambda i: (0, 0))],
        out_specs=[pl.BlockSpec((tm, d), lambda i: (i, 0))],
        out_shape=[jax.ShapeDtypeStruct((rows, d), BF16)],
        dims=("parallel",), args=(h, g), comm=comm)[0]


def _ffn_up_call(name, u, wg, wu, tm, comm=None):
    rows, d = u.shape
    ns, fs, _ = wg.shape

    def body(u_ref, wg_ref, wu_ref, g_ref, p_ref, a_ref):
        uu = u_ref[...]
        g = _dot_nt(uu, wg_ref[0])
        p = _dot_nt(uu, wu_ref[0])
        g_ref[0] = g.astype(BF16)
        p_ref[0] = p.astype(BF16)
        a_ref[0] = (g * jax.nn.sigmoid(g) * p).astype(BF16)

    wspec = pl.BlockSpec((1, fs, d), lambda s, i: (s, 0, 0))
    ospec = pl.BlockSpec((1, tm, fs), lambda s, i: (s, i, 0))
    oshape = jax.ShapeDtypeStruct((ns, rows, fs), BF16)
    return _hosted_call(
        body, name=name, grid=(ns, rows // tm),
        in_specs=[pl.BlockSpec((tm, d), lambda s, i: (i, 0)), wspec, wspec],
        out_specs=[ospec, ospec, ospec], out_shape=[oshape, oshape, oshape],
        dims=("parallel", "parallel"), args=(u, wg, wu), comm=comm)


def _loss_tail(x, g, t, row, d):
    r = _rms(x, d)
    n = x * r
    err = jnp.where(row >= CHUNK, n * g - t, 0.0)
    dout = err * (1.0 / d)
    dh = _rms_bwd(dout * g, n, r, d)
    dg = jnp.sum(dout * n, axis=0, keepdims=True)
    part = jnp.sum(jnp.sum(err * err, axis=1, keepdims=True), axis=0, keepdims=True) * (0.5 / d)
    return dh, dg, jnp.broadcast_to(part, (1, 128))


def _ffn_down_call(name, a, wd, h, tm, comm=None, next_gain=None):
    rows, d = h.shape
    ns, _, fs = a.shape
    more = next_gain is not None

    def body(a_ref, wd_ref, h_ref, *rest):
        acc = h_ref[...]
        for s in range(ns):
            acc = acc + 0.5 * _dot(a_ref[s], wd_ref[s])
        rest[-2 if more else -1][...] = acc
        if more:
            rest[-1][...] = (acc * _rms(acc, d) * rest[0][...]).astype(BF16)

    full = pl.BlockSpec((tm, d), lambda i: (i, 0))
    res = _hosted_call(
        body, name=name, grid=(rows // tm,),
        in_specs=[pl.BlockSpec((ns, tm, fs), lambda i: (0, i, 0)), _resident((ns, fs, d)), full]
        + ([pl.BlockSpec((1, d), lambda i: (0, 0))] if more else []),
        out_specs=[full] * (2 if more else 1),
        out_shape=[jax.ShapeDtypeStruct((rows, d), F32)] + ([jax.ShapeDtypeStruct((rows, d), BF16)] if more else []),
        dims=("parallel",), args=(a, wd, h) + ((next_gain,) if more else ()), comm=comm)
    return res if more else res[0]


def _ffn_down_loss_call(name, a, wd, h, g, target, lp, tm):
    rows, d = h.shape
    ns, _, fs = a.shape
    tpe = lp // tm

    def body(a_ref, wd_ref, h_ref, g_ref, t_hbm, dh_ref, dhb_ref, dg_ref, loss_ref, t_buf, t_sem):
        i = pl.program_id(0)
        ex, j = i // tpe, i % tpe
        head = j == 0

        def fetch_head():
            return pltpu.make_async_copy(t_hbm.at[ex, pl.ds(0, tm - CHUNK)], t_buf.at[pl.ds(CHUNK, tm - CHUNK)],
                                         t_sem.at[0])

        def fetch_rest():
            start = pl.multiple_of(jnp.maximum(j * tm - CHUNK, 0), 8)
            return pltpu.make_async_copy(t_hbm.at[ex, pl.ds(start, tm)], t_buf, t_sem.at[0])

        @pl.when(head)
        def _():
            t_buf[0:CHUNK, :] = jnp.zeros((CHUNK, d), F32)
            fetch_head().start()

        @pl.when(jnp.logical_not(head))
        def _():
            fetch_rest().start()

        acc = h_ref[...]
        for s in range(ns):
            acc = acc + 0.5 * _dot(a_ref[s], wd_ref[s])

        @pl.when(head)
        def _():
            fetch_head().wait()

        @pl.when(jnp.logical_not(head))
        def _():
            fetch_rest().wait()

        row = j * tm + lax.broadcasted_iota(jnp.int32, (tm, 1), 0)
        dh, dg, loss = _loss_tail(acc, g_ref[...], t_buf[...], row, d)
        dh_ref[...] = dh
        dhb_ref[...] = dh.astype(BF16)

        @pl.when(i == 0)
        def _():
            dg_ref[...] = dg
            loss_ref[...] = loss

        @pl.when(i != 0)
        def _():
            dg_ref[...] += dg
            loss_ref[...] += loss

    full = pl.BlockSpec((tm, d), lambda i: (i, 0))
    gspec = pl.BlockSpec((1, d), lambda i: (0, 0))
    return _hosted_call(
        body, name=name, grid=(rows // tm,),
        in_specs=[pl.BlockSpec((ns, tm, fs), lambda i: (0, i, 0)), _resident((ns, fs, d)), full, gspec,
                  pl.BlockSpec(memory_space=pl.ANY)],
        out_specs=[full, full, gspec, pl.BlockSpec((1, 128), lambda i: (0, 0))],
        out_shape=[jax.ShapeDtypeStruct((rows, d), F32), jax.ShapeDtypeStruct((rows, d), BF16),
                   jax.ShapeDtypeStruct((1, d), F32), jax.ShapeDtypeStruct((1, 128), F32)],
        scratch_shapes=[pltpu.VMEM((tm, d), F32), pltpu.SemaphoreType.DMA((1,))],
        args=(a, wd, h, g, target))


def _mm_call(name, a, bt, tm, out_dtype):
    rows, k = a.shape
    n = bt.shape[0]

    def body(a_ref, b_ref, o_ref):
        o_ref[...] = _dot_nt(a_ref[...], b_ref[...]).astype(out_dtype)

    return pl.pallas_call(
        body, name=name, grid=(rows // tm,),
        in_specs=[pl.BlockSpec((tm, k), lambda i: (i, 0)), pl.BlockSpec((n, k), lambda i: (0, 0))],
        out_specs=pl.BlockSpec((tm, n), lambda i: (i, 0)),
        out_shape=jax.ShapeDtypeStruct((rows, n), out_dtype),
        compiler_params=_params(("parallel",)))(a, bt)


def _mla_heads(z, gql, gkvl, wuq, wuk, wuv):
    cq = z[:, 0:Q_RANK]
    ckv = z[:, Q_RANK:Z_KR]
    kr = z[:, Z_KR:Z_MLA]
    rq = _rms(cq, Q_RANK)
    nq = cq * rq
    cqn = (nq * gql).astype(BF16)
    rkv = _rms(ckv, KV_RANK)
    nkv = ckv * rkv
    ckvn = (nkv * gkvl).astype(BF16)
    qraw = _dot_nt(cqn, wuq)
    knope = _dot(ckvn, wuk)
    v = _dot(ckvn, wuv)
    skr = jnp.sum(kr * kr, axis=-1, keepdims=True)
    heads = []
    for hd in range(HEADS):
        qh = qraw[:, hd * D_QKP:(hd + 1) * D_QKP]
        rqh = lax.rsqrt(jnp.sum(qh * qh, axis=-1, keepdims=True) * (1.0 / D_QK) + EPS)
        kn = knope[:, hd * D_NOPE:(hd + 1) * D_NOPE]
        rkh = lax.rsqrt((jnp.sum(kn * kn, axis=-1, keepdims=True) + skr) * (1.0 / D_QK) + EPS)
        heads.append((qh * rqh, rqh, kn * rkh, kr * rkh, rkh))
    return dict(rq=rq, nq=nq, cqn=cqn, rkv=rkv, nkv=nkv, ckvn=ckvn, v=v, heads=heads)


def _mla_prep_call(z, gql, gkvl, gqh, gkh, wuq, wuk, wuv, tabs, lp, tm):
    rows = z.shape[0]
    tpe = lp // tm

    def body(z_ref, gql_ref, gkvl_ref, gqh_ref, gkh_ref, wuq_ref, wuk_ref, wuv_ref, c_ref, s1_ref, s2_ref,
             q_ref, k_ref, v_ref):
        m = _mla_heads(z_ref[...], gql_ref[...], gkvl_ref[...], wuq_ref[...], wuk_ref[...], wuv_ref[...])
        c, s1, s2 = c_ref[...], s1_ref[...], s2_ref[...]
        gq, gk = gqh_ref[...], gkh_ref[...]
        row = (pl.program_id(0) % tpe) * tm + lax.broadcasted_iota(jnp.int32, (tm, 1), 0)
        spare = (lax.broadcasted_iota(jnp.int32, (1, D_QKP - D_NOPE), 1) == D_ROPE).astype(F32)
        kmask = jnp.where(row < PAD_ROWS, NEG_INF * math.sqrt(D_QK), 0.0) * spare
        for hd in range(HEADS):
            qn, _, knn, krn, _ = m["heads"][hd]
            qg = qn * gq
            q_ref[hd, :, 0:D_NOPE] = qg[:, 0:D_NOPE].astype(BF16)
            q_ref[hd, :, D_NOPE:D_QKP] = (_rope(qg[:, D_NOPE:D_QKP], c, s1, s2) + spare).astype(BF16)
            k_ref[hd, :, 0:D_NOPE] = (knn * gk[:, 0:D_NOPE]).astype(BF16)
            k_ref[hd, :, D_NOPE:D_QKP] = (_rope(krn * gk[:, D_NOPE:D_QKP], c, s1, s2) + kmask).astype(BF16)
            v_ref[hd] = m["v"][:, hd * D_V:(hd + 1) * D_V].astype(BF16)

    def const(shape):
        return pl.BlockSpec(shape, lambda i: tuple(0 for _ in shape))

    tab = pl.BlockSpec((tm, 128), lambda i: (i % tpe, 0))
    return pl.pallas_call(
        body, name="mla_prep", grid=(rows // tm,),
        in_specs=[pl.BlockSpec((tm, Z_MLA), lambda i: (i, 0)), const((1, Q_RANK)), const((1, KV_RANK)),
                  const((1, D_QKP)), const((1, D_QKP)), const((HEADS * D_QKP, Q_RANK)),
                  const((KV_RANK, HEADS * D_NOPE)), const((KV_RANK, HEADS * D_V)), tab, tab, tab],
        out_specs=[pl.BlockSpec((HEADS, tm, D_QKP), lambda i: (0, i, 0)),
                   pl.BlockSpec((HEADS, tm, D_QKP), lambda i: (0, i, 0)),
                   pl.BlockSpec((HEADS, tm, D_V), lambda i: (0, i, 0))],
        out_shape=[jax.ShapeDtypeStruct((HEADS, rows, D_QKP), BF16),
                   jax.ShapeDtypeStruct((HEADS, rows, D_QKP), BF16),
                   jax.ShapeDtypeStruct((HEADS, rows, D_V), BF16)],
        compiler_params=_params(("parallel",)))(z, gql, gkvl, gqh, gkh, wuq, wuk, wuv, *tabs)


Q_BLOCK_ROWS = 528
Q_BLOCK_ROWS_BWD = 192


def _q_block(lp):
    return _row_tile(lp, Q_BLOCK_ROWS)


def _key_end(ext, lp):
    return min(lp, -(-ext // CHUNK) * CHUNK)


def _diag_bias(qb, j0, nk):
    shift = CHUNK.bit_length() - 1
    r = jnp.right_shift(j0 + lax.broadcasted_iota(jnp.int32, (qb, nk), 0), shift)
    c = jnp.right_shift(j0 + lax.broadcasted_iota(jnp.int32, (qb, nk), 1), shift)
    return jnp.where(c <= r, 0.0, NEG_INF)


def _attn_fwd_call(q, k, v, nb, lp, comm=None):
    rows = nb * lp
    qb = _q_block(lp)
    scale = 1.0 / math.sqrt(D_QK)

    def body(q_ref, k_ref, v_ref, o_ref, lse_ref):
        for j in range(lp // qb):
            j0, ext = j * qb, (j + 1) * qb
            kend = _key_end(ext, lp)
            qj = q_ref[0, j0:ext, :]
            sd = _dot_nt(qj, k_ref[0, j0:kend, :]) * scale + _diag_bias(qb, j0, kend - j0)
            mx = jnp.max(sd, axis=-1, keepdims=True)
            if j > 0:
                so = _dot_nt(qj, k_ref[0, 0:j0, :]) * scale
                mx = jnp.maximum(mx, jnp.max(so, axis=-1, keepdims=True))
            pd = jnp.exp(sd - mx)
            l = jnp.sum(pd, axis=-1, keepdims=True)
            o = _dot(pd.astype(BF16), v_ref[0, j0:kend, :])
            if j > 0:
                po = jnp.exp(so - mx)
                l = l + jnp.sum(po, axis=-1, keepdims=True)
                o = o + _dot(po.astype(BF16), v_ref[0, 0:j0, :])
            o_ref[j0:ext, :] = o / l
            lse_ref[0, j0:ext, :] = mx + jnp.log(l)

    return _hosted_call(
        body, name="attn_fwd", grid=(nb, HEADS),
        in_specs=[pl.BlockSpec((1, lp, D_QKP), lambda b, h: (h, b, 0)),
                  pl.BlockSpec((1, lp, D_QKP), lambda b, h: (h, b, 0)),
                  pl.BlockSpec((1, lp, D_V), lambda b, h: (h, b, 0))],
        out_specs=[pl.BlockSpec((lp, D_V), lambda b, h: (b, h)),
                   pl.BlockSpec((1, lp, 1), lambda b, h: (h, b, 0))],
        out_shape=[jax.ShapeDtypeStruct((rows, MLA_W), F32),
                   jax.ShapeDtypeStruct((HEADS, rows, 1), F32)],
        dims=("parallel", "parallel"), args=(q, k, v), comm=comm)


def _attn_bwd_call(q, k, v, o, lse, do, nb, lp, comm=None):
    rows = nb * lp
    qb = _row_tile(lp, Q_BLOCK_ROWS_BWD)
    scale = 1.0 / math.sqrt(D_QK)

    def body(q_ref, k_ref, v_ref, o_ref, lse_ref, do_ref, dq_ref, dk_ref, dv_ref, dk_acc, dv_acc):
        dk_acc[...] = jnp.zeros_like(dk_acc)
        dv_acc[...] = jnp.zeros_like(dv_acc)
        shift = CHUNK.bit_length() - 1
        for j in range(lp // qb):
            j0, ext = j * qb, (j + 1) * qb
            kend = _key_end(ext, lp)
            qj = q_ref[0, j0:ext, :]
            doj = do_ref[j0:ext, :]
            delta = jnp.sum(doj * o_ref[j0:ext, :], axis=-1, keepdims=True)
            dob = doj.astype(BF16)
            kk = k_ref[0, 0:kend, :]
            qchunk = jnp.right_shift(j0 + lax.broadcasted_iota(jnp.int32, (qb, kend), 0), shift)
            kchunk = jnp.right_shift(lax.broadcasted_iota(jnp.int32, (qb, kend), 1), shift)
            s = _dot_nt(qj, kk) * scale - lse_ref[0, j0:ext, :]
            p = jnp.where(kchunk <= qchunk, jnp.exp(s), 0.0)
            dv_acc[0:kend, :] += _dot_tn(p.astype(BF16), dob)
            dp = _dot_nt(dob, v_ref[0, 0:kend, :])
            ds = (p * (dp - delta) * scale).astype(BF16)
            dq_ref[0, j0:ext, :] = _dot(ds, kk).astype(BF16)
            dk_acc[0:kend, :] += _dot_tn(ds, qj)
        dk_ref[0] = dk_acc[...].astype(BF16)
        dv_ref[0] = dv_acc[...].astype(BF16)

    qspec = pl.BlockSpec((1, lp, D_QKP), lambda b, h: (h, b, 0))
    vspec = pl.BlockSpec((1, lp, D_V), lambda b, h: (h, b, 0))
    ospec = pl.BlockSpec((lp, D_V), lambda b, h: (b, h))
    return _hosted_call(
        body, name="attn_bwd", grid=(nb, HEADS),
        in_specs=[qspec, qspec, vspec, ospec, pl.BlockSpec((1, lp, 1), lambda b, h: (h, b, 0)), ospec],
        out_specs=[qspec, qspec, vspec],
        out_shape=[jax.ShapeDtypeStruct((HEADS, rows, D_QKP), BF16),
                   jax.ShapeDtypeStruct((HEADS, rows, D_QKP), BF16),
                   jax.ShapeDtypeStruct((HEADS, rows, D_V), BF16)],
        scratch_shapes=[pltpu.VMEM((lp, D_QKP), F32), pltpu.VMEM((lp, D_V), F32)],
        dims=("parallel", "parallel"), args=(q, k, v, o, lse, do), comm=comm)


def _lru_gates(u, cw, cb, wa, ba, wx, bx, lam, lp):
    xc = (cw[3:4, :] * u + cw[2:3, :] * pltpu.roll(u, 1, 0) + cw[1:2, :] * pltpu.roll(u, 2, 0)
          + cw[0:1, :] * pltpu.roll(u, 3, 0) + cb)
    xcb = xc.astype(BF16)
    r = _sigmoid(_dot(xcb, wa) + ba)
    i = _sigmoid(_dot(xcb, wx) + bx)
    sp = _softplus_neg(lam)
    la = -C_RGLRU * r * sp
    a = jnp.exp(la)
    x2 = 2.0 * la
    e2 = a * a
    m2 = jnp.maximum(jnp.where(x2 > -0.01, -x2 * (1.0 + 0.5 * x2), 1.0 - e2), 1e-30)
    rs = lax.rsqrt(m2)
    row = lax.broadcasted_iota(jnp.int32, (lp, LRU_TILE), 0)
    first = row == PAD_ROWS
    valid = row >= PAD_ROWS
    mult_eff = jnp.where(first, 1.0, m2 * rs)
    return dict(xc=xc, xcb=xcb, r=r, i=i, sp=sp, a=a, e2=e2, rs=rs, mult_eff=mult_eff, first=first, valid=valid)


def _scan_rows(a, b, a_s, b_s, out_ref, lp, reverse):
    sub = lax.broadcasted_iota(jnp.int32, (lp, LRU_TILE), 0) & 7
    for dist in (1, 2, 4):
        shift = lp - dist if reverse else dist
        keep = (sub + dist <= 7) if reverse else (sub >= dist)
        a_sh = pltpu.roll(a, shift, 0)
        b_sh = pltpu.roll(b, shift, 0)
        b = jnp.where(keep, a * b_sh + b, b)
        a = jnp.where(keep, a * a_sh, a)
    a_s[...] = a
    b_s[...] = b
    n_groups = lp // 8
    edge = 0 if reverse else 7

    def group(gi, carry):
        r0 = pl.multiple_of(((n_groups - 1 - gi) if reverse else gi) * 8, 8)
        a8 = a_s[pl.ds(r0, 8), :]
        b8 = b_s[pl.ds(r0, 8), :]
        out_ref[pl.ds(r0, 8), :] = a8 * carry + b8
        return a8[edge:edge + 1, :] * carry + b8[edge:edge + 1, :]

    lax.fori_loop(0, n_groups, group, jnp.zeros((1, LRU_TILE), F32), unroll=4)


def _lru_specs(lp):
    seq = lambda col0, stride=1: pl.BlockSpec((lp, LRU_TILE), lambda t, b: (b, col0 + stride * t))
    cw = pl.BlockSpec((1, CONV_K, LRU_TILE), lambda t, b: (t, 0, 0))
    vec = pl.BlockSpec((1, LRU_TILE), lambda t, b: (0, t))
    mat = pl.BlockSpec((1, LRU_TILE, LRU_TILE), lambda t, b: (t, 0, 0))
    return seq, cw, vec, mat


def _lru_fwd_call(z, cw, cb, wa, ba, wx, bx, lam, nb, lp, comm=None):
    rows = nb * lp
    seq, cwspec, vec, mat = _lru_specs(lp)

    def body(u_ref, g_ref, cw_ref, cb_ref, wa_ref, ba_ref, wx_ref, bx_ref, lam_ref, y_ref, hs_ref, a_s, b_s):
        m = _lru_gates(u_ref[...], cw_ref[0], cb_ref[...], wa_ref[0], ba_ref[...], wx_ref[0], bx_ref[...],
                       lam_ref[...], lp)
        a = jnp.where(m["valid"], m["a"], 0.0)
        b = jnp.where(m["valid"], m["mult_eff"] * (m["i"] * m["xc"]), 0.0)
        _scan_rows(a, b, a_s, b_s, hs_ref, lp, reverse=False)
        gl, _ = _gelu(g_ref[...])
        y_ref[...] = hs_ref[...] * gl

    oshape = jax.ShapeDtypeStruct((rows, LRU_W), F32)
    return _hosted_call(
        body, name="lru_fwd", grid=(N_LRU_TILES, nb),
        in_specs=[seq(Z_U // LRU_TILE, 2), seq(Z_U // LRU_TILE + 1, 2), cwspec, vec, mat, vec, mat, vec, vec],
        out_specs=[seq(0), seq(0)], out_shape=[oshape, oshape],
        scratch_shapes=[pltpu.VMEM((lp, LRU_TILE), F32), pltpu.VMEM((lp, LRU_TILE), F32)],
        dims=("parallel", "parallel"), args=(z, z, cw, cb, wa, ba, wx, bx, lam), comm=comm)


def _lru_bwd_call(z, dz, hs, dy, cw, cb, wa, ba, wx, bx, lam, nb, lp, comm=None):
    rows = nb * lp
    seq, cwspec, vec, mat = _lru_specs(lp)

    def body(u_ref, g_ref, hs_ref, dy_ref, cw_ref, cb_ref, wa_ref, ba_ref, wx_ref, bx_ref, lam_ref, dz_in,
             dz_ref, dcw_ref, dcb_ref, dwa_ref, dba_ref, dwx_ref, dbx_ref, dlam_ref, a_s, b_s, d_s):
        du_ref = dz_ref.at[:, 0:LRU_TILE]
        dg_ref = dz_ref.at[:, LRU_TILE:2 * LRU_TILE]
        b_idx = pl.program_id(1)
        u = u_ref[...]
        cw = cw_ref[0]
        wa, wx = wa_ref[0], wx_ref[0]
        lam = lam_ref[...]
        m = _lru_gates(u, cw, cb_ref[...], wa, ba_ref[...], wx, bx_ref[...], lam, lp)
        gate = g_ref[...]
        gl, th = _gelu(gate)
        dy = dy_ref[...]
        hs = hs_ref[...]
        dg_ref[...] = (dy * hs * _gelu_grad(gate, th)).astype(BF16)
        a_eff = jnp.where(m["valid"], m["a"], 0.0)
        _scan_rows(pltpu.roll(a_eff, lp - 1, 0), dy * gl, a_s, b_s, d_s, lp, reverse=True)
        ds = d_s[...]
        xc, r, i = m["xc"], m["r"], m["i"]
        row = lax.broadcasted_iota(jnp.int32, (lp, LRU_TILE), 0)
        da = ds * jnp.where(row >= 1, pltpu.roll(hs, 1, 0), 0.0)
        db = jnp.where(m["valid"], ds, 0.0)
        di = db * m["mult_eff"] * xc
        dxc = db * m["mult_eff"] * i
        live = m["valid"] & jnp.logical_not(m["first"])
        dm = jnp.where(live, db * i * xc, 0.0)
        dla = da * m["a"] - dm * (m["e2"] * m["rs"])
        dr = dla * (-C_RGLRU * m["sp"])
        dsp = jnp.sum(dla * (-C_RGLRU * r), axis=0, keepdims=True)
        dpr = (dr * r * (1.0 - r))
        dpi = (di * i * (1.0 - i))
        dprb, dpib = dpr.astype(BF16), dpi.astype(BF16)
        dxc = dxc + _dot_nt(dprb, wa) + _dot_nt(dpib, wx)
        du = (cw[3:4, :] * dxc + cw[2:3, :] * pltpu.roll(dxc, lp - 1, 0) + cw[1:2, :] * pltpu.roll(dxc, lp - 2, 0)
              + cw[0:1, :] * pltpu.roll(dxc, lp - 3, 0))
        du_ref[...] = jnp.where(m["valid"], du, 0.0).astype(BF16)
        tap = lax.broadcasted_iota(jnp.int32, (CONV_K, LRU_TILE), 0)
        dcw = jnp.zeros((CONV_K, LRU_TILE), F32)
        for kk in range(CONV_K):
            shifted = u if kk == CONV_K - 1 else pltpu.roll(u, CONV_K - 1 - kk, 0)
            dcw = jnp.where(tap == kk, jnp.sum(dxc * shifted, axis=0, keepdims=True), dcw)
        parts = [(dcw_ref, dcw[None]), (dcb_ref, jnp.sum(dxc, axis=0, keepdims=True)[None]),
                 (dwa_ref, _dot_tn(m["xcb"], dprb)[None]), (dba_ref, jnp.sum(dpr, axis=0, keepdims=True)[None]),
                 (dwx_ref, _dot_tn(m["xcb"], dpib)[None]), (dbx_ref, jnp.sum(dpi, axis=0, keepdims=True)[None]),
                 (dlam_ref, (dsp * (-jax.nn.sigmoid(-lam)))[None])]

        @pl.when(b_idx == 0)
        def _():
            for ref, val in parts:
                ref[...] = val

        @pl.when(b_idx != 0)
        def _():
            for ref, val in parts:
                ref[...] += val

    vec3 = pl.BlockSpec((1, 1, LRU_TILE), lambda t, b: (t, 0, 0))
    vshape = jax.ShapeDtypeStruct((N_LRU_TILES, 1, LRU_TILE), F32)
    mshape = jax.ShapeDtypeStruct((N_LRU_TILES, LRU_TILE, LRU_TILE), F32)
    pair = pl.BlockSpec((lp, 2 * LRU_TILE), lambda t, b: (b, Z_U // (2 * LRU_TILE) + t))
    return _hosted_call(
        body, name="lru_bwd", grid=(N_LRU_TILES, nb),
        in_specs=[seq(Z_U // LRU_TILE, 2), seq(Z_U // LRU_TILE + 1, 2), seq(0), seq(0), cwspec, vec, mat, vec, mat,
                  vec, vec, pl.BlockSpec(memory_space=pl.ANY)],
        out_specs=[pair, cwspec, vec3, mat, vec3, mat, vec3, vec3],
        out_shape=[jax.ShapeDtypeStruct(dz.shape, dz.dtype), jax.ShapeDtypeStruct((N_LRU_TILES, CONV_K, LRU_TILE), F32),
                   vshape, mshape, vshape, mshape, vshape, vshape],
        scratch_shapes=[pltpu.VMEM((lp, LRU_TILE), F32)] * 3,
        dims=("parallel", "arbitrary"), args=(z, z, hs, dy, cw, cb, wa, ba, wx, bx, lam, dz), comm=comm,
        aliases={11: 0})


def _mix_out_call(ya, yl, ga, gl, wout, h, next_gain, tm):
    rows, d = h.shape

    def body(ya_ref, yl_ref, ga_ref, gl_ref, w_ref, h_ref, ng_ref, y_ref, o_ref, u_ref):
        a = ya_ref[...]
        l = yl_ref[...]
        an = (a * _rms(a, MLA_W) * ga_ref[...]).astype(BF16)
        ln = (l * _rms(l, LRU_W) * gl_ref[...]).astype(BF16)
        y_ref[:, 0:MLA_W] = an
        y_ref[:, MLA_W:MLA_W + LRU_W] = ln
        out = h_ref[...] + _dot(an, w_ref[0:MLA_W, :]) + _dot(ln, w_ref[MLA_W:MLA_W + LRU_W, :])
        o_ref[...] = out
        u_ref[...] = (out * _rms(out, d) * ng_ref[...]).astype(BF16)

    half = pl.BlockSpec((tm, MLA_W), lambda i: (i, 0))
    g = pl.BlockSpec((1, MLA_W), lambda i: (0, 0))
    full = pl.BlockSpec((tm, d), lambda i: (i, 0))
    return pl.pallas_call(
        body, name="mix_out", grid=(rows // tm,),
        in_specs=[half, half, g, g, pl.BlockSpec((MLA_W + LRU_W, d), lambda i: (0, 0)), full,
                  pl.BlockSpec((1, d), lambda i: (0, 0))],
        out_specs=[full, full, full],
        out_shape=[jax.ShapeDtypeStruct((rows, MLA_W + LRU_W), BF16), jax.ShapeDtypeStruct((rows, d), F32),
                   jax.ShapeDtypeStruct((rows, d), BF16)],
        compiler_params=_params(("parallel",)))(ya, yl, ga, gl, wout, h, next_gain)


def _ffn_dact_call(name, dhb, wd, gate, up, tm, comm=None):
    rows, d = dhb.shape
    ns, fs, _ = wd.shape

    nsub = 2 if tm % 32 == 0 else 1
    sub = tm // nsub

    def body(dh_ref, wd_ref, g_ref, p_ref, dg_ref, dp_ref):
        wd = wd_ref[0]
        for r in range(nsub):
            rs = slice(r * sub, (r + 1) * sub)
            da = (0.5 * _dot_nt(dh_ref[rs, :], wd)).astype(BF16)
            g = g_ref[0, rs, :]
            p = p_ref[0, rs, :]
            sg = jax.nn.sigmoid(g)
            dg_ref[0, rs, :] = (da * p) * (sg * (1.0 + g * (1.0 - sg)))
            dp_ref[0, rs, :] = da * (g * sg)

    aspec = pl.BlockSpec((1, tm, fs), lambda s, i: (s, i, 0))
    oshape = jax.ShapeDtypeStruct((ns, rows, fs), BF16)
    return _hosted_call(
        body, name=name, grid=(ns, rows // tm),
        in_specs=[pl.BlockSpec((tm, d), lambda s, i: (i, 0)), pl.BlockSpec((1, fs, d), lambda s, i: (s, 0, 0)),
                  aspec, aspec],
        out_specs=[aspec, aspec], out_shape=[oshape, oshape],
        dims=("parallel", "parallel"), args=(dhb, wd, gate, up), comm=comm)


def _norm_in_bwd_call(name, pieces, h, g, dres, tm, comm=None, part=(0, 1), prev=None, mix=None):
    rows, d = h.shape
    npc = len(pieces)
    steps = rows // tm // part[1]
    off = part[0] * steps
    n_prev = 0 if prev is None else 2
    n_mix = 0 if mix is None else 5

    def body(*refs):
        d_refs = refs[0:2 * npc:2]
        w_refs = refs[1:2 * npc:2]
        h_ref, g_ref, dres_ref = refs[2 * npc:2 * npc + 3]
        mix_in_refs = refs[2 * npc + 3:2 * npc + 3 + n_mix]
        dh_ref, dhb_ref, dg_ref = refs[2 * npc + 3 + n_mix + n_prev:2 * npc + 6 + n_mix + n_prev]
        mix_out_refs = refs[2 * npc + 6 + n_mix + n_prev:]
        du = jnp.zeros((tm, d), F32)
        for d_ref, w_ref in zip(d_refs, w_refs):
            if len(d_ref.shape) == 3:
                for s in range(d_ref.shape[0]):
                    du = du + _dot(d_ref[s], w_ref[s])
            else:
                du = du + _dot(d_ref[...], w_ref[...])
        x = h_ref[...]
        r = _rms(x, d)
        n = x * r
        dh = dres_ref[...] + _rms_bwd(du * g_ref[...], n, r, d)
        dhb = dh.astype(BF16)
        dh_ref[...] = dh
        dhb_ref[...] = dhb
        sums = [(dg_ref, jnp.sum(du * n, axis=0, keepdims=True))]
        if mix is not None:
            wo_ref, ya_ref, yl_ref, ga_ref, gl_ref = mix_in_refs
            dya_ref, dyl_ref, dga_ref, dgl_ref = mix_out_refs
            dy = _dot_nt(dhb, wo_ref[...])
            for val, gain_ref, lo, out_ref, acc_ref in ((ya_ref[...], ga_ref, 0, dya_ref, dga_ref),
                                                        (yl_ref[...], gl_ref, MLA_W, dyl_ref, dgl_ref)):
                rb = _rms(val, MLA_W)
                nb_ = val * rb
                dyn = dy[:, lo:lo + MLA_W]
                out_ref[...] = _rms_bwd(dyn * gain_ref[...], nb_, rb, MLA_W)
                sums.append((acc_ref, jnp.sum(dyn * nb_, axis=0, keepdims=True)))

        @pl.when(pl.program_id(0) == 0)
        def _():
            for ref, val in sums:
                ref[...] = val

        @pl.when(pl.program_id(0) != 0)
        def _():
            for ref, val in sums:
                ref[...] += val

    in_specs, args = [], []
    for dd, w in pieces:
        if dd.ndim == 3:
            in_specs.append(pl.BlockSpec((dd.shape[0], tm, dd.shape[2]), lambda i: (0, i + off, 0)))
            in_specs.append(_resident(w.shape))
        else:
            in_specs.append(pl.BlockSpec((tm, dd.shape[1]), lambda i: (i + off, 0)))
            in_specs.append(_resident(w.shape))
        args += [dd, w]
    full = pl.BlockSpec((tm, d), lambda i: (i + off, 0))
    gspec = pl.BlockSpec((1, d), lambda i: (0, 0))
    half = pl.BlockSpec((tm, MLA_W), lambda i: (i + off, 0))
    hgain = pl.BlockSpec((1, MLA_W), lambda i: (0, 0))
    mix_in_specs = [] if mix is None else [_resident(mix[0].shape), half, half, hgain, hgain]
    mix_out_specs = [] if mix is None else [half, half, hgain, hgain]
    mix_out_shapes = [] if mix is None else [
        jax.ShapeDtypeStruct((rows, MLA_W), F32), jax.ShapeDtypeStruct((rows, LRU_W), F32),
        jax.ShapeDtypeStruct((1, MLA_W), F32), jax.ShapeDtypeStruct((1, LRU_W), F32)]
    n_in = len(in_specs) + 3 + n_mix
    return _hosted_call(
        body, name=name, grid=(steps,),
        in_specs=in_specs + [full, gspec, full] + mix_in_specs + _any_specs(n_prev),
        out_specs=[full, full, gspec] + mix_out_specs,
        out_shape=[jax.ShapeDtypeStruct((rows, d), F32), jax.ShapeDtypeStruct((rows, d), BF16),
                   jax.ShapeDtypeStruct((1, d), F32)] + mix_out_shapes,
        args=(*args, h, g, dres, *(mix or ()), *(prev or ())), comm=comm,
        aliases={n_in: 0, n_in + 1: 1} if prev is not None else {})


def _wgrad_call(name, a, b, scale=1.0, comm=None, bf16_copy=False):
    a3, b3 = a.ndim == 3, b.ndim == 3
    ns = a.shape[0] if a3 else (b.shape[0] if b3 else 1)
    rows, m = a.shape[-2:]
    n = b.shape[-1]
    tmm = m if a3 else _col_tile(m, 256)

    def body(a_ref, b_ref, *o_refs):
        av = a_ref[0] if a3 else a_ref[...]
        bv = b_ref[0] if b3 else b_ref[...]
        res = _dot_tn(av, bv)
        if scale != 1.0:
            res = res * scale
        for o_ref in o_refs:
            if a3 or b3:
                o_ref[0] = res.astype(o_ref.dtype)
            else:
                o_ref[...] = res.astype(o_ref.dtype)

    aspec = (pl.BlockSpec((1, rows, tmm), lambda s, j: (s, 0, j)) if a3
             else pl.BlockSpec((rows, tmm), lambda s, j: (0, j)))
    bspec = (pl.BlockSpec((1, rows, n), lambda s, j: (s, 0, 0)) if b3
             else pl.BlockSpec((rows, n), lambda s, j: (0, 0)))
    if a3 or b3:
        ospec = pl.BlockSpec((1, tmm, n), lambda s, j: (s, j, 0))
        oshape = jax.ShapeDtypeStruct((ns, m, n), F32)
    else:
        ospec = pl.BlockSpec((tmm, n), lambda s, j: (j, 0))
        oshape = jax.ShapeDtypeStruct((m, n), F32)
    shapes = [oshape] + ([jax.ShapeDtypeStruct(oshape.shape, BF16)] if bf16_copy else [])
    res = _hosted_call(
        body, name=name, grid=(ns, m // tmm), in_specs=[aspec, bspec], out_specs=[ospec] * len(shapes),
        out_shape=shapes, dims=("parallel", "parallel"), args=(a, b), comm=comm)
    return res if bf16_copy else res[0]


def _mla_prep_bwd_call(z, dq, dk, dv, gql, gkvl, gqh, gkh, wuq, wuk, wuv, tabs, lp, tm, comm=None):
    rows = z.shape[0]
    tpe = lp // tm

    def body(z_ref, dq_ref, dk_ref, dv_ref, gql_ref, gkvl_ref, gqh_ref, gkh_ref, wuq_ref, wuk_ref, wuv_ref,
             c_ref, s1_ref, s2_ref, dz_ref, dgql_ref, dgkvl_ref, dgqh_ref, dgkh_ref, dwuq_ref, dwuk_ref, dwuv_ref,
             dqp_ref, dkn_ref, dvv_ref):
        gql, gkvl = gql_ref[...], gkvl_ref[...]
        gq, gk = gqh_ref[...], gkh_ref[...]
        wuq, wuk, wuv = wuq_ref[...], wuk_ref[...], wuv_ref[...]
        m = _mla_heads(z_ref[...], gql, gkvl, wuq, wuk, wuv)
        c, s1, s2 = c_ref[...], s1_ref[...], s2_ref[...]
        dgq = jnp.zeros((1, D_QKP), F32)
        dgk = jnp.zeros((1, D_QKP), F32)
        dkr = jnp.zeros((tm, D_QKP - D_NOPE), F32)
        for hd in range(HEADS):
            qn, rqh, knn, krn, rkh = m["heads"][hd]
            dqg = jnp.concatenate([dq_ref[hd, :, 0:D_NOPE].astype(F32),
                                   _rope_t(dq_ref[hd, :, D_NOPE:D_QKP].astype(F32), c, s1, s2)], axis=1)
            dgq = dgq + jnp.sum(dqg * qn, axis=0, keepdims=True)
            dqn = dqg * gq
            dqr = rqh * (dqn - qn * (jnp.sum(dqn * qn, axis=-1, keepdims=True) * (1.0 / D_QK)))
            dqp_ref[:, hd * D_QKP:(hd + 1) * D_QKP] = dqr.astype(BF16)
            kn_full = jnp.concatenate([knn, krn], axis=1)
            dkg = jnp.concatenate([dk_ref[hd, :, 0:D_NOPE].astype(F32),
                                   _rope_t(dk_ref[hd, :, D_NOPE:D_QKP].astype(F32), c, s1, s2)], axis=1)
            dgk = dgk + jnp.sum(dkg * kn_full, axis=0, keepdims=True)
            dkn = dkg * gk
            dkraw = rkh * (dkn - kn_full * (jnp.sum(dkn * kn_full, axis=-1, keepdims=True) * (1.0 / D_QK)))
            dkn_ref[:, hd * D_NOPE:(hd + 1) * D_NOPE] = dkraw[:, 0:D_NOPE].astype(BF16)
            dkr = dkr + dkraw[:, D_NOPE:D_QKP]
            dvv_ref[:, hd * D_V:(hd + 1) * D_V] = dv_ref[hd]
        dcqn = _dot(dqp_ref[...], wuq)
        dckvn = _dot_nt(dkn_ref[...], wuk) + _dot_nt(dvv_ref[...], wuv)
        dz_ref[:, 0:Q_RANK] = _rms_bwd(dcqn * gql, m["nq"], m["rq"], Q_RANK).astype(BF16)
        dz_ref[:, Q_RANK:Z_KR] = _rms_bwd(dckvn * gkvl, m["nkv"], m["rkv"], KV_RANK).astype(BF16)
        dz_ref[:, Z_KR:Z_MLA] = dkr.astype(BF16)
        parts = [(dgql_ref, jnp.sum(dcqn * m["nq"], axis=0, keepdims=True)),
                 (dgkvl_ref, jnp.sum(dckvn * m["nkv"], axis=0, keepdims=True)), (dgqh_ref, dgq), (dgkh_ref, dgk),
                 (dwuq_ref, _dot_tn(dqp_ref[...], m["cqn"])), (dwuk_ref, _dot_tn(m["ckvn"], dkn_ref[...])),
                 (dwuv_ref, _dot_tn(m["ckvn"], dvv_ref[...]))]

        @pl.when(pl.program_id(0) == 0)
        def _():
            for ref, val in parts:
                ref[...] = val

        @pl.when(pl.program_id(0) != 0)
        def _():
            for ref, val in parts:
                ref[...] += val

    def const(shape):
        return pl.BlockSpec(shape, lambda i: tuple(0 for _ in shape))

    tab = pl.BlockSpec((tm, 128), lambda i: (i % tpe, 0))
    hq = pl.BlockSpec((HEADS, tm, D_QKP), lambda i: (0, i, 0))
    hv = pl.BlockSpec((HEADS, tm, D_V), lambda i: (0, i, 0))

    def rowspec(n):
        return pl.BlockSpec((tm, n), lambda i: (i, 0))

    return _hosted_call(
        body, name="mla_prep_bwd", grid=(rows // tm,),
        in_specs=[rowspec(Z_MLA), hq, hq, hv, const((1, Q_RANK)), const((1, KV_RANK)), const((1, D_QKP)),
                  const((1, D_QKP)), const((HEADS * D_QKP, Q_RANK)), const((KV_RANK, HEADS * D_NOPE)),
                  const((KV_RANK, HEADS * D_V)), tab, tab, tab],
        out_specs=[rowspec(Z_MLA), const((1, Q_RANK)), const((1, KV_RANK)), const((1, D_QKP)), const((1, D_QKP)),
                   const((HEADS * D_QKP, Q_RANK)), const((KV_RANK, HEADS * D_NOPE)), const((KV_RANK, HEADS * D_V))],
        out_shape=[jax.ShapeDtypeStruct((rows, Z_W), BF16),
                   jax.ShapeDtypeStruct((1, Q_RANK), F32), jax.ShapeDtypeStruct((1, KV_RANK), F32),
                   jax.ShapeDtypeStruct((1, D_QKP), F32), jax.ShapeDtypeStruct((1, D_QKP), F32),
                   jax.ShapeDtypeStruct((HEADS * D_QKP, Q_RANK), F32),
                   jax.ShapeDtypeStruct((KV_RANK, HEADS * D_NOPE), F32), jax.ShapeDtypeStruct((KV_RANK, HEADS * D_V), F32)],
        scratch_shapes=[pltpu.VMEM((tm, HEADS * D_QKP), BF16), pltpu.VMEM((tm, HEADS * D_NOPE), BF16),
                        pltpu.VMEM((tm, HEADS * D_V), BF16)],
        args=(z, dq, dk, dv, gql, gkvl, gqh, gkh, wuq, wuk, wuv, *tabs), comm=comm)


def _local_step(h0, target, w, nb, lp, sched=None):
    tm = _row_tile(nb * lp, MATMUL_ROWS)
    te = _row_tile(lp, ELEMENTWISE_ROWS)
    tabs = _rope_tables(lp)
    g = {}
    if sched is None:
        host = lambda stage: None
    else:
        sched.g = g
        host = sched.host

    def ffn_act(tag, u):
        gate, up, act = _ffn_up_call(tag + "_up", u, w[tag + "_w_gate"], w[tag + "_w_up"], tm, host(tag + "_up"))
        return u, gate, up, act

    def ffn_bwd(tag, h, saved, dh, dhb, split, mix=None):
        u, gate, up, act = saved
        dgate, dup = _ffn_dact_call(tag + "_dact", dhb, w[tag + "_w_down"], gate, up, tm, host(tag + "_dact"))
        for nm, call, a, b, scale in (("_w_down", "_dwd", act, dhb, 0.5), ("_w_gate", "_dwg", dgate, u, 1.0),
                                      ("_w_up", "_dwu", dup, u, 1.0)):
            g[tag + nm], g[tag + nm + BF16_COPY] = _wgrad_call(tag + call, a, b, scale, host(tag + call),
                                                               bf16_copy=True)
        pieces = [(dgate, w[tag + "_w_gate"]), (dup, w[tag + "_w_up"])]
        if not split:
            res = _norm_in_bwd_call(tag + "_din", pieces, h, w[tag + "_norm"], dh, te, host(tag + "_din"), mix=mix)
            g[tag + "_norm"] = res[2]
            return (res[0], res[1], *res[3:])
        dh_a, dhb_a, dg_a = _norm_in_bwd_call(tag + "_din_a", pieces, h, w[tag + "_norm"], dh, te,
                                              host(tag + "_din_a"), part=(0, 2))
        dh_in, dhb_in, dg_b = _norm_in_bwd_call(tag + "_din_b", pieces, h, w[tag + "_norm"], dh, te,
                                                host(tag + "_din_b"), part=(1, 2), prev=(dh_a, dhb_a))
        g[tag + "_norm"] = dg_a + dg_b
        return dh_in, dhb_in

    s1 = ffn_act("ffn1", _rmsnorm_call("ffn1_norm", h0, w["ffn1_norm"], te, host("ffn1_norm")))
    h1, un = _ffn_down_call("ffn1_down", s1[3], w["ffn1_w_down"], h0, te, host("ffn1_down"), w["mix_norm"])
    z = _mm_call("mix_in", un, w["w_in"], tm, F32)
    mla_w = (w["q_latent_norm"], w["kv_latent_norm"], w["q_head_norm"], w["k_head_norm"], w["w_uq"], w["w_uk"],
             w["w_uv"])
    q, k, v = _mla_prep_call(z, *mla_w, tabs, lp, te)
    o, lse = _attn_fwd_call(q, k, v, nb, lp, host("attn_fwd"))
    lru_w = (w["conv_w"], w["conv_b"], w["gate_a_w"], w["gate_a_b"], w["gate_x_w"], w["gate_x_b"], w["lru_lambda"])
    yl, hs = _lru_fwd_call(z, *lru_w, nb, lp, host("lru_fwd"))
    y, h2, u2 = _mix_out_call(o, yl, w["attn_out_norm"], w["lru_out_norm"], w["w_out"], h1, w["ffn2_norm"], te)
    s2 = ffn_act("ffn2", u2)
    dh3, dh3b, g["final_norm"], loss = _ffn_down_loss_call("ffn2_down", s2[3], w["ffn2_w_down"], h2, w["final_norm"],
                                                           target, lp, te)
    g["loss"] = loss

    dh2, dh2b, dya, dyl, g["attn_out_norm"], g["lru_out_norm"] = ffn_bwd(
        "ffn2", h2, s2, dh3, dh3b, False, (w["w_out"], o, yl, w["attn_out_norm"], w["lru_out_norm"]))
    g["w_out"] = _wgrad_call("dw_out", y, dh2b)
    dq, dk, dv = _attn_bwd_call(q, k, v, o, lse, dya, nb, lp, host("attn_bwd"))
    (dz_mla, g["q_latent_norm"], g["kv_latent_norm"], g["q_head_norm"], g["k_head_norm"], g["w_uq"], g["w_uk"],
     g["w_uv"]) = _mla_prep_bwd_call(z, dq, dk, dv, *mla_w, tabs, lp, te, host("mla_prep_bwd"))
    (dz, g["conv_w"], g["conv_b"], g["gate_a_w"], g["gate_a_b"], g["gate_x_w"], g["gate_x_b"],
     g["lru_lambda"]) = _lru_bwd_call(z, dz_mla, hs, dyl, *lru_w, nb, lp, host("lru_bwd"))
    g["w_in"] = _wgrad_call("dw_in", dz, un)
    dh1, dh1b, g["mix_norm"] = _norm_in_bwd_call("mix_din", [(dz, w["w_in"])], h1, w["mix_norm"], dh2, te,
                                                 host("mix_din"))
    dh0 = ffn_bwd("ffn1", h0, s1, dh1, dh1b, True)[0]
    return loss, dh0, g


def _place():
    x, y, c = lax.axis_index("x"), lax.axis_index("y"), lax.axis_index("c")
    return x, y, c, [(1 - x, y), (x, 1 - y), (1 - x, 1 - y)]


def _any_specs(n):
    return [pl.BlockSpec(memory_space=pl.ANY)] * n


def _remote(src, dst, sems, k, dev):
    send_sems, recv_sems, base = sems
    return pltpu.make_async_remote_copy(src_ref=src, dst_ref=dst, send_sem=send_sems.at[base + k],
                                        recv_sem=recv_sems.at[base + k], device_id=dev, device_id_type=MESH)


EW_VMEM_BYTES = 24 * 1024 * 1024


def _fit_rows(rows, cols, blocks):
    return _row_tile(rows, max(16, int(EW_VMEM_BYTES // (8 * blocks)) // cols))


class _Geom:
    def __init__(self, n0, n1, blocks=1.0):
        self.n0, self.n1 = n0, n1
        self.axis = 0 if n0 % 32 == 0 else 1
        self.h0, self.h1 = (n0 // 2, n1) if self.axis == 0 else (n0, n1 // 2)
        self.tr = _fit_rows(self.h0, self.h1, blocks)
        self.nblk = self.h0 // self.tr

    def half_ref(self, ref, lead, idx):
        if self.axis == 0:
            return ref.at[(*lead, pl.ds(idx * self.h0, self.h0))]
        return ref.at[(*lead, slice(None), pl.ds(idx * self.h1, self.h1))]

    def half_block(self, lead, i, idx):
        return (*lead, idx * self.nblk + i, 0) if self.axis == 0 else (*lead, i, idx)


class _Comm:
    def __init__(self, ins, out_shapes, aliases, n_sems, start, finish, deliver):
        self.ins, self.out_shapes, self.aliases, self.n_sems = list(ins), list(out_shapes), dict(aliases), n_sems
        self.start, self.finish, self.deliver = start, finish, deliver

    def scratch(self):
        return [pltpu.SemaphoreType.DMA((self.n_sems,)), pltpu.SemaphoreType.DMA((self.n_sems,))]


def _comm_call(name, comm):
    n_in = len(comm.ins)

    def body(*refs):
        ins, outs, sems = refs[:n_in], refs[n_in:-2], (*refs[-2:], 0)
        comm.start(ins, outs, sems)
        comm.finish(ins, outs, sems)

    res = pl.pallas_call(
        body, name=name, out_shape=comm.out_shapes, in_specs=_any_specs(n_in),
        out_specs=_any_specs(len(comm.out_shapes)), input_output_aliases=comm.aliases,
        scratch_shapes=comm.scratch())(*comm.ins)
    return comm.deliver(list(res))


def _hosted_call(body, *, name, grid, in_specs, out_specs, out_shape, args, scratch_shapes=(), dims=None, comm=None,
                 prefetch=None, aliases=None):
    aliases = dict(aliases or {})
    in_specs, out_specs, out_shape = list(in_specs), list(out_specs), list(out_shape)
    n_pre = 0 if prefetch is None else 1

    def call(fn, in_specs, out_specs, out_shape, scratch, aliases, dims, args):
        if prefetch is None:
            return pl.pallas_call(
                fn, name=name, grid=grid, in_specs=in_specs, out_specs=out_specs, out_shape=out_shape,
                scratch_shapes=scratch, input_output_aliases=aliases, compiler_params=_params(dims))(*args)
        spec = pltpu.PrefetchScalarGridSpec(num_scalar_prefetch=1, grid=grid, in_specs=in_specs, out_specs=out_specs,
                                            scratch_shapes=scratch)
        return pl.pallas_call(
            fn, name=name, grid_spec=spec, out_shape=out_shape,
            input_output_aliases={i + 1: o for i, o in aliases.items()}, compiler_params=_params(dims))(prefetch, *args)

    if comm is None:
        return list(call(body, in_specs, out_specs, out_shape, list(scratch_shapes), aliases,
                         dims or ("arbitrary",) * len(grid), args))
    n_in, n_out, n_ci, n_co = len(in_specs), len(out_specs), len(comm.ins), len(comm.out_shapes)

    def wrapped(*refs):
        pre, refs = refs[:n_pre], refs[n_pre:]
        ins, cins = refs[:n_in], refs[n_in:n_in + n_ci]
        outs = refs[n_in + n_ci:n_in + n_ci + n_out]
        couts = refs[n_in + n_ci + n_out:n_in + n_ci + n_out + n_co]
        scratch, sems = refs[n_in + n_ci + n_out + n_co:-2], (*refs[-2:], 0)
        first = functools.reduce(jnp.logical_and, [pl.program_id(k) == 0 for k in range(len(grid))])
        last = functools.reduce(jnp.logical_and, [pl.program_id(k) == grid[k] - 1 for k in range(len(grid))])

        @pl.when(first)
        def _():
            comm.start(cins, couts, sems)

        body(*pre, *ins, *outs, *scratch)

        @pl.when(last)
        def _():
            comm.finish(cins, couts, sems)

    res = call(wrapped, in_specs + _any_specs(n_ci), out_specs + _any_specs(n_co), out_shape + comm.out_shapes,
               list(scratch_shapes) + comm.scratch(),
               {**aliases, **{n_in + i: n_out + o for i, o in comm.aliases.items()}},
               ("arbitrary",) * len(grid), (*args, *comm.ins))
    comm.deliver(list(res[n_out:]))
    return list(res[:n_out])


def _gather_comm(bufs, deliver):
    n = len(bufs)
    geoms = [_Geom(*b.shape[1:]) for b in bufs]

    def first(outs, sems):
        x, y, c, chips = _place()
        cps = []
        for a in range(n):
            mine = geoms[a].half_ref(outs[a], (2 * x + y,), c)
            cps += [_remote(mine, mine, sems, 6 * a + j, (cx, cy, c)) for j, (cx, cy) in enumerate(chips)]
        return cps

    def start(ins, outs, sems):
        for cp in first(outs, sems):
            cp.start()

    def finish(ins, outs, sems):
        x, y, c, chips = _place()
        sib = (x, y, 1 - c)
        passed = []
        for a in range(n):
            for j, (cx, cy) in enumerate(chips):
                land = geoms[a].half_ref(outs[a], (2 * cx + cy,), c)
                _remote(land, land, sems, 6 * a + j, sib).wait_recv()
                cp = _remote(land, land, sems, 6 * a + 3 + j, sib)
                cp.start()
                passed.append(cp)
        for a in range(n):
            for j, (cx, cy) in enumerate(chips):
                land = geoms[a].half_ref(outs[a], (2 * cx + cy,), 1 - c)
                _remote(land, land, sems, 6 * a + 3 + j, sib).wait_recv()
        for cp in first(outs, sems) + passed:
            cp.wait_send()

    return _Comm(bufs, [jax.ShapeDtypeStruct(b.shape, b.dtype) for b in bufs], {a: a for a in range(n)}, 6 * n,
                 start, finish, deliver)


def _reduce_pair_comm(grads, deliver):
    n = len(grads)
    geoms = [_Geom(*a.shape[1:]) for a in grads]

    def copies(ins, outs, sems):
        x, y, c, _ = _place()
        return [_remote(geoms[a].half_ref(ins[a], (slice(None),), 1 - c), outs[a], sems, a, (x, y, 1 - c))
                for a in range(n)]

    def start(ins, outs, sems):
        for cp in copies(ins, outs, sems):
            cp.start()

    def finish(ins, outs, sems):
        cps = copies(ins, outs, sems)
        for cp in cps:
            cp.wait_recv()
        for cp in cps:
            cp.wait_send()

    shapes = [jax.ShapeDtypeStruct((N_SHARD, g.h0, g.h1), a.dtype) for a, g in zip(grads, geoms)]
    return _Comm(grads, shapes, {}, n, start, finish, deliver)


def _reduce_chips_comm(parts, deliver):
    n = len(parts)

    def copies(ins, outs, sems):
        x, y, c, chips = _place()
        return [_remote(ins[a].at[2 * cx + cy], outs[a].at[j], sems, 3 * a + j, (cx, cy, c))
                for a in range(n) for j, (cx, cy) in enumerate(chips)]

    def start(ins, outs, sems):
        for cp in copies(ins, outs, sems):
            cp.start()

    def finish(ins, outs, sems):
        cps = copies(ins, outs, sems)
        for cp in cps:
            cp.wait_recv()
        for cp in cps:
            cp.wait_send()

    shapes = [jax.ShapeDtypeStruct((3,) + a.shape[1:], a.dtype) for a in parts]
    return _Comm(parts, shapes, {}, 3 * n, start, finish, deliver)


def _share_pair_comm(bufs, deliver):
    n = len(bufs)
    geoms = [_Geom(*b.shape) for b in bufs]

    def copies(outs, sems):
        x, y, c, _ = _place()
        cps = []
        for a in range(n):
            mine = geoms[a].half_ref(outs[a], (), c)
            cps.append(_remote(mine, mine, sems, a, (x, y, 1 - c)))
        return cps

    def start(ins, outs, sems):
        for cp in copies(outs, sems):
            cp.start()

    def finish(ins, outs, sems):
        x, y, c, _ = _place()
        for a in range(n):
            land = geoms[a].half_ref(outs[a], (), 1 - c)
            _remote(land, land, sems, a, (x, y, 1 - c)).wait_recv()
        for cp in copies(outs, sems):
            cp.wait_send()

    return _Comm(bufs, [jax.ShapeDtypeStruct(b.shape, b.dtype) for b in bufs], {a: a for a in range(n)}, n,
                 start, finish, deliver)


def _small_comm(pack, deliver):
    r, d = pack.shape

    def copies(ins, outs, sems):
        x, y, c, _ = _place()
        cps = []
        for k in range(1, 8):
            peer = (x ^ ((k >> 2) & 1), y ^ ((k >> 1) & 1), c ^ (k & 1))
            cps.append(_remote(ins[0], outs[0].at[4 * x + 2 * y + c], sems, k - 1, peer))
        return cps

    def start(ins, outs, sems):
        for cp in copies(ins, outs, sems):
            cp.start()

    def finish(ins, outs, sems):
        cps = copies(ins, outs, sems)
        for cp in cps:
            cp.wait_recv()
        for cp in cps:
            cp.wait_send()

    return _Comm([pack], [jax.ShapeDtypeStruct((8, r, d), pack.dtype)], {}, 7, start, finish, deliver)


def _join_comms(comms):
    comms = [c for c in comms if c is not None]
    if len(comms) <= 1:
        return comms[0] if comms else None
    ins, out_shapes, aliases, spans, n_sems = [], [], {}, [], 0
    for c in comms:
        aliases.update({len(ins) + i: len(out_shapes) + o for i, o in c.aliases.items()})
        spans.append((len(ins), len(ins) + len(c.ins), len(out_shapes), len(out_shapes) + len(c.out_shapes), n_sems))
        ins += c.ins
        out_shapes += c.out_shapes
        n_sems += c.n_sems

    def run(which):
        def go(all_ins, all_outs, sems):
            for c, (i0, i1, o0, o1, base) in zip(comms, spans):
                getattr(c, which)(all_ins[i0:i1], all_outs[o0:o1], (sems[0], sems[1], sems[2] + base))
        return go

    def deliver(outs):
        for c, (_, _, o0, o1, _) in zip(comms, spans):
            c.deliver(outs[o0:o1])
        return outs

    return _Comm(ins, out_shapes, aliases, n_sems, run("start"), run("finish"), deliver)


def _ew_call(name, fn, ins, out_dtypes):
    shape = ins[0].shape
    cols = shape[-1]
    rows = 1
    for s_ in shape[:-1]:
        rows *= s_
    ins2 = [a.reshape(rows, cols) for a in ins]
    tr = rows
    for t in range(16, min(rows, max(16, (1 << 19) // cols)) + 1, 16):
        if rows % t == 0:
            tr = t
    no = len(out_dtypes)

    def body(*refs):
        outs = fn(*[r[...] for r in refs[:len(ins2)]])
        for ref, val in zip(refs[len(ins2):], outs):
            ref[...] = val.astype(ref.dtype)

    spec = pl.BlockSpec((tr, cols), lambda i: (i, 0))
    res = pl.pallas_call(
        body, name=name, grid=(rows // tr,), in_specs=[spec] * len(ins2), out_specs=[spec] * no,
        out_shape=[jax.ShapeDtypeStruct((rows, cols), dt) for dt in out_dtypes],
        compiler_params=_params(("parallel",)))(*ins2)
    return [r.reshape(shape) for r in res]


def _adamw_math(w, g, m, v):
    m = ADAM_B1 * m + (1.0 - ADAM_B1) * g
    v = ADAM_B2 * v + (1.0 - ADAM_B2) * (g * g)
    m_hat = m / (1.0 - ADAM_B1 ** ADAM_STEP)
    v_hat = v / (1.0 - ADAM_B2 ** ADAM_STEP)
    delta = -ADAM_LR * (m_hat / (jnp.sqrt(v_hat) + ADAM_EPS) + ADAM_WD * w)
    return delta, m, v


def _adamw_call(name, w, g, m, v):
    return _ew_call(name, _adamw_math, [w, g, m, v], [F32, F32, F32])


def _tiled_call(name, fn, place, grid, in_items, out_items, comm=None):
    ni = len(in_items)

    def body(place_ref, *refs):
        vals = fn(*[r[...] for r in refs[:ni]])
        for ref, val in zip(refs[ni:], vals):
            ref[...] = val.astype(ref.dtype)

    return _hosted_call(
        body, name=name, grid=grid, in_specs=[pl.BlockSpec(blk, imap) for _, blk, imap in in_items],
        out_specs=[pl.BlockSpec(blk, imap) for _, _, blk, imap in out_items],
        out_shape=[jax.ShapeDtypeStruct(shp, dt) for shp, dt, _, _ in out_items],
        args=[a for a, _, _ in in_items], prefetch=place, comm=comm)


def _cast_call(name, place, shards, comm=None):
    n0, n1 = shards[0].shape
    tr = _fit_rows(n0, n1, 1.5 * len(shards))
    ins = [(a, (tr, n1), lambda i, p: (i, 0)) for a in shards]
    outs = [((N_SHARD, n0, n1), BF16, (1, tr, n1), lambda i, p: (p[0], i, 0)) for _ in shards]
    return _tiled_call(name, lambda *v: [x[None] for x in v], place, (n0 // tr,), ins, outs, comm)


def _pair_sum_call(name, place, fulls, gots):
    k = len(fulls)
    g = _Geom(*fulls[0].shape[1:], blocks=2.5 * k)
    blk = (1, g.tr, g.h1)
    ins = [(a, blk, lambda s, i, p: g.half_block((s,), i, p[1])) for a in fulls]
    ins += [(a, blk, lambda s, i, p: (s, i, 0)) for a in gots]
    outs = [((N_SHARD, g.h0, g.h1), BF16, blk, lambda s, i, p: (s, i, 0)) for _ in fulls]
    return _tiled_call(name, lambda *v: [v[j] + v[k + j] for j in range(k)], place, (N_SHARD, g.nblk), ins, outs)


def _chip_sum_call(name, place, fulls, gots, recvs, comm=None):
    k = len(fulls)
    g = _Geom(*fulls[0].shape[1:], blocks=4.5 * k)
    blk = (1, g.tr, g.h1)
    ins = [(a, blk, lambda i, p: g.half_block((p[0],), i, p[1])) for a in fulls]
    ins += [(a, blk, lambda i, p: (p[0], i, 0)) for a in gots]
    ins += [(a, (3, g.tr, g.h1), lambda i, p: (0, i, 0)) for a in recvs]
    outs = [((g.n0, g.n1), F32, (g.tr, g.h1), lambda i, p: g.half_block((), i, p[1])) for _ in fulls]

    def fn(*v):
        res = []
        for j in range(k):
            r = v[2 * k + j].astype(F32)
            res.append(v[j][0] + v[k + j][0] + r[0] + r[1] + r[2])
        return res

    return _tiled_call(name, fn, place, (g.nblk,), ins, outs, comm)


def _adamw_group_call(name, ws, gs, ms, vs, comm=None):
    k = len(ws)
    n0, n1 = ws[0].shape
    tr = _fit_rows(n0, n1, 8 * k)
    spec = pl.BlockSpec((tr, n1), lambda i: (i, 0))

    def body(*refs):
        for j in range(k):
            g = refs[k + j][...]
            delta, m, vv = _adamw_math(refs[j][...], g, refs[2 * k + j][...], refs[3 * k + j][...])
            for ref, val in zip(refs[4 * k + 4 * j:4 * k + 4 * j + 4], (g, delta, m, vv)):
                ref[...] = val

    flat = _hosted_call(
        body, name=name, grid=(n0 // tr,), in_specs=[spec] * (4 * k), out_specs=[spec] * (4 * k),
        out_shape=[jax.ShapeDtypeStruct((n0, n1), F32)] * (4 * k), dims=("parallel",),
        args=(*ws, *gs, *ms, *vs), comm=comm)
    return [flat[4 * j:4 * j + 4] for j in range(k)]


def _small_update_call(me, early, own_early, late, own_late, states):
    nd, r, d = early.shape
    widths = [a.shape[1] for a in states[0][0]]
    nw = len(widths)
    n_state = nw + 2

    def body(me_ref, e_ref, oe_ref, l_ref, ol_ref, *refs):
        state_refs, refs = refs[:3 * n_state], refs[3 * n_state:]
        (gs_ref, d_ref, nm_ref, nv_ref), rows, packs = refs[:4], refs[4:4 + 4 * nw], refs[4 + 4 * nw:]
        for kind, pack in enumerate(packs):
            srefs = state_refs[kind * n_state:(kind + 1) * n_state]
            pack[...] = jnp.zeros_like(pack)
            for k, width in enumerate(widths):
                pack[k:k + 1, 0:width] = srefs[k][...]
            pack[ROW_GATE_A:ROW_GATE_A + 32, :] = srefs[nw][...]
            pack[ROW_GATE_X:ROW_GATE_X + 32, :] = srefs[nw + 1][...]
        w_ref, m_ref, v_ref = packs
        mine = me_ref[0]

        def total(g_ref, own_ref):
            acc = None
            for k in range(nd):
                part = jnp.where(mine == k, own_ref[...], g_ref[k])
                acc = part if acc is None else acc + part
            return acc

        gs = total(e_ref, oe_ref)
        ls = total(l_ref, ol_ref)
        gs_ref[...] = gs
        first = gs[0:8] + ls[0:8]
        gs_ref[0:8, :] = first
        gs_ref[ROW_META:ROW_META + N_META, :] = gs[ROW_META:ROW_META + N_META] + ls[8:8 + N_META]
        grads = jnp.concatenate([first, gs[8:SMALL_ADAM_ROWS]], axis=0)
        delta, m, v = _adamw_math(w_ref[...], grads, m_ref[...], v_ref[...])
        d_ref[...] = delta
        nm_ref[...] = m
        nv_ref[...] = v
        for kind, pack_ref in enumerate((gs_ref, d_ref, nm_ref, nv_ref)):
            for k, width in enumerate(widths):
                rows[kind * nw + k][...] = pack_ref[k:k + 1, 0:width]

    vm = pl.BlockSpec(memory_space=pltpu.VMEM)
    ashape = jax.ShapeDtypeStruct((SMALL_ADAM_ROWS, d), F32)
    row_shapes = [jax.ShapeDtypeStruct((1, width), F32) for _ in range(4) for width in widths]
    flat_states = [a for rows_, ga, gx in states for a in (*rows_, ga, gx)]
    res = pl.pallas_call(
        body, name="small_update", in_specs=[pl.BlockSpec(memory_space=pltpu.SMEM)] + [vm] * (4 + 3 * n_state),
        out_specs=[vm] * (4 + 4 * nw),
        out_shape=[jax.ShapeDtypeStruct((r, d), F32), ashape, ashape, ashape] + row_shapes,
        scratch_shapes=[pltpu.VMEM((SMALL_ADAM_ROWS, d), F32)] * 3,
        compiler_params=pltpu.CompilerParams(vmem_limit_bytes=VMEM_LIMIT_BYTES))(
            me, early, own_early, late, own_late, *flat_states)
    return res[:4], [res[4 + kind * nw:4 + (kind + 1) * nw] for kind in range(4)]


SMALL_NAMES = ["ffn1_norm", "mix_norm", "ffn2_norm", "final_norm", "q_latent_norm", "kv_latent_norm",
               "q_head_norm", "k_head_norm", "conv_b", "gate_a_b", "gate_x_b", "lru_lambda", "attn_out_norm",
               "lru_out_norm"]
ROW_CONV_W = 14
ROW_GATE_A = 16
ROW_GATE_X = 48
ROW_META = 80
ROW_LOSS = 96


def _row(a):
    flat = a.reshape(1, -1)
    return jnp.pad(flat, ((0, 0), (0, D_MODEL - flat.shape[1])))


def _pack_small(t, rows):
    parts = [_row(t[nm]) for nm in SMALL_NAMES]
    parts.append(t["conv_w"].reshape(2, D_MODEL))
    parts.append(t["gate_a_w"].reshape(32, D_MODEL))
    parts.append(t["gate_x_w"].reshape(32, D_MODEL))
    p = jnp.concatenate(parts, axis=0)
    return jnp.pad(p, ((0, rows - p.shape[0]), (0, 0)))


def _early_pack(g):
    gs = {nm: g.get(nm, jnp.zeros((1, D_MODEL), F32)) for nm in SMALL_NAMES}
    gs["q_head_norm"] = g["q_head_norm"][:, 0:D_QK]
    gs["k_head_norm"] = g["k_head_norm"][:, 0:D_QK]
    for nm in ("conv_b", "gate_a_b", "gate_x_b", "lru_lambda"):
        gs[nm] = g[nm].reshape(1, LRU_W)
    gs["conv_w"] = g["conv_w"].transpose(1, 0, 2).reshape(CONV_K, LRU_W)
    gs["gate_a_w"] = _gate_blocks(g["gate_a_w"])
    gs["gate_x_w"] = _gate_blocks(g["gate_x_w"])
    return jnp.concatenate([_pack_small(gs, ROW_META), jnp.zeros((N_META, D_MODEL), F32), _row(g["loss"][:, 0:1]),
                            jnp.zeros((SMALL_ROWS - ROW_LOSS - 1, D_MODEL), F32)], axis=0)


def _unpack_small(p, rows, like):
    out = dict(zip(SMALL_NAMES, rows))
    out["gate_a_w"] = p[ROW_GATE_A:ROW_GATE_A + 32].reshape(like["gate_a_w"].shape)
    out["gate_x_w"] = p[ROW_GATE_X:ROW_GATE_X + 32].reshape(like["gate_x_w"].shape)
    return out


def _gate_dense(wg):
    w4 = wg[0].reshape(N_LRU_TILES, 2, 64, 64)
    zero = jnp.zeros((N_LRU_TILES, 64, 64), wg.dtype)
    top = jnp.concatenate([w4[:, 0], zero], axis=2)
    bot = jnp.concatenate([zero, w4[:, 1]], axis=2)
    return jnp.concatenate([top, bot], axis=1).astype(BF16)


def _gate_blocks(dw):
    return jnp.stack([dw[:, 0:64, 0:64], dw[:, 64:128, 64:128]], axis=1).reshape(8, 64, 64)


BIG_NAMES = ["ffn1_w_gate", "ffn1_w_up", "ffn1_w_down", "w_in", "w_uq", "w_uk", "w_uv", "w_out", "ffn2_w_gate",
             "ffn2_w_up", "ffn2_w_down"]
BIG_GROUPS = [["ffn1_w_gate", "ffn1_w_up", "ffn1_w_down", "ffn2_w_gate", "ffn2_w_up", "ffn2_w_down"], ["w_in"],
              ["w_uq"], ["w_uk", "w_uv"], ["w_out"]]
TRANSPOSED = ("ffn1_w_gate", "ffn1_w_up", "ffn2_w_gate", "ffn2_w_up", "w_in", "w_uq")


def _to2d(nm, a):
    return a[0].T if nm in TRANSPOSED else a[0]


def _from2d(nm, a):
    return (a.T if nm in TRANSPOSED else a)[None]


WEIGHT_NAMES = ["meta_tokens", "ffn1_norm", "ffn1_w_gate", "ffn1_w_up", "ffn1_w_down", "mix_norm", "w_in",
                "q_latent_norm", "w_uq", "kv_latent_norm", "w_uk", "w_uv", "q_head_norm", "k_head_norm", "conv_w",
                "conv_b", "gate_a_w", "gate_a_b", "gate_x_w", "gate_x_b", "lru_lambda", "attn_out_norm",
                "lru_out_norm", "w_out", "ffn2_norm", "ffn2_w_gate", "ffn2_w_up", "ffn2_w_down", "final_norm"]


def _weight_from(nm, slots):
    if nm == "w_in":
        win = slots.reshape(IN_WIDTH, D_MODEL)
        lru = win[Z_KR + D_ROPE:].reshape(2, N_LRU_TILES, LRU_TILE, D_MODEL).transpose(1, 0, 2, 3)
        return jnp.concatenate([win[0:Z_KR + D_ROPE], jnp.zeros((128 - D_ROPE, D_MODEL), BF16),
                                lru.reshape(2 * LRU_W, D_MODEL)], axis=0)
    if nm == "w_uq":
        return jnp.pad(slots, ((0, 0), (0, D_QKP - D_QK), (0, 0))).reshape(HEADS * D_QKP, Q_RANK)
    if nm in ("w_uk", "w_uv"):
        return slots.transpose(1, 0, 2).reshape(KV_RANK, HEADS * D_NOPE)
    if nm == "w_out":
        return slots.reshape(D_MODEL, D_MODEL)
    return slots


def _small_weights(p, small):
    w = {nm: p[nm] for nm in SMALL_NAMES}
    w["q_head_norm"] = jnp.pad(p["q_head_norm"], ((0, 0), (0, D_QKP - D_QK)))
    w["k_head_norm"] = jnp.pad(p["k_head_norm"], ((0, 0), (0, D_QKP - D_QK)))
    w["conv_w"] = small[:, N_META:N_META + 2, :].reshape(N_SHARD, CONV_K, LRU_TILE)
    w["gate_a_w"] = _gate_dense(p["gate_a_w"])
    w["gate_x_w"] = _gate_dense(p["gate_x_w"])
    meta = small[:, 0:N_META, :].transpose(1, 0, 2).reshape(N_META, D_MODEL)
    return w, meta


def _full_weights(p, gathered, small):
    w, meta = _small_weights(p, small)
    w.update({nm: _weight_from(nm, gathered[nm]) for nm in BIG_NAMES})
    return w, meta


def _shard_grad(nm, g):
    if nm == "w_in":
        lru = g[Z_MLA:].reshape(N_LRU_TILES, 2, LRU_TILE, D_MODEL).transpose(1, 0, 2, 3).reshape(2 * LRU_W, D_MODEL)
        return jnp.concatenate([g[0:Z_KR + D_ROPE], lru], axis=0).reshape(N_SHARD, IN_WIDTH // N_SHARD, D_MODEL)
    if nm == "w_uq":
        return g.reshape(HEADS, D_QKP, Q_RANK)[:, 0:D_QK, :]
    if nm in ("w_uk", "w_uv"):
        return g.reshape(KV_RANK, HEADS, D_NOPE).transpose(1, 0, 2)
    if nm == "w_out":
        return g.reshape(N_SHARD, D_MODEL // N_SHARD, D_MODEL)
    return g


def _shard_grads(g):
    return {nm: _shard_grad(nm, g[nm]) for nm in BIG_NAMES}


GATHER_AT = {"ffn1_norm": ["ffn1_w_gate", "ffn1_w_up"],
             "ffn1_up": ["ffn1_w_down", "w_in", "w_uq", "w_uk", "w_uv", "w_out"],
             "attn_fwd": ["ffn2_w_down", "ffn2_w_gate"], "lru_fwd": ["ffn2_w_up"]}
PAIR_AT = [("ffn2_din", ["ffn2_w_gate", "ffn2_w_up", "ffn2_w_down"]),
           ("mix_din", ["w_out", "w_uq", "w_uk", "w_uv", "w_in"]),
           ("ffn1_dwg", ["ffn1_w_down"]), ("ffn1_dwu", ["ffn1_w_gate"]), ("ffn1_din_a", ["ffn1_w_up"])]
CHIPS_AT = [("attn_bwd", ["ffn2_w_down", "ffn2_w_gate"]), ("mla_prep_bwd", ["ffn2_w_up"]),
            ("ffn1_dact", ["w_out", "w_uq", "w_uk", "w_uv", "w_in"]),
            ("ffn1_dwu", ["ffn1_w_down"]), ("ffn1_din_a", ["ffn1_w_gate"]), ("ffn1_din_b", ["ffn1_w_up"])]
BF16_COPY = "@bf16"
SHARE_EARLY_GROUPS, SHARE_EARLY_AT = 3, "ffn1_dwd"
SMALL_EARLY_AT = "mix_din"


def _same_shape_groups(names):
    return [[nm for nm in grp if nm in names] for grp in BIG_GROUPS if any(nm in names for nm in grp)]


class _Sched:
    def __init__(self, place, w, slots):
        self.place, self.w, self.slots = place, w, slots
        self.g = None
        self.sharded, self.from_pair, self.chip_bf16, self.from_chips = {}, {}, {}, {}
        self.early = self.early_all = None
        self.shared = {}

    def host(self, stage):
        comms = []
        if stage in GATHER_AT:
            comms.append(self.gather(GATHER_AT[stage]))
        comms += [self.chips(names) for at, names in CHIPS_AT if at == stage]
        comms += [self.pair(names) for at, names in PAIR_AT if at == stage]
        if stage == SMALL_EARLY_AT:
            comms.append(self.small_early())
        if stage == SHARE_EARLY_AT:
            comms.append(self.share_early())
        return _join_comms(comms)

    def small_early(self):
        self.early = _early_pack(self.g)

        def deliver(outs):
            self.early_all = outs[0]
            return outs

        return _small_comm(self.early, deliver)

    def gather(self, names):
        def deliver(outs):
            self.w.update({nm: _weight_from(nm, o) for nm, o in zip(names, outs)})
            return outs

        return _gather_comm([self.slots[nm] for nm in names], deliver)

    def pair(self, names):
        self.sharded.update({nm: _shard_grad(nm, self.g[nm]) for nm in names})
        send = [_shard_grad(nm, self.g.get(nm + BF16_COPY, self.g[nm])) for nm in names]

        def deliver(outs):
            self.from_pair.update(zip(names, outs))
            for grp in _same_shape_groups(names):
                sums = _pair_sum_call("pair_sum_" + grp[0], self.place, [self.sharded[nm] for nm in grp],
                                      [self.from_pair[nm] for nm in grp])
                self.chip_bf16.update(zip(grp, sums))
            return outs

        return _reduce_pair_comm(send, deliver)

    def chips(self, names):
        def deliver(outs):
            self.from_chips.update(zip(names, outs))
            return outs

        return _reduce_chips_comm([self.chip_bf16[nm] for nm in names], deliver)

    def chip_sums(self, names, comm=None):
        out = {}
        for grp in _same_shape_groups(names):
            sums = _chip_sum_call("chip_sum_" + grp[0], self.place, [self.sharded[nm] for nm in grp],
                                  [self.from_pair[nm] for nm in grp], [self.from_chips[nm] for nm in grp], comm)
            comm = None
            out.update(zip(grp, sums))
        return out

    def share_early(self):
        names = [nm for at, grp in CHIPS_AT[:SHARE_EARLY_GROUPS] for nm in grp]
        mine = self.chip_sums(names)
        return _share_pair_comm([mine[nm] for nm in names], lambda o: self.shared.update(zip(names, o)))


def kernel(x, meta_tokens, ffn1_norm, ffn1_w_gate, ffn1_w_up, ffn1_w_down, mix_norm, w_in, q_latent_norm, w_uq, kv_latent_norm, w_uk, w_uv, q_head_norm, k_head_norm, conv_w, conv_b, gate_a_w, gate_a_b, gate_x_w, gate_x_b, lru_lambda, attn_out_norm, lru_out_norm, w_out, ffn2_norm, ffn2_w_gate, ffn2_w_up, ffn2_w_down, final_norm, loss_target, m_meta_tokens, m_ffn1_norm, m_ffn1_w_gate, m_ffn1_w_up, m_ffn1_w_down, m_mix_norm, m_w_in, m_q_latent_norm, m_w_uq, m_kv_latent_norm, m_w_uk, m_w_uv, m_q_head_norm, m_k_head_norm, m_conv_w, m_conv_b, m_gate_a_w, m_gate_a_b, m_gate_x_w, m_gate_x_b, m_lru_lambda, m_attn_out_norm, m_lru_out_norm, m_w_out, m_ffn2_norm, m_ffn2_w_gate, m_ffn2_w_up, m_ffn2_w_down, m_final_norm, v_meta_tokens, v_ffn1_norm, v_ffn1_w_gate, v_ffn1_w_up, v_ffn1_w_down, v_mix_norm, v_w_in, v_q_latent_norm, v_w_uq, v_kv_latent_norm, v_w_uk, v_w_uv, v_q_head_norm, v_k_head_norm, v_conv_w, v_conv_b, v_gate_a_w, v_gate_a_b, v_gate_x_w, v_gate_x_b, v_lru_lambda, v_attn_out_norm, v_lru_out_norm, v_w_out, v_ffn2_norm, v_ffn2_w_gate, v_ffn2_w_up, v_ffn2_w_down, v_final_norm):
    args = locals()
    p = {nm: args[nm] for nm in WEIGHT_NAMES}
    mom = {nm: args["m_" + nm] for nm in WEIGHT_NAMES}
    var = {nm: args["v_" + nm] for nm in WEIGHT_NAMES}
    nb, seq, d = x.shape
    lp = CHUNK + seq
    xi, yi, ci = lax.axis_index("x"), lax.axis_index("y"), lax.axis_index("c")
    chip = 2 * xi + yi

    place = jnp.stack([chip, ci]).astype(jnp.int32)
    p2 = {nm: _to2d(nm, p[nm]) for nm in BIG_NAMES}
    m2 = {nm: _to2d(nm, mom[nm]) for nm in BIG_NAMES}
    v2 = {nm: _to2d(nm, var[nm]) for nm in BIG_NAMES}

    slots = {}
    small_shard = jnp.concatenate(
        [meta_tokens, conv_w[0].reshape(2, 2 * LRU_TILE), jnp.zeros((14, 2 * LRU_TILE), F32)], axis=0)
    small_slots = lax.dynamic_update_slice(jnp.zeros((N_SHARD,) + small_shard.shape, F32), small_shard[None],
                                           (chip, 0, 0))
    first = []
    comm = _gather_comm([small_slots], first.extend)
    for grp in BIG_GROUPS:
        for nm, buf in zip(grp, _cast_call("cast_" + grp[0], place, [p2[nm] for nm in grp], comm)):
            slots[nm] = buf
        comm = None
    w, meta = _small_weights(p, first[0])
    sched = _Sched(place, w, slots)

    h0 = jnp.concatenate(
        [jnp.zeros((nb, PAD_ROWS, d), F32), jnp.broadcast_to(meta[None], (nb, N_META, d)), x], axis=1)
    loss_part, dh0, g = _local_step(h0.reshape(nb * lp, d), loss_target, w, nb, lp, sched)
    dh0 = dh0.reshape(nb, lp, d)
    grad_x = dh0[:, CHUNK:, :]

    late = jnp.concatenate([g["ffn1_norm"], g["mix_norm"], jnp.zeros((6, D_MODEL), F32),
                            jnp.sum(dh0[:, PAD_ROWS:CHUNK, :], axis=0)], axis=0)
    shared = sched.shared
    rest = [nm for at, names in CHIPS_AT[SHARE_EARLY_GROUPS:] if at is not None for nm in names]
    last = [nm for at, names in CHIPS_AT if at is None for nm in names]
    late_box = {}
    mine = sched.chip_sums(rest, _small_comm(late, lambda o: late_box.update(all=o[0])))
    share = _share_pair_comm([mine[nm] for nm in rest], lambda o: shared.update(zip(rest, o)))
    _comm_call("share_pair", _join_comms([share, sched.chips(last) if last else None]))
    if last:
        mine = sched.chip_sums(last)
        _comm_call("share_last", _share_pair_comm([mine[nm] for nm in last], lambda o: shared.update(zip(last, o))))
    late_all = late_box["all"]
    small_like = {nm: p[nm] for nm in SMALL_NAMES + ["gate_a_w", "gate_x_w"]}

    def state(t):
        return ([t[nm] for nm in SMALL_NAMES], t["gate_a_w"].reshape(32, D_MODEL), t["gate_x_w"].reshape(32, D_MODEL))

    me = (4 * xi + 2 * yi + ci).astype(jnp.int32).reshape(1)
    (gsum, dsm, msm, vsm), rows = _small_update_call(me, sched.early_all, sched.early, late_all, late,
                                                     [state(p), state(mom), state(var)])
    grads = _unpack_small(gsum, rows[0], small_like)
    delta = _unpack_small(dsm, rows[1], small_like)
    new_m = _unpack_small(msm, rows[2], small_like)
    new_v = _unpack_small(vsm, rows[3], small_like)
    loss = gsum[ROW_LOSS, 0]
    gmeta = gsum[ROW_META:ROW_META + N_META].reshape(N_META, N_SHARD, D_MODEL // N_SHARD)
    grads["meta_tokens"] = lax.dynamic_index_in_dim(gmeta, chip, axis=1, keepdims=False)
    gconv = gsum[ROW_CONV_W:ROW_CONV_W + 2].reshape(CONV_K, N_SHARD, LRU_TILE)
    grads["conv_w"] = lax.dynamic_index_in_dim(gconv, chip, axis=1, keepdims=False)[None]
    for nm in ("meta_tokens", "conv_w"):
        delta[nm], new_m[nm], new_v[nm] = _adamw_call("adamw_" + nm, p[nm], grads[nm], mom[nm], var[nm])

    for names in BIG_GROUPS:
        res = _adamw_group_call("adamw_" + names[0], [p2[nm] for nm in names], [shared[nm] for nm in names],
                                [m2[nm] for nm in names], [v2[nm] for nm in names])
        for nm, (gg, dd, mm, vv) in zip(names, res):
            grads[nm], delta[nm], new_m[nm], new_v[nm] = (_from2d(nm, t) for t in (gg, dd, mm, vv))

    return (loss, grad_x, *[grads[nm] for nm in WEIGHT_NAMES], *[delta[nm] for nm in WEIGHT_NAMES],
            *[new_m[nm] for nm in WEIGHT_NAMES], *[new_v[nm] for nm in WEIGHT_NAMES])
```

```python
import functools
import math

import jax
import jax.numpy as jnp
import numpy as np
from jax import lax
from jax.experimental import pallas as pl
from jax.experimental.pallas import tpu as pltpu

F32 = jnp.float32
BF16 = jnp.bfloat16
MESH = pl.DeviceIdType.MESH

D_MODEL = 1024
N_META = 16
CHUNK = 64
PAD_ROWS = CHUNK - N_META
HEADS = 4
D_NOPE = 128
D_ROPE = 64
D_QK = D_NOPE + D_ROPE
D_QKP = 256
D_V = 128
KV_RANK = 256
Q_RANK = 384
MLA_W = HEADS * D_V
LRU_W = 512
LRU_TILE = 128
N_LRU_TILES = LRU_W // LRU_TILE
CONV_K = 4
C_RGLRU = 8.0
ROPE_THETA = 10000.0
D_FF = 2816
N_SHARD = 4
EPS = 1e-6
NEG_INF = -1e30
Z_KR = Q_RANK + KV_RANK
Z_MLA = Z_KR + 128
Z_U = Z_MLA
Z_G = Z_U + LRU_W
Z_W = Z_G + LRU_W
IN_WIDTH = Q_RANK + KV_RANK + D_ROPE + 2 * LRU_W

ADAM_LR = 0.001
ADAM_B1 = 0.9
ADAM_B2 = 0.999
ADAM_EPS = 1e-08
ADAM_WD = 0.01
ADAM_STEP = 10

VMEM_LIMIT_BYTES = 56 * 1024 * 1024
MATMUL_ROWS = 1408
ELEMENTWISE_ROWS = 512
SMALL_ROWS = 104
SMALL_ADAM_ROWS = 80


def _params(sem):
    return pltpu.CompilerParams(dimension_semantics=sem, vmem_limit_bytes=VMEM_LIMIT_BYTES)


def _resident(shape):
    return pl.BlockSpec(tuple(shape), lambda i: (0,) * len(shape), pipeline_mode=pl.Buffered(1))


def _row_tile(rows, target):
    best = 16
    for t in range(16, min(rows, target) + 1, 16):
        if rows % t == 0:
            best = t
    return best


def _col_tile(cols, target):
    best = cols
    for t in range(128, min(cols, target) + 1, 128):
        if cols % t == 0:
            best = t
    return best


def _dot(a, b):
    return jnp.dot(a, b, preferred_element_type=F32)


def _dot_nt(a, b):
    return lax.dot_general(a, b, (((1,), (1,)), ((), ())), preferred_element_type=F32)


def _dot_tn(a, b):
    return lax.dot_general(a, b, (((0,), (0,)), ((), ())), preferred_element_type=F32)


def _rms(x, n):
    return lax.rsqrt(jnp.sum(x * x, axis=-1, keepdims=True) * (1.0 / n) + EPS)


def _rms_bwd(dn, nrm, r, n):
    return r * (dn - nrm * (jnp.sum(dn * nrm, axis=-1, keepdims=True) * (1.0 / n)))


def _gelu(x):
    k = math.sqrt(2.0 / math.pi)
    t = jnp.tanh(k * (x + 0.044715 * x * x * x))
    return 0.5 * x * (1.0 + t), t


def _gelu_grad(x, t):
    k = math.sqrt(2.0 / math.pi)
    return 0.5 * (1.0 + t) + 0.5 * x * (1.0 - t * t) * k * (1.0 + 3.0 * 0.044715 * x * x)


def _sigmoid(x):
    return 0.5 + 0.5 * jnp.tanh(0.5 * x)


def _softplus_neg(lam):
    e = jnp.exp(-jnp.abs(lam))
    log1p = jnp.where(e < 0.01, e * (1.0 - e * (0.5 - e * (1.0 / 3 - e * 0.25))), jnp.log(1.0 + e))
    return jnp.maximum(-lam, 0.0) + log1p


def _rope(t, c, s1, s2):
    return t * c + pltpu.roll(t, 96, 1) * s1 + pltpu.roll(t, 32, 1) * s2


def _rope_t(d, c, s1, s2):
    return d * c + pltpu.roll(d * s1, 32, 1) + pltpu.roll(d * s2, 96, 1)


def _rope_tables(lp):
    pos = (np.arange(lp, dtype=np.int32) - PAD_ROWS).astype(np.float32)
    inv_freq = (ROPE_THETA ** (-np.arange(0, D_ROPE // 2, dtype=np.float32) / (D_ROPE // 2))).astype(np.float32)
    ang = (pos[:, None] * inv_freq[None, :]).astype(np.float32).astype(np.float64)
    cos, sin = np.cos(ang).astype(np.float32), np.sin(ang).astype(np.float32)
    z = np.zeros_like(cos)
    return (jnp.asarray(np.concatenate([cos, cos, z, z], 1)), jnp.asarray(np.concatenate([-sin, z, z, z], 1)),
            jnp.asarray(np.concatenate([z, sin, z, z], 1)))


def _rmsnorm_call(name, h, g, tm, comm=None):
    rows, d = h.shape

    def body(h_ref, g_ref, o_ref):
        x = h_ref[...]
        o_ref[...] = (x * _rms(x, d) * g_ref[...]).astype(BF16)

    return _hosted_call(
        body, name=name, grid=(rows // tm,),
        in_specs=[pl.BlockSpec((tm, d), lambda i: (i, 0)), pl.BlockSpec((1, d), lambda i: (0, 0))],
        out_specs=[pl.BlockSpec((tm, d), lambda i: (i, 0))],
        out_shape=[jax.ShapeDtypeStruct((rows, d), BF16)],
        dims=("parallel",), args=(h, g), comm=comm)[0]


def _ffn_up_call(name, u, wg, wu, tm, comm=None):
    rows, d = u.shape
    ns, fs, _ = wg.shape

    def body(u_ref, wg_ref, wu_ref, g_ref, p_ref, a_ref):
        uu = u_ref[...]
        g = _dot_nt(uu, wg_ref[0])
        p = _dot_nt(uu, wu_ref[0])
        g_ref[0] = g.astype(BF16)
        p_ref[0] = p.astype(BF16)
        a_ref[0] = (g * jax.nn.sigmoid(g) * p).astype(BF16)

    wspec = pl.BlockSpec((1, fs, d), lambda s, i: (s, 0, 0))
    ospec = pl.BlockSpec((1, tm, fs), lambda s, i: (s, i, 0))
    oshape = jax.ShapeDtypeStruct((ns, rows, fs), BF16)
    return _hosted_call(
        body, name=name, grid=(ns, rows // tm),
        in_specs=[pl.BlockSpec((tm, d), lambda s, i: (i, 0)), wspec, wspec],
        out_specs=[ospec, ospec, ospec], out_shape=[oshape, oshape, oshape],
        dims=("parallel", "parallel"), args=(u, wg, wu), comm=comm)


def _loss_tail(x, g, t, row, d):
    r = _rms(x, d)
    n = x * r
    err = jnp.where(row >= CHUNK, n * g - t, 0.0)
    dout = err * (1.0 / d)
    dh = _rms_bwd(dout * g, n, r, d)
    dg = jnp.sum(dout * n, axis=0, keepdims=True)
    part = jnp.sum(jnp.sum(err * err, axis=1, keepdims=True), axis=0, keepdims=True) * (0.5 / d)
    return dh, dg, jnp.broadcast_to(part, (1, 128))


def _ffn_down_call(name, a, wd, h, tm, comm=None, next_gain=None):
    rows, d = h.shape
    ns, _, fs = a.shape
    more = next_gain is not None

    def body(a_ref, wd_ref, h_ref, *rest):
        acc = h_ref[...]
        for s in range(ns):
            acc = acc + 0.5 * _dot(a_ref[s], wd_ref[s])
        rest[-2 if more else -1][...] = acc
        if more:
            rest[-1][...] = (acc * _rms(acc, d) * rest[0][...]).astype(BF16)

    full = pl.BlockSpec((tm, d), lambda i: (i, 0))
    res = _hosted_call(
        body, name=name, grid=(rows // tm,),
        in_specs=[pl.BlockSpec((ns, tm, fs), lambda i: (0, i, 0)), _resident((ns, fs, d)), full]
        + ([pl.BlockSpec((1, d), lambda i: (0, 0))] if more else []),
        out_specs=[full] * (2 if more else 1),
        out_shape=[jax.ShapeDtypeStruct((rows, d), F32)] + ([jax.ShapeDtypeStruct((rows, d), BF16)] if more else []),
        dims=("parallel",), args=(a, wd, h) + ((next_gain,) if more else ()), comm=comm)
    return res if more else res[0]


def _ffn_down_loss_call(name, a, wd, h, g, target, lp, tm):
    rows, d = h.shape
    ns, _, fs = a.shape
    tpe = lp // tm

    def body(a_ref, wd_ref, h_ref, g_ref, t_hbm, dh_ref, dhb_ref, dg_ref, loss_ref, t_buf, t_sem):
        i = pl.program_id(0)
        j = i % tpe

        def fetch(step, start):
            slot = step % 2
            ex, jj = step // tpe, step % tpe

            @pl.when(jj == 0)
            def _():
                cp = pltpu.make_async_copy(t_hbm.at[ex, pl.ds(0, tm - CHUNK)],
                                           t_buf.at[slot, pl.ds(CHUNK, tm - CHUNK)], t_sem.at[slot])
                if start:
                    t_buf[slot, 0:CHUNK, :] = jnp.zeros((CHUNK, d), F32)
                    cp.start()
                else:
                    cp.wait()

            @pl.when(jj != 0)
            def _():
                first = pl.multiple_of(jnp.maximum(jj * tm - CHUNK, 0), 8)
                cp = pltpu.make_async_copy(t_hbm.at[ex, pl.ds(first, tm)], t_buf.at[slot], t_sem.at[slot])
                if start:
                    cp.start()
                else:
                    cp.wait()

        @pl.when(i == 0)
        def _():
            fetch(i, True)

        @pl.when(i + 1 < rows // tm)
        def _():
            fetch(i + 1, True)

        acc = h_ref[...]
        for s in range(ns):
            acc = acc + 0.5 * _dot(a_ref[s], wd_ref[s])
        fetch(i, False)
        row = j * tm + lax.broadcasted_iota(jnp.int32, (tm, 1), 0)
        dh, dg, loss = _loss_tail(acc, g_ref[...], t_buf[i % 2], row, d)
        dh_ref[...] = dh
        dhb_ref[...] = dh.astype(BF16)

        @pl.when(i == 0)
        def _():
            dg_ref[...] = dg
            loss_ref[...] = loss

        @pl.when(i != 0)
        def _():
            dg_ref[...] += dg
            loss_ref[...] += loss

    full = pl.BlockSpec((tm, d), lambda i: (i, 0))
    gspec = pl.BlockSpec((1, d), lambda i: (0, 0))
    return _hosted_call(
        body, name=name, grid=(rows // tm,),
        in_specs=[pl.BlockSpec((ns, tm, fs), lambda i: (0, i, 0)), _resident((ns, fs, d)), full, gspec,
                  pl.BlockSpec(memory_space=pl.ANY)],
        out_specs=[full, full, gspec, pl.BlockSpec((1, 128), lambda i: (0, 0))],
        out_shape=[jax.ShapeDtypeStruct((rows, d), F32), jax.ShapeDtypeStruct((rows, d), BF16),
                   jax.ShapeDtypeStruct((1, d), F32), jax.ShapeDtypeStruct((1, 128), F32)],
        scratch_shapes=[pltpu.VMEM((2, tm, d), F32), pltpu.SemaphoreType.DMA((2,))],
        args=(a, wd, h, g, target))


def _mm_call(name, a, bt, tm, out_dtype):
    rows, k = a.shape
    n = bt.shape[0]

    def body(a_ref, b_ref, o_ref):
        o_ref[...] = _dot_nt(a_ref[...], b_ref[...]).astype(out_dtype)

    return pl.pallas_call(
        body, name=name, grid=(rows // tm,),
        in_specs=[pl.BlockSpec((tm, k), lambda i: (i, 0)), pl.BlockSpec((n, k), lambda i: (0, 0))],
        out_specs=pl.BlockSpec((tm, n), lambda i: (i, 0)),
        out_shape=jax.ShapeDtypeStruct((rows, n), out_dtype),
        compiler_params=_params(("parallel",)))(a, bt)


def _mla_heads(z, gql, gkvl, wuq, wuk, wuv):
    cq = z[:, 0:Q_RANK]
    ckv = z[:, Q_RANK:Z_KR]
    kr = z[:, Z_KR:Z_MLA]
    rq = _rms(cq, Q_RANK)
    nq = cq * rq
    cqn = (nq * gql).astype(BF16)
    rkv = _rms(ckv, KV_RANK)
    nkv = ckv * rkv
    ckvn = (nkv * gkvl).astype(BF16)
    qraw = _dot_nt(cqn, wuq)
    knope = _dot(ckvn, wuk)
    v = _dot(ckvn, wuv)
    skr = jnp.sum(kr * kr, axis=-1, keepdims=True)
    heads = []
    for hd in range(HEADS):
        qh = qraw[:, hd * D_QKP:(hd + 1) * D_QKP]
        rqh = lax.rsqrt(jnp.sum(qh * qh, axis=-1, keepdims=True) * (1.0 / D_QK) + EPS)
        kn = knope[:, hd * D_NOPE:(hd + 1) * D_NOPE]
        rkh = lax.rsqrt((jnp.sum(kn * kn, axis=-1, keepdims=True) + skr) * (1.0 / D_QK) + EPS)
        heads.append((qh * rqh, rqh, kn * rkh, kr * rkh, rkh))
    return dict(rq=rq, nq=nq, cqn=cqn, rkv=rkv, nkv=nkv, ckvn=ckvn, v=v, heads=heads)


def _mla_prep_call(z, gql, gkvl, gqh, gkh, wuq, wuk, wuv, tabs, lp, tm):
    rows = z.shape[0]
    tpe = lp // tm

    def body(z_ref, gql_ref, gkvl_ref, gqh_ref, gkh_ref, wuq_ref, wuk_ref, wuv_ref, c_ref, s1_ref, s2_ref,
             q_ref, k_ref, v_ref):
        m = _mla_heads(z_ref[...], gql_ref[...], gkvl_ref[...], wuq_ref[...], wuk_ref[...], wuv_ref[...])
        c, s1, s2 = c_ref[...], s1_ref[...], s2_ref[...]
        gq, gk = gqh_ref[...], gkh_ref[...]
        row = (pl.program_id(0) % tpe) * tm + lax.broadcasted_iota(jnp.int32, (tm, 1), 0)
        spare = (lax.broadcasted_iota(jnp.int32, (1, D_QKP - D_NOPE), 1) == D_ROPE).astype(F32)
        kmask = jnp.where(row < PAD_ROWS, NEG_INF * math.sqrt(D_QK), 0.0) * spare
        for hd in range(HEADS):
            qn, _, knn, krn, _ = m["heads"][hd]
            qg = qn * gq
            q_ref[hd, :, 0:D_NOPE] = qg[:, 0:D_NOPE].astype(BF16)
            q_ref[hd, :, D_NOPE:D_QKP] = (_rope(qg[:, D_NOPE:D_QKP], c, s1, s2) + spare).astype(BF16)
            k_ref[hd, :, 0:D_NOPE] = (knn * gk[:, 0:D_NOPE]).astype(BF16)
            k_ref[hd, :, D_NOPE:D_QKP] = (_rope(krn * gk[:, D_NOPE:D_QKP], c, s1, s2) + kmask).astype(BF16)
            v_ref[hd] = m["v"][:, hd * D_V:(hd + 1) * D_V].astype(BF16)

    def const(shape):
        return pl.BlockSpec(shape, lambda i: tuple(0 for _ in shape))

    tab = pl.BlockSpec((tm, 128), lambda i: (i % tpe, 0))
    return pl.pallas_call(
        body, name="mla_prep", grid=(rows // tm,),
        in_specs=[pl.BlockSpec((tm, Z_MLA), lambda i: (i, 0)), const((1, Q_RANK)), const((1, KV_RANK)),
                  const((1, D_QKP)), const((1, D_QKP)), const((HEADS * D_QKP, Q_RANK)),
                  const((KV_RANK, HEADS * D_NOPE)), const((KV_RANK, HEADS * D_V)), tab, tab, tab],
        out_specs=[pl.BlockSpec((HEADS, tm, D_QKP), lambda i: (0, i, 0)),
                   pl.BlockSpec((HEADS, tm, D_QKP), lambda i: (0, i, 0)),
                   pl.BlockSpec((HEADS, tm, D_V), lambda i: (0, i, 0))],
        out_shape=[jax.ShapeDtypeStruct((HEADS, rows, D_QKP), BF16),
                   jax.ShapeDtypeStruct((HEADS, rows, D_QKP), BF16),
                   jax.ShapeDtypeStruct((HEADS, rows, D_V), BF16)],
        compiler_params=_params(("parallel",)))(z, gql, gkvl, gqh, gkh, wuq, wuk, wuv, *tabs)


Q_BLOCK_ROWS = 528
Q_BLOCK_ROWS_BWD = 192


def _q_block(lp):
    return _row_tile(lp, Q_BLOCK_ROWS)


def _key_end(ext, lp):
    return min(lp, -(-ext // CHUNK) * CHUNK)


def _diag_bias(qb, j0, nk):
    shift = CHUNK.bit_length() - 1
    r = jnp.right_shift(j0 + lax.broadcasted_iota(jnp.int32, (qb, nk), 0), shift)
    c = jnp.right_shift(j0 + lax.broadcasted_iota(jnp.int32, (qb, nk), 1), shift)
    return jnp.where(c <= r, 0.0, NEG_INF)


def _attn_fwd_call(q, k, v, nb, lp, comm=None):
    rows = nb * lp
    qb = _q_block(lp)
    scale = 1.0 / math.sqrt(D_QK)

    def body(q_ref, k_ref, v_ref, o_ref, lse_ref):
        for j in range(lp // qb):
            j0, ext = j * qb, (j + 1) * qb
            kend = _key_end(ext, lp)
            qj = q_ref[0, j0:ext, :]
            sd = _dot_nt(qj, k_ref[0, j0:kend, :]) * scale + _diag_bias(qb, j0, kend - j0)
            mx = jnp.max(sd, axis=-1, keepdims=True)
            if j > 0:
                so = _dot_nt(qj, k_ref[0, 0:j0, :]) * scale
                mx = jnp.maximum(mx, jnp.max(so, axis=-1, keepdims=True))
            pd = jnp.exp(sd - mx)
            l = jnp.sum(pd, axis=-1, keepdims=True)
            o = _dot(pd.astype(BF16), v_ref[0, j0:kend, :])
            if j > 0:
                po = jnp.exp(so - mx)
                l = l + jnp.sum(po, axis=-1, keepdims=True)
                o = o + _dot(po.astype(BF16), v_ref[0, 0:j0, :])
            o_ref[j0:ext, :] = o / l
            lse_ref[0, j0:ext, :] = mx + jnp.log(l)

    return _hosted_call(
        body, name="attn_fwd", grid=(nb, HEADS),
        in_specs=[pl.BlockSpec((1, lp, D_QKP), lambda b, h: (h, b, 0)),
                  pl.BlockSpec((1, lp, D_QKP), lambda b, h: (h, b, 0)),
                  pl.BlockSpec((1, lp, D_V), lambda b, h: (h, b, 0))],
        out_specs=[pl.BlockSpec((lp, D_V), lambda b, h: (b, h)),
                   pl.BlockSpec((1, lp, 1), lambda b, h: (h, b, 0))],
        out_shape=[jax.ShapeDtypeStruct((rows, MLA_W), F32),
                   jax.ShapeDtypeStruct((HEADS, rows, 1), F32)],
        dims=("parallel", "parallel"), args=(q, k, v), comm=comm)


def _attn_bwd_call(q, k, v, o, lse, do, nb, lp, comm=None):
    rows = nb * lp
    qb = _row_tile(lp, Q_BLOCK_ROWS_BWD)
    scale = 1.0 / math.sqrt(D_QK)

    def body(q_ref, k_ref, v_ref, o_ref, lse_ref, do_ref, dq_ref, dk_ref, dv_ref, dk_acc, dv_acc):
        dk_acc[...] = jnp.zeros_like(dk_acc)
        dv_acc[...] = jnp.zeros_like(dv_acc)
        shift = CHUNK.bit_length() - 1
        for j in range(lp // qb):
            j0, ext = j * qb, (j + 1) * qb
            kend = _key_end(ext, lp)
            qj = q_ref[0, j0:ext, :]
            doj = do_ref[j0:ext, :]
            delta = jnp.sum(doj * o_ref[j0:ext, :], axis=-1, keepdims=True)
            dob = doj.astype(BF16)
            kk = k_ref[0, 0:kend, :]
            qchunk = jnp.right_shift(j0 + lax.broadcasted_iota(jnp.int32, (qb, kend), 0), shift)
            kchunk = jnp.right_shift(lax.broadcasted_iota(jnp.int32, (qb, kend), 1), shift)
            s = _dot_nt(qj, kk) * scale - lse_ref[0, j0:ext, :]
            p = jnp.where(kchunk <= qchunk, jnp.exp(s), 0.0)
            dv_acc[0:kend, :] += _dot_tn(p.astype(BF16), dob)
            dp = _dot_nt(dob, v_ref[0, 0:kend, :])
            ds = (p * (dp - delta) * scale).astype(BF16)
            dq_ref[0, j0:ext, :] = _dot(ds, kk).astype(BF16)
            dk_acc[0:kend, :] += _dot_tn(ds, qj)
        dk_ref[0] = dk_acc[...].astype(BF16)
        dv_ref[0] = dv_acc[...].astype(BF16)

    qspec = pl.BlockSpec((1, lp, D_QKP), lambda b, h: (h, b, 0))
    vspec = pl.BlockSpec((1, lp, D_V), lambda b, h: (h, b, 0))
    ospec = pl.BlockSpec((lp, D_V), lambda b, h: (b, h))
    return _hosted_call(
        body, name="attn_bwd", grid=(nb, HEADS),
        in_specs=[qspec, qspec, vspec, ospec, pl.BlockSpec((1, lp, 1), lambda b, h: (h, b, 0)), ospec],
        out_specs=[qspec, qspec, vspec],
        out_shape=[jax.ShapeDtypeStruct((HEADS, rows, D_QKP), BF16),
                   jax.ShapeDtypeStruct((HEADS, rows, D_QKP), BF16),
                   jax.ShapeDtypeStruct((HEADS, rows, D_V), BF16)],
        scratch_shapes=[pltpu.VMEM((lp, D_QKP), F32), pltpu.VMEM((lp, D_V), F32)],
        dims=("parallel", "parallel"), args=(q, k, v, o, lse, do), comm=comm)


def _lru_gates(u, cw, cb, wa, ba, wx, bx, lam, lp):
    xc = (cw[3:4, :] * u + cw[2:3, :] * pltpu.roll(u, 1, 0) + cw[1:2, :] * pltpu.roll(u, 2, 0)
          + cw[0:1, :] * pltpu.roll(u, 3, 0) + cb)
    xcb = xc.astype(BF16)
    r = _sigmoid(_dot(xcb, wa) + ba)
    i = _sigmoid(_dot(xcb, wx) + bx)
    sp = _softplus_neg(lam)
    la = -C_RGLRU * r * sp
    a = jnp.exp(la)
    x2 = 2.0 * la
    e2 = a * a
    m2 = jnp.maximum(jnp.where(x2 > -0.01, -x2 * (1.0 + 0.5 * x2), 1.0 - e2), 1e-30)
    rs = lax.rsqrt(m2)
    row = lax.broadcasted_iota(jnp.int32, (lp, LRU_TILE), 0)
    first = row == PAD_ROWS
    valid = row >= PAD_ROWS
    mult_eff = jnp.where(first, 1.0, m2 * rs)
    return dict(xc=xc, xcb=xcb, r=r, i=i, sp=sp, a=a, e2=e2, rs=rs, mult_eff=mult_eff, first=first, valid=valid)


def _scan_rows(a, b, a_s, b_s, out_ref, lp, reverse):
    sub = lax.broadcasted_iota(jnp.int32, (lp, LRU_TILE), 0) & 7
    for dist in (1, 2, 4):
        shift = lp - dist if reverse else dist
        keep = (sub + dist <= 7) if reverse else (sub >= dist)
        a_sh = pltpu.roll(a, shift, 0)
        b_sh = pltpu.roll(b, shift, 0)
        b = jnp.where(keep, a * b_sh + b, b)
        a = jnp.where(keep, a * a_sh, a)
    a_s[...] = a
    b_s[...] = b
    n_groups = lp // 8
    edge = 0 if reverse else 7

    def group(gi, carry):
        r0 = pl.multiple_of(((n_groups - 1 - gi) if reverse else gi) * 8, 8)
        a8 = a_s[pl.ds(r0, 8), :]
        b8 = b_s[pl.ds(r0, 8), :]
        out_ref[pl.ds(r0, 8), :] = a8 * carry + b8
        return a8[edge:edge + 1, :] * carry + b8[edge:edge + 1, :]

    lax.fori_loop(0, n_groups, group, jnp.zeros((1, LRU_TILE), F32), unroll=4)


def _lru_specs(lp):
    seq = lambda col0, stride=1: pl.BlockSpec((lp, LRU_TILE), lambda t, b: (b, col0 + stride * t))
    cw = pl.BlockSpec((1, CONV_K, LRU_TILE), lambda t, b: (t, 0, 0))
    vec = pl.BlockSpec((1, LRU_TILE), lambda t, b: (0, t))
    mat = pl.BlockSpec((1, LRU_TILE, LRU_TILE), lambda t, b: (t, 0, 0))
    return seq, cw, vec, mat


def _lru_fwd_call(z, cw, cb, wa, ba, wx, bx, lam, nb, lp, comm=None):
    rows = nb * lp
    seq, cwspec, vec, mat = _lru_specs(lp)

    def body(u_ref, g_ref, cw_ref, cb_ref, wa_ref, ba_ref, wx_ref, bx_ref, lam_ref, y_ref, hs_ref, a_s, b_s):
        m = _lru_gates(u_ref[...], cw_ref[0], cb_ref[...], wa_ref[0], ba_ref[...], wx_ref[0], bx_ref[...],
                       lam_ref[...], lp)
        a = jnp.where(m["valid"], m["a"], 0.0)
        b = jnp.where(m["valid"], m["mult_eff"] * (m["i"] * m["xc"]), 0.0)
        _scan_rows(a, b, a_s, b_s, hs_ref, lp, reverse=False)
        gl, _ = _gelu(g_ref[...])
        y_ref[...] = hs_ref[...] * gl

    oshape = jax.ShapeDtypeStruct((rows, LRU_W), F32)
    return _hosted_call(
        body, name="lru_fwd", grid=(N_LRU_TILES, nb),
        in_specs=[seq(Z_U // LRU_TILE, 2), seq(Z_U // LRU_TILE + 1, 2), cwspec, vec, mat, vec, mat, vec, vec],
        out_specs=[seq(0), seq(0)], out_shape=[oshape, oshape],
        scratch_shapes=[pltpu.VMEM((lp, LRU_TILE), F32), pltpu.VMEM((lp, LRU_TILE), F32)],
        dims=("parallel", "parallel"), args=(z, z, cw, cb, wa, ba, wx, bx, lam), comm=comm)


def _lru_bwd_call(z, dz, hs, dy, cw, cb, wa, ba, wx, bx, lam, nb, lp, comm=None):
    rows = nb * lp
    seq, cwspec, vec, mat = _lru_specs(lp)

    def body(u_ref, g_ref, hs_ref, dy_ref, cw_ref, cb_ref, wa_ref, ba_ref, wx_ref, bx_ref, lam_ref, dz_in,
             dz_ref, dcw_ref, dcb_ref, dwa_ref, dba_ref, dwx_ref, dbx_ref, dlam_ref, a_s, b_s, d_s):
        du_ref = dz_ref.at[:, 0:LRU_TILE]
        dg_ref = dz_ref.at[:, LRU_TILE:2 * LRU_TILE]
        b_idx = pl.program_id(1)
        u = u_ref[...]
        cw = cw_ref[0]
        wa, wx = wa_ref[0], wx_ref[0]
        lam = lam_ref[...]
        m = _lru_gates(u, cw, cb_ref[...], wa, ba_ref[...], wx, bx_ref[...], lam, lp)
        gate = g_ref[...]
        gl, th = _gelu(gate)
        dy = dy_ref[...]
        hs = hs_ref[...]
        dg_ref[...] = (dy * hs * _gelu_grad(gate, th)).astype(BF16)
        a_eff = jnp.where(m["valid"], m["a"], 0.0)
        _scan_rows(pltpu.roll(a_eff, lp - 1, 0), dy * gl, a_s, b_s, d_s, lp, reverse=True)
        ds = d_s[...]
        xc, r, i = m["xc"], m["r"], m["i"]
        row = lax.broadcasted_iota(jnp.int32, (lp, LRU_TILE), 0)
        da = ds * jnp.where(row >= 1, pltpu.roll(hs, 1, 0), 0.0)
        db = jnp.where(m["valid"], ds, 0.0)
        di = db * m["mult_eff"] * xc
        dxc = db * m["mult_eff"] * i
        live = m["valid"] & jnp.logical_not(m["first"])
        dm = jnp.where(live, db * i * xc, 0.0)
        dla = da * m["a"] - dm * (m["e2"] * m["rs"])
        dr = dla * (-C_RGLRU * m["sp"])
        dsp = jnp.sum(dla * (-C_RGLRU * r), axis=0, keepdims=True)
        dpr = (dr * r * (1.0 - r))
        dpi = (di * i * (1.0 - i))
        dprb, dpib = dpr.astype(BF16), dpi.astype(BF16)
        dxc = dxc + _dot_nt(dprb, wa) + _dot_nt(dpib, wx)
        du = (cw[3:4, :] * dxc + cw[2:3, :] * pltpu.roll(dxc, lp - 1, 0) + cw[1:2, :] * pltpu.roll(dxc, lp - 2, 0)
              + cw[0:1, :] * pltpu.roll(dxc, lp - 3, 0))
        du_ref[...] = jnp.where(m["valid"], du, 0.0).astype(BF16)
        tap = lax.broadcasted_iota(jnp.int32, (CONV_K, LRU_TILE), 0)
        dcw = jnp.zeros((CONV_K, LRU_TILE), F32)
        for kk in range(CONV_K):
            shifted = u if kk == CONV_K - 1 else pltpu.roll(u, CONV_K - 1 - kk, 0)
            dcw = jnp.where(tap == kk, jnp.sum(dxc * shifted, axis=0, keepdims=True), dcw)
        parts = [(dcw_ref, dcw[None]), (dcb_ref, jnp.sum(dxc, axis=0, keepdims=True)[None]),
                 (dwa_ref, _dot_tn(m["xcb"], dprb)[None]), (dba_ref, jnp.sum(dpr, axis=0, keepdims=True)[None]),
                 (dwx_ref, _dot_tn(m["xcb"], dpib)[None]), (dbx_ref, jnp.sum(dpi, axis=0, keepdims=True)[None]),
                 (dlam_ref, (dsp * (-jax.nn.sigmoid(-lam)))[None])]

        @pl.when(b_idx == 0)
        def _():
            for ref, val in parts:
                ref[...] = val

        @pl.when(b_idx != 0)
        def _():
            for ref, val in parts:
                ref[...] += val

    vec3 = pl.BlockSpec((1, 1, LRU_TILE), lambda t, b: (t, 0, 0))
    vshape = jax.ShapeDtypeStruct((N_LRU_TILES, 1, LRU_TILE), F32)
    mshape = jax.ShapeDtypeStruct((N_LRU_TILES, LRU_TILE, LRU_TILE), F32)
    pair = pl.BlockSpec((lp, 2 * LRU_TILE), lambda t, b: (b, Z_U // (2 * LRU_TILE) + t))
    return _hosted_call(
        body, name="lru_bwd", grid=(N_LRU_TILES, nb),
        in_specs=[seq(Z_U // LRU_TILE, 2), seq(Z_U // LRU_TILE + 1, 2), seq(0), seq(0), cwspec, vec, mat, vec, mat,
                  vec, vec, pl.BlockSpec(memory_space=pl.ANY)],
        out_specs=[pair, cwspec, vec3, mat, vec3, mat, vec3, vec3],
        out_shape=[jax.ShapeDtypeStruct(dz.shape, dz.dtype), jax.ShapeDtypeStruct((N_LRU_TILES, CONV_K, LRU_TILE), F32),
                   vshape, mshape, vshape, mshape, vshape, vshape],
        scratch_shapes=[pltpu.VMEM((lp, LRU_TILE), F32)] * 3,
        dims=("parallel", "arbitrary"), args=(z, z, hs, dy, cw, cb, wa, ba, wx, bx, lam, dz), comm=comm,
        aliases={11: 0})


def _mix_out_call(ya, yl, ga, gl, wout, h, next_gain, tm):
    rows, d = h.shape

    def body(ya_ref, yl_ref, ga_ref, gl_ref, w_ref, h_ref, ng_ref, y_ref, o_ref, u_ref):
        a = ya_ref[...]
        l = yl_ref[...]
        an = (a * _rms(a, MLA_W) * ga_ref[...]).astype(BF16)
        ln = (l * _rms(l, LRU_W) * gl_ref[...]).astype(BF16)
        y_ref[:, 0:MLA_W] = an
        y_ref[:, MLA_W:MLA_W + LRU_W] = ln
        out = h_ref[...] + _dot(an, w_ref[0:MLA_W, :]) + _dot(ln, w_ref[MLA_W:MLA_W + LRU_W, :])
        o_ref[...] = out
        u_ref[...] = (out * _rms(out, d) * ng_ref[...]).astype(BF16)

    half = pl.BlockSpec((tm, MLA_W), lambda i: (i, 0))
    g = pl.BlockSpec((1, MLA_W), lambda i: (0, 0))
    full = pl.BlockSpec((tm, d), lambda i: (i, 0))
    return pl.pallas_call(
        body, name="mix_out", grid=(rows // tm,),
        in_specs=[half, half, g, g, pl.BlockSpec((MLA_W + LRU_W, d), lambda i: (0, 0)), full,
                  pl.BlockSpec((1, d), lambda i: (0, 0))],
        out_specs=[full, full, full],
        out_shape=[jax.ShapeDtypeStruct((rows, MLA_W + LRU_W), BF16), jax.ShapeDtypeStruct((rows, d), F32),
                   jax.ShapeDtypeStruct((rows, d), BF16)],
        compiler_params=_params(("parallel",)))(ya, yl, ga, gl, wout, h, next_gain)


def _ffn_dact_call(name, dhb, wd, gate, up, tm, comm=None):
    rows, d = dhb.shape
    ns, fs, _ = wd.shape

    nsub = 2 if tm % 32 == 0 else 1
    sub = tm // nsub

    def body(dh_ref, wd_ref, g_ref, p_ref, dg_ref, dp_ref):
        wd = wd_ref[0]
        for r in range(nsub):
            rs = slice(r * sub, (r + 1) * sub)
            da = (0.5 * _dot_nt(dh_ref[rs, :], wd)).astype(BF16)
            g = g_ref[0, rs, :]
            p = p_ref[0, rs, :]
            sg = jax.nn.sigmoid(g)
            dg_ref[0, rs, :] = (da * p) * (sg * (1.0 + g * (1.0 - sg)))
            dp_ref[0, rs, :] = da * (g * sg)

    aspec = pl.BlockSpec((1, tm, fs), lambda s, i: (s, i, 0))
    oshape = jax.ShapeDtypeStruct((ns, rows, fs), BF16)
    return _hosted_call(
        body, name=name, grid=(ns, rows // tm),
        in_specs=[pl.BlockSpec((tm, d), lambda s, i: (i, 0)), pl.BlockSpec((1, fs, d), lambda s, i: (s, 0, 0)),
                  aspec, aspec],
        out_specs=[aspec, aspec], out_shape=[oshape, oshape],
        dims=("parallel", "parallel"), args=(dhb, wd, gate, up), comm=comm)


def _norm_in_bwd_call(name, pieces, h, g, dres, tm, comm=None, part=(0, 1), prev=None, mix=None):
    rows, d = h.shape
    npc = len(pieces)
    steps = rows // tm // part[1]
    off = part[0] * steps
    n_prev = 0 if prev is None else 2
    n_mix = 0 if mix is None else 5

    def body(*refs):
        d_refs = refs[0:2 * npc:2]
        w_refs = refs[1:2 * npc:2]
        h_ref, g_ref, dres_ref = refs[2 * npc:2 * npc + 3]
        mix_in_refs = refs[2 * npc + 3:2 * npc + 3 + n_mix]
        dh_ref, dhb_ref, dg_ref = refs[2 * npc + 3 + n_mix + n_prev:2 * npc + 6 + n_mix + n_prev]
        mix_out_refs = refs[2 * npc + 6 + n_mix + n_prev:]
        du = jnp.zeros((tm, d), F32)
        for d_ref, w_ref in zip(d_refs, w_refs):
            if len(d_ref.shape) == 3:
                for s in range(d_ref.shape[0]):
                    du = du + _dot(d_ref[s], w_ref[s])
            else:
                du = du + _dot(d_ref[...], w_ref[...])
        x = h_ref[...]
        r = _rms(x, d)
        n = x * r
        dh = dres_ref[...] + _rms_bwd(du * g_ref[...], n, r, d)
        dhb = dh.astype(BF16)
        dh_ref[...] = dh
        dhb_ref[...] = dhb
        sums = [(dg_ref, jnp.sum(du * n, axis=0, keepdims=True))]
        if mix is not None:
            wo_ref, ya_ref, yl_ref, ga_ref, gl_ref = mix_in_refs
            dya_ref, dyl_ref, dga_ref, dgl_ref = mix_out_refs
            dy = _dot_nt(dhb, wo_ref[...])
            for val, gain_ref, lo, out_ref, acc_ref in ((ya_ref[...], ga_ref, 0, dya_ref, dga_ref),
                                                        (yl_ref[...], gl_ref, MLA_W, dyl_ref, dgl_ref)):
                rb = _rms(val, MLA_W)
                nb_ = val * rb
                dyn = dy[:, lo:lo + MLA_W]
                out_ref[...] = _rms_bwd(dyn * gain_ref[...], nb_, rb, MLA_W)
                sums.append((acc_ref, jnp.sum(dyn * nb_, axis=0, keepdims=True)))

        @pl.when(pl.program_id(0) == 0)
        def _():
            for ref, val in sums:
                ref[...] = val

        @pl.when(pl.program_id(0) != 0)
        def _():
            for ref, val in sums:
                ref[...] += val

    in_specs, args = [], []
    for dd, w in pieces:
        if dd.ndim == 3:
            in_specs.append(pl.BlockSpec((dd.shape[0], tm, dd.shape[2]), lambda i: (0, i + off, 0)))
            in_specs.append(_resident(w.shape))
        else:
            in_specs.append(pl.BlockSpec((tm, dd.shape[1]), lambda i: (i + off, 0)))
            in_specs.append(_resident(w.shape))
        args += [dd, w]
    full = pl.BlockSpec((tm, d), lambda i: (i + off, 0))
    gspec = pl.BlockSpec((1, d), lambda i: (0, 0))
    half = pl.BlockSpec((tm, MLA_W), lambda i: (i + off, 0))
    hgain = pl.BlockSpec((1, MLA_W), lambda i: (0, 0))
    mix_in_specs = [] if mix is None else [_resident(mix[0].shape), half, half, hgain, hgain]
    mix_out_specs = [] if mix is None else [half, half, hgain, hgain]
    mix_out_shapes = [] if mix is None else [
        jax.ShapeDtypeStruct((rows, MLA_W), F32), jax.ShapeDtypeStruct((rows, LRU_W), F32),
        jax.ShapeDtypeStruct((1, MLA_W), F32), jax.ShapeDtypeStruct((1, LRU_W), F32)]
    n_in = len(in_specs) + 3 + n_mix
    return _hosted_call(
        body, name=name, grid=(steps,),
        in_specs=in_specs + [full, gspec, full] + mix_in_specs + _any_specs(n_prev),
        out_specs=[full, full, gspec] + mix_out_specs,
        out_shape=[jax.ShapeDtypeStruct((rows, d), F32), jax.ShapeDtypeStruct((rows, d), BF16),
                   jax.ShapeDtypeStruct((1, d), F32)] + mix_out_shapes,
        args=(*args, h, g, dres, *(mix or ()), *(prev or ())), comm=comm,
        aliases={n_in: 0, n_in + 1: 1} if prev is not None else {})


def _wgrad_call(name, a, b, scale=1.0, comm=None, bf16_copy=False):
    a3, b3 = a.ndim == 3, b.ndim == 3
    ns = a.shape[0] if a3 else (b.shape[0] if b3 else 1)
    rows, m = a.shape[-2:]
    n = b.shape[-1]
    tmm = m if a3 else _col_tile(m, 256)

    def body(a_ref, b_ref, *o_refs):
        av = a_ref[0] if a3 else a_ref[...]
        bv = b_ref[0] if b3 else b_ref[...]
        res = _dot_tn(av, bv)
        if scale != 1.0:
            res = res * scale
        for o_ref in o_refs:
            if a3 or b3:
                o_ref[0] = res.astype(o_ref.dtype)
            else:
                o_ref[...] = res.astype(o_ref.dtype)

    aspec = (pl.BlockSpec((1, rows, tmm), lambda s, j: (s, 0, j)) if a3
             else pl.BlockSpec((rows, tmm), lambda s, j: (0, j)))
    bspec = (pl.BlockSpec((1, rows, n), lambda s, j: (s, 0, 0)) if b3
             else pl.BlockSpec((rows, n), lambda s, j: (0, 0)))
    if a3 or b3:
        ospec = pl.BlockSpec((1, tmm, n), lambda s, j: (s, j, 0))
        oshape = jax.ShapeDtypeStruct((ns, m, n), F32)
    else:
        ospec = pl.BlockSpec((tmm, n), lambda s, j: (j, 0))
        oshape = jax.ShapeDtypeStruct((m, n), F32)
    shapes = [oshape] + ([jax.ShapeDtypeStruct(oshape.shape, BF16)] if bf16_copy else [])
    res = _hosted_call(
        body, name=name, grid=(ns, m // tmm), in_specs=[aspec, bspec], out_specs=[ospec] * len(shapes),
        out_shape=shapes, dims=("parallel", "parallel"), args=(a, b), comm=comm)
    return res if bf16_copy else res[0]


def _mla_prep_bwd_call(z, dq, dk, dv, gql, gkvl, gqh, gkh, wuq, wuk, wuv, tabs, lp, tm, comm=None):
    rows = z.shape[0]
    tpe = lp // tm

    def body(z_ref, dq_ref, dk_ref, dv_ref, gql_ref, gkvl_ref, gqh_ref, gkh_ref, wuq_ref, wuk_ref, wuv_ref,
             c_ref, s1_ref, s2_ref, dz_ref, dgql_ref, dgkvl_ref, dgqh_ref, dgkh_ref, dwuq_ref, dwuk_ref, dwuv_ref,
             dqp_ref, dkn_ref, dvv_ref):
        gql, gkvl = gql_ref[...], gkvl_ref[...]
        gq, gk = gqh_ref[...], gkh_ref[...]
        wuq, wuk, wuv = wuq_ref[...], wuk_ref[...], wuv_ref[...]
        m = _mla_heads(z_ref[...], gql, gkvl, wuq, wuk, wuv)
        c, s1, s2 = c_ref[...], s1_ref[...], s2_ref[...]
        dgq = jnp.zeros((1, D_QKP), F32)
        dgk = jnp.zeros((1, D_QKP), F32)
        dkr = jnp.zeros((tm, D_QKP - D_NOPE), F32)
        for hd in range(HEADS):
            qn, rqh, knn, krn, rkh = m["heads"][hd]
            dqg = jnp.concatenate([dq_ref[hd, :, 0:D_NOPE].astype(F32),
                                   _rope_t(dq_ref[hd, :, D_NOPE:D_QKP].astype(F32), c, s1, s2)], axis=1)
            dgq = dgq + jnp.sum(dqg * qn, axis=0, keepdims=True)
            dqn = dqg * gq
            dqr = rqh * (dqn - qn * (jnp.sum(dqn * qn, axis=-1, keepdims=True) * (1.0 / D_QK)))
            dqp_ref[:, hd * D_QKP:(hd + 1) * D_QKP] = dqr.astype(BF16)
            kn_full = jnp.concatenate([knn, krn], axis=1)
            dkg = jnp.concatenate([dk_ref[hd, :, 0:D_NOPE].astype(F32),
                                   _rope_t(dk_ref[hd, :, D_NOPE:D_QKP].astype(F32), c, s1, s2)], axis=1)
            dgk = dgk + jnp.sum(dkg * kn_full, axis=0, keepdims=True)
            dkn = dkg * gk
            dkraw = rkh * (dkn - kn_full * (jnp.sum(dkn * kn_full, axis=-1, keepdims=True) * (1.0 / D_QK)))
            dkn_ref[:, hd * D_NOPE:(hd + 1) * D_NOPE] = dkraw[:, 0:D_NOPE].astype(BF16)
            dkr = dkr + dkraw[:, D_NOPE:D_QKP]
            dvv_ref[:, hd * D_V:(hd + 1) * D_V] = dv_ref[hd]
        dcqn = _dot(dqp_ref[...], wuq)
        dckvn = _dot_nt(dkn_ref[...], wuk) + _dot_nt(dvv_ref[...], wuv)
        dz_ref[:, 0:Q_RANK] = _rms_bwd(dcqn * gql, m["nq"], m["rq"], Q_RANK).astype(BF16)
        dz_ref[:, Q_RANK:Z_KR] = _rms_bwd(dckvn * gkvl, m["nkv"], m["rkv"], KV_RANK).astype(BF16)
        dz_ref[:, Z_KR:Z_MLA] = dkr.astype(BF16)
        parts = [(dgql_ref, jnp.sum(dcqn * m["nq"], axis=0, keepdims=True)),
                 (dgkvl_ref, jnp.sum(dckvn * m["nkv"], axis=0, keepdims=True)), (dgqh_ref, dgq), (dgkh_ref, dgk),
                 (dwuq_ref, _dot_tn(dqp_ref[...], m["cqn"])), (dwuk_ref, _dot_tn(m["ckvn"], dkn_ref[...])),
                 (dwuv_ref, _dot_tn(m["ckvn"], dvv_ref[...]))]

        @pl.when(pl.program_id(0) == 0)
        def _():
            for ref, val in parts:
                ref[...] = val

        @pl.when(pl.program_id(0) != 0)
        def _():
            for ref, val in parts:
                ref[...] += val

    def const(shape):
        return pl.BlockSpec(shape, lambda i: tuple(0 for _ in shape))

    tab = pl.BlockSpec((tm, 128), lambda i: (i % tpe, 0))
    hq = pl.BlockSpec((HEADS, tm, D_QKP), lambda i: (0, i, 0))
    hv = pl.BlockSpec((HEADS, tm, D_V), lambda i: (0, i, 0))

    def rowspec(n):
        return pl.BlockSpec((tm, n), lambda i: (i, 0))

    return _hosted_call(
        body, name="mla_prep_bwd", grid=(rows // tm,),
        in_specs=[rowspec(Z_MLA), hq, hq, hv, const((1, Q_RANK)), const((1, KV_RANK)), const((1, D_QKP)),
                  const((1, D_QKP)), const((HEADS * D_QKP, Q_RANK)), const((KV_RANK, HEADS * D_NOPE)),
                  const((KV_RANK, HEADS * D_V)), tab, tab, tab],
        out_specs=[rowspec(Z_MLA), const((1, Q_RANK)), const((1, KV_RANK)), const((1, D_QKP)), const((1, D_QKP)),
                   const((HEADS * D_QKP, Q_RANK)), const((KV_RANK, HEADS * D_NOPE)), const((KV_RANK, HEADS * D_V))],
        out_shape=[jax.ShapeDtypeStruct((rows, Z_W), BF16),
                   jax.ShapeDtypeStruct((1, Q_RANK), F32), jax.ShapeDtypeStruct((1, KV_RANK), F32),
                   jax.ShapeDtypeStruct((1, D_QKP), F32), jax.ShapeDtypeStruct((1, D_QKP), F32),
                   jax.ShapeDtypeStruct((HEADS * D_QKP, Q_RANK), F32),
                   jax.ShapeDtypeStruct((KV_RANK, HEADS * D_NOPE), F32), jax.ShapeDtypeStruct((KV_RANK, HEADS * D_V), F32)],
        scratch_shapes=[pltpu.VMEM((tm, HEADS * D_QKP), BF16), pltpu.VMEM((tm, HEADS * D_NOPE), BF16),
                        pltpu.VMEM((tm, HEADS * D_V), BF16)],
        args=(z, dq, dk, dv, gql, gkvl, gqh, gkh, wuq, wuk, wuv, *tabs), comm=comm)


def _local_step(h0, target, w, nb, lp, sched=None):
    tm = _row_tile(nb * lp, MATMUL_ROWS)
    te = _row_tile(lp, ELEMENTWISE_ROWS)
    tabs = _rope_tables(lp)
    g = {}
    if sched is None:
        host = lambda stage: None
    else:
        sched.g = g
        host = sched.host

    def ffn_act(tag, u):
        gate, up, act = _ffn_up_call(tag + "_up", u, w[tag + "_w_gate"], w[tag + "_w_up"], tm, host(tag + "_up"))
        return u, gate, up, act

    def ffn_bwd(tag, h, saved, dh, dhb, split, mix=None):
        u, gate, up, act = saved
        dgate, dup = _ffn_dact_call(tag + "_dact", dhb, w[tag + "_w_down"], gate, up, tm, host(tag + "_dact"))
        for nm, call, a, b, scale in (("_w_down", "_dwd", act, dhb, 0.5), ("_w_gate", "_dwg", dgate, u, 1.0),
                                      ("_w_up", "_dwu", dup, u, 1.0)):
            g[tag + nm], g[tag + nm + BF16_COPY] = _wgrad_call(tag + call, a, b, scale, host(tag + call),
                                                               bf16_copy=True)
        pieces = [(dgate, w[tag + "_w_gate"]), (dup, w[tag + "_w_up"])]
        if not split:
            res = _norm_in_bwd_call(tag + "_din", pieces, h, w[tag + "_norm"], dh, te, host(tag + "_din"), mix=mix)
            g[tag + "_norm"] = res[2]
            return (res[0], res[1], *res[3:])
        dh_a, dhb_a, dg_a = _norm_in_bwd_call(tag + "_din_a", pieces, h, w[tag + "_norm"], dh, te,
                                              host(tag + "_din_a"), part=(0, 2))
        dh_in, dhb_in, dg_b = _norm_in_bwd_call(tag + "_din_b", pieces, h, w[tag + "_norm"], dh, te,
                                                host(tag + "_din_b"), part=(1, 2), prev=(dh_a, dhb_a))
        g[tag + "_norm"] = dg_a + dg_b
        return dh_in, dhb_in

    s1 = ffn_act("ffn1", _rmsnorm_call("ffn1_norm", h0, w["ffn1_norm"], te, host("ffn1_norm")))
    h1, un = _ffn_down_call("ffn1_down", s1[3], w["ffn1_w_down"], h0, te, host("ffn1_down"), w["mix_norm"])
    z = _mm_call("mix_in", un, w["w_in"], tm, F32)
    mla_w = (w["q_latent_norm"], w["kv_latent_norm"], w["q_head_norm"], w["k_head_norm"], w["w_uq"], w["w_uk"],
             w["w_uv"])
    q, k, v = _mla_prep_call(z, *mla_w, tabs, lp, te)
    o, lse = _attn_fwd_call(q, k, v, nb, lp, host("attn_fwd"))
    lru_w = (w["conv_w"], w["conv_b"], w["gate_a_w"], w["gate_a_b"], w["gate_x_w"], w["gate_x_b"], w["lru_lambda"])
    yl, hs = _lru_fwd_call(z, *lru_w, nb, lp, host("lru_fwd"))
    y, h2, u2 = _mix_out_call(o, yl, w["attn_out_norm"], w["lru_out_norm"], w["w_out"], h1, w["ffn2_norm"], te)
    s2 = ffn_act("ffn2", u2)
    dh3, dh3b, g["final_norm"], loss = _ffn_down_loss_call("ffn2_down", s2[3], w["ffn2_w_down"], h2, w["final_norm"],
                                                           target, lp, te)
    g["loss"] = loss

    dh2, dh2b, dya, dyl, g["attn_out_norm"], g["lru_out_norm"] = ffn_bwd(
        "ffn2", h2, s2, dh3, dh3b, False, (w["w_out"], o, yl, w["attn_out_norm"], w["lru_out_norm"]))
    g["w_out"] = _wgrad_call("dw_out", y, dh2b)
    dq, dk, dv = _attn_bwd_call(q, k, v, o, lse, dya, nb, lp, host("attn_bwd"))
    (dz_mla, g["q_latent_norm"], g["kv_latent_norm"], g["q_head_norm"], g["k_head_norm"], g["w_uq"], g["w_uk"],
     g["w_uv"]) = _mla_prep_bwd_call(z, dq, dk, dv, *mla_w, tabs, lp, te, host("mla_prep_bwd"))
    (dz, g["conv_w"], g["conv_b"], g["gate_a_w"], g["gate_a_b"], g["gate_x_w"], g["gate_x_b"],
     g["lru_lambda"]) = _lru_bwd_call(z, dz_mla, hs, dyl, *lru_w, nb, lp, host("lru_bwd"))
    g["w_in"] = _wgrad_call("dw_in", dz, un)
    dh1, dh1b, g["mix_norm"] = _norm_in_bwd_call("mix_din", [(dz, w["w_in"])], h1, w["mix_norm"], dh2, te,
                                                 host("mix_din"))
    dh0 = ffn_bwd("ffn1", h0, s1, dh1, dh1b, True)[0]
    return loss, dh0, g


def _place():
    x, y, c = lax.axis_index("x"), lax.axis_index("y"), lax.axis_index("c")
    return x, y, c, [(1 - x, y), (x, 1 - y), (1 - x, 1 - y)]


def _any_specs(n):
    return [pl.BlockSpec(memory_space=pl.ANY)] * n


def _remote(src, dst, sems, k, dev):
    send_sems, recv_sems, base = sems
    return pltpu.make_async_remote_copy(src_ref=src, dst_ref=dst, send_sem=send_sems.at[base + k],
                                        recv_sem=recv_sems.at[base + k], device_id=dev, device_id_type=MESH)


EW_VMEM_BYTES = 24 * 1024 * 1024


def _fit_rows(rows, cols, blocks):
    return _row_tile(rows, max(16, int(EW_VMEM_BYTES // (8 * blocks)) // cols))


class _Geom:
    def __init__(self, n0, n1, blocks=1.0):
        self.n0, self.n1 = n0, n1
        self.axis = 0 if n0 % 32 == 0 else 1
        self.h0, self.h1 = (n0 // 2, n1) if self.axis == 0 else (n0, n1 // 2)
        self.tr = _fit_rows(self.h0, self.h1, blocks)
        self.nblk = self.h0 // self.tr

    def half_ref(self, ref, lead, idx):
        if self.axis == 0:
            return ref.at[(*lead, pl.ds(idx * self.h0, self.h0))]
        return ref.at[(*lead, slice(None), pl.ds(idx * self.h1, self.h1))]

    def half_block(self, lead, i, idx):
        return (*lead, idx * self.nblk + i, 0) if self.axis == 0 else (*lead, i, idx)


class _Comm:
    def __init__(self, ins, out_shapes, aliases, n_sems, start, finish, deliver):
        self.ins, self.out_shapes, self.aliases, self.n_sems = list(ins), list(out_shapes), dict(aliases), n_sems
        self.start, self.finish, self.deliver = start, finish, deliver

    def scratch(self):
        return [pltpu.SemaphoreType.DMA((self.n_sems,)), pltpu.SemaphoreType.DMA((self.n_sems,))]


def _comm_call(name, comm):
    n_in = len(comm.ins)

    def body(*refs):
        ins, outs, sems = refs[:n_in], refs[n_in:-2], (*refs[-2:], 0)
        comm.start(ins, outs, sems)
        comm.finish(ins, outs, sems)

    res = pl.pallas_call(
        body, name=name, out_shape=comm.out_shapes, in_specs=_any_specs(n_in),
        out_specs=_any_specs(len(comm.out_shapes)), input_output_aliases=comm.aliases,
        scratch_shapes=comm.scratch())(*comm.ins)
    return comm.deliver(list(res))


def _hosted_call(body, *, name, grid, in_specs, out_specs, out_shape, args, scratch_shapes=(), dims=None, comm=None,
                 prefetch=None, aliases=None):
    aliases = dict(aliases or {})
    in_specs, out_specs, out_shape = list(in_specs), list(out_specs), list(out_shape)
    n_pre = 0 if prefetch is None else 1

    def call(fn, in_specs, out_specs, out_shape, scratch, aliases, dims, args):
        if prefetch is None:
            return pl.pallas_call(
                fn, name=name, grid=grid, in_specs=in_specs, out_specs=out_specs, out_shape=out_shape,
                scratch_shapes=scratch, input_output_aliases=aliases, compiler_params=_params(dims))(*args)
        spec = pltpu.PrefetchScalarGridSpec(num_scalar_prefetch=1, grid=grid, in_specs=in_specs, out_specs=out_specs,
                                            scratch_shapes=scratch)
        return pl.pallas_call(
            fn, name=name, grid_spec=spec, out_shape=out_shape,
            input_output_aliases={i + 1: o for i, o in aliases.items()}, compiler_params=_params(dims))(prefetch, *args)

    if comm is None:
        return list(call(body, in_specs, out_specs, out_shape, list(scratch_shapes), aliases,
                         dims or ("arbitrary",) * len(grid), args))
    n_in, n_out, n_ci, n_co = len(in_specs), len(out_specs), len(comm.ins), len(comm.out_shapes)

    def wrapped(*refs):
        pre, refs = refs[:n_pre], refs[n_pre:]
        ins, cins = refs[:n_in], refs[n_in:n_in + n_ci]
        outs = refs[n_in + n_ci:n_in + n_ci + n_out]
        couts = refs[n_in + n_ci + n_out:n_in + n_ci + n_out + n_co]
        scratch, sems = refs[n_in + n_ci + n_out + n_co:-2], (*refs[-2:], 0)
        first = functools.reduce(jnp.logical_and, [pl.program_id(k) == 0 for k in range(len(grid))])
        last = functools.reduce(jnp.logical_and, [pl.program_id(k) == grid[k] - 1 for k in range(len(grid))])

        @pl.when(first)
        def _():
            comm.start(cins, couts, sems)

        body(*pre, *ins, *outs, *scratch)

        @pl.when(last)
        def _():
            comm.finish(cins, couts, sems)

    res = call(wrapped, in_specs + _any_specs(n_ci), out_specs + _any_specs(n_co), out_shape + comm.out_shapes,
               list(scratch_shapes) + comm.scratch(),
               {**aliases, **{n_in + i: n_out + o for i, o in comm.aliases.items()}},
               ("arbitrary",) * len(grid), (*args, *comm.ins))
    comm.deliver(list(res[n_out:]))
    return list(res[:n_out])


def _gather_comm(bufs, deliver):
    n = len(bufs)
    geoms = [_Geom(*b.shape[1:]) for b in bufs]

    def first(outs, sems):
        x, y, c, chips = _place()
        cps = []
        for a in range(n):
            mine = geoms[a].half_ref(outs[a], (2 * x + y,), c)
            cps += [_remote(mine, mine, sems, 6 * a + j, (cx, cy, c)) for j, (cx, cy) in enumerate(chips)]
        return cps

    def start(ins, outs, sems):
        for cp in first(outs, sems):
            cp.start()

    def finish(ins, outs, sems):
        x, y, c, chips = _place()
        sib = (x, y, 1 - c)
        passed = []
        for a in range(n):
            for j, (cx, cy) in enumerate(chips):
                land = geoms[a].half_ref(outs[a], (2 * cx + cy,), c)
                _remote(land, land, sems, 6 * a + j, sib).wait_recv()
                cp = _remote(land, land, sems, 6 * a + 3 + j, sib)
                cp.start()
                passed.append(cp)
        for a in range(n):
            for j, (cx, cy) in enumerate(chips):
                land = geoms[a].half_ref(outs[a], (2 * cx + cy,), 1 - c)
                _remote(land, land, sems, 6 * a + 3 + j, sib).wait_recv()
        for cp in first(outs, sems) + passed:
            cp.wait_send()

    return _Comm(bufs, [jax.ShapeDtypeStruct(b.shape, b.dtype) for b in bufs], {a: a for a in range(n)}, 6 * n,
                 start, finish, deliver)


def _reduce_pair_comm(grads, deliver):
    n = len(grads)
    geoms = [_Geom(*a.shape[1:]) for a in grads]

    def copies(ins, outs, sems):
        x, y, c, _ = _place()
        return [_remote(geoms[a].half_ref(ins[a], (slice(None),), 1 - c), outs[a], sems, a, (x, y, 1 - c))
                for a in range(n)]

    def start(ins, outs, sems):
        for cp in copies(ins, outs, sems):
            cp.start()

    def finish(ins, outs, sems):
        cps = copies(ins, outs, sems)
        for cp in cps:
            cp.wait_recv()
        for cp in cps:
            cp.wait_send()

    shapes = [jax.ShapeDtypeStruct((N_SHARD, g.h0, g.h1), a.dtype) for a, g in zip(grads, geoms)]
    return _Comm(grads, shapes, {}, n, start, finish, deliver)


def _reduce_chips_comm(parts, deliver):
    n = len(parts)

    def copies(ins, outs, sems):
        x, y, c, chips = _place()
        return [_remote(ins[a].at[2 * cx + cy], outs[a].at[j], sems, 3 * a + j, (cx, cy, c))
                for a in range(n) for j, (cx, cy) in enumerate(chips)]

    def start(ins, outs, sems):
        for cp in copies(ins, outs, sems):
            cp.start()

    def finish(ins, outs, sems):
        cps = copies(ins, outs, sems)
        for cp in cps:
            cp.wait_recv()
        for cp in cps:
            cp.wait_send()

    shapes = [jax.ShapeDtypeStruct((3,) + a.shape[1:], a.dtype) for a in parts]
    return _Comm(parts, shapes, {}, 3 * n, start, finish, deliver)


def _share_pair_comm(bufs, deliver):
    n = len(bufs)
    geoms = [_Geom(*b.shape) for b in bufs]

    def copies(outs, sems):
        x, y, c, _ = _place()
        cps = []
        for a in range(n):
            mine = geoms[a].half_ref(outs[a], (), c)
            cps.append(_remote(mine, mine, sems, a, (x, y, 1 - c)))
        return cps

    def start(ins, outs, sems):
        for cp in copies(outs, sems):
            cp.start()

    def finish(ins, outs, sems):
        x, y, c, _ = _place()
        for a in range(n):
            land = geoms[a].half_ref(outs[a], (), 1 - c)
            _remote(land, land, sems, a, (x, y, 1 - c)).wait_recv()
        for cp in copies(outs, sems):
            cp.wait_send()

    return _Comm(bufs, [jax.ShapeDtypeStruct(b.shape, b.dtype) for b in bufs], {a: a for a in range(n)}, n,
                 start, finish, deliver)


def _small_comm(pack, deliver):
    r, d = pack.shape

    def copies(ins, outs, sems):
        x, y, c, _ = _place()
        cps = []
        for k in range(1, 8):
            peer = (x ^ ((k >> 2) & 1), y ^ ((k >> 1) & 1), c ^ (k & 1))
            cps.append(_remote(ins[0], outs[0].at[4 * x + 2 * y + c], sems, k - 1, peer))
        return cps

    def start(ins, outs, sems):
        for cp in copies(ins, outs, sems):
            cp.start()

    def finish(ins, outs, sems):
        cps = copies(ins, outs, sems)
        for cp in cps:
            cp.wait_recv()
        for cp in cps:
            cp.wait_send()

    return _Comm([pack], [jax.ShapeDtypeStruct((8, r, d), pack.dtype)], {}, 7, start, finish, deliver)


def _join_comms(comms):
    comms = [c for c in comms if c is not None]
    if len(comms) <= 1:
        return comms[0] if comms else None
    ins, out_shapes, aliases, spans, n_sems = [], [], {}, [], 0
    for c in comms:
        aliases.update({len(ins) + i: len(out_shapes) + o for i, o in c.aliases.items()})
        spans.append((len(ins), len(ins) + len(c.ins), len(out_shapes), len(out_shapes) + len(c.out_shapes), n_sems))
        ins += c.ins
        out_shapes += c.out_shapes
        n_sems += c.n_sems

    def run(which):
        def go(all_ins, all_outs, sems):
            for c, (i0, i1, o0, o1, base) in zip(comms, spans):
                getattr(c, which)(all_ins[i0:i1], all_outs[o0:o1], (sems[0], sems[1], sems[2] + base))
        return go

    def deliver(outs):
        for c, (_, _, o0, o1, _) in zip(comms, spans):
            c.deliver(outs[o0:o1])
        return outs

    return _Comm(ins, out_shapes, aliases, n_sems, run("start"), run("finish"), deliver)


def _ew_call(name, fn, ins, out_dtypes):
    shape = ins[0].shape
    cols = shape[-1]
    rows = 1
    for s_ in shape[:-1]:
        rows *= s_
    ins2 = [a.reshape(rows, cols) for a in ins]
    tr = rows
    for t in range(16, min(rows, max(16, (1 << 19) // cols)) + 1, 16):
        if rows % t == 0:
            tr = t
    no = len(out_dtypes)

    def body(*refs):
        outs = fn(*[r[...] for r in refs[:len(ins2)]])
        for ref, val in zip(refs[len(ins2):], outs):
            ref[...] = val.astype(ref.dtype)

    spec = pl.BlockSpec((tr, cols), lambda i: (i, 0))
    res = pl.pallas_call(
        body, name=name, grid=(rows // tr,), in_specs=[spec] * len(ins2), out_specs=[spec] * no,
        out_shape=[jax.ShapeDtypeStruct((rows, cols), dt) for dt in out_dtypes],
        compiler_params=_params(("parallel",)))(*ins2)
    return [r.reshape(shape) for r in res]


def _adamw_math(w, g, m, v):
    m = ADAM_B1 * m + (1.0 - ADAM_B1) * g
    v = ADAM_B2 * v + (1.0 - ADAM_B2) * (g * g)
    m_hat = m / (1.0 - ADAM_B1 ** ADAM_STEP)
    v_hat = v / (1.0 - ADAM_B2 ** ADAM_STEP)
    delta = -ADAM_LR * (m_hat / (jnp.sqrt(v_hat) + ADAM_EPS) + ADAM_WD * w)
    return delta, m, v


def _adamw_call(name, w, g, m, v):
    return _ew_call(name, _adamw_math, [w, g, m, v], [F32, F32, F32])


def _tiled_call(name, fn, place, grid, in_items, out_items, comm=None):
    ni = len(in_items)

    def body(place_ref, *refs):
        vals = fn(*[r[...] for r in refs[:ni]])
        for ref, val in zip(refs[ni:], vals):
            ref[...] = val.astype(ref.dtype)

    return _hosted_call(
        body, name=name, grid=grid, in_specs=[pl.BlockSpec(blk, imap) for _, blk, imap in in_items],
        out_specs=[pl.BlockSpec(blk, imap) for _, _, blk, imap in out_items],
        out_shape=[jax.ShapeDtypeStruct(shp, dt) for shp, dt, _, _ in out_items],
        args=[a for a, _, _ in in_items], prefetch=place, comm=comm)


def _cast_call(name, place, shards, comm=None):
    n0, n1 = shards[0].shape
    tr = _fit_rows(n0, n1, 1.5 * len(shards))
    ins = [(a, (tr, n1), lambda i, p: (i, 0)) for a in shards]
    outs = [((N_SHARD, n0, n1), BF16, (1, tr, n1), lambda i, p: (p[0], i, 0)) for _ in shards]
    return _tiled_call(name, lambda *v: [x[None] for x in v], place, (n0 // tr,), ins, outs, comm)


def _pair_sum_call(name, place, fulls, gots):
    k = len(fulls)
    g = _Geom(*fulls[0].shape[1:], blocks=2.5 * k)
    blk = (1, g.tr, g.h1)
    ins = [(a, blk, lambda s, i, p: g.half_block((s,), i, p[1])) for a in fulls]
    ins += [(a, blk, lambda s, i, p: (s, i, 0)) for a in gots]
    outs = [((N_SHARD, g.h0, g.h1), BF16, blk, lambda s, i, p: (s, i, 0)) for _ in fulls]
    return _tiled_call(name, lambda *v: [v[j] + v[k + j] for j in range(k)], place, (N_SHARD, g.nblk), ins, outs)


def _chip_sum_call(name, place, fulls, gots, recvs, comm=None):
    k = len(fulls)
    g = _Geom(*fulls[0].shape[1:], blocks=4.5 * k)
    blk = (1, g.tr, g.h1)
    ins = [(a, blk, lambda i, p: g.half_block((p[0],), i, p[1])) for a in fulls]
    ins += [(a, blk, lambda i, p: (p[0], i, 0)) for a in gots]
    ins += [(a, (3, g.tr, g.h1), lambda i, p: (0, i, 0)) for a in recvs]
    outs = [((g.n0, g.n1), F32, (g.tr, g.h1), lambda i, p: g.half_block((), i, p[1])) for _ in fulls]

    def fn(*v):
        res = []
        for j in range(k):
            r = v[2 * k + j].astype(F32)
            res.append(v[j][0] + v[k + j][0] + r[0] + r[1] + r[2])
        return res

    return _tiled_call(name, fn, place, (g.nblk,), ins, outs, comm)


def _adamw_group_call(name, ws, gs, ms, vs, comm=None):
    k = len(ws)
    n0, n1 = ws[0].shape
    tr = _fit_rows(n0, n1, 8 * k)
    spec = pl.BlockSpec((tr, n1), lambda i: (i, 0))

    def body(*refs):
        for j in range(k):
            g = refs[k + j][...]
            delta, m, vv = _adamw_math(refs[j][...], g, refs[2 * k + j][...], refs[3 * k + j][...])
            for ref, val in zip(refs[4 * k + 4 * j:4 * k + 4 * j + 4], (g, delta, m, vv)):
                ref[...] = val

    flat = _hosted_call(
        body, name=name, grid=(n0 // tr,), in_specs=[spec] * (4 * k), out_specs=[spec] * (4 * k),
        out_shape=[jax.ShapeDtypeStruct((n0, n1), F32)] * (4 * k), dims=("parallel",),
        args=(*ws, *gs, *ms, *vs), comm=comm)
    return [flat[4 * j:4 * j + 4] for j in range(k)]


def _small_update_call(me, early, own_early, late, own_late, states):
    nd, r, d = early.shape
    widths = [a.shape[1] for a in states[0][0]]
    nw = len(widths)
    n_state = nw + 2

    def body(me_ref, e_ref, oe_ref, l_ref, ol_ref, *refs):
        state_refs, refs = refs[:3 * n_state], refs[3 * n_state:]
        (gs_ref, d_ref, nm_ref, nv_ref), rows, packs = refs[:4], refs[4:4 + 4 * nw], refs[4 + 4 * nw:]
        for kind, pack in enumerate(packs):
            srefs = state_refs[kind * n_state:(kind + 1) * n_state]
            pack[...] = jnp.zeros_like(pack)
            for k, width in enumerate(widths):
                pack[k:k + 1, 0:width] = srefs[k][...]
            pack[ROW_GATE_A:ROW_GATE_A + 32, :] = srefs[nw][...]
            pack[ROW_GATE_X:ROW_GATE_X + 32, :] = srefs[nw + 1][...]
        w_ref, m_ref, v_ref = packs
        mine = me_ref[0]

        def total(g_ref, own_ref):
            acc = None
            for k in range(nd):
                part = jnp.where(mine == k, own_ref[...], g_ref[k])
                acc = part if acc is None else acc + part
            return acc

        gs = total(e_ref, oe_ref)
        ls = total(l_ref, ol_ref)
        gs_ref[...] = gs
        first = gs[0:8] + ls[0:8]
        gs_ref[0:8, :] = first
        gs_ref[ROW_META:ROW_META + N_META, :] = gs[ROW_META:ROW_META + N_META] + ls[8:8 + N_META]
        grads = jnp.concatenate([first, gs[8:SMALL_ADAM_ROWS]], axis=0)
        delta, m, v = _adamw_math(w_ref[...], grads, m_ref[...], v_ref[...])
        d_ref[...] = delta
        nm_ref[...] = m
        nv_ref[...] = v
        for kind, pack_ref in enumerate((gs_ref, d_ref, nm_ref, nv_ref)):
            for k, width in enumerate(widths):
                rows[kind * nw + k][...] = pack_ref[k:k + 1, 0:width]

    vm = pl.BlockSpec(memory_space=pltpu.VMEM)
    ashape = jax.ShapeDtypeStruct((SMALL_ADAM_ROWS, d), F32)
    row_shapes = [jax.ShapeDtypeStruct((1, width), F32) for _ in range(4) for width in widths]
    flat_states = [a for rows_, ga, gx in states for a in (*rows_, ga, gx)]
    res = pl.pallas_call(
        body, name="small_update", in_specs=[pl.BlockSpec(memory_space=pltpu.SMEM)] + [vm] * (4 + 3 * n_state),
        out_specs=[vm] * (4 + 4 * nw),
        out_shape=[jax.ShapeDtypeStruct((r, d), F32), ashape, ashape, ashape] + row_shapes,
        scratch_shapes=[pltpu.VMEM((SMALL_ADAM_ROWS, d), F32)] * 3,
        compiler_params=pltpu.CompilerParams(vmem_limit_bytes=VMEM_LIMIT_BYTES))(
            me, early, own_early, late, own_late, *flat_states)
    return res[:4], [res[4 + kind * nw:4 + (kind + 1) * nw] for kind in range(4)]


SMALL_NAMES = ["ffn1_norm", "mix_norm", "ffn2_norm", "final_norm", "q_latent_norm", "kv_latent_norm",
               "q_head_norm", "k_head_norm", "conv_b", "gate_a_b", "gate_x_b", "lru_lambda", "attn_out_norm",
               "lru_out_norm"]
ROW_CONV_W = 14
ROW_GATE_A = 16
ROW_GATE_X = 48
ROW_META = 80
ROW_LOSS = 96


def _row(a):
    flat = a.reshape(1, -1)
    return jnp.pad(flat, ((0, 0), (0, D_MODEL - flat.shape[1])))


def _pack_small(t, rows):
    parts = [_row(t[nm]) for nm in SMALL_NAMES]
    parts.append(t["conv_w"].reshape(2, D_MODEL))
    parts.append(t["gate_a_w"].reshape(32, D_MODEL))
    parts.append(t["gate_x_w"].reshape(32, D_MODEL))
    p = jnp.concatenate(parts, axis=0)
    return jnp.pad(p, ((0, rows - p.shape[0]), (0, 0)))


def _early_pack(g):
    gs = {nm: g.get(nm, jnp.zeros((1, D_MODEL), F32)) for nm in SMALL_NAMES}
    gs["q_head_norm"] = g["q_head_norm"][:, 0:D_QK]
    gs["k_head_norm"] = g["k_head_norm"][:, 0:D_QK]
    for nm in ("conv_b", "gate_a_b", "gate_x_b", "lru_lambda"):
        gs[nm] = g[nm].reshape(1, LRU_W)
    gs["conv_w"] = g["conv_w"].transpose(1, 0, 2).reshape(CONV_K, LRU_W)
    gs["gate_a_w"] = _gate_blocks(g["gate_a_w"])
    gs["gate_x_w"] = _gate_blocks(g["gate_x_w"])
    return jnp.concatenate([_pack_small(gs, ROW_META), jnp.zeros((N_META, D_MODEL), F32), _row(g["loss"][:, 0:1]),
                            jnp.zeros((SMALL_ROWS - ROW_LOSS - 1, D_MODEL), F32)], axis=0)


def _unpack_small(p, rows, like):
    out = dict(zip(SMALL_NAMES, rows))
    out["gate_a_w"] = p[ROW_GATE_A:ROW_GATE_A + 32].reshape(like["gate_a_w"].shape)
    out["gate_x_w"] = p[ROW_GATE_X:ROW_GATE_X + 32].reshape(like["gate_x_w"].shape)
    return out


def _gate_dense(wg):
    w4 = wg[0].reshape(N_LRU_TILES, 2, 64, 64)
    zero = jnp.zeros((N_LRU_TILES, 64, 64), wg.dtype)
    top = jnp.concatenate([w4[:, 0], zero], axis=2)
    bot = jnp.concatenate([zero, w4[:, 1]], axis=2)
    return jnp.concatenate([top, bot], axis=1).astype(BF16)


def _gate_blocks(dw):
    return jnp.stack([dw[:, 0:64, 0:64], dw[:, 64:128, 64:128]], axis=1).reshape(8, 64, 64)


BIG_NAMES = ["ffn1_w_gate", "ffn1_w_up", "ffn1_w_down", "w_in", "w_uq", "w_uk", "w_uv", "w_out", "ffn2_w_gate",
             "ffn2_w_up", "ffn2_w_down"]
BIG_GROUPS = [["ffn1_w_gate", "ffn1_w_up", "ffn1_w_down", "ffn2_w_gate", "ffn2_w_up", "ffn2_w_down"], ["w_in"],
              ["w_uq"], ["w_uk", "w_uv"], ["w_out"]]
TRANSPOSED = ("ffn1_w_gate", "ffn1_w_up", "ffn2_w_gate", "ffn2_w_up", "w_in", "w_uq")


def _to2d(nm, a):
    return a[0].T if nm in TRANSPOSED else a[0]


def _from2d(nm, a):
    return (a.T if nm in TRANSPOSED else a)[None]


WEIGHT_NAMES = ["meta_tokens", "ffn1_norm", "ffn1_w_gate", "ffn1_w_up", "ffn1_w_down", "mix_norm", "w_in",
                "q_latent_norm", "w_uq", "kv_latent_norm", "w_uk", "w_uv", "q_head_norm", "k_head_norm", "conv_w",
                "conv_b", "gate_a_w", "gate_a_b", "gate_x_w", "gate_x_b", "lru_lambda", "attn_out_norm",
                "lru_out_norm", "w_out", "ffn2_norm", "ffn2_w_gate", "ffn2_w_up", "ffn2_w_down", "final_norm"]


def _weight_from(nm, slots):
    if nm == "w_in":
        win = slots.reshape(IN_WIDTH, D_MODEL)
        lru = win[Z_KR + D_ROPE:].reshape(2, N_LRU_TILES, LRU_TILE, D_MODEL).transpose(1, 0, 2, 3)
        return jnp.concatenate([win[0:Z_KR + D_ROPE], jnp.zeros((128 - D_ROPE, D_MODEL), BF16),
                                lru.reshape(2 * LRU_W, D_MODEL)], axis=0)
    if nm == "w_uq":
        return jnp.pad(slots, ((0, 0), (0, D_QKP - D_QK), (0, 0))).reshape(HEADS * D_QKP, Q_RANK)
    if nm in ("w_uk", "w_uv"):
        return slots.transpose(1, 0, 2).reshape(KV_RANK, HEADS * D_NOPE)
    if nm == "w_out":
        return slots.reshape(D_MODEL, D_MODEL)
    return slots


def _small_weights(p, small):
    w = {nm: p[nm] for nm in SMALL_NAMES}
    w["q_head_norm"] = jnp.pad(p["q_head_norm"], ((0, 0), (0, D_QKP - D_QK)))
    w["k_head_norm"] = jnp.pad(p["k_head_norm"], ((0, 0), (0, D_QKP - D_QK)))
    w["conv_w"] = small[:, N_META:N_META + 2, :].reshape(N_SHARD, CONV_K, LRU_TILE)
    w["gate_a_w"] = _gate_dense(p["gate_a_w"])
    w["gate_x_w"] = _gate_dense(p["gate_x_w"])
    meta = small[:, 0:N_META, :].transpose(1, 0, 2).reshape(N_META, D_MODEL)
    return w, meta


def _full_weights(p, gathered, small):
    w, meta = _small_weights(p, small)
    w.update({nm: _weight_from(nm, gathered[nm]) for nm in BIG_NAMES})
    return w, meta


def _shard_grad(nm, g):
    if nm == "w_in":
        lru = g[Z_MLA:].reshape(N_LRU_TILES, 2, LRU_TILE, D_MODEL).transpose(1, 0, 2, 3).reshape(2 * LRU_W, D_MODEL)
        return jnp.concatenate([g[0:Z_KR + D_ROPE], lru], axis=0).reshape(N_SHARD, IN_WIDTH // N_SHARD, D_MODEL)
    if nm == "w_uq":
        return g.reshape(HEADS, D_QKP, Q_RANK)[:, 0:D_QK, :]
    if nm in ("w_uk", "w_uv"):
        return g.reshape(KV_RANK, HEADS, D_NOPE).transpose(1, 0, 2)
    if nm == "w_out":
        return g.reshape(N_SHARD, D_MODEL // N_SHARD, D_MODEL)
    return g


def _shard_grads(g):
    return {nm: _shard_grad(nm, g[nm]) for nm in BIG_NAMES}


GATHER_AT = {"ffn1_norm": ["ffn1_w_gate", "ffn1_w_up"],
             "ffn1_up": ["ffn1_w_down", "w_in", "w_uq", "w_uk", "w_uv", "w_out"],
             "attn_fwd": ["ffn2_w_down", "ffn2_w_gate"], "lru_fwd": ["ffn2_w_up"]}
PAIR_AT = [("ffn2_din", ["ffn2_w_gate", "ffn2_w_up", "ffn2_w_down"]),
           ("mix_din", ["w_out", "w_uq", "w_uk", "w_uv", "w_in"]),
           ("ffn1_dwg", ["ffn1_w_down"]), ("ffn1_dwu", ["ffn1_w_gate"]), ("ffn1_din_a", ["ffn1_w_up"])]
CHIPS_AT = [("attn_bwd", ["ffn2_w_down", "ffn2_w_gate"]), ("mla_prep_bwd", ["ffn2_w_up"]),
            ("ffn1_dact", ["w_out", "w_uq", "w_uk", "w_uv", "w_in"]),
            ("ffn1_dwu", ["ffn1_w_down"]), ("ffn1_din_a", ["ffn1_w_gate"]), ("ffn1_din_b", ["ffn1_w_up"])]
BF16_COPY = "@bf16"
SHARE_EARLY_GROUPS, SHARE_EARLY_AT = 3, "ffn1_dwd"
SMALL_EARLY_AT = "mix_din"


def _same_shape_groups(names):
    return [[nm for nm in grp if nm in names] for grp in BIG_GROUPS if any(nm in names for nm in grp)]


class _Sched:
    def __init__(self, place, w, slots):
        self.place, self.w, self.slots = place, w, slots
        self.g = None
        self.sharded, self.from_pair, self.chip_bf16, self.from_chips = {}, {}, {}, {}
        self.early = self.early_all = None
        self.shared = {}

    def host(self, stage):
        comms = []
        if stage in GATHER_AT:
            comms.append(self.gather(GATHER_AT[stage]))
        comms += [self.chips(names) for at, names in CHIPS_AT if at == stage]
        comms += [self.pair(names) for at, names in PAIR_AT if at == stage]
        if stage == SMALL_EARLY_AT:
            comms.append(self.small_early())
        if stage == SHARE_EARLY_AT:
            comms.append(self.share_early())
        return _join_comms(comms)

    def small_early(self):
        self.early = _early_pack(self.g)

        def deliver(outs):
            self.early_all = outs[0]
            return outs

        return _small_comm(self.early, deliver)

    def gather(self, names):
        def deliver(outs):
            self.w.update({nm: _weight_from(nm, o) for nm, o in zip(names, outs)})
            return outs

        return _gather_comm([self.slots[nm] for nm in names], deliver)

    def pair(self, names):
        self.sharded.update({nm: _shard_grad(nm, self.g[nm]) for nm in names})
        send = [_shard_grad(nm, self.g.get(nm + BF16_COPY, self.g[nm])) for nm in names]

        def deliver(outs):
            self.from_pair.update(zip(names, outs))
            for grp in _same_shape_groups(names):
                sums = _pair_sum_call("pair_sum_" + grp[0], self.place, [self.sharded[nm] for nm in grp],
                                      [self.from_pair[nm] for nm in grp])
                self.chip_bf16.update(zip(grp, sums))
            return outs

        return _reduce_pair_comm(send, deliver)

    def chips(self, names):
        def deliver(outs):
            self.from_chips.update(zip(names, outs))
            return outs

        return _reduce_chips_comm([self.chip_bf16[nm] for nm in names], deliver)

    def chip_sums(self, names, comm=None):
        out = {}
        for grp in _same_shape_groups(names):
            sums = _chip_sum_call("chip_sum_" + grp[0], self.place, [self.sharded[nm] for nm in grp],
                                  [self.from_pair[nm] for nm in grp], [self.from_chips[nm] for nm in grp], comm)
            comm = None
            out.update(zip(grp, sums))
        return out

    def share_early(self):
        names = [nm for at, grp in CHIPS_AT[:SHARE_EARLY_GROUPS] for nm in grp]
        mine = self.chip_sums(names)
        return _share_pair_comm([mine[nm] for nm in names], lambda o: self.shared.update(zip(names, o)))


def kernel(x, meta_tokens, ffn1_norm, ffn1_w_gate, ffn1_w_up, ffn1_w_down, mix_norm, w_in, q_latent_norm, w_uq, kv_latent_norm, w_uk, w_uv, q_head_norm, k_head_norm, conv_w, conv_b, gate_a_w, gate_a_b, gate_x_w, gate_x_b, lru_lambda, attn_out_norm, lru_out_norm, w_out, ffn2_norm, ffn2_w_gate, ffn2_w_up, ffn2_w_down, final_norm, loss_target, m_meta_tokens, m_ffn1_norm, m_ffn1_w_gate, m_ffn1_w_up, m_ffn1_w_down, m_mix_norm, m_w_in, m_q_latent_norm, m_w_uq, m_kv_latent_norm, m_w_uk, m_w_uv, m_q_head_norm, m_k_head_norm, m_conv_w, m_conv_b, m_gate_a_w, m_gate_a_b, m_gate_x_w, m_gate_x_b, m_lru_lambda, m_attn_out_norm, m_lru_out_norm, m_w_out, m_ffn2_norm, m_ffn2_w_gate, m_ffn2_w_up, m_ffn2_w_down, m_final_norm, v_meta_tokens, v_ffn1_norm, v_ffn1_w_gate, v_ffn1_w_up, v_ffn1_w_down, v_mix_norm, v_w_in, v_q_latent_norm, v_w_uq, v_kv_latent_norm, v_w_uk, v_w_uv, v_q_head_norm, v_k_head_norm, v_conv_w, v_conv_b, v_gate_a_w, v_gate_a_b, v_gate_x_w, v_gate_x_b, v_lru_lambda, v_attn_out_norm, v_lru_out_norm, v_w_out, v_ffn2_norm, v_ffn2_w_gate, v_ffn2_w_up, v_ffn2_w_down, v_final_norm):
    args = locals()
    p = {nm: args[nm] for nm in WEIGHT_NAMES}
    mom = {nm: args["m_" + nm] for nm in WEIGHT_NAMES}
    var = {nm: args["v_" + nm] for nm in WEIGHT_NAMES}
    nb, seq, d = x.shape
    lp = CHUNK + seq
    xi, yi, ci = lax.axis_index("x"), lax.axis_index("y"), lax.axis_index("c")
    chip = 2 * xi + yi

    place = jnp.stack([chip, ci]).astype(jnp.int32)
    p2 = {nm: _to2d(nm, p[nm]) for nm in BIG_NAMES}
    m2 = {nm: _to2d(nm, mom[nm]) for nm in BIG_NAMES}
    v2 = {nm: _to2d(nm, var[nm]) for nm in BIG_NAMES}

    slots = {}
    small_shard = jnp.concatenate(
        [meta_tokens, conv_w[0].reshape(2, 2 * LRU_TILE), jnp.zeros((14, 2 * LRU_TILE), F32)], axis=0)
    small_slots = lax.dynamic_update_slice(jnp.zeros((N_SHARD,) + small_shard.shape, F32), small_shard[None],
                                           (chip, 0, 0))
    first = []
    comm = _gather_comm([small_slots], first.extend)
    for grp in BIG_GROUPS:
        for nm, buf in zip(grp, _cast_call("cast_" + grp[0], place, [p2[nm] for nm in grp], comm)):
            slots[nm] = buf
        comm = None
    w, meta = _small_weights(p, first[0])
    sched = _Sched(place, w, slots)

    h0 = jnp.concatenate(
        [jnp.zeros((nb, PAD_ROWS, d), F32), jnp.broadcast_to(meta[None], (nb, N_META, d)), x], axis=1)
    loss_part, dh0, g = _local_step(h0.reshape(nb * lp, d), loss_target, w, nb, lp, sched)
    dh0 = dh0.reshape(nb, lp, d)
    grad_x = dh0[:, CHUNK:, :]

    late = jnp.concatenate([g["ffn1_norm"], g["mix_norm"], jnp.zeros((6, D_MODEL), F32),
                            jnp.sum(dh0[:, PAD_ROWS:CHUNK, :], axis=0)], axis=0)
    shared = sched.shared
    rest = [nm for at, names in CHIPS_AT[SHARE_EARLY_GROUPS:] if at is not None for nm in names]
    last = [nm for at, names in CHIPS_AT if at is None for nm in names]
    late_box = {}
    mine = sched.chip_sums(rest, _small_comm(late, lambda o: late_box.update(all=o[0])))
    share = _share_pair_comm([mine[nm] for nm in rest], lambda o: shared.update(zip(rest, o)))
    _comm_call("share_pair", _join_comms([share, sched.chips(last) if last else None]))
    if last:
        mine = sched.chip_sums(last)
        _comm_call("share_last", _share_pair_comm([mine[nm] for nm in last], lambda o: shared.update(zip(last, o))))
    late_all = late_box["all"]
    small_like = {nm: p[nm] for nm in SMALL_NAMES + ["gate_a_w", "gate_x_w"]}

    def state(t):
        return ([t[nm] for nm in SMALL_NAMES], t["gate_a_w"].reshape(32, D_MODEL), t["gate_x_w"].reshape(32, D_MODEL))

    me = (4 * xi + 2 * yi + ci).astype(jnp.int32).reshape(1)
    (gsum, dsm, msm, vsm), rows = _small_update_call(me, sched.early_all, sched.early, late_all, late,
                                                     [state(p), state(mom), state(var)])
    grads = _unpack_small(gsum, rows[0], small_like)
    delta = _unpack_small(dsm, rows[1], small_like)
    new_m = _unpack_small(msm, rows[2], small_like)
    new_v = _unpack_small(vsm, rows[3], small_like)
    loss = gsum[ROW_LOSS, 0]
    gmeta = gsum[ROW_META:ROW_META + N_META].reshape(N_META, N_SHARD, D_MODEL // N_SHARD)
    grads["meta_tokens"] = lax.dynamic_index_in_dim(gmeta, chip, axis=1, keepdims=False)
    gconv = gsum[ROW_CONV_W:ROW_CONV_W + 2].reshape(CONV_K, N_SHARD, LRU_TILE)
    grads["conv_w"] = lax.dynamic_index_in_dim(gconv, chip, axis=1, keepdims=False)[None]
    for nm in ("meta_tokens", "conv_w"):
        delta[nm], new_m[nm], new_v[nm] = _adamw_call("adamw_" + nm, p[nm], grads[nm], mom[nm], var[nm])

    for names in BIG_GROUPS:
        res = _adamw_group_call("adamw_" + names[0], [p2[nm] for nm in names], [shared[nm] for nm in names],
                                [m2[nm] for nm in names], [v2[nm] for nm in names])
        for nm, (gg, dd, mm, vv) in zip(names, res):
            grads[nm], delta[nm], new_m[nm], new_v[nm] = (_from2d(nm, t) for t in (gg, dd, mm, vv))

    return (loss, grad_x, *[grads[nm] for nm in WEIGHT_NAMES], *[delta[nm] for nm in WEIGHT_NAMES],
            *[new_m[nm] for nm in WEIGHT_NAMES], *[new_v[nm] for nm in WEIGHT_NAMES])
```
